```python
import math
import jax, jax.numpy as jnp
from jax import lax
import numpy as np

D_MODEL = 1024
BATCH = 8
SEQ = 4096
DEPTH = 4

D_MIX = D_MODEL
D_POOL = D_MIX // 2
D_SSM = D_MIX - D_POOL
POOL_WINDOWS = (2, 4, 8, 16)
N_POOL_GROUPS = len(POOL_WINDOWS)
POOL_GROUP = D_POOL // N_POOL_GROUPS
SSM_GROUP = 16
N_SSM_GROUPS = D_SSM // SSM_GROUP
SSM_STATE = 64
D_FF = -(-8 * D_MODEL // (3 * 256)) * 256
RMS_EPS = 1e-6
DT_MIN = 1e-3
DT_MAX = 1e-1

kernel_name = "hybrid_pool_s5_parallel_heads"


def rmsnorm(x, g):
    xf = x.astype(jnp.float32)
    y = xf * lax.rsqrt(jnp.mean(xf * xf, axis=-1, keepdims=True) + RMS_EPS)
    return y.astype(x.dtype) * g


def pool_mixer(u, w_pool, scale):
    b, l, _ = u.shape
    ug = u.astype(jnp.float32).reshape(b, l, N_POOL_GROUPS, POOL_GROUP)
    cs = jnp.cumsum(ug, axis=1)
    n_pos = jnp.arange(1, l + 1, dtype=jnp.float32)
    diffs = []
    for gi, w in enumerate(POOL_WINDOWS):
        c = cs[:, :, gi]
        lagged = jnp.pad(c, ((0, 0), (w, 0), (0, 0)))[:, :l]
        mean = (c - lagged) / jnp.minimum(n_pos, float(w))[None, :, None]
        diffs.append(mean - ug[:, :, gi])
    d = jnp.stack(diffs, axis=2)
    y = jnp.einsum('blgc,gcd->blgd', d, w_pool.astype(jnp.float32)).reshape(b, l, D_POOL)
    return (y * scale.astype(jnp.float32)).astype(u.dtype)


def _complex_affine_combine(e1, e2):
    a1r, a1i, b1r, b1i = e1
    a2r, a2i, b2r, b2i = e2
    ar = a2r * a1r - a2i * a1i
    ai = a2r * a1i + a2i * a1r
    br = a2r * b1r - a2i * b1i + b2r
    bi = a2r * b1i + a2i * b1r + b2i
    return ar, ai, br, bi


def ssm_mixer(u, lam_re, lam_im, log_dt, b_re, b_im, c_re, c_im, d_skip, w_glu, b_glu):
    f32 = jnp.float32
    bsz, l, _ = u.shape
    ug = u.astype(f32).reshape(bsz, l, N_SSM_GROUPS, SSM_GROUP)
    lr, li = lam_re.astype(f32), lam_im.astype(f32)
    dt = jnp.exp(log_dt.astype(f32))[:, None]
    mag = jnp.exp(lr * dt)
    abar_r = mag * jnp.cos(li * dt)
    abar_i = mag * jnp.sin(li * dt)
    den = lr * lr + li * li
    nr, ni = abar_r - 1.0, abar_i
    coef_r = (nr * lr + ni * li) / den
    coef_i = (ni * lr - nr * li) / den
    br, bi = b_re.astype(f32), b_im.astype(f32)
    bbar_r = coef_r[..., None] * br - coef_i[..., None] * bi
    bbar_i = coef_r[..., None] * bi + coef_i[..., None] * br
    bu_r = jnp.einsum('blgh,gph->blgp', ug, bbar_r)
    bu_i = jnp.einsum('blgh,gph->blgp', ug, bbar_i)
    a_r = jnp.broadcast_to(abar_r, (1, l, N_SSM_GROUPS, SSM_STATE))
    a_i = jnp.broadcast_to(abar_i, (1, l, N_SSM_GROUPS, SSM_STATE))
    _, _, s_r, s_i = lax.associative_scan(_complex_affine_combine, (a_r, a_i, bu_r, bu_i), axis=1)
    y = (jnp.einsum('blgp,ghp->blgh', s_r, c_re.astype(f32))
         - jnp.einsum('blgp,ghp->blgh', s_i, c_im.astype(f32))
         + d_skip.astype(f32) * ug).reshape(bsz, l, D_SSM)
    y = jax.nn.gelu(y)
    y = y * jax.nn.sigmoid(y @ w_glu.astype(f32) + b_glu.astype(f32))
    return y.astype(u.dtype)


def swiglu(h, w_gate, w_up, w_down):
    return (jax.nn.silu(h @ w_gate) * (h @ w_up)) @ w_down


def _fwd_setup_inputs(seed: int = 0) -> dict:
    key = jax.random.key(seed)
    ks = jax.random.split(key, 24)
    f32 = jnp.float32
    nrm = lambda k, s, sc: jax.random.normal(k, s, f32) * sc
    res_scale = (2 * DEPTH) ** -0.5
    n_idx = jnp.arange(SSM_STATE, dtype=f32)
    lam_re = -0.5 + nrm(ks[5], (DEPTH, N_SSM_GROUPS, SSM_STATE), 0.01)
    lam_im = math.pi * n_idx[None, None, :] + nrm(ks[6], (DEPTH, N_SSM_GROUPS, SSM_STATE), 0.01)
    log_dt = jax.random.uniform(ks[7], (DEPTH, N_SSM_GROUPS), f32, math.log(DT_MIN), math.log(DT_MAX))
    return {
        "x": nrm(ks[0], (BATCH, SEQ, D_MODEL), 1.0),
        "norm_mix": 1.0 + nrm(ks[1], (DEPTH, D_MODEL), 0.02),
        "w_in": nrm(ks[2], (DEPTH, D_MODEL, D_MIX), D_MODEL ** -0.5),
        "w_pool": nrm(ks[3], (DEPTH, N_POOL_GROUPS, POOL_GROUP, POOL_GROUP), POOL_GROUP ** -0.5),
        "pool_scale": 1.0 + nrm(ks[4], (DEPTH, D_POOL), 0.02),
        "lam_re": lam_re,
        "lam_im": lam_im,
        "log_dt": log_dt,
        "b_re": nrm(ks[8], (DEPTH, N_SSM_GROUPS, SSM_STATE, SSM_GROUP), (2 * SSM_GROUP) ** -0.5),
        "b_im": nrm(ks[9], (DEPTH, N_SSM_GROUPS, SSM_STATE, SSM_GROUP), (2 * SSM_GROUP) ** -0.5),
        "c_re": nrm(ks[10], (DEPTH, N_SSM_GROUPS, SSM_GROUP, SSM_STATE), (2 * SSM_STATE) ** -0.5),
        "c_im": nrm(ks[11], (DEPTH, N_SSM_GROUPS, SSM_GROUP, SSM_STATE), (2 * SSM_STATE) ** -0.5),
        "d_skip": nrm(ks[12], (DEPTH, N_SSM_GROUPS, SSM_GROUP), 1.0),
        "w_glu": nrm(ks[13], (DEPTH, D_SSM, D_SSM), D_SSM ** -0.5),
        "b_glu": nrm(ks[14], (DEPTH, D_SSM), 0.01),
        "w_out": nrm(ks[15], (DEPTH, D_MIX, D_MODEL), D_MIX ** -0.5 * res_scale),
        "norm_ffn": 1.0 + nrm(ks[16], (DEPTH, D_MODEL), 0.02),
        "w_gate": nrm(ks[17], (DEPTH, D_MODEL, D_FF), D_MODEL ** -0.5),
        "w_up": nrm(ks[18], (DEPTH, D_MODEL, D_FF), D_MODEL ** -0.5),
        "w_down": nrm(ks[19], (DEPTH, D_FF, D_MODEL), D_FF ** -0.5 * res_scale),
        "norm_final": 1.0 + nrm(ks[20], (D_MODEL,), 0.02),
    }


def _fwd_reference(x, norm_mix, w_in, w_pool, pool_scale, lam_re, lam_im, log_dt, b_re, b_im,
              c_re, c_im, d_skip, w_glu, b_glu, w_out, norm_ffn, w_gate, w_up, w_down,
              norm_final):
    h = x
    for i in range(DEPTH):
        u = rmsnorm(h, norm_mix[i]) @ w_in[i]
        u_pool, u_ssm = u[..., :D_POOL], u[..., D_POOL:]
        y_pool = pool_mixer(u_pool, w_pool[i], pool_scale[i])
        y_ssm = ssm_mixer(u_ssm, lam_re[i], lam_im[i], log_dt[i], b_re[i], b_im[i],
                          c_re[i], c_im[i], d_skip[i], w_glu[i], b_glu[i])
        h = h + jnp.concatenate([y_pool, y_ssm], axis=-1) @ w_out[i]
        h = h + swiglu(rmsnorm(h, norm_ffn[i]), w_gate[i], w_up[i], w_down[i])
    return rmsnorm(h, norm_final)


import jax as _jax
import jax.numpy as _jnp

TWIN_FORMAT = 'train_step'
FWD_PARAMS = ['x', 'norm_mix', 'w_in', 'w_pool', 'pool_scale', 'lam_re', 'lam_im', 'log_dt', 'b_re', 'b_im', 'c_re', 'c_im', 'd_skip', 'w_glu', 'b_glu', 'w_out', 'norm_ffn', 'w_gate', 'w_up', 'w_down', 'norm_final']
TWIN_WEIGHTS = ['norm_mix', 'w_in', 'w_pool', 'pool_scale', 'lam_re', 'lam_im', 'log_dt', 'b_re', 'b_im', 'c_re', 'c_im', 'd_skip', 'w_glu', 'b_glu', 'w_out', 'norm_ffn', 'w_gate', 'w_up', 'w_down', 'norm_final']
TWIN_DIFF_INPUT = 'x'
TWIN_INPUTS = ['x', 'norm_mix', 'w_in', 'w_pool', 'pool_scale', 'lam_re', 'lam_im', 'log_dt', 'b_re', 'b_im', 'c_re', 'c_im', 'd_skip', 'w_glu', 'b_glu', 'w_out', 'norm_ffn', 'w_gate', 'w_up', 'w_down', 'norm_final', 'loss_target', 'm_norm_mix', 'm_w_in', 'm_w_pool', 'm_pool_scale', 'm_lam_re', 'm_lam_im', 'm_log_dt', 'm_b_re', 'm_b_im', 'm_c_re', 'm_c_im', 'm_d_skip', 'm_w_glu', 'm_b_glu', 'm_w_out', 'm_norm_ffn', 'm_w_gate', 'm_w_up', 'm_w_down', 'm_norm_final', 'v_norm_mix', 'v_w_in', 'v_w_pool', 'v_pool_scale', 'v_lam_re', 'v_lam_im', 'v_log_dt', 'v_b_re', 'v_b_im', 'v_c_re', 'v_c_im', 'v_d_skip', 'v_w_glu', 'v_b_glu', 'v_w_out', 'v_norm_ffn', 'v_w_gate', 'v_w_up', 'v_w_down', 'v_norm_final']
TWIN_OUTPUTS = ['loss', 'grad_x', 'grad_norm_mix', 'grad_w_in', 'grad_w_pool', 'grad_pool_scale', 'grad_lam_re', 'grad_lam_im', 'grad_log_dt', 'grad_b_re', 'grad_b_im', 'grad_c_re', 'grad_c_im', 'grad_d_skip', 'grad_w_glu', 'grad_b_glu', 'grad_w_out', 'grad_norm_ffn', 'grad_w_gate', 'grad_w_up', 'grad_w_down', 'grad_norm_final', 'delta_norm_mix', 'delta_w_in', 'delta_w_pool', 'delta_pool_scale', 'delta_lam_re', 'delta_lam_im', 'delta_log_dt', 'delta_b_re', 'delta_b_im', 'delta_c_re', 'delta_c_im', 'delta_d_skip', 'delta_w_glu', 'delta_b_glu', 'delta_w_out', 'delta_norm_ffn', 'delta_w_gate', 'delta_w_up', 'delta_w_down', 'delta_norm_final', 'new_m_norm_mix', 'new_m_w_in', 'new_m_w_pool', 'new_m_pool_scale', 'new_m_lam_re', 'new_m_lam_im', 'new_m_log_dt', 'new_m_b_re', 'new_m_b_im', 'new_m_c_re', 'new_m_c_im', 'new_m_d_skip', 'new_m_w_glu', 'new_m_b_glu', 'new_m_w_out', 'new_m_norm_ffn', 'new_m_w_gate', 'new_m_w_up', 'new_m_w_down', 'new_m_norm_final', 'new_v_norm_mix', 'new_v_w_in', 'new_v_w_pool', 'new_v_pool_scale', 'new_v_lam_re', 'new_v_lam_im', 'new_v_log_dt', 'new_v_b_re', 'new_v_b_im', 'new_v_c_re', 'new_v_c_im', 'new_v_d_skip', 'new_v_w_glu', 'new_v_b_glu', 'new_v_w_out', 'new_v_norm_ffn', 'new_v_w_gate', 'new_v_w_up', 'new_v_w_down', 'new_v_norm_final']
TWIN_LEAF_KINDS = {'loss': 'loss', 'grad_x': 'grad_x', 'grad_norm_mix': 'grad_w', 'grad_w_in': 'grad_w', 'grad_w_pool': 'grad_w', 'grad_pool_scale': 'grad_w', 'grad_lam_re': 'grad_w', 'grad_lam_im': 'grad_w', 'grad_log_dt': 'grad_w', 'grad_b_re': 'grad_w', 'grad_b_im': 'grad_w', 'grad_c_re': 'grad_w', 'grad_c_im': 'grad_w', 'grad_d_skip': 'grad_w', 'grad_w_glu': 'grad_w', 'grad_b_glu': 'grad_w', 'grad_w_out': 'grad_w', 'grad_norm_ffn': 'grad_w', 'grad_w_gate': 'grad_w', 'grad_w_up': 'grad_w', 'grad_w_down': 'grad_w', 'grad_norm_final': 'grad_w', 'delta_norm_mix': 'delta_w', 'delta_w_in': 'delta_w', 'delta_w_pool': 'delta_w', 'delta_pool_scale': 'delta_w', 'delta_lam_re': 'delta_w', 'delta_lam_im': 'delta_w', 'delta_log_dt': 'delta_w', 'delta_b_re': 'delta_w', 'delta_b_im': 'delta_w', 'delta_c_re': 'delta_w', 'delta_c_im': 'delta_w', 'delta_d_skip': 'delta_w', 'delta_w_glu': 'delta_w', 'delta_b_glu': 'delta_w', 'delta_w_out': 'delta_w', 'delta_norm_ffn': 'delta_w', 'delta_w_gate': 'delta_w', 'delta_w_up': 'delta_w', 'delta_w_down': 'delta_w', 'delta_norm_final': 'delta_w', 'new_m_norm_mix': 'new_m', 'new_m_w_in': 'new_m', 'new_m_w_pool': 'new_m', 'new_m_pool_scale': 'new_m', 'new_m_lam_re': 'new_m', 'new_m_lam_im': 'new_m', 'new_m_log_dt': 'new_m', 'new_m_b_re': 'new_m', 'new_m_b_im': 'new_m', 'new_m_c_re': 'new_m', 'new_m_c_im': 'new_m', 'new_m_d_skip': 'new_m', 'new_m_w_glu': 'new_m', 'new_m_b_glu': 'new_m', 'new_m_w_out': 'new_m', 'new_m_norm_ffn': 'new_m', 'new_m_w_gate': 'new_m', 'new_m_w_up': 'new_m', 'new_m_w_down': 'new_m', 'new_m_norm_final': 'new_m', 'new_v_norm_mix': 'new_v', 'new_v_w_in': 'new_v', 'new_v_w_pool': 'new_v', 'new_v_pool_scale': 'new_v', 'new_v_lam_re': 'new_v', 'new_v_lam_im': 'new_v', 'new_v_log_dt': 'new_v', 'new_v_b_re': 'new_v', 'new_v_b_im': 'new_v', 'new_v_c_re': 'new_v', 'new_v_c_im': 'new_v', 'new_v_d_skip': 'new_v', 'new_v_w_glu': 'new_v', 'new_v_b_glu': 'new_v', 'new_v_w_out': 'new_v', 'new_v_norm_ffn': 'new_v', 'new_v_w_gate': 'new_v', 'new_v_w_up': 'new_v', 'new_v_w_down': 'new_v', 'new_v_norm_final': 'new_v'}


def _forward(args):
    return _fwd_reference(*[args[k] for k in FWD_PARAMS])


def _output_shape():
    def fwd():
        inp = _fwd_setup_inputs(0)
        return _fwd_reference(*[inp[k] for k in FWD_PARAMS])
    out = _jax.eval_shape(fwd)
    return out.shape, out.dtype

N_MICROBATCH = 1
ADAM_LR = 0.001
ADAM_B1 = 0.9
ADAM_B2 = 0.999
ADAM_EPS = 1e-08
ADAM_WD = 0.01
ADAM_STEP = 10
PER_EXAMPLE_BATCH_AXIS = {'x': 0, 'loss_target': 0}
SHARED_INPUTS = []
_WEIGHT_DTYPES = {'norm_mix': _jnp.float32, 'w_in': _jnp.float32, 'w_pool': _jnp.float32, 'pool_scale': _jnp.float32, 'lam_re': _jnp.float32, 'lam_im': _jnp.float32, 'log_dt': _jnp.float32, 'b_re': _jnp.float32, 'b_im': _jnp.float32, 'c_re': _jnp.float32, 'c_im': _jnp.float32, 'd_skip': _jnp.float32, 'w_glu': _jnp.float32, 'b_glu': _jnp.float32, 'w_out': _jnp.float32, 'norm_ffn': _jnp.float32, 'w_gate': _jnp.float32, 'w_up': _jnp.float32, 'w_down': _jnp.float32, 'norm_final': _jnp.float32}
MOMENT_SCALE = {'norm_mix': 4.146824e-02, 'w_in': 4.083548e-02, 'w_pool': 5.330655e-02, 'pool_scale': 5.383482e-02, 'lam_re': 1.056461e-03, 'lam_im': 1.116271e-03, 'log_dt': 6.247190e-01, 'b_re': 6.905659e-04, 'b_im': 6.964063e-04, 'c_re': 1.372347e-03, 'c_im': 1.382281e-03, 'd_skip': 2.376532e-02, 'w_glu': 6.178143e-03, 'b_glu': 9.701080e-03, 'w_out': 1.148404e-01, 'norm_ffn': 5.038358e-02, 'w_gate': 2.156236e-02, 'w_up': 2.090164e-02, 'w_down': 9.792858e-02, 'norm_final': 3.204827e+01}


def _to_microbatches(a, axis):
    t = _jnp.moveaxis(a, axis, 0)
    t = t.reshape((N_MICROBATCH, t.shape[0] // N_MICROBATCH) + t.shape[1:])
    return _jnp.moveaxis(t, 1, axis + 1)


def setup_inputs(seed: int = 0) -> dict:
    inp = _fwd_setup_inputs(seed)
    key = _jax.random.fold_in(_jax.random.key(seed), 7919)
    shape, _ = _output_shape()
    out = dict(inp)
    out["loss_target"] = _jax.random.normal(_jax.random.fold_in(key, 0), shape, _jnp.float32)
    for i, name in enumerate(TWIN_WEIGHTS):
        w = inp[name].astype(_jnp.float32)
        if MOMENT_SCALE is None:
            s = _jnp.sqrt(_jnp.mean(_jnp.square(w)) + 1e-30)
        else:
            s = MOMENT_SCALE[name]
        km, kv = _jax.random.split(_jax.random.fold_in(key, i + 1))
        out[name] = w
        out["m_" + name] = s * _jax.random.normal(km, w.shape, _jnp.float32)
        out["v_" + name] = (s * s) * _jax.random.uniform(kv, w.shape, _jnp.float32, 0.5, 1.5)
    if N_MICROBATCH > 1:
        for name, axis in PER_EXAMPLE_BATCH_AXIS.items():
            out[name] = _to_microbatches(out[name], axis)
    return {'x': out['x'], 'norm_mix': out['norm_mix'], 'w_in': out['w_in'], 'w_pool': out['w_pool'], 'pool_scale': out['pool_scale'], 'lam_re': out['lam_re'], 'lam_im': out['lam_im'], 'log_dt': out['log_dt'], 'b_re': out['b_re'], 'b_im': out['b_im'], 'c_re': out['c_re'], 'c_im': out['c_im'], 'd_skip': out['d_skip'], 'w_glu': out['w_glu'], 'b_glu': out['b_glu'], 'w_out': out['w_out'], 'norm_ffn': out['norm_ffn'], 'w_gate': out['w_gate'], 'w_up': out['w_up'], 'w_down': out['w_down'], 'norm_final': out['norm_final'], 'loss_target': out['loss_target'], 'm_norm_mix': out['m_norm_mix'], 'm_w_in': out['m_w_in'], 'm_w_pool': out['m_w_pool'], 'm_pool_scale': out['m_pool_scale'], 'm_lam_re': out['m_lam_re'], 'm_lam_im': out['m_lam_im'], 'm_log_dt': out['m_log_dt'], 'm_b_re': out['m_b_re'], 'm_b_im': out['m_b_im'], 'm_c_re': out['m_c_re'], 'm_c_im': out['m_c_im'], 'm_d_skip': out['m_d_skip'], 'm_w_glu': out['m_w_glu'], 'm_b_glu': out['m_b_glu'], 'm_w_out': out['m_w_out'], 'm_norm_ffn': out['m_norm_ffn'], 'm_w_gate': out['m_w_gate'], 'm_w_up': out['m_w_up'], 'm_w_down': out['m_w_down'], 'm_norm_final': out['m_norm_final'], 'v_norm_mix': out['v_norm_mix'], 'v_w_in': out['v_w_in'], 'v_w_pool': out['v_w_pool'], 'v_pool_scale': out['v_pool_scale'], 'v_lam_re': out['v_lam_re'], 'v_lam_im': out['v_lam_im'], 'v_log_dt': out['v_log_dt'], 'v_b_re': out['v_b_re'], 'v_b_im': out['v_b_im'], 'v_c_re': out['v_c_re'], 'v_c_im': out['v_c_im'], 'v_d_skip': out['v_d_skip'], 'v_w_glu': out['v_w_glu'], 'v_b_glu': out['v_b_glu'], 'v_w_out': out['v_w_out'], 'v_norm_ffn': out['v_norm_ffn'], 'v_w_gate': out['v_w_gate'], 'v_w_up': out['v_w_up'], 'v_w_down': out['v_w_down'], 'v_norm_final': out['v_norm_final']}


def _loss(weights, diff, rest, loss_target):
    with _jax.named_scope("forward"):
        args = {**rest, TWIN_DIFF_INPUT: diff, **{k: w.astype(_WEIGHT_DTYPES[k]) for k, w in weights.items()}}
        y = _forward(args)
    with _jax.named_scope("loss_head"):
        err = _jnp.square(y.astype(_jnp.float32) - loss_target)
        return 0.5 * _jnp.sum(_jnp.mean(err, axis=-1)) if err.ndim else 0.5 * err


def _adamw(w, g, m, v):
    m = ADAM_B1 * m + (1.0 - ADAM_B1) * g
    v = ADAM_B2 * v + (1.0 - ADAM_B2) * _jnp.square(g)
    m_hat = m / (1.0 - ADAM_B1 ** ADAM_STEP)
    v_hat = v / (1.0 - ADAM_B2 ** ADAM_STEP)
    delta = -ADAM_LR * (m_hat / (_jnp.sqrt(v_hat) + ADAM_EPS) + ADAM_WD * w)
    return delta, m, v


def reference(x, norm_mix, w_in, w_pool, pool_scale, lam_re, lam_im, log_dt, b_re, b_im, c_re, c_im, d_skip, w_glu, b_glu, w_out, norm_ffn, w_gate, w_up, w_down, norm_final, loss_target, m_norm_mix, m_w_in, m_w_pool, m_pool_scale, m_lam_re, m_lam_im, m_log_dt, m_b_re, m_b_im, m_c_re, m_c_im, m_d_skip, m_w_glu, m_b_glu, m_w_out, m_norm_ffn, m_w_gate, m_w_up, m_w_down, m_norm_final, v_norm_mix, v_w_in, v_w_pool, v_pool_scale, v_lam_re, v_lam_im, v_log_dt, v_b_re, v_b_im, v_c_re, v_c_im, v_d_skip, v_w_glu, v_b_glu, v_w_out, v_norm_ffn, v_w_gate, v_w_up, v_w_down, v_norm_final):
    given = dict(x=x, norm_mix=norm_mix, w_in=w_in, w_pool=w_pool, pool_scale=pool_scale, lam_re=lam_re, lam_im=lam_im, log_dt=log_dt, b_re=b_re, b_im=b_im, c_re=c_re, c_im=c_im, d_skip=d_skip, w_glu=w_glu, b_glu=b_glu, w_out=w_out, norm_ffn=norm_ffn, w_gate=w_gate, w_up=w_up, w_down=w_down, norm_final=norm_final, loss_target=loss_target, m_norm_mix=m_norm_mix, m_w_in=m_w_in, m_w_pool=m_w_pool, m_pool_scale=m_pool_scale, m_lam_re=m_lam_re, m_lam_im=m_lam_im, m_log_dt=m_log_dt, m_b_re=m_b_re, m_b_im=m_b_im, m_c_re=m_c_re, m_c_im=m_c_im, m_d_skip=m_d_skip, m_w_glu=m_w_glu, m_b_glu=m_b_glu, m_w_out=m_w_out, m_norm_ffn=m_norm_ffn, m_w_gate=m_w_gate, m_w_up=m_w_up, m_w_down=m_w_down, m_norm_final=m_norm_final, v_norm_mix=v_norm_mix, v_w_in=v_w_in, v_w_pool=v_w_pool, v_pool_scale=v_pool_scale, v_lam_re=v_lam_re, v_lam_im=v_lam_im, v_log_dt=v_log_dt, v_b_re=v_b_re, v_b_im=v_b_im, v_c_re=v_c_re, v_c_im=v_c_im, v_d_skip=v_d_skip, v_w_glu=v_w_glu, v_b_glu=v_b_glu, v_w_out=v_w_out, v_norm_ffn=v_norm_ffn, v_w_gate=v_w_gate, v_w_up=v_w_up, v_w_down=v_w_down, v_norm_final=v_norm_final)
    weights = {n: given[n] for n in TWIN_WEIGHTS}
    shared = {n: given[n] for n in SHARED_INPUTS}
    per_example = {n: given[n] for n in ['x']}
    grad_fn = _jax.value_and_grad(_loss, argnums=(0, 1))

    def one_microbatch(ex, loss_target):
        ex = dict(ex)
        diff = ex.pop(TWIN_DIFF_INPUT)
        return grad_fn(weights, diff, {**shared, **ex}, loss_target)

    if N_MICROBATCH == 1:
        loss, (grad_w, grad_x) = one_microbatch(per_example, given["loss_target"])
    else:
        def body(carry, xs):
            loss_sum, grad_sum = carry
            l_k, (gw_k, gx_k) = one_microbatch(xs[0], xs[1])
            with _jax.named_scope("update"):
                return (loss_sum + l_k, _jax.tree.map(_jnp.add, grad_sum, gw_k)), gx_k

        init = (_jnp.zeros((), _jnp.float32), _jax.tree.map(_jnp.zeros_like, weights))
        (loss, grad_w), grad_x = _jax.lax.scan(body, init, (per_example, given["loss_target"]))
    with _jax.named_scope("update"):
        delta_w, new_m, new_v = {}, {}, {}
        for n in TWIN_WEIGHTS:
            delta_w[n], new_m[n], new_v[n] = _adamw(weights[n], grad_w[n], given["m_" + n], given["v_" + n])
    return (loss, grad_x, *[grad_w[n] for n in TWIN_WEIGHTS], *[delta_w[n] for n in TWIN_WEIGHTS],
            *[new_m[n] for n in TWIN_WEIGHTS], *[new_v[n] for n in TWIN_WEIGHTS])
```

```python
import functools
import math

import jax
import jax.numpy as jnp
from jax import lax
from jax.experimental import pallas as pl
from jax.experimental.pallas import tpu as pltpu

F32 = jnp.float32
BF16 = jnp.bfloat16

D_MODEL = 1024
D_POOL = 512
D_SSM = 512
POOL_WINDOWS = (2, 4, 8, 16)
POOL_GROUP = 128
POOL_HALO = 16
N_SSM_GROUPS = 32
SSM_GROUP = 16
SSM_STATE = 64
N_STATE = N_SSM_GROUPS * SSM_STATE
N_PAIRS = N_SSM_GROUPS // 2
D_FF = 2816
N_SHARD = 4
FF_SHARD = D_FF // N_SHARD
RMS_EPS = 1e-6

ADAM_LR = 0.001
ADAM_B1 = 0.9
ADAM_B2 = 0.999
ADAM_EPS = 1e-08
ADAM_WD = 0.01
ADAM_STEP = 10

P1_ROWS = 1280
P1_WD_BLK = (704, 0)
P1_GLU_BLK = (64, 11)
P1_IN_BLK = (256, 3)
P1_OUT_BLK = (256, 4)
P2_ROWS = 2048

SUBLANES = 8
VMEM_LIMIT = 56 * 1024 * 1024

TM = 512
TM_FFN = 512
TS = 256
SCAN_LANES = 256


def _cparams(n_axes):
    return pltpu.CompilerParams(dimension_semantics=("arbitrary",) * n_axes, vmem_limit_bytes=VMEM_LIMIT)


def _dot(a, b):
    return jnp.dot(a, b, preferred_element_type=F32)


def _dot_nt(a, b):
    return lax.dot_general(a, b, (((1,), (1,)), ((), ())), preferred_element_type=F32)


def _dot_tn(a, b):
    return lax.dot_general(a, b, (((0,), (0,)), ((), ())), preferred_element_type=F32)


def _rms_hat(x):
    r = lax.rsqrt(jnp.mean(x * x, axis=-1, keepdims=True) + RMS_EPS)
    return x * r, r


def _rms_bwd(d_hat, xhat, r):
    return r * (d_hat - xhat * jnp.mean(d_hat * xhat, axis=-1, keepdims=True))


def _sigmoid(x):
    return 1.0 / (1.0 + jnp.exp(-x))


_GELU_C = math.sqrt(2.0 / math.pi)
_GELU_K = 0.044715


def _gelu(x):
    return 0.5 * x * (1.0 + jnp.tanh(_GELU_C * (x + _GELU_K * x * x * x)))


def _gelu_grad(x):
    th = jnp.tanh(_GELU_C * (x + _GELU_K * x * x * x))
    return 0.5 * (1.0 + th) + 0.5 * x * (1.0 - th * th) * _GELU_C * (1.0 + 3.0 * _GELU_K * x * x)


def _glu_weight(ref):
    v = ref[...]
    return jnp.concatenate([v[:, :, :D_SSM], v[:, :, D_SSM:]], axis=1).reshape(D_SSM, D_SSM)


def _glu_pack(w):
    v = w.reshape(N_SHARD, 128, D_SSM)
    return jnp.concatenate([v[:, :64, :], v[:, 64:, :]], axis=2)


def _pool_diff(ext, row0, tm):
    rows = row0 + lax.broadcasted_iota(jnp.int32, (tm, 1), 0)
    outs = []
    for gi, w in enumerate(POOL_WINDOWS):
        e = ext[:, gi * POOL_GROUP:(gi + 1) * POOL_GROUP]
        s = e
        k = 1
        while k < w:
            s = s + pltpu.roll(s, k, 0)
            k *= 2
        inv = 1.0 / jnp.minimum(rows + 1, w).astype(F32)
        outs.append(s[POOL_HALO:, :] * inv - e[POOL_HALO:, :])
    return outs


def _mix_in_fwd(h, g1, wp1, layer, w_pool, scale):
    L = h.shape[0]
    tm = min(TM, L)

    def body(h_ref, g_ref, w_ref, wp_ref, sc_ref, u_ref, yp_ref, carry):
        i = pl.program_id(0)

        @pl.when(i == 0)
        def _():
            carry[...] = jnp.zeros_like(carry)

        xhat, _ = _rms_hat(h_ref[...])
        n1 = (xhat * g_ref[...]).astype(BF16)
        u = _dot(n1, w_ref[...].reshape(D_MODEL, D_MODEL))
        u_ref[...] = u
        up = u[:, :D_POOL]
        ext = jnp.concatenate([carry[...], up], axis=0)
        carry[...] = up[tm - POOL_HALO:, :]
        diffs = _pool_diff(ext, i * tm, tm)
        for gi in range(4):
            cols = slice(gi * POOL_GROUP, (gi + 1) * POOL_GROUP)
            yp_ref[:, cols] = _dot(diffs[gi].astype(BF16), wp_ref[gi]) * sc_ref[:, cols]

    blk, idx = P1_IN_BLK
    return pl.pallas_call(
        body, name="mix_in_fwd", grid=(L // tm,),
        in_specs=[pl.BlockSpec((tm, D_MODEL), lambda i: (i, 0)),
                  pl.BlockSpec((1, D_MODEL), lambda i: (0, 0)),
                  pl.BlockSpec((N_SHARD, None, blk, D_MODEL), lambda i: (0, layer, idx, 0)),
                  pl.BlockSpec((4, POOL_GROUP, POOL_GROUP), lambda i: (0, 0, 0)),
                  pl.BlockSpec((1, D_POOL), lambda i: (0, 0))],
        out_specs=[pl.BlockSpec((tm, D_MODEL), lambda i: (i, 0)),
                   pl.BlockSpec((tm, D_POOL), lambda i: (i, 0))],
        out_shape=[jax.ShapeDtypeStruct((L, D_MODEL), F32), jax.ShapeDtypeStruct((L, D_POOL), F32)],
        scratch_shapes=[pltpu.VMEM((POOL_HALO, D_POOL), F32)],
        compiler_params=_cparams(1),
    )(h, g1, wp1, w_pool, scale)


def _cmul(xr, xi, yr, yi):
    return xr * yr - xi * yi, xr * yi + xi * yr


def _scan_tables(ar, ai, tab, reverse):
    c = ar.shape[1]
    row = lax.broadcasted_iota(jnp.int32, (SUBLANES, c), 0)
    a2r, a2i = _cmul(ar, ai, ar, ai)
    a4r, a4i = _cmul(a2r, a2i, a2r, a2i)
    zero = jnp.zeros((SUBLANES, c), F32)
    for n, (s, pr, pi) in enumerate(((1, ar, ai), (2, a2r, a2i), (4, a4r, a4i))):
        keep = (row < SUBLANES - s) if reverse else (row >= s)
        tab[2 * n] = jnp.where(keep, pr, zero)
        tab[2 * n + 1] = jnp.where(keep, pi, zero)
    cr, ci = ar, ai
    tr, ti = zero, zero
    for n in range(SUBLANES):
        at = (SUBLANES - 1 - n) if reverse else n
        tr = jnp.where(row == at, cr, tr)
        ti = jnp.where(row == at, ci, ti)
        cr, ci = _cmul(cr, ci, ar, ai)
    tab[6] = tr
    tab[7] = ti


def _ssm_fwd(u, bpad, cpad, ar, ai, dskip):
    L = u.shape[0]
    ts = min(TS, L)
    nq = 4
    cq = N_STATE // nq

    def body(u_ref, bp_ref, cp_ref, ar_ref, ai_ref, dsk_ref, sre_ref, sim_ref, y_ref, cr, ci, tab):
        t = pl.program_id(1)

        @pl.when(t == 0)
        def _():
            cr[...] = jnp.zeros_like(cr)
            ci[...] = jnp.zeros_like(ci)
            _scan_tables(ar_ref[...], ai_ref[...], tab, reverse=False)

        uf = u_ref[...]
        ub = uf.astype(BF16)
        for jj in range(4):
            bu = _dot(ub, bp_ref[jj])
            sre_ref[:, jj * 128:(jj + 1) * 128] = bu[:, :128]
            sim_ref[:, jj * 128:(jj + 1) * 128] = bu[:, 128:]

        for cc in range(cq // SCAN_LANES):
            cols = slice(cc * SCAN_LANES, (cc + 1) * SCAN_LANES)
            steps = [(s, tab[2 * n, :, cols], tab[2 * n + 1, :, cols]) for n, s in enumerate((1, 2, 4))]
            pr, pi = tab[6, :, cols], tab[7, :, cols]

            def step(i, carry, cols=cols, steps=steps, pr=pr, pi=pi):
                c_r, c_i = carry
                r0 = pl.multiple_of(i * SUBLANES, SUBLANES)
                xr = sre_ref[pl.ds(r0, SUBLANES), cols]
                xi = sim_ref[pl.ds(r0, SUBLANES), cols]
                for s, tr, ti in steps:
                    rr = pltpu.roll(xr, s, 0)
                    ri = pltpu.roll(xi, s, 0)
                    xr, xi = xr + tr * rr - ti * ri, xi + tr * ri + ti * rr
                xr, xi = xr + pr * c_r - pi * c_i, xi + pr * c_i + pi * c_r
                sre_ref[pl.ds(r0, SUBLANES), cols] = xr
                sim_ref[pl.ds(r0, SUBLANES), cols] = xi
                shp = (SUBLANES, SCAN_LANES)
                return (jnp.broadcast_to(xr[SUBLANES - 1:, :], shp), jnp.broadcast_to(xi[SUBLANES - 1:, :], shp))

            c_r, c_i = lax.fori_loop(0, ts // SUBLANES, step, (cr[:, cols], ci[:, cols]))
            cr[:, cols] = c_r
            ci[:, cols] = c_i

        acc = dsk_ref[...] * uf
        for jj in range(4):
            cols = slice(jj * 128, (jj + 1) * 128)
            scat = jnp.concatenate([sre_ref[:, cols], sim_ref[:, cols]], axis=1).astype(BF16)
            acc = acc + _dot(scat, cp_ref[jj])
        y_ref[...] = acc

    return pl.pallas_call(
        body, name="ssm_fwd", grid=(nq, L // ts),
        in_specs=[pl.BlockSpec((ts, 128), lambda q, t: (t, 4 + q)),
                  pl.BlockSpec((4, 128, 256), lambda q, t: (q, 0, 0)),
                  pl.BlockSpec((4, 256, 128), lambda q, t: (q, 0, 0)),
                  pl.BlockSpec((1, cq), lambda q, t: (0, q)),
                  pl.BlockSpec((1, cq), lambda q, t: (0, q)),
                  pl.BlockSpec((1, 128), lambda q, t: (0, q))],
        out_specs=[pl.BlockSpec((ts, cq), lambda q, t: (t, q)),
                   pl.BlockSpec((ts, cq), lambda q, t: (t, q)),
                   pl.BlockSpec((ts, 128), lambda q, t: (t, q))],
        out_shape=[jax.ShapeDtypeStruct((L, N_STATE), F32), jax.ShapeDtypeStruct((L, N_STATE), F32),
                   jax.ShapeDtypeStruct((L, D_SSM), F32)],
        scratch_shapes=[pltpu.VMEM((SUBLANES, cq), F32), pltpu.VMEM((SUBLANES, cq), F32),
                        pltpu.VMEM((8, SUBLANES, cq), F32)],
        compiler_params=_cparams(2),
    )(u, bpad, cpad, ar, ai, dskip)


def _mix_out_fwd(yraw, ypool, h, wp1, layer, b_glu):
    L = h.shape[0]
    tm = min(TM, L)

    def body(yr_ref, yp_ref, h_ref, wglu_ref, b_ref, wout_ref, o_ref):
        y = _gelu(yr_ref[...])
        z = _dot(y.astype(BF16), _glu_weight(wglu_ref)) + b_ref[...]
        o = y * _sigmoid(z)
        mix = jnp.concatenate([yp_ref[...], o], axis=1).astype(BF16)
        o_ref[...] = h_ref[...] + _dot(mix, wout_ref[...].reshape(D_MODEL, D_MODEL))

    gb, gi = P1_GLU_BLK
    ob, oi = P1_OUT_BLK
    return pl.pallas_call(
        body, name="mix_out_fwd", grid=(L // tm,),
        in_specs=[pl.BlockSpec((tm, D_SSM), lambda i: (i, 0)),
                  pl.BlockSpec((tm, D_POOL), lambda i: (i, 0)),
                  pl.BlockSpec((tm, D_MODEL), lambda i: (i, 0)),
                  pl.BlockSpec((N_SHARD, None, gb, D_MODEL), lambda i: (0, layer, gi, 0)),
                  pl.BlockSpec((1, D_SSM), lambda i: (0, 0)),
                  pl.BlockSpec((N_SHARD, None, ob, D_MODEL), lambda i: (0, layer, oi, 0))],
        out_specs=pl.BlockSpec((tm, D_MODEL), lambda i: (i, 0)),
        out_shape=jax.ShapeDtypeStruct((L, D_MODEL), F32),
        compiler_params=_cparams(1),
    )(yraw, ypool, h, wp1, b_glu, wp1)


def _ffn_fwd(h, g2, wp1, wp2, layer):
    L = h.shape[0]
    tm = min(TM_FFN, L)

    def body(h_ref, g_ref, wgu_ref, wd_ref, o_ref, n2_ref, gate_ref, up_ref):
        k = pl.program_id(1)

        @pl.when(k == 0)
        def _():
            x = h_ref[...]
            xhat, _ = _rms_hat(x)
            n2_ref[...] = (xhat * g_ref[...]).astype(BF16)
            o_ref[...] = x

        n2 = n2_ref[...]
        gate = _dot(n2, wgu_ref[:D_MODEL, :])
        up = _dot(n2, wgu_ref[D_MODEL:, :])
        gate_ref[...] = gate.astype(BF16)
        up_ref[...] = up.astype(BF16)
        act = (gate * _sigmoid(gate) * up).astype(BF16)
        o_ref[...] += _dot(act, wd_ref[...])

    wb, wi = P1_WD_BLK
    act_shape = jax.ShapeDtypeStruct((N_SHARD, L, FF_SHARD), BF16)
    return pl.pallas_call(
        body, name="ffn_fwd", grid=(L // tm, N_SHARD),
        in_specs=[pl.BlockSpec((tm, D_MODEL), lambda m, k: (m, 0)),
                  pl.BlockSpec((1, D_MODEL), lambda m, k: (0, 0)),
                  pl.BlockSpec((None, None, P2_ROWS, FF_SHARD), lambda m, k: (k, layer, 0, 0)),
                  pl.BlockSpec((None, None, wb, D_MODEL), lambda m, k: (k, layer, wi, 0))],
        out_specs=[pl.BlockSpec((tm, D_MODEL), lambda m, k: (m, 0)),
                   pl.BlockSpec((tm, D_MODEL), lambda m, k: (m, 0)),
                   pl.BlockSpec((None, tm, FF_SHARD), lambda m, k: (k, m, 0)),
                   pl.BlockSpec((None, tm, FF_SHARD), lambda m, k: (k, m, 0))],
        out_shape=[jax.ShapeDtypeStruct((L, D_MODEL), F32), jax.ShapeDtypeStruct((L, D_MODEL), BF16),
                   act_shape, act_shape],
        compiler_params=_cparams(2),
    )(h, g2, wp2, wp1)


def _final_fwd_bwd(h, gf, target):
    L = h.shape[0]
    tm = min(TM, L)

    def body(h_ref, g_ref, t_ref, dh_ref, loss_ref, dg_ref):
        i = pl.program_id(0)

        @pl.when(i == 0)
        def _():
            loss_ref[...] = jnp.zeros_like(loss_ref)
            dg_ref[...] = jnp.zeros_like(dg_ref)

        xhat, r = _rms_hat(h_ref[...])
        g = g_ref[...]
        e = xhat * g - t_ref[...]
        loss_ref[...] += 0.5 * jnp.sum(jnp.mean(e * e, axis=-1, keepdims=True), axis=0, keepdims=True)
        dy = e * (1.0 / D_MODEL)
        dg_ref[...] += jnp.sum(dy * xhat, axis=0, keepdims=True)
        dh_ref[...] = _rms_bwd(dy * g, xhat, r)

    return pl.pallas_call(
        body, name="final_fwd_bwd", grid=(L // tm,),
        in_specs=[pl.BlockSpec((tm, D_MODEL), lambda i: (i, 0)),
                  pl.BlockSpec((1, D_MODEL), lambda i: (0, 0)),
                  pl.BlockSpec((tm, D_MODEL), lambda i: (i, 0))],
        out_specs=[pl.BlockSpec((tm, D_MODEL), lambda i: (i, 0)),
                   pl.BlockSpec((1, 1), lambda i: (0, 0)),
                   pl.BlockSpec((1, D_MODEL), lambda i: (0, 0))],
        out_shape=[jax.ShapeDtypeStruct((L, D_MODEL), F32), jax.ShapeDtypeStruct((1, 1), F32),
                   jax.ShapeDtypeStruct((1, D_MODEL), F32)],
        compiler_params=_cparams(1),
    )(h, gf, target)


def _ffn_bwd_act(dh, h, g2, gate_s, up_s, wp1, wp2, layer):
    L = h.shape[0]
    tm = min(TM_FFN, L)

    def body(dh_ref, h_ref, g_ref, gate_ref, up_ref, wgu_ref, wd_ref,
             dhm_ref, dg_ref, dgate_ref, dup_ref, act_ref, dhb_ref, dn2):
        m, k = pl.program_id(0), pl.program_id(1)

        @pl.when(jnp.logical_and(m == 0, k == 0))
        def _():
            dg_ref[...] = jnp.zeros_like(dg_ref)

        @pl.when(k == 0)
        def _():
            dhb_ref[...] = dh_ref[...].astype(BF16)
            dn2[...] = jnp.zeros_like(dn2)

        dact = _dot_nt(dhb_ref[...], wd_ref[...])
        gate = gate_ref[...].astype(F32)
        up = up_ref[...].astype(F32)
        sg = _sigmoid(gate)
        silu = gate * sg
        dgate = (dact * up * (sg * (1.0 + gate * (1.0 - sg)))).astype(BF16)
        dup = (dact * silu).astype(BF16)
        dgate_ref[...] = dgate
        dup_ref[...] = dup
        act_ref[...] = (silu * up).astype(BF16)
        dn2[...] += _dot_nt(dgate, wgu_ref[:D_MODEL, :]) + _dot_nt(dup, wgu_ref[D_MODEL:, :])

        @pl.when(k == N_SHARD - 1)
        def _():
            xhat, r = _rms_hat(h_ref[...])
            d = dn2[...]
            dg_ref[...] += jnp.sum(d * xhat, axis=0, keepdims=True)
            dhm_ref[...] = dh_ref[...] + _rms_bwd(d * g_ref[...], xhat, r)

    wb, wi = P1_WD_BLK
    act_spec = pl.BlockSpec((None, tm, FF_SHARD), lambda m, k: (k, m, 0))
    act_shape = jax.ShapeDtypeStruct((N_SHARD, L, FF_SHARD), BF16)
    row_spec = pl.BlockSpec((tm, D_MODEL), lambda m, k: (m, 0))
    return pl.pallas_call(
        body, name="ffn_bwd_act", grid=(L // tm, N_SHARD),
        in_specs=[row_spec, row_spec,
                  pl.BlockSpec((1, D_MODEL), lambda m, k: (0, 0)),
                  act_spec, act_spec,
                  pl.BlockSpec((None, None, P2_ROWS, FF_SHARD), lambda m, k: (k, layer, 0, 0)),
                  pl.BlockSpec((None, None, wb, D_MODEL), lambda m, k: (k, layer, wi, 0))],
        out_specs=[row_spec,
                   pl.BlockSpec((1, D_MODEL), lambda m, k: (0, 0)),
                   act_spec, act_spec, act_spec, row_spec],
        out_shape=[jax.ShapeDtypeStruct((L, D_MODEL), F32), jax.ShapeDtypeStruct((1, D_MODEL), F32),
                   act_shape, act_shape, act_shape, jax.ShapeDtypeStruct((L, D_MODEL), BF16)],
        scratch_shapes=[pltpu.VMEM((tm, D_MODEL), F32)],
        compiler_params=_cparams(2),
    )(dh, h, g2, gate_s, up_s, wp2, wp1)


def _ffn_bwd_w(n2, dgate_s, dup_s, act_s, dhb, g1buf, g2buf, layer):
    L = n2.shape[0]
    tm = min(TM_FFN, L)

    def body(n2_ref, dgate_ref, dup_ref, act_ref, dhb_ref, g1_in, g2_in, g1_ref, g2_ref):
        m = pl.program_id(1)

        @pl.when(m == 0)
        def _():
            g1_ref[...] = jnp.zeros_like(g1_ref)
            g2_ref[...] = jnp.zeros_like(g2_ref)

        n2v = n2_ref[...]
        g2_ref[:D_MODEL, :] += _dot_tn(n2v, dgate_ref[...])
        g2_ref[D_MODEL:, :] += _dot_tn(n2v, dup_ref[...])
        g1_ref[...] += _dot_tn(act_ref[...], dhb_ref[...])

    wb, wi = P1_WD_BLK
    act_spec = pl.BlockSpec((None, tm, FF_SHARD), lambda k, m: (k, m, 0))
    row_spec = pl.BlockSpec((tm, D_MODEL), lambda k, m: (m, 0))
    any_spec = pl.BlockSpec(memory_space=pl.ANY)
    return pl.pallas_call(
        body, name="ffn_bwd_w", grid=(N_SHARD, L // tm),
        in_specs=[row_spec, act_spec, act_spec, act_spec, row_spec, any_spec, any_spec],
        out_specs=[pl.BlockSpec((None, None, wb, D_MODEL), lambda k, m: (layer, k, wi, 0)),
                   pl.BlockSpec((None, None, P2_ROWS, FF_SHARD), lambda k, m: (layer, k, 0, 0))],
        out_shape=[jax.ShapeDtypeStruct(g1buf.shape, F32), jax.ShapeDtypeStruct(g2buf.shape, F32)],
        input_output_aliases={5: 0, 6: 1},
        compiler_params=_cparams(2),
    )(n2, dgate_s, dup_s, act_s, dhb, g1buf, g2buf)


def _mix_out_bwd(dhm, yraw, ypool, wp1, layer, b_glu, g1buf):
    L = dhm.shape[0]
    tm = min(TM, L)

    def body(dhm_ref, yr_ref, yp_ref, wglu_ref, b_ref, wout_ref, g1_in,
             dyr_ref, dyp_ref, db_ref, g1_ref, dwout, dwglu, gpack):
        i = pl.program_id(0)

        @pl.when(i == 0)
        def _():
            db_ref[...] = jnp.zeros_like(db_ref)
            dwout[...] = jnp.zeros_like(dwout)
            dwglu[...] = jnp.zeros_like(dwglu)

        dhb = dhm_ref[...].astype(BF16)
        wglu = _glu_weight(wglu_ref)
        dmix = _dot_nt(dhb, wout_ref[...].reshape(D_MODEL, D_MODEL))
        dyp_ref[...] = dmix[:, :D_POOL]
        d_o = dmix[:, D_POOL:]
        yraw_v = yr_ref[...]
        y = _gelu(yraw_v)
        yb = y.astype(BF16)
        sig = _sigmoid(_dot(yb, wglu) + b_ref[...])
        mix = jnp.concatenate([yp_ref[...], y * sig], axis=1).astype(BF16)
        dwout[...] += _dot_tn(mix, dhb).reshape(N_SHARD, 256, D_MODEL)
        dz = d_o * y * sig * (1.0 - sig)
        dzb = dz.astype(BF16)
        db_ref[...] += jnp.sum(dz, axis=0, keepdims=True)
        dwglu[...] += _dot_tn(yb, dzb)
        dy = d_o * sig + _dot_nt(dzb, wglu)
        dyr_ref[...] = dy * _gelu_grad(yraw_v)

        @pl.when(i == n_steps - 1)
        def _():
            gpack[...] = _glu_pack(dwglu[...])
            pltpu.sync_copy(gpack, g1_ref.at[layer, :, pl.ds(gb * gi, gb), :])
            pltpu.sync_copy(dwout, g1_ref.at[layer, :, pl.ds(ob * oi, ob), :])

    gb, gi = P1_GLU_BLK
    ob, oi = P1_OUT_BLK
    n_steps = L // tm
    return pl.pallas_call(
        body, name="mix_out_bwd", grid=(n_steps,),
        in_specs=[pl.BlockSpec((tm, D_MODEL), lambda i: (i, 0)),
                  pl.BlockSpec((tm, D_SSM), lambda i: (i, 0)),
                  pl.BlockSpec((tm, D_POOL), lambda i: (i, 0)),
                  pl.BlockSpec((N_SHARD, None, gb, D_MODEL), lambda i: (0, layer, gi, 0)),
                  pl.BlockSpec((1, D_SSM), lambda i: (0, 0)),
                  pl.BlockSpec((N_SHARD, None, ob, D_MODEL), lambda i: (0, layer, oi, 0)),
                  pl.BlockSpec(memory_space=pl.ANY)],
        out_specs=[pl.BlockSpec((tm, D_SSM), lambda i: (i, 0)),
                   pl.BlockSpec((tm, D_POOL), lambda i: (i, 0)),
                   pl.BlockSpec((1, D_SSM), lambda i: (0, 0)),
                   pl.BlockSpec(memory_space=pl.ANY)],
        out_shape=[jax.ShapeDtypeStruct((L, D_SSM), F32), jax.ShapeDtypeStruct((L, D_POOL), F32),
                   jax.ShapeDtypeStruct((1, D_SSM), F32),
                   jax.ShapeDtypeStruct(g1buf.shape, F32)],
        scratch_shapes=[pltpu.VMEM((N_SHARD, ob, D_MODEL), F32), pltpu.VMEM((D_SSM, D_SSM), F32),
                        pltpu.VMEM((N_SHARD, gb, D_MODEL), F32)],
        input_output_aliases={6: 3},
        compiler_params=_cparams(1),
    )(dhm, yraw, ypool, wp1, b_glu, wp1, g1buf)


def _ssm_bwd(dyraw, u, sre, sim, cpad_t, bpad_t, ar, ai, dskip):
    L = u.shape[0]
    ts = min(TS, L)
    nt = L // ts
    nq = 4
    cq = N_STATE // nq

    def body(dy_ref, u_ref, sre_ref, sim_ref, ct_ref, bt_ref, ar_ref, ai_ref, dsk_ref,
             du_ref, dcp_ref, dbp_ref, dar_ref, dai_ref, ddsk_ref, gre, gim, cr, ci, tab, accr, acci):
        t = pl.program_id(1)

        @pl.when(t == 0)
        def _():
            for ref in (cr, ci, accr, acci, dcp_ref, dbp_ref, ddsk_ref):
                ref[...] = jnp.zeros_like(ref)
            _scan_tables(ar_ref[...], -ai_ref[...], tab, reverse=True)

        dy = dy_ref[...]
        dyb = dy.astype(BF16)
        uf = u_ref[...]
        ub = uf.astype(BF16)
        for jj in range(4):
            cols = slice(jj * 128, (jj + 1) * 128)
            ds = _dot(dyb, ct_ref[jj])
            gre[:, cols] = ds[:, :128]
            gim[:, cols] = ds[:, 128:]
            scat = jnp.concatenate([sre_ref[:, cols], sim_ref[:, cols]], axis=1).astype(BF16)
            dcp_ref[jj] += _dot_tn(scat, dyb)

        n_grp = ts // SUBLANES
        shp = (SUBLANES, SCAN_LANES)
        last_row = lax.broadcasted_iota(jnp.int32, shp, 0) == SUBLANES - 1
        for cc in range(cq // SCAN_LANES):
            cols = slice(cc * SCAN_LANES, (cc + 1) * SCAN_LANES)
            steps = [(s, tab[2 * n, :, cols], tab[2 * n + 1, :, cols]) for n, s in enumerate((1, 2, 4))]
            qr, qi = tab[6, :, cols], tab[7, :, cols]

            def step(i, carry, cols=cols, steps=steps, qr=qr, qi=qi):
                c_r, c_i, a_r, a_i = carry
                r0 = pl.multiple_of((n_grp - 1 - i) * SUBLANES, SUBLANES)
                xr = gre[pl.ds(r0, SUBLANES), cols]
                xi = gim[pl.ds(r0, SUBLANES), cols]
                for s, tr, ti in steps:
                    rr = pltpu.roll(xr, SUBLANES - s, 0)
                    ri = pltpu.roll(xi, SUBLANES - s, 0)
                    xr, xi = xr + tr * rr - ti * ri, xi + tr * ri + ti * rr
                xr, xi = xr + qr * c_r - qi * c_i, xi + qr * c_i + qi * c_r
                gre[pl.ds(r0, SUBLANES), cols] = xr
                gim[pl.ds(r0, SUBLANES), cols] = xi
                nr = jnp.where(last_row, c_r, pltpu.roll(xr, SUBLANES - 1, 0))
                ni = jnp.where(last_row, c_i, pltpu.roll(xi, SUBLANES - 1, 0))
                sr = sre_ref[pl.ds(r0, SUBLANES), cols]
                si = sim_ref[pl.ds(r0, SUBLANES), cols]
                a_r = a_r + sr * nr + si * ni
                a_i = a_i + sr * ni - si * nr
                return (jnp.broadcast_to(xr[:1, :], shp), jnp.broadcast_to(xi[:1, :], shp), a_r, a_i)

            c_r, c_i, a_r, a_i = lax.fori_loop(
                0, n_grp, step, (cr[:, cols], ci[:, cols], accr[:, cols], acci[:, cols]))
            cr[:, cols] = c_r
            ci[:, cols] = c_i
            accr[:, cols] = a_r
            acci[:, cols] = a_i

        acc = dsk_ref[...] * dy
        for jj in range(4):
            cols = slice(jj * 128, (jj + 1) * 128)
            gcat = jnp.concatenate([gre[:, cols], gim[:, cols]], axis=1).astype(BF16)
            acc = acc + _dot(gcat, bt_ref[jj])
            dbp_ref[jj] += _dot_tn(ub, gcat)
        du_ref[...] = acc
        ddsk_ref[...] += jnp.sum(dy * uf, axis=0, keepdims=True)

        @pl.when(t == nt - 1)
        def _():
            dar_ref[...] = jnp.sum(accr[...], axis=0, keepdims=True)
            dai_ref[...] = jnp.sum(acci[...], axis=0, keepdims=True)

    f32_scr = lambda *s: pltpu.VMEM(s, F32)
    return pl.pallas_call(
        body, name="ssm_bwd", grid=(nq, nt),
        in_specs=[pl.BlockSpec((ts, 128), lambda q, t: (nt - 1 - t, q)),
                  pl.BlockSpec((ts, 128), lambda q, t: (nt - 1 - t, 4 + q)),
                  pl.BlockSpec((ts, cq), lambda q, t: (nt - 1 - t, q)),
                  pl.BlockSpec((ts, cq), lambda q, t: (nt - 1 - t, q)),
                  pl.BlockSpec((4, 128, 256), lambda q, t: (q, 0, 0)),
                  pl.BlockSpec((4, 256, 128), lambda q, t: (q, 0, 0)),
                  pl.BlockSpec((1, cq), lambda q, t: (0, q)),
                  pl.BlockSpec((1, cq), lambda q, t: (0, q)),
                  pl.BlockSpec((1, 128), lambda q, t: (0, q))],
        out_specs=[pl.BlockSpec((ts, 128), lambda q, t: (nt - 1 - t, q)),
                   pl.BlockSpec((4, 256, 128), lambda q, t: (q, 0, 0)),
                   pl.BlockSpec((4, 128, 256), lambda q, t: (q, 0, 0)),
                   pl.BlockSpec((1, cq), lambda q, t: (0, q)),
                   pl.BlockSpec((1, cq), lambda q, t: (0, q)),
                   pl.BlockSpec((1, 128), lambda q, t: (0, q))],
        out_shape=[jax.ShapeDtypeStruct((L, D_SSM), F32),
                   jax.ShapeDtypeStruct((N_PAIRS, 256, 128), F32), jax.ShapeDtypeStruct((N_PAIRS, 128, 256), F32),
                   jax.ShapeDtypeStruct((1, N_STATE), F32), jax.ShapeDtypeStruct((1, N_STATE), F32),
                   jax.ShapeDtypeStruct((1, D_SSM), F32)],
        scratch_shapes=[f32_scr(ts, cq), f32_scr(ts, cq), f32_scr(SUBLANES, cq), f32_scr(SUBLANES, cq),
                        f32_scr(8, SUBLANES, cq), f32_scr(SUBLANES, cq), f32_scr(SUBLANES, cq)],
        compiler_params=_cparams(2),
    )(dyraw, u, sre, sim, cpad_t, bpad_t, ar, ai, dskip)


def _pool_bwd(dyp, u, w_pool, scale):
    L = u.shape[0]
    tm = min(TM, L)
    nt = L // tm
    halo_per_tile = tm // POOL_HALO

    def body(dyp_ref, u_ref, halo_ref, wp_ref, sc_ref, du_ref, dwp_ref, dsc_ref, carry):
        i = pl.program_id(0)
        tile = nt - 1 - i

        @pl.when(i == 0)
        def _():
            carry[...] = jnp.zeros_like(carry)
            dwp_ref[...] = jnp.zeros_like(dwp_ref)
            dsc_ref[...] = jnp.zeros_like(dsc_ref)

        up = u_ref[...]
        halo = jnp.where(tile > 0, halo_ref[...], jnp.zeros_like(halo_ref))
        diffs = _pool_diff(jnp.concatenate([halo, up], axis=0), tile * tm, tm)
        rows = tile * tm + lax.broadcasted_iota(jnp.int32, (tm, 1), 0)
        n_ext = tm + POOL_HALO
        for gi, w in enumerate(POOL_WINDOWS):
            cols = slice(gi * POOL_GROUP, (gi + 1) * POOL_GROUP)
            db = diffs[gi].astype(BF16)
            dyp = dyp_ref[:, cols]
            dsc_ref[:, cols] += jnp.sum(dyp * _dot(db, wp_ref[gi]), axis=0, keepdims=True)
            dp = (dyp * sc_ref[:, cols]).astype(BF16)
            ddiff = _dot_nt(dp, wp_ref[gi])
            dwp_ref[gi] += _dot_tn(db, dp)
            e = ddiff * (1.0 / jnp.minimum(rows + 1, w).astype(F32))
            s = jnp.concatenate([e, carry[:, cols]], axis=0)
            k = 1
            while k < w:
                s = s + pltpu.roll(s, n_ext - k, 0)
                k *= 2
            du_ref[:, cols] = s[:tm, :] - ddiff
            carry[:, cols] = e[:POOL_HALO, :]

    return pl.pallas_call(
        body, name="pool_bwd", grid=(nt,),
        in_specs=[pl.BlockSpec((tm, D_POOL), lambda i: (nt - 1 - i, 0)),
                  pl.BlockSpec((tm, D_POOL), lambda i: (nt - 1 - i, 0)),
                  pl.BlockSpec((POOL_HALO, D_POOL), lambda i: (jnp.maximum((nt - 1 - i) * halo_per_tile - 1, 0), 0)),
                  pl.BlockSpec((4, POOL_GROUP, POOL_GROUP), lambda i: (0, 0, 0)),
                  pl.BlockSpec((1, D_POOL), lambda i: (0, 0))],
        out_specs=[pl.BlockSpec((tm, D_POOL), lambda i: (nt - 1 - i, 0)),
                   pl.BlockSpec((4, POOL_GROUP, POOL_GROUP), lambda i: (0, 0, 0)),
                   pl.BlockSpec((1, D_POOL), lambda i: (0, 0))],
        out_shape=[jax.ShapeDtypeStruct((L, D_POOL), F32),
                   jax.ShapeDtypeStruct((4, POOL_GROUP, POOL_GROUP), F32),
                   jax.ShapeDtypeStruct((1, D_POOL), F32)],
        scratch_shapes=[pltpu.VMEM((POOL_HALO, D_POOL), F32)],
        compiler_params=_cparams(1),
    )(dyp, u, u, w_pool, scale)


def _mix_in_bwd(dup, dus, h, dhm, g1, wp1, layer, g1buf):
    L = h.shape[0]
    tm = min(TM, L)
    n_steps = L // tm
    blk, idx = P1_IN_BLK

    def body(dup_ref, dus_ref, h_ref, dhm_ref, g_ref, w_ref, g1_in, dh_ref, dg_ref, g1_ref, dwin):
        i = pl.program_id(0)

        @pl.when(i == 0)
        def _():
            dg_ref[...] = jnp.zeros_like(dg_ref)
            dwin[...] = jnp.zeros_like(dwin)

        du = jnp.concatenate([dup_ref[...], dus_ref[...]], axis=1).astype(BF16)
        dn1 = _dot_nt(du, w_ref[...].reshape(D_MODEL, D_MODEL))
        xhat, r = _rms_hat(h_ref[...])
        g = g_ref[...]
        n1 = (xhat * g).astype(BF16)
        dwin[...] += _dot_tn(n1, du).reshape(N_SHARD, blk, D_MODEL)
        dg_ref[...] += jnp.sum(dn1 * xhat, axis=0, keepdims=True)
        dh_ref[...] = dhm_ref[...] + _rms_bwd(dn1 * g, xhat, r)

        @pl.when(i == n_steps - 1)
        def _():
            pltpu.sync_copy(dwin, g1_ref.at[layer, :, pl.ds(blk * idx, blk), :])

    row_spec = pl.BlockSpec((tm, D_MODEL), lambda i: (i, 0))
    half_spec = pl.BlockSpec((tm, D_POOL), lambda i: (i, 0))
    return pl.pallas_call(
        body, name="mix_in_bwd", grid=(n_steps,),
        in_specs=[half_spec, half_spec, row_spec, row_spec,
                  pl.BlockSpec((1, D_MODEL), lambda i: (0, 0)),
                  pl.BlockSpec((N_SHARD, None, blk, D_MODEL), lambda i: (0, layer, idx, 0)),
                  pl.BlockSpec(memory_space=pl.ANY)],
        out_specs=[row_spec, pl.BlockSpec((1, D_MODEL), lambda i: (0, 0)), pl.BlockSpec(memory_space=pl.ANY)],
        out_shape=[jax.ShapeDtypeStruct((L, D_MODEL), F32), jax.ShapeDtypeStruct((1, D_MODEL), F32),
                   jax.ShapeDtypeStruct(g1buf.shape, F32)],
        scratch_shapes=[pltpu.VMEM((N_SHARD, blk, D_MODEL), F32)],
        input_output_aliases={6: 2},
        compiler_params=_cparams(1),
    )(dup, dus, h, dhm, g1, wp1, g1buf)


def _disc_math(lr, li, ldt, br_t, bi_t):
    dt = jnp.exp(ldt)
    mag = jnp.exp(lr * dt)
    ang = li * dt
    ar = mag * jnp.cos(ang)
    ai = mag * jnp.sin(ang)
    den = lr * lr + li * li
    nr, ni = ar - 1.0, ai
    cr = (nr * lr + ni * li) / den
    ci = (ni * lr - nr * li) / den
    return ar, ai, cr * br_t - ci * bi_t, cr * bi_t + ci * br_t


def _disc_fwd(lr, li, ldt, br_t, bi_t):
    def body(lr_ref, li_ref, ldt_ref, br_ref, bi_ref, ar_ref, ai_ref, bbr_ref, bbi_ref):
        ar, ai, bbr, bbi = _disc_math(lr_ref[...], li_ref[...], ldt_ref[...], br_ref[...], bi_ref[...])
        ar_ref[...] = ar
        ai_ref[...] = ai
        bbr_ref[...] = bbr
        bbi_ref[...] = bbi

    shapes = [jax.ShapeDtypeStruct(a.shape, F32) for a in (lr, li, br_t, bi_t)]
    return pl.pallas_call(body, name="ssm_disc_fwd", out_shape=shapes,
                          compiler_params=pltpu.CompilerParams(vmem_limit_bytes=VMEM_LIMIT))(lr, li, ldt, br_t, bi_t)


def _disc_bwd(lr, li, ldt, br_t, bi_t, dar, dai, dbbr, dbbi):
    def body(lr_ref, li_ref, ldt_ref, br_ref, bi_ref, dar_ref, dai_ref, dbbr_ref, dbbi_ref,
             dlr_ref, dli_ref, dldt_ref, dbr_ref, dbi_ref):
        prim = (lr_ref[...], li_ref[...], ldt_ref[...], br_ref[...], bi_ref[...])
        _, pullback = jax.vjp(_disc_math, *prim)
        dlr, dli, dldt, dbr, dbi = pullback((dar_ref[...], dai_ref[...], dbbr_ref[...], dbbi_ref[...]))
        dlr_ref[...] = dlr
        dli_ref[...] = dli
        dldt_ref[...] = dldt
        dbr_ref[...] = dbr
        dbi_ref[...] = dbi

    shapes = [jax.ShapeDtypeStruct(a.shape, F32) for a in (lr, li, ldt, br_t, bi_t)]
    return pl.pallas_call(body, name="ssm_disc_bwd", out_shape=shapes,
                          compiler_params=pltpu.CompilerParams(vmem_limit_bytes=VMEM_LIMIT))(
        lr, li, ldt, br_t, bi_t, dar, dai, dbbr, dbbi)


def _pad_pairs(m_re, m_im):
    def blocks(m):
        v = m.transpose(0, 2, 1).reshape(N_PAIRS, 2, SSM_GROUP, SSM_STATE)
        return jnp.einsum("ab,jahp->jahbp", jnp.eye(2, dtype=m.dtype), v).reshape(N_PAIRS, 32, 128)
    both = jnp.concatenate([blocks(m_re), blocks(m_im)], axis=-1)
    place = jax.nn.one_hot(jnp.arange(N_PAIRS) % 4, 4, dtype=both.dtype)
    return jnp.einsum("jk,jrc->jkrc", place, both).reshape(N_PAIRS, 128, 256)


def _unpad_pairs(x):
    place = jax.nn.one_hot(jnp.arange(N_PAIRS) % 4, 4, dtype=x.dtype)
    both = jnp.einsum("jk,jkrc->jrc", place, x.reshape(N_PAIRS, 4, 32, 256))

    def unblock(v):
        v = v.reshape(N_PAIRS, 2, SSM_GROUP, 2, SSM_STATE)
        d = jnp.einsum("ab,jahbp->jahp", jnp.eye(2, dtype=x.dtype), v)
        return d.reshape(N_SSM_GROUPS, SSM_GROUP, SSM_STATE).transpose(0, 2, 1)
    return unblock(both[..., :128]), unblock(both[..., 128:])


def _adamw_math(w, g, m, v):
    m = ADAM_B1 * m + (1.0 - ADAM_B1) * g
    v = ADAM_B2 * v + (1.0 - ADAM_B2) * (g * g)
    m_hat = m / (1.0 - ADAM_B1 ** ADAM_STEP)
    v_hat = v / (1.0 - ADAM_B2 ** ADAM_STEP)
    delta = -ADAM_LR * (m_hat / (jnp.sqrt(v_hat) + ADAM_EPS) + ADAM_WD * w)
    return delta, m, v


def _adamw(name, w, m, v, gbuf, g_block, g_row0, row_tile, glu=False):
    nl, r, c = w.shape
    n_tiles = r // row_tile
    g_rows, g_cols = g_block
    g_tile = g_rows // n_tiles
    g_off = g_row0 // g_tile

    def body(w_ref, m_ref, v_ref, g_ref, go_ref, d_ref, mo_ref, vo_ref):
        g = g_ref[...]
        if glu:
            g = jnp.concatenate([g[:, :D_SSM], g[:, D_SSM:]], axis=0)
        delta, mn, vn = _adamw_math(w_ref[...], g, m_ref[...], v_ref[...])
        go_ref[...] = g
        d_ref[...] = delta
        mo_ref[...] = mn
        vo_ref[...] = vn

    w_spec = pl.BlockSpec((None, row_tile, c), lambda l, j: (l, j, 0))
    shape = jax.ShapeDtypeStruct(w.shape, F32)
    return pl.pallas_call(
        body, name=name, grid=(nl, n_tiles),
        in_specs=[w_spec, w_spec, w_spec, pl.BlockSpec((None, g_tile, g_cols), lambda l, j: (l, g_off + j, 0))],
        out_specs=[w_spec] * 4,
        out_shape=[shape] * 4,
        compiler_params=_cparams(2),
    )(w, m, v, gbuf)


def _pack_weights(w_in, w_glu, w_out, w_down, w_gate, w_up):
    nl = w_in.shape[0]

    def body(in_ref, glu_ref, out_ref, dn_ref, gate_ref, up_ref, p1_ref, p2_ref):
        p1_ref[0:704, :] = dn_ref[...].astype(BF16)
        g = glu_ref[...]
        p1_ref[704:768, :] = jnp.concatenate([g[:64, :], g[64:, :]], axis=1).astype(BF16)
        p1_ref[768:1024, :] = in_ref[...].astype(BF16)
        p1_ref[1024:1280, :] = out_ref[...].astype(BF16)
        p2_ref[:D_MODEL, :] = gate_ref[...].astype(BF16)
        p2_ref[D_MODEL:, :] = up_ref[...].astype(BF16)

    def spec(a):
        return pl.BlockSpec((None,) + a.shape[1:], lambda l: (l, 0, 0))

    ins = (w_in, w_glu, w_out, w_down, w_gate, w_up)
    return pl.pallas_call(
        body, name="pack_weights", grid=(nl,),
        in_specs=[spec(a) for a in ins],
        out_specs=[pl.BlockSpec((None, P1_ROWS, D_MODEL), lambda l: (l, 0, 0)),
                   pl.BlockSpec((None, P2_ROWS, FF_SHARD), lambda l: (l, 0, 0))],
        out_shape=[jax.ShapeDtypeStruct((nl, P1_ROWS, D_MODEL), BF16),
                   jax.ShapeDtypeStruct((nl, P2_ROWS, FF_SHARD), BF16)],
        compiler_params=_cparams(1),
    )(*ins)


MESH = pl.DeviceIdType.MESH
_ANY = pl.BlockSpec(memory_space=pl.ANY)
_HALVES = (P1_ROWS // 2, P2_ROWS // 2)


def _mesh_pos():
    return lax.axis_index("x"), lax.axis_index("y"), lax.axis_index("c")


def _other_chips(x, y):
    return [(1 - x, y), (x, 1 - y), (1 - x, 1 - y)]


def _remote(src, dst, send_sems, recv_sems, n, to):
    return pltpu.make_async_remote_copy(src_ref=src, dst_ref=dst, send_sem=send_sems.at[n],
                                        recv_sem=recv_sems.at[n], device_id=to, device_id_type=MESH)


def _all_gather_weights(wp1, wp2):
    nl = wp1.shape[0]

    def body(s1, s2, o1, o2, send_sems, recv_sems, local_sems):
        x, y, c = _mesh_pos()
        k = 2 * x + y
        sib = (x, y, 1 - c)
        chips = _other_chips(x, y)
        arrays = ((s1, o1, _HALVES[0]), (s2, o2, _HALVES[1]))

        def rows(half, hr):
            return pl.ds(half * hr, hr)

        own = [pltpu.make_async_copy(s, o.at[k], local_sems.at[a]) for a, (s, o, _) in enumerate(arrays)]
        for cp in own:
            cp.start()
        sends = []
        for a, (s, o, hr) in enumerate(arrays):
            for j, (px, py) in enumerate(chips):
                cp = _remote(s.at[:, rows(c, hr), :], o.at[k, :, rows(c, hr), :], send_sems, recv_sems,
                             6 * a + j, (px, py, c))
                cp.start()
                sends.append(cp)
        for j, (px, py) in enumerate(chips):
            for a, (s, o, hr) in enumerate(arrays):
                piece = o.at[2 * px + py, :, rows(c, hr), :]
                _remote(piece, piece, send_sems, recv_sems, 6 * a + j, (px, py, c)).wait_recv()
                cp = _remote(piece, piece, send_sems, recv_sems, 6 * a + 3 + j, sib)
                cp.start()
                sends.append(cp)
        for j, (px, py) in enumerate(chips):
            for a, (s, o, hr) in enumerate(arrays):
                piece = o.at[2 * px + py, :, rows(1 - c, hr), :]
                _remote(piece, piece, send_sems, recv_sems, 6 * a + 3 + j, sib).wait_recv()
        for cp in sends:
            cp.wait_send()
        for cp in own:
            cp.wait()

    return pl.pallas_call(
        body, name="all_gather_weights",
        in_specs=[_ANY, _ANY], out_specs=[_ANY, _ANY],
        out_shape=[jax.ShapeDtypeStruct((N_SHARD,) + wp1.shape, BF16),
                   jax.ShapeDtypeStruct((N_SHARD,) + wp2.shape, BF16)],
        scratch_shapes=[pltpu.SemaphoreType.DMA((12,)), pltpu.SemaphoreType.DMA((12,)),
                        pltpu.SemaphoreType.DMA((2,))],
    )(wp1, wp2)


def _rs_to_sibling(g1, g2):
    def body(g1_ref, g2_ref, b1, b2, send_sems, recv_sems):
        x, y, c = _mesh_pos()
        cps = []
        for a, (g, b, hr) in enumerate(((g1_ref, b1, _HALVES[0]), (g2_ref, b2, _HALVES[1]))):
            cp = _remote(g.at[:, :, pl.ds((1 - c) * hr, hr), :], b, send_sems, recv_sems, a, (x, y, 1 - c))
            cp.start()
            cps.append(cp)
        for cp in cps:
            cp.wait()

    nl = g1.shape[0]
    return pl.pallas_call(
        body, name="rs_to_sibling",
        in_specs=[_ANY, _ANY], out_specs=[_ANY, _ANY],
        out_shape=[jax.ShapeDtypeStruct((nl, N_SHARD, _HALVES[0], D_MODEL), F32),
                   jax.ShapeDtypeStruct((nl, N_SHARD, _HALVES[1], FF_SHARD), F32)],
        scratch_shapes=[pltpu.SemaphoreType.DMA((2,)), pltpu.SemaphoreType.DMA((2,))],
    )(g1, g2)


def _rs_add(name, ids, g, buf, row_tile):
    nl, _, hr, cols = buf.shape
    n_rt = hr // row_tile

    def body(ids_ref, g_ref, b_ref, own_ref, tb_ref):
        t = g_ref[...] + b_ref[...]
        tb_ref[...] = t.astype(BF16)

        @pl.when(pl.program_id(2) == ids_ref[1])
        def _():
            own_ref[...] = t

    blk = (None, None, row_tile, cols)
    grid_spec = pltpu.PrefetchScalarGridSpec(
        num_scalar_prefetch=1, grid=(nl, n_rt, N_SHARD),
        in_specs=[pl.BlockSpec(blk, lambda l, j, s, ids_ref: (l, s, ids_ref[0] * n_rt + j, 0)),
                  pl.BlockSpec(blk, lambda l, j, s, ids_ref: (l, s, j, 0))],
        out_specs=[pl.BlockSpec((None, row_tile, cols), lambda l, j, s, ids_ref: (l, j, 0)),
                   pl.BlockSpec(blk, lambda l, j, s, ids_ref: (l, s, j, 0))])
    return pl.pallas_call(
        body, name=name, grid_spec=grid_spec,
        out_shape=[jax.ShapeDtypeStruct((nl, hr, cols), F32), jax.ShapeDtypeStruct(buf.shape, BF16)],
        compiler_params=_cparams(3),
    )(ids, g, buf)


def _rs_to_chips(t1, t2):
    def body(t1_ref, t2_ref, b1, b2, send_sems, recv_sems):
        x, y, c = _mesh_pos()
        cps = []
        for a, (t, b) in enumerate(((t1_ref, b1), (t2_ref, b2))):
            for j, (px, py) in enumerate(_other_chips(x, y)):
                cp = _remote(t.at[:, 2 * px + py], b.at[j], send_sems, recv_sems, 3 * a + j, (px, py, c))
                cp.start()
                cps.append(cp)
        for cp in cps:
            cp.wait()

    nl = t1.shape[0]
    return pl.pallas_call(
        body, name="rs_to_chips",
        in_specs=[_ANY, _ANY], out_specs=[_ANY, _ANY],
        out_shape=[jax.ShapeDtypeStruct((3, nl, _HALVES[0], D_MODEL), BF16),
                   jax.ShapeDtypeStruct((3, nl, _HALVES[1], FF_SHARD), BF16)],
        scratch_shapes=[pltpu.SemaphoreType.DMA((6,)), pltpu.SemaphoreType.DMA((6,))],
    )(t1, t2)


def _rs_sum(name, own, bufb, row_tile):
    nl, hr, cols = own.shape

    def body(own_ref, b_ref, f_ref):
        f_ref[...] = ((own_ref[...] + b_ref[0].astype(F32)) + b_ref[1].astype(F32)) + b_ref[2].astype(F32)

    return pl.pallas_call(
        body, name=name, grid=(nl, hr // row_tile),
        in_specs=[pl.BlockSpec((None, row_tile, cols), lambda l, j: (l, j, 0)),
                  pl.BlockSpec((3, None, row_tile, cols), lambda l, j: (0, l, j, 0))],
        out_specs=pl.BlockSpec((None, row_tile, cols), lambda l, j: (l, j, 0)),
        out_shape=jax.ShapeDtypeStruct(own.shape, F32),
        compiler_params=_cparams(2),
    )(own, bufb)


def _rs_exchange(f1, f2):
    def body(f1_ref, f2_ref, o1, o2, send_sems, recv_sems, local_sems):
        x, y, c = _mesh_pos()
        cps = []
        for a, (f, o, hr) in enumerate(((f1_ref, o1, _HALVES[0]), (f2_ref, o2, _HALVES[1]))):
            mine = o.at[:, pl.ds(c * hr, hr), :]
            loc = pltpu.make_async_copy(f, mine, local_sems.at[a])
            loc.start()
            cp = _remote(f, mine, send_sems, recv_sems, a, (x, y, 1 - c))
            cp.start()
            cps += [loc, cp]
        for cp in cps:
            cp.wait()

    nl = f1.shape[0]
    return pl.pallas_call(
        body, name="rs_exchange",
        in_specs=[_ANY, _ANY], out_specs=[_ANY, _ANY],
        out_shape=[jax.ShapeDtypeStruct((nl, P1_ROWS, D_MODEL), F32),
                   jax.ShapeDtypeStruct((nl, P2_ROWS, FF_SHARD), F32)],
        scratch_shapes=[pltpu.SemaphoreType.DMA((2,)), pltpu.SemaphoreType.DMA((2,)),
                        pltpu.SemaphoreType.DMA((2,))],
    )(f1, f2)


def _small_all_reduce(s):
    n_rows = s.shape[0]
    hr = n_rows // 2

    def body(s_ref, o_ref, sibbuf, tbuf, cbuf, fbuf, send_sems, recv_sems):
        x, y, c = _mesh_pos()
        sib = (x, y, 1 - c)
        mine = pl.ds(pl.multiple_of(c * hr, SUBLANES), hr)
        theirs = pl.ds(pl.multiple_of((1 - c) * hr, SUBLANES), hr)
        first = _remote(s_ref.at[theirs], sibbuf, send_sems, recv_sems, 0, sib)
        first.start()
        first.wait()
        tbuf[...] = s_ref[mine, :] + sibbuf[...]
        cps = []
        for j, (px, py) in enumerate(_other_chips(x, y)):
            cp = _remote(tbuf, cbuf.at[j], send_sems, recv_sems, 1 + j, (px, py, c))
            cp.start()
            cps.append(cp)
        for cp in cps:
            cp.wait()
        f = (tbuf[...] + cbuf[1]) + (cbuf[0] + cbuf[2])
        fbuf[...] = f
        o_ref[mine, :] = f
        last = _remote(fbuf, o_ref.at[mine], send_sems, recv_sems, 4, sib)
        last.start()
        last.wait()

    vmem = pl.BlockSpec(memory_space=pltpu.VMEM)
    return pl.pallas_call(
        body, name="small_all_reduce",
        in_specs=[vmem], out_specs=vmem,
        out_shape=jax.ShapeDtypeStruct(s.shape, F32),
        scratch_shapes=[pltpu.VMEM((hr, D_MODEL), F32), pltpu.VMEM((hr, D_MODEL), F32),
                        pltpu.VMEM((3, hr, D_MODEL), F32), pltpu.VMEM((hr, D_MODEL), F32),
                        pltpu.SemaphoreType.DMA((5,)), pltpu.SemaphoreType.DMA((5,))],
        compiler_params=pltpu.CompilerParams(vmem_limit_bytes=VMEM_LIMIT),
    )(s)


def _reduce_scatter_grads(ids, g1, g2):
    b1, b2 = _rs_to_sibling(g1, g2)
    own1, t1 = _rs_add("rs_add_p1", ids, g1, b1, 320)
    own2, t2 = _rs_add("rs_add_p2", ids, g2, b2, 512)
    c1, c2 = _rs_to_chips(t1, t2)
    f1 = _rs_sum("rs_sum_p1", own1, c1, 320)
    f2 = _rs_sum("rs_sum_p2", own2, c2, 512)
    return _rs_exchange(f1, f2)


_SMALL = ("norm_mix", "w_pool", "pool_scale", "lam_re", "lam_im", "log_dt", "b_re", "b_im", "c_re", "c_im",
          "d_skip", "b_glu", "norm_ffn", "norm_final")
_WEIGHTS = ("norm_mix", "w_in", "w_pool", "pool_scale", "lam_re", "lam_im", "log_dt", "b_re", "b_im", "c_re",
            "c_im", "d_skip", "w_glu", "b_glu", "w_out", "norm_ffn", "w_gate", "w_up", "w_down", "norm_final")


def _local_step(x, target, wp1, wp2, p):
    nl = p["norm_mix"].shape[0]
    n_rows = nl * N_SSM_GROUPS
    lr = p["lam_re"].reshape(n_rows, 1, SSM_STATE)
    li = p["lam_im"].reshape(n_rows, 1, SSM_STATE)
    ldt = p["log_dt"].reshape(n_rows, 1, 1)
    br_t = p["b_re"].reshape(n_rows, SSM_STATE, SSM_GROUP).transpose(0, 2, 1)
    bi_t = p["b_im"].reshape(n_rows, SSM_STATE, SSM_GROUP).transpose(0, 2, 1)
    ar, ai, bbr_t, bbi_t = _disc_fwd(lr, li, ldt, br_t, bi_t)
    ar = ar.reshape(nl, 1, N_STATE)
    ai = ai.reshape(nl, 1, N_STATE)
    bbr = bbr_t.transpose(0, 2, 1).reshape(nl, N_SSM_GROUPS, SSM_STATE, SSM_GROUP)
    bbi = bbi_t.transpose(0, 2, 1).reshape(nl, N_SSM_GROUPS, SSM_STATE, SSM_GROUP)
    w_pool = p["w_pool"].astype(BF16)

    layers = []
    h = x
    for l in range(nl):
        bpad = _pad_pairs(bbr[l], bbi[l]).astype(BF16)
        cpad_t = _pad_pairs(p["c_re"][l].transpose(0, 2, 1), -p["c_im"][l].transpose(0, 2, 1)).astype(BF16)
        dskip = p["d_skip"][l].reshape(1, D_SSM)
        u, ypool = _mix_in_fwd(h, p["norm_mix"][l:l + 1], wp1, l, w_pool[l], p["pool_scale"][l:l + 1])
        sre, sim, yraw = _ssm_fwd(u, bpad, cpad_t.transpose(0, 2, 1), ar[l], ai[l], dskip)
        hm = _mix_out_fwd(yraw, ypool, h, wp1, l, p["b_glu"][l:l + 1])
        h_next, n2, gate_s, up_s = _ffn_fwd(hm, p["norm_ffn"][l:l + 1], wp1, wp2, l)
        layers.append(dict(h=h, u=u, ypool=ypool, sre=sre, sim=sim, yraw=yraw, hm=hm, n2=n2, gate_s=gate_s,
                           up_s=up_s, bpad_t=bpad.transpose(0, 2, 1), cpad_t=cpad_t, dskip=dskip))
        h = h_next

    dh, loss, d_norm_final = _final_fwd_bwd(h, p["norm_final"].reshape(1, D_MODEL), target)

    g1 = lax.empty((nl, N_SHARD, P1_ROWS, D_MODEL), F32)
    g2 = lax.empty((nl, N_SHARD, P2_ROWS, FF_SHARD), F32)
    per_layer = {n: [None] * nl for n in ("norm_mix", "w_pool", "pool_scale", "c_re", "c_im", "d_skip", "b_glu",
                                          "norm_ffn", "dar", "dai", "dbbr_t", "dbbi_t")}
    for l in reversed(range(nl)):
        s = layers[l]
        dhm, dg2, dgate_s, dup_s, act_s, dhb = _ffn_bwd_act(dh, s["hm"], p["norm_ffn"][l:l + 1], s["gate_s"],
                                                             s["up_s"], wp1, wp2, l)
        g1, g2 = _ffn_bwd_w(s["n2"], dgate_s, dup_s, act_s, dhb, g1, g2, l)
        dyraw, dyp, db_glu, g1 = _mix_out_bwd(dhm, s["yraw"], s["ypool"], wp1, l, p["b_glu"][l:l + 1], g1)
        dus, dcp, dbp, dar, dai, ddsk = _ssm_bwd(dyraw, s["u"], s["sre"], s["sim"], s["cpad_t"], s["bpad_t"],
                                                  ar[l], ai[l], s["dskip"])
        dup, dwp, dsc = _pool_bwd(dyp, s["u"], w_pool[l], p["pool_scale"][l:l + 1])
        dh, dg1, g1 = _mix_in_bwd(dup, dus, s["h"], dhm, p["norm_mix"][l:l + 1], wp1, l, g1)
        dc_re, dc_im = _unpad_pairs(dcp.transpose(0, 2, 1))
        dbbr, dbbi = _unpad_pairs(dbp)
        per_layer["norm_mix"][l] = dg1[0]
        per_layer["w_pool"][l] = dwp
        per_layer["pool_scale"][l] = dsc[0]
        per_layer["c_re"][l] = dc_re.transpose(0, 2, 1)
        per_layer["c_im"][l] = -dc_im.transpose(0, 2, 1)
        per_layer["d_skip"][l] = ddsk.reshape(N_SSM_GROUPS, SSM_GROUP)
        per_layer["b_glu"][l] = db_glu[0]
        per_layer["norm_ffn"][l] = dg2[0]
        per_layer["dar"][l] = dar.reshape(N_SSM_GROUPS, 1, SSM_STATE)
        per_layer["dai"][l] = dai.reshape(N_SSM_GROUPS, 1, SSM_STATE)
        per_layer["dbbr_t"][l] = dbbr.transpose(0, 2, 1)
        per_layer["dbbi_t"][l] = dbbi.transpose(0, 2, 1)

    st = {n: jnp.stack(v) for n, v in per_layer.items()}
    cat = lambda a: a.reshape((n_rows,) + a.shape[2:])
    dlr, dli, dldt, dbr_t, dbi_t = _disc_bwd(lr, li, ldt, br_t, bi_t, cat(st["dar"]), cat(st["dai"]),
                                              cat(st["dbbr_t"]), cat(st["dbbi_t"]))
    small = {n: st[n] for n in ("norm_mix", "w_pool", "pool_scale", "c_re", "c_im", "d_skip", "b_glu", "norm_ffn")}
    small["lam_re"] = dlr.reshape(nl, N_SSM_GROUPS, SSM_STATE)
    small["lam_im"] = dli.reshape(nl, N_SSM_GROUPS, SSM_STATE)
    small["log_dt"] = dldt.reshape(nl, N_SSM_GROUPS)
    small["b_re"] = dbr_t.transpose(0, 2, 1).reshape(nl, N_SSM_GROUPS, SSM_STATE, SSM_GROUP)
    small["b_im"] = dbi_t.transpose(0, 2, 1).reshape(nl, N_SSM_GROUPS, SSM_STATE, SSM_GROUP)
    small["norm_final"] = d_norm_final[0]
    return loss, dh, g1, g2, small


def _flatten_small(d):
    flat = jnp.concatenate([d[n].reshape(-1) for n in _SMALL])
    n_rows = -(-flat.shape[0] // (32 * D_MODEL)) * 32
    return jnp.pad(flat, (0, n_rows * D_MODEL - flat.shape[0])).reshape(n_rows, D_MODEL)


def _split_small(flat, like):
    flat = flat.reshape(-1)
    out, at = {}, 0
    for n in _SMALL:
        size = like[n].size
        out[n] = flat[at:at + size].reshape(like[n].shape)
        at += size
    return out


def kernel(x, norm_mix, w_in, w_pool, pool_scale, lam_re, lam_im, log_dt, b_re, b_im, c_re, c_im, d_skip, w_glu, b_glu, w_out, norm_ffn, w_gate, w_up, w_down, norm_final, loss_target, m_norm_mix, m_w_in, m_w_pool, m_pool_scale, m_lam_re, m_lam_im, m_log_dt, m_b_re, m_b_im, m_c_re, m_c_im, m_d_skip, m_w_glu, m_b_glu, m_w_out, m_norm_ffn, m_w_gate, m_w_up, m_w_down, m_norm_final, v_norm_mix, v_w_in, v_w_pool, v_pool_scale, v_lam_re, v_lam_im, v_log_dt, v_b_re, v_b_im, v_c_re, v_c_im, v_d_skip, v_w_glu, v_b_glu, v_w_out, v_norm_ffn, v_w_gate, v_w_up, v_w_down, v_norm_final):
    given = dict(locals())
    w = {n: given[n] for n in _WEIGHTS}
    m = {n: given["m_" + n] for n in _WEIGHTS}
    v = {n: given["v_" + n] for n in _WEIGHTS}
    ids = jnp.stack([lax.axis_index("c"), 2 * lax.axis_index("x") + lax.axis_index("y")]).astype(jnp.int32)

    wp1_own, wp2_own = _pack_weights(w_in, w_glu, w_out, w_down, w_gate, w_up)
    wp1, wp2 = _all_gather_weights(wp1_own, wp2_own)
    loss, grad_x, g1, g2, small = _local_step(x[0], loss_target[0], wp1, wp2, {n: w[n] for n in _SMALL})
    loss = lax.psum(loss[0, 0], ("x", "y", "c"))

    gr1, gr2 = _reduce_scatter_grads(ids, g1, g2)
    small_sum = _small_all_reduce(_flatten_small(small))

    res = {}
    big = (("w_in", gr1, (256, D_MODEL), 768, 256, False), ("w_out", gr1, (256, D_MODEL), 1024, 256, False),
           ("w_down", gr1, (704, D_MODEL), 0, 352, False), ("w_glu", gr1, (64, D_MODEL), 704, 128, True),
           ("w_gate", gr2, (D_MODEL, FF_SHARD), 0, 256, False), ("w_up", gr2, (D_MODEL, FF_SHARD), D_MODEL, 256, False))
    for n, gbuf, g_block, g_row0, row_tile, glu in big:
        res[n] = _adamw("adamw_" + n, w[n], m[n], v[n], gbuf, g_block, g_row0, row_tile, glu)
    flat = [_flatten_small(d)[None] for d in (w, m, v)]
    n_rows = flat[0].shape[1]
    outs = _adamw("adamw_small", *flat, small_sum[None], (n_rows, D_MODEL), 0, n_rows // 4)
    parts = [_split_small(o[0], w) for o in outs]
    for n in _SMALL:
        res[n] = tuple(part[n] for part in parts)

    return (loss, grad_x[None], *[res[n][0] for n in _WEIGHTS], *[res[n][1] for n in _WEIGHTS],
            *[res[n][2] for n in _WEIGHTS], *[res[n][3] for n in _WEIGHTS])
```

```python
import functools
import math

import jax
import jax.numpy as jnp
from jax import lax
from jax.experimental import pallas as pl
from jax.experimental.pallas import tpu as pltpu

F32 = jnp.float32
BF16 = jnp.bfloat16

D_MODEL = 1024
D_POOL = 512
D_SSM = 512
POOL_WINDOWS = (2, 4, 8, 16)
POOL_GROUP = 128
POOL_HALO = 16
N_SSM_GROUPS = 32
SSM_GROUP = 16
SSM_STATE = 64
N_STATE = N_SSM_GROUPS * SSM_STATE
N_PAIRS = N_SSM_GROUPS // 2
D_FF = 2816
N_SHARD = 4
FF_SHARD = D_FF // N_SHARD
RMS_EPS = 1e-6

ADAM_LR = 0.001
ADAM_B1 = 0.9
ADAM_B2 = 0.999
ADAM_EPS = 1e-08
ADAM_WD = 0.01
ADAM_STEP = 10

P_ROWS = 2816
P_WD_BLK = (704, 0)
P_WG_BLK = (704, 1)
P_WU_BLK = (704, 2)
P_FF_ROWS = 2112
P_GLU_BLK = (64, 33)
P_GLU_PAD = 192
P_IN_BLK = (256, 9)
P_OUT_BLK = (256, 10)

SUBLANES = 8
VMEM_LIMIT = 56 * 1024 * 1024

TM = 512
TM_FFN = 512
TS = 256
SCAN_LANES = 512


def _cparams(n_axes):
    return pltpu.CompilerParams(dimension_semantics=("arbitrary",) * n_axes, vmem_limit_bytes=VMEM_LIMIT)


def _dot(a, b):
    return jnp.dot(a, b, preferred_element_type=F32)


def _dot_nt(a, b):
    return lax.dot_general(a, b, (((1,), (1,)), ((), ())), preferred_element_type=F32)


def _dot_tn(a, b):
    return lax.dot_general(a, b, (((0,), (0,)), ((), ())), preferred_element_type=F32)


def _rms_hat(x):
    r = lax.rsqrt(jnp.mean(x * x, axis=-1, keepdims=True) + RMS_EPS)
    return x * r, r


def _rms_bwd(d_hat, xhat, r):
    return r * (d_hat - xhat * jnp.mean(d_hat * xhat, axis=-1, keepdims=True))


def _sigmoid(x):
    return 1.0 / (1.0 + jnp.exp(-x))


_GELU_C = math.sqrt(2.0 / math.pi)
_GELU_K = 0.044715


def _gelu(x):
    return 0.5 * x * (1.0 + jnp.tanh(_GELU_C * (x + _GELU_K * x * x * x)))


def _gelu_grad(x):
    th = jnp.tanh(_GELU_C * (x + _GELU_K * x * x * x))
    return 0.5 * (1.0 + th) + 0.5 * x * (1.0 - th * th) * _GELU_C * (1.0 + 3.0 * _GELU_K * x * x)


def _glu_weight(ref):
    v = ref[...]
    return jnp.concatenate([v[:, :, :D_SSM], v[:, :, D_SSM:]], axis=1).reshape(D_SSM, D_SSM)


def _glu_pack(w):
    v = w.reshape(N_SHARD, 128, D_SSM)
    return jnp.concatenate([v[:, :64, :], v[:, 64:, :]], axis=2)


def _pool_diff(ext, row0, tm):
    rows = row0 + lax.broadcasted_iota(jnp.int32, (tm, 1), 0)
    outs = []
    for gi, w in enumerate(POOL_WINDOWS):
        e = ext[:, gi * POOL_GROUP:(gi + 1) * POOL_GROUP]
        s = e
        k = 1
        while k < w:
            s = s + pltpu.roll(s, k, 0)
            k *= 2
        inv = 1.0 / jnp.minimum(rows + 1, w).astype(F32)
        outs.append(s[POOL_HALO:, :] * inv - e[POOL_HALO:, :])
    return outs


def _mix_in_fwd(h, g1, wp, layer, w_pool, scale):
    L = h.shape[0]
    tm = min(TM, L)

    def body(h_ref, g_ref, w_ref, wp_ref, sc_ref, u_ref, yp_ref, carry):
        i = pl.program_id(0)

        @pl.when(i == 0)
        def _():
            carry[...] = jnp.zeros_like(carry)

        xhat, _ = _rms_hat(h_ref[...])
        n1 = (xhat * g_ref[...]).astype(BF16)
        u = _dot(n1, w_ref[...].reshape(D_MODEL, D_MODEL))
        u_ref[...] = u
        up = u[:, :D_POOL]
        ext = jnp.concatenate([carry[...], up], axis=0)
        carry[...] = up[tm - POOL_HALO:, :]
        diffs = _pool_diff(ext, i * tm, tm)
        for gi in range(4):
            cols = slice(gi * POOL_GROUP, (gi + 1) * POOL_GROUP)
            yp_ref[:, cols] = _dot(diffs[gi].astype(BF16), wp_ref[gi]) * sc_ref[:, cols]

    blk, idx = P_IN_BLK
    return pl.pallas_call(
        body, name="mix_in_fwd", grid=(L // tm,),
        in_specs=[pl.BlockSpec((tm, D_MODEL), lambda i: (i, 0)),
                  pl.BlockSpec((1, D_MODEL), lambda i: (0, 0)),
                  pl.BlockSpec((N_SHARD, None, blk, D_MODEL), lambda i: (0, layer, idx, 0)),
                  pl.BlockSpec((4, POOL_GROUP, POOL_GROUP), lambda i: (0, 0, 0)),
                  pl.BlockSpec((1, D_POOL), lambda i: (0, 0))],
        out_specs=[pl.BlockSpec((tm, D_MODEL), lambda i: (i, 0)),
                   pl.BlockSpec((tm, D_POOL), lambda i: (i, 0))],
        out_shape=[jax.ShapeDtypeStruct((L, D_MODEL), F32), jax.ShapeDtypeStruct((L, D_POOL), F32)],
        scratch_shapes=[pltpu.VMEM((POOL_HALO, D_POOL), F32)],
        compiler_params=_cparams(1),
    )(h, g1, wp, w_pool, scale)


def _cmul(xr, xi, yr, yi):
    return xr * yr - xi * yi, xr * yi + xi * yr


def _scan_tables(ar, ai, tab, reverse):
    c = ar.shape[1]
    row = lax.broadcasted_iota(jnp.int32, (SUBLANES, c), 0)
    a2r, a2i = _cmul(ar, ai, ar, ai)
    a4r, a4i = _cmul(a2r, a2i, a2r, a2i)
    zero = jnp.zeros((SUBLANES, c), F32)
    for n, (s, pr, pi) in enumerate(((1, ar, ai), (2, a2r, a2i), (4, a4r, a4i))):
        keep = (row < SUBLANES - s) if reverse else (row >= s)
        tab[2 * n] = jnp.where(keep, pr, zero)
        tab[2 * n + 1] = jnp.where(keep, pi, zero)
    cr, ci = ar, ai
    tr, ti = zero, zero
    for n in range(SUBLANES):
        at = (SUBLANES - 1 - n) if reverse else n
        tr = jnp.where(row == at, cr, tr)
        ti = jnp.where(row == at, ci, ti)
        cr, ci = _cmul(cr, ci, ar, ai)
    tab[6] = tr
    tab[7] = ti


def _ssm_fwd(u, bpad, cpad, ar, ai, dskip):
    L = u.shape[0]
    ts = min(TS, L)
    nq = 4
    cq = N_STATE // nq

    def body(u_ref, bp_ref, cp_ref, ar_ref, ai_ref, dsk_ref, sre_ref, sim_ref, y_ref, cr, ci, tab):
        t = pl.program_id(1)

        @pl.when(t == 0)
        def _():
            cr[...] = jnp.zeros_like(cr)
            ci[...] = jnp.zeros_like(ci)
            _scan_tables(ar_ref[...], ai_ref[...], tab, reverse=False)

        uf = u_ref[...]
        ub = uf.astype(BF16)
        for jj in range(4):
            bu = _dot(ub, bp_ref[jj])
            sre_ref[:, jj * 128:(jj + 1) * 128] = bu[:, :128]
            sim_ref[:, jj * 128:(jj + 1) * 128] = bu[:, 128:]

        for cc in range(cq // SCAN_LANES):
            cols = slice(cc * SCAN_LANES, (cc + 1) * SCAN_LANES)
            def step(i, carry, cols=cols):
                c_r, c_i = carry
                r0 = pl.multiple_of(i * SUBLANES, SUBLANES)
                xr = sre_ref[pl.ds(r0, SUBLANES), cols]
                xi = sim_ref[pl.ds(r0, SUBLANES), cols]
                for n, s in enumerate((1, 2, 4)):
                    tr, ti = tab[2 * n, :, cols], tab[2 * n + 1, :, cols]
                    rr = pltpu.roll(xr, s, 0)
                    ri = pltpu.roll(xi, s, 0)
                    xr, xi = xr + tr * rr - ti * ri, xi + tr * ri + ti * rr
                pr, pi = tab[6, :, cols], tab[7, :, cols]
                xr, xi = xr + pr * c_r - pi * c_i, xi + pr * c_i + pi * c_r
                sre_ref[pl.ds(r0, SUBLANES), cols] = xr
                sim_ref[pl.ds(r0, SUBLANES), cols] = xi
                shp = (SUBLANES, SCAN_LANES)
                return (jnp.broadcast_to(xr[SUBLANES - 1:, :], shp), jnp.broadcast_to(xi[SUBLANES - 1:, :], shp))

            c_r, c_i = lax.fori_loop(0, ts // SUBLANES, step, (cr[:, cols], ci[:, cols]), unroll=2)
            cr[:, cols] = c_r
            ci[:, cols] = c_i

        acc = dsk_ref[...] * uf
        for jj in range(4):
            cols = slice(jj * 128, (jj + 1) * 128)
            scat = jnp.concatenate([sre_ref[:, cols], sim_ref[:, cols]], axis=1).astype(BF16)
            acc = acc + _dot(scat, cp_ref[jj])
        y_ref[...] = acc

    return pl.pallas_call(
        body, name="ssm_fwd", grid=(nq, L // ts),
        in_specs=[pl.BlockSpec((ts, 128), lambda q, t: (t, 4 + q)),
                  pl.BlockSpec((4, 128, 256), lambda q, t: (q, 0, 0)),
                  pl.BlockSpec((4, 256, 128), lambda q, t: (q, 0, 0)),
                  pl.BlockSpec((1, cq), lambda q, t: (0, q)),
                  pl.BlockSpec((1, cq), lambda q, t: (0, q)),
                  pl.BlockSpec((1, 128), lambda q, t: (0, q))],
        out_specs=[pl.BlockSpec((ts, cq), lambda q, t: (t, q)),
                   pl.BlockSpec((ts, cq), lambda q, t: (t, q)),
                   pl.BlockSpec((ts, 128), lambda q, t: (t, q))],
        out_shape=[jax.ShapeDtypeStruct((L, N_STATE), F32), jax.ShapeDtypeStruct((L, N_STATE), F32),
                   jax.ShapeDtypeStruct((L, D_SSM), F32)],
        scratch_shapes=[pltpu.VMEM((SUBLANES, cq), F32), pltpu.VMEM((SUBLANES, cq), F32),
                        pltpu.VMEM((8, SUBLANES, cq), F32)],
        compiler_params=_cparams(2),
    )(u, bpad, cpad, ar, ai, dskip)


def _mix_out_fwd(yraw, ypool, h, wp, layer, b_glu):
    L = h.shape[0]
    tm = min(TM, L)

    def body(yr_ref, yp_ref, h_ref, wglu_ref, b_ref, wout_ref, o_ref):
        y = _gelu(yr_ref[...])
        z = _dot(y.astype(BF16), _glu_weight(wglu_ref)) + b_ref[...]
        o = y * _sigmoid(z)
        mix = jnp.concatenate([yp_ref[...], o], axis=1).astype(BF16)
        o_ref[...] = h_ref[...] + _dot(mix, wout_ref[...].reshape(D_MODEL, D_MODEL))

    gb, gi = P_GLU_BLK
    ob, oi = P_OUT_BLK
    return pl.pallas_call(
        body, name="mix_out_fwd", grid=(L // tm,),
        in_specs=[pl.BlockSpec((tm, D_SSM), lambda i: (i, 0)),
                  pl.BlockSpec((tm, D_POOL), lambda i: (i, 0)),
                  pl.BlockSpec((tm, D_MODEL), lambda i: (i, 0)),
                  pl.BlockSpec((N_SHARD, None, gb, D_MODEL), lambda i: (0, layer, gi, 0)),
                  pl.BlockSpec((1, D_SSM), lambda i: (0, 0)),
                  pl.BlockSpec((N_SHARD, None, ob, D_MODEL), lambda i: (0, layer, oi, 0))],
        out_specs=pl.BlockSpec((tm, D_MODEL), lambda i: (i, 0)),
        out_shape=jax.ShapeDtypeStruct((L, D_MODEL), F32),
        compiler_params=_cparams(1),
    )(yraw, ypool, h, wp, b_glu, wp)


def _ffn_weights(ref):
    return ref[0:FF_SHARD, :], ref[FF_SHARD:2 * FF_SHARD, :], ref[2 * FF_SHARD:P_FF_ROWS, :]


def _ffn_fwd(h, g2, wp, layer):
    L = h.shape[0]
    tm = min(TM_FFN, L)

    def body(h_ref, g_ref, w_ref, o_ref, n2_ref, gate_ref, up_ref):
        k = pl.program_id(1)

        @pl.when(k == 0)
        def _():
            x = h_ref[...]
            xhat, _ = _rms_hat(x)
            n2_ref[...] = (xhat * g_ref[...]).astype(BF16)
            o_ref[...] = x

        wd, wg_t, wu_t = _ffn_weights(w_ref)
        n2 = n2_ref[...]
        gate = _dot_nt(n2, wg_t)
        up = _dot_nt(n2, wu_t)
        gate_ref[...] = gate.astype(BF16)
        up_ref[...] = up.astype(BF16)
        act = (gate * _sigmoid(gate) * up).astype(BF16)
        o_ref[...] += _dot(act, wd)

    act_shape = jax.ShapeDtypeStruct((N_SHARD, L, FF_SHARD), BF16)
    return pl.pallas_call(
        body, name="ffn_fwd", grid=(L // tm, N_SHARD),
        in_specs=[pl.BlockSpec((tm, D_MODEL), lambda m, k: (m, 0)),
                  pl.BlockSpec((1, D_MODEL), lambda m, k: (0, 0)),
                  pl.BlockSpec((None, None, P_FF_ROWS, D_MODEL), lambda m, k: (k, layer, 0, 0))],
        out_specs=[pl.BlockSpec((tm, D_MODEL), lambda m, k: (m, 0)),
                   pl.BlockSpec((tm, D_MODEL), lambda m, k: (m, 0)),
                   pl.BlockSpec((None, tm, FF_SHARD), lambda m, k: (k, m, 0)),
                   pl.BlockSpec((None, tm, FF_SHARD), lambda m, k: (k, m, 0))],
        out_shape=[jax.ShapeDtypeStruct((L, D_MODEL), F32), jax.ShapeDtypeStruct((L, D_MODEL), BF16),
                   act_shape, act_shape],
        compiler_params=_cparams(2),
    )(h, g2, wp)


def _final_fwd_bwd(h, gf, target):
    L = h.shape[0]
    tm = min(TM, L)

    def body(h_ref, g_ref, t_ref, dh_ref, loss_ref, dg_ref):
        i = pl.program_id(0)

        @pl.when(i == 0)
        def _():
            loss_ref[...] = jnp.zeros_like(loss_ref)
            dg_ref[...] = jnp.zeros_like(dg_ref)

        xhat, r = _rms_hat(h_ref[...])
        g = g_ref[...]
        e = xhat * g - t_ref[...]
        loss_ref[...] += 0.5 * jnp.sum(jnp.mean(e * e, axis=-1, keepdims=True), axis=0, keepdims=True)
        dy = e * (1.0 / D_MODEL)
        dg_ref[...] += jnp.sum(dy * xhat, axis=0, keepdims=True)
        dh_ref[...] = _rms_bwd(dy * g, xhat, r)

    return pl.pallas_call(
        body, name="final_fwd_bwd", grid=(L // tm,),
        in_specs=[pl.BlockSpec((tm, D_MODEL), lambda i: (i, 0)),
                  pl.BlockSpec((1, D_MODEL), lambda i: (0, 0)),
                  pl.BlockSpec((tm, D_MODEL), lambda i: (i, 0))],
        out_specs=[pl.BlockSpec((tm, D_MODEL), lambda i: (i, 0)),
                   pl.BlockSpec((1, 1), lambda i: (0, 0)),
                   pl.BlockSpec((1, D_MODEL), lambda i: (0, 0))],
        out_shape=[jax.ShapeDtypeStruct((L, D_MODEL), F32), jax.ShapeDtypeStruct((1, 1), F32),
                   jax.ShapeDtypeStruct((1, D_MODEL), F32)],
        compiler_params=_cparams(1),
    )(h, gf, target)


def _ffn_bwd_act(dh, h, g2, gate_s, up_s, wp, layer):
    L = h.shape[0]
    tm = min(TM_FFN, L)

    def body(dh_ref, h_ref, g_ref, gate_ref, up_ref, w_ref,
             dhm_ref, dg_ref, dgate_ref, dup_ref, act_ref, dhb_ref, dn2):
        m, k = pl.program_id(0), pl.program_id(1)

        @pl.when(jnp.logical_and(m == 0, k == 0))
        def _():
            dg_ref[...] = jnp.zeros_like(dg_ref)

        @pl.when(k == 0)
        def _():
            dhb_ref[...] = dh_ref[...].astype(BF16)
            dn2[...] = jnp.zeros_like(dn2)

        wd, wg_t, wu_t = _ffn_weights(w_ref)
        dact = _dot_nt(dhb_ref[...], wd)
        gate = gate_ref[...].astype(F32)
        up = up_ref[...].astype(F32)
        sg = _sigmoid(gate)
        silu = gate * sg
        dgate = (dact * up * (sg * (1.0 + gate * (1.0 - sg)))).astype(BF16)
        dup = (dact * silu).astype(BF16)
        dgate_ref[...] = dgate
        dup_ref[...] = dup
        act_ref[...] = (silu * up).astype(BF16)
        dn2[...] += _dot(dgate, wg_t) + _dot(dup, wu_t)

        @pl.when(k == N_SHARD - 1)
        def _():
            xhat, r = _rms_hat(h_ref[...])
            d = dn2[...]
            dg_ref[...] += jnp.sum(d * xhat, axis=0, keepdims=True)
            dhm_ref[...] = dh_ref[...] + _rms_bwd(d * g_ref[...], xhat, r)

    act_spec = pl.BlockSpec((None, tm, FF_SHARD), lambda m, k: (k, m, 0))
    act_shape = jax.ShapeDtypeStruct((N_SHARD, L, FF_SHARD), BF16)
    row_spec = pl.BlockSpec((tm, D_MODEL), lambda m, k: (m, 0))
    return pl.pallas_call(
        body, name="ffn_bwd_act", grid=(L // tm, N_SHARD),
        in_specs=[row_spec, row_spec,
                  pl.BlockSpec((1, D_MODEL), lambda m, k: (0, 0)),
                  act_spec, act_spec,
                  pl.BlockSpec((None, None, P_FF_ROWS, D_MODEL), lambda m, k: (k, layer, 0, 0))],
        out_specs=[row_spec,
                   pl.BlockSpec((1, D_MODEL), lambda m, k: (0, 0)),
                   act_spec, act_spec, act_spec, row_spec],
        out_shape=[jax.ShapeDtypeStruct((L, D_MODEL), F32), jax.ShapeDtypeStruct((1, D_MODEL), F32),
                   act_shape, act_shape, act_shape, jax.ShapeDtypeStruct((L, D_MODEL), BF16)],
        scratch_shapes=[pltpu.VMEM((tm, D_MODEL), F32)],
        compiler_params=_cparams(2),
    )(dh, h, g2, gate_s, up_s, wp)


def _ffn_bwd_w(n2, dgate_s, dup_s, act_s, dhb, gbuf, layer):
    L = n2.shape[0]
    tm = min(TM_FFN, L)

    def body(n2_ref, dgate_ref, dup_ref, act_ref, dhb_ref, g_in, g_ref):
        m = pl.program_id(1)

        @pl.when(m == 0)
        def _():
            g_ref[...] = jnp.zeros_like(g_ref)

        n2v = n2_ref[...]
        g_ref[0:FF_SHARD, :] += _dot_tn(act_ref[...], dhb_ref[...])
        g_ref[FF_SHARD:2 * FF_SHARD, :] += _dot_tn(dgate_ref[...], n2v)
        g_ref[2 * FF_SHARD:P_FF_ROWS, :] += _dot_tn(dup_ref[...], n2v)

    act_spec = pl.BlockSpec((None, tm, FF_SHARD), lambda k, m: (k, m, 0))
    row_spec = pl.BlockSpec((tm, D_MODEL), lambda k, m: (m, 0))
    return pl.pallas_call(
        body, name="ffn_bwd_w", grid=(N_SHARD, L // tm),
        in_specs=[row_spec, act_spec, act_spec, act_spec, row_spec, pl.BlockSpec(memory_space=pl.ANY)],
        out_specs=pl.BlockSpec((None, None, P_FF_ROWS, D_MODEL), lambda k, m: (layer, k, 0, 0)),
        out_shape=jax.ShapeDtypeStruct(gbuf.shape, F32),
        input_output_aliases={5: 0},
        compiler_params=_cparams(2),
    )(n2, dgate_s, dup_s, act_s, dhb, gbuf)


def _mix_out_bwd(dhm, yraw, ypool, wp, layer, b_glu, gbuf):
    L = dhm.shape[0]
    tm = min(TM, L)

    def body(dhm_ref, yr_ref, yp_ref, wglu_ref, b_ref, wout_ref, g1_in,
             dyr_ref, dyp_ref, db_ref, g1_ref, dwout, dwglu, gpack):
        i = pl.program_id(0)

        @pl.when(i == 0)
        def _():
            db_ref[...] = jnp.zeros_like(db_ref)
            dwout[...] = jnp.zeros_like(dwout)
            dwglu[...] = jnp.zeros_like(dwglu)

        dhb = dhm_ref[...].astype(BF16)
        wglu = _glu_weight(wglu_ref)
        dmix = _dot_nt(dhb, wout_ref[...].reshape(D_MODEL, D_MODEL))
        dyp_ref[...] = dmix[:, :D_POOL]
        d_o = dmix[:, D_POOL:]
        yraw_v = yr_ref[...]
        y = _gelu(yraw_v)
        yb = y.astype(BF16)
        sig = _sigmoid(_dot(yb, wglu) + b_ref[...])
        mix = jnp.concatenate([yp_ref[...], y * sig], axis=1).astype(BF16)
        dwout[...] += _dot_tn(mix, dhb).reshape(N_SHARD, 256, D_MODEL)
        dz = d_o * y * sig * (1.0 - sig)
        dzb = dz.astype(BF16)
        db_ref[...] += jnp.sum(dz, axis=0, keepdims=True)
        dwglu[...] += _dot_tn(yb, dzb)
        dy = d_o * sig + _dot_nt(dzb, wglu)
        dyr_ref[...] = dy * _gelu_grad(yraw_v)

        @pl.when(i == n_steps - 1)
        def _():
            gpack[:, :gb, :] = _glu_pack(dwglu[...])
            gpack[:, gb:, :] = jnp.zeros((N_SHARD, P_GLU_PAD - gb, D_MODEL), F32)
            pltpu.sync_copy(gpack, g1_ref.at[layer, :, pl.ds(gb * gi, P_GLU_PAD), :])
            pltpu.sync_copy(dwout, g1_ref.at[layer, :, pl.ds(ob * oi, ob), :])

    gb, gi = P_GLU_BLK
    ob, oi = P_OUT_BLK
    n_steps = L // tm
    return pl.pallas_call(
        body, name="mix_out_bwd", grid=(n_steps,),
        in_specs=[pl.BlockSpec((tm, D_MODEL), lambda i: (i, 0)),
                  pl.BlockSpec((tm, D_SSM), lambda i: (i, 0)),
                  pl.BlockSpec((tm, D_POOL), lambda i: (i, 0)),
                  pl.BlockSpec((N_SHARD, None, gb, D_MODEL), lambda i: (0, layer, gi, 0)),
                  pl.BlockSpec((1, D_SSM), lambda i: (0, 0)),
                  pl.BlockSpec((N_SHARD, None, ob, D_MODEL), lambda i: (0, layer, oi, 0)),
                  pl.BlockSpec(memory_space=pl.ANY)],
        out_specs=[pl.BlockSpec((tm, D_SSM), lambda i: (i, 0)),
                   pl.BlockSpec((tm, D_POOL), lambda i: (i, 0)),
                   pl.BlockSpec((1, D_SSM), lambda i: (0, 0)),
                   pl.BlockSpec(memory_space=pl.ANY)],
        out_shape=[jax.ShapeDtypeStruct((L, D_SSM), F32), jax.ShapeDtypeStruct((L, D_POOL), F32),
                   jax.ShapeDtypeStruct((1, D_SSM), F32),
                   jax.ShapeDtypeStruct(gbuf.shape, F32)],
        scratch_shapes=[pltpu.VMEM((N_SHARD, ob, D_MODEL), F32), pltpu.VMEM((D_SSM, D_SSM), F32),
                        pltpu.VMEM((N_SHARD, P_GLU_PAD, D_MODEL), F32)],
        input_output_aliases={6: 3},
        compiler_params=_cparams(1),
    )(dhm, yraw, ypool, wp, b_glu, wp, gbuf)


def _ssm_bwd(dyraw, u, sre, sim, cpad_t, bpad_t, ar, ai, dskip):
    L = u.shape[0]
    ts = min(TS, L)
    nt = L // ts
    nq = 4
    cq = N_STATE // nq

    def body(dy_ref, u_ref, sre_ref, sim_ref, ct_ref, bt_ref, ar_ref, ai_ref, dsk_ref,
             du_ref, dcp_ref, dbp_ref, dar_ref, dai_ref, ddsk_ref, gre, gim, cr, ci, tab, accr, acci):
        t = pl.program_id(1)

        @pl.when(t == 0)
        def _():
            for ref in (cr, ci, accr, acci, dcp_ref, dbp_ref, ddsk_ref):
                ref[...] = jnp.zeros_like(ref)
            _scan_tables(ar_ref[...], -ai_ref[...], tab, reverse=True)

        dy = dy_ref[...]
        dyb = dy.astype(BF16)
        uf = u_ref[...]
        ub = uf.astype(BF16)
        for jj in range(4):
            cols = slice(jj * 128, (jj + 1) * 128)
            ds = _dot(dyb, ct_ref[jj])
            gre[:, cols] = ds[:, :128]
            gim[:, cols] = ds[:, 128:]
            scat = jnp.concatenate([sre_ref[:, cols], sim_ref[:, cols]], axis=1).astype(BF16)
            dcp_ref[jj] += _dot_tn(scat, dyb)

        n_grp = ts // SUBLANES
        shp = (SUBLANES, SCAN_LANES)
        last_row = lax.broadcasted_iota(jnp.int32, shp, 0) == SUBLANES - 1
        for cc in range(cq // SCAN_LANES):
            cols = slice(cc * SCAN_LANES, (cc + 1) * SCAN_LANES)
            def step(i, carry, cols=cols):
                c_r, c_i, a_r, a_i = carry
                r0 = pl.multiple_of((n_grp - 1 - i) * SUBLANES, SUBLANES)
                xr = gre[pl.ds(r0, SUBLANES), cols]
                xi = gim[pl.ds(r0, SUBLANES), cols]
                for n, s in enumerate((1, 2, 4)):
                    tr, ti = tab[2 * n, :, cols], tab[2 * n + 1, :, cols]
                    rr = pltpu.roll(xr, SUBLANES - s, 0)
                    ri = pltpu.roll(xi, SUBLANES - s, 0)
                    xr, xi = xr + tr * rr - ti * ri, xi + tr * ri + ti * rr
                qr, qi = tab[6, :, cols], tab[7, :, cols]
                xr, xi = xr + qr * c_r - qi * c_i, xi + qr * c_i + qi * c_r
                gre[pl.ds(r0, SUBLANES), cols] = xr
                gim[pl.ds(r0, SUBLANES), cols] = xi
                nr = jnp.where(last_row, c_r, pltpu.roll(xr, SUBLANES - 1, 0))
                ni = jnp.where(last_row, c_i, pltpu.roll(xi, SUBLANES - 1, 0))
                sr = sre_ref[pl.ds(r0, SUBLANES), cols]
                si = sim_ref[pl.ds(r0, SUBLANES), cols]
                a_r = a_r + sr * nr + si * ni
                a_i = a_i + sr * ni - si * nr
                return (jnp.broadcast_to(xr[:1, :], shp), jnp.broadcast_to(xi[:1, :], shp), a_r, a_i)

            c_r, c_i, a_r, a_i = lax.fori_loop(
                0, n_grp, step, (cr[:, cols], ci[:, cols], accr[:, cols], acci[:, cols]), unroll=2)
            cr[:, cols] = c_r
            ci[:, cols] = c_i
            accr[:, cols] = a_r
            acci[:, cols] = a_i

        acc = dsk_ref[...] * dy
        for jj in range(4):
            cols = slice(jj * 128, (jj + 1) * 128)
            gcat = jnp.concatenate([gre[:, cols], gim[:, cols]], axis=1).astype(BF16)
            acc = acc + _dot(gcat, bt_ref[jj])
            dbp_ref[jj] += _dot_tn(ub, gcat)
        du_ref[...] = acc
        ddsk_ref[...] += jnp.sum(dy * uf, axis=0, keepdims=True)

        @pl.when(t == nt - 1)
        def _():
            dar_ref[...] = jnp.sum(accr[...], axis=0, keepdims=True)
            dai_ref[...] = jnp.sum(acci[...], axis=0, keepdims=True)

    f32_scr = lambda *s: pltpu.VMEM(s, F32)
    return pl.pallas_call(
        body, name="ssm_bwd", grid=(nq, nt),
        in_specs=[pl.BlockSpec((ts, 128), lambda q, t: (nt - 1 - t, q)),
                  pl.BlockSpec((ts, 128), lambda q, t: (nt - 1 - t, 4 + q)),
                  pl.BlockSpec((ts, cq), lambda q, t: (nt - 1 - t, q)),
                  pl.BlockSpec((ts, cq), lambda q, t: (nt - 1 - t, q)),
                  pl.BlockSpec((4, 128, 256), lambda q, t: (q, 0, 0)),
                  pl.BlockSpec((4, 256, 128), lambda q, t: (q, 0, 0)),
                  pl.BlockSpec((1, cq), lambda q, t: (0, q)),
                  pl.BlockSpec((1, cq), lambda q, t: (0, q)),
                  pl.BlockSpec((1, 128), lambda q, t: (0, q))],
        out_specs=[pl.BlockSpec((ts, 128), lambda q, t: (nt - 1 - t, q)),
                   pl.BlockSpec((4, 256, 128), lambda q, t: (q, 0, 0)),
                   pl.BlockSpec((4, 128, 256), lambda q, t: (q, 0, 0)),
                   pl.BlockSpec((1, cq), lambda q, t: (0, q)),
                   pl.BlockSpec((1, cq), lambda q, t: (0, q)),
                   pl.BlockSpec((1, 128), lambda q, t: (0, q))],
        out_shape=[jax.ShapeDtypeStruct((L, D_SSM), F32),
                   jax.ShapeDtypeStruct((N_PAIRS, 256, 128), F32), jax.ShapeDtypeStruct((N_PAIRS, 128, 256), F32),
                   jax.ShapeDtypeStruct((1, N_STATE), F32), jax.ShapeDtypeStruct((1, N_STATE), F32),
                   jax.ShapeDtypeStruct((1, D_SSM), F32)],
        scratch_shapes=[f32_scr(ts, cq), f32_scr(ts, cq), f32_scr(SUBLANES, cq), f32_scr(SUBLANES, cq),
                        f32_scr(8, SUBLANES, cq), f32_scr(SUBLANES, cq), f32_scr(SUBLANES, cq)],
        compiler_params=_cparams(2),
    )(dyraw, u, sre, sim, cpad_t, bpad_t, ar, ai, dskip)


def _pool_bwd(dyp, u, w_pool, scale):
    L = u.shape[0]
    tm = min(TM, L)
    nt = L // tm
    halo_per_tile = tm // POOL_HALO

    def body(dyp_ref, u_ref, halo_ref, wp_ref, sc_ref, du_ref, dwp_ref, dsc_ref, carry):
        i = pl.program_id(0)
        tile = nt - 1 - i

        @pl.when(i == 0)
        def _():
            carry[...] = jnp.zeros_like(carry)
            dwp_ref[...] = jnp.zeros_like(dwp_ref)
            dsc_ref[...] = jnp.zeros_like(dsc_ref)

        up = u_ref[...]
        halo = jnp.where(tile > 0, halo_ref[...], jnp.zeros_like(halo_ref))
        diffs = _pool_diff(jnp.concatenate([halo, up], axis=0), tile * tm, tm)
        rows = tile * tm + lax.broadcasted_iota(jnp.int32, (tm, 1), 0)
        n_ext = tm + POOL_HALO
        for gi, w in enumerate(POOL_WINDOWS):
            cols = slice(gi * POOL_GROUP, (gi + 1) * POOL_GROUP)
            db = diffs[gi].astype(BF16)
            dyp = dyp_ref[:, cols]
            dsc_ref[:, cols] += jnp.sum(dyp * _dot(db, wp_ref[gi]), axis=0, keepdims=True)
            dp = (dyp * sc_ref[:, cols]).astype(BF16)
            ddiff = _dot_nt(dp, wp_ref[gi])
            dwp_ref[gi] += _dot_tn(db, dp)
            e = ddiff * (1.0 / jnp.minimum(rows + 1, w).astype(F32))
            s = jnp.concatenate([e, carry[:, cols]], axis=0)
            k = 1
            while k < w:
                s = s + pltpu.roll(s, n_ext - k, 0)
                k *= 2
            du_ref[:, cols] = s[:tm, :] - ddiff
            carry[:, cols] = e[:POOL_HALO, :]

    return pl.pallas_call(
        body, name="pool_bwd", grid=(nt,),
        in_specs=[pl.BlockSpec((tm, D_POOL), lambda i: (nt - 1 - i, 0)),
                  pl.BlockSpec((tm, D_POOL), lambda i: (nt - 1 - i, 0)),
                  pl.BlockSpec((POOL_HALO, D_POOL), lambda i: (jnp.maximum((nt - 1 - i) * halo_per_tile - 1, 0), 0)),
                  pl.BlockSpec((4, POOL_GROUP, POOL_GROUP), lambda i: (0, 0, 0)),
                  pl.BlockSpec((1, D_POOL), lambda i: (0, 0))],
        out_specs=[pl.BlockSpec((tm, D_POOL), lambda i: (nt - 1 - i, 0)),
                   pl.BlockSpec((4, POOL_GROUP, POOL_GROUP), lambda i: (0, 0, 0)),
                   pl.BlockSpec((1, D_POOL), lambda i: (0, 0))],
        out_shape=[jax.ShapeDtypeStruct((L, D_POOL), F32),
                   jax.ShapeDtypeStruct((4, POOL_GROUP, POOL_GROUP), F32),
                   jax.ShapeDtypeStruct((1, D_POOL), F32)],
        scratch_shapes=[pltpu.VMEM((POOL_HALO, D_POOL), F32)],
        compiler_params=_cparams(1),
    )(dyp, u, u, w_pool, scale)


def _mix_in_bwd(dup, dus, h, dhm, g1, wp, layer, gbuf):
    L = h.shape[0]
    tm = min(TM, L)
    n_steps = L // tm
    blk, idx = P_IN_BLK

    def body(dup_ref, dus_ref, h_ref, dhm_ref, g_ref, w_ref, g1_in, dh_ref, dg_ref, g1_ref, dwin):
        i = pl.program_id(0)

        @pl.when(i == 0)
        def _():
            dg_ref[...] = jnp.zeros_like(dg_ref)
            dwin[...] = jnp.zeros_like(dwin)

        du = jnp.concatenate([dup_ref[...], dus_ref[...]], axis=1).astype(BF16)
        dn1 = _dot_nt(du, w_ref[...].reshape(D_MODEL, D_MODEL))
        xhat, r = _rms_hat(h_ref[...])
        g = g_ref[...]
        n1 = (xhat * g).astype(BF16)
        dwin[...] += _dot_tn(n1, du).reshape(N_SHARD, blk, D_MODEL)
        dg_ref[...] += jnp.sum(dn1 * xhat, axis=0, keepdims=True)
        dh_ref[...] = dhm_ref[...] + _rms_bwd(dn1 * g, xhat, r)

        @pl.when(i == n_steps - 1)
        def _():
            pltpu.sync_copy(dwin, g1_ref.at[layer, :, pl.ds(blk * idx, blk), :])

    row_spec = pl.BlockSpec((tm, D_MODEL), lambda i: (i, 0))
    half_spec = pl.BlockSpec((tm, D_POOL), lambda i: (i, 0))
    return pl.pallas_call(
        body, name="mix_in_bwd", grid=(n_steps,),
        in_specs=[half_spec, half_spec, row_spec, row_spec,
                  pl.BlockSpec((1, D_MODEL), lambda i: (0, 0)),
                  pl.BlockSpec((N_SHARD, None, blk, D_MODEL), lambda i: (0, layer, idx, 0)),
                  pl.BlockSpec(memory_space=pl.ANY)],
        out_specs=[row_spec, pl.BlockSpec((1, D_MODEL), lambda i: (0, 0)), pl.BlockSpec(memory_space=pl.ANY)],
        out_shape=[jax.ShapeDtypeStruct((L, D_MODEL), F32), jax.ShapeDtypeStruct((1, D_MODEL), F32),
                   jax.ShapeDtypeStruct(gbuf.shape, F32)],
        scratch_shapes=[pltpu.VMEM((N_SHARD, blk, D_MODEL), F32)],
        input_output_aliases={6: 2},
        compiler_params=_cparams(1),
    )(dup, dus, h, dhm, g1, wp, gbuf)


def _disc_math(lr, li, ldt, br_t, bi_t):
    dt = jnp.exp(ldt)
    mag = jnp.exp(lr * dt)
    ang = li * dt
    ar = mag * jnp.cos(ang)
    ai = mag * jnp.sin(ang)
    den = lr * lr + li * li
    nr, ni = ar - 1.0, ai
    cr = (nr * lr + ni * li) / den
    ci = (ni * lr - nr * li) / den
    return ar, ai, cr * br_t - ci * bi_t, cr * bi_t + ci * br_t


def _disc_fwd(lr, li, ldt, br_t, bi_t):
    def body(lr_ref, li_ref, ldt_ref, br_ref, bi_ref, ar_ref, ai_ref, bbr_ref, bbi_ref):
        ar, ai, bbr, bbi = _disc_math(lr_ref[...], li_ref[...], ldt_ref[...], br_ref[...], bi_ref[...])
        ar_ref[...] = ar
        ai_ref[...] = ai
        bbr_ref[...] = bbr
        bbi_ref[...] = bbi

    shapes = [jax.ShapeDtypeStruct(a.shape, F32) for a in (lr, li, br_t, bi_t)]
    return pl.pallas_call(body, name="ssm_disc_fwd", out_shape=shapes,
                          compiler_params=pltpu.CompilerParams(vmem_limit_bytes=VMEM_LIMIT))(lr, li, ldt, br_t, bi_t)


def _disc_bwd(lr, li, ldt, br_t, bi_t, dar, dai, dbbr, dbbi):
    def body(lr_ref, li_ref, ldt_ref, br_ref, bi_ref, dar_ref, dai_ref, dbbr_ref, dbbi_ref,
             dlr_ref, dli_ref, dldt_ref, dbr_ref, dbi_ref):
        prim = (lr_ref[...], li_ref[...], ldt_ref[...], br_ref[...], bi_ref[...])
        _, pullback = jax.vjp(_disc_math, *prim)
        dlr, dli, dldt, dbr, dbi = pullback((dar_ref[...], dai_ref[...], dbbr_ref[...], dbbi_ref[...]))
        dlr_ref[...] = dlr
        dli_ref[...] = dli
        dldt_ref[...] = dldt
        dbr_ref[...] = dbr
        dbi_ref[...] = dbi

    shapes = [jax.ShapeDtypeStruct(a.shape, F32) for a in (lr, li, ldt, br_t, bi_t)]
    return pl.pallas_call(body, name="ssm_disc_bwd", out_shape=shapes,
                          compiler_params=pltpu.CompilerParams(vmem_limit_bytes=VMEM_LIMIT))(
        lr, li, ldt, br_t, bi_t, dar, dai, dbbr, dbbi)


def _pad_pairs(m_re, m_im):
    def blocks(m):
        v = m.transpose(0, 2, 1).reshape(N_PAIRS, 2, SSM_GROUP, SSM_STATE)
        return jnp.einsum("ab,jahp->jahbp", jnp.eye(2, dtype=m.dtype), v).reshape(N_PAIRS, 32, 128)
    both = jnp.concatenate([blocks(m_re), blocks(m_im)], axis=-1)
    place = jax.nn.one_hot(jnp.arange(N_PAIRS) % 4, 4, dtype=both.dtype)
    return jnp.einsum("jk,jrc->jkrc", place, both).reshape(N_PAIRS, 128, 256)


def _unpad_pairs(x):
    place = jax.nn.one_hot(jnp.arange(N_PAIRS) % 4, 4, dtype=x.dtype)
    both = jnp.einsum("jk,jkrc->jrc", place, x.reshape(N_PAIRS, 4, 32, 256))

    def unblock(v):
        v = v.reshape(N_PAIRS, 2, SSM_GROUP, 2, SSM_STATE)
        d = jnp.einsum("ab,jahbp->jahp", jnp.eye(2, dtype=x.dtype), v)
        return d.reshape(N_SSM_GROUPS, SSM_GROUP, SSM_STATE).transpose(0, 2, 1)
    return unblock(both[..., :128]), unblock(both[..., 128:])


def _adamw_math(w, g, m, v):
    m = ADAM_B1 * m + (1.0 - ADAM_B1) * g
    v = ADAM_B2 * v + (1.0 - ADAM_B2) * (g * g)
    m_hat = m / (1.0 - ADAM_B1 ** ADAM_STEP)
    v_hat = v / (1.0 - ADAM_B2 ** ADAM_STEP)
    delta = -ADAM_LR * (m_hat / (jnp.sqrt(v_hat) + ADAM_EPS) + ADAM_WD * w)
    return delta, m, v


def _adamw(name, w, m, v, gbuf, g_block, g_row0, row_tile, glu=False):
    nl, r, c = w.shape
    n_tiles = r // row_tile
    g_rows, g_cols = g_block
    g_tile = g_rows // n_tiles
    g_off = g_row0 // g_tile

    def body(w_ref, m_ref, v_ref, g_ref, go_ref, d_ref, mo_ref, vo_ref):
        g = g_ref[...]
        if glu:
            g = jnp.concatenate([g[:, :D_SSM], g[:, D_SSM:]], axis=0)
        delta, mn, vn = _adamw_math(w_ref[...], g, m_ref[...], v_ref[...])
        go_ref[...] = g
        d_ref[...] = delta
        mo_ref[...] = mn
        vo_ref[...] = vn

    w_spec = pl.BlockSpec((None, row_tile, c), lambda l, j: (l, j, 0))
    shape = jax.ShapeDtypeStruct(w.shape, F32)
    return pl.pallas_call(
        body, name=name, grid=(nl, n_tiles),
        in_specs=[w_spec, w_spec, w_spec, pl.BlockSpec((None, g_tile, g_cols), lambda l, j: (l, g_off + j, 0))],
        out_specs=[w_spec] * 4,
        out_shape=[shape] * 4,
        compiler_params=_cparams(2),
    )(w, m, v, gbuf)


def _pack_weights(ids, w_in, w_glu, w_out, w_down, w_gate_t, w_up_t):
    nl = w_in.shape[0]
    gb, gi = P_GLU_BLK
    ib, ii = P_IN_BLK
    ob, oi = P_OUT_BLK

    def body(ids_ref, in_ref, glu_ref, out_ref, dn_ref, gate_ref, up_ref, p_ref):
        p_ref[0:FF_SHARD, :] = dn_ref[...].astype(BF16)
        p_ref[FF_SHARD:2 * FF_SHARD, :] = gate_ref[...].astype(BF16)
        p_ref[2 * FF_SHARD:P_FF_ROWS, :] = up_ref[...].astype(BF16)
        g = glu_ref[...]
        p_ref[gb * gi:gb * (gi + 1), :] = jnp.concatenate([g[:gb, :], g[gb:, :]], axis=1).astype(BF16)
        p_ref[gb * (gi + 1):ib * ii, :] = jnp.zeros((P_GLU_PAD - gb, D_MODEL), BF16)
        p_ref[ib * ii:ib * (ii + 1), :] = in_ref[...].astype(BF16)
        p_ref[ob * oi:ob * (oi + 1), :] = out_ref[...].astype(BF16)

    def spec(a):
        return pl.BlockSpec((None,) + a.shape[1:], lambda l, ids_ref: (l, 0, 0))

    ins = (w_in, w_glu, w_out, w_down, w_gate_t, w_up_t)
    grid_spec = pltpu.PrefetchScalarGridSpec(
        num_scalar_prefetch=1, grid=(nl,),
        in_specs=[spec(a) for a in ins],
        out_specs=pl.BlockSpec((None, None, P_ROWS, D_MODEL), lambda l, ids_ref: (ids_ref[1], l, 0, 0)))
    return pl.pallas_call(
        body, name="pack_weights", grid_spec=grid_spec,
        out_shape=jax.ShapeDtypeStruct((N_SHARD, nl, P_ROWS, D_MODEL), BF16),
        compiler_params=_cparams(1),
    )(ids, *ins)


MESH = pl.DeviceIdType.MESH
_ANY = pl.BlockSpec(memory_space=pl.ANY)
P_HALF = P_ROWS // 2
RS_ROW_TILE = 352


def _mesh_pos():
    return lax.axis_index("x"), lax.axis_index("y"), lax.axis_index("c")


def _other_chips(x, y):
    return [(1 - x, y), (x, 1 - y), (1 - x, 1 - y)]


def _remote(src, dst, send_sems, recv_sems, n, to):
    return pltpu.make_async_remote_copy(src_ref=src, dst_ref=dst, send_sem=send_sems.at[n],
                                        recv_sem=recv_sems.at[n], device_id=to, device_id_type=MESH)


def _all_gather_weights(wp):
    def body(w_in, o, send_sems, recv_sems):
        x, y, c = _mesh_pos()
        k = 2 * x + y
        sib = (x, y, 1 - c)
        chips = _other_chips(x, y)

        def piece(shard, half):
            return o.at[shard, :, pl.ds(half * P_HALF, P_HALF), :]

        sends = []
        for j, (px, py) in enumerate(chips):
            cp = _remote(piece(k, c), piece(k, c), send_sems, recv_sems, j, (px, py, c))
            cp.start()
            sends.append(cp)
        for j, (px, py) in enumerate(chips):
            landed = piece(2 * px + py, c)
            _remote(landed, landed, send_sems, recv_sems, j, (px, py, c)).wait_recv()
            cp = _remote(landed, landed, send_sems, recv_sems, 3 + j, sib)
            cp.start()
            sends.append(cp)
        for j, (px, py) in enumerate(chips):
            passed = piece(2 * px + py, 1 - c)
            _remote(passed, passed, send_sems, recv_sems, 3 + j, sib).wait_recv()
        for cp in sends:
            cp.wait_send()

    return pl.pallas_call(
        body, name="all_gather_weights",
        in_specs=[_ANY], out_specs=_ANY,
        out_shape=jax.ShapeDtypeStruct(wp.shape, BF16),
        scratch_shapes=[pltpu.SemaphoreType.DMA((6,)), pltpu.SemaphoreType.DMA((6,))],
        input_output_aliases={0: 0},
    )(wp)


def _rs_to_sibling(g):
    def body(g_ref, b, send_sems, recv_sems):
        x, y, c = _mesh_pos()
        cp = _remote(g_ref.at[:, :, pl.ds((1 - c) * P_HALF, P_HALF), :], b, send_sems, recv_sems, 0, (x, y, 1 - c))
        cp.start()
        cp.wait()

    nl = g.shape[0]
    return pl.pallas_call(
        body, name="rs_to_sibling",
        in_specs=[_ANY], out_specs=_ANY,
        out_shape=jax.ShapeDtypeStruct((nl, N_SHARD, P_HALF, D_MODEL), F32),
        scratch_shapes=[pltpu.SemaphoreType.DMA((1,)), pltpu.SemaphoreType.DMA((1,))],
    )(g)


def _rs_add(name, ids, g, buf, row_tile):
    nl, _, hr, cols = buf.shape
    n_rt = hr // row_tile

    def body(ids_ref, g_ref, b_ref, own_ref, tb_ref):
        t = g_ref[...] + b_ref[...]
        tb_ref[...] = t.astype(BF16)

        @pl.when(pl.program_id(2) == ids_ref[1])
        def _():
            own_ref[...] = t

    blk = (None, None, row_tile, cols)
    grid_spec = pltpu.PrefetchScalarGridSpec(
        num_scalar_prefetch=1, grid=(nl, n_rt, N_SHARD),
        in_specs=[pl.BlockSpec(blk, lambda l, j, s, ids_ref: (l, s, ids_ref[0] * n_rt + j, 0)),
                  pl.BlockSpec(blk, lambda l, j, s, ids_ref: (l, s, j, 0))],
        out_specs=[pl.BlockSpec((None, row_tile, cols), lambda l, j, s, ids_ref: (l, j, 0)),
                   pl.BlockSpec(blk, lambda l, j, s, ids_ref: (l, s, j, 0))])
    return pl.pallas_call(
        body, name=name, grid_spec=grid_spec,
        out_shape=[jax.ShapeDtypeStruct((nl, hr, cols), F32), jax.ShapeDtypeStruct(buf.shape, BF16)],
        compiler_params=_cparams(3),
    )(ids, g, buf)


def _rs_to_chips(t):
    def body(t_ref, b, send_sems, recv_sems):
        x, y, c = _mesh_pos()
        cps = []
        for j, (px, py) in enumerate(_other_chips(x, y)):
            cp = _remote(t_ref.at[:, 2 * px + py], b.at[j], send_sems, recv_sems, j, (px, py, c))
            cp.start()
            cps.append(cp)
        for cp in cps:
            cp.wait()

    nl = t.shape[0]
    return pl.pallas_call(
        body, name="rs_to_chips",
        in_specs=[_ANY], out_specs=_ANY,
        out_shape=jax.ShapeDtypeStruct((3, nl, P_HALF, D_MODEL), BF16),
        scratch_shapes=[pltpu.SemaphoreType.DMA((3,)), pltpu.SemaphoreType.DMA((3,))],
    )(t)


def _rs_sum(ids, own, bufb, row_tile):
    nl, hr, cols = own.shape
    n_rt = hr // row_tile

    def body(ids_ref, own_ref, b_ref, f_ref):
        f_ref[...] = ((own_ref[...] + b_ref[0].astype(F32)) + b_ref[1].astype(F32)) + b_ref[2].astype(F32)

    grid_spec = pltpu.PrefetchScalarGridSpec(
        num_scalar_prefetch=1, grid=(nl, n_rt),
        in_specs=[pl.BlockSpec((None, row_tile, cols), lambda l, j, ids_ref: (l, j, 0)),
                  pl.BlockSpec((3, None, row_tile, cols), lambda l, j, ids_ref: (0, l, j, 0))],
        out_specs=pl.BlockSpec((None, row_tile, cols), lambda l, j, ids_ref: (l, ids_ref[0] * n_rt + j, 0)))
    return pl.pallas_call(
        body, name="rs_sum", grid_spec=grid_spec,
        out_shape=jax.ShapeDtypeStruct((nl, 2 * hr, cols), F32),
        compiler_params=_cparams(2),
    )(ids, own, bufb)


def _rs_exchange(f):
    def body(f_in, o, send_sems, recv_sems):
        x, y, c = _mesh_pos()
        mine = o.at[:, pl.ds(c * P_HALF, P_HALF), :]
        cp = _remote(mine, mine, send_sems, recv_sems, 0, (x, y, 1 - c))
        cp.start()
        cp.wait_send()
        theirs = o.at[:, pl.ds((1 - c) * P_HALF, P_HALF), :]
        _remote(theirs, theirs, send_sems, recv_sems, 0, (x, y, 1 - c)).wait_recv()

    return pl.pallas_call(
        body, name="rs_exchange",
        in_specs=[_ANY], out_specs=_ANY,
        out_shape=jax.ShapeDtypeStruct(f.shape, F32),
        scratch_shapes=[pltpu.SemaphoreType.DMA((1,)), pltpu.SemaphoreType.DMA((1,))],
        input_output_aliases={0: 0},
    )(f)


def _small_all_reduce(s):
    n_rows = s.shape[0]
    hr = n_rows // 2

    def body(s_ref, o_ref, sibbuf, tbuf, cbuf, fbuf, send_sems, recv_sems):
        x, y, c = _mesh_pos()
        sib = (x, y, 1 - c)
        mine = pl.ds(pl.multiple_of(c * hr, SUBLANES), hr)
        theirs = pl.ds(pl.multiple_of((1 - c) * hr, SUBLANES), hr)
        first = _remote(s_ref.at[theirs], sibbuf, send_sems, recv_sems, 0, sib)
        first.start()
        first.wait()
        tbuf[...] = s_ref[mine, :] + sibbuf[...]
        cps = []
        for j, (px, py) in enumerate(_other_chips(x, y)):
            cp = _remote(tbuf, cbuf.at[j], send_sems, recv_sems, 1 + j, (px, py, c))
            cp.start()
            cps.append(cp)
        for cp in cps:
            cp.wait()
        f = (tbuf[...] + cbuf[1]) + (cbuf[0] + cbuf[2])
        fbuf[...] = f
        o_ref[mine, :] = f
        last = _remote(fbuf, o_ref.at[mine], send_sems, recv_sems, 4, sib)
        last.start()
        last.wait()

    vmem = pl.BlockSpec(memory_space=pltpu.VMEM)
    return pl.pallas_call(
        body, name="small_all_reduce",
        in_specs=[vmem], out_specs=vmem,
        out_shape=jax.ShapeDtypeStruct(s.shape, F32),
        scratch_shapes=[pltpu.VMEM((hr, D_MODEL), F32), pltpu.VMEM((hr, D_MODEL), F32),
                        pltpu.VMEM((3, hr, D_MODEL), F32), pltpu.VMEM((hr, D_MODEL), F32),
                        pltpu.SemaphoreType.DMA((5,)), pltpu.SemaphoreType.DMA((5,))],
        compiler_params=pltpu.CompilerParams(vmem_limit_bytes=VMEM_LIMIT),
    )(s)


def _reduce_scatter_grads(ids, g):
    own, t = _rs_add("rs_add", ids, g, _rs_to_sibling(g), RS_ROW_TILE)
    return _rs_exchange(_rs_sum(ids, own, _rs_to_chips(t), RS_ROW_TILE))


_SMALL = ("norm_mix", "w_pool", "pool_scale", "lam_re", "lam_im", "log_dt", "b_re", "b_im", "c_re", "c_im",
          "d_skip", "b_glu", "norm_ffn", "norm_final")
_WEIGHTS = ("norm_mix", "w_in", "w_pool", "pool_scale", "lam_re", "lam_im", "log_dt", "b_re", "b_im", "c_re",
            "c_im", "d_skip", "w_glu", "b_glu", "w_out", "norm_ffn", "w_gate", "w_up", "w_down", "norm_final")


def _local_step(x, target, wp, p):
    nl = p["norm_mix"].shape[0]
    n_rows = nl * N_SSM_GROUPS
    lr = p["lam_re"].reshape(n_rows, 1, SSM_STATE)
    li = p["lam_im"].reshape(n_rows, 1, SSM_STATE)
    ldt = p["log_dt"].reshape(n_rows, 1, 1)
    br_t = p["b_re"].reshape(n_rows, SSM_STATE, SSM_GROUP).transpose(0, 2, 1)
    bi_t = p["b_im"].reshape(n_rows, SSM_STATE, SSM_GROUP).transpose(0, 2, 1)
    ar, ai, bbr_t, bbi_t = _disc_fwd(lr, li, ldt, br_t, bi_t)
    ar = ar.reshape(nl, 1, N_STATE)
    ai = ai.reshape(nl, 1, N_STATE)
    bbr = bbr_t.transpose(0, 2, 1).reshape(nl, N_SSM_GROUPS, SSM_STATE, SSM_GROUP)
    bbi = bbi_t.transpose(0, 2, 1).reshape(nl, N_SSM_GROUPS, SSM_STATE, SSM_GROUP)
    w_pool = p["w_pool"].astype(BF16)

    layers = []
    h = x
    for l in range(nl):
        bpad = _pad_pairs(bbr[l], bbi[l]).astype(BF16)
        cpad_t = _pad_pairs(p["c_re"][l].transpose(0, 2, 1), -p["c_im"][l].transpose(0, 2, 1)).astype(BF16)
        dskip = p["d_skip"][l].reshape(1, D_SSM)
        u, ypool = _mix_in_fwd(h, p["norm_mix"][l:l + 1], wp, l, w_pool[l], p["pool_scale"][l:l + 1])
        sre, sim, yraw = _ssm_fwd(u, bpad, cpad_t.transpose(0, 2, 1), ar[l], ai[l], dskip)
        hm = _mix_out_fwd(yraw, ypool, h, wp, l, p["b_glu"][l:l + 1])
        h_next, n2, gate_s, up_s = _ffn_fwd(hm, p["norm_ffn"][l:l + 1], wp, l)
        layers.append(dict(h=h, u=u, ypool=ypool, sre=sre, sim=sim, yraw=yraw, hm=hm, n2=n2, gate_s=gate_s,
                           up_s=up_s, bpad_t=bpad.transpose(0, 2, 1), cpad_t=cpad_t, dskip=dskip))
        h = h_next

    dh, loss, d_norm_final = _final_fwd_bwd(h, p["norm_final"].reshape(1, D_MODEL), target)

    g1 = lax.empty((nl, N_SHARD, P_ROWS, D_MODEL), F32)
    per_layer = {n: [None] * nl for n in ("norm_mix", "w_pool", "pool_scale", "c_re", "c_im", "d_skip", "b_glu",
                                          "norm_ffn", "dar", "dai", "dbbr_t", "dbbi_t")}
    for l in reversed(range(nl)):
        s = layers[l]
        dhm, dg2, dgate_s, dup_s, act_s, dhb = _ffn_bwd_act(dh, s["hm"], p["norm_ffn"][l:l + 1], s["gate_s"],
                                                             s["up_s"], wp, l)
        g1 = _ffn_bwd_w(s["n2"], dgate_s, dup_s, act_s, dhb, g1, l)
        dyraw, dyp, db_glu, g1 = _mix_out_bwd(dhm, s["yraw"], s["ypool"], wp, l, p["b_glu"][l:l + 1], g1)
        dus, dcp, dbp, dar, dai, ddsk = _ssm_bwd(dyraw, s["u"], s["sre"], s["sim"], s["cpad_t"], s["bpad_t"],
                                                  ar[l], ai[l], s["dskip"])
        dup, dwp, dsc = _pool_bwd(dyp, s["u"], w_pool[l], p["pool_scale"][l:l + 1])
        dh, dg1, g1 = _mix_in_bwd(dup, dus, s["h"], dhm, p["norm_mix"][l:l + 1], wp, l, g1)
        dc_re, dc_im = _unpad_pairs(dcp.transpose(0, 2, 1))
        dbbr, dbbi = _unpad_pairs(dbp)
        per_layer["norm_mix"][l] = dg1[0]
        per_layer["w_pool"][l] = dwp
        per_layer["pool_scale"][l] = dsc[0]
        per_layer["c_re"][l] = dc_re.transpose(0, 2, 1)
        per_layer["c_im"][l] = -dc_im.transpose(0, 2, 1)
        per_layer["d_skip"][l] = ddsk.reshape(N_SSM_GROUPS, SSM_GROUP)
        per_layer["b_glu"][l] = db_glu[0]
        per_layer["norm_ffn"][l] = dg2[0]
        per_layer["dar"][l] = dar.reshape(N_SSM_GROUPS, 1, SSM_STATE)
        per_layer["dai"][l] = dai.reshape(N_SSM_GROUPS, 1, SSM_STATE)
        per_layer["dbbr_t"][l] = dbbr.transpose(0, 2, 1)
        per_layer["dbbi_t"][l] = dbbi.transpose(0, 2, 1)

    st = {n: jnp.stack(v) for n, v in per_layer.items()}
    cat = lambda a: a.reshape((n_rows,) + a.shape[2:])
    dlr, dli, dldt, dbr_t, dbi_t = _disc_bwd(lr, li, ldt, br_t, bi_t, cat(st["dar"]), cat(st["dai"]),
                                              cat(st["dbbr_t"]), cat(st["dbbi_t"]))
    small = {n: st[n] for n in ("norm_mix", "w_pool", "pool_scale", "c_re", "c_im", "d_skip", "b_glu", "norm_ffn")}
    small["lam_re"] = dlr.reshape(nl, N_SSM_GROUPS, SSM_STATE)
    small["lam_im"] = dli.reshape(nl, N_SSM_GROUPS, SSM_STATE)
    small["log_dt"] = dldt.reshape(nl, N_SSM_GROUPS)
    small["b_re"] = dbr_t.transpose(0, 2, 1).reshape(nl, N_SSM_GROUPS, SSM_STATE, SSM_GROUP)
    small["b_im"] = dbi_t.transpose(0, 2, 1).reshape(nl, N_SSM_GROUPS, SSM_STATE, SSM_GROUP)
    small["norm_final"] = d_norm_final[0]
    return loss, dh, g1, small


def _flatten_small(d):
    flat = jnp.concatenate([d[n].reshape(-1) for n in _SMALL])
    n_rows = -(-flat.shape[0] // (32 * D_MODEL)) * 32
    return jnp.pad(flat, (0, n_rows * D_MODEL - flat.shape[0])).reshape(n_rows, D_MODEL)


def _split_small(flat, like):
    flat = flat.reshape(-1)
    out, at = {}, 0
    for n in _SMALL:
        size = like[n].size
        out[n] = flat[at:at + size].reshape(like[n].shape)
        at += size
    return out


def kernel(x, norm_mix, w_in, w_pool, pool_scale, lam_re, lam_im, log_dt, b_re, b_im, c_re, c_im, d_skip, w_glu, b_glu, w_out, norm_ffn, w_gate, w_up, w_down, norm_final, loss_target, m_norm_mix, m_w_in, m_w_pool, m_pool_scale, m_lam_re, m_lam_im, m_log_dt, m_b_re, m_b_im, m_c_re, m_c_im, m_d_skip, m_w_glu, m_b_glu, m_w_out, m_norm_ffn, m_w_gate, m_w_up, m_w_down, m_norm_final, v_norm_mix, v_w_in, v_w_pool, v_pool_scale, v_lam_re, v_lam_im, v_log_dt, v_b_re, v_b_im, v_c_re, v_c_im, v_d_skip, v_w_glu, v_b_glu, v_w_out, v_norm_ffn, v_w_gate, v_w_up, v_w_down, v_norm_final):
    given = dict(locals())
    w = {n: given[n] for n in _WEIGHTS}
    m = {n: given["m_" + n] for n in _WEIGHTS}
    v = {n: given["v_" + n] for n in _WEIGHTS}
    ids = jnp.stack([lax.axis_index("c"), 2 * lax.axis_index("x") + lax.axis_index("y")]).astype(jnp.int32)

    t_names = ("w_gate", "w_up")
    tr = lambda a: a.transpose(0, 2, 1)
    for d in (w, m, v):
        d.update({n: tr(d[n]) for n in t_names})

    wp = _all_gather_weights(_pack_weights(ids, w["w_in"], w["w_glu"], w["w_out"], w["w_down"], w["w_gate"], w["w_up"]))
    loss, grad_x, g, small = _local_step(x[0], loss_target[0], wp, {n: w[n] for n in _SMALL})
    loss = lax.psum(loss[0, 0], ("x", "y", "c"))

    gr = _reduce_scatter_grads(ids, g)
    small_sum = _small_all_reduce(_flatten_small(small))

    res = {}
    big = (("w_in", P_IN_BLK, 256, False), ("w_out", P_OUT_BLK, 256, False), ("w_down", P_WD_BLK, 352, False),
           ("w_gate", P_WG_BLK, 352, False), ("w_up", P_WU_BLK, 352, False), ("w_glu", P_GLU_BLK, 128, True))
    for n, (blk, idx), row_tile, glu in big:
        res[n] = _adamw("adamw_" + n, w[n], m[n], v[n], gr, (blk, D_MODEL), blk * idx, row_tile, glu)
    for n in t_names:
        res[n] = tuple(tr(a) for a in res[n])
    flat = [_flatten_small(d)[None] for d in (w, m, v)]
    n_rows = flat[0].shape[1]
    outs = _adamw("adamw_small", *flat, small_sum[None], (n_rows, D_MODEL), 0, n_rows // 4)
    parts = [_split_small(o[0], w) for o in outs]
    for n in _SMALL:
        res[n] = tuple(part[n] for part in parts)

    return (loss, grad_x[None], *[res[n][0] for n in _WEIGHTS], *[res[n][1] for n in _WEIGHTS],
            *[res[n][2] for n in _WEIGHTS], *[res[n][3] for n in _WEIGHTS])
```

```python
import functools
import math

import jax
import jax.numpy as jnp
from jax import lax
from jax.experimental import pallas as pl
from jax.experimental.pallas import tpu as pltpu

F32 = jnp.float32
BF16 = jnp.bfloat16

D_MODEL = 1024
D_POOL = 512
D_SSM = 512
POOL_WINDOWS = (2, 4, 8, 16)
POOL_GROUP = 128
POOL_HALO = 16
N_SSM_GROUPS = 32
SSM_GROUP = 16
SSM_STATE = 64
N_STATE = N_SSM_GROUPS * SSM_STATE
N_PAIRS = N_SSM_GROUPS // 2
D_FF = 2816
N_SHARD = 4
FF_SHARD = D_FF // N_SHARD
RMS_EPS = 1e-6

ADAM_LR = 0.001
ADAM_B1 = 0.9
ADAM_B2 = 0.999
ADAM_EPS = 1e-08
ADAM_WD = 0.01
ADAM_STEP = 10

P_ROWS = 2816
P_WD_BLK = (704, 0)
P_WG_BLK = (704, 1)
P_WU_BLK = (704, 2)
P_FF_ROWS = 2112
P_GLU_BLK = (64, 33)
P_GLU_PAD = 192
P_IN_BLK = (256, 9)
P_OUT_BLK = (256, 10)

SUBLANES = 8
VMEM_LIMIT = 56 * 1024 * 1024

TM = 512
TM_FFN = 512
TS = 256
SCAN_LANES = 512


def _cparams(n_axes):
    return pltpu.CompilerParams(dimension_semantics=("arbitrary",) * n_axes, vmem_limit_bytes=VMEM_LIMIT)


def _dot(a, b):
    return jnp.dot(a, b, preferred_element_type=F32)


def _dot_nt(a, b):
    return lax.dot_general(a, b, (((1,), (1,)), ((), ())), preferred_element_type=F32)


def _dot_tn(a, b):
    return lax.dot_general(a, b, (((0,), (0,)), ((), ())), preferred_element_type=F32)


def _rms_hat(x):
    r = lax.rsqrt(jnp.mean(x * x, axis=-1, keepdims=True) + RMS_EPS)
    return x * r, r


def _rms_bwd(d_hat, xhat, r):
    return r * (d_hat - xhat * jnp.mean(d_hat * xhat, axis=-1, keepdims=True))


def _sigmoid(x):
    return 1.0 / (1.0 + jnp.exp(-x))


_GELU_C = math.sqrt(2.0 / math.pi)
_GELU_K = 0.044715


def _gelu(x):
    return 0.5 * x * (1.0 + jnp.tanh(_GELU_C * (x + _GELU_K * x * x * x)))


def _gelu_grad(x):
    th = jnp.tanh(_GELU_C * (x + _GELU_K * x * x * x))
    return 0.5 * (1.0 + th) + 0.5 * x * (1.0 - th * th) * _GELU_C * (1.0 + 3.0 * _GELU_K * x * x)


def _glu_weight(ref):
    v = ref[...]
    return jnp.concatenate([v[:, :, :D_SSM], v[:, :, D_SSM:]], axis=1).reshape(D_SSM, D_SSM)


def _glu_pack(w):
    v = w.reshape(N_SHARD, 128, D_SSM)
    return jnp.concatenate([v[:, :64, :], v[:, 64:, :]], axis=2)


def _pool_diff(ext, row0, tm):
    rows = row0 + lax.broadcasted_iota(jnp.int32, (tm, 1), 0)
    outs = []
    for gi, w in enumerate(POOL_WINDOWS):
        e = ext[:, gi * POOL_GROUP:(gi + 1) * POOL_GROUP]
        s = e
        k = 1
        while k < w:
            s = s + pltpu.roll(s, k, 0)
            k *= 2
        inv = 1.0 / jnp.minimum(rows + 1, w).astype(F32)
        outs.append(s[POOL_HALO:, :] * inv - e[POOL_HALO:, :])
    return outs


def _mix_in_fwd(h, g1, wp, layer, w_pool, scale):
    L = h.shape[0]
    tm = min(TM, L)

    def body(h_ref, g_ref, w_ref, wp_ref, sc_ref, u_ref, yp_ref, carry):
        i = pl.program_id(0)

        @pl.when(i == 0)
        def _():
            carry[...] = jnp.zeros_like(carry)

        xhat, _ = _rms_hat(h_ref[...])
        n1 = (xhat * g_ref[...]).astype(BF16)
        u = _dot(n1, w_ref[...].reshape(D_MODEL, D_MODEL))
        u_ref[...] = u
        up = u[:, :D_POOL]
        ext = jnp.concatenate([carry[...], up], axis=0)
        carry[...] = up[tm - POOL_HALO:, :]
        diffs = _pool_diff(ext, i * tm, tm)
        for gi in range(4):
            cols = slice(gi * POOL_GROUP, (gi + 1) * POOL_GROUP)
            yp_ref[:, cols] = _dot(diffs[gi].astype(BF16), wp_ref[gi]) * sc_ref[:, cols]

    blk, idx = P_IN_BLK
    return pl.pallas_call(
        body, name="mix_in_fwd", grid=(L // tm,),
        in_specs=[pl.BlockSpec((tm, D_MODEL), lambda i: (i, 0)),
                  pl.BlockSpec((1, D_MODEL), lambda i: (0, 0)),
                  pl.BlockSpec((N_SHARD, None, blk, D_MODEL), lambda i: (0, layer, idx, 0)),
                  pl.BlockSpec((4, POOL_GROUP, POOL_GROUP), lambda i: (0, 0, 0)),
                  pl.BlockSpec((1, D_POOL), lambda i: (0, 0))],
        out_specs=[pl.BlockSpec((tm, D_MODEL), lambda i: (i, 0)),
                   pl.BlockSpec((tm, D_POOL), lambda i: (i, 0))],
        out_shape=[jax.ShapeDtypeStruct((L, D_MODEL), F32), jax.ShapeDtypeStruct((L, D_POOL), F32)],
        scratch_shapes=[pltpu.VMEM((POOL_HALO, D_POOL), F32)],
        compiler_params=_cparams(1),
    )(h, g1, wp, w_pool, scale)


def _cmul(xr, xi, yr, yi):
    return xr * yr - xi * yi, xr * yi + xi * yr


def _scan_tables(ar, ai, tab, reverse):
    c = ar.shape[1]
    row = lax.broadcasted_iota(jnp.int32, (SUBLANES, c), 0)
    a2r, a2i = _cmul(ar, ai, ar, ai)
    a4r, a4i = _cmul(a2r, a2i, a2r, a2i)
    zero = jnp.zeros((SUBLANES, c), F32)
    for n, (s, pr, pi) in enumerate(((1, ar, ai), (2, a2r, a2i), (4, a4r, a4i))):
        keep = (row < SUBLANES - s) if reverse else (row >= s)
        tab[2 * n] = jnp.where(keep, pr, zero)
        tab[2 * n + 1] = jnp.where(keep, pi, zero)
    cr, ci = ar, ai
    tr, ti = zero, zero
    for n in range(SUBLANES):
        at = (SUBLANES - 1 - n) if reverse else n
        tr = jnp.where(row == at, cr, tr)
        ti = jnp.where(row == at, ci, ti)
        cr, ci = _cmul(cr, ci, ar, ai)
    tab[6] = tr
    tab[7] = ti


def _ssm_fwd(u, bpad, cpad, ar, ai, dskip):
    L = u.shape[0]
    ts = min(TS, L)
    nq = 4
    cq = N_STATE // nq

    def body(u_ref, bp_ref, cp_ref, ar_ref, ai_ref, dsk_ref, sre_ref, sim_ref, y_ref, cr, ci, tab):
        t = pl.program_id(1)

        @pl.when(t == 0)
        def _():
            cr[...] = jnp.zeros_like(cr)
            ci[...] = jnp.zeros_like(ci)
            _scan_tables(ar_ref[...], ai_ref[...], tab, reverse=False)

        uf = u_ref[...]
        ub = uf.astype(BF16)
        for jj in range(4):
            bu = _dot(ub, bp_ref[jj])
            sre_ref[:, jj * 128:(jj + 1) * 128] = bu[:, :128]
            sim_ref[:, jj * 128:(jj + 1) * 128] = bu[:, 128:]

        for cc in range(cq // SCAN_LANES):
            cols = slice(cc * SCAN_LANES, (cc + 1) * SCAN_LANES)
            def step(i, carry, cols=cols):
                c_r, c_i = carry
                r0 = pl.multiple_of(i * SUBLANES, SUBLANES)
                xr = sre_ref[pl.ds(r0, SUBLANES), cols]
                xi = sim_ref[pl.ds(r0, SUBLANES), cols]
                for n, s in enumerate((1, 2, 4)):
                    tr, ti = tab[2 * n, :, cols], tab[2 * n + 1, :, cols]
                    rr = pltpu.roll(xr, s, 0)
                    ri = pltpu.roll(xi, s, 0)
                    xr, xi = xr + tr * rr - ti * ri, xi + tr * ri + ti * rr
                pr, pi = tab[6, :, cols], tab[7, :, cols]
                xr, xi = xr + pr * c_r - pi * c_i, xi + pr * c_i + pi * c_r
                sre_ref[pl.ds(r0, SUBLANES), cols] = xr
                sim_ref[pl.ds(r0, SUBLANES), cols] = xi
                shp = (SUBLANES, SCAN_LANES)
                return (jnp.broadcast_to(xr[SUBLANES - 1:, :], shp), jnp.broadcast_to(xi[SUBLANES - 1:, :], shp))

            c_r, c_i = lax.fori_loop(0, ts // SUBLANES, step, (cr[:, cols], ci[:, cols]), unroll=2)
            cr[:, cols] = c_r
            ci[:, cols] = c_i

        acc = dsk_ref[...] * uf
        for jj in range(4):
            cols = slice(jj * 128, (jj + 1) * 128)
            scat = jnp.concatenate([sre_ref[:, cols], sim_ref[:, cols]], axis=1).astype(BF16)
            acc = acc + _dot(scat, cp_ref[jj])
        y_ref[...] = acc

    return pl.pallas_call(
        body, name="ssm_fwd", grid=(nq, L // ts),
        in_specs=[pl.BlockSpec((ts, 128), lambda q, t: (t, 4 + q)),
                  pl.BlockSpec((4, 128, 256), lambda q, t: (q, 0, 0)),
                  pl.BlockSpec((4, 256, 128), lambda q, t: (q, 0, 0)),
                  pl.BlockSpec((1, cq), lambda q, t: (0, q)),
                  pl.BlockSpec((1, cq), lambda q, t: (0, q)),
                  pl.BlockSpec((1, 128), lambda q, t: (0, q))],
        out_specs=[pl.BlockSpec((ts, cq), lambda q, t: (t, q)),
                   pl.BlockSpec((ts, cq), lambda q, t: (t, q)),
                   pl.BlockSpec((ts, 128), lambda q, t: (t, q))],
        out_shape=[jax.ShapeDtypeStruct((L, N_STATE), F32), jax.ShapeDtypeStruct((L, N_STATE), F32),
                   jax.ShapeDtypeStruct((L, D_SSM), F32)],
        scratch_shapes=[pltpu.VMEM((SUBLANES, cq), F32), pltpu.VMEM((SUBLANES, cq), F32),
                        pltpu.VMEM((8, SUBLANES, cq), F32)],
        compiler_params=_cparams(2),
    )(u, bpad, cpad, ar, ai, dskip)


def _mix_out_fwd(yraw, ypool, h, wp, layer, b_glu):
    L = h.shape[0]
    tm = min(TM, L)

    def body(yr_ref, yp_ref, h_ref, wglu_ref, b_ref, wout_ref, o_ref):
        y = _gelu(yr_ref[...])
        z = _dot(y.astype(BF16), _glu_weight(wglu_ref)) + b_ref[...]
        o = y * _sigmoid(z)
        mix = jnp.concatenate([yp_ref[...], o], axis=1).astype(BF16)
        o_ref[...] = h_ref[...] + _dot(mix, wout_ref[...].reshape(D_MODEL, D_MODEL))

    gb, gi = P_GLU_BLK
    ob, oi = P_OUT_BLK
    return pl.pallas_call(
        body, name="mix_out_fwd", grid=(L // tm,),
        in_specs=[pl.BlockSpec((tm, D_SSM), lambda i: (i, 0)),
                  pl.BlockSpec((tm, D_POOL), lambda i: (i, 0)),
                  pl.BlockSpec((tm, D_MODEL), lambda i: (i, 0)),
                  pl.BlockSpec((N_SHARD, None, gb, D_MODEL), lambda i: (0, layer, gi, 0)),
                  pl.BlockSpec((1, D_SSM), lambda i: (0, 0)),
                  pl.BlockSpec((N_SHARD, None, ob, D_MODEL), lambda i: (0, layer, oi, 0))],
        out_specs=pl.BlockSpec((tm, D_MODEL), lambda i: (i, 0)),
        out_shape=jax.ShapeDtypeStruct((L, D_MODEL), F32),
        compiler_params=_cparams(1),
    )(yraw, ypool, h, wp, b_glu, wp)


def _ffn_weights(ref):
    return ref[0:FF_SHARD, :], ref[FF_SHARD:2 * FF_SHARD, :], ref[2 * FF_SHARD:P_FF_ROWS, :]


def _ffn_fwd(h, g2, wp, layer):
    L = h.shape[0]
    tm = min(TM_FFN, L)

    def body(h_ref, g_ref, w_ref, o_ref, n2_ref, gate_ref, up_ref):
        k = pl.program_id(1)

        @pl.when(k == 0)
        def _():
            x = h_ref[...]
            xhat, _ = _rms_hat(x)
            n2_ref[...] = (xhat * g_ref[...]).astype(BF16)
            o_ref[...] = x

        wd, wg_t, wu_t = _ffn_weights(w_ref)
        n2 = n2_ref[...]
        gate = _dot_nt(n2, wg_t)
        up = _dot_nt(n2, wu_t)
        gate_ref[...] = gate.astype(BF16)
        up_ref[...] = up.astype(BF16)
        act = (gate * _sigmoid(gate) * up).astype(BF16)
        o_ref[...] += _dot(act, wd)

    act_shape = jax.ShapeDtypeStruct((N_SHARD, L, FF_SHARD), BF16)
    return pl.pallas_call(
        body, name="ffn_fwd", grid=(L // tm, N_SHARD),
        in_specs=[pl.BlockSpec((tm, D_MODEL), lambda m, k: (m, 0)),
                  pl.BlockSpec((1, D_MODEL), lambda m, k: (0, 0)),
                  pl.BlockSpec((None, None, P_FF_ROWS, D_MODEL), lambda m, k: (k, layer, 0, 0))],
        out_specs=[pl.BlockSpec((tm, D_MODEL), lambda m, k: (m, 0)),
                   pl.BlockSpec((tm, D_MODEL), lambda m, k: (m, 0)),
                   pl.BlockSpec((None, tm, FF_SHARD), lambda m, k: (k, m, 0)),
                   pl.BlockSpec((None, tm, FF_SHARD), lambda m, k: (k, m, 0))],
        out_shape=[jax.ShapeDtypeStruct((L, D_MODEL), F32), jax.ShapeDtypeStruct((L, D_MODEL), BF16),
                   act_shape, act_shape],
        compiler_params=_cparams(2),
    )(h, g2, wp)


def _final_fwd_bwd(h, gf, target):
    L = h.shape[0]
    tm = min(TM, L)

    def body(h_ref, g_ref, t_ref, dh_ref, loss_ref, dg_ref):
        i = pl.program_id(0)

        @pl.when(i == 0)
        def _():
            loss_ref[...] = jnp.zeros_like(loss_ref)
            dg_ref[...] = jnp.zeros_like(dg_ref)

        xhat, r = _rms_hat(h_ref[...])
        g = g_ref[...]
        e = xhat * g - t_ref[...]
        loss_ref[...] += 0.5 * jnp.sum(jnp.mean(e * e, axis=-1, keepdims=True), axis=0, keepdims=True)
        dy = e * (1.0 / D_MODEL)
        dg_ref[...] += jnp.sum(dy * xhat, axis=0, keepdims=True)
        dh_ref[...] = _rms_bwd(dy * g, xhat, r)

    return pl.pallas_call(
        body, name="final_fwd_bwd", grid=(L // tm,),
        in_specs=[pl.BlockSpec((tm, D_MODEL), lambda i: (i, 0)),
                  pl.BlockSpec((1, D_MODEL), lambda i: (0, 0)),
                  pl.BlockSpec((tm, D_MODEL), lambda i: (i, 0))],
        out_specs=[pl.BlockSpec((tm, D_MODEL), lambda i: (i, 0)),
                   pl.BlockSpec((1, 1), lambda i: (0, 0)),
                   pl.BlockSpec((1, D_MODEL), lambda i: (0, 0))],
        out_shape=[jax.ShapeDtypeStruct((L, D_MODEL), F32), jax.ShapeDtypeStruct((1, 1), F32),
                   jax.ShapeDtypeStruct((1, D_MODEL), F32)],
        compiler_params=_cparams(1),
    )(h, gf, target)


def _ffn_bwd_act(dh, h, g2, gate_s, up_s, wp, layer):
    L = h.shape[0]
    tm = min(TM_FFN, L)

    def body(dh_ref, h_ref, g_ref, gate_ref, up_ref, w_ref,
             dhm_ref, dg_ref, dgate_ref, dup_ref, act_ref, dhb_ref, dn2):
        m, k = pl.program_id(0), pl.program_id(1)

        @pl.when(jnp.logical_and(m == 0, k == 0))
        def _():
            dg_ref[...] = jnp.zeros_like(dg_ref)

        @pl.when(k == 0)
        def _():
            dhb_ref[...] = dh_ref[...].astype(BF16)
            dn2[...] = jnp.zeros_like(dn2)

        wd, wg_t, wu_t = _ffn_weights(w_ref)
        dact = _dot_nt(dhb_ref[...], wd)
        gate = gate_ref[...].astype(F32)
        up = up_ref[...].astype(F32)
        sg = _sigmoid(gate)
        silu = gate * sg
        dgate = (dact * up * (sg * (1.0 + gate * (1.0 - sg)))).astype(BF16)
        dup = (dact * silu).astype(BF16)
        dgate_ref[...] = dgate
        dup_ref[...] = dup
        act_ref[...] = (silu * up).astype(BF16)
        dn2[...] += _dot(dgate, wg_t) + _dot(dup, wu_t)

        @pl.when(k == N_SHARD - 1)
        def _():
            xhat, r = _rms_hat(h_ref[...])
            d = dn2[...]
            dg_ref[...] += jnp.sum(d * xhat, axis=0, keepdims=True)
            dhm_ref[...] = dh_ref[...] + _rms_bwd(d * g_ref[...], xhat, r)

    act_spec = pl.BlockSpec((None, tm, FF_SHARD), lambda m, k: (k, m, 0))
    act_shape = jax.ShapeDtypeStruct((N_SHARD, L, FF_SHARD), BF16)
    row_spec = pl.BlockSpec((tm, D_MODEL), lambda m, k: (m, 0))
    return pl.pallas_call(
        body, name="ffn_bwd_act", grid=(L // tm, N_SHARD),
        in_specs=[row_spec, row_spec,
                  pl.BlockSpec((1, D_MODEL), lambda m, k: (0, 0)),
                  act_spec, act_spec,
                  pl.BlockSpec((None, None, P_FF_ROWS, D_MODEL), lambda m, k: (k, layer, 0, 0))],
        out_specs=[row_spec,
                   pl.BlockSpec((1, D_MODEL), lambda m, k: (0, 0)),
                   act_spec, act_spec, act_spec, row_spec],
        out_shape=[jax.ShapeDtypeStruct((L, D_MODEL), F32), jax.ShapeDtypeStruct((1, D_MODEL), F32),
                   act_shape, act_shape, act_shape, jax.ShapeDtypeStruct((L, D_MODEL), BF16)],
        scratch_shapes=[pltpu.VMEM((tm, D_MODEL), F32)],
        compiler_params=_cparams(2),
    )(dh, h, g2, gate_s, up_s, wp)


def _ffn_bwd_w(n2, dgate_s, dup_s, act_s, dhb, gbuf, layer):
    L = n2.shape[0]
    tm = min(TM_FFN, L)

    def body(n2_ref, dgate_ref, dup_ref, act_ref, dhb_ref, g_in, g_ref):
        m = pl.program_id(1)

        @pl.when(m == 0)
        def _():
            g_ref[...] = jnp.zeros_like(g_ref)

        n2v = n2_ref[...]
        g_ref[0:FF_SHARD, :] += _dot_tn(act_ref[...], dhb_ref[...])
        g_ref[FF_SHARD:2 * FF_SHARD, :] += _dot_tn(dgate_ref[...], n2v)
        g_ref[2 * FF_SHARD:P_FF_ROWS, :] += _dot_tn(dup_ref[...], n2v)

    act_spec = pl.BlockSpec((None, tm, FF_SHARD), lambda k, m: (k, m, 0))
    row_spec = pl.BlockSpec((tm, D_MODEL), lambda k, m: (m, 0))
    return pl.pallas_call(
        body, name="ffn_bwd_w", grid=(N_SHARD, L // tm),
        in_specs=[row_spec, act_spec, act_spec, act_spec, row_spec, pl.BlockSpec(memory_space=pl.ANY)],
        out_specs=pl.BlockSpec((None, None, P_FF_ROWS, D_MODEL), lambda k, m: (layer, k, 0, 0)),
        out_shape=jax.ShapeDtypeStruct(gbuf.shape, F32),
        input_output_aliases={5: 0},
        compiler_params=_cparams(2),
    )(n2, dgate_s, dup_s, act_s, dhb, gbuf)


def _mix_out_bwd(dhm, yraw, ypool, wp, layer, b_glu, gbuf):
    L = dhm.shape[0]
    tm = min(TM, L)

    def body(dhm_ref, yr_ref, yp_ref, wglu_ref, b_ref, wout_ref, g1_in,
             dyr_ref, dyp_ref, db_ref, g1_ref, dwout, dwglu, gpack):
        i = pl.program_id(0)

        @pl.when(i == 0)
        def _():
            db_ref[...] = jnp.zeros_like(db_ref)
            dwout[...] = jnp.zeros_like(dwout)
            dwglu[...] = jnp.zeros_like(dwglu)

        dhb = dhm_ref[...].astype(BF16)
        wglu = _glu_weight(wglu_ref)
        dmix = _dot_nt(dhb, wout_ref[...].reshape(D_MODEL, D_MODEL))
        dyp_ref[...] = dmix[:, :D_POOL]
        d_o = dmix[:, D_POOL:]
        yraw_v = yr_ref[...]
        y = _gelu(yraw_v)
        yb = y.astype(BF16)
        sig = _sigmoid(_dot(yb, wglu) + b_ref[...])
        mix = jnp.concatenate([yp_ref[...], y * sig], axis=1).astype(BF16)
        dwout[...] += _dot_tn(mix, dhb).reshape(N_SHARD, 256, D_MODEL)
        dz = d_o * y * sig * (1.0 - sig)
        dzb = dz.astype(BF16)
        db_ref[...] += jnp.sum(dz, axis=0, keepdims=True)
        dwglu[...] += _dot_tn(yb, dzb)
        dy = d_o * sig + _dot_nt(dzb, wglu)
        dyr_ref[...] = dy * _gelu_grad(yraw_v)

        @pl.when(i == n_steps - 1)
        def _():
            gpack[:, :gb, :] = _glu_pack(dwglu[...])
            gpack[:, gb:, :] = jnp.zeros((N_SHARD, P_GLU_PAD - gb, D_MODEL), F32)
            pltpu.sync_copy(gpack, g1_ref.at[layer, :, pl.ds(gb * gi, P_GLU_PAD), :])
            pltpu.sync_copy(dwout, g1_ref.at[layer, :, pl.ds(ob * oi, ob), :])

    gb, gi = P_GLU_BLK
    ob, oi = P_OUT_BLK
    n_steps = L // tm
    return pl.pallas_call(
        body, name="mix_out_bwd", grid=(n_steps,),
        in_specs=[pl.BlockSpec((tm, D_MODEL), lambda i: (i, 0)),
                  pl.BlockSpec((tm, D_SSM), lambda i: (i, 0)),
                  pl.BlockSpec((tm, D_POOL), lambda i: (i, 0)),
                  pl.BlockSpec((N_SHARD, None, gb, D_MODEL), lambda i: (0, layer, gi, 0)),
                  pl.BlockSpec((1, D_SSM), lambda i: (0, 0)),
                  pl.BlockSpec((N_SHARD, None, ob, D_MODEL), lambda i: (0, layer, oi, 0)),
                  pl.BlockSpec(memory_space=pl.ANY)],
        out_specs=[pl.BlockSpec((tm, D_SSM), lambda i: (i, 0)),
                   pl.BlockSpec((tm, D_POOL), lambda i: (i, 0)),
                   pl.BlockSpec((1, D_SSM), lambda i: (0, 0)),
                   pl.BlockSpec(memory_space=pl.ANY)],
        out_shape=[jax.ShapeDtypeStruct((L, D_SSM), F32), jax.ShapeDtypeStruct((L, D_POOL), F32),
                   jax.ShapeDtypeStruct((1, D_SSM), F32),
                   jax.ShapeDtypeStruct(gbuf.shape, F32)],
        scratch_shapes=[pltpu.VMEM((N_SHARD, ob, D_MODEL), F32), pltpu.VMEM((D_SSM, D_SSM), F32),
                        pltpu.VMEM((N_SHARD, P_GLU_PAD, D_MODEL), F32)],
        input_output_aliases={6: 3},
        compiler_params=_cparams(1),
    )(dhm, yraw, ypool, wp, b_glu, wp, gbuf)


def _ssm_bwd(dyraw, u, sre, sim, cpad_t, bpad_t, ar, ai, dskip):
    L = u.shape[0]
    ts = min(TS, L)
    nt = L // ts
    nq = 4
    cq = N_STATE // nq

    def body(dy_ref, u_ref, sre_ref, sim_ref, ct_ref, bt_ref, ar_ref, ai_ref, dsk_ref,
             du_ref, dcp_ref, dbp_ref, dar_ref, dai_ref, ddsk_ref, gre, gim, cr, ci, tab, accr, acci):
        t = pl.program_id(1)

        @pl.when(t == 0)
        def _():
            for ref in (cr, ci, accr, acci, dcp_ref, dbp_ref, ddsk_ref):
                ref[...] = jnp.zeros_like(ref)
            _scan_tables(ar_ref[...], -ai_ref[...], tab, reverse=True)

        dy = dy_ref[...]
        dyb = dy.astype(BF16)
        uf = u_ref[...]
        ub = uf.astype(BF16)
        for jj in range(4):
            cols = slice(jj * 128, (jj + 1) * 128)
            ds = _dot(dyb, ct_ref[jj])
            gre[:, cols] = ds[:, :128]
            gim[:, cols] = ds[:, 128:]
            scat = jnp.concatenate([sre_ref[:, cols], sim_ref[:, cols]], axis=1).astype(BF16)
            dcp_ref[jj] += _dot_tn(scat, dyb)

        n_grp = ts // SUBLANES
        shp = (SUBLANES, SCAN_LANES)
        last_row = lax.broadcasted_iota(jnp.int32, shp, 0) == SUBLANES - 1
        for cc in range(cq // SCAN_LANES):
            cols = slice(cc * SCAN_LANES, (cc + 1) * SCAN_LANES)
            def step(i, carry, cols=cols):
                c_r, c_i, a_r, a_i = carry
                r0 = pl.multiple_of((n_grp - 1 - i) * SUBLANES, SUBLANES)
                xr = gre[pl.ds(r0, SUBLANES), cols]
                xi = gim[pl.ds(r0, SUBLANES), cols]
                for n, s in enumerate((1, 2, 4)):
                    tr, ti = tab[2 * n, :, cols], tab[2 * n + 1, :, cols]
                    rr = pltpu.roll(xr, SUBLANES - s, 0)
                    ri = pltpu.roll(xi, SUBLANES - s, 0)
                    xr, xi = xr + tr * rr - ti * ri, xi + tr * ri + ti * rr
                qr, qi = tab[6, :, cols], tab[7, :, cols]
                xr, xi = xr + qr * c_r - qi * c_i, xi + qr * c_i + qi * c_r
                gre[pl.ds(r0, SUBLANES), cols] = xr
                gim[pl.ds(r0, SUBLANES), cols] = xi
                nr = jnp.where(last_row, c_r, pltpu.roll(xr, SUBLANES - 1, 0))
                ni = jnp.where(last_row, c_i, pltpu.roll(xi, SUBLANES - 1, 0))
                sr = sre_ref[pl.ds(r0, SUBLANES), cols]
                si = sim_ref[pl.ds(r0, SUBLANES), cols]
                a_r = a_r + sr * nr + si * ni
                a_i = a_i + sr * ni - si * nr
                return (jnp.broadcast_to(xr[:1, :], shp), jnp.broadcast_to(xi[:1, :], shp), a_r, a_i)

            c_r, c_i, a_r, a_i = lax.fori_loop(
                0, n_grp, step, (cr[:, cols], ci[:, cols], accr[:, cols], acci[:, cols]), unroll=2)
            cr[:, cols] = c_r
            ci[:, cols] = c_i
            accr[:, cols] = a_r
            acci[:, cols] = a_i

        acc = dsk_ref[...] * dy
        for jj in range(4):
            cols = slice(jj * 128, (jj + 1) * 128)
            gcat = jnp.concatenate([gre[:, cols], gim[:, cols]], axis=1).astype(BF16)
            acc = acc + _dot(gcat, bt_ref[jj])
            dbp_ref[jj] += _dot_tn(ub, gcat)
        du_ref[...] = acc
        ddsk_ref[...] += jnp.sum(dy * uf, axis=0, keepdims=True)

        @pl.when(t == nt - 1)
        def _():
            dar_ref[...] = jnp.sum(accr[...], axis=0, keepdims=True)
            dai_ref[...] = jnp.sum(acci[...], axis=0, keepdims=True)

    f32_scr = lambda *s: pltpu.VMEM(s, F32)
    return pl.pallas_call(
        body, name="ssm_bwd", grid=(nq, nt),
        in_specs=[pl.BlockSpec((ts, 128), lambda q, t: (nt - 1 - t, q)),
                  pl.BlockSpec((ts, 128), lambda q, t: (nt - 1 - t, 4 + q)),
                  pl.BlockSpec((ts, cq), lambda q, t: (nt - 1 - t, q)),
                  pl.BlockSpec((ts, cq), lambda q, t: (nt - 1 - t, q)),
                  pl.BlockSpec((4, 128, 256), lambda q, t: (q, 0, 0)),
                  pl.BlockSpec((4, 256, 128), lambda q, t: (q, 0, 0)),
                  pl.BlockSpec((1, cq), lambda q, t: (0, q)),
                  pl.BlockSpec((1, cq), lambda q, t: (0, q)),
                  pl.BlockSpec((1, 128), lambda q, t: (0, q))],
        out_specs=[pl.BlockSpec((ts, 128), lambda q, t: (nt - 1 - t, q)),
                   pl.BlockSpec((4, 256, 128), lambda q, t: (q, 0, 0)),
                   pl.BlockSpec((4, 128, 256), lambda q, t: (q, 0, 0)),
                   pl.BlockSpec((1, cq), lambda q, t: (0, q)),
                   pl.BlockSpec((1, cq), lambda q, t: (0, q)),
                   pl.BlockSpec((1, 128), lambda q, t: (0, q))],
        out_shape=[jax.ShapeDtypeStruct((L, D_SSM), F32),
                   jax.ShapeDtypeStruct((N_PAIRS, 256, 128), F32), jax.ShapeDtypeStruct((N_PAIRS, 128, 256), F32),
                   jax.ShapeDtypeStruct((1, N_STATE), F32), jax.ShapeDtypeStruct((1, N_STATE), F32),
                   jax.ShapeDtypeStruct((1, D_SSM), F32)],
        scratch_shapes=[f32_scr(ts, cq), f32_scr(ts, cq), f32_scr(SUBLANES, cq), f32_scr(SUBLANES, cq),
                        f32_scr(8, SUBLANES, cq), f32_scr(SUBLANES, cq), f32_scr(SUBLANES, cq)],
        compiler_params=_cparams(2),
    )(dyraw, u, sre, sim, cpad_t, bpad_t, ar, ai, dskip)


def _pool_bwd(dyp, u, w_pool, scale):
    L = u.shape[0]
    tm = min(TM, L)
    nt = L // tm
    halo_per_tile = tm // POOL_HALO

    def body(dyp_ref, u_ref, halo_ref, wp_ref, sc_ref, du_ref, dwp_ref, dsc_ref, carry):
        i = pl.program_id(0)
        tile = nt - 1 - i

        @pl.when(i == 0)
        def _():
            carry[...] = jnp.zeros_like(carry)
            dwp_ref[...] = jnp.zeros_like(dwp_ref)
            dsc_ref[...] = jnp.zeros_like(dsc_ref)

        up = u_ref[...]
        halo = jnp.where(tile > 0, halo_ref[...], jnp.zeros_like(halo_ref))
        diffs = _pool_diff(jnp.concatenate([halo, up], axis=0), tile * tm, tm)
        rows = tile * tm + lax.broadcasted_iota(jnp.int32, (tm, 1), 0)
        n_ext = tm + POOL_HALO
        for gi, w in enumerate(POOL_WINDOWS):
            cols = slice(gi * POOL_GROUP, (gi + 1) * POOL_GROUP)
            db = diffs[gi].astype(BF16)
            dyp = dyp_ref[:, cols]
            dsc_ref[:, cols] += jnp.sum(dyp * _dot(db, wp_ref[gi]), axis=0, keepdims=True)
            dp = (dyp * sc_ref[:, cols]).astype(BF16)
            ddiff = _dot_nt(dp, wp_ref[gi])
            dwp_ref[gi] += _dot_tn(db, dp)
            e = ddiff * (1.0 / jnp.minimum(rows + 1, w).astype(F32))
            s = jnp.concatenate([e, carry[:, cols]], axis=0)
            k = 1
            while k < w:
                s = s + pltpu.roll(s, n_ext - k, 0)
                k *= 2
            du_ref[:, cols] = s[:tm, :] - ddiff
            carry[:, cols] = e[:POOL_HALO, :]

    return pl.pallas_call(
        body, name="pool_bwd", grid=(nt,),
        in_specs=[pl.BlockSpec((tm, D_POOL), lambda i: (nt - 1 - i, 0)),
                  pl.BlockSpec((tm, D_POOL), lambda i: (nt - 1 - i, 0)),
                  pl.BlockSpec((POOL_HALO, D_POOL), lambda i: (jnp.maximum((nt - 1 - i) * halo_per_tile - 1, 0), 0)),
                  pl.BlockSpec((4, POOL_GROUP, POOL_GROUP), lambda i: (0, 0, 0)),
                  pl.BlockSpec((1, D_POOL), lambda i: (0, 0))],
        out_specs=[pl.BlockSpec((tm, D_POOL), lambda i: (nt - 1 - i, 0)),
                   pl.BlockSpec((4, POOL_GROUP, POOL_GROUP), lambda i: (0, 0, 0)),
                   pl.BlockSpec((1, D_POOL), lambda i: (0, 0))],
        out_shape=[jax.ShapeDtypeStruct((L, D_POOL), F32),
                   jax.ShapeDtypeStruct((4, POOL_GROUP, POOL_GROUP), F32),
                   jax.ShapeDtypeStruct((1, D_POOL), F32)],
        scratch_shapes=[pltpu.VMEM((POOL_HALO, D_POOL), F32)],
        compiler_params=_cparams(1),
    )(dyp, u, u, w_pool, scale)


def _mix_in_bwd(dup, dus, h, dhm, g1, wp, layer, gbuf):
    L = h.shape[0]
    tm = min(TM, L)
    n_steps = L // tm
    blk, idx = P_IN_BLK

    def body(dup_ref, dus_ref, h_ref, dhm_ref, g_ref, w_ref, g1_in, dh_ref, dg_ref, g1_ref, dwin):
        i = pl.program_id(0)

        @pl.when(i == 0)
        def _():
            dg_ref[...] = jnp.zeros_like(dg_ref)
            dwin[...] = jnp.zeros_like(dwin)

        du = jnp.concatenate([dup_ref[...], dus_ref[...]], axis=1).astype(BF16)
        dn1 = _dot_nt(du, w_ref[...].reshape(D_MODEL, D_MODEL))
        xhat, r = _rms_hat(h_ref[...])
        g = g_ref[...]
        n1 = (xhat * g).astype(BF16)
        dwin[...] += _dot_tn(n1, du).reshape(N_SHARD, blk, D_MODEL)
        dg_ref[...] += jnp.sum(dn1 * xhat, axis=0, keepdims=True)
        dh_ref[...] = dhm_ref[...] + _rms_bwd(dn1 * g, xhat, r)

        @pl.when(i == n_steps - 1)
        def _():
            pltpu.sync_copy(dwin, g1_ref.at[layer, :, pl.ds(blk * idx, blk), :])

    row_spec = pl.BlockSpec((tm, D_MODEL), lambda i: (i, 0))
    half_spec = pl.BlockSpec((tm, D_POOL), lambda i: (i, 0))
    return pl.pallas_call(
        body, name="mix_in_bwd", grid=(n_steps,),
        in_specs=[half_spec, half_spec, row_spec, row_spec,
                  pl.BlockSpec((1, D_MODEL), lambda i: (0, 0)),
                  pl.BlockSpec((N_SHARD, None, blk, D_MODEL), lambda i: (0, layer, idx, 0)),
                  pl.BlockSpec(memory_space=pl.ANY)],
        out_specs=[row_spec, pl.BlockSpec((1, D_MODEL), lambda i: (0, 0)), pl.BlockSpec(memory_space=pl.ANY)],
        out_shape=[jax.ShapeDtypeStruct((L, D_MODEL), F32), jax.ShapeDtypeStruct((1, D_MODEL), F32),
                   jax.ShapeDtypeStruct(gbuf.shape, F32)],
        scratch_shapes=[pltpu.VMEM((N_SHARD, blk, D_MODEL), F32)],
        input_output_aliases={6: 2},
        compiler_params=_cparams(1),
    )(dup, dus, h, dhm, g1, wp, gbuf)


def _disc_math(lr, li, ldt, br_t, bi_t):
    dt = jnp.exp(ldt)
    mag = jnp.exp(lr * dt)
    ang = li * dt
    ar = mag * jnp.cos(ang)
    ai = mag * jnp.sin(ang)
    den = lr * lr + li * li
    nr, ni = ar - 1.0, ai
    cr = (nr * lr + ni * li) / den
    ci = (ni * lr - nr * li) / den
    return ar, ai, cr * br_t - ci * bi_t, cr * bi_t + ci * br_t


def _disc_fwd(lr, li, ldt, br_t, bi_t):
    def body(lr_ref, li_ref, ldt_ref, br_ref, bi_ref, ar_ref, ai_ref, bbr_ref, bbi_ref):
        ar, ai, bbr, bbi = _disc_math(lr_ref[...], li_ref[...], ldt_ref[...], br_ref[...], bi_ref[...])
        ar_ref[...] = ar
        ai_ref[...] = ai
        bbr_ref[...] = bbr
        bbi_ref[...] = bbi

    shapes = [jax.ShapeDtypeStruct(a.shape, F32) for a in (lr, li, br_t, bi_t)]
    return pl.pallas_call(body, name="ssm_disc_fwd", out_shape=shapes,
                          compiler_params=pltpu.CompilerParams(vmem_limit_bytes=VMEM_LIMIT))(lr, li, ldt, br_t, bi_t)


def _disc_bwd(lr, li, ldt, br_t, bi_t, dar, dai, dbbr, dbbi):
    def body(lr_ref, li_ref, ldt_ref, br_ref, bi_ref, dar_ref, dai_ref, dbbr_ref, dbbi_ref,
             dlr_ref, dli_ref, dldt_ref, dbr_ref, dbi_ref):
        prim = (lr_ref[...], li_ref[...], ldt_ref[...], br_ref[...], bi_ref[...])
        _, pullback = jax.vjp(_disc_math, *prim)
        dlr, dli, dldt, dbr, dbi = pullback((dar_ref[...], dai_ref[...], dbbr_ref[...], dbbi_ref[...]))
        dlr_ref[...] = dlr
        dli_ref[...] = dli
        dldt_ref[...] = dldt
        dbr_ref[...] = dbr
        dbi_ref[...] = dbi

    shapes = [jax.ShapeDtypeStruct(a.shape, F32) for a in (lr, li, ldt, br_t, bi_t)]
    return pl.pallas_call(body, name="ssm_disc_bwd", out_shape=shapes,
                          compiler_params=pltpu.CompilerParams(vmem_limit_bytes=VMEM_LIMIT))(
        lr, li, ldt, br_t, bi_t, dar, dai, dbbr, dbbi)


def _pad_pairs(m_re, m_im):
    def blocks(m):
        v = m.transpose(0, 2, 1).reshape(N_PAIRS, 2, SSM_GROUP, SSM_STATE)
        return jnp.einsum("ab,jahp->jahbp", jnp.eye(2, dtype=m.dtype), v).reshape(N_PAIRS, 32, 128)
    both = jnp.concatenate([blocks(m_re), blocks(m_im)], axis=-1)
    place = jax.nn.one_hot(jnp.arange(N_PAIRS) % 4, 4, dtype=both.dtype)
    return jnp.einsum("jk,jrc->jkrc", place, both).reshape(N_PAIRS, 128, 256)


def _unpad_pairs(x):
    place = jax.nn.one_hot(jnp.arange(N_PAIRS) % 4, 4, dtype=x.dtype)
    both = jnp.einsum("jk,jkrc->jrc", place, x.reshape(N_PAIRS, 4, 32, 256))

    def unblock(v):
        v = v.reshape(N_PAIRS, 2, SSM_GROUP, 2, SSM_STATE)
        d = jnp.einsum("ab,jahbp->jahp", jnp.eye(2, dtype=x.dtype), v)
        return d.reshape(N_SSM_GROUPS, SSM_GROUP, SSM_STATE).transpose(0, 2, 1)
    return unblock(both[..., :128]), unblock(both[..., 128:])


def _adamw_math(w, g, m, v):
    m = ADAM_B1 * m + (1.0 - ADAM_B1) * g
    v = ADAM_B2 * v + (1.0 - ADAM_B2) * (g * g)
    m_hat = m / (1.0 - ADAM_B1 ** ADAM_STEP)
    v_hat = v / (1.0 - ADAM_B2 ** ADAM_STEP)
    delta = -ADAM_LR * (m_hat / (jnp.sqrt(v_hat) + ADAM_EPS) + ADAM_WD * w)
    return delta, m, v


def _adamw(name, w, m, v, gbuf, g_block, g_row0, row_tile, glu=False):
    nl, r, c = w.shape
    n_tiles = r // row_tile
    g_rows, g_cols = g_block
    g_tile = g_rows // n_tiles
    g_off = g_row0 // g_tile

    def body(w_ref, m_ref, v_ref, g_ref, go_ref, d_ref, mo_ref, vo_ref):
        g = g_ref[...]
        if glu:
            g = jnp.concatenate([g[:, :D_SSM], g[:, D_SSM:]], axis=0)
        delta, mn, vn = _adamw_math(w_ref[...], g, m_ref[...], v_ref[...])
        go_ref[...] = g
        d_ref[...] = delta
        mo_ref[...] = mn
        vo_ref[...] = vn

    w_spec = pl.BlockSpec((None, row_tile, c), lambda l, j: (l, j, 0))
    shape = jax.ShapeDtypeStruct(w.shape, F32)
    return pl.pallas_call(
        body, name=name, grid=(nl, n_tiles),
        in_specs=[w_spec, w_spec, w_spec, pl.BlockSpec((None, g_tile, g_cols), lambda l, j: (l, g_off + j, 0))],
        out_specs=[w_spec] * 4,
        out_shape=[shape] * 4,
        compiler_params=_cparams(2),
    )(w, m, v, gbuf)


def _pack_weights(ids, layer, w_in, w_glu, w_out, w_down, w_gate_t, w_up_t):
    gb, gi = P_GLU_BLK
    ib, ii = P_IN_BLK
    ob, oi = P_OUT_BLK

    def body(ids_ref, in_ref, glu_ref, out_ref, dn_ref, gate_ref, up_ref, p_ref):
        p_ref[0:FF_SHARD, :] = dn_ref[...].astype(BF16)
        p_ref[FF_SHARD:2 * FF_SHARD, :] = gate_ref[...].astype(BF16)
        p_ref[2 * FF_SHARD:P_FF_ROWS, :] = up_ref[...].astype(BF16)
        g = glu_ref[...]
        p_ref[gb * gi:gb * (gi + 1), :] = jnp.concatenate([g[:gb, :], g[gb:, :]], axis=1).astype(BF16)
        p_ref[gb * (gi + 1):ib * ii, :] = jnp.zeros((P_GLU_PAD - gb, D_MODEL), BF16)
        p_ref[ib * ii:ib * (ii + 1), :] = in_ref[...].astype(BF16)
        p_ref[ob * oi:ob * (oi + 1), :] = out_ref[...].astype(BF16)

    def spec(a):
        return pl.BlockSpec((None,) + a.shape[1:], lambda i, ids_ref: (layer, 0, 0))

    ins = (w_in, w_glu, w_out, w_down, w_gate_t, w_up_t)
    grid_spec = pltpu.PrefetchScalarGridSpec(
        num_scalar_prefetch=1, grid=(1,),
        in_specs=[spec(a) for a in ins],
        out_specs=pl.BlockSpec((None, None, P_ROWS, D_MODEL), lambda i, ids_ref: (ids_ref[1], 0, 0, 0)))
    return pl.pallas_call(
        body, name="pack_weights", grid_spec=grid_spec,
        out_shape=jax.ShapeDtypeStruct((N_SHARD, 1, P_ROWS, D_MODEL), BF16),
        compiler_params=_cparams(1),
    )(ids, *ins)


MESH = pl.DeviceIdType.MESH
_ANY = pl.BlockSpec(memory_space=pl.ANY)
P_HALF = P_ROWS // 2
RS_ROW_TILE = 352


def _mesh_pos():
    return lax.axis_index("x"), lax.axis_index("y"), lax.axis_index("c")


def _other_chips(x, y):
    return [(1 - x, y), (x, 1 - y), (1 - x, 1 - y)]


def _remote(src, dst, send_sems, recv_sems, n, to):
    return pltpu.make_async_remote_copy(src_ref=src, dst_ref=dst, send_sem=send_sems.at[n],
                                        recv_sem=recv_sems.at[n], device_id=to, device_id_type=MESH)


def _all_gather_weights(wp):
    def body(w_in, o, send_sems, recv_sems):
        x, y, c = _mesh_pos()
        k = 2 * x + y
        sib = (x, y, 1 - c)
        chips = _other_chips(x, y)

        def piece(shard, half):
            return o.at[shard, :, pl.ds(half * P_HALF, P_HALF), :]

        sends = []
        for j, (px, py) in enumerate(chips):
            cp = _remote(piece(k, c), piece(k, c), send_sems, recv_sems, j, (px, py, c))
            cp.start()
            sends.append(cp)
        for j, (px, py) in enumerate(chips):
            landed = piece(2 * px + py, c)
            _remote(landed, landed, send_sems, recv_sems, j, (px, py, c)).wait_recv()
            cp = _remote(landed, landed, send_sems, recv_sems, 3 + j, sib)
            cp.start()
            sends.append(cp)
        for j, (px, py) in enumerate(chips):
            passed = piece(2 * px + py, 1 - c)
            _remote(passed, passed, send_sems, recv_sems, 3 + j, sib).wait_recv()
        for cp in sends:
            cp.wait_send()

    return pl.pallas_call(
        body, name="all_gather_weights",
        in_specs=[_ANY], out_specs=_ANY,
        out_shape=jax.ShapeDtypeStruct(wp.shape, BF16),
        scratch_shapes=[pltpu.SemaphoreType.DMA((6,)), pltpu.SemaphoreType.DMA((6,))],
        input_output_aliases={0: 0},
    )(wp)


_HBM = pl.BlockSpec(memory_space=pltpu.HBM)
_SEM = pl.BlockSpec(memory_space=pltpu.SEMAPHORE)
_EFFECT = pltpu.CompilerParams(has_side_effects=pltpu.SideEffectType.DATAFLOW_SIDE_EFFECTING)
_TOKEN = jax.ShapeDtypeStruct((8, 128), F32)


def _in_hbm(a):
    return pltpu.with_memory_space_constraint(a, pltpu.HBM)


def _ag_start(name, wp):
    def body(w_ref, send_sems, recv_sems, w_thru, token):
        x, y, c = _mesh_pos()
        mine = w_ref.at[2 * x + y, :, pl.ds(c * P_HALF, P_HALF), :]
        for j, (px, py) in enumerate(_other_chips(x, y)):
            _remote(mine, mine, send_sems, recv_sems, j, (px, py, c)).start()
        token[...] = jnp.zeros_like(token)

    return pl.pallas_call(
        body, name=name,
        out_shape=(pltpu.SemaphoreType.DMA((3,)), pltpu.SemaphoreType.DMA((3,)), pltpu.HBM(wp.shape, wp.dtype), _TOKEN),
        in_specs=(_HBM,), out_specs=(_SEM, _SEM, _HBM, pl.BlockSpec(memory_space=pltpu.VMEM)),
        input_output_aliases={0: 2}, compiler_params=_EFFECT,
    )(_in_hbm(wp))


def _ag_wait(name, send_sems, recv_sems, wp, after):
    def body(w_ref, send_sems, recv_sems, after_ref, w_out):
        x, y, c = _mesh_pos()
        mine = w_ref.at[2 * x + y, :, pl.ds(c * P_HALF, P_HALF), :]
        for j, (px, py) in enumerate(_other_chips(x, y)):
            landed = w_ref.at[2 * px + py, :, pl.ds(c * P_HALF, P_HALF), :]
            cp = _remote(mine, landed, send_sems, recv_sems, j, (px, py, c))
            cp.wait_send()
            cp.wait_recv()

    return pl.pallas_call(
        body, name=name, out_shape=pltpu.HBM(wp.shape, wp.dtype),
        in_specs=(_HBM, _SEM, _SEM, _ANY), out_specs=_HBM,
        input_output_aliases={0: 0}, compiler_params=_EFFECT,
    )(wp, send_sems, recv_sems, after)


def _ag_forward(wp):
    def body(w_in, o, send_sems, recv_sems):
        x, y, c = _mesh_pos()
        sib = (x, y, 1 - c)
        chips = _other_chips(x, y)
        sends = []
        for j, (px, py) in enumerate(chips):
            landed = o.at[2 * px + py, :, pl.ds(c * P_HALF, P_HALF), :]
            cp = _remote(landed, landed, send_sems, recv_sems, j, sib)
            cp.start()
            sends.append(cp)
        for j, (px, py) in enumerate(chips):
            passed = o.at[2 * px + py, :, pl.ds((1 - c) * P_HALF, P_HALF), :]
            _remote(passed, passed, send_sems, recv_sems, j, sib).wait_recv()
        for cp in sends:
            cp.wait_send()

    return pl.pallas_call(
        body, name="ag_forward",
        in_specs=[_ANY], out_specs=_ANY,
        out_shape=jax.ShapeDtypeStruct(wp.shape, wp.dtype),
        scratch_shapes=[pltpu.SemaphoreType.DMA((3,)), pltpu.SemaphoreType.DMA((3,))],
        input_output_aliases={0: 0},
    )(wp)


def _rs_chips_start(name, t):
    nl = t.shape[0]

    def body(t_ref, land_ref, send_sems, recv_sems, t_thru, land_thru, token):
        x, y, c = _mesh_pos()
        for j, (px, py) in enumerate(_other_chips(x, y)):
            _remote(t_ref.at[:, 2 * px + py], land_ref.at[j], send_sems, recv_sems, j, (px, py, c)).start()
        token[...] = jnp.zeros_like(token)

    land = lax.empty((3, nl, P_HALF, D_MODEL), BF16)
    return pl.pallas_call(
        body, name=name,
        out_shape=(pltpu.SemaphoreType.DMA((3,)), pltpu.SemaphoreType.DMA((3,)), pltpu.HBM(t.shape, t.dtype),
                   pltpu.HBM(land.shape, land.dtype), _TOKEN),
        in_specs=(_HBM, _HBM), out_specs=(_SEM, _SEM, _HBM, _HBM, pl.BlockSpec(memory_space=pltpu.VMEM)),
        input_output_aliases={0: 2, 1: 3}, compiler_params=_EFFECT,
    )(_in_hbm(t), _in_hbm(land))


def _rs_chips_wait(name, send_sems, recv_sems, t, land, after):
    def body(t_ref, land_ref, send_sems, recv_sems, after_ref, t_dead, land_out):
        x, y, c = _mesh_pos()
        for j, (px, py) in enumerate(_other_chips(x, y)):
            cp = _remote(t_ref.at[:, 2 * px + py], land_ref.at[j], send_sems, recv_sems, j, (px, py, c))
            cp.wait_send()
            cp.wait_recv()

    return pl.pallas_call(
        body, name=name, out_shape=(pltpu.HBM(t.shape, t.dtype), pltpu.HBM(land.shape, land.dtype)),
        in_specs=(_HBM, _HBM, _SEM, _SEM, _ANY), out_specs=(_HBM, _HBM),
        input_output_aliases={0: 0, 1: 1}, compiler_params=_EFFECT,
    )(t, land, send_sems, recv_sems, after)[1]


def _rs_to_sibling(g):
    def body(g_ref, b, send_sems, recv_sems):
        x, y, c = _mesh_pos()
        cp = _remote(g_ref.at[:, :, pl.ds((1 - c) * P_HALF, P_HALF), :], b, send_sems, recv_sems, 0, (x, y, 1 - c))
        cp.start()
        cp.wait()

    nl = g.shape[0]
    return pl.pallas_call(
        body, name="rs_to_sibling",
        in_specs=[_ANY], out_specs=_ANY,
        out_shape=jax.ShapeDtypeStruct((nl, N_SHARD, P_HALF, D_MODEL), F32),
        scratch_shapes=[pltpu.SemaphoreType.DMA((1,)), pltpu.SemaphoreType.DMA((1,))],
    )(g)


def _rs_add(name, ids, g, buf, row_tile):
    nl, _, hr, cols = buf.shape
    n_rt = hr // row_tile

    def body(ids_ref, g_ref, b_ref, own_ref, tb_ref):
        t = g_ref[...] + b_ref[...]
        tb_ref[...] = t.astype(BF16)

        @pl.when(pl.program_id(2) == ids_ref[1])
        def _():
            own_ref[...] = t

    blk = (None, None, row_tile, cols)
    grid_spec = pltpu.PrefetchScalarGridSpec(
        num_scalar_prefetch=1, grid=(nl, n_rt, N_SHARD),
        in_specs=[pl.BlockSpec(blk, lambda l, j, s, ids_ref: (l, s, ids_ref[0] * n_rt + j, 0)),
                  pl.BlockSpec(blk, lambda l, j, s, ids_ref: (l, s, j, 0))],
        out_specs=[pl.BlockSpec((None, row_tile, cols), lambda l, j, s, ids_ref: (l, j, 0)),
                   pl.BlockSpec(blk, lambda l, j, s, ids_ref: (l, s, j, 0))])
    return pl.pallas_call(
        body, name=name, grid_spec=grid_spec,
        out_shape=[jax.ShapeDtypeStruct((nl, hr, cols), F32), jax.ShapeDtypeStruct(buf.shape, BF16)],
        compiler_params=_cparams(3),
    )(ids, g, buf)


def _rs_sum(ids, own, bufb, row_tile):
    nl, hr, cols = own.shape
    n_rt = hr // row_tile

    def body(ids_ref, own_ref, b_ref, f_ref):
        f_ref[...] = ((own_ref[...] + b_ref[0].astype(F32)) + b_ref[1].astype(F32)) + b_ref[2].astype(F32)

    grid_spec = pltpu.PrefetchScalarGridSpec(
        num_scalar_prefetch=1, grid=(nl, n_rt),
        in_specs=[pl.BlockSpec((None, row_tile, cols), lambda l, j, ids_ref: (l, j, 0)),
                  pl.BlockSpec((3, None, row_tile, cols), lambda l, j, ids_ref: (0, l, j, 0))],
        out_specs=pl.BlockSpec((None, row_tile, cols), lambda l, j, ids_ref: (l, ids_ref[0] * n_rt + j, 0)))
    return pl.pallas_call(
        body, name="rs_sum", grid_spec=grid_spec,
        out_shape=jax.ShapeDtypeStruct((nl, 2 * hr, cols), F32),
        compiler_params=_cparams(2),
    )(ids, own, bufb)


def _rs_exchange(f):
    def body(f_in, o, send_sems, recv_sems):
        x, y, c = _mesh_pos()
        mine = o.at[:, pl.ds(c * P_HALF, P_HALF), :]
        cp = _remote(mine, mine, send_sems, recv_sems, 0, (x, y, 1 - c))
        cp.start()
        cp.wait_send()
        theirs = o.at[:, pl.ds((1 - c) * P_HALF, P_HALF), :]
        _remote(theirs, theirs, send_sems, recv_sems, 0, (x, y, 1 - c)).wait_recv()

    return pl.pallas_call(
        body, name="rs_exchange",
        in_specs=[_ANY], out_specs=_ANY,
        out_shape=jax.ShapeDtypeStruct(f.shape, F32),
        scratch_shapes=[pltpu.SemaphoreType.DMA((1,)), pltpu.SemaphoreType.DMA((1,))],
        input_output_aliases={0: 0},
    )(f)


def _small_all_reduce(s):
    n_rows = s.shape[0]
    hr = n_rows // 2

    def body(s_ref, o_ref, sibbuf, tbuf, cbuf, fbuf, send_sems, recv_sems):
        x, y, c = _mesh_pos()
        sib = (x, y, 1 - c)
        mine = pl.ds(pl.multiple_of(c * hr, SUBLANES), hr)
        theirs = pl.ds(pl.multiple_of((1 - c) * hr, SUBLANES), hr)
        first = _remote(s_ref.at[theirs], sibbuf, send_sems, recv_sems, 0, sib)
        first.start()
        first.wait()
        tbuf[...] = s_ref[mine, :] + sibbuf[...]
        cps = []
        for j, (px, py) in enumerate(_other_chips(x, y)):
            cp = _remote(tbuf, cbuf.at[j], send_sems, recv_sems, 1 + j, (px, py, c))
            cp.start()
            cps.append(cp)
        for cp in cps:
            cp.wait()
        f = (tbuf[...] + cbuf[1]) + (cbuf[0] + cbuf[2])
        fbuf[...] = f
        o_ref[mine, :] = f
        last = _remote(fbuf, o_ref.at[mine], send_sems, recv_sems, 4, sib)
        last.start()
        last.wait()

    vmem = pl.BlockSpec(memory_space=pltpu.VMEM)
    return pl.pallas_call(
        body, name="small_all_reduce",
        in_specs=[vmem], out_specs=vmem,
        out_shape=jax.ShapeDtypeStruct(s.shape, F32),
        scratch_shapes=[pltpu.VMEM((hr, D_MODEL), F32), pltpu.VMEM((hr, D_MODEL), F32),
                        pltpu.VMEM((3, hr, D_MODEL), F32), pltpu.VMEM((hr, D_MODEL), F32),
                        pltpu.SemaphoreType.DMA((5,)), pltpu.SemaphoreType.DMA((5,))],
        compiler_params=pltpu.CompilerParams(vmem_limit_bytes=VMEM_LIMIT),
    )(s)


_SMALL = ("norm_mix", "w_pool", "pool_scale", "lam_re", "lam_im", "log_dt", "b_re", "b_im", "c_re", "c_im",
          "d_skip", "b_glu", "norm_ffn", "norm_final")
_WEIGHTS = ("norm_mix", "w_in", "w_pool", "pool_scale", "lam_re", "lam_im", "log_dt", "b_re", "b_im", "c_re",
            "c_im", "d_skip", "w_glu", "b_glu", "w_out", "norm_ffn", "w_gate", "w_up", "w_down", "norm_final")


def _local_step(x, target, p, get_weights, put_grads):
    nl = p["norm_mix"].shape[0]

    def tied(a, token):
        return a if token is None else a + token
    n_rows = nl * N_SSM_GROUPS
    lr = p["lam_re"].reshape(n_rows, 1, SSM_STATE)
    li = p["lam_im"].reshape(n_rows, 1, SSM_STATE)
    ldt = p["log_dt"].reshape(n_rows, 1, 1)
    br_t = p["b_re"].reshape(n_rows, SSM_STATE, SSM_GROUP).transpose(0, 2, 1)
    bi_t = p["b_im"].reshape(n_rows, SSM_STATE, SSM_GROUP).transpose(0, 2, 1)
    ar, ai, bbr_t, bbi_t = _disc_fwd(lr, li, ldt, br_t, bi_t)
    ar = ar.reshape(nl, 1, N_STATE)
    ai = ai.reshape(nl, 1, N_STATE)
    bbr = bbr_t.transpose(0, 2, 1).reshape(nl, N_SSM_GROUPS, SSM_STATE, SSM_GROUP)
    bbi = bbi_t.transpose(0, 2, 1).reshape(nl, N_SSM_GROUPS, SSM_STATE, SSM_GROUP)
    w_pool = p["w_pool"].astype(BF16)

    layers = []
    h = x
    for l in range(nl):
        bpad = _pad_pairs(bbr[l], bbi[l]).astype(BF16)
        cpad_t = _pad_pairs(p["c_re"][l].transpose(0, 2, 1), -p["c_im"][l].transpose(0, 2, 1)).astype(BF16)
        dskip = p["d_skip"][l].reshape(1, D_SSM)
        wp, token = get_weights(l, h)
        u, ypool = _mix_in_fwd(h, tied(p["norm_mix"][l:l + 1], token), wp, 0, w_pool[l], p["pool_scale"][l:l + 1])
        sre, sim, yraw = _ssm_fwd(u, bpad, cpad_t.transpose(0, 2, 1), ar[l], ai[l], dskip)
        hm = _mix_out_fwd(yraw, ypool, h, wp, 0, p["b_glu"][l:l + 1])
        h_next, n2, gate_s, up_s = _ffn_fwd(hm, p["norm_ffn"][l:l + 1], wp, 0)
        layers.append(dict(h=h, u=u, ypool=ypool, sre=sre, sim=sim, yraw=yraw, hm=hm, n2=n2, gate_s=gate_s, wp=wp,
                           up_s=up_s, bpad_t=bpad.transpose(0, 2, 1), cpad_t=cpad_t, dskip=dskip))
        h = h_next

    dh, loss, d_norm_final = _final_fwd_bwd(h, p["norm_final"].reshape(1, D_MODEL), target)

    per_layer = {n: [None] * nl for n in ("norm_mix", "w_pool", "pool_scale", "c_re", "c_im", "d_skip", "b_glu",
                                          "norm_ffn", "dar", "dai", "dbbr_t", "dbbi_t")}
    token = None
    for l in reversed(range(nl)):
        s = layers[l]
        wp = s["wp"]
        g1 = lax.empty((1, N_SHARD, P_ROWS, D_MODEL), F32)
        dhm, dg2, dgate_s, dup_s, act_s, dhb = _ffn_bwd_act(dh, s["hm"], tied(p["norm_ffn"][l:l + 1], token),
                                                             s["gate_s"], s["up_s"], wp, 0)
        g1 = _ffn_bwd_w(s["n2"], dgate_s, dup_s, act_s, dhb, g1, 0)
        dyraw, dyp, db_glu, g1 = _mix_out_bwd(dhm, s["yraw"], s["ypool"], wp, 0, p["b_glu"][l:l + 1], g1)
        dus, dcp, dbp, dar, dai, ddsk = _ssm_bwd(dyraw, s["u"], s["sre"], s["sim"], s["cpad_t"], s["bpad_t"],
                                                  ar[l], ai[l], s["dskip"])
        dup, dwp, dsc = _pool_bwd(dyp, s["u"], w_pool[l], p["pool_scale"][l:l + 1])
        dh, dg1, g1 = _mix_in_bwd(dup, dus, s["h"], dhm, p["norm_mix"][l:l + 1], wp, 0, g1)
        token = put_grads(l, g1)
        dc_re, dc_im = _unpad_pairs(dcp.transpose(0, 2, 1))
        dbbr, dbbi = _unpad_pairs(dbp)
        per_layer["norm_mix"][l] = dg1[0]
        per_layer["w_pool"][l] = dwp
        per_layer["pool_scale"][l] = dsc[0]
        per_layer["c_re"][l] = dc_re.transpose(0, 2, 1)
        per_layer["c_im"][l] = -dc_im.transpose(0, 2, 1)
        per_layer["d_skip"][l] = ddsk.reshape(N_SSM_GROUPS, SSM_GROUP)
        per_layer["b_glu"][l] = db_glu[0]
        per_layer["norm_ffn"][l] = dg2[0]
        per_layer["dar"][l] = dar.reshape(N_SSM_GROUPS, 1, SSM_STATE)
        per_layer["dai"][l] = dai.reshape(N_SSM_GROUPS, 1, SSM_STATE)
        per_layer["dbbr_t"][l] = dbbr.transpose(0, 2, 1)
        per_layer["dbbi_t"][l] = dbbi.transpose(0, 2, 1)

    st = {n: jnp.stack(v) for n, v in per_layer.items()}
    cat = lambda a: a.reshape((n_rows,) + a.shape[2:])
    dlr, dli, dldt, dbr_t, dbi_t = _disc_bwd(lr, li, ldt, br_t, bi_t, cat(st["dar"]), cat(st["dai"]),
                                              cat(st["dbbr_t"]), cat(st["dbbi_t"]))
    small = {n: st[n] for n in ("norm_mix", "w_pool", "pool_scale", "c_re", "c_im", "d_skip", "b_glu", "norm_ffn")}
    small["lam_re"] = dlr.reshape(nl, N_SSM_GROUPS, SSM_STATE)
    small["lam_im"] = dli.reshape(nl, N_SSM_GROUPS, SSM_STATE)
    small["log_dt"] = dldt.reshape(nl, N_SSM_GROUPS)
    small["b_re"] = dbr_t.transpose(0, 2, 1).reshape(nl, N_SSM_GROUPS, SSM_STATE, SSM_GROUP)
    small["b_im"] = dbi_t.transpose(0, 2, 1).reshape(nl, N_SSM_GROUPS, SSM_STATE, SSM_GROUP)
    small["norm_final"] = d_norm_final[0]
    return loss, dh, small


def _flatten_small(d):
    flat = jnp.concatenate([d[n].reshape(-1) for n in _SMALL])
    n_rows = -(-flat.shape[0] // (32 * D_MODEL)) * 32
    return jnp.pad(flat, (0, n_rows * D_MODEL - flat.shape[0])).reshape(n_rows, D_MODEL)


def _split_small(flat, like):
    flat = flat.reshape(-1)
    out, at = {}, 0
    for n in _SMALL:
        size = like[n].size
        out[n] = flat[at:at + size].reshape(like[n].shape)
        at += size
    return out


def kernel(x, norm_mix, w_in, w_pool, pool_scale, lam_re, lam_im, log_dt, b_re, b_im, c_re, c_im, d_skip, w_glu, b_glu, w_out, norm_ffn, w_gate, w_up, w_down, norm_final, loss_target, m_norm_mix, m_w_in, m_w_pool, m_pool_scale, m_lam_re, m_lam_im, m_log_dt, m_b_re, m_b_im, m_c_re, m_c_im, m_d_skip, m_w_glu, m_b_glu, m_w_out, m_norm_ffn, m_w_gate, m_w_up, m_w_down, m_norm_final, v_norm_mix, v_w_in, v_w_pool, v_pool_scale, v_lam_re, v_lam_im, v_log_dt, v_b_re, v_b_im, v_c_re, v_c_im, v_d_skip, v_w_glu, v_b_glu, v_w_out, v_norm_ffn, v_w_gate, v_w_up, v_w_down, v_norm_final):
    given = dict(locals())
    w = {n: given[n] for n in _WEIGHTS}
    m = {n: given["m_" + n] for n in _WEIGHTS}
    v = {n: given["v_" + n] for n in _WEIGHTS}
    ids = jnp.stack([lax.axis_index("c"), 2 * lax.axis_index("x") + lax.axis_index("y")]).astype(jnp.int32)

    t_names = ("w_gate", "w_up")
    tr = lambda a: a.transpose(0, 2, 1)
    for d in (w, m, v):
        d.update({n: tr(d[n]) for n in t_names})

    nl = norm_mix.shape[0]
    packed = [_pack_weights(ids, l, w["w_in"], w["w_glu"], w["w_out"], w["w_down"], w["w_gate"], w["w_up"])
              for l in range(nl)]
    first = _all_gather_weights(packed[0])
    started = {l: _ag_start(f"ag_start_{l}", packed[l]) for l in range(1, nl)}
    first_token = sum(s[3][:1, :1] for s in started.values()) if started else None

    def get_weights(l, after):
        if l == 0:
            return first, first_token
        send_sems, recv_sems, buf, _ = started[l]
        return _ag_forward(_ag_wait(f"ag_wait_{l}", send_sems, recv_sems, buf, after)), None

    in_flight, reduced = {}, [None] * nl

    def finish(l, after):
        send_sems, recv_sems, t, land, own = in_flight.pop(l)
        land = _rs_chips_wait(f"rs_chips_wait_{l}", send_sems, recv_sems, t, land, after)
        reduced[l] = _rs_exchange(_rs_sum(ids, own, land, RS_ROW_TILE))

    def put_grads(l, g):
        own, t = _rs_add("rs_add", ids, g, _rs_to_sibling(g), RS_ROW_TILE)
        send_sems, recv_sems, t, land, token = _rs_chips_start(f"rs_chips_start_{l}", t)
        in_flight[l] = (send_sems, recv_sems, t, land, own)
        if l + 1 in in_flight:
            finish(l + 1, token)
        return token[:1, :1]

    loss, grad_x, small = _local_step(x[0], loss_target[0], {n: w[n] for n in _SMALL}, get_weights, put_grads)
    loss = lax.psum(loss[0, 0], ("x", "y", "c"))
    small_sum = _small_all_reduce(_flatten_small(small))
    finish(0, small_sum)
    gr = jnp.concatenate(reduced, axis=0)

    res = {}
    big = (("w_in", P_IN_BLK, 256, False), ("w_out", P_OUT_BLK, 256, False), ("w_down", P_WD_BLK, 352, False),
           ("w_gate", P_WG_BLK, 352, False), ("w_up", P_WU_BLK, 352, False), ("w_glu", P_GLU_BLK, 128, True))
    for n, (blk, idx), row_tile, glu in big:
        res[n] = _adamw("adamw_" + n, w[n], m[n], v[n], gr, (blk, D_MODEL), blk * idx, row_tile, glu)
    for n in t_names:
        res[n] = tuple(tr(a) for a in res[n])
    flat = [_flatten_small(d)[None] for d in (w, m, v)]
    n_rows = flat[0].shape[1]
    outs = _adamw("adamw_small", *flat, small_sum[None], (n_rows, D_MODEL), 0, n_rows // 4)
    parts = [_split_small(o[0], w) for o in outs]
    for n in _SMALL:
        res[n] = tuple(part[n] for part in parts)

    return (loss, grad_x[None], *[res[n][0] for n in _WEIGHTS], *[res[n][1] for n in _WEIGHTS],
            *[res[n][2] for n in _WEIGHTS], *[res[n][3] for n in _WEIGHTS])
```

```python
import functools
import math

import jax
import jax.numpy as jnp
from jax import lax
from jax.experimental import pallas as pl
from jax.experimental.pallas import tpu as pltpu

F32 = jnp.float32
BF16 = jnp.bfloat16

D_MODEL = 1024
D_POOL = 512
D_SSM = 512
POOL_WINDOWS = (2, 4, 8, 16)
POOL_GROUP = 128
POOL_HALO = 16
N_SSM_GROUPS = 32
SSM_GROUP = 16
SSM_STATE = 64
N_STATE = N_SSM_GROUPS * SSM_STATE
N_PAIRS = N_SSM_GROUPS // 2
D_FF = 2816
N_SHARD = 4
FF_SHARD = D_FF // N_SHARD
RMS_EPS = 1e-6

ADAM_LR = 0.001
ADAM_B1 = 0.9
ADAM_B2 = 0.999
ADAM_EPS = 1e-08
ADAM_WD = 0.01
ADAM_STEP = 10

P_ROWS = 2816
P_WD_BLK = (704, 0)
P_WG_BLK = (704, 1)
P_WU_BLK = (704, 2)
P_FF_ROWS = 2112
P_GLU_BLK = (64, 33)
P_GLU_PAD = 192
P_IN_BLK = (256, 9)
P_OUT_BLK = (256, 10)

SUBLANES = 8
VMEM_LIMIT = 56 * 1024 * 1024

TM = 512
TM_FFN = 512
TS = 256
SCAN_LANES = 512


def _cparams(n_axes):
    return pltpu.CompilerParams(dimension_semantics=("arbitrary",) * n_axes, vmem_limit_bytes=VMEM_LIMIT)


def _dot(a, b):
    return jnp.dot(a, b, preferred_element_type=F32)


def _dot_nt(a, b):
    return lax.dot_general(a, b, (((1,), (1,)), ((), ())), preferred_element_type=F32)


def _dot_tn(a, b):
    return lax.dot_general(a, b, (((0,), (0,)), ((), ())), preferred_element_type=F32)


def _rms_hat(x):
    r = lax.rsqrt(jnp.mean(x * x, axis=-1, keepdims=True) + RMS_EPS)
    return x * r, r


def _rms_bwd(d_hat, xhat, r):
    return r * (d_hat - xhat * jnp.mean(d_hat * xhat, axis=-1, keepdims=True))


def _sigmoid(x):
    return 1.0 / (1.0 + jnp.exp(-x))


_GELU_C = math.sqrt(2.0 / math.pi)
_GELU_K = 0.044715


def _gelu(x):
    return 0.5 * x * (1.0 + jnp.tanh(_GELU_C * (x + _GELU_K * x * x * x)))


def _gelu_grad(x):
    th = jnp.tanh(_GELU_C * (x + _GELU_K * x * x * x))
    return 0.5 * (1.0 + th) + 0.5 * x * (1.0 - th * th) * _GELU_C * (1.0 + 3.0 * _GELU_K * x * x)


def _glu_weight(ref):
    v = ref[...]
    return jnp.concatenate([v[:, :, :D_SSM], v[:, :, D_SSM:]], axis=1).reshape(D_SSM, D_SSM)


def _glu_pack(w):
    v = w.reshape(N_SHARD, 128, D_SSM)
    return jnp.concatenate([v[:, :64, :], v[:, 64:, :]], axis=2)


def _pool_diff(ext, row0, tm):
    rows = row0 + lax.broadcasted_iota(jnp.int32, (tm, 1), 0)
    outs = []
    for gi, w in enumerate(POOL_WINDOWS):
        e = ext[:, gi * POOL_GROUP:(gi + 1) * POOL_GROUP]
        s = e
        k = 1
        while k < w:
            s = s + pltpu.roll(s, k, 0)
            k *= 2
        inv = 1.0 / jnp.minimum(rows + 1, w).astype(F32)
        outs.append(s[POOL_HALO:, :] * inv - e[POOL_HALO:, :])
    return outs


def _mix_in_fwd(h, g1, wp, layer, w_pool, scale):
    L = h.shape[0]
    tm = min(TM, L)

    def body(h_ref, g_ref, w_ref, wp_ref, sc_ref, u_ref, yp_ref, carry):
        i = pl.program_id(0)

        @pl.when(i == 0)
        def _():
            carry[...] = jnp.zeros_like(carry)

        xhat, _ = _rms_hat(h_ref[...])
        n1 = (xhat * g_ref[...]).astype(BF16)
        u = _dot(n1, w_ref[...].reshape(D_MODEL, D_MODEL))
        u_ref[...] = u
        up = u[:, :D_POOL]
        ext = jnp.concatenate([carry[...], up], axis=0)
        carry[...] = up[tm - POOL_HALO:, :]
        diffs = _pool_diff(ext, i * tm, tm)
        for gi in range(4):
            cols = slice(gi * POOL_GROUP, (gi + 1) * POOL_GROUP)
            yp_ref[:, cols] = _dot(diffs[gi].astype(BF16), wp_ref[gi]) * sc_ref[:, cols]

    blk, idx = P_IN_BLK
    return pl.pallas_call(
        body, name="mix_in_fwd", grid=(L // tm,),
        in_specs=[pl.BlockSpec((tm, D_MODEL), lambda i: (i, 0)),
                  pl.BlockSpec((1, D_MODEL), lambda i: (0, 0)),
                  pl.BlockSpec((N_SHARD, None, blk, D_MODEL), lambda i: (0, layer, idx, 0)),
                  pl.BlockSpec((4, POOL_GROUP, POOL_GROUP), lambda i: (0, 0, 0)),
                  pl.BlockSpec((1, D_POOL), lambda i: (0, 0))],
        out_specs=[pl.BlockSpec((tm, D_MODEL), lambda i: (i, 0)),
                   pl.BlockSpec((tm, D_POOL), lambda i: (i, 0))],
        out_shape=[jax.ShapeDtypeStruct((L, D_MODEL), F32), jax.ShapeDtypeStruct((L, D_POOL), F32)],
        scratch_shapes=[pltpu.VMEM((POOL_HALO, D_POOL), F32)],
        compiler_params=_cparams(1),
    )(h, g1, wp, w_pool, scale)


def _cmul(xr, xi, yr, yi):
    return xr * yr - xi * yi, xr * yi + xi * yr


def _scan_tables(ar, ai, tab, reverse):
    c = ar.shape[1]
    row = lax.broadcasted_iota(jnp.int32, (SUBLANES, c), 0)
    a2r, a2i = _cmul(ar, ai, ar, ai)
    a4r, a4i = _cmul(a2r, a2i, a2r, a2i)
    zero = jnp.zeros((SUBLANES, c), F32)
    for n, (s, pr, pi) in enumerate(((1, ar, ai), (2, a2r, a2i), (4, a4r, a4i))):
        keep = (row < SUBLANES - s) if reverse else (row >= s)
        tab[2 * n] = jnp.where(keep, pr, zero)
        tab[2 * n + 1] = jnp.where(keep, pi, zero)
    cr, ci = ar, ai
    tr, ti = zero, zero
    for n in range(SUBLANES):
        at = (SUBLANES - 1 - n) if reverse else n
        tr = jnp.where(row == at, cr, tr)
        ti = jnp.where(row == at, ci, ti)
        cr, ci = _cmul(cr, ci, ar, ai)
    tab[6] = tr
    tab[7] = ti


def _ssm_fwd(u, bpad, cpad, ar, ai, dskip):
    L = u.shape[0]
    ts = min(TS, L)
    nq = 4
    cq = N_STATE // nq

    def body(u_ref, bp_ref, cp_ref, ar_ref, ai_ref, dsk_ref, sre_ref, sim_ref, y_ref, cr, ci, tab):
        t = pl.program_id(1)

        @pl.when(t == 0)
        def _():
            cr[...] = jnp.zeros_like(cr)
            ci[...] = jnp.zeros_like(ci)
            _scan_tables(ar_ref[...], ai_ref[...], tab, reverse=False)

        uf = u_ref[...]
        ub = uf.astype(BF16)
        for jj in range(4):
            bu = _dot(ub, bp_ref[jj])
            sre_ref[:, jj * 128:(jj + 1) * 128] = bu[:, :128]
            sim_ref[:, jj * 128:(jj + 1) * 128] = bu[:, 128:]

        for cc in range(cq // SCAN_LANES):
            cols = slice(cc * SCAN_LANES, (cc + 1) * SCAN_LANES)
            def step(i, carry, cols=cols):
                c_r, c_i = carry
                r0 = pl.multiple_of(i * SUBLANES, SUBLANES)
                xr = sre_ref[pl.ds(r0, SUBLANES), cols]
                xi = sim_ref[pl.ds(r0, SUBLANES), cols]
                for n, s in enumerate((1, 2, 4)):
                    tr, ti = tab[2 * n, :, cols], tab[2 * n + 1, :, cols]
                    rr = pltpu.roll(xr, s, 0)
                    ri = pltpu.roll(xi, s, 0)
                    xr, xi = xr + tr * rr - ti * ri, xi + tr * ri + ti * rr
                pr, pi = tab[6, :, cols], tab[7, :, cols]
                xr, xi = xr + pr * c_r - pi * c_i, xi + pr * c_i + pi * c_r
                sre_ref[pl.ds(r0, SUBLANES), cols] = xr
                sim_ref[pl.ds(r0, SUBLANES), cols] = xi
                shp = (SUBLANES, SCAN_LANES)
                return (jnp.broadcast_to(xr[SUBLANES - 1:, :], shp), jnp.broadcast_to(xi[SUBLANES - 1:, :], shp))

            c_r, c_i = lax.fori_loop(0, ts // SUBLANES, step, (cr[:, cols], ci[:, cols]), unroll=2)
            cr[:, cols] = c_r
            ci[:, cols] = c_i

        acc = dsk_ref[...] * uf
        for jj in range(4):
            cols = slice(jj * 128, (jj + 1) * 128)
            scat = jnp.concatenate([sre_ref[:, cols], sim_ref[:, cols]], axis=1).astype(BF16)
            acc = acc + _dot(scat, cp_ref[jj])
        y_ref[...] = acc

    return pl.pallas_call(
        body, name="ssm_fwd", grid=(nq, L // ts),
        in_specs=[pl.BlockSpec((ts, 128), lambda q, t: (t, 4 + q)),
                  pl.BlockSpec((4, 128, 256), lambda q, t: (q, 0, 0)),
                  pl.BlockSpec((4, 256, 128), lambda q, t: (q, 0, 0)),
                  pl.BlockSpec((1, cq), lambda q, t: (0, q)),
                  pl.BlockSpec((1, cq), lambda q, t: (0, q)),
                  pl.BlockSpec((1, 128), lambda q, t: (0, q))],
        out_specs=[pl.BlockSpec((ts, cq), lambda q, t: (t, q)),
                   pl.BlockSpec((ts, cq), lambda q, t: (t, q)),
                   pl.BlockSpec((ts, 128), lambda q, t: (t, q))],
        out_shape=[jax.ShapeDtypeStruct((L, N_STATE), F32), jax.ShapeDtypeStruct((L, N_STATE), F32),
                   jax.ShapeDtypeStruct((L, D_SSM), F32)],
        scratch_shapes=[pltpu.VMEM((SUBLANES, cq), F32), pltpu.VMEM((SUBLANES, cq), F32),
                        pltpu.VMEM((8, SUBLANES, cq), F32)],
        compiler_params=_cparams(2),
    )(u, bpad, cpad, ar, ai, dskip)


def _mix_out_fwd(yraw, ypool, h, wp, layer, b_glu):
    L = h.shape[0]
    tm = min(TM, L)

    def body(yr_ref, yp_ref, h_ref, wglu_ref, b_ref, wout_ref, o_ref):
        y = _gelu(yr_ref[...])
        z = _dot(y.astype(BF16), _glu_weight(wglu_ref)) + b_ref[...]
        o = y * _sigmoid(z)
        mix = jnp.concatenate([yp_ref[...], o], axis=1).astype(BF16)
        o_ref[...] = h_ref[...] + _dot(mix, wout_ref[...].reshape(D_MODEL, D_MODEL))

    gb, gi = P_GLU_BLK
    ob, oi = P_OUT_BLK
    return pl.pallas_call(
        body, name="mix_out_fwd", grid=(L // tm,),
        in_specs=[pl.BlockSpec((tm, D_SSM), lambda i: (i, 0)),
                  pl.BlockSpec((tm, D_POOL), lambda i: (i, 0)),
                  pl.BlockSpec((tm, D_MODEL), lambda i: (i, 0)),
                  pl.BlockSpec((N_SHARD, None, gb, D_MODEL), lambda i: (0, layer, gi, 0)),
                  pl.BlockSpec((1, D_SSM), lambda i: (0, 0)),
                  pl.BlockSpec((N_SHARD, None, ob, D_MODEL), lambda i: (0, layer, oi, 0))],
        out_specs=pl.BlockSpec((tm, D_MODEL), lambda i: (i, 0)),
        out_shape=jax.ShapeDtypeStruct((L, D_MODEL), F32),
        compiler_params=_cparams(1),
    )(yraw, ypool, h, wp, b_glu, wp)


def _ffn_weights(ref):
    return ref[0:FF_SHARD, :], ref[FF_SHARD:2 * FF_SHARD, :], ref[2 * FF_SHARD:P_FF_ROWS, :]


def _ffn_fwd(h, g2, wp, layer):
    L = h.shape[0]
    tm = min(TM_FFN, L)

    def body(h_ref, g_ref, w_ref, o_ref, n2_ref, gate_ref, up_ref):
        k = pl.program_id(1)

        @pl.when(k == 0)
        def _():
            x = h_ref[...]
            xhat, _ = _rms_hat(x)
            n2_ref[...] = (xhat * g_ref[...]).astype(BF16)
            o_ref[...] = x

        wd, wg_t, wu_t = _ffn_weights(w_ref)
        n2 = n2_ref[...]
        gate = _dot_nt(n2, wg_t)
        up = _dot_nt(n2, wu_t)
        gate_ref[...] = gate.astype(BF16)
        up_ref[...] = up.astype(BF16)
        act = (gate * _sigmoid(gate) * up).astype(BF16)
        o_ref[...] += _dot(act, wd)

    act_shape = jax.ShapeDtypeStruct((N_SHARD, L, FF_SHARD), BF16)
    return pl.pallas_call(
        body, name="ffn_fwd", grid=(L // tm, N_SHARD),
        in_specs=[pl.BlockSpec((tm, D_MODEL), lambda m, k: (m, 0)),
                  pl.BlockSpec((1, D_MODEL), lambda m, k: (0, 0)),
                  pl.BlockSpec((None, None, P_FF_ROWS, D_MODEL), lambda m, k: (k, layer, 0, 0))],
        out_specs=[pl.BlockSpec((tm, D_MODEL), lambda m, k: (m, 0)),
                   pl.BlockSpec((tm, D_MODEL), lambda m, k: (m, 0)),
                   pl.BlockSpec((None, tm, FF_SHARD), lambda m, k: (k, m, 0)),
                   pl.BlockSpec((None, tm, FF_SHARD), lambda m, k: (k, m, 0))],
        out_shape=[jax.ShapeDtypeStruct((L, D_MODEL), F32), jax.ShapeDtypeStruct((L, D_MODEL), BF16),
                   act_shape, act_shape],
        compiler_params=_cparams(2),
    )(h, g2, wp)


def _final_fwd_bwd(h, gf, target):
    L = h.shape[0]
    tm = min(TM, L)

    def body(h_ref, g_ref, t_ref, dh_ref, loss_ref, dg_ref):
        i = pl.program_id(0)

        @pl.when(i == 0)
        def _():
            loss_ref[...] = jnp.zeros_like(loss_ref)
            dg_ref[...] = jnp.zeros_like(dg_ref)

        xhat, r = _rms_hat(h_ref[...])
        g = g_ref[...]
        e = xhat * g - t_ref[...]
        loss_ref[...] += 0.5 * jnp.sum(jnp.mean(e * e, axis=-1, keepdims=True), axis=0, keepdims=True)
        dy = e * (1.0 / D_MODEL)
        dg_ref[...] += jnp.sum(dy * xhat, axis=0, keepdims=True)
        dh_ref[...] = _rms_bwd(dy * g, xhat, r)

    return pl.pallas_call(
        body, name="final_fwd_bwd", grid=(L // tm,),
        in_specs=[pl.BlockSpec((tm, D_MODEL), lambda i: (i, 0)),
                  pl.BlockSpec((1, D_MODEL), lambda i: (0, 0)),
                  pl.BlockSpec((tm, D_MODEL), lambda i: (i, 0))],
        out_specs=[pl.BlockSpec((tm, D_MODEL), lambda i: (i, 0)),
                   pl.BlockSpec((1, 1), lambda i: (0, 0)),
                   pl.BlockSpec((1, D_MODEL), lambda i: (0, 0))],
        out_shape=[jax.ShapeDtypeStruct((L, D_MODEL), F32), jax.ShapeDtypeStruct((1, 1), F32),
                   jax.ShapeDtypeStruct((1, D_MODEL), F32)],
        compiler_params=_cparams(1),
    )(h, gf, target)


def _ffn_bwd_act(dh, h, g2, gate_s, up_s, wp, layer):
    L = h.shape[0]
    tm = min(TM_FFN, L)

    def body(dh_ref, h_ref, g_ref, gate_ref, up_ref, w_ref,
             dhm_ref, dg_ref, dgate_ref, dup_ref, act_ref, dhb_ref, dn2):
        m, k = pl.program_id(0), pl.program_id(1)

        @pl.when(jnp.logical_and(m == 0, k == 0))
        def _():
            dg_ref[...] = jnp.zeros_like(dg_ref)

        @pl.when(k == 0)
        def _():
            dhb_ref[...] = dh_ref[...].astype(BF16)
            dn2[...] = jnp.zeros_like(dn2)

        wd, wg_t, wu_t = _ffn_weights(w_ref)
        dact = _dot_nt(dhb_ref[...], wd)
        gate = gate_ref[...].astype(F32)
        up = up_ref[...].astype(F32)
        sg = _sigmoid(gate)
        silu = gate * sg
        dgate = (dact * up * (sg * (1.0 + gate * (1.0 - sg)))).astype(BF16)
        dup = (dact * silu).astype(BF16)
        dgate_ref[...] = dgate
        dup_ref[...] = dup
        act_ref[...] = (silu * up).astype(BF16)
        dn2[...] += _dot(dgate, wg_t) + _dot(dup, wu_t)

        @pl.when(k == N_SHARD - 1)
        def _():
            xhat, r = _rms_hat(h_ref[...])
            d = dn2[...]
            dg_ref[...] += jnp.sum(d * xhat, axis=0, keepdims=True)
            dhm_ref[...] = dh_ref[...] + _rms_bwd(d * g_ref[...], xhat, r)

    act_spec = pl.BlockSpec((None, tm, FF_SHARD), lambda m, k: (k, m, 0))
    act_shape = jax.ShapeDtypeStruct((N_SHARD, L, FF_SHARD), BF16)
    row_spec = pl.BlockSpec((tm, D_MODEL), lambda m, k: (m, 0))
    return pl.pallas_call(
        body, name="ffn_bwd_act", grid=(L // tm, N_SHARD),
        in_specs=[row_spec, row_spec,
                  pl.BlockSpec((1, D_MODEL), lambda m, k: (0, 0)),
                  act_spec, act_spec,
                  pl.BlockSpec((None, None, P_FF_ROWS, D_MODEL), lambda m, k: (k, layer, 0, 0))],
        out_specs=[row_spec,
                   pl.BlockSpec((1, D_MODEL), lambda m, k: (0, 0)),
                   act_spec, act_spec, act_spec, row_spec],
        out_shape=[jax.ShapeDtypeStruct((L, D_MODEL), F32), jax.ShapeDtypeStruct((1, D_MODEL), F32),
                   act_shape, act_shape, act_shape, jax.ShapeDtypeStruct((L, D_MODEL), BF16)],
        scratch_shapes=[pltpu.VMEM((tm, D_MODEL), F32)],
        compiler_params=_cparams(2),
    )(dh, h, g2, gate_s, up_s, wp)


def _ffn_bwd_w(n2, dgate_s, dup_s, act_s, dhb, gbuf, layer):
    L = n2.shape[0]
    tm = min(TM_FFN, L)

    def body(n2_ref, dgate_ref, dup_ref, act_ref, dhb_ref, g_in, g_ref):
        m = pl.program_id(1)

        @pl.when(m == 0)
        def _():
            g_ref[...] = jnp.zeros_like(g_ref)

        n2v = n2_ref[...]
        g_ref[0:FF_SHARD, :] += _dot_tn(act_ref[...], dhb_ref[...])
        g_ref[FF_SHARD:2 * FF_SHARD, :] += _dot_tn(dgate_ref[...], n2v)
        g_ref[2 * FF_SHARD:P_FF_ROWS, :] += _dot_tn(dup_ref[...], n2v)

    act_spec = pl.BlockSpec((None, tm, FF_SHARD), lambda k, m: (k, m, 0))
    row_spec = pl.BlockSpec((tm, D_MODEL), lambda k, m: (m, 0))
    return pl.pallas_call(
        body, name="ffn_bwd_w", grid=(N_SHARD, L // tm),
        in_specs=[row_spec, act_spec, act_spec, act_spec, row_spec, pl.BlockSpec(memory_space=pl.ANY)],
        out_specs=pl.BlockSpec((None, None, P_FF_ROWS, D_MODEL), lambda k, m: (layer, k, 0, 0)),
        out_shape=jax.ShapeDtypeStruct(gbuf.shape, F32),
        input_output_aliases={5: 0},
        compiler_params=_cparams(2),
    )(n2, dgate_s, dup_s, act_s, dhb, gbuf)


def _mix_out_bwd(dhm, yraw, ypool, wp, layer, b_glu, gbuf):
    L = dhm.shape[0]
    tm = min(TM, L)

    def body(dhm_ref, yr_ref, yp_ref, wglu_ref, b_ref, wout_ref, g1_in,
             dyr_ref, dyp_ref, db_ref, g1_ref, dwout, dwglu, gpack):
        i = pl.program_id(0)

        @pl.when(i == 0)
        def _():
            db_ref[...] = jnp.zeros_like(db_ref)
            dwout[...] = jnp.zeros_like(dwout)
            dwglu[...] = jnp.zeros_like(dwglu)

        dhb = dhm_ref[...].astype(BF16)
        wglu = _glu_weight(wglu_ref)
        dmix = _dot_nt(dhb, wout_ref[...].reshape(D_MODEL, D_MODEL))
        dyp_ref[...] = dmix[:, :D_POOL]
        d_o = dmix[:, D_POOL:]
        yraw_v = yr_ref[...]
        y = _gelu(yraw_v)
        yb = y.astype(BF16)
        sig = _sigmoid(_dot(yb, wglu) + b_ref[...])
        mix = jnp.concatenate([yp_ref[...], y * sig], axis=1).astype(BF16)
        dwout[...] += _dot_tn(mix, dhb).reshape(N_SHARD, 256, D_MODEL)
        dz = d_o * y * sig * (1.0 - sig)
        dzb = dz.astype(BF16)
        db_ref[...] += jnp.sum(dz, axis=0, keepdims=True)
        dwglu[...] += _dot_tn(yb, dzb)
        dy = d_o * sig + _dot_nt(dzb, wglu)
        dyr_ref[...] = dy * _gelu_grad(yraw_v)

        @pl.when(i == n_steps - 1)
        def _():
            gpack[:, :gb, :] = _glu_pack(dwglu[...])
            gpack[:, gb:, :] = jnp.zeros((N_SHARD, P_GLU_PAD - gb, D_MODEL), F32)
            pltpu.sync_copy(gpack, g1_ref.at[layer, :, pl.ds(gb * gi, P_GLU_PAD), :])
            pltpu.sync_copy(dwout, g1_ref.at[layer, :, pl.ds(ob * oi, ob), :])

    gb, gi = P_GLU_BLK
    ob, oi = P_OUT_BLK
    n_steps = L // tm
    return pl.pallas_call(
        body, name="mix_out_bwd", grid=(n_steps,),
        in_specs=[pl.BlockSpec((tm, D_MODEL), lambda i: (i, 0)),
                  pl.BlockSpec((tm, D_SSM), lambda i: (i, 0)),
                  pl.BlockSpec((tm, D_POOL), lambda i: (i, 0)),
                  pl.BlockSpec((N_SHARD, None, gb, D_MODEL), lambda i: (0, layer, gi, 0)),
                  pl.BlockSpec((1, D_SSM), lambda i: (0, 0)),
                  pl.BlockSpec((N_SHARD, None, ob, D_MODEL), lambda i: (0, layer, oi, 0)),
                  pl.BlockSpec(memory_space=pl.ANY)],
        out_specs=[pl.BlockSpec((tm, D_SSM), lambda i: (i, 0)),
                   pl.BlockSpec((tm, D_POOL), lambda i: (i, 0)),
                   pl.BlockSpec((1, D_SSM), lambda i: (0, 0)),
                   pl.BlockSpec(memory_space=pl.ANY)],
        out_shape=[jax.ShapeDtypeStruct((L, D_SSM), F32), jax.ShapeDtypeStruct((L, D_POOL), F32),
                   jax.ShapeDtypeStruct((1, D_SSM), F32),
                   jax.ShapeDtypeStruct(gbuf.shape, F32)],
        scratch_shapes=[pltpu.VMEM((N_SHARD, ob, D_MODEL), F32), pltpu.VMEM((D_SSM, D_SSM), F32),
                        pltpu.VMEM((N_SHARD, P_GLU_PAD, D_MODEL), F32)],
        input_output_aliases={6: 3},
        compiler_params=_cparams(1),
    )(dhm, yraw, ypool, wp, b_glu, wp, gbuf)


def _ssm_bwd(dyraw, u, sre, sim, cpad_t, bpad_t, ar, ai, dskip):
    L = u.shape[0]
    ts = min(TS, L)
    nt = L // ts
    nq = 4
    cq = N_STATE // nq

    def body(dy_ref, u_ref, sre_ref, sim_ref, ct_ref, bt_ref, ar_ref, ai_ref, dsk_ref,
             du_ref, dcp_ref, dbp_ref, dar_ref, dai_ref, ddsk_ref, gre, gim, cr, ci, tab, accr, acci):
        t = pl.program_id(1)

        @pl.when(t == 0)
        def _():
            for ref in (cr, ci, accr, acci, dcp_ref, dbp_ref, ddsk_ref):
                ref[...] = jnp.zeros_like(ref)
            _scan_tables(ar_ref[...], -ai_ref[...], tab, reverse=True)

        dy = dy_ref[...]
        dyb = dy.astype(BF16)
        uf = u_ref[...]
        ub = uf.astype(BF16)
        for jj in range(4):
            cols = slice(jj * 128, (jj + 1) * 128)
            ds = _dot(dyb, ct_ref[jj])
            gre[:, cols] = ds[:, :128]
            gim[:, cols] = ds[:, 128:]
            scat = jnp.concatenate([sre_ref[:, cols], sim_ref[:, cols]], axis=1).astype(BF16)
            dcp_ref[jj] += _dot_tn(scat, dyb)

        n_grp = ts // SUBLANES
        shp = (SUBLANES, SCAN_LANES)
        last_row = lax.broadcasted_iota(jnp.int32, shp, 0) == SUBLANES - 1
        for cc in range(cq // SCAN_LANES):
            cols = slice(cc * SCAN_LANES, (cc + 1) * SCAN_LANES)
            def step(i, carry, cols=cols):
                c_r, c_i, a_r, a_i = carry
                r0 = pl.multiple_of((n_grp - 1 - i) * SUBLANES, SUBLANES)
                xr = gre[pl.ds(r0, SUBLANES), cols]
                xi = gim[pl.ds(r0, SUBLANES), cols]
                for n, s in enumerate((1, 2, 4)):
                    tr, ti = tab[2 * n, :, cols], tab[2 * n + 1, :, cols]
                    rr = pltpu.roll(xr, SUBLANES - s, 0)
                    ri = pltpu.roll(xi, SUBLANES - s, 0)
                    xr, xi = xr + tr * rr - ti * ri, xi + tr * ri + ti * rr
                qr, qi = tab[6, :, cols], tab[7, :, cols]
                xr, xi = xr + qr * c_r - qi * c_i, xi + qr * c_i + qi * c_r
                gre[pl.ds(r0, SUBLANES), cols] = xr
                gim[pl.ds(r0, SUBLANES), cols] = xi
                nr = jnp.where(last_row, c_r, pltpu.roll(xr, SUBLANES - 1, 0))
                ni = jnp.where(last_row, c_i, pltpu.roll(xi, SUBLANES - 1, 0))
                sr = sre_ref[pl.ds(r0, SUBLANES), cols]
                si = sim_ref[pl.ds(r0, SUBLANES), cols]
                a_r = a_r + sr * nr + si * ni
                a_i = a_i + sr * ni - si * nr
                return (jnp.broadcast_to(xr[:1, :], shp), jnp.broadcast_to(xi[:1, :], shp), a_r, a_i)

            c_r, c_i, a_r, a_i = lax.fori_loop(
                0, n_grp, step, (cr[:, cols], ci[:, cols], accr[:, cols], acci[:, cols]), unroll=2)
            cr[:, cols] = c_r
            ci[:, cols] = c_i
            accr[:, cols] = a_r
            acci[:, cols] = a_i

        acc = dsk_ref[...] * dy
        for jj in range(4):
            cols = slice(jj * 128, (jj + 1) * 128)
            gcat = jnp.concatenate([gre[:, cols], gim[:, cols]], axis=1).astype(BF16)
            acc = acc + _dot(gcat, bt_ref[jj])
            dbp_ref[jj] += _dot_tn(ub, gcat)
        du_ref[...] = acc
        ddsk_ref[...] += jnp.sum(dy * uf, axis=0, keepdims=True)

        @pl.when(t == nt - 1)
        def _():
            dar_ref[...] = jnp.sum(accr[...], axis=0, keepdims=True)
            dai_ref[...] = jnp.sum(acci[...], axis=0, keepdims=True)

    f32_scr = lambda *s: pltpu.VMEM(s, F32)
    return pl.pallas_call(
        body, name="ssm_bwd", grid=(nq, nt),
        in_specs=[pl.BlockSpec((ts, 128), lambda q, t: (nt - 1 - t, q)),
                  pl.BlockSpec((ts, 128), lambda q, t: (nt - 1 - t, 4 + q)),
                  pl.BlockSpec((ts, cq), lambda q, t: (nt - 1 - t, q)),
                  pl.BlockSpec((ts, cq), lambda q, t: (nt - 1 - t, q)),
                  pl.BlockSpec((4, 128, 256), lambda q, t: (q, 0, 0)),
                  pl.BlockSpec((4, 256, 128), lambda q, t: (q, 0, 0)),
                  pl.BlockSpec((1, cq), lambda q, t: (0, q)),
                  pl.BlockSpec((1, cq), lambda q, t: (0, q)),
                  pl.BlockSpec((1, 128), lambda q, t: (0, q))],
        out_specs=[pl.BlockSpec((ts, 128), lambda q, t: (nt - 1 - t, q)),
                   pl.BlockSpec((4, 256, 128), lambda q, t: (q, 0, 0)),
                   pl.BlockSpec((4, 128, 256), lambda q, t: (q, 0, 0)),
                   pl.BlockSpec((1, cq), lambda q, t: (0, q)),
                   pl.BlockSpec((1, cq), lambda q, t: (0, q)),
                   pl.BlockSpec((1, 128), lambda q, t: (0, q))],
        out_shape=[jax.ShapeDtypeStruct((L, D_SSM), F32),
                   jax.ShapeDtypeStruct((N_PAIRS, 256, 128), F32), jax.ShapeDtypeStruct((N_PAIRS, 128, 256), F32),
                   jax.ShapeDtypeStruct((1, N_STATE), F32), jax.ShapeDtypeStruct((1, N_STATE), F32),
                   jax.ShapeDtypeStruct((1, D_SSM), F32)],
        scratch_shapes=[f32_scr(ts, cq), f32_scr(ts, cq), f32_scr(SUBLANES, cq), f32_scr(SUBLANES, cq),
                        f32_scr(8, SUBLANES, cq), f32_scr(SUBLANES, cq), f32_scr(SUBLANES, cq)],
        compiler_params=_cparams(2),
    )(dyraw, u, sre, sim, cpad_t, bpad_t, ar, ai, dskip)


def _pool_bwd(dyp, u, w_pool, scale):
    L = u.shape[0]
    tm = min(TM, L)
    nt = L // tm
    halo_per_tile = tm // POOL_HALO

    def body(dyp_ref, u_ref, halo_ref, wp_ref, sc_ref, du_ref, dwp_ref, dsc_ref, carry):
        i = pl.program_id(0)
        tile = nt - 1 - i

        @pl.when(i == 0)
        def _():
            carry[...] = jnp.zeros_like(carry)
            dwp_ref[...] = jnp.zeros_like(dwp_ref)
            dsc_ref[...] = jnp.zeros_like(dsc_ref)

        up = u_ref[...]
        halo = jnp.where(tile > 0, halo_ref[...], jnp.zeros_like(halo_ref))
        diffs = _pool_diff(jnp.concatenate([halo, up], axis=0), tile * tm, tm)
        rows = tile * tm + lax.broadcasted_iota(jnp.int32, (tm, 1), 0)
        n_ext = tm + POOL_HALO
        for gi, w in enumerate(POOL_WINDOWS):
            cols = slice(gi * POOL_GROUP, (gi + 1) * POOL_GROUP)
            db = diffs[gi].astype(BF16)
            dyp = dyp_ref[:, cols]
            dsc_ref[:, cols] += jnp.sum(dyp * _dot(db, wp_ref[gi]), axis=0, keepdims=True)
            dp = (dyp * sc_ref[:, cols]).astype(BF16)
            ddiff = _dot_nt(dp, wp_ref[gi])
            dwp_ref[gi] += _dot_tn(db, dp)
            e = ddiff * (1.0 / jnp.minimum(rows + 1, w).astype(F32))
            s = jnp.concatenate([e, carry[:, cols]], axis=0)
            k = 1
            while k < w:
                s = s + pltpu.roll(s, n_ext - k, 0)
                k *= 2
            du_ref[:, cols] = s[:tm, :] - ddiff
            carry[:, cols] = e[:POOL_HALO, :]

    return pl.pallas_call(
        body, name="pool_bwd", grid=(nt,),
        in_specs=[pl.BlockSpec((tm, D_POOL), lambda i: (nt - 1 - i, 0)),
                  pl.BlockSpec((tm, D_POOL), lambda i: (nt - 1 - i, 0)),
                  pl.BlockSpec((POOL_HALO, D_POOL), lambda i: (jnp.maximum((nt - 1 - i) * halo_per_tile - 1, 0), 0)),
                  pl.BlockSpec((4, POOL_GROUP, POOL_GROUP), lambda i: (0, 0, 0)),
                  pl.BlockSpec((1, D_POOL), lambda i: (0, 0))],
        out_specs=[pl.BlockSpec((tm, D_POOL), lambda i: (nt - 1 - i, 0)),
                   pl.BlockSpec((4, POOL_GROUP, POOL_GROUP), lambda i: (0, 0, 0)),
                   pl.BlockSpec((1, D_POOL), lambda i: (0, 0))],
        out_shape=[jax.ShapeDtypeStruct((L, D_POOL), F32),
                   jax.ShapeDtypeStruct((4, POOL_GROUP, POOL_GROUP), F32),
                   jax.ShapeDtypeStruct((1, D_POOL), F32)],
        scratch_shapes=[pltpu.VMEM((POOL_HALO, D_POOL), F32)],
        compiler_params=_cparams(1),
    )(dyp, u, u, w_pool, scale)


def _mix_in_bwd(dup, dus, h, dhm, g1, wp, layer, gbuf):
    L = h.shape[0]
    tm = min(TM, L)
    n_steps = L // tm
    blk, idx = P_IN_BLK

    def body(dup_ref, dus_ref, h_ref, dhm_ref, g_ref, w_ref, g1_in, dh_ref, dg_ref, g1_ref, dwin):
        i = pl.program_id(0)

        @pl.when(i == 0)
        def _():
            dg_ref[...] = jnp.zeros_like(dg_ref)
            dwin[...] = jnp.zeros_like(dwin)

        du = jnp.concatenate([dup_ref[...], dus_ref[...]], axis=1).astype(BF16)
        dn1 = _dot_nt(du, w_ref[...].reshape(D_MODEL, D_MODEL))
        xhat, r = _rms_hat(h_ref[...])
        g = g_ref[...]
        n1 = (xhat * g).astype(BF16)
        dwin[...] += _dot_tn(n1, du).reshape(N_SHARD, blk, D_MODEL)
        dg_ref[...] += jnp.sum(dn1 * xhat, axis=0, keepdims=True)
        dh_ref[...] = dhm_ref[...] + _rms_bwd(dn1 * g, xhat, r)

        @pl.when(i == n_steps - 1)
        def _():
            pltpu.sync_copy(dwin, g1_ref.at[layer, :, pl.ds(blk * idx, blk), :])

    row_spec = pl.BlockSpec((tm, D_MODEL), lambda i: (i, 0))
    half_spec = pl.BlockSpec((tm, D_POOL), lambda i: (i, 0))
    return pl.pallas_call(
        body, name="mix_in_bwd", grid=(n_steps,),
        in_specs=[half_spec, half_spec, row_spec, row_spec,
                  pl.BlockSpec((1, D_MODEL), lambda i: (0, 0)),
                  pl.BlockSpec((N_SHARD, None, blk, D_MODEL), lambda i: (0, layer, idx, 0)),
                  pl.BlockSpec(memory_space=pl.ANY)],
        out_specs=[row_spec, pl.BlockSpec((1, D_MODEL), lambda i: (0, 0)), pl.BlockSpec(memory_space=pl.ANY)],
        out_shape=[jax.ShapeDtypeStruct((L, D_MODEL), F32), jax.ShapeDtypeStruct((1, D_MODEL), F32),
                   jax.ShapeDtypeStruct(gbuf.shape, F32)],
        scratch_shapes=[pltpu.VMEM((N_SHARD, blk, D_MODEL), F32)],
        input_output_aliases={6: 2},
        compiler_params=_cparams(1),
    )(dup, dus, h, dhm, g1, wp, gbuf)


def _disc_math(lr, li, ldt, br_t, bi_t):
    dt = jnp.exp(ldt)
    mag = jnp.exp(lr * dt)
    ang = li * dt
    ar = mag * jnp.cos(ang)
    ai = mag * jnp.sin(ang)
    den = lr * lr + li * li
    nr, ni = ar - 1.0, ai
    cr = (nr * lr + ni * li) / den
    ci = (ni * lr - nr * li) / den
    return ar, ai, cr * br_t - ci * bi_t, cr * bi_t + ci * br_t


def _disc_fwd(lr, li, ldt, br_t, bi_t):
    def body(lr_ref, li_ref, ldt_ref, br_ref, bi_ref, ar_ref, ai_ref, bbr_ref, bbi_ref):
        ar, ai, bbr, bbi = _disc_math(lr_ref[...], li_ref[...], ldt_ref[...], br_ref[...], bi_ref[...])
        ar_ref[...] = ar
        ai_ref[...] = ai
        bbr_ref[...] = bbr
        bbi_ref[...] = bbi

    shapes = [jax.ShapeDtypeStruct(a.shape, F32) for a in (lr, li, br_t, bi_t)]
    return pl.pallas_call(body, name="ssm_disc_fwd", out_shape=shapes,
                          compiler_params=pltpu.CompilerParams(vmem_limit_bytes=VMEM_LIMIT))(lr, li, ldt, br_t, bi_t)


def _disc_bwd(lr, li, ldt, br_t, bi_t, dar, dai, dbbr, dbbi):
    def body(lr_ref, li_ref, ldt_ref, br_ref, bi_ref, dar_ref, dai_ref, dbbr_ref, dbbi_ref,
             dlr_ref, dli_ref, dldt_ref, dbr_ref, dbi_ref):
        prim = (lr_ref[...], li_ref[...], ldt_ref[...], br_ref[...], bi_ref[...])
        _, pullback = jax.vjp(_disc_math, *prim)
        dlr, dli, dldt, dbr, dbi = pullback((dar_ref[...], dai_ref[...], dbbr_ref[...], dbbi_ref[...]))
        dlr_ref[...] = dlr
        dli_ref[...] = dli
        dldt_ref[...] = dldt
        dbr_ref[...] = dbr
        dbi_ref[...] = dbi

    shapes = [jax.ShapeDtypeStruct(a.shape, F32) for a in (lr, li, ldt, br_t, bi_t)]
    return pl.pallas_call(body, name="ssm_disc_bwd", out_shape=shapes,
                          compiler_params=pltpu.CompilerParams(vmem_limit_bytes=VMEM_LIMIT))(
        lr, li, ldt, br_t, bi_t, dar, dai, dbbr, dbbi)


def _pad_pairs(m_re, m_im):
    def blocks(m):
        v = m.transpose(0, 2, 1).reshape(N_PAIRS, 2, SSM_GROUP, SSM_STATE)
        return jnp.einsum("ab,jahp->jahbp", jnp.eye(2, dtype=m.dtype), v).reshape(N_PAIRS, 32, 128)
    both = jnp.concatenate([blocks(m_re), blocks(m_im)], axis=-1)
    place = jax.nn.one_hot(jnp.arange(N_PAIRS) % 4, 4, dtype=both.dtype)
    return jnp.einsum("jk,jrc->jkrc", place, both).reshape(N_PAIRS, 128, 256)


def _unpad_pairs(x):
    place = jax.nn.one_hot(jnp.arange(N_PAIRS) % 4, 4, dtype=x.dtype)
    both = jnp.einsum("jk,jkrc->jrc", place, x.reshape(N_PAIRS, 4, 32, 256))

    def unblock(v):
        v = v.reshape(N_PAIRS, 2, SSM_GROUP, 2, SSM_STATE)
        d = jnp.einsum("ab,jahbp->jahp", jnp.eye(2, dtype=x.dtype), v)
        return d.reshape(N_SSM_GROUPS, SSM_GROUP, SSM_STATE).transpose(0, 2, 1)
    return unblock(both[..., :128]), unblock(both[..., 128:])


def _adamw_math(w, g, m, v):
    m = ADAM_B1 * m + (1.0 - ADAM_B1) * g
    v = ADAM_B2 * v + (1.0 - ADAM_B2) * (g * g)
    m_hat = m / (1.0 - ADAM_B1 ** ADAM_STEP)
    v_hat = v / (1.0 - ADAM_B2 ** ADAM_STEP)
    delta = -ADAM_LR * (m_hat / (jnp.sqrt(v_hat) + ADAM_EPS) + ADAM_WD * w)
    return delta, m, v


def _adamw(name, w, m, v, gbuf, g_block, g_row0, row_tile, glu=False):
    nl, r, c = w.shape
    n_tiles = r // row_tile
    g_rows, g_cols = g_block
    g_tile = g_rows // n_tiles
    g_off = g_row0 // g_tile

    def body(w_ref, m_ref, v_ref, g_ref, go_ref, d_ref, mo_ref, vo_ref):
        g = g_ref[...]
        if glu:
            g = jnp.concatenate([g[:, :D_SSM], g[:, D_SSM:]], axis=0)
        delta, mn, vn = _adamw_math(w_ref[...], g, m_ref[...], v_ref[...])
        go_ref[...] = g
        d_ref[...] = delta
        mo_ref[...] = mn
        vo_ref[...] = vn

    w_spec = pl.BlockSpec((None, row_tile, c), lambda l, j: (l, j, 0))
    shape = jax.ShapeDtypeStruct(w.shape, F32)
    return pl.pallas_call(
        body, name=name, grid=(nl, n_tiles),
        in_specs=[w_spec, w_spec, w_spec, pl.BlockSpec((None, g_tile, g_cols), lambda l, j: (l, g_off + j, 0))],
        out_specs=[w_spec] * 4,
        out_shape=[shape] * 4,
        compiler_params=_cparams(2),
    )(w, m, v, gbuf)


def _pack_weights(ids, layer, w_in, w_glu, w_out, w_down, w_gate_t, w_up_t):
    gb, gi = P_GLU_BLK
    ib, ii = P_IN_BLK
    ob, oi = P_OUT_BLK

    def body(ids_ref, in_ref, glu_ref, out_ref, dn_ref, gate_ref, up_ref, p_ref):
        p_ref[0:FF_SHARD, :] = dn_ref[...].astype(BF16)
        p_ref[FF_SHARD:2 * FF_SHARD, :] = gate_ref[...].astype(BF16)
        p_ref[2 * FF_SHARD:P_FF_ROWS, :] = up_ref[...].astype(BF16)
        g = glu_ref[...]
        p_ref[gb * gi:gb * (gi + 1), :] = jnp.concatenate([g[:gb, :], g[gb:, :]], axis=1).astype(BF16)
        p_ref[gb * (gi + 1):ib * ii, :] = jnp.zeros((P_GLU_PAD - gb, D_MODEL), BF16)
        p_ref[ib * ii:ib * (ii + 1), :] = in_ref[...].astype(BF16)
        p_ref[ob * oi:ob * (oi + 1), :] = out_ref[...].astype(BF16)

    def spec(a):
        return pl.BlockSpec((None,) + a.shape[1:], lambda i, ids_ref: (layer, 0, 0))

    ins = (w_in, w_glu, w_out, w_down, w_gate_t, w_up_t)
    grid_spec = pltpu.PrefetchScalarGridSpec(
        num_scalar_prefetch=1, grid=(1,),
        in_specs=[spec(a) for a in ins],
        out_specs=pl.BlockSpec((None, None, P_ROWS, D_MODEL), lambda i, ids_ref: (ids_ref[1], 0, 0, 0)))
    return pl.pallas_call(
        body, name="pack_weights", grid_spec=grid_spec,
        out_shape=jax.ShapeDtypeStruct((N_SHARD, 1, P_ROWS, D_MODEL), BF16),
        compiler_params=_cparams(1),
    )(ids, *ins)


MESH = pl.DeviceIdType.MESH
_ANY = pl.BlockSpec(memory_space=pl.ANY)
P_HALF = P_ROWS // 2
RS_ROW_TILE = 352


def _mesh_pos():
    return lax.axis_index("x"), lax.axis_index("y"), lax.axis_index("c")


def _other_chips(x, y):
    return [(1 - x, y), (x, 1 - y), (1 - x, 1 - y)]


def _remote(src, dst, send_sems, recv_sems, n, to):
    return pltpu.make_async_remote_copy(src_ref=src, dst_ref=dst, send_sem=send_sems.at[n],
                                        recv_sem=recv_sems.at[n], device_id=to, device_id_type=MESH)


def _all_gather_weights(wp):
    def body(w_in, o, send_sems, recv_sems):
        x, y, c = _mesh_pos()
        k = 2 * x + y
        sib = (x, y, 1 - c)
        chips = _other_chips(x, y)

        def piece(shard, half):
            return o.at[shard, :, pl.ds(half * P_HALF, P_HALF), :]

        sends = []
        for j, (px, py) in enumerate(chips):
            cp = _remote(piece(k, c), piece(k, c), send_sems, recv_sems, j, (px, py, c))
            cp.start()
            sends.append(cp)
        for j, (px, py) in enumerate(chips):
            landed = piece(2 * px + py, c)
            _remote(landed, landed, send_sems, recv_sems, j, (px, py, c)).wait_recv()
            cp = _remote(landed, landed, send_sems, recv_sems, 3 + j, sib)
            cp.start()
            sends.append(cp)
        for j, (px, py) in enumerate(chips):
            passed = piece(2 * px + py, 1 - c)
            _remote(passed, passed, send_sems, recv_sems, 3 + j, sib).wait_recv()
        for cp in sends:
            cp.wait_send()

    return pl.pallas_call(
        body, name="all_gather_weights",
        in_specs=[_ANY], out_specs=_ANY,
        out_shape=jax.ShapeDtypeStruct(wp.shape, BF16),
        scratch_shapes=[pltpu.SemaphoreType.DMA((6,)), pltpu.SemaphoreType.DMA((6,))],
        input_output_aliases={0: 0},
    )(wp)


_HBM = pl.BlockSpec(memory_space=pltpu.HBM)
_SEM = pl.BlockSpec(memory_space=pltpu.SEMAPHORE)
_EFFECT = pltpu.CompilerParams(has_side_effects=pltpu.SideEffectType.DATAFLOW_SIDE_EFFECTING)
_TOKEN = jax.ShapeDtypeStruct((8, 128), F32)


def _in_hbm(a):
    return pltpu.with_memory_space_constraint(a, pltpu.HBM)


def _ag_start(name, wp, after):
    def body(w_ref, after_ref, send_sems, recv_sems, w_thru, token):
        x, y, c = _mesh_pos()
        mine = w_ref.at[2 * x + y, :, pl.ds(c * P_HALF, P_HALF), :]
        for j, (px, py) in enumerate(_other_chips(x, y)):
            _remote(mine, mine, send_sems, recv_sems, j, (px, py, c)).start()
        token[...] = jnp.zeros_like(token)

    return pl.pallas_call(
        body, name=name,
        out_shape=(pltpu.SemaphoreType.DMA((3,)), pltpu.SemaphoreType.DMA((3,)), pltpu.HBM(wp.shape, wp.dtype), _TOKEN),
        in_specs=(_HBM, _ANY), out_specs=(_SEM, _SEM, _HBM, pl.BlockSpec(memory_space=pltpu.VMEM)),
        input_output_aliases={0: 2}, compiler_params=_EFFECT,
    )(_in_hbm(wp), after)


def _ag_wait(name, send_sems, recv_sems, wp, after):
    def body(w_ref, send_sems, recv_sems, after_ref, w_out):
        x, y, c = _mesh_pos()
        mine = w_ref.at[2 * x + y, :, pl.ds(c * P_HALF, P_HALF), :]
        for j, (px, py) in enumerate(_other_chips(x, y)):
            landed = w_ref.at[2 * px + py, :, pl.ds(c * P_HALF, P_HALF), :]
            cp = _remote(mine, landed, send_sems, recv_sems, j, (px, py, c))
            cp.wait_send()
            cp.wait_recv()

    return pl.pallas_call(
        body, name=name, out_shape=pltpu.HBM(wp.shape, wp.dtype),
        in_specs=(_HBM, _SEM, _SEM, _ANY), out_specs=_HBM,
        input_output_aliases={0: 0}, compiler_params=_EFFECT,
    )(wp, send_sems, recv_sems, after)


def _ag_forward(wp):
    def body(w_in, o, send_sems, recv_sems):
        x, y, c = _mesh_pos()
        sib = (x, y, 1 - c)
        chips = _other_chips(x, y)
        sends = []
        for j, (px, py) in enumerate(chips):
            landed = o.at[2 * px + py, :, pl.ds(c * P_HALF, P_HALF), :]
            cp = _remote(landed, landed, send_sems, recv_sems, j, sib)
            cp.start()
            sends.append(cp)
        for j, (px, py) in enumerate(chips):
            passed = o.at[2 * px + py, :, pl.ds((1 - c) * P_HALF, P_HALF), :]
            _remote(passed, passed, send_sems, recv_sems, j, sib).wait_recv()
        for cp in sends:
            cp.wait_send()

    return pl.pallas_call(
        body, name="ag_forward",
        in_specs=[_ANY], out_specs=_ANY,
        out_shape=jax.ShapeDtypeStruct(wp.shape, wp.dtype),
        scratch_shapes=[pltpu.SemaphoreType.DMA((3,)), pltpu.SemaphoreType.DMA((3,))],
        input_output_aliases={0: 0},
    )(wp)


def _rs_chips_start(name, t):
    nl = t.shape[0]

    def body(t_ref, land_ref, send_sems, recv_sems, t_thru, land_thru, token):
        x, y, c = _mesh_pos()
        for j, (px, py) in enumerate(_other_chips(x, y)):
            _remote(t_ref.at[:, 2 * px + py], land_ref.at[j], send_sems, recv_sems, j, (px, py, c)).start()
        token[...] = jnp.zeros_like(token)

    land = lax.empty((3, nl, P_HALF, D_MODEL), BF16)
    return pl.pallas_call(
        body, name=name,
        out_shape=(pltpu.SemaphoreType.DMA((3,)), pltpu.SemaphoreType.DMA((3,)), pltpu.HBM(t.shape, t.dtype),
                   pltpu.HBM(land.shape, land.dtype), _TOKEN),
        in_specs=(_HBM, _HBM), out_specs=(_SEM, _SEM, _HBM, _HBM, pl.BlockSpec(memory_space=pltpu.VMEM)),
        input_output_aliases={0: 2, 1: 3}, compiler_params=_EFFECT,
    )(_in_hbm(t), _in_hbm(land))


def _rs_chips_wait(name, send_sems, recv_sems, t, land, after):
    def body(t_ref, land_ref, send_sems, recv_sems, after_ref, t_dead, land_out):
        x, y, c = _mesh_pos()
        for j, (px, py) in enumerate(_other_chips(x, y)):
            cp = _remote(t_ref.at[:, 2 * px + py], land_ref.at[j], send_sems, recv_sems, j, (px, py, c))
            cp.wait_send()
            cp.wait_recv()

    return pl.pallas_call(
        body, name=name, out_shape=(pltpu.HBM(t.shape, t.dtype), pltpu.HBM(land.shape, land.dtype)),
        in_specs=(_HBM, _HBM, _SEM, _SEM, _ANY), out_specs=(_HBM, _HBM),
        input_output_aliases={0: 0, 1: 1}, compiler_params=_EFFECT,
    )(t, land, send_sems, recv_sems, after)[1]


def _rs_to_sibling(g):
    def body(g_ref, b, send_sems, recv_sems):
        x, y, c = _mesh_pos()
        cp = _remote(g_ref.at[:, :, pl.ds((1 - c) * P_HALF, P_HALF), :], b, send_sems, recv_sems, 0, (x, y, 1 - c))
        cp.start()
        cp.wait()

    nl = g.shape[0]
    return pl.pallas_call(
        body, name="rs_to_sibling",
        in_specs=[_ANY], out_specs=_ANY,
        out_shape=jax.ShapeDtypeStruct((nl, N_SHARD, P_HALF, D_MODEL), F32),
        scratch_shapes=[pltpu.SemaphoreType.DMA((1,)), pltpu.SemaphoreType.DMA((1,))],
    )(g)


def _rs_add(name, ids, g, buf, row_tile):
    nl, _, hr, cols = buf.shape
    n_rt = hr // row_tile

    def body(ids_ref, g_ref, b_ref, own_ref, tb_ref):
        t = g_ref[...] + b_ref[...]
        tb_ref[...] = t.astype(BF16)

        @pl.when(pl.program_id(2) == ids_ref[1])
        def _():
            own_ref[...] = t

    blk = (None, None, row_tile, cols)
    grid_spec = pltpu.PrefetchScalarGridSpec(
        num_scalar_prefetch=1, grid=(nl, n_rt, N_SHARD),
        in_specs=[pl.BlockSpec(blk, lambda l, j, s, ids_ref: (l, s, ids_ref[0] * n_rt + j, 0)),
                  pl.BlockSpec(blk, lambda l, j, s, ids_ref: (l, s, j, 0))],
        out_specs=[pl.BlockSpec((None, row_tile, cols), lambda l, j, s, ids_ref: (l, j, 0)),
                   pl.BlockSpec(blk, lambda l, j, s, ids_ref: (l, s, j, 0))])
    return pl.pallas_call(
        body, name=name, grid_spec=grid_spec,
        out_shape=[jax.ShapeDtypeStruct((nl, hr, cols), F32), jax.ShapeDtypeStruct(buf.shape, BF16)],
        compiler_params=_cparams(3),
    )(ids, g, buf)


def _rs_sum(ids, own, bufb, row_tile):
    nl, hr, cols = own.shape
    n_rt = hr // row_tile

    def body(ids_ref, own_ref, b_ref, f_ref):
        f_ref[...] = ((own_ref[...] + b_ref[0].astype(F32)) + b_ref[1].astype(F32)) + b_ref[2].astype(F32)

    grid_spec = pltpu.PrefetchScalarGridSpec(
        num_scalar_prefetch=1, grid=(nl, n_rt),
        in_specs=[pl.BlockSpec((None, row_tile, cols), lambda l, j, ids_ref: (l, j, 0)),
                  pl.BlockSpec((3, None, row_tile, cols), lambda l, j, ids_ref: (0, l, j, 0))],
        out_specs=pl.BlockSpec((None, row_tile, cols), lambda l, j, ids_ref: (l, ids_ref[0] * n_rt + j, 0)))
    return pl.pallas_call(
        body, name="rs_sum", grid_spec=grid_spec,
        out_shape=jax.ShapeDtypeStruct((nl, 2 * hr, cols), F32),
        compiler_params=_cparams(2),
    )(ids, own, bufb)


def _rs_exchange(f):
    def body(f_in, o, send_sems, recv_sems):
        x, y, c = _mesh_pos()
        mine = o.at[:, pl.ds(c * P_HALF, P_HALF), :]
        cp = _remote(mine, mine, send_sems, recv_sems, 0, (x, y, 1 - c))
        cp.start()
        cp.wait_send()
        theirs = o.at[:, pl.ds((1 - c) * P_HALF, P_HALF), :]
        _remote(theirs, theirs, send_sems, recv_sems, 0, (x, y, 1 - c)).wait_recv()

    return pl.pallas_call(
        body, name="rs_exchange",
        in_specs=[_ANY], out_specs=_ANY,
        out_shape=jax.ShapeDtypeStruct(f.shape, F32),
        scratch_shapes=[pltpu.SemaphoreType.DMA((1,)), pltpu.SemaphoreType.DMA((1,))],
        input_output_aliases={0: 0},
    )(f)


def _small_all_reduce(s):
    n_rows = s.shape[0]
    hr = n_rows // 2

    def body(s_ref, o_ref, sibbuf, tbuf, cbuf, fbuf, send_sems, recv_sems):
        x, y, c = _mesh_pos()
        sib = (x, y, 1 - c)
        mine = pl.ds(pl.multiple_of(c * hr, SUBLANES), hr)
        theirs = pl.ds(pl.multiple_of((1 - c) * hr, SUBLANES), hr)
        first = _remote(s_ref.at[theirs], sibbuf, send_sems, recv_sems, 0, sib)
        first.start()
        first.wait()
        tbuf[...] = s_ref[mine, :] + sibbuf[...]
        cps = []
        for j, (px, py) in enumerate(_other_chips(x, y)):
            cp = _remote(tbuf, cbuf.at[j], send_sems, recv_sems, 1 + j, (px, py, c))
            cp.start()
            cps.append(cp)
        for cp in cps:
            cp.wait()
        f = (tbuf[...] + cbuf[1]) + (cbuf[0] + cbuf[2])
        fbuf[...] = f
        o_ref[mine, :] = f
        last = _remote(fbuf, o_ref.at[mine], send_sems, recv_sems, 4, sib)
        last.start()
        last.wait()

    vmem = pl.BlockSpec(memory_space=pltpu.VMEM)
    return pl.pallas_call(
        body, name="small_all_reduce",
        in_specs=[vmem], out_specs=vmem,
        out_shape=jax.ShapeDtypeStruct(s.shape, F32),
        scratch_shapes=[pltpu.VMEM((hr, D_MODEL), F32), pltpu.VMEM((hr, D_MODEL), F32),
                        pltpu.VMEM((3, hr, D_MODEL), F32), pltpu.VMEM((hr, D_MODEL), F32),
                        pltpu.SemaphoreType.DMA((5,)), pltpu.SemaphoreType.DMA((5,))],
        compiler_params=pltpu.CompilerParams(vmem_limit_bytes=VMEM_LIMIT),
    )(s)


_SMALL = ("norm_mix", "w_pool", "pool_scale", "lam_re", "lam_im", "log_dt", "b_re", "b_im", "c_re", "c_im",
          "d_skip", "b_glu", "norm_ffn", "norm_final")
_WEIGHTS = ("norm_mix", "w_in", "w_pool", "pool_scale", "lam_re", "lam_im", "log_dt", "b_re", "b_im", "c_re",
            "c_im", "d_skip", "w_glu", "b_glu", "w_out", "norm_ffn", "w_gate", "w_up", "w_down", "norm_final")


def _local_step(x, target, p, get_weights, put_grads):
    nl = p["norm_mix"].shape[0]

    def tied(a, token):
        return a if token is None else a + token
    n_rows = nl * N_SSM_GROUPS
    lr = p["lam_re"].reshape(n_rows, 1, SSM_STATE)
    li = p["lam_im"].reshape(n_rows, 1, SSM_STATE)
    ldt = p["log_dt"].reshape(n_rows, 1, 1)
    br_t = p["b_re"].reshape(n_rows, SSM_STATE, SSM_GROUP).transpose(0, 2, 1)
    bi_t = p["b_im"].reshape(n_rows, SSM_STATE, SSM_GROUP).transpose(0, 2, 1)
    ar, ai, bbr_t, bbi_t = _disc_fwd(lr, li, ldt, br_t, bi_t)
    ar = ar.reshape(nl, 1, N_STATE)
    ai = ai.reshape(nl, 1, N_STATE)
    bbr = bbr_t.transpose(0, 2, 1).reshape(nl, N_SSM_GROUPS, SSM_STATE, SSM_GROUP)
    bbi = bbi_t.transpose(0, 2, 1).reshape(nl, N_SSM_GROUPS, SSM_STATE, SSM_GROUP)
    w_pool = p["w_pool"].astype(BF16)

    layers = []
    h = x
    for l in range(nl):
        bpad = _pad_pairs(bbr[l], bbi[l]).astype(BF16)
        cpad_t = _pad_pairs(p["c_re"][l].transpose(0, 2, 1), -p["c_im"][l].transpose(0, 2, 1)).astype(BF16)
        dskip = p["d_skip"][l].reshape(1, D_SSM)
        wp, token = get_weights(l, h)
        u, ypool = _mix_in_fwd(h, tied(p["norm_mix"][l:l + 1], token), wp, 0, w_pool[l], p["pool_scale"][l:l + 1])
        sre, sim, yraw = _ssm_fwd(u, bpad, cpad_t.transpose(0, 2, 1), ar[l], ai[l], dskip)
        hm = _mix_out_fwd(yraw, ypool, h, wp, 0, p["b_glu"][l:l + 1])
        h_next, n2, gate_s, up_s = _ffn_fwd(hm, p["norm_ffn"][l:l + 1], wp, 0)
        layers.append(dict(h=h, u=u, ypool=ypool, sre=sre, sim=sim, yraw=yraw, hm=hm, n2=n2, gate_s=gate_s, wp=wp,
                           up_s=up_s, bpad_t=bpad.transpose(0, 2, 1), cpad_t=cpad_t, dskip=dskip))
        h = h_next

    dh, loss, d_norm_final = _final_fwd_bwd(h, p["norm_final"].reshape(1, D_MODEL), target)

    per_layer = {n: [None] * nl for n in ("norm_mix", "w_pool", "pool_scale", "c_re", "c_im", "d_skip", "b_glu",
                                          "norm_ffn", "dar", "dai", "dbbr_t", "dbbi_t")}
    token = None
    for l in reversed(range(nl)):
        s = layers[l]
        wp = s["wp"]
        g1 = lax.empty((1, N_SHARD, P_ROWS, D_MODEL), F32)
        dhm, dg2, dgate_s, dup_s, act_s, dhb = _ffn_bwd_act(dh, s["hm"], tied(p["norm_ffn"][l:l + 1], token),
                                                             s["gate_s"], s["up_s"], wp, 0)
        g1 = _ffn_bwd_w(s["n2"], dgate_s, dup_s, act_s, dhb, g1, 0)
        dyraw, dyp, db_glu, g1 = _mix_out_bwd(dhm, s["yraw"], s["ypool"], wp, 0, p["b_glu"][l:l + 1], g1)
        dus, dcp, dbp, dar, dai, ddsk = _ssm_bwd(dyraw, s["u"], s["sre"], s["sim"], s["cpad_t"], s["bpad_t"],
                                                  ar[l], ai[l], s["dskip"])
        dup, dwp, dsc = _pool_bwd(dyp, s["u"], w_pool[l], p["pool_scale"][l:l + 1])
        dh, dg1, g1 = _mix_in_bwd(dup, dus, s["h"], dhm, p["norm_mix"][l:l + 1], wp, 0, g1)
        token = put_grads(l, g1)
        dc_re, dc_im = _unpad_pairs(dcp.transpose(0, 2, 1))
        dbbr, dbbi = _unpad_pairs(dbp)
        per_layer["norm_mix"][l] = dg1[0]
        per_layer["w_pool"][l] = dwp
        per_layer["pool_scale"][l] = dsc[0]
        per_layer["c_re"][l] = dc_re.transpose(0, 2, 1)
        per_layer["c_im"][l] = -dc_im.transpose(0, 2, 1)
        per_layer["d_skip"][l] = ddsk.reshape(N_SSM_GROUPS, SSM_GROUP)
        per_layer["b_glu"][l] = db_glu[0]
        per_layer["norm_ffn"][l] = dg2[0]
        per_layer["dar"][l] = dar.reshape(N_SSM_GROUPS, 1, SSM_STATE)
        per_layer["dai"][l] = dai.reshape(N_SSM_GROUPS, 1, SSM_STATE)
        per_layer["dbbr_t"][l] = dbbr.transpose(0, 2, 1)
        per_layer["dbbi_t"][l] = dbbi.transpose(0, 2, 1)

    st = {n: jnp.stack(v) for n, v in per_layer.items()}
    cat = lambda a: a.reshape((n_rows,) + a.shape[2:])
    dlr, dli, dldt, dbr_t, dbi_t = _disc_bwd(lr, li, ldt, br_t, bi_t, cat(st["dar"]), cat(st["dai"]),
                                              cat(st["dbbr_t"]), cat(st["dbbi_t"]))
    small = {n: st[n] for n in ("norm_mix", "w_pool", "pool_scale", "c_re", "c_im", "d_skip", "b_glu", "norm_ffn")}
    small["lam_re"] = dlr.reshape(nl, N_SSM_GROUPS, SSM_STATE)
    small["lam_im"] = dli.reshape(nl, N_SSM_GROUPS, SSM_STATE)
    small["log_dt"] = dldt.reshape(nl, N_SSM_GROUPS)
    small["b_re"] = dbr_t.transpose(0, 2, 1).reshape(nl, N_SSM_GROUPS, SSM_STATE, SSM_GROUP)
    small["b_im"] = dbi_t.transpose(0, 2, 1).reshape(nl, N_SSM_GROUPS, SSM_STATE, SSM_GROUP)
    small["norm_final"] = d_norm_final[0]
    return loss, dh, small


def _flatten_small(d):
    flat = jnp.concatenate([d[n].reshape(-1) for n in _SMALL])
    n_rows = -(-flat.shape[0] // (32 * D_MODEL)) * 32
    return jnp.pad(flat, (0, n_rows * D_MODEL - flat.shape[0])).reshape(n_rows, D_MODEL)


def _split_small(flat, like):
    flat = flat.reshape(-1)
    out, at = {}, 0
    for n in _SMALL:
        size = like[n].size
        out[n] = flat[at:at + size].reshape(like[n].shape)
        at += size
    return out


def kernel(x, norm_mix, w_in, w_pool, pool_scale, lam_re, lam_im, log_dt, b_re, b_im, c_re, c_im, d_skip, w_glu, b_glu, w_out, norm_ffn, w_gate, w_up, w_down, norm_final, loss_target, m_norm_mix, m_w_in, m_w_pool, m_pool_scale, m_lam_re, m_lam_im, m_log_dt, m_b_re, m_b_im, m_c_re, m_c_im, m_d_skip, m_w_glu, m_b_glu, m_w_out, m_norm_ffn, m_w_gate, m_w_up, m_w_down, m_norm_final, v_norm_mix, v_w_in, v_w_pool, v_pool_scale, v_lam_re, v_lam_im, v_log_dt, v_b_re, v_b_im, v_c_re, v_c_im, v_d_skip, v_w_glu, v_b_glu, v_w_out, v_norm_ffn, v_w_gate, v_w_up, v_w_down, v_norm_final):
    given = dict(locals())
    w = {n: given[n] for n in _WEIGHTS}
    m = {n: given["m_" + n] for n in _WEIGHTS}
    v = {n: given["v_" + n] for n in _WEIGHTS}
    ids = jnp.stack([lax.axis_index("c"), 2 * lax.axis_index("x") + lax.axis_index("y")]).astype(jnp.int32)

    t_names = ("w_gate", "w_up")
    tr = lambda a: a.transpose(0, 2, 1)
    for d in (w, m, v):
        d.update({n: tr(d[n]) for n in t_names})

    nl = norm_mix.shape[0]
    packed = [_pack_weights(ids, l, w["w_in"], w["w_glu"], w["w_out"], w["w_down"], w["w_gate"], w["w_up"])
              for l in range(nl)]
    first = _all_gather_weights(packed[0])
    started, last = {}, first
    for l in range(1, nl):
        started[l] = _ag_start(f"ag_start_{l}", packed[l], last)
        last = started[l][3]
    first_token = last[:1, :1] if started else None

    def get_weights(l, after):
        if l == 0:
            return first, first_token
        send_sems, recv_sems, buf, _ = started[l]
        return _ag_forward(_ag_wait(f"ag_wait_{l}", send_sems, recv_sems, buf, after)), None

    in_flight, reduced = {}, [None] * nl

    def finish(l, after):
        send_sems, recv_sems, t, land, own = in_flight.pop(l)
        land = _rs_chips_wait(f"rs_chips_wait_{l}", send_sems, recv_sems, t, land, after)
        reduced[l] = _rs_exchange(_rs_sum(ids, own, land, RS_ROW_TILE))

    def put_grads(l, g):
        own, t = _rs_add("rs_add", ids, g, _rs_to_sibling(g), RS_ROW_TILE)
        send_sems, recv_sems, t, land, token = _rs_chips_start(f"rs_chips_start_{l}", t)
        in_flight[l] = (send_sems, recv_sems, t, land, own)
        if l + 1 in in_flight:
            finish(l + 1, token)
        return token[:1, :1]

    loss, grad_x, small = _local_step(x[0], loss_target[0], {n: w[n] for n in _SMALL}, get_weights, put_grads)
    loss = lax.psum(loss[0, 0], ("x", "y", "c"))
    small_sum = _small_all_reduce(_flatten_small(small))
    finish(0, small_sum)
    gr = jnp.concatenate(reduced, axis=0)

    res = {}
    big = (("w_in", P_IN_BLK, 256, False), ("w_out", P_OUT_BLK, 256, False), ("w_down", P_WD_BLK, 352, False),
           ("w_gate", P_WG_BLK, 352, False), ("w_up", P_WU_BLK, 352, False), ("w_glu", P_GLU_BLK, 128, True))
    for n, (blk, idx), row_tile, glu in big:
        res[n] = _adamw("adamw_" + n, w[n], m[n], v[n], gr, (blk, D_MODEL), blk * idx, row_tile, glu)
    for n in t_names:
        res[n] = tuple(tr(a) for a in res[n])
    flat = [_flatten_small(d)[None] for d in (w, m, v)]
    n_rows = flat[0].shape[1]
    outs = _adamw("adamw_small", *flat, small_sum[None], (n_rows, D_MODEL), 0, n_rows // 4)
    parts = [_split_small(o[0], w) for o in outs]
    for n in _SMALL:
        res[n] = tuple(part[n] for part in parts)

    return (loss, grad_x[None], *[res[n][0] for n in _WEIGHTS], *[res[n][1] for n in _WEIGHTS],
            *[res[n][2] for n in _WEIGHTS], *[res[n][3] for n in _WEIGHTS])
```

```python
import functools
import math

import jax
import jax.numpy as jnp
from jax import lax
from jax.experimental import pallas as pl
from jax.experimental.pallas import tpu as pltpu

F32 = jnp.float32
BF16 = jnp.bfloat16

D_MODEL = 1024
D_POOL = 512
D_SSM = 512
POOL_WINDOWS = (2, 4, 8, 16)
POOL_GROUP = 128
POOL_HALO = 16
N_SSM_GROUPS = 32
SSM_GROUP = 16
SSM_STATE = 64
N_STATE = N_SSM_GROUPS * SSM_STATE
N_PAIRS = N_SSM_GROUPS // 2
D_FF = 2816
N_SHARD = 4
FF_SHARD = D_FF // N_SHARD
RMS_EPS = 1e-6

ADAM_LR = 0.001
ADAM_B1 = 0.9
ADAM_B2 = 0.999
ADAM_EPS = 1e-08
ADAM_WD = 0.01
ADAM_STEP = 10

P_ROWS = 2816
P_WD_BLK = (704, 0)
P_WG_BLK = (704, 1)
P_WU_BLK = (704, 2)
P_FF_ROWS = 2112
P_GLU_BLK = (64, 33)
P_GLU_PAD = 192
P_IN_BLK = (256, 9)
P_OUT_BLK = (256, 10)

SUBLANES = 8
VMEM_LIMIT = 56 * 1024 * 1024

TM = 512
TM_FFN = 512
TS = 256
SCAN_LANES = 512


def _cparams(n_axes):
    return pltpu.CompilerParams(dimension_semantics=("arbitrary",) * n_axes, vmem_limit_bytes=VMEM_LIMIT)


def _dot(a, b):
    return jnp.dot(a, b, preferred_element_type=F32)


def _dot_nt(a, b):
    return lax.dot_general(a, b, (((1,), (1,)), ((), ())), preferred_element_type=F32)


def _dot_tn(a, b):
    return lax.dot_general(a, b, (((0,), (0,)), ((), ())), preferred_element_type=F32)


def _rms_hat(x):
    r = lax.rsqrt(jnp.mean(x * x, axis=-1, keepdims=True) + RMS_EPS)
    return x * r, r


def _rms_bwd(d_hat, xhat, r):
    return r * (d_hat - xhat * jnp.mean(d_hat * xhat, axis=-1, keepdims=True))


def _sigmoid(x):
    return 1.0 / (1.0 + jnp.exp(-x))


_GELU_C = math.sqrt(2.0 / math.pi)
_GELU_K = 0.044715


def _gelu(x):
    return 0.5 * x * (1.0 + jnp.tanh(_GELU_C * (x + _GELU_K * x * x * x)))


def _gelu_grad(x):
    th = jnp.tanh(_GELU_C * (x + _GELU_K * x * x * x))
    return 0.5 * (1.0 + th) + 0.5 * x * (1.0 - th * th) * _GELU_C * (1.0 + 3.0 * _GELU_K * x * x)


def _glu_weight(ref):
    v = ref[...]
    return jnp.concatenate([v[:, :, :D_SSM], v[:, :, D_SSM:]], axis=1).reshape(D_SSM, D_SSM)


def _glu_pack(w):
    v = w.reshape(N_SHARD, 128, D_SSM)
    return jnp.concatenate([v[:, :64, :], v[:, 64:, :]], axis=2)


def _pool_diff(ext, row0, tm):
    rows = row0 + lax.broadcasted_iota(jnp.int32, (tm, 1), 0)
    outs = []
    for gi, w in enumerate(POOL_WINDOWS):
        e = ext[:, gi * POOL_GROUP:(gi + 1) * POOL_GROUP]
        s = e
        k = 1
        while k < w:
            s = s + pltpu.roll(s, k, 0)
            k *= 2
        inv = 1.0 / jnp.minimum(rows + 1, w).astype(F32)
        outs.append(s[POOL_HALO:, :] * inv - e[POOL_HALO:, :])
    return outs


def _mix_in_fwd(h, g1, wp, layer, w_pool, scale):
    L = h.shape[0]
    tm = min(TM, L)

    def body(h_ref, g_ref, w_ref, wp_ref, sc_ref, u_ref, yp_ref, carry):
        i = pl.program_id(0)

        @pl.when(i == 0)
        def _():
            carry[...] = jnp.zeros_like(carry)

        xhat, _ = _rms_hat(h_ref[...])
        n1 = (xhat * g_ref[...]).astype(BF16)
        u = _dot(n1, w_ref[...].reshape(D_MODEL, D_MODEL))
        u_ref[...] = u
        up = u[:, :D_POOL]
        ext = jnp.concatenate([carry[...], up], axis=0)
        carry[...] = up[tm - POOL_HALO:, :]
        diffs = _pool_diff(ext, i * tm, tm)
        for gi in range(4):
            cols = slice(gi * POOL_GROUP, (gi + 1) * POOL_GROUP)
            yp_ref[:, cols] = _dot(diffs[gi].astype(BF16), wp_ref[gi]) * sc_ref[:, cols]

    blk, idx = P_IN_BLK
    return pl.pallas_call(
        body, name="mix_in_fwd", grid=(L // tm,),
        in_specs=[pl.BlockSpec((tm, D_MODEL), lambda i: (i, 0)),
                  pl.BlockSpec((None, 1, D_MODEL), lambda i: (layer, 0, 0)),
                  pl.BlockSpec((N_SHARD, None, blk, D_MODEL), lambda i: (0, 0, idx, 0)),
                  pl.BlockSpec((None, 4, POOL_GROUP, POOL_GROUP), lambda i: (layer, 0, 0, 0)),
                  pl.BlockSpec((None, 1, D_POOL), lambda i: (layer, 0, 0))],
        out_specs=[pl.BlockSpec((tm, D_MODEL), lambda i: (i, 0)),
                   pl.BlockSpec((tm, D_POOL), lambda i: (i, 0))],
        out_shape=[jax.ShapeDtypeStruct((L, D_MODEL), F32), jax.ShapeDtypeStruct((L, D_POOL), F32)],
        scratch_shapes=[pltpu.VMEM((POOL_HALO, D_POOL), F32)],
        compiler_params=_cparams(1),
    )(h, g1, wp, w_pool, scale)


def _cmul(xr, xi, yr, yi):
    return xr * yr - xi * yi, xr * yi + xi * yr


def _scan_tables(ar, ai, tab, reverse):
    c = ar.shape[1]
    row = lax.broadcasted_iota(jnp.int32, (SUBLANES, c), 0)
    a2r, a2i = _cmul(ar, ai, ar, ai)
    a4r, a4i = _cmul(a2r, a2i, a2r, a2i)
    zero = jnp.zeros((SUBLANES, c), F32)
    for n, (s, pr, pi) in enumerate(((1, ar, ai), (2, a2r, a2i), (4, a4r, a4i))):
        keep = (row < SUBLANES - s) if reverse else (row >= s)
        tab[2 * n] = jnp.where(keep, pr, zero)
        tab[2 * n + 1] = jnp.where(keep, pi, zero)
    cr, ci = ar, ai
    tr, ti = zero, zero
    for n in range(SUBLANES):
        at = (SUBLANES - 1 - n) if reverse else n
        tr = jnp.where(row == at, cr, tr)
        ti = jnp.where(row == at, ci, ti)
        cr, ci = _cmul(cr, ci, ar, ai)
    tab[6] = tr
    tab[7] = ti


def _ssm_fwd(u, layer, bpad, cpad, ar, ai, dskip):
    L = u.shape[0]
    ts = min(TS, L)
    nq = 4
    cq = N_STATE // nq

    def body(u_ref, bp_ref, cp_ref, ar_ref, ai_ref, dsk_ref, sre_ref, sim_ref, y_ref, cr, ci, tab):
        t = pl.program_id(1)

        @pl.when(t == 0)
        def _():
            cr[...] = jnp.zeros_like(cr)
            ci[...] = jnp.zeros_like(ci)
            _scan_tables(ar_ref[...], ai_ref[...], tab, reverse=False)

        uf = u_ref[...]
        ub = uf.astype(BF16)
        for jj in range(4):
            bu = _dot(ub, bp_ref[jj])
            sre_ref[:, jj * 128:(jj + 1) * 128] = bu[:, :128]
            sim_ref[:, jj * 128:(jj + 1) * 128] = bu[:, 128:]

        for cc in range(cq // SCAN_LANES):
            cols = slice(cc * SCAN_LANES, (cc + 1) * SCAN_LANES)
            def step(i, carry, cols=cols):
                c_r, c_i = carry
                r0 = pl.multiple_of(i * SUBLANES, SUBLANES)
                xr = sre_ref[pl.ds(r0, SUBLANES), cols]
                xi = sim_ref[pl.ds(r0, SUBLANES), cols]
                for n, s in enumerate((1, 2, 4)):
                    tr, ti = tab[2 * n, :, cols], tab[2 * n + 1, :, cols]
                    rr = pltpu.roll(xr, s, 0)
                    ri = pltpu.roll(xi, s, 0)
                    xr, xi = xr + tr * rr - ti * ri, xi + tr * ri + ti * rr
                pr, pi = tab[6, :, cols], tab[7, :, cols]
                xr, xi = xr + pr * c_r - pi * c_i, xi + pr * c_i + pi * c_r
                sre_ref[pl.ds(r0, SUBLANES), cols] = xr
                sim_ref[pl.ds(r0, SUBLANES), cols] = xi
                shp = (SUBLANES, SCAN_LANES)
                return (jnp.broadcast_to(xr[SUBLANES - 1:, :], shp), jnp.broadcast_to(xi[SUBLANES - 1:, :], shp))

            c_r, c_i = lax.fori_loop(0, ts // SUBLANES, step, (cr[:, cols], ci[:, cols]), unroll=2)
            cr[:, cols] = c_r
            ci[:, cols] = c_i

        acc = dsk_ref[...] * uf
        for jj in range(4):
            cols = slice(jj * 128, (jj + 1) * 128)
            scat = jnp.concatenate([sre_ref[:, cols], sim_ref[:, cols]], axis=1).astype(BF16)
            acc = acc + _dot(scat, cp_ref[jj])
        y_ref[...] = acc

    return pl.pallas_call(
        body, name="ssm_fwd", grid=(nq, L // ts),
        in_specs=[pl.BlockSpec((ts, 128), lambda q, t: (t, 4 + q)),
                  pl.BlockSpec((None, 4, 128, 256), lambda q, t: (layer, q, 0, 0)),
                  pl.BlockSpec((None, 4, 256, 128), lambda q, t: (layer, q, 0, 0)),
                  pl.BlockSpec((None, 1, cq), lambda q, t: (layer, 0, q)),
                  pl.BlockSpec((None, 1, cq), lambda q, t: (layer, 0, q)),
                  pl.BlockSpec((None, 1, 128), lambda q, t: (layer, 0, q))],
        out_specs=[pl.BlockSpec((ts, cq), lambda q, t: (t, q)),
                   pl.BlockSpec((ts, cq), lambda q, t: (t, q)),
                   pl.BlockSpec((ts, 128), lambda q, t: (t, q))],
        out_shape=[jax.ShapeDtypeStruct((L, N_STATE), F32), jax.ShapeDtypeStruct((L, N_STATE), F32),
                   jax.ShapeDtypeStruct((L, D_SSM), F32)],
        scratch_shapes=[pltpu.VMEM((SUBLANES, cq), F32), pltpu.VMEM((SUBLANES, cq), F32),
                        pltpu.VMEM((8, SUBLANES, cq), F32)],
        compiler_params=_cparams(2),
    )(u, bpad, cpad, ar, ai, dskip)


def _mix_out_fwd(yraw, ypool, h, wp, layer, b_glu):
    L = h.shape[0]
    tm = min(TM, L)

    def body(yr_ref, yp_ref, h_ref, wglu_ref, b_ref, wout_ref, o_ref):
        y = _gelu(yr_ref[...])
        z = _dot(y.astype(BF16), _glu_weight(wglu_ref)) + b_ref[...]
        o = y * _sigmoid(z)
        mix = jnp.concatenate([yp_ref[...], o], axis=1).astype(BF16)
        o_ref[...] = h_ref[...] + _dot(mix, wout_ref[...].reshape(D_MODEL, D_MODEL))

    gb, gi = P_GLU_BLK
    ob, oi = P_OUT_BLK
    return pl.pallas_call(
        body, name="mix_out_fwd", grid=(L // tm,),
        in_specs=[pl.BlockSpec((tm, D_SSM), lambda i: (i, 0)),
                  pl.BlockSpec((tm, D_POOL), lambda i: (i, 0)),
                  pl.BlockSpec((tm, D_MODEL), lambda i: (i, 0)),
                  pl.BlockSpec((N_SHARD, None, gb, D_MODEL), lambda i: (0, 0, gi, 0)),
                  pl.BlockSpec((None, 1, D_SSM), lambda i: (layer, 0, 0)),
                  pl.BlockSpec((N_SHARD, None, ob, D_MODEL), lambda i: (0, 0, oi, 0))],
        out_specs=pl.BlockSpec((tm, D_MODEL), lambda i: (i, 0)),
        out_shape=jax.ShapeDtypeStruct((L, D_MODEL), F32),
        compiler_params=_cparams(1),
    )(yraw, ypool, h, wp, b_glu, wp)


def _ffn_weights(ref, k):
    return ref[k, 0:FF_SHARD, :], ref[k, FF_SHARD:2 * FF_SHARD, :], ref[k, 2 * FF_SHARD:P_FF_ROWS, :]


def _ffn_weight_spec():
    return pl.BlockSpec((N_SHARD, None, P_FF_ROWS, D_MODEL), lambda m, k: (0, 0, 0, 0),
                        pipeline_mode=pl.Buffered(1))


def _ffn_fwd(h, g2, wp, layer):
    L = h.shape[0]
    tm = min(TM_FFN, L)

    def body(h_ref, g_ref, w_ref, o_ref, n2_ref, gate_ref, up_ref):
        k = pl.program_id(1)

        @pl.when(k == 0)
        def _():
            x = h_ref[...]
            xhat, _ = _rms_hat(x)
            n2_ref[...] = (xhat * g_ref[...]).astype(BF16)
            o_ref[...] = x

        wd, wg_t, wu_t = _ffn_weights(w_ref, k)
        n2 = n2_ref[...]
        gate = _dot_nt(n2, wg_t)
        up = _dot_nt(n2, wu_t)
        gate_ref[...] = gate.astype(BF16)
        up_ref[...] = up.astype(BF16)
        act = (gate * _sigmoid(gate) * up).astype(BF16)
        o_ref[...] += _dot(act, wd)

    act_shape = jax.ShapeDtypeStruct((N_SHARD, L, FF_SHARD), BF16)
    return pl.pallas_call(
        body, name="ffn_fwd", grid=(L // tm, N_SHARD),
        in_specs=[pl.BlockSpec((tm, D_MODEL), lambda m, k: (m, 0)),
                  pl.BlockSpec((None, 1, D_MODEL), lambda m, k: (layer, 0, 0)),
                  _ffn_weight_spec()],
        out_specs=[pl.BlockSpec((tm, D_MODEL), lambda m, k: (m, 0)),
                   pl.BlockSpec((tm, D_MODEL), lambda m, k: (m, 0)),
                   pl.BlockSpec((None, tm, FF_SHARD), lambda m, k: (k, m, 0)),
                   pl.BlockSpec((None, tm, FF_SHARD), lambda m, k: (k, m, 0))],
        out_shape=[jax.ShapeDtypeStruct((L, D_MODEL), F32), jax.ShapeDtypeStruct((L, D_MODEL), BF16),
                   act_shape, act_shape],
        compiler_params=_cparams(2),
    )(h, g2, wp)


def _final_fwd_bwd(h, gf, target):
    L = h.shape[0]
    tm = min(TM, L)

    def body(h_ref, g_ref, t_ref, dh_ref, loss_ref, dg_ref):
        i = pl.program_id(0)

        @pl.when(i == 0)
        def _():
            loss_ref[...] = jnp.zeros_like(loss_ref)
            dg_ref[...] = jnp.zeros_like(dg_ref)

        xhat, r = _rms_hat(h_ref[...])
        g = g_ref[...]
        e = xhat * g - t_ref[...]
        loss_ref[...] += 0.5 * jnp.sum(jnp.mean(e * e, axis=-1, keepdims=True), axis=0, keepdims=True)
        dy = e * (1.0 / D_MODEL)
        dg_ref[...] += jnp.sum(dy * xhat, axis=0, keepdims=True)
        dh_ref[...] = _rms_bwd(dy * g, xhat, r)

    return pl.pallas_call(
        body, name="final_fwd_bwd", grid=(L // tm,),
        in_specs=[pl.BlockSpec((tm, D_MODEL), lambda i: (i, 0)),
                  pl.BlockSpec((1, D_MODEL), lambda i: (0, 0)),
                  pl.BlockSpec((tm, D_MODEL), lambda i: (i, 0))],
        out_specs=[pl.BlockSpec((tm, D_MODEL), lambda i: (i, 0)),
                   pl.BlockSpec((1, 1), lambda i: (0, 0)),
                   pl.BlockSpec((1, D_MODEL), lambda i: (0, 0))],
        out_shape=[jax.ShapeDtypeStruct((L, D_MODEL), F32), jax.ShapeDtypeStruct((1, 1), F32),
                   jax.ShapeDtypeStruct((1, D_MODEL), F32)],
        compiler_params=_cparams(1),
    )(h, gf, target)


def _ffn_bwd_act(dh, h, g2, gate_s, up_s, wp, layer):
    L = h.shape[0]
    tm = min(TM_FFN, L)

    def body(dh_ref, h_ref, g_ref, gate_ref, up_ref, w_ref,
             dhm_ref, dg_ref, dgate_ref, dup_ref, act_ref, dhb_ref, dn2):
        m, k = pl.program_id(0), pl.program_id(1)

        @pl.when(jnp.logical_and(m == 0, k == 0))
        def _():
            dg_ref[...] = jnp.zeros_like(dg_ref)

        @pl.when(k == 0)
        def _():
            dhb_ref[...] = dh_ref[...].astype(BF16)
            dn2[...] = jnp.zeros_like(dn2)

        wd, wg_t, wu_t = _ffn_weights(w_ref, k)
        dact = _dot_nt(dhb_ref[...], wd)
        gate = gate_ref[...].astype(F32)
        up = up_ref[...].astype(F32)
        sg = _sigmoid(gate)
        silu = gate * sg
        dgate = (dact * up * (sg * (1.0 + gate * (1.0 - sg)))).astype(BF16)
        dup = (dact * silu).astype(BF16)
        dgate_ref[...] = dgate
        dup_ref[...] = dup
        act_ref[...] = (silu * up).astype(BF16)
        dn2[...] += _dot(dgate, wg_t) + _dot(dup, wu_t)

        @pl.when(k == N_SHARD - 1)
        def _():
            xhat, r = _rms_hat(h_ref[...])
            d = dn2[...]
            dg_ref[...] += jnp.sum(d * xhat, axis=0, keepdims=True)
            dhm_ref[...] = dh_ref[...] + _rms_bwd(d * g_ref[...], xhat, r)

    act_spec = pl.BlockSpec((None, tm, FF_SHARD), lambda m, k: (k, m, 0))
    act_shape = jax.ShapeDtypeStruct((N_SHARD, L, FF_SHARD), BF16)
    row_spec = pl.BlockSpec((tm, D_MODEL), lambda m, k: (m, 0))
    return pl.pallas_call(
        body, name="ffn_bwd_act", grid=(L // tm, N_SHARD),
        in_specs=[row_spec, row_spec,
                  pl.BlockSpec((None, 1, D_MODEL), lambda m, k: (layer, 0, 0)),
                  act_spec, act_spec,
                  _ffn_weight_spec()],
        out_specs=[row_spec,
                   pl.BlockSpec((1, D_MODEL), lambda m, k: (0, 0)),
                   act_spec, act_spec, act_spec, row_spec],
        out_shape=[jax.ShapeDtypeStruct((L, D_MODEL), F32), jax.ShapeDtypeStruct((1, D_MODEL), F32),
                   act_shape, act_shape, act_shape, jax.ShapeDtypeStruct((L, D_MODEL), BF16)],
        scratch_shapes=[pltpu.VMEM((tm, D_MODEL), F32)],
        compiler_params=_cparams(2),
    )(dh, h, g2, gate_s, up_s, wp)


def _ffn_bwd_w(n2, dgate_s, dup_s, act_s, dhb, gbuf):
    L = n2.shape[0]
    tm = min(TM_FFN, L)

    def body(n2_ref, dgate_ref, dup_ref, act_ref, dhb_ref, g_in, g_ref):
        m = pl.program_id(1)

        @pl.when(m == 0)
        def _():
            g_ref[...] = jnp.zeros_like(g_ref)

        n2v = n2_ref[...]
        g_ref[0:FF_SHARD, :] += _dot_tn(act_ref[...], dhb_ref[...])
        g_ref[FF_SHARD:2 * FF_SHARD, :] += _dot_tn(dgate_ref[...], n2v)
        g_ref[2 * FF_SHARD:P_FF_ROWS, :] += _dot_tn(dup_ref[...], n2v)

    act_spec = pl.BlockSpec((None, tm, FF_SHARD), lambda k, m: (k, m, 0))
    row_spec = pl.BlockSpec((tm, D_MODEL), lambda k, m: (m, 0))
    return pl.pallas_call(
        body, name="ffn_bwd_w", grid=(N_SHARD, L // tm),
        in_specs=[row_spec, act_spec, act_spec, act_spec, row_spec, pl.BlockSpec(memory_space=pl.ANY)],
        out_specs=pl.BlockSpec((None, None, P_FF_ROWS, D_MODEL), lambda k, m: (0, k, 0, 0)),
        out_shape=jax.ShapeDtypeStruct(gbuf.shape, F32),
        input_output_aliases={5: 0},
        compiler_params=_cparams(2),
    )(n2, dgate_s, dup_s, act_s, dhb, gbuf)


def _mix_out_bwd(dhm, yraw, ypool, wp, layer, b_glu, gbuf):
    L = dhm.shape[0]
    tm = min(TM, L)

    def body(dhm_ref, yr_ref, yp_ref, wglu_ref, b_ref, wout_ref, g1_in,
             dyr_ref, dyp_ref, db_ref, g1_ref, dwout, dwglu, gpack):
        i = pl.program_id(0)

        @pl.when(i == 0)
        def _():
            db_ref[...] = jnp.zeros_like(db_ref)
            dwout[...] = jnp.zeros_like(dwout)
            dwglu[...] = jnp.zeros_like(dwglu)

        dhb = dhm_ref[...].astype(BF16)
        wglu = _glu_weight(wglu_ref)
        dmix = _dot_nt(dhb, wout_ref[...].reshape(D_MODEL, D_MODEL))
        dyp_ref[...] = dmix[:, :D_POOL]
        d_o = dmix[:, D_POOL:]
        yraw_v = yr_ref[...]
        y = _gelu(yraw_v)
        yb = y.astype(BF16)
        sig = _sigmoid(_dot(yb, wglu) + b_ref[...])
        mix = jnp.concatenate([yp_ref[...], y * sig], axis=1).astype(BF16)
        dwout[...] += _dot_tn(mix, dhb).reshape(N_SHARD, 256, D_MODEL)
        dz = d_o * y * sig * (1.0 - sig)
        dzb = dz.astype(BF16)
        db_ref[...] += jnp.sum(dz, axis=0, keepdims=True)
        dwglu[...] += _dot_tn(yb, dzb)
        dy = d_o * sig + _dot_nt(dzb, wglu)
        dyr_ref[...] = dy * _gelu_grad(yraw_v)

        @pl.when(i == n_steps - 1)
        def _():
            gpack[:, :gb, :] = _glu_pack(dwglu[...])
            gpack[:, gb:, :] = jnp.zeros((N_SHARD, P_GLU_PAD - gb, D_MODEL), F32)
            pltpu.sync_copy(gpack, g1_ref.at[0, :, pl.ds(gb * gi, P_GLU_PAD), :])
            pltpu.sync_copy(dwout, g1_ref.at[0, :, pl.ds(ob * oi, ob), :])

    gb, gi = P_GLU_BLK
    ob, oi = P_OUT_BLK
    n_steps = L // tm
    return pl.pallas_call(
        body, name="mix_out_bwd", grid=(n_steps,),
        in_specs=[pl.BlockSpec((tm, D_MODEL), lambda i: (i, 0)),
                  pl.BlockSpec((tm, D_SSM), lambda i: (i, 0)),
                  pl.BlockSpec((tm, D_POOL), lambda i: (i, 0)),
                  pl.BlockSpec((N_SHARD, None, gb, D_MODEL), lambda i: (0, 0, gi, 0)),
                  pl.BlockSpec((None, 1, D_SSM), lambda i: (layer, 0, 0)),
                  pl.BlockSpec((N_SHARD, None, ob, D_MODEL), lambda i: (0, 0, oi, 0)),
                  pl.BlockSpec(memory_space=pl.ANY)],
        out_specs=[pl.BlockSpec((tm, D_SSM), lambda i: (i, 0)),
                   pl.BlockSpec((tm, D_POOL), lambda i: (i, 0)),
                   pl.BlockSpec((1, D_SSM), lambda i: (0, 0)),
                   pl.BlockSpec(memory_space=pl.ANY)],
        out_shape=[jax.ShapeDtypeStruct((L, D_SSM), F32), jax.ShapeDtypeStruct((L, D_POOL), F32),
                   jax.ShapeDtypeStruct((1, D_SSM), F32),
                   jax.ShapeDtypeStruct(gbuf.shape, F32)],
        scratch_shapes=[pltpu.VMEM((N_SHARD, ob, D_MODEL), F32), pltpu.VMEM((D_SSM, D_SSM), F32),
                        pltpu.VMEM((N_SHARD, P_GLU_PAD, D_MODEL), F32)],
        input_output_aliases={6: 3},
        compiler_params=_cparams(1),
    )(dhm, yraw, ypool, wp, b_glu, wp, gbuf)


def _ssm_bwd(dyraw, u, sre, sim, layer, cpad_t, bpad_t, ar, ai, dskip):
    L = u.shape[0]
    ts = min(TS, L)
    nt = L // ts
    nq = 4
    cq = N_STATE // nq

    def body(dy_ref, u_ref, sre_ref, sim_ref, ct_ref, bt_ref, ar_ref, ai_ref, dsk_ref,
             du_ref, dcp_ref, dbp_ref, dar_ref, dai_ref, ddsk_ref, gre, gim, cr, ci, tab, accr, acci):
        t = pl.program_id(1)

        @pl.when(t == 0)
        def _():
            for ref in (cr, ci, accr, acci, dcp_ref, dbp_ref, ddsk_ref):
                ref[...] = jnp.zeros_like(ref)
            _scan_tables(ar_ref[...], -ai_ref[...], tab, reverse=True)

        dy = dy_ref[...]
        dyb = dy.astype(BF16)
        uf = u_ref[...]
        ub = uf.astype(BF16)
        for jj in range(4):
            cols = slice(jj * 128, (jj + 1) * 128)
            ds = _dot(dyb, ct_ref[jj])
            gre[:, cols] = ds[:, :128]
            gim[:, cols] = ds[:, 128:]
            scat = jnp.concatenate([sre_ref[:, cols], sim_ref[:, cols]], axis=1).astype(BF16)
            dcp_ref[jj] += _dot_tn(scat, dyb)

        n_grp = ts // SUBLANES
        shp = (SUBLANES, SCAN_LANES)
        last_row = lax.broadcasted_iota(jnp.int32, shp, 0) == SUBLANES - 1
        for cc in range(cq // SCAN_LANES):
            cols = slice(cc * SCAN_LANES, (cc + 1) * SCAN_LANES)
            def step(i, carry, cols=cols):
                c_r, c_i, a_r, a_i = carry
                r0 = pl.multiple_of((n_grp - 1 - i) * SUBLANES, SUBLANES)
                xr = gre[pl.ds(r0, SUBLANES), cols]
                xi = gim[pl.ds(r0, SUBLANES), cols]
                for n, s in enumerate((1, 2, 4)):
                    tr, ti = tab[2 * n, :, cols], tab[2 * n + 1, :, cols]
                    rr = pltpu.roll(xr, SUBLANES - s, 0)
                    ri = pltpu.roll(xi, SUBLANES - s, 0)
                    xr, xi = xr + tr * rr - ti * ri, xi + tr * ri + ti * rr
                qr, qi = tab[6, :, cols], tab[7, :, cols]
                xr, xi = xr + qr * c_r - qi * c_i, xi + qr * c_i + qi * c_r
                gre[pl.ds(r0, SUBLANES), cols] = xr
                gim[pl.ds(r0, SUBLANES), cols] = xi
                nr = jnp.where(last_row, c_r, pltpu.roll(xr, SUBLANES - 1, 0))
                ni = jnp.where(last_row, c_i, pltpu.roll(xi, SUBLANES - 1, 0))
                sr = sre_ref[pl.ds(r0, SUBLANES), cols]
                si = sim_ref[pl.ds(r0, SUBLANES), cols]
                a_r = a_r + sr * nr + si * ni
                a_i = a_i + sr * ni - si * nr
                return (jnp.broadcast_to(xr[:1, :], shp), jnp.broadcast_to(xi[:1, :], shp), a_r, a_i)

            c_r, c_i, a_r, a_i = lax.fori_loop(
                0, n_grp, step, (cr[:, cols], ci[:, cols], accr[:, cols], acci[:, cols]), unroll=2)
            cr[:, cols] = c_r
            ci[:, cols] = c_i
            accr[:, cols] = a_r
            acci[:, cols] = a_i

        acc = dsk_ref[...] * dy
        for jj in range(4):
            cols = slice(jj * 128, (jj + 1) * 128)
            gcat = jnp.concatenate([gre[:, cols], gim[:, cols]], axis=1).astype(BF16)
            acc = acc + _dot(gcat, bt_ref[jj])
            dbp_ref[jj] += _dot_tn(ub, gcat)
        du_ref[...] = acc
        ddsk_ref[...] += jnp.sum(dy * uf, axis=0, keepdims=True)

        @pl.when(t == nt - 1)
        def _():
            dar_ref[...] = jnp.sum(accr[...], axis=0, keepdims=True)
            dai_ref[...] = jnp.sum(acci[...], axis=0, keepdims=True)

    f32_scr = lambda *s: pltpu.VMEM(s, F32)
    return pl.pallas_call(
        body, name="ssm_bwd", grid=(nq, nt),
        in_specs=[pl.BlockSpec((ts, 128), lambda q, t: (nt - 1 - t, q)),
                  pl.BlockSpec((ts, 128), lambda q, t: (nt - 1 - t, 4 + q)),
                  pl.BlockSpec((ts, cq), lambda q, t: (nt - 1 - t, q)),
                  pl.BlockSpec((ts, cq), lambda q, t: (nt - 1 - t, q)),
                  pl.BlockSpec((None, 4, 128, 256), lambda q, t: (layer, q, 0, 0)),
                  pl.BlockSpec((None, 4, 256, 128), lambda q, t: (layer, q, 0, 0)),
                  pl.BlockSpec((None, 1, cq), lambda q, t: (layer, 0, q)),
                  pl.BlockSpec((None, 1, cq), lambda q, t: (layer, 0, q)),
                  pl.BlockSpec((None, 1, 128), lambda q, t: (layer, 0, q))],
        out_specs=[pl.BlockSpec((ts, 128), lambda q, t: (nt - 1 - t, q)),
                   pl.BlockSpec((4, 256, 128), lambda q, t: (q, 0, 0)),
                   pl.BlockSpec((4, 128, 256), lambda q, t: (q, 0, 0)),
                   pl.BlockSpec((1, cq), lambda q, t: (0, q)),
                   pl.BlockSpec((1, cq), lambda q, t: (0, q)),
                   pl.BlockSpec((1, 128), lambda q, t: (0, q))],
        out_shape=[jax.ShapeDtypeStruct((L, D_SSM), F32),
                   jax.ShapeDtypeStruct((N_PAIRS, 256, 128), F32), jax.ShapeDtypeStruct((N_PAIRS, 128, 256), F32),
                   jax.ShapeDtypeStruct((1, N_STATE), F32), jax.ShapeDtypeStruct((1, N_STATE), F32),
                   jax.ShapeDtypeStruct((1, D_SSM), F32)],
        scratch_shapes=[f32_scr(ts, cq), f32_scr(ts, cq), f32_scr(SUBLANES, cq), f32_scr(SUBLANES, cq),
                        f32_scr(8, SUBLANES, cq), f32_scr(SUBLANES, cq), f32_scr(SUBLANES, cq)],
        compiler_params=_cparams(2),
    )(dyraw, u, sre, sim, cpad_t, bpad_t, ar, ai, dskip)


def _pool_bwd(dyp, u, layer, w_pool, scale):
    L = u.shape[0]
    tm = min(TM, L)
    nt = L // tm
    halo_per_tile = tm // POOL_HALO

    def body(dyp_ref, u_ref, halo_ref, wp_ref, sc_ref, du_ref, dwp_ref, dsc_ref, carry):
        i = pl.program_id(0)
        tile = nt - 1 - i

        @pl.when(i == 0)
        def _():
            carry[...] = jnp.zeros_like(carry)
            dwp_ref[...] = jnp.zeros_like(dwp_ref)
            dsc_ref[...] = jnp.zeros_like(dsc_ref)

        up = u_ref[...]
        halo = jnp.where(tile > 0, halo_ref[...], jnp.zeros_like(halo_ref))
        diffs = _pool_diff(jnp.concatenate([halo, up], axis=0), tile * tm, tm)
        rows = tile * tm + lax.broadcasted_iota(jnp.int32, (tm, 1), 0)
        n_ext = tm + POOL_HALO
        for gi, w in enumerate(POOL_WINDOWS):
            cols = slice(gi * POOL_GROUP, (gi + 1) * POOL_GROUP)
            db = diffs[gi].astype(BF16)
            dyp = dyp_ref[:, cols]
            dsc_ref[:, cols] += jnp.sum(dyp * _dot(db, wp_ref[gi]), axis=0, keepdims=True)
            dp = (dyp * sc_ref[:, cols]).astype(BF16)
            ddiff = _dot_nt(dp, wp_ref[gi])
            dwp_ref[gi] += _dot_tn(db, dp)
            e = ddiff * (1.0 / jnp.minimum(rows + 1, w).astype(F32))
            s = jnp.concatenate([e, carry[:, cols]], axis=0)
            k = 1
            while k < w:
                s = s + pltpu.roll(s, n_ext - k, 0)
                k *= 2
            du_ref[:, cols] = s[:tm, :] - ddiff
            carry[:, cols] = e[:POOL_HALO, :]

    return pl.pallas_call(
        body, name="pool_bwd", grid=(nt,),
        in_specs=[pl.BlockSpec((tm, D_POOL), lambda i: (nt - 1 - i, 0)),
                  pl.BlockSpec((tm, D_POOL), lambda i: (nt - 1 - i, 0)),
                  pl.BlockSpec((POOL_HALO, D_POOL), lambda i: (jnp.maximum((nt - 1 - i) * halo_per_tile - 1, 0), 0)),
                  pl.BlockSpec((None, 4, POOL_GROUP, POOL_GROUP), lambda i: (layer, 0, 0, 0)),
                  pl.BlockSpec((None, 1, D_POOL), lambda i: (layer, 0, 0))],
        out_specs=[pl.BlockSpec((tm, D_POOL), lambda i: (nt - 1 - i, 0)),
                   pl.BlockSpec((4, POOL_GROUP, POOL_GROUP), lambda i: (0, 0, 0)),
                   pl.BlockSpec((1, D_POOL), lambda i: (0, 0))],
        out_shape=[jax.ShapeDtypeStruct((L, D_POOL), F32),
                   jax.ShapeDtypeStruct((4, POOL_GROUP, POOL_GROUP), F32),
                   jax.ShapeDtypeStruct((1, D_POOL), F32)],
        scratch_shapes=[pltpu.VMEM((POOL_HALO, D_POOL), F32)],
        compiler_params=_cparams(1),
    )(dyp, u, u, w_pool, scale)


def _mix_in_bwd(dup, dus, h, dhm, g1, wp, layer, gbuf):
    L = h.shape[0]
    tm = min(TM, L)
    n_steps = L // tm
    blk, idx = P_IN_BLK

    def body(dup_ref, dus_ref, h_ref, dhm_ref, g_ref, w_ref, g1_in, dh_ref, dg_ref, g1_ref, dwin):
        i = pl.program_id(0)

        @pl.when(i == 0)
        def _():
            dg_ref[...] = jnp.zeros_like(dg_ref)
            dwin[...] = jnp.zeros_like(dwin)

        du = jnp.concatenate([dup_ref[...], dus_ref[...]], axis=1).astype(BF16)
        dn1 = _dot_nt(du, w_ref[...].reshape(D_MODEL, D_MODEL))
        xhat, r = _rms_hat(h_ref[...])
        g = g_ref[...]
        n1 = (xhat * g).astype(BF16)
        dwin[...] += _dot_tn(n1, du).reshape(N_SHARD, blk, D_MODEL)
        dg_ref[...] += jnp.sum(dn1 * xhat, axis=0, keepdims=True)
        dh_ref[...] = dhm_ref[...] + _rms_bwd(dn1 * g, xhat, r)

        @pl.when(i == n_steps - 1)
        def _():
            pltpu.sync_copy(dwin, g1_ref.at[0, :, pl.ds(blk * idx, blk), :])

    row_spec = pl.BlockSpec((tm, D_MODEL), lambda i: (i, 0))
    half_spec = pl.BlockSpec((tm, D_POOL), lambda i: (i, 0))
    return pl.pallas_call(
        body, name="mix_in_bwd", grid=(n_steps,),
        in_specs=[half_spec, half_spec, row_spec, row_spec,
                  pl.BlockSpec((None, 1, D_MODEL), lambda i: (layer, 0, 0)),
                  pl.BlockSpec((N_SHARD, None, blk, D_MODEL), lambda i: (0, 0, idx, 0)),
                  pl.BlockSpec(memory_space=pl.ANY)],
        out_specs=[row_spec, pl.BlockSpec((1, D_MODEL), lambda i: (0, 0)), pl.BlockSpec(memory_space=pl.ANY)],
        out_shape=[jax.ShapeDtypeStruct((L, D_MODEL), F32), jax.ShapeDtypeStruct((1, D_MODEL), F32),
                   jax.ShapeDtypeStruct(gbuf.shape, F32)],
        scratch_shapes=[pltpu.VMEM((N_SHARD, blk, D_MODEL), F32)],
        input_output_aliases={6: 2},
        compiler_params=_cparams(1),
    )(dup, dus, h, dhm, g1, wp, gbuf)


def _disc_math(lr, li, ldt, br_t, bi_t):
    dt = jnp.exp(ldt)
    mag = jnp.exp(lr * dt)
    ang = li * dt
    ar = mag * jnp.cos(ang)
    ai = mag * jnp.sin(ang)
    den = lr * lr + li * li
    nr, ni = ar - 1.0, ai
    cr = (nr * lr + ni * li) / den
    ci = (ni * lr - nr * li) / den
    return ar, ai, cr * br_t - ci * bi_t, cr * bi_t + ci * br_t


def _disc_fwd(lr, li, ldt, br_t, bi_t):
    def body(lr_ref, li_ref, ldt_ref, br_ref, bi_ref, ar_ref, ai_ref, bbr_ref, bbi_ref):
        ar, ai, bbr, bbi = _disc_math(lr_ref[...], li_ref[...], ldt_ref[...], br_ref[...], bi_ref[...])
        ar_ref[...] = ar
        ai_ref[...] = ai
        bbr_ref[...] = bbr
        bbi_ref[...] = bbi

    shapes = [jax.ShapeDtypeStruct(a.shape, F32) for a in (lr, li, br_t, bi_t)]
    return pl.pallas_call(body, name="ssm_disc_fwd", out_shape=shapes,
                          compiler_params=pltpu.CompilerParams(vmem_limit_bytes=VMEM_LIMIT))(lr, li, ldt, br_t, bi_t)


def _disc_bwd(lr, li, ldt, br_t, bi_t, dar, dai, dbbr, dbbi):
    def body(lr_ref, li_ref, ldt_ref, br_ref, bi_ref, dar_ref, dai_ref, dbbr_ref, dbbi_ref,
             dlr_ref, dli_ref, dldt_ref, dbr_ref, dbi_ref):
        prim = (lr_ref[...], li_ref[...], ldt_ref[...], br_ref[...], bi_ref[...])
        _, pullback = jax.vjp(_disc_math, *prim)
        dlr, dli, dldt, dbr, dbi = pullback((dar_ref[...], dai_ref[...], dbbr_ref[...], dbbi_ref[...]))
        dlr_ref[...] = dlr
        dli_ref[...] = dli
        dldt_ref[...] = dldt
        dbr_ref[...] = dbr
        dbi_ref[...] = dbi

    shapes = [jax.ShapeDtypeStruct(a.shape, F32) for a in (lr, li, ldt, br_t, bi_t)]
    return pl.pallas_call(body, name="ssm_disc_bwd", out_shape=shapes,
                          compiler_params=pltpu.CompilerParams(vmem_limit_bytes=VMEM_LIMIT))(
        lr, li, ldt, br_t, bi_t, dar, dai, dbbr, dbbi)


def _pad_pairs(m_re, m_im):
    def blocks(m):
        v = m.transpose(0, 2, 1).reshape(N_PAIRS, 2, SSM_GROUP, SSM_STATE)
        return jnp.einsum("ab,jahp->jahbp", jnp.eye(2, dtype=m.dtype), v).reshape(N_PAIRS, 32, 128)
    both = jnp.concatenate([blocks(m_re), blocks(m_im)], axis=-1)
    place = jax.nn.one_hot(jnp.arange(N_PAIRS) % 4, 4, dtype=both.dtype)
    return jnp.einsum("jk,jrc->jkrc", place, both).reshape(N_PAIRS, 128, 256)


def _unpad_pairs(x):
    place = jax.nn.one_hot(jnp.arange(N_PAIRS) % 4, 4, dtype=x.dtype)
    both = jnp.einsum("jk,jkrc->jrc", place, x.reshape(N_PAIRS, 4, 32, 256))

    def unblock(v):
        v = v.reshape(N_PAIRS, 2, SSM_GROUP, 2, SSM_STATE)
        d = jnp.einsum("ab,jahbp->jahp", jnp.eye(2, dtype=x.dtype), v)
        return d.reshape(N_SSM_GROUPS, SSM_GROUP, SSM_STATE).transpose(0, 2, 1)
    return unblock(both[..., :128]), unblock(both[..., 128:])


def _adamw_math(w, g, m, v):
    m = ADAM_B1 * m + (1.0 - ADAM_B1) * g
    v = ADAM_B2 * v + (1.0 - ADAM_B2) * (g * g)
    m_hat = m / (1.0 - ADAM_B1 ** ADAM_STEP)
    v_hat = v / (1.0 - ADAM_B2 ** ADAM_STEP)
    delta = -ADAM_LR * (m_hat / (jnp.sqrt(v_hat) + ADAM_EPS) + ADAM_WD * w)
    return delta, m, v


def _adamw(name, w, m, v, gbuf, g_block, g_row0, row_tile, glu=False):
    nl, r, c = w.shape
    n_tiles = r // row_tile
    g_rows, g_cols = g_block
    g_tile = g_rows // n_tiles
    g_off = g_row0 // g_tile

    def body(w_ref, m_ref, v_ref, g_ref, go_ref, d_ref, mo_ref, vo_ref):
        g = g_ref[...]
        if glu:
            g = jnp.concatenate([g[:, :D_SSM], g[:, D_SSM:]], axis=0)
        delta, mn, vn = _adamw_math(w_ref[...], g, m_ref[...], v_ref[...])
        go_ref[...] = g
        d_ref[...] = delta
        mo_ref[...] = mn
        vo_ref[...] = vn

    w_spec = pl.BlockSpec((None, row_tile, c), lambda l, j: (l, j, 0))
    shape = jax.ShapeDtypeStruct(w.shape, F32)
    return pl.pallas_call(
        body, name=name, grid=(nl, n_tiles),
        in_specs=[w_spec, w_spec, w_spec, pl.BlockSpec((None, g_tile, g_cols), lambda l, j: (l, g_off + j, 0))],
        out_specs=[w_spec] * 4,
        out_shape=[shape] * 4,
        compiler_params=_cparams(2),
    )(w, m, v, gbuf)


def _pack_weights(ids, layer, w_in, w_glu, w_out, w_down, w_gate_t, w_up_t):
    gb, gi = P_GLU_BLK
    ib, ii = P_IN_BLK
    ob, oi = P_OUT_BLK

    def body(ids_ref, in_ref, glu_ref, out_ref, dn_ref, gate_ref, up_ref, p_ref):
        p_ref[0:FF_SHARD, :] = dn_ref[...].astype(BF16)
        p_ref[FF_SHARD:2 * FF_SHARD, :] = gate_ref[...].astype(BF16)
        p_ref[2 * FF_SHARD:P_FF_ROWS, :] = up_ref[...].astype(BF16)
        g = glu_ref[...]
        p_ref[gb * gi:gb * (gi + 1), :] = jnp.concatenate([g[:gb, :], g[gb:, :]], axis=1).astype(BF16)
        p_ref[gb * (gi + 1):ib * ii, :] = jnp.zeros((P_GLU_PAD - gb, D_MODEL), BF16)
        p_ref[ib * ii:ib * (ii + 1), :] = in_ref[...].astype(BF16)
        p_ref[ob * oi:ob * (oi + 1), :] = out_ref[...].astype(BF16)

    def spec(a):
        return pl.BlockSpec((None,) + a.shape[1:], lambda i, ids_ref: (layer, 0, 0))

    ins = (w_in, w_glu, w_out, w_down, w_gate_t, w_up_t)
    grid_spec = pltpu.PrefetchScalarGridSpec(
        num_scalar_prefetch=1, grid=(1,),
        in_specs=[spec(a) for a in ins],
        out_specs=pl.BlockSpec((None, None, P_ROWS, D_MODEL), lambda i, ids_ref: (ids_ref[1], 0, 0, 0)))
    return pl.pallas_call(
        body, name="pack_weights", grid_spec=grid_spec,
        out_shape=jax.ShapeDtypeStruct((N_SHARD, 1, P_ROWS, D_MODEL), BF16),
        compiler_params=_cparams(1),
    )(ids, *ins)


MESH = pl.DeviceIdType.MESH
_ANY = pl.BlockSpec(memory_space=pl.ANY)
P_HALF = P_ROWS // 2
RS_ROW_TILE = 352


def _mesh_pos():
    return lax.axis_index("x"), lax.axis_index("y"), lax.axis_index("c")


def _other_chips(x, y):
    return [(1 - x, y), (x, 1 - y), (1 - x, 1 - y)]


def _remote(src, dst, send_sems, recv_sems, n, to):
    return pltpu.make_async_remote_copy(src_ref=src, dst_ref=dst, send_sem=send_sems.at[n],
                                        recv_sem=recv_sems.at[n], device_id=to, device_id_type=MESH)


def _all_gather_weights(wp):
    def body(w_in, o, send_sems, recv_sems):
        x, y, c = _mesh_pos()
        k = 2 * x + y
        sib = (x, y, 1 - c)
        chips = _other_chips(x, y)

        def piece(shard, half):
            return o.at[shard, :, pl.ds(half * P_HALF, P_HALF), :]

        sends = []
        for j, (px, py) in enumerate(chips):
            cp = _remote(piece(k, c), piece(k, c), send_sems, recv_sems, j, (px, py, c))
            cp.start()
            sends.append(cp)
        for j, (px, py) in enumerate(chips):
            landed = piece(2 * px + py, c)
            _remote(landed, landed, send_sems, recv_sems, j, (px, py, c)).wait_recv()
            cp = _remote(landed, landed, send_sems, recv_sems, 3 + j, sib)
            cp.start()
            sends.append(cp)
        for j, (px, py) in enumerate(chips):
            passed = piece(2 * px + py, 1 - c)
            _remote(passed, passed, send_sems, recv_sems, 3 + j, sib).wait_recv()
        for cp in sends:
            cp.wait_send()

    return pl.pallas_call(
        body, name="all_gather_weights",
        in_specs=[_ANY], out_specs=_ANY,
        out_shape=jax.ShapeDtypeStruct(wp.shape, BF16),
        scratch_shapes=[pltpu.SemaphoreType.DMA((6,)), pltpu.SemaphoreType.DMA((6,))],
        input_output_aliases={0: 0},
    )(wp)


_HBM = pl.BlockSpec(memory_space=pltpu.HBM)
_SEM = pl.BlockSpec(memory_space=pltpu.SEMAPHORE)
_EFFECT = pltpu.CompilerParams(has_side_effects=pltpu.SideEffectType.DATAFLOW_SIDE_EFFECTING)
_TOKEN = jax.ShapeDtypeStruct((8, 128), F32)


def _in_hbm(a):
    return pltpu.with_memory_space_constraint(a, pltpu.HBM)


def _ag_start(name, wp, after):
    def body(w_ref, after_ref, send_sems, recv_sems, w_thru, token):
        x, y, c = _mesh_pos()
        mine = w_ref.at[2 * x + y, :, pl.ds(c * P_HALF, P_HALF), :]
        for j, (px, py) in enumerate(_other_chips(x, y)):
            _remote(mine, mine, send_sems, recv_sems, j, (px, py, c)).start()
        token[...] = jnp.zeros_like(token)

    return pl.pallas_call(
        body, name=name,
        out_shape=(pltpu.SemaphoreType.DMA((3,)), pltpu.SemaphoreType.DMA((3,)), pltpu.HBM(wp.shape, wp.dtype), _TOKEN),
        in_specs=(_HBM, _ANY), out_specs=(_SEM, _SEM, _HBM, pl.BlockSpec(memory_space=pltpu.VMEM)),
        input_output_aliases={0: 2}, compiler_params=_EFFECT,
    )(_in_hbm(wp), after)


def _ag_wait(name, send_sems, recv_sems, wp, after):
    def body(w_ref, send_sems, recv_sems, after_ref, w_out):
        x, y, c = _mesh_pos()
        mine = w_ref.at[2 * x + y, :, pl.ds(c * P_HALF, P_HALF), :]
        for j, (px, py) in enumerate(_other_chips(x, y)):
            landed = w_ref.at[2 * px + py, :, pl.ds(c * P_HALF, P_HALF), :]
            cp = _remote(mine, landed, send_sems, recv_sems, j, (px, py, c))
            cp.wait_send()
            cp.wait_recv()

    return pl.pallas_call(
        body, name=name, out_shape=pltpu.HBM(wp.shape, wp.dtype),
        in_specs=(_HBM, _SEM, _SEM, _ANY), out_specs=_HBM,
        input_output_aliases={0: 0}, compiler_params=_EFFECT,
    )(wp, send_sems, recv_sems, after)


def _ag_forward(wp):
    def body(w_in, o, send_sems, recv_sems):
        x, y, c = _mesh_pos()
        sib = (x, y, 1 - c)
        chips = _other_chips(x, y)
        sends = []
        for j, (px, py) in enumerate(chips):
            landed = o.at[2 * px + py, :, pl.ds(c * P_HALF, P_HALF), :]
            cp = _remote(landed, landed, send_sems, recv_sems, j, sib)
            cp.start()
            sends.append(cp)
        for j, (px, py) in enumerate(chips):
            passed = o.at[2 * px + py, :, pl.ds((1 - c) * P_HALF, P_HALF), :]
            _remote(passed, passed, send_sems, recv_sems, j, sib).wait_recv()
        for cp in sends:
            cp.wait_send()

    return pl.pallas_call(
        body, name="ag_forward",
        in_specs=[_ANY], out_specs=_ANY,
        out_shape=jax.ShapeDtypeStruct(wp.shape, wp.dtype),
        scratch_shapes=[pltpu.SemaphoreType.DMA((3,)), pltpu.SemaphoreType.DMA((3,))],
        input_output_aliases={0: 0},
    )(wp)


def _rs_chips_start(name, t):
    nl = t.shape[0]

    def body(t_ref, land_ref, send_sems, recv_sems, t_thru, land_thru, token):
        x, y, c = _mesh_pos()
        for j, (px, py) in enumerate(_other_chips(x, y)):
            _remote(t_ref.at[:, 2 * px + py], land_ref.at[j], send_sems, recv_sems, j, (px, py, c)).start()
        token[...] = jnp.zeros_like(token)

    land = lax.empty((3, nl, P_HALF, D_MODEL), BF16)
    return pl.pallas_call(
        body, name=name,
        out_shape=(pltpu.SemaphoreType.DMA((3,)), pltpu.SemaphoreType.DMA((3,)), pltpu.HBM(t.shape, t.dtype),
                   pltpu.HBM(land.shape, land.dtype), _TOKEN),
        in_specs=(_HBM, _HBM), out_specs=(_SEM, _SEM, _HBM, _HBM, pl.BlockSpec(memory_space=pltpu.VMEM)),
        input_output_aliases={0: 2, 1: 3}, compiler_params=_EFFECT,
    )(_in_hbm(t), _in_hbm(land))


def _rs_chips_wait(name, send_sems, recv_sems, t, land, after):
    def body(t_ref, land_ref, send_sems, recv_sems, after_ref, t_dead, land_out):
        x, y, c = _mesh_pos()
        for j, (px, py) in enumerate(_other_chips(x, y)):
            cp = _remote(t_ref.at[:, 2 * px + py], land_ref.at[j], send_sems, recv_sems, j, (px, py, c))
            cp.wait_send()
            cp.wait_recv()

    return pl.pallas_call(
        body, name=name, out_shape=(pltpu.HBM(t.shape, t.dtype), pltpu.HBM(land.shape, land.dtype)),
        in_specs=(_HBM, _HBM, _SEM, _SEM, _ANY), out_specs=(_HBM, _HBM),
        input_output_aliases={0: 0, 1: 1}, compiler_params=_EFFECT,
    )(t, land, send_sems, recv_sems, after)[1]


def _rs_to_sibling(g):
    def body(g_ref, b, send_sems, recv_sems):
        x, y, c = _mesh_pos()
        cp = _remote(g_ref.at[:, :, pl.ds((1 - c) * P_HALF, P_HALF), :], b, send_sems, recv_sems, 0, (x, y, 1 - c))
        cp.start()
        cp.wait()

    nl = g.shape[0]
    return pl.pallas_call(
        body, name="rs_to_sibling",
        in_specs=[_ANY], out_specs=_ANY,
        out_shape=jax.ShapeDtypeStruct((nl, N_SHARD, P_HALF, D_MODEL), F32),
        scratch_shapes=[pltpu.SemaphoreType.DMA((1,)), pltpu.SemaphoreType.DMA((1,))],
    )(g)


def _rs_add(name, ids, g, buf, row_tile):
    nl, _, hr, cols = buf.shape
    n_rt = hr // row_tile

    def body(ids_ref, g_ref, b_ref, own_ref, tb_ref):
        t = g_ref[...] + b_ref[...]
        tb_ref[...] = t.astype(BF16)

        @pl.when(pl.program_id(2) == ids_ref[1])
        def _():
            own_ref[...] = t

    blk = (None, None, row_tile, cols)
    grid_spec = pltpu.PrefetchScalarGridSpec(
        num_scalar_prefetch=1, grid=(nl, n_rt, N_SHARD),
        in_specs=[pl.BlockSpec(blk, lambda l, j, s, ids_ref: (l, s, ids_ref[0] * n_rt + j, 0)),
                  pl.BlockSpec(blk, lambda l, j, s, ids_ref: (l, s, j, 0))],
        out_specs=[pl.BlockSpec((None, row_tile, cols), lambda l, j, s, ids_ref: (l, j, 0)),
                   pl.BlockSpec(blk, lambda l, j, s, ids_ref: (l, s, j, 0))])
    return pl.pallas_call(
        body, name=name, grid_spec=grid_spec,
        out_shape=[jax.ShapeDtypeStruct((nl, hr, cols), F32), jax.ShapeDtypeStruct(buf.shape, BF16)],
        compiler_params=_cparams(3),
    )(ids, g, buf)


def _rs_sum(ids, layer, own, bufb, reduced, row_tile):
    _, hr, cols = own.shape
    n_rt = hr // row_tile

    def body(ids_ref, own_ref, b_ref, reduced_in, f_ref):
        f_ref[...] = ((own_ref[...] + b_ref[0].astype(F32)) + b_ref[1].astype(F32)) + b_ref[2].astype(F32)

    grid_spec = pltpu.PrefetchScalarGridSpec(
        num_scalar_prefetch=1, grid=(n_rt,),
        in_specs=[pl.BlockSpec((None, row_tile, cols), lambda j, ids_ref: (0, j, 0)),
                  pl.BlockSpec((3, None, row_tile, cols), lambda j, ids_ref: (0, 0, j, 0)),
                  pl.BlockSpec(memory_space=pl.ANY)],
        out_specs=pl.BlockSpec((None, row_tile, cols), lambda j, ids_ref: (layer, ids_ref[0] * n_rt + j, 0)))
    return pl.pallas_call(
        body, name="rs_sum", grid_spec=grid_spec,
        out_shape=jax.ShapeDtypeStruct(reduced.shape, F32),
        input_output_aliases={3: 0},
        compiler_params=_cparams(1),
    )(ids, own, bufb, reduced)


def _rs_exchange(f, layer):
    def body(f_in, o, send_sems, recv_sems):
        x, y, c = _mesh_pos()
        mine = o.at[layer, pl.ds(c * P_HALF, P_HALF), :]
        cp = _remote(mine, mine, send_sems, recv_sems, 0, (x, y, 1 - c))
        cp.start()
        cp.wait_send()
        theirs = o.at[layer, pl.ds((1 - c) * P_HALF, P_HALF), :]
        _remote(theirs, theirs, send_sems, recv_sems, 0, (x, y, 1 - c)).wait_recv()

    return pl.pallas_call(
        body, name="rs_exchange",
        in_specs=[_ANY], out_specs=_ANY,
        out_shape=jax.ShapeDtypeStruct(f.shape, F32),
        scratch_shapes=[pltpu.SemaphoreType.DMA((1,)), pltpu.SemaphoreType.DMA((1,))],
        input_output_aliases={0: 0},
    )(f)


def _small_all_reduce(s):
    n_rows = s.shape[0]
    hr = n_rows // 2

    def body(s_ref, o_ref, sibbuf, tbuf, cbuf, fbuf, send_sems, recv_sems):
        x, y, c = _mesh_pos()
        sib = (x, y, 1 - c)
        mine = pl.ds(pl.multiple_of(c * hr, SUBLANES), hr)
        theirs = pl.ds(pl.multiple_of((1 - c) * hr, SUBLANES), hr)
        first = _remote(s_ref.at[theirs], sibbuf, send_sems, recv_sems, 0, sib)
        first.start()
        first.wait()
        tbuf[...] = s_ref[mine, :] + sibbuf[...]
        cps = []
        for j, (px, py) in enumerate(_other_chips(x, y)):
            cp = _remote(tbuf, cbuf.at[j], send_sems, recv_sems, 1 + j, (px, py, c))
            cp.start()
            cps.append(cp)
        for cp in cps:
            cp.wait()
        f = (tbuf[...] + cbuf[1]) + (cbuf[0] + cbuf[2])
        fbuf[...] = f
        o_ref[mine, :] = f
        last = _remote(fbuf, o_ref.at[mine], send_sems, recv_sems, 4, sib)
        last.start()
        last.wait()

    vmem = pl.BlockSpec(memory_space=pltpu.VMEM)
    return pl.pallas_call(
        body, name="small_all_reduce",
        in_specs=[vmem], out_specs=vmem,
        out_shape=jax.ShapeDtypeStruct(s.shape, F32),
        scratch_shapes=[pltpu.VMEM((hr, D_MODEL), F32), pltpu.VMEM((hr, D_MODEL), F32),
                        pltpu.VMEM((3, hr, D_MODEL), F32), pltpu.VMEM((hr, D_MODEL), F32),
                        pltpu.SemaphoreType.DMA((5,)), pltpu.SemaphoreType.DMA((5,))],
        compiler_params=pltpu.CompilerParams(vmem_limit_bytes=VMEM_LIMIT),
    )(s)


_SMALL = ("norm_mix", "w_pool", "pool_scale", "lam_re", "lam_im", "log_dt", "b_re", "b_im", "c_re", "c_im",
          "d_skip", "b_glu", "norm_ffn", "norm_final")
_WEIGHTS = ("norm_mix", "w_in", "w_pool", "pool_scale", "lam_re", "lam_im", "log_dt", "b_re", "b_im", "c_re",
            "c_im", "d_skip", "w_glu", "b_glu", "w_out", "norm_ffn", "w_gate", "w_up", "w_down", "norm_final")


def _local_step(x, target, p, get_weights, put_grads):
    nl = p["norm_mix"].shape[0]

    def tied(a, token):
        return a if token is None else a + token
    n_rows = nl * N_SSM_GROUPS
    lr = p["lam_re"].reshape(n_rows, 1, SSM_STATE)
    li = p["lam_im"].reshape(n_rows, 1, SSM_STATE)
    ldt = p["log_dt"].reshape(n_rows, 1, 1)
    br_t = p["b_re"].reshape(n_rows, SSM_STATE, SSM_GROUP).transpose(0, 2, 1)
    bi_t = p["b_im"].reshape(n_rows, SSM_STATE, SSM_GROUP).transpose(0, 2, 1)
    ar, ai, bbr_t, bbi_t = _disc_fwd(lr, li, ldt, br_t, bi_t)
    ar = ar.reshape(nl, 1, N_STATE)
    ai = ai.reshape(nl, 1, N_STATE)
    bbr = bbr_t.transpose(0, 2, 1).reshape(nl, N_SSM_GROUPS, SSM_STATE, SSM_GROUP)
    bbi = bbi_t.transpose(0, 2, 1).reshape(nl, N_SSM_GROUPS, SSM_STATE, SSM_GROUP)
    w_pool = p["w_pool"].astype(BF16)
    p = dict(p)
    for n in ("norm_mix", "pool_scale", "b_glu", "norm_ffn"):
        p[n] = p[n].reshape(nl, 1, -1)
    swap = lambda a: jnp.swapaxes(a, -1, -2)
    bpad = jax.vmap(_pad_pairs)(bbr, bbi).astype(BF16)
    cpad_t = jax.vmap(_pad_pairs)(swap(p["c_re"]), -swap(p["c_im"])).astype(BF16)
    bpad_t, cpad = swap(bpad), swap(cpad_t)
    dskip = p["d_skip"].reshape(nl, 1, D_SSM)

    layers = []
    h = x
    for l in range(nl):
        wp, token = get_weights(l, h)
        u, ypool = _mix_in_fwd(h, tied(p["norm_mix"], token), wp, l, w_pool, p["pool_scale"])
        sre, sim, yraw = _ssm_fwd(u, l, bpad, cpad, ar, ai, dskip)
        hm = _mix_out_fwd(yraw, ypool, h, wp, l, p["b_glu"])
        h_next, n2, gate_s, up_s = _ffn_fwd(hm, p["norm_ffn"], wp, l)
        layers.append(dict(h=h, u=u, ypool=ypool, sre=sre, sim=sim, yraw=yraw, hm=hm, n2=n2, gate_s=gate_s, wp=wp,
                           up_s=up_s))
        h = h_next

    dh, loss, d_norm_final = _final_fwd_bwd(h, p["norm_final"].reshape(1, D_MODEL), target)

    raw = {n: [None] * nl for n in ("dg1", "dwp", "dsc", "dcp", "dbp", "ddsk", "db_glu", "dg2", "dar", "dai")}
    token = None
    for l in reversed(range(nl)):
        s = layers[l]
        wp = s["wp"]
        g1 = lax.empty((1, N_SHARD, P_ROWS, D_MODEL), F32)
        dhm, dg2, dgate_s, dup_s, act_s, dhb = _ffn_bwd_act(dh, s["hm"], tied(p["norm_ffn"], token),
                                                             s["gate_s"], s["up_s"], wp, l)
        g1 = _ffn_bwd_w(s["n2"], dgate_s, dup_s, act_s, dhb, g1)
        dyraw, dyp, db_glu, g1 = _mix_out_bwd(dhm, s["yraw"], s["ypool"], wp, l, p["b_glu"], g1)
        dus, dcp, dbp, dar, dai, ddsk = _ssm_bwd(dyraw, s["u"], s["sre"], s["sim"], l, cpad_t, bpad_t, ar, ai, dskip)
        dup, dwp, dsc = _pool_bwd(dyp, s["u"], l, w_pool, p["pool_scale"])
        dh, dg1, g1 = _mix_in_bwd(dup, dus, s["h"], dhm, p["norm_mix"], wp, l, g1)
        token = put_grads(l, g1)
        for n, a in (("dg1", dg1), ("dwp", dwp), ("dsc", dsc), ("dcp", dcp), ("dbp", dbp), ("ddsk", ddsk),
                     ("db_glu", db_glu), ("dg2", dg2), ("dar", dar), ("dai", dai)):
            raw[n][l] = a

    st = {n: jnp.stack(v) for n, v in raw.items()}
    dc_re, dc_im = jax.vmap(_unpad_pairs)(swap(st["dcp"]))
    dbbr, dbbi = jax.vmap(_unpad_pairs)(st["dbp"])
    rows = lambda a: a.reshape((n_rows,) + a.shape[2:])
    dlr, dli, dldt, dbr_t, dbi_t = _disc_bwd(lr, li, ldt, br_t, bi_t, st["dar"].reshape(n_rows, 1, SSM_STATE),
                                              st["dai"].reshape(n_rows, 1, SSM_STATE), rows(swap(dbbr)),
                                              rows(swap(dbbi)))
    small = {"norm_mix": st["dg1"][:, 0], "w_pool": st["dwp"], "pool_scale": st["dsc"][:, 0], "c_re": swap(dc_re),
             "c_im": -swap(dc_im), "d_skip": st["ddsk"].reshape(nl, N_SSM_GROUPS, SSM_GROUP),
             "b_glu": st["db_glu"][:, 0], "norm_ffn": st["dg2"][:, 0]}
    small["lam_re"] = dlr.reshape(nl, N_SSM_GROUPS, SSM_STATE)
    small["lam_im"] = dli.reshape(nl, N_SSM_GROUPS, SSM_STATE)
    small["log_dt"] = dldt.reshape(nl, N_SSM_GROUPS)
    small["b_re"] = dbr_t.transpose(0, 2, 1).reshape(nl, N_SSM_GROUPS, SSM_STATE, SSM_GROUP)
    small["b_im"] = dbi_t.transpose(0, 2, 1).reshape(nl, N_SSM_GROUPS, SSM_STATE, SSM_GROUP)
    small["norm_final"] = d_norm_final[0]
    return loss, dh, small


def _flatten_small(d):
    flat = jnp.concatenate([d[n].reshape(-1) for n in _SMALL])
    n_rows = -(-flat.shape[0] // (32 * D_MODEL)) * 32
    return jnp.pad(flat, (0, n_rows * D_MODEL - flat.shape[0])).reshape(n_rows, D_MODEL)


def _split_small(flat, like):
    flat = flat.reshape(-1)
    out, at = {}, 0
    for n in _SMALL:
        size = like[n].size
        out[n] = flat[at:at + size].reshape(like[n].shape)
        at += size
    return out


def kernel(x, norm_mix, w_in, w_pool, pool_scale, lam_re, lam_im, log_dt, b_re, b_im, c_re, c_im, d_skip, w_glu, b_glu, w_out, norm_ffn, w_gate, w_up, w_down, norm_final, loss_target, m_norm_mix, m_w_in, m_w_pool, m_pool_scale, m_lam_re, m_lam_im, m_log_dt, m_b_re, m_b_im, m_c_re, m_c_im, m_d_skip, m_w_glu, m_b_glu, m_w_out, m_norm_ffn, m_w_gate, m_w_up, m_w_down, m_norm_final, v_norm_mix, v_w_in, v_w_pool, v_pool_scale, v_lam_re, v_lam_im, v_log_dt, v_b_re, v_b_im, v_c_re, v_c_im, v_d_skip, v_w_glu, v_b_glu, v_w_out, v_norm_ffn, v_w_gate, v_w_up, v_w_down, v_norm_final):
    given = dict(locals())
    w = {n: given[n] for n in _WEIGHTS}
    m = {n: given["m_" + n] for n in _WEIGHTS}
    v = {n: given["v_" + n] for n in _WEIGHTS}
    ids = jnp.stack([lax.axis_index("c"), 2 * lax.axis_index("x") + lax.axis_index("y")]).astype(jnp.int32)

    t_names = ("w_gate", "w_up")
    tr = lambda a: a.transpose(0, 2, 1)
    for d in (w, m, v):
        d.update({n: tr(d[n]) for n in t_names})

    nl = norm_mix.shape[0]
    packed = [_pack_weights(ids, l, w["w_in"], w["w_glu"], w["w_out"], w["w_down"], w["w_gate"], w["w_up"])
              for l in range(nl)]
    first = _all_gather_weights(packed[0])
    started, last = {}, first
    for l in range(1, nl):
        started[l] = _ag_start(f"ag_start_{l}", packed[l], last)
        last = started[l][3]
    first_token = last[:1, :1] if started else None

    def get_weights(l, after):
        if l == 0:
            return first, first_token
        send_sems, recv_sems, buf, _ = started[l]
        return _ag_forward(_ag_wait(f"ag_wait_{l}", send_sems, recv_sems, buf, after)), None

    in_flight, reduced = {}, [lax.empty((nl, P_ROWS, D_MODEL), F32)]

    def finish(l, after):
        send_sems, recv_sems, t, land, own = in_flight.pop(l)
        land = _rs_chips_wait(f"rs_chips_wait_{l}", send_sems, recv_sems, t, land, after)
        reduced[0] = _rs_exchange(_rs_sum(ids, l, own, land, reduced[0], RS_ROW_TILE), l)

    def put_grads(l, g):
        own, t = _rs_add("rs_add", ids, g, _rs_to_sibling(g), RS_ROW_TILE)
        send_sems, recv_sems, t, land, token = _rs_chips_start(f"rs_chips_start_{l}", t)
        in_flight[l] = (send_sems, recv_sems, t, land, own)
        if l + 1 in in_flight:
            finish(l + 1, token)
        return token[:1, :1]

    loss, grad_x, small = _local_step(x[0], loss_target[0], {n: w[n] for n in _SMALL}, get_weights, put_grads)
    loss = lax.psum(loss[0, 0], ("x", "y", "c"))
    small_sum = _small_all_reduce(_flatten_small(small))
    finish(0, small_sum)
    gr = reduced[0]

    res = {}
    big = (("w_in", P_IN_BLK, 256, False), ("w_out", P_OUT_BLK, 256, False), ("w_down", P_WD_BLK, 352, False),
           ("w_gate", P_WG_BLK, 352, False), ("w_up", P_WU_BLK, 352, False), ("w_glu", P_GLU_BLK, 128, True))
    for n, (blk, idx), row_tile, glu in big:
        res[n] = _adamw("adamw_" + n, w[n], m[n], v[n], gr, (blk, D_MODEL), blk * idx, row_tile, glu)
    for n in t_names:
        res[n] = tuple(tr(a) for a in res[n])
    flat = [_flatten_small(d)[None] for d in (w, m, v)]
    n_rows = flat[0].shape[1]
    outs = _adamw("adamw_small", *flat, small_sum[None], (n_rows, D_MODEL), 0, n_rows // 4)
    parts = [_split_small(o[0], w) for o in outs]
    for n in _SMALL:
        res[n] = tuple(part[n] for part in parts)

    return (loss, grad_x[None], *[res[n][0] for n in _WEIGHTS], *[res[n][1] for n in _WEIGHTS],
            *[res[n][2] for n in _WEIGHTS], *[res[n][3] for n in _WEIGHTS])
```

```python
import functools
import math

import jax
import jax.numpy as jnp
from jax import lax
from jax.experimental import pallas as pl
from jax.experimental.pallas import tpu as pltpu

F32 = jnp.float32
BF16 = jnp.bfloat16

D_MODEL = 1024
D_POOL = 512
D_SSM = 512
POOL_WINDOWS = (2, 4, 8, 16)
POOL_GROUP = 128
POOL_HALO = 16
N_SSM_GROUPS = 32
SSM_GROUP = 16
SSM_STATE = 64
N_STATE = N_SSM_GROUPS * SSM_STATE
N_PAIRS = N_SSM_GROUPS // 2
D_FF = 2816
N_SHARD = 4
FF_SHARD = D_FF // N_SHARD
RMS_EPS = 1e-6

ADAM_LR = 0.001
ADAM_B1 = 0.9
ADAM_B2 = 0.999
ADAM_EPS = 1e-08
ADAM_WD = 0.01
ADAM_STEP = 10

P_ROWS = 2816
P_WD_BLK = (704, 0)
P_WG_BLK = (704, 1)
P_WU_BLK = (704, 2)
P_FF_ROWS = 2112
P_GLU_BLK = (64, 33)
P_GLU_PAD = 192
P_IN_BLK = (256, 9)
P_OUT_BLK = (256, 10)

SUBLANES = 8
VMEM_LIMIT = 56 * 1024 * 1024

TM = 512
TM_FFN = 512
FFN_SUB_ROWS = 128
TS = 256
SCAN_LANES = 512


def _cparams(n_axes):
    return pltpu.CompilerParams(dimension_semantics=("arbitrary",) * n_axes, vmem_limit_bytes=VMEM_LIMIT)


def _dot(a, b):
    return jnp.dot(a, b, preferred_element_type=F32)


def _dot_nt(a, b):
    return lax.dot_general(a, b, (((1,), (1,)), ((), ())), preferred_element_type=F32)


def _dot_tn(a, b):
    return lax.dot_general(a, b, (((0,), (0,)), ((), ())), preferred_element_type=F32)


def _rms_hat(x):
    r = lax.rsqrt(jnp.mean(x * x, axis=-1, keepdims=True) + RMS_EPS)
    return x * r, r


def _rms_bwd(d_hat, xhat, r):
    return r * (d_hat - xhat * jnp.mean(d_hat * xhat, axis=-1, keepdims=True))


def _sigmoid(x):
    return 1.0 / (1.0 + jnp.exp(-x))


_GELU_C = math.sqrt(2.0 / math.pi)
_GELU_K = 0.044715


def _gelu(x):
    return 0.5 * x * (1.0 + jnp.tanh(_GELU_C * (x + _GELU_K * x * x * x)))


def _gelu_grad(x):
    th = jnp.tanh(_GELU_C * (x + _GELU_K * x * x * x))
    return 0.5 * (1.0 + th) + 0.5 * x * (1.0 - th * th) * _GELU_C * (1.0 + 3.0 * _GELU_K * x * x)


def _glu_weight(ref):
    v = ref[...]
    return jnp.concatenate([v[:, :, :D_SSM], v[:, :, D_SSM:]], axis=1).reshape(D_SSM, D_SSM)


def _glu_pack(w):
    v = w.reshape(N_SHARD, 128, D_SSM)
    return jnp.concatenate([v[:, :64, :], v[:, 64:, :]], axis=2)


def _pool_diff(ext, row0, tm):
    rows = row0 + lax.broadcasted_iota(jnp.int32, (tm, 1), 0)
    outs = []
    for gi, w in enumerate(POOL_WINDOWS):
        e = ext[:, gi * POOL_GROUP:(gi + 1) * POOL_GROUP]
        s = e
        k = 1
        while k < w:
            s = s + pltpu.roll(s, k, 0)
            k *= 2
        inv = 1.0 / jnp.minimum(rows + 1, w).astype(F32)
        outs.append(s[POOL_HALO:, :] * inv - e[POOL_HALO:, :])
    return outs


def _mix_in_fwd(h, g1, wp, layer, w_pool, scale):
    L = h.shape[0]
    tm = min(TM, L)

    def body(h_ref, g_ref, w_ref, wp_ref, sc_ref, u_ref, yp_ref, carry):
        i = pl.program_id(0)

        @pl.when(i == 0)
        def _():
            carry[...] = jnp.zeros_like(carry)

        xhat, _ = _rms_hat(h_ref[...])
        n1 = (xhat * g_ref[...]).astype(BF16)
        u = _dot(n1, w_ref[...].reshape(D_MODEL, D_MODEL))
        u_ref[...] = u
        up = u[:, :D_POOL]
        ext = jnp.concatenate([carry[...], up], axis=0)
        carry[...] = up[tm - POOL_HALO:, :]
        diffs = _pool_diff(ext, i * tm, tm)
        for gi in range(4):
            cols = slice(gi * POOL_GROUP, (gi + 1) * POOL_GROUP)
            yp_ref[:, cols] = _dot(diffs[gi].astype(BF16), wp_ref[gi]) * sc_ref[:, cols]

    blk, idx = P_IN_BLK
    return pl.pallas_call(
        body, name="mix_in_fwd", grid=(L // tm,),
        in_specs=[pl.BlockSpec((tm, D_MODEL), lambda i: (i, 0)),
                  pl.BlockSpec((None, 1, D_MODEL), lambda i: (layer, 0, 0)),
                  pl.BlockSpec((N_SHARD, None, blk, D_MODEL), lambda i: (0, 0, idx, 0)),
                  pl.BlockSpec((None, 4, POOL_GROUP, POOL_GROUP), lambda i: (layer, 0, 0, 0)),
                  pl.BlockSpec((None, 1, D_POOL), lambda i: (layer, 0, 0))],
        out_specs=[pl.BlockSpec((tm, D_MODEL), lambda i: (i, 0)),
                   pl.BlockSpec((tm, D_POOL), lambda i: (i, 0))],
        out_shape=[jax.ShapeDtypeStruct((L, D_MODEL), F32), jax.ShapeDtypeStruct((L, D_POOL), F32)],
        scratch_shapes=[pltpu.VMEM((POOL_HALO, D_POOL), F32)],
        compiler_params=_cparams(1),
    )(h, g1, wp, w_pool, scale)


def _cmul(xr, xi, yr, yi):
    return xr * yr - xi * yi, xr * yi + xi * yr


def _scan_tables(ar, ai, tab, reverse):
    c = ar.shape[1]
    row = lax.broadcasted_iota(jnp.int32, (SUBLANES, c), 0)
    a2r, a2i = _cmul(ar, ai, ar, ai)
    a4r, a4i = _cmul(a2r, a2i, a2r, a2i)
    zero = jnp.zeros((SUBLANES, c), F32)
    for n, (s, pr, pi) in enumerate(((1, ar, ai), (2, a2r, a2i), (4, a4r, a4i))):
        keep = (row < SUBLANES - s) if reverse else (row >= s)
        tab[2 * n] = jnp.where(keep, pr, zero)
        tab[2 * n + 1] = jnp.where(keep, pi, zero)
    cr, ci = ar, ai
    tr, ti = zero, zero
    for n in range(SUBLANES):
        at = (SUBLANES - 1 - n) if reverse else n
        tr = jnp.where(row == at, cr, tr)
        ti = jnp.where(row == at, ci, ti)
        cr, ci = _cmul(cr, ci, ar, ai)
    tab[6] = tr
    tab[7] = ti


def _ssm_fwd(u, layer, bpad, cpad, ar, ai, dskip):
    L = u.shape[0]
    ts = min(TS, L)
    nq = 4
    cq = N_STATE // nq

    def body(u_ref, bp_ref, cp_ref, ar_ref, ai_ref, dsk_ref, sre_ref, sim_ref, y_ref, cr, ci, tab):
        t = pl.program_id(1)

        @pl.when(t == 0)
        def _():
            cr[...] = jnp.zeros_like(cr)
            ci[...] = jnp.zeros_like(ci)
            _scan_tables(ar_ref[...], ai_ref[...], tab, reverse=False)

        uf = u_ref[...]
        ub = uf.astype(BF16)
        for jj in range(4):
            bu = _dot(ub, bp_ref[jj])
            sre_ref[:, jj * 128:(jj + 1) * 128] = bu[:, :128]
            sim_ref[:, jj * 128:(jj + 1) * 128] = bu[:, 128:]

        for cc in range(cq // SCAN_LANES):
            cols = slice(cc * SCAN_LANES, (cc + 1) * SCAN_LANES)
            def step(i, carry, cols=cols):
                c_r, c_i = carry
                r0 = pl.multiple_of(i * SUBLANES, SUBLANES)
                xr = sre_ref[pl.ds(r0, SUBLANES), cols]
                xi = sim_ref[pl.ds(r0, SUBLANES), cols]
                for n, s in enumerate((1, 2, 4)):
                    tr, ti = tab[2 * n, :, cols], tab[2 * n + 1, :, cols]
                    rr = pltpu.roll(xr, s, 0)
                    ri = pltpu.roll(xi, s, 0)
                    xr, xi = xr + tr * rr - ti * ri, xi + tr * ri + ti * rr
                pr, pi = tab[6, :, cols], tab[7, :, cols]
                xr, xi = xr + pr * c_r - pi * c_i, xi + pr * c_i + pi * c_r
                sre_ref[pl.ds(r0, SUBLANES), cols] = xr
                sim_ref[pl.ds(r0, SUBLANES), cols] = xi
                shp = (SUBLANES, SCAN_LANES)
                return (jnp.broadcast_to(xr[SUBLANES - 1:, :], shp), jnp.broadcast_to(xi[SUBLANES - 1:, :], shp))

            c_r, c_i = lax.fori_loop(0, ts // SUBLANES, step, (cr[:, cols], ci[:, cols]), unroll=2)
            cr[:, cols] = c_r
            ci[:, cols] = c_i

        acc = dsk_ref[...] * uf
        for jj in range(4):
            cols = slice(jj * 128, (jj + 1) * 128)
            scat = jnp.concatenate([sre_ref[:, cols], sim_ref[:, cols]], axis=1).astype(BF16)
            acc = acc + _dot(scat, cp_ref[jj])
        y_ref[...] = acc

    return pl.pallas_call(
        body, name="ssm_fwd", grid=(nq, L // ts),
        in_specs=[pl.BlockSpec((ts, 128), lambda q, t: (t, 4 + q)),
                  pl.BlockSpec((None, 4, 128, 256), lambda q, t: (layer, q, 0, 0)),
                  pl.BlockSpec((None, 4, 256, 128), lambda q, t: (layer, q, 0, 0)),
                  pl.BlockSpec((None, 1, cq), lambda q, t: (layer, 0, q)),
                  pl.BlockSpec((None, 1, cq), lambda q, t: (layer, 0, q)),
                  pl.BlockSpec((None, 1, 128), lambda q, t: (layer, 0, q))],
        out_specs=[pl.BlockSpec((ts, cq), lambda q, t: (t, q)),
                   pl.BlockSpec((ts, cq), lambda q, t: (t, q)),
                   pl.BlockSpec((ts, 128), lambda q, t: (t, q))],
        out_shape=[jax.ShapeDtypeStruct((L, N_STATE), F32), jax.ShapeDtypeStruct((L, N_STATE), F32),
                   jax.ShapeDtypeStruct((L, D_SSM), F32)],
        scratch_shapes=[pltpu.VMEM((SUBLANES, cq), F32), pltpu.VMEM((SUBLANES, cq), F32),
                        pltpu.VMEM((8, SUBLANES, cq), F32)],
        compiler_params=_cparams(2),
    )(u, bpad, cpad, ar, ai, dskip)


def _mix_out_fwd(yraw, ypool, h, wp, layer, b_glu):
    L = h.shape[0]
    tm = min(TM, L)

    def body(yr_ref, yp_ref, h_ref, wglu_ref, b_ref, wout_ref, o_ref):
        y = _gelu(yr_ref[...])
        z = _dot(y.astype(BF16), _glu_weight(wglu_ref)) + b_ref[...]
        o = y * _sigmoid(z)
        mix = jnp.concatenate([yp_ref[...], o], axis=1).astype(BF16)
        o_ref[...] = h_ref[...] + _dot(mix, wout_ref[...].reshape(D_MODEL, D_MODEL))

    gb, gi = P_GLU_BLK
    ob, oi = P_OUT_BLK
    return pl.pallas_call(
        body, name="mix_out_fwd", grid=(L // tm,),
        in_specs=[pl.BlockSpec((tm, D_SSM), lambda i: (i, 0)),
                  pl.BlockSpec((tm, D_POOL), lambda i: (i, 0)),
                  pl.BlockSpec((tm, D_MODEL), lambda i: (i, 0)),
                  pl.BlockSpec((N_SHARD, None, gb, D_MODEL), lambda i: (0, 0, gi, 0)),
                  pl.BlockSpec((None, 1, D_SSM), lambda i: (layer, 0, 0)),
                  pl.BlockSpec((N_SHARD, None, ob, D_MODEL), lambda i: (0, 0, oi, 0))],
        out_specs=pl.BlockSpec((tm, D_MODEL), lambda i: (i, 0)),
        out_shape=jax.ShapeDtypeStruct((L, D_MODEL), F32),
        compiler_params=_cparams(1),
    )(yraw, ypool, h, wp, b_glu, wp)


def _ffn_weights(ref, k):
    return ref[k, 0:FF_SHARD, :], ref[k, FF_SHARD:2 * FF_SHARD, :], ref[k, 2 * FF_SHARD:P_FF_ROWS, :]


def _ffn_weight_spec():
    return pl.BlockSpec((N_SHARD, None, P_FF_ROWS, D_MODEL), lambda m, k: (0, 0, 0, 0),
                        pipeline_mode=pl.Buffered(1))


def _ffn_fwd(h, g2, wp, layer):
    L = h.shape[0]
    tm = min(TM_FFN, L)

    def body(h_ref, g_ref, w_ref, o_ref, n2_ref, gate_ref, up_ref):
        k = pl.program_id(1)

        @pl.when(k == 0)
        def _():
            x = h_ref[...]
            xhat, _ = _rms_hat(x)
            n2_ref[...] = (xhat * g_ref[...]).astype(BF16)
            o_ref[...] = x

        wd, wg_t, wu_t = _ffn_weights(w_ref, k)
        n2 = n2_ref[...]
        gate = _dot_nt(n2, wg_t)
        up = _dot_nt(n2, wu_t)
        gate_ref[...] = gate.astype(BF16)
        up_ref[...] = up.astype(BF16)
        act = (gate * _sigmoid(gate) * up).astype(BF16)
        o_ref[...] += _dot(act, wd)

    act_shape = jax.ShapeDtypeStruct((N_SHARD, L, FF_SHARD), BF16)
    return pl.pallas_call(
        body, name="ffn_fwd", grid=(L // tm, N_SHARD),
        in_specs=[pl.BlockSpec((tm, D_MODEL), lambda m, k: (m, 0)),
                  pl.BlockSpec((None, 1, D_MODEL), lambda m, k: (layer, 0, 0)),
                  _ffn_weight_spec()],
        out_specs=[pl.BlockSpec((tm, D_MODEL), lambda m, k: (m, 0)),
                   pl.BlockSpec((tm, D_MODEL), lambda m, k: (m, 0)),
                   pl.BlockSpec((None, tm, FF_SHARD), lambda m, k: (k, m, 0)),
                   pl.BlockSpec((None, tm, FF_SHARD), lambda m, k: (k, m, 0))],
        out_shape=[jax.ShapeDtypeStruct((L, D_MODEL), F32), jax.ShapeDtypeStruct((L, D_MODEL), BF16),
                   act_shape, act_shape],
        compiler_params=_cparams(2),
    )(h, g2, wp)


def _final_fwd_bwd(h, gf, target):
    L = h.shape[0]
    tm = min(TM, L)

    def body(h_ref, g_ref, t_ref, dh_ref, loss_ref, dg_ref):
        i = pl.program_id(0)

        @pl.when(i == 0)
        def _():
            loss_ref[...] = jnp.zeros_like(loss_ref)
            dg_ref[...] = jnp.zeros_like(dg_ref)

        xhat, r = _rms_hat(h_ref[...])
        g = g_ref[...]
        e = xhat * g - t_ref[...]
        loss_ref[...] += 0.5 * jnp.sum(jnp.mean(e * e, axis=-1, keepdims=True), axis=0, keepdims=True)
        dy = e * (1.0 / D_MODEL)
        dg_ref[...] += jnp.sum(dy * xhat, axis=0, keepdims=True)
        dh_ref[...] = _rms_bwd(dy * g, xhat, r)

    return pl.pallas_call(
        body, name="final_fwd_bwd", grid=(L // tm,),
        in_specs=[pl.BlockSpec((tm, D_MODEL), lambda i: (i, 0)),
                  pl.BlockSpec((1, D_MODEL), lambda i: (0, 0)),
                  pl.BlockSpec((tm, D_MODEL), lambda i: (i, 0))],
        out_specs=[pl.BlockSpec((tm, D_MODEL), lambda i: (i, 0)),
                   pl.BlockSpec((1, 1), lambda i: (0, 0)),
                   pl.BlockSpec((1, D_MODEL), lambda i: (0, 0))],
        out_shape=[jax.ShapeDtypeStruct((L, D_MODEL), F32), jax.ShapeDtypeStruct((1, 1), F32),
                   jax.ShapeDtypeStruct((1, D_MODEL), F32)],
        compiler_params=_cparams(1),
    )(h, gf, target)


def _ffn_bwd_act(dh, h, g2, gate_s, up_s, wp, layer):
    L = h.shape[0]
    tm = min(TM_FFN, L)
    sub = min(FFN_SUB_ROWS, tm)

    def body(dh_ref, h_ref, g_ref, gate_ref, up_ref, w_ref,
             dhm_ref, dg_ref, dgate_ref, dup_ref, act_ref, dhb_ref, dn2):
        m, k = pl.program_id(0), pl.program_id(1)

        @pl.when(jnp.logical_and(m == 0, k == 0))
        def _():
            dg_ref[...] = jnp.zeros_like(dg_ref)

        @pl.when(k == 0)
        def _():
            dhb_ref[...] = dh_ref[...].astype(BF16)
            dn2[...] = jnp.zeros_like(dn2)

        wd, wg_t, wu_t = _ffn_weights(w_ref, k)
        for r in range(tm // sub):
            rows = slice(r * sub, (r + 1) * sub)
            dact = _dot_nt(dhb_ref[rows, :], wd)
            gate = gate_ref[rows, :].astype(F32)
            up = up_ref[rows, :].astype(F32)
            sg = _sigmoid(gate)
            silu = gate * sg
            dgate = (dact * up * (sg * (1.0 + gate * (1.0 - sg)))).astype(BF16)
            dup = (dact * silu).astype(BF16)
            dgate_ref[rows, :] = dgate
            dup_ref[rows, :] = dup
            act_ref[rows, :] = (silu * up).astype(BF16)
            dn2[rows, :] += _dot(dgate, wg_t) + _dot(dup, wu_t)

        @pl.when(k == N_SHARD - 1)
        def _():
            xhat, r = _rms_hat(h_ref[...])
            d = dn2[...]
            dg_ref[...] += jnp.sum(d * xhat, axis=0, keepdims=True)
            dhm_ref[...] = dh_ref[...] + _rms_bwd(d * g_ref[...], xhat, r)

    act_spec = pl.BlockSpec((None, tm, FF_SHARD), lambda m, k: (k, m, 0))
    act_shape = jax.ShapeDtypeStruct((N_SHARD, L, FF_SHARD), BF16)
    row_spec = pl.BlockSpec((tm, D_MODEL), lambda m, k: (m, 0))
    return pl.pallas_call(
        body, name="ffn_bwd_act", grid=(L // tm, N_SHARD),
        in_specs=[row_spec, row_spec,
                  pl.BlockSpec((None, 1, D_MODEL), lambda m, k: (layer, 0, 0)),
                  act_spec, act_spec,
                  _ffn_weight_spec()],
        out_specs=[row_spec,
                   pl.BlockSpec((1, D_MODEL), lambda m, k: (0, 0)),
                   act_spec, act_spec, act_spec, row_spec],
        out_shape=[jax.ShapeDtypeStruct((L, D_MODEL), F32), jax.ShapeDtypeStruct((1, D_MODEL), F32),
                   act_shape, act_shape, act_shape, jax.ShapeDtypeStruct((L, D_MODEL), BF16)],
        scratch_shapes=[pltpu.VMEM((tm, D_MODEL), F32)],
        compiler_params=_cparams(2),
    )(dh, h, g2, gate_s, up_s, wp)


def _ffn_bwd_w(n2, dgate_s, dup_s, act_s, dhb, gbuf):
    L = n2.shape[0]
    tm = min(TM_FFN, L)

    def body(n2_ref, dgate_ref, dup_ref, act_ref, dhb_ref, g_in, g_ref):
        m = pl.program_id(1)

        @pl.when(m == 0)
        def _():
            g_ref[...] = jnp.zeros_like(g_ref)

        n2v = n2_ref[...]
        g_ref[0:FF_SHARD, :] += _dot_tn(act_ref[...], dhb_ref[...])
        g_ref[FF_SHARD:2 * FF_SHARD, :] += _dot_tn(dgate_ref[...], n2v)
        g_ref[2 * FF_SHARD:P_FF_ROWS, :] += _dot_tn(dup_ref[...], n2v)

    act_spec = pl.BlockSpec((None, tm, FF_SHARD), lambda k, m: (k, m, 0))
    row_spec = pl.BlockSpec((tm, D_MODEL), lambda k, m: (m, 0))
    return pl.pallas_call(
        body, name="ffn_bwd_w", grid=(N_SHARD, L // tm),
        in_specs=[row_spec, act_spec, act_spec, act_spec, row_spec, pl.BlockSpec(memory_space=pl.ANY)],
        out_specs=pl.BlockSpec((None, None, P_FF_ROWS, D_MODEL), lambda k, m: (0, k, 0, 0)),
        out_shape=jax.ShapeDtypeStruct(gbuf.shape, F32),
        input_output_aliases={5: 0},
        compiler_params=_cparams(2),
    )(n2, dgate_s, dup_s, act_s, dhb, gbuf)


def _mix_out_bwd(dhm, yraw, ypool, wp, layer, b_glu, gbuf):
    L = dhm.shape[0]
    tm = min(TM, L)

    def body(dhm_ref, yr_ref, yp_ref, wglu_ref, b_ref, wout_ref, g1_in,
             dyr_ref, dyp_ref, db_ref, g1_ref, dwout, dwglu, gpack):
        i = pl.program_id(0)

        @pl.when(i == 0)
        def _():
            db_ref[...] = jnp.zeros_like(db_ref)
            dwout[...] = jnp.zeros_like(dwout)
            dwglu[...] = jnp.zeros_like(dwglu)

        dhb = dhm_ref[...].astype(BF16)
        wglu = _glu_weight(wglu_ref)
        dmix = _dot_nt(dhb, wout_ref[...].reshape(D_MODEL, D_MODEL))
        dyp_ref[...] = dmix[:, :D_POOL]
        d_o = dmix[:, D_POOL:]
        yraw_v = yr_ref[...]
        y = _gelu(yraw_v)
        yb = y.astype(BF16)
        sig = _sigmoid(_dot(yb, wglu) + b_ref[...])
        mix = jnp.concatenate([yp_ref[...], y * sig], axis=1).astype(BF16)
        dwout[...] += _dot_tn(mix, dhb).reshape(N_SHARD, 256, D_MODEL)
        dz = d_o * y * sig * (1.0 - sig)
        dzb = dz.astype(BF16)
        db_ref[...] += jnp.sum(dz, axis=0, keepdims=True)
        dwglu[...] += _dot_tn(yb, dzb)
        dy = d_o * sig + _dot_nt(dzb, wglu)
        dyr_ref[...] = dy * _gelu_grad(yraw_v)

        @pl.when(i == n_steps - 1)
        def _():
            gpack[:, :gb, :] = _glu_pack(dwglu[...])
            gpack[:, gb:, :] = jnp.zeros((N_SHARD, P_GLU_PAD - gb, D_MODEL), F32)
            pltpu.sync_copy(gpack, g1_ref.at[0, :, pl.ds(gb * gi, P_GLU_PAD), :])
            pltpu.sync_copy(dwout, g1_ref.at[0, :, pl.ds(ob * oi, ob), :])

    gb, gi = P_GLU_BLK
    ob, oi = P_OUT_BLK
    n_steps = L // tm
    return pl.pallas_call(
        body, name="mix_out_bwd", grid=(n_steps,),
        in_specs=[pl.BlockSpec((tm, D_MODEL), lambda i: (i, 0)),
                  pl.BlockSpec((tm, D_SSM), lambda i: (i, 0)),
                  pl.BlockSpec((tm, D_POOL), lambda i: (i, 0)),
                  pl.BlockSpec((N_SHARD, None, gb, D_MODEL), lambda i: (0, 0, gi, 0)),
                  pl.BlockSpec((None, 1, D_SSM), lambda i: (layer, 0, 0)),
                  pl.BlockSpec((N_SHARD, None, ob, D_MODEL), lambda i: (0, 0, oi, 0)),
                  pl.BlockSpec(memory_space=pl.ANY)],
        out_specs=[pl.BlockSpec((tm, D_SSM), lambda i: (i, 0)),
                   pl.BlockSpec((tm, D_POOL), lambda i: (i, 0)),
                   pl.BlockSpec((1, D_SSM), lambda i: (0, 0)),
                   pl.BlockSpec(memory_space=pl.ANY)],
        out_shape=[jax.ShapeDtypeStruct((L, D_SSM), F32), jax.ShapeDtypeStruct((L, D_POOL), F32),
                   jax.ShapeDtypeStruct((1, D_SSM), F32),
                   jax.ShapeDtypeStruct(gbuf.shape, F32)],
        scratch_shapes=[pltpu.VMEM((N_SHARD, ob, D_MODEL), F32), pltpu.VMEM((D_SSM, D_SSM), F32),
                        pltpu.VMEM((N_SHARD, P_GLU_PAD, D_MODEL), F32)],
        input_output_aliases={6: 3},
        compiler_params=_cparams(1),
    )(dhm, yraw, ypool, wp, b_glu, wp, gbuf)


def _ssm_bwd(dyraw, u, sre, sim, layer, cpad_t, bpad_t, ar, ai, dskip):
    L = u.shape[0]
    ts = min(TS, L)
    nt = L // ts
    nq = 4
    cq = N_STATE // nq

    def body(dy_ref, u_ref, sre_ref, sim_ref, ct_ref, bt_ref, ar_ref, ai_ref, dsk_ref,
             du_ref, dcp_ref, dbp_ref, dar_ref, dai_ref, ddsk_ref, gre, gim, cr, ci, tab, accr, acci):
        t = pl.program_id(1)

        @pl.when(t == 0)
        def _():
            for ref in (cr, ci, accr, acci, dcp_ref, dbp_ref, ddsk_ref):
                ref[...] = jnp.zeros_like(ref)
            _scan_tables(ar_ref[...], -ai_ref[...], tab, reverse=True)

        dy = dy_ref[...]
        dyb = dy.astype(BF16)
        uf = u_ref[...]
        ub = uf.astype(BF16)
        for jj in range(4):
            cols = slice(jj * 128, (jj + 1) * 128)
            ds = _dot(dyb, ct_ref[jj])
            gre[:, cols] = ds[:, :128]
            gim[:, cols] = ds[:, 128:]
            scat = jnp.concatenate([sre_ref[:, cols], sim_ref[:, cols]], axis=1).astype(BF16)
            dcp_ref[jj] += _dot_tn(scat, dyb)

        n_grp = ts // SUBLANES
        shp = (SUBLANES, SCAN_LANES)
        last_row = lax.broadcasted_iota(jnp.int32, shp, 0) == SUBLANES - 1
        for cc in range(cq // SCAN_LANES):
            cols = slice(cc * SCAN_LANES, (cc + 1) * SCAN_LANES)
            def step(i, carry, cols=cols):
                c_r, c_i, a_r, a_i = carry
                r0 = pl.multiple_of((n_grp - 1 - i) * SUBLANES, SUBLANES)
                xr = gre[pl.ds(r0, SUBLANES), cols]
                xi = gim[pl.ds(r0, SUBLANES), cols]
                for n, s in enumerate((1, 2, 4)):
                    tr, ti = tab[2 * n, :, cols], tab[2 * n + 1, :, cols]
                    rr = pltpu.roll(xr, SUBLANES - s, 0)
                    ri = pltpu.roll(xi, SUBLANES - s, 0)
                    xr, xi = xr + tr * rr - ti * ri, xi + tr * ri + ti * rr
                qr, qi = tab[6, :, cols], tab[7, :, cols]
                xr, xi = xr + qr * c_r - qi * c_i, xi + qr * c_i + qi * c_r
                gre[pl.ds(r0, SUBLANES), cols] = xr
                gim[pl.ds(r0, SUBLANES), cols] = xi
                nr = jnp.where(last_row, c_r, pltpu.roll(xr, SUBLANES - 1, 0))
                ni = jnp.where(last_row, c_i, pltpu.roll(xi, SUBLANES - 1, 0))
                sr = sre_ref[pl.ds(r0, SUBLANES), cols]
                si = sim_ref[pl.ds(r0, SUBLANES), cols]
                a_r = a_r + sr * nr + si * ni
                a_i = a_i + sr * ni - si * nr
                return (jnp.broadcast_to(xr[:1, :], shp), jnp.broadcast_to(xi[:1, :], shp), a_r, a_i)

            c_r, c_i, a_r, a_i = lax.fori_loop(
                0, n_grp, step, (cr[:, cols], ci[:, cols], accr[:, cols], acci[:, cols]), unroll=2)
            cr[:, cols] = c_r
            ci[:, cols] = c_i
            accr[:, cols] = a_r
            acci[:, cols] = a_i

        acc = dsk_ref[...] * dy
        for jj in range(4):
            cols = slice(jj * 128, (jj + 1) * 128)
            gcat = jnp.concatenate([gre[:, cols], gim[:, cols]], axis=1).astype(BF16)
            acc = acc + _dot(gcat, bt_ref[jj])
            dbp_ref[jj] += _dot_tn(ub, gcat)
        du_ref[...] = acc
        ddsk_ref[...] += jnp.sum(dy * uf, axis=0, keepdims=True)

        @pl.when(t == nt - 1)
        def _():
            dar_ref[...] = jnp.sum(accr[...], axis=0, keepdims=True)
            dai_ref[...] = jnp.sum(acci[...], axis=0, keepdims=True)

    f32_scr = lambda *s: pltpu.VMEM(s, F32)
    return pl.pallas_call(
        body, name="ssm_bwd", grid=(nq, nt),
        in_specs=[pl.BlockSpec((ts, 128), lambda q, t: (nt - 1 - t, q)),
                  pl.BlockSpec((ts, 128), lambda q, t: (nt - 1 - t, 4 + q)),
                  pl.BlockSpec((ts, cq), lambda q, t: (nt - 1 - t, q)),
                  pl.BlockSpec((ts, cq), lambda q, t: (nt - 1 - t, q)),
                  pl.BlockSpec((None, 4, 128, 256), lambda q, t: (layer, q, 0, 0)),
                  pl.BlockSpec((None, 4, 256, 128), lambda q, t: (layer, q, 0, 0)),
                  pl.BlockSpec((None, 1, cq), lambda q, t: (layer, 0, q)),
                  pl.BlockSpec((None, 1, cq), lambda q, t: (layer, 0, q)),
                  pl.BlockSpec((None, 1, 128), lambda q, t: (layer, 0, q))],
        out_specs=[pl.BlockSpec((ts, 128), lambda q, t: (nt - 1 - t, q)),
                   pl.BlockSpec((4, 256, 128), lambda q, t: (q, 0, 0)),
                   pl.BlockSpec((4, 128, 256), lambda q, t: (q, 0, 0)),
                   pl.BlockSpec((1, cq), lambda q, t: (0, q)),
                   pl.BlockSpec((1, cq), lambda q, t: (0, q)),
                   pl.BlockSpec((1, 128), lambda q, t: (0, q))],
        out_shape=[jax.ShapeDtypeStruct((L, D_SSM), F32),
                   jax.ShapeDtypeStruct((N_PAIRS, 256, 128), F32), jax.ShapeDtypeStruct((N_PAIRS, 128, 256), F32),
                   jax.ShapeDtypeStruct((1, N_STATE), F32), jax.ShapeDtypeStruct((1, N_STATE), F32),
                   jax.ShapeDtypeStruct((1, D_SSM), F32)],
        scratch_shapes=[f32_scr(ts, cq), f32_scr(ts, cq), f32_scr(SUBLANES, cq), f32_scr(SUBLANES, cq),
                        f32_scr(8, SUBLANES, cq), f32_scr(SUBLANES, cq), f32_scr(SUBLANES, cq)],
        compiler_params=_cparams(2),
    )(dyraw, u, sre, sim, cpad_t, bpad_t, ar, ai, dskip)


def _pool_bwd(dyp, u, layer, w_pool, scale):
    L = u.shape[0]
    tm = min(TM, L)
    nt = L // tm
    halo_per_tile = tm // POOL_HALO

    def body(dyp_ref, u_ref, halo_ref, wp_ref, sc_ref, du_ref, dwp_ref, dsc_ref, carry):
        i = pl.program_id(0)
        tile = nt - 1 - i

        @pl.when(i == 0)
        def _():
            carry[...] = jnp.zeros_like(carry)
            dwp_ref[...] = jnp.zeros_like(dwp_ref)
            dsc_ref[...] = jnp.zeros_like(dsc_ref)

        up = u_ref[...]
        halo = jnp.where(tile > 0, halo_ref[...], jnp.zeros_like(halo_ref))
        diffs = _pool_diff(jnp.concatenate([halo, up], axis=0), tile * tm, tm)
        rows = tile * tm + lax.broadcasted_iota(jnp.int32, (tm, 1), 0)
        n_ext = tm + POOL_HALO
        for gi, w in enumerate(POOL_WINDOWS):
            cols = slice(gi * POOL_GROUP, (gi + 1) * POOL_GROUP)
            db = diffs[gi].astype(BF16)
            dyp = dyp_ref[:, cols]
            dsc_ref[:, cols] += jnp.sum(dyp * _dot(db, wp_ref[gi]), axis=0, keepdims=True)
            dp = (dyp * sc_ref[:, cols]).astype(BF16)
            ddiff = _dot_nt(dp, wp_ref[gi])
            dwp_ref[gi] += _dot_tn(db, dp)
            e = ddiff * (1.0 / jnp.minimum(rows + 1, w).astype(F32))
            s = jnp.concatenate([e, carry[:, cols]], axis=0)
            k = 1
            while k < w:
                s = s + pltpu.roll(s, n_ext - k, 0)
                k *= 2
            du_ref[:, cols] = s[:tm, :] - ddiff
            carry[:, cols] = e[:POOL_HALO, :]

    return pl.pallas_call(
        body, name="pool_bwd", grid=(nt,),
        in_specs=[pl.BlockSpec((tm, D_POOL), lambda i: (nt - 1 - i, 0)),
                  pl.BlockSpec((tm, D_POOL), lambda i: (nt - 1 - i, 0)),
                  pl.BlockSpec((POOL_HALO, D_POOL), lambda i: (jnp.maximum((nt - 1 - i) * halo_per_tile - 1, 0), 0)),
                  pl.BlockSpec((None, 4, POOL_GROUP, POOL_GROUP), lambda i: (layer, 0, 0, 0)),
                  pl.BlockSpec((None, 1, D_POOL), lambda i: (layer, 0, 0))],
        out_specs=[pl.BlockSpec((tm, D_POOL), lambda i: (nt - 1 - i, 0)),
                   pl.BlockSpec((4, POOL_GROUP, POOL_GROUP), lambda i: (0, 0, 0)),
                   pl.BlockSpec((1, D_POOL), lambda i: (0, 0))],
        out_shape=[jax.ShapeDtypeStruct((L, D_POOL), F32),
                   jax.ShapeDtypeStruct((4, POOL_GROUP, POOL_GROUP), F32),
                   jax.ShapeDtypeStruct((1, D_POOL), F32)],
        scratch_shapes=[pltpu.VMEM((POOL_HALO, D_POOL), F32)],
        compiler_params=_cparams(1),
    )(dyp, u, u, w_pool, scale)


def _mix_in_bwd(dup, dus, h, dhm, g1, wp, layer, gbuf):
    L = h.shape[0]
    tm = min(TM, L)
    n_steps = L // tm
    blk, idx = P_IN_BLK

    def body(dup_ref, dus_ref, h_ref, dhm_ref, g_ref, w_ref, g1_in, dh_ref, dg_ref, g1_ref, dwin):
        i = pl.program_id(0)

        @pl.when(i == 0)
        def _():
            dg_ref[...] = jnp.zeros_like(dg_ref)
            dwin[...] = jnp.zeros_like(dwin)

        du = jnp.concatenate([dup_ref[...], dus_ref[...]], axis=1).astype(BF16)
        dn1 = _dot_nt(du, w_ref[...].reshape(D_MODEL, D_MODEL))
        xhat, r = _rms_hat(h_ref[...])
        g = g_ref[...]
        n1 = (xhat * g).astype(BF16)
        dwin[...] += _dot_tn(n1, du).reshape(N_SHARD, blk, D_MODEL)
        dg_ref[...] += jnp.sum(dn1 * xhat, axis=0, keepdims=True)
        dh_ref[...] = dhm_ref[...] + _rms_bwd(dn1 * g, xhat, r)

        @pl.when(i == n_steps - 1)
        def _():
            pltpu.sync_copy(dwin, g1_ref.at[0, :, pl.ds(blk * idx, blk), :])

    row_spec = pl.BlockSpec((tm, D_MODEL), lambda i: (i, 0))
    half_spec = pl.BlockSpec((tm, D_POOL), lambda i: (i, 0))
    return pl.pallas_call(
        body, name="mix_in_bwd", grid=(n_steps,),
        in_specs=[half_spec, half_spec, row_spec, row_spec,
                  pl.BlockSpec((None, 1, D_MODEL), lambda i: (layer, 0, 0)),
                  pl.BlockSpec((N_SHARD, None, blk, D_MODEL), lambda i: (0, 0, idx, 0)),
                  pl.BlockSpec(memory_space=pl.ANY)],
        out_specs=[row_spec, pl.BlockSpec((1, D_MODEL), lambda i: (0, 0)), pl.BlockSpec(memory_space=pl.ANY)],
        out_shape=[jax.ShapeDtypeStruct((L, D_MODEL), F32), jax.ShapeDtypeStruct((1, D_MODEL), F32),
                   jax.ShapeDtypeStruct(gbuf.shape, F32)],
        scratch_shapes=[pltpu.VMEM((N_SHARD, blk, D_MODEL), F32)],
        input_output_aliases={6: 2},
        compiler_params=_cparams(1),
    )(dup, dus, h, dhm, g1, wp, gbuf)


def _disc_math(lr, li, ldt, br_t, bi_t):
    dt = jnp.exp(ldt)
    mag = jnp.exp(lr * dt)
    ang = li * dt
    ar = mag * jnp.cos(ang)
    ai = mag * jnp.sin(ang)
    den = lr * lr + li * li
    nr, ni = ar - 1.0, ai
    cr = (nr * lr + ni * li) / den
    ci = (ni * lr - nr * li) / den
    return ar, ai, cr * br_t - ci * bi_t, cr * bi_t + ci * br_t


def _disc_fwd(lr, li, ldt, br_t, bi_t):
    def body(lr_ref, li_ref, ldt_ref, br_ref, bi_ref, ar_ref, ai_ref, bbr_ref, bbi_ref):
        ar, ai, bbr, bbi = _disc_math(lr_ref[...], li_ref[...], ldt_ref[...], br_ref[...], bi_ref[...])
        ar_ref[...] = ar
        ai_ref[...] = ai
        bbr_ref[...] = bbr
        bbi_ref[...] = bbi

    shapes = [jax.ShapeDtypeStruct(a.shape, F32) for a in (lr, li, br_t, bi_t)]
    return pl.pallas_call(body, name="ssm_disc_fwd", out_shape=shapes,
                          compiler_params=pltpu.CompilerParams(vmem_limit_bytes=VMEM_LIMIT))(lr, li, ldt, br_t, bi_t)


def _disc_bwd(lr, li, ldt, br_t, bi_t, dar, dai, dbbr, dbbi):
    def body(lr_ref, li_ref, ldt_ref, br_ref, bi_ref, dar_ref, dai_ref, dbbr_ref, dbbi_ref,
             dlr_ref, dli_ref, dldt_ref, dbr_ref, dbi_ref):
        prim = (lr_ref[...], li_ref[...], ldt_ref[...], br_ref[...], bi_ref[...])
        _, pullback = jax.vjp(_disc_math, *prim)
        dlr, dli, dldt, dbr, dbi = pullback((dar_ref[...], dai_ref[...], dbbr_ref[...], dbbi_ref[...]))
        dlr_ref[...] = dlr
        dli_ref[...] = dli
        dldt_ref[...] = dldt
        dbr_ref[...] = dbr
        dbi_ref[...] = dbi

    shapes = [jax.ShapeDtypeStruct(a.shape, F32) for a in (lr, li, ldt, br_t, bi_t)]
    return pl.pallas_call(body, name="ssm_disc_bwd", out_shape=shapes,
                          compiler_params=pltpu.CompilerParams(vmem_limit_bytes=VMEM_LIMIT))(
        lr, li, ldt, br_t, bi_t, dar, dai, dbbr, dbbi)


def _pad_pairs(m_re, m_im):
    def blocks(m):
        v = m.transpose(0, 2, 1).reshape(N_PAIRS, 2, SSM_GROUP, SSM_STATE)
        return jnp.einsum("ab,jahp->jahbp", jnp.eye(2, dtype=m.dtype), v).reshape(N_PAIRS, 32, 128)
    both = jnp.concatenate([blocks(m_re), blocks(m_im)], axis=-1)
    place = jax.nn.one_hot(jnp.arange(N_PAIRS) % 4, 4, dtype=both.dtype)
    return jnp.einsum("jk,jrc->jkrc", place, both).reshape(N_PAIRS, 128, 256)


def _unpad_pairs(x):
    place = jax.nn.one_hot(jnp.arange(N_PAIRS) % 4, 4, dtype=x.dtype)
    both = jnp.einsum("jk,jkrc->jrc", place, x.reshape(N_PAIRS, 4, 32, 256))

    def unblock(v):
        v = v.reshape(N_PAIRS, 2, SSM_GROUP, 2, SSM_STATE)
        d = jnp.einsum("ab,jahbp->jahp", jnp.eye(2, dtype=x.dtype), v)
        return d.reshape(N_SSM_GROUPS, SSM_GROUP, SSM_STATE).transpose(0, 2, 1)
    return unblock(both[..., :128]), unblock(both[..., 128:])


def _adamw_math(w, g, m, v):
    m = ADAM_B1 * m + (1.0 - ADAM_B1) * g
    v = ADAM_B2 * v + (1.0 - ADAM_B2) * (g * g)
    m_hat = m / (1.0 - ADAM_B1 ** ADAM_STEP)
    v_hat = v / (1.0 - ADAM_B2 ** ADAM_STEP)
    delta = -ADAM_LR * (m_hat / (jnp.sqrt(v_hat) + ADAM_EPS) + ADAM_WD * w)
    return delta, m, v


def _adamw(name, w, m, v, gbuf, g_block, g_row0, row_tile, glu=False):
    nl, r, c = w.shape
    n_tiles = r // row_tile
    g_rows, g_cols = g_block
    g_tile = g_rows // n_tiles
    g_off = g_row0 // g_tile

    def body(w_ref, m_ref, v_ref, g_ref, go_ref, d_ref, mo_ref, vo_ref):
        g = g_ref[...]
        if glu:
            g = jnp.concatenate([g[:, :D_SSM], g[:, D_SSM:]], axis=0)
        delta, mn, vn = _adamw_math(w_ref[...], g, m_ref[...], v_ref[...])
        go_ref[...] = g
        d_ref[...] = delta
        mo_ref[...] = mn
        vo_ref[...] = vn

    w_spec = pl.BlockSpec((None, row_tile, c), lambda l, j: (l, j, 0))
    shape = jax.ShapeDtypeStruct(w.shape, F32)
    return pl.pallas_call(
        body, name=name, grid=(nl, n_tiles),
        in_specs=[w_spec, w_spec, w_spec, pl.BlockSpec((None, g_tile, g_cols), lambda l, j: (l, g_off + j, 0))],
        out_specs=[w_spec] * 4,
        out_shape=[shape] * 4,
        compiler_params=_cparams(2),
    )(w, m, v, gbuf)


def _pack_weights(ids, layer, w_in, w_glu, w_out, w_down, w_gate_t, w_up_t):
    gb, gi = P_GLU_BLK
    ib, ii = P_IN_BLK
    ob, oi = P_OUT_BLK

    def body(ids_ref, in_ref, glu_ref, out_ref, dn_ref, gate_ref, up_ref, p_ref):
        p_ref[0:FF_SHARD, :] = dn_ref[...].astype(BF16)
        p_ref[FF_SHARD:2 * FF_SHARD, :] = gate_ref[...].astype(BF16)
        p_ref[2 * FF_SHARD:P_FF_ROWS, :] = up_ref[...].astype(BF16)
        g = glu_ref[...]
        p_ref[gb * gi:gb * (gi + 1), :] = jnp.concatenate([g[:gb, :], g[gb:, :]], axis=1).astype(BF16)
        p_ref[gb * (gi + 1):ib * ii, :] = jnp.zeros((P_GLU_PAD - gb, D_MODEL), BF16)
        p_ref[ib * ii:ib * (ii + 1), :] = in_ref[...].astype(BF16)
        p_ref[ob * oi:ob * (oi + 1), :] = out_ref[...].astype(BF16)

    def spec(a):
        return pl.BlockSpec((None,) + a.shape[1:], lambda i, ids_ref: (layer, 0, 0))

    ins = (w_in, w_glu, w_out, w_down, w_gate_t, w_up_t)
    grid_spec = pltpu.PrefetchScalarGridSpec(
        num_scalar_prefetch=1, grid=(1,),
        in_specs=[spec(a) for a in ins],
        out_specs=pl.BlockSpec((None, None, P_ROWS, D_MODEL), lambda i, ids_ref: (ids_ref[1], 0, 0, 0)))
    return pl.pallas_call(
        body, name="pack_weights", grid_spec=grid_spec,
        out_shape=jax.ShapeDtypeStruct((N_SHARD, 1, P_ROWS, D_MODEL), BF16),
        compiler_params=_cparams(1),
    )(ids, *ins)


MESH = pl.DeviceIdType.MESH
_ANY = pl.BlockSpec(memory_space=pl.ANY)
P_HALF = P_ROWS // 2
RS_ROW_TILE = 352


def _mesh_pos():
    return lax.axis_index("x"), lax.axis_index("y"), lax.axis_index("c")


def _other_chips(x, y):
    return [(1 - x, y), (x, 1 - y), (1 - x, 1 - y)]


def _remote(src, dst, send_sems, recv_sems, n, to):
    return pltpu.make_async_remote_copy(src_ref=src, dst_ref=dst, send_sem=send_sems.at[n],
                                        recv_sem=recv_sems.at[n], device_id=to, device_id_type=MESH)


_HBM = pl.BlockSpec(memory_space=pltpu.HBM)
_SEM = pl.BlockSpec(memory_space=pltpu.SEMAPHORE)
_EFFECT = pltpu.CompilerParams(has_side_effects=pltpu.SideEffectType.DATAFLOW_SIDE_EFFECTING)
_TOKEN = jax.ShapeDtypeStruct((8, 128), F32)


def _in_hbm(a):
    return pltpu.with_memory_space_constraint(a, pltpu.HBM)


def _ag_start(name, wp, after):
    def body(w_ref, after_ref, send_sems, recv_sems, w_thru, token):
        x, y, c = _mesh_pos()
        mine = w_ref.at[2 * x + y, :, pl.ds(c * P_HALF, P_HALF), :]
        for j, (px, py) in enumerate(_other_chips(x, y)):
            _remote(mine, mine, send_sems, recv_sems, j, (px, py, c)).start()
        token[...] = jnp.zeros_like(token)

    return pl.pallas_call(
        body, name=name,
        out_shape=(pltpu.SemaphoreType.DMA((3,)), pltpu.SemaphoreType.DMA((3,)), pltpu.HBM(wp.shape, wp.dtype), _TOKEN),
        in_specs=(_HBM, _ANY), out_specs=(_SEM, _SEM, _HBM, pl.BlockSpec(memory_space=pltpu.VMEM)),
        input_output_aliases={0: 2}, compiler_params=_EFFECT,
    )(_in_hbm(wp), after)


def _ag_wait(name, send_sems, recv_sems, wp, after):
    def body(w_ref, send_sems, recv_sems, *rest):
        x, y, c = _mesh_pos()
        mine = w_ref.at[2 * x + y, :, pl.ds(c * P_HALF, P_HALF), :]
        for j, (px, py) in enumerate(_other_chips(x, y)):
            landed = w_ref.at[2 * px + py, :, pl.ds(c * P_HALF, P_HALF), :]
            cp = _remote(mine, landed, send_sems, recv_sems, j, (px, py, c))
            cp.wait_send()
            cp.wait_recv()

    return pl.pallas_call(
        body, name=name, out_shape=pltpu.HBM(wp.shape, wp.dtype),
        in_specs=(_HBM, _SEM, _SEM) + (_ANY,) * len(after), out_specs=_HBM,
        input_output_aliases={0: 0}, compiler_params=_EFFECT,
    )(wp, send_sems, recv_sems, *after)


def _ag_forward(wp):
    def body(w_in, o, send_sems, recv_sems):
        x, y, c = _mesh_pos()
        sib = (x, y, 1 - c)
        chips = _other_chips(x, y)
        sends = []
        for j, (px, py) in enumerate(chips):
            landed = o.at[2 * px + py, :, pl.ds(c * P_HALF, P_HALF), :]
            cp = _remote(landed, landed, send_sems, recv_sems, j, sib)
            cp.start()
            sends.append(cp)
        for j, (px, py) in enumerate(chips):
            passed = o.at[2 * px + py, :, pl.ds((1 - c) * P_HALF, P_HALF), :]
            _remote(passed, passed, send_sems, recv_sems, j, sib).wait_recv()
        for cp in sends:
            cp.wait_send()

    return pl.pallas_call(
        body, name="ag_forward",
        in_specs=[_ANY], out_specs=_ANY,
        out_shape=jax.ShapeDtypeStruct(wp.shape, wp.dtype),
        scratch_shapes=[pltpu.SemaphoreType.DMA((3,)), pltpu.SemaphoreType.DMA((3,))],
        input_output_aliases={0: 0},
    )(wp)


def _rs_chips_start(name, t):
    nl = t.shape[0]

    def body(t_ref, land_ref, send_sems, recv_sems, t_thru, land_thru, token):
        x, y, c = _mesh_pos()
        for j, (px, py) in enumerate(_other_chips(x, y)):
            _remote(t_ref.at[:, 2 * px + py], land_ref.at[j], send_sems, recv_sems, j, (px, py, c)).start()
        token[...] = jnp.zeros_like(token)

    land = lax.empty((3, nl, P_HALF, D_MODEL), BF16)
    return pl.pallas_call(
        body, name=name,
        out_shape=(pltpu.SemaphoreType.DMA((3,)), pltpu.SemaphoreType.DMA((3,)), pltpu.HBM(t.shape, t.dtype),
                   pltpu.HBM(land.shape, land.dtype), _TOKEN),
        in_specs=(_HBM, _HBM), out_specs=(_SEM, _SEM, _HBM, _HBM, pl.BlockSpec(memory_space=pltpu.VMEM)),
        input_output_aliases={0: 2, 1: 3}, compiler_params=_EFFECT,
    )(_in_hbm(t), _in_hbm(land))


def _rs_chips_wait(name, send_sems, recv_sems, t, land, after):
    def body(t_ref, land_ref, send_sems, recv_sems, *rest):
        x, y, c = _mesh_pos()
        for j, (px, py) in enumerate(_other_chips(x, y)):
            cp = _remote(t_ref.at[:, 2 * px + py], land_ref.at[j], send_sems, recv_sems, j, (px, py, c))
            cp.wait_send()
            cp.wait_recv()

    return pl.pallas_call(
        body, name=name, out_shape=(pltpu.HBM(t.shape, t.dtype), pltpu.HBM(land.shape, land.dtype)),
        in_specs=(_HBM, _HBM, _SEM, _SEM) + (_ANY,) * len(after), out_specs=(_HBM, _HBM),
        input_output_aliases={0: 0, 1: 1}, compiler_params=_EFFECT,
    )(t, land, send_sems, recv_sems, *after)[1]


def _rs_sibling_start(name, g):
    nl = g.shape[0]

    def body(g_ref, land_ref, send_sems, recv_sems, g_thru, land_thru, token):
        x, y, c = _mesh_pos()
        _remote(g_ref.at[:, :, pl.ds((1 - c) * P_HALF, P_HALF), :], land_ref, send_sems, recv_sems, 0,
                (x, y, 1 - c)).start()
        token[...] = jnp.zeros_like(token)

    land = lax.empty((nl, N_SHARD, P_HALF, D_MODEL), F32)
    return pl.pallas_call(
        body, name=name,
        out_shape=(pltpu.SemaphoreType.DMA((1,)), pltpu.SemaphoreType.DMA((1,)), pltpu.HBM(g.shape, g.dtype),
                   pltpu.HBM(land.shape, land.dtype), _TOKEN),
        in_specs=(_HBM, _HBM), out_specs=(_SEM, _SEM, _HBM, _HBM, pl.BlockSpec(memory_space=pltpu.VMEM)),
        input_output_aliases={0: 2, 1: 3}, compiler_params=_EFFECT,
    )(_in_hbm(g), _in_hbm(land))


def _rs_sibling_wait(name, send_sems, recv_sems, g, land, after):
    def body(g_ref, land_ref, send_sems, recv_sems, *rest):
        x, y, c = _mesh_pos()
        cp = _remote(g_ref.at[:, :, pl.ds((1 - c) * P_HALF, P_HALF), :], land_ref, send_sems, recv_sems, 0,
                     (x, y, 1 - c))
        cp.wait_send()
        cp.wait_recv()

    return pl.pallas_call(
        body, name=name, out_shape=(pltpu.HBM(g.shape, g.dtype), pltpu.HBM(land.shape, land.dtype)),
        in_specs=(_HBM, _HBM, _SEM, _SEM) + (_ANY,) * len(after), out_specs=(_HBM, _HBM),
        input_output_aliases={0: 0, 1: 1}, compiler_params=_EFFECT,
    )(g, land, send_sems, recv_sems, *after)


def _rs_add(name, ids, g, buf, row_tile):
    nl, _, hr, cols = buf.shape
    n_rt = hr // row_tile

    def body(ids_ref, g_ref, b_ref, own_ref, tb_ref):
        t = g_ref[...] + b_ref[...]
        tb_ref[...] = t.astype(BF16)

        @pl.when(pl.program_id(2) == ids_ref[1])
        def _():
            own_ref[...] = t

    blk = (None, None, row_tile, cols)
    grid_spec = pltpu.PrefetchScalarGridSpec(
        num_scalar_prefetch=1, grid=(nl, n_rt, N_SHARD),
        in_specs=[pl.BlockSpec(blk, lambda l, j, s, ids_ref: (l, s, ids_ref[0] * n_rt + j, 0)),
                  pl.BlockSpec(blk, lambda l, j, s, ids_ref: (l, s, j, 0))],
        out_specs=[pl.BlockSpec((None, row_tile, cols), lambda l, j, s, ids_ref: (l, j, 0)),
                   pl.BlockSpec(blk, lambda l, j, s, ids_ref: (l, s, j, 0))])
    return pl.pallas_call(
        body, name=name, grid_spec=grid_spec,
        out_shape=[jax.ShapeDtypeStruct((nl, hr, cols), F32), jax.ShapeDtypeStruct(buf.shape, BF16)],
        compiler_params=_cparams(3),
    )(ids, g, buf)


def _rs_sum(ids, layer, own, bufb, reduced, row_tile):
    _, hr, cols = own.shape
    n_rt = hr // row_tile

    def body(ids_ref, own_ref, b_ref, reduced_in, f_ref):
        f_ref[...] = ((own_ref[...] + b_ref[0].astype(F32)) + b_ref[1].astype(F32)) + b_ref[2].astype(F32)

    grid_spec = pltpu.PrefetchScalarGridSpec(
        num_scalar_prefetch=1, grid=(n_rt,),
        in_specs=[pl.BlockSpec((None, row_tile, cols), lambda j, ids_ref: (0, j, 0)),
                  pl.BlockSpec((3, None, row_tile, cols), lambda j, ids_ref: (0, 0, j, 0)),
                  pl.BlockSpec(memory_space=pl.ANY)],
        out_specs=pl.BlockSpec((None, row_tile, cols), lambda j, ids_ref: (layer, ids_ref[0] * n_rt + j, 0)))
    return pl.pallas_call(
        body, name="rs_sum", grid_spec=grid_spec,
        out_shape=jax.ShapeDtypeStruct(reduced.shape, F32),
        input_output_aliases={3: 0},
        compiler_params=_cparams(1),
    )(ids, own, bufb, reduced)


def _rs_exchange(f, layer):
    def body(f_in, o, send_sems, recv_sems):
        x, y, c = _mesh_pos()
        mine = o.at[layer, pl.ds(c * P_HALF, P_HALF), :]
        cp = _remote(mine, mine, send_sems, recv_sems, 0, (x, y, 1 - c))
        cp.start()
        cp.wait_send()
        theirs = o.at[layer, pl.ds((1 - c) * P_HALF, P_HALF), :]
        _remote(theirs, theirs, send_sems, recv_sems, 0, (x, y, 1 - c)).wait_recv()

    return pl.pallas_call(
        body, name="rs_exchange",
        in_specs=[_ANY], out_specs=_ANY,
        out_shape=jax.ShapeDtypeStruct(f.shape, F32),
        scratch_shapes=[pltpu.SemaphoreType.DMA((1,)), pltpu.SemaphoreType.DMA((1,))],
        input_output_aliases={0: 0},
    )(f)


def _small_all_reduce(s):
    n_rows = s.shape[0]
    hr = n_rows // 2

    def body(s_ref, o_ref, sibbuf, tbuf, cbuf, fbuf, send_sems, recv_sems):
        x, y, c = _mesh_pos()
        sib = (x, y, 1 - c)
        mine = pl.ds(pl.multiple_of(c * hr, SUBLANES), hr)
        theirs = pl.ds(pl.multiple_of((1 - c) * hr, SUBLANES), hr)
        first = _remote(s_ref.at[theirs], sibbuf, send_sems, recv_sems, 0, sib)
        first.start()
        first.wait()
        tbuf[...] = s_ref[mine, :] + sibbuf[...]
        cps = []
        for j, (px, py) in enumerate(_other_chips(x, y)):
            cp = _remote(tbuf, cbuf.at[j], send_sems, recv_sems, 1 + j, (px, py, c))
            cp.start()
            cps.append(cp)
        for cp in cps:
            cp.wait()
        f = (tbuf[...] + cbuf[1]) + (cbuf[0] + cbuf[2])
        fbuf[...] = f
        o_ref[mine, :] = f
        last = _remote(fbuf, o_ref.at[mine], send_sems, recv_sems, 4, sib)
        last.start()
        last.wait()

    vmem = pl.BlockSpec(memory_space=pltpu.VMEM)
    return pl.pallas_call(
        body, name="small_all_reduce",
        in_specs=[vmem], out_specs=vmem,
        out_shape=jax.ShapeDtypeStruct(s.shape, F32),
        scratch_shapes=[pltpu.VMEM((hr, D_MODEL), F32), pltpu.VMEM((hr, D_MODEL), F32),
                        pltpu.VMEM((3, hr, D_MODEL), F32), pltpu.VMEM((hr, D_MODEL), F32),
                        pltpu.SemaphoreType.DMA((5,)), pltpu.SemaphoreType.DMA((5,))],
        compiler_params=pltpu.CompilerParams(vmem_limit_bytes=VMEM_LIMIT),
    )(s)


_SMALL = ("norm_mix", "w_pool", "pool_scale", "lam_re", "lam_im", "log_dt", "b_re", "b_im", "c_re", "c_im",
          "d_skip", "b_glu", "norm_ffn", "norm_final")
_WEIGHTS = ("norm_mix", "w_in", "w_pool", "pool_scale", "lam_re", "lam_im", "log_dt", "b_re", "b_im", "c_re",
            "c_im", "d_skip", "w_glu", "b_glu", "w_out", "norm_ffn", "w_gate", "w_up", "w_down", "norm_final")


def _local_step(x, target, p, get_weights, ffn_bwd_done, put_grads):
    nl = p["norm_mix"].shape[0]

    def tied(a, token):
        return a if token is None else a + token
    n_rows = nl * N_SSM_GROUPS
    lr = p["lam_re"].reshape(n_rows, 1, SSM_STATE)
    li = p["lam_im"].reshape(n_rows, 1, SSM_STATE)
    ldt = p["log_dt"].reshape(n_rows, 1, 1)
    br_t = p["b_re"].reshape(n_rows, SSM_STATE, SSM_GROUP).transpose(0, 2, 1)
    bi_t = p["b_im"].reshape(n_rows, SSM_STATE, SSM_GROUP).transpose(0, 2, 1)
    ar, ai, bbr_t, bbi_t = _disc_fwd(lr, li, ldt, br_t, bi_t)
    ar = ar.reshape(nl, 1, N_STATE)
    ai = ai.reshape(nl, 1, N_STATE)
    bbr = bbr_t.transpose(0, 2, 1).reshape(nl, N_SSM_GROUPS, SSM_STATE, SSM_GROUP)
    bbi = bbi_t.transpose(0, 2, 1).reshape(nl, N_SSM_GROUPS, SSM_STATE, SSM_GROUP)
    w_pool = p["w_pool"].astype(BF16)
    p = dict(p)
    for n in ("norm_mix", "pool_scale", "b_glu", "norm_ffn"):
        p[n] = p[n].reshape(nl, 1, -1)
    swap = lambda a: jnp.swapaxes(a, -1, -2)
    bpad = jax.vmap(_pad_pairs)(bbr, bbi).astype(BF16)
    cpad_t = jax.vmap(_pad_pairs)(swap(p["c_re"]), -swap(p["c_im"])).astype(BF16)
    bpad_t, cpad = swap(bpad), swap(cpad_t)
    dskip = p["d_skip"].reshape(nl, 1, D_SSM)

    layers = []
    h = x
    for l in range(nl):
        wp = get_weights(l, [h] if l else [h, bpad, cpad, bpad_t, cpad_t, ar, ai])
        u, ypool = _mix_in_fwd(h, p["norm_mix"], wp, l, w_pool, p["pool_scale"])
        sre, sim, yraw = _ssm_fwd(u, l, bpad, cpad, ar, ai, dskip)
        hm = _mix_out_fwd(yraw, ypool, h, wp, l, p["b_glu"])
        h_next, n2, gate_s, up_s = _ffn_fwd(hm, p["norm_ffn"], wp, l)
        layers.append(dict(h=h, u=u, ypool=ypool, sre=sre, sim=sim, yraw=yraw, hm=hm, n2=n2, gate_s=gate_s, wp=wp,
                           up_s=up_s))
        h = h_next

    dh, loss, d_norm_final = _final_fwd_bwd(h, p["norm_final"].reshape(1, D_MODEL), target)

    raw = {n: [None] * nl for n in ("dg1", "dwp", "dsc", "dcp", "dbp", "ddsk", "db_glu", "dg2", "dar", "dai")}
    token = None
    for l in reversed(range(nl)):
        s = layers[l]
        wp = s["wp"]
        g1 = lax.empty((1, N_SHARD, P_ROWS, D_MODEL), F32)
        dhm, dg2, dgate_s, dup_s, act_s, dhb = _ffn_bwd_act(dh, s["hm"], tied(p["norm_ffn"], token),
                                                             s["gate_s"], s["up_s"], wp, l)
        g1 = _ffn_bwd_w(s["n2"], dgate_s, dup_s, act_s, dhb, g1)
        token = ffn_bwd_done(l, [g1])
        dyraw, dyp, db_glu, g1 = _mix_out_bwd(dhm, s["yraw"], s["ypool"], wp, l, tied(p["b_glu"], token), g1)
        dus, dcp, dbp, dar, dai, ddsk = _ssm_bwd(dyraw, s["u"], s["sre"], s["sim"], l, cpad_t, bpad_t, ar, ai, dskip)
        dup, dwp, dsc = _pool_bwd(dyp, s["u"], l, w_pool, p["pool_scale"])
        dh, dg1, g1 = _mix_in_bwd(dup, dus, s["h"], dhm, p["norm_mix"], wp, l, g1)
        token = put_grads(l, g1)
        for n, a in (("dg1", dg1), ("dwp", dwp), ("dsc", dsc), ("dcp", dcp), ("dbp", dbp), ("ddsk", ddsk),
                     ("db_glu", db_glu), ("dg2", dg2), ("dar", dar), ("dai", dai)):
            raw[n][l] = a

    st = {n: jnp.stack(v) for n, v in raw.items()}
    dc_re, dc_im = jax.vmap(_unpad_pairs)(swap(st["dcp"]))
    dbbr, dbbi = jax.vmap(_unpad_pairs)(st["dbp"])
    rows = lambda a: a.reshape((n_rows,) + a.shape[2:])
    dlr, dli, dldt, dbr_t, dbi_t = _disc_bwd(lr, li, ldt, br_t, bi_t, st["dar"].reshape(n_rows, 1, SSM_STATE),
                                              st["dai"].reshape(n_rows, 1, SSM_STATE), rows(swap(dbbr)),
                                              rows(swap(dbbi)))
    small = {"norm_mix": st["dg1"][:, 0], "w_pool": st["dwp"], "pool_scale": st["dsc"][:, 0], "c_re": swap(dc_re),
             "c_im": -swap(dc_im), "d_skip": st["ddsk"].reshape(nl, N_SSM_GROUPS, SSM_GROUP),
             "b_glu": st["db_glu"][:, 0], "norm_ffn": st["dg2"][:, 0]}
    small["lam_re"] = dlr.reshape(nl, N_SSM_GROUPS, SSM_STATE)
    small["lam_im"] = dli.reshape(nl, N_SSM_GROUPS, SSM_STATE)
    small["log_dt"] = dldt.reshape(nl, N_SSM_GROUPS)
    small["b_re"] = dbr_t.transpose(0, 2, 1).reshape(nl, N_SSM_GROUPS, SSM_STATE, SSM_GROUP)
    small["b_im"] = dbi_t.transpose(0, 2, 1).reshape(nl, N_SSM_GROUPS, SSM_STATE, SSM_GROUP)
    small["norm_final"] = d_norm_final[0]
    return loss, dh, small


def _flatten_small(d):
    flat = jnp.concatenate([d[n].reshape(-1) for n in _SMALL])
    n_rows = -(-flat.shape[0] // (32 * D_MODEL)) * 32
    return jnp.pad(flat, (0, n_rows * D_MODEL - flat.shape[0])).reshape(n_rows, D_MODEL)


def _split_small(flat, like):
    flat = flat.reshape(-1)
    out, at = {}, 0
    for n in _SMALL:
        size = like[n].size
        out[n] = flat[at:at + size].reshape(like[n].shape)
        at += size
    return out


def kernel(x, norm_mix, w_in, w_pool, pool_scale, lam_re, lam_im, log_dt, b_re, b_im, c_re, c_im, d_skip, w_glu, b_glu, w_out, norm_ffn, w_gate, w_up, w_down, norm_final, loss_target, m_norm_mix, m_w_in, m_w_pool, m_pool_scale, m_lam_re, m_lam_im, m_log_dt, m_b_re, m_b_im, m_c_re, m_c_im, m_d_skip, m_w_glu, m_b_glu, m_w_out, m_norm_ffn, m_w_gate, m_w_up, m_w_down, m_norm_final, v_norm_mix, v_w_in, v_w_pool, v_pool_scale, v_lam_re, v_lam_im, v_log_dt, v_b_re, v_b_im, v_c_re, v_c_im, v_d_skip, v_w_glu, v_b_glu, v_w_out, v_norm_ffn, v_w_gate, v_w_up, v_w_down, v_norm_final):
    given = dict(locals())
    w = {n: given[n] for n in _WEIGHTS}
    m = {n: given["m_" + n] for n in _WEIGHTS}
    v = {n: given["v_" + n] for n in _WEIGHTS}
    ids = jnp.stack([lax.axis_index("c"), 2 * lax.axis_index("x") + lax.axis_index("y")]).astype(jnp.int32)

    t_names = ("w_gate", "w_up")
    tr = lambda a: a.transpose(0, 2, 1)
    for d in (w, m, v):
        d.update({n: tr(d[n]) for n in t_names})

    nl = norm_mix.shape[0]
    packed = [_pack_weights(ids, l, w["w_in"], w["w_glu"], w["w_out"], w["w_down"], w["w_gate"], w["w_up"])
              for l in range(nl)]
    started, last = {}, ids
    for l in range(nl):
        started[l] = _ag_start(f"ag_start_{l}", packed[l], last)
        last = started[l][3]
    flat = [_flatten_small(d)[None] for d in (w, m, v)]

    def get_weights(l, after):
        send_sems, recv_sems, buf, _ = started[l]
        after = after + ([last] + flat if l == 0 else [])
        return _ag_forward(_ag_wait(f"ag_wait_{l}", send_sems, recv_sems, buf, after))

    to_sibling, to_chips, reduced = {}, {}, [lax.empty((nl, P_ROWS, D_MODEL), F32)]

    def put_grads(l, g):
        to_sibling[l] = _rs_sibling_start(f"rs_sibling_start_{l}", g)
        token = to_sibling[l][4]
        if l + 1 in to_chips:
            finish(l + 1, [token])
        return token[:1, :1]

    def ffn_bwd_done(l, after):
        return send_to_chips(l + 1, after) if l + 1 in to_sibling else None

    def send_to_chips(l, after):
        send_sems, recv_sems, g, land, _ = to_sibling.pop(l)
        g, land = _rs_sibling_wait(f"rs_sibling_wait_{l}", send_sems, recv_sems, g, land, after)
        own, t = _rs_add("rs_add", ids, g, land, RS_ROW_TILE)
        send_sems, recv_sems, t, land, token = _rs_chips_start(f"rs_chips_start_{l}", t)
        to_chips[l] = (send_sems, recv_sems, t, land, own)
        return token[:1, :1]

    def finish(l, after):
        send_sems, recv_sems, t, land, own = to_chips.pop(l)
        land = _rs_chips_wait(f"rs_chips_wait_{l}", send_sems, recv_sems, t, land, after)
        reduced[0] = _rs_exchange(_rs_sum(ids, l, own, land, reduced[0], RS_ROW_TILE), l)

    loss, grad_x, small = _local_step(x[0], loss_target[0], {n: w[n] for n in _SMALL}, get_weights, ffn_bwd_done,
                                      put_grads)
    loss = lax.psum(loss[0, 0], ("x", "y", "c"))
    token = send_to_chips(0, [small["norm_final"]])
    small["norm_final"] = small["norm_final"] + token[0]
    small_sum = _small_all_reduce(_flatten_small(small))
    finish(0, [small_sum])
    gr = reduced[0]

    res = {}
    big = (("w_in", P_IN_BLK, 256, False), ("w_out", P_OUT_BLK, 256, False), ("w_down", P_WD_BLK, 352, False),
           ("w_gate", P_WG_BLK, 352, False), ("w_up", P_WU_BLK, 352, False), ("w_glu", P_GLU_BLK, 128, True))
    for n, (blk, idx), row_tile, glu in big:
        res[n] = _adamw("adamw_" + n, w[n], m[n], v[n], gr, (blk, D_MODEL), blk * idx, row_tile, glu)
    for n in t_names:
        res[n] = tuple(tr(a) for a in res[n])
    n_rows = flat[0].shape[1]
    outs = _adamw("adamw_small", *flat, small_sum[None], (n_rows, D_MODEL), 0, n_rows // 4)
    parts = [_split_small(o[0], w) for o in outs]
    for n in _SMALL:
        res[n] = tuple(part[n] for part in parts)

    return (loss, grad_x[None], *[res[n][0] for n in _WEIGHTS], *[res[n][1] for n in _WEIGHTS],
            *[res[n][2] for n in _WEIGHTS], *[res[n][3] for n in _WEIGHTS])
```

```python
import functools
import math

import jax
import jax.numpy as jnp
from jax import lax
from jax.experimental import pallas as pl
from jax.experimental.pallas import tpu as pltpu

F32 = jnp.float32
BF16 = jnp.bfloat16

D_MODEL = 1024
D_POOL = 512
D_SSM = 512
POOL_WINDOWS = (2, 4, 8, 16)
POOL_GROUP = 128
POOL_HALO = 16
N_SSM_GROUPS = 32
SSM_GROUP = 16
SSM_STATE = 64
N_STATE = N_SSM_GROUPS * SSM_STATE
N_PAIRS = N_SSM_GROUPS // 2
D_FF = 2816
N_SHARD = 4
FF_SHARD = D_FF // N_SHARD
RMS_EPS = 1e-6

ADAM_LR = 0.001
ADAM_B1 = 0.9
ADAM_B2 = 0.999
ADAM_EPS = 1e-08
ADAM_WD = 0.01
ADAM_STEP = 10

P_ROWS = 2816
P_WD_BLK = (704, 0)
P_WG_BLK = (704, 1)
P_WU_BLK = (704, 2)
P_FF_ROWS = 2112
P_GLU_BLK = (64, 33)
P_GLU_PAD = 192
P_IN_BLK = (256, 9)
P_OUT_BLK = (256, 10)

SUBLANES = 8
VMEM_LIMIT = 56 * 1024 * 1024

TM = 512
TM_FFN = 512
FFN_SUB_ROWS = 128
TS = 512
SCAN_LANES = 512


def _cparams(n_axes):
    return pltpu.CompilerParams(dimension_semantics=("arbitrary",) * n_axes, vmem_limit_bytes=VMEM_LIMIT)


def _dot(a, b):
    return jnp.dot(a, b, preferred_element_type=F32)


def _dot_nt(a, b):
    return lax.dot_general(a, b, (((1,), (1,)), ((), ())), preferred_element_type=F32)


def _dot_tn(a, b):
    return lax.dot_general(a, b, (((0,), (0,)), ((), ())), preferred_element_type=F32)


def _rms_hat(x):
    r = lax.rsqrt(jnp.mean(x * x, axis=-1, keepdims=True) + RMS_EPS)
    return x * r, r


def _rms_bwd(d_hat, xhat, r):
    return r * (d_hat - xhat * jnp.mean(d_hat * xhat, axis=-1, keepdims=True))


def _sigmoid(x):
    return 1.0 / (1.0 + jnp.exp(-x))


_GELU_C = math.sqrt(2.0 / math.pi)
_GELU_K = 0.044715


def _gelu(x):
    return 0.5 * x * (1.0 + jnp.tanh(_GELU_C * (x + _GELU_K * x * x * x)))


def _gelu_grad(x):
    th = jnp.tanh(_GELU_C * (x + _GELU_K * x * x * x))
    return 0.5 * (1.0 + th) + 0.5 * x * (1.0 - th * th) * _GELU_C * (1.0 + 3.0 * _GELU_K * x * x)


def _glu_weight(ref):
    v = ref[...]
    return jnp.concatenate([v[:, :, :D_SSM], v[:, :, D_SSM:]], axis=1).reshape(D_SSM, D_SSM)


def _glu_pack(w):
    v = w.reshape(N_SHARD, 128, D_SSM)
    return jnp.concatenate([v[:, :64, :], v[:, 64:, :]], axis=2)


def _pool_diff(ext, row0, tm):
    rows = row0 + lax.broadcasted_iota(jnp.int32, (tm, 1), 0)
    outs = []
    for gi, w in enumerate(POOL_WINDOWS):
        e = ext[:, gi * POOL_GROUP:(gi + 1) * POOL_GROUP]
        s = e
        k = 1
        while k < w:
            s = s + pltpu.roll(s, k, 0)
            k *= 2
        inv = 1.0 / jnp.minimum(rows + 1, w).astype(F32)
        outs.append(s[POOL_HALO:, :] * inv - e[POOL_HALO:, :])
    return outs


def _mix_in_fwd(h, g1, wp, layer, w_pool, scale):
    L = h.shape[0]
    tm = min(TM, L)

    def body(h_ref, g_ref, w_ref, wp_ref, sc_ref, u_ref, yp_ref, carry):
        i = pl.program_id(0)

        @pl.when(i == 0)
        def _():
            carry[...] = jnp.zeros_like(carry)

        xhat, _ = _rms_hat(h_ref[...])
        n1 = (xhat * g_ref[...]).astype(BF16)
        u = _dot(n1, w_ref[...].reshape(D_MODEL, D_MODEL))
        u_ref[...] = u
        up = u[:, :D_POOL]
        ext = jnp.concatenate([carry[...], up], axis=0)
        carry[...] = up[tm - POOL_HALO:, :]
        diffs = _pool_diff(ext, i * tm, tm)
        for gi in range(4):
            cols = slice(gi * POOL_GROUP, (gi + 1) * POOL_GROUP)
            yp_ref[:, cols] = _dot(diffs[gi].astype(BF16), wp_ref[gi]) * sc_ref[:, cols]

    blk, idx = P_IN_BLK
    return pl.pallas_call(
        body, name="mix_in_fwd", grid=(L // tm,),
        in_specs=[pl.BlockSpec((tm, D_MODEL), lambda i: (i, 0)),
                  pl.BlockSpec((None, 1, D_MODEL), lambda i: (layer, 0, 0)),
                  pl.BlockSpec((N_SHARD, None, blk, D_MODEL), lambda i: (0, 0, idx, 0)),
                  pl.BlockSpec((None, 4, POOL_GROUP, POOL_GROUP), lambda i: (layer, 0, 0, 0)),
                  pl.BlockSpec((None, 1, D_POOL), lambda i: (layer, 0, 0))],
        out_specs=[pl.BlockSpec((tm, D_MODEL), lambda i: (i, 0)),
                   pl.BlockSpec((tm, D_POOL), lambda i: (i, 0))],
        out_shape=[jax.ShapeDtypeStruct((L, D_MODEL), F32), jax.ShapeDtypeStruct((L, D_POOL), F32)],
        scratch_shapes=[pltpu.VMEM((POOL_HALO, D_POOL), F32)],
        compiler_params=_cparams(1),
    )(h, g1, wp, w_pool, scale)


def _cmul(xr, xi, yr, yi):
    return xr * yr - xi * yi, xr * yi + xi * yr


def _scan_tables(ar, ai, tab, reverse):
    c = ar.shape[1]
    row = lax.broadcasted_iota(jnp.int32, (SUBLANES, c), 0)
    a2r, a2i = _cmul(ar, ai, ar, ai)
    a4r, a4i = _cmul(a2r, a2i, a2r, a2i)
    zero = jnp.zeros((SUBLANES, c), F32)
    for n, (s, pr, pi) in enumerate(((1, ar, ai), (2, a2r, a2i), (4, a4r, a4i))):
        keep = (row < SUBLANES - s) if reverse else (row >= s)
        tab[2 * n] = jnp.where(keep, pr, zero)
        tab[2 * n + 1] = jnp.where(keep, pi, zero)
    cr, ci = ar, ai
    tr, ti = zero, zero
    for n in range(SUBLANES):
        at = (SUBLANES - 1 - n) if reverse else n
        tr = jnp.where(row == at, cr, tr)
        ti = jnp.where(row == at, ci, ti)
        cr, ci = _cmul(cr, ci, ar, ai)
    tab[6] = tr
    tab[7] = ti


def _ssm_fwd(u, layer, bpad, cpad, ar, ai, dskip):
    L = u.shape[0]
    ts = min(TS, L)
    nq = 4
    cq = N_STATE // nq

    def body(u_ref, bp_ref, cp_ref, ar_ref, ai_ref, dsk_ref, sre_ref, sim_ref, y_ref, cr, ci, tab):
        t = pl.program_id(1)

        @pl.when(t == 0)
        def _():
            cr[...] = jnp.zeros_like(cr)
            ci[...] = jnp.zeros_like(ci)
            _scan_tables(ar_ref[...], ai_ref[...], tab, reverse=False)

        uf = u_ref[...]
        ub = uf.astype(BF16)
        for jj in range(4):
            bu = _dot(ub, bp_ref[jj])
            sre_ref[:, jj * 128:(jj + 1) * 128] = bu[:, :128]
            sim_ref[:, jj * 128:(jj + 1) * 128] = bu[:, 128:]

        for cc in range(cq // SCAN_LANES):
            cols = slice(cc * SCAN_LANES, (cc + 1) * SCAN_LANES)
            def step(i, carry, cols=cols):
                c_r, c_i = carry
                r0 = pl.multiple_of(i * SUBLANES, SUBLANES)
                xr = sre_ref[pl.ds(r0, SUBLANES), cols]
                xi = sim_ref[pl.ds(r0, SUBLANES), cols]
                for n, s in enumerate((1, 2, 4)):
                    tr, ti = tab[2 * n, :, cols], tab[2 * n + 1, :, cols]
                    rr = pltpu.roll(xr, s, 0)
                    ri = pltpu.roll(xi, s, 0)
                    xr, xi = xr + tr * rr - ti * ri, xi + tr * ri + ti * rr
                pr, pi = tab[6, :, cols], tab[7, :, cols]
                xr, xi = xr + pr * c_r - pi * c_i, xi + pr * c_i + pi * c_r
                sre_ref[pl.ds(r0, SUBLANES), cols] = xr
                sim_ref[pl.ds(r0, SUBLANES), cols] = xi
                shp = (SUBLANES, SCAN_LANES)
                return (jnp.broadcast_to(xr[SUBLANES - 1:, :], shp), jnp.broadcast_to(xi[SUBLANES - 1:, :], shp))

            c_r, c_i = lax.fori_loop(0, ts // SUBLANES, step, (cr[:, cols], ci[:, cols]), unroll=2)
            cr[:, cols] = c_r
            ci[:, cols] = c_i

        acc = dsk_ref[...] * uf
        for jj in range(4):
            cols = slice(jj * 128, (jj + 1) * 128)
            scat = jnp.concatenate([sre_ref[:, cols], sim_ref[:, cols]], axis=1).astype(BF16)
            acc = acc + _dot(scat, cp_ref[jj])
        y_ref[...] = acc

    return pl.pallas_call(
        body, name="ssm_fwd", grid=(nq, L // ts),
        in_specs=[pl.BlockSpec((ts, 128), lambda q, t: (t, 4 + q)),
                  pl.BlockSpec((None, 4, 128, 256), lambda q, t: (layer, q, 0, 0)),
                  pl.BlockSpec((None, 4, 256, 128), lambda q, t: (layer, q, 0, 0)),
                  pl.BlockSpec((None, 1, cq), lambda q, t: (layer, 0, q)),
                  pl.BlockSpec((None, 1, cq), lambda q, t: (layer, 0, q)),
                  pl.BlockSpec((None, 1, 128), lambda q, t: (layer, 0, q))],
        out_specs=[pl.BlockSpec((ts, cq), lambda q, t: (t, q)),
                   pl.BlockSpec((ts, cq), lambda q, t: (t, q)),
                   pl.BlockSpec((ts, 128), lambda q, t: (t, q))],
        out_shape=[jax.ShapeDtypeStruct((L, N_STATE), F32), jax.ShapeDtypeStruct((L, N_STATE), F32),
                   jax.ShapeDtypeStruct((L, D_SSM), F32)],
        scratch_shapes=[pltpu.VMEM((SUBLANES, cq), F32), pltpu.VMEM((SUBLANES, cq), F32),
                        pltpu.VMEM((8, SUBLANES, cq), F32)],
        compiler_params=_cparams(2),
    )(u, bpad, cpad, ar, ai, dskip)


def _mix_out_fwd(yraw, ypool, h, wp, layer, b_glu):
    L = h.shape[0]
    tm = min(TM, L)

    def body(yr_ref, yp_ref, h_ref, wglu_ref, b_ref, wout_ref, o_ref):
        y = _gelu(yr_ref[...])
        z = _dot(y.astype(BF16), _glu_weight(wglu_ref)) + b_ref[...]
        o = y * _sigmoid(z)
        mix = jnp.concatenate([yp_ref[...], o], axis=1).astype(BF16)
        o_ref[...] = h_ref[...] + _dot(mix, wout_ref[...].reshape(D_MODEL, D_MODEL))

    gb, gi = P_GLU_BLK
    ob, oi = P_OUT_BLK
    return pl.pallas_call(
        body, name="mix_out_fwd", grid=(L // tm,),
        in_specs=[pl.BlockSpec((tm, D_SSM), lambda i: (i, 0)),
                  pl.BlockSpec((tm, D_POOL), lambda i: (i, 0)),
                  pl.BlockSpec((tm, D_MODEL), lambda i: (i, 0)),
                  pl.BlockSpec((N_SHARD, None, gb, D_MODEL), lambda i: (0, 0, gi, 0)),
                  pl.BlockSpec((None, 1, D_SSM), lambda i: (layer, 0, 0)),
                  pl.BlockSpec((N_SHARD, None, ob, D_MODEL), lambda i: (0, 0, oi, 0))],
        out_specs=pl.BlockSpec((tm, D_MODEL), lambda i: (i, 0)),
        out_shape=jax.ShapeDtypeStruct((L, D_MODEL), F32),
        compiler_params=_cparams(1),
    )(yraw, ypool, h, wp, b_glu, wp)


def _ffn_weights(ref, k):
    return ref[k, 0:FF_SHARD, :], ref[k, FF_SHARD:2 * FF_SHARD, :], ref[k, 2 * FF_SHARD:P_FF_ROWS, :]


def _ffn_weight_spec():
    return pl.BlockSpec((N_SHARD, None, P_FF_ROWS, D_MODEL), lambda m, k: (0, 0, 0, 0),
                        pipeline_mode=pl.Buffered(1))


def _ffn_fwd(h, g2, wp, layer):
    L = h.shape[0]
    tm = min(TM_FFN, L)

    def body(h_ref, g_ref, w_ref, o_ref, n2_ref, gate_ref, up_ref):
        k = pl.program_id(1)

        @pl.when(k == 0)
        def _():
            x = h_ref[...]
            xhat, _ = _rms_hat(x)
            n2_ref[...] = (xhat * g_ref[...]).astype(BF16)
            o_ref[...] = x

        wd, wg_t, wu_t = _ffn_weights(w_ref, k)
        n2 = n2_ref[...]
        gate = _dot_nt(n2, wg_t)
        up = _dot_nt(n2, wu_t)
        gate_ref[...] = gate.astype(BF16)
        up_ref[...] = up.astype(BF16)
        act = (gate * _sigmoid(gate) * up).astype(BF16)
        o_ref[...] += _dot(act, wd)

    act_shape = jax.ShapeDtypeStruct((N_SHARD, L, FF_SHARD), BF16)
    return pl.pallas_call(
        body, name="ffn_fwd", grid=(L // tm, N_SHARD),
        in_specs=[pl.BlockSpec((tm, D_MODEL), lambda m, k: (m, 0)),
                  pl.BlockSpec((None, 1, D_MODEL), lambda m, k: (layer, 0, 0)),
                  _ffn_weight_spec()],
        out_specs=[pl.BlockSpec((tm, D_MODEL), lambda m, k: (m, 0)),
                   pl.BlockSpec((tm, D_MODEL), lambda m, k: (m, 0)),
                   pl.BlockSpec((None, tm, FF_SHARD), lambda m, k: (k, m, 0)),
                   pl.BlockSpec((None, tm, FF_SHARD), lambda m, k: (k, m, 0))],
        out_shape=[jax.ShapeDtypeStruct((L, D_MODEL), F32), jax.ShapeDtypeStruct((L, D_MODEL), BF16),
                   act_shape, act_shape],
        compiler_params=_cparams(2),
    )(h, g2, wp)


def _final_fwd_bwd(h, gf, target):
    L = h.shape[0]
    tm = min(TM, L)

    def body(h_ref, g_ref, t_ref, dh_ref, loss_ref, dg_ref):
        i = pl.program_id(0)

        @pl.when(i == 0)
        def _():
            loss_ref[...] = jnp.zeros_like(loss_ref)
            dg_ref[...] = jnp.zeros_like(dg_ref)

        xhat, r = _rms_hat(h_ref[...])
        g = g_ref[...]
        e = xhat * g - t_ref[...]
        loss_ref[...] += 0.5 * jnp.sum(jnp.mean(e * e, axis=-1, keepdims=True), axis=0, keepdims=True)
        dy = e * (1.0 / D_MODEL)
        dg_ref[...] += jnp.sum(dy * xhat, axis=0, keepdims=True)
        dh_ref[...] = _rms_bwd(dy * g, xhat, r)

    return pl.pallas_call(
        body, name="final_fwd_bwd", grid=(L // tm,),
        in_specs=[pl.BlockSpec((tm, D_MODEL), lambda i: (i, 0)),
                  pl.BlockSpec((1, D_MODEL), lambda i: (0, 0)),
                  pl.BlockSpec((tm, D_MODEL), lambda i: (i, 0))],
        out_specs=[pl.BlockSpec((tm, D_MODEL), lambda i: (i, 0)),
                   pl.BlockSpec((1, 1), lambda i: (0, 0)),
                   pl.BlockSpec((1, D_MODEL), lambda i: (0, 0))],
        out_shape=[jax.ShapeDtypeStruct((L, D_MODEL), F32), jax.ShapeDtypeStruct((1, 1), F32),
                   jax.ShapeDtypeStruct((1, D_MODEL), F32)],
        compiler_params=_cparams(1),
    )(h, gf, target)


def _ffn_bwd_act(dh, h, g2, gate_s, up_s, wp, layer):
    L = h.shape[0]
    tm = min(TM_FFN, L)
    sub = min(FFN_SUB_ROWS, tm)

    def body(dh_ref, h_ref, g_ref, gate_ref, up_ref, w_ref,
             dhm_ref, dg_ref, dgate_ref, dup_ref, act_ref, dhb_ref, dn2):
        m, k = pl.program_id(0), pl.program_id(1)

        @pl.when(jnp.logical_and(m == 0, k == 0))
        def _():
            dg_ref[...] = jnp.zeros_like(dg_ref)

        @pl.when(k == 0)
        def _():
            dhb_ref[...] = dh_ref[...].astype(BF16)
            dn2[...] = jnp.zeros_like(dn2)

        wd, wg_t, wu_t = _ffn_weights(w_ref, k)
        for r in range(tm // sub):
            rows = slice(r * sub, (r + 1) * sub)
            dact = _dot_nt(dhb_ref[rows, :], wd)
            gate = gate_ref[rows, :].astype(F32)
            up = up_ref[rows, :].astype(F32)
            sg = _sigmoid(gate)
            silu = gate * sg
            dgate = (dact * up * (sg * (1.0 + gate * (1.0 - sg)))).astype(BF16)
            dup = (dact * silu).astype(BF16)
            dgate_ref[rows, :] = dgate
            dup_ref[rows, :] = dup
            act_ref[rows, :] = (silu * up).astype(BF16)
            dn2[rows, :] += _dot(dgate, wg_t) + _dot(dup, wu_t)

        @pl.when(k == N_SHARD - 1)
        def _():
            xhat, r = _rms_hat(h_ref[...])
            d = dn2[...]
            dg_ref[...] += jnp.sum(d * xhat, axis=0, keepdims=True)
            dhm_ref[...] = dh_ref[...] + _rms_bwd(d * g_ref[...], xhat, r)

    act_spec = pl.BlockSpec((None, tm, FF_SHARD), lambda m, k: (k, m, 0))
    act_shape = jax.ShapeDtypeStruct((N_SHARD, L, FF_SHARD), BF16)
    row_spec = pl.BlockSpec((tm, D_MODEL), lambda m, k: (m, 0))
    return pl.pallas_call(
        body, name="ffn_bwd_act", grid=(L // tm, N_SHARD),
        in_specs=[row_spec, row_spec,
                  pl.BlockSpec((None, 1, D_MODEL), lambda m, k: (layer, 0, 0)),
                  act_spec, act_spec,
                  _ffn_weight_spec()],
        out_specs=[row_spec,
                   pl.BlockSpec((1, D_MODEL), lambda m, k: (0, 0)),
                   act_spec, act_spec, act_spec, row_spec],
        out_shape=[jax.ShapeDtypeStruct((L, D_MODEL), F32), jax.ShapeDtypeStruct((1, D_MODEL), F32),
                   act_shape, act_shape, act_shape, jax.ShapeDtypeStruct((L, D_MODEL), BF16)],
        scratch_shapes=[pltpu.VMEM((tm, D_MODEL), F32)],
        compiler_params=_cparams(2),
    )(dh, h, g2, gate_s, up_s, wp)


def _ffn_bwd_w(n2, dgate_s, dup_s, act_s, dhb, gbuf):
    L = n2.shape[0]
    tm = min(TM_FFN, L)

    def body(n2_ref, dgate_ref, dup_ref, act_ref, dhb_ref, g_in, g_ref):
        m = pl.program_id(1)

        @pl.when(m == 0)
        def _():
            g_ref[...] = jnp.zeros_like(g_ref)

        n2v = n2_ref[...]
        g_ref[0:FF_SHARD, :] += _dot_tn(act_ref[...], dhb_ref[...])
        g_ref[FF_SHARD:2 * FF_SHARD, :] += _dot_tn(dgate_ref[...], n2v)
        g_ref[2 * FF_SHARD:P_FF_ROWS, :] += _dot_tn(dup_ref[...], n2v)

    act_spec = pl.BlockSpec((None, tm, FF_SHARD), lambda k, m: (k, m, 0))
    row_spec = pl.BlockSpec((tm, D_MODEL), lambda k, m: (m, 0))
    return pl.pallas_call(
        body, name="ffn_bwd_w", grid=(N_SHARD, L // tm),
        in_specs=[row_spec, act_spec, act_spec, act_spec, row_spec, pl.BlockSpec(memory_space=pl.ANY)],
        out_specs=pl.BlockSpec((None, None, P_FF_ROWS, D_MODEL), lambda k, m: (0, k, 0, 0)),
        out_shape=jax.ShapeDtypeStruct(gbuf.shape, F32),
        input_output_aliases={5: 0},
        compiler_params=_cparams(2),
    )(n2, dgate_s, dup_s, act_s, dhb, gbuf)


def _mix_out_bwd(dhm, yraw, ypool, wp, layer, b_glu, gbuf):
    L = dhm.shape[0]
    tm = min(TM, L)

    def body(dhm_ref, yr_ref, yp_ref, wglu_ref, b_ref, wout_ref, g1_in,
             dyr_ref, dyp_ref, db_ref, g1_ref, dwout, dwglu, gpack):
        i = pl.program_id(0)

        @pl.when(i == 0)
        def _():
            db_ref[...] = jnp.zeros_like(db_ref)
            dwout[...] = jnp.zeros_like(dwout)
            dwglu[...] = jnp.zeros_like(dwglu)

        dhb = dhm_ref[...].astype(BF16)
        wglu = _glu_weight(wglu_ref)
        dmix = _dot_nt(dhb, wout_ref[...].reshape(D_MODEL, D_MODEL))
        dyp_ref[...] = dmix[:, :D_POOL]
        d_o = dmix[:, D_POOL:]
        yraw_v = yr_ref[...]
        y = _gelu(yraw_v)
        yb = y.astype(BF16)
        sig = _sigmoid(_dot(yb, wglu) + b_ref[...])
        mix = jnp.concatenate([yp_ref[...], y * sig], axis=1).astype(BF16)
        dwout[...] += _dot_tn(mix, dhb).reshape(N_SHARD, 256, D_MODEL)
        dz = d_o * y * sig * (1.0 - sig)
        dzb = dz.astype(BF16)
        db_ref[...] += jnp.sum(dz, axis=0, keepdims=True)
        dwglu[...] += _dot_tn(yb, dzb)
        dy = d_o * sig + _dot_nt(dzb, wglu)
        dyr_ref[...] = dy * _gelu_grad(yraw_v)

        @pl.when(i == n_steps - 1)
        def _():
            gpack[:, :gb, :] = _glu_pack(dwglu[...])
            gpack[:, gb:, :] = jnp.zeros((N_SHARD, P_GLU_PAD - gb, D_MODEL), F32)
            pltpu.sync_copy(gpack, g1_ref.at[0, :, pl.ds(gb * gi, P_GLU_PAD), :])
            pltpu.sync_copy(dwout, g1_ref.at[0, :, pl.ds(ob * oi, ob), :])

    gb, gi = P_GLU_BLK
    ob, oi = P_OUT_BLK
    n_steps = L // tm
    return pl.pallas_call(
        body, name="mix_out_bwd", grid=(n_steps,),
        in_specs=[pl.BlockSpec((tm, D_MODEL), lambda i: (i, 0)),
                  pl.BlockSpec((tm, D_SSM), lambda i: (i, 0)),
                  pl.BlockSpec((tm, D_POOL), lambda i: (i, 0)),
                  pl.BlockSpec((N_SHARD, None, gb, D_MODEL), lambda i: (0, 0, gi, 0)),
                  pl.BlockSpec((None, 1, D_SSM), lambda i: (layer, 0, 0)),
                  pl.BlockSpec((N_SHARD, None, ob, D_MODEL), lambda i: (0, 0, oi, 0)),
                  pl.BlockSpec(memory_space=pl.ANY)],
        out_specs=[pl.BlockSpec((tm, D_SSM), lambda i: (i, 0)),
                   pl.BlockSpec((tm, D_POOL), lambda i: (i, 0)),
                   pl.BlockSpec((1, D_SSM), lambda i: (0, 0)),
                   pl.BlockSpec(memory_space=pl.ANY)],
        out_shape=[jax.ShapeDtypeStruct((L, D_SSM), F32), jax.ShapeDtypeStruct((L, D_POOL), F32),
                   jax.ShapeDtypeStruct((1, D_SSM), F32),
                   jax.ShapeDtypeStruct(gbuf.shape, F32)],
        scratch_shapes=[pltpu.VMEM((N_SHARD, ob, D_MODEL), F32), pltpu.VMEM((D_SSM, D_SSM), F32),
                        pltpu.VMEM((N_SHARD, P_GLU_PAD, D_MODEL), F32)],
        input_output_aliases={6: 3},
        compiler_params=_cparams(1),
    )(dhm, yraw, ypool, wp, b_glu, wp, gbuf)


def _ssm_bwd(dyraw, u, sre, sim, layer, cpad_t, bpad_t, ar, ai, dskip):
    L = u.shape[0]
    ts = min(TS, L)
    nt = L // ts
    nq = 4
    cq = N_STATE // nq

    def body(dy_ref, u_ref, sre_ref, sim_ref, ct_ref, bt_ref, ar_ref, ai_ref, dsk_ref,
             du_ref, dcp_ref, dbp_ref, dar_ref, dai_ref, ddsk_ref, gre, gim, cr, ci, tab, accr, acci):
        t = pl.program_id(1)

        @pl.when(t == 0)
        def _():
            for ref in (cr, ci, accr, acci, dcp_ref, dbp_ref, ddsk_ref):
                ref[...] = jnp.zeros_like(ref)
            _scan_tables(ar_ref[...], -ai_ref[...], tab, reverse=True)

        dy = dy_ref[...]
        dyb = dy.astype(BF16)
        uf = u_ref[...]
        ub = uf.astype(BF16)
        for jj in range(4):
            cols = slice(jj * 128, (jj + 1) * 128)
            ds = _dot(dyb, ct_ref[jj])
            gre[:, cols] = ds[:, :128]
            gim[:, cols] = ds[:, 128:]
            scat = jnp.concatenate([sre_ref[:, cols], sim_ref[:, cols]], axis=1).astype(BF16)
            dcp_ref[jj] += _dot_tn(scat, dyb)

        n_grp = ts // SUBLANES
        shp = (SUBLANES, SCAN_LANES)
        last_row = lax.broadcasted_iota(jnp.int32, shp, 0) == SUBLANES - 1
        for cc in range(cq // SCAN_LANES):
            cols = slice(cc * SCAN_LANES, (cc + 1) * SCAN_LANES)
            def step(i, carry, cols=cols):
                c_r, c_i, a_r, a_i = carry
                r0 = pl.multiple_of((n_grp - 1 - i) * SUBLANES, SUBLANES)
                xr = gre[pl.ds(r0, SUBLANES), cols]
                xi = gim[pl.ds(r0, SUBLANES), cols]
                for n, s in enumerate((1, 2, 4)):
                    tr, ti = tab[2 * n, :, cols], tab[2 * n + 1, :, cols]
                    rr = pltpu.roll(xr, SUBLANES - s, 0)
                    ri = pltpu.roll(xi, SUBLANES - s, 0)
                    xr, xi = xr + tr * rr - ti * ri, xi + tr * ri + ti * rr
                qr, qi = tab[6, :, cols], tab[7, :, cols]
                xr, xi = xr + qr * c_r - qi * c_i, xi + qr * c_i + qi * c_r
                gre[pl.ds(r0, SUBLANES), cols] = xr
                gim[pl.ds(r0, SUBLANES), cols] = xi
                nr = jnp.where(last_row, c_r, pltpu.roll(xr, SUBLANES - 1, 0))
                ni = jnp.where(last_row, c_i, pltpu.roll(xi, SUBLANES - 1, 0))
                sr = sre_ref[pl.ds(r0, SUBLANES), cols]
                si = sim_ref[pl.ds(r0, SUBLANES), cols]
                a_r = a_r + sr * nr + si * ni
                a_i = a_i + sr * ni - si * nr
                return (jnp.broadcast_to(xr[:1, :], shp), jnp.broadcast_to(xi[:1, :], shp), a_r, a_i)

            c_r, c_i, a_r, a_i = lax.fori_loop(
                0, n_grp, step, (cr[:, cols], ci[:, cols], accr[:, cols], acci[:, cols]), unroll=2)
            cr[:, cols] = c_r
            ci[:, cols] = c_i
            accr[:, cols] = a_r
            acci[:, cols] = a_i

        acc = dsk_ref[...] * dy
        for jj in range(4):
            cols = slice(jj * 128, (jj + 1) * 128)
            gcat = jnp.concatenate([gre[:, cols], gim[:, cols]], axis=1).astype(BF16)
            acc = acc + _dot(gcat, bt_ref[jj])
            dbp_ref[jj] += _dot_tn(ub, gcat)
        du_ref[...] = acc
        ddsk_ref[...] += jnp.sum(dy * uf, axis=0, keepdims=True)

        @pl.when(t == nt - 1)
        def _():
            dar_ref[...] = jnp.sum(accr[...], axis=0, keepdims=True)
            dai_ref[...] = jnp.sum(acci[...], axis=0, keepdims=True)

    f32_scr = lambda *s: pltpu.VMEM(s, F32)
    return pl.pallas_call(
        body, name="ssm_bwd", grid=(nq, nt),
        in_specs=[pl.BlockSpec((ts, 128), lambda q, t: (nt - 1 - t, q)),
                  pl.BlockSpec((ts, 128), lambda q, t: (nt - 1 - t, 4 + q)),
                  pl.BlockSpec((ts, cq), lambda q, t: (nt - 1 - t, q)),
                  pl.BlockSpec((ts, cq), lambda q, t: (nt - 1 - t, q)),
                  pl.BlockSpec((None, 4, 128, 256), lambda q, t: (layer, q, 0, 0)),
                  pl.BlockSpec((None, 4, 256, 128), lambda q, t: (layer, q, 0, 0)),
                  pl.BlockSpec((None, 1, cq), lambda q, t: (layer, 0, q)),
                  pl.BlockSpec((None, 1, cq), lambda q, t: (layer, 0, q)),
                  pl.BlockSpec((None, 1, 128), lambda q, t: (layer, 0, q))],
        out_specs=[pl.BlockSpec((ts, 128), lambda q, t: (nt - 1 - t, q)),
                   pl.BlockSpec((4, 256, 128), lambda q, t: (q, 0, 0)),
                   pl.BlockSpec((4, 128, 256), lambda q, t: (q, 0, 0)),
                   pl.BlockSpec((1, cq), lambda q, t: (0, q)),
                   pl.BlockSpec((1, cq), lambda q, t: (0, q)),
                   pl.BlockSpec((1, 128), lambda q, t: (0, q))],
        out_shape=[jax.ShapeDtypeStruct((L, D_SSM), F32),
                   jax.ShapeDtypeStruct((N_PAIRS, 256, 128), F32), jax.ShapeDtypeStruct((N_PAIRS, 128, 256), F32),
                   jax.ShapeDtypeStruct((1, N_STATE), F32), jax.ShapeDtypeStruct((1, N_STATE), F32),
                   jax.ShapeDtypeStruct((1, D_SSM), F32)],
        scratch_shapes=[f32_scr(ts, cq), f32_scr(ts, cq), f32_scr(SUBLANES, cq), f32_scr(SUBLANES, cq),
                        f32_scr(8, SUBLANES, cq), f32_scr(SUBLANES, cq), f32_scr(SUBLANES, cq)],
        compiler_params=_cparams(2),
    )(dyraw, u, sre, sim, cpad_t, bpad_t, ar, ai, dskip)


def _pool_bwd(dyp, u, layer, w_pool, scale):
    L = u.shape[0]
    tm = min(TM, L)
    nt = L // tm
    halo_per_tile = tm // POOL_HALO

    def body(dyp_ref, u_ref, halo_ref, wp_ref, sc_ref, du_ref, dwp_ref, dsc_ref, carry):
        i = pl.program_id(0)
        tile = nt - 1 - i

        @pl.when(i == 0)
        def _():
            carry[...] = jnp.zeros_like(carry)
            dwp_ref[...] = jnp.zeros_like(dwp_ref)
            dsc_ref[...] = jnp.zeros_like(dsc_ref)

        up = u_ref[...]
        halo = jnp.where(tile > 0, halo_ref[...], jnp.zeros_like(halo_ref))
        diffs = _pool_diff(jnp.concatenate([halo, up], axis=0), tile * tm, tm)
        rows = tile * tm + lax.broadcasted_iota(jnp.int32, (tm, 1), 0)
        n_ext = tm + POOL_HALO
        for gi, w in enumerate(POOL_WINDOWS):
            cols = slice(gi * POOL_GROUP, (gi + 1) * POOL_GROUP)
            db = diffs[gi].astype(BF16)
            dyp = dyp_ref[:, cols]
            dsc_ref[:, cols] += jnp.sum(dyp * _dot(db, wp_ref[gi]), axis=0, keepdims=True)
            dp = (dyp * sc_ref[:, cols]).astype(BF16)
            ddiff = _dot_nt(dp, wp_ref[gi])
            dwp_ref[gi] += _dot_tn(db, dp)
            e = ddiff * (1.0 / jnp.minimum(rows + 1, w).astype(F32))
            s = jnp.concatenate([e, carry[:, cols]], axis=0)
            k = 1
            while k < w:
                s = s + pltpu.roll(s, n_ext - k, 0)
                k *= 2
            du_ref[:, cols] = s[:tm, :] - ddiff
            carry[:, cols] = e[:POOL_HALO, :]

    return pl.pallas_call(
        body, name="pool_bwd", grid=(nt,),
        in_specs=[pl.BlockSpec((tm, D_POOL), lambda i: (nt - 1 - i, 0)),
                  pl.BlockSpec((tm, D_POOL), lambda i: (nt - 1 - i, 0)),
                  pl.BlockSpec((POOL_HALO, D_POOL), lambda i: (jnp.maximum((nt - 1 - i) * halo_per_tile - 1, 0), 0)),
                  pl.BlockSpec((None, 4, POOL_GROUP, POOL_GROUP), lambda i: (layer, 0, 0, 0)),
                  pl.BlockSpec((None, 1, D_POOL), lambda i: (layer, 0, 0))],
        out_specs=[pl.BlockSpec((tm, D_POOL), lambda i: (nt - 1 - i, 0)),
                   pl.BlockSpec((4, POOL_GROUP, POOL_GROUP), lambda i: (0, 0, 0)),
                   pl.BlockSpec((1, D_POOL), lambda i: (0, 0))],
        out_shape=[jax.ShapeDtypeStruct((L, D_POOL), F32),
                   jax.ShapeDtypeStruct((4, POOL_GROUP, POOL_GROUP), F32),
                   jax.ShapeDtypeStruct((1, D_POOL), F32)],
        scratch_shapes=[pltpu.VMEM((POOL_HALO, D_POOL), F32)],
        compiler_params=_cparams(1),
    )(dyp, u, u, w_pool, scale)


def _mix_in_bwd(dup, dus, h, dhm, g1, wp, layer, gbuf):
    L = h.shape[0]
    tm = min(TM, L)
    n_steps = L // tm
    blk, idx = P_IN_BLK

    def body(dup_ref, dus_ref, h_ref, dhm_ref, g_ref, w_ref, g1_in, dh_ref, dg_ref, g1_ref, dwin):
        i = pl.program_id(0)

        @pl.when(i == 0)
        def _():
            dg_ref[...] = jnp.zeros_like(dg_ref)
            dwin[...] = jnp.zeros_like(dwin)

        du = jnp.concatenate([dup_ref[...], dus_ref[...]], axis=1).astype(BF16)
        dn1 = _dot_nt(du, w_ref[...].reshape(D_MODEL, D_MODEL))
        xhat, r = _rms_hat(h_ref[...])
        g = g_ref[...]
        n1 = (xhat * g).astype(BF16)
        dwin[...] += _dot_tn(n1, du).reshape(N_SHARD, blk, D_MODEL)
        dg_ref[...] += jnp.sum(dn1 * xhat, axis=0, keepdims=True)
        dh_ref[...] = dhm_ref[...] + _rms_bwd(dn1 * g, xhat, r)

        @pl.when(i == n_steps - 1)
        def _():
            pltpu.sync_copy(dwin, g1_ref.at[0, :, pl.ds(blk * idx, blk), :])

    row_spec = pl.BlockSpec((tm, D_MODEL), lambda i: (i, 0))
    half_spec = pl.BlockSpec((tm, D_POOL), lambda i: (i, 0))
    return pl.pallas_call(
        body, name="mix_in_bwd", grid=(n_steps,),
        in_specs=[half_spec, half_spec, row_spec, row_spec,
                  pl.BlockSpec((None, 1, D_MODEL), lambda i: (layer, 0, 0)),
                  pl.BlockSpec((N_SHARD, None, blk, D_MODEL), lambda i: (0, 0, idx, 0)),
                  pl.BlockSpec(memory_space=pl.ANY)],
        out_specs=[row_spec, pl.BlockSpec((1, D_MODEL), lambda i: (0, 0)), pl.BlockSpec(memory_space=pl.ANY)],
        out_shape=[jax.ShapeDtypeStruct((L, D_MODEL), F32), jax.ShapeDtypeStruct((1, D_MODEL), F32),
                   jax.ShapeDtypeStruct(gbuf.shape, F32)],
        scratch_shapes=[pltpu.VMEM((N_SHARD, blk, D_MODEL), F32)],
        input_output_aliases={6: 2},
        compiler_params=_cparams(1),
    )(dup, dus, h, dhm, g1, wp, gbuf)


def _disc_math(lr, li, ldt, br_t, bi_t):
    dt = jnp.exp(ldt)
    mag = jnp.exp(lr * dt)
    ang = li * dt
    ar = mag * jnp.cos(ang)
    ai = mag * jnp.sin(ang)
    den = lr * lr + li * li
    nr, ni = ar - 1.0, ai
    cr = (nr * lr + ni * li) / den
    ci = (ni * lr - nr * li) / den
    return ar, ai, cr * br_t - ci * bi_t, cr * bi_t + ci * br_t


def _disc_fwd(lr, li, ldt, br_t, bi_t):
    def body(lr_ref, li_ref, ldt_ref, br_ref, bi_ref, ar_ref, ai_ref, bbr_ref, bbi_ref):
        ar, ai, bbr, bbi = _disc_math(lr_ref[...], li_ref[...], ldt_ref[...], br_ref[...], bi_ref[...])
        ar_ref[...] = ar
        ai_ref[...] = ai
        bbr_ref[...] = bbr
        bbi_ref[...] = bbi

    shapes = [jax.ShapeDtypeStruct(a.shape, F32) for a in (lr, li, br_t, bi_t)]
    return pl.pallas_call(body, name="ssm_disc_fwd", out_shape=shapes,
                          compiler_params=pltpu.CompilerParams(vmem_limit_bytes=VMEM_LIMIT))(lr, li, ldt, br_t, bi_t)


def _disc_bwd(lr, li, ldt, br_t, bi_t, dar, dai, dbbr, dbbi):
    def body(lr_ref, li_ref, ldt_ref, br_ref, bi_ref, dar_ref, dai_ref, dbbr_ref, dbbi_ref,
             dlr_ref, dli_ref, dldt_ref, dbr_ref, dbi_ref):
        prim = (lr_ref[...], li_ref[...], ldt_ref[...], br_ref[...], bi_ref[...])
        _, pullback = jax.vjp(_disc_math, *prim)
        dlr, dli, dldt, dbr, dbi = pullback((dar_ref[...], dai_ref[...], dbbr_ref[...], dbbi_ref[...]))
        dlr_ref[...] = dlr
        dli_ref[...] = dli
        dldt_ref[...] = dldt
        dbr_ref[...] = dbr
        dbi_ref[...] = dbi

    shapes = [jax.ShapeDtypeStruct(a.shape, F32) for a in (lr, li, ldt, br_t, bi_t)]
    return pl.pallas_call(body, name="ssm_disc_bwd", out_shape=shapes,
                          compiler_params=pltpu.CompilerParams(vmem_limit_bytes=VMEM_LIMIT))(
        lr, li, ldt, br_t, bi_t, dar, dai, dbbr, dbbi)


def _pad_pairs(m_re, m_im):
    def blocks(m):
        v = m.transpose(0, 2, 1).reshape(N_PAIRS, 2, SSM_GROUP, SSM_STATE)
        return jnp.einsum("ab,jahp->jahbp", jnp.eye(2, dtype=m.dtype), v).reshape(N_PAIRS, 32, 128)
    both = jnp.concatenate([blocks(m_re), blocks(m_im)], axis=-1)
    place = jax.nn.one_hot(jnp.arange(N_PAIRS) % 4, 4, dtype=both.dtype)
    return jnp.einsum("jk,jrc->jkrc", place, both).reshape(N_PAIRS, 128, 256)


def _unpad_pairs(x):
    place = jax.nn.one_hot(jnp.arange(N_PAIRS) % 4, 4, dtype=x.dtype)
    both = jnp.einsum("jk,jkrc->jrc", place, x.reshape(N_PAIRS, 4, 32, 256))

    def unblock(v):
        v = v.reshape(N_PAIRS, 2, SSM_GROUP, 2, SSM_STATE)
        d = jnp.einsum("ab,jahbp->jahp", jnp.eye(2, dtype=x.dtype), v)
        return d.reshape(N_SSM_GROUPS, SSM_GROUP, SSM_STATE).transpose(0, 2, 1)
    return unblock(both[..., :128]), unblock(both[..., 128:])


def _adamw_math(w, g, m, v):
    m = ADAM_B1 * m + (1.0 - ADAM_B1) * g
    v = ADAM_B2 * v + (1.0 - ADAM_B2) * (g * g)
    m_hat = m / (1.0 - ADAM_B1 ** ADAM_STEP)
    v_hat = v / (1.0 - ADAM_B2 ** ADAM_STEP)
    delta = -ADAM_LR * (m_hat / (jnp.sqrt(v_hat) + ADAM_EPS) + ADAM_WD * w)
    return delta, m, v


def _adamw(name, layer, w, m, v, gbuf, g_block, g_row0, row_tile, outs=None, after=(), glu=False):
    nl, r, c = w.shape
    n_tiles = r // row_tile
    g_rows, g_cols = g_block
    g_tile = g_rows // n_tiles
    g_off = g_row0 // g_tile
    if outs is None:
        outs = [lax.empty(w.shape, F32) for _ in range(4)]

    def body(w_ref, m_ref, v_ref, g_ref, *rest):
        go_ref, d_ref, mo_ref, vo_ref = rest[-4:]
        g = g_ref[...]
        if glu:
            g = jnp.concatenate([g[:, :D_SSM], g[:, D_SSM:]], axis=0)
        delta, mn, vn = _adamw_math(w_ref[...], g, m_ref[...], v_ref[...])
        go_ref[...] = g
        d_ref[...] = delta
        mo_ref[...] = mn
        vo_ref[...] = vn

    w_spec = pl.BlockSpec((None, row_tile, c), lambda j: (layer, j, 0))
    shape = jax.ShapeDtypeStruct(w.shape, F32)
    return pl.pallas_call(
        body, name=name, grid=(n_tiles,),
        in_specs=[w_spec, w_spec, w_spec, pl.BlockSpec((None, g_tile, g_cols), lambda j: (0, g_off + j, 0))]
        + [_ANY] * (4 + len(after)),
        out_specs=[w_spec] * 4,
        out_shape=[shape] * 4,
        input_output_aliases={4: 0, 5: 1, 6: 2, 7: 3},
        compiler_params=_cparams(1),
    )(w, m, v, gbuf, *outs, *after)


def _pack_weights(ids, layer, w_in, w_glu, w_out, w_down, w_gate_t, w_up_t):
    gb, gi = P_GLU_BLK
    ib, ii = P_IN_BLK
    ob, oi = P_OUT_BLK

    def body(ids_ref, in_ref, glu_ref, out_ref, dn_ref, gate_ref, up_ref, p_ref):
        p_ref[0:FF_SHARD, :] = dn_ref[...].astype(BF16)
        p_ref[FF_SHARD:2 * FF_SHARD, :] = gate_ref[...].astype(BF16)
        p_ref[2 * FF_SHARD:P_FF_ROWS, :] = up_ref[...].astype(BF16)
        g = glu_ref[...]
        p_ref[gb * gi:gb * (gi + 1), :] = jnp.concatenate([g[:gb, :], g[gb:, :]], axis=1).astype(BF16)
        p_ref[gb * (gi + 1):ib * ii, :] = jnp.zeros((P_GLU_PAD - gb, D_MODEL), BF16)
        p_ref[ib * ii:ib * (ii + 1), :] = in_ref[...].astype(BF16)
        p_ref[ob * oi:ob * (oi + 1), :] = out_ref[...].astype(BF16)

    def spec(a):
        return pl.BlockSpec((None,) + a.shape[1:], lambda i, ids_ref: (layer, 0, 0))

    ins = (w_in, w_glu, w_out, w_down, w_gate_t, w_up_t)
    grid_spec = pltpu.PrefetchScalarGridSpec(
        num_scalar_prefetch=1, grid=(1,),
        in_specs=[spec(a) for a in ins],
        out_specs=pl.BlockSpec((None, None, P_ROWS, D_MODEL), lambda i, ids_ref: (ids_ref[1], 0, 0, 0)))
    return pl.pallas_call(
        body, name="pack_weights", grid_spec=grid_spec,
        out_shape=jax.ShapeDtypeStruct((N_SHARD, 1, P_ROWS, D_MODEL), BF16),
        compiler_params=_cparams(1),
    )(ids, *ins)


MESH = pl.DeviceIdType.MESH
_ANY = pl.BlockSpec(memory_space=pl.ANY)
P_HALF = P_ROWS // 2
RS_ROW_TILE = 352


def _mesh_pos():
    return lax.axis_index("x"), lax.axis_index("y"), lax.axis_index("c")


def _other_chips(x, y):
    return [(1 - x, y), (x, 1 - y), (1 - x, 1 - y)]


def _remote(src, dst, send_sems, recv_sems, n, to):
    return pltpu.make_async_remote_copy(src_ref=src, dst_ref=dst, send_sem=send_sems.at[n],
                                        recv_sem=recv_sems.at[n], device_id=to, device_id_type=MESH)


_HBM = pl.BlockSpec(memory_space=pltpu.HBM)
_SEM = pl.BlockSpec(memory_space=pltpu.SEMAPHORE)
_EFFECT = pltpu.CompilerParams(has_side_effects=pltpu.SideEffectType.DATAFLOW_SIDE_EFFECTING)
_TOKEN = jax.ShapeDtypeStruct((8, 128), F32)


def _in_hbm(a):
    return pltpu.with_memory_space_constraint(a, pltpu.HBM)


def _ag_start(name, wp, after):
    def body(w_ref, after_ref, send_sems, recv_sems, w_thru, token):
        x, y, c = _mesh_pos()
        mine = w_ref.at[2 * x + y, :, pl.ds(c * P_HALF, P_HALF), :]
        for j, (px, py) in enumerate(_other_chips(x, y)):
            _remote(mine, mine, send_sems, recv_sems, j, (px, py, c)).start()
        token[...] = jnp.zeros_like(token)

    return pl.pallas_call(
        body, name=name,
        out_shape=(pltpu.SemaphoreType.DMA((3,)), pltpu.SemaphoreType.DMA((3,)), pltpu.HBM(wp.shape, wp.dtype), _TOKEN),
        in_specs=(_HBM, _ANY), out_specs=(_SEM, _SEM, _HBM, pl.BlockSpec(memory_space=pltpu.VMEM)),
        input_output_aliases={0: 2}, compiler_params=_EFFECT,
    )(_in_hbm(wp), after)


def _ag_wait(name, send_sems, recv_sems, wp, after):
    def body(w_ref, send_sems, recv_sems, *rest):
        x, y, c = _mesh_pos()
        mine = w_ref.at[2 * x + y, :, pl.ds(c * P_HALF, P_HALF), :]
        for j, (px, py) in enumerate(_other_chips(x, y)):
            landed = w_ref.at[2 * px + py, :, pl.ds(c * P_HALF, P_HALF), :]
            cp = _remote(mine, landed, send_sems, recv_sems, j, (px, py, c))
            cp.wait_send()
            cp.wait_recv()

    return pl.pallas_call(
        body, name=name, out_shape=pltpu.HBM(wp.shape, wp.dtype),
        in_specs=(_HBM, _SEM, _SEM) + (_ANY,) * len(after), out_specs=_HBM,
        input_output_aliases={0: 0}, compiler_params=_EFFECT,
    )(wp, send_sems, recv_sems, *after)


def _ag_forward(wp):
    def body(w_in, o, send_sems, recv_sems):
        x, y, c = _mesh_pos()
        sib = (x, y, 1 - c)
        chips = _other_chips(x, y)
        sends = []
        for j, (px, py) in enumerate(chips):
            landed = o.at[2 * px + py, :, pl.ds(c * P_HALF, P_HALF), :]
            cp = _remote(landed, landed, send_sems, recv_sems, j, sib)
            cp.start()
            sends.append(cp)
        for j, (px, py) in enumerate(chips):
            passed = o.at[2 * px + py, :, pl.ds((1 - c) * P_HALF, P_HALF), :]
            _remote(passed, passed, send_sems, recv_sems, j, sib).wait_recv()
        for cp in sends:
            cp.wait_send()

    return pl.pallas_call(
        body, name="ag_forward",
        in_specs=[_ANY], out_specs=_ANY,
        out_shape=jax.ShapeDtypeStruct(wp.shape, wp.dtype),
        scratch_shapes=[pltpu.SemaphoreType.DMA((3,)), pltpu.SemaphoreType.DMA((3,))],
        input_output_aliases={0: 0},
    )(wp)


def _rs_chips_start(name, t):
    nl = t.shape[0]

    def body(t_ref, land_ref, send_sems, recv_sems, t_thru, land_thru, token):
        x, y, c = _mesh_pos()
        for j, (px, py) in enumerate(_other_chips(x, y)):
            _remote(t_ref.at[:, 2 * px + py], land_ref.at[j], send_sems, recv_sems, j, (px, py, c)).start()
        token[...] = jnp.zeros_like(token)

    land = lax.empty((3, nl, P_HALF, D_MODEL), BF16)
    return pl.pallas_call(
        body, name=name,
        out_shape=(pltpu.SemaphoreType.DMA((3,)), pltpu.SemaphoreType.DMA((3,)), pltpu.HBM(t.shape, t.dtype),
                   pltpu.HBM(land.shape, land.dtype), _TOKEN),
        in_specs=(_HBM, _HBM), out_specs=(_SEM, _SEM, _HBM, _HBM, pl.BlockSpec(memory_space=pltpu.VMEM)),
        input_output_aliases={0: 2, 1: 3}, compiler_params=_EFFECT,
    )(_in_hbm(t), _in_hbm(land))


def _rs_chips_wait(name, send_sems, recv_sems, t, land, after):
    def body(t_ref, land_ref, send_sems, recv_sems, *rest):
        x, y, c = _mesh_pos()
        for j, (px, py) in enumerate(_other_chips(x, y)):
            cp = _remote(t_ref.at[:, 2 * px + py], land_ref.at[j], send_sems, recv_sems, j, (px, py, c))
            cp.wait_send()
            cp.wait_recv()

    return pl.pallas_call(
        body, name=name, out_shape=(pltpu.HBM(t.shape, t.dtype), pltpu.HBM(land.shape, land.dtype)),
        in_specs=(_HBM, _HBM, _SEM, _SEM) + (_ANY,) * len(after), out_specs=(_HBM, _HBM),
        input_output_aliases={0: 0, 1: 1}, compiler_params=_EFFECT,
    )(t, land, send_sems, recv_sems, *after)[1]


def _rs_sibling_start(name, g):
    nl = g.shape[0]

    def body(g_ref, land_ref, send_sems, recv_sems, g_thru, land_thru, token):
        x, y, c = _mesh_pos()
        _remote(g_ref.at[:, :, pl.ds((1 - c) * P_HALF, P_HALF), :], land_ref, send_sems, recv_sems, 0,
                (x, y, 1 - c)).start()
        token[...] = jnp.zeros_like(token)

    land = lax.empty((nl, N_SHARD, P_HALF, D_MODEL), F32)
    return pl.pallas_call(
        body, name=name,
        out_shape=(pltpu.SemaphoreType.DMA((1,)), pltpu.SemaphoreType.DMA((1,)), pltpu.HBM(g.shape, g.dtype),
                   pltpu.HBM(land.shape, land.dtype), _TOKEN),
        in_specs=(_HBM, _HBM), out_specs=(_SEM, _SEM, _HBM, _HBM, pl.BlockSpec(memory_space=pltpu.VMEM)),
        input_output_aliases={0: 2, 1: 3}, compiler_params=_EFFECT,
    )(_in_hbm(g), _in_hbm(land))


def _rs_sibling_wait(name, send_sems, recv_sems, g, land, after):
    def body(g_ref, land_ref, send_sems, recv_sems, *rest):
        x, y, c = _mesh_pos()
        cp = _remote(g_ref.at[:, :, pl.ds((1 - c) * P_HALF, P_HALF), :], land_ref, send_sems, recv_sems, 0,
                     (x, y, 1 - c))
        cp.wait_send()
        cp.wait_recv()

    return pl.pallas_call(
        body, name=name, out_shape=(pltpu.HBM(g.shape, g.dtype), pltpu.HBM(land.shape, land.dtype)),
        in_specs=(_HBM, _HBM, _SEM, _SEM) + (_ANY,) * len(after), out_specs=(_HBM, _HBM),
        input_output_aliases={0: 0, 1: 1}, compiler_params=_EFFECT,
    )(g, land, send_sems, recv_sems, *after)


def _rs_add(name, ids, g, buf, row_tile):
    nl, _, hr, cols = buf.shape
    n_rt = hr // row_tile

    def body(ids_ref, g_ref, b_ref, own_ref, tb_ref):
        t = g_ref[...] + b_ref[...]
        tb_ref[...] = t.astype(BF16)

        @pl.when(pl.program_id(2) == ids_ref[1])
        def _():
            own_ref[...] = t

    blk = (None, None, row_tile, cols)
    grid_spec = pltpu.PrefetchScalarGridSpec(
        num_scalar_prefetch=1, grid=(nl, n_rt, N_SHARD),
        in_specs=[pl.BlockSpec(blk, lambda l, j, s, ids_ref: (l, s, ids_ref[0] * n_rt + j, 0)),
                  pl.BlockSpec(blk, lambda l, j, s, ids_ref: (l, s, j, 0))],
        out_specs=[pl.BlockSpec((None, row_tile, cols), lambda l, j, s, ids_ref: (l, j, 0)),
                   pl.BlockSpec(blk, lambda l, j, s, ids_ref: (l, s, j, 0))])
    return pl.pallas_call(
        body, name=name, grid_spec=grid_spec,
        out_shape=[jax.ShapeDtypeStruct((nl, hr, cols), F32), jax.ShapeDtypeStruct(buf.shape, BF16)],
        compiler_params=_cparams(3),
    )(ids, g, buf)


def _rs_sum(ids, layer, own, bufb, reduced, row_tile):
    _, hr, cols = own.shape
    n_rt = hr // row_tile

    def body(ids_ref, own_ref, b_ref, reduced_in, f_ref):
        f_ref[...] = ((own_ref[...] + b_ref[0].astype(F32)) + b_ref[1].astype(F32)) + b_ref[2].astype(F32)

    grid_spec = pltpu.PrefetchScalarGridSpec(
        num_scalar_prefetch=1, grid=(n_rt,),
        in_specs=[pl.BlockSpec((None, row_tile, cols), lambda j, ids_ref: (0, j, 0)),
                  pl.BlockSpec((3, None, row_tile, cols), lambda j, ids_ref: (0, 0, j, 0)),
                  pl.BlockSpec(memory_space=pl.ANY)],
        out_specs=pl.BlockSpec((None, row_tile, cols), lambda j, ids_ref: (layer, ids_ref[0] * n_rt + j, 0)))
    return pl.pallas_call(
        body, name="rs_sum", grid_spec=grid_spec,
        out_shape=jax.ShapeDtypeStruct(reduced.shape, F32),
        input_output_aliases={3: 0},
        compiler_params=_cparams(1),
    )(ids, own, bufb, reduced)


def _rs_exchange(f, layer):
    def body(f_in, o, send_sems, recv_sems):
        x, y, c = _mesh_pos()
        mine = o.at[layer, pl.ds(c * P_HALF, P_HALF), :]
        cp = _remote(mine, mine, send_sems, recv_sems, 0, (x, y, 1 - c))
        cp.start()
        cp.wait_send()
        theirs = o.at[layer, pl.ds((1 - c) * P_HALF, P_HALF), :]
        _remote(theirs, theirs, send_sems, recv_sems, 0, (x, y, 1 - c)).wait_recv()

    return pl.pallas_call(
        body, name="rs_exchange",
        in_specs=[_ANY], out_specs=_ANY,
        out_shape=jax.ShapeDtypeStruct(f.shape, F32),
        scratch_shapes=[pltpu.SemaphoreType.DMA((1,)), pltpu.SemaphoreType.DMA((1,))],
        input_output_aliases={0: 0},
    )(f)


def _small_all_reduce(s):
    n_rows = s.shape[0]
    hr = n_rows // 2

    def body(s_ref, o_ref, sibbuf, tbuf, cbuf, fbuf, send_sems, recv_sems):
        x, y, c = _mesh_pos()
        sib = (x, y, 1 - c)
        mine = pl.ds(pl.multiple_of(c * hr, SUBLANES), hr)
        theirs = pl.ds(pl.multiple_of((1 - c) * hr, SUBLANES), hr)
        first = _remote(s_ref.at[theirs], sibbuf, send_sems, recv_sems, 0, sib)
        first.start()
        first.wait()
        tbuf[...] = s_ref[mine, :] + sibbuf[...]
        cps = []
        for j, (px, py) in enumerate(_other_chips(x, y)):
            cp = _remote(tbuf, cbuf.at[j], send_sems, recv_sems, 1 + j, (px, py, c))
            cp.start()
            cps.append(cp)
        for cp in cps:
            cp.wait()
        f = (tbuf[...] + cbuf[1]) + (cbuf[0] + cbuf[2])
        fbuf[...] = f
        o_ref[mine, :] = f
        last = _remote(fbuf, o_ref.at[mine], send_sems, recv_sems, 4, sib)
        last.start()
        last.wait()

    vmem = pl.BlockSpec(memory_space=pltpu.VMEM)
    return pl.pallas_call(
        body, name="small_all_reduce",
        in_specs=[vmem], out_specs=vmem,
        out_shape=jax.ShapeDtypeStruct(s.shape, F32),
        scratch_shapes=[pltpu.VMEM((hr, D_MODEL), F32), pltpu.VMEM((hr, D_MODEL), F32),
                        pltpu.VMEM((3, hr, D_MODEL), F32), pltpu.VMEM((hr, D_MODEL), F32),
                        pltpu.SemaphoreType.DMA((5,)), pltpu.SemaphoreType.DMA((5,))],
        compiler_params=pltpu.CompilerParams(vmem_limit_bytes=VMEM_LIMIT),
    )(s)


_SMALL = ("norm_mix", "w_pool", "pool_scale", "lam_re", "lam_im", "log_dt", "b_re", "b_im", "c_re", "c_im",
          "d_skip", "b_glu", "norm_ffn", "norm_final")
_WEIGHTS = ("norm_mix", "w_in", "w_pool", "pool_scale", "lam_re", "lam_im", "log_dt", "b_re", "b_im", "c_re",
            "c_im", "d_skip", "w_glu", "b_glu", "w_out", "norm_ffn", "w_gate", "w_up", "w_down", "norm_final")


def _local_step(x, target, p, get_weights, ffn_bwd_done, put_grads):
    nl = p["norm_mix"].shape[0]

    def tied(a, token):
        return a if token is None else a + token
    n_rows = nl * N_SSM_GROUPS
    lr = p["lam_re"].reshape(n_rows, 1, SSM_STATE)
    li = p["lam_im"].reshape(n_rows, 1, SSM_STATE)
    ldt = p["log_dt"].reshape(n_rows, 1, 1)
    br_t = p["b_re"].reshape(n_rows, SSM_STATE, SSM_GROUP).transpose(0, 2, 1)
    bi_t = p["b_im"].reshape(n_rows, SSM_STATE, SSM_GROUP).transpose(0, 2, 1)
    ar, ai, bbr_t, bbi_t = _disc_fwd(lr, li, ldt, br_t, bi_t)
    ar = ar.reshape(nl, 1, N_STATE)
    ai = ai.reshape(nl, 1, N_STATE)
    bbr = bbr_t.transpose(0, 2, 1).reshape(nl, N_SSM_GROUPS, SSM_STATE, SSM_GROUP)
    bbi = bbi_t.transpose(0, 2, 1).reshape(nl, N_SSM_GROUPS, SSM_STATE, SSM_GROUP)
    w_pool = p["w_pool"].astype(BF16)
    p = dict(p)
    for n in ("norm_mix", "pool_scale", "b_glu", "norm_ffn"):
        p[n] = p[n].reshape(nl, 1, -1)
    swap = lambda a: jnp.swapaxes(a, -1, -2)
    bpad = jax.vmap(_pad_pairs)(bbr, bbi).astype(BF16)
    cpad_t = jax.vmap(_pad_pairs)(swap(p["c_re"]), -swap(p["c_im"])).astype(BF16)
    bpad_t, cpad = swap(bpad), swap(cpad_t)
    dskip = p["d_skip"].reshape(nl, 1, D_SSM)

    layers = []
    h = x
    for l in range(nl):
        wp = get_weights(l, [h] if l else [h, bpad, cpad, bpad_t, cpad_t, ar, ai])
        u, ypool = _mix_in_fwd(h, p["norm_mix"], wp, l, w_pool, p["pool_scale"])
        sre, sim, yraw = _ssm_fwd(u, l, bpad, cpad, ar, ai, dskip)
        hm = _mix_out_fwd(yraw, ypool, h, wp, l, p["b_glu"])
        h_next, n2, gate_s, up_s = _ffn_fwd(hm, p["norm_ffn"], wp, l)
        layers.append(dict(h=h, u=u, ypool=ypool, sre=sre, sim=sim, yraw=yraw, hm=hm, n2=n2, gate_s=gate_s, wp=wp,
                           up_s=up_s))
        h = h_next

    dh, loss, d_norm_final = _final_fwd_bwd(h, p["norm_final"].reshape(1, D_MODEL), target)

    raw = {n: [None] * nl for n in ("dg1", "dwp", "dsc", "dcp", "dbp", "ddsk", "db_glu", "dg2", "dar", "dai")}
    token = None
    for l in reversed(range(nl)):
        s = layers[l]
        wp = s["wp"]
        g1 = lax.empty((1, N_SHARD, P_ROWS, D_MODEL), F32)
        dhm, dg2, dgate_s, dup_s, act_s, dhb = _ffn_bwd_act(dh, s["hm"], tied(p["norm_ffn"], token),
                                                             s["gate_s"], s["up_s"], wp, l)
        g1 = _ffn_bwd_w(s["n2"], dgate_s, dup_s, act_s, dhb, g1)
        token = ffn_bwd_done(l, [g1])
        dyraw, dyp, db_glu, g1 = _mix_out_bwd(dhm, s["yraw"], s["ypool"], wp, l, tied(p["b_glu"], token), g1)
        dus, dcp, dbp, dar, dai, ddsk = _ssm_bwd(dyraw, s["u"], s["sre"], s["sim"], l, cpad_t, bpad_t, ar, ai, dskip)
        dup, dwp, dsc = _pool_bwd(dyp, s["u"], l, w_pool, p["pool_scale"])
        dh, dg1, g1 = _mix_in_bwd(dup, dus, s["h"], dhm, p["norm_mix"], wp, l, g1)
        token = put_grads(l, g1)
        for n, a in (("dg1", dg1), ("dwp", dwp), ("dsc", dsc), ("dcp", dcp), ("dbp", dbp), ("ddsk", ddsk),
                     ("db_glu", db_glu), ("dg2", dg2), ("dar", dar), ("dai", dai)):
            raw[n][l] = a

    st = {n: jnp.stack(v) for n, v in raw.items()}
    dc_re, dc_im = jax.vmap(_unpad_pairs)(swap(st["dcp"]))
    dbbr, dbbi = jax.vmap(_unpad_pairs)(st["dbp"])
    rows = lambda a: a.reshape((n_rows,) + a.shape[2:])
    dlr, dli, dldt, dbr_t, dbi_t = _disc_bwd(lr, li, ldt, br_t, bi_t, st["dar"].reshape(n_rows, 1, SSM_STATE),
                                              st["dai"].reshape(n_rows, 1, SSM_STATE), rows(swap(dbbr)),
                                              rows(swap(dbbi)))
    small = {"norm_mix": st["dg1"][:, 0], "w_pool": st["dwp"], "pool_scale": st["dsc"][:, 0], "c_re": swap(dc_re),
             "c_im": -swap(dc_im), "d_skip": st["ddsk"].reshape(nl, N_SSM_GROUPS, SSM_GROUP),
             "b_glu": st["db_glu"][:, 0], "norm_ffn": st["dg2"][:, 0]}
    small["lam_re"] = dlr.reshape(nl, N_SSM_GROUPS, SSM_STATE)
    small["lam_im"] = dli.reshape(nl, N_SSM_GROUPS, SSM_STATE)
    small["log_dt"] = dldt.reshape(nl, N_SSM_GROUPS)
    small["b_re"] = dbr_t.transpose(0, 2, 1).reshape(nl, N_SSM_GROUPS, SSM_STATE, SSM_GROUP)
    small["b_im"] = dbi_t.transpose(0, 2, 1).reshape(nl, N_SSM_GROUPS, SSM_STATE, SSM_GROUP)
    small["norm_final"] = d_norm_final[0]
    return loss, dh, small


def _flatten_small(d):
    flat = jnp.concatenate([d[n].reshape(-1) for n in _SMALL])
    n_rows = -(-flat.shape[0] // (32 * D_MODEL)) * 32
    return jnp.pad(flat, (0, n_rows * D_MODEL - flat.shape[0])).reshape(n_rows, D_MODEL)


def _split_small(flat, like):
    flat = flat.reshape(-1)
    out, at = {}, 0
    for n in _SMALL:
        size = like[n].size
        out[n] = flat[at:at + size].reshape(like[n].shape)
        at += size
    return out


def kernel(x, norm_mix, w_in, w_pool, pool_scale, lam_re, lam_im, log_dt, b_re, b_im, c_re, c_im, d_skip, w_glu, b_glu, w_out, norm_ffn, w_gate, w_up, w_down, norm_final, loss_target, m_norm_mix, m_w_in, m_w_pool, m_pool_scale, m_lam_re, m_lam_im, m_log_dt, m_b_re, m_b_im, m_c_re, m_c_im, m_d_skip, m_w_glu, m_b_glu, m_w_out, m_norm_ffn, m_w_gate, m_w_up, m_w_down, m_norm_final, v_norm_mix, v_w_in, v_w_pool, v_pool_scale, v_lam_re, v_lam_im, v_log_dt, v_b_re, v_b_im, v_c_re, v_c_im, v_d_skip, v_w_glu, v_b_glu, v_w_out, v_norm_ffn, v_w_gate, v_w_up, v_w_down, v_norm_final):
    given = dict(locals())
    w = {n: given[n] for n in _WEIGHTS}
    m = {n: given["m_" + n] for n in _WEIGHTS}
    v = {n: given["v_" + n] for n in _WEIGHTS}
    ids = jnp.stack([lax.axis_index("c"), 2 * lax.axis_index("x") + lax.axis_index("y")]).astype(jnp.int32)

    t_names = ("w_gate", "w_up")
    tr = lambda a: a.transpose(0, 2, 1)
    for d in (w, m, v):
        d.update({n: tr(d[n]) for n in t_names})

    nl = norm_mix.shape[0]
    packed = [_pack_weights(ids, l, w["w_in"], w["w_glu"], w["w_out"], w["w_down"], w["w_gate"], w["w_up"])
              for l in range(nl)]
    started, last = {}, ids
    for l in range(nl):
        started[l] = _ag_start(f"ag_start_{l}", packed[l], last)
        last = started[l][3]
    flat = [_flatten_small(d)[None] for d in (w, m, v)]

    def get_weights(l, after):
        send_sems, recv_sems, buf, _ = started[l]
        after = after + ([last] + flat if l == 0 else [])
        return _ag_forward(_ag_wait(f"ag_wait_{l}", send_sems, recv_sems, buf, after))

    to_sibling, to_chips, reduced = {}, {}, {}

    def put_grads(l, g):
        to_sibling[l] = _rs_sibling_start(f"rs_sibling_start_{l}", g)
        token = to_sibling[l][4]
        if l + 1 in to_chips:
            finish(l + 1, [token])
        return token[:1, :1]

    def ffn_bwd_done(l, after):
        return send_to_chips(l + 1, after) if l + 1 in to_sibling else None

    def send_to_chips(l, after):
        send_sems, recv_sems, g, land, _ = to_sibling.pop(l)
        g, land = _rs_sibling_wait(f"rs_sibling_wait_{l}", send_sems, recv_sems, g, land, after)
        own, t = _rs_add("rs_add", ids, g, land, RS_ROW_TILE)
        send_sems, recv_sems, t, land, token = _rs_chips_start(f"rs_chips_start_{l}", t)
        to_chips[l] = (send_sems, recv_sems, t, land, own)
        return token[:1, :1]

    def finish(l, after):
        send_sems, recv_sems, t, land, own = to_chips.pop(l)
        land = _rs_chips_wait(f"rs_chips_wait_{l}", send_sems, recv_sems, t, land, after)
        shard = lax.empty((1, P_ROWS, D_MODEL), F32)
        reduced[l] = _rs_exchange(_rs_sum(ids, 0, own, land, shard, RS_ROW_TILE), 0)

    loss, grad_x, small = _local_step(x[0], loss_target[0], {n: w[n] for n in _SMALL}, get_weights, ffn_bwd_done,
                                      put_grads)
    loss = lax.psum(loss[0, 0], ("x", "y", "c"))
    token = send_to_chips(0, [small["norm_final"]])

    big = (("w_in", P_IN_BLK, 256, False), ("w_out", P_OUT_BLK, 256, False), ("w_down", P_WD_BLK, 352, False),
           ("w_gate", P_WG_BLK, 352, False), ("w_up", P_WU_BLK, 352, False), ("w_glu", P_GLU_BLK, 128, True))
    res = {n: None for n, *_ in big}

    def adamw_layer(l, after):
        for n, (blk, idx), row_tile, glu in big:
            res[n] = _adamw("adamw_" + n, l, w[n], m[n], v[n], reduced[l], (blk, D_MODEL), blk * idx, row_tile,
                            res[n], after, glu)

    for l in reversed(range(1, nl)):
        adamw_layer(l, [token])
    small["norm_final"] = small["norm_final"] + token[0]
    small_sum = _small_all_reduce(_flatten_small(small))
    finish(0, [small_sum] + [r[0] for r in res.values() if r is not None])
    adamw_layer(0, [])
    for n in t_names:
        res[n] = tuple(tr(a) for a in res[n])
    n_rows = flat[0].shape[1]
    outs = _adamw("adamw_small", 0, *flat, small_sum[None], (n_rows, D_MODEL), 0, n_rows // 4)
    parts = [_split_small(o[0], w) for o in outs]
    for n in _SMALL:
        res[n] = tuple(part[n] for part in parts)

    return (loss, grad_x[None], *[res[n][0] for n in _WEIGHTS], *[res[n][1] for n in _WEIGHTS],
            *[res[n][2] for n in _WEIGHTS], *[res[n][3] for n in _WEIGHTS])
```

```python
import functools
import math

import jax
import jax.numpy as jnp
from jax import lax
from jax.experimental import pallas as pl
from jax.experimental.pallas import tpu as pltpu

F32 = jnp.float32
BF16 = jnp.bfloat16

D_MODEL = 1024
D_POOL = 512
D_SSM = 512
POOL_WINDOWS = (2, 4, 8, 16)
POOL_GROUP = 128
POOL_HALO = 16
N_SSM_GROUPS = 32
SSM_GROUP = 16
SSM_STATE = 64
N_STATE = N_SSM_GROUPS * SSM_STATE
N_PAIRS = N_SSM_GROUPS // 2
D_FF = 2816
N_SHARD = 4
FF_SHARD = D_FF // N_SHARD
RMS_EPS = 1e-6

ADAM_LR = 0.001
ADAM_B1 = 0.9
ADAM_B2 = 0.999
ADAM_EPS = 1e-08
ADAM_WD = 0.01
ADAM_STEP = 10

P_ROWS = 2816
P_WD_BLK = (704, 0)
P_WG_BLK = (704, 1)
P_WU_BLK = (704, 2)
P_FF_ROWS = 2112
P_GLU_BLK = (64, 33)
P_GLU_PAD = 192
P_IN_BLK = (256, 9)
P_OUT_BLK = (256, 10)

SUBLANES = 8
VMEM_LIMIT = 56 * 1024 * 1024

TM = 512
TM_FFN = 512
FFN_SPLIT = 2
TS = 1024
SCAN_LANES = 512


def _cparams(n_axes):
    return pltpu.CompilerParams(dimension_semantics=("arbitrary",) * n_axes, vmem_limit_bytes=VMEM_LIMIT)


def _dot(a, b):
    return jnp.dot(a, b, preferred_element_type=F32)


def _dot_nt(a, b):
    return lax.dot_general(a, b, (((1,), (1,)), ((), ())), preferred_element_type=F32)


def _dot_tn(a, b):
    return lax.dot_general(a, b, (((0,), (0,)), ((), ())), preferred_element_type=F32)


def _rms_hat(x):
    r = lax.rsqrt(jnp.mean(x * x, axis=-1, keepdims=True) + RMS_EPS)
    return x * r, r


def _rms_bwd(d_hat, xhat, r):
    return r * (d_hat - xhat * jnp.mean(d_hat * xhat, axis=-1, keepdims=True))


def _sigmoid(x):
    return 1.0 / (1.0 + jnp.exp(-x))


_GELU_C = math.sqrt(2.0 / math.pi)
_GELU_K = 0.044715


def _gelu(x):
    return 0.5 * x * (1.0 + jnp.tanh(_GELU_C * (x + _GELU_K * x * x * x)))


def _gelu_grad(x):
    th = jnp.tanh(_GELU_C * (x + _GELU_K * x * x * x))
    return 0.5 * (1.0 + th) + 0.5 * x * (1.0 - th * th) * _GELU_C * (1.0 + 3.0 * _GELU_K * x * x)


def _glu_weight(ref):
    v = ref[...]
    return jnp.concatenate([v[:, :, :D_SSM], v[:, :, D_SSM:]], axis=1).reshape(D_SSM, D_SSM)


def _glu_pack(w):
    v = w.reshape(N_SHARD, 128, D_SSM)
    return jnp.concatenate([v[:, :64, :], v[:, 64:, :]], axis=2)


def _pool_diff(ext, row0, tm):
    rows = row0 + lax.broadcasted_iota(jnp.int32, (tm, 1), 0)
    outs = []
    for gi, w in enumerate(POOL_WINDOWS):
        e = ext[:, gi * POOL_GROUP:(gi + 1) * POOL_GROUP]
        s = e
        k = 1
        while k < w:
            s = s + pltpu.roll(s, k, 0)
            k *= 2
        inv = 1.0 / jnp.minimum(rows + 1, w).astype(F32)
        outs.append(s[POOL_HALO:, :] * inv - e[POOL_HALO:, :])
    return outs


def _mix_in_fwd(h, g1, wp, layer, w_pool, scale):
    L = h.shape[0]
    tm = min(TM, L)

    def body(h_ref, g_ref, w_ref, wp_ref, sc_ref, u_ref, yp_ref, carry):
        i = pl.program_id(0)

        @pl.when(i == 0)
        def _():
            carry[...] = jnp.zeros_like(carry)

        xhat, _ = _rms_hat(h_ref[...])
        n1 = (xhat * g_ref[...]).astype(BF16)
        u = _dot(n1, w_ref[...].reshape(D_MODEL, D_MODEL))
        u_ref[...] = u
        up = u[:, :D_POOL]
        ext = jnp.concatenate([carry[...], up], axis=0)
        carry[...] = up[tm - POOL_HALO:, :]
        diffs = _pool_diff(ext, i * tm, tm)
        for gi in range(4):
            cols = slice(gi * POOL_GROUP, (gi + 1) * POOL_GROUP)
            yp_ref[:, cols] = _dot(diffs[gi].astype(BF16), wp_ref[gi]) * sc_ref[:, cols]

    blk, idx = P_IN_BLK
    return pl.pallas_call(
        body, name="mix_in_fwd", grid=(L // tm,),
        in_specs=[pl.BlockSpec((tm, D_MODEL), lambda i: (i, 0)),
                  pl.BlockSpec((None, 1, D_MODEL), lambda i: (layer, 0, 0)),
                  pl.BlockSpec((N_SHARD, None, blk, D_MODEL), lambda i: (0, 0, idx, 0)),
                  pl.BlockSpec((None, 4, POOL_GROUP, POOL_GROUP), lambda i: (layer, 0, 0, 0)),
                  pl.BlockSpec((None, 1, D_POOL), lambda i: (layer, 0, 0))],
        out_specs=[pl.BlockSpec((tm, D_MODEL), lambda i: (i, 0)),
                   pl.BlockSpec((tm, D_POOL), lambda i: (i, 0))],
        out_shape=[jax.ShapeDtypeStruct((L, D_MODEL), F32), jax.ShapeDtypeStruct((L, D_POOL), F32)],
        scratch_shapes=[pltpu.VMEM((POOL_HALO, D_POOL), F32)],
        compiler_params=_cparams(1),
    )(h, g1, wp, w_pool, scale)


def _cmul(xr, xi, yr, yi):
    return xr * yr - xi * yi, xr * yi + xi * yr


def _scan_tables(ar, ai, tab, reverse):
    c = ar.shape[1]
    row = lax.broadcasted_iota(jnp.int32, (SUBLANES, c), 0)
    a2r, a2i = _cmul(ar, ai, ar, ai)
    a4r, a4i = _cmul(a2r, a2i, a2r, a2i)
    zero = jnp.zeros((SUBLANES, c), F32)
    for n, (s, pr, pi) in enumerate(((1, ar, ai), (2, a2r, a2i), (4, a4r, a4i))):
        keep = (row < SUBLANES - s) if reverse else (row >= s)
        tab[2 * n] = jnp.where(keep, pr, zero)
        tab[2 * n + 1] = jnp.where(keep, pi, zero)
    cr, ci = ar, ai
    tr, ti = zero, zero
    for n in range(SUBLANES):
        at = (SUBLANES - 1 - n) if reverse else n
        tr = jnp.where(row == at, cr, tr)
        ti = jnp.where(row == at, ci, ti)
        cr, ci = _cmul(cr, ci, ar, ai)
    tab[6] = tr
    tab[7] = ti


def _ssm_fwd(u, layer, bpad, cpad, ar, ai, dskip):
    L = u.shape[0]
    ts = min(TS, L)
    nq = 4
    cq = N_STATE // nq

    def body(u_ref, bp_ref, cp_ref, ar_ref, ai_ref, dsk_ref, sre_ref, sim_ref, y_ref, cr, ci, tab):
        t = pl.program_id(1)

        @pl.when(t == 0)
        def _():
            cr[...] = jnp.zeros_like(cr)
            ci[...] = jnp.zeros_like(ci)
            _scan_tables(ar_ref[...], ai_ref[...], tab, reverse=False)

        uf = u_ref[...]
        ub = uf.astype(BF16)
        for jj in range(4):
            bu = _dot(ub, bp_ref[jj])
            sre_ref[:, jj * 128:(jj + 1) * 128] = bu[:, :128]
            sim_ref[:, jj * 128:(jj + 1) * 128] = bu[:, 128:]

        for cc in range(cq // SCAN_LANES):
            cols = slice(cc * SCAN_LANES, (cc + 1) * SCAN_LANES)
            def step(i, carry, cols=cols):
                c_r, c_i = carry
                r0 = pl.multiple_of(i * SUBLANES, SUBLANES)
                xr = sre_ref[pl.ds(r0, SUBLANES), cols]
                xi = sim_ref[pl.ds(r0, SUBLANES), cols]
                for n, s in enumerate((1, 2, 4)):
                    tr, ti = tab[2 * n, :, cols], tab[2 * n + 1, :, cols]
                    rr = pltpu.roll(xr, s, 0)
                    ri = pltpu.roll(xi, s, 0)
                    xr, xi = xr + tr * rr - ti * ri, xi + tr * ri + ti * rr
                pr, pi = tab[6, :, cols], tab[7, :, cols]
                xr, xi = xr + pr * c_r - pi * c_i, xi + pr * c_i + pi * c_r
                sre_ref[pl.ds(r0, SUBLANES), cols] = xr
                sim_ref[pl.ds(r0, SUBLANES), cols] = xi
                shp = (SUBLANES, SCAN_LANES)
                return (jnp.broadcast_to(xr[SUBLANES - 1:, :], shp), jnp.broadcast_to(xi[SUBLANES - 1:, :], shp))

            c_r, c_i = lax.fori_loop(0, ts // SUBLANES, step, (cr[:, cols], ci[:, cols]), unroll=2)
            cr[:, cols] = c_r
            ci[:, cols] = c_i

        acc = dsk_ref[...] * uf
        for jj in range(4):
            cols = slice(jj * 128, (jj + 1) * 128)
            scat = jnp.concatenate([sre_ref[:, cols], sim_ref[:, cols]], axis=1).astype(BF16)
            acc = acc + _dot(scat, cp_ref[jj])
        y_ref[...] = acc

    return pl.pallas_call(
        body, name="ssm_fwd", grid=(nq, L // ts),
        in_specs=[pl.BlockSpec((ts, 128), lambda q, t: (t, 4 + q)),
                  pl.BlockSpec((None, 4, 128, 256), lambda q, t: (layer, q, 0, 0)),
                  pl.BlockSpec((None, 4, 256, 128), lambda q, t: (layer, q, 0, 0)),
                  pl.BlockSpec((None, 1, cq), lambda q, t: (layer, 0, q)),
                  pl.BlockSpec((None, 1, cq), lambda q, t: (layer, 0, q)),
                  pl.BlockSpec((None, 1, 128), lambda q, t: (layer, 0, q))],
        out_specs=[pl.BlockSpec((ts, cq), lambda q, t: (t, q)),
                   pl.BlockSpec((ts, cq), lambda q, t: (t, q)),
                   pl.BlockSpec((ts, 128), lambda q, t: (t, q))],
        out_shape=[jax.ShapeDtypeStruct((L, N_STATE), F32), jax.ShapeDtypeStruct((L, N_STATE), F32),
                   jax.ShapeDtypeStruct((L, D_SSM), F32)],
        scratch_shapes=[pltpu.VMEM((SUBLANES, cq), F32), pltpu.VMEM((SUBLANES, cq), F32),
                        pltpu.VMEM((8, SUBLANES, cq), F32)],
        compiler_params=_cparams(2),
    )(u, bpad, cpad, ar, ai, dskip)


def _mix_out_fwd(yraw, ypool, h, wp, layer, b_glu):
    L = h.shape[0]
    tm = min(TM, L)

    def body(yr_ref, yp_ref, h_ref, wglu_ref, b_ref, wout_ref, o_ref):
        y = _gelu(yr_ref[...])
        z = _dot(y.astype(BF16), _glu_weight(wglu_ref)) + b_ref[...]
        o = y * _sigmoid(z)
        mix = jnp.concatenate([yp_ref[...], o], axis=1).astype(BF16)
        o_ref[...] = h_ref[...] + _dot(mix, wout_ref[...].reshape(D_MODEL, D_MODEL))

    gb, gi = P_GLU_BLK
    ob, oi = P_OUT_BLK
    return pl.pallas_call(
        body, name="mix_out_fwd", grid=(L // tm,),
        in_specs=[pl.BlockSpec((tm, D_SSM), lambda i: (i, 0)),
                  pl.BlockSpec((tm, D_POOL), lambda i: (i, 0)),
                  pl.BlockSpec((tm, D_MODEL), lambda i: (i, 0)),
                  pl.BlockSpec((N_SHARD, None, gb, D_MODEL), lambda i: (0, 0, gi, 0)),
                  pl.BlockSpec((None, 1, D_SSM), lambda i: (layer, 0, 0)),
                  pl.BlockSpec((N_SHARD, None, ob, D_MODEL), lambda i: (0, 0, oi, 0))],
        out_specs=pl.BlockSpec((tm, D_MODEL), lambda i: (i, 0)),
        out_shape=jax.ShapeDtypeStruct((L, D_MODEL), F32),
        compiler_params=_cparams(1),
    )(yraw, ypool, h, wp, b_glu, wp)


def _ffn_weights(ref, k):
    return ref[k, 0:FF_SHARD, :], ref[k, FF_SHARD:2 * FF_SHARD, :], ref[k, 2 * FF_SHARD:P_FF_ROWS, :]


def _ffn_weight_spec():
    return pl.BlockSpec((N_SHARD, None, P_FF_ROWS, D_MODEL), lambda m, k: (0, 0, 0, 0),
                        pipeline_mode=pl.Buffered(1))


def _ffn_fwd(h, g2, wp, layer):
    L = h.shape[0]
    tm = min(TM_FFN, L)

    def body(h_ref, g_ref, w_ref, o_ref, n2_ref, act_ref, dgate_ref, dup_ref):
        k = pl.program_id(1)

        @pl.when(k == 0)
        def _():
            x = h_ref[...]
            xhat, _ = _rms_hat(x)
            n2_ref[...] = (xhat * g_ref[...]).astype(BF16)
            o_ref[...] = x

        wd, wg_t, wu_t = _ffn_weights(w_ref, k)
        n2 = n2_ref[...]
        gate = _dot_nt(n2, wg_t)
        up = _dot_nt(n2, wu_t)
        sg = _sigmoid(gate)
        silu = gate * sg
        act = (silu * up).astype(BF16)
        act_ref[...] = act
        dgate_ref[...] = (up * (sg * (1.0 + gate * (1.0 - sg)))).astype(BF16)
        dup_ref[...] = silu.astype(BF16)
        o_ref[...] += _dot(act, wd)

    act_shape = jax.ShapeDtypeStruct((N_SHARD, L, FF_SHARD), BF16)
    return pl.pallas_call(
        body, name="ffn_fwd", grid=(L // tm, N_SHARD),
        in_specs=[pl.BlockSpec((tm, D_MODEL), lambda m, k: (m, 0)),
                  pl.BlockSpec((None, 1, D_MODEL), lambda m, k: (layer, 0, 0)),
                  _ffn_weight_spec()],
        out_specs=[pl.BlockSpec((tm, D_MODEL), lambda m, k: (m, 0)),
                   pl.BlockSpec((tm, D_MODEL), lambda m, k: (m, 0)),
                   pl.BlockSpec((None, tm, FF_SHARD), lambda m, k: (k, m, 0)),
                   pl.BlockSpec((None, tm, FF_SHARD), lambda m, k: (k, m, 0)),
                   pl.BlockSpec((None, tm, FF_SHARD), lambda m, k: (k, m, 0))],
        out_shape=[jax.ShapeDtypeStruct((L, D_MODEL), F32), jax.ShapeDtypeStruct((L, D_MODEL), BF16),
                   act_shape, act_shape, act_shape],
        compiler_params=_cparams(2),
    )(h, g2, wp)


def _final_fwd_bwd(h, gf, target):
    L = h.shape[0]
    tm = min(TM, L)

    def body(h_ref, g_ref, t_ref, dh_ref, loss_ref, dg_ref):
        i = pl.program_id(0)

        @pl.when(i == 0)
        def _():
            loss_ref[...] = jnp.zeros_like(loss_ref)
            dg_ref[...] = jnp.zeros_like(dg_ref)

        xhat, r = _rms_hat(h_ref[...])
        g = g_ref[...]
        e = xhat * g - t_ref[...]
        loss_ref[...] += 0.5 * jnp.sum(jnp.mean(e * e, axis=-1, keepdims=True), axis=0, keepdims=True)
        dy = e * (1.0 / D_MODEL)
        dg_ref[...] += jnp.sum(dy * xhat, axis=0, keepdims=True)
        dh_ref[...] = _rms_bwd(dy * g, xhat, r)

    return pl.pallas_call(
        body, name="final_fwd_bwd", grid=(L // tm,),
        in_specs=[pl.BlockSpec((tm, D_MODEL), lambda i: (i, 0)),
                  pl.BlockSpec((1, D_MODEL), lambda i: (0, 0)),
                  pl.BlockSpec((tm, D_MODEL), lambda i: (i, 0))],
        out_specs=[pl.BlockSpec((tm, D_MODEL), lambda i: (i, 0)),
                   pl.BlockSpec((1, 1), lambda i: (0, 0)),
                   pl.BlockSpec((1, D_MODEL), lambda i: (0, 0))],
        out_shape=[jax.ShapeDtypeStruct((L, D_MODEL), F32), jax.ShapeDtypeStruct((1, 1), F32),
                   jax.ShapeDtypeStruct((1, D_MODEL), F32)],
        compiler_params=_cparams(1),
    )(h, gf, target)


def _ffn_bwd_act(dh, h, g2, fgate_s, fup_s, wp, layer):
    L = h.shape[0]
    tm = min(TM_FFN, L)
    sub = tm // FFN_SPLIT

    def body(dh_ref, h_ref, g_ref, fgate_ref, fup_ref, w_ref,
             dhm_ref, dg_ref, dgate_ref, dup_ref, dhb_ref, dn2):
        m, k = pl.program_id(0), pl.program_id(1)

        @pl.when(jnp.logical_and(m == 0, k == 0))
        def _():
            dg_ref[...] = jnp.zeros_like(dg_ref)

        @pl.when(k == 0)
        def _():
            dhb_ref[...] = dh_ref[...].astype(BF16)
            dn2[...] = jnp.zeros_like(dn2)

        wd, wg_t, wu_t = _ffn_weights(w_ref, k)
        for rows in (slice(r * sub, (r + 1) * sub) for r in range(tm // sub)):
            dact = _dot_nt(dhb_ref[rows, :], wd)
            dgate = (dact * fgate_ref[rows, :].astype(F32)).astype(BF16)
            dup = (dact * fup_ref[rows, :].astype(F32)).astype(BF16)
            dgate_ref[rows, :] = dgate
            dup_ref[rows, :] = dup
            dn2[rows, :] += _dot(dgate, wg_t) + _dot(dup, wu_t)

        @pl.when(k == N_SHARD - 1)
        def _():
            xhat, r = _rms_hat(h_ref[...])
            d = dn2[...]
            dg_ref[...] += jnp.sum(d * xhat, axis=0, keepdims=True)
            dhm_ref[...] = dh_ref[...] + _rms_bwd(d * g_ref[...], xhat, r)

    act_spec = pl.BlockSpec((None, tm, FF_SHARD), lambda m, k: (k, m, 0))
    act_shape = jax.ShapeDtypeStruct((N_SHARD, L, FF_SHARD), BF16)
    row_spec = pl.BlockSpec((tm, D_MODEL), lambda m, k: (m, 0))
    return pl.pallas_call(
        body, name="ffn_bwd_act", grid=(L // tm, N_SHARD),
        in_specs=[row_spec, row_spec,
                  pl.BlockSpec((None, 1, D_MODEL), lambda m, k: (layer, 0, 0)),
                  act_spec, act_spec,
                  _ffn_weight_spec()],
        out_specs=[row_spec,
                   pl.BlockSpec((1, D_MODEL), lambda m, k: (0, 0)),
                   act_spec, act_spec, row_spec],
        out_shape=[jax.ShapeDtypeStruct((L, D_MODEL), F32), jax.ShapeDtypeStruct((1, D_MODEL), F32),
                   act_shape, act_shape, jax.ShapeDtypeStruct((L, D_MODEL), BF16)],
        scratch_shapes=[pltpu.VMEM((tm, D_MODEL), F32)],
        compiler_params=_cparams(2),
    )(dh, h, g2, fgate_s, fup_s, wp)


def _ffn_bwd_w(n2, dgate_s, dup_s, act_s, dhb, gbuf):
    L = n2.shape[0]
    tm = min(TM_FFN, L)

    def body(n2_ref, dgate_ref, dup_ref, act_ref, dhb_ref, g_in, g_ref):
        m = pl.program_id(1)

        @pl.when(m == 0)
        def _():
            g_ref[...] = jnp.zeros_like(g_ref)

        n2v = n2_ref[...]
        g_ref[0:FF_SHARD, :] += _dot_tn(act_ref[...], dhb_ref[...])
        g_ref[FF_SHARD:2 * FF_SHARD, :] += _dot_tn(dgate_ref[...], n2v)
        g_ref[2 * FF_SHARD:P_FF_ROWS, :] += _dot_tn(dup_ref[...], n2v)

    act_spec = pl.BlockSpec((None, tm, FF_SHARD), lambda k, m: (k, m, 0))
    row_spec = pl.BlockSpec((tm, D_MODEL), lambda k, m: (m, 0))
    return pl.pallas_call(
        body, name="ffn_bwd_w", grid=(N_SHARD, L // tm),
        in_specs=[row_spec, act_spec, act_spec, act_spec, row_spec, pl.BlockSpec(memory_space=pl.ANY)],
        out_specs=pl.BlockSpec((None, None, P_FF_ROWS, D_MODEL), lambda k, m: (0, k, 0, 0)),
        out_shape=jax.ShapeDtypeStruct(gbuf.shape, F32),
        input_output_aliases={5: 0},
        compiler_params=_cparams(2),
    )(n2, dgate_s, dup_s, act_s, dhb, gbuf)


def _mix_out_bwd(dhm, yraw, ypool, wp, layer, b_glu, gbuf):
    L = dhm.shape[0]
    tm = min(TM, L)

    def body(dhm_ref, yr_ref, yp_ref, wglu_ref, b_ref, wout_ref, g1_in,
             dyr_ref, dyp_ref, db_ref, g1_ref, dwout, dwglu, gpack):
        i = pl.program_id(0)

        @pl.when(i == 0)
        def _():
            db_ref[...] = jnp.zeros_like(db_ref)
            dwout[...] = jnp.zeros_like(dwout)
            dwglu[...] = jnp.zeros_like(dwglu)

        dhb = dhm_ref[...].astype(BF16)
        wglu = _glu_weight(wglu_ref)
        dmix = _dot_nt(dhb, wout_ref[...].reshape(D_MODEL, D_MODEL))
        dyp_ref[...] = dmix[:, :D_POOL]
        d_o = dmix[:, D_POOL:]
        yraw_v = yr_ref[...]
        y = _gelu(yraw_v)
        yb = y.astype(BF16)
        sig = _sigmoid(_dot(yb, wglu) + b_ref[...])
        mix = jnp.concatenate([yp_ref[...], y * sig], axis=1).astype(BF16)
        dwout[...] += _dot_tn(mix, dhb).reshape(N_SHARD, 256, D_MODEL)
        dz = d_o * y * sig * (1.0 - sig)
        dzb = dz.astype(BF16)
        db_ref[...] += jnp.sum(dz, axis=0, keepdims=True)
        dwglu[...] += _dot_tn(yb, dzb)
        dy = d_o * sig + _dot_nt(dzb, wglu)
        dyr_ref[...] = dy * _gelu_grad(yraw_v)

        @pl.when(i == n_steps - 1)
        def _():
            gpack[:, :gb, :] = _glu_pack(dwglu[...])
            gpack[:, gb:, :] = jnp.zeros((N_SHARD, P_GLU_PAD - gb, D_MODEL), F32)
            pltpu.sync_copy(gpack, g1_ref.at[0, :, pl.ds(gb * gi, P_GLU_PAD), :])
            pltpu.sync_copy(dwout, g1_ref.at[0, :, pl.ds(ob * oi, ob), :])

    gb, gi = P_GLU_BLK
    ob, oi = P_OUT_BLK
    n_steps = L // tm
    return pl.pallas_call(
        body, name="mix_out_bwd", grid=(n_steps,),
        in_specs=[pl.BlockSpec((tm, D_MODEL), lambda i: (i, 0)),
                  pl.BlockSpec((tm, D_SSM), lambda i: (i, 0)),
                  pl.BlockSpec((tm, D_POOL), lambda i: (i, 0)),
                  pl.BlockSpec((N_SHARD, None, gb, D_MODEL), lambda i: (0, 0, gi, 0)),
                  pl.BlockSpec((None, 1, D_SSM), lambda i: (layer, 0, 0)),
                  pl.BlockSpec((N_SHARD, None, ob, D_MODEL), lambda i: (0, 0, oi, 0)),
                  pl.BlockSpec(memory_space=pl.ANY)],
        out_specs=[pl.BlockSpec((tm, D_SSM), lambda i: (i, 0)),
                   pl.BlockSpec((tm, D_POOL), lambda i: (i, 0)),
                   pl.BlockSpec((1, D_SSM), lambda i: (0, 0)),
                   pl.BlockSpec(memory_space=pl.ANY)],
        out_shape=[jax.ShapeDtypeStruct((L, D_SSM), F32), jax.ShapeDtypeStruct((L, D_POOL), F32),
                   jax.ShapeDtypeStruct((1, D_SSM), F32),
                   jax.ShapeDtypeStruct(gbuf.shape, F32)],
        scratch_shapes=[pltpu.VMEM((N_SHARD, ob, D_MODEL), F32), pltpu.VMEM((D_SSM, D_SSM), F32),
                        pltpu.VMEM((N_SHARD, P_GLU_PAD, D_MODEL), F32)],
        input_output_aliases={6: 3},
        compiler_params=_cparams(1),
    )(dhm, yraw, ypool, wp, b_glu, wp, gbuf)


def _ssm_bwd(dyraw, u, sre, sim, layer, cpad_t, bpad_t, ar, ai, dskip):
    L = u.shape[0]
    ts = min(TS, L)
    nt = L // ts
    nq = 4
    cq = N_STATE // nq

    def body(dy_ref, u_ref, sre_ref, sim_ref, ct_ref, bt_ref, ar_ref, ai_ref, dsk_ref,
             du_ref, dcp_ref, dbp_ref, dar_ref, dai_ref, ddsk_ref, gre, gim, cr, ci, tab, accr, acci):
        t = pl.program_id(1)

        @pl.when(t == 0)
        def _():
            for ref in (cr, ci, accr, acci, dcp_ref, dbp_ref, ddsk_ref):
                ref[...] = jnp.zeros_like(ref)
            _scan_tables(ar_ref[...], -ai_ref[...], tab, reverse=True)

        dy = dy_ref[...]
        dyb = dy.astype(BF16)
        uf = u_ref[...]
        ub = uf.astype(BF16)
        for jj in range(4):
            cols = slice(jj * 128, (jj + 1) * 128)
            ds = _dot(dyb, ct_ref[jj])
            gre[:, cols] = ds[:, :128]
            gim[:, cols] = ds[:, 128:]
            scat = jnp.concatenate([sre_ref[:, cols], sim_ref[:, cols]], axis=1).astype(BF16)
            dcp_ref[jj] += _dot_tn(scat, dyb)

        n_grp = ts // SUBLANES
        shp = (SUBLANES, SCAN_LANES)
        last_row = lax.broadcasted_iota(jnp.int32, shp, 0) == SUBLANES - 1
        for cc in range(cq // SCAN_LANES):
            cols = slice(cc * SCAN_LANES, (cc + 1) * SCAN_LANES)
            def step(i, carry, cols=cols):
                c_r, c_i, a_r, a_i = carry
                r0 = pl.multiple_of((n_grp - 1 - i) * SUBLANES, SUBLANES)
                xr = gre[pl.ds(r0, SUBLANES), cols]
                xi = gim[pl.ds(r0, SUBLANES), cols]
                for n, s in enumerate((1, 2, 4)):
                    tr, ti = tab[2 * n, :, cols], tab[2 * n + 1, :, cols]
                    rr = pltpu.roll(xr, SUBLANES - s, 0)
                    ri = pltpu.roll(xi, SUBLANES - s, 0)
                    xr, xi = xr + tr * rr - ti * ri, xi + tr * ri + ti * rr
                qr, qi = tab[6, :, cols], tab[7, :, cols]
                xr, xi = xr + qr * c_r - qi * c_i, xi + qr * c_i + qi * c_r
                gre[pl.ds(r0, SUBLANES), cols] = xr
                gim[pl.ds(r0, SUBLANES), cols] = xi
                nr = jnp.where(last_row, c_r, pltpu.roll(xr, SUBLANES - 1, 0))
                ni = jnp.where(last_row, c_i, pltpu.roll(xi, SUBLANES - 1, 0))
                sr = sre_ref[pl.ds(r0, SUBLANES), cols]
                si = sim_ref[pl.ds(r0, SUBLANES), cols]
                a_r = a_r + sr * nr + si * ni
                a_i = a_i + sr * ni - si * nr
                return (jnp.broadcast_to(xr[:1, :], shp), jnp.broadcast_to(xi[:1, :], shp), a_r, a_i)

            c_r, c_i, a_r, a_i = lax.fori_loop(
                0, n_grp, step, (cr[:, cols], ci[:, cols], accr[:, cols], acci[:, cols]), unroll=2)
            cr[:, cols] = c_r
            ci[:, cols] = c_i
            accr[:, cols] = a_r
            acci[:, cols] = a_i

        acc = dsk_ref[...] * dy
        for jj in range(4):
            cols = slice(jj * 128, (jj + 1) * 128)
            gcat = jnp.concatenate([gre[:, cols], gim[:, cols]], axis=1).astype(BF16)
            acc = acc + _dot(gcat, bt_ref[jj])
            dbp_ref[jj] += _dot_tn(ub, gcat)
        du_ref[...] = acc
        ddsk_ref[...] += jnp.sum(dy * uf, axis=0, keepdims=True)

        @pl.when(t == nt - 1)
        def _():
            dar_ref[...] = jnp.sum(accr[...], axis=0, keepdims=True)
            dai_ref[...] = jnp.sum(acci[...], axis=0, keepdims=True)

    f32_scr = lambda *s: pltpu.VMEM(s, F32)
    return pl.pallas_call(
        body, name="ssm_bwd", grid=(nq, nt),
        in_specs=[pl.BlockSpec((ts, 128), lambda q, t: (nt - 1 - t, q)),
                  pl.BlockSpec((ts, 128), lambda q, t: (nt - 1 - t, 4 + q)),
                  pl.BlockSpec((ts, cq), lambda q, t: (nt - 1 - t, q)),
                  pl.BlockSpec((ts, cq), lambda q, t: (nt - 1 - t, q)),
                  pl.BlockSpec((None, 4, 128, 256), lambda q, t: (layer, q, 0, 0)),
                  pl.BlockSpec((None, 4, 256, 128), lambda q, t: (layer, q, 0, 0)),
                  pl.BlockSpec((None, 1, cq), lambda q, t: (layer, 0, q)),
                  pl.BlockSpec((None, 1, cq), lambda q, t: (layer, 0, q)),
                  pl.BlockSpec((None, 1, 128), lambda q, t: (layer, 0, q))],
        out_specs=[pl.BlockSpec((ts, 128), lambda q, t: (nt - 1 - t, q)),
                   pl.BlockSpec((4, 256, 128), lambda q, t: (q, 0, 0)),
                   pl.BlockSpec((4, 128, 256), lambda q, t: (q, 0, 0)),
                   pl.BlockSpec((1, cq), lambda q, t: (0, q)),
                   pl.BlockSpec((1, cq), lambda q, t: (0, q)),
                   pl.BlockSpec((1, 128), lambda q, t: (0, q))],
        out_shape=[jax.ShapeDtypeStruct((L, D_SSM), F32),
                   jax.ShapeDtypeStruct((N_PAIRS, 256, 128), F32), jax.ShapeDtypeStruct((N_PAIRS, 128, 256), F32),
                   jax.ShapeDtypeStruct((1, N_STATE), F32), jax.ShapeDtypeStruct((1, N_STATE), F32),
                   jax.ShapeDtypeStruct((1, D_SSM), F32)],
        scratch_shapes=[f32_scr(ts, cq), f32_scr(ts, cq), f32_scr(SUBLANES, cq), f32_scr(SUBLANES, cq),
                        f32_scr(8, SUBLANES, cq), f32_scr(SUBLANES, cq), f32_scr(SUBLANES, cq)],
        compiler_params=_cparams(2),
    )(dyraw, u, sre, sim, cpad_t, bpad_t, ar, ai, dskip)


def _pool_bwd(dyp, u, layer, w_pool, scale):
    L = u.shape[0]
    tm = min(TM, L)
    nt = L // tm
    halo_per_tile = tm // POOL_HALO

    def body(dyp_ref, u_ref, halo_ref, wp_ref, sc_ref, du_ref, dwp_ref, dsc_ref, carry):
        i = pl.program_id(0)
        tile = nt - 1 - i

        @pl.when(i == 0)
        def _():
            carry[...] = jnp.zeros_like(carry)
            dwp_ref[...] = jnp.zeros_like(dwp_ref)
            dsc_ref[...] = jnp.zeros_like(dsc_ref)

        up = u_ref[...]
        halo = jnp.where(tile > 0, halo_ref[...], jnp.zeros_like(halo_ref))
        diffs = _pool_diff(jnp.concatenate([halo, up], axis=0), tile * tm, tm)
        rows = tile * tm + lax.broadcasted_iota(jnp.int32, (tm, 1), 0)
        n_ext = tm + POOL_HALO
        for gi, w in enumerate(POOL_WINDOWS):
            cols = slice(gi * POOL_GROUP, (gi + 1) * POOL_GROUP)
            db = diffs[gi].astype(BF16)
            dyp = dyp_ref[:, cols]
            dsc_ref[:, cols] += jnp.sum(dyp * _dot(db, wp_ref[gi]), axis=0, keepdims=True)
            dp = (dyp * sc_ref[:, cols]).astype(BF16)
            ddiff = _dot_nt(dp, wp_ref[gi])
            dwp_ref[gi] += _dot_tn(db, dp)
            e = ddiff * (1.0 / jnp.minimum(rows + 1, w).astype(F32))
            s = jnp.concatenate([e, carry[:, cols]], axis=0)
            k = 1
            while k < w:
                s = s + pltpu.roll(s, n_ext - k, 0)
                k *= 2
            du_ref[:, cols] = s[:tm, :] - ddiff
            carry[:, cols] = e[:POOL_HALO, :]

    return pl.pallas_call(
        body, name="pool_bwd", grid=(nt,),
        in_specs=[pl.BlockSpec((tm, D_POOL), lambda i: (nt - 1 - i, 0)),
                  pl.BlockSpec((tm, D_POOL), lambda i: (nt - 1 - i, 0)),
                  pl.BlockSpec((POOL_HALO, D_POOL), lambda i: (jnp.maximum((nt - 1 - i) * halo_per_tile - 1, 0), 0)),
                  pl.BlockSpec((None, 4, POOL_GROUP, POOL_GROUP), lambda i: (layer, 0, 0, 0)),
                  pl.BlockSpec((None, 1, D_POOL), lambda i: (layer, 0, 0))],
        out_specs=[pl.BlockSpec((tm, D_POOL), lambda i: (nt - 1 - i, 0)),
                   pl.BlockSpec((4, POOL_GROUP, POOL_GROUP), lambda i: (0, 0, 0)),
                   pl.BlockSpec((1, D_POOL), lambda i: (0, 0))],
        out_shape=[jax.ShapeDtypeStruct((L, D_POOL), F32),
                   jax.ShapeDtypeStruct((4, POOL_GROUP, POOL_GROUP), F32),
                   jax.ShapeDtypeStruct((1, D_POOL), F32)],
        scratch_shapes=[pltpu.VMEM((POOL_HALO, D_POOL), F32)],
        compiler_params=_cparams(1),
    )(dyp, u, u, w_pool, scale)


def _mix_in_bwd(dup, dus, h, dhm, g1, wp, layer, gbuf):
    L = h.shape[0]
    tm = min(TM, L)
    n_steps = L // tm
    blk, idx = P_IN_BLK

    def body(dup_ref, dus_ref, h_ref, dhm_ref, g_ref, w_ref, g1_in, dh_ref, dg_ref, g1_ref, dwin):
        i = pl.program_id(0)

        @pl.when(i == 0)
        def _():
            dg_ref[...] = jnp.zeros_like(dg_ref)
            dwin[...] = jnp.zeros_like(dwin)

        du = jnp.concatenate([dup_ref[...], dus_ref[...]], axis=1).astype(BF16)
        dn1 = _dot_nt(du, w_ref[...].reshape(D_MODEL, D_MODEL))
        xhat, r = _rms_hat(h_ref[...])
        g = g_ref[...]
        n1 = (xhat * g).astype(BF16)
        dwin[...] += _dot_tn(n1, du).reshape(N_SHARD, blk, D_MODEL)
        dg_ref[...] += jnp.sum(dn1 * xhat, axis=0, keepdims=True)
        dh_ref[...] = dhm_ref[...] + _rms_bwd(dn1 * g, xhat, r)

        @pl.when(i == n_steps - 1)
        def _():
            pltpu.sync_copy(dwin, g1_ref.at[0, :, pl.ds(blk * idx, blk), :])

    row_spec = pl.BlockSpec((tm, D_MODEL), lambda i: (i, 0))
    half_spec = pl.BlockSpec((tm, D_POOL), lambda i: (i, 0))
    return pl.pallas_call(
        body, name="mix_in_bwd", grid=(n_steps,),
        in_specs=[half_spec, half_spec, row_spec, row_spec,
                  pl.BlockSpec((None, 1, D_MODEL), lambda i: (layer, 0, 0)),
                  pl.BlockSpec((N_SHARD, None, blk, D_MODEL), lambda i: (0, 0, idx, 0)),
                  pl.BlockSpec(memory_space=pl.ANY)],
        out_specs=[row_spec, pl.BlockSpec((1, D_MODEL), lambda i: (0, 0)), pl.BlockSpec(memory_space=pl.ANY)],
        out_shape=[jax.ShapeDtypeStruct((L, D_MODEL), F32), jax.ShapeDtypeStruct((1, D_MODEL), F32),
                   jax.ShapeDtypeStruct(gbuf.shape, F32)],
        scratch_shapes=[pltpu.VMEM((N_SHARD, blk, D_MODEL), F32)],
        input_output_aliases={6: 2},
        compiler_params=_cparams(1),
    )(dup, dus, h, dhm, g1, wp, gbuf)


def _disc_math(lr, li, ldt, br_t, bi_t):
    dt = jnp.exp(ldt)
    mag = jnp.exp(lr * dt)
    ang = li * dt
    ar = mag * jnp.cos(ang)
    ai = mag * jnp.sin(ang)
    den = lr * lr + li * li
    nr, ni = ar - 1.0, ai
    cr = (nr * lr + ni * li) / den
    ci = (ni * lr - nr * li) / den
    return ar, ai, cr * br_t - ci * bi_t, cr * bi_t + ci * br_t


def _disc_fwd(lr, li, ldt, br_t, bi_t):
    def body(lr_ref, li_ref, ldt_ref, br_ref, bi_ref, ar_ref, ai_ref, bbr_ref, bbi_ref):
        ar, ai, bbr, bbi = _disc_math(lr_ref[...], li_ref[...], ldt_ref[...], br_ref[...], bi_ref[...])
        ar_ref[...] = ar
        ai_ref[...] = ai
        bbr_ref[...] = bbr
        bbi_ref[...] = bbi

    shapes = [jax.ShapeDtypeStruct(a.shape, F32) for a in (lr, li, br_t, bi_t)]
    return pl.pallas_call(body, name="ssm_disc_fwd", out_shape=shapes,
                          compiler_params=pltpu.CompilerParams(vmem_limit_bytes=VMEM_LIMIT))(lr, li, ldt, br_t, bi_t)


def _disc_bwd(lr, li, ldt, br_t, bi_t, dar, dai, dbbr, dbbi):
    def body(lr_ref, li_ref, ldt_ref, br_ref, bi_ref, dar_ref, dai_ref, dbbr_ref, dbbi_ref,
             dlr_ref, dli_ref, dldt_ref, dbr_ref, dbi_ref):
        prim = (lr_ref[...], li_ref[...], ldt_ref[...], br_ref[...], bi_ref[...])
        _, pullback = jax.vjp(_disc_math, *prim)
        dlr, dli, dldt, dbr, dbi = pullback((dar_ref[...], dai_ref[...], dbbr_ref[...], dbbi_ref[...]))
        dlr_ref[...] = dlr
        dli_ref[...] = dli
        dldt_ref[...] = dldt
        dbr_ref[...] = dbr
        dbi_ref[...] = dbi

    shapes = [jax.ShapeDtypeStruct(a.shape, F32) for a in (lr, li, ldt, br_t, bi_t)]
    return pl.pallas_call(body, name="ssm_disc_bwd", out_shape=shapes,
                          compiler_params=pltpu.CompilerParams(vmem_limit_bytes=VMEM_LIMIT))(
        lr, li, ldt, br_t, bi_t, dar, dai, dbbr, dbbi)


def _pad_pairs(m_re, m_im):
    def blocks(m):
        v = m.transpose(0, 2, 1).reshape(N_PAIRS, 2, SSM_GROUP, SSM_STATE)
        return jnp.einsum("ab,jahp->jahbp", jnp.eye(2, dtype=m.dtype), v).reshape(N_PAIRS, 32, 128)
    both = jnp.concatenate([blocks(m_re), blocks(m_im)], axis=-1)
    place = jax.nn.one_hot(jnp.arange(N_PAIRS) % 4, 4, dtype=both.dtype)
    return jnp.einsum("jk,jrc->jkrc", place, both).reshape(N_PAIRS, 128, 256)


def _unpad_pairs(x):
    place = jax.nn.one_hot(jnp.arange(N_PAIRS) % 4, 4, dtype=x.dtype)
    both = jnp.einsum("jk,jkrc->jrc", place, x.reshape(N_PAIRS, 4, 32, 256))

    def unblock(v):
        v = v.reshape(N_PAIRS, 2, SSM_GROUP, 2, SSM_STATE)
        d = jnp.einsum("ab,jahbp->jahp", jnp.eye(2, dtype=x.dtype), v)
        return d.reshape(N_SSM_GROUPS, SSM_GROUP, SSM_STATE).transpose(0, 2, 1)
    return unblock(both[..., :128]), unblock(both[..., 128:])


def _adamw_math(w, g, m, v):
    m = ADAM_B1 * m + (1.0 - ADAM_B1) * g
    v = ADAM_B2 * v + (1.0 - ADAM_B2) * (g * g)
    m_hat = m / (1.0 - ADAM_B1 ** ADAM_STEP)
    v_hat = v / (1.0 - ADAM_B2 ** ADAM_STEP)
    delta = -ADAM_LR * (m_hat / (jnp.sqrt(v_hat) + ADAM_EPS) + ADAM_WD * w)
    return delta, m, v


def _adamw(name, layer, w, m, v, gbuf, g_block, g_row0, row_tile, outs=None, after=(), glu=False):
    nl, r, c = w.shape
    n_tiles = r // row_tile
    g_rows, g_cols = g_block
    g_tile = g_rows // n_tiles
    g_off = g_row0 // g_tile
    if outs is None:
        outs = [lax.empty(w.shape, F32) for _ in range(4)]

    def body(w_ref, m_ref, v_ref, g_ref, *rest):
        go_ref, d_ref, mo_ref, vo_ref = rest[-4:]
        g = g_ref[...]
        if glu:
            g = jnp.concatenate([g[:, :D_SSM], g[:, D_SSM:]], axis=0)
        delta, mn, vn = _adamw_math(w_ref[...], g, m_ref[...], v_ref[...])
        go_ref[...] = g
        d_ref[...] = delta
        mo_ref[...] = mn
        vo_ref[...] = vn

    w_spec = pl.BlockSpec((None, row_tile, c), lambda j: (layer, j, 0))
    shape = jax.ShapeDtypeStruct(w.shape, F32)
    return pl.pallas_call(
        body, name=name, grid=(n_tiles,),
        in_specs=[w_spec, w_spec, w_spec, pl.BlockSpec((None, g_tile, g_cols), lambda j: (0, g_off + j, 0))]
        + [_ANY] * (4 + len(after)),
        out_specs=[w_spec] * 4,
        out_shape=[shape] * 4,
        input_output_aliases={4: 0, 5: 1, 6: 2, 7: 3},
        compiler_params=_cparams(1),
    )(w, m, v, gbuf, *outs, *after)


def _pack_weights(ids, layer, w_in, w_glu, w_out, w_down, w_gate_t, w_up_t):
    gb, gi = P_GLU_BLK
    ib, ii = P_IN_BLK
    ob, oi = P_OUT_BLK

    def body(ids_ref, in_ref, glu_ref, out_ref, dn_ref, gate_ref, up_ref, p_ref):
        p_ref[0:FF_SHARD, :] = dn_ref[...].astype(BF16)
        p_ref[FF_SHARD:2 * FF_SHARD, :] = gate_ref[...].astype(BF16)
        p_ref[2 * FF_SHARD:P_FF_ROWS, :] = up_ref[...].astype(BF16)
        g = glu_ref[...]
        p_ref[gb * gi:gb * (gi + 1), :] = jnp.concatenate([g[:gb, :], g[gb:, :]], axis=1).astype(BF16)
        p_ref[gb * (gi + 1):ib * ii, :] = jnp.zeros((P_GLU_PAD - gb, D_MODEL), BF16)
        p_ref[ib * ii:ib * (ii + 1), :] = in_ref[...].astype(BF16)
        p_ref[ob * oi:ob * (oi + 1), :] = out_ref[...].astype(BF16)

    def spec(a):
        return pl.BlockSpec((None,) + a.shape[1:], lambda i, ids_ref: (layer, 0, 0))

    ins = (w_in, w_glu, w_out, w_down, w_gate_t, w_up_t)
    grid_spec = pltpu.PrefetchScalarGridSpec(
        num_scalar_prefetch=1, grid=(1,),
        in_specs=[spec(a) for a in ins],
        out_specs=pl.BlockSpec((None, None, P_ROWS, D_MODEL), lambda i, ids_ref: (ids_ref[1], 0, 0, 0)))
    return pl.pallas_call(
        body, name="pack_weights", grid_spec=grid_spec,
        out_shape=jax.ShapeDtypeStruct((N_SHARD, 1, P_ROWS, D_MODEL), BF16),
        compiler_params=_cparams(1),
    )(ids, *ins)


MESH = pl.DeviceIdType.MESH
_ANY = pl.BlockSpec(memory_space=pl.ANY)
P_HALF = P_ROWS // 2
RS_ROW_TILE = 352


def _mesh_pos():
    return lax.axis_index("x"), lax.axis_index("y"), lax.axis_index("c")


def _other_chips(x, y):
    return [(1 - x, y), (x, 1 - y), (1 - x, 1 - y)]


def _remote(src, dst, send_sems, recv_sems, n, to):
    return pltpu.make_async_remote_copy(src_ref=src, dst_ref=dst, send_sem=send_sems.at[n],
                                        recv_sem=recv_sems.at[n], device_id=to, device_id_type=MESH)


_HBM = pl.BlockSpec(memory_space=pltpu.HBM)
_SEM = pl.BlockSpec(memory_space=pltpu.SEMAPHORE)
_EFFECT = pltpu.CompilerParams(has_side_effects=pltpu.SideEffectType.DATAFLOW_SIDE_EFFECTING)
_TOKEN = jax.ShapeDtypeStruct((8, 128), F32)


def _in_hbm(a):
    return pltpu.with_memory_space_constraint(a, pltpu.HBM)


def _ag_start(name, wp, after):
    def body(w_ref, after_ref, send_sems, recv_sems, w_thru, token):
        x, y, c = _mesh_pos()
        mine = w_ref.at[2 * x + y, :, pl.ds(c * P_HALF, P_HALF), :]
        for j, (px, py) in enumerate(_other_chips(x, y)):
            _remote(mine, mine, send_sems, recv_sems, j, (px, py, c)).start()
        token[...] = jnp.zeros_like(token)

    return pl.pallas_call(
        body, name=name,
        out_shape=(pltpu.SemaphoreType.DMA((3,)), pltpu.SemaphoreType.DMA((3,)), pltpu.HBM(wp.shape, wp.dtype), _TOKEN),
        in_specs=(_HBM, _ANY), out_specs=(_SEM, _SEM, _HBM, pl.BlockSpec(memory_space=pltpu.VMEM)),
        input_output_aliases={0: 2}, compiler_params=_EFFECT,
    )(_in_hbm(wp), after)


def _ag_wait(name, send_sems, recv_sems, wp, after):
    def body(w_ref, send_sems, recv_sems, *rest):
        x, y, c = _mesh_pos()
        mine = w_ref.at[2 * x + y, :, pl.ds(c * P_HALF, P_HALF), :]
        for j, (px, py) in enumerate(_other_chips(x, y)):
            landed = w_ref.at[2 * px + py, :, pl.ds(c * P_HALF, P_HALF), :]
            cp = _remote(mine, landed, send_sems, recv_sems, j, (px, py, c))
            cp.wait_send()
            cp.wait_recv()

    return pl.pallas_call(
        body, name=name, out_shape=pltpu.HBM(wp.shape, wp.dtype),
        in_specs=(_HBM, _SEM, _SEM) + (_ANY,) * len(after), out_specs=_HBM,
        input_output_aliases={0: 0}, compiler_params=_EFFECT,
    )(wp, send_sems, recv_sems, *after)


def _ag_forward(wp):
    def body(w_in, o, send_sems, recv_sems):
        x, y, c = _mesh_pos()
        sib = (x, y, 1 - c)
        chips = _other_chips(x, y)
        sends = []
        for j, (px, py) in enumerate(chips):
            landed = o.at[2 * px + py, :, pl.ds(c * P_HALF, P_HALF), :]
            cp = _remote(landed, landed, send_sems, recv_sems, j, sib)
            cp.start()
            sends.append(cp)
        for j, (px, py) in enumerate(chips):
            passed = o.at[2 * px + py, :, pl.ds((1 - c) * P_HALF, P_HALF), :]
            _remote(passed, passed, send_sems, recv_sems, j, sib).wait_recv()
        for cp in sends:
            cp.wait_send()

    return pl.pallas_call(
        body, name="ag_forward",
        in_specs=[_ANY], out_specs=_ANY,
        out_shape=jax.ShapeDtypeStruct(wp.shape, wp.dtype),
        scratch_shapes=[pltpu.SemaphoreType.DMA((3,)), pltpu.SemaphoreType.DMA((3,))],
        input_output_aliases={0: 0},
    )(wp)


def _rs_chips_start(name, t):
    nl = t.shape[0]

    def body(t_ref, land_ref, send_sems, recv_sems, t_thru, land_thru, token):
        x, y, c = _mesh_pos()
        for j, (px, py) in enumerate(_other_chips(x, y)):
            _remote(t_ref.at[:, 2 * px + py], land_ref.at[j], send_sems, recv_sems, j, (px, py, c)).start()
        token[...] = jnp.zeros_like(token)

    land = lax.empty((3, nl, P_HALF, D_MODEL), BF16)
    return pl.pallas_call(
        body, name=name,
        out_shape=(pltpu.SemaphoreType.DMA((3,)), pltpu.SemaphoreType.DMA((3,)), pltpu.HBM(t.shape, t.dtype),
                   pltpu.HBM(land.shape, land.dtype), _TOKEN),
        in_specs=(_HBM, _HBM), out_specs=(_SEM, _SEM, _HBM, _HBM, pl.BlockSpec(memory_space=pltpu.VMEM)),
        input_output_aliases={0: 2, 1: 3}, compiler_params=_EFFECT,
    )(_in_hbm(t), _in_hbm(land))


def _rs_chips_wait(name, send_sems, recv_sems, t, land, after):
    def body(t_ref, land_ref, send_sems, recv_sems, *rest):
        x, y, c = _mesh_pos()
        for j, (px, py) in enumerate(_other_chips(x, y)):
            cp = _remote(t_ref.at[:, 2 * px + py], land_ref.at[j], send_sems, recv_sems, j, (px, py, c))
            cp.wait_send()
            cp.wait_recv()

    return pl.pallas_call(
        body, name=name, out_shape=(pltpu.HBM(t.shape, t.dtype), pltpu.HBM(land.shape, land.dtype)),
        in_specs=(_HBM, _HBM, _SEM, _SEM) + (_ANY,) * len(after), out_specs=(_HBM, _HBM),
        input_output_aliases={0: 0, 1: 1}, compiler_params=_EFFECT,
    )(t, land, send_sems, recv_sems, *after)[1]


def _rs_sibling_start(name, g):
    nl = g.shape[0]

    def body(g_ref, land_ref, send_sems, recv_sems, g_thru, land_thru, token):
        x, y, c = _mesh_pos()
        _remote(g_ref.at[:, :, pl.ds((1 - c) * P_HALF, P_HALF), :], land_ref, send_sems, recv_sems, 0,
                (x, y, 1 - c)).start()
        token[...] = jnp.zeros_like(token)

    land = lax.empty((nl, N_SHARD, P_HALF, D_MODEL), F32)
    return pl.pallas_call(
        body, name=name,
        out_shape=(pltpu.SemaphoreType.DMA((1,)), pltpu.SemaphoreType.DMA((1,)), pltpu.HBM(g.shape, g.dtype),
                   pltpu.HBM(land.shape, land.dtype), _TOKEN),
        in_specs=(_HBM, _HBM), out_specs=(_SEM, _SEM, _HBM, _HBM, pl.BlockSpec(memory_space=pltpu.VMEM)),
        input_output_aliases={0: 2, 1: 3}, compiler_params=_EFFECT,
    )(_in_hbm(g), _in_hbm(land))


def _rs_sibling_wait(name, send_sems, recv_sems, g, land, after):
    def body(g_ref, land_ref, send_sems, recv_sems, *rest):
        x, y, c = _mesh_pos()
        cp = _remote(g_ref.at[:, :, pl.ds((1 - c) * P_HALF, P_HALF), :], land_ref, send_sems, recv_sems, 0,
                     (x, y, 1 - c))
        cp.wait_send()
        cp.wait_recv()

    return pl.pallas_call(
        body, name=name, out_shape=(pltpu.HBM(g.shape, g.dtype), pltpu.HBM(land.shape, land.dtype)),
        in_specs=(_HBM, _HBM, _SEM, _SEM) + (_ANY,) * len(after), out_specs=(_HBM, _HBM),
        input_output_aliases={0: 0, 1: 1}, compiler_params=_EFFECT,
    )(g, land, send_sems, recv_sems, *after)


def _rs_add(name, ids, g, buf, row_tile):
    nl, _, hr, cols = buf.shape
    n_rt = hr // row_tile

    def body(ids_ref, g_ref, b_ref, own_ref, tb_ref):
        t = g_ref[...] + b_ref[...]
        tb_ref[...] = t.astype(BF16)

        @pl.when(pl.program_id(2) == ids_ref[1])
        def _():
            own_ref[...] = t

    blk = (None, None, row_tile, cols)
    grid_spec = pltpu.PrefetchScalarGridSpec(
        num_scalar_prefetch=1, grid=(nl, n_rt, N_SHARD),
        in_specs=[pl.BlockSpec(blk, lambda l, j, s, ids_ref: (l, s, ids_ref[0] * n_rt + j, 0)),
                  pl.BlockSpec(blk, lambda l, j, s, ids_ref: (l, s, j, 0))],
        out_specs=[pl.BlockSpec((None, row_tile, cols), lambda l, j, s, ids_ref: (l, j, 0)),
                   pl.BlockSpec(blk, lambda l, j, s, ids_ref: (l, s, j, 0))])
    return pl.pallas_call(
        body, name=name, grid_spec=grid_spec,
        out_shape=[jax.ShapeDtypeStruct((nl, hr, cols), F32), jax.ShapeDtypeStruct(buf.shape, BF16)],
        compiler_params=_cparams(3),
    )(ids, g, buf)


def _rs_sum(ids, layer, own, bufb, reduced, row_tile):
    _, hr, cols = own.shape
    n_rt = hr // row_tile

    def body(ids_ref, own_ref, b_ref, reduced_in, f_ref):
        f_ref[...] = ((own_ref[...] + b_ref[0].astype(F32)) + b_ref[1].astype(F32)) + b_ref[2].astype(F32)

    grid_spec = pltpu.PrefetchScalarGridSpec(
        num_scalar_prefetch=1, grid=(n_rt,),
        in_specs=[pl.BlockSpec((None, row_tile, cols), lambda j, ids_ref: (0, j, 0)),
                  pl.BlockSpec((3, None, row_tile, cols), lambda j, ids_ref: (0, 0, j, 0)),
                  pl.BlockSpec(memory_space=pl.ANY)],
        out_specs=pl.BlockSpec((None, row_tile, cols), lambda j, ids_ref: (layer, ids_ref[0] * n_rt + j, 0)))
    return pl.pallas_call(
        body, name="rs_sum", grid_spec=grid_spec,
        out_shape=jax.ShapeDtypeStruct(reduced.shape, F32),
        input_output_aliases={3: 0},
        compiler_params=_cparams(1),
    )(ids, own, bufb, reduced)


def _rs_exchange(f, layer):
    def body(f_in, o, send_sems, recv_sems):
        x, y, c = _mesh_pos()
        mine = o.at[layer, pl.ds(c * P_HALF, P_HALF), :]
        cp = _remote(mine, mine, send_sems, recv_sems, 0, (x, y, 1 - c))
        cp.start()
        cp.wait_send()
        theirs = o.at[layer, pl.ds((1 - c) * P_HALF, P_HALF), :]
        _remote(theirs, theirs, send_sems, recv_sems, 0, (x, y, 1 - c)).wait_recv()

    return pl.pallas_call(
        body, name="rs_exchange",
        in_specs=[_ANY], out_specs=_ANY,
        out_shape=jax.ShapeDtypeStruct(f.shape, F32),
        scratch_shapes=[pltpu.SemaphoreType.DMA((1,)), pltpu.SemaphoreType.DMA((1,))],
        input_output_aliases={0: 0},
    )(f)


def _small_all_reduce(s):
    n_rows = s.shape[0]
    hr = n_rows // 2

    def body(s_ref, o_ref, sibbuf, tbuf, cbuf, fbuf, send_sems, recv_sems):
        x, y, c = _mesh_pos()
        sib = (x, y, 1 - c)
        mine = pl.ds(pl.multiple_of(c * hr, SUBLANES), hr)
        theirs = pl.ds(pl.multiple_of((1 - c) * hr, SUBLANES), hr)
        first = _remote(s_ref.at[theirs], sibbuf, send_sems, recv_sems, 0, sib)
        first.start()
        first.wait()
        tbuf[...] = s_ref[mine, :] + sibbuf[...]
        cps = []
        for j, (px, py) in enumerate(_other_chips(x, y)):
            cp = _remote(tbuf, cbuf.at[j], send_sems, recv_sems, 1 + j, (px, py, c))
            cp.start()
            cps.append(cp)
        for cp in cps:
            cp.wait()
        f = (tbuf[...] + cbuf[1]) + (cbuf[0] + cbuf[2])
        fbuf[...] = f
        o_ref[mine, :] = f
        last = _remote(fbuf, o_ref.at[mine], send_sems, recv_sems, 4, sib)
        last.start()
        last.wait()

    vmem = pl.BlockSpec(memory_space=pltpu.VMEM)
    return pl.pallas_call(
        body, name="small_all_reduce",
        in_specs=[vmem], out_specs=vmem,
        out_shape=jax.ShapeDtypeStruct(s.shape, F32),
        scratch_shapes=[pltpu.VMEM((hr, D_MODEL), F32), pltpu.VMEM((hr, D_MODEL), F32),
                        pltpu.VMEM((3, hr, D_MODEL), F32), pltpu.VMEM((hr, D_MODEL), F32),
                        pltpu.SemaphoreType.DMA((5,)), pltpu.SemaphoreType.DMA((5,))],
        compiler_params=pltpu.CompilerParams(vmem_limit_bytes=VMEM_LIMIT),
    )(s)


_SMALL = ("norm_mix", "w_pool", "pool_scale", "lam_re", "lam_im", "log_dt", "b_re", "b_im", "c_re", "c_im",
          "d_skip", "b_glu", "norm_ffn", "norm_final")
_WEIGHTS = ("norm_mix", "w_in", "w_pool", "pool_scale", "lam_re", "lam_im", "log_dt", "b_re", "b_im", "c_re",
            "c_im", "d_skip", "w_glu", "b_glu", "w_out", "norm_ffn", "w_gate", "w_up", "w_down", "norm_final")


def _local_step(x, target, p, get_weights, ffn_bwd_done, put_grads):
    nl = p["norm_mix"].shape[0]

    def tied(a, token):
        return a if token is None else a + token
    n_rows = nl * N_SSM_GROUPS
    lr = p["lam_re"].reshape(n_rows, 1, SSM_STATE)
    li = p["lam_im"].reshape(n_rows, 1, SSM_STATE)
    ldt = p["log_dt"].reshape(n_rows, 1, 1)
    br_t = p["b_re"].reshape(n_rows, SSM_STATE, SSM_GROUP).transpose(0, 2, 1)
    bi_t = p["b_im"].reshape(n_rows, SSM_STATE, SSM_GROUP).transpose(0, 2, 1)
    ar, ai, bbr_t, bbi_t = _disc_fwd(lr, li, ldt, br_t, bi_t)
    ar = ar.reshape(nl, 1, N_STATE)
    ai = ai.reshape(nl, 1, N_STATE)
    bbr = bbr_t.transpose(0, 2, 1).reshape(nl, N_SSM_GROUPS, SSM_STATE, SSM_GROUP)
    bbi = bbi_t.transpose(0, 2, 1).reshape(nl, N_SSM_GROUPS, SSM_STATE, SSM_GROUP)
    w_pool = p["w_pool"].astype(BF16)
    p = dict(p)
    for n in ("norm_mix", "pool_scale", "b_glu", "norm_ffn"):
        p[n] = p[n].reshape(nl, 1, -1)
    swap = lambda a: jnp.swapaxes(a, -1, -2)
    bpad = jax.vmap(_pad_pairs)(bbr, bbi).astype(BF16)
    cpad_t = jax.vmap(_pad_pairs)(swap(p["c_re"]), -swap(p["c_im"])).astype(BF16)
    bpad_t, cpad = swap(bpad), swap(cpad_t)
    dskip = p["d_skip"].reshape(nl, 1, D_SSM)

    layers = []
    h = x
    for l in range(nl):
        wp = get_weights(l, [h] if l else [h, bpad, cpad, bpad_t, cpad_t, ar, ai])
        u, ypool = _mix_in_fwd(h, p["norm_mix"], wp, l, w_pool, p["pool_scale"])
        sre, sim, yraw = _ssm_fwd(u, l, bpad, cpad, ar, ai, dskip)
        hm = _mix_out_fwd(yraw, ypool, h, wp, l, p["b_glu"])
        h_next, n2, act_s, fgate_s, fup_s = _ffn_fwd(hm, p["norm_ffn"], wp, l)
        layers.append(dict(h=h, u=u, ypool=ypool, sre=sre, sim=sim, yraw=yraw, hm=hm, n2=n2, act_s=act_s, wp=wp,
                           fgate_s=fgate_s, fup_s=fup_s))
        h = h_next

    dh, loss, d_norm_final = _final_fwd_bwd(h, p["norm_final"].reshape(1, D_MODEL), target)

    raw = {n: [None] * nl for n in ("dg1", "dwp", "dsc", "dcp", "dbp", "ddsk", "db_glu", "dg2", "dar", "dai")}
    token = None
    for l in reversed(range(nl)):
        s = layers[l]
        wp = s["wp"]
        g1 = lax.empty((1, N_SHARD, P_ROWS, D_MODEL), F32)
        dhm, dg2, dgate_s, dup_s, dhb = _ffn_bwd_act(dh, s["hm"], tied(p["norm_ffn"], token), s["fgate_s"],
                                                      s["fup_s"], wp, l)
        g1 = _ffn_bwd_w(s["n2"], dgate_s, dup_s, s["act_s"], dhb, g1)
        token = ffn_bwd_done(l, [g1])
        dyraw, dyp, db_glu, g1 = _mix_out_bwd(dhm, s["yraw"], s["ypool"], wp, l, tied(p["b_glu"], token), g1)
        dus, dcp, dbp, dar, dai, ddsk = _ssm_bwd(dyraw, s["u"], s["sre"], s["sim"], l, cpad_t, bpad_t, ar, ai, dskip)
        dup, dwp, dsc = _pool_bwd(dyp, s["u"], l, w_pool, p["pool_scale"])
        dh, dg1, g1 = _mix_in_bwd(dup, dus, s["h"], dhm, p["norm_mix"], wp, l, g1)
        token = put_grads(l, g1)
        for n, a in (("dg1", dg1), ("dwp", dwp), ("dsc", dsc), ("dcp", dcp), ("dbp", dbp), ("ddsk", ddsk),
                     ("db_glu", db_glu), ("dg2", dg2), ("dar", dar), ("dai", dai)):
            raw[n][l] = a

    st = {n: jnp.stack(v) for n, v in raw.items()}
    dc_re, dc_im = jax.vmap(_unpad_pairs)(swap(st["dcp"]))
    dbbr, dbbi = jax.vmap(_unpad_pairs)(st["dbp"])
    rows = lambda a: a.reshape((n_rows,) + a.shape[2:])
    dlr, dli, dldt, dbr_t, dbi_t = _disc_bwd(lr, li, ldt, br_t, bi_t, st["dar"].reshape(n_rows, 1, SSM_STATE),
                                              st["dai"].reshape(n_rows, 1, SSM_STATE), rows(swap(dbbr)),
                                              rows(swap(dbbi)))
    small = {"norm_mix": st["dg1"][:, 0], "w_pool": st["dwp"], "pool_scale": st["dsc"][:, 0], "c_re": swap(dc_re),
             "c_im": -swap(dc_im), "d_skip": st["ddsk"].reshape(nl, N_SSM_GROUPS, SSM_GROUP),
             "b_glu": st["db_glu"][:, 0], "norm_ffn": st["dg2"][:, 0]}
    small["lam_re"] = dlr.reshape(nl, N_SSM_GROUPS, SSM_STATE)
    small["lam_im"] = dli.reshape(nl, N_SSM_GROUPS, SSM_STATE)
    small["log_dt"] = dldt.reshape(nl, N_SSM_GROUPS)
    small["b_re"] = dbr_t.transpose(0, 2, 1).reshape(nl, N_SSM_GROUPS, SSM_STATE, SSM_GROUP)
    small["b_im"] = dbi_t.transpose(0, 2, 1).reshape(nl, N_SSM_GROUPS, SSM_STATE, SSM_GROUP)
    small["norm_final"] = d_norm_final[0]
    return loss, dh, small


def _flatten_small(d):
    flat = jnp.concatenate([d[n].reshape(-1) for n in _SMALL])
    n_rows = -(-flat.shape[0] // (32 * D_MODEL)) * 32
    return jnp.pad(flat, (0, n_rows * D_MODEL - flat.shape[0])).reshape(n_rows, D_MODEL)


def _split_small(flat, like):
    flat = flat.reshape(-1)
    out, at = {}, 0
    for n in _SMALL:
        size = like[n].size
        out[n] = flat[at:at + size].reshape(like[n].shape)
        at += size
    return out


def kernel(x, norm_mix, w_in, w_pool, pool_scale, lam_re, lam_im, log_dt, b_re, b_im, c_re, c_im, d_skip, w_glu, b_glu, w_out, norm_ffn, w_gate, w_up, w_down, norm_final, loss_target, m_norm_mix, m_w_in, m_w_pool, m_pool_scale, m_lam_re, m_lam_im, m_log_dt, m_b_re, m_b_im, m_c_re, m_c_im, m_d_skip, m_w_glu, m_b_glu, m_w_out, m_norm_ffn, m_w_gate, m_w_up, m_w_down, m_norm_final, v_norm_mix, v_w_in, v_w_pool, v_pool_scale, v_lam_re, v_lam_im, v_log_dt, v_b_re, v_b_im, v_c_re, v_c_im, v_d_skip, v_w_glu, v_b_glu, v_w_out, v_norm_ffn, v_w_gate, v_w_up, v_w_down, v_norm_final):
    given = dict(locals())
    w = {n: given[n] for n in _WEIGHTS}
    m = {n: given["m_" + n] for n in _WEIGHTS}
    v = {n: given["v_" + n] for n in _WEIGHTS}
    ids = jnp.stack([lax.axis_index("c"), 2 * lax.axis_index("x") + lax.axis_index("y")]).astype(jnp.int32)

    t_names = ("w_gate", "w_up")
    tr = lambda a: a.transpose(0, 2, 1)
    for d in (w, m, v):
        d.update({n: tr(d[n]) for n in t_names})

    nl = norm_mix.shape[0]
    packed = [_pack_weights(ids, l, w["w_in"], w["w_glu"], w["w_out"], w["w_down"], w["w_gate"], w["w_up"])
              for l in range(nl)]
    started, last = {}, ids
    for l in range(nl):
        started[l] = _ag_start(f"ag_start_{l}", packed[l], last)
        last = started[l][3]
    flat = [_flatten_small(d)[None] for d in (w, m, v)]

    def get_weights(l, after):
        send_sems, recv_sems, buf, _ = started[l]
        after = after + ([last] + flat if l == 0 else [])
        return _ag_forward(_ag_wait(f"ag_wait_{l}", send_sems, recv_sems, buf, after))

    to_sibling, to_chips, reduced = {}, {}, {}

    def put_grads(l, g):
        to_sibling[l] = _rs_sibling_start(f"rs_sibling_start_{l}", g)
        token = to_sibling[l][4]
        if l + 1 in to_chips:
            finish(l + 1, [token])
        return token[:1, :1]

    def ffn_bwd_done(l, after):
        return send_to_chips(l + 1, after) if l + 1 in to_sibling else None

    def send_to_chips(l, after):
        send_sems, recv_sems, g, land, _ = to_sibling.pop(l)
        g, land = _rs_sibling_wait(f"rs_sibling_wait_{l}", send_sems, recv_sems, g, land, after)
        own, t = _rs_add("rs_add", ids, g, land, RS_ROW_TILE)
        send_sems, recv_sems, t, land, token = _rs_chips_start(f"rs_chips_start_{l}", t)
        to_chips[l] = (send_sems, recv_sems, t, land, own)
        return token[:1, :1]

    def finish(l, after):
        send_sems, recv_sems, t, land, own = to_chips.pop(l)
        land = _rs_chips_wait(f"rs_chips_wait_{l}", send_sems, recv_sems, t, land, after)
        shard = lax.empty((1, P_ROWS, D_MODEL), F32)
        reduced[l] = _rs_exchange(_rs_sum(ids, 0, own, land, shard, RS_ROW_TILE), 0)

    loss, grad_x, small = _local_step(x[0], loss_target[0], {n: w[n] for n in _SMALL}, get_weights, ffn_bwd_done,
                                      put_grads)
    loss = lax.psum(loss[0, 0], ("x", "y", "c"))
    token = send_to_chips(0, [small["norm_final"]])

    big = (("w_in", P_IN_BLK, 256, False), ("w_out", P_OUT_BLK, 256, False), ("w_down", P_WD_BLK, 352, False),
           ("w_gate", P_WG_BLK, 352, False), ("w_up", P_WU_BLK, 352, False), ("w_glu", P_GLU_BLK, 128, True))
    res = {n: None for n, *_ in big}

    def adamw_layer(l, after):
        for n, (blk, idx), row_tile, glu in big:
            res[n] = _adamw("adamw_" + n, l, w[n], m[n], v[n], reduced[l], (blk, D_MODEL), blk * idx, row_tile,
                            res[n], after, glu)

    for l in reversed(range(1, nl)):
        adamw_layer(l, [token])
    small["norm_final"] = small["norm_final"] + token[0]
    small_sum = _small_all_reduce(_flatten_small(small))
    finish(0, [small_sum] + [r[0] for r in res.values() if r is not None])
    adamw_layer(0, [])
    for n in t_names:
        res[n] = tuple(tr(a) for a in res[n])
    n_rows = flat[0].shape[1]
    outs = _adamw("adamw_small", 0, *flat, small_sum[None], (n_rows, D_MODEL), 0, n_rows // 4)
    parts = [_split_small(o[0], w) for o in outs]
    for n in _SMALL:
        res[n] = tuple(part[n] for part in parts)

    return (loss, grad_x[None], *[res[n][0] for n in _WEIGHTS], *[res[n][1] for n in _WEIGHTS],
            *[res[n][2] for n in _WEIGHTS], *[res[n][3] for n in _WEIGHTS])
```

```python
import functools
import math

import jax
import jax.numpy as jnp
from jax import lax
from jax.experimental import pallas as pl
from jax.experimental.pallas import tpu as pltpu

F32 = jnp.float32
BF16 = jnp.bfloat16

D_MODEL = 1024
D_POOL = 512
D_SSM = 512
POOL_WINDOWS = (2, 4, 8, 16)
POOL_GROUP = 128
POOL_HALO = 16
N_SSM_GROUPS = 32
SSM_GROUP = 16
SSM_STATE = 64
N_STATE = N_SSM_GROUPS * SSM_STATE
N_PAIRS = N_SSM_GROUPS // 2
D_FF = 2816
N_SHARD = 4
FF_SHARD = D_FF // N_SHARD
RMS_EPS = 1e-6

ADAM_LR = 0.001
ADAM_B1 = 0.9
ADAM_B2 = 0.999
ADAM_EPS = 1e-08
ADAM_WD = 0.01
ADAM_STEP = 10

P_ROWS = 2816
P_WD_BLK = (704, 0)
P_WG_BLK = (704, 1)
P_WU_BLK = (704, 2)
P_FF_ROWS = 2112
P_GLU_BLK = (64, 33)
P_GLU_PAD = 192
P_IN_BLK = (256, 9)
P_OUT_BLK = (256, 10)

SUBLANES = 8
VMEM_LIMIT = 56 * 1024 * 1024

TM = 512
TM_FFN = 512
FFN_SPLIT = 2
TS = 1024
SCAN_LANES = 512


def _cparams(n_axes):
    return pltpu.CompilerParams(dimension_semantics=("arbitrary",) * n_axes, vmem_limit_bytes=VMEM_LIMIT)


def _dot(a, b):
    return jnp.dot(a, b, preferred_element_type=F32)


def _dot_nt(a, b):
    return lax.dot_general(a, b, (((1,), (1,)), ((), ())), preferred_element_type=F32)


def _dot_tn(a, b):
    return lax.dot_general(a, b, (((0,), (0,)), ((), ())), preferred_element_type=F32)


def _rms_hat(x):
    r = lax.rsqrt(jnp.mean(x * x, axis=-1, keepdims=True) + RMS_EPS)
    return x * r, r


def _rms_bwd(d_hat, xhat, r):
    return r * (d_hat - xhat * jnp.mean(d_hat * xhat, axis=-1, keepdims=True))


def _sigmoid(x):
    return 1.0 / (1.0 + jnp.exp(-x))


_GELU_C = math.sqrt(2.0 / math.pi)
_GELU_K = 0.044715


def _gelu(x):
    return 0.5 * x * (1.0 + jnp.tanh(_GELU_C * (x + _GELU_K * x * x * x)))


def _gelu_grad(x):
    th = jnp.tanh(_GELU_C * (x + _GELU_K * x * x * x))
    return 0.5 * (1.0 + th) + 0.5 * x * (1.0 - th * th) * _GELU_C * (1.0 + 3.0 * _GELU_K * x * x)


def _glu_weight(ref):
    v = ref[...]
    return jnp.concatenate([v[:, :, :D_SSM], v[:, :, D_SSM:]], axis=1).reshape(D_SSM, D_SSM)


def _glu_pack(w):
    v = w.reshape(N_SHARD, 128, D_SSM)
    return jnp.concatenate([v[:, :64, :], v[:, 64:, :]], axis=2)


def _pool_diff(ext, row0, tm):
    rows = row0 + lax.broadcasted_iota(jnp.int32, (tm, 1), 0)
    outs = []
    for gi, w in enumerate(POOL_WINDOWS):
        e = ext[:, gi * POOL_GROUP:(gi + 1) * POOL_GROUP]
        s = e
        k = 1
        while k < w:
            s = s + pltpu.roll(s, k, 0)
            k *= 2
        inv = 1.0 / jnp.minimum(rows + 1, w).astype(F32)
        outs.append(s[POOL_HALO:, :] * inv - e[POOL_HALO:, :])
    return outs


def _mix_in_fwd(h, g1, wp, layer, w_pool, scale):
    L = h.shape[0]
    tm = min(TM, L)

    def body(h_ref, g_ref, w_ref, wp_ref, sc_ref, u_ref, yp_ref, carry):
        i = pl.program_id(0)

        @pl.when(i == 0)
        def _():
            carry[...] = jnp.zeros_like(carry)

        xhat, _ = _rms_hat(h_ref[...])
        n1 = (xhat * g_ref[...]).astype(BF16)
        u = _dot(n1, w_ref[...].reshape(D_MODEL, D_MODEL))
        u_ref[...] = u
        up = u[:, :D_POOL]
        ext = jnp.concatenate([carry[...], up], axis=0)
        carry[...] = up[tm - POOL_HALO:, :]
        diffs = _pool_diff(ext, i * tm, tm)
        for gi in range(4):
            cols = slice(gi * POOL_GROUP, (gi + 1) * POOL_GROUP)
            yp_ref[:, cols] = _dot(diffs[gi].astype(BF16), wp_ref[gi]) * sc_ref[:, cols]

    blk, idx = P_IN_BLK
    return pl.pallas_call(
        body, name="mix_in_fwd", grid=(L // tm,),
        in_specs=[pl.BlockSpec((tm, D_MODEL), lambda i: (i, 0)),
                  pl.BlockSpec((None, 1, D_MODEL), lambda i: (layer, 0, 0)),
                  pl.BlockSpec((N_SHARD, None, blk, D_MODEL), lambda i: (0, 0, idx, 0)),
                  pl.BlockSpec((None, 4, POOL_GROUP, POOL_GROUP), lambda i: (layer, 0, 0, 0)),
                  pl.BlockSpec((None, 1, D_POOL), lambda i: (layer, 0, 0))],
        out_specs=[pl.BlockSpec((tm, D_MODEL), lambda i: (i, 0)),
                   pl.BlockSpec((tm, D_POOL), lambda i: (i, 0))],
        out_shape=[jax.ShapeDtypeStruct((L, D_MODEL), F32), jax.ShapeDtypeStruct((L, D_POOL), F32)],
        scratch_shapes=[pltpu.VMEM((POOL_HALO, D_POOL), F32)],
        compiler_params=_cparams(1),
    )(h, g1, wp, w_pool, scale)


def _cmul(xr, xi, yr, yi):
    return xr * yr - xi * yi, xr * yi + xi * yr


def _scan_tables(ar, ai, tab, reverse):
    c = ar.shape[1]
    row = lax.broadcasted_iota(jnp.int32, (SUBLANES, c), 0)
    a2r, a2i = _cmul(ar, ai, ar, ai)
    a4r, a4i = _cmul(a2r, a2i, a2r, a2i)
    zero = jnp.zeros((SUBLANES, c), F32)
    for n, (s, pr, pi) in enumerate(((1, ar, ai), (2, a2r, a2i), (4, a4r, a4i))):
        keep = (row < SUBLANES - s) if reverse else (row >= s)
        tab[2 * n] = jnp.where(keep, pr, zero)
        tab[2 * n + 1] = jnp.where(keep, pi, zero)
    cr, ci = ar, ai
    tr, ti = zero, zero
    for n in range(SUBLANES):
        at = (SUBLANES - 1 - n) if reverse else n
        tr = jnp.where(row == at, cr, tr)
        ti = jnp.where(row == at, ci, ti)
        cr, ci = _cmul(cr, ci, ar, ai)
    tab[6] = tr
    tab[7] = ti


def _ssm_fwd(u, layer, bpad, cpad, ar, ai, dskip):
    L = u.shape[0]
    ts = min(TS, L)
    nq = 4
    cq = N_STATE // nq

    def body(u_ref, bp_ref, cp_ref, ar_ref, ai_ref, dsk_ref, sre_ref, sim_ref, y_ref, cr, ci, tab):
        t = pl.program_id(1)

        @pl.when(t == 0)
        def _():
            cr[...] = jnp.zeros_like(cr)
            ci[...] = jnp.zeros_like(ci)
            _scan_tables(ar_ref[...], ai_ref[...], tab, reverse=False)

        uf = u_ref[...]
        ub = uf.astype(BF16)
        for jj in range(4):
            bu = _dot(ub, bp_ref[jj])
            sre_ref[:, jj * 128:(jj + 1) * 128] = bu[:, :128]
            sim_ref[:, jj * 128:(jj + 1) * 128] = bu[:, 128:]

        for cc in range(cq // SCAN_LANES):
            cols = slice(cc * SCAN_LANES, (cc + 1) * SCAN_LANES)
            def step(i, carry, cols=cols):
                c_r, c_i = carry
                r0 = pl.multiple_of(i * SUBLANES, SUBLANES)
                xr = sre_ref[pl.ds(r0, SUBLANES), cols]
                xi = sim_ref[pl.ds(r0, SUBLANES), cols]
                for n, s in enumerate((1, 2, 4)):
                    tr, ti = tab[2 * n, :, cols], tab[2 * n + 1, :, cols]
                    rr = pltpu.roll(xr, s, 0)
                    ri = pltpu.roll(xi, s, 0)
                    xr, xi = xr + tr * rr - ti * ri, xi + tr * ri + ti * rr
                pr, pi = tab[6, :, cols], tab[7, :, cols]
                xr, xi = xr + pr * c_r - pi * c_i, xi + pr * c_i + pi * c_r
                sre_ref[pl.ds(r0, SUBLANES), cols] = xr
                sim_ref[pl.ds(r0, SUBLANES), cols] = xi
                shp = (SUBLANES, SCAN_LANES)
                return (jnp.broadcast_to(xr[SUBLANES - 1:, :], shp), jnp.broadcast_to(xi[SUBLANES - 1:, :], shp))

            c_r, c_i = lax.fori_loop(0, ts // SUBLANES, step, (cr[:, cols], ci[:, cols]), unroll=2)
            cr[:, cols] = c_r
            ci[:, cols] = c_i

        acc = dsk_ref[...] * uf
        for jj in range(4):
            cols = slice(jj * 128, (jj + 1) * 128)
            scat = jnp.concatenate([sre_ref[:, cols], sim_ref[:, cols]], axis=1).astype(BF16)
            acc = acc + _dot(scat, cp_ref[jj])
        y_ref[...] = acc

    return pl.pallas_call(
        body, name="ssm_fwd", grid=(nq, L // ts),
        in_specs=[pl.BlockSpec((ts, 128), lambda q, t: (t, 4 + q)),
                  pl.BlockSpec((None, 4, 128, 256), lambda q, t: (layer, q, 0, 0)),
                  pl.BlockSpec((None, 4, 256, 128), lambda q, t: (layer, q, 0, 0)),
                  pl.BlockSpec((None, 1, cq), lambda q, t: (layer, 0, q)),
                  pl.BlockSpec((None, 1, cq), lambda q, t: (layer, 0, q)),
                  pl.BlockSpec((None, 1, 128), lambda q, t: (layer, 0, q))],
        out_specs=[pl.BlockSpec((ts, cq), lambda q, t: (t, q)),
                   pl.BlockSpec((ts, cq), lambda q, t: (t, q)),
                   pl.BlockSpec((ts, 128), lambda q, t: (t, q))],
        out_shape=[jax.ShapeDtypeStruct((L, N_STATE), F32), jax.ShapeDtypeStruct((L, N_STATE), F32),
                   jax.ShapeDtypeStruct((L, D_SSM), F32)],
        scratch_shapes=[pltpu.VMEM((SUBLANES, cq), F32), pltpu.VMEM((SUBLANES, cq), F32),
                        pltpu.VMEM((8, SUBLANES, cq), F32)],
        compiler_params=_cparams(2),
    )(u, bpad, cpad, ar, ai, dskip)


def _mix_out_fwd(yraw, ypool, h, wp, layer, b_glu):
    L = h.shape[0]
    tm = min(TM, L)

    def body(yr_ref, yp_ref, h_ref, wglu_ref, b_ref, wout_ref, o_ref):
        y = _gelu(yr_ref[...])
        z = _dot(y.astype(BF16), _glu_weight(wglu_ref)) + b_ref[...]
        o = y * _sigmoid(z)
        mix = jnp.concatenate([yp_ref[...], o], axis=1).astype(BF16)
        o_ref[...] = h_ref[...] + _dot(mix, wout_ref[...].reshape(D_MODEL, D_MODEL))

    gb, gi = P_GLU_BLK
    ob, oi = P_OUT_BLK
    return pl.pallas_call(
        body, name="mix_out_fwd", grid=(L // tm,),
        in_specs=[pl.BlockSpec((tm, D_SSM), lambda i: (i, 0)),
                  pl.BlockSpec((tm, D_POOL), lambda i: (i, 0)),
                  pl.BlockSpec((tm, D_MODEL), lambda i: (i, 0)),
                  pl.BlockSpec((N_SHARD, None, gb, D_MODEL), lambda i: (0, 0, gi, 0)),
                  pl.BlockSpec((None, 1, D_SSM), lambda i: (layer, 0, 0)),
                  pl.BlockSpec((N_SHARD, None, ob, D_MODEL), lambda i: (0, 0, oi, 0))],
        out_specs=pl.BlockSpec((tm, D_MODEL), lambda i: (i, 0)),
        out_shape=jax.ShapeDtypeStruct((L, D_MODEL), F32),
        compiler_params=_cparams(1),
    )(yraw, ypool, h, wp, b_glu, wp)


def _ffn_weights(ref, k):
    return ref[k, 0:FF_SHARD, :], ref[k, FF_SHARD:2 * FF_SHARD, :], ref[k, 2 * FF_SHARD:P_FF_ROWS, :]


def _ffn_weight_spec():
    return pl.BlockSpec((N_SHARD, None, P_FF_ROWS, D_MODEL), lambda m, k: (0, 0, 0, 0),
                        pipeline_mode=pl.Buffered(1))


def _ffn_fwd(h, g2, wp, layer):
    L = h.shape[0]
    tm = min(TM_FFN, L)

    def body(h_ref, g_ref, w_ref, o_ref, n2_ref, act_ref, dgate_ref, dup_ref):
        k = pl.program_id(1)

        @pl.when(k == 0)
        def _():
            x = h_ref[...]
            xhat, _ = _rms_hat(x)
            n2_ref[...] = (xhat * g_ref[...]).astype(BF16)
            o_ref[...] = x

        wd, wg_t, wu_t = _ffn_weights(w_ref, k)
        n2 = n2_ref[...]
        gate = _dot_nt(n2, wg_t)
        up = _dot_nt(n2, wu_t)
        sg = _sigmoid(gate)
        silu = gate * sg
        act = (silu * up).astype(BF16)
        act_ref[...] = act
        dgate_ref[...] = (up * (sg * (1.0 + gate * (1.0 - sg)))).astype(BF16)
        dup_ref[...] = silu.astype(BF16)
        o_ref[...] += _dot(act, wd)

    act_shape = jax.ShapeDtypeStruct((N_SHARD, L, FF_SHARD), BF16)
    return pl.pallas_call(
        body, name="ffn_fwd", grid=(L // tm, N_SHARD),
        in_specs=[pl.BlockSpec((tm, D_MODEL), lambda m, k: (m, 0)),
                  pl.BlockSpec((None, 1, D_MODEL), lambda m, k: (layer, 0, 0)),
                  _ffn_weight_spec()],
        out_specs=[pl.BlockSpec((tm, D_MODEL), lambda m, k: (m, 0)),
                   pl.BlockSpec((tm, D_MODEL), lambda m, k: (m, 0)),
                   pl.BlockSpec((None, tm, FF_SHARD), lambda m, k: (k, m, 0)),
                   pl.BlockSpec((None, tm, FF_SHARD), lambda m, k: (k, m, 0)),
                   pl.BlockSpec((None, tm, FF_SHARD), lambda m, k: (k, m, 0))],
        out_shape=[jax.ShapeDtypeStruct((L, D_MODEL), F32), jax.ShapeDtypeStruct((L, D_MODEL), BF16),
                   act_shape, act_shape, act_shape],
        compiler_params=_cparams(2),
    )(h, g2, wp)


def _final_fwd_bwd(h, gf, target):
    L = h.shape[0]
    tm = min(TM, L)

    def body(h_ref, g_ref, t_ref, dh_ref, loss_ref, dg_ref):
        i = pl.program_id(0)

        @pl.when(i == 0)
        def _():
            loss_ref[...] = jnp.zeros_like(loss_ref)
            dg_ref[...] = jnp.zeros_like(dg_ref)

        xhat, r = _rms_hat(h_ref[...])
        g = g_ref[...]
        e = xhat * g - t_ref[...]
        loss_ref[...] += 0.5 * jnp.sum(jnp.mean(e * e, axis=-1, keepdims=True), axis=0, keepdims=True)
        dy = e * (1.0 / D_MODEL)
        dg_ref[...] += jnp.sum(dy * xhat, axis=0, keepdims=True)
        dh_ref[...] = _rms_bwd(dy * g, xhat, r)

    return pl.pallas_call(
        body, name="final_fwd_bwd", grid=(L // tm,),
        in_specs=[pl.BlockSpec((tm, D_MODEL), lambda i: (i, 0)),
                  pl.BlockSpec((1, D_MODEL), lambda i: (0, 0)),
                  pl.BlockSpec((tm, D_MODEL), lambda i: (i, 0))],
        out_specs=[pl.BlockSpec((tm, D_MODEL), lambda i: (i, 0)),
                   pl.BlockSpec((1, 1), lambda i: (0, 0)),
                   pl.BlockSpec((1, D_MODEL), lambda i: (0, 0))],
        out_shape=[jax.ShapeDtypeStruct((L, D_MODEL), F32), jax.ShapeDtypeStruct((1, 1), F32),
                   jax.ShapeDtypeStruct((1, D_MODEL), F32)],
        compiler_params=_cparams(1),
    )(h, gf, target)


def _ffn_bwd_act(dh, h, g2, fgate_s, fup_s, wp, layer):
    L = h.shape[0]
    tm = min(TM_FFN, L)
    sub = tm // FFN_SPLIT

    def body(dh_ref, h_ref, g_ref, fgate_ref, fup_ref, w_ref,
             dhm_ref, dg_ref, dgate_ref, dup_ref, dhb_ref, dn2):
        m, k = pl.program_id(0), pl.program_id(1)

        @pl.when(jnp.logical_and(m == 0, k == 0))
        def _():
            dg_ref[...] = jnp.zeros_like(dg_ref)

        @pl.when(k == 0)
        def _():
            dhb_ref[...] = dh_ref[...].astype(BF16)
            dn2[...] = jnp.zeros_like(dn2)

        wd, wg_t, wu_t = _ffn_weights(w_ref, k)
        for rows in (slice(r * sub, (r + 1) * sub) for r in range(tm // sub)):
            dact = _dot_nt(dhb_ref[rows, :], wd)
            dgate = (dact * fgate_ref[rows, :].astype(F32)).astype(BF16)
            dup = (dact * fup_ref[rows, :].astype(F32)).astype(BF16)
            dgate_ref[rows, :] = dgate
            dup_ref[rows, :] = dup
            dn2[rows, :] += _dot(dgate, wg_t) + _dot(dup, wu_t)

        @pl.when(k == N_SHARD - 1)
        def _():
            xhat, r = _rms_hat(h_ref[...])
            d = dn2[...]
            dg_ref[...] += jnp.sum(d * xhat, axis=0, keepdims=True)
            dhm_ref[...] = dh_ref[...] + _rms_bwd(d * g_ref[...], xhat, r)

    act_spec = pl.BlockSpec((None, tm, FF_SHARD), lambda m, k: (k, m, 0))
    act_shape = jax.ShapeDtypeStruct((N_SHARD, L, FF_SHARD), BF16)
    row_spec = pl.BlockSpec((tm, D_MODEL), lambda m, k: (m, 0))
    return pl.pallas_call(
        body, name="ffn_bwd_act", grid=(L // tm, N_SHARD),
        in_specs=[row_spec, row_spec,
                  pl.BlockSpec((None, 1, D_MODEL), lambda m, k: (layer, 0, 0)),
                  act_spec, act_spec,
                  _ffn_weight_spec()],
        out_specs=[row_spec,
                   pl.BlockSpec((1, D_MODEL), lambda m, k: (0, 0)),
                   act_spec, act_spec, row_spec],
        out_shape=[jax.ShapeDtypeStruct((L, D_MODEL), F32), jax.ShapeDtypeStruct((1, D_MODEL), F32),
                   act_shape, act_shape, jax.ShapeDtypeStruct((L, D_MODEL), BF16)],
        scratch_shapes=[pltpu.VMEM((tm, D_MODEL), F32)],
        compiler_params=_cparams(2),
    )(dh, h, g2, fgate_s, fup_s, wp)


def _ffn_bwd_w(n2, dgate_s, dup_s, act_s, dhb, gbuf):
    L = n2.shape[0]
    tm = min(TM_FFN, L)

    def body(n2_ref, dgate_ref, dup_ref, act_ref, dhb_ref, g_in, g_ref):
        m = pl.program_id(1)

        @pl.when(m == 0)
        def _():
            g_ref[...] = jnp.zeros_like(g_ref)

        n2v = n2_ref[...]
        g_ref[0:FF_SHARD, :] += _dot_tn(act_ref[...], dhb_ref[...])
        g_ref[FF_SHARD:2 * FF_SHARD, :] += _dot_tn(dgate_ref[...], n2v)
        g_ref[2 * FF_SHARD:P_FF_ROWS, :] += _dot_tn(dup_ref[...], n2v)

    act_spec = pl.BlockSpec((None, tm, FF_SHARD), lambda k, m: (k, m, 0))
    row_spec = pl.BlockSpec((tm, D_MODEL), lambda k, m: (m, 0))
    return pl.pallas_call(
        body, name="ffn_bwd_w", grid=(N_SHARD, L // tm),
        in_specs=[row_spec, act_spec, act_spec, act_spec, row_spec, pl.BlockSpec(memory_space=pl.ANY)],
        out_specs=pl.BlockSpec((None, None, P_FF_ROWS, D_MODEL), lambda k, m: (0, k, 0, 0)),
        out_shape=jax.ShapeDtypeStruct(gbuf.shape, F32),
        input_output_aliases={5: 0},
        compiler_params=_cparams(2),
    )(n2, dgate_s, dup_s, act_s, dhb, gbuf)


def _mix_out_bwd(dhm, yraw, ypool, wp, layer, b_glu, gbuf):
    L = dhm.shape[0]
    tm = min(TM, L)

    def body(dhm_ref, yr_ref, yp_ref, wglu_ref, b_ref, wout_ref, g1_in,
             dyr_ref, dyp_ref, db_ref, g1_ref, dwout, dwglu, gpack):
        i = pl.program_id(0)

        @pl.when(i == 0)
        def _():
            db_ref[...] = jnp.zeros_like(db_ref)
            dwout[...] = jnp.zeros_like(dwout)
            dwglu[...] = jnp.zeros_like(dwglu)

        dhb = dhm_ref[...].astype(BF16)
        wglu = _glu_weight(wglu_ref)
        dmix = _dot_nt(dhb, wout_ref[...].reshape(D_MODEL, D_MODEL))
        dyp_ref[...] = dmix[:, :D_POOL]
        d_o = dmix[:, D_POOL:]
        yraw_v = yr_ref[...]
        y = _gelu(yraw_v)
        yb = y.astype(BF16)
        sig = _sigmoid(_dot(yb, wglu) + b_ref[...])
        mix = jnp.concatenate([yp_ref[...], y * sig], axis=1).astype(BF16)
        dwout[...] += _dot_tn(mix, dhb).reshape(N_SHARD, 256, D_MODEL)
        dz = d_o * y * sig * (1.0 - sig)
        dzb = dz.astype(BF16)
        db_ref[...] += jnp.sum(dz, axis=0, keepdims=True)
        dwglu[...] += _dot_tn(yb, dzb)
        dy = d_o * sig + _dot_nt(dzb, wglu)
        dyr_ref[...] = dy * _gelu_grad(yraw_v)

        @pl.when(i == n_steps - 1)
        def _():
            gpack[:, :gb, :] = _glu_pack(dwglu[...])
            gpack[:, gb:, :] = jnp.zeros((N_SHARD, P_GLU_PAD - gb, D_MODEL), F32)
            pltpu.sync_copy(gpack, g1_ref.at[0, :, pl.ds(gb * gi, P_GLU_PAD), :])
            pltpu.sync_copy(dwout, g1_ref.at[0, :, pl.ds(ob * oi, ob), :])

    gb, gi = P_GLU_BLK
    ob, oi = P_OUT_BLK
    n_steps = L // tm
    return pl.pallas_call(
        body, name="mix_out_bwd", grid=(n_steps,),
        in_specs=[pl.BlockSpec((tm, D_MODEL), lambda i: (i, 0)),
                  pl.BlockSpec((tm, D_SSM), lambda i: (i, 0)),
                  pl.BlockSpec((tm, D_POOL), lambda i: (i, 0)),
                  pl.BlockSpec((N_SHARD, None, gb, D_MODEL), lambda i: (0, 0, gi, 0)),
                  pl.BlockSpec((None, 1, D_SSM), lambda i: (layer, 0, 0)),
                  pl.BlockSpec((N_SHARD, None, ob, D_MODEL), lambda i: (0, 0, oi, 0)),
                  pl.BlockSpec(memory_space=pl.ANY)],
        out_specs=[pl.BlockSpec((tm, D_SSM), lambda i: (i, 0)),
                   pl.BlockSpec((tm, D_POOL), lambda i: (i, 0)),
                   pl.BlockSpec((1, D_SSM), lambda i: (0, 0)),
                   pl.BlockSpec(memory_space=pl.ANY)],
        out_shape=[jax.ShapeDtypeStruct((L, D_SSM), F32), jax.ShapeDtypeStruct((L, D_POOL), F32),
                   jax.ShapeDtypeStruct((1, D_SSM), F32),
                   jax.ShapeDtypeStruct(gbuf.shape, F32)],
        scratch_shapes=[pltpu.VMEM((N_SHARD, ob, D_MODEL), F32), pltpu.VMEM((D_SSM, D_SSM), F32),
                        pltpu.VMEM((N_SHARD, P_GLU_PAD, D_MODEL), F32)],
        input_output_aliases={6: 3},
        compiler_params=_cparams(1),
    )(dhm, yraw, ypool, wp, b_glu, wp, gbuf)


def _ssm_bwd(dyraw, u, sre, sim, layer, cpad_t, bpad_t, ar, ai, dskip):
    L = u.shape[0]
    ts = min(TS, L)
    nt = L // ts
    nq = 4
    cq = N_STATE // nq

    def body(dy_ref, u_ref, sre_ref, sim_ref, ct_ref, bt_ref, ar_ref, ai_ref, dsk_ref,
             du_ref, dcp_ref, dbp_ref, dar_ref, dai_ref, ddsk_ref, gre, gim, cr, ci, tab, accr, acci):
        t = pl.program_id(1)

        @pl.when(t == 0)
        def _():
            for ref in (cr, ci, accr, acci, dcp_ref, dbp_ref, ddsk_ref):
                ref[...] = jnp.zeros_like(ref)
            _scan_tables(ar_ref[...], -ai_ref[...], tab, reverse=True)

        dy = dy_ref[...]
        dyb = dy.astype(BF16)
        uf = u_ref[...]
        ub = uf.astype(BF16)
        for jj in range(4):
            cols = slice(jj * 128, (jj + 1) * 128)
            ds = _dot(dyb, ct_ref[jj])
            gre[:, cols] = ds[:, :128]
            gim[:, cols] = ds[:, 128:]
            scat = jnp.concatenate([sre_ref[:, cols], sim_ref[:, cols]], axis=1).astype(BF16)
            dcp_ref[jj] += _dot_tn(scat, dyb)

        n_grp = ts // SUBLANES
        shp = (SUBLANES, SCAN_LANES)
        last_row = lax.broadcasted_iota(jnp.int32, shp, 0) == SUBLANES - 1
        for cc in range(cq // SCAN_LANES):
            cols = slice(cc * SCAN_LANES, (cc + 1) * SCAN_LANES)
            def step(i, carry, cols=cols):
                c_r, c_i, a_r, a_i = carry
                r0 = pl.multiple_of((n_grp - 1 - i) * SUBLANES, SUBLANES)
                xr = gre[pl.ds(r0, SUBLANES), cols]
                xi = gim[pl.ds(r0, SUBLANES), cols]
                for n, s in enumerate((1, 2, 4)):
                    tr, ti = tab[2 * n, :, cols], tab[2 * n + 1, :, cols]
                    rr = pltpu.roll(xr, SUBLANES - s, 0)
                    ri = pltpu.roll(xi, SUBLANES - s, 0)
                    xr, xi = xr + tr * rr - ti * ri, xi + tr * ri + ti * rr
                qr, qi = tab[6, :, cols], tab[7, :, cols]
                xr, xi = xr + qr * c_r - qi * c_i, xi + qr * c_i + qi * c_r
                gre[pl.ds(r0, SUBLANES), cols] = xr
                gim[pl.ds(r0, SUBLANES), cols] = xi
                nr = jnp.where(last_row, c_r, pltpu.roll(xr, SUBLANES - 1, 0))
                ni = jnp.where(last_row, c_i, pltpu.roll(xi, SUBLANES - 1, 0))
                sr = sre_ref[pl.ds(r0, SUBLANES), cols]
                si = sim_ref[pl.ds(r0, SUBLANES), cols]
                a_r = a_r + sr * nr + si * ni
                a_i = a_i + sr * ni - si * nr
                return (jnp.broadcast_to(xr[:1, :], shp), jnp.broadcast_to(xi[:1, :], shp), a_r, a_i)

            c_r, c_i, a_r, a_i = lax.fori_loop(
                0, n_grp, step, (cr[:, cols], ci[:, cols], accr[:, cols], acci[:, cols]), unroll=2)
            cr[:, cols] = c_r
            ci[:, cols] = c_i
            accr[:, cols] = a_r
            acci[:, cols] = a_i

        acc = dsk_ref[...] * dy
        for jj in range(4):
            cols = slice(jj * 128, (jj + 1) * 128)
            gcat = jnp.concatenate([gre[:, cols], gim[:, cols]], axis=1).astype(BF16)
            acc = acc + _dot(gcat, bt_ref[jj])
            dbp_ref[jj] += _dot_tn(ub, gcat)
        du_ref[...] = acc
        ddsk_ref[...] += jnp.sum(dy * uf, axis=0, keepdims=True)

        @pl.when(t == nt - 1)
        def _():
            dar_ref[...] = jnp.sum(accr[...], axis=0, keepdims=True)
            dai_ref[...] = jnp.sum(acci[...], axis=0, keepdims=True)

    f32_scr = lambda *s: pltpu.VMEM(s, F32)
    return pl.pallas_call(
        body, name="ssm_bwd", grid=(nq, nt),
        in_specs=[pl.BlockSpec((ts, 128), lambda q, t: (nt - 1 - t, q)),
                  pl.BlockSpec((ts, 128), lambda q, t: (nt - 1 - t, 4 + q)),
                  pl.BlockSpec((ts, cq), lambda q, t: (nt - 1 - t, q)),
                  pl.BlockSpec((ts, cq), lambda q, t: (nt - 1 - t, q)),
                  pl.BlockSpec((None, 4, 128, 256), lambda q, t: (layer, q, 0, 0)),
                  pl.BlockSpec((None, 4, 256, 128), lambda q, t: (layer, q, 0, 0)),
                  pl.BlockSpec((None, 1, cq), lambda q, t: (layer, 0, q)),
                  pl.BlockSpec((None, 1, cq), lambda q, t: (layer, 0, q)),
                  pl.BlockSpec((None, 1, 128), lambda q, t: (layer, 0, q))],
        out_specs=[pl.BlockSpec((ts, 128), lambda q, t: (nt - 1 - t, q)),
                   pl.BlockSpec((4, 256, 128), lambda q, t: (q, 0, 0)),
                   pl.BlockSpec((4, 128, 256), lambda q, t: (q, 0, 0)),
                   pl.BlockSpec((1, cq), lambda q, t: (0, q)),
                   pl.BlockSpec((1, cq), lambda q, t: (0, q)),
                   pl.BlockSpec((1, 128), lambda q, t: (0, q))],
        out_shape=[jax.ShapeDtypeStruct((L, D_SSM), F32),
                   jax.ShapeDtypeStruct((N_PAIRS, 256, 128), F32), jax.ShapeDtypeStruct((N_PAIRS, 128, 256), F32),
                   jax.ShapeDtypeStruct((1, N_STATE), F32), jax.ShapeDtypeStruct((1, N_STATE), F32),
                   jax.ShapeDtypeStruct((1, D_SSM), F32)],
        scratch_shapes=[f32_scr(ts, cq), f32_scr(ts, cq), f32_scr(SUBLANES, cq), f32_scr(SUBLANES, cq),
                        f32_scr(8, SUBLANES, cq), f32_scr(SUBLANES, cq), f32_scr(SUBLANES, cq)],
        compiler_params=_cparams(2),
    )(dyraw, u, sre, sim, cpad_t, bpad_t, ar, ai, dskip)


def _pool_bwd(dyp, u, layer, w_pool, scale):
    L = u.shape[0]
    tm = min(TM, L)
    nt = L // tm
    halo_per_tile = tm // POOL_HALO

    def body(dyp_ref, u_ref, halo_ref, wp_ref, sc_ref, du_ref, dwp_ref, dsc_ref, carry):
        i = pl.program_id(0)
        tile = nt - 1 - i

        @pl.when(i == 0)
        def _():
            carry[...] = jnp.zeros_like(carry)
            dwp_ref[...] = jnp.zeros_like(dwp_ref)
            dsc_ref[...] = jnp.zeros_like(dsc_ref)

        up = u_ref[...]
        halo = jnp.where(tile > 0, halo_ref[...], jnp.zeros_like(halo_ref))
        diffs = _pool_diff(jnp.concatenate([halo, up], axis=0), tile * tm, tm)
        rows = tile * tm + lax.broadcasted_iota(jnp.int32, (tm, 1), 0)
        n_ext = tm + POOL_HALO
        for gi, w in enumerate(POOL_WINDOWS):
            cols = slice(gi * POOL_GROUP, (gi + 1) * POOL_GROUP)
            db = diffs[gi].astype(BF16)
            dyp = dyp_ref[:, cols]
            dsc_ref[:, cols] += jnp.sum(dyp * _dot(db, wp_ref[gi]), axis=0, keepdims=True)
            dp = (dyp * sc_ref[:, cols]).astype(BF16)
            ddiff = _dot_nt(dp, wp_ref[gi])
            dwp_ref[gi] += _dot_tn(db, dp)
            e = ddiff * (1.0 / jnp.minimum(rows + 1, w).astype(F32))
            s = jnp.concatenate([e, carry[:, cols]], axis=0)
            k = 1
            while k < w:
                s = s + pltpu.roll(s, n_ext - k, 0)
                k *= 2
            du_ref[:, cols] = s[:tm, :] - ddiff
            carry[:, cols] = e[:POOL_HALO, :]

    return pl.pallas_call(
        body, name="pool_bwd", grid=(nt,),
        in_specs=[pl.BlockSpec((tm, D_POOL), lambda i: (nt - 1 - i, 0)),
                  pl.BlockSpec((tm, D_POOL), lambda i: (nt - 1 - i, 0)),
                  pl.BlockSpec((POOL_HALO, D_POOL), lambda i: (jnp.maximum((nt - 1 - i) * halo_per_tile - 1, 0), 0)),
                  pl.BlockSpec((None, 4, POOL_GROUP, POOL_GROUP), lambda i: (layer, 0, 0, 0)),
                  pl.BlockSpec((None, 1, D_POOL), lambda i: (layer, 0, 0))],
        out_specs=[pl.BlockSpec((tm, D_POOL), lambda i: (nt - 1 - i, 0)),
                   pl.BlockSpec((4, POOL_GROUP, POOL_GROUP), lambda i: (0, 0, 0)),
                   pl.BlockSpec((1, D_POOL), lambda i: (0, 0))],
        out_shape=[jax.ShapeDtypeStruct((L, D_POOL), F32),
                   jax.ShapeDtypeStruct((4, POOL_GROUP, POOL_GROUP), F32),
                   jax.ShapeDtypeStruct((1, D_POOL), F32)],
        scratch_shapes=[pltpu.VMEM((POOL_HALO, D_POOL), F32)],
        compiler_params=_cparams(1),
    )(dyp, u, u, w_pool, scale)


def _mix_in_bwd(dup, dus, h, dhm, g1, wp, layer, gbuf):
    L = h.shape[0]
    tm = min(TM, L)
    n_steps = L // tm
    blk, idx = P_IN_BLK

    def body(dup_ref, dus_ref, h_ref, dhm_ref, g_ref, w_ref, g1_in, dh_ref, dg_ref, g1_ref, dwin):
        i = pl.program_id(0)

        @pl.when(i == 0)
        def _():
            dg_ref[...] = jnp.zeros_like(dg_ref)
            dwin[...] = jnp.zeros_like(dwin)

        du = jnp.concatenate([dup_ref[...], dus_ref[...]], axis=1).astype(BF16)
        dn1 = _dot_nt(du, w_ref[...].reshape(D_MODEL, D_MODEL))
        xhat, r = _rms_hat(h_ref[...])
        g = g_ref[...]
        n1 = (xhat * g).astype(BF16)
        dwin[...] += _dot_tn(n1, du).reshape(N_SHARD, blk, D_MODEL)
        dg_ref[...] += jnp.sum(dn1 * xhat, axis=0, keepdims=True)
        dh_ref[...] = dhm_ref[...] + _rms_bwd(dn1 * g, xhat, r)

        @pl.when(i == n_steps - 1)
        def _():
            pltpu.sync_copy(dwin, g1_ref.at[0, :, pl.ds(blk * idx, blk), :])

    row_spec = pl.BlockSpec((tm, D_MODEL), lambda i: (i, 0))
    half_spec = pl.BlockSpec((tm, D_POOL), lambda i: (i, 0))
    return pl.pallas_call(
        body, name="mix_in_bwd", grid=(n_steps,),
        in_specs=[half_spec, half_spec, row_spec, row_spec,
                  pl.BlockSpec((None, 1, D_MODEL), lambda i: (layer, 0, 0)),
                  pl.BlockSpec((N_SHARD, None, blk, D_MODEL), lambda i: (0, 0, idx, 0)),
                  pl.BlockSpec(memory_space=pl.ANY)],
        out_specs=[row_spec, pl.BlockSpec((1, D_MODEL), lambda i: (0, 0)), pl.BlockSpec(memory_space=pl.ANY)],
        out_shape=[jax.ShapeDtypeStruct((L, D_MODEL), F32), jax.ShapeDtypeStruct((1, D_MODEL), F32),
                   jax.ShapeDtypeStruct(gbuf.shape, F32)],
        scratch_shapes=[pltpu.VMEM((N_SHARD, blk, D_MODEL), F32)],
        input_output_aliases={6: 2},
        compiler_params=_cparams(1),
    )(dup, dus, h, dhm, g1, wp, gbuf)


def _disc_math(lr, li, ldt, br_t, bi_t):
    dt = jnp.exp(ldt)
    mag = jnp.exp(lr * dt)
    ang = li * dt
    ar = mag * jnp.cos(ang)
    ai = mag * jnp.sin(ang)
    den = lr * lr + li * li
    nr, ni = ar - 1.0, ai
    cr = (nr * lr + ni * li) / den
    ci = (ni * lr - nr * li) / den
    return ar, ai, cr * br_t - ci * bi_t, cr * bi_t + ci * br_t


def _disc_fwd(lr, li, ldt, br_t, bi_t):
    def body(lr_ref, li_ref, ldt_ref, br_ref, bi_ref, ar_ref, ai_ref, bbr_ref, bbi_ref):
        ar, ai, bbr, bbi = _disc_math(lr_ref[...], li_ref[...], ldt_ref[...], br_ref[...], bi_ref[...])
        ar_ref[...] = ar
        ai_ref[...] = ai
        bbr_ref[...] = bbr
        bbi_ref[...] = bbi

    shapes = [jax.ShapeDtypeStruct(a.shape, F32) for a in (lr, li, br_t, bi_t)]
    return pl.pallas_call(body, name="ssm_disc_fwd", out_shape=shapes,
                          compiler_params=pltpu.CompilerParams(vmem_limit_bytes=VMEM_LIMIT))(lr, li, ldt, br_t, bi_t)


def _disc_bwd(lr, li, ldt, br_t, bi_t, dar, dai, dbbr, dbbi):
    def body(lr_ref, li_ref, ldt_ref, br_ref, bi_ref, dar_ref, dai_ref, dbbr_ref, dbbi_ref,
             dlr_ref, dli_ref, dldt_ref, dbr_ref, dbi_ref):
        prim = (lr_ref[...], li_ref[...], ldt_ref[...], br_ref[...], bi_ref[...])
        _, pullback = jax.vjp(_disc_math, *prim)
        dlr, dli, dldt, dbr, dbi = pullback((dar_ref[...], dai_ref[...], dbbr_ref[...], dbbi_ref[...]))
        dlr_ref[...] = dlr
        dli_ref[...] = dli
        dldt_ref[...] = dldt
        dbr_ref[...] = dbr
        dbi_ref[...] = dbi

    shapes = [jax.ShapeDtypeStruct(a.shape, F32) for a in (lr, li, ldt, br_t, bi_t)]
    return pl.pallas_call(body, name="ssm_disc_bwd", out_shape=shapes,
                          compiler_params=pltpu.CompilerParams(vmem_limit_bytes=VMEM_LIMIT))(
        lr, li, ldt, br_t, bi_t, dar, dai, dbbr, dbbi)


def _pad_pairs(m_re, m_im):
    def blocks(m):
        v = m.transpose(0, 2, 1).reshape(N_PAIRS, 2, SSM_GROUP, SSM_STATE)
        return jnp.einsum("ab,jahp->jahbp", jnp.eye(2, dtype=m.dtype), v).reshape(N_PAIRS, 32, 128)
    both = jnp.concatenate([blocks(m_re), blocks(m_im)], axis=-1)
    place = jax.nn.one_hot(jnp.arange(N_PAIRS) % 4, 4, dtype=both.dtype)
    return jnp.einsum("jk,jrc->jkrc", place, both).reshape(N_PAIRS, 128, 256)


def _unpad_pairs(x):
    place = jax.nn.one_hot(jnp.arange(N_PAIRS) % 4, 4, dtype=x.dtype)
    both = jnp.einsum("jk,jkrc->jrc", place, x.reshape(N_PAIRS, 4, 32, 256))

    def unblock(v):
        v = v.reshape(N_PAIRS, 2, SSM_GROUP, 2, SSM_STATE)
        d = jnp.einsum("ab,jahbp->jahp", jnp.eye(2, dtype=x.dtype), v)
        return d.reshape(N_SSM_GROUPS, SSM_GROUP, SSM_STATE).transpose(0, 2, 1)
    return unblock(both[..., :128]), unblock(both[..., 128:])


def _adamw_math(w, g, m, v):
    m = ADAM_B1 * m + (1.0 - ADAM_B1) * g
    v = ADAM_B2 * v + (1.0 - ADAM_B2) * (g * g)
    m_hat = m / (1.0 - ADAM_B1 ** ADAM_STEP)
    v_hat = v / (1.0 - ADAM_B2 ** ADAM_STEP)
    delta = -ADAM_LR * (m_hat / (jnp.sqrt(v_hat) + ADAM_EPS) + ADAM_WD * w)
    return delta, m, v


def _adamw(name, layer, w, m, v, gbuf, g_block, g_row0, row_tile, outs=None, after=(), glu=False):
    nl, r, c = w.shape
    n_tiles = r // row_tile
    g_rows, g_cols = g_block
    g_tile = g_rows // n_tiles
    g_off = g_row0 // g_tile
    if outs is None:
        outs = [lax.empty(w.shape, F32) for _ in range(4)]

    def body(w_ref, m_ref, v_ref, g_ref, *rest):
        go_ref, d_ref, mo_ref, vo_ref = rest[-4:]
        g = g_ref[...]
        if glu:
            g = jnp.concatenate([g[:, :D_SSM], g[:, D_SSM:]], axis=0)
        delta, mn, vn = _adamw_math(w_ref[...], g, m_ref[...], v_ref[...])
        go_ref[...] = g
        d_ref[...] = delta
        mo_ref[...] = mn
        vo_ref[...] = vn

    w_spec = pl.BlockSpec((None, row_tile, c), lambda j: (layer, j, 0))
    shape = jax.ShapeDtypeStruct(w.shape, F32)
    return pl.pallas_call(
        body, name=name, grid=(n_tiles,),
        in_specs=[w_spec, w_spec, w_spec, pl.BlockSpec((None, g_tile, g_cols), lambda j: (0, g_off + j, 0))]
        + [_ANY] * (4 + len(after)),
        out_specs=[w_spec] * 4,
        out_shape=[shape] * 4,
        input_output_aliases={4: 0, 5: 1, 6: 2, 7: 3},
        compiler_params=_cparams(1),
    )(w, m, v, gbuf, *outs, *after)


def _pack_weights(ids, layer, w_in, w_glu, w_out, w_down, w_gate_t, w_up_t):
    gb, gi = P_GLU_BLK
    ib, ii = P_IN_BLK
    ob, oi = P_OUT_BLK

    def body(ids_ref, in_ref, glu_ref, out_ref, dn_ref, gate_ref, up_ref, p_ref):
        p_ref[0:FF_SHARD, :] = dn_ref[...].astype(BF16)
        p_ref[FF_SHARD:2 * FF_SHARD, :] = gate_ref[...].astype(BF16)
        p_ref[2 * FF_SHARD:P_FF_ROWS, :] = up_ref[...].astype(BF16)
        g = glu_ref[...]
        p_ref[gb * gi:gb * (gi + 1), :] = jnp.concatenate([g[:gb, :], g[gb:, :]], axis=1).astype(BF16)
        p_ref[gb * (gi + 1):ib * ii, :] = jnp.zeros((P_GLU_PAD - gb, D_MODEL), BF16)
        p_ref[ib * ii:ib * (ii + 1), :] = in_ref[...].astype(BF16)
        p_ref[ob * oi:ob * (oi + 1), :] = out_ref[...].astype(BF16)

    def spec(a):
        return pl.BlockSpec((None,) + a.shape[1:], lambda i, ids_ref: (layer, 0, 0))

    ins = (w_in, w_glu, w_out, w_down, w_gate_t, w_up_t)
    grid_spec = pltpu.PrefetchScalarGridSpec(
        num_scalar_prefetch=1, grid=(1,),
        in_specs=[spec(a) for a in ins],
        out_specs=pl.BlockSpec((None, None, P_ROWS, D_MODEL), lambda i, ids_ref: (ids_ref[1], 0, 0, 0)))
    return pl.pallas_call(
        body, name="pack_weights", grid_spec=grid_spec,
        out_shape=jax.ShapeDtypeStruct((N_SHARD, 1, P_ROWS, D_MODEL), BF16),
        compiler_params=_cparams(1),
    )(ids, *ins)


MESH = pl.DeviceIdType.MESH
_ANY = pl.BlockSpec(memory_space=pl.ANY)
P_HALF = P_ROWS // 2
RS_ROW_TILE = 352


def _mesh_pos():
    return lax.axis_index("x"), lax.axis_index("y"), lax.axis_index("c")


def _other_chips(x, y):
    return [(1 - x, y), (x, 1 - y), (1 - x, 1 - y)]


def _remote(src, dst, send_sems, recv_sems, n, to):
    return pltpu.make_async_remote_copy(src_ref=src, dst_ref=dst, send_sem=send_sems.at[n],
                                        recv_sem=recv_sems.at[n], device_id=to, device_id_type=MESH)


_HBM = pl.BlockSpec(memory_space=pltpu.HBM)
_SEM = pl.BlockSpec(memory_space=pltpu.SEMAPHORE)
_EFFECT = pltpu.CompilerParams(has_side_effects=pltpu.SideEffectType.DATAFLOW_SIDE_EFFECTING)
_TOKEN = jax.ShapeDtypeStruct((8, 128), F32)


def _in_hbm(a):
    return pltpu.with_memory_space_constraint(a, pltpu.HBM)


def _ag_start(name, wp, after):
    def body(w_ref, after_ref, send_sems, recv_sems, w_thru, token):
        x, y, c = _mesh_pos()
        mine = w_ref.at[2 * x + y, :, pl.ds(c * P_HALF, P_HALF), :]
        for j, (px, py) in enumerate(_other_chips(x, y)):
            _remote(mine, mine, send_sems, recv_sems, j, (px, py, c)).start()
        token[...] = jnp.zeros_like(token)

    return pl.pallas_call(
        body, name=name,
        out_shape=(pltpu.SemaphoreType.DMA((3,)), pltpu.SemaphoreType.DMA((3,)), pltpu.HBM(wp.shape, wp.dtype), _TOKEN),
        in_specs=(_HBM, _ANY), out_specs=(_SEM, _SEM, _HBM, pl.BlockSpec(memory_space=pltpu.VMEM)),
        input_output_aliases={0: 2}, compiler_params=_EFFECT,
    )(_in_hbm(wp), after)


def _ag_wait(name, send_sems, recv_sems, wp, after):
    def body(w_ref, send_sems, recv_sems, *rest):
        x, y, c = _mesh_pos()
        mine = w_ref.at[2 * x + y, :, pl.ds(c * P_HALF, P_HALF), :]
        for j, (px, py) in enumerate(_other_chips(x, y)):
            landed = w_ref.at[2 * px + py, :, pl.ds(c * P_HALF, P_HALF), :]
            cp = _remote(mine, landed, send_sems, recv_sems, j, (px, py, c))
            cp.wait_send()
            cp.wait_recv()

    return pl.pallas_call(
        body, name=name, out_shape=pltpu.HBM(wp.shape, wp.dtype),
        in_specs=(_HBM, _SEM, _SEM) + (_ANY,) * len(after), out_specs=_HBM,
        input_output_aliases={0: 0}, compiler_params=_EFFECT,
    )(wp, send_sems, recv_sems, *after)


def _ag_forward(wp):
    def body(w_in, o, send_sems, recv_sems):
        x, y, c = _mesh_pos()
        sib = (x, y, 1 - c)
        chips = _other_chips(x, y)
        sends = []
        for j, (px, py) in enumerate(chips):
            landed = o.at[2 * px + py, :, pl.ds(c * P_HALF, P_HALF), :]
            cp = _remote(landed, landed, send_sems, recv_sems, j, sib)
            cp.start()
            sends.append(cp)
        for j, (px, py) in enumerate(chips):
            passed = o.at[2 * px + py, :, pl.ds((1 - c) * P_HALF, P_HALF), :]
            _remote(passed, passed, send_sems, recv_sems, j, sib).wait_recv()
        for cp in sends:
            cp.wait_send()

    return pl.pallas_call(
        body, name="ag_forward",
        in_specs=[_ANY], out_specs=_ANY,
        out_shape=jax.ShapeDtypeStruct(wp.shape, wp.dtype),
        scratch_shapes=[pltpu.SemaphoreType.DMA((3,)), pltpu.SemaphoreType.DMA((3,))],
        input_output_aliases={0: 0},
    )(wp)


def _rs_chips_start(name, t):
    nl = t.shape[0]

    def body(t_ref, land_ref, send_sems, recv_sems, t_thru, land_thru, token):
        x, y, c = _mesh_pos()
        for j, (px, py) in enumerate(_other_chips(x, y)):
            _remote(t_ref.at[:, 2 * px + py], land_ref.at[j], send_sems, recv_sems, j, (px, py, c)).start()
        token[...] = jnp.zeros_like(token)

    land = lax.empty((3, nl, P_HALF, D_MODEL), BF16)
    return pl.pallas_call(
        body, name=name,
        out_shape=(pltpu.SemaphoreType.DMA((3,)), pltpu.SemaphoreType.DMA((3,)), pltpu.HBM(t.shape, t.dtype),
                   pltpu.HBM(land.shape, land.dtype), _TOKEN),
        in_specs=(_HBM, _HBM), out_specs=(_SEM, _SEM, _HBM, _HBM, pl.BlockSpec(memory_space=pltpu.VMEM)),
        input_output_aliases={0: 2, 1: 3}, compiler_params=_EFFECT,
    )(_in_hbm(t), _in_hbm(land))


def _rs_chips_wait(name, send_sems, recv_sems, t, land, after):
    def body(t_ref, land_ref, send_sems, recv_sems, *rest):
        x, y, c = _mesh_pos()
        for j, (px, py) in enumerate(_other_chips(x, y)):
            cp = _remote(t_ref.at[:, 2 * px + py], land_ref.at[j], send_sems, recv_sems, j, (px, py, c))
            cp.wait_send()
            cp.wait_recv()

    return pl.pallas_call(
        body, name=name, out_shape=(pltpu.HBM(t.shape, t.dtype), pltpu.HBM(land.shape, land.dtype)),
        in_specs=(_HBM, _HBM, _SEM, _SEM) + (_ANY,) * len(after), out_specs=(_HBM, _HBM),
        input_output_aliases={0: 0, 1: 1}, compiler_params=_EFFECT,
    )(t, land, send_sems, recv_sems, *after)[1]


def _rs_sibling_start(name, g):
    nl = g.shape[0]

    def body(g_ref, land_ref, send_sems, recv_sems, g_thru, land_thru, token):
        x, y, c = _mesh_pos()
        _remote(g_ref.at[:, :, pl.ds((1 - c) * P_HALF, P_HALF), :], land_ref, send_sems, recv_sems, 0,
                (x, y, 1 - c)).start()
        token[...] = jnp.zeros_like(token)

    land = lax.empty((nl, N_SHARD, P_HALF, D_MODEL), F32)
    return pl.pallas_call(
        body, name=name,
        out_shape=(pltpu.SemaphoreType.DMA((1,)), pltpu.SemaphoreType.DMA((1,)), pltpu.HBM(g.shape, g.dtype),
                   pltpu.HBM(land.shape, land.dtype), _TOKEN),
        in_specs=(_HBM, _HBM), out_specs=(_SEM, _SEM, _HBM, _HBM, pl.BlockSpec(memory_space=pltpu.VMEM)),
        input_output_aliases={0: 2, 1: 3}, compiler_params=_EFFECT,
    )(_in_hbm(g), _in_hbm(land))


def _rs_sibling_wait(name, send_sems, recv_sems, g, land, after):
    def body(g_ref, land_ref, send_sems, recv_sems, *rest):
        x, y, c = _mesh_pos()
        cp = _remote(g_ref.at[:, :, pl.ds((1 - c) * P_HALF, P_HALF), :], land_ref, send_sems, recv_sems, 0,
                     (x, y, 1 - c))
        cp.wait_send()
        cp.wait_recv()

    return pl.pallas_call(
        body, name=name, out_shape=(pltpu.HBM(g.shape, g.dtype), pltpu.HBM(land.shape, land.dtype)),
        in_specs=(_HBM, _HBM, _SEM, _SEM) + (_ANY,) * len(after), out_specs=(_HBM, _HBM),
        input_output_aliases={0: 0, 1: 1}, compiler_params=_EFFECT,
    )(g, land, send_sems, recv_sems, *after)


def _rs_add(name, ids, g, buf, row_tile):
    nl, _, hr, cols = buf.shape
    n_rt = hr // row_tile

    def body(ids_ref, g_ref, b_ref, own_ref, tb_ref):
        t = g_ref[...] + b_ref[...]
        tb_ref[...] = t.astype(BF16)

        @pl.when(pl.program_id(2) == ids_ref[1])
        def _():
            own_ref[...] = t

    blk = (None, None, row_tile, cols)
    grid_spec = pltpu.PrefetchScalarGridSpec(
        num_scalar_prefetch=1, grid=(nl, n_rt, N_SHARD),
        in_specs=[pl.BlockSpec(blk, lambda l, j, s, ids_ref: (l, s, ids_ref[0] * n_rt + j, 0)),
                  pl.BlockSpec(blk, lambda l, j, s, ids_ref: (l, s, j, 0))],
        out_specs=[pl.BlockSpec((None, row_tile, cols), lambda l, j, s, ids_ref: (l, j, 0)),
                   pl.BlockSpec(blk, lambda l, j, s, ids_ref: (l, s, j, 0))])
    return pl.pallas_call(
        body, name=name, grid_spec=grid_spec,
        out_shape=[jax.ShapeDtypeStruct((nl, hr, cols), F32), jax.ShapeDtypeStruct(buf.shape, BF16)],
        compiler_params=_cparams(3),
    )(ids, g, buf)


def _rs_sum(ids, layer, own, bufb, reduced, row_tile):
    _, hr, cols = own.shape
    n_rt = hr // row_tile

    def body(ids_ref, own_ref, b_ref, reduced_in, f_ref):
        f_ref[...] = ((own_ref[...] + b_ref[0].astype(F32)) + b_ref[1].astype(F32)) + b_ref[2].astype(F32)

    grid_spec = pltpu.PrefetchScalarGridSpec(
        num_scalar_prefetch=1, grid=(n_rt,),
        in_specs=[pl.BlockSpec((None, row_tile, cols), lambda j, ids_ref: (0, j, 0)),
                  pl.BlockSpec((3, None, row_tile, cols), lambda j, ids_ref: (0, 0, j, 0)),
                  pl.BlockSpec(memory_space=pl.ANY)],
        out_specs=pl.BlockSpec((None, row_tile, cols), lambda j, ids_ref: (layer, ids_ref[0] * n_rt + j, 0)))
    return pl.pallas_call(
        body, name="rs_sum", grid_spec=grid_spec,
        out_shape=jax.ShapeDtypeStruct(reduced.shape, F32),
        input_output_aliases={3: 0},
        compiler_params=_cparams(1),
    )(ids, own, bufb, reduced)


def _rs_exchange(f, layer):
    def body(f_in, o, send_sems, recv_sems):
        x, y, c = _mesh_pos()
        mine = o.at[layer, pl.ds(c * P_HALF, P_HALF), :]
        cp = _remote(mine, mine, send_sems, recv_sems, 0, (x, y, 1 - c))
        cp.start()
        cp.wait_send()
        theirs = o.at[layer, pl.ds((1 - c) * P_HALF, P_HALF), :]
        _remote(theirs, theirs, send_sems, recv_sems, 0, (x, y, 1 - c)).wait_recv()

    return pl.pallas_call(
        body, name="rs_exchange",
        in_specs=[_ANY], out_specs=_ANY,
        out_shape=jax.ShapeDtypeStruct(f.shape, F32),
        scratch_shapes=[pltpu.SemaphoreType.DMA((1,)), pltpu.SemaphoreType.DMA((1,))],
        input_output_aliases={0: 0},
    )(f)


def _small_all_reduce(s):
    n_rows = s.shape[0]
    hr = n_rows // 2

    def body(s_ref, o_ref, sibbuf, tbuf, cbuf, fbuf, send_sems, recv_sems):
        x, y, c = _mesh_pos()
        sib = (x, y, 1 - c)
        mine = pl.ds(pl.multiple_of(c * hr, SUBLANES), hr)
        theirs = pl.ds(pl.multiple_of((1 - c) * hr, SUBLANES), hr)
        first = _remote(s_ref.at[theirs], sibbuf, send_sems, recv_sems, 0, sib)
        first.start()
        first.wait()
        tbuf[...] = s_ref[mine, :] + sibbuf[...]
        cps = []
        for j, (px, py) in enumerate(_other_chips(x, y)):
            cp = _remote(tbuf, cbuf.at[j], send_sems, recv_sems, 1 + j, (px, py, c))
            cp.start()
            cps.append(cp)
        for cp in cps:
            cp.wait()
        f = (tbuf[...] + cbuf[1]) + (cbuf[0] + cbuf[2])
        fbuf[...] = f
        o_ref[mine, :] = f
        last = _remote(fbuf, o_ref.at[mine], send_sems, recv_sems, 4, sib)
        last.start()
        last.wait()

    vmem = pl.BlockSpec(memory_space=pltpu.VMEM)
    return pl.pallas_call(
        body, name="small_all_reduce",
        in_specs=[vmem], out_specs=vmem,
        out_shape=jax.ShapeDtypeStruct(s.shape, F32),
        scratch_shapes=[pltpu.VMEM((hr, D_MODEL), F32), pltpu.VMEM((hr, D_MODEL), F32),
                        pltpu.VMEM((3, hr, D_MODEL), F32), pltpu.VMEM((hr, D_MODEL), F32),
                        pltpu.SemaphoreType.DMA((5,)), pltpu.SemaphoreType.DMA((5,))],
        compiler_params=pltpu.CompilerParams(vmem_limit_bytes=VMEM_LIMIT),
    )(s)


_SMALL = ("norm_mix", "w_pool", "pool_scale", "lam_re", "lam_im", "log_dt", "b_re", "b_im", "c_re", "c_im",
          "d_skip", "b_glu", "norm_ffn", "norm_final")
_WEIGHTS = ("norm_mix", "w_in", "w_pool", "pool_scale", "lam_re", "lam_im", "log_dt", "b_re", "b_im", "c_re",
            "c_im", "d_skip", "w_glu", "b_glu", "w_out", "norm_ffn", "w_gate", "w_up", "w_down", "norm_final")


def _local_step(x, target, p, get_weights, ffn_bwd_done, put_grads):
    nl = p["norm_mix"].shape[0]

    def tied(a, token):
        return a if token is None else a + token
    n_rows = nl * N_SSM_GROUPS
    lr = p["lam_re"].reshape(n_rows, 1, SSM_STATE)
    li = p["lam_im"].reshape(n_rows, 1, SSM_STATE)
    ldt = p["log_dt"].reshape(n_rows, 1, 1)
    br_t = p["b_re"].reshape(n_rows, SSM_STATE, SSM_GROUP).transpose(0, 2, 1)
    bi_t = p["b_im"].reshape(n_rows, SSM_STATE, SSM_GROUP).transpose(0, 2, 1)
    ar, ai, bbr_t, bbi_t = _disc_fwd(lr, li, ldt, br_t, bi_t)
    ar = ar.reshape(nl, 1, N_STATE)
    ai = ai.reshape(nl, 1, N_STATE)
    bbr = bbr_t.transpose(0, 2, 1).reshape(nl, N_SSM_GROUPS, SSM_STATE, SSM_GROUP)
    bbi = bbi_t.transpose(0, 2, 1).reshape(nl, N_SSM_GROUPS, SSM_STATE, SSM_GROUP)
    w_pool = p["w_pool"].astype(BF16)
    p = dict(p)
    for n in ("norm_mix", "pool_scale", "b_glu", "norm_ffn"):
        p[n] = p[n].reshape(nl, 1, -1)
    swap = lambda a: jnp.swapaxes(a, -1, -2)
    bpad = jax.vmap(_pad_pairs)(bbr, bbi).astype(BF16)
    cpad_t = jax.vmap(_pad_pairs)(swap(p["c_re"]), -swap(p["c_im"])).astype(BF16)
    bpad_t, cpad = swap(bpad), swap(cpad_t)
    dskip = p["d_skip"].reshape(nl, 1, D_SSM)

    layers = []
    h = x
    for l in range(nl):
        wp = get_weights(l, [h] if l else [h, bpad, cpad, bpad_t, cpad_t, ar, ai])
        u, ypool = _mix_in_fwd(h, p["norm_mix"], wp, l, w_pool, p["pool_scale"])
        sre, sim, yraw = _ssm_fwd(u, l, bpad, cpad, ar, ai, dskip)
        hm = _mix_out_fwd(yraw, ypool, h, wp, l, p["b_glu"])
        h_next, n2, act_s, fgate_s, fup_s = _ffn_fwd(hm, p["norm_ffn"], wp, l)
        layers.append(dict(h=h, u=u, ypool=ypool, sre=sre, sim=sim, yraw=yraw, hm=hm, n2=n2, act_s=act_s, wp=wp,
                           fgate_s=fgate_s, fup_s=fup_s))
        h = h_next

    dh, loss, d_norm_final = _final_fwd_bwd(h, p["norm_final"].reshape(1, D_MODEL), target)

    raw = {n: [None] * nl for n in ("dg1", "dwp", "dsc", "dcp", "dbp", "ddsk", "db_glu", "dg2", "dar", "dai")}
    token = None
    for l in reversed(range(nl)):
        s = layers[l]
        wp = s["wp"]
        g1 = lax.empty((1, N_SHARD, P_ROWS, D_MODEL), F32)
        dhm, dg2, dgate_s, dup_s, dhb = _ffn_bwd_act(dh, s["hm"], tied(p["norm_ffn"], token), s["fgate_s"],
                                                      s["fup_s"], wp, l)
        g1 = _ffn_bwd_w(s["n2"], dgate_s, dup_s, s["act_s"], dhb, g1)
        token = ffn_bwd_done(l, [g1])
        dyraw, dyp, db_glu, g1 = _mix_out_bwd(dhm, s["yraw"], s["ypool"], wp, l, tied(p["b_glu"], token), g1)
        dus, dcp, dbp, dar, dai, ddsk = _ssm_bwd(dyraw, s["u"], s["sre"], s["sim"], l, cpad_t, bpad_t, ar, ai, dskip)
        dup, dwp, dsc = _pool_bwd(dyp, s["u"], l, w_pool, p["pool_scale"])
        dh, dg1, g1 = _mix_in_bwd(dup, dus, s["h"], dhm, p["norm_mix"], wp, l, g1)
        token = put_grads(l, g1)
        for n, a in (("dg1", dg1), ("dwp", dwp), ("dsc", dsc), ("dcp", dcp), ("dbp", dbp), ("ddsk", ddsk),
                     ("db_glu", db_glu), ("dg2", dg2), ("dar", dar), ("dai", dai)):
            raw[n][l] = a

    st = {n: jnp.stack(v) for n, v in raw.items()}
    dc_re, dc_im = jax.vmap(_unpad_pairs)(swap(st["dcp"]))
    dbbr, dbbi = jax.vmap(_unpad_pairs)(st["dbp"])
    rows = lambda a: a.reshape((n_rows,) + a.shape[2:])
    dlr, dli, dldt, dbr_t, dbi_t = _disc_bwd(lr, li, ldt, br_t, bi_t, st["dar"].reshape(n_rows, 1, SSM_STATE),
                                              st["dai"].reshape(n_rows, 1, SSM_STATE), rows(swap(dbbr)),
                                              rows(swap(dbbi)))
    small = {"norm_mix": st["dg1"][:, 0], "w_pool": st["dwp"], "pool_scale": st["dsc"][:, 0], "c_re": swap(dc_re),
             "c_im": -swap(dc_im), "d_skip": st["ddsk"].reshape(nl, N_SSM_GROUPS, SSM_GROUP),
             "b_glu": st["db_glu"][:, 0], "norm_ffn": st["dg2"][:, 0]}
    small["lam_re"] = dlr.reshape(nl, N_SSM_GROUPS, SSM_STATE)
    small["lam_im"] = dli.reshape(nl, N_SSM_GROUPS, SSM_STATE)
    small["log_dt"] = dldt.reshape(nl, N_SSM_GROUPS)
    small["b_re"] = dbr_t.reshape(nl, N_SSM_GROUPS, SSM_GROUP, SSM_STATE)
    small["b_im"] = dbi_t.reshape(nl, N_SSM_GROUPS, SSM_GROUP, SSM_STATE)
    small["d_skip"] = small["d_skip"].transpose(_SMALL_VIEW["d_skip"])
    small["norm_final"] = d_norm_final
    return loss, dh, small


_SMALL_VIEW = {"b_re": (0, 1, 3, 2), "b_im": (0, 1, 3, 2), "d_skip": (0, 2, 1)}
_SMALL_GROUPS = (("b_re", "b_im"), ("c_re", "c_im"), ("lam_re", "lam_im"), ("norm_mix", "norm_ffn"),
                 ("pool_scale", "b_glu"), ("w_pool",), ("log_dt",), ("d_skip",), ("norm_final",))


def _view(n, a):
    a = a.transpose(_SMALL_VIEW[n]) if n in _SMALL_VIEW else a
    return a[None] if a.ndim == 1 else a


def _unview(n, a, shape):
    a = a.reshape(shape) if len(shape) == 1 else a
    return a.transpose(_SMALL_VIEW[n]) if n in _SMALL_VIEW else a


def _flatten_small(views):
    flat = jnp.concatenate([views[n].reshape(-1) for n in _SMALL])
    n_rows = -(-flat.shape[0] // (32 * D_MODEL)) * 32
    return jnp.pad(flat, (0, n_rows * D_MODEL - flat.shape[0])).reshape(n_rows, D_MODEL)


def _split_small(flat, like):
    flat = flat.reshape(-1)
    out, at = {}, 0
    for n in _SMALL:
        size = like[n].size
        out[n] = flat[at:at + size].reshape(like[n].shape)
        at += size
    return out


def _adamw_small(name, ws, ms, vs, gs):
    k = len(ws)

    def body(*refs):
        ins, outs = refs[:4 * k], refs[4 * k:]
        for i in range(k):
            w, m, v, g = (ins[j * k + i][...] for j in range(4))
            delta, mn, vn = _adamw_math(w, g, m, v)
            outs[i][...] = delta
            outs[k + i][...] = mn
            outs[2 * k + i][...] = vn

    shapes = [jax.ShapeDtypeStruct(w.shape, F32) for w in ws] * 3
    outs = pl.pallas_call(body, name=name, out_shape=shapes,
                          compiler_params=pltpu.CompilerParams(vmem_limit_bytes=VMEM_LIMIT))(*ws, *ms, *vs, *gs)
    return outs[:k], outs[k:2 * k], outs[2 * k:]


def kernel(x, norm_mix, w_in, w_pool, pool_scale, lam_re, lam_im, log_dt, b_re, b_im, c_re, c_im, d_skip, w_glu, b_glu, w_out, norm_ffn, w_gate, w_up, w_down, norm_final, loss_target, m_norm_mix, m_w_in, m_w_pool, m_pool_scale, m_lam_re, m_lam_im, m_log_dt, m_b_re, m_b_im, m_c_re, m_c_im, m_d_skip, m_w_glu, m_b_glu, m_w_out, m_norm_ffn, m_w_gate, m_w_up, m_w_down, m_norm_final, v_norm_mix, v_w_in, v_w_pool, v_pool_scale, v_lam_re, v_lam_im, v_log_dt, v_b_re, v_b_im, v_c_re, v_c_im, v_d_skip, v_w_glu, v_b_glu, v_w_out, v_norm_ffn, v_w_gate, v_w_up, v_w_down, v_norm_final):
    given = dict(locals())
    w = {n: given[n] for n in _WEIGHTS}
    m = {n: given["m_" + n] for n in _WEIGHTS}
    v = {n: given["v_" + n] for n in _WEIGHTS}
    ids = jnp.stack([lax.axis_index("c"), 2 * lax.axis_index("x") + lax.axis_index("y")]).astype(jnp.int32)

    t_names = ("w_gate", "w_up")
    tr = lambda a: a.transpose(0, 2, 1)
    for d in (w, m, v):
        d.update({n: tr(d[n]) for n in t_names})

    nl = norm_mix.shape[0]
    packed = [_pack_weights(ids, l, w["w_in"], w["w_glu"], w["w_out"], w["w_down"], w["w_gate"], w["w_up"])
              for l in range(nl)]
    started, last = {}, ids
    for l in range(nl):
        started[l] = _ag_start(f"ag_start_{l}", packed[l], last)
        last = started[l][3]
    views = [{n: _view(n, d[n]) for n in _SMALL} for d in (w, m, v)]

    def get_weights(l, after):
        send_sems, recv_sems, buf, _ = started[l]
        after = after + ([last] if l == 0 else [])
        return _ag_forward(_ag_wait(f"ag_wait_{l}", send_sems, recv_sems, buf, after))

    to_sibling, to_chips, reduced = {}, {}, {}

    def put_grads(l, g):
        to_sibling[l] = _rs_sibling_start(f"rs_sibling_start_{l}", g)
        token = to_sibling[l][4]
        if l + 1 in to_chips:
            finish(l + 1, [token])
        return token[:1, :1]

    def ffn_bwd_done(l, after):
        return send_to_chips(l + 1, after) if l + 1 in to_sibling else None

    def send_to_chips(l, after):
        send_sems, recv_sems, g, land, _ = to_sibling.pop(l)
        g, land = _rs_sibling_wait(f"rs_sibling_wait_{l}", send_sems, recv_sems, g, land, after)
        own, t = _rs_add("rs_add", ids, g, land, RS_ROW_TILE)
        send_sems, recv_sems, t, land, token = _rs_chips_start(f"rs_chips_start_{l}", t)
        to_chips[l] = (send_sems, recv_sems, t, land, own)
        return token[:1, :1]

    def finish(l, after):
        send_sems, recv_sems, t, land, own = to_chips.pop(l)
        land = _rs_chips_wait(f"rs_chips_wait_{l}", send_sems, recv_sems, t, land, after)
        shard = lax.empty((1, P_ROWS, D_MODEL), F32)
        reduced[l] = _rs_exchange(_rs_sum(ids, 0, own, land, shard, RS_ROW_TILE), 0)

    loss, grad_x, small = _local_step(x[0], loss_target[0], {n: w[n] for n in _SMALL}, get_weights, ffn_bwd_done,
                                      put_grads)
    loss = lax.psum(loss[0, 0], ("x", "y", "c"))
    token = send_to_chips(0, [small["norm_final"]])

    big = (("w_in", P_IN_BLK, 256, False), ("w_out", P_OUT_BLK, 256, False), ("w_down", P_WD_BLK, 352, False),
           ("w_gate", P_WG_BLK, 352, False), ("w_up", P_WU_BLK, 352, False), ("w_glu", P_GLU_BLK, 128, True))
    res = {n: None for n, *_ in big}

    def adamw_layer(l, after):
        for n, (blk, idx), row_tile, glu in big:
            res[n] = _adamw("adamw_" + n, l, w[n], m[n], v[n], reduced[l], (blk, D_MODEL), blk * idx, row_tile,
                            res[n], after, glu)

    for l in reversed(range(1, nl)):
        adamw_layer(l, [token])
    small["norm_final"] = small["norm_final"] + token[:1, :1]
    small_sum = _small_all_reduce(_flatten_small(small))
    finish(0, [small_sum] + [r[0] for r in res.values() if r is not None])
    adamw_layer(0, [])
    for n in t_names:
        res[n] = tuple(tr(a) for a in res[n])
    g_views = _split_small(small_sum, views[0])
    for group in _SMALL_GROUPS:
        deltas, new_ms, new_vs = _adamw_small("adamw_" + group[0], *[[d[n] for n in group] for d in views],
                                              [g_views[n] for n in group])
        for i, n in enumerate(group):
            res[n] = tuple(_unview(n, a, w[n].shape) for a in (g_views[n], deltas[i], new_ms[i], new_vs[i]))

    return (loss, grad_x[None], *[res[n][0] for n in _WEIGHTS], *[res[n][1] for n in _WEIGHTS],
            *[res[n][2] for n in _WEIGHTS], *[res[n][3] for n in _WEIGHTS])
```

```python
import functools
import math

import jax
import jax.numpy as jnp
from jax import lax
from jax.experimental import pallas as pl
from jax.experimental.pallas import tpu as pltpu

F32 = jnp.float32
BF16 = jnp.bfloat16

D_MODEL = 1024
D_POOL = 512
D_SSM = 512
POOL_WINDOWS = (2, 4, 8, 16)
POOL_GROUP = 128
POOL_HALO = 16
N_SSM_GROUPS = 32
SSM_GROUP = 16
SSM_STATE = 64
N_STATE = N_SSM_GROUPS * SSM_STATE
N_PAIRS = N_SSM_GROUPS // 2
D_FF = 2816
N_SHARD = 4
FF_SHARD = D_FF // N_SHARD
RMS_EPS = 1e-6

ADAM_LR = 0.001
ADAM_B1 = 0.9
ADAM_B2 = 0.999
ADAM_EPS = 1e-08
ADAM_WD = 0.01
ADAM_STEP = 10

P_ROWS = 2816
P_WD_BLK = (704, 0)
P_WG_BLK = (704, 1)
P_WU_BLK = (704, 2)
P_FF_ROWS = 2112
P_GLU_BLK = (64, 33)
P_GLU_PAD = 192
P_IN_BLK = (256, 9)
P_OUT_BLK = (256, 10)

SUBLANES = 8
VMEM_LIMIT = 56 * 1024 * 1024

TM = 512
TM_FFN = 512
TM_FFN_FWD = 1024
FFN_SPLIT = 2
TS = 1024
SCAN_LANES = 512


def _cparams(n_axes):
    return pltpu.CompilerParams(dimension_semantics=("arbitrary",) * n_axes, vmem_limit_bytes=VMEM_LIMIT)


def _dot(a, b):
    return jnp.dot(a, b, preferred_element_type=F32)


def _dot_nt(a, b):
    return lax.dot_general(a, b, (((1,), (1,)), ((), ())), preferred_element_type=F32)


def _dot_tn(a, b):
    return lax.dot_general(a, b, (((0,), (0,)), ((), ())), preferred_element_type=F32)


def _rms_hat(x):
    r = lax.rsqrt(jnp.mean(x * x, axis=-1, keepdims=True) + RMS_EPS)
    return x * r, r


def _rms_bwd(d_hat, xhat, r):
    return r * (d_hat - xhat * jnp.mean(d_hat * xhat, axis=-1, keepdims=True))


def _sigmoid(x):
    return 1.0 / (1.0 + jnp.exp(-x))


_GELU_C = math.sqrt(2.0 / math.pi)
_GELU_K = 0.044715


def _gelu(x):
    return 0.5 * x * (1.0 + jnp.tanh(_GELU_C * (x + _GELU_K * x * x * x)))


def _gelu_grad(x):
    th = jnp.tanh(_GELU_C * (x + _GELU_K * x * x * x))
    return 0.5 * (1.0 + th) + 0.5 * x * (1.0 - th * th) * _GELU_C * (1.0 + 3.0 * _GELU_K * x * x)


def _glu_weight(ref):
    v = ref[...]
    return jnp.concatenate([v[:, :, :D_SSM], v[:, :, D_SSM:]], axis=1).reshape(D_SSM, D_SSM)


def _glu_pack(w):
    v = w.reshape(N_SHARD, 128, D_SSM)
    return jnp.concatenate([v[:, :64, :], v[:, 64:, :]], axis=2)


def _pool_diff(ext, row0, tm):
    rows = row0 + lax.broadcasted_iota(jnp.int32, (tm, 1), 0)
    outs = []
    for gi, w in enumerate(POOL_WINDOWS):
        e = ext[:, gi * POOL_GROUP:(gi + 1) * POOL_GROUP]
        s = e
        k = 1
        while k < w:
            s = s + pltpu.roll(s, k, 0)
            k *= 2
        inv = 1.0 / jnp.minimum(rows + 1, w).astype(F32)
        outs.append(s[POOL_HALO:, :] * inv - e[POOL_HALO:, :])
    return outs


def _mix_in_fwd(h, g1, wp, layer, w_pool, scale):
    L = h.shape[0]
    tm = min(TM, L)

    def body(h_ref, g_ref, w_ref, wp_ref, sc_ref, u_ref, yp_ref, carry):
        i = pl.program_id(0)

        @pl.when(i == 0)
        def _():
            carry[...] = jnp.zeros_like(carry)

        xhat, _ = _rms_hat(h_ref[...])
        n1 = (xhat * g_ref[...]).astype(BF16)
        u = _dot(n1, w_ref[...].reshape(D_MODEL, D_MODEL))
        u_ref[...] = u
        up = u[:, :D_POOL]
        ext = jnp.concatenate([carry[...], up], axis=0)
        carry[...] = up[tm - POOL_HALO:, :]
        diffs = _pool_diff(ext, i * tm, tm)
        for gi in range(4):
            cols = slice(gi * POOL_GROUP, (gi + 1) * POOL_GROUP)
            yp_ref[:, cols] = _dot(diffs[gi].astype(BF16), wp_ref[gi]) * sc_ref[:, cols]

    blk, idx = P_IN_BLK
    return pl.pallas_call(
        body, name="mix_in_fwd", grid=(L // tm,),
        in_specs=[pl.BlockSpec((tm, D_MODEL), lambda i: (i, 0)),
                  pl.BlockSpec((None, 1, D_MODEL), lambda i: (layer, 0, 0)),
                  pl.BlockSpec((N_SHARD, None, blk, D_MODEL), lambda i: (0, 0, idx, 0)),
                  pl.BlockSpec((None, 4, POOL_GROUP, POOL_GROUP), lambda i: (layer, 0, 0, 0)),
                  pl.BlockSpec((None, 1, D_POOL), lambda i: (layer, 0, 0))],
        out_specs=[pl.BlockSpec((tm, D_MODEL), lambda i: (i, 0)),
                   pl.BlockSpec((tm, D_POOL), lambda i: (i, 0))],
        out_shape=[jax.ShapeDtypeStruct((L, D_MODEL), F32), jax.ShapeDtypeStruct((L, D_POOL), F32)],
        scratch_shapes=[pltpu.VMEM((POOL_HALO, D_POOL), F32)],
        compiler_params=_cparams(1),
    )(h, g1, wp, w_pool, scale)


def _cmul(xr, xi, yr, yi):
    return xr * yr - xi * yi, xr * yi + xi * yr


def _scan_tables(ar, ai, tab, reverse):
    c = ar.shape[1]
    row = lax.broadcasted_iota(jnp.int32, (SUBLANES, c), 0)
    a2r, a2i = _cmul(ar, ai, ar, ai)
    a4r, a4i = _cmul(a2r, a2i, a2r, a2i)
    zero = jnp.zeros((SUBLANES, c), F32)
    for n, (s, pr, pi) in enumerate(((1, ar, ai), (2, a2r, a2i), (4, a4r, a4i))):
        keep = (row < SUBLANES - s) if reverse else (row >= s)
        tab[2 * n] = jnp.where(keep, pr, zero)
        tab[2 * n + 1] = jnp.where(keep, pi, zero)
    cr, ci = ar, ai
    tr, ti = zero, zero
    for n in range(SUBLANES):
        at = (SUBLANES - 1 - n) if reverse else n
        tr = jnp.where(row == at, cr, tr)
        ti = jnp.where(row == at, ci, ti)
        cr, ci = _cmul(cr, ci, ar, ai)
    tab[6] = tr
    tab[7] = ti


def _ssm_fwd(u, layer, bpad, cpad, ar, ai, dskip):
    L = u.shape[0]
    ts = min(TS, L)
    nq = 4
    cq = N_STATE // nq

    def body(u_ref, bp_ref, cp_ref, ar_ref, ai_ref, dsk_ref, sre_ref, sim_ref, y_ref, cr, ci, tab):
        t = pl.program_id(1)

        @pl.when(t == 0)
        def _():
            cr[...] = jnp.zeros_like(cr)
            ci[...] = jnp.zeros_like(ci)
            _scan_tables(ar_ref[...], ai_ref[...], tab, reverse=False)

        uf = u_ref[...]
        ub = uf.astype(BF16)
        for jj in range(4):
            bu = _dot(ub, bp_ref[jj])
            sre_ref[:, jj * 128:(jj + 1) * 128] = bu[:, :128]
            sim_ref[:, jj * 128:(jj + 1) * 128] = bu[:, 128:]

        for cc in range(cq // SCAN_LANES):
            cols = slice(cc * SCAN_LANES, (cc + 1) * SCAN_LANES)
            def step(i, carry, cols=cols):
                c_r, c_i = carry
                r0 = pl.multiple_of(i * SUBLANES, SUBLANES)
                xr = sre_ref[pl.ds(r0, SUBLANES), cols]
                xi = sim_ref[pl.ds(r0, SUBLANES), cols]
                for n, s in enumerate((1, 2, 4)):
                    tr, ti = tab[2 * n, :, cols], tab[2 * n + 1, :, cols]
                    rr = pltpu.roll(xr, s, 0)
                    ri = pltpu.roll(xi, s, 0)
                    xr, xi = xr + tr * rr - ti * ri, xi + tr * ri + ti * rr
                pr, pi = tab[6, :, cols], tab[7, :, cols]
                xr, xi = xr + pr * c_r - pi * c_i, xi + pr * c_i + pi * c_r
                sre_ref[pl.ds(r0, SUBLANES), cols] = xr
                sim_ref[pl.ds(r0, SUBLANES), cols] = xi
                shp = (SUBLANES, SCAN_LANES)
                return (jnp.broadcast_to(xr[SUBLANES - 1:, :], shp), jnp.broadcast_to(xi[SUBLANES - 1:, :], shp))

            c_r, c_i = lax.fori_loop(0, ts // SUBLANES, step, (cr[:, cols], ci[:, cols]), unroll=2)
            cr[:, cols] = c_r
            ci[:, cols] = c_i

        acc = dsk_ref[...] * uf
        for jj in range(4):
            cols = slice(jj * 128, (jj + 1) * 128)
            scat = jnp.concatenate([sre_ref[:, cols], sim_ref[:, cols]], axis=1).astype(BF16)
            acc = acc + _dot(scat, cp_ref[jj])
        y_ref[...] = acc

    return pl.pallas_call(
        body, name="ssm_fwd", grid=(nq, L // ts),
        in_specs=[pl.BlockSpec((ts, 128), lambda q, t: (t, 4 + q)),
                  pl.BlockSpec((None, 4, 128, 256), lambda q, t: (layer, q, 0, 0)),
                  pl.BlockSpec((None, 4, 256, 128), lambda q, t: (layer, q, 0, 0)),
                  pl.BlockSpec((None, 1, cq), lambda q, t: (layer, 0, q)),
                  pl.BlockSpec((None, 1, cq), lambda q, t: (layer, 0, q)),
                  pl.BlockSpec((None, 1, 128), lambda q, t: (layer, 0, q))],
        out_specs=[pl.BlockSpec((ts, cq), lambda q, t: (t, q)),
                   pl.BlockSpec((ts, cq), lambda q, t: (t, q)),
                   pl.BlockSpec((ts, 128), lambda q, t: (t, q))],
        out_shape=[jax.ShapeDtypeStruct((L, N_STATE), F32), jax.ShapeDtypeStruct((L, N_STATE), F32),
                   jax.ShapeDtypeStruct((L, D_SSM), F32)],
        scratch_shapes=[pltpu.VMEM((SUBLANES, cq), F32), pltpu.VMEM((SUBLANES, cq), F32),
                        pltpu.VMEM((8, SUBLANES, cq), F32)],
        compiler_params=_cparams(2),
    )(u, bpad, cpad, ar, ai, dskip)


def _mix_out_fwd(yraw, ypool, h, wp, layer, b_glu):
    L = h.shape[0]
    tm = min(TM, L)

    def body(yr_ref, yp_ref, h_ref, wglu_ref, b_ref, wout_ref, o_ref):
        y = _gelu(yr_ref[...])
        z = _dot(y.astype(BF16), _glu_weight(wglu_ref)) + b_ref[...]
        o = y * _sigmoid(z)
        mix = jnp.concatenate([yp_ref[...], o], axis=1).astype(BF16)
        o_ref[...] = h_ref[...] + _dot(mix, wout_ref[...].reshape(D_MODEL, D_MODEL))

    gb, gi = P_GLU_BLK
    ob, oi = P_OUT_BLK
    return pl.pallas_call(
        body, name="mix_out_fwd", grid=(L // tm,),
        in_specs=[pl.BlockSpec((tm, D_SSM), lambda i: (i, 0)),
                  pl.BlockSpec((tm, D_POOL), lambda i: (i, 0)),
                  pl.BlockSpec((tm, D_MODEL), lambda i: (i, 0)),
                  pl.BlockSpec((N_SHARD, None, gb, D_MODEL), lambda i: (0, 0, gi, 0)),
                  pl.BlockSpec((None, 1, D_SSM), lambda i: (layer, 0, 0)),
                  pl.BlockSpec((N_SHARD, None, ob, D_MODEL), lambda i: (0, 0, oi, 0))],
        out_specs=pl.BlockSpec((tm, D_MODEL), lambda i: (i, 0)),
        out_shape=jax.ShapeDtypeStruct((L, D_MODEL), F32),
        compiler_params=_cparams(1),
    )(yraw, ypool, h, wp, b_glu, wp)


def _ffn_weights(ref, k):
    return ref[k, 0:FF_SHARD, :], ref[k, FF_SHARD:2 * FF_SHARD, :], ref[k, 2 * FF_SHARD:P_FF_ROWS, :]


def _ffn_weight_spec():
    return pl.BlockSpec((N_SHARD, None, P_FF_ROWS, D_MODEL), lambda m, k: (0, 0, 0, 0),
                        pipeline_mode=pl.Buffered(1))


def _ffn_fwd(h, g2, wp, layer):
    L = h.shape[0]
    tm = min(TM_FFN_FWD, L)

    def body(h_ref, g_ref, w_ref, o_ref, n2_ref, act_ref, dgate_ref, dup_ref):
        k = pl.program_id(1)

        @pl.when(k == 0)
        def _():
            x = h_ref[...]
            xhat, _ = _rms_hat(x)
            n2_ref[...] = (xhat * g_ref[...]).astype(BF16)
            o_ref[...] = x

        wd, wg_t, wu_t = _ffn_weights(w_ref, k)
        n2 = n2_ref[...]
        gate = _dot_nt(n2, wg_t)
        up = _dot_nt(n2, wu_t)
        sg = _sigmoid(gate)
        silu = gate * sg
        act = (silu * up).astype(BF16)
        act_ref[...] = act
        dgate_ref[...] = (up * (sg * (1.0 + gate * (1.0 - sg)))).astype(BF16)
        dup_ref[...] = silu.astype(BF16)
        o_ref[...] += _dot(act, wd)

    act_shape = jax.ShapeDtypeStruct((N_SHARD, L, FF_SHARD), BF16)
    return pl.pallas_call(
        body, name="ffn_fwd", grid=(L // tm, N_SHARD),
        in_specs=[pl.BlockSpec((tm, D_MODEL), lambda m, k: (m, 0)),
                  pl.BlockSpec((None, 1, D_MODEL), lambda m, k: (layer, 0, 0)),
                  _ffn_weight_spec()],
        out_specs=[pl.BlockSpec((tm, D_MODEL), lambda m, k: (m, 0)),
                   pl.BlockSpec((tm, D_MODEL), lambda m, k: (m, 0)),
                   pl.BlockSpec((None, tm, FF_SHARD), lambda m, k: (k, m, 0)),
                   pl.BlockSpec((None, tm, FF_SHARD), lambda m, k: (k, m, 0)),
                   pl.BlockSpec((None, tm, FF_SHARD), lambda m, k: (k, m, 0))],
        out_shape=[jax.ShapeDtypeStruct((L, D_MODEL), F32), jax.ShapeDtypeStruct((L, D_MODEL), BF16),
                   act_shape, act_shape, act_shape],
        compiler_params=_cparams(2),
    )(h, g2, wp)


def _final_fwd_bwd(h, gf, target):
    L = h.shape[0]
    tm = min(TM, L)

    def body(h_ref, g_ref, t_ref, dh_ref, loss_ref, dg_ref):
        i = pl.program_id(0)

        @pl.when(i == 0)
        def _():
            loss_ref[...] = jnp.zeros_like(loss_ref)
            dg_ref[...] = jnp.zeros_like(dg_ref)

        xhat, r = _rms_hat(h_ref[...])
        g = g_ref[...]
        e = xhat * g - t_ref[...]
        loss_ref[...] += 0.5 * jnp.sum(jnp.mean(e * e, axis=-1, keepdims=True), axis=0, keepdims=True)
        dy = e * (1.0 / D_MODEL)
        dg_ref[...] += jnp.sum(dy * xhat, axis=0, keepdims=True)
        dh_ref[...] = _rms_bwd(dy * g, xhat, r)

    return pl.pallas_call(
        body, name="final_fwd_bwd", grid=(L // tm,),
        in_specs=[pl.BlockSpec((tm, D_MODEL), lambda i: (i, 0)),
                  pl.BlockSpec((1, D_MODEL), lambda i: (0, 0)),
                  pl.BlockSpec((tm, D_MODEL), lambda i: (i, 0))],
        out_specs=[pl.BlockSpec((tm, D_MODEL), lambda i: (i, 0)),
                   pl.BlockSpec((1, 1), lambda i: (0, 0)),
                   pl.BlockSpec((1, D_MODEL), lambda i: (0, 0))],
        out_shape=[jax.ShapeDtypeStruct((L, D_MODEL), F32), jax.ShapeDtypeStruct((1, 1), F32),
                   jax.ShapeDtypeStruct((1, D_MODEL), F32)],
        compiler_params=_cparams(1),
    )(h, gf, target)


def _ffn_bwd_act(dh, h, g2, fgate_s, fup_s, wp, layer):
    L = h.shape[0]
    tm = min(TM_FFN, L)
    sub = tm // FFN_SPLIT

    def body(dh_ref, h_ref, g_ref, fgate_ref, fup_ref, w_ref,
             dhm_ref, dg_ref, dgate_ref, dup_ref, dhb_ref, dn2):
        m, k = pl.program_id(0), pl.program_id(1)

        @pl.when(jnp.logical_and(m == 0, k == 0))
        def _():
            dg_ref[...] = jnp.zeros_like(dg_ref)

        @pl.when(k == 0)
        def _():
            dhb_ref[...] = dh_ref[...].astype(BF16)
            dn2[...] = jnp.zeros_like(dn2)

        wd, wg_t, wu_t = _ffn_weights(w_ref, k)
        for rows in (slice(r * sub, (r + 1) * sub) for r in range(tm // sub)):
            dact = _dot_nt(dhb_ref[rows, :], wd)
            dgate = (dact * fgate_ref[rows, :].astype(F32)).astype(BF16)
            dup = (dact * fup_ref[rows, :].astype(F32)).astype(BF16)
            dgate_ref[rows, :] = dgate
            dup_ref[rows, :] = dup
            dn2[rows, :] += _dot(dgate, wg_t) + _dot(dup, wu_t)

        @pl.when(k == N_SHARD - 1)
        def _():
            xhat, r = _rms_hat(h_ref[...])
            d = dn2[...]
            dg_ref[...] += jnp.sum(d * xhat, axis=0, keepdims=True)
            dhm_ref[...] = dh_ref[...] + _rms_bwd(d * g_ref[...], xhat, r)

    act_spec = pl.BlockSpec((None, tm, FF_SHARD), lambda m, k: (k, m, 0))
    act_shape = jax.ShapeDtypeStruct((N_SHARD, L, FF_SHARD), BF16)
    row_spec = pl.BlockSpec((tm, D_MODEL), lambda m, k: (m, 0))
    return pl.pallas_call(
        body, name="ffn_bwd_act", grid=(L // tm, N_SHARD),
        in_specs=[row_spec, row_spec,
                  pl.BlockSpec((None, 1, D_MODEL), lambda m, k: (layer, 0, 0)),
                  act_spec, act_spec,
                  _ffn_weight_spec()],
        out_specs=[row_spec,
                   pl.BlockSpec((1, D_MODEL), lambda m, k: (0, 0)),
                   act_spec, act_spec, row_spec],
        out_shape=[jax.ShapeDtypeStruct((L, D_MODEL), F32), jax.ShapeDtypeStruct((1, D_MODEL), F32),
                   act_shape, act_shape, jax.ShapeDtypeStruct((L, D_MODEL), BF16)],
        scratch_shapes=[pltpu.VMEM((tm, D_MODEL), F32)],
        compiler_params=_cparams(2),
    )(dh, h, g2, fgate_s, fup_s, wp)


def _ffn_bwd_w(n2, dgate_s, dup_s, act_s, dhb, gbuf):
    L = n2.shape[0]
    tm = min(TM_FFN, L)

    def body(n2_ref, dgate_ref, dup_ref, act_ref, dhb_ref, g_in, g_ref):
        m = pl.program_id(1)

        @pl.when(m == 0)
        def _():
            g_ref[...] = jnp.zeros_like(g_ref)

        n2v = n2_ref[...]
        g_ref[0:FF_SHARD, :] += _dot_tn(act_ref[...], dhb_ref[...])
        g_ref[FF_SHARD:2 * FF_SHARD, :] += _dot_tn(dgate_ref[...], n2v)
        g_ref[2 * FF_SHARD:P_FF_ROWS, :] += _dot_tn(dup_ref[...], n2v)

    act_spec = pl.BlockSpec((None, tm, FF_SHARD), lambda k, m: (k, m, 0))
    row_spec = pl.BlockSpec((tm, D_MODEL), lambda k, m: (m, 0))
    return pl.pallas_call(
        body, name="ffn_bwd_w", grid=(N_SHARD, L // tm),
        in_specs=[row_spec, act_spec, act_spec, act_spec, row_spec, pl.BlockSpec(memory_space=pl.ANY)],
        out_specs=pl.BlockSpec((None, None, P_FF_ROWS, D_MODEL), lambda k, m: (0, k, 0, 0)),
        out_shape=jax.ShapeDtypeStruct(gbuf.shape, F32),
        input_output_aliases={5: 0},
        compiler_params=_cparams(2),
    )(n2, dgate_s, dup_s, act_s, dhb, gbuf)


def _mix_out_bwd(dhm, yraw, ypool, wp, layer, b_glu, gbuf):
    L = dhm.shape[0]
    tm = min(TM, L)

    def body(dhm_ref, yr_ref, yp_ref, wglu_ref, b_ref, wout_ref, g1_in,
             dyr_ref, dyp_ref, db_ref, g1_ref, dwout, dwglu, gpack):
        i = pl.program_id(0)

        @pl.when(i == 0)
        def _():
            db_ref[...] = jnp.zeros_like(db_ref)
            dwout[...] = jnp.zeros_like(dwout)
            dwglu[...] = jnp.zeros_like(dwglu)

        dhb = dhm_ref[...].astype(BF16)
        wglu = _glu_weight(wglu_ref)
        dmix = _dot_nt(dhb, wout_ref[...].reshape(D_MODEL, D_MODEL))
        dyp_ref[...] = dmix[:, :D_POOL]
        d_o = dmix[:, D_POOL:]
        yraw_v = yr_ref[...]
        y = _gelu(yraw_v)
        yb = y.astype(BF16)
        sig = _sigmoid(_dot(yb, wglu) + b_ref[...])
        mix = jnp.concatenate([yp_ref[...], y * sig], axis=1).astype(BF16)
        dwout[...] += _dot_tn(mix, dhb).reshape(N_SHARD, 256, D_MODEL)
        dz = d_o * y * sig * (1.0 - sig)
        dzb = dz.astype(BF16)
        db_ref[...] += jnp.sum(dz, axis=0, keepdims=True)
        dwglu[...] += _dot_tn(yb, dzb)
        dy = d_o * sig + _dot_nt(dzb, wglu)
        dyr_ref[...] = dy * _gelu_grad(yraw_v)

        @pl.when(i == n_steps - 1)
        def _():
            gpack[:, :gb, :] = _glu_pack(dwglu[...])
            gpack[:, gb:, :] = jnp.zeros((N_SHARD, P_GLU_PAD - gb, D_MODEL), F32)
            pltpu.sync_copy(gpack, g1_ref.at[0, :, pl.ds(gb * gi, P_GLU_PAD), :])
            pltpu.sync_copy(dwout, g1_ref.at[0, :, pl.ds(ob * oi, ob), :])

    gb, gi = P_GLU_BLK
    ob, oi = P_OUT_BLK
    n_steps = L // tm
    return pl.pallas_call(
        body, name="mix_out_bwd", grid=(n_steps,),
        in_specs=[pl.BlockSpec((tm, D_MODEL), lambda i: (i, 0)),
                  pl.BlockSpec((tm, D_SSM), lambda i: (i, 0)),
                  pl.BlockSpec((tm, D_POOL), lambda i: (i, 0)),
                  pl.BlockSpec((N_SHARD, None, gb, D_MODEL), lambda i: (0, 0, gi, 0)),
                  pl.BlockSpec((None, 1, D_SSM), lambda i: (layer, 0, 0)),
                  pl.BlockSpec((N_SHARD, None, ob, D_MODEL), lambda i: (0, 0, oi, 0)),
                  pl.BlockSpec(memory_space=pl.ANY)],
        out_specs=[pl.BlockSpec((tm, D_SSM), lambda i: (i, 0)),
                   pl.BlockSpec((tm, D_POOL), lambda i: (i, 0)),
                   pl.BlockSpec((1, D_SSM), lambda i: (0, 0)),
                   pl.BlockSpec(memory_space=pl.ANY)],
        out_shape=[jax.ShapeDtypeStruct((L, D_SSM), F32), jax.ShapeDtypeStruct((L, D_POOL), F32),
                   jax.ShapeDtypeStruct((1, D_SSM), F32),
                   jax.ShapeDtypeStruct(gbuf.shape, F32)],
        scratch_shapes=[pltpu.VMEM((N_SHARD, ob, D_MODEL), F32), pltpu.VMEM((D_SSM, D_SSM), F32),
                        pltpu.VMEM((N_SHARD, P_GLU_PAD, D_MODEL), F32)],
        input_output_aliases={6: 3},
        compiler_params=_cparams(1),
    )(dhm, yraw, ypool, wp, b_glu, wp, gbuf)


def _ssm_bwd(dyraw, u, sre, sim, layer, cpad_t, bpad_t, ar, ai, dskip):
    L = u.shape[0]
    ts = min(TS, L)
    nt = L // ts
    nq = 4
    cq = N_STATE // nq

    def body(dy_ref, u_ref, sre_ref, sim_ref, ct_ref, bt_ref, ar_ref, ai_ref, dsk_ref,
             du_ref, dcp_ref, dbp_ref, dar_ref, dai_ref, ddsk_ref, gre, gim, cr, ci, tab, accr, acci):
        t = pl.program_id(1)

        @pl.when(t == 0)
        def _():
            for ref in (cr, ci, accr, acci, dcp_ref, dbp_ref, ddsk_ref):
                ref[...] = jnp.zeros_like(ref)
            _scan_tables(ar_ref[...], -ai_ref[...], tab, reverse=True)

        dy = dy_ref[...]
        dyb = dy.astype(BF16)
        uf = u_ref[...]
        ub = uf.astype(BF16)
        for jj in range(4):
            cols = slice(jj * 128, (jj + 1) * 128)
            ds = _dot(dyb, ct_ref[jj])
            gre[:, cols] = ds[:, :128]
            gim[:, cols] = ds[:, 128:]
            scat = jnp.concatenate([sre_ref[:, cols], sim_ref[:, cols]], axis=1).astype(BF16)
            dcp_ref[jj] += _dot_tn(scat, dyb)

        n_grp = ts // SUBLANES
        shp = (SUBLANES, SCAN_LANES)
        last_row = lax.broadcasted_iota(jnp.int32, shp, 0) == SUBLANES - 1
        for cc in range(cq // SCAN_LANES):
            cols = slice(cc * SCAN_LANES, (cc + 1) * SCAN_LANES)
            def step(i, carry, cols=cols):
                c_r, c_i, a_r, a_i = carry
                r0 = pl.multiple_of((n_grp - 1 - i) * SUBLANES, SUBLANES)
                xr = gre[pl.ds(r0, SUBLANES), cols]
                xi = gim[pl.ds(r0, SUBLANES), cols]
                for n, s in enumerate((1, 2, 4)):
                    tr, ti = tab[2 * n, :, cols], tab[2 * n + 1, :, cols]
                    rr = pltpu.roll(xr, SUBLANES - s, 0)
                    ri = pltpu.roll(xi, SUBLANES - s, 0)
                    xr, xi = xr + tr * rr - ti * ri, xi + tr * ri + ti * rr
                qr, qi = tab[6, :, cols], tab[7, :, cols]
                xr, xi = xr + qr * c_r - qi * c_i, xi + qr * c_i + qi * c_r
                gre[pl.ds(r0, SUBLANES), cols] = xr
                gim[pl.ds(r0, SUBLANES), cols] = xi
                nr = jnp.where(last_row, c_r, pltpu.roll(xr, SUBLANES - 1, 0))
                ni = jnp.where(last_row, c_i, pltpu.roll(xi, SUBLANES - 1, 0))
                sr = sre_ref[pl.ds(r0, SUBLANES), cols]
                si = sim_ref[pl.ds(r0, SUBLANES), cols]
                a_r = a_r + sr * nr + si * ni
                a_i = a_i + sr * ni - si * nr
                return (jnp.broadcast_to(xr[:1, :], shp), jnp.broadcast_to(xi[:1, :], shp), a_r, a_i)

            c_r, c_i, a_r, a_i = lax.fori_loop(
                0, n_grp, step, (cr[:, cols], ci[:, cols], accr[:, cols], acci[:, cols]), unroll=2)
            cr[:, cols] = c_r
            ci[:, cols] = c_i
            accr[:, cols] = a_r
            acci[:, cols] = a_i

        acc = dsk_ref[...] * dy
        for jj in range(4):
            cols = slice(jj * 128, (jj + 1) * 128)
            gcat = jnp.concatenate([gre[:, cols], gim[:, cols]], axis=1).astype(BF16)
            acc = acc + _dot(gcat, bt_ref[jj])
            dbp_ref[jj] += _dot_tn(ub, gcat)
        du_ref[...] = acc
        ddsk_ref[...] += jnp.sum(dy * uf, axis=0, keepdims=True)

        @pl.when(t == nt - 1)
        def _():
            dar_ref[...] = jnp.sum(accr[...], axis=0, keepdims=True)
            dai_ref[...] = jnp.sum(acci[...], axis=0, keepdims=True)

    f32_scr = lambda *s: pltpu.VMEM(s, F32)
    return pl.pallas_call(
        body, name="ssm_bwd", grid=(nq, nt),
        in_specs=[pl.BlockSpec((ts, 128), lambda q, t: (nt - 1 - t, q)),
                  pl.BlockSpec((ts, 128), lambda q, t: (nt - 1 - t, 4 + q)),
                  pl.BlockSpec((ts, cq), lambda q, t: (nt - 1 - t, q)),
                  pl.BlockSpec((ts, cq), lambda q, t: (nt - 1 - t, q)),
                  pl.BlockSpec((None, 4, 128, 256), lambda q, t: (layer, q, 0, 0)),
                  pl.BlockSpec((None, 4, 256, 128), lambda q, t: (layer, q, 0, 0)),
                  pl.BlockSpec((None, 1, cq), lambda q, t: (layer, 0, q)),
                  pl.BlockSpec((None, 1, cq), lambda q, t: (layer, 0, q)),
                  pl.BlockSpec((None, 1, 128), lambda q, t: (layer, 0, q))],
        out_specs=[pl.BlockSpec((ts, 128), lambda q, t: (nt - 1 - t, q)),
                   pl.BlockSpec((4, 256, 128), lambda q, t: (q, 0, 0)),
                   pl.BlockSpec((4, 128, 256), lambda q, t: (q, 0, 0)),
                   pl.BlockSpec((1, cq), lambda q, t: (0, q)),
                   pl.BlockSpec((1, cq), lambda q, t: (0, q)),
                   pl.BlockSpec((1, 128), lambda q, t: (0, q))],
        out_shape=[jax.ShapeDtypeStruct((L, D_SSM), F32),
                   jax.ShapeDtypeStruct((N_PAIRS, 256, 128), F32), jax.ShapeDtypeStruct((N_PAIRS, 128, 256), F32),
                   jax.ShapeDtypeStruct((1, N_STATE), F32), jax.ShapeDtypeStruct((1, N_STATE), F32),
                   jax.ShapeDtypeStruct((1, D_SSM), F32)],
        scratch_shapes=[f32_scr(ts, cq), f32_scr(ts, cq), f32_scr(SUBLANES, cq), f32_scr(SUBLANES, cq),
                        f32_scr(8, SUBLANES, cq), f32_scr(SUBLANES, cq), f32_scr(SUBLANES, cq)],
        compiler_params=_cparams(2),
    )(dyraw, u, sre, sim, cpad_t, bpad_t, ar, ai, dskip)


def _pool_bwd(dyp, u, layer, w_pool, scale):
    L = u.shape[0]
    tm = min(TM, L)
    nt = L // tm
    halo_per_tile = tm // POOL_HALO

    def body(dyp_ref, u_ref, halo_ref, wp_ref, sc_ref, du_ref, dwp_ref, dsc_ref, carry):
        i = pl.program_id(0)
        tile = nt - 1 - i

        @pl.when(i == 0)
        def _():
            carry[...] = jnp.zeros_like(carry)
            dwp_ref[...] = jnp.zeros_like(dwp_ref)
            dsc_ref[...] = jnp.zeros_like(dsc_ref)

        up = u_ref[...]
        halo = jnp.where(tile > 0, halo_ref[...], jnp.zeros_like(halo_ref))
        diffs = _pool_diff(jnp.concatenate([halo, up], axis=0), tile * tm, tm)
        rows = tile * tm + lax.broadcasted_iota(jnp.int32, (tm, 1), 0)
        n_ext = tm + POOL_HALO
        for gi, w in enumerate(POOL_WINDOWS):
            cols = slice(gi * POOL_GROUP, (gi + 1) * POOL_GROUP)
            db = diffs[gi].astype(BF16)
            dyp = dyp_ref[:, cols]
            dsc_ref[:, cols] += jnp.sum(dyp * _dot(db, wp_ref[gi]), axis=0, keepdims=True)
            dp = (dyp * sc_ref[:, cols]).astype(BF16)
            ddiff = _dot_nt(dp, wp_ref[gi])
            dwp_ref[gi] += _dot_tn(db, dp)
            e = ddiff * (1.0 / jnp.minimum(rows + 1, w).astype(F32))
            s = jnp.concatenate([e, carry[:, cols]], axis=0)
            k = 1
            while k < w:
                s = s + pltpu.roll(s, n_ext - k, 0)
                k *= 2
            du_ref[:, cols] = s[:tm, :] - ddiff
            carry[:, cols] = e[:POOL_HALO, :]

    return pl.pallas_call(
        body, name="pool_bwd", grid=(nt,),
        in_specs=[pl.BlockSpec((tm, D_POOL), lambda i: (nt - 1 - i, 0)),
                  pl.BlockSpec((tm, D_POOL), lambda i: (nt - 1 - i, 0)),
                  pl.BlockSpec((POOL_HALO, D_POOL), lambda i: (jnp.maximum((nt - 1 - i) * halo_per_tile - 1, 0), 0)),
                  pl.BlockSpec((None, 4, POOL_GROUP, POOL_GROUP), lambda i: (layer, 0, 0, 0)),
                  pl.BlockSpec((None, 1, D_POOL), lambda i: (layer, 0, 0))],
        out_specs=[pl.BlockSpec((tm, D_POOL), lambda i: (nt - 1 - i, 0)),
                   pl.BlockSpec((4, POOL_GROUP, POOL_GROUP), lambda i: (0, 0, 0)),
                   pl.BlockSpec((1, D_POOL), lambda i: (0, 0))],
        out_shape=[jax.ShapeDtypeStruct((L, D_POOL), F32),
                   jax.ShapeDtypeStruct((4, POOL_GROUP, POOL_GROUP), F32),
                   jax.ShapeDtypeStruct((1, D_POOL), F32)],
        scratch_shapes=[pltpu.VMEM((POOL_HALO, D_POOL), F32)],
        compiler_params=_cparams(1),
    )(dyp, u, u, w_pool, scale)


def _mix_in_bwd(dup, dus, h, dhm, g1, wp, layer, gbuf):
    L = h.shape[0]
    tm = min(TM, L)
    n_steps = L // tm
    blk, idx = P_IN_BLK

    def body(dup_ref, dus_ref, h_ref, dhm_ref, g_ref, w_ref, g1_in, dh_ref, dg_ref, g1_ref, dwin):
        i = pl.program_id(0)

        @pl.when(i == 0)
        def _():
            dg_ref[...] = jnp.zeros_like(dg_ref)
            dwin[...] = jnp.zeros_like(dwin)

        du = jnp.concatenate([dup_ref[...], dus_ref[...]], axis=1).astype(BF16)
        dn1 = _dot_nt(du, w_ref[...].reshape(D_MODEL, D_MODEL))
        xhat, r = _rms_hat(h_ref[...])
        g = g_ref[...]
        n1 = (xhat * g).astype(BF16)
        dwin[...] += _dot_tn(n1, du).reshape(N_SHARD, blk, D_MODEL)
        dg_ref[...] += jnp.sum(dn1 * xhat, axis=0, keepdims=True)
        dh_ref[...] = dhm_ref[...] + _rms_bwd(dn1 * g, xhat, r)

        @pl.when(i == n_steps - 1)
        def _():
            pltpu.sync_copy(dwin, g1_ref.at[0, :, pl.ds(blk * idx, blk), :])

    row_spec = pl.BlockSpec((tm, D_MODEL), lambda i: (i, 0))
    half_spec = pl.BlockSpec((tm, D_POOL), lambda i: (i, 0))
    return pl.pallas_call(
        body, name="mix_in_bwd", grid=(n_steps,),
        in_specs=[half_spec, half_spec, row_spec, row_spec,
                  pl.BlockSpec((None, 1, D_MODEL), lambda i: (layer, 0, 0)),
                  pl.BlockSpec((N_SHARD, None, blk, D_MODEL), lambda i: (0, 0, idx, 0)),
                  pl.BlockSpec(memory_space=pl.ANY)],
        out_specs=[row_spec, pl.BlockSpec((1, D_MODEL), lambda i: (0, 0)), pl.BlockSpec(memory_space=pl.ANY)],
        out_shape=[jax.ShapeDtypeStruct((L, D_MODEL), F32), jax.ShapeDtypeStruct((1, D_MODEL), F32),
                   jax.ShapeDtypeStruct(gbuf.shape, F32)],
        scratch_shapes=[pltpu.VMEM((N_SHARD, blk, D_MODEL), F32)],
        input_output_aliases={6: 2},
        compiler_params=_cparams(1),
    )(dup, dus, h, dhm, g1, wp, gbuf)


def _disc_math(lr, li, ldt, br_t, bi_t):
    dt = jnp.exp(ldt)
    mag = jnp.exp(lr * dt)
    ang = li * dt
    ar = mag * jnp.cos(ang)
    ai = mag * jnp.sin(ang)
    den = lr * lr + li * li
    nr, ni = ar - 1.0, ai
    cr = (nr * lr + ni * li) / den
    ci = (ni * lr - nr * li) / den
    return ar, ai, cr * br_t - ci * bi_t, cr * bi_t + ci * br_t


def _disc_fwd(lr, li, ldt, br_t, bi_t):
    def body(lr_ref, li_ref, ldt_ref, br_ref, bi_ref, ar_ref, ai_ref, bbr_ref, bbi_ref):
        ar, ai, bbr, bbi = _disc_math(lr_ref[...], li_ref[...], ldt_ref[...], br_ref[...], bi_ref[...])
        ar_ref[...] = ar
        ai_ref[...] = ai
        bbr_ref[...] = bbr
        bbi_ref[...] = bbi

    shapes = [jax.ShapeDtypeStruct(a.shape, F32) for a in (lr, li, br_t, bi_t)]
    return pl.pallas_call(body, name="ssm_disc_fwd", out_shape=shapes,
                          compiler_params=pltpu.CompilerParams(vmem_limit_bytes=VMEM_LIMIT))(lr, li, ldt, br_t, bi_t)


def _disc_bwd(lr, li, ldt, br_t, bi_t, dar, dai, dbbr, dbbi):
    def body(lr_ref, li_ref, ldt_ref, br_ref, bi_ref, dar_ref, dai_ref, dbbr_ref, dbbi_ref,
             dlr_ref, dli_ref, dldt_ref, dbr_ref, dbi_ref):
        prim = (lr_ref[...], li_ref[...], ldt_ref[...], br_ref[...], bi_ref[...])
        _, pullback = jax.vjp(_disc_math, *prim)
        dlr, dli, dldt, dbr, dbi = pullback((dar_ref[...], dai_ref[...], dbbr_ref[...], dbbi_ref[...]))
        dlr_ref[...] = dlr
        dli_ref[...] = dli
        dldt_ref[...] = dldt
        dbr_ref[...] = dbr
        dbi_ref[...] = dbi

    shapes = [jax.ShapeDtypeStruct(a.shape, F32) for a in (lr, li, ldt, br_t, bi_t)]
    return pl.pallas_call(body, name="ssm_disc_bwd", out_shape=shapes,
                          compiler_params=pltpu.CompilerParams(vmem_limit_bytes=VMEM_LIMIT))(
        lr, li, ldt, br_t, bi_t, dar, dai, dbbr, dbbi)


def _pad_pairs(m_re, m_im):
    def blocks(m):
        v = m.transpose(0, 2, 1).reshape(N_PAIRS, 2, SSM_GROUP, SSM_STATE)
        return jnp.einsum("ab,jahp->jahbp", jnp.eye(2, dtype=m.dtype), v).reshape(N_PAIRS, 32, 128)
    both = jnp.concatenate([blocks(m_re), blocks(m_im)], axis=-1)
    place = jax.nn.one_hot(jnp.arange(N_PAIRS) % 4, 4, dtype=both.dtype)
    return jnp.einsum("jk,jrc->jkrc", place, both).reshape(N_PAIRS, 128, 256)


def _unpad_pairs(x):
    place = jax.nn.one_hot(jnp.arange(N_PAIRS) % 4, 4, dtype=x.dtype)
    both = jnp.einsum("jk,jkrc->jrc", place, x.reshape(N_PAIRS, 4, 32, 256))

    def unblock(v):
        v = v.reshape(N_PAIRS, 2, SSM_GROUP, 2, SSM_STATE)
        d = jnp.einsum("ab,jahbp->jahp", jnp.eye(2, dtype=x.dtype), v)
        return d.reshape(N_SSM_GROUPS, SSM_GROUP, SSM_STATE).transpose(0, 2, 1)
    return unblock(both[..., :128]), unblock(both[..., 128:])


def _adamw_math(w, g, m, v):
    m = ADAM_B1 * m + (1.0 - ADAM_B1) * g
    v = ADAM_B2 * v + (1.0 - ADAM_B2) * (g * g)
    m_hat = m / (1.0 - ADAM_B1 ** ADAM_STEP)
    v_hat = v / (1.0 - ADAM_B2 ** ADAM_STEP)
    delta = -ADAM_LR * (m_hat / (jnp.sqrt(v_hat) + ADAM_EPS) + ADAM_WD * w)
    return delta, m, v


def _adamw(name, layer, w, m, v, gbuf, g_block, g_row0, row_tile, outs=None, after=(), glu=False):
    nl, r, c = w.shape
    n_tiles = r // row_tile
    g_rows, g_cols = g_block
    g_tile = g_rows // n_tiles
    g_off = g_row0 // g_tile
    if outs is None:
        outs = [lax.empty(w.shape, F32) for _ in range(4)]

    def body(w_ref, m_ref, v_ref, g_ref, *rest):
        go_ref, d_ref, mo_ref, vo_ref = rest[-4:]
        g = g_ref[...]
        if glu:
            g = jnp.concatenate([g[:, :D_SSM], g[:, D_SSM:]], axis=0)
        delta, mn, vn = _adamw_math(w_ref[...], g, m_ref[...], v_ref[...])
        go_ref[...] = g
        d_ref[...] = delta
        mo_ref[...] = mn
        vo_ref[...] = vn

    w_spec = pl.BlockSpec((None, row_tile, c), lambda j: (layer, j, 0))
    shape = jax.ShapeDtypeStruct(w.shape, F32)
    return pl.pallas_call(
        body, name=name, grid=(n_tiles,),
        in_specs=[w_spec, w_spec, w_spec, pl.BlockSpec((None, g_tile, g_cols), lambda j: (0, g_off + j, 0))]
        + [_ANY] * (4 + len(after)),
        out_specs=[w_spec] * 4,
        out_shape=[shape] * 4,
        input_output_aliases={4: 0, 5: 1, 6: 2, 7: 3},
        compiler_params=_cparams(1),
    )(w, m, v, gbuf, *outs, *after)


def _pack_weights(ids, layer, w_in, w_glu, w_out, w_down, w_gate_t, w_up_t):
    gb, gi = P_GLU_BLK
    ib, ii = P_IN_BLK
    ob, oi = P_OUT_BLK

    def body(ids_ref, in_ref, glu_ref, out_ref, dn_ref, gate_ref, up_ref, p_ref):
        p_ref[0:FF_SHARD, :] = dn_ref[...].astype(BF16)
        p_ref[FF_SHARD:2 * FF_SHARD, :] = gate_ref[...].astype(BF16)
        p_ref[2 * FF_SHARD:P_FF_ROWS, :] = up_ref[...].astype(BF16)
        g = glu_ref[...]
        p_ref[gb * gi:gb * (gi + 1), :] = jnp.concatenate([g[:gb, :], g[gb:, :]], axis=1).astype(BF16)
        p_ref[gb * (gi + 1):ib * ii, :] = jnp.zeros((P_GLU_PAD - gb, D_MODEL), BF16)
        p_ref[ib * ii:ib * (ii + 1), :] = in_ref[...].astype(BF16)
        p_ref[ob * oi:ob * (oi + 1), :] = out_ref[...].astype(BF16)

    def spec(a):
        return pl.BlockSpec((None,) + a.shape[1:], lambda i, ids_ref: (layer, 0, 0))

    ins = (w_in, w_glu, w_out, w_down, w_gate_t, w_up_t)
    grid_spec = pltpu.PrefetchScalarGridSpec(
        num_scalar_prefetch=1, grid=(1,),
        in_specs=[spec(a) for a in ins],
        out_specs=pl.BlockSpec((None, None, P_ROWS, D_MODEL), lambda i, ids_ref: (ids_ref[1], 0, 0, 0)))
    return pl.pallas_call(
        body, name="pack_weights", grid_spec=grid_spec,
        out_shape=jax.ShapeDtypeStruct((N_SHARD, 1, P_ROWS, D_MODEL), BF16),
        compiler_params=_cparams(1),
    )(ids, *ins)


MESH = pl.DeviceIdType.MESH
_ANY = pl.BlockSpec(memory_space=pl.ANY)
P_HALF = P_ROWS // 2
RS_ROW_TILE = 352


def _mesh_pos():
    return lax.axis_index("x"), lax.axis_index("y"), lax.axis_index("c")


def _other_chips(x, y):
    return [(1 - x, y), (x, 1 - y), (1 - x, 1 - y)]


def _remote(src, dst, send_sems, recv_sems, n, to):
    return pltpu.make_async_remote_copy(src_ref=src, dst_ref=dst, send_sem=send_sems.at[n],
                                        recv_sem=recv_sems.at[n], device_id=to, device_id_type=MESH)


_HBM = pl.BlockSpec(memory_space=pltpu.HBM)
_SEM = pl.BlockSpec(memory_space=pltpu.SEMAPHORE)
_EFFECT = pltpu.CompilerParams(has_side_effects=pltpu.SideEffectType.DATAFLOW_SIDE_EFFECTING)
_TOKEN = jax.ShapeDtypeStruct((8, 128), F32)


def _in_hbm(a):
    return pltpu.with_memory_space_constraint(a, pltpu.HBM)


def _ag_start(name, wp, after):
    def body(w_ref, after_ref, send_sems, recv_sems, w_thru, token):
        x, y, c = _mesh_pos()
        mine = w_ref.at[2 * x + y, :, pl.ds(c * P_HALF, P_HALF), :]
        for j, (px, py) in enumerate(_other_chips(x, y)):
            _remote(mine, mine, send_sems, recv_sems, j, (px, py, c)).start()
        token[...] = jnp.zeros_like(token)

    return pl.pallas_call(
        body, name=name,
        out_shape=(pltpu.SemaphoreType.DMA((3,)), pltpu.SemaphoreType.DMA((3,)), pltpu.HBM(wp.shape, wp.dtype), _TOKEN),
        in_specs=(_HBM, _ANY), out_specs=(_SEM, _SEM, _HBM, pl.BlockSpec(memory_space=pltpu.VMEM)),
        input_output_aliases={0: 2}, compiler_params=_EFFECT,
    )(_in_hbm(wp), after)


def _ag_wait(name, send_sems, recv_sems, wp, after):
    def body(w_ref, send_sems, recv_sems, *rest):
        x, y, c = _mesh_pos()
        mine = w_ref.at[2 * x + y, :, pl.ds(c * P_HALF, P_HALF), :]
        for j, (px, py) in enumerate(_other_chips(x, y)):
            landed = w_ref.at[2 * px + py, :, pl.ds(c * P_HALF, P_HALF), :]
            cp = _remote(mine, landed, send_sems, recv_sems, j, (px, py, c))
            cp.wait_send()
            cp.wait_recv()

    return pl.pallas_call(
        body, name=name, out_shape=pltpu.HBM(wp.shape, wp.dtype),
        in_specs=(_HBM, _SEM, _SEM) + (_ANY,) * len(after), out_specs=_HBM,
        input_output_aliases={0: 0}, compiler_params=_EFFECT,
    )(wp, send_sems, recv_sems, *after)


def _ag_forward(wp):
    def body(w_in, o, send_sems, recv_sems):
        x, y, c = _mesh_pos()
        sib = (x, y, 1 - c)
        chips = _other_chips(x, y)
        sends = []
        for j, (px, py) in enumerate(chips):
            landed = o.at[2 * px + py, :, pl.ds(c * P_HALF, P_HALF), :]
            cp = _remote(landed, landed, send_sems, recv_sems, j, sib)
            cp.start()
            sends.append(cp)
        for j, (px, py) in enumerate(chips):
            passed = o.at[2 * px + py, :, pl.ds((1 - c) * P_HALF, P_HALF), :]
            _remote(passed, passed, send_sems, recv_sems, j, sib).wait_recv()
        for cp in sends:
            cp.wait_send()

    return pl.pallas_call(
        body, name="ag_forward",
        in_specs=[_ANY], out_specs=_ANY,
        out_shape=jax.ShapeDtypeStruct(wp.shape, wp.dtype),
        scratch_shapes=[pltpu.SemaphoreType.DMA((3,)), pltpu.SemaphoreType.DMA((3,))],
        input_output_aliases={0: 0},
    )(wp)


def _rs_chips_start(name, t):
    nl = t.shape[0]

    def body(t_ref, land_ref, send_sems, recv_sems, t_thru, land_thru, token):
        x, y, c = _mesh_pos()
        for j, (px, py) in enumerate(_other_chips(x, y)):
            _remote(t_ref.at[:, 2 * px + py], land_ref.at[j], send_sems, recv_sems, j, (px, py, c)).start()
        token[...] = jnp.zeros_like(token)

    land = lax.empty((3, nl, P_HALF, D_MODEL), BF16)
    return pl.pallas_call(
        body, name=name,
        out_shape=(pltpu.SemaphoreType.DMA((3,)), pltpu.SemaphoreType.DMA((3,)), pltpu.HBM(t.shape, t.dtype),
                   pltpu.HBM(land.shape, land.dtype), _TOKEN),
        in_specs=(_HBM, _HBM), out_specs=(_SEM, _SEM, _HBM, _HBM, pl.BlockSpec(memory_space=pltpu.VMEM)),
        input_output_aliases={0: 2, 1: 3}, compiler_params=_EFFECT,
    )(_in_hbm(t), _in_hbm(land))


def _rs_chips_wait(name, send_sems, recv_sems, t, land, after):
    def body(t_ref, land_ref, send_sems, recv_sems, *rest):
        x, y, c = _mesh_pos()
        for j, (px, py) in enumerate(_other_chips(x, y)):
            cp = _remote(t_ref.at[:, 2 * px + py], land_ref.at[j], send_sems, recv_sems, j, (px, py, c))
            cp.wait_send()
            cp.wait_recv()

    return pl.pallas_call(
        body, name=name, out_shape=(pltpu.HBM(t.shape, t.dtype), pltpu.HBM(land.shape, land.dtype)),
        in_specs=(_HBM, _HBM, _SEM, _SEM) + (_ANY,) * len(after), out_specs=(_HBM, _HBM),
        input_output_aliases={0: 0, 1: 1}, compiler_params=_EFFECT,
    )(t, land, send_sems, recv_sems, *after)[1]


def _rs_sibling_start(name, g):
    nl = g.shape[0]

    def body(g_ref, land_ref, send_sems, recv_sems, g_thru, land_thru, token):
        x, y, c = _mesh_pos()
        _remote(g_ref.at[:, :, pl.ds((1 - c) * P_HALF, P_HALF), :], land_ref, send_sems, recv_sems, 0,
                (x, y, 1 - c)).start()
        token[...] = jnp.zeros_like(token)

    land = lax.empty((nl, N_SHARD, P_HALF, D_MODEL), F32)
    return pl.pallas_call(
        body, name=name,
        out_shape=(pltpu.SemaphoreType.DMA((1,)), pltpu.SemaphoreType.DMA((1,)), pltpu.HBM(g.shape, g.dtype),
                   pltpu.HBM(land.shape, land.dtype), _TOKEN),
        in_specs=(_HBM, _HBM), out_specs=(_SEM, _SEM, _HBM, _HBM, pl.BlockSpec(memory_space=pltpu.VMEM)),
        input_output_aliases={0: 2, 1: 3}, compiler_params=_EFFECT,
    )(_in_hbm(g), _in_hbm(land))


def _rs_sibling_wait(name, send_sems, recv_sems, g, land, after):
    def body(g_ref, land_ref, send_sems, recv_sems, *rest):
        x, y, c = _mesh_pos()
        cp = _remote(g_ref.at[:, :, pl.ds((1 - c) * P_HALF, P_HALF), :], land_ref, send_sems, recv_sems, 0,
                     (x, y, 1 - c))
        cp.wait_send()
        cp.wait_recv()

    return pl.pallas_call(
        body, name=name, out_shape=(pltpu.HBM(g.shape, g.dtype), pltpu.HBM(land.shape, land.dtype)),
        in_specs=(_HBM, _HBM, _SEM, _SEM) + (_ANY,) * len(after), out_specs=(_HBM, _HBM),
        input_output_aliases={0: 0, 1: 1}, compiler_params=_EFFECT,
    )(g, land, send_sems, recv_sems, *after)


def _rs_add(name, ids, g, buf, row_tile):
    nl, _, hr, cols = buf.shape
    n_rt = hr // row_tile

    def body(ids_ref, g_ref, b_ref, own_ref, tb_ref):
        t = g_ref[...] + b_ref[...]
        tb_ref[...] = t.astype(BF16)

        @pl.when(pl.program_id(2) == ids_ref[1])
        def _():
            own_ref[...] = t

    blk = (None, None, row_tile, cols)
    grid_spec = pltpu.PrefetchScalarGridSpec(
        num_scalar_prefetch=1, grid=(nl, n_rt, N_SHARD),
        in_specs=[pl.BlockSpec(blk, lambda l, j, s, ids_ref: (l, s, ids_ref[0] * n_rt + j, 0)),
                  pl.BlockSpec(blk, lambda l, j, s, ids_ref: (l, s, j, 0))],
        out_specs=[pl.BlockSpec((None, row_tile, cols), lambda l, j, s, ids_ref: (l, j, 0)),
                   pl.BlockSpec(blk, lambda l, j, s, ids_ref: (l, s, j, 0))])
    return pl.pallas_call(
        body, name=name, grid_spec=grid_spec,
        out_shape=[jax.ShapeDtypeStruct((nl, hr, cols), F32), jax.ShapeDtypeStruct(buf.shape, BF16)],
        compiler_params=_cparams(3),
    )(ids, g, buf)


def _rs_sum(ids, layer, own, bufb, reduced, row_tile):
    _, hr, cols = own.shape
    n_rt = hr // row_tile

    def body(ids_ref, own_ref, b_ref, reduced_in, f_ref):
        f_ref[...] = ((own_ref[...] + b_ref[0].astype(F32)) + b_ref[1].astype(F32)) + b_ref[2].astype(F32)

    grid_spec = pltpu.PrefetchScalarGridSpec(
        num_scalar_prefetch=1, grid=(n_rt,),
        in_specs=[pl.BlockSpec((None, row_tile, cols), lambda j, ids_ref: (0, j, 0)),
                  pl.BlockSpec((3, None, row_tile, cols), lambda j, ids_ref: (0, 0, j, 0)),
                  pl.BlockSpec(memory_space=pl.ANY)],
        out_specs=pl.BlockSpec((None, row_tile, cols), lambda j, ids_ref: (layer, ids_ref[0] * n_rt + j, 0)))
    return pl.pallas_call(
        body, name="rs_sum", grid_spec=grid_spec,
        out_shape=jax.ShapeDtypeStruct(reduced.shape, F32),
        input_output_aliases={3: 0},
        compiler_params=_cparams(1),
    )(ids, own, bufb, reduced)


def _rs_exchange(f, layer):
    def body(f_in, o, send_sems, recv_sems):
        x, y, c = _mesh_pos()
        mine = o.at[layer, pl.ds(c * P_HALF, P_HALF), :]
        cp = _remote(mine, mine, send_sems, recv_sems, 0, (x, y, 1 - c))
        cp.start()
        cp.wait_send()
        theirs = o.at[layer, pl.ds((1 - c) * P_HALF, P_HALF), :]
        _remote(theirs, theirs, send_sems, recv_sems, 0, (x, y, 1 - c)).wait_recv()

    return pl.pallas_call(
        body, name="rs_exchange",
        in_specs=[_ANY], out_specs=_ANY,
        out_shape=jax.ShapeDtypeStruct(f.shape, F32),
        scratch_shapes=[pltpu.SemaphoreType.DMA((1,)), pltpu.SemaphoreType.DMA((1,))],
        input_output_aliases={0: 0},
    )(f)


def _small_all_reduce(s):
    n_rows = s.shape[0]
    hr = n_rows // 2
    qr = hr // N_SHARD

    def body(s_ref, o_ref, sibbuf, tbuf, qbuf, fbuf, send_sems, recv_sems):
        x, y, c = _mesh_pos()
        k = 2 * x + y
        sib = (x, y, 1 - c)
        chips = _other_chips(x, y)
        mine = pl.ds(pl.multiple_of(c * hr, SUBLANES), hr)
        theirs = pl.ds(pl.multiple_of((1 - c) * hr, SUBLANES), hr)

        def quarter(shard):
            return pl.ds(pl.multiple_of(shard * qr, SUBLANES), qr)

        first = _remote(s_ref.at[theirs], sibbuf, send_sems, recv_sems, 0, sib)
        first.start()
        first.wait()
        tbuf[...] = s_ref[mine, :] + sibbuf[...]
        cps = []
        for j, (px, py) in enumerate(chips):
            cp = _remote(tbuf.at[quarter(2 * px + py)], qbuf.at[j], send_sems, recv_sems, 1 + j, (px, py, c))
            cp.start()
            cps.append(cp)
        for cp in cps:
            cp.wait()
        fbuf[quarter(k), :] = (tbuf[quarter(k), :] + qbuf[1]) + (qbuf[0] + qbuf[2])
        cps = []
        for j, (px, py) in enumerate(chips):
            cp = _remote(fbuf.at[quarter(k)], fbuf.at[quarter(k)], send_sems, recv_sems, 4 + j, (px, py, c))
            cp.start()
            cps.append(cp)
        for j, (px, py) in enumerate(chips):
            got = fbuf.at[quarter(2 * px + py)]
            _remote(got, got, send_sems, recv_sems, 4 + j, (px, py, c)).wait_recv()
        for cp in cps:
            cp.wait_send()
        o_ref[mine, :] = fbuf[...]
        last = _remote(fbuf, o_ref.at[mine], send_sems, recv_sems, 7, sib)
        last.start()
        last.wait()

    vmem = pl.BlockSpec(memory_space=pltpu.VMEM)
    return pl.pallas_call(
        body, name="small_all_reduce",
        in_specs=[vmem], out_specs=vmem,
        out_shape=jax.ShapeDtypeStruct(s.shape, F32),
        scratch_shapes=[pltpu.VMEM((hr, D_MODEL), F32), pltpu.VMEM((hr, D_MODEL), F32),
                        pltpu.VMEM((3, qr, D_MODEL), F32), pltpu.VMEM((hr, D_MODEL), F32),
                        pltpu.SemaphoreType.DMA((8,)), pltpu.SemaphoreType.DMA((8,))],
        compiler_params=pltpu.CompilerParams(vmem_limit_bytes=VMEM_LIMIT),
    )(s)


_SMALL = ("norm_mix", "w_pool", "pool_scale", "lam_re", "lam_im", "log_dt", "b_re", "b_im", "c_re", "c_im",
          "d_skip", "b_glu", "norm_ffn", "norm_final")
_WEIGHTS = ("norm_mix", "w_in", "w_pool", "pool_scale", "lam_re", "lam_im", "log_dt", "b_re", "b_im", "c_re",
            "c_im", "d_skip", "w_glu", "b_glu", "w_out", "norm_ffn", "w_gate", "w_up", "w_down", "norm_final")


def _local_step(x, target, p, get_weights, ffn_bwd_done, put_grads):
    nl = p["norm_mix"].shape[0]

    def tied(a, token):
        return a if token is None else a + token
    n_rows = nl * N_SSM_GROUPS
    lr = p["lam_re"].reshape(n_rows, 1, SSM_STATE)
    li = p["lam_im"].reshape(n_rows, 1, SSM_STATE)
    ldt = p["log_dt"].reshape(n_rows, 1, 1)
    br_t = p["b_re"].reshape(n_rows, SSM_STATE, SSM_GROUP).transpose(0, 2, 1)
    bi_t = p["b_im"].reshape(n_rows, SSM_STATE, SSM_GROUP).transpose(0, 2, 1)
    ar, ai, bbr_t, bbi_t = _disc_fwd(lr, li, ldt, br_t, bi_t)
    ar = ar.reshape(nl, 1, N_STATE)
    ai = ai.reshape(nl, 1, N_STATE)
    bbr = bbr_t.transpose(0, 2, 1).reshape(nl, N_SSM_GROUPS, SSM_STATE, SSM_GROUP)
    bbi = bbi_t.transpose(0, 2, 1).reshape(nl, N_SSM_GROUPS, SSM_STATE, SSM_GROUP)
    w_pool = p["w_pool"].astype(BF16)
    p = dict(p)
    for n in ("norm_mix", "pool_scale", "b_glu", "norm_ffn"):
        p[n] = p[n].reshape(nl, 1, -1)
    swap = lambda a: jnp.swapaxes(a, -1, -2)
    bpad = jax.vmap(_pad_pairs)(bbr, bbi).astype(BF16)
    cpad_t = jax.vmap(_pad_pairs)(swap(p["c_re"]), -swap(p["c_im"])).astype(BF16)
    bpad_t, cpad = swap(bpad), swap(cpad_t)
    dskip = p["d_skip"].reshape(nl, 1, D_SSM)

    layers = []
    h = x
    for l in range(nl):
        wp = get_weights(l, [h] if l else [h, bpad, cpad, bpad_t, cpad_t, ar, ai])
        u, ypool = _mix_in_fwd(h, p["norm_mix"], wp, l, w_pool, p["pool_scale"])
        sre, sim, yraw = _ssm_fwd(u, l, bpad, cpad, ar, ai, dskip)
        hm = _mix_out_fwd(yraw, ypool, h, wp, l, p["b_glu"])
        h_next, n2, act_s, fgate_s, fup_s = _ffn_fwd(hm, p["norm_ffn"], wp, l)
        layers.append(dict(h=h, u=u, ypool=ypool, sre=sre, sim=sim, yraw=yraw, hm=hm, n2=n2, act_s=act_s, wp=wp,
                           fgate_s=fgate_s, fup_s=fup_s))
        h = h_next

    dh, loss, d_norm_final = _final_fwd_bwd(h, p["norm_final"].reshape(1, D_MODEL), target)

    raw = {n: [None] * nl for n in ("dg1", "dwp", "dsc", "dcp", "dbp", "ddsk", "db_glu", "dg2", "dar", "dai")}
    token = None
    for l in reversed(range(nl)):
        s = layers[l]
        wp = s["wp"]
        g1 = lax.empty((1, N_SHARD, P_ROWS, D_MODEL), F32)
        dhm, dg2, dgate_s, dup_s, dhb = _ffn_bwd_act(dh, s["hm"], tied(p["norm_ffn"], token), s["fgate_s"],
                                                      s["fup_s"], wp, l)
        g1 = _ffn_bwd_w(s["n2"], dgate_s, dup_s, s["act_s"], dhb, g1)
        token = ffn_bwd_done(l, [g1])
        dyraw, dyp, db_glu, g1 = _mix_out_bwd(dhm, s["yraw"], s["ypool"], wp, l, tied(p["b_glu"], token), g1)
        dus, dcp, dbp, dar, dai, ddsk = _ssm_bwd(dyraw, s["u"], s["sre"], s["sim"], l, cpad_t, bpad_t, ar, ai, dskip)
        dup, dwp, dsc = _pool_bwd(dyp, s["u"], l, w_pool, p["pool_scale"])
        dh, dg1, g1 = _mix_in_bwd(dup, dus, s["h"], dhm, p["norm_mix"], wp, l, g1)
        token = put_grads(l, g1)
        for n, a in (("dg1", dg1), ("dwp", dwp), ("dsc", dsc), ("dcp", dcp), ("dbp", dbp), ("ddsk", ddsk),
                     ("db_glu", db_glu), ("dg2", dg2), ("dar", dar), ("dai", dai)):
            raw[n][l] = a

    st = {n: jnp.stack(v) for n, v in raw.items()}
    dc_re, dc_im = jax.vmap(_unpad_pairs)(swap(st["dcp"]))
    dbbr, dbbi = jax.vmap(_unpad_pairs)(st["dbp"])
    rows = lambda a: a.reshape((n_rows,) + a.shape[2:])
    dlr, dli, dldt, dbr_t, dbi_t = _disc_bwd(lr, li, ldt, br_t, bi_t, st["dar"].reshape(n_rows, 1, SSM_STATE),
                                              st["dai"].reshape(n_rows, 1, SSM_STATE), rows(swap(dbbr)),
                                              rows(swap(dbbi)))
    small = {"norm_mix": st["dg1"][:, 0], "w_pool": st["dwp"], "pool_scale": st["dsc"][:, 0], "c_re": swap(dc_re),
             "c_im": -swap(dc_im), "d_skip": st["ddsk"].reshape(nl, N_SSM_GROUPS, SSM_GROUP),
             "b_glu": st["db_glu"][:, 0], "norm_ffn": st["dg2"][:, 0]}
    small["lam_re"] = dlr.reshape(nl, N_SSM_GROUPS, SSM_STATE)
    small["lam_im"] = dli.reshape(nl, N_SSM_GROUPS, SSM_STATE)
    small["log_dt"] = dldt.reshape(nl, N_SSM_GROUPS)
    small["b_re"] = dbr_t.reshape(nl, N_SSM_GROUPS, SSM_GROUP, SSM_STATE)
    small["b_im"] = dbi_t.reshape(nl, N_SSM_GROUPS, SSM_GROUP, SSM_STATE)
    small["d_skip"] = small["d_skip"].transpose(_SMALL_VIEW["d_skip"])
    small["norm_final"] = d_norm_final
    return loss, dh, small


_SMALL_VIEW = {"b_re": (0, 1, 3, 2), "b_im": (0, 1, 3, 2), "d_skip": (0, 2, 1)}
_SMALL_GROUPS = (("b_re", "b_im"), ("c_re", "c_im"), ("lam_re", "lam_im"), ("norm_mix", "norm_ffn"),
                 ("pool_scale", "b_glu"), ("w_pool",), ("log_dt",), ("d_skip",), ("norm_final",))


def _view(n, a):
    a = a.transpose(_SMALL_VIEW[n]) if n in _SMALL_VIEW else a
    return a[None] if a.ndim == 1 else a


def _unview(n, a, shape):
    a = a.reshape(shape) if len(shape) == 1 else a
    return a.transpose(_SMALL_VIEW[n]) if n in _SMALL_VIEW else a


def _flatten_small(views):
    flat = jnp.concatenate([views[n].reshape(-1) for n in _SMALL])
    n_rows = -(-flat.shape[0] // (64 * D_MODEL)) * 64
    return jnp.pad(flat, (0, n_rows * D_MODEL - flat.shape[0])).reshape(n_rows, D_MODEL)


def _split_small(flat, like):
    flat = flat.reshape(-1)
    out, at = {}, 0
    for n in _SMALL:
        size = like[n].size
        out[n] = flat[at:at + size].reshape(like[n].shape)
        at += size
    return out


def _adamw_small(name, ws, ms, vs, gs):
    k = len(ws)

    def body(*refs):
        ins, outs = refs[:4 * k], refs[4 * k:]
        for i in range(k):
            w, m, v, g = (ins[j * k + i][...] for j in range(4))
            delta, mn, vn = _adamw_math(w, g, m, v)
            outs[i][...] = delta
            outs[k + i][...] = mn
            outs[2 * k + i][...] = vn

    shapes = [jax.ShapeDtypeStruct(w.shape, F32) for w in ws] * 3
    outs = pl.pallas_call(body, name=name, out_shape=shapes,
                          compiler_params=pltpu.CompilerParams(vmem_limit_bytes=VMEM_LIMIT))(*ws, *ms, *vs, *gs)
    return outs[:k], outs[k:2 * k], outs[2 * k:]


def kernel(x, norm_mix, w_in, w_pool, pool_scale, lam_re, lam_im, log_dt, b_re, b_im, c_re, c_im, d_skip, w_glu, b_glu, w_out, norm_ffn, w_gate, w_up, w_down, norm_final, loss_target, m_norm_mix, m_w_in, m_w_pool, m_pool_scale, m_lam_re, m_lam_im, m_log_dt, m_b_re, m_b_im, m_c_re, m_c_im, m_d_skip, m_w_glu, m_b_glu, m_w_out, m_norm_ffn, m_w_gate, m_w_up, m_w_down, m_norm_final, v_norm_mix, v_w_in, v_w_pool, v_pool_scale, v_lam_re, v_lam_im, v_log_dt, v_b_re, v_b_im, v_c_re, v_c_im, v_d_skip, v_w_glu, v_b_glu, v_w_out, v_norm_ffn, v_w_gate, v_w_up, v_w_down, v_norm_final):
    given = dict(locals())
    w = {n: given[n] for n in _WEIGHTS}
    m = {n: given["m_" + n] for n in _WEIGHTS}
    v = {n: given["v_" + n] for n in _WEIGHTS}
    ids = jnp.stack([lax.axis_index("c"), 2 * lax.axis_index("x") + lax.axis_index("y")]).astype(jnp.int32)

    t_names = ("w_gate", "w_up")
    tr = lambda a: a.transpose(0, 2, 1)
    for d in (w, m, v):
        d.update({n: tr(d[n]) for n in t_names})

    nl = norm_mix.shape[0]
    packed = [_pack_weights(ids, l, w["w_in"], w["w_glu"], w["w_out"], w["w_down"], w["w_gate"], w["w_up"])
              for l in range(nl)]
    started, last = {}, ids
    for l in range(nl):
        started[l] = _ag_start(f"ag_start_{l}", packed[l], last)
        last = started[l][3]
    views = [{n: _view(n, d[n]) for n in _SMALL} for d in (w, m, v)]

    def get_weights(l, after):
        send_sems, recv_sems, buf, _ = started[l]
        after = after + ([last] if l == 0 else [])
        return _ag_forward(_ag_wait(f"ag_wait_{l}", send_sems, recv_sems, buf, after))

    to_sibling, to_chips, reduced = {}, {}, {}

    def put_grads(l, g):
        to_sibling[l] = _rs_sibling_start(f"rs_sibling_start_{l}", g)
        token = to_sibling[l][4]
        if l + 1 in to_chips:
            finish(l + 1, [token])
        return token[:1, :1]

    def ffn_bwd_done(l, after):
        return send_to_chips(l + 1, after) if l + 1 in to_sibling else None

    def send_to_chips(l, after):
        send_sems, recv_sems, g, land, _ = to_sibling.pop(l)
        g, land = _rs_sibling_wait(f"rs_sibling_wait_{l}", send_sems, recv_sems, g, land, after)
        own, t = _rs_add("rs_add", ids, g, land, RS_ROW_TILE)
        send_sems, recv_sems, t, land, token = _rs_chips_start(f"rs_chips_start_{l}", t)
        to_chips[l] = (send_sems, recv_sems, t, land, own)
        return token[:1, :1]

    def finish(l, after):
        send_sems, recv_sems, t, land, own = to_chips.pop(l)
        land = _rs_chips_wait(f"rs_chips_wait_{l}", send_sems, recv_sems, t, land, after)
        shard = lax.empty((1, P_ROWS, D_MODEL), F32)
        reduced[l] = _rs_exchange(_rs_sum(ids, 0, own, land, shard, RS_ROW_TILE), 0)

    loss, grad_x, small = _local_step(x[0], loss_target[0], {n: w[n] for n in _SMALL}, get_weights, ffn_bwd_done,
                                      put_grads)
    loss = lax.psum(loss[0, 0], ("x", "y", "c"))
    token = send_to_chips(0, [small["norm_final"]])

    big = (("w_in", P_IN_BLK, 256, False), ("w_out", P_OUT_BLK, 256, False), ("w_down", P_WD_BLK, 352, False),
           ("w_gate", P_WG_BLK, 352, False), ("w_up", P_WU_BLK, 352, False), ("w_glu", P_GLU_BLK, 128, True))
    res = {n: None for n, *_ in big}

    def adamw_layer(l, after):
        for n, (blk, idx), row_tile, glu in big:
            res[n] = _adamw("adamw_" + n, l, w[n], m[n], v[n], reduced[l], (blk, D_MODEL), blk * idx, row_tile,
                            res[n], after, glu)

    for l in reversed(range(1, nl)):
        adamw_layer(l, [token])
    small["norm_final"] = small["norm_final"] + token[:1, :1]
    small_sum = _small_all_reduce(_flatten_small(small))
    finish(0, [small_sum] + [r[0] for r in res.values() if r is not None])
    adamw_layer(0, [])
    for n in t_names:
        res[n] = tuple(tr(a) for a in res[n])
    g_views = _split_small(small_sum, views[0])
    for group in _SMALL_GROUPS:
        deltas, new_ms, new_vs = _adamw_small("adamw_" + group[0], *[[d[n] for n in group] for d in views],
                                              [g_views[n] for n in group])
        for i, n in enumerate(group):
            res[n] = tuple(_unview(n, a, w[n].shape) for a in (g_views[n], deltas[i], new_ms[i], new_vs[i]))

    return (loss, grad_x[None], *[res[n][0] for n in _WEIGHTS], *[res[n][1] for n in _WEIGHTS],
            *[res[n][2] for n in _WEIGHTS], *[res[n][3] for n in _WEIGHTS])
```

```python
import functools
import math

import jax
import jax.numpy as jnp
from jax import lax
from jax.experimental import pallas as pl
from jax.experimental.pallas import tpu as pltpu

F32 = jnp.float32
BF16 = jnp.bfloat16

D_MODEL = 1024
D_POOL = 512
D_SSM = 512
POOL_WINDOWS = (2, 4, 8, 16)
POOL_GROUP = 128
POOL_HALO = 16
N_SSM_GROUPS = 32
SSM_GROUP = 16
SSM_STATE = 64
N_STATE = N_SSM_GROUPS * SSM_STATE
N_PAIRS = N_SSM_GROUPS // 2
D_FF = 2816
N_SHARD = 4
FF_SHARD = D_FF // N_SHARD
RMS_EPS = 1e-6

ADAM_LR = 0.001
ADAM_B1 = 0.9
ADAM_B2 = 0.999
ADAM_EPS = 1e-08
ADAM_WD = 0.01
ADAM_STEP = 10

P_ROWS = 2816
P_WD_BLK = (704, 0)
P_WG_BLK = (704, 1)
P_WU_BLK = (704, 2)
P_FF_ROWS = 2112
P_GLU_BLK = (64, 33)
P_GLU_PAD = 192
P_IN_BLK = (256, 9)
P_OUT_BLK = (256, 10)

SUBLANES = 8
VMEM_LIMIT = 56 * 1024 * 1024

TM = 512
TM_FFN = 512
TM_FFN_FWD = 1024
FFN_SPLIT = 2
TS = 1024
SCAN_LANES = 512


def _cparams(n_axes):
    return pltpu.CompilerParams(dimension_semantics=("arbitrary",) * n_axes, vmem_limit_bytes=VMEM_LIMIT)


def _dot(a, b):
    return jnp.dot(a, b, preferred_element_type=F32)


def _dot_nt(a, b):
    return lax.dot_general(a, b, (((1,), (1,)), ((), ())), preferred_element_type=F32)


def _dot_tn(a, b):
    return lax.dot_general(a, b, (((0,), (0,)), ((), ())), preferred_element_type=F32)


def _rms_hat(x):
    r = lax.rsqrt(jnp.mean(x * x, axis=-1, keepdims=True) + RMS_EPS)
    return x * r, r


def _rms_bwd(d_hat, xhat, r):
    return r * (d_hat - xhat * jnp.mean(d_hat * xhat, axis=-1, keepdims=True))


def _sigmoid(x):
    return 1.0 / (1.0 + jnp.exp(-x))


_GELU_C = math.sqrt(2.0 / math.pi)
_GELU_K = 0.044715


def _gelu(x):
    return 0.5 * x * (1.0 + jnp.tanh(_GELU_C * (x + _GELU_K * x * x * x)))


def _gelu_grad(x):
    th = jnp.tanh(_GELU_C * (x + _GELU_K * x * x * x))
    return 0.5 * (1.0 + th) + 0.5 * x * (1.0 - th * th) * _GELU_C * (1.0 + 3.0 * _GELU_K * x * x)


def _glu_weight(ref):
    v = ref[...]
    return jnp.concatenate([v[:, :, :D_SSM], v[:, :, D_SSM:]], axis=1).reshape(D_SSM, D_SSM)


def _glu_pack(w):
    v = w.reshape(N_SHARD, 128, D_SSM)
    return jnp.concatenate([v[:, :64, :], v[:, 64:, :]], axis=2)


def _pool_diff(ext, row0, tm):
    rows = row0 + lax.broadcasted_iota(jnp.int32, (tm, 1), 0)
    outs = []
    for gi, w in enumerate(POOL_WINDOWS):
        e = ext[:, gi * POOL_GROUP:(gi + 1) * POOL_GROUP]
        s = e
        k = 1
        while k < w:
            s = s + pltpu.roll(s, k, 0)
            k *= 2
        inv = 1.0 / jnp.minimum(rows + 1, w).astype(F32)
        outs.append(s[POOL_HALO:, :] * inv - e[POOL_HALO:, :])
    return outs


def _mix_in_fwd(h, g1, wp, layer, w_pool, scale):
    L = h.shape[0]
    tm = min(TM, L)

    def body(h_ref, g_ref, w_ref, wp_ref, sc_ref, u_ref, yp_ref, carry):
        i = pl.program_id(0)

        @pl.when(i == 0)
        def _():
            carry[...] = jnp.zeros_like(carry)

        xhat, _ = _rms_hat(h_ref[...])
        n1 = (xhat * g_ref[...]).astype(BF16)
        u = _dot(n1, w_ref[...].reshape(D_MODEL, D_MODEL))
        u_ref[...] = u
        up = u[:, :D_POOL]
        ext = jnp.concatenate([carry[...], up], axis=0)
        carry[...] = up[tm - POOL_HALO:, :]
        diffs = _pool_diff(ext, i * tm, tm)
        for gi in range(4):
            cols = slice(gi * POOL_GROUP, (gi + 1) * POOL_GROUP)
            yp_ref[:, cols] = _dot(diffs[gi].astype(BF16), wp_ref[gi]) * sc_ref[:, cols]

    blk, idx = P_IN_BLK
    return pl.pallas_call(
        body, name="mix_in_fwd", grid=(L // tm,),
        in_specs=[pl.BlockSpec((tm, D_MODEL), lambda i: (i, 0)),
                  pl.BlockSpec((None, 1, D_MODEL), lambda i: (layer, 0, 0)),
                  pl.BlockSpec((N_SHARD, None, blk, D_MODEL), lambda i: (0, 0, idx, 0)),
                  pl.BlockSpec((None, 4, POOL_GROUP, POOL_GROUP), lambda i: (layer, 0, 0, 0)),
                  pl.BlockSpec((None, 1, D_POOL), lambda i: (layer, 0, 0))],
        out_specs=[pl.BlockSpec((tm, D_MODEL), lambda i: (i, 0)),
                   pl.BlockSpec((tm, D_POOL), lambda i: (i, 0))],
        out_shape=[jax.ShapeDtypeStruct((L, D_MODEL), F32), jax.ShapeDtypeStruct((L, D_POOL), F32)],
        scratch_shapes=[pltpu.VMEM((POOL_HALO, D_POOL), F32)],
        compiler_params=_cparams(1),
    )(h, g1, wp, w_pool, scale)


def _cmul(xr, xi, yr, yi):
    return xr * yr - xi * yi, xr * yi + xi * yr


SCAN_BLOCK = 64
N_SCAN_TABLES = 26


def _permute_rows(src, dst, n_rows):
    for b in range(n_rows // SCAN_BLOCK):
        for tau in range(SUBLANES):
            dst[pl.ds(SCAN_BLOCK * b + SUBLANES * tau, SUBLANES), :] = (
                src[pl.ds(SCAN_BLOCK * b + tau, SUBLANES, stride=SUBLANES), :])


def _scan_tables(ar, ai, tab, reverse):
    c = ar.shape[1]
    row = lax.broadcasted_iota(jnp.int32, (SUBLANES, c), 0)
    zero = jnp.zeros((SUBLANES, c), F32)
    full = lambda v: jnp.broadcast_to(v, (SUBLANES, c))
    pw = [(ar, ai)]
    for _ in range(SUBLANES - 1):
        pw.append(_cmul(*pw[-1], ar, ai))
    a8 = pw[-1]
    a16 = _cmul(*a8, *a8)
    a32 = _cmul(*a16, *a16)
    tab[0] = full(ar)
    tab[1] = full(ai)
    for n, (s, (pr, pi)) in enumerate(((1, a8), (2, a16), (4, a32))):
        keep = (row < SUBLANES - s) if reverse else (row >= s)
        tab[2 + 2 * n] = jnp.where(keep, pr, zero)
        tab[3 + 2 * n] = jnp.where(keep, pi, zero)
    cur = a8
    qr, qi = zero, zero
    for n in range(SUBLANES):
        at = (SUBLANES - 1 - n) if reverse else n
        qr = jnp.where(row == at, cur[0], qr)
        qi = jnp.where(row == at, cur[1], qi)
        cur = _cmul(*cur, *a8)
    tab[8] = qr
    tab[9] = qi
    for tau in range(SUBLANES):
        pr, pi = pw[SUBLANES - 1 - tau] if reverse else pw[tau]
        tab[10 + 2 * tau] = full(pr)
        tab[11 + 2 * tau] = full(pi)


def _cmac(xr, xi, ar, ai, yr, yi):
    return xr + ar * yr - ai * yi, xi + ar * yi + ai * yr


def _chain_segments(er, ei, c_r, c_i, tab, cols, reverse):
    tr, ti = er, ei
    for n, s in enumerate((1, 2, 4)):
        shift = SUBLANES - s if reverse else s
        tr, ti = _cmac(tr, ti, tab[2 + 2 * n, :, cols], tab[3 + 2 * n, :, cols],
                       pltpu.roll(tr, shift, 0), pltpu.roll(ti, shift, 0))
    return _cmac(tr, ti, tab[8, :, cols], tab[9, :, cols], c_r, c_i)


def _ssm_fwd(u, layer, bpad, cpad, ar, ai, dskip):
    L = u.shape[0]
    ts = min(TS, L)
    nq = 4
    cq = N_STATE // nq

    def body(u_ref, bp_ref, cp_ref, ar_ref, ai_ref, dsk_ref, sre_ref, sim_ref, y_ref, cr, ci, tab, up, yp):
        t = pl.program_id(1)

        @pl.when(t == 0)
        def _():
            cr[...] = jnp.zeros_like(cr)
            ci[...] = jnp.zeros_like(ci)
            _scan_tables(ar_ref[...], ai_ref[...], tab, reverse=False)

        _permute_rows(u_ref, up, ts)
        uf = up[...]
        ub = uf.astype(BF16)
        for jj in range(4):
            bu = _dot(ub, bp_ref[jj])
            sre_ref[:, jj * 128:(jj + 1) * 128] = bu[:, :128]
            sim_ref[:, jj * 128:(jj + 1) * 128] = bu[:, 128:]

        shp = (SUBLANES, SCAN_LANES)
        first_row = lax.broadcasted_iota(jnp.int32, shp, 0) == 0
        for cc in range(cq // SCAN_LANES):
            cols = slice(cc * SCAN_LANES, (cc + 1) * SCAN_LANES)

            def block(b, carry, cols=cols):
                c_r, c_i = carry
                base = pl.multiple_of(b * SCAN_BLOCK, SCAN_BLOCK)
                rows = lambda tau: pl.ds(base + SUBLANES * tau, SUBLANES)
                a_r, a_i = tab[0, :, cols], tab[1, :, cols]
                ys = [(sre_ref[rows(0), cols], sim_ref[rows(0), cols])]
                for tau in range(1, SUBLANES):
                    ys.append(_cmac(sre_ref[rows(tau), cols], sim_ref[rows(tau), cols], a_r, a_i, *ys[-1]))
                tr, ti = _chain_segments(*ys[-1], c_r, c_i, tab, cols, reverse=False)
                in_r = jnp.where(first_row, c_r, pltpu.roll(tr, 1, 0))
                in_i = jnp.where(first_row, c_i, pltpu.roll(ti, 1, 0))
                for tau in range(SUBLANES):
                    sr, si = _cmac(*ys[tau], tab[10 + 2 * tau, :, cols], tab[11 + 2 * tau, :, cols], in_r, in_i)
                    sre_ref[rows(tau), cols] = sr
                    sim_ref[rows(tau), cols] = si
                return (jnp.broadcast_to(tr[SUBLANES - 1:, :], shp), jnp.broadcast_to(ti[SUBLANES - 1:, :], shp))

            c_r, c_i = lax.fori_loop(0, ts // SCAN_BLOCK, block, (cr[:, cols], ci[:, cols]), unroll=2)
            cr[:, cols] = c_r
            ci[:, cols] = c_i

        acc = dsk_ref[...] * uf
        for jj in range(4):
            cols = slice(jj * 128, (jj + 1) * 128)
            scat = jnp.concatenate([sre_ref[:, cols], sim_ref[:, cols]], axis=1).astype(BF16)
            acc = acc + _dot(scat, cp_ref[jj])
        yp[...] = acc
        _permute_rows(yp, y_ref, ts)

    return pl.pallas_call(
        body, name="ssm_fwd", grid=(nq, L // ts),
        in_specs=[pl.BlockSpec((ts, 128), lambda q, t: (t, 4 + q)),
                  pl.BlockSpec((None, 4, 128, 256), lambda q, t: (layer, q, 0, 0)),
                  pl.BlockSpec((None, 4, 256, 128), lambda q, t: (layer, q, 0, 0)),
                  pl.BlockSpec((None, 1, cq), lambda q, t: (layer, 0, q)),
                  pl.BlockSpec((None, 1, cq), lambda q, t: (layer, 0, q)),
                  pl.BlockSpec((None, 1, 128), lambda q, t: (layer, 0, q))],
        out_specs=[pl.BlockSpec((ts, cq), lambda q, t: (t, q)),
                   pl.BlockSpec((ts, cq), lambda q, t: (t, q)),
                   pl.BlockSpec((ts, 128), lambda q, t: (t, q))],
        out_shape=[jax.ShapeDtypeStruct((L, N_STATE), F32), jax.ShapeDtypeStruct((L, N_STATE), F32),
                   jax.ShapeDtypeStruct((L, D_SSM), F32)],
        scratch_shapes=[pltpu.VMEM((SUBLANES, cq), F32), pltpu.VMEM((SUBLANES, cq), F32),
                        pltpu.VMEM((N_SCAN_TABLES, SUBLANES, cq), F32),
                        pltpu.VMEM((ts, 128), F32), pltpu.VMEM((ts, 128), F32)],
        compiler_params=_cparams(2),
    )(u, bpad, cpad, ar, ai, dskip)


def _mix_out_fwd(yraw, ypool, h, wp, layer, b_glu):
    L = h.shape[0]
    tm = min(TM, L)

    def body(yr_ref, yp_ref, h_ref, wglu_ref, b_ref, wout_ref, o_ref):
        y = _gelu(yr_ref[...])
        z = _dot(y.astype(BF16), _glu_weight(wglu_ref)) + b_ref[...]
        o = y * _sigmoid(z)
        mix = jnp.concatenate([yp_ref[...], o], axis=1).astype(BF16)
        o_ref[...] = h_ref[...] + _dot(mix, wout_ref[...].reshape(D_MODEL, D_MODEL))

    gb, gi = P_GLU_BLK
    ob, oi = P_OUT_BLK
    return pl.pallas_call(
        body, name="mix_out_fwd", grid=(L // tm,),
        in_specs=[pl.BlockSpec((tm, D_SSM), lambda i: (i, 0)),
                  pl.BlockSpec((tm, D_POOL), lambda i: (i, 0)),
                  pl.BlockSpec((tm, D_MODEL), lambda i: (i, 0)),
                  pl.BlockSpec((N_SHARD, None, gb, D_MODEL), lambda i: (0, 0, gi, 0)),
                  pl.BlockSpec((None, 1, D_SSM), lambda i: (layer, 0, 0)),
                  pl.BlockSpec((N_SHARD, None, ob, D_MODEL), lambda i: (0, 0, oi, 0))],
        out_specs=pl.BlockSpec((tm, D_MODEL), lambda i: (i, 0)),
        out_shape=jax.ShapeDtypeStruct((L, D_MODEL), F32),
        compiler_params=_cparams(1),
    )(yraw, ypool, h, wp, b_glu, wp)


def _ffn_weights(ref, k):
    return ref[k, 0:FF_SHARD, :], ref[k, FF_SHARD:2 * FF_SHARD, :], ref[k, 2 * FF_SHARD:P_FF_ROWS, :]


def _ffn_weight_spec():
    return pl.BlockSpec((N_SHARD, None, P_FF_ROWS, D_MODEL), lambda m, k: (0, 0, 0, 0),
                        pipeline_mode=pl.Buffered(1))


def _ffn_fwd(h, g2, wp, layer):
    L = h.shape[0]
    tm = min(TM_FFN_FWD, L)

    def body(h_ref, g_ref, w_ref, o_ref, n2_ref, act_ref, dgate_ref, dup_ref):
        k = pl.program_id(1)

        @pl.when(k == 0)
        def _():
            x = h_ref[...]
            xhat, _ = _rms_hat(x)
            n2_ref[...] = (xhat * g_ref[...]).astype(BF16)
            o_ref[...] = x

        wd, wg_t, wu_t = _ffn_weights(w_ref, k)
        n2 = n2_ref[...]
        gate = _dot_nt(n2, wg_t)
        up = _dot_nt(n2, wu_t)
        sg = _sigmoid(gate)
        silu = gate * sg
        act = (silu * up).astype(BF16)
        act_ref[...] = act
        dgate_ref[...] = (up * (sg * (1.0 + gate * (1.0 - sg)))).astype(BF16)
        dup_ref[...] = silu.astype(BF16)
        o_ref[...] += _dot(act, wd)

    act_shape = jax.ShapeDtypeStruct((N_SHARD, L, FF_SHARD), BF16)
    return pl.pallas_call(
        body, name="ffn_fwd", grid=(L // tm, N_SHARD),
        in_specs=[pl.BlockSpec((tm, D_MODEL), lambda m, k: (m, 0)),
                  pl.BlockSpec((None, 1, D_MODEL), lambda m, k: (layer, 0, 0)),
                  _ffn_weight_spec()],
        out_specs=[pl.BlockSpec((tm, D_MODEL), lambda m, k: (m, 0)),
                   pl.BlockSpec((tm, D_MODEL), lambda m, k: (m, 0)),
                   pl.BlockSpec((None, tm, FF_SHARD), lambda m, k: (k, m, 0)),
                   pl.BlockSpec((None, tm, FF_SHARD), lambda m, k: (k, m, 0)),
                   pl.BlockSpec((None, tm, FF_SHARD), lambda m, k: (k, m, 0))],
        out_shape=[jax.ShapeDtypeStruct((L, D_MODEL), F32), jax.ShapeDtypeStruct((L, D_MODEL), BF16),
                   act_shape, act_shape, act_shape],
        compiler_params=_cparams(2),
    )(h, g2, wp)


def _final_fwd_bwd(h, gf, target):
    L = h.shape[0]
    tm = min(TM, L)

    def body(h_ref, g_ref, t_ref, dh_ref, loss_ref, dg_ref):
        i = pl.program_id(0)

        @pl.when(i == 0)
        def _():
            loss_ref[...] = jnp.zeros_like(loss_ref)
            dg_ref[...] = jnp.zeros_like(dg_ref)

        xhat, r = _rms_hat(h_ref[...])
        g = g_ref[...]
        e = xhat * g - t_ref[...]
        loss_ref[...] += 0.5 * jnp.sum(jnp.mean(e * e, axis=-1, keepdims=True), axis=0, keepdims=True)
        dy = e * (1.0 / D_MODEL)
        dg_ref[...] += jnp.sum(dy * xhat, axis=0, keepdims=True)
        dh_ref[...] = _rms_bwd(dy * g, xhat, r)

    return pl.pallas_call(
        body, name="final_fwd_bwd", grid=(L // tm,),
        in_specs=[pl.BlockSpec((tm, D_MODEL), lambda i: (i, 0)),
                  pl.BlockSpec((1, D_MODEL), lambda i: (0, 0)),
                  pl.BlockSpec((tm, D_MODEL), lambda i: (i, 0))],
        out_specs=[pl.BlockSpec((tm, D_MODEL), lambda i: (i, 0)),
                   pl.BlockSpec((1, 1), lambda i: (0, 0)),
                   pl.BlockSpec((1, D_MODEL), lambda i: (0, 0))],
        out_shape=[jax.ShapeDtypeStruct((L, D_MODEL), F32), jax.ShapeDtypeStruct((1, 1), F32),
                   jax.ShapeDtypeStruct((1, D_MODEL), F32)],
        compiler_params=_cparams(1),
    )(h, gf, target)


def _ffn_bwd_act(dh, h, g2, fgate_s, fup_s, wp, layer):
    L = h.shape[0]
    tm = min(TM_FFN, L)
    sub = tm // FFN_SPLIT

    def body(dh_ref, h_ref, g_ref, fgate_ref, fup_ref, w_ref,
             dhm_ref, dg_ref, dgate_ref, dup_ref, dhb_ref, dn2):
        m, k = pl.program_id(0), pl.program_id(1)

        @pl.when(jnp.logical_and(m == 0, k == 0))
        def _():
            dg_ref[...] = jnp.zeros_like(dg_ref)

        @pl.when(k == 0)
        def _():
            dhb_ref[...] = dh_ref[...].astype(BF16)
            dn2[...] = jnp.zeros_like(dn2)

        wd, wg_t, wu_t = _ffn_weights(w_ref, k)
        for rows in (slice(r * sub, (r + 1) * sub) for r in range(tm // sub)):
            dact = _dot_nt(dhb_ref[rows, :], wd)
            dgate = (dact * fgate_ref[rows, :].astype(F32)).astype(BF16)
            dup = (dact * fup_ref[rows, :].astype(F32)).astype(BF16)
            dgate_ref[rows, :] = dgate
            dup_ref[rows, :] = dup
            dn2[rows, :] += _dot(dgate, wg_t) + _dot(dup, wu_t)

        @pl.when(k == N_SHARD - 1)
        def _():
            xhat, r = _rms_hat(h_ref[...])
            d = dn2[...]
            dg_ref[...] += jnp.sum(d * xhat, axis=0, keepdims=True)
            dhm_ref[...] = dh_ref[...] + _rms_bwd(d * g_ref[...], xhat, r)

    act_spec = pl.BlockSpec((None, tm, FF_SHARD), lambda m, k: (k, m, 0))
    act_shape = jax.ShapeDtypeStruct((N_SHARD, L, FF_SHARD), BF16)
    row_spec = pl.BlockSpec((tm, D_MODEL), lambda m, k: (m, 0))
    return pl.pallas_call(
        body, name="ffn_bwd_act", grid=(L // tm, N_SHARD),
        in_specs=[row_spec, row_spec,
                  pl.BlockSpec((None, 1, D_MODEL), lambda m, k: (layer, 0, 0)),
                  act_spec, act_spec,
                  _ffn_weight_spec()],
        out_specs=[row_spec,
                   pl.BlockSpec((1, D_MODEL), lambda m, k: (0, 0)),
                   act_spec, act_spec, row_spec],
        out_shape=[jax.ShapeDtypeStruct((L, D_MODEL), F32), jax.ShapeDtypeStruct((1, D_MODEL), F32),
                   act_shape, act_shape, jax.ShapeDtypeStruct((L, D_MODEL), BF16)],
        scratch_shapes=[pltpu.VMEM((tm, D_MODEL), F32)],
        compiler_params=_cparams(2),
    )(dh, h, g2, fgate_s, fup_s, wp)


def _ffn_bwd_w(n2, dgate_s, dup_s, act_s, dhb, gbuf):
    L = n2.shape[0]
    tm = min(TM_FFN, L)

    def body(n2_ref, dgate_ref, dup_ref, act_ref, dhb_ref, g_in, g_ref):
        m = pl.program_id(1)

        @pl.when(m == 0)
        def _():
            g_ref[...] = jnp.zeros_like(g_ref)

        n2v = n2_ref[...]
        g_ref[0:FF_SHARD, :] += _dot_tn(act_ref[...], dhb_ref[...])
        g_ref[FF_SHARD:2 * FF_SHARD, :] += _dot_tn(dgate_ref[...], n2v)
        g_ref[2 * FF_SHARD:P_FF_ROWS, :] += _dot_tn(dup_ref[...], n2v)

    act_spec = pl.BlockSpec((None, tm, FF_SHARD), lambda k, m: (k, m, 0))
    row_spec = pl.BlockSpec((tm, D_MODEL), lambda k, m: (m, 0))
    return pl.pallas_call(
        body, name="ffn_bwd_w", grid=(N_SHARD, L // tm),
        in_specs=[row_spec, act_spec, act_spec, act_spec, row_spec, pl.BlockSpec(memory_space=pl.ANY)],
        out_specs=pl.BlockSpec((None, None, P_FF_ROWS, D_MODEL), lambda k, m: (0, k, 0, 0)),
        out_shape=jax.ShapeDtypeStruct(gbuf.shape, F32),
        input_output_aliases={5: 0},
        compiler_params=_cparams(2),
    )(n2, dgate_s, dup_s, act_s, dhb, gbuf)


def _mix_out_bwd(dhm, yraw, ypool, wp, layer, b_glu, gbuf):
    L = dhm.shape[0]
    tm = min(TM, L)

    def body(dhm_ref, yr_ref, yp_ref, wglu_ref, b_ref, wout_ref, g1_in,
             dyr_ref, dyp_ref, db_ref, g1_ref, dwout, dwglu, gpack):
        i = pl.program_id(0)

        @pl.when(i == 0)
        def _():
            db_ref[...] = jnp.zeros_like(db_ref)
            dwout[...] = jnp.zeros_like(dwout)
            dwglu[...] = jnp.zeros_like(dwglu)

        dhb = dhm_ref[...].astype(BF16)
        wglu = _glu_weight(wglu_ref)
        dmix = _dot_nt(dhb, wout_ref[...].reshape(D_MODEL, D_MODEL))
        dyp_ref[...] = dmix[:, :D_POOL]
        d_o = dmix[:, D_POOL:]
        yraw_v = yr_ref[...]
        y = _gelu(yraw_v)
        yb = y.astype(BF16)
        sig = _sigmoid(_dot(yb, wglu) + b_ref[...])
        mix = jnp.concatenate([yp_ref[...], y * sig], axis=1).astype(BF16)
        dwout[...] += _dot_tn(mix, dhb).reshape(N_SHARD, 256, D_MODEL)
        dz = d_o * y * sig * (1.0 - sig)
        dzb = dz.astype(BF16)
        db_ref[...] += jnp.sum(dz, axis=0, keepdims=True)
        dwglu[...] += _dot_tn(yb, dzb)
        dy = d_o * sig + _dot_nt(dzb, wglu)
        dyr_ref[...] = dy * _gelu_grad(yraw_v)

        @pl.when(i == n_steps - 1)
        def _():
            gpack[:, :gb, :] = _glu_pack(dwglu[...])
            gpack[:, gb:, :] = jnp.zeros((N_SHARD, P_GLU_PAD - gb, D_MODEL), F32)
            pltpu.sync_copy(gpack, g1_ref.at[0, :, pl.ds(gb * gi, P_GLU_PAD), :])
            pltpu.sync_copy(dwout, g1_ref.at[0, :, pl.ds(ob * oi, ob), :])

    gb, gi = P_GLU_BLK
    ob, oi = P_OUT_BLK
    n_steps = L // tm
    return pl.pallas_call(
        body, name="mix_out_bwd", grid=(n_steps,),
        in_specs=[pl.BlockSpec((tm, D_MODEL), lambda i: (i, 0)),
                  pl.BlockSpec((tm, D_SSM), lambda i: (i, 0)),
                  pl.BlockSpec((tm, D_POOL), lambda i: (i, 0)),
                  pl.BlockSpec((N_SHARD, None, gb, D_MODEL), lambda i: (0, 0, gi, 0)),
                  pl.BlockSpec((None, 1, D_SSM), lambda i: (layer, 0, 0)),
                  pl.BlockSpec((N_SHARD, None, ob, D_MODEL), lambda i: (0, 0, oi, 0)),
                  pl.BlockSpec(memory_space=pl.ANY)],
        out_specs=[pl.BlockSpec((tm, D_SSM), lambda i: (i, 0)),
                   pl.BlockSpec((tm, D_POOL), lambda i: (i, 0)),
                   pl.BlockSpec((1, D_SSM), lambda i: (0, 0)),
                   pl.BlockSpec(memory_space=pl.ANY)],
        out_shape=[jax.ShapeDtypeStruct((L, D_SSM), F32), jax.ShapeDtypeStruct((L, D_POOL), F32),
                   jax.ShapeDtypeStruct((1, D_SSM), F32),
                   jax.ShapeDtypeStruct(gbuf.shape, F32)],
        scratch_shapes=[pltpu.VMEM((N_SHARD, ob, D_MODEL), F32), pltpu.VMEM((D_SSM, D_SSM), F32),
                        pltpu.VMEM((N_SHARD, P_GLU_PAD, D_MODEL), F32)],
        input_output_aliases={6: 3},
        compiler_params=_cparams(1),
    )(dhm, yraw, ypool, wp, b_glu, wp, gbuf)


def _ssm_bwd(dyraw, u, sre, sim, layer, cpad_t, bpad_t, ar, ai, dskip):
    L = u.shape[0]
    ts = min(TS, L)
    nt = L // ts
    nq = 4
    cq = N_STATE // nq

    def body(dy_ref, u_ref, sre_ref, sim_ref, ct_ref, bt_ref, ar_ref, ai_ref, dsk_ref,
             du_ref, dcp_ref, dbp_ref, dar_ref, dai_ref, ddsk_ref, gre, gim, cr, ci, tab, accr, acci, up, dyp):
        t = pl.program_id(1)

        @pl.when(t == 0)
        def _():
            for ref in (cr, ci, accr, acci, dcp_ref, dbp_ref, ddsk_ref):
                ref[...] = jnp.zeros_like(ref)
            _scan_tables(ar_ref[...], -ai_ref[...], tab, reverse=True)

        _permute_rows(dy_ref, dyp, ts)
        _permute_rows(u_ref, up, ts)
        dy = dyp[...]
        dyb = dy.astype(BF16)
        uf = up[...]
        ub = uf.astype(BF16)
        for jj in range(4):
            cols = slice(jj * 128, (jj + 1) * 128)
            ds = _dot(dyb, ct_ref[jj])
            gre[:, cols] = ds[:, :128]
            gim[:, cols] = ds[:, 128:]
            scat = jnp.concatenate([sre_ref[:, cols], sim_ref[:, cols]], axis=1).astype(BF16)
            dcp_ref[jj] += _dot_tn(scat, dyb)

        n_blk = ts // SCAN_BLOCK
        shp = (SUBLANES, SCAN_LANES)
        last_row = lax.broadcasted_iota(jnp.int32, shp, 0) == SUBLANES - 1
        for cc in range(cq // SCAN_LANES):
            cols = slice(cc * SCAN_LANES, (cc + 1) * SCAN_LANES)

            def block(i, carry, cols=cols):
                c_r, c_i, a_r, a_i = carry
                base = pl.multiple_of((n_blk - 1 - i) * SCAN_BLOCK, SCAN_BLOCK)
                rows = lambda tau: pl.ds(base + SUBLANES * tau, SUBLANES)
                m_r, m_i = tab[0, :, cols], tab[1, :, cols]
                ys = [None] * SUBLANES
                ys[SUBLANES - 1] = (gre[rows(SUBLANES - 1), cols], gim[rows(SUBLANES - 1), cols])
                for tau in reversed(range(SUBLANES - 1)):
                    ys[tau] = _cmac(gre[rows(tau), cols], gim[rows(tau), cols], m_r, m_i, *ys[tau + 1])
                tr, ti = _chain_segments(*ys[0], c_r, c_i, tab, cols, reverse=True)
                in_r = jnp.where(last_row, c_r, pltpu.roll(tr, SUBLANES - 1, 0))
                in_i = jnp.where(last_row, c_i, pltpu.roll(ti, SUBLANES - 1, 0))
                gs = [_cmac(*ys[tau], tab[10 + 2 * tau, :, cols], tab[11 + 2 * tau, :, cols], in_r, in_i)
                      for tau in range(SUBLANES)]
                for tau in range(SUBLANES):
                    gre[rows(tau), cols] = gs[tau][0]
                    gim[rows(tau), cols] = gs[tau][1]
                    if tau < SUBLANES - 1:
                        nr, ni = gs[tau + 1]
                    else:
                        nr = jnp.where(last_row, c_r, pltpu.roll(gs[0][0], SUBLANES - 1, 0))
                        ni = jnp.where(last_row, c_i, pltpu.roll(gs[0][1], SUBLANES - 1, 0))
                    sr, si = sre_ref[rows(tau), cols], sim_ref[rows(tau), cols]
                    a_r = a_r + sr * nr + si * ni
                    a_i = a_i + sr * ni - si * nr
                return (jnp.broadcast_to(tr[:1, :], shp), jnp.broadcast_to(ti[:1, :], shp), a_r, a_i)

            c_r, c_i, a_r, a_i = lax.fori_loop(
                0, n_blk, block, (cr[:, cols], ci[:, cols], accr[:, cols], acci[:, cols]), unroll=2)
            cr[:, cols] = c_r
            ci[:, cols] = c_i
            accr[:, cols] = a_r
            acci[:, cols] = a_i

        acc = dsk_ref[...] * dy
        for jj in range(4):
            cols = slice(jj * 128, (jj + 1) * 128)
            gcat = jnp.concatenate([gre[:, cols], gim[:, cols]], axis=1).astype(BF16)
            acc = acc + _dot(gcat, bt_ref[jj])
            dbp_ref[jj] += _dot_tn(ub, gcat)
        ddsk_ref[...] += jnp.sum(dy * uf, axis=0, keepdims=True)
        dyp[...] = acc
        _permute_rows(dyp, du_ref, ts)

        @pl.when(t == nt - 1)
        def _():
            dar_ref[...] = jnp.sum(accr[...], axis=0, keepdims=True)
            dai_ref[...] = jnp.sum(acci[...], axis=0, keepdims=True)

    f32_scr = lambda *s: pltpu.VMEM(s, F32)
    return pl.pallas_call(
        body, name="ssm_bwd", grid=(nq, nt),
        in_specs=[pl.BlockSpec((ts, 128), lambda q, t: (nt - 1 - t, q)),
                  pl.BlockSpec((ts, 128), lambda q, t: (nt - 1 - t, 4 + q)),
                  pl.BlockSpec((ts, cq), lambda q, t: (nt - 1 - t, q)),
                  pl.BlockSpec((ts, cq), lambda q, t: (nt - 1 - t, q)),
                  pl.BlockSpec((None, 4, 128, 256), lambda q, t: (layer, q, 0, 0)),
                  pl.BlockSpec((None, 4, 256, 128), lambda q, t: (layer, q, 0, 0)),
                  pl.BlockSpec((None, 1, cq), lambda q, t: (layer, 0, q)),
                  pl.BlockSpec((None, 1, cq), lambda q, t: (layer, 0, q)),
                  pl.BlockSpec((None, 1, 128), lambda q, t: (layer, 0, q))],
        out_specs=[pl.BlockSpec((ts, 128), lambda q, t: (nt - 1 - t, q)),
                   pl.BlockSpec((4, 256, 128), lambda q, t: (q, 0, 0)),
                   pl.BlockSpec((4, 128, 256), lambda q, t: (q, 0, 0)),
                   pl.BlockSpec((1, cq), lambda q, t: (0, q)),
                   pl.BlockSpec((1, cq), lambda q, t: (0, q)),
                   pl.BlockSpec((1, 128), lambda q, t: (0, q))],
        out_shape=[jax.ShapeDtypeStruct((L, D_SSM), F32),
                   jax.ShapeDtypeStruct((N_PAIRS, 256, 128), F32), jax.ShapeDtypeStruct((N_PAIRS, 128, 256), F32),
                   jax.ShapeDtypeStruct((1, N_STATE), F32), jax.ShapeDtypeStruct((1, N_STATE), F32),
                   jax.ShapeDtypeStruct((1, D_SSM), F32)],
        scratch_shapes=[f32_scr(ts, cq), f32_scr(ts, cq), f32_scr(SUBLANES, cq), f32_scr(SUBLANES, cq),
                        f32_scr(N_SCAN_TABLES, SUBLANES, cq), f32_scr(SUBLANES, cq), f32_scr(SUBLANES, cq),
                        f32_scr(ts, 128), f32_scr(ts, 128)],
        compiler_params=_cparams(2),
    )(dyraw, u, sre, sim, cpad_t, bpad_t, ar, ai, dskip)


def _pool_bwd(dyp, u, layer, w_pool, scale):
    L = u.shape[0]
    tm = min(TM, L)
    nt = L // tm
    halo_per_tile = tm // POOL_HALO

    def body(dyp_ref, u_ref, halo_ref, wp_ref, sc_ref, du_ref, dwp_ref, dsc_ref, carry):
        i = pl.program_id(0)
        tile = nt - 1 - i

        @pl.when(i == 0)
        def _():
            carry[...] = jnp.zeros_like(carry)
            dwp_ref[...] = jnp.zeros_like(dwp_ref)
            dsc_ref[...] = jnp.zeros_like(dsc_ref)

        up = u_ref[...]
        halo = jnp.where(tile > 0, halo_ref[...], jnp.zeros_like(halo_ref))
        diffs = _pool_diff(jnp.concatenate([halo, up], axis=0), tile * tm, tm)
        rows = tile * tm + lax.broadcasted_iota(jnp.int32, (tm, 1), 0)
        n_ext = tm + POOL_HALO
        for gi, w in enumerate(POOL_WINDOWS):
            cols = slice(gi * POOL_GROUP, (gi + 1) * POOL_GROUP)
            db = diffs[gi].astype(BF16)
            dyp = dyp_ref[:, cols]
            dsc_ref[:, cols] += jnp.sum(dyp * _dot(db, wp_ref[gi]), axis=0, keepdims=True)
            dp = (dyp * sc_ref[:, cols]).astype(BF16)
            ddiff = _dot_nt(dp, wp_ref[gi])
            dwp_ref[gi] += _dot_tn(db, dp)
            e = ddiff * (1.0 / jnp.minimum(rows + 1, w).astype(F32))
            s = jnp.concatenate([e, carry[:, cols]], axis=0)
            k = 1
            while k < w:
                s = s + pltpu.roll(s, n_ext - k, 0)
                k *= 2
            du_ref[:, cols] = s[:tm, :] - ddiff
            carry[:, cols] = e[:POOL_HALO, :]

    return pl.pallas_call(
        body, name="pool_bwd", grid=(nt,),
        in_specs=[pl.BlockSpec((tm, D_POOL), lambda i: (nt - 1 - i, 0)),
                  pl.BlockSpec((tm, D_POOL), lambda i: (nt - 1 - i, 0)),
                  pl.BlockSpec((POOL_HALO, D_POOL), lambda i: (jnp.maximum((nt - 1 - i) * halo_per_tile - 1, 0), 0)),
                  pl.BlockSpec((None, 4, POOL_GROUP, POOL_GROUP), lambda i: (layer, 0, 0, 0)),
                  pl.BlockSpec((None, 1, D_POOL), lambda i: (layer, 0, 0))],
        out_specs=[pl.BlockSpec((tm, D_POOL), lambda i: (nt - 1 - i, 0)),
                   pl.BlockSpec((4, POOL_GROUP, POOL_GROUP), lambda i: (0, 0, 0)),
                   pl.BlockSpec((1, D_POOL), lambda i: (0, 0))],
        out_shape=[jax.ShapeDtypeStruct((L, D_POOL), F32),
                   jax.ShapeDtypeStruct((4, POOL_GROUP, POOL_GROUP), F32),
                   jax.ShapeDtypeStruct((1, D_POOL), F32)],
        scratch_shapes=[pltpu.VMEM((POOL_HALO, D_POOL), F32)],
        compiler_params=_cparams(1),
    )(dyp, u, u, w_pool, scale)


def _mix_in_bwd(dup, dus, h, dhm, g1, wp, layer, gbuf):
    L = h.shape[0]
    tm = min(TM, L)
    n_steps = L // tm
    blk, idx = P_IN_BLK

    def body(dup_ref, dus_ref, h_ref, dhm_ref, g_ref, w_ref, g1_in, dh_ref, dg_ref, g1_ref, dwin):
        i = pl.program_id(0)

        @pl.when(i == 0)
        def _():
            dg_ref[...] = jnp.zeros_like(dg_ref)
            dwin[...] = jnp.zeros_like(dwin)

        du = jnp.concatenate([dup_ref[...], dus_ref[...]], axis=1).astype(BF16)
        dn1 = _dot_nt(du, w_ref[...].reshape(D_MODEL, D_MODEL))
        xhat, r = _rms_hat(h_ref[...])
        g = g_ref[...]
        n1 = (xhat * g).astype(BF16)
        dwin[...] += _dot_tn(n1, du).reshape(N_SHARD, blk, D_MODEL)
        dg_ref[...] += jnp.sum(dn1 * xhat, axis=0, keepdims=True)
        dh_ref[...] = dhm_ref[...] + _rms_bwd(dn1 * g, xhat, r)

        @pl.when(i == n_steps - 1)
        def _():
            pltpu.sync_copy(dwin, g1_ref.at[0, :, pl.ds(blk * idx, blk), :])

    row_spec = pl.BlockSpec((tm, D_MODEL), lambda i: (i, 0))
    half_spec = pl.BlockSpec((tm, D_POOL), lambda i: (i, 0))
    return pl.pallas_call(
        body, name="mix_in_bwd", grid=(n_steps,),
        in_specs=[half_spec, half_spec, row_spec, row_spec,
                  pl.BlockSpec((None, 1, D_MODEL), lambda i: (layer, 0, 0)),
                  pl.BlockSpec((N_SHARD, None, blk, D_MODEL), lambda i: (0, 0, idx, 0)),
                  pl.BlockSpec(memory_space=pl.ANY)],
        out_specs=[row_spec, pl.BlockSpec((1, D_MODEL), lambda i: (0, 0)), pl.BlockSpec(memory_space=pl.ANY)],
        out_shape=[jax.ShapeDtypeStruct((L, D_MODEL), F32), jax.ShapeDtypeStruct((1, D_MODEL), F32),
                   jax.ShapeDtypeStruct(gbuf.shape, F32)],
        scratch_shapes=[pltpu.VMEM((N_SHARD, blk, D_MODEL), F32)],
        input_output_aliases={6: 2},
        compiler_params=_cparams(1),
    )(dup, dus, h, dhm, g1, wp, gbuf)


def _disc_math(lr, li, ldt, br_t, bi_t):
    dt = jnp.exp(ldt)
    mag = jnp.exp(lr * dt)
    ang = li * dt
    ar = mag * jnp.cos(ang)
    ai = mag * jnp.sin(ang)
    den = lr * lr + li * li
    nr, ni = ar - 1.0, ai
    cr = (nr * lr + ni * li) / den
    ci = (ni * lr - nr * li) / den
    return ar, ai, cr * br_t - ci * bi_t, cr * bi_t + ci * br_t


def _disc_fwd(lr, li, ldt, br_t, bi_t):
    def body(lr_ref, li_ref, ldt_ref, br_ref, bi_ref, ar_ref, ai_ref, bbr_ref, bbi_ref):
        ar, ai, bbr, bbi = _disc_math(lr_ref[...], li_ref[...], ldt_ref[...], br_ref[...], bi_ref[...])
        ar_ref[...] = ar
        ai_ref[...] = ai
        bbr_ref[...] = bbr
        bbi_ref[...] = bbi

    shapes = [jax.ShapeDtypeStruct(a.shape, F32) for a in (lr, li, br_t, bi_t)]
    return pl.pallas_call(body, name="ssm_disc_fwd", out_shape=shapes,
                          compiler_params=pltpu.CompilerParams(vmem_limit_bytes=VMEM_LIMIT))(lr, li, ldt, br_t, bi_t)


def _disc_bwd(lr, li, ldt, br_t, bi_t, dar, dai, dbbr, dbbi):
    def body(lr_ref, li_ref, ldt_ref, br_ref, bi_ref, dar_ref, dai_ref, dbbr_ref, dbbi_ref,
             dlr_ref, dli_ref, dldt_ref, dbr_ref, dbi_ref):
        prim = (lr_ref[...], li_ref[...], ldt_ref[...], br_ref[...], bi_ref[...])
        _, pullback = jax.vjp(_disc_math, *prim)
        dlr, dli, dldt, dbr, dbi = pullback((dar_ref[...], dai_ref[...], dbbr_ref[...], dbbi_ref[...]))
        dlr_ref[...] = dlr
        dli_ref[...] = dli
        dldt_ref[...] = dldt
        dbr_ref[...] = dbr
        dbi_ref[...] = dbi

    shapes = [jax.ShapeDtypeStruct(a.shape, F32) for a in (lr, li, ldt, br_t, bi_t)]
    return pl.pallas_call(body, name="ssm_disc_bwd", out_shape=shapes,
                          compiler_params=pltpu.CompilerParams(vmem_limit_bytes=VMEM_LIMIT))(
        lr, li, ldt, br_t, bi_t, dar, dai, dbbr, dbbi)


def _pad_pairs(m_re, m_im):
    def blocks(m):
        v = m.transpose(0, 2, 1).reshape(N_PAIRS, 2, SSM_GROUP, SSM_STATE)
        return jnp.einsum("ab,jahp->jahbp", jnp.eye(2, dtype=m.dtype), v).reshape(N_PAIRS, 32, 128)
    both = jnp.concatenate([blocks(m_re), blocks(m_im)], axis=-1)
    place = jax.nn.one_hot(jnp.arange(N_PAIRS) % 4, 4, dtype=both.dtype)
    return jnp.einsum("jk,jrc->jkrc", place, both).reshape(N_PAIRS, 128, 256)


def _unpad_pairs(x):
    place = jax.nn.one_hot(jnp.arange(N_PAIRS) % 4, 4, dtype=x.dtype)
    both = jnp.einsum("jk,jkrc->jrc", place, x.reshape(N_PAIRS, 4, 32, 256))

    def unblock(v):
        v = v.reshape(N_PAIRS, 2, SSM_GROUP, 2, SSM_STATE)
        d = jnp.einsum("ab,jahbp->jahp", jnp.eye(2, dtype=x.dtype), v)
        return d.reshape(N_SSM_GROUPS, SSM_GROUP, SSM_STATE).transpose(0, 2, 1)
    return unblock(both[..., :128]), unblock(both[..., 128:])


def _adamw_math(w, g, m, v):
    m = ADAM_B1 * m + (1.0 - ADAM_B1) * g
    v = ADAM_B2 * v + (1.0 - ADAM_B2) * (g * g)
    m_hat = m / (1.0 - ADAM_B1 ** ADAM_STEP)
    v_hat = v / (1.0 - ADAM_B2 ** ADAM_STEP)
    delta = -ADAM_LR * (m_hat / (jnp.sqrt(v_hat) + ADAM_EPS) + ADAM_WD * w)
    return delta, m, v


def _adamw(name, layer, w, m, v, gbuf, g_block, g_row0, row_tile, outs=None, after=(), glu=False):
    nl, r, c = w.shape
    n_tiles = r // row_tile
    g_rows, g_cols = g_block
    g_tile = g_rows // n_tiles
    g_off = g_row0 // g_tile
    if outs is None:
        outs = [lax.empty(w.shape, F32) for _ in range(4)]

    def body(w_ref, m_ref, v_ref, g_ref, *rest):
        go_ref, d_ref, mo_ref, vo_ref = rest[-4:]
        g = g_ref[...]
        if glu:
            g = jnp.concatenate([g[:, :D_SSM], g[:, D_SSM:]], axis=0)
        delta, mn, vn = _adamw_math(w_ref[...], g, m_ref[...], v_ref[...])
        go_ref[...] = g
        d_ref[...] = delta
        mo_ref[...] = mn
        vo_ref[...] = vn

    w_spec = pl.BlockSpec((None, row_tile, c), lambda j: (layer, j, 0))
    shape = jax.ShapeDtypeStruct(w.shape, F32)
    return pl.pallas_call(
        body, name=name, grid=(n_tiles,),
        in_specs=[w_spec, w_spec, w_spec, pl.BlockSpec((None, g_tile, g_cols), lambda j: (0, g_off + j, 0))]
        + [_ANY] * (4 + len(after)),
        out_specs=[w_spec] * 4,
        out_shape=[shape] * 4,
        input_output_aliases={4: 0, 5: 1, 6: 2, 7: 3},
        compiler_params=_cparams(1),
    )(w, m, v, gbuf, *outs, *after)


def _pack_weights(ids, layer, w_in, w_glu, w_out, w_down, w_gate_t, w_up_t):
    gb, gi = P_GLU_BLK
    ib, ii = P_IN_BLK
    ob, oi = P_OUT_BLK

    def body(ids_ref, in_ref, glu_ref, out_ref, dn_ref, gate_ref, up_ref, p_ref):
        p_ref[0:FF_SHARD, :] = dn_ref[...].astype(BF16)
        p_ref[FF_SHARD:2 * FF_SHARD, :] = gate_ref[...].astype(BF16)
        p_ref[2 * FF_SHARD:P_FF_ROWS, :] = up_ref[...].astype(BF16)
        g = glu_ref[...]
        p_ref[gb * gi:gb * (gi + 1), :] = jnp.concatenate([g[:gb, :], g[gb:, :]], axis=1).astype(BF16)
        p_ref[gb * (gi + 1):ib * ii, :] = jnp.zeros((P_GLU_PAD - gb, D_MODEL), BF16)
        p_ref[ib * ii:ib * (ii + 1), :] = in_ref[...].astype(BF16)
        p_ref[ob * oi:ob * (oi + 1), :] = out_ref[...].astype(BF16)

    def spec(a):
        return pl.BlockSpec((None,) + a.shape[1:], lambda i, ids_ref: (layer, 0, 0))

    ins = (w_in, w_glu, w_out, w_down, w_gate_t, w_up_t)
    grid_spec = pltpu.PrefetchScalarGridSpec(
        num_scalar_prefetch=1, grid=(1,),
        in_specs=[spec(a) for a in ins],
        out_specs=pl.BlockSpec((None, None, P_ROWS, D_MODEL), lambda i, ids_ref: (ids_ref[1], 0, 0, 0)))
    return pl.pallas_call(
        body, name="pack_weights", grid_spec=grid_spec,
        out_shape=jax.ShapeDtypeStruct((N_SHARD, 1, P_ROWS, D_MODEL), BF16),
        compiler_params=_cparams(1),
    )(ids, *ins)


MESH = pl.DeviceIdType.MESH
_ANY = pl.BlockSpec(memory_space=pl.ANY)
P_HALF = P_ROWS // 2
RS_ROW_TILE = 352


def _mesh_pos():
    return lax.axis_index("x"), lax.axis_index("y"), lax.axis_index("c")


def _other_chips(x, y):
    return [(1 - x, y), (x, 1 - y), (1 - x, 1 - y)]


def _remote(src, dst, send_sems, recv_sems, n, to):
    return pltpu.make_async_remote_copy(src_ref=src, dst_ref=dst, send_sem=send_sems.at[n],
                                        recv_sem=recv_sems.at[n], device_id=to, device_id_type=MESH)


_HBM = pl.BlockSpec(memory_space=pltpu.HBM)
_SEM = pl.BlockSpec(memory_space=pltpu.SEMAPHORE)
_EFFECT = pltpu.CompilerParams(has_side_effects=pltpu.SideEffectType.DATAFLOW_SIDE_EFFECTING)
_TOKEN = jax.ShapeDtypeStruct((8, 128), F32)


def _in_hbm(a):
    return pltpu.with_memory_space_constraint(a, pltpu.HBM)


def _ag_start(name, wp, after):
    def body(w_ref, after_ref, send_sems, recv_sems, w_thru, token):
        x, y, c = _mesh_pos()
        mine = w_ref.at[2 * x + y, :, pl.ds(c * P_HALF, P_HALF), :]
        for j, (px, py) in enumerate(_other_chips(x, y)):
            _remote(mine, mine, send_sems, recv_sems, j, (px, py, c)).start()
        token[...] = jnp.zeros_like(token)

    return pl.pallas_call(
        body, name=name,
        out_shape=(pltpu.SemaphoreType.DMA((3,)), pltpu.SemaphoreType.DMA((3,)), pltpu.HBM(wp.shape, wp.dtype), _TOKEN),
        in_specs=(_HBM, _ANY), out_specs=(_SEM, _SEM, _HBM, pl.BlockSpec(memory_space=pltpu.VMEM)),
        input_output_aliases={0: 2}, compiler_params=_EFFECT,
    )(_in_hbm(wp), after)


def _ag_wait(name, send_sems, recv_sems, wp, after):
    def body(w_ref, send_sems, recv_sems, *rest):
        x, y, c = _mesh_pos()
        mine = w_ref.at[2 * x + y, :, pl.ds(c * P_HALF, P_HALF), :]
        for j, (px, py) in enumerate(_other_chips(x, y)):
            landed = w_ref.at[2 * px + py, :, pl.ds(c * P_HALF, P_HALF), :]
            cp = _remote(mine, landed, send_sems, recv_sems, j, (px, py, c))
            cp.wait_send()
            cp.wait_recv()

    return pl.pallas_call(
        body, name=name, out_shape=pltpu.HBM(wp.shape, wp.dtype),
        in_specs=(_HBM, _SEM, _SEM) + (_ANY,) * len(after), out_specs=_HBM,
        input_output_aliases={0: 0}, compiler_params=_EFFECT,
    )(wp, send_sems, recv_sems, *after)


def _ag_forward(wp):
    def body(w_in, o, send_sems, recv_sems):
        x, y, c = _mesh_pos()
        sib = (x, y, 1 - c)
        chips = _other_chips(x, y)
        sends = []
        for j, (px, py) in enumerate(chips):
            landed = o.at[2 * px + py, :, pl.ds(c * P_HALF, P_HALF), :]
            cp = _remote(landed, landed, send_sems, recv_sems, j, sib)
            cp.start()
            sends.append(cp)
        for j, (px, py) in enumerate(chips):
            passed = o.at[2 * px + py, :, pl.ds((1 - c) * P_HALF, P_HALF), :]
            _remote(passed, passed, send_sems, recv_sems, j, sib).wait_recv()
        for cp in sends:
            cp.wait_send()

    return pl.pallas_call(
        body, name="ag_forward",
        in_specs=[_ANY], out_specs=_ANY,
        out_shape=jax.ShapeDtypeStruct(wp.shape, wp.dtype),
        scratch_shapes=[pltpu.SemaphoreType.DMA((3,)), pltpu.SemaphoreType.DMA((3,))],
        input_output_aliases={0: 0},
    )(wp)


def _rs_chips_start(name, t):
    nl = t.shape[0]

    def body(t_ref, land_ref, send_sems, recv_sems, t_thru, land_thru, token):
        x, y, c = _mesh_pos()
        for j, (px, py) in enumerate(_other_chips(x, y)):
            _remote(t_ref.at[:, 2 * px + py], land_ref.at[j], send_sems, recv_sems, j, (px, py, c)).start()
        token[...] = jnp.zeros_like(token)

    land = lax.empty((3, nl, P_HALF, D_MODEL), BF16)
    return pl.pallas_call(
        body, name=name,
        out_shape=(pltpu.SemaphoreType.DMA((3,)), pltpu.SemaphoreType.DMA((3,)), pltpu.HBM(t.shape, t.dtype),
                   pltpu.HBM(land.shape, land.dtype), _TOKEN),
        in_specs=(_HBM, _HBM), out_specs=(_SEM, _SEM, _HBM, _HBM, pl.BlockSpec(memory_space=pltpu.VMEM)),
        input_output_aliases={0: 2, 1: 3}, compiler_params=_EFFECT,
    )(_in_hbm(t), _in_hbm(land))


def _rs_chips_wait(name, send_sems, recv_sems, t, land, after):
    def body(t_ref, land_ref, send_sems, recv_sems, *rest):
        x, y, c = _mesh_pos()
        for j, (px, py) in enumerate(_other_chips(x, y)):
            cp = _remote(t_ref.at[:, 2 * px + py], land_ref.at[j], send_sems, recv_sems, j, (px, py, c))
            cp.wait_send()
            cp.wait_recv()

    return pl.pallas_call(
        body, name=name, out_shape=(pltpu.HBM(t.shape, t.dtype), pltpu.HBM(land.shape, land.dtype)),
        in_specs=(_HBM, _HBM, _SEM, _SEM) + (_ANY,) * len(after), out_specs=(_HBM, _HBM),
        input_output_aliases={0: 0, 1: 1}, compiler_params=_EFFECT,
    )(t, land, send_sems, recv_sems, *after)[1]


def _rs_sibling_start(name, g):
    nl = g.shape[0]

    def body(g_ref, land_ref, send_sems, recv_sems, g_thru, land_thru, token):
        x, y, c = _mesh_pos()
        _remote(g_ref.at[:, :, pl.ds((1 - c) * P_HALF, P_HALF), :], land_ref, send_sems, recv_sems, 0,
                (x, y, 1 - c)).start()
        token[...] = jnp.zeros_like(token)

    land = lax.empty((nl, N_SHARD, P_HALF, D_MODEL), F32)
    return pl.pallas_call(
        body, name=name,
        out_shape=(pltpu.SemaphoreType.DMA((1,)), pltpu.SemaphoreType.DMA((1,)), pltpu.HBM(g.shape, g.dtype),
                   pltpu.HBM(land.shape, land.dtype), _TOKEN),
        in_specs=(_HBM, _HBM), out_specs=(_SEM, _SEM, _HBM, _HBM, pl.BlockSpec(memory_space=pltpu.VMEM)),
        input_output_aliases={0: 2, 1: 3}, compiler_params=_EFFECT,
    )(_in_hbm(g), _in_hbm(land))


def _rs_sibling_wait(name, send_sems, recv_sems, g, land, after):
    def body(g_ref, land_ref, send_sems, recv_sems, *rest):
        x, y, c = _mesh_pos()
        cp = _remote(g_ref.at[:, :, pl.ds((1 - c) * P_HALF, P_HALF), :], land_ref, send_sems, recv_sems, 0,
                     (x, y, 1 - c))
        cp.wait_send()
        cp.wait_recv()

    return pl.pallas_call(
        body, name=name, out_shape=(pltpu.HBM(g.shape, g.dtype), pltpu.HBM(land.shape, land.dtype)),
        in_specs=(_HBM, _HBM, _SEM, _SEM) + (_ANY,) * len(after), out_specs=(_HBM, _HBM),
        input_output_aliases={0: 0, 1: 1}, compiler_params=_EFFECT,
    )(g, land, send_sems, recv_sems, *after)


def _rs_add(name, ids, g, buf, row_tile):
    nl, _, hr, cols = buf.shape
    n_rt = hr // row_tile

    def body(ids_ref, g_ref, b_ref, own_ref, tb_ref):
        t = g_ref[...] + b_ref[...]
        tb_ref[...] = t.astype(BF16)

        @pl.when(pl.program_id(2) == ids_ref[1])
        def _():
            own_ref[...] = t

    blk = (None, None, row_tile, cols)
    grid_spec = pltpu.PrefetchScalarGridSpec(
        num_scalar_prefetch=1, grid=(nl, n_rt, N_SHARD),
        in_specs=[pl.BlockSpec(blk, lambda l, j, s, ids_ref: (l, s, ids_ref[0] * n_rt + j, 0)),
                  pl.BlockSpec(blk, lambda l, j, s, ids_ref: (l, s, j, 0))],
        out_specs=[pl.BlockSpec((None, row_tile, cols), lambda l, j, s, ids_ref: (l, j, 0)),
                   pl.BlockSpec(blk, lambda l, j, s, ids_ref: (l, s, j, 0))])
    return pl.pallas_call(
        body, name=name, grid_spec=grid_spec,
        out_shape=[jax.ShapeDtypeStruct((nl, hr, cols), F32), jax.ShapeDtypeStruct(buf.shape, BF16)],
        compiler_params=_cparams(3),
    )(ids, g, buf)


def _rs_sum(ids, layer, own, bufb, reduced, row_tile):
    _, hr, cols = own.shape
    n_rt = hr // row_tile

    def body(ids_ref, own_ref, b_ref, reduced_in, f_ref):
        f_ref[...] = ((own_ref[...] + b_ref[0].astype(F32)) + b_ref[1].astype(F32)) + b_ref[2].astype(F32)

    grid_spec = pltpu.PrefetchScalarGridSpec(
        num_scalar_prefetch=1, grid=(n_rt,),
        in_specs=[pl.BlockSpec((None, row_tile, cols), lambda j, ids_ref: (0, j, 0)),
                  pl.BlockSpec((3, None, row_tile, cols), lambda j, ids_ref: (0, 0, j, 0)),
                  pl.BlockSpec(memory_space=pl.ANY)],
        out_specs=pl.BlockSpec((None, row_tile, cols), lambda j, ids_ref: (layer, ids_ref[0] * n_rt + j, 0)))
    return pl.pallas_call(
        body, name="rs_sum", grid_spec=grid_spec,
        out_shape=jax.ShapeDtypeStruct(reduced.shape, F32),
        input_output_aliases={3: 0},
        compiler_params=_cparams(1),
    )(ids, own, bufb, reduced)


def _rs_exchange(f, layer):
    def body(f_in, o, send_sems, recv_sems):
        x, y, c = _mesh_pos()
        mine = o.at[layer, pl.ds(c * P_HALF, P_HALF), :]
        cp = _remote(mine, mine, send_sems, recv_sems, 0, (x, y, 1 - c))
        cp.start()
        cp.wait_send()
        theirs = o.at[layer, pl.ds((1 - c) * P_HALF, P_HALF), :]
        _remote(theirs, theirs, send_sems, recv_sems, 0, (x, y, 1 - c)).wait_recv()

    return pl.pallas_call(
        body, name="rs_exchange",
        in_specs=[_ANY], out_specs=_ANY,
        out_shape=jax.ShapeDtypeStruct(f.shape, F32),
        scratch_shapes=[pltpu.SemaphoreType.DMA((1,)), pltpu.SemaphoreType.DMA((1,))],
        input_output_aliases={0: 0},
    )(f)


def _small_all_reduce(s):
    n_rows = s.shape[0]
    hr = n_rows // 2
    qr = hr // N_SHARD

    def body(s_ref, o_ref, sibbuf, tbuf, qbuf, fbuf, send_sems, recv_sems):
        x, y, c = _mesh_pos()
        k = 2 * x + y
        sib = (x, y, 1 - c)
        chips = _other_chips(x, y)
        mine = pl.ds(pl.multiple_of(c * hr, SUBLANES), hr)
        theirs = pl.ds(pl.multiple_of((1 - c) * hr, SUBLANES), hr)

        def quarter(shard):
            return pl.ds(pl.multiple_of(shard * qr, SUBLANES), qr)

        first = _remote(s_ref.at[theirs], sibbuf, send_sems, recv_sems, 0, sib)
        first.start()
        first.wait()
        tbuf[...] = s_ref[mine, :] + sibbuf[...]
        cps = []
        for j, (px, py) in enumerate(chips):
            cp = _remote(tbuf.at[quarter(2 * px + py)], qbuf.at[j], send_sems, recv_sems, 1 + j, (px, py, c))
            cp.start()
            cps.append(cp)
        for cp in cps:
            cp.wait()
        fbuf[quarter(k), :] = (tbuf[quarter(k), :] + qbuf[1]) + (qbuf[0] + qbuf[2])
        cps = []
        for j, (px, py) in enumerate(chips):
            cp = _remote(fbuf.at[quarter(k)], fbuf.at[quarter(k)], send_sems, recv_sems, 4 + j, (px, py, c))
            cp.start()
            cps.append(cp)
        for j, (px, py) in enumerate(chips):
            got = fbuf.at[quarter(2 * px + py)]
            _remote(got, got, send_sems, recv_sems, 4 + j, (px, py, c)).wait_recv()
        for cp in cps:
            cp.wait_send()
        o_ref[mine, :] = fbuf[...]
        last = _remote(fbuf, o_ref.at[mine], send_sems, recv_sems, 7, sib)
        last.start()
        last.wait()

    vmem = pl.BlockSpec(memory_space=pltpu.VMEM)
    return pl.pallas_call(
        body, name="small_all_reduce",
        in_specs=[vmem], out_specs=vmem,
        out_shape=jax.ShapeDtypeStruct(s.shape, F32),
        scratch_shapes=[pltpu.VMEM((hr, D_MODEL), F32), pltpu.VMEM((hr, D_MODEL), F32),
                        pltpu.VMEM((3, qr, D_MODEL), F32), pltpu.VMEM((hr, D_MODEL), F32),
                        pltpu.SemaphoreType.DMA((8,)), pltpu.SemaphoreType.DMA((8,))],
        compiler_params=pltpu.CompilerParams(vmem_limit_bytes=VMEM_LIMIT),
    )(s)


_SMALL = ("norm_mix", "w_pool", "pool_scale", "lam_re", "lam_im", "log_dt", "b_re", "b_im", "c_re", "c_im",
          "d_skip", "b_glu", "norm_ffn", "norm_final")
_WEIGHTS = ("norm_mix", "w_in", "w_pool", "pool_scale", "lam_re", "lam_im", "log_dt", "b_re", "b_im", "c_re",
            "c_im", "d_skip", "w_glu", "b_glu", "w_out", "norm_ffn", "w_gate", "w_up", "w_down", "norm_final")


def _local_step(x, target, p, get_weights, ffn_bwd_done, put_grads):
    nl = p["norm_mix"].shape[0]

    def tied(a, token):
        return a if token is None else a + token
    n_rows = nl * N_SSM_GROUPS
    lr = p["lam_re"].reshape(n_rows, 1, SSM_STATE)
    li = p["lam_im"].reshape(n_rows, 1, SSM_STATE)
    ldt = p["log_dt"].reshape(n_rows, 1, 1)
    br_t = p["b_re"].reshape(n_rows, SSM_STATE, SSM_GROUP).transpose(0, 2, 1)
    bi_t = p["b_im"].reshape(n_rows, SSM_STATE, SSM_GROUP).transpose(0, 2, 1)
    ar, ai, bbr_t, bbi_t = _disc_fwd(lr, li, ldt, br_t, bi_t)
    ar = ar.reshape(nl, 1, N_STATE)
    ai = ai.reshape(nl, 1, N_STATE)
    bbr = bbr_t.transpose(0, 2, 1).reshape(nl, N_SSM_GROUPS, SSM_STATE, SSM_GROUP)
    bbi = bbi_t.transpose(0, 2, 1).reshape(nl, N_SSM_GROUPS, SSM_STATE, SSM_GROUP)
    w_pool = p["w_pool"].astype(BF16)
    p = dict(p)
    for n in ("norm_mix", "pool_scale", "b_glu", "norm_ffn"):
        p[n] = p[n].reshape(nl, 1, -1)
    swap = lambda a: jnp.swapaxes(a, -1, -2)
    bpad = jax.vmap(_pad_pairs)(bbr, bbi).astype(BF16)
    cpad_t = jax.vmap(_pad_pairs)(swap(p["c_re"]), -swap(p["c_im"])).astype(BF16)
    bpad_t, cpad = swap(bpad), swap(cpad_t)
    dskip = p["d_skip"].reshape(nl, 1, D_SSM)

    layers = []
    h = x
    for l in range(nl):
        wp = get_weights(l, [h] if l else [h, bpad, cpad, bpad_t, cpad_t, ar, ai])
        u, ypool = _mix_in_fwd(h, p["norm_mix"], wp, l, w_pool, p["pool_scale"])
        sre, sim, yraw = _ssm_fwd(u, l, bpad, cpad, ar, ai, dskip)
        hm = _mix_out_fwd(yraw, ypool, h, wp, l, p["b_glu"])
        h_next, n2, act_s, fgate_s, fup_s = _ffn_fwd(hm, p["norm_ffn"], wp, l)
        layers.append(dict(h=h, u=u, ypool=ypool, sre=sre, sim=sim, yraw=yraw, hm=hm, n2=n2, act_s=act_s, wp=wp,
                           fgate_s=fgate_s, fup_s=fup_s))
        h = h_next

    dh, loss, d_norm_final = _final_fwd_bwd(h, p["norm_final"].reshape(1, D_MODEL), target)

    raw = {n: [None] * nl for n in ("dg1", "dwp", "dsc", "dcp", "dbp", "ddsk", "db_glu", "dg2", "dar", "dai")}
    token = None
    for l in reversed(range(nl)):
        s = layers[l]
        wp = s["wp"]
        g1 = lax.empty((1, N_SHARD, P_ROWS, D_MODEL), F32)
        dhm, dg2, dgate_s, dup_s, dhb = _ffn_bwd_act(dh, s["hm"], tied(p["norm_ffn"], token), s["fgate_s"],
                                                      s["fup_s"], wp, l)
        g1 = _ffn_bwd_w(s["n2"], dgate_s, dup_s, s["act_s"], dhb, g1)
        token = ffn_bwd_done(l, [g1])
        dyraw, dyp, db_glu, g1 = _mix_out_bwd(dhm, s["yraw"], s["ypool"], wp, l, tied(p["b_glu"], token), g1)
        dus, dcp, dbp, dar, dai, ddsk = _ssm_bwd(dyraw, s["u"], s["sre"], s["sim"], l, cpad_t, bpad_t, ar, ai, dskip)
        dup, dwp, dsc = _pool_bwd(dyp, s["u"], l, w_pool, p["pool_scale"])
        dh, dg1, g1 = _mix_in_bwd(dup, dus, s["h"], dhm, p["norm_mix"], wp, l, g1)
        token = put_grads(l, g1)
        for n, a in (("dg1", dg1), ("dwp", dwp), ("dsc", dsc), ("dcp", dcp), ("dbp", dbp), ("ddsk", ddsk),
                     ("db_glu", db_glu), ("dg2", dg2), ("dar", dar), ("dai", dai)):
            raw[n][l] = a

    st = {n: jnp.stack(v) for n, v in raw.items()}
    dc_re, dc_im = jax.vmap(_unpad_pairs)(swap(st["dcp"]))
    dbbr, dbbi = jax.vmap(_unpad_pairs)(st["dbp"])
    rows = lambda a: a.reshape((n_rows,) + a.shape[2:])
    dlr, dli, dldt, dbr_t, dbi_t = _disc_bwd(lr, li, ldt, br_t, bi_t, st["dar"].reshape(n_rows, 1, SSM_STATE),
                                              st["dai"].reshape(n_rows, 1, SSM_STATE), rows(swap(dbbr)),
                                              rows(swap(dbbi)))
    small = {"norm_mix": st["dg1"][:, 0], "w_pool": st["dwp"], "pool_scale": st["dsc"][:, 0], "c_re": swap(dc_re),
             "c_im": -swap(dc_im), "d_skip": st["ddsk"].reshape(nl, N_SSM_GROUPS, SSM_GROUP),
             "b_glu": st["db_glu"][:, 0], "norm_ffn": st["dg2"][:, 0]}
    small["lam_re"] = dlr.reshape(nl, N_SSM_GROUPS, SSM_STATE)
    small["lam_im"] = dli.reshape(nl, N_SSM_GROUPS, SSM_STATE)
    small["log_dt"] = dldt.reshape(nl, N_SSM_GROUPS)
    small["b_re"] = dbr_t.reshape(nl, N_SSM_GROUPS, SSM_GROUP, SSM_STATE)
    small["b_im"] = dbi_t.reshape(nl, N_SSM_GROUPS, SSM_GROUP, SSM_STATE)
    small["d_skip"] = small["d_skip"].transpose(_SMALL_VIEW["d_skip"])
    small["norm_final"] = d_norm_final
    return loss, dh, small


_SMALL_VIEW = {"b_re": (0, 1, 3, 2), "b_im": (0, 1, 3, 2), "d_skip": (0, 2, 1)}
_SMALL_GROUPS = (("b_re", "b_im"), ("c_re", "c_im"), ("lam_re", "lam_im"), ("norm_mix", "norm_ffn"),
                 ("pool_scale", "b_glu"), ("w_pool",), ("log_dt",), ("d_skip",), ("norm_final",))


def _view(n, a):
    a = a.transpose(_SMALL_VIEW[n]) if n in _SMALL_VIEW else a
    return a[None] if a.ndim == 1 else a


def _unview(n, a, shape):
    a = a.reshape(shape) if len(shape) == 1 else a
    return a.transpose(_SMALL_VIEW[n]) if n in _SMALL_VIEW else a


def _flatten_small(views):
    flat = jnp.concatenate([views[n].reshape(-1) for n in _SMALL])
    n_rows = -(-flat.shape[0] // (64 * D_MODEL)) * 64
    return jnp.pad(flat, (0, n_rows * D_MODEL - flat.shape[0])).reshape(n_rows, D_MODEL)


def _split_small(flat, like):
    flat = flat.reshape(-1)
    out, at = {}, 0
    for n in _SMALL:
        size = like[n].size
        out[n] = flat[at:at + size].reshape(like[n].shape)
        at += size
    return out


def _adamw_small(name, ws, ms, vs, gs):
    k = len(ws)

    def body(*refs):
        ins, outs = refs[:4 * k], refs[4 * k:]
        for i in range(k):
            w, m, v, g = (ins[j * k + i][...] for j in range(4))
            delta, mn, vn = _adamw_math(w, g, m, v)
            outs[i][...] = delta
            outs[k + i][...] = mn
            outs[2 * k + i][...] = vn

    shapes = [jax.ShapeDtypeStruct(w.shape, F32) for w in ws] * 3
    outs = pl.pallas_call(body, name=name, out_shape=shapes,
                          compiler_params=pltpu.CompilerParams(vmem_limit_bytes=VMEM_LIMIT))(*ws, *ms, *vs, *gs)
    return outs[:k], outs[k:2 * k], outs[2 * k:]


def kernel(x, norm_mix, w_in, w_pool, pool_scale, lam_re, lam_im, log_dt, b_re, b_im, c_re, c_im, d_skip, w_glu, b_glu, w_out, norm_ffn, w_gate, w_up, w_down, norm_final, loss_target, m_norm_mix, m_w_in, m_w_pool, m_pool_scale, m_lam_re, m_lam_im, m_log_dt, m_b_re, m_b_im, m_c_re, m_c_im, m_d_skip, m_w_glu, m_b_glu, m_w_out, m_norm_ffn, m_w_gate, m_w_up, m_w_down, m_norm_final, v_norm_mix, v_w_in, v_w_pool, v_pool_scale, v_lam_re, v_lam_im, v_log_dt, v_b_re, v_b_im, v_c_re, v_c_im, v_d_skip, v_w_glu, v_b_glu, v_w_out, v_norm_ffn, v_w_gate, v_w_up, v_w_down, v_norm_final):
    given = dict(locals())
    w = {n: given[n] for n in _WEIGHTS}
    m = {n: given["m_" + n] for n in _WEIGHTS}
    v = {n: given["v_" + n] for n in _WEIGHTS}
    ids = jnp.stack([lax.axis_index("c"), 2 * lax.axis_index("x") + lax.axis_index("y")]).astype(jnp.int32)

    t_names = ("w_gate", "w_up")
    tr = lambda a: a.transpose(0, 2, 1)
    for d in (w, m, v):
        d.update({n: tr(d[n]) for n in t_names})

    nl = norm_mix.shape[0]
    packed = [_pack_weights(ids, l, w["w_in"], w["w_glu"], w["w_out"], w["w_down"], w["w_gate"], w["w_up"])
              for l in range(nl)]
    started, last = {}, ids
    for l in range(nl):
        started[l] = _ag_start(f"ag_start_{l}", packed[l], last)
        last = started[l][3]
    views = [{n: _view(n, d[n]) for n in _SMALL} for d in (w, m, v)]

    def get_weights(l, after):
        send_sems, recv_sems, buf, _ = started[l]
        after = after + ([last] if l == 0 else [])
        return _ag_forward(_ag_wait(f"ag_wait_{l}", send_sems, recv_sems, buf, after))

    to_sibling, to_chips, reduced = {}, {}, {}

    def put_grads(l, g):
        to_sibling[l] = _rs_sibling_start(f"rs_sibling_start_{l}", g)
        token = to_sibling[l][4]
        if l + 1 in to_chips:
            finish(l + 1, [token])
        return token[:1, :1]

    def ffn_bwd_done(l, after):
        return send_to_chips(l + 1, after) if l + 1 in to_sibling else None

    def send_to_chips(l, after):
        send_sems, recv_sems, g, land, _ = to_sibling.pop(l)
        g, land = _rs_sibling_wait(f"rs_sibling_wait_{l}", send_sems, recv_sems, g, land, after)
        own, t = _rs_add("rs_add", ids, g, land, RS_ROW_TILE)
        send_sems, recv_sems, t, land, token = _rs_chips_start(f"rs_chips_start_{l}", t)
        to_chips[l] = (send_sems, recv_sems, t, land, own)
        return token[:1, :1]

    def finish(l, after):
        send_sems, recv_sems, t, land, own = to_chips.pop(l)
        land = _rs_chips_wait(f"rs_chips_wait_{l}", send_sems, recv_sems, t, land, after)
        shard = lax.empty((1, P_ROWS, D_MODEL), F32)
        reduced[l] = _rs_exchange(_rs_sum(ids, 0, own, land, shard, RS_ROW_TILE), 0)

    loss, grad_x, small = _local_step(x[0], loss_target[0], {n: w[n] for n in _SMALL}, get_weights, ffn_bwd_done,
                                      put_grads)
    loss = lax.psum(loss[0, 0], ("x", "y", "c"))
    token = send_to_chips(0, [small["norm_final"]])

    big = (("w_in", P_IN_BLK, 256, False), ("w_out", P_OUT_BLK, 256, False), ("w_down", P_WD_BLK, 352, False),
           ("w_gate", P_WG_BLK, 352, False), ("w_up", P_WU_BLK, 352, False), ("w_glu", P_GLU_BLK, 128, True))
    res = {n: None for n, *_ in big}

    def adamw_layer(l, after):
        for n, (blk, idx), row_tile, glu in big:
            res[n] = _adamw("adamw_" + n, l, w[n], m[n], v[n], reduced[l], (blk, D_MODEL), blk * idx, row_tile,
                            res[n], after, glu)

    for l in reversed(range(1, nl)):
        adamw_layer(l, [token])
    small["norm_final"] = small["norm_final"] + token[:1, :1]
    small_sum = _small_all_reduce(_flatten_small(small))
    finish(0, [small_sum] + [r[0] for r in res.values() if r is not None])
    adamw_layer(0, [])
    for n in t_names:
        res[n] = tuple(tr(a) for a in res[n])
    g_views = _split_small(small_sum, views[0])
    for group in _SMALL_GROUPS:
        deltas, new_ms, new_vs = _adamw_small("adamw_" + group[0], *[[d[n] for n in group] for d in views],
                                              [g_views[n] for n in group])
        for i, n in enumerate(group):
            res[n] = tuple(_unview(n, a, w[n].shape) for a in (g_views[n], deltas[i], new_ms[i], new_vs[i]))

    return (loss, grad_x[None], *[res[n][0] for n in _WEIGHTS], *[res[n][1] for n in _WEIGHTS],
            *[res[n][2] for n in _WEIGHTS], *[res[n][3] for n in _WEIGHTS])
```

```python
import functools
import math

import jax
import jax.numpy as jnp
from jax import lax
from jax.experimental import pallas as pl
from jax.experimental.pallas import tpu as pltpu

F32 = jnp.float32
BF16 = jnp.bfloat16

D_MODEL = 1024
D_POOL = 512
D_SSM = 512
POOL_WINDOWS = (2, 4, 8, 16)
POOL_GROUP = 128
POOL_HALO = 16
N_SSM_GROUPS = 32
SSM_GROUP = 16
SSM_STATE = 64
N_STATE = N_SSM_GROUPS * SSM_STATE
N_PAIRS = N_SSM_GROUPS // 2
D_FF = 2816
N_SHARD = 4
FF_SHARD = D_FF // N_SHARD
RMS_EPS = 1e-6

ADAM_LR = 0.001
ADAM_B1 = 0.9
ADAM_B2 = 0.999
ADAM_EPS = 1e-08
ADAM_WD = 0.01
ADAM_STEP = 10

P_ROWS = 2816
P_WD_BLK = (704, 0)
P_WG_BLK = (704, 1)
P_WU_BLK = (704, 2)
P_FF_ROWS = 2112
P_GLU_BLK = (64, 33)
P_GLU_PAD = 192
P_IN_BLK = (256, 9)
P_OUT_BLK = (256, 10)

SUBLANES = 8
VMEM_LIMIT = 56 * 1024 * 1024

TM = 1024
TM_FFN = 512
TM_FFN_LONG = 1024
FFN_SPLIT = 2
TS = 1024
SCAN_LANES = 512


def _cparams(n_axes):
    return pltpu.CompilerParams(dimension_semantics=("arbitrary",) * n_axes, vmem_limit_bytes=VMEM_LIMIT)


def _dot(a, b):
    return jnp.dot(a, b, preferred_element_type=F32)


def _dot_nt(a, b):
    return lax.dot_general(a, b, (((1,), (1,)), ((), ())), preferred_element_type=F32)


def _dot_tn(a, b):
    return lax.dot_general(a, b, (((0,), (0,)), ((), ())), preferred_element_type=F32)


def _rms_hat(x):
    r = lax.rsqrt(jnp.mean(x * x, axis=-1, keepdims=True) + RMS_EPS)
    return x * r, r


def _rms_bwd(d_hat, xhat, r):
    return r * (d_hat - xhat * jnp.mean(d_hat * xhat, axis=-1, keepdims=True))


def _sigmoid(x):
    return 1.0 / (1.0 + jnp.exp(-x))


_GELU_C = math.sqrt(2.0 / math.pi)
_GELU_K = 0.044715


def _gelu(x):
    return 0.5 * x * (1.0 + jnp.tanh(_GELU_C * (x + _GELU_K * x * x * x)))


def _gelu_grad(x):
    th = jnp.tanh(_GELU_C * (x + _GELU_K * x * x * x))
    return 0.5 * (1.0 + th) + 0.5 * x * (1.0 - th * th) * _GELU_C * (1.0 + 3.0 * _GELU_K * x * x)


def _glu_weight(ref):
    v = ref[...]
    return jnp.concatenate([v[:, :, :D_SSM], v[:, :, D_SSM:]], axis=1).reshape(D_SSM, D_SSM)


def _glu_pack(w):
    v = w.reshape(N_SHARD, 128, D_SSM)
    return jnp.concatenate([v[:, :64, :], v[:, 64:, :]], axis=2)


def _pool_diff(ext, row0, tm):
    rows = row0 + lax.broadcasted_iota(jnp.int32, (tm, 1), 0)
    outs = []
    for gi, w in enumerate(POOL_WINDOWS):
        e = ext[:, gi * POOL_GROUP:(gi + 1) * POOL_GROUP]
        s = e
        k = 1
        while k < w:
            s = s + pltpu.roll(s, k, 0)
            k *= 2
        inv = 1.0 / jnp.minimum(rows + 1, w).astype(F32)
        outs.append(s[POOL_HALO:, :] * inv - e[POOL_HALO:, :])
    return outs


def _mix_in_fwd(h, g1, wp, layer, w_pool, scale):
    L = h.shape[0]
    tm = min(TM, L)

    def body(h_ref, g_ref, w_ref, wp_ref, sc_ref, u_ref, yp_ref, carry):
        i = pl.program_id(0)

        @pl.when(i == 0)
        def _():
            carry[...] = jnp.zeros_like(carry)

        xhat, _ = _rms_hat(h_ref[...])
        n1 = (xhat * g_ref[...]).astype(BF16)
        u = _dot(n1, w_ref[...].reshape(D_MODEL, D_MODEL))
        u_ref[...] = u
        up = u[:, :D_POOL]
        ext = jnp.concatenate([carry[...], up], axis=0)
        carry[...] = up[tm - POOL_HALO:, :]
        diffs = _pool_diff(ext, i * tm, tm)
        for gi in range(4):
            cols = slice(gi * POOL_GROUP, (gi + 1) * POOL_GROUP)
            yp_ref[:, cols] = _dot(diffs[gi].astype(BF16), wp_ref[gi]) * sc_ref[:, cols]

    blk, idx = P_IN_BLK
    return pl.pallas_call(
        body, name="mix_in_fwd", grid=(L // tm,),
        in_specs=[pl.BlockSpec((tm, D_MODEL), lambda i: (i, 0)),
                  pl.BlockSpec((None, 1, D_MODEL), lambda i: (layer, 0, 0)),
                  pl.BlockSpec((N_SHARD, None, blk, D_MODEL), lambda i: (0, 0, idx, 0)),
                  pl.BlockSpec((None, 4, POOL_GROUP, POOL_GROUP), lambda i: (layer, 0, 0, 0)),
                  pl.BlockSpec((None, 1, D_POOL), lambda i: (layer, 0, 0))],
        out_specs=[pl.BlockSpec((tm, D_MODEL), lambda i: (i, 0)),
                   pl.BlockSpec((tm, D_POOL), lambda i: (i, 0))],
        out_shape=[jax.ShapeDtypeStruct((L, D_MODEL), F32), jax.ShapeDtypeStruct((L, D_POOL), F32)],
        scratch_shapes=[pltpu.VMEM((POOL_HALO, D_POOL), F32)],
        compiler_params=_cparams(1),
    )(h, g1, wp, w_pool, scale)


def _cmul(xr, xi, yr, yi):
    return xr * yr - xi * yi, xr * yi + xi * yr


SCAN_BLOCK = 64
N_SCAN_TABLES = 26


def _permute_rows(src, dst, n_rows):
    for b in range(n_rows // SCAN_BLOCK):
        for tau in range(SUBLANES):
            dst[pl.ds(SCAN_BLOCK * b + SUBLANES * tau, SUBLANES), :] = (
                src[pl.ds(SCAN_BLOCK * b + tau, SUBLANES, stride=SUBLANES), :])


def _scan_tables(ar, ai, tab, reverse):
    c = ar.shape[1]
    row = lax.broadcasted_iota(jnp.int32, (SUBLANES, c), 0)
    zero = jnp.zeros((SUBLANES, c), F32)
    full = lambda v: jnp.broadcast_to(v, (SUBLANES, c))
    pw = [(ar, ai)]
    for _ in range(SUBLANES - 1):
        pw.append(_cmul(*pw[-1], ar, ai))
    a8 = pw[-1]
    a16 = _cmul(*a8, *a8)
    a32 = _cmul(*a16, *a16)
    tab[0] = full(ar)
    tab[1] = full(ai)
    for n, (s, (pr, pi)) in enumerate(((1, a8), (2, a16), (4, a32))):
        keep = (row < SUBLANES - s) if reverse else (row >= s)
        tab[2 + 2 * n] = jnp.where(keep, pr, zero)
        tab[3 + 2 * n] = jnp.where(keep, pi, zero)
    cur = a8
    qr, qi = zero, zero
    for n in range(SUBLANES):
        at = (SUBLANES - 1 - n) if reverse else n
        qr = jnp.where(row == at, cur[0], qr)
        qi = jnp.where(row == at, cur[1], qi)
        cur = _cmul(*cur, *a8)
    tab[8] = qr
    tab[9] = qi
    for tau in range(SUBLANES):
        pr, pi = pw[SUBLANES - 1 - tau] if reverse else pw[tau]
        tab[10 + 2 * tau] = full(pr)
        tab[11 + 2 * tau] = full(pi)


def _cmac(xr, xi, ar, ai, yr, yi):
    return xr + ar * yr - ai * yi, xi + ar * yi + ai * yr


def _chain_segments(er, ei, c_r, c_i, tab, cols, reverse):
    tr, ti = er, ei
    for n, s in enumerate((1, 2, 4)):
        shift = SUBLANES - s if reverse else s
        tr, ti = _cmac(tr, ti, tab[2 + 2 * n, :, cols], tab[3 + 2 * n, :, cols],
                       pltpu.roll(tr, shift, 0), pltpu.roll(ti, shift, 0))
    return _cmac(tr, ti, tab[8, :, cols], tab[9, :, cols], c_r, c_i)


def _ssm_fwd(u, layer, bpad, cpad, ar, ai, dskip):
    L = u.shape[0]
    ts = min(TS, L)
    nq = 4
    cq = N_STATE // nq

    def body(u_ref, bp_ref, cp_ref, ar_ref, ai_ref, dsk_ref, sre_ref, sim_ref, y_ref, cr, ci, tab, up, yp):
        t = pl.program_id(1)

        @pl.when(t == 0)
        def _():
            cr[...] = jnp.zeros_like(cr)
            ci[...] = jnp.zeros_like(ci)
            _scan_tables(ar_ref[...], ai_ref[...], tab, reverse=False)

        _permute_rows(u_ref, up, ts)
        uf = up[...]
        ub = uf.astype(BF16)
        for jj in range(4):
            bu = _dot(ub, bp_ref[jj])
            sre_ref[:, jj * 128:(jj + 1) * 128] = bu[:, :128]
            sim_ref[:, jj * 128:(jj + 1) * 128] = bu[:, 128:]

        shp = (SUBLANES, SCAN_LANES)
        first_row = lax.broadcasted_iota(jnp.int32, shp, 0) == 0
        for cc in range(cq // SCAN_LANES):
            cols = slice(cc * SCAN_LANES, (cc + 1) * SCAN_LANES)

            def block(b, carry, cols=cols):
                c_r, c_i = carry
                base = pl.multiple_of(b * SCAN_BLOCK, SCAN_BLOCK)
                rows = lambda tau: pl.ds(base + SUBLANES * tau, SUBLANES)
                a_r, a_i = tab[0, :, cols], tab[1, :, cols]
                ys = [(sre_ref[rows(0), cols], sim_ref[rows(0), cols])]
                for tau in range(1, SUBLANES):
                    ys.append(_cmac(sre_ref[rows(tau), cols], sim_ref[rows(tau), cols], a_r, a_i, *ys[-1]))
                tr, ti = _chain_segments(*ys[-1], c_r, c_i, tab, cols, reverse=False)
                in_r = jnp.where(first_row, c_r, pltpu.roll(tr, 1, 0))
                in_i = jnp.where(first_row, c_i, pltpu.roll(ti, 1, 0))
                for tau in range(SUBLANES):
                    sr, si = _cmac(*ys[tau], tab[10 + 2 * tau, :, cols], tab[11 + 2 * tau, :, cols], in_r, in_i)
                    sre_ref[rows(tau), cols] = sr
                    sim_ref[rows(tau), cols] = si
                return (jnp.broadcast_to(tr[SUBLANES - 1:, :], shp), jnp.broadcast_to(ti[SUBLANES - 1:, :], shp))

            c_r, c_i = lax.fori_loop(0, ts // SCAN_BLOCK, block, (cr[:, cols], ci[:, cols]), unroll=2)
            cr[:, cols] = c_r
            ci[:, cols] = c_i

        acc = dsk_ref[...] * uf
        for jj in range(4):
            cols = slice(jj * 128, (jj + 1) * 128)
            scat = jnp.concatenate([sre_ref[:, cols], sim_ref[:, cols]], axis=1).astype(BF16)
            acc = acc + _dot(scat, cp_ref[jj])
        yp[...] = acc
        _permute_rows(yp, y_ref, ts)

    return pl.pallas_call(
        body, name="ssm_fwd", grid=(nq, L // ts),
        in_specs=[pl.BlockSpec((ts, 128), lambda q, t: (t, 4 + q)),
                  pl.BlockSpec((None, 4, 128, 256), lambda q, t: (layer, q, 0, 0)),
                  pl.BlockSpec((None, 4, 256, 128), lambda q, t: (layer, q, 0, 0)),
                  pl.BlockSpec((None, 1, cq), lambda q, t: (layer, 0, q)),
                  pl.BlockSpec((None, 1, cq), lambda q, t: (layer, 0, q)),
                  pl.BlockSpec((None, 1, 128), lambda q, t: (layer, 0, q))],
        out_specs=[pl.BlockSpec((ts, cq), lambda q, t: (t, q)),
                   pl.BlockSpec((ts, cq), lambda q, t: (t, q)),
                   pl.BlockSpec((ts, 128), lambda q, t: (t, q))],
        out_shape=[jax.ShapeDtypeStruct((L, N_STATE), F32), jax.ShapeDtypeStruct((L, N_STATE), F32),
                   jax.ShapeDtypeStruct((L, D_SSM), F32)],
        scratch_shapes=[pltpu.VMEM((SUBLANES, cq), F32), pltpu.VMEM((SUBLANES, cq), F32),
                        pltpu.VMEM((N_SCAN_TABLES, SUBLANES, cq), F32),
                        pltpu.VMEM((ts, 128), F32), pltpu.VMEM((ts, 128), F32)],
        compiler_params=_cparams(2),
    )(u, bpad, cpad, ar, ai, dskip)


def _mix_out_fwd(yraw, ypool, h, wp, layer, b_glu):
    L = h.shape[0]
    tm = min(TM, L)

    def body(yr_ref, yp_ref, h_ref, wglu_ref, b_ref, wout_ref, o_ref):
        y = _gelu(yr_ref[...])
        z = _dot(y.astype(BF16), _glu_weight(wglu_ref)) + b_ref[...]
        o = y * _sigmoid(z)
        mix = jnp.concatenate([yp_ref[...], o], axis=1).astype(BF16)
        o_ref[...] = h_ref[...] + _dot(mix, wout_ref[...].reshape(D_MODEL, D_MODEL))

    gb, gi = P_GLU_BLK
    ob, oi = P_OUT_BLK
    return pl.pallas_call(
        body, name="mix_out_fwd", grid=(L // tm,),
        in_specs=[pl.BlockSpec((tm, D_SSM), lambda i: (i, 0)),
                  pl.BlockSpec((tm, D_POOL), lambda i: (i, 0)),
                  pl.BlockSpec((tm, D_MODEL), lambda i: (i, 0)),
                  pl.BlockSpec((N_SHARD, None, gb, D_MODEL), lambda i: (0, 0, gi, 0)),
                  pl.BlockSpec((None, 1, D_SSM), lambda i: (layer, 0, 0)),
                  pl.BlockSpec((N_SHARD, None, ob, D_MODEL), lambda i: (0, 0, oi, 0))],
        out_specs=pl.BlockSpec((tm, D_MODEL), lambda i: (i, 0)),
        out_shape=jax.ShapeDtypeStruct((L, D_MODEL), F32),
        compiler_params=_cparams(1),
    )(yraw, ypool, h, wp, b_glu, wp)


def _ffn_weights(ref, k):
    return ref[k, 0:FF_SHARD, :], ref[k, FF_SHARD:2 * FF_SHARD, :], ref[k, 2 * FF_SHARD:P_FF_ROWS, :]


def _ffn_weight_spec():
    return pl.BlockSpec((N_SHARD, None, P_FF_ROWS, D_MODEL), lambda m, k: (0, 0, 0, 0),
                        pipeline_mode=pl.Buffered(1))


def _ffn_fwd(h, g2, wp, layer):
    L = h.shape[0]
    tm = min(TM_FFN_LONG, L)

    def body(h_ref, g_ref, w_ref, o_ref, n2_ref, act_ref, dgate_ref, dup_ref):
        k = pl.program_id(1)

        @pl.when(k == 0)
        def _():
            x = h_ref[...]
            xhat, _ = _rms_hat(x)
            n2_ref[...] = (xhat * g_ref[...]).astype(BF16)
            o_ref[...] = x

        wd, wg_t, wu_t = _ffn_weights(w_ref, k)
        n2 = n2_ref[...]
        gate = _dot_nt(n2, wg_t)
        up = _dot_nt(n2, wu_t)
        sg = _sigmoid(gate)
        silu = gate * sg
        act = (silu * up).astype(BF16)
        act_ref[...] = act
        dgate_ref[...] = (up * (sg * (1.0 + gate * (1.0 - sg)))).astype(BF16)
        dup_ref[...] = silu.astype(BF16)
        o_ref[...] += _dot(act, wd)

    act_shape = jax.ShapeDtypeStruct((N_SHARD, L, FF_SHARD), BF16)
    return pl.pallas_call(
        body, name="ffn_fwd", grid=(L // tm, N_SHARD),
        in_specs=[pl.BlockSpec((tm, D_MODEL), lambda m, k: (m, 0)),
                  pl.BlockSpec((None, 1, D_MODEL), lambda m, k: (layer, 0, 0)),
                  _ffn_weight_spec()],
        out_specs=[pl.BlockSpec((tm, D_MODEL), lambda m, k: (m, 0)),
                   pl.BlockSpec((tm, D_MODEL), lambda m, k: (m, 0)),
                   pl.BlockSpec((None, tm, FF_SHARD), lambda m, k: (k, m, 0)),
                   pl.BlockSpec((None, tm, FF_SHARD), lambda m, k: (k, m, 0)),
                   pl.BlockSpec((None, tm, FF_SHARD), lambda m, k: (k, m, 0))],
        out_shape=[jax.ShapeDtypeStruct((L, D_MODEL), F32), jax.ShapeDtypeStruct((L, D_MODEL), BF16),
                   act_shape, act_shape, act_shape],
        compiler_params=_cparams(2),
    )(h, g2, wp)


def _final_fwd_bwd(h, gf, target):
    L = h.shape[0]
    tm = min(TM, L)

    def body(h_ref, g_ref, t_ref, dh_ref, loss_ref, dg_ref):
        i = pl.program_id(0)

        @pl.when(i == 0)
        def _():
            loss_ref[...] = jnp.zeros_like(loss_ref)
            dg_ref[...] = jnp.zeros_like(dg_ref)

        xhat, r = _rms_hat(h_ref[...])
        g = g_ref[...]
        e = xhat * g - t_ref[...]
        loss_ref[...] += 0.5 * jnp.sum(jnp.mean(e * e, axis=-1, keepdims=True), axis=0, keepdims=True)
        dy = e * (1.0 / D_MODEL)
        dg_ref[...] += jnp.sum(dy * xhat, axis=0, keepdims=True)
        dh_ref[...] = _rms_bwd(dy * g, xhat, r)

    return pl.pallas_call(
        body, name="final_fwd_bwd", grid=(L // tm,),
        in_specs=[pl.BlockSpec((tm, D_MODEL), lambda i: (i, 0)),
                  pl.BlockSpec((1, D_MODEL), lambda i: (0, 0)),
                  pl.BlockSpec((tm, D_MODEL), lambda i: (i, 0))],
        out_specs=[pl.BlockSpec((tm, D_MODEL), lambda i: (i, 0)),
                   pl.BlockSpec((1, 1), lambda i: (0, 0)),
                   pl.BlockSpec((1, D_MODEL), lambda i: (0, 0))],
        out_shape=[jax.ShapeDtypeStruct((L, D_MODEL), F32), jax.ShapeDtypeStruct((1, 1), F32),
                   jax.ShapeDtypeStruct((1, D_MODEL), F32)],
        compiler_params=_cparams(1),
    )(h, gf, target)


def _ffn_bwd_act(dh, h, g2, fgate_s, fup_s, wp, layer):
    L = h.shape[0]
    tm = min(TM_FFN, L)
    sub = tm // FFN_SPLIT

    def body(dh_ref, h_ref, g_ref, fgate_ref, fup_ref, w_ref,
             dhm_ref, dg_ref, dgate_ref, dup_ref, dhb_ref, dn2):
        m, k = pl.program_id(0), pl.program_id(1)

        @pl.when(jnp.logical_and(m == 0, k == 0))
        def _():
            dg_ref[...] = jnp.zeros_like(dg_ref)

        @pl.when(k == 0)
        def _():
            dhb_ref[...] = dh_ref[...].astype(BF16)
            dn2[...] = jnp.zeros_like(dn2)

        wd, wg_t, wu_t = _ffn_weights(w_ref, k)
        for rows in (slice(r * sub, (r + 1) * sub) for r in range(tm // sub)):
            dact = _dot_nt(dhb_ref[rows, :], wd)
            dgate = (dact * fgate_ref[rows, :].astype(F32)).astype(BF16)
            dup = (dact * fup_ref[rows, :].astype(F32)).astype(BF16)
            dgate_ref[rows, :] = dgate
            dup_ref[rows, :] = dup
            dn2[rows, :] += _dot(dgate, wg_t) + _dot(dup, wu_t)

        @pl.when(k == N_SHARD - 1)
        def _():
            xhat, r = _rms_hat(h_ref[...])
            d = dn2[...]
            dg_ref[...] += jnp.sum(d * xhat, axis=0, keepdims=True)
            dhm_ref[...] = dh_ref[...] + _rms_bwd(d * g_ref[...], xhat, r)

    act_spec = pl.BlockSpec((None, tm, FF_SHARD), lambda m, k: (k, m, 0))
    act_shape = jax.ShapeDtypeStruct((N_SHARD, L, FF_SHARD), BF16)
    row_spec = pl.BlockSpec((tm, D_MODEL), lambda m, k: (m, 0))
    return pl.pallas_call(
        body, name="ffn_bwd_act", grid=(L // tm, N_SHARD),
        in_specs=[row_spec, row_spec,
                  pl.BlockSpec((None, 1, D_MODEL), lambda m, k: (layer, 0, 0)),
                  act_spec, act_spec,
                  _ffn_weight_spec()],
        out_specs=[row_spec,
                   pl.BlockSpec((1, D_MODEL), lambda m, k: (0, 0)),
                   act_spec, act_spec, row_spec],
        out_shape=[jax.ShapeDtypeStruct((L, D_MODEL), F32), jax.ShapeDtypeStruct((1, D_MODEL), F32),
                   act_shape, act_shape, jax.ShapeDtypeStruct((L, D_MODEL), BF16)],
        scratch_shapes=[pltpu.VMEM((tm, D_MODEL), F32)],
        compiler_params=_cparams(2),
    )(dh, h, g2, fgate_s, fup_s, wp)


def _ffn_bwd_w(n2, dgate_s, dup_s, act_s, dhb, gbuf):
    L = n2.shape[0]
    tm = min(TM_FFN_LONG, L)

    def body(n2_ref, dgate_ref, dup_ref, act_ref, dhb_ref, g_in, g_ref):
        m = pl.program_id(1)

        @pl.when(m == 0)
        def _():
            g_ref[...] = jnp.zeros_like(g_ref)

        n2v = n2_ref[...]
        g_ref[0:FF_SHARD, :] += _dot_tn(act_ref[...], dhb_ref[...])
        g_ref[FF_SHARD:2 * FF_SHARD, :] += _dot_tn(dgate_ref[...], n2v)
        g_ref[2 * FF_SHARD:P_FF_ROWS, :] += _dot_tn(dup_ref[...], n2v)

    act_spec = pl.BlockSpec((None, tm, FF_SHARD), lambda k, m: (k, m, 0))
    row_spec = pl.BlockSpec((tm, D_MODEL), lambda k, m: (m, 0))
    return pl.pallas_call(
        body, name="ffn_bwd_w", grid=(N_SHARD, L // tm),
        in_specs=[row_spec, act_spec, act_spec, act_spec, row_spec, pl.BlockSpec(memory_space=pl.ANY)],
        out_specs=pl.BlockSpec((None, None, P_FF_ROWS, D_MODEL), lambda k, m: (0, k, 0, 0)),
        out_shape=jax.ShapeDtypeStruct(gbuf.shape, F32),
        input_output_aliases={5: 0},
        compiler_params=_cparams(2),
    )(n2, dgate_s, dup_s, act_s, dhb, gbuf)


def _mix_out_bwd(dhm, yraw, ypool, wp, layer, b_glu, gbuf):
    L = dhm.shape[0]
    tm = min(TM, L)

    def body(dhm_ref, yr_ref, yp_ref, wglu_ref, b_ref, wout_ref, g1_in,
             dyr_ref, dyp_ref, db_ref, g1_ref, dwout, dwglu, gpack):
        i = pl.program_id(0)

        @pl.when(i == 0)
        def _():
            db_ref[...] = jnp.zeros_like(db_ref)
            dwout[...] = jnp.zeros_like(dwout)
            dwglu[...] = jnp.zeros_like(dwglu)

        dhb = dhm_ref[...].astype(BF16)
        wglu = _glu_weight(wglu_ref)
        dmix = _dot_nt(dhb, wout_ref[...].reshape(D_MODEL, D_MODEL))
        dyp_ref[...] = dmix[:, :D_POOL]
        d_o = dmix[:, D_POOL:]
        yraw_v = yr_ref[...]
        y = _gelu(yraw_v)
        yb = y.astype(BF16)
        sig = _sigmoid(_dot(yb, wglu) + b_ref[...])
        mix = jnp.concatenate([yp_ref[...], y * sig], axis=1).astype(BF16)
        dwout[...] += _dot_tn(mix, dhb).reshape(N_SHARD, 256, D_MODEL)
        dz = d_o * y * sig * (1.0 - sig)
        dzb = dz.astype(BF16)
        db_ref[...] += jnp.sum(dz, axis=0, keepdims=True)
        dwglu[...] += _dot_tn(yb, dzb)
        dy = d_o * sig + _dot_nt(dzb, wglu)
        dyr_ref[...] = dy * _gelu_grad(yraw_v)

        @pl.when(i == n_steps - 1)
        def _():
            gpack[:, :gb, :] = _glu_pack(dwglu[...])
            gpack[:, gb:, :] = jnp.zeros((N_SHARD, P_GLU_PAD - gb, D_MODEL), F32)
            pltpu.sync_copy(gpack, g1_ref.at[0, :, pl.ds(gb * gi, P_GLU_PAD), :])
            pltpu.sync_copy(dwout, g1_ref.at[0, :, pl.ds(ob * oi, ob), :])

    gb, gi = P_GLU_BLK
    ob, oi = P_OUT_BLK
    n_steps = L // tm
    return pl.pallas_call(
        body, name="mix_out_bwd", grid=(n_steps,),
        in_specs=[pl.BlockSpec((tm, D_MODEL), lambda i: (i, 0)),
                  pl.BlockSpec((tm, D_SSM), lambda i: (i, 0)),
                  pl.BlockSpec((tm, D_POOL), lambda i: (i, 0)),
                  pl.BlockSpec((N_SHARD, None, gb, D_MODEL), lambda i: (0, 0, gi, 0)),
                  pl.BlockSpec((None, 1, D_SSM), lambda i: (layer, 0, 0)),
                  pl.BlockSpec((N_SHARD, None, ob, D_MODEL), lambda i: (0, 0, oi, 0)),
                  pl.BlockSpec(memory_space=pl.ANY)],
        out_specs=[pl.BlockSpec((tm, D_SSM), lambda i: (i, 0)),
                   pl.BlockSpec((tm, D_POOL), lambda i: (i, 0)),
                   pl.BlockSpec((1, D_SSM), lambda i: (0, 0)),
                   pl.BlockSpec(memory_space=pl.ANY)],
        out_shape=[jax.ShapeDtypeStruct((L, D_SSM), F32), jax.ShapeDtypeStruct((L, D_POOL), F32),
                   jax.ShapeDtypeStruct((1, D_SSM), F32),
                   jax.ShapeDtypeStruct(gbuf.shape, F32)],
        scratch_shapes=[pltpu.VMEM((N_SHARD, ob, D_MODEL), F32), pltpu.VMEM((D_SSM, D_SSM), F32),
                        pltpu.VMEM((N_SHARD, P_GLU_PAD, D_MODEL), F32)],
        input_output_aliases={6: 3},
        compiler_params=_cparams(1),
    )(dhm, yraw, ypool, wp, b_glu, wp, gbuf)


def _ssm_bwd(dyraw, u, sre, sim, layer, cpad_t, bpad_t, ar, ai, dskip):
    L = u.shape[0]
    ts = min(TS, L)
    nt = L // ts
    nq = 4
    cq = N_STATE // nq

    def body(dy_ref, u_ref, sre_ref, sim_ref, ct_ref, bt_ref, ar_ref, ai_ref, dsk_ref,
             du_ref, dcp_ref, dbp_ref, dar_ref, dai_ref, ddsk_ref, gre, gim, cr, ci, tab, accr, acci, up, dyp):
        t = pl.program_id(1)

        @pl.when(t == 0)
        def _():
            for ref in (cr, ci, accr, acci, dcp_ref, dbp_ref, ddsk_ref):
                ref[...] = jnp.zeros_like(ref)
            _scan_tables(ar_ref[...], -ai_ref[...], tab, reverse=True)

        _permute_rows(dy_ref, dyp, ts)
        _permute_rows(u_ref, up, ts)
        dy = dyp[...]
        dyb = dy.astype(BF16)
        uf = up[...]
        ub = uf.astype(BF16)
        for jj in range(4):
            cols = slice(jj * 128, (jj + 1) * 128)
            ds = _dot(dyb, ct_ref[jj])
            gre[:, cols] = ds[:, :128]
            gim[:, cols] = ds[:, 128:]
            scat = jnp.concatenate([sre_ref[:, cols], sim_ref[:, cols]], axis=1).astype(BF16)
            dcp_ref[jj] += _dot_tn(scat, dyb)

        n_blk = ts // SCAN_BLOCK
        shp = (SUBLANES, SCAN_LANES)
        last_row = lax.broadcasted_iota(jnp.int32, shp, 0) == SUBLANES - 1
        for cc in range(cq // SCAN_LANES):
            cols = slice(cc * SCAN_LANES, (cc + 1) * SCAN_LANES)

            def block(i, carry, cols=cols):
                c_r, c_i, a_r, a_i = carry
                base = pl.multiple_of((n_blk - 1 - i) * SCAN_BLOCK, SCAN_BLOCK)
                rows = lambda tau: pl.ds(base + SUBLANES * tau, SUBLANES)
                m_r, m_i = tab[0, :, cols], tab[1, :, cols]
                ys = [None] * SUBLANES
                ys[SUBLANES - 1] = (gre[rows(SUBLANES - 1), cols], gim[rows(SUBLANES - 1), cols])
                for tau in reversed(range(SUBLANES - 1)):
                    ys[tau] = _cmac(gre[rows(tau), cols], gim[rows(tau), cols], m_r, m_i, *ys[tau + 1])
                tr, ti = _chain_segments(*ys[0], c_r, c_i, tab, cols, reverse=True)
                in_r = jnp.where(last_row, c_r, pltpu.roll(tr, SUBLANES - 1, 0))
                in_i = jnp.where(last_row, c_i, pltpu.roll(ti, SUBLANES - 1, 0))
                gs = [_cmac(*ys[tau], tab[10 + 2 * tau, :, cols], tab[11 + 2 * tau, :, cols], in_r, in_i)
                      for tau in range(SUBLANES)]
                for tau in range(SUBLANES):
                    gre[rows(tau), cols] = gs[tau][0]
                    gim[rows(tau), cols] = gs[tau][1]
                    if tau < SUBLANES - 1:
                        nr, ni = gs[tau + 1]
                    else:
                        nr = jnp.where(last_row, c_r, pltpu.roll(gs[0][0], SUBLANES - 1, 0))
                        ni = jnp.where(last_row, c_i, pltpu.roll(gs[0][1], SUBLANES - 1, 0))
                    sr, si = sre_ref[rows(tau), cols], sim_ref[rows(tau), cols]
                    a_r = a_r + sr * nr + si * ni
                    a_i = a_i + sr * ni - si * nr
                return (jnp.broadcast_to(tr[:1, :], shp), jnp.broadcast_to(ti[:1, :], shp), a_r, a_i)

            c_r, c_i, a_r, a_i = lax.fori_loop(
                0, n_blk, block, (cr[:, cols], ci[:, cols], accr[:, cols], acci[:, cols]), unroll=2)
            cr[:, cols] = c_r
            ci[:, cols] = c_i
            accr[:, cols] = a_r
            acci[:, cols] = a_i

        acc = dsk_ref[...] * dy
        for jj in range(4):
            cols = slice(jj * 128, (jj + 1) * 128)
            gcat = jnp.concatenate([gre[:, cols], gim[:, cols]], axis=1).astype(BF16)
            acc = acc + _dot(gcat, bt_ref[jj])
            dbp_ref[jj] += _dot_tn(ub, gcat)
        ddsk_ref[...] += jnp.sum(dy * uf, axis=0, keepdims=True)
        dyp[...] = acc
        _permute_rows(dyp, du_ref, ts)

        @pl.when(t == nt - 1)
        def _():
            dar_ref[...] = jnp.sum(accr[...], axis=0, keepdims=True)
            dai_ref[...] = jnp.sum(acci[...], axis=0, keepdims=True)

    f32_scr = lambda *s: pltpu.VMEM(s, F32)
    return pl.pallas_call(
        body, name="ssm_bwd", grid=(nq, nt),
        in_specs=[pl.BlockSpec((ts, 128), lambda q, t: (nt - 1 - t, q)),
                  pl.BlockSpec((ts, 128), lambda q, t: (nt - 1 - t, 4 + q)),
                  pl.BlockSpec((ts, cq), lambda q, t: (nt - 1 - t, q)),
                  pl.BlockSpec((ts, cq), lambda q, t: (nt - 1 - t, q)),
                  pl.BlockSpec((None, 4, 128, 256), lambda q, t: (layer, q, 0, 0)),
                  pl.BlockSpec((None, 4, 256, 128), lambda q, t: (layer, q, 0, 0)),
                  pl.BlockSpec((None, 1, cq), lambda q, t: (layer, 0, q)),
                  pl.BlockSpec((None, 1, cq), lambda q, t: (layer, 0, q)),
                  pl.BlockSpec((None, 1, 128), lambda q, t: (layer, 0, q))],
        out_specs=[pl.BlockSpec((ts, 128), lambda q, t: (nt - 1 - t, q)),
                   pl.BlockSpec((4, 256, 128), lambda q, t: (q, 0, 0)),
                   pl.BlockSpec((4, 128, 256), lambda q, t: (q, 0, 0)),
                   pl.BlockSpec((1, cq), lambda q, t: (0, q)),
                   pl.BlockSpec((1, cq), lambda q, t: (0, q)),
                   pl.BlockSpec((1, 128), lambda q, t: (0, q))],
        out_shape=[jax.ShapeDtypeStruct((L, D_SSM), F32),
                   jax.ShapeDtypeStruct((N_PAIRS, 256, 128), F32), jax.ShapeDtypeStruct((N_PAIRS, 128, 256), F32),
                   jax.ShapeDtypeStruct((1, N_STATE), F32), jax.ShapeDtypeStruct((1, N_STATE), F32),
                   jax.ShapeDtypeStruct((1, D_SSM), F32)],
        scratch_shapes=[f32_scr(ts, cq), f32_scr(ts, cq), f32_scr(SUBLANES, cq), f32_scr(SUBLANES, cq),
                        f32_scr(N_SCAN_TABLES, SUBLANES, cq), f32_scr(SUBLANES, cq), f32_scr(SUBLANES, cq),
                        f32_scr(ts, 128), f32_scr(ts, 128)],
        compiler_params=_cparams(2),
    )(dyraw, u, sre, sim, cpad_t, bpad_t, ar, ai, dskip)


def _pool_bwd(dyp, u, layer, w_pool, scale):
    L = u.shape[0]
    tm = min(TM, L)
    nt = L // tm
    halo_per_tile = tm // POOL_HALO

    def body(dyp_ref, u_ref, halo_ref, wp_ref, sc_ref, du_ref, dwp_ref, dsc_ref, carry):
        i = pl.program_id(0)
        tile = nt - 1 - i

        @pl.when(i == 0)
        def _():
            carry[...] = jnp.zeros_like(carry)
            dwp_ref[...] = jnp.zeros_like(dwp_ref)
            dsc_ref[...] = jnp.zeros_like(dsc_ref)

        up = u_ref[...]
        halo = jnp.where(tile > 0, halo_ref[...], jnp.zeros_like(halo_ref))
        diffs = _pool_diff(jnp.concatenate([halo, up], axis=0), tile * tm, tm)
        rows = tile * tm + lax.broadcasted_iota(jnp.int32, (tm, 1), 0)
        n_ext = tm + POOL_HALO
        for gi, w in enumerate(POOL_WINDOWS):
            cols = slice(gi * POOL_GROUP, (gi + 1) * POOL_GROUP)
            db = diffs[gi].astype(BF16)
            dyp = dyp_ref[:, cols]
            dsc_ref[:, cols] += jnp.sum(dyp * _dot(db, wp_ref[gi]), axis=0, keepdims=True)
            dp = (dyp * sc_ref[:, cols]).astype(BF16)
            ddiff = _dot_nt(dp, wp_ref[gi])
            dwp_ref[gi] += _dot_tn(db, dp)
            e = ddiff * (1.0 / jnp.minimum(rows + 1, w).astype(F32))
            s = jnp.concatenate([e, carry[:, cols]], axis=0)
            k = 1
            while k < w:
                s = s + pltpu.roll(s, n_ext - k, 0)
                k *= 2
            du_ref[:, cols] = s[:tm, :] - ddiff
            carry[:, cols] = e[:POOL_HALO, :]

    return pl.pallas_call(
        body, name="pool_bwd", grid=(nt,),
        in_specs=[pl.BlockSpec((tm, D_POOL), lambda i: (nt - 1 - i, 0)),
                  pl.BlockSpec((tm, D_POOL), lambda i: (nt - 1 - i, 0)),
                  pl.BlockSpec((POOL_HALO, D_POOL), lambda i: (jnp.maximum((nt - 1 - i) * halo_per_tile - 1, 0), 0)),
                  pl.BlockSpec((None, 4, POOL_GROUP, POOL_GROUP), lambda i: (layer, 0, 0, 0)),
                  pl.BlockSpec((None, 1, D_POOL), lambda i: (layer, 0, 0))],
        out_specs=[pl.BlockSpec((tm, D_POOL), lambda i: (nt - 1 - i, 0)),
                   pl.BlockSpec((4, POOL_GROUP, POOL_GROUP), lambda i: (0, 0, 0)),
                   pl.BlockSpec((1, D_POOL), lambda i: (0, 0))],
        out_shape=[jax.ShapeDtypeStruct((L, D_POOL), F32),
                   jax.ShapeDtypeStruct((4, POOL_GROUP, POOL_GROUP), F32),
                   jax.ShapeDtypeStruct((1, D_POOL), F32)],
        scratch_shapes=[pltpu.VMEM((POOL_HALO, D_POOL), F32)],
        compiler_params=_cparams(1),
    )(dyp, u, u, w_pool, scale)


def _mix_in_bwd(dup, dus, h, dhm, g1, wp, layer, gbuf):
    L = h.shape[0]
    tm = min(TM, L)
    n_steps = L // tm
    blk, idx = P_IN_BLK

    def body(dup_ref, dus_ref, h_ref, dhm_ref, g_ref, w_ref, g1_in, dh_ref, dg_ref, g1_ref, dwin):
        i = pl.program_id(0)

        @pl.when(i == 0)
        def _():
            dg_ref[...] = jnp.zeros_like(dg_ref)
            dwin[...] = jnp.zeros_like(dwin)

        du = jnp.concatenate([dup_ref[...], dus_ref[...]], axis=1).astype(BF16)
        dn1 = _dot_nt(du, w_ref[...].reshape(D_MODEL, D_MODEL))
        xhat, r = _rms_hat(h_ref[...])
        g = g_ref[...]
        n1 = (xhat * g).astype(BF16)
        dwin[...] += _dot_tn(n1, du).reshape(N_SHARD, blk, D_MODEL)
        dg_ref[...] += jnp.sum(dn1 * xhat, axis=0, keepdims=True)
        dh_ref[...] = dhm_ref[...] + _rms_bwd(dn1 * g, xhat, r)

        @pl.when(i == n_steps - 1)
        def _():
            pltpu.sync_copy(dwin, g1_ref.at[0, :, pl.ds(blk * idx, blk), :])

    row_spec = pl.BlockSpec((tm, D_MODEL), lambda i: (i, 0))
    half_spec = pl.BlockSpec((tm, D_POOL), lambda i: (i, 0))
    return pl.pallas_call(
        body, name="mix_in_bwd", grid=(n_steps,),
        in_specs=[half_spec, half_spec, row_spec, row_spec,
                  pl.BlockSpec((None, 1, D_MODEL), lambda i: (layer, 0, 0)),
                  pl.BlockSpec((N_SHARD, None, blk, D_MODEL), lambda i: (0, 0, idx, 0)),
                  pl.BlockSpec(memory_space=pl.ANY)],
        out_specs=[row_spec, pl.BlockSpec((1, D_MODEL), lambda i: (0, 0)), pl.BlockSpec(memory_space=pl.ANY)],
        out_shape=[jax.ShapeDtypeStruct((L, D_MODEL), F32), jax.ShapeDtypeStruct((1, D_MODEL), F32),
                   jax.ShapeDtypeStruct(gbuf.shape, F32)],
        scratch_shapes=[pltpu.VMEM((N_SHARD, blk, D_MODEL), F32)],
        input_output_aliases={6: 2},
        compiler_params=_cparams(1),
    )(dup, dus, h, dhm, g1, wp, gbuf)


def _disc_math(lr, li, ldt, br_t, bi_t):
    dt = jnp.exp(ldt)
    mag = jnp.exp(lr * dt)
    ang = li * dt
    ar = mag * jnp.cos(ang)
    ai = mag * jnp.sin(ang)
    den = lr * lr + li * li
    nr, ni = ar - 1.0, ai
    cr = (nr * lr + ni * li) / den
    ci = (ni * lr - nr * li) / den
    return ar, ai, cr * br_t - ci * bi_t, cr * bi_t + ci * br_t


def _disc_fwd(lr, li, ldt, br_t, bi_t):
    def body(lr_ref, li_ref, ldt_ref, br_ref, bi_ref, ar_ref, ai_ref, bbr_ref, bbi_ref):
        ar, ai, bbr, bbi = _disc_math(lr_ref[...], li_ref[...], ldt_ref[...], br_ref[...], bi_ref[...])
        ar_ref[...] = ar
        ai_ref[...] = ai
        bbr_ref[...] = bbr
        bbi_ref[...] = bbi

    shapes = [jax.ShapeDtypeStruct(a.shape, F32) for a in (lr, li, br_t, bi_t)]
    return pl.pallas_call(body, name="ssm_disc_fwd", out_shape=shapes,
                          compiler_params=pltpu.CompilerParams(vmem_limit_bytes=VMEM_LIMIT))(lr, li, ldt, br_t, bi_t)


def _disc_bwd(lr, li, ldt, br_t, bi_t, dar, dai, dbbr, dbbi):
    def body(lr_ref, li_ref, ldt_ref, br_ref, bi_ref, dar_ref, dai_ref, dbbr_ref, dbbi_ref,
             dlr_ref, dli_ref, dldt_ref, dbr_ref, dbi_ref):
        prim = (lr_ref[...], li_ref[...], ldt_ref[...], br_ref[...], bi_ref[...])
        _, pullback = jax.vjp(_disc_math, *prim)
        dlr, dli, dldt, dbr, dbi = pullback((dar_ref[...], dai_ref[...], dbbr_ref[...], dbbi_ref[...]))
        dlr_ref[...] = dlr
        dli_ref[...] = dli
        dldt_ref[...] = dldt
        dbr_ref[...] = dbr
        dbi_ref[...] = dbi

    shapes = [jax.ShapeDtypeStruct(a.shape, F32) for a in (lr, li, ldt, br_t, bi_t)]
    return pl.pallas_call(body, name="ssm_disc_bwd", out_shape=shapes,
                          compiler_params=pltpu.CompilerParams(vmem_limit_bytes=VMEM_LIMIT))(
        lr, li, ldt, br_t, bi_t, dar, dai, dbbr, dbbi)


def _pad_pairs(m_re, m_im):
    def blocks(m):
        v = m.transpose(0, 2, 1).reshape(N_PAIRS, 2, SSM_GROUP, SSM_STATE)
        return jnp.einsum("ab,jahp->jahbp", jnp.eye(2, dtype=m.dtype), v).reshape(N_PAIRS, 32, 128)
    both = jnp.concatenate([blocks(m_re), blocks(m_im)], axis=-1)
    place = jax.nn.one_hot(jnp.arange(N_PAIRS) % 4, 4, dtype=both.dtype)
    return jnp.einsum("jk,jrc->jkrc", place, both).reshape(N_PAIRS, 128, 256)


def _unpad_pairs(x):
    place = jax.nn.one_hot(jnp.arange(N_PAIRS) % 4, 4, dtype=x.dtype)
    both = jnp.einsum("jk,jkrc->jrc", place, x.reshape(N_PAIRS, 4, 32, 256))

    def unblock(v):
        v = v.reshape(N_PAIRS, 2, SSM_GROUP, 2, SSM_STATE)
        d = jnp.einsum("ab,jahbp->jahp", jnp.eye(2, dtype=x.dtype), v)
        return d.reshape(N_SSM_GROUPS, SSM_GROUP, SSM_STATE).transpose(0, 2, 1)
    return unblock(both[..., :128]), unblock(both[..., 128:])


def _adamw_math(w, g, m, v):
    m = ADAM_B1 * m + (1.0 - ADAM_B1) * g
    v = ADAM_B2 * v + (1.0 - ADAM_B2) * (g * g)
    m_hat = m / (1.0 - ADAM_B1 ** ADAM_STEP)
    v_hat = v / (1.0 - ADAM_B2 ** ADAM_STEP)
    delta = -ADAM_LR * (m_hat / (jnp.sqrt(v_hat) + ADAM_EPS) + ADAM_WD * w)
    return delta, m, v


def _adamw(name, layer, w, m, v, gbuf, g_block, g_row0, row_tile, outs=None, after=(), glu=False):
    nl, r, c = w.shape
    n_tiles = r // row_tile
    g_rows, g_cols = g_block
    g_tile = g_rows // n_tiles
    g_off = g_row0 // g_tile
    if outs is None:
        outs = [lax.empty(w.shape, F32) for _ in range(4)]

    def body(w_ref, m_ref, v_ref, g_ref, *rest):
        go_ref, d_ref, mo_ref, vo_ref = rest[-4:]
        g = g_ref[...]
        if glu:
            g = jnp.concatenate([g[:, :D_SSM], g[:, D_SSM:]], axis=0)
        delta, mn, vn = _adamw_math(w_ref[...], g, m_ref[...], v_ref[...])
        go_ref[...] = g
        d_ref[...] = delta
        mo_ref[...] = mn
        vo_ref[...] = vn

    w_spec = pl.BlockSpec((None, row_tile, c), lambda j: (layer, j, 0))
    shape = jax.ShapeDtypeStruct(w.shape, F32)
    return pl.pallas_call(
        body, name=name, grid=(n_tiles,),
        in_specs=[w_spec, w_spec, w_spec, pl.BlockSpec((None, g_tile, g_cols), lambda j: (0, g_off + j, 0))]
        + [_ANY] * (4 + len(after)),
        out_specs=[w_spec] * 4,
        out_shape=[shape] * 4,
        input_output_aliases={4: 0, 5: 1, 6: 2, 7: 3},
        compiler_params=_cparams(1),
    )(w, m, v, gbuf, *outs, *after)


def _pack_weights(ids, layer, w_in, w_glu, w_out, w_down, w_gate_t, w_up_t):
    gb, gi = P_GLU_BLK
    ib, ii = P_IN_BLK
    ob, oi = P_OUT_BLK

    def body(ids_ref, in_ref, glu_ref, out_ref, dn_ref, gate_ref, up_ref, p_ref):
        p_ref[0:FF_SHARD, :] = dn_ref[...].astype(BF16)
        p_ref[FF_SHARD:2 * FF_SHARD, :] = gate_ref[...].astype(BF16)
        p_ref[2 * FF_SHARD:P_FF_ROWS, :] = up_ref[...].astype(BF16)
        g = glu_ref[...]
        p_ref[gb * gi:gb * (gi + 1), :] = jnp.concatenate([g[:gb, :], g[gb:, :]], axis=1).astype(BF16)
        p_ref[gb * (gi + 1):ib * ii, :] = jnp.zeros((P_GLU_PAD - gb, D_MODEL), BF16)
        p_ref[ib * ii:ib * (ii + 1), :] = in_ref[...].astype(BF16)
        p_ref[ob * oi:ob * (oi + 1), :] = out_ref[...].astype(BF16)

    def spec(a):
        return pl.BlockSpec((None,) + a.shape[1:], lambda i, ids_ref: (layer, 0, 0))

    ins = (w_in, w_glu, w_out, w_down, w_gate_t, w_up_t)
    grid_spec = pltpu.PrefetchScalarGridSpec(
        num_scalar_prefetch=1, grid=(1,),
        in_specs=[spec(a) for a in ins],
        out_specs=pl.BlockSpec((None, None, P_ROWS, D_MODEL), lambda i, ids_ref: (ids_ref[1], 0, 0, 0)))
    return pl.pallas_call(
        body, name="pack_weights", grid_spec=grid_spec,
        out_shape=jax.ShapeDtypeStruct((N_SHARD, 1, P_ROWS, D_MODEL), BF16),
        compiler_params=_cparams(1),
    )(ids, *ins)


MESH = pl.DeviceIdType.MESH
_ANY = pl.BlockSpec(memory_space=pl.ANY)
P_HALF = P_ROWS // 2
RS_ROW_TILE = 352


def _mesh_pos():
    return lax.axis_index("x"), lax.axis_index("y"), lax.axis_index("c")


def _other_chips(x, y):
    return [(1 - x, y), (x, 1 - y), (1 - x, 1 - y)]


def _remote(src, dst, send_sems, recv_sems, n, to):
    return pltpu.make_async_remote_copy(src_ref=src, dst_ref=dst, send_sem=send_sems.at[n],
                                        recv_sem=recv_sems.at[n], device_id=to, device_id_type=MESH)


_HBM = pl.BlockSpec(memory_space=pltpu.HBM)
_SEM = pl.BlockSpec(memory_space=pltpu.SEMAPHORE)
_EFFECT = pltpu.CompilerParams(has_side_effects=pltpu.SideEffectType.DATAFLOW_SIDE_EFFECTING)
_TOKEN = jax.ShapeDtypeStruct((8, 128), F32)


def _in_hbm(a):
    return pltpu.with_memory_space_constraint(a, pltpu.HBM)


def _ag_start(name, wp, after):
    def body(w_ref, after_ref, send_sems, recv_sems, w_thru, token):
        x, y, c = _mesh_pos()
        mine = w_ref.at[2 * x + y, :, pl.ds(c * P_HALF, P_HALF), :]
        for j, (px, py) in enumerate(_other_chips(x, y)):
            _remote(mine, mine, send_sems, recv_sems, j, (px, py, c)).start()
        token[...] = jnp.zeros_like(token)

    return pl.pallas_call(
        body, name=name,
        out_shape=(pltpu.SemaphoreType.DMA((3,)), pltpu.SemaphoreType.DMA((3,)), pltpu.HBM(wp.shape, wp.dtype), _TOKEN),
        in_specs=(_HBM, _ANY), out_specs=(_SEM, _SEM, _HBM, pl.BlockSpec(memory_space=pltpu.VMEM)),
        input_output_aliases={0: 2}, compiler_params=_EFFECT,
    )(_in_hbm(wp), after)


def _ag_wait(name, send_sems, recv_sems, wp, after):
    def body(w_ref, send_sems, recv_sems, *rest):
        x, y, c = _mesh_pos()
        mine = w_ref.at[2 * x + y, :, pl.ds(c * P_HALF, P_HALF), :]
        for j, (px, py) in enumerate(_other_chips(x, y)):
            landed = w_ref.at[2 * px + py, :, pl.ds(c * P_HALF, P_HALF), :]
            cp = _remote(mine, landed, send_sems, recv_sems, j, (px, py, c))
            cp.wait_send()
            cp.wait_recv()

    return pl.pallas_call(
        body, name=name, out_shape=pltpu.HBM(wp.shape, wp.dtype),
        in_specs=(_HBM, _SEM, _SEM) + (_ANY,) * len(after), out_specs=_HBM,
        input_output_aliases={0: 0}, compiler_params=_EFFECT,
    )(wp, send_sems, recv_sems, *after)


def _ag_forward(wp):
    def body(w_in, o, send_sems, recv_sems):
        x, y, c = _mesh_pos()
        sib = (x, y, 1 - c)
        chips = _other_chips(x, y)
        sends = []
        for j, (px, py) in enumerate(chips):
            landed = o.at[2 * px + py, :, pl.ds(c * P_HALF, P_HALF), :]
            cp = _remote(landed, landed, send_sems, recv_sems, j, sib)
            cp.start()
            sends.append(cp)
        for j, (px, py) in enumerate(chips):
            passed = o.at[2 * px + py, :, pl.ds((1 - c) * P_HALF, P_HALF), :]
            _remote(passed, passed, send_sems, recv_sems, j, sib).wait_recv()
        for cp in sends:
            cp.wait_send()

    return pl.pallas_call(
        body, name="ag_forward",
        in_specs=[_ANY], out_specs=_ANY,
        out_shape=jax.ShapeDtypeStruct(wp.shape, wp.dtype),
        scratch_shapes=[pltpu.SemaphoreType.DMA((3,)), pltpu.SemaphoreType.DMA((3,))],
        input_output_aliases={0: 0},
    )(wp)


def _rs_chips_start(name, t):
    nl = t.shape[0]

    def body(t_ref, land_ref, send_sems, recv_sems, t_thru, land_thru, token):
        x, y, c = _mesh_pos()
        for j, (px, py) in enumerate(_other_chips(x, y)):
            _remote(t_ref.at[:, 2 * px + py], land_ref.at[j], send_sems, recv_sems, j, (px, py, c)).start()
        token[...] = jnp.zeros_like(token)

    land = lax.empty((3, nl, P_HALF, D_MODEL), BF16)
    return pl.pallas_call(
        body, name=name,
        out_shape=(pltpu.SemaphoreType.DMA((3,)), pltpu.SemaphoreType.DMA((3,)), pltpu.HBM(t.shape, t.dtype),
                   pltpu.HBM(land.shape, land.dtype), _TOKEN),
        in_specs=(_HBM, _HBM), out_specs=(_SEM, _SEM, _HBM, _HBM, pl.BlockSpec(memory_space=pltpu.VMEM)),
        input_output_aliases={0: 2, 1: 3}, compiler_params=_EFFECT,
    )(_in_hbm(t), _in_hbm(land))


def _rs_chips_wait(name, send_sems, recv_sems, t, land, after):
    def body(t_ref, land_ref, send_sems, recv_sems, *rest):
        x, y, c = _mesh_pos()
        for j, (px, py) in enumerate(_other_chips(x, y)):
            cp = _remote(t_ref.at[:, 2 * px + py], land_ref.at[j], send_sems, recv_sems, j, (px, py, c))
            cp.wait_send()
            cp.wait_recv()

    return pl.pallas_call(
        body, name=name, out_shape=(pltpu.HBM(t.shape, t.dtype), pltpu.HBM(land.shape, land.dtype)),
        in_specs=(_HBM, _HBM, _SEM, _SEM) + (_ANY,) * len(after), out_specs=(_HBM, _HBM),
        input_output_aliases={0: 0, 1: 1}, compiler_params=_EFFECT,
    )(t, land, send_sems, recv_sems, *after)[1]


def _rs_sibling_start(name, g):
    nl = g.shape[0]

    def body(g_ref, land_ref, send_sems, recv_sems, g_thru, land_thru, token):
        x, y, c = _mesh_pos()
        _remote(g_ref.at[:, :, pl.ds((1 - c) * P_HALF, P_HALF), :], land_ref, send_sems, recv_sems, 0,
                (x, y, 1 - c)).start()
        token[...] = jnp.zeros_like(token)

    land = lax.empty((nl, N_SHARD, P_HALF, D_MODEL), F32)
    return pl.pallas_call(
        body, name=name,
        out_shape=(pltpu.SemaphoreType.DMA((1,)), pltpu.SemaphoreType.DMA((1,)), pltpu.HBM(g.shape, g.dtype),
                   pltpu.HBM(land.shape, land.dtype), _TOKEN),
        in_specs=(_HBM, _HBM), out_specs=(_SEM, _SEM, _HBM, _HBM, pl.BlockSpec(memory_space=pltpu.VMEM)),
        input_output_aliases={0: 2, 1: 3}, compiler_params=_EFFECT,
    )(_in_hbm(g), _in_hbm(land))


def _rs_sibling_wait(name, send_sems, recv_sems, g, land, after):
    def body(g_ref, land_ref, send_sems, recv_sems, *rest):
        x, y, c = _mesh_pos()
        cp = _remote(g_ref.at[:, :, pl.ds((1 - c) * P_HALF, P_HALF), :], land_ref, send_sems, recv_sems, 0,
                     (x, y, 1 - c))
        cp.wait_send()
        cp.wait_recv()

    return pl.pallas_call(
        body, name=name, out_shape=(pltpu.HBM(g.shape, g.dtype), pltpu.HBM(land.shape, land.dtype)),
        in_specs=(_HBM, _HBM, _SEM, _SEM) + (_ANY,) * len(after), out_specs=(_HBM, _HBM),
        input_output_aliases={0: 0, 1: 1}, compiler_params=_EFFECT,
    )(g, land, send_sems, recv_sems, *after)


def _rs_add(name, ids, g, buf, row_tile):
    nl, _, hr, cols = buf.shape
    n_rt = hr // row_tile

    def body(ids_ref, g_ref, b_ref, own_ref, tb_ref):
        t = g_ref[...] + b_ref[...]
        tb_ref[...] = t.astype(BF16)

        @pl.when(pl.program_id(2) == ids_ref[1])
        def _():
            own_ref[...] = t

    blk = (None, None, row_tile, cols)
    grid_spec = pltpu.PrefetchScalarGridSpec(
        num_scalar_prefetch=1, grid=(nl, n_rt, N_SHARD),
        in_specs=[pl.BlockSpec(blk, lambda l, j, s, ids_ref: (l, s, ids_ref[0] * n_rt + j, 0)),
                  pl.BlockSpec(blk, lambda l, j, s, ids_ref: (l, s, j, 0))],
        out_specs=[pl.BlockSpec((None, row_tile, cols), lambda l, j, s, ids_ref: (l, j, 0)),
                   pl.BlockSpec(blk, lambda l, j, s, ids_ref: (l, s, j, 0))])
    return pl.pallas_call(
        body, name=name, grid_spec=grid_spec,
        out_shape=[jax.ShapeDtypeStruct((nl, hr, cols), F32), jax.ShapeDtypeStruct(buf.shape, BF16)],
        compiler_params=_cparams(3),
    )(ids, g, buf)


def _rs_sum(ids, layer, own, bufb, reduced, row_tile):
    _, hr, cols = own.shape
    n_rt = hr // row_tile

    def body(ids_ref, own_ref, b_ref, reduced_in, f_ref):
        f_ref[...] = ((own_ref[...] + b_ref[0].astype(F32)) + b_ref[1].astype(F32)) + b_ref[2].astype(F32)

    grid_spec = pltpu.PrefetchScalarGridSpec(
        num_scalar_prefetch=1, grid=(n_rt,),
        in_specs=[pl.BlockSpec((None, row_tile, cols), lambda j, ids_ref: (0, j, 0)),
                  pl.BlockSpec((3, None, row_tile, cols), lambda j, ids_ref: (0, 0, j, 0)),
                  pl.BlockSpec(memory_space=pl.ANY)],
        out_specs=pl.BlockSpec((None, row_tile, cols), lambda j, ids_ref: (layer, ids_ref[0] * n_rt + j, 0)))
    return pl.pallas_call(
        body, name="rs_sum", grid_spec=grid_spec,
        out_shape=jax.ShapeDtypeStruct(reduced.shape, F32),
        input_output_aliases={3: 0},
        compiler_params=_cparams(1),
    )(ids, own, bufb, reduced)


def _rs_exchange(f, layer):
    def body(f_in, o, send_sems, recv_sems):
        x, y, c = _mesh_pos()
        mine = o.at[layer, pl.ds(c * P_HALF, P_HALF), :]
        cp = _remote(mine, mine, send_sems, recv_sems, 0, (x, y, 1 - c))
        cp.start()
        cp.wait_send()
        theirs = o.at[layer, pl.ds((1 - c) * P_HALF, P_HALF), :]
        _remote(theirs, theirs, send_sems, recv_sems, 0, (x, y, 1 - c)).wait_recv()

    return pl.pallas_call(
        body, name="rs_exchange",
        in_specs=[_ANY], out_specs=_ANY,
        out_shape=jax.ShapeDtypeStruct(f.shape, F32),
        scratch_shapes=[pltpu.SemaphoreType.DMA((1,)), pltpu.SemaphoreType.DMA((1,))],
        input_output_aliases={0: 0},
    )(f)


def _small_all_reduce(s):
    n_rows = s.shape[0]
    hr = n_rows // 2
    qr = hr // N_SHARD

    def body(s_ref, o_ref, sibbuf, tbuf, qbuf, fbuf, send_sems, recv_sems):
        x, y, c = _mesh_pos()
        k = 2 * x + y
        sib = (x, y, 1 - c)
        chips = _other_chips(x, y)
        mine = pl.ds(pl.multiple_of(c * hr, SUBLANES), hr)
        theirs = pl.ds(pl.multiple_of((1 - c) * hr, SUBLANES), hr)

        def quarter(shard):
            return pl.ds(pl.multiple_of(shard * qr, SUBLANES), qr)

        first = _remote(s_ref.at[theirs], sibbuf, send_sems, recv_sems, 0, sib)
        first.start()
        first.wait()
        tbuf[...] = s_ref[mine, :] + sibbuf[...]
        cps = []
        for j, (px, py) in enumerate(chips):
            cp = _remote(tbuf.at[quarter(2 * px + py)], qbuf.at[j], send_sems, recv_sems, 1 + j, (px, py, c))
            cp.start()
            cps.append(cp)
        for cp in cps:
            cp.wait()
        fbuf[quarter(k), :] = (tbuf[quarter(k), :] + qbuf[1]) + (qbuf[0] + qbuf[2])
        cps = []
        for j, (px, py) in enumerate(chips):
            cp = _remote(fbuf.at[quarter(k)], fbuf.at[quarter(k)], send_sems, recv_sems, 4 + j, (px, py, c))
            cp.start()
            cps.append(cp)
        for j, (px, py) in enumerate(chips):
            got = fbuf.at[quarter(2 * px + py)]
            _remote(got, got, send_sems, recv_sems, 4 + j, (px, py, c)).wait_recv()
        for cp in cps:
            cp.wait_send()
        o_ref[mine, :] = fbuf[...]
        last = _remote(fbuf, o_ref.at[mine], send_sems, recv_sems, 7, sib)
        last.start()
        last.wait()

    vmem = pl.BlockSpec(memory_space=pltpu.VMEM)
    return pl.pallas_call(
        body, name="small_all_reduce",
        in_specs=[vmem], out_specs=vmem,
        out_shape=jax.ShapeDtypeStruct(s.shape, F32),
        scratch_shapes=[pltpu.VMEM((hr, D_MODEL), F32), pltpu.VMEM((hr, D_MODEL), F32),
                        pltpu.VMEM((3, qr, D_MODEL), F32), pltpu.VMEM((hr, D_MODEL), F32),
                        pltpu.SemaphoreType.DMA((8,)), pltpu.SemaphoreType.DMA((8,))],
        compiler_params=pltpu.CompilerParams(vmem_limit_bytes=VMEM_LIMIT),
    )(s)


_SMALL = ("norm_mix", "w_pool", "pool_scale", "lam_re", "lam_im", "log_dt", "b_re", "b_im", "c_re", "c_im",
          "d_skip", "b_glu", "norm_ffn", "norm_final")
_WEIGHTS = ("norm_mix", "w_in", "w_pool", "pool_scale", "lam_re", "lam_im", "log_dt", "b_re", "b_im", "c_re",
            "c_im", "d_skip", "w_glu", "b_glu", "w_out", "norm_ffn", "w_gate", "w_up", "w_down", "norm_final")


def _local_step(x, target, p, get_weights, ffn_bwd_done, put_grads):
    nl = p["norm_mix"].shape[0]

    def tied(a, token):
        return a if token is None else a + token
    n_rows = nl * N_SSM_GROUPS
    lr = p["lam_re"].reshape(n_rows, 1, SSM_STATE)
    li = p["lam_im"].reshape(n_rows, 1, SSM_STATE)
    ldt = p["log_dt"].reshape(n_rows, 1, 1)
    br_t = p["b_re"].reshape(n_rows, SSM_STATE, SSM_GROUP).transpose(0, 2, 1)
    bi_t = p["b_im"].reshape(n_rows, SSM_STATE, SSM_GROUP).transpose(0, 2, 1)
    ar, ai, bbr_t, bbi_t = _disc_fwd(lr, li, ldt, br_t, bi_t)
    ar = ar.reshape(nl, 1, N_STATE)
    ai = ai.reshape(nl, 1, N_STATE)
    bbr = bbr_t.transpose(0, 2, 1).reshape(nl, N_SSM_GROUPS, SSM_STATE, SSM_GROUP)
    bbi = bbi_t.transpose(0, 2, 1).reshape(nl, N_SSM_GROUPS, SSM_STATE, SSM_GROUP)
    w_pool = p["w_pool"].astype(BF16)
    p = dict(p)
    for n in ("norm_mix", "pool_scale", "b_glu", "norm_ffn"):
        p[n] = p[n].reshape(nl, 1, -1)
    swap = lambda a: jnp.swapaxes(a, -1, -2)
    bpad = jax.vmap(_pad_pairs)(bbr, bbi).astype(BF16)
    cpad_t = jax.vmap(_pad_pairs)(swap(p["c_re"]), -swap(p["c_im"])).astype(BF16)
    bpad_t, cpad = swap(bpad), swap(cpad_t)
    dskip = p["d_skip"].reshape(nl, 1, D_SSM)

    layers = []
    h = x
    for l in range(nl):
        wp = get_weights(l, [h] if l else [h, bpad, cpad, bpad_t, cpad_t, ar, ai])
        u, ypool = _mix_in_fwd(h, p["norm_mix"], wp, l, w_pool, p["pool_scale"])
        sre, sim, yraw = _ssm_fwd(u, l, bpad, cpad, ar, ai, dskip)
        hm = _mix_out_fwd(yraw, ypool, h, wp, l, p["b_glu"])
        h_next, n2, act_s, fgate_s, fup_s = _ffn_fwd(hm, p["norm_ffn"], wp, l)
        layers.append(dict(h=h, u=u, ypool=ypool, sre=sre, sim=sim, yraw=yraw, hm=hm, n2=n2, act_s=act_s, wp=wp,
                           fgate_s=fgate_s, fup_s=fup_s))
        h = h_next

    dh, loss, d_norm_final = _final_fwd_bwd(h, p["norm_final"].reshape(1, D_MODEL), target)

    raw = {n: [None] * nl for n in ("dg1", "dwp", "dsc", "dcp", "dbp", "ddsk", "db_glu", "dg2", "dar", "dai")}
    token = None
    for l in reversed(range(nl)):
        s = layers[l]
        wp = s["wp"]
        g1 = lax.empty((1, N_SHARD, P_ROWS, D_MODEL), F32)
        dhm, dg2, dgate_s, dup_s, dhb = _ffn_bwd_act(dh, s["hm"], tied(p["norm_ffn"], token), s["fgate_s"],
                                                      s["fup_s"], wp, l)
        g1 = _ffn_bwd_w(s["n2"], dgate_s, dup_s, s["act_s"], dhb, g1)
        token = ffn_bwd_done(l, [g1])
        dyraw, dyp, db_glu, g1 = _mix_out_bwd(dhm, s["yraw"], s["ypool"], wp, l, tied(p["b_glu"], token), g1)
        dus, dcp, dbp, dar, dai, ddsk = _ssm_bwd(dyraw, s["u"], s["sre"], s["sim"], l, cpad_t, bpad_t, ar, ai, dskip)
        dup, dwp, dsc = _pool_bwd(dyp, s["u"], l, w_pool, p["pool_scale"])
        dh, dg1, g1 = _mix_in_bwd(dup, dus, s["h"], dhm, p["norm_mix"], wp, l, g1)
        token = put_grads(l, g1)
        for n, a in (("dg1", dg1), ("dwp", dwp), ("dsc", dsc), ("dcp", dcp), ("dbp", dbp), ("ddsk", ddsk),
                     ("db_glu", db_glu), ("dg2", dg2), ("dar", dar), ("dai", dai)):
            raw[n][l] = a

    st = {n: jnp.stack(v) for n, v in raw.items()}
    dc_re, dc_im = jax.vmap(_unpad_pairs)(swap(st["dcp"]))
    dbbr, dbbi = jax.vmap(_unpad_pairs)(st["dbp"])
    rows = lambda a: a.reshape((n_rows,) + a.shape[2:])
    dlr, dli, dldt, dbr_t, dbi_t = _disc_bwd(lr, li, ldt, br_t, bi_t, st["dar"].reshape(n_rows, 1, SSM_STATE),
                                              st["dai"].reshape(n_rows, 1, SSM_STATE), rows(swap(dbbr)),
                                              rows(swap(dbbi)))
    small = {"norm_mix": st["dg1"][:, 0], "w_pool": st["dwp"], "pool_scale": st["dsc"][:, 0], "c_re": swap(dc_re),
             "c_im": -swap(dc_im), "d_skip": st["ddsk"].reshape(nl, N_SSM_GROUPS, SSM_GROUP),
             "b_glu": st["db_glu"][:, 0], "norm_ffn": st["dg2"][:, 0]}
    small["lam_re"] = dlr.reshape(nl, N_SSM_GROUPS, SSM_STATE)
    small["lam_im"] = dli.reshape(nl, N_SSM_GROUPS, SSM_STATE)
    small["log_dt"] = dldt.reshape(nl, N_SSM_GROUPS)
    small["b_re"] = dbr_t.reshape(nl, N_SSM_GROUPS, SSM_GROUP, SSM_STATE)
    small["b_im"] = dbi_t.reshape(nl, N_SSM_GROUPS, SSM_GROUP, SSM_STATE)
    small["d_skip"] = small["d_skip"].transpose(_SMALL_VIEW["d_skip"])
    small["norm_final"] = d_norm_final
    return loss, dh, small


_SMALL_VIEW = {"b_re": (0, 1, 3, 2), "b_im": (0, 1, 3, 2), "d_skip": (0, 2, 1)}
_SMALL_GROUPS = (("b_re", "b_im"), ("c_re", "c_im"), ("lam_re", "lam_im"), ("norm_mix", "norm_ffn"),
                 ("pool_scale", "b_glu"), ("w_pool",), ("log_dt",), ("d_skip",), ("norm_final",))


def _view(n, a):
    a = a.transpose(_SMALL_VIEW[n]) if n in _SMALL_VIEW else a
    return a[None] if a.ndim == 1 else a


def _unview(n, a, shape):
    a = a.reshape(shape) if len(shape) == 1 else a
    return a.transpose(_SMALL_VIEW[n]) if n in _SMALL_VIEW else a


def _flatten_small(views):
    flat = jnp.concatenate([views[n].reshape(-1) for n in _SMALL])
    n_rows = -(-flat.shape[0] // (64 * D_MODEL)) * 64
    return jnp.pad(flat, (0, n_rows * D_MODEL - flat.shape[0])).reshape(n_rows, D_MODEL)


def _split_small(flat, like):
    flat = flat.reshape(-1)
    out, at = {}, 0
    for n in _SMALL:
        size = like[n].size
        out[n] = flat[at:at + size].reshape(like[n].shape)
        at += size
    return out


def _adamw_small(name, ws, ms, vs, gs):
    k = len(ws)

    def body(*refs):
        ins, outs = refs[:4 * k], refs[4 * k:]
        for i in range(k):
            w, m, v, g = (ins[j * k + i][...] for j in range(4))
            delta, mn, vn = _adamw_math(w, g, m, v)
            outs[i][...] = delta
            outs[k + i][...] = mn
            outs[2 * k + i][...] = vn

    shapes = [jax.ShapeDtypeStruct(w.shape, F32) for w in ws] * 3
    outs = pl.pallas_call(body, name=name, out_shape=shapes,
                          compiler_params=pltpu.CompilerParams(vmem_limit_bytes=VMEM_LIMIT))(*ws, *ms, *vs, *gs)
    return outs[:k], outs[k:2 * k], outs[2 * k:]


def kernel(x, norm_mix, w_in, w_pool, pool_scale, lam_re, lam_im, log_dt, b_re, b_im, c_re, c_im, d_skip, w_glu, b_glu, w_out, norm_ffn, w_gate, w_up, w_down, norm_final, loss_target, m_norm_mix, m_w_in, m_w_pool, m_pool_scale, m_lam_re, m_lam_im, m_log_dt, m_b_re, m_b_im, m_c_re, m_c_im, m_d_skip, m_w_glu, m_b_glu, m_w_out, m_norm_ffn, m_w_gate, m_w_up, m_w_down, m_norm_final, v_norm_mix, v_w_in, v_w_pool, v_pool_scale, v_lam_re, v_lam_im, v_log_dt, v_b_re, v_b_im, v_c_re, v_c_im, v_d_skip, v_w_glu, v_b_glu, v_w_out, v_norm_ffn, v_w_gate, v_w_up, v_w_down, v_norm_final):
    given = dict(locals())
    w = {n: given[n] for n in _WEIGHTS}
    m = {n: given["m_" + n] for n in _WEIGHTS}
    v = {n: given["v_" + n] for n in _WEIGHTS}
    ids = jnp.stack([lax.axis_index("c"), 2 * lax.axis_index("x") + lax.axis_index("y")]).astype(jnp.int32)

    t_names = ("w_gate", "w_up")
    tr = lambda a: a.transpose(0, 2, 1)
    for d in (w, m, v):
        d.update({n: tr(d[n]) for n in t_names})

    nl = norm_mix.shape[0]
    packed = [_pack_weights(ids, l, w["w_in"], w["w_glu"], w["w_out"], w["w_down"], w["w_gate"], w["w_up"])
              for l in range(nl)]
    started, last = {}, ids
    for l in range(nl):
        started[l] = _ag_start(f"ag_start_{l}", packed[l], last)
        last = started[l][3]
    views = [{n: _view(n, d[n]) for n in _SMALL} for d in (w, m, v)]

    def get_weights(l, after):
        send_sems, recv_sems, buf, _ = started[l]
        after = after + ([last] if l == 0 else [])
        return _ag_forward(_ag_wait(f"ag_wait_{l}", send_sems, recv_sems, buf, after))

    to_sibling, to_chips, reduced = {}, {}, {}

    def put_grads(l, g):
        to_sibling[l] = _rs_sibling_start(f"rs_sibling_start_{l}", g)
        token = to_sibling[l][4]
        if l + 1 in to_chips:
            finish(l + 1, [token])
        return token[:1, :1]

    def ffn_bwd_done(l, after):
        return send_to_chips(l + 1, after) if l + 1 in to_sibling else None

    def send_to_chips(l, after):
        send_sems, recv_sems, g, land, _ = to_sibling.pop(l)
        g, land = _rs_sibling_wait(f"rs_sibling_wait_{l}", send_sems, recv_sems, g, land, after)
        own, t = _rs_add("rs_add", ids, g, land, RS_ROW_TILE)
        send_sems, recv_sems, t, land, token = _rs_chips_start(f"rs_chips_start_{l}", t)
        to_chips[l] = (send_sems, recv_sems, t, land, own)
        return token[:1, :1]

    def finish(l, after):
        send_sems, recv_sems, t, land, own = to_chips.pop(l)
        land = _rs_chips_wait(f"rs_chips_wait_{l}", send_sems, recv_sems, t, land, after)
        shard = lax.empty((1, P_ROWS, D_MODEL), F32)
        reduced[l] = _rs_exchange(_rs_sum(ids, 0, own, land, shard, RS_ROW_TILE), 0)

    loss, grad_x, small = _local_step(x[0], loss_target[0], {n: w[n] for n in _SMALL}, get_weights, ffn_bwd_done,
                                      put_grads)
    loss = lax.psum(loss[0, 0], ("x", "y", "c"))
    token = send_to_chips(0, [small["norm_final"]])

    big = (("w_in", P_IN_BLK, 64, False), ("w_out", P_OUT_BLK, 64, False), ("w_down", P_WD_BLK, 88, False),
           ("w_gate", P_WG_BLK, 88, False), ("w_up", P_WU_BLK, 88, False), ("w_glu", P_GLU_BLK, 128, True))
    res = {n: None for n, *_ in big}

    def adamw_layer(l, after):
        for n, (blk, idx), row_tile, glu in big:
            res[n] = _adamw("adamw_" + n, l, w[n], m[n], v[n], reduced[l], (blk, D_MODEL), blk * idx, row_tile,
                            res[n], after, glu)

    for l in reversed(range(1, nl)):
        adamw_layer(l, [token])
    small["norm_final"] = small["norm_final"] + token[:1, :1]
    small_sum = _small_all_reduce(_flatten_small(small))
    finish(0, [small_sum] + [r[0] for r in res.values() if r is not None])
    adamw_layer(0, [])
    for n in t_names:
        res[n] = tuple(tr(a) for a in res[n])
    g_views = _split_small(small_sum, views[0])
    for group in _SMALL_GROUPS:
        deltas, new_ms, new_vs = _adamw_small("adamw_" + group[0], *[[d[n] for n in group] for d in views],
                                              [g_views[n] for n in group])
        for i, n in enumerate(group):
            res[n] = tuple(_unview(n, a, w[n].shape) for a in (g_views[n], deltas[i], new_ms[i], new_vs[i]))

    return (loss, grad_x[None], *[res[n][0] for n in _WEIGHTS], *[res[n][1] for n in _WEIGHTS],
            *[res[n][2] for n in _WEIGHTS], *[res[n][3] for n in _WEIGHTS])
```

```python
import functools
import math

import jax
import jax.numpy as jnp
from jax import lax
from jax.experimental import pallas as pl
from jax.experimental.pallas import tpu as pltpu

F32 = jnp.float32
BF16 = jnp.bfloat16

D_MODEL = 1024
D_POOL = 512
D_SSM = 512
POOL_WINDOWS = (2, 4, 8, 16)
POOL_GROUP = 128
POOL_HALO = 16
N_SSM_GROUPS = 32
SSM_GROUP = 16
SSM_STATE = 64
N_STATE = N_SSM_GROUPS * SSM_STATE
N_PAIRS = N_SSM_GROUPS // 2
D_FF = 2816
N_SHARD = 4
FF_SHARD = D_FF // N_SHARD
RMS_EPS = 1e-6

ADAM_LR = 0.001
ADAM_B1 = 0.9
ADAM_B2 = 0.999
ADAM_EPS = 1e-08
ADAM_WD = 0.01
ADAM_STEP = 10

P_ROWS = 2816
P_WD_BLK = (704, 0)
P_WG_BLK = (704, 1)
P_WU_BLK = (704, 2)
P_FF_ROWS = 2112
P_GLU_BLK = (64, 33)
P_GLU_PAD = 192
P_IN_BLK = (256, 9)
P_OUT_BLK = (256, 10)

SUBLANES = 8
VMEM_LIMIT = 56 * 1024 * 1024

TM = 1024
TM_FFN = 512
TM_FFN_LONG = 1024
FFN_SPLIT = 2
TS = 1024
SCAN_LANES = 512


def _cparams(n_axes):
    return pltpu.CompilerParams(dimension_semantics=("arbitrary",) * n_axes, vmem_limit_bytes=VMEM_LIMIT)


def _dot(a, b):
    return jnp.dot(a, b, preferred_element_type=F32)


def _dot_nt(a, b):
    return lax.dot_general(a, b, (((1,), (1,)), ((), ())), preferred_element_type=F32)


def _dot_tn(a, b):
    return lax.dot_general(a, b, (((0,), (0,)), ((), ())), preferred_element_type=F32)


def _rms_hat(x):
    r = lax.rsqrt(jnp.mean(x * x, axis=-1, keepdims=True) + RMS_EPS)
    return x * r, r


def _rms_bwd(d_hat, xhat, r):
    return r * (d_hat - xhat * jnp.mean(d_hat * xhat, axis=-1, keepdims=True))


def _sigmoid(x):
    return 1.0 / (1.0 + jnp.exp(-x))


_GELU_C = math.sqrt(2.0 / math.pi)
_GELU_K = 0.044715


def _gelu(x):
    return 0.5 * x * (1.0 + jnp.tanh(_GELU_C * (x + _GELU_K * x * x * x)))


def _gelu_grad(x):
    th = jnp.tanh(_GELU_C * (x + _GELU_K * x * x * x))
    return 0.5 * (1.0 + th) + 0.5 * x * (1.0 - th * th) * _GELU_C * (1.0 + 3.0 * _GELU_K * x * x)


def _glu_weight(ref):
    v = ref[...]
    return jnp.concatenate([v[:, :, :D_SSM], v[:, :, D_SSM:]], axis=1).reshape(D_SSM, D_SSM)


def _glu_pack(w):
    v = w.reshape(N_SHARD, 128, D_SSM)
    return jnp.concatenate([v[:, :64, :], v[:, 64:, :]], axis=2)


def _pool_diff(ext, row0, tm):
    rows = row0 + lax.broadcasted_iota(jnp.int32, (tm, 1), 0)
    outs = []
    for gi, w in enumerate(POOL_WINDOWS):
        e = ext[:, gi * POOL_GROUP:(gi + 1) * POOL_GROUP]
        s = e
        k = 1
        while k < w:
            s = s + pltpu.roll(s, k, 0)
            k *= 2
        inv = 1.0 / jnp.minimum(rows + 1, w).astype(F32)
        outs.append(s[POOL_HALO:, :] * inv - e[POOL_HALO:, :])
    return outs


def _mix_in_fwd(h, g1, wp, layer, w_pool, scale):
    L = h.shape[0]
    tm = min(TM, L)

    def body(h_ref, g_ref, w_ref, wp_ref, sc_ref, u_ref, yp_ref, carry):
        i = pl.program_id(0)

        @pl.when(i == 0)
        def _():
            carry[...] = jnp.zeros_like(carry)

        xhat, _ = _rms_hat(h_ref[...])
        n1 = (xhat * g_ref[...]).astype(BF16)
        u = _dot(n1, w_ref[...].reshape(D_MODEL, D_MODEL))
        u_ref[...] = u
        up = u[:, :D_POOL]
        ext = jnp.concatenate([carry[...], up], axis=0)
        carry[...] = up[tm - POOL_HALO:, :]
        diffs = _pool_diff(ext, i * tm, tm)
        for gi in range(4):
            cols = slice(gi * POOL_GROUP, (gi + 1) * POOL_GROUP)
            yp_ref[:, cols] = _dot(diffs[gi].astype(BF16), wp_ref[gi]) * sc_ref[:, cols]

    blk, idx = P_IN_BLK
    return pl.pallas_call(
        body, name="mix_in_fwd", grid=(L // tm,),
        in_specs=[pl.BlockSpec((tm, D_MODEL), lambda i: (i, 0)),
                  pl.BlockSpec((None, 1, D_MODEL), lambda i: (layer, 0, 0)),
                  pl.BlockSpec((N_SHARD, None, blk, D_MODEL), lambda i: (0, 0, idx, 0)),
                  pl.BlockSpec((None, 4, POOL_GROUP, POOL_GROUP), lambda i: (layer, 0, 0, 0)),
                  pl.BlockSpec((None, 1, D_POOL), lambda i: (layer, 0, 0))],
        out_specs=[pl.BlockSpec((tm, D_MODEL), lambda i: (i, 0)),
                   pl.BlockSpec((tm, D_POOL), lambda i: (i, 0))],
        out_shape=[jax.ShapeDtypeStruct((L, D_MODEL), F32), jax.ShapeDtypeStruct((L, D_POOL), F32)],
        scratch_shapes=[pltpu.VMEM((POOL_HALO, D_POOL), F32)],
        compiler_params=_cparams(1),
    )(h, g1, wp, w_pool, scale)


def _cmul(xr, xi, yr, yi):
    return xr * yr - xi * yi, xr * yi + xi * yr


SCAN_BLOCK = 64
N_SCAN_TABLES = 26


def _permute_rows(src, dst, n_rows):
    for b in range(n_rows // SCAN_BLOCK):
        for tau in range(SUBLANES):
            dst[pl.ds(SCAN_BLOCK * b + SUBLANES * tau, SUBLANES), :] = (
                src[pl.ds(SCAN_BLOCK * b + tau, SUBLANES, stride=SUBLANES), :])


def _scan_tables(ar, ai, tab, reverse):
    c = ar.shape[1]
    row = lax.broadcasted_iota(jnp.int32, (SUBLANES, c), 0)
    zero = jnp.zeros((SUBLANES, c), F32)
    full = lambda v: jnp.broadcast_to(v, (SUBLANES, c))
    pw = [(ar, ai)]
    for _ in range(SUBLANES - 1):
        pw.append(_cmul(*pw[-1], ar, ai))
    a8 = pw[-1]
    a16 = _cmul(*a8, *a8)
    a32 = _cmul(*a16, *a16)
    tab[0] = full(ar)
    tab[1] = full(ai)
    for n, (s, (pr, pi)) in enumerate(((1, a8), (2, a16), (4, a32))):
        keep = (row < SUBLANES - s) if reverse else (row >= s)
        tab[2 + 2 * n] = jnp.where(keep, pr, zero)
        tab[3 + 2 * n] = jnp.where(keep, pi, zero)
    cur = a8
    qr, qi = zero, zero
    for n in range(SUBLANES):
        at = (SUBLANES - 1 - n) if reverse else n
        qr = jnp.where(row == at, cur[0], qr)
        qi = jnp.where(row == at, cur[1], qi)
        cur = _cmul(*cur, *a8)
    tab[8] = qr
    tab[9] = qi
    for tau in range(SUBLANES):
        pr, pi = pw[SUBLANES - 1 - tau] if reverse else pw[tau]
        tab[10 + 2 * tau] = full(pr)
        tab[11 + 2 * tau] = full(pi)


def _cmac(xr, xi, ar, ai, yr, yi):
    return xr + ar * yr - ai * yi, xi + ar * yi + ai * yr


def _chain_segments(er, ei, c_r, c_i, tab, cols, reverse):
    tr, ti = er, ei
    for n, s in enumerate((1, 2, 4)):
        shift = SUBLANES - s if reverse else s
        tr, ti = _cmac(tr, ti, tab[2 + 2 * n, :, cols], tab[3 + 2 * n, :, cols],
                       pltpu.roll(tr, shift, 0), pltpu.roll(ti, shift, 0))
    return _cmac(tr, ti, tab[8, :, cols], tab[9, :, cols], c_r, c_i)


def _ssm_fwd(u, layer, bpad, cpad, ar, ai, dskip):
    L = u.shape[0]
    ts = min(TS, L)
    nq = 4
    cq = N_STATE // nq

    def body(u_ref, bp_ref, cp_ref, ar_ref, ai_ref, dsk_ref, sre_ref, sim_ref, y_ref, cr, ci, tab, up, yp):
        t = pl.program_id(1)

        @pl.when(t == 0)
        def _():
            cr[...] = jnp.zeros_like(cr)
            ci[...] = jnp.zeros_like(ci)
            _scan_tables(ar_ref[...], ai_ref[...], tab, reverse=False)

        _permute_rows(u_ref, up, ts)
        uf = up[...]
        ub = uf.astype(BF16)
        for jj in range(4):
            bu = _dot(ub, bp_ref[jj])
            sre_ref[:, jj * 128:(jj + 1) * 128] = bu[:, :128]
            sim_ref[:, jj * 128:(jj + 1) * 128] = bu[:, 128:]

        shp = (SUBLANES, SCAN_LANES)
        first_row = lax.broadcasted_iota(jnp.int32, shp, 0) == 0
        for cc in range(cq // SCAN_LANES):
            cols = slice(cc * SCAN_LANES, (cc + 1) * SCAN_LANES)

            def block(b, carry, cols=cols):
                c_r, c_i = carry
                base = pl.multiple_of(b * SCAN_BLOCK, SCAN_BLOCK)
                rows = lambda tau: pl.ds(base + SUBLANES * tau, SUBLANES)
                a_r, a_i = tab[0, :, cols], tab[1, :, cols]
                ys = [(sre_ref[rows(0), cols], sim_ref[rows(0), cols])]
                for tau in range(1, SUBLANES):
                    ys.append(_cmac(sre_ref[rows(tau), cols], sim_ref[rows(tau), cols], a_r, a_i, *ys[-1]))
                tr, ti = _chain_segments(*ys[-1], c_r, c_i, tab, cols, reverse=False)
                in_r = jnp.where(first_row, c_r, pltpu.roll(tr, 1, 0))
                in_i = jnp.where(first_row, c_i, pltpu.roll(ti, 1, 0))
                for tau in range(SUBLANES):
                    sr, si = _cmac(*ys[tau], tab[10 + 2 * tau, :, cols], tab[11 + 2 * tau, :, cols], in_r, in_i)
                    sre_ref[rows(tau), cols] = sr
                    sim_ref[rows(tau), cols] = si
                return (jnp.broadcast_to(tr[SUBLANES - 1:, :], shp), jnp.broadcast_to(ti[SUBLANES - 1:, :], shp))

            c_r, c_i = lax.fori_loop(0, ts // SCAN_BLOCK, block, (cr[:, cols], ci[:, cols]), unroll=2)
            cr[:, cols] = c_r
            ci[:, cols] = c_i

        acc = dsk_ref[...] * uf
        for jj in range(4):
            cols = slice(jj * 128, (jj + 1) * 128)
            scat = jnp.concatenate([sre_ref[:, cols], sim_ref[:, cols]], axis=1).astype(BF16)
            acc = acc + _dot(scat, cp_ref[jj])
        yp[...] = acc
        _permute_rows(yp, y_ref, ts)

    return pl.pallas_call(
        body, name="ssm_fwd", grid=(nq, L // ts),
        in_specs=[pl.BlockSpec((ts, 128), lambda q, t: (t, 4 + q)),
                  pl.BlockSpec((None, 4, 128, 256), lambda q, t: (layer, q, 0, 0)),
                  pl.BlockSpec((None, 4, 256, 128), lambda q, t: (layer, q, 0, 0)),
                  pl.BlockSpec((None, 1, cq), lambda q, t: (layer, 0, q)),
                  pl.BlockSpec((None, 1, cq), lambda q, t: (layer, 0, q)),
                  pl.BlockSpec((None, 1, 128), lambda q, t: (layer, 0, q))],
        out_specs=[pl.BlockSpec((ts, cq), lambda q, t: (t, q)),
                   pl.BlockSpec((ts, cq), lambda q, t: (t, q)),
                   pl.BlockSpec((ts, 128), lambda q, t: (t, q))],
        out_shape=[jax.ShapeDtypeStruct((L, N_STATE), F32), jax.ShapeDtypeStruct((L, N_STATE), F32),
                   jax.ShapeDtypeStruct((L, D_SSM), F32)],
        scratch_shapes=[pltpu.VMEM((SUBLANES, cq), F32), pltpu.VMEM((SUBLANES, cq), F32),
                        pltpu.VMEM((N_SCAN_TABLES, SUBLANES, cq), F32),
                        pltpu.VMEM((ts, 128), F32), pltpu.VMEM((ts, 128), F32)],
        compiler_params=_cparams(2),
    )(u, bpad, cpad, ar, ai, dskip)


def _mix_out_fwd(yraw, ypool, h, wp, layer, b_glu):
    L = h.shape[0]
    tm = min(TM, L)

    def body(yr_ref, yp_ref, h_ref, wglu_ref, b_ref, wout_ref, o_ref):
        y = _gelu(yr_ref[...])
        z = _dot(y.astype(BF16), _glu_weight(wglu_ref)) + b_ref[...]
        o = y * _sigmoid(z)
        mix = jnp.concatenate([yp_ref[...], o], axis=1).astype(BF16)
        o_ref[...] = h_ref[...] + _dot(mix, wout_ref[...].reshape(D_MODEL, D_MODEL))

    gb, gi = P_GLU_BLK
    ob, oi = P_OUT_BLK
    return pl.pallas_call(
        body, name="mix_out_fwd", grid=(L // tm,),
        in_specs=[pl.BlockSpec((tm, D_SSM), lambda i: (i, 0)),
                  pl.BlockSpec((tm, D_POOL), lambda i: (i, 0)),
                  pl.BlockSpec((tm, D_MODEL), lambda i: (i, 0)),
                  pl.BlockSpec((N_SHARD, None, gb, D_MODEL), lambda i: (0, 0, gi, 0)),
                  pl.BlockSpec((None, 1, D_SSM), lambda i: (layer, 0, 0)),
                  pl.BlockSpec((N_SHARD, None, ob, D_MODEL), lambda i: (0, 0, oi, 0))],
        out_specs=pl.BlockSpec((tm, D_MODEL), lambda i: (i, 0)),
        out_shape=jax.ShapeDtypeStruct((L, D_MODEL), F32),
        compiler_params=_cparams(1),
    )(yraw, ypool, h, wp, b_glu, wp)


def _ffn_weights(ref, k):
    return ref[k, 0:FF_SHARD, :], ref[k, FF_SHARD:2 * FF_SHARD, :], ref[k, 2 * FF_SHARD:P_FF_ROWS, :]


def _ffn_weight_spec():
    return pl.BlockSpec((N_SHARD, None, P_FF_ROWS, D_MODEL), lambda m, k: (0, 0, 0, 0),
                        pipeline_mode=pl.Buffered(1))


def _ffn_fwd(h, g2, wp, layer):
    L = h.shape[0]
    tm = min(TM_FFN_LONG, L)

    def body(h_ref, g_ref, w_ref, o_ref, n2_ref, act_ref, dgate_ref, dup_ref):
        k = pl.program_id(1)

        @pl.when(k == 0)
        def _():
            x = h_ref[...]
            xhat, _ = _rms_hat(x)
            n2_ref[...] = (xhat * g_ref[...]).astype(BF16)
            o_ref[...] = x

        wd, wg_t, wu_t = _ffn_weights(w_ref, k)
        n2 = n2_ref[...]
        gate = _dot_nt(n2, wg_t)
        up = _dot_nt(n2, wu_t)
        sg = _sigmoid(gate)
        silu = gate * sg
        act = (silu * up).astype(BF16)
        act_ref[...] = act
        dgate_ref[...] = (up * (sg * (1.0 + gate * (1.0 - sg)))).astype(BF16)
        dup_ref[...] = silu.astype(BF16)
        o_ref[...] += _dot(act, wd)

    act_shape = jax.ShapeDtypeStruct((N_SHARD, L, FF_SHARD), BF16)
    return pl.pallas_call(
        body, name="ffn_fwd", grid=(L // tm, N_SHARD),
        in_specs=[pl.BlockSpec((tm, D_MODEL), lambda m, k: (m, 0)),
                  pl.BlockSpec((None, 1, D_MODEL), lambda m, k: (layer, 0, 0)),
                  _ffn_weight_spec()],
        out_specs=[pl.BlockSpec((tm, D_MODEL), lambda m, k: (m, 0)),
                   pl.BlockSpec((tm, D_MODEL), lambda m, k: (m, 0)),
                   pl.BlockSpec((None, tm, FF_SHARD), lambda m, k: (k, m, 0)),
                   pl.BlockSpec((None, tm, FF_SHARD), lambda m, k: (k, m, 0)),
                   pl.BlockSpec((None, tm, FF_SHARD), lambda m, k: (k, m, 0))],
        out_shape=[jax.ShapeDtypeStruct((L, D_MODEL), F32), jax.ShapeDtypeStruct((L, D_MODEL), BF16),
                   act_shape, act_shape, act_shape],
        compiler_params=_cparams(2),
    )(h, g2, wp)


def _final_fwd_bwd(h, gf, target):
    L = h.shape[0]
    tm = min(TM, L)

    def body(h_ref, g_ref, t_ref, dh_ref, loss_ref, dg_ref):
        i = pl.program_id(0)

        @pl.when(i == 0)
        def _():
            loss_ref[...] = jnp.zeros_like(loss_ref)
            dg_ref[...] = jnp.zeros_like(dg_ref)

        xhat, r = _rms_hat(h_ref[...])
        g = g_ref[...]
        e = xhat * g - t_ref[...]
        loss_ref[...] += 0.5 * jnp.sum(jnp.mean(e * e, axis=-1, keepdims=True), axis=0, keepdims=True)
        dy = e * (1.0 / D_MODEL)
        dg_ref[...] += jnp.sum(dy * xhat, axis=0, keepdims=True)
        dh_ref[...] = _rms_bwd(dy * g, xhat, r)

    return pl.pallas_call(
        body, name="final_fwd_bwd", grid=(L // tm,),
        in_specs=[pl.BlockSpec((tm, D_MODEL), lambda i: (i, 0)),
                  pl.BlockSpec((1, D_MODEL), lambda i: (0, 0)),
                  pl.BlockSpec((tm, D_MODEL), lambda i: (i, 0))],
        out_specs=[pl.BlockSpec((tm, D_MODEL), lambda i: (i, 0)),
                   pl.BlockSpec((1, 1), lambda i: (0, 0)),
                   pl.BlockSpec((1, D_MODEL), lambda i: (0, 0))],
        out_shape=[jax.ShapeDtypeStruct((L, D_MODEL), F32), jax.ShapeDtypeStruct((1, 1), F32),
                   jax.ShapeDtypeStruct((1, D_MODEL), F32)],
        compiler_params=_cparams(1),
    )(h, gf, target)


def _ffn_bwd_act(dh, h, g2, fgate_s, fup_s, wp, layer):
    L = h.shape[0]
    tm = min(TM_FFN, L)
    sub = tm // FFN_SPLIT

    def body(dh_ref, h_ref, g_ref, fgate_ref, fup_ref, w_ref,
             dhm_ref, dg_ref, dgate_ref, dup_ref, dhb_ref, dn2):
        m, k = pl.program_id(0), pl.program_id(1)

        @pl.when(jnp.logical_and(m == 0, k == 0))
        def _():
            dg_ref[...] = jnp.zeros_like(dg_ref)

        @pl.when(k == 0)
        def _():
            dhb_ref[...] = dh_ref[...].astype(BF16)
            dn2[...] = jnp.zeros_like(dn2)

        wd, wg_t, wu_t = _ffn_weights(w_ref, k)
        for rows in (slice(r * sub, (r + 1) * sub) for r in range(tm // sub)):
            dact = _dot_nt(dhb_ref[rows, :], wd)
            dgate = (dact * fgate_ref[rows, :].astype(F32)).astype(BF16)
            dup = (dact * fup_ref[rows, :].astype(F32)).astype(BF16)
            dgate_ref[rows, :] = dgate
            dup_ref[rows, :] = dup
            dn2[rows, :] += _dot(dgate, wg_t) + _dot(dup, wu_t)

        @pl.when(k == N_SHARD - 1)
        def _():
            xhat, r = _rms_hat(h_ref[...])
            d = dn2[...]
            dg_ref[...] += jnp.sum(d * xhat, axis=0, keepdims=True)
            dhm_ref[...] = dh_ref[...] + _rms_bwd(d * g_ref[...], xhat, r)

    act_spec = pl.BlockSpec((None, tm, FF_SHARD), lambda m, k: (k, m, 0))
    act_shape = jax.ShapeDtypeStruct((N_SHARD, L, FF_SHARD), BF16)
    row_spec = pl.BlockSpec((tm, D_MODEL), lambda m, k: (m, 0))
    return pl.pallas_call(
        body, name="ffn_bwd_act", grid=(L // tm, N_SHARD),
        in_specs=[row_spec, row_spec,
                  pl.BlockSpec((None, 1, D_MODEL), lambda m, k: (layer, 0, 0)),
                  act_spec, act_spec,
                  _ffn_weight_spec()],
        out_specs=[row_spec,
                   pl.BlockSpec((1, D_MODEL), lambda m, k: (0, 0)),
                   act_spec, act_spec, row_spec],
        out_shape=[jax.ShapeDtypeStruct((L, D_MODEL), F32), jax.ShapeDtypeStruct((1, D_MODEL), F32),
                   act_shape, act_shape, jax.ShapeDtypeStruct((L, D_MODEL), BF16)],
        scratch_shapes=[pltpu.VMEM((tm, D_MODEL), F32)],
        compiler_params=_cparams(2),
    )(dh, h, g2, fgate_s, fup_s, wp)


def _ffn_bwd_w(n2, dgate_s, dup_s, act_s, dhb, gbuf):
    L = n2.shape[0]
    tm = min(TM_FFN_LONG, L)

    def body(n2_ref, dgate_ref, dup_ref, act_ref, dhb_ref, g_in, g_ref):
        m = pl.program_id(1)

        @pl.when(m == 0)
        def _():
            g_ref[...] = jnp.zeros_like(g_ref)

        n2v = n2_ref[...]
        g_ref[0:FF_SHARD, :] += _dot_tn(act_ref[...], dhb_ref[...])
        g_ref[FF_SHARD:2 * FF_SHARD, :] += _dot_tn(dgate_ref[...], n2v)
        g_ref[2 * FF_SHARD:P_FF_ROWS, :] += _dot_tn(dup_ref[...], n2v)

    act_spec = pl.BlockSpec((None, tm, FF_SHARD), lambda k, m: (k, m, 0))
    row_spec = pl.BlockSpec((tm, D_MODEL), lambda k, m: (m, 0))
    return pl.pallas_call(
        body, name="ffn_bwd_w", grid=(N_SHARD, L // tm),
        in_specs=[row_spec, act_spec, act_spec, act_spec, row_spec, pl.BlockSpec(memory_space=pl.ANY)],
        out_specs=pl.BlockSpec((None, None, P_FF_ROWS, D_MODEL), lambda k, m: (0, k, 0, 0)),
        out_shape=jax.ShapeDtypeStruct(gbuf.shape, F32),
        input_output_aliases={5: 0},
        compiler_params=_cparams(2),
    )(n2, dgate_s, dup_s, act_s, dhb, gbuf)


def _mix_out_bwd(dhm, yraw, ypool, wp, layer, b_glu, gbuf):
    L = dhm.shape[0]
    tm = min(TM, L)

    def body(dhm_ref, yr_ref, yp_ref, wglu_ref, b_ref, wout_ref, g1_in,
             dyr_ref, dyp_ref, db_ref, g1_ref, dwout, dwglu, gpack):
        i = pl.program_id(0)

        @pl.when(i == 0)
        def _():
            db_ref[...] = jnp.zeros_like(db_ref)
            dwout[...] = jnp.zeros_like(dwout)
            dwglu[...] = jnp.zeros_like(dwglu)

        dhb = dhm_ref[...].astype(BF16)
        wglu = _glu_weight(wglu_ref)
        dmix = _dot_nt(dhb, wout_ref[...].reshape(D_MODEL, D_MODEL))
        dyp_ref[...] = dmix[:, :D_POOL]
        d_o = dmix[:, D_POOL:]
        yraw_v = yr_ref[...]
        y = _gelu(yraw_v)
        yb = y.astype(BF16)
        sig = _sigmoid(_dot(yb, wglu) + b_ref[...])
        mix = jnp.concatenate([yp_ref[...], y * sig], axis=1).astype(BF16)
        dwout[...] += _dot_tn(mix, dhb).reshape(N_SHARD, 256, D_MODEL)
        dz = d_o * y * sig * (1.0 - sig)
        dzb = dz.astype(BF16)
        db_ref[...] += jnp.sum(dz, axis=0, keepdims=True)
        dwglu[...] += _dot_tn(yb, dzb)
        dy = d_o * sig + _dot_nt(dzb, wglu)
        dyr_ref[...] = dy * _gelu_grad(yraw_v)

        @pl.when(i == n_steps - 1)
        def _():
            gpack[:, :gb, :] = _glu_pack(dwglu[...])
            gpack[:, gb:, :] = jnp.zeros((N_SHARD, P_GLU_PAD - gb, D_MODEL), F32)
            pltpu.sync_copy(gpack, g1_ref.at[0, :, pl.ds(gb * gi, P_GLU_PAD), :])
            pltpu.sync_copy(dwout, g1_ref.at[0, :, pl.ds(ob * oi, ob), :])

    gb, gi = P_GLU_BLK
    ob, oi = P_OUT_BLK
    n_steps = L // tm
    return pl.pallas_call(
        body, name="mix_out_bwd", grid=(n_steps,),
        in_specs=[pl.BlockSpec((tm, D_MODEL), lambda i: (i, 0)),
                  pl.BlockSpec((tm, D_SSM), lambda i: (i, 0)),
                  pl.BlockSpec((tm, D_POOL), lambda i: (i, 0)),
                  pl.BlockSpec((N_SHARD, None, gb, D_MODEL), lambda i: (0, 0, gi, 0)),
                  pl.BlockSpec((None, 1, D_SSM), lambda i: (layer, 0, 0)),
                  pl.BlockSpec((N_SHARD, None, ob, D_MODEL), lambda i: (0, 0, oi, 0)),
                  pl.BlockSpec(memory_space=pl.ANY)],
        out_specs=[pl.BlockSpec((tm, D_SSM), lambda i: (i, 0)),
                   pl.BlockSpec((tm, D_POOL), lambda i: (i, 0)),
                   pl.BlockSpec((1, D_SSM), lambda i: (0, 0)),
                   pl.BlockSpec(memory_space=pl.ANY)],
        out_shape=[jax.ShapeDtypeStruct((L, D_SSM), F32), jax.ShapeDtypeStruct((L, D_POOL), F32),
                   jax.ShapeDtypeStruct((1, D_SSM), F32),
                   jax.ShapeDtypeStruct(gbuf.shape, F32)],
        scratch_shapes=[pltpu.VMEM((N_SHARD, ob, D_MODEL), F32), pltpu.VMEM((D_SSM, D_SSM), F32),
                        pltpu.VMEM((N_SHARD, P_GLU_PAD, D_MODEL), F32)],
        input_output_aliases={6: 3},
        compiler_params=_cparams(1),
    )(dhm, yraw, ypool, wp, b_glu, wp, gbuf)


def _ssm_bwd(dyraw, u, sre, sim, layer, cpad_t, bpad_t, ar, ai, dskip):
    L = u.shape[0]
    ts = min(TS, L)
    nt = L // ts
    nq = 4
    cq = N_STATE // nq

    def body(dy_ref, u_ref, sre_ref, sim_ref, ct_ref, bt_ref, ar_ref, ai_ref, dsk_ref,
             du_ref, dcp_ref, dbp_ref, dar_ref, dai_ref, ddsk_ref, gre, gim, cr, ci, tab, accr, acci, up, dyp):
        t = pl.program_id(1)

        @pl.when(t == 0)
        def _():
            for ref in (cr, ci, accr, acci, dcp_ref, dbp_ref, ddsk_ref):
                ref[...] = jnp.zeros_like(ref)
            _scan_tables(ar_ref[...], -ai_ref[...], tab, reverse=True)

        _permute_rows(dy_ref, dyp, ts)
        _permute_rows(u_ref, up, ts)
        dy = dyp[...]
        dyb = dy.astype(BF16)
        uf = up[...]
        ub = uf.astype(BF16)
        for jj in range(4):
            cols = slice(jj * 128, (jj + 1) * 128)
            ds = _dot(dyb, ct_ref[jj])
            gre[:, cols] = ds[:, :128]
            gim[:, cols] = ds[:, 128:]
            scat = jnp.concatenate([sre_ref[:, cols], sim_ref[:, cols]], axis=1).astype(BF16)
            dcp_ref[jj] += _dot_tn(scat, dyb)

        n_blk = ts // SCAN_BLOCK
        shp = (SUBLANES, SCAN_LANES)
        last_row = lax.broadcasted_iota(jnp.int32, shp, 0) == SUBLANES - 1
        for cc in range(cq // SCAN_LANES):
            cols = slice(cc * SCAN_LANES, (cc + 1) * SCAN_LANES)

            def block(i, carry, cols=cols):
                c_r, c_i, a_r, a_i = carry
                base = pl.multiple_of((n_blk - 1 - i) * SCAN_BLOCK, SCAN_BLOCK)
                rows = lambda tau: pl.ds(base + SUBLANES * tau, SUBLANES)
                m_r, m_i = tab[0, :, cols], tab[1, :, cols]
                ys = [None] * SUBLANES
                ys[SUBLANES - 1] = (gre[rows(SUBLANES - 1), cols], gim[rows(SUBLANES - 1), cols])
                for tau in reversed(range(SUBLANES - 1)):
                    ys[tau] = _cmac(gre[rows(tau), cols], gim[rows(tau), cols], m_r, m_i, *ys[tau + 1])
                tr, ti = _chain_segments(*ys[0], c_r, c_i, tab, cols, reverse=True)
                in_r = jnp.where(last_row, c_r, pltpu.roll(tr, SUBLANES - 1, 0))
                in_i = jnp.where(last_row, c_i, pltpu.roll(ti, SUBLANES - 1, 0))
                gs = [_cmac(*ys[tau], tab[10 + 2 * tau, :, cols], tab[11 + 2 * tau, :, cols], in_r, in_i)
                      for tau in range(SUBLANES)]
                for tau in range(SUBLANES):
                    gre[rows(tau), cols] = gs[tau][0]
                    gim[rows(tau), cols] = gs[tau][1]
                    if tau < SUBLANES - 1:
                        nr, ni = gs[tau + 1]
                    else:
                        nr = jnp.where(last_row, c_r, pltpu.roll(gs[0][0], SUBLANES - 1, 0))
                        ni = jnp.where(last_row, c_i, pltpu.roll(gs[0][1], SUBLANES - 1, 0))
                    sr, si = sre_ref[rows(tau), cols], sim_ref[rows(tau), cols]
                    a_r = a_r + sr * nr + si * ni
                    a_i = a_i + sr * ni - si * nr
                return (jnp.broadcast_to(tr[:1, :], shp), jnp.broadcast_to(ti[:1, :], shp), a_r, a_i)

            c_r, c_i, a_r, a_i = lax.fori_loop(
                0, n_blk, block, (cr[:, cols], ci[:, cols], accr[:, cols], acci[:, cols]), unroll=2)
            cr[:, cols] = c_r
            ci[:, cols] = c_i
            accr[:, cols] = a_r
            acci[:, cols] = a_i

        acc = dsk_ref[...] * dy
        for jj in range(4):
            cols = slice(jj * 128, (jj + 1) * 128)
            gcat = jnp.concatenate([gre[:, cols], gim[:, cols]], axis=1).astype(BF16)
            acc = acc + _dot(gcat, bt_ref[jj])
            dbp_ref[jj] += _dot_tn(ub, gcat)
        ddsk_ref[...] += jnp.sum(dy * uf, axis=0, keepdims=True)
        dyp[...] = acc
        _permute_rows(dyp, du_ref, ts)

        @pl.when(t == nt - 1)
        def _():
            dar_ref[...] = jnp.sum(accr[...], axis=0, keepdims=True)
            dai_ref[...] = jnp.sum(acci[...], axis=0, keepdims=True)

    f32_scr = lambda *s: pltpu.VMEM(s, F32)
    return pl.pallas_call(
        body, name="ssm_bwd", grid=(nq, nt),
        in_specs=[pl.BlockSpec((ts, 128), lambda q, t: (nt - 1 - t, q)),
                  pl.BlockSpec((ts, 128), lambda q, t: (nt - 1 - t, 4 + q)),
                  pl.BlockSpec((ts, cq), lambda q, t: (nt - 1 - t, q)),
                  pl.BlockSpec((ts, cq), lambda q, t: (nt - 1 - t, q)),
                  pl.BlockSpec((None, 4, 128, 256), lambda q, t: (layer, q, 0, 0)),
                  pl.BlockSpec((None, 4, 256, 128), lambda q, t: (layer, q, 0, 0)),
                  pl.BlockSpec((None, 1, cq), lambda q, t: (layer, 0, q)),
                  pl.BlockSpec((None, 1, cq), lambda q, t: (layer, 0, q)),
                  pl.BlockSpec((None, 1, 128), lambda q, t: (layer, 0, q))],
        out_specs=[pl.BlockSpec((ts, 128), lambda q, t: (nt - 1 - t, q)),
                   pl.BlockSpec((4, 256, 128), lambda q, t: (q, 0, 0)),
                   pl.BlockSpec((4, 128, 256), lambda q, t: (q, 0, 0)),
                   pl.BlockSpec((1, cq), lambda q, t: (0, q)),
                   pl.BlockSpec((1, cq), lambda q, t: (0, q)),
                   pl.BlockSpec((1, 128), lambda q, t: (0, q))],
        out_shape=[jax.ShapeDtypeStruct((L, D_SSM), F32),
                   jax.ShapeDtypeStruct((N_PAIRS, 256, 128), F32), jax.ShapeDtypeStruct((N_PAIRS, 128, 256), F32),
                   jax.ShapeDtypeStruct((1, N_STATE), F32), jax.ShapeDtypeStruct((1, N_STATE), F32),
                   jax.ShapeDtypeStruct((1, D_SSM), F32)],
        scratch_shapes=[f32_scr(ts, cq), f32_scr(ts, cq), f32_scr(SUBLANES, cq), f32_scr(SUBLANES, cq),
                        f32_scr(N_SCAN_TABLES, SUBLANES, cq), f32_scr(SUBLANES, cq), f32_scr(SUBLANES, cq),
                        f32_scr(ts, 128), f32_scr(ts, 128)],
        compiler_params=_cparams(2),
    )(dyraw, u, sre, sim, cpad_t, bpad_t, ar, ai, dskip)


def _pool_bwd(dyp, u, layer, w_pool, scale):
    L = u.shape[0]
    tm = min(TM, L)
    nt = L // tm
    halo_per_tile = tm // POOL_HALO

    def body(dyp_ref, u_ref, halo_ref, wp_ref, sc_ref, du_ref, dwp_ref, dsc_ref, carry):
        i = pl.program_id(0)
        tile = nt - 1 - i

        @pl.when(i == 0)
        def _():
            carry[...] = jnp.zeros_like(carry)
            dwp_ref[...] = jnp.zeros_like(dwp_ref)
            dsc_ref[...] = jnp.zeros_like(dsc_ref)

        up = u_ref[...]
        halo = jnp.where(tile > 0, halo_ref[...], jnp.zeros_like(halo_ref))
        diffs = _pool_diff(jnp.concatenate([halo, up], axis=0), tile * tm, tm)
        rows = tile * tm + lax.broadcasted_iota(jnp.int32, (tm, 1), 0)
        n_ext = tm + POOL_HALO
        for gi, w in enumerate(POOL_WINDOWS):
            cols = slice(gi * POOL_GROUP, (gi + 1) * POOL_GROUP)
            db = diffs[gi].astype(BF16)
            dyp = dyp_ref[:, cols]
            dsc_ref[:, cols] += jnp.sum(dyp * _dot(db, wp_ref[gi]), axis=0, keepdims=True)
            dp = (dyp * sc_ref[:, cols]).astype(BF16)
            ddiff = _dot_nt(dp, wp_ref[gi])
            dwp_ref[gi] += _dot_tn(db, dp)
            e = ddiff * (1.0 / jnp.minimum(rows + 1, w).astype(F32))
            s = jnp.concatenate([e, carry[:, cols]], axis=0)
            k = 1
            while k < w:
                s = s + pltpu.roll(s, n_ext - k, 0)
                k *= 2
            du_ref[:, cols] = s[:tm, :] - ddiff
            carry[:, cols] = e[:POOL_HALO, :]

    return pl.pallas_call(
        body, name="pool_bwd", grid=(nt,),
        in_specs=[pl.BlockSpec((tm, D_POOL), lambda i: (nt - 1 - i, 0)),
                  pl.BlockSpec((tm, D_POOL), lambda i: (nt - 1 - i, 0)),
                  pl.BlockSpec((POOL_HALO, D_POOL), lambda i: (jnp.maximum((nt - 1 - i) * halo_per_tile - 1, 0), 0)),
                  pl.BlockSpec((None, 4, POOL_GROUP, POOL_GROUP), lambda i: (layer, 0, 0, 0)),
                  pl.BlockSpec((None, 1, D_POOL), lambda i: (layer, 0, 0))],
        out_specs=[pl.BlockSpec((tm, D_POOL), lambda i: (nt - 1 - i, 0)),
                   pl.BlockSpec((4, POOL_GROUP, POOL_GROUP), lambda i: (0, 0, 0)),
                   pl.BlockSpec((1, D_POOL), lambda i: (0, 0))],
        out_shape=[jax.ShapeDtypeStruct((L, D_POOL), F32),
                   jax.ShapeDtypeStruct((4, POOL_GROUP, POOL_GROUP), F32),
                   jax.ShapeDtypeStruct((1, D_POOL), F32)],
        scratch_shapes=[pltpu.VMEM((POOL_HALO, D_POOL), F32)],
        compiler_params=_cparams(1),
    )(dyp, u, u, w_pool, scale)


def _mix_in_bwd(dup, dus, h, dhm, g1, wp, layer, gbuf):
    L = h.shape[0]
    tm = min(TM, L)
    n_steps = L // tm
    blk, idx = P_IN_BLK

    def body(dup_ref, dus_ref, h_ref, dhm_ref, g_ref, w_ref, g1_in, dh_ref, dg_ref, g1_ref, dwin):
        i = pl.program_id(0)

        @pl.when(i == 0)
        def _():
            dg_ref[...] = jnp.zeros_like(dg_ref)
            dwin[...] = jnp.zeros_like(dwin)

        du = jnp.concatenate([dup_ref[...], dus_ref[...]], axis=1).astype(BF16)
        dn1 = _dot_nt(du, w_ref[...].reshape(D_MODEL, D_MODEL))
        xhat, r = _rms_hat(h_ref[...])
        g = g_ref[...]
        n1 = (xhat * g).astype(BF16)
        dwin[...] += _dot_tn(n1, du).reshape(N_SHARD, blk, D_MODEL)
        dg_ref[...] += jnp.sum(dn1 * xhat, axis=0, keepdims=True)
        dh_ref[...] = dhm_ref[...] + _rms_bwd(dn1 * g, xhat, r)

        @pl.when(i == n_steps - 1)
        def _():
            pltpu.sync_copy(dwin, g1_ref.at[0, :, pl.ds(blk * idx, blk), :])

    row_spec = pl.BlockSpec((tm, D_MODEL), lambda i: (i, 0))
    half_spec = pl.BlockSpec((tm, D_POOL), lambda i: (i, 0))
    return pl.pallas_call(
        body, name="mix_in_bwd", grid=(n_steps,),
        in_specs=[half_spec, half_spec, row_spec, row_spec,
                  pl.BlockSpec((None, 1, D_MODEL), lambda i: (layer, 0, 0)),
                  pl.BlockSpec((N_SHARD, None, blk, D_MODEL), lambda i: (0, 0, idx, 0)),
                  pl.BlockSpec(memory_space=pl.ANY)],
        out_specs=[row_spec, pl.BlockSpec((1, D_MODEL), lambda i: (0, 0)), pl.BlockSpec(memory_space=pl.ANY)],
        out_shape=[jax.ShapeDtypeStruct((L, D_MODEL), F32), jax.ShapeDtypeStruct((1, D_MODEL), F32),
                   jax.ShapeDtypeStruct(gbuf.shape, F32)],
        scratch_shapes=[pltpu.VMEM((N_SHARD, blk, D_MODEL), F32)],
        input_output_aliases={6: 2},
        compiler_params=_cparams(1),
    )(dup, dus, h, dhm, g1, wp, gbuf)


def _disc_math(lr, li, ldt, br_t, bi_t):
    dt = jnp.exp(ldt)
    mag = jnp.exp(lr * dt)
    ang = li * dt
    ar = mag * jnp.cos(ang)
    ai = mag * jnp.sin(ang)
    den = lr * lr + li * li
    nr, ni = ar - 1.0, ai
    cr = (nr * lr + ni * li) / den
    ci = (ni * lr - nr * li) / den
    return ar, ai, cr * br_t - ci * bi_t, cr * bi_t + ci * br_t


def _disc_fwd(lr, li, ldt, br_t, bi_t):
    def body(lr_ref, li_ref, ldt_ref, br_ref, bi_ref, ar_ref, ai_ref, bbr_ref, bbi_ref):
        ar, ai, bbr, bbi = _disc_math(lr_ref[...], li_ref[...], ldt_ref[...], br_ref[...], bi_ref[...])
        ar_ref[...] = ar
        ai_ref[...] = ai
        bbr_ref[...] = bbr
        bbi_ref[...] = bbi

    shapes = [jax.ShapeDtypeStruct(a.shape, F32) for a in (lr, li, br_t, bi_t)]
    return pl.pallas_call(body, name="ssm_disc_fwd", out_shape=shapes,
                          compiler_params=pltpu.CompilerParams(vmem_limit_bytes=VMEM_LIMIT))(lr, li, ldt, br_t, bi_t)


def _disc_bwd(lr, li, ldt, br_t, bi_t, dar, dai, dbbr, dbbi):
    def body(lr_ref, li_ref, ldt_ref, br_ref, bi_ref, dar_ref, dai_ref, dbbr_ref, dbbi_ref,
             dlr_ref, dli_ref, dldt_ref, dbr_ref, dbi_ref):
        prim = (lr_ref[...], li_ref[...], ldt_ref[...], br_ref[...], bi_ref[...])
        _, pullback = jax.vjp(_disc_math, *prim)
        dlr, dli, dldt, dbr, dbi = pullback((dar_ref[...], dai_ref[...], dbbr_ref[...], dbbi_ref[...]))
        dlr_ref[...] = dlr
        dli_ref[...] = dli
        dldt_ref[...] = dldt
        dbr_ref[...] = dbr
        dbi_ref[...] = dbi

    shapes = [jax.ShapeDtypeStruct(a.shape, F32) for a in (lr, li, ldt, br_t, bi_t)]
    return pl.pallas_call(body, name="ssm_disc_bwd", out_shape=shapes,
                          compiler_params=pltpu.CompilerParams(vmem_limit_bytes=VMEM_LIMIT))(
        lr, li, ldt, br_t, bi_t, dar, dai, dbbr, dbbi)


def _pad_pairs(m_re, m_im):
    def blocks(m):
        v = m.transpose(0, 2, 1).reshape(N_PAIRS, 2, SSM_GROUP, SSM_STATE)
        return jnp.einsum("ab,jahp->jahbp", jnp.eye(2, dtype=m.dtype), v).reshape(N_PAIRS, 32, 128)
    both = jnp.concatenate([blocks(m_re), blocks(m_im)], axis=-1)
    place = jax.nn.one_hot(jnp.arange(N_PAIRS) % 4, 4, dtype=both.dtype)
    return jnp.einsum("jk,jrc->jkrc", place, both).reshape(N_PAIRS, 128, 256)


def _unpad_pairs(x):
    place = jax.nn.one_hot(jnp.arange(N_PAIRS) % 4, 4, dtype=x.dtype)
    both = jnp.einsum("jk,jkrc->jrc", place, x.reshape(N_PAIRS, 4, 32, 256))

    def unblock(v):
        v = v.reshape(N_PAIRS, 2, SSM_GROUP, 2, SSM_STATE)
        d = jnp.einsum("ab,jahbp->jahp", jnp.eye(2, dtype=x.dtype), v)
        return d.reshape(N_SSM_GROUPS, SSM_GROUP, SSM_STATE).transpose(0, 2, 1)
    return unblock(both[..., :128]), unblock(both[..., 128:])


def _adamw_math(w, g, m, v):
    m = ADAM_B1 * m + (1.0 - ADAM_B1) * g
    v = ADAM_B2 * v + (1.0 - ADAM_B2) * (g * g)
    m_hat = m / (1.0 - ADAM_B1 ** ADAM_STEP)
    v_hat = v / (1.0 - ADAM_B2 ** ADAM_STEP)
    delta = -ADAM_LR * (m_hat / (jnp.sqrt(v_hat) + ADAM_EPS) + ADAM_WD * w)
    return delta, m, v


def _adamw(name, layer, w, m, v, gbuf, g_block, g_row0, row_tile, outs=None, after=(), glu=False):
    nl, r, c = w.shape
    n_tiles = r // row_tile
    g_rows, g_cols = g_block
    g_tile = g_rows // n_tiles
    g_off = g_row0 // g_tile
    if outs is None:
        outs = [lax.empty(w.shape, F32) for _ in range(4)]

    def body(w_ref, m_ref, v_ref, g_ref, *rest):
        go_ref, d_ref, mo_ref, vo_ref = rest[-4:]
        g = g_ref[...]
        if glu:
            g = jnp.concatenate([g[:, :D_SSM], g[:, D_SSM:]], axis=0)
        delta, mn, vn = _adamw_math(w_ref[...], g, m_ref[...], v_ref[...])
        go_ref[...] = g
        d_ref[...] = delta
        mo_ref[...] = mn
        vo_ref[...] = vn

    w_spec = pl.BlockSpec((None, row_tile, c), lambda j: (layer, j, 0))
    shape = jax.ShapeDtypeStruct(w.shape, F32)
    return pl.pallas_call(
        body, name=name, grid=(n_tiles,),
        in_specs=[w_spec, w_spec, w_spec, pl.BlockSpec((None, g_tile, g_cols), lambda j: (0, g_off + j, 0))]
        + [_ANY] * (4 + len(after)),
        out_specs=[w_spec] * 4,
        out_shape=[shape] * 4,
        input_output_aliases={4: 0, 5: 1, 6: 2, 7: 3},
        compiler_params=_cparams(1),
    )(w, m, v, gbuf, *outs, *after)


def _pack_weights(ids, layer, w_in, w_glu, w_out, w_down, w_gate_t, w_up_t):
    gb, gi = P_GLU_BLK
    ib, ii = P_IN_BLK
    ob, oi = P_OUT_BLK

    def body(ids_ref, in_ref, glu_ref, out_ref, dn_ref, gate_ref, up_ref, p_ref):
        p_ref[0:FF_SHARD, :] = dn_ref[...].astype(BF16)
        p_ref[FF_SHARD:2 * FF_SHARD, :] = gate_ref[...].astype(BF16)
        p_ref[2 * FF_SHARD:P_FF_ROWS, :] = up_ref[...].astype(BF16)
        g = glu_ref[...]
        p_ref[gb * gi:gb * (gi + 1), :] = jnp.concatenate([g[:gb, :], g[gb:, :]], axis=1).astype(BF16)
        p_ref[gb * (gi + 1):ib * ii, :] = jnp.zeros((P_GLU_PAD - gb, D_MODEL), BF16)
        p_ref[ib * ii:ib * (ii + 1), :] = in_ref[...].astype(BF16)
        p_ref[ob * oi:ob * (oi + 1), :] = out_ref[...].astype(BF16)

    def spec(a):
        return pl.BlockSpec((None,) + a.shape[1:], lambda i, ids_ref: (layer, 0, 0))

    ins = (w_in, w_glu, w_out, w_down, w_gate_t, w_up_t)
    grid_spec = pltpu.PrefetchScalarGridSpec(
        num_scalar_prefetch=1, grid=(1,),
        in_specs=[spec(a) for a in ins],
        out_specs=pl.BlockSpec((None, None, P_ROWS, D_MODEL), lambda i, ids_ref: (ids_ref[1], 0, 0, 0)))
    return pl.pallas_call(
        body, name="pack_weights", grid_spec=grid_spec,
        out_shape=jax.ShapeDtypeStruct((N_SHARD, 1, P_ROWS, D_MODEL), BF16),
        compiler_params=_cparams(1),
    )(ids, *ins)


MESH = pl.DeviceIdType.MESH
_ANY = pl.BlockSpec(memory_space=pl.ANY)
P_HALF = P_ROWS // 2
RS_ROW_TILE = 352


def _mesh_pos():
    return lax.axis_index("x"), lax.axis_index("y"), lax.axis_index("c")


def _other_chips(x, y):
    return [(1 - x, y), (x, 1 - y), (1 - x, 1 - y)]


def _remote(src, dst, send_sems, recv_sems, n, to):
    return pltpu.make_async_remote_copy(src_ref=src, dst_ref=dst, send_sem=send_sems.at[n],
                                        recv_sem=recv_sems.at[n], device_id=to, device_id_type=MESH)


_HBM = pl.BlockSpec(memory_space=pltpu.HBM)
_SEM = pl.BlockSpec(memory_space=pltpu.SEMAPHORE)
_EFFECT = pltpu.CompilerParams(has_side_effects=pltpu.SideEffectType.DATAFLOW_SIDE_EFFECTING)
_TOKEN = jax.ShapeDtypeStruct((8, 128), F32)


def _in_hbm(a):
    return pltpu.with_memory_space_constraint(a, pltpu.HBM)


def _ag_start(name, wp, after):
    def body(w_ref, after_ref, send_sems, recv_sems, w_thru, token):
        x, y, c = _mesh_pos()
        mine = w_ref.at[2 * x + y, :, pl.ds(c * P_HALF, P_HALF), :]
        for j, (px, py) in enumerate(_other_chips(x, y)):
            _remote(mine, mine, send_sems, recv_sems, j, (px, py, c)).start()
        token[...] = jnp.zeros_like(token)

    return pl.pallas_call(
        body, name=name,
        out_shape=(pltpu.SemaphoreType.DMA((3,)), pltpu.SemaphoreType.DMA((3,)), pltpu.HBM(wp.shape, wp.dtype), _TOKEN),
        in_specs=(_HBM, _ANY), out_specs=(_SEM, _SEM, _HBM, pl.BlockSpec(memory_space=pltpu.VMEM)),
        input_output_aliases={0: 2}, compiler_params=_EFFECT,
    )(_in_hbm(wp), after)


def _ag_wait(name, send_sems, recv_sems, wp, after):
    def body(w_ref, send_sems, recv_sems, *rest):
        x, y, c = _mesh_pos()
        mine = w_ref.at[2 * x + y, :, pl.ds(c * P_HALF, P_HALF), :]
        for j, (px, py) in enumerate(_other_chips(x, y)):
            landed = w_ref.at[2 * px + py, :, pl.ds(c * P_HALF, P_HALF), :]
            cp = _remote(mine, landed, send_sems, recv_sems, j, (px, py, c))
            cp.wait_send()
            cp.wait_recv()

    return pl.pallas_call(
        body, name=name, out_shape=pltpu.HBM(wp.shape, wp.dtype),
        in_specs=(_HBM, _SEM, _SEM) + (_ANY,) * len(after), out_specs=_HBM,
        input_output_aliases={0: 0}, compiler_params=_EFFECT,
    )(wp, send_sems, recv_sems, *after)


def _ag_forward(wp):
    def body(w_in, o, send_sems, recv_sems):
        x, y, c = _mesh_pos()
        sib = (x, y, 1 - c)
        chips = _other_chips(x, y)
        sends = []
        for j, (px, py) in enumerate(chips):
            landed = o.at[2 * px + py, :, pl.ds(c * P_HALF, P_HALF), :]
            cp = _remote(landed, landed, send_sems, recv_sems, j, sib)
            cp.start()
            sends.append(cp)
        for j, (px, py) in enumerate(chips):
            passed = o.at[2 * px + py, :, pl.ds((1 - c) * P_HALF, P_HALF), :]
            _remote(passed, passed, send_sems, recv_sems, j, sib).wait_recv()
        for cp in sends:
            cp.wait_send()

    return pl.pallas_call(
        body, name="ag_forward",
        in_specs=[_ANY], out_specs=_ANY,
        out_shape=jax.ShapeDtypeStruct(wp.shape, wp.dtype),
        scratch_shapes=[pltpu.SemaphoreType.DMA((3,)), pltpu.SemaphoreType.DMA((3,))],
        input_output_aliases={0: 0},
    )(wp)


def _rs_chips_start(name, t):
    nl = t.shape[0]

    def body(t_ref, land_ref, send_sems, recv_sems, t_thru, land_thru, token):
        x, y, c = _mesh_pos()
        for j, (px, py) in enumerate(_other_chips(x, y)):
            _remote(t_ref.at[:, 2 * px + py], land_ref.at[j], send_sems, recv_sems, j, (px, py, c)).start()
        token[...] = jnp.zeros_like(token)

    land = lax.empty((3, nl, P_HALF, D_MODEL), BF16)
    return pl.pallas_call(
        body, name=name,
        out_shape=(pltpu.SemaphoreType.DMA((3,)), pltpu.SemaphoreType.DMA((3,)), pltpu.HBM(t.shape, t.dtype),
                   pltpu.HBM(land.shape, land.dtype), _TOKEN),
        in_specs=(_HBM, _HBM), out_specs=(_SEM, _SEM, _HBM, _HBM, pl.BlockSpec(memory_space=pltpu.VMEM)),
        input_output_aliases={0: 2, 1: 3}, compiler_params=_EFFECT,
    )(_in_hbm(t), _in_hbm(land))


def _rs_chips_wait(name, send_sems, recv_sems, t, land, after):
    def body(t_ref, land_ref, send_sems, recv_sems, *rest):
        x, y, c = _mesh_pos()
        for j, (px, py) in enumerate(_other_chips(x, y)):
            cp = _remote(t_ref.at[:, 2 * px + py], land_ref.at[j], send_sems, recv_sems, j, (px, py, c))
            cp.wait_send()
            cp.wait_recv()

    return pl.pallas_call(
        body, name=name, out_shape=(pltpu.HBM(t.shape, t.dtype), pltpu.HBM(land.shape, land.dtype)),
        in_specs=(_HBM, _HBM, _SEM, _SEM) + (_ANY,) * len(after), out_specs=(_HBM, _HBM),
        input_output_aliases={0: 0, 1: 1}, compiler_params=_EFFECT,
    )(t, land, send_sems, recv_sems, *after)[1]


def _rs_sibling_start(name, g):
    nl = g.shape[0]

    def body(g_ref, land_ref, send_sems, recv_sems, g_thru, land_thru, token):
        x, y, c = _mesh_pos()
        _remote(g_ref.at[:, :, pl.ds((1 - c) * P_HALF, P_HALF), :], land_ref, send_sems, recv_sems, 0,
                (x, y, 1 - c)).start()
        token[...] = jnp.zeros_like(token)

    land = lax.empty((nl, N_SHARD, P_HALF, D_MODEL), F32)
    return pl.pallas_call(
        body, name=name,
        out_shape=(pltpu.SemaphoreType.DMA((1,)), pltpu.SemaphoreType.DMA((1,)), pltpu.HBM(g.shape, g.dtype),
                   pltpu.HBM(land.shape, land.dtype), _TOKEN),
        in_specs=(_HBM, _HBM), out_specs=(_SEM, _SEM, _HBM, _HBM, pl.BlockSpec(memory_space=pltpu.VMEM)),
        input_output_aliases={0: 2, 1: 3}, compiler_params=_EFFECT,
    )(_in_hbm(g), _in_hbm(land))


def _rs_sibling_wait(name, send_sems, recv_sems, g, land, after):
    def body(g_ref, land_ref, send_sems, recv_sems, *rest):
        x, y, c = _mesh_pos()
        cp = _remote(g_ref.at[:, :, pl.ds((1 - c) * P_HALF, P_HALF), :], land_ref, send_sems, recv_sems, 0,
                     (x, y, 1 - c))
        cp.wait_send()
        cp.wait_recv()

    return pl.pallas_call(
        body, name=name, out_shape=(pltpu.HBM(g.shape, g.dtype), pltpu.HBM(land.shape, land.dtype)),
        in_specs=(_HBM, _HBM, _SEM, _SEM) + (_ANY,) * len(after), out_specs=(_HBM, _HBM),
        input_output_aliases={0: 0, 1: 1}, compiler_params=_EFFECT,
    )(g, land, send_sems, recv_sems, *after)


def _rs_add(name, ids, g, buf, row_tile):
    nl, _, hr, cols = buf.shape
    n_rt = hr // row_tile

    def body(ids_ref, g_ref, b_ref, own_ref, tb_ref):
        t = g_ref[...] + b_ref[...]
        tb_ref[...] = t.astype(BF16)

        @pl.when(pl.program_id(2) == ids_ref[1])
        def _():
            own_ref[...] = t

    blk = (None, None, row_tile, cols)
    grid_spec = pltpu.PrefetchScalarGridSpec(
        num_scalar_prefetch=1, grid=(nl, n_rt, N_SHARD),
        in_specs=[pl.BlockSpec(blk, lambda l, j, s, ids_ref: (l, s, ids_ref[0] * n_rt + j, 0)),
                  pl.BlockSpec(blk, lambda l, j, s, ids_ref: (l, s, j, 0))],
        out_specs=[pl.BlockSpec((None, row_tile, cols), lambda l, j, s, ids_ref: (l, j, 0)),
                   pl.BlockSpec(blk, lambda l, j, s, ids_ref: (l, s, j, 0))])
    return pl.pallas_call(
        body, name=name, grid_spec=grid_spec,
        out_shape=[jax.ShapeDtypeStruct((nl, hr, cols), F32), jax.ShapeDtypeStruct(buf.shape, BF16)],
        compiler_params=_cparams(3),
    )(ids, g, buf)


def _rs_sum(ids, layer, own, bufb, reduced, row_tile):
    _, hr, cols = own.shape
    n_rt = hr // row_tile

    def body(ids_ref, own_ref, b_ref, reduced_in, f_ref):
        f_ref[...] = ((own_ref[...] + b_ref[0].astype(F32)) + b_ref[1].astype(F32)) + b_ref[2].astype(F32)

    grid_spec = pltpu.PrefetchScalarGridSpec(
        num_scalar_prefetch=1, grid=(n_rt,),
        in_specs=[pl.BlockSpec((None, row_tile, cols), lambda j, ids_ref: (0, j, 0)),
                  pl.BlockSpec((3, None, row_tile, cols), lambda j, ids_ref: (0, 0, j, 0)),
                  pl.BlockSpec(memory_space=pl.ANY)],
        out_specs=pl.BlockSpec((None, row_tile, cols), lambda j, ids_ref: (layer, ids_ref[0] * n_rt + j, 0)))
    return pl.pallas_call(
        body, name="rs_sum", grid_spec=grid_spec,
        out_shape=jax.ShapeDtypeStruct(reduced.shape, F32),
        input_output_aliases={3: 0},
        compiler_params=_cparams(1),
    )(ids, own, bufb, reduced)


def _rs_exchange_start(name, f):
    def body(f_ref, send_sems, recv_sems, f_thru):
        x, y, c = _mesh_pos()
        mine = f_ref.at[:, pl.ds(c * P_HALF, P_HALF), :]
        _remote(mine, mine, send_sems, recv_sems, 0, (x, y, 1 - c)).start()

    return pl.pallas_call(
        body, name=name,
        out_shape=(pltpu.SemaphoreType.DMA((1,)), pltpu.SemaphoreType.DMA((1,)), pltpu.HBM(f.shape, f.dtype)),
        in_specs=(_HBM,), out_specs=(_SEM, _SEM, _HBM),
        input_output_aliases={0: 2}, compiler_params=_EFFECT,
    )(_in_hbm(f))


def _rs_exchange_wait(name, send_sems, recv_sems, f, after):
    def body(f_ref, send_sems, recv_sems, *rest):
        x, y, c = _mesh_pos()
        mine = f_ref.at[:, pl.ds(c * P_HALF, P_HALF), :]
        theirs = f_ref.at[:, pl.ds((1 - c) * P_HALF, P_HALF), :]
        cp = _remote(mine, theirs, send_sems, recv_sems, 0, (x, y, 1 - c))
        cp.wait_send()
        cp.wait_recv()

    return pl.pallas_call(
        body, name=name, out_shape=pltpu.HBM(f.shape, f.dtype),
        in_specs=(_HBM, _SEM, _SEM) + (_ANY,) * len(after), out_specs=_HBM,
        input_output_aliases={0: 0}, compiler_params=_EFFECT,
    )(f, send_sems, recv_sems, *after)


def _small_all_reduce(s):
    n_rows = s.shape[0]
    hr = n_rows // 2
    qr = hr // N_SHARD

    def body(s_ref, o_ref, sibbuf, tbuf, qbuf, fbuf, send_sems, recv_sems):
        x, y, c = _mesh_pos()
        k = 2 * x + y
        sib = (x, y, 1 - c)
        chips = _other_chips(x, y)
        mine = pl.ds(pl.multiple_of(c * hr, SUBLANES), hr)
        theirs = pl.ds(pl.multiple_of((1 - c) * hr, SUBLANES), hr)

        def quarter(shard):
            return pl.ds(pl.multiple_of(shard * qr, SUBLANES), qr)

        first = _remote(s_ref.at[theirs], sibbuf, send_sems, recv_sems, 0, sib)
        first.start()
        first.wait()
        tbuf[...] = s_ref[mine, :] + sibbuf[...]
        cps = []
        for j, (px, py) in enumerate(chips):
            cp = _remote(tbuf.at[quarter(2 * px + py)], qbuf.at[j], send_sems, recv_sems, 1 + j, (px, py, c))
            cp.start()
            cps.append(cp)
        for cp in cps:
            cp.wait()
        fbuf[quarter(k), :] = (tbuf[quarter(k), :] + qbuf[1]) + (qbuf[0] + qbuf[2])
        cps = []
        for j, (px, py) in enumerate(chips):
            cp = _remote(fbuf.at[quarter(k)], fbuf.at[quarter(k)], send_sems, recv_sems, 4 + j, (px, py, c))
            cp.start()
            cps.append(cp)
        for j, (px, py) in enumerate(chips):
            got = fbuf.at[quarter(2 * px + py)]
            _remote(got, got, send_sems, recv_sems, 4 + j, (px, py, c)).wait_recv()
        for cp in cps:
            cp.wait_send()
        o_ref[mine, :] = fbuf[...]
        last = _remote(fbuf, o_ref.at[mine], send_sems, recv_sems, 7, sib)
        last.start()
        last.wait()

    vmem = pl.BlockSpec(memory_space=pltpu.VMEM)
    return pl.pallas_call(
        body, name="small_all_reduce",
        in_specs=[vmem], out_specs=vmem,
        out_shape=jax.ShapeDtypeStruct(s.shape, F32),
        scratch_shapes=[pltpu.VMEM((hr, D_MODEL), F32), pltpu.VMEM((hr, D_MODEL), F32),
                        pltpu.VMEM((3, qr, D_MODEL), F32), pltpu.VMEM((hr, D_MODEL), F32),
                        pltpu.SemaphoreType.DMA((8,)), pltpu.SemaphoreType.DMA((8,))],
        compiler_params=pltpu.CompilerParams(vmem_limit_bytes=VMEM_LIMIT),
    )(s)


_SMALL = ("norm_mix", "w_pool", "pool_scale", "lam_re", "lam_im", "log_dt", "b_re", "b_im", "c_re", "c_im",
          "d_skip", "b_glu", "norm_ffn", "norm_final")
_WEIGHTS = ("norm_mix", "w_in", "w_pool", "pool_scale", "lam_re", "lam_im", "log_dt", "b_re", "b_im", "c_re",
            "c_im", "d_skip", "w_glu", "b_glu", "w_out", "norm_ffn", "w_gate", "w_up", "w_down", "norm_final")


def _local_step(x, target, p, get_weights, ffn_bwd_done, put_grads):
    nl = p["norm_mix"].shape[0]

    def tied(a, token):
        return a if token is None else a + token
    n_rows = nl * N_SSM_GROUPS
    lr = p["lam_re"].reshape(n_rows, 1, SSM_STATE)
    li = p["lam_im"].reshape(n_rows, 1, SSM_STATE)
    ldt = p["log_dt"].reshape(n_rows, 1, 1)
    br_t = p["b_re"].reshape(n_rows, SSM_STATE, SSM_GROUP).transpose(0, 2, 1)
    bi_t = p["b_im"].reshape(n_rows, SSM_STATE, SSM_GROUP).transpose(0, 2, 1)
    ar, ai, bbr_t, bbi_t = _disc_fwd(lr, li, ldt, br_t, bi_t)
    ar = ar.reshape(nl, 1, N_STATE)
    ai = ai.reshape(nl, 1, N_STATE)
    bbr = bbr_t.transpose(0, 2, 1).reshape(nl, N_SSM_GROUPS, SSM_STATE, SSM_GROUP)
    bbi = bbi_t.transpose(0, 2, 1).reshape(nl, N_SSM_GROUPS, SSM_STATE, SSM_GROUP)
    w_pool = p["w_pool"].astype(BF16)
    p = dict(p)
    for n in ("norm_mix", "pool_scale", "b_glu", "norm_ffn"):
        p[n] = p[n].reshape(nl, 1, -1)
    swap = lambda a: jnp.swapaxes(a, -1, -2)
    bpad = jax.vmap(_pad_pairs)(bbr, bbi).astype(BF16)
    cpad_t = jax.vmap(_pad_pairs)(swap(p["c_re"]), -swap(p["c_im"])).astype(BF16)
    bpad_t, cpad = swap(bpad), swap(cpad_t)
    dskip = p["d_skip"].reshape(nl, 1, D_SSM)

    layers = []
    h = x
    for l in range(nl):
        wp = get_weights(l, [h] if l else [h, bpad, cpad, bpad_t, cpad_t, ar, ai])
        u, ypool = _mix_in_fwd(h, p["norm_mix"], wp, l, w_pool, p["pool_scale"])
        sre, sim, yraw = _ssm_fwd(u, l, bpad, cpad, ar, ai, dskip)
        hm = _mix_out_fwd(yraw, ypool, h, wp, l, p["b_glu"])
        h_next, n2, act_s, fgate_s, fup_s = _ffn_fwd(hm, p["norm_ffn"], wp, l)
        layers.append(dict(h=h, u=u, ypool=ypool, sre=sre, sim=sim, yraw=yraw, hm=hm, n2=n2, act_s=act_s, wp=wp,
                           fgate_s=fgate_s, fup_s=fup_s))
        h = h_next

    dh, loss, d_norm_final = _final_fwd_bwd(h, p["norm_final"].reshape(1, D_MODEL), target)

    raw = {n: [None] * nl for n in ("dg1", "dwp", "dsc", "dcp", "dbp", "ddsk", "db_glu", "dg2", "dar", "dai")}
    token = None
    for l in reversed(range(nl)):
        s = layers[l]
        wp = s["wp"]
        g1 = lax.empty((1, N_SHARD, P_ROWS, D_MODEL), F32)
        dhm, dg2, dgate_s, dup_s, dhb = _ffn_bwd_act(dh, s["hm"], tied(p["norm_ffn"], token), s["fgate_s"],
                                                      s["fup_s"], wp, l)
        g1 = _ffn_bwd_w(s["n2"], dgate_s, dup_s, s["act_s"], dhb, g1)
        token = ffn_bwd_done(l, [g1])
        dyraw, dyp, db_glu, g1 = _mix_out_bwd(dhm, s["yraw"], s["ypool"], wp, l, tied(p["b_glu"], token), g1)
        dus, dcp, dbp, dar, dai, ddsk = _ssm_bwd(dyraw, s["u"], s["sre"], s["sim"], l, cpad_t, bpad_t, ar, ai, dskip)
        dup, dwp, dsc = _pool_bwd(dyp, s["u"], l, w_pool, p["pool_scale"])
        dh, dg1, g1 = _mix_in_bwd(dup, dus, s["h"], dhm, p["norm_mix"], wp, l, g1)
        token = put_grads(l, g1)
        for n, a in (("dg1", dg1), ("dwp", dwp), ("dsc", dsc), ("dcp", dcp), ("dbp", dbp), ("ddsk", ddsk),
                     ("db_glu", db_glu), ("dg2", dg2), ("dar", dar), ("dai", dai)):
            raw[n][l] = a

    st = {n: jnp.stack(v) for n, v in raw.items()}
    dc_re, dc_im = jax.vmap(_unpad_pairs)(swap(st["dcp"]))
    dbbr, dbbi = jax.vmap(_unpad_pairs)(st["dbp"])
    rows = lambda a: a.reshape((n_rows,) + a.shape[2:])
    dlr, dli, dldt, dbr_t, dbi_t = _disc_bwd(lr, li, ldt, br_t, bi_t, st["dar"].reshape(n_rows, 1, SSM_STATE),
                                              st["dai"].reshape(n_rows, 1, SSM_STATE), rows(swap(dbbr)),
                                              rows(swap(dbbi)))
    small = {"norm_mix": st["dg1"][:, 0], "w_pool": st["dwp"], "pool_scale": st["dsc"][:, 0], "c_re": swap(dc_re),
             "c_im": -swap(dc_im), "d_skip": st["ddsk"].reshape(nl, N_SSM_GROUPS, SSM_GROUP),
             "b_glu": st["db_glu"][:, 0], "norm_ffn": st["dg2"][:, 0]}
    small["lam_re"] = dlr.reshape(nl, N_SSM_GROUPS, SSM_STATE)
    small["lam_im"] = dli.reshape(nl, N_SSM_GROUPS, SSM_STATE)
    small["log_dt"] = dldt.reshape(nl, N_SSM_GROUPS)
    small["b_re"] = dbr_t.reshape(nl, N_SSM_GROUPS, SSM_GROUP, SSM_STATE)
    small["b_im"] = dbi_t.reshape(nl, N_SSM_GROUPS, SSM_GROUP, SSM_STATE)
    small["d_skip"] = small["d_skip"].transpose(_SMALL_VIEW["d_skip"])
    small["norm_final"] = d_norm_final
    return loss, dh, small


_SMALL_VIEW = {"b_re": (0, 1, 3, 2), "b_im": (0, 1, 3, 2), "d_skip": (0, 2, 1)}
_SMALL_GROUPS = (("b_re", "b_im"), ("c_re", "c_im"), ("lam_re", "lam_im"), ("norm_mix", "norm_ffn"),
                 ("pool_scale", "b_glu"), ("w_pool",), ("log_dt",), ("d_skip",), ("norm_final",))


def _view(n, a):
    a = a.transpose(_SMALL_VIEW[n]) if n in _SMALL_VIEW else a
    return a[None] if a.ndim == 1 else a


def _unview(n, a, shape):
    a = a.reshape(shape) if len(shape) == 1 else a
    return a.transpose(_SMALL_VIEW[n]) if n in _SMALL_VIEW else a


def _flatten_small(views):
    flat = jnp.concatenate([views[n].reshape(-1) for n in _SMALL])
    n_rows = -(-flat.shape[0] // (64 * D_MODEL)) * 64
    return jnp.pad(flat, (0, n_rows * D_MODEL - flat.shape[0])).reshape(n_rows, D_MODEL)


def _split_small(flat, like):
    flat = flat.reshape(-1)
    out, at = {}, 0
    for n in _SMALL:
        size = like[n].size
        out[n] = flat[at:at + size].reshape(like[n].shape)
        at += size
    return out


def _adamw_small(name, ws, ms, vs, gs):
    k = len(ws)

    def body(*refs):
        ins, outs = refs[:4 * k], refs[4 * k:]
        for i in range(k):
            w, m, v, g = (ins[j * k + i][...] for j in range(4))
            delta, mn, vn = _adamw_math(w, g, m, v)
            outs[i][...] = delta
            outs[k + i][...] = mn
            outs[2 * k + i][...] = vn

    shapes = [jax.ShapeDtypeStruct(w.shape, F32) for w in ws] * 3
    outs = pl.pallas_call(body, name=name, out_shape=shapes,
                          compiler_params=pltpu.CompilerParams(vmem_limit_bytes=VMEM_LIMIT))(*ws, *ms, *vs, *gs)
    return outs[:k], outs[k:2 * k], outs[2 * k:]


def kernel(x, norm_mix, w_in, w_pool, pool_scale, lam_re, lam_im, log_dt, b_re, b_im, c_re, c_im, d_skip, w_glu, b_glu, w_out, norm_ffn, w_gate, w_up, w_down, norm_final, loss_target, m_norm_mix, m_w_in, m_w_pool, m_pool_scale, m_lam_re, m_lam_im, m_log_dt, m_b_re, m_b_im, m_c_re, m_c_im, m_d_skip, m_w_glu, m_b_glu, m_w_out, m_norm_ffn, m_w_gate, m_w_up, m_w_down, m_norm_final, v_norm_mix, v_w_in, v_w_pool, v_pool_scale, v_lam_re, v_lam_im, v_log_dt, v_b_re, v_b_im, v_c_re, v_c_im, v_d_skip, v_w_glu, v_b_glu, v_w_out, v_norm_ffn, v_w_gate, v_w_up, v_w_down, v_norm_final):
    given = dict(locals())
    w = {n: given[n] for n in _WEIGHTS}
    m = {n: given["m_" + n] for n in _WEIGHTS}
    v = {n: given["v_" + n] for n in _WEIGHTS}
    ids = jnp.stack([lax.axis_index("c"), 2 * lax.axis_index("x") + lax.axis_index("y")]).astype(jnp.int32)

    t_names = ("w_gate", "w_up")
    tr = lambda a: a.transpose(0, 2, 1)
    for d in (w, m, v):
        d.update({n: tr(d[n]) for n in t_names})

    nl = norm_mix.shape[0]
    packed = [_pack_weights(ids, l, w["w_in"], w["w_glu"], w["w_out"], w["w_down"], w["w_gate"], w["w_up"])
              for l in range(nl)]
    started, last = {}, ids
    for l in range(nl):
        started[l] = _ag_start(f"ag_start_{l}", packed[l], last)
        last = started[l][3]
    views = [{n: _view(n, d[n]) for n in _SMALL} for d in (w, m, v)]

    def get_weights(l, after):
        send_sems, recv_sems, buf, _ = started[l]
        after = after + ([last] if l == 0 else [])
        return _ag_forward(_ag_wait(f"ag_wait_{l}", send_sems, recv_sems, buf, after))

    to_sibling, to_chips, reduced = {}, {}, {}

    def put_grads(l, g):
        to_sibling[l] = _rs_sibling_start(f"rs_sibling_start_{l}", g)
        token = to_sibling[l][4]
        if l + 1 in to_chips:
            finish(l + 1, [token])
        return token[:1, :1]

    def ffn_bwd_done(l, after):
        return send_to_chips(l + 1, after)[:1, :1] if l + 1 in to_sibling else None

    def send_to_chips(l, after):
        send_sems, recv_sems, g, land, _ = to_sibling.pop(l)
        g, land = _rs_sibling_wait(f"rs_sibling_wait_{l}", send_sems, recv_sems, g, land, after)
        own, t = _rs_add("rs_add", ids, g, land, RS_ROW_TILE)
        send_sems, recv_sems, t, land, token = _rs_chips_start(f"rs_chips_start_{l}", t)
        to_chips[l] = (send_sems, recv_sems, t, land, own)
        return token

    def finish(l, after):
        send_sems, recv_sems, t, land, own = to_chips.pop(l)
        land = _rs_chips_wait(f"rs_chips_wait_{l}", send_sems, recv_sems, t, land, after)
        shard = lax.empty((1, P_ROWS, D_MODEL), F32)
        reduced[l] = _rs_exchange_start(f"rs_exchange_start_{l}", _rs_sum(ids, 0, own, land, shard, RS_ROW_TILE))

    loss, grad_x, small = _local_step(x[0], loss_target[0], {n: w[n] for n in _SMALL}, get_weights, ffn_bwd_done,
                                      put_grads)
    loss = lax.psum(loss[0, 0], ("x", "y", "c"))
    small_flat = _flatten_small(small)
    token = send_to_chips(0, [small_flat])

    big = (("w_in", P_IN_BLK, 256, False), ("w_out", P_OUT_BLK, 256, False), ("w_down", P_WD_BLK, 352, False),
           ("w_gate", P_WG_BLK, 352, False), ("w_up", P_WU_BLK, 352, False), ("w_glu", P_GLU_BLK, 128, True))
    res = {n: None for n, *_ in big}

    def adamw_layer(l, after):
        send_sems, recv_sems, shard = reduced[l]
        shard = _rs_exchange_wait(f"rs_exchange_wait_{l}", send_sems, recv_sems, shard, after)
        for n, (blk, idx), row_tile, glu in big:
            res[n] = _adamw("adamw_" + n, l, w[n], m[n], v[n], shard, (blk, D_MODEL), blk * idx, row_tile, res[n], (), glu)

    for l in reversed(range(1, nl)):
        adamw_layer(l, [token])
    small_sum = _small_all_reduce(small_flat + token[:1, :1])
    finish(0, [small_sum] + [r[0] for r in res.values() if r is not None])
    adamw_layer(0, [])
    for n in t_names:
        res[n] = tuple(tr(a) for a in res[n])
    g_views = _split_small(small_sum, views[0])
    for group in _SMALL_GROUPS:
        deltas, new_ms, new_vs = _adamw_small("adamw_" + group[0], *[[d[n] for n in group] for d in views],
                                              [g_views[n] for n in group])
        for i, n in enumerate(group):
            res[n] = tuple(_unview(n, a, w[n].shape) for a in (g_views[n], deltas[i], new_ms[i], new_vs[i]))

    return (loss, grad_x[None], *[res[n][0] for n in _WEIGHTS], *[res[n][1] for n in _WEIGHTS],
            *[res[n][2] for n in _WEIGHTS], *[res[n][3] for n in _WEIGHTS])
```

```python
import functools
import math

import jax
import jax.numpy as jnp
from jax import lax
from jax.experimental import pallas as pl
from jax.experimental.pallas import tpu as pltpu

F32 = jnp.float32
BF16 = jnp.bfloat16

D_MODEL = 1024
D_POOL = 512
D_SSM = 512
POOL_WINDOWS = (2, 4, 8, 16)
POOL_GROUP = 128
POOL_HALO = 16
N_SSM_GROUPS = 32
SSM_GROUP = 16
SSM_STATE = 64
N_STATE = N_SSM_GROUPS * SSM_STATE
N_PAIRS = N_SSM_GROUPS // 2
D_FF = 2816
N_SHARD = 4
FF_SHARD = D_FF // N_SHARD
RMS_EPS = 1e-6

ADAM_LR = 0.001
ADAM_B1 = 0.9
ADAM_B2 = 0.999
ADAM_EPS = 1e-08
ADAM_WD = 0.01
ADAM_STEP = 10

P_ROWS = 2816
P_WD_BLK = (704, 0)
P_WG_BLK = (704, 1)
P_WU_BLK = (704, 2)
P_FF_ROWS = 2112
P_GLU_BLK = (64, 33)
P_GLU_PAD = 192
P_IN_BLK = (256, 9)
P_OUT_BLK = (256, 10)

SUBLANES = 8
VMEM_LIMIT = 56 * 1024 * 1024

TM = 1024
TM_FFN = 512
TM_FFN_LONG = 1024
FFN_SPLIT = 2
TS = 1024
SCAN_LANES = 512


def _cparams(n_axes):
    return pltpu.CompilerParams(dimension_semantics=("arbitrary",) * n_axes, vmem_limit_bytes=VMEM_LIMIT)


def _dot(a, b):
    return jnp.dot(a, b, preferred_element_type=F32)


def _dot_nt(a, b):
    return lax.dot_general(a, b, (((1,), (1,)), ((), ())), preferred_element_type=F32)


def _dot_tn(a, b):
    return lax.dot_general(a, b, (((0,), (0,)), ((), ())), preferred_element_type=F32)


def _rms_hat(x):
    r = lax.rsqrt(jnp.mean(x * x, axis=-1, keepdims=True) + RMS_EPS)
    return x * r, r


def _rms_bwd(d_hat, xhat, r):
    return r * (d_hat - xhat * jnp.mean(d_hat * xhat, axis=-1, keepdims=True))


def _sigmoid(x):
    return 1.0 / (1.0 + jnp.exp(-x))


_GELU_C = math.sqrt(2.0 / math.pi)
_GELU_K = 0.044715


def _gelu(x):
    return 0.5 * x * (1.0 + jnp.tanh(_GELU_C * (x + _GELU_K * x * x * x)))


def _gelu_grad(x):
    th = jnp.tanh(_GELU_C * (x + _GELU_K * x * x * x))
    return 0.5 * (1.0 + th) + 0.5 * x * (1.0 - th * th) * _GELU_C * (1.0 + 3.0 * _GELU_K * x * x)


def _glu_weight(ref):
    v = ref[...]
    return jnp.concatenate([v[:, :, :D_SSM], v[:, :, D_SSM:]], axis=1).reshape(D_SSM, D_SSM)


def _glu_pack(w):
    v = w.reshape(N_SHARD, 128, D_SSM)
    return jnp.concatenate([v[:, :64, :], v[:, 64:, :]], axis=2)


def _pool_diff(ext, row0, tm):
    rows = row0 + lax.broadcasted_iota(jnp.int32, (tm, 1), 0)
    outs = []
    for gi, w in enumerate(POOL_WINDOWS):
        e = ext[:, gi * POOL_GROUP:(gi + 1) * POOL_GROUP]
        s = e
        k = 1
        while k < w:
            s = s + pltpu.roll(s, k, 0)
            k *= 2
        inv = 1.0 / jnp.minimum(rows + 1, w).astype(F32)
        outs.append(s[POOL_HALO:, :] * inv - e[POOL_HALO:, :])
    return outs


def _mix_in_fwd(h, g1, wp, layer, w_pool, scale):
    L = h.shape[0]
    tm = min(TM, L)

    def body(h_ref, g_ref, w_ref, wp_ref, sc_ref, u_ref, yp_ref, carry):
        i = pl.program_id(0)

        @pl.when(i == 0)
        def _():
            carry[...] = jnp.zeros_like(carry)

        xhat, _ = _rms_hat(h_ref[...])
        n1 = (xhat * g_ref[...]).astype(BF16)
        u = _dot(n1, w_ref[...].reshape(D_MODEL, D_MODEL))
        u_ref[...] = u
        up = u[:, :D_POOL]
        ext = jnp.concatenate([carry[...], up], axis=0)
        carry[...] = up[tm - POOL_HALO:, :]
        diffs = _pool_diff(ext, i * tm, tm)
        for gi in range(4):
            cols = slice(gi * POOL_GROUP, (gi + 1) * POOL_GROUP)
            yp_ref[:, cols] = _dot(diffs[gi].astype(BF16), wp_ref[gi]) * sc_ref[:, cols]

    blk, idx = P_IN_BLK
    return pl.pallas_call(
        body, name="mix_in_fwd", grid=(L // tm,),
        in_specs=[pl.BlockSpec((tm, D_MODEL), lambda i: (i, 0)),
                  pl.BlockSpec((None, 1, D_MODEL), lambda i: (layer, 0, 0)),
                  pl.BlockSpec((N_SHARD, None, blk, D_MODEL), lambda i: (0, 0, idx, 0)),
                  pl.BlockSpec((None, 4, POOL_GROUP, POOL_GROUP), lambda i: (layer, 0, 0, 0)),
                  pl.BlockSpec((None, 1, D_POOL), lambda i: (layer, 0, 0))],
        out_specs=[pl.BlockSpec((tm, D_MODEL), lambda i: (i, 0)),
                   pl.BlockSpec((tm, D_POOL), lambda i: (i, 0))],
        out_shape=[jax.ShapeDtypeStruct((L, D_MODEL), F32), jax.ShapeDtypeStruct((L, D_POOL), F32)],
        scratch_shapes=[pltpu.VMEM((POOL_HALO, D_POOL), F32)],
        compiler_params=_cparams(1),
    )(h, g1, wp, w_pool, scale)


def _cmul(xr, xi, yr, yi):
    return xr * yr - xi * yi, xr * yi + xi * yr


SCAN_BLOCK = 64
N_SCAN_TABLES = 26


def _permute_rows(src, dst, n_rows):
    for b in range(n_rows // SCAN_BLOCK):
        for tau in range(SUBLANES):
            dst[pl.ds(SCAN_BLOCK * b + SUBLANES * tau, SUBLANES), :] = (
                src[pl.ds(SCAN_BLOCK * b + tau, SUBLANES, stride=SUBLANES), :])


def _scan_tables(ar, ai, tab, reverse):
    c = ar.shape[1]
    row = lax.broadcasted_iota(jnp.int32, (SUBLANES, c), 0)
    zero = jnp.zeros((SUBLANES, c), F32)
    full = lambda v: jnp.broadcast_to(v, (SUBLANES, c))
    pw = [(ar, ai)]
    for _ in range(SUBLANES - 1):
        pw.append(_cmul(*pw[-1], ar, ai))
    a8 = pw[-1]
    a16 = _cmul(*a8, *a8)
    a32 = _cmul(*a16, *a16)
    tab[0] = full(ar)
    tab[1] = full(ai)
    for n, (s, (pr, pi)) in enumerate(((1, a8), (2, a16), (4, a32))):
        keep = (row < SUBLANES - s) if reverse else (row >= s)
        tab[2 + 2 * n] = jnp.where(keep, pr, zero)
        tab[3 + 2 * n] = jnp.where(keep, pi, zero)
    cur = a8
    qr, qi = zero, zero
    for n in range(SUBLANES):
        at = (SUBLANES - 1 - n) if reverse else n
        qr = jnp.where(row == at, cur[0], qr)
        qi = jnp.where(row == at, cur[1], qi)
        cur = _cmul(*cur, *a8)
    tab[8] = qr
    tab[9] = qi
    for tau in range(SUBLANES):
        pr, pi = pw[SUBLANES - 1 - tau] if reverse else pw[tau]
        tab[10 + 2 * tau] = full(pr)
        tab[11 + 2 * tau] = full(pi)


def _cmac(xr, xi, ar, ai, yr, yi):
    return xr + ar * yr - ai * yi, xi + ar * yi + ai * yr


def _chain_segments(er, ei, c_r, c_i, tab, cols, reverse):
    tr, ti = er, ei
    for n, s in enumerate((1, 2, 4)):
        shift = SUBLANES - s if reverse else s
        tr, ti = _cmac(tr, ti, tab[2 + 2 * n, :, cols], tab[3 + 2 * n, :, cols],
                       pltpu.roll(tr, shift, 0), pltpu.roll(ti, shift, 0))
    return _cmac(tr, ti, tab[8, :, cols], tab[9, :, cols], c_r, c_i)


def _ssm_fwd(u, layer, bpad, cpad, ar, ai, dskip):
    L = u.shape[0]
    ts = min(TS, L)
    nq = 4
    cq = N_STATE // nq

    def body(u_ref, bp_ref, cp_ref, ar_ref, ai_ref, dsk_ref, sre_ref, sim_ref, y_ref, cr, ci, tab, up, yp):
        t = pl.program_id(1)

        @pl.when(t == 0)
        def _():
            cr[...] = jnp.zeros_like(cr)
            ci[...] = jnp.zeros_like(ci)
            _scan_tables(ar_ref[...], ai_ref[...], tab, reverse=False)

        _permute_rows(u_ref, up, ts)
        uf = up[...]
        ub = uf.astype(BF16)
        for jj in range(4):
            bu = _dot(ub, bp_ref[jj])
            sre_ref[:, jj * 128:(jj + 1) * 128] = bu[:, :128]
            sim_ref[:, jj * 128:(jj + 1) * 128] = bu[:, 128:]

        shp = (SUBLANES, SCAN_LANES)
        first_row = lax.broadcasted_iota(jnp.int32, shp, 0) == 0
        for cc in range(cq // SCAN_LANES):
            cols = slice(cc * SCAN_LANES, (cc + 1) * SCAN_LANES)

            def block(b, carry, cols=cols):
                c_r, c_i = carry
                base = pl.multiple_of(b * SCAN_BLOCK, SCAN_BLOCK)
                rows = lambda tau: pl.ds(base + SUBLANES * tau, SUBLANES)
                a_r, a_i = tab[0, :, cols], tab[1, :, cols]
                ys = [(sre_ref[rows(0), cols], sim_ref[rows(0), cols])]
                for tau in range(1, SUBLANES):
                    ys.append(_cmac(sre_ref[rows(tau), cols], sim_ref[rows(tau), cols], a_r, a_i, *ys[-1]))
                tr, ti = _chain_segments(*ys[-1], c_r, c_i, tab, cols, reverse=False)
                in_r = jnp.where(first_row, c_r, pltpu.roll(tr, 1, 0))
                in_i = jnp.where(first_row, c_i, pltpu.roll(ti, 1, 0))
                for tau in range(SUBLANES):
                    sr, si = _cmac(*ys[tau], tab[10 + 2 * tau, :, cols], tab[11 + 2 * tau, :, cols], in_r, in_i)
                    sre_ref[rows(tau), cols] = sr
                    sim_ref[rows(tau), cols] = si
                return (jnp.broadcast_to(tr[SUBLANES - 1:, :], shp), jnp.broadcast_to(ti[SUBLANES - 1:, :], shp))

            c_r, c_i = lax.fori_loop(0, ts // SCAN_BLOCK, block, (cr[:, cols], ci[:, cols]), unroll=2)
            cr[:, cols] = c_r
            ci[:, cols] = c_i

        acc = dsk_ref[...] * uf
        for jj in range(4):
            cols = slice(jj * 128, (jj + 1) * 128)
            scat = jnp.concatenate([sre_ref[:, cols], sim_ref[:, cols]], axis=1).astype(BF16)
            acc = acc + _dot(scat, cp_ref[jj])
        yp[...] = acc
        _permute_rows(yp, y_ref, ts)

    return pl.pallas_call(
        body, name="ssm_fwd", grid=(nq, L // ts),
        in_specs=[pl.BlockSpec((ts, 128), lambda q, t: (t, 4 + q)),
                  pl.BlockSpec((None, 4, 128, 256), lambda q, t: (layer, q, 0, 0)),
                  pl.BlockSpec((None, 4, 256, 128), lambda q, t: (layer, q, 0, 0)),
                  pl.BlockSpec((None, 1, cq), lambda q, t: (layer, 0, q)),
                  pl.BlockSpec((None, 1, cq), lambda q, t: (layer, 0, q)),
                  pl.BlockSpec((None, 1, 128), lambda q, t: (layer, 0, q))],
        out_specs=[pl.BlockSpec((ts, cq), lambda q, t: (t, q)),
                   pl.BlockSpec((ts, cq), lambda q, t: (t, q)),
                   pl.BlockSpec((ts, 128), lambda q, t: (t, q))],
        out_shape=[jax.ShapeDtypeStruct((L, N_STATE), F32), jax.ShapeDtypeStruct((L, N_STATE), F32),
                   jax.ShapeDtypeStruct((L, D_SSM), F32)],
        scratch_shapes=[pltpu.VMEM((SUBLANES, cq), F32), pltpu.VMEM((SUBLANES, cq), F32),
                        pltpu.VMEM((N_SCAN_TABLES, SUBLANES, cq), F32),
                        pltpu.VMEM((ts, 128), F32), pltpu.VMEM((ts, 128), F32)],
        compiler_params=_cparams(2),
    )(u, bpad, cpad, ar, ai, dskip)


def _mix_out_fwd(yraw, ypool, h, wp, layer, b_glu):
    L = h.shape[0]
    tm = min(TM, L)

    def body(yr_ref, yp_ref, h_ref, wglu_ref, b_ref, wout_ref, o_ref):
        y = _gelu(yr_ref[...])
        z = _dot(y.astype(BF16), _glu_weight(wglu_ref)) + b_ref[...]
        o = y * _sigmoid(z)
        mix = jnp.concatenate([yp_ref[...], o], axis=1).astype(BF16)
        o_ref[...] = h_ref[...] + _dot(mix, wout_ref[...].reshape(D_MODEL, D_MODEL))

    gb, gi = P_GLU_BLK
    ob, oi = P_OUT_BLK
    return pl.pallas_call(
        body, name="mix_out_fwd", grid=(L // tm,),
        in_specs=[pl.BlockSpec((tm, D_SSM), lambda i: (i, 0)),
                  pl.BlockSpec((tm, D_POOL), lambda i: (i, 0)),
                  pl.BlockSpec((tm, D_MODEL), lambda i: (i, 0)),
                  pl.BlockSpec((N_SHARD, None, gb, D_MODEL), lambda i: (0, 0, gi, 0)),
                  pl.BlockSpec((None, 1, D_SSM), lambda i: (layer, 0, 0)),
                  pl.BlockSpec((N_SHARD, None, ob, D_MODEL), lambda i: (0, 0, oi, 0))],
        out_specs=pl.BlockSpec((tm, D_MODEL), lambda i: (i, 0)),
        out_shape=jax.ShapeDtypeStruct((L, D_MODEL), F32),
        compiler_params=_cparams(1),
    )(yraw, ypool, h, wp, b_glu, wp)


def _ffn_weights(ref, k):
    return ref[k, 0:FF_SHARD, :], ref[k, FF_SHARD:2 * FF_SHARD, :], ref[k, 2 * FF_SHARD:P_FF_ROWS, :]


def _ffn_weight_spec():
    return pl.BlockSpec((N_SHARD, None, P_FF_ROWS, D_MODEL), lambda m, k: (0, 0, 0, 0),
                        pipeline_mode=pl.Buffered(1))


def _ffn_fwd(h, g2, wp, layer):
    L = h.shape[0]
    tm = min(TM_FFN_LONG, L)

    def body(h_ref, g_ref, w_ref, o_ref, n2_ref, act_ref, dgate_ref, dup_ref):
        k = pl.program_id(1)

        @pl.when(k == 0)
        def _():
            x = h_ref[...]
            xhat, _ = _rms_hat(x)
            n2_ref[...] = (xhat * g_ref[...]).astype(BF16)
            o_ref[...] = x

        wd, wg_t, wu_t = _ffn_weights(w_ref, k)
        n2 = n2_ref[...]
        gate = _dot_nt(n2, wg_t)
        up = _dot_nt(n2, wu_t)
        sg = _sigmoid(gate)
        silu = gate * sg
        act = (silu * up).astype(BF16)
        act_ref[...] = act
        dgate_ref[...] = (up * (sg * (1.0 + gate * (1.0 - sg)))).astype(BF16)
        dup_ref[...] = silu.astype(BF16)
        o_ref[...] += _dot(act, wd)

    act_shape = jax.ShapeDtypeStruct((N_SHARD, L, FF_SHARD), BF16)
    return pl.pallas_call(
        body, name="ffn_fwd", grid=(L // tm, N_SHARD),
        in_specs=[pl.BlockSpec((tm, D_MODEL), lambda m, k: (m, 0)),
                  pl.BlockSpec((None, 1, D_MODEL), lambda m, k: (layer, 0, 0)),
                  _ffn_weight_spec()],
        out_specs=[pl.BlockSpec((tm, D_MODEL), lambda m, k: (m, 0)),
                   pl.BlockSpec((tm, D_MODEL), lambda m, k: (m, 0)),
                   pl.BlockSpec((None, tm, FF_SHARD), lambda m, k: (k, m, 0)),
                   pl.BlockSpec((None, tm, FF_SHARD), lambda m, k: (k, m, 0)),
                   pl.BlockSpec((None, tm, FF_SHARD), lambda m, k: (k, m, 0))],
        out_shape=[jax.ShapeDtypeStruct((L, D_MODEL), F32), jax.ShapeDtypeStruct((L, D_MODEL), BF16),
                   act_shape, act_shape, act_shape],
        compiler_params=_cparams(2),
    )(h, g2, wp)


def _final_fwd_bwd(h, gf, target):
    L = h.shape[0]
    tm = min(TM, L)

    def body(h_ref, g_ref, t_ref, dh_ref, loss_ref, dg_ref):
        i = pl.program_id(0)

        @pl.when(i == 0)
        def _():
            loss_ref[...] = jnp.zeros_like(loss_ref)
            dg_ref[...] = jnp.zeros_like(dg_ref)

        xhat, r = _rms_hat(h_ref[...])
        g = g_ref[...]
        e = xhat * g - t_ref[...]
        loss_ref[...] += 0.5 * jnp.sum(jnp.mean(e * e, axis=-1, keepdims=True), axis=0, keepdims=True)
        dy = e * (1.0 / D_MODEL)
        dg_ref[...] += jnp.sum(dy * xhat, axis=0, keepdims=True)
        dh_ref[...] = _rms_bwd(dy * g, xhat, r)

    return pl.pallas_call(
        body, name="final_fwd_bwd", grid=(L // tm,),
        in_specs=[pl.BlockSpec((tm, D_MODEL), lambda i: (i, 0)),
                  pl.BlockSpec((1, D_MODEL), lambda i: (0, 0)),
                  pl.BlockSpec((tm, D_MODEL), lambda i: (i, 0))],
        out_specs=[pl.BlockSpec((tm, D_MODEL), lambda i: (i, 0)),
                   pl.BlockSpec((1, 1), lambda i: (0, 0)),
                   pl.BlockSpec((1, D_MODEL), lambda i: (0, 0))],
        out_shape=[jax.ShapeDtypeStruct((L, D_MODEL), F32), jax.ShapeDtypeStruct((1, 1), F32),
                   jax.ShapeDtypeStruct((1, D_MODEL), F32)],
        compiler_params=_cparams(1),
    )(h, gf, target)


def _ffn_bwd_act(dh, h, g2, fgate_s, fup_s, wp, layer):
    L = h.shape[0]
    tm = min(TM_FFN, L)
    sub = tm // FFN_SPLIT

    def body(dh_ref, h_ref, g_ref, fgate_ref, fup_ref, w_ref,
             dhm_ref, dg_ref, dgate_ref, dup_ref, dhb_ref, dn2):
        m, k = pl.program_id(0), pl.program_id(1)

        @pl.when(jnp.logical_and(m == 0, k == 0))
        def _():
            dg_ref[...] = jnp.zeros_like(dg_ref)

        @pl.when(k == 0)
        def _():
            dhb_ref[...] = dh_ref[...].astype(BF16)
            dn2[...] = jnp.zeros_like(dn2)

        wd, wg_t, wu_t = _ffn_weights(w_ref, k)
        for rows in (slice(r * sub, (r + 1) * sub) for r in range(tm // sub)):
            dact = _dot_nt(dhb_ref[rows, :], wd)
            dgate = (dact * fgate_ref[rows, :].astype(F32)).astype(BF16)
            dup = (dact * fup_ref[rows, :].astype(F32)).astype(BF16)
            dgate_ref[rows, :] = dgate
            dup_ref[rows, :] = dup
            dn2[rows, :] += _dot(dgate, wg_t) + _dot(dup, wu_t)

        @pl.when(k == N_SHARD - 1)
        def _():
            xhat, r = _rms_hat(h_ref[...])
            d = dn2[...]
            dg_ref[...] += jnp.sum(d * xhat, axis=0, keepdims=True)
            dhm_ref[...] = dh_ref[...] + _rms_bwd(d * g_ref[...], xhat, r)

    act_spec = pl.BlockSpec((None, tm, FF_SHARD), lambda m, k: (k, m, 0))
    act_shape = jax.ShapeDtypeStruct((N_SHARD, L, FF_SHARD), BF16)
    row_spec = pl.BlockSpec((tm, D_MODEL), lambda m, k: (m, 0))
    return pl.pallas_call(
        body, name="ffn_bwd_act", grid=(L // tm, N_SHARD),
        in_specs=[row_spec, row_spec,
                  pl.BlockSpec((None, 1, D_MODEL), lambda m, k: (layer, 0, 0)),
                  act_spec, act_spec,
                  _ffn_weight_spec()],
        out_specs=[row_spec,
                   pl.BlockSpec((1, D_MODEL), lambda m, k: (0, 0)),
                   act_spec, act_spec, row_spec],
        out_shape=[jax.ShapeDtypeStruct((L, D_MODEL), F32), jax.ShapeDtypeStruct((1, D_MODEL), F32),
                   act_shape, act_shape, jax.ShapeDtypeStruct((L, D_MODEL), BF16)],
        scratch_shapes=[pltpu.VMEM((tm, D_MODEL), F32)],
        compiler_params=_cparams(2),
    )(dh, h, g2, fgate_s, fup_s, wp)


def _ffn_bwd_w(n2, dgate_s, dup_s, act_s, dhb, gbuf):
    L = n2.shape[0]
    tm = min(TM_FFN_LONG, L)

    def body(n2_ref, dgate_ref, dup_ref, act_ref, dhb_ref, g_in, g_ref):
        m = pl.program_id(1)

        @pl.when(m == 0)
        def _():
            g_ref[...] = jnp.zeros_like(g_ref)

        n2v = n2_ref[...]
        g_ref[0:FF_SHARD, :] += _dot_tn(act_ref[...], dhb_ref[...])
        g_ref[FF_SHARD:2 * FF_SHARD, :] += _dot_tn(dgate_ref[...], n2v)
        g_ref[2 * FF_SHARD:P_FF_ROWS, :] += _dot_tn(dup_ref[...], n2v)

    act_spec = pl.BlockSpec((None, tm, FF_SHARD), lambda k, m: (k, m, 0))
    row_spec = pl.BlockSpec((tm, D_MODEL), lambda k, m: (m, 0))
    return pl.pallas_call(
        body, name="ffn_bwd_w", grid=(N_SHARD, L // tm),
        in_specs=[row_spec, act_spec, act_spec, act_spec, row_spec, pl.BlockSpec(memory_space=pl.ANY)],
        out_specs=pl.BlockSpec((None, None, P_FF_ROWS, D_MODEL), lambda k, m: (0, k, 0, 0)),
        out_shape=jax.ShapeDtypeStruct(gbuf.shape, F32),
        input_output_aliases={5: 0},
        compiler_params=_cparams(2),
    )(n2, dgate_s, dup_s, act_s, dhb, gbuf)


def _mix_out_bwd(dhm, yraw, ypool, wp, layer, b_glu, gbuf):
    L = dhm.shape[0]
    tm = min(TM, L)

    def body(dhm_ref, yr_ref, yp_ref, wglu_ref, b_ref, wout_ref, g1_in,
             dyr_ref, dyp_ref, db_ref, g1_ref, dwout, dwglu, gpack):
        i = pl.program_id(0)

        @pl.when(i == 0)
        def _():
            db_ref[...] = jnp.zeros_like(db_ref)
            dwout[...] = jnp.zeros_like(dwout)
            dwglu[...] = jnp.zeros_like(dwglu)

        dhb = dhm_ref[...].astype(BF16)
        wglu = _glu_weight(wglu_ref)
        dmix = _dot_nt(dhb, wout_ref[...].reshape(D_MODEL, D_MODEL))
        dyp_ref[...] = dmix[:, :D_POOL]
        d_o = dmix[:, D_POOL:]
        yraw_v = yr_ref[...]
        y = _gelu(yraw_v)
        yb = y.astype(BF16)
        sig = _sigmoid(_dot(yb, wglu) + b_ref[...])
        mix = jnp.concatenate([yp_ref[...], y * sig], axis=1).astype(BF16)
        dwout[...] += _dot_tn(mix, dhb).reshape(N_SHARD, 256, D_MODEL)
        dz = d_o * y * sig * (1.0 - sig)
        dzb = dz.astype(BF16)
        db_ref[...] += jnp.sum(dz, axis=0, keepdims=True)
        dwglu[...] += _dot_tn(yb, dzb)
        dy = d_o * sig + _dot_nt(dzb, wglu)
        dyr_ref[...] = dy * _gelu_grad(yraw_v)

        @pl.when(i == n_steps - 1)
        def _():
            gpack[:, :gb, :] = _glu_pack(dwglu[...])
            gpack[:, gb:, :] = jnp.zeros((N_SHARD, P_GLU_PAD - gb, D_MODEL), F32)
            pltpu.sync_copy(gpack, g1_ref.at[0, :, pl.ds(gb * gi, P_GLU_PAD), :])
            pltpu.sync_copy(dwout, g1_ref.at[0, :, pl.ds(ob * oi, ob), :])

    gb, gi = P_GLU_BLK
    ob, oi = P_OUT_BLK
    n_steps = L // tm
    return pl.pallas_call(
        body, name="mix_out_bwd", grid=(n_steps,),
        in_specs=[pl.BlockSpec((tm, D_MODEL), lambda i: (i, 0)),
                  pl.BlockSpec((tm, D_SSM), lambda i: (i, 0)),
                  pl.BlockSpec((tm, D_POOL), lambda i: (i, 0)),
                  pl.BlockSpec((N_SHARD, None, gb, D_MODEL), lambda i: (0, 0, gi, 0)),
                  pl.BlockSpec((None, 1, D_SSM), lambda i: (layer, 0, 0)),
                  pl.BlockSpec((N_SHARD, None, ob, D_MODEL), lambda i: (0, 0, oi, 0)),
                  pl.BlockSpec(memory_space=pl.ANY)],
        out_specs=[pl.BlockSpec((tm, D_SSM), lambda i: (i, 0)),
                   pl.BlockSpec((tm, D_POOL), lambda i: (i, 0)),
                   pl.BlockSpec((1, D_SSM), lambda i: (0, 0)),
                   pl.BlockSpec(memory_space=pl.ANY)],
        out_shape=[jax.ShapeDtypeStruct((L, D_SSM), F32), jax.ShapeDtypeStruct((L, D_POOL), F32),
                   jax.ShapeDtypeStruct((1, D_SSM), F32),
                   jax.ShapeDtypeStruct(gbuf.shape, F32)],
        scratch_shapes=[pltpu.VMEM((N_SHARD, ob, D_MODEL), F32), pltpu.VMEM((D_SSM, D_SSM), F32),
                        pltpu.VMEM((N_SHARD, P_GLU_PAD, D_MODEL), F32)],
        input_output_aliases={6: 3},
        compiler_params=_cparams(1),
    )(dhm, yraw, ypool, wp, b_glu, wp, gbuf)


def _ssm_bwd(dyraw, u, sre, sim, layer, cpad_t, bpad_t, ar, ai, dskip):
    L = u.shape[0]
    ts = min(TS, L)
    nt = L // ts
    nq = 4
    cq = N_STATE // nq

    def body(dy_ref, u_ref, sre_ref, sim_ref, ct_ref, bt_ref, ar_ref, ai_ref, dsk_ref,
             du_ref, dcp_ref, dbp_ref, dar_ref, dai_ref, ddsk_ref, gre, gim, cr, ci, tab, accr, acci, up, dyp):
        t = pl.program_id(1)

        @pl.when(t == 0)
        def _():
            for ref in (cr, ci, accr, acci, dcp_ref, dbp_ref, ddsk_ref):
                ref[...] = jnp.zeros_like(ref)
            _scan_tables(ar_ref[...], -ai_ref[...], tab, reverse=True)

        _permute_rows(dy_ref, dyp, ts)
        _permute_rows(u_ref, up, ts)
        dy = dyp[...]
        dyb = dy.astype(BF16)
        uf = up[...]
        ub = uf.astype(BF16)
        for jj in range(4):
            cols = slice(jj * 128, (jj + 1) * 128)
            ds = _dot(dyb, ct_ref[jj])
            gre[:, cols] = ds[:, :128]
            gim[:, cols] = ds[:, 128:]
            scat = jnp.concatenate([sre_ref[:, cols], sim_ref[:, cols]], axis=1).astype(BF16)
            dcp_ref[jj] += _dot_tn(scat, dyb)

        n_blk = ts // SCAN_BLOCK
        shp = (SUBLANES, SCAN_LANES)
        last_row = lax.broadcasted_iota(jnp.int32, shp, 0) == SUBLANES - 1
        for cc in range(cq // SCAN_LANES):
            cols = slice(cc * SCAN_LANES, (cc + 1) * SCAN_LANES)

            def block(i, carry, cols=cols):
                c_r, c_i, a_r, a_i = carry
                base = pl.multiple_of((n_blk - 1 - i) * SCAN_BLOCK, SCAN_BLOCK)
                rows = lambda tau: pl.ds(base + SUBLANES * tau, SUBLANES)
                m_r, m_i = tab[0, :, cols], tab[1, :, cols]
                ys = [None] * SUBLANES
                ys[SUBLANES - 1] = (gre[rows(SUBLANES - 1), cols], gim[rows(SUBLANES - 1), cols])
                for tau in reversed(range(SUBLANES - 1)):
                    ys[tau] = _cmac(gre[rows(tau), cols], gim[rows(tau), cols], m_r, m_i, *ys[tau + 1])
                tr, ti = _chain_segments(*ys[0], c_r, c_i, tab, cols, reverse=True)
                in_r = jnp.where(last_row, c_r, pltpu.roll(tr, SUBLANES - 1, 0))
                in_i = jnp.where(last_row, c_i, pltpu.roll(ti, SUBLANES - 1, 0))
                gs = [_cmac(*ys[tau], tab[10 + 2 * tau, :, cols], tab[11 + 2 * tau, :, cols], in_r, in_i)
                      for tau in range(SUBLANES)]
                for tau in range(SUBLANES):
                    gre[rows(tau), cols] = gs[tau][0]
                    gim[rows(tau), cols] = gs[tau][1]
                    if tau < SUBLANES - 1:
                        nr, ni = gs[tau + 1]
                    else:
                        nr = jnp.where(last_row, c_r, pltpu.roll(gs[0][0], SUBLANES - 1, 0))
                        ni = jnp.where(last_row, c_i, pltpu.roll(gs[0][1], SUBLANES - 1, 0))
                    sr, si = sre_ref[rows(tau), cols], sim_ref[rows(tau), cols]
                    a_r = a_r + sr * nr + si * ni
                    a_i = a_i + sr * ni - si * nr
                return (jnp.broadcast_to(tr[:1, :], shp), jnp.broadcast_to(ti[:1, :], shp), a_r, a_i)

            c_r, c_i, a_r, a_i = lax.fori_loop(
                0, n_blk, block, (cr[:, cols], ci[:, cols], accr[:, cols], acci[:, cols]), unroll=2)
            cr[:, cols] = c_r
            ci[:, cols] = c_i
            accr[:, cols] = a_r
            acci[:, cols] = a_i

        acc = dsk_ref[...] * dy
        for jj in range(4):
            cols = slice(jj * 128, (jj + 1) * 128)
            gcat = jnp.concatenate([gre[:, cols], gim[:, cols]], axis=1).astype(BF16)
            acc = acc + _dot(gcat, bt_ref[jj])
            dbp_ref[jj] += _dot_tn(ub, gcat)
        ddsk_ref[...] += jnp.sum(dy * uf, axis=0, keepdims=True)
        dyp[...] = acc
        _permute_rows(dyp, du_ref, ts)

        @pl.when(t == nt - 1)
        def _():
            dar_ref[...] = jnp.sum(accr[...], axis=0, keepdims=True)
            dai_ref[...] = jnp.sum(acci[...], axis=0, keepdims=True)

    f32_scr = lambda *s: pltpu.VMEM(s, F32)
    return pl.pallas_call(
        body, name="ssm_bwd", grid=(nq, nt),
        in_specs=[pl.BlockSpec((ts, 128), lambda q, t: (nt - 1 - t, q)),
                  pl.BlockSpec((ts, 128), lambda q, t: (nt - 1 - t, 4 + q)),
                  pl.BlockSpec((ts, cq), lambda q, t: (nt - 1 - t, q)),
                  pl.BlockSpec((ts, cq), lambda q, t: (nt - 1 - t, q)),
                  pl.BlockSpec((None, 4, 128, 256), lambda q, t: (layer, q, 0, 0)),
                  pl.BlockSpec((None, 4, 256, 128), lambda q, t: (layer, q, 0, 0)),
                  pl.BlockSpec((None, 1, cq), lambda q, t: (layer, 0, q)),
                  pl.BlockSpec((None, 1, cq), lambda q, t: (layer, 0, q)),
                  pl.BlockSpec((None, 1, 128), lambda q, t: (layer, 0, q))],
        out_specs=[pl.BlockSpec((ts, 128), lambda q, t: (nt - 1 - t, q)),
                   pl.BlockSpec((4, 256, 128), lambda q, t: (q, 0, 0)),
                   pl.BlockSpec((4, 128, 256), lambda q, t: (q, 0, 0)),
                   pl.BlockSpec((1, cq), lambda q, t: (0, q)),
                   pl.BlockSpec((1, cq), lambda q, t: (0, q)),
                   pl.BlockSpec((1, 128), lambda q, t: (0, q))],
        out_shape=[jax.ShapeDtypeStruct((L, D_SSM), F32),
                   jax.ShapeDtypeStruct((N_PAIRS, 256, 128), F32), jax.ShapeDtypeStruct((N_PAIRS, 128, 256), F32),
                   jax.ShapeDtypeStruct((1, N_STATE), F32), jax.ShapeDtypeStruct((1, N_STATE), F32),
                   jax.ShapeDtypeStruct((1, D_SSM), F32)],
        scratch_shapes=[f32_scr(ts, cq), f32_scr(ts, cq), f32_scr(SUBLANES, cq), f32_scr(SUBLANES, cq),
                        f32_scr(N_SCAN_TABLES, SUBLANES, cq), f32_scr(SUBLANES, cq), f32_scr(SUBLANES, cq),
                        f32_scr(ts, 128), f32_scr(ts, 128)],
        compiler_params=_cparams(2),
    )(dyraw, u, sre, sim, cpad_t, bpad_t, ar, ai, dskip)


def _pool_bwd(dyp, u, layer, w_pool, scale):
    L = u.shape[0]
    tm = min(TM, L)
    nt = L // tm
    halo_per_tile = tm // POOL_HALO

    def body(dyp_ref, u_ref, halo_ref, wp_ref, sc_ref, du_ref, dwp_ref, dsc_ref, carry):
        i = pl.program_id(0)
        tile = nt - 1 - i

        @pl.when(i == 0)
        def _():
            carry[...] = jnp.zeros_like(carry)
            dwp_ref[...] = jnp.zeros_like(dwp_ref)
            dsc_ref[...] = jnp.zeros_like(dsc_ref)

        up = u_ref[...]
        halo = jnp.where(tile > 0, halo_ref[...], jnp.zeros_like(halo_ref))
        diffs = _pool_diff(jnp.concatenate([halo, up], axis=0), tile * tm, tm)
        rows = tile * tm + lax.broadcasted_iota(jnp.int32, (tm, 1), 0)
        n_ext = tm + POOL_HALO
        for gi, w in enumerate(POOL_WINDOWS):
            cols = slice(gi * POOL_GROUP, (gi + 1) * POOL_GROUP)
            db = diffs[gi].astype(BF16)
            dyp = dyp_ref[:, cols]
            dsc_ref[:, cols] += jnp.sum(dyp * _dot(db, wp_ref[gi]), axis=0, keepdims=True)
            dp = (dyp * sc_ref[:, cols]).astype(BF16)
            ddiff = _dot_nt(dp, wp_ref[gi])
            dwp_ref[gi] += _dot_tn(db, dp)
            e = ddiff * (1.0 / jnp.minimum(rows + 1, w).astype(F32))
            s = jnp.concatenate([e, carry[:, cols]], axis=0)
            k = 1
            while k < w:
                s = s + pltpu.roll(s, n_ext - k, 0)
                k *= 2
            du_ref[:, cols] = s[:tm, :] - ddiff
            carry[:, cols] = e[:POOL_HALO, :]

    return pl.pallas_call(
        body, name="pool_bwd", grid=(nt,),
        in_specs=[pl.BlockSpec((tm, D_POOL), lambda i: (nt - 1 - i, 0)),
                  pl.BlockSpec((tm, D_POOL), lambda i: (nt - 1 - i, 0)),
                  pl.BlockSpec((POOL_HALO, D_POOL), lambda i: (jnp.maximum((nt - 1 - i) * halo_per_tile - 1, 0), 0)),
                  pl.BlockSpec((None, 4, POOL_GROUP, POOL_GROUP), lambda i: (layer, 0, 0, 0)),
                  pl.BlockSpec((None, 1, D_POOL), lambda i: (layer, 0, 0))],
        out_specs=[pl.BlockSpec((tm, D_POOL), lambda i: (nt - 1 - i, 0)),
                   pl.BlockSpec((4, POOL_GROUP, POOL_GROUP), lambda i: (0, 0, 0)),
                   pl.BlockSpec((1, D_POOL), lambda i: (0, 0))],
        out_shape=[jax.ShapeDtypeStruct((L, D_POOL), F32),
                   jax.ShapeDtypeStruct((4, POOL_GROUP, POOL_GROUP), F32),
                   jax.ShapeDtypeStruct((1, D_POOL), F32)],
        scratch_shapes=[pltpu.VMEM((POOL_HALO, D_POOL), F32)],
        compiler_params=_cparams(1),
    )(dyp, u, u, w_pool, scale)


def _mix_in_bwd(dup, dus, h, dhm, g1, wp, layer, gbuf):
    L = h.shape[0]
    tm = min(TM, L)
    n_steps = L // tm
    blk, idx = P_IN_BLK

    def body(dup_ref, dus_ref, h_ref, dhm_ref, g_ref, w_ref, g1_in, dh_ref, dg_ref, g1_ref, dwin):
        i = pl.program_id(0)

        @pl.when(i == 0)
        def _():
            dg_ref[...] = jnp.zeros_like(dg_ref)
            dwin[...] = jnp.zeros_like(dwin)

        du = jnp.concatenate([dup_ref[...], dus_ref[...]], axis=1).astype(BF16)
        dn1 = _dot_nt(du, w_ref[...].reshape(D_MODEL, D_MODEL))
        xhat, r = _rms_hat(h_ref[...])
        g = g_ref[...]
        n1 = (xhat * g).astype(BF16)
        dwin[...] += _dot_tn(n1, du).reshape(N_SHARD, blk, D_MODEL)
        dg_ref[...] += jnp.sum(dn1 * xhat, axis=0, keepdims=True)
        dh_ref[...] = dhm_ref[...] + _rms_bwd(dn1 * g, xhat, r)

        @pl.when(i == n_steps - 1)
        def _():
            pltpu.sync_copy(dwin, g1_ref.at[0, :, pl.ds(blk * idx, blk), :])

    row_spec = pl.BlockSpec((tm, D_MODEL), lambda i: (i, 0))
    half_spec = pl.BlockSpec((tm, D_POOL), lambda i: (i, 0))
    return pl.pallas_call(
        body, name="mix_in_bwd", grid=(n_steps,),
        in_specs=[half_spec, half_spec, row_spec, row_spec,
                  pl.BlockSpec((None, 1, D_MODEL), lambda i: (layer, 0, 0)),
                  pl.BlockSpec((N_SHARD, None, blk, D_MODEL), lambda i: (0, 0, idx, 0)),
                  pl.BlockSpec(memory_space=pl.ANY)],
        out_specs=[row_spec, pl.BlockSpec((1, D_MODEL), lambda i: (0, 0)), pl.BlockSpec(memory_space=pl.ANY)],
        out_shape=[jax.ShapeDtypeStruct((L, D_MODEL), F32), jax.ShapeDtypeStruct((1, D_MODEL), F32),
                   jax.ShapeDtypeStruct(gbuf.shape, F32)],
        scratch_shapes=[pltpu.VMEM((N_SHARD, blk, D_MODEL), F32)],
        input_output_aliases={6: 2},
        compiler_params=_cparams(1),
    )(dup, dus, h, dhm, g1, wp, gbuf)


def _disc_math(lr, li, ldt, br_t, bi_t):
    dt = jnp.exp(ldt)
    mag = jnp.exp(lr * dt)
    ang = li * dt
    ar = mag * jnp.cos(ang)
    ai = mag * jnp.sin(ang)
    den = lr * lr + li * li
    nr, ni = ar - 1.0, ai
    cr = (nr * lr + ni * li) / den
    ci = (ni * lr - nr * li) / den
    return ar, ai, cr * br_t - ci * bi_t, cr * bi_t + ci * br_t


def _disc_fwd(lr, li, ldt, br_t, bi_t):
    def body(lr_ref, li_ref, ldt_ref, br_ref, bi_ref, ar_ref, ai_ref, bbr_ref, bbi_ref):
        ar, ai, bbr, bbi = _disc_math(lr_ref[...], li_ref[...], ldt_ref[...], br_ref[...], bi_ref[...])
        ar_ref[...] = ar
        ai_ref[...] = ai
        bbr_ref[...] = bbr
        bbi_ref[...] = bbi

    shapes = [jax.ShapeDtypeStruct(a.shape, F32) for a in (lr, li, br_t, bi_t)]
    return pl.pallas_call(body, name="ssm_disc_fwd", out_shape=shapes,
                          compiler_params=pltpu.CompilerParams(vmem_limit_bytes=VMEM_LIMIT))(lr, li, ldt, br_t, bi_t)


def _disc_bwd(lr, li, ldt, br_t, bi_t, dar, dai, dbbr, dbbi):
    def body(lr_ref, li_ref, ldt_ref, br_ref, bi_ref, dar_ref, dai_ref, dbbr_ref, dbbi_ref,
             dlr_ref, dli_ref, dldt_ref, dbr_ref, dbi_ref):
        prim = (lr_ref[...], li_ref[...], ldt_ref[...], br_ref[...], bi_ref[...])
        _, pullback = jax.vjp(_disc_math, *prim)
        dlr, dli, dldt, dbr, dbi = pullback((dar_ref[...], dai_ref[...], dbbr_ref[...], dbbi_ref[...]))
        dlr_ref[...] = dlr
        dli_ref[...] = dli
        dldt_ref[...] = dldt
        dbr_ref[...] = dbr
        dbi_ref[...] = dbi

    shapes = [jax.ShapeDtypeStruct(a.shape, F32) for a in (lr, li, ldt, br_t, bi_t)]
    return pl.pallas_call(body, name="ssm_disc_bwd", out_shape=shapes,
                          compiler_params=pltpu.CompilerParams(vmem_limit_bytes=VMEM_LIMIT))(
        lr, li, ldt, br_t, bi_t, dar, dai, dbbr, dbbi)


def _pad_pairs(m_re, m_im):
    def blocks(m):
        v = m.transpose(0, 2, 1).reshape(N_PAIRS, 2, SSM_GROUP, SSM_STATE)
        return jnp.einsum("ab,jahp->jahbp", jnp.eye(2, dtype=m.dtype), v).reshape(N_PAIRS, 32, 128)
    both = jnp.concatenate([blocks(m_re), blocks(m_im)], axis=-1)
    place = jax.nn.one_hot(jnp.arange(N_PAIRS) % 4, 4, dtype=both.dtype)
    return jnp.einsum("jk,jrc->jkrc", place, both).reshape(N_PAIRS, 128, 256)


def _unpad_pairs(x):
    place = jax.nn.one_hot(jnp.arange(N_PAIRS) % 4, 4, dtype=x.dtype)
    both = jnp.einsum("jk,jkrc->jrc", place, x.reshape(N_PAIRS, 4, 32, 256))

    def unblock(v):
        v = v.reshape(N_PAIRS, 2, SSM_GROUP, 2, SSM_STATE)
        d = jnp.einsum("ab,jahbp->jahp", jnp.eye(2, dtype=x.dtype), v)
        return d.reshape(N_SSM_GROUPS, SSM_GROUP, SSM_STATE).transpose(0, 2, 1)
    return unblock(both[..., :128]), unblock(both[..., 128:])


def _adamw_math(w, g, m, v):
    m = ADAM_B1 * m + (1.0 - ADAM_B1) * g
    v = ADAM_B2 * v + (1.0 - ADAM_B2) * (g * g)
    m_hat = m / (1.0 - ADAM_B1 ** ADAM_STEP)
    v_hat = v / (1.0 - ADAM_B2 ** ADAM_STEP)
    delta = -ADAM_LR * (m_hat / (jnp.sqrt(v_hat) + ADAM_EPS) + ADAM_WD * w)
    return delta, m, v


def _adamw(name, layer, w, m, v, gbuf, g_block, g_row0, row_tile, outs=None, after=(), glu=False):
    nl, r, c = w.shape
    n_tiles = r // row_tile
    g_rows, g_cols = g_block
    g_tile = g_rows // n_tiles
    g_off = g_row0 // g_tile
    if outs is None:
        outs = [lax.empty(w.shape, F32) for _ in range(4)]

    def body(w_ref, m_ref, v_ref, g_ref, *rest):
        go_ref, d_ref, mo_ref, vo_ref = rest[-4:]
        g = g_ref[...]
        if glu:
            g = jnp.concatenate([g[:, :D_SSM], g[:, D_SSM:]], axis=0)
        delta, mn, vn = _adamw_math(w_ref[...], g, m_ref[...], v_ref[...])
        go_ref[...] = g
        d_ref[...] = delta
        mo_ref[...] = mn
        vo_ref[...] = vn

    w_spec = pl.BlockSpec((None, row_tile, c), lambda j: (layer, j, 0))
    shape = jax.ShapeDtypeStruct(w.shape, F32)
    return pl.pallas_call(
        body, name=name, grid=(n_tiles,),
        in_specs=[w_spec, w_spec, w_spec, pl.BlockSpec((None, g_tile, g_cols), lambda j: (0, g_off + j, 0))]
        + [_ANY] * (4 + len(after)),
        out_specs=[w_spec] * 4,
        out_shape=[shape] * 4,
        input_output_aliases={4: 0, 5: 1, 6: 2, 7: 3},
        compiler_params=_cparams(1),
    )(w, m, v, gbuf, *outs, *after)


def _pack_weights(ids, layer, w_in, w_glu, w_out, w_down, w_gate_t, w_up_t):
    gb, gi = P_GLU_BLK
    ib, ii = P_IN_BLK
    ob, oi = P_OUT_BLK

    def body(ids_ref, in_ref, glu_ref, out_ref, dn_ref, gate_ref, up_ref, p_ref):
        p_ref[0:FF_SHARD, :] = dn_ref[...].astype(BF16)
        p_ref[FF_SHARD:2 * FF_SHARD, :] = gate_ref[...].astype(BF16)
        p_ref[2 * FF_SHARD:P_FF_ROWS, :] = up_ref[...].astype(BF16)
        g = glu_ref[...]
        p_ref[gb * gi:gb * (gi + 1), :] = jnp.concatenate([g[:gb, :], g[gb:, :]], axis=1).astype(BF16)
        p_ref[gb * (gi + 1):ib * ii, :] = jnp.zeros((P_GLU_PAD - gb, D_MODEL), BF16)
        p_ref[ib * ii:ib * (ii + 1), :] = in_ref[...].astype(BF16)
        p_ref[ob * oi:ob * (oi + 1), :] = out_ref[...].astype(BF16)

    def spec(a):
        return pl.BlockSpec((None,) + a.shape[1:], lambda i, ids_ref: (layer, 0, 0))

    ins = (w_in, w_glu, w_out, w_down, w_gate_t, w_up_t)
    grid_spec = pltpu.PrefetchScalarGridSpec(
        num_scalar_prefetch=1, grid=(1,),
        in_specs=[spec(a) for a in ins],
        out_specs=pl.BlockSpec((None, None, P_ROWS, D_MODEL), lambda i, ids_ref: (ids_ref[1], 0, 0, 0)))
    return pl.pallas_call(
        body, name="pack_weights", grid_spec=grid_spec,
        out_shape=jax.ShapeDtypeStruct((N_SHARD, 1, P_ROWS, D_MODEL), BF16),
        compiler_params=_cparams(1),
    )(ids, *ins)


MESH = pl.DeviceIdType.MESH
_ANY = pl.BlockSpec(memory_space=pl.ANY)
P_HALF = P_ROWS // 2
RS_ROW_TILE = 352


def _mesh_pos():
    return lax.axis_index("x"), lax.axis_index("y"), lax.axis_index("c")


def _other_chips(x, y):
    return [(1 - x, y), (x, 1 - y), (1 - x, 1 - y)]


def _remote(src, dst, send_sems, recv_sems, n, to):
    return pltpu.make_async_remote_copy(src_ref=src, dst_ref=dst, send_sem=send_sems.at[n],
                                        recv_sem=recv_sems.at[n], device_id=to, device_id_type=MESH)


_HBM = pl.BlockSpec(memory_space=pltpu.HBM)
_SEM = pl.BlockSpec(memory_space=pltpu.SEMAPHORE)
_EFFECT = pltpu.CompilerParams(has_side_effects=pltpu.SideEffectType.DATAFLOW_SIDE_EFFECTING)
_TOKEN = jax.ShapeDtypeStruct((8, 128), F32)


def _in_hbm(a):
    return pltpu.with_memory_space_constraint(a, pltpu.HBM)


def _ag_start(name, wp, after):
    def body(w_ref, after_ref, send_sems, recv_sems, w_thru, token):
        x, y, c = _mesh_pos()
        mine = w_ref.at[2 * x + y, :, pl.ds(c * P_HALF, P_HALF), :]
        for j, (px, py) in enumerate(_other_chips(x, y)):
            _remote(mine, mine, send_sems, recv_sems, j, (px, py, c)).start()
        token[...] = jnp.zeros_like(token)

    return pl.pallas_call(
        body, name=name,
        out_shape=(pltpu.SemaphoreType.DMA((3,)), pltpu.SemaphoreType.DMA((3,)), pltpu.HBM(wp.shape, wp.dtype), _TOKEN),
        in_specs=(_HBM, _ANY), out_specs=(_SEM, _SEM, _HBM, pl.BlockSpec(memory_space=pltpu.VMEM)),
        input_output_aliases={0: 2}, compiler_params=_EFFECT,
    )(_in_hbm(wp), after)


def _ag_wait(name, send_sems, recv_sems, wp, after):
    def body(w_ref, send_sems, recv_sems, *rest):
        x, y, c = _mesh_pos()
        mine = w_ref.at[2 * x + y, :, pl.ds(c * P_HALF, P_HALF), :]
        for j, (px, py) in enumerate(_other_chips(x, y)):
            landed = w_ref.at[2 * px + py, :, pl.ds(c * P_HALF, P_HALF), :]
            cp = _remote(mine, landed, send_sems, recv_sems, j, (px, py, c))
            cp.wait_send()
            cp.wait_recv()

    return pl.pallas_call(
        body, name=name, out_shape=pltpu.HBM(wp.shape, wp.dtype),
        in_specs=(_HBM, _SEM, _SEM) + (_ANY,) * len(after), out_specs=_HBM,
        input_output_aliases={0: 0}, compiler_params=_EFFECT,
    )(wp, send_sems, recv_sems, *after)


def _ag_forward(wp):
    def body(w_in, o, send_sems, recv_sems):
        x, y, c = _mesh_pos()
        sib = (x, y, 1 - c)
        chips = _other_chips(x, y)
        sends = []
        for j, (px, py) in enumerate(chips):
            landed = o.at[2 * px + py, :, pl.ds(c * P_HALF, P_HALF), :]
            cp = _remote(landed, landed, send_sems, recv_sems, j, sib)
            cp.start()
            sends.append(cp)
        for j, (px, py) in enumerate(chips):
            passed = o.at[2 * px + py, :, pl.ds((1 - c) * P_HALF, P_HALF), :]
            _remote(passed, passed, send_sems, recv_sems, j, sib).wait_recv()
        for cp in sends:
            cp.wait_send()

    return pl.pallas_call(
        body, name="ag_forward",
        in_specs=[_ANY], out_specs=_ANY,
        out_shape=jax.ShapeDtypeStruct(wp.shape, wp.dtype),
        scratch_shapes=[pltpu.SemaphoreType.DMA((3,)), pltpu.SemaphoreType.DMA((3,))],
        input_output_aliases={0: 0},
    )(wp)


def _rs_chips_start(name, t):
    nl = t.shape[0]

    def body(t_ref, land_ref, send_sems, recv_sems, t_thru, land_thru, token):
        x, y, c = _mesh_pos()
        for j, (px, py) in enumerate(_other_chips(x, y)):
            _remote(t_ref.at[:, 2 * px + py], land_ref.at[j], send_sems, recv_sems, j, (px, py, c)).start()
        token[...] = jnp.zeros_like(token)

    land = lax.empty((3, nl, P_HALF, D_MODEL), BF16)
    return pl.pallas_call(
        body, name=name,
        out_shape=(pltpu.SemaphoreType.DMA((3,)), pltpu.SemaphoreType.DMA((3,)), pltpu.HBM(t.shape, t.dtype),
                   pltpu.HBM(land.shape, land.dtype), _TOKEN),
        in_specs=(_HBM, _HBM), out_specs=(_SEM, _SEM, _HBM, _HBM, pl.BlockSpec(memory_space=pltpu.VMEM)),
        input_output_aliases={0: 2, 1: 3}, compiler_params=_EFFECT,
    )(_in_hbm(t), _in_hbm(land))


def _rs_chips_wait(name, send_sems, recv_sems, t, land, after):
    def body(t_ref, land_ref, send_sems, recv_sems, *rest):
        x, y, c = _mesh_pos()
        for j, (px, py) in enumerate(_other_chips(x, y)):
            cp = _remote(t_ref.at[:, 2 * px + py], land_ref.at[j], send_sems, recv_sems, j, (px, py, c))
            cp.wait_send()
            cp.wait_recv()

    return pl.pallas_call(
        body, name=name, out_shape=(pltpu.HBM(t.shape, t.dtype), pltpu.HBM(land.shape, land.dtype)),
        in_specs=(_HBM, _HBM, _SEM, _SEM) + (_ANY,) * len(after), out_specs=(_HBM, _HBM),
        input_output_aliases={0: 0, 1: 1}, compiler_params=_EFFECT,
    )(t, land, send_sems, recv_sems, *after)[1]


def _rs_sibling_start(name, g):
    nl = g.shape[0]

    def body(g_ref, land_ref, send_sems, recv_sems, g_thru, land_thru, token):
        x, y, c = _mesh_pos()
        _remote(g_ref.at[:, :, pl.ds((1 - c) * P_HALF, P_HALF), :], land_ref, send_sems, recv_sems, 0,
                (x, y, 1 - c)).start()
        token[...] = jnp.zeros_like(token)

    land = lax.empty((nl, N_SHARD, P_HALF, D_MODEL), F32)
    return pl.pallas_call(
        body, name=name,
        out_shape=(pltpu.SemaphoreType.DMA((1,)), pltpu.SemaphoreType.DMA((1,)), pltpu.HBM(g.shape, g.dtype),
                   pltpu.HBM(land.shape, land.dtype), _TOKEN),
        in_specs=(_HBM, _HBM), out_specs=(_SEM, _SEM, _HBM, _HBM, pl.BlockSpec(memory_space=pltpu.VMEM)),
        input_output_aliases={0: 2, 1: 3}, compiler_params=_EFFECT,
    )(_in_hbm(g), _in_hbm(land))


def _rs_sibling_wait(name, send_sems, recv_sems, g, land, after):
    def body(g_ref, land_ref, send_sems, recv_sems, *rest):
        x, y, c = _mesh_pos()
        cp = _remote(g_ref.at[:, :, pl.ds((1 - c) * P_HALF, P_HALF), :], land_ref, send_sems, recv_sems, 0,
                     (x, y, 1 - c))
        cp.wait_send()
        cp.wait_recv()

    return pl.pallas_call(
        body, name=name, out_shape=(pltpu.HBM(g.shape, g.dtype), pltpu.HBM(land.shape, land.dtype)),
        in_specs=(_HBM, _HBM, _SEM, _SEM) + (_ANY,) * len(after), out_specs=(_HBM, _HBM),
        input_output_aliases={0: 0, 1: 1}, compiler_params=_EFFECT,
    )(g, land, send_sems, recv_sems, *after)


def _rs_add(name, ids, g, buf, row_tile):
    nl, _, hr, cols = buf.shape
    n_rt = hr // row_tile

    def body(ids_ref, g_ref, b_ref, own_ref, tb_ref):
        t = g_ref[...] + b_ref[...]
        tb_ref[...] = t.astype(BF16)

        @pl.when(pl.program_id(2) == ids_ref[1])
        def _():
            own_ref[...] = t

    blk = (None, None, row_tile, cols)
    grid_spec = pltpu.PrefetchScalarGridSpec(
        num_scalar_prefetch=1, grid=(nl, n_rt, N_SHARD),
        in_specs=[pl.BlockSpec(blk, lambda l, j, s, ids_ref: (l, s, ids_ref[0] * n_rt + j, 0)),
                  pl.BlockSpec(blk, lambda l, j, s, ids_ref: (l, s, j, 0))],
        out_specs=[pl.BlockSpec((None, row_tile, cols), lambda l, j, s, ids_ref: (l, j, 0)),
                   pl.BlockSpec(blk, lambda l, j, s, ids_ref: (l, s, j, 0))])
    return pl.pallas_call(
        body, name=name, grid_spec=grid_spec,
        out_shape=[jax.ShapeDtypeStruct((nl, hr, cols), F32), jax.ShapeDtypeStruct(buf.shape, BF16)],
        compiler_params=_cparams(3),
    )(ids, g, buf)


def _rs_sum(ids, layer, own, bufb, reduced, row_tile):
    _, hr, cols = own.shape
    n_rt = hr // row_tile

    def body(ids_ref, own_ref, b_ref, reduced_in, f_ref):
        f_ref[...] = ((own_ref[...] + b_ref[0].astype(F32)) + b_ref[1].astype(F32)) + b_ref[2].astype(F32)

    grid_spec = pltpu.PrefetchScalarGridSpec(
        num_scalar_prefetch=1, grid=(n_rt,),
        in_specs=[pl.BlockSpec((None, row_tile, cols), lambda j, ids_ref: (0, j, 0)),
                  pl.BlockSpec((3, None, row_tile, cols), lambda j, ids_ref: (0, 0, j, 0)),
                  pl.BlockSpec(memory_space=pl.ANY)],
        out_specs=pl.BlockSpec((None, row_tile, cols), lambda j, ids_ref: (layer, ids_ref[0] * n_rt + j, 0)))
    return pl.pallas_call(
        body, name="rs_sum", grid_spec=grid_spec,
        out_shape=jax.ShapeDtypeStruct(reduced.shape, F32),
        input_output_aliases={3: 0},
        compiler_params=_cparams(1),
    )(ids, own, bufb, reduced)


def _rs_exchange_start(name, f):
    def body(f_ref, send_sems, recv_sems, f_thru):
        x, y, c = _mesh_pos()
        mine = f_ref.at[:, pl.ds(c * P_HALF, P_HALF), :]
        _remote(mine, mine, send_sems, recv_sems, 0, (x, y, 1 - c)).start()

    return pl.pallas_call(
        body, name=name,
        out_shape=(pltpu.SemaphoreType.DMA((1,)), pltpu.SemaphoreType.DMA((1,)), pltpu.HBM(f.shape, f.dtype)),
        in_specs=(_HBM,), out_specs=(_SEM, _SEM, _HBM),
        input_output_aliases={0: 2}, compiler_params=_EFFECT,
    )(_in_hbm(f))


def _rs_exchange_wait(name, send_sems, recv_sems, f, after):
    def body(f_ref, send_sems, recv_sems, *rest):
        x, y, c = _mesh_pos()
        mine = f_ref.at[:, pl.ds(c * P_HALF, P_HALF), :]
        theirs = f_ref.at[:, pl.ds((1 - c) * P_HALF, P_HALF), :]
        cp = _remote(mine, theirs, send_sems, recv_sems, 0, (x, y, 1 - c))
        cp.wait_send()
        cp.wait_recv()

    return pl.pallas_call(
        body, name=name, out_shape=pltpu.HBM(f.shape, f.dtype),
        in_specs=(_HBM, _SEM, _SEM) + (_ANY,) * len(after), out_specs=_HBM,
        input_output_aliases={0: 0}, compiler_params=_EFFECT,
    )(f, send_sems, recv_sems, *after)


def _small_all_reduce(s, after=()):
    n_rows = s.shape[0]
    hr = n_rows // 2
    qr = hr // N_SHARD

    def body(s_ref, *rest):
        o_ref, sibbuf, tbuf, qbuf, fbuf, send_sems, recv_sems = rest[len(after):]
        x, y, c = _mesh_pos()
        k = 2 * x + y
        sib = (x, y, 1 - c)
        chips = _other_chips(x, y)
        mine = pl.ds(pl.multiple_of(c * hr, SUBLANES), hr)
        theirs = pl.ds(pl.multiple_of((1 - c) * hr, SUBLANES), hr)

        def quarter(shard):
            return pl.ds(pl.multiple_of(shard * qr, SUBLANES), qr)

        first = _remote(s_ref.at[theirs], sibbuf, send_sems, recv_sems, 0, sib)
        first.start()
        first.wait()
        tbuf[...] = s_ref[mine, :] + sibbuf[...]
        cps = []
        for j, (px, py) in enumerate(chips):
            cp = _remote(tbuf.at[quarter(2 * px + py)], qbuf.at[j], send_sems, recv_sems, 1 + j, (px, py, c))
            cp.start()
            cps.append(cp)
        for cp in cps:
            cp.wait()
        fbuf[quarter(k), :] = (tbuf[quarter(k), :] + qbuf[1]) + (qbuf[0] + qbuf[2])
        cps = []
        for j, (px, py) in enumerate(chips):
            cp = _remote(fbuf.at[quarter(k)], fbuf.at[quarter(k)], send_sems, recv_sems, 4 + j, (px, py, c))
            cp.start()
            cps.append(cp)
        for j, (px, py) in enumerate(chips):
            got = fbuf.at[quarter(2 * px + py)]
            _remote(got, got, send_sems, recv_sems, 4 + j, (px, py, c)).wait_recv()
        for cp in cps:
            cp.wait_send()
        o_ref[mine, :] = fbuf[...]
        last = _remote(fbuf, o_ref.at[mine], send_sems, recv_sems, 7, sib)
        last.start()
        last.wait()

    vmem = pl.BlockSpec(memory_space=pltpu.VMEM)
    return pl.pallas_call(
        body, name="small_all_reduce",
        in_specs=[vmem] + [_ANY] * len(after), out_specs=vmem,
        out_shape=jax.ShapeDtypeStruct(s.shape, F32),
        scratch_shapes=[pltpu.VMEM((hr, D_MODEL), F32), pltpu.VMEM((hr, D_MODEL), F32),
                        pltpu.VMEM((3, qr, D_MODEL), F32), pltpu.VMEM((hr, D_MODEL), F32),
                        pltpu.SemaphoreType.DMA((8,)), pltpu.SemaphoreType.DMA((8,))],
        compiler_params=pltpu.CompilerParams(vmem_limit_bytes=VMEM_LIMIT),
    )(s, *after)


_SMALL = ("norm_mix", "w_pool", "pool_scale", "lam_re", "lam_im", "log_dt", "b_re", "b_im", "c_re", "c_im",
          "d_skip", "b_glu", "norm_ffn", "norm_final")
_WEIGHTS = ("norm_mix", "w_in", "w_pool", "pool_scale", "lam_re", "lam_im", "log_dt", "b_re", "b_im", "c_re",
            "c_im", "d_skip", "w_glu", "b_glu", "w_out", "norm_ffn", "w_gate", "w_up", "w_down", "norm_final")


def _local_step(x, target, p, get_weights, ffn_bwd_done, put_grads):
    nl = p["norm_mix"].shape[0]

    def tied(a, token):
        return a if token is None else a + token
    n_rows = nl * N_SSM_GROUPS
    lr = p["lam_re"].reshape(n_rows, 1, SSM_STATE)
    li = p["lam_im"].reshape(n_rows, 1, SSM_STATE)
    ldt = p["log_dt"].reshape(n_rows, 1, 1)
    br_t = p["b_re"].reshape(n_rows, SSM_STATE, SSM_GROUP).transpose(0, 2, 1)
    bi_t = p["b_im"].reshape(n_rows, SSM_STATE, SSM_GROUP).transpose(0, 2, 1)
    ar, ai, bbr_t, bbi_t = _disc_fwd(lr, li, ldt, br_t, bi_t)
    ar = ar.reshape(nl, 1, N_STATE)
    ai = ai.reshape(nl, 1, N_STATE)
    bbr = bbr_t.transpose(0, 2, 1).reshape(nl, N_SSM_GROUPS, SSM_STATE, SSM_GROUP)
    bbi = bbi_t.transpose(0, 2, 1).reshape(nl, N_SSM_GROUPS, SSM_STATE, SSM_GROUP)
    w_pool = p["w_pool"].astype(BF16)
    p = dict(p)
    for n in ("norm_mix", "pool_scale", "b_glu", "norm_ffn"):
        p[n] = p[n].reshape(nl, 1, -1)
    swap = lambda a: jnp.swapaxes(a, -1, -2)
    bpad = jax.vmap(_pad_pairs)(bbr, bbi).astype(BF16)
    cpad_t = jax.vmap(_pad_pairs)(swap(p["c_re"]), -swap(p["c_im"])).astype(BF16)
    bpad_t, cpad = swap(bpad), swap(cpad_t)
    dskip = p["d_skip"].reshape(nl, 1, D_SSM)

    layers = []
    h = x
    for l in range(nl):
        wp = get_weights(l, [h] if l else [h, bpad, cpad, bpad_t, cpad_t, ar, ai])
        u, ypool = _mix_in_fwd(h, p["norm_mix"], wp, l, w_pool, p["pool_scale"])
        sre, sim, yraw = _ssm_fwd(u, l, bpad, cpad, ar, ai, dskip)
        hm = _mix_out_fwd(yraw, ypool, h, wp, l, p["b_glu"])
        h_next, n2, act_s, fgate_s, fup_s = _ffn_fwd(hm, p["norm_ffn"], wp, l)
        layers.append(dict(h=h, u=u, ypool=ypool, sre=sre, sim=sim, yraw=yraw, hm=hm, n2=n2, act_s=act_s, wp=wp,
                           fgate_s=fgate_s, fup_s=fup_s))
        h = h_next

    dh, loss, d_norm_final = _final_fwd_bwd(h, p["norm_final"].reshape(1, D_MODEL), target)

    raw = {n: [None] * nl for n in ("dg1", "dwp", "dsc", "dcp", "dbp", "ddsk", "db_glu", "dg2", "dar", "dai")}
    token = None
    for l in reversed(range(nl)):
        s = layers[l]
        wp = s["wp"]
        g1 = lax.empty((1, N_SHARD, P_ROWS, D_MODEL), F32)
        dhm, dg2, dgate_s, dup_s, dhb = _ffn_bwd_act(dh, s["hm"], tied(p["norm_ffn"], token), s["fgate_s"],
                                                      s["fup_s"], wp, l)
        g1 = _ffn_bwd_w(s["n2"], dgate_s, dup_s, s["act_s"], dhb, g1)
        token = ffn_bwd_done(l, [g1])
        dyraw, dyp, db_glu, g1 = _mix_out_bwd(dhm, s["yraw"], s["ypool"], wp, l, tied(p["b_glu"], token), g1)
        dus, dcp, dbp, dar, dai, ddsk = _ssm_bwd(dyraw, s["u"], s["sre"], s["sim"], l, cpad_t, bpad_t, ar, ai, dskip)
        dup, dwp, dsc = _pool_bwd(dyp, s["u"], l, w_pool, p["pool_scale"])
        dh, dg1, g1 = _mix_in_bwd(dup, dus, s["h"], dhm, p["norm_mix"], wp, l, g1)
        token = put_grads(l, g1)
        for n, a in (("dg1", dg1), ("dwp", dwp), ("dsc", dsc), ("dcp", dcp), ("dbp", dbp), ("ddsk", ddsk),
                     ("db_glu", db_glu), ("dg2", dg2), ("dar", dar), ("dai", dai)):
            raw[n][l] = a

    st = {n: jnp.stack(v) for n, v in raw.items()}
    dc_re, dc_im = jax.vmap(_unpad_pairs)(swap(st["dcp"]))
    dbbr, dbbi = jax.vmap(_unpad_pairs)(st["dbp"])
    rows = lambda a: a.reshape((n_rows,) + a.shape[2:])
    dlr, dli, dldt, dbr_t, dbi_t = _disc_bwd(lr, li, ldt, br_t, bi_t, st["dar"].reshape(n_rows, 1, SSM_STATE),
                                              st["dai"].reshape(n_rows, 1, SSM_STATE), rows(swap(dbbr)),
                                              rows(swap(dbbi)))
    small = {"norm_mix": st["dg1"][:, 0], "w_pool": st["dwp"], "pool_scale": st["dsc"][:, 0], "c_re": swap(dc_re),
             "c_im": -swap(dc_im), "d_skip": st["ddsk"].reshape(nl, N_SSM_GROUPS, SSM_GROUP),
             "b_glu": st["db_glu"][:, 0], "norm_ffn": st["dg2"][:, 0]}
    small["lam_re"] = dlr.reshape(nl, N_SSM_GROUPS, SSM_STATE)
    small["lam_im"] = dli.reshape(nl, N_SSM_GROUPS, SSM_STATE)
    small["log_dt"] = dldt.reshape(nl, N_SSM_GROUPS)
    small["b_re"] = dbr_t.reshape(nl, N_SSM_GROUPS, SSM_GROUP, SSM_STATE)
    small["b_im"] = dbi_t.reshape(nl, N_SSM_GROUPS, SSM_GROUP, SSM_STATE)
    small["d_skip"] = small["d_skip"].transpose(_SMALL_VIEW["d_skip"])
    small["norm_final"] = d_norm_final
    return loss, dh, small


_SMALL_VIEW = {"b_re": (0, 1, 3, 2), "b_im": (0, 1, 3, 2), "d_skip": (0, 2, 1)}
_SMALL_GROUPS = (("b_re", "b_im"), ("c_re", "c_im"), ("lam_re", "lam_im"), ("norm_mix", "norm_ffn"),
                 ("pool_scale", "b_glu"), ("w_pool",), ("log_dt",), ("d_skip",), ("norm_final",))


def _view(n, a):
    a = a.transpose(_SMALL_VIEW[n]) if n in _SMALL_VIEW else a
    return a[None] if a.ndim == 1 else a


def _unview(n, a, shape):
    a = a.reshape(shape) if len(shape) == 1 else a
    return a.transpose(_SMALL_VIEW[n]) if n in _SMALL_VIEW else a


def _flatten_small(views):
    flat = jnp.concatenate([views[n].reshape(-1) for n in _SMALL])
    n_rows = -(-flat.shape[0] // (64 * D_MODEL)) * 64
    return jnp.pad(flat, (0, n_rows * D_MODEL - flat.shape[0])).reshape(n_rows, D_MODEL)


def _split_small(flat, like):
    flat = flat.reshape(-1)
    out, at = {}, 0
    for n in _SMALL:
        size = like[n].size
        out[n] = flat[at:at + size].reshape(like[n].shape)
        at += size
    return out


def _adamw_small(name, ws, ms, vs, gs):
    k = len(ws)

    def body(*refs):
        ins, outs = refs[:4 * k], refs[4 * k:]
        for i in range(k):
            w, m, v, g = (ins[j * k + i][...] for j in range(4))
            delta, mn, vn = _adamw_math(w, g, m, v)
            outs[i][...] = delta
            outs[k + i][...] = mn
            outs[2 * k + i][...] = vn

    shapes = [jax.ShapeDtypeStruct(w.shape, F32) for w in ws] * 3
    outs = pl.pallas_call(body, name=name, out_shape=shapes,
                          compiler_params=pltpu.CompilerParams(vmem_limit_bytes=VMEM_LIMIT))(*ws, *ms, *vs, *gs)
    return outs[:k], outs[k:2 * k], outs[2 * k:]


def kernel(x, norm_mix, w_in, w_pool, pool_scale, lam_re, lam_im, log_dt, b_re, b_im, c_re, c_im, d_skip, w_glu, b_glu, w_out, norm_ffn, w_gate, w_up, w_down, norm_final, loss_target, m_norm_mix, m_w_in, m_w_pool, m_pool_scale, m_lam_re, m_lam_im, m_log_dt, m_b_re, m_b_im, m_c_re, m_c_im, m_d_skip, m_w_glu, m_b_glu, m_w_out, m_norm_ffn, m_w_gate, m_w_up, m_w_down, m_norm_final, v_norm_mix, v_w_in, v_w_pool, v_pool_scale, v_lam_re, v_lam_im, v_log_dt, v_b_re, v_b_im, v_c_re, v_c_im, v_d_skip, v_w_glu, v_b_glu, v_w_out, v_norm_ffn, v_w_gate, v_w_up, v_w_down, v_norm_final):
    given = dict(locals())
    w = {n: given[n] for n in _WEIGHTS}
    m = {n: given["m_" + n] for n in _WEIGHTS}
    v = {n: given["v_" + n] for n in _WEIGHTS}
    ids = jnp.stack([lax.axis_index("c"), 2 * lax.axis_index("x") + lax.axis_index("y")]).astype(jnp.int32)

    t_names = ("w_gate", "w_up")
    tr = lambda a: a.transpose(0, 2, 1)
    for d in (w, m, v):
        d.update({n: tr(d[n]) for n in t_names})

    nl = norm_mix.shape[0]
    packed = [_pack_weights(ids, l, w["w_in"], w["w_glu"], w["w_out"], w["w_down"], w["w_gate"], w["w_up"])
              for l in range(nl)]
    started, last = {}, ids
    for l in range(nl):
        started[l] = _ag_start(f"ag_start_{l}", packed[l], last)
        last = started[l][3]
    views = [{n: _view(n, d[n]) for n in _SMALL} for d in (w, m, v)]

    def get_weights(l, after):
        send_sems, recv_sems, buf, _ = started[l]
        after = after + ([last] if l == 0 else [])
        return _ag_forward(_ag_wait(f"ag_wait_{l}", send_sems, recv_sems, buf, after))

    to_sibling, to_chips, reduced = {}, {}, {}

    def put_grads(l, g):
        to_sibling[l] = _rs_sibling_start(f"rs_sibling_start_{l}", g)
        token = to_sibling[l][4]
        if l + 1 in to_chips:
            finish(l + 1, [token])
        return token[:1, :1]

    def ffn_bwd_done(l, after):
        return send_to_chips(l + 1, after)[:1, :1] if l + 1 in to_sibling else None

    def send_to_chips(l, after):
        send_sems, recv_sems, g, land, _ = to_sibling.pop(l)
        g, land = _rs_sibling_wait(f"rs_sibling_wait_{l}", send_sems, recv_sems, g, land, after)
        own, t = _rs_add("rs_add", ids, g, land, RS_ROW_TILE)
        send_sems, recv_sems, t, land, token = _rs_chips_start(f"rs_chips_start_{l}", t)
        to_chips[l] = (send_sems, recv_sems, t, land, own)
        return token

    def finish(l, after):
        send_sems, recv_sems, t, land, own = to_chips.pop(l)
        land = _rs_chips_wait(f"rs_chips_wait_{l}", send_sems, recv_sems, t, land, after)
        shard = lax.empty((1, P_ROWS, D_MODEL), F32)
        reduced[l] = _rs_exchange_start(f"rs_exchange_start_{l}", _rs_sum(ids, 0, own, land, shard, RS_ROW_TILE))

    loss, grad_x, small = _local_step(x[0], loss_target[0], {n: w[n] for n in _SMALL}, get_weights, ffn_bwd_done,
                                      put_grads)
    loss = lax.psum(loss[0, 0], ("x", "y", "c"))
    small_flat = _flatten_small(small)
    token = send_to_chips(0, [small_flat])

    big = (("w_in", P_IN_BLK, 256, False), ("w_out", P_OUT_BLK, 256, False), ("w_down", P_WD_BLK, 352, False),
           ("w_gate", P_WG_BLK, 352, False), ("w_up", P_WU_BLK, 352, False), ("w_glu", P_GLU_BLK, 128, True))
    res = {n: None for n, *_ in big}

    def adamw_layer(l, after):
        send_sems, recv_sems, shard = reduced[l]
        shard = _rs_exchange_wait(f"rs_exchange_wait_{l}", send_sems, recv_sems, shard, after)
        for n, (blk, idx), row_tile, glu in big:
            res[n] = _adamw("adamw_" + n, l, w[n], m[n], v[n], shard, (blk, D_MODEL), blk * idx, row_tile, res[n], (), glu)

    for l in reversed(range(1, nl)):
        adamw_layer(l, [token])
    updated = [r[0] for r in res.values() if r is not None]
    small_sum = _small_all_reduce(small_flat, [token] + updated)
    finish(0, [small_sum] + updated)
    adamw_layer(0, [])
    for n in t_names:
        res[n] = tuple(tr(a) for a in res[n])
    g_views = _split_small(small_sum, views[0])
    for group in _SMALL_GROUPS:
        deltas, new_ms, new_vs = _adamw_small("adamw_" + group[0], *[[d[n] for n in group] for d in views],
                                              [g_views[n] for n in group])
        for i, n in enumerate(group):
            res[n] = tuple(_unview(n, a, w[n].shape) for a in (g_views[n], deltas[i], new_ms[i], new_vs[i]))

    return (loss, grad_x[None], *[res[n][0] for n in _WEIGHTS], *[res[n][1] for n in _WEIGHTS],
            *[res[n][2] for n in _WEIGHTS], *[res[n][3] for n in _WEIGHTS])
```

```python
import functools
import math

import jax
import jax.numpy as jnp
from jax import lax
from jax.experimental import pallas as pl
from jax.experimental.pallas import tpu as pltpu

F32 = jnp.float32
BF16 = jnp.bfloat16

D_MODEL = 1024
D_POOL = 512
D_SSM = 512
POOL_WINDOWS = (2, 4, 8, 16)
POOL_GROUP = 128
POOL_HALO = 16
N_SSM_GROUPS = 32
SSM_GROUP = 16
SSM_STATE = 64
N_STATE = N_SSM_GROUPS * SSM_STATE
N_PAIRS = N_SSM_GROUPS // 2
D_FF = 2816
N_SHARD = 4
FF_SHARD = D_FF // N_SHARD
RMS_EPS = 1e-6

ADAM_LR = 0.001
ADAM_B1 = 0.9
ADAM_B2 = 0.999
ADAM_EPS = 1e-08
ADAM_WD = 0.01
ADAM_STEP = 10

P_ROWS = 2816
P_WD_BLK = (704, 0)
P_WG_BLK = (704, 1)
P_WU_BLK = (704, 2)
P_FF_ROWS = 2112
P_GLU_BLK = (64, 33)
P_GLU_PAD = 192
P_IN_BLK = (256, 9)
P_OUT_BLK = (256, 10)

SUBLANES = 8
VMEM_LIMIT = 56 * 1024 * 1024

TM = 1024
TM_FFN = 512
TM_FFN_LONG = 1024
FFN_SPLIT = 2
TS = 1024
SCAN_LANES = 512


def _cparams(n_axes):
    return pltpu.CompilerParams(dimension_semantics=("arbitrary",) * n_axes, vmem_limit_bytes=VMEM_LIMIT)


def _dot(a, b):
    return jnp.dot(a, b, preferred_element_type=F32)


def _dot_nt(a, b):
    return lax.dot_general(a, b, (((1,), (1,)), ((), ())), preferred_element_type=F32)


def _dot_tn(a, b):
    return lax.dot_general(a, b, (((0,), (0,)), ((), ())), preferred_element_type=F32)


def _rms_hat(x):
    r = lax.rsqrt(jnp.mean(x * x, axis=-1, keepdims=True) + RMS_EPS)
    return x * r, r


def _rms_bwd(d_hat, xhat, r):
    return r * (d_hat - xhat * jnp.mean(d_hat * xhat, axis=-1, keepdims=True))


def _sigmoid(x):
    return 1.0 / (1.0 + jnp.exp(-x))


_GELU_C = math.sqrt(2.0 / math.pi)
_GELU_K = 0.044715


def _gelu(x):
    return 0.5 * x * (1.0 + jnp.tanh(_GELU_C * (x + _GELU_K * x * x * x)))


def _gelu_grad(x):
    th = jnp.tanh(_GELU_C * (x + _GELU_K * x * x * x))
    return 0.5 * (1.0 + th) + 0.5 * x * (1.0 - th * th) * _GELU_C * (1.0 + 3.0 * _GELU_K * x * x)


def _glu_weight(ref):
    v = ref[...]
    return jnp.concatenate([v[:, :, :D_SSM], v[:, :, D_SSM:]], axis=1).reshape(D_SSM, D_SSM)


def _glu_pack(w):
    v = w.reshape(N_SHARD, 128, D_SSM)
    return jnp.concatenate([v[:, :64, :], v[:, 64:, :]], axis=2)


def _pool_diff(ext, row0, tm):
    rows = row0 + lax.broadcasted_iota(jnp.int32, (tm, 1), 0)
    outs = []
    for gi, w in enumerate(POOL_WINDOWS):
        e = ext[:, gi * POOL_GROUP:(gi + 1) * POOL_GROUP]
        s = e
        k = 1
        while k < w:
            s = s + pltpu.roll(s, k, 0)
            k *= 2
        inv = 1.0 / jnp.minimum(rows + 1, w).astype(F32)
        outs.append(s[POOL_HALO:, :] * inv - e[POOL_HALO:, :])
    return outs


def _mix_in_fwd(h, g1, wp, layer, w_pool, scale):
    L = h.shape[0]
    tm = min(TM, L)

    def body(h_ref, g_ref, w_ref, wp_ref, sc_ref, u_ref, yp_ref, carry):
        i = pl.program_id(0)

        @pl.when(i == 0)
        def _():
            carry[...] = jnp.zeros_like(carry)

        xhat, _ = _rms_hat(h_ref[...])
        n1 = (xhat * g_ref[...]).astype(BF16)
        u = _dot(n1, w_ref[...].reshape(D_MODEL, D_MODEL))
        u_ref[...] = u
        up = u[:, :D_POOL]
        ext = jnp.concatenate([carry[...], up], axis=0)
        carry[...] = up[tm - POOL_HALO:, :]
        diffs = _pool_diff(ext, i * tm, tm)
        for gi in range(4):
            cols = slice(gi * POOL_GROUP, (gi + 1) * POOL_GROUP)
            yp_ref[:, cols] = _dot(diffs[gi].astype(BF16), wp_ref[gi]) * sc_ref[:, cols]

    blk, idx = P_IN_BLK
    return pl.pallas_call(
        body, name="mix_in_fwd", grid=(L // tm,),
        in_specs=[pl.BlockSpec((tm, D_MODEL), lambda i: (i, 0)),
                  pl.BlockSpec((None, 1, D_MODEL), lambda i: (layer, 0, 0)),
                  pl.BlockSpec((N_SHARD, None, blk, D_MODEL), lambda i: (0, 0, idx, 0)),
                  pl.BlockSpec((None, 4, POOL_GROUP, POOL_GROUP), lambda i: (layer, 0, 0, 0)),
                  pl.BlockSpec((None, 1, D_POOL), lambda i: (layer, 0, 0))],
        out_specs=[pl.BlockSpec((tm, D_MODEL), lambda i: (i, 0)),
                   pl.BlockSpec((tm, D_POOL), lambda i: (i, 0))],
        out_shape=[jax.ShapeDtypeStruct((L, D_MODEL), F32), jax.ShapeDtypeStruct((L, D_POOL), F32)],
        scratch_shapes=[pltpu.VMEM((POOL_HALO, D_POOL), F32)],
        compiler_params=_cparams(1),
    )(h, g1, wp, w_pool, scale)


def _cmul(xr, xi, yr, yi):
    return xr * yr - xi * yi, xr * yi + xi * yr


SCAN_BLOCK = 64
N_SCAN_TABLES = 26


def _permute_rows(src, dst, n_rows):
    for b in range(n_rows // SCAN_BLOCK):
        for tau in range(SUBLANES):
            dst[pl.ds(SCAN_BLOCK * b + SUBLANES * tau, SUBLANES), :] = (
                src[pl.ds(SCAN_BLOCK * b + tau, SUBLANES, stride=SUBLANES), :])


def _scan_tables(ar, ai, tab, reverse):
    c = ar.shape[1]
    row = lax.broadcasted_iota(jnp.int32, (SUBLANES, c), 0)
    zero = jnp.zeros((SUBLANES, c), F32)
    full = lambda v: jnp.broadcast_to(v, (SUBLANES, c))
    pw = [(ar, ai)]
    for _ in range(SUBLANES - 1):
        pw.append(_cmul(*pw[-1], ar, ai))
    a8 = pw[-1]
    a16 = _cmul(*a8, *a8)
    a32 = _cmul(*a16, *a16)
    tab[0] = full(ar)
    tab[1] = full(ai)
    for n, (s, (pr, pi)) in enumerate(((1, a8), (2, a16), (4, a32))):
        keep = (row < SUBLANES - s) if reverse else (row >= s)
        tab[2 + 2 * n] = jnp.where(keep, pr, zero)
        tab[3 + 2 * n] = jnp.where(keep, pi, zero)
    cur = a8
    qr, qi = zero, zero
    for n in range(SUBLANES):
        at = (SUBLANES - 1 - n) if reverse else n
        qr = jnp.where(row == at, cur[0], qr)
        qi = jnp.where(row == at, cur[1], qi)
        cur = _cmul(*cur, *a8)
    tab[8] = qr
    tab[9] = qi
    for tau in range(SUBLANES):
        pr, pi = pw[SUBLANES - 1 - tau] if reverse else pw[tau]
        tab[10 + 2 * tau] = full(pr)
        tab[11 + 2 * tau] = full(pi)


def _cmac(xr, xi, ar, ai, yr, yi):
    return xr + ar * yr - ai * yi, xi + ar * yi + ai * yr


def _chain_segments(er, ei, c_r, c_i, tab, cols, reverse):
    tr, ti = er, ei
    for n, s in enumerate((1, 2, 4)):
        shift = SUBLANES - s if reverse else s
        tr, ti = _cmac(tr, ti, tab[2 + 2 * n, :, cols], tab[3 + 2 * n, :, cols],
                       pltpu.roll(tr, shift, 0), pltpu.roll(ti, shift, 0))
    return _cmac(tr, ti, tab[8, :, cols], tab[9, :, cols], c_r, c_i)


def _ssm_fwd(u, layer, bpad, cpad, ar, ai, dskip):
    L = u.shape[0]
    ts = min(TS, L)
    nq = 4
    cq = N_STATE // nq

    def body(u_ref, bp_ref, cp_ref, ar_ref, ai_ref, dsk_ref, sre_ref, sim_ref, y_ref, cr, ci, tab, up, yp):
        t = pl.program_id(1)

        @pl.when(t == 0)
        def _():
            cr[...] = jnp.zeros_like(cr)
            ci[...] = jnp.zeros_like(ci)
            _scan_tables(ar_ref[...], ai_ref[...], tab, reverse=False)

        _permute_rows(u_ref, up, ts)
        uf = up[...]
        ub = uf.astype(BF16)
        for jj in range(4):
            bu = _dot(ub, bp_ref[jj])
            sre_ref[:, jj * 128:(jj + 1) * 128] = bu[:, :128]
            sim_ref[:, jj * 128:(jj + 1) * 128] = bu[:, 128:]

        shp = (SUBLANES, SCAN_LANES)
        first_row = lax.broadcasted_iota(jnp.int32, shp, 0) == 0
        for cc in range(cq // SCAN_LANES):
            cols = slice(cc * SCAN_LANES, (cc + 1) * SCAN_LANES)

            def block(b, carry, cols=cols):
                c_r, c_i = carry
                base = pl.multiple_of(b * SCAN_BLOCK, SCAN_BLOCK)
                rows = lambda tau: pl.ds(base + SUBLANES * tau, SUBLANES)
                a_r, a_i = tab[0, :, cols], tab[1, :, cols]
                ys = [(sre_ref[rows(0), cols], sim_ref[rows(0), cols])]
                for tau in range(1, SUBLANES):
                    ys.append(_cmac(sre_ref[rows(tau), cols], sim_ref[rows(tau), cols], a_r, a_i, *ys[-1]))
                tr, ti = _chain_segments(*ys[-1], c_r, c_i, tab, cols, reverse=False)
                in_r = jnp.where(first_row, c_r, pltpu.roll(tr, 1, 0))
                in_i = jnp.where(first_row, c_i, pltpu.roll(ti, 1, 0))
                for tau in range(SUBLANES):
                    sr, si = _cmac(*ys[tau], tab[10 + 2 * tau, :, cols], tab[11 + 2 * tau, :, cols], in_r, in_i)
                    sre_ref[rows(tau), cols] = sr
                    sim_ref[rows(tau), cols] = si
                return (jnp.broadcast_to(tr[SUBLANES - 1:, :], shp), jnp.broadcast_to(ti[SUBLANES - 1:, :], shp))

            c_r, c_i = lax.fori_loop(0, ts // SCAN_BLOCK, block, (cr[:, cols], ci[:, cols]), unroll=2)
            cr[:, cols] = c_r
            ci[:, cols] = c_i

        acc = dsk_ref[...] * uf
        for jj in range(4):
            cols = slice(jj * 128, (jj + 1) * 128)
            scat = jnp.concatenate([sre_ref[:, cols], sim_ref[:, cols]], axis=1).astype(BF16)
            acc = acc + _dot(scat, cp_ref[jj])
        yp[...] = acc
        _permute_rows(yp, y_ref, ts)

    return pl.pallas_call(
        body, name="ssm_fwd", grid=(nq, L // ts),
        in_specs=[pl.BlockSpec((ts, 128), lambda q, t: (t, 4 + q)),
                  pl.BlockSpec((None, 4, 128, 256), lambda q, t: (layer, q, 0, 0)),
                  pl.BlockSpec((None, 4, 256, 128), lambda q, t: (layer, q, 0, 0)),
                  pl.BlockSpec((None, 1, cq), lambda q, t: (layer, 0, q)),
                  pl.BlockSpec((None, 1, cq), lambda q, t: (layer, 0, q)),
                  pl.BlockSpec((None, 1, 128), lambda q, t: (layer, 0, q))],
        out_specs=[pl.BlockSpec((ts, cq), lambda q, t: (t, q)),
                   pl.BlockSpec((ts, cq), lambda q, t: (t, q)),
                   pl.BlockSpec((ts, 128), lambda q, t: (t, q))],
        out_shape=[jax.ShapeDtypeStruct((L, N_STATE), F32), jax.ShapeDtypeStruct((L, N_STATE), F32),
                   jax.ShapeDtypeStruct((L, D_SSM), F32)],
        scratch_shapes=[pltpu.VMEM((SUBLANES, cq), F32), pltpu.VMEM((SUBLANES, cq), F32),
                        pltpu.VMEM((N_SCAN_TABLES, SUBLANES, cq), F32),
                        pltpu.VMEM((ts, 128), F32), pltpu.VMEM((ts, 128), F32)],
        compiler_params=_cparams(2),
    )(u, bpad, cpad, ar, ai, dskip)


def _mix_out_fwd(yraw, ypool, h, wp, layer, b_glu):
    L = h.shape[0]
    tm = min(TM, L)

    def body(yr_ref, yp_ref, h_ref, wglu_ref, b_ref, wout_ref, o_ref):
        y = _gelu(yr_ref[...])
        z = _dot(y.astype(BF16), _glu_weight(wglu_ref)) + b_ref[...]
        o = y * _sigmoid(z)
        mix = jnp.concatenate([yp_ref[...], o], axis=1).astype(BF16)
        o_ref[...] = h_ref[...] + _dot(mix, wout_ref[...].reshape(D_MODEL, D_MODEL))

    gb, gi = P_GLU_BLK
    ob, oi = P_OUT_BLK
    return pl.pallas_call(
        body, name="mix_out_fwd", grid=(L // tm,),
        in_specs=[pl.BlockSpec((tm, D_SSM), lambda i: (i, 0)),
                  pl.BlockSpec((tm, D_POOL), lambda i: (i, 0)),
                  pl.BlockSpec((tm, D_MODEL), lambda i: (i, 0)),
                  pl.BlockSpec((N_SHARD, None, gb, D_MODEL), lambda i: (0, 0, gi, 0)),
                  pl.BlockSpec((None, 1, D_SSM), lambda i: (layer, 0, 0)),
                  pl.BlockSpec((N_SHARD, None, ob, D_MODEL), lambda i: (0, 0, oi, 0))],
        out_specs=pl.BlockSpec((tm, D_MODEL), lambda i: (i, 0)),
        out_shape=jax.ShapeDtypeStruct((L, D_MODEL), F32),
        compiler_params=_cparams(1),
    )(yraw, ypool, h, wp, b_glu, wp)


def _ffn_weights(ref, k):
    return ref[k, 0:FF_SHARD, :], ref[k, FF_SHARD:2 * FF_SHARD, :], ref[k, 2 * FF_SHARD:P_FF_ROWS, :]


def _ffn_weight_spec():
    return pl.BlockSpec((N_SHARD, None, P_FF_ROWS, D_MODEL), lambda m, k: (0, 0, 0, 0),
                        pipeline_mode=pl.Buffered(1))


def _ffn_fwd(h, g2, wp, layer):
    L = h.shape[0]
    tm = min(TM_FFN_LONG, L)

    def body(h_ref, g_ref, w_ref, o_ref, n2_ref, act_ref, dgate_ref, dup_ref):
        k = pl.program_id(1)

        @pl.when(k == 0)
        def _():
            x = h_ref[...]
            xhat, _ = _rms_hat(x)
            n2_ref[...] = (xhat * g_ref[...]).astype(BF16)
            o_ref[...] = x

        wd, wg_t, wu_t = _ffn_weights(w_ref, k)
        n2 = n2_ref[...]
        gate = _dot_nt(n2, wg_t)
        up = _dot_nt(n2, wu_t)
        sg = _sigmoid(gate)
        silu = gate * sg
        act = (silu * up).astype(BF16)
        act_ref[...] = act
        dgate_ref[...] = (up * (sg * (1.0 + gate * (1.0 - sg)))).astype(BF16)
        dup_ref[...] = silu.astype(BF16)
        o_ref[...] += _dot(act, wd)

    act_shape = jax.ShapeDtypeStruct((N_SHARD, L, FF_SHARD), BF16)
    return pl.pallas_call(
        body, name="ffn_fwd", grid=(L // tm, N_SHARD),
        in_specs=[pl.BlockSpec((tm, D_MODEL), lambda m, k: (m, 0)),
                  pl.BlockSpec((None, 1, D_MODEL), lambda m, k: (layer, 0, 0)),
                  _ffn_weight_spec()],
        out_specs=[pl.BlockSpec((tm, D_MODEL), lambda m, k: (m, 0)),
                   pl.BlockSpec((tm, D_MODEL), lambda m, k: (m, 0)),
                   pl.BlockSpec((None, tm, FF_SHARD), lambda m, k: (k, m, 0)),
                   pl.BlockSpec((None, tm, FF_SHARD), lambda m, k: (k, m, 0)),
                   pl.BlockSpec((None, tm, FF_SHARD), lambda m, k: (k, m, 0))],
        out_shape=[jax.ShapeDtypeStruct((L, D_MODEL), F32), jax.ShapeDtypeStruct((L, D_MODEL), BF16),
                   act_shape, act_shape, act_shape],
        compiler_params=_cparams(2),
    )(h, g2, wp)


def _final_fwd_bwd(h, gf, target):
    L = h.shape[0]
    tm = min(TM, L)

    def body(h_ref, g_ref, t_ref, dh_ref, loss_ref, dg_ref):
        i = pl.program_id(0)

        @pl.when(i == 0)
        def _():
            loss_ref[...] = jnp.zeros_like(loss_ref)
            dg_ref[...] = jnp.zeros_like(dg_ref)

        xhat, r = _rms_hat(h_ref[...])
        g = g_ref[...]
        e = xhat * g - t_ref[...]
        loss_ref[...] += 0.5 * jnp.sum(jnp.mean(e * e, axis=-1, keepdims=True), axis=0, keepdims=True)
        dy = e * (1.0 / D_MODEL)
        dg_ref[...] += jnp.sum(dy * xhat, axis=0, keepdims=True)
        dh_ref[...] = _rms_bwd(dy * g, xhat, r)

    return pl.pallas_call(
        body, name="final_fwd_bwd", grid=(L // tm,),
        in_specs=[pl.BlockSpec((tm, D_MODEL), lambda i: (i, 0)),
                  pl.BlockSpec((1, D_MODEL), lambda i: (0, 0)),
                  pl.BlockSpec((tm, D_MODEL), lambda i: (i, 0))],
        out_specs=[pl.BlockSpec((tm, D_MODEL), lambda i: (i, 0)),
                   pl.BlockSpec((1, 1), lambda i: (0, 0)),
                   pl.BlockSpec((1, D_MODEL), lambda i: (0, 0))],
        out_shape=[jax.ShapeDtypeStruct((L, D_MODEL), F32), jax.ShapeDtypeStruct((1, 1), F32),
                   jax.ShapeDtypeStruct((1, D_MODEL), F32)],
        compiler_params=_cparams(1),
    )(h, gf, target)


def _ffn_bwd_act(dh, h, g2, fgate_s, fup_s, wp, layer):
    L = h.shape[0]
    tm = min(TM_FFN, L)
    sub = tm // FFN_SPLIT

    def body(dh_ref, h_ref, g_ref, fgate_ref, fup_ref, w_ref,
             dhm_ref, dg_ref, dgate_ref, dup_ref, dhb_ref):
        m, k = pl.program_id(0), pl.program_id(1)
        dn2 = dhm_ref

        @pl.when(jnp.logical_and(m == 0, k == 0))
        def _():
            dg_ref[...] = jnp.zeros_like(dg_ref)

        @pl.when(k == 0)
        def _():
            dhb_ref[...] = dh_ref[...].astype(BF16)
            dn2[...] = jnp.zeros_like(dn2)

        wd, wg_t, wu_t = _ffn_weights(w_ref, k)
        for rows in (slice(r * sub, (r + 1) * sub) for r in range(tm // sub)):
            dact = _dot_nt(dhb_ref[rows, :], wd)
            dgate = (dact * fgate_ref[rows, :].astype(F32)).astype(BF16)
            dup = (dact * fup_ref[rows, :].astype(F32)).astype(BF16)
            dgate_ref[rows, :] = dgate
            dup_ref[rows, :] = dup
            dn2[rows, :] += _dot(dgate, wg_t) + _dot(dup, wu_t)

        @pl.when(k == N_SHARD - 1)
        def _():
            xhat, r = _rms_hat(h_ref[...])
            d = dn2[...]
            dg_ref[...] += jnp.sum(d * xhat, axis=0, keepdims=True)
            dhm_ref[...] = dh_ref[...] + _rms_bwd(d * g_ref[...], xhat, r)

    act_spec = pl.BlockSpec((None, tm, FF_SHARD), lambda m, k: (k, m, 0))
    act_shape = jax.ShapeDtypeStruct((N_SHARD, L, FF_SHARD), BF16)
    row_spec = pl.BlockSpec((tm, D_MODEL), lambda m, k: (m, 0))
    return pl.pallas_call(
        body, name="ffn_bwd_act", grid=(L // tm, N_SHARD),
        in_specs=[row_spec, row_spec,
                  pl.BlockSpec((None, 1, D_MODEL), lambda m, k: (layer, 0, 0)),
                  act_spec, act_spec,
                  _ffn_weight_spec()],
        out_specs=[row_spec,
                   pl.BlockSpec((1, D_MODEL), lambda m, k: (0, 0)),
                   act_spec, act_spec, row_spec],
        out_shape=[jax.ShapeDtypeStruct((L, D_MODEL), F32), jax.ShapeDtypeStruct((1, D_MODEL), F32),
                   act_shape, act_shape, jax.ShapeDtypeStruct((L, D_MODEL), BF16)],
        compiler_params=_cparams(2),
    )(dh, h, g2, fgate_s, fup_s, wp)


def _ffn_bwd_w(n2, dgate_s, dup_s, act_s, dhb, gbuf):
    L = n2.shape[0]
    tm = min(TM_FFN_LONG, L)

    def body(n2_ref, dgate_ref, dup_ref, act_ref, dhb_ref, g_in, g_ref):
        m = pl.program_id(1)

        @pl.when(m == 0)
        def _():
            g_ref[...] = jnp.zeros_like(g_ref)

        n2v = n2_ref[...]
        g_ref[0:FF_SHARD, :] += _dot_tn(act_ref[...], dhb_ref[...])
        g_ref[FF_SHARD:2 * FF_SHARD, :] += _dot_tn(dgate_ref[...], n2v)
        g_ref[2 * FF_SHARD:P_FF_ROWS, :] += _dot_tn(dup_ref[...], n2v)

    act_spec = pl.BlockSpec((None, tm, FF_SHARD), lambda k, m: (k, m, 0))
    row_spec = pl.BlockSpec((tm, D_MODEL), lambda k, m: (m, 0))
    return pl.pallas_call(
        body, name="ffn_bwd_w", grid=(N_SHARD, L // tm),
        in_specs=[row_spec, act_spec, act_spec, act_spec, row_spec, pl.BlockSpec(memory_space=pl.ANY)],
        out_specs=pl.BlockSpec((None, None, P_FF_ROWS, D_MODEL), lambda k, m: (0, k, 0, 0)),
        out_shape=jax.ShapeDtypeStruct(gbuf.shape, F32),
        input_output_aliases={5: 0},
        compiler_params=_cparams(2),
    )(n2, dgate_s, dup_s, act_s, dhb, gbuf)


def _mix_out_bwd(dhm, yraw, ypool, wp, layer, b_glu, gbuf):
    L = dhm.shape[0]
    tm = min(TM, L)

    def body(dhm_ref, yr_ref, yp_ref, wglu_ref, b_ref, wout_ref, g1_in,
             dyr_ref, dyp_ref, db_ref, g1_ref, dwout, dwglu, gpack):
        i = pl.program_id(0)

        @pl.when(i == 0)
        def _():
            db_ref[...] = jnp.zeros_like(db_ref)
            dwout[...] = jnp.zeros_like(dwout)
            dwglu[...] = jnp.zeros_like(dwglu)

        dhb = dhm_ref[...].astype(BF16)
        wglu = _glu_weight(wglu_ref)
        dmix = _dot_nt(dhb, wout_ref[...].reshape(D_MODEL, D_MODEL))
        dyp_ref[...] = dmix[:, :D_POOL]
        d_o = dmix[:, D_POOL:]
        yraw_v = yr_ref[...]
        y = _gelu(yraw_v)
        yb = y.astype(BF16)
        sig = _sigmoid(_dot(yb, wglu) + b_ref[...])
        mix = jnp.concatenate([yp_ref[...], y * sig], axis=1).astype(BF16)
        dwout[...] += _dot_tn(mix, dhb).reshape(N_SHARD, 256, D_MODEL)
        dz = d_o * y * sig * (1.0 - sig)
        dzb = dz.astype(BF16)
        db_ref[...] += jnp.sum(dz, axis=0, keepdims=True)
        dwglu[...] += _dot_tn(yb, dzb)
        dy = d_o * sig + _dot_nt(dzb, wglu)
        dyr_ref[...] = dy * _gelu_grad(yraw_v)

        @pl.when(i == n_steps - 1)
        def _():
            gpack[:, :gb, :] = _glu_pack(dwglu[...])
            gpack[:, gb:, :] = jnp.zeros((N_SHARD, P_GLU_PAD - gb, D_MODEL), F32)
            pltpu.sync_copy(gpack, g1_ref.at[0, :, pl.ds(gb * gi, P_GLU_PAD), :])
            pltpu.sync_copy(dwout, g1_ref.at[0, :, pl.ds(ob * oi, ob), :])

    gb, gi = P_GLU_BLK
    ob, oi = P_OUT_BLK
    n_steps = L // tm
    return pl.pallas_call(
        body, name="mix_out_bwd", grid=(n_steps,),
        in_specs=[pl.BlockSpec((tm, D_MODEL), lambda i: (i, 0)),
                  pl.BlockSpec((tm, D_SSM), lambda i: (i, 0)),
                  pl.BlockSpec((tm, D_POOL), lambda i: (i, 0)),
                  pl.BlockSpec((N_SHARD, None, gb, D_MODEL), lambda i: (0, 0, gi, 0)),
                  pl.BlockSpec((None, 1, D_SSM), lambda i: (layer, 0, 0)),
                  pl.BlockSpec((N_SHARD, None, ob, D_MODEL), lambda i: (0, 0, oi, 0)),
                  pl.BlockSpec(memory_space=pl.ANY)],
        out_specs=[pl.BlockSpec((tm, D_SSM), lambda i: (i, 0)),
                   pl.BlockSpec((tm, D_POOL), lambda i: (i, 0)),
                   pl.BlockSpec((1, D_SSM), lambda i: (0, 0)),
                   pl.BlockSpec(memory_space=pl.ANY)],
        out_shape=[jax.ShapeDtypeStruct((L, D_SSM), F32), jax.ShapeDtypeStruct((L, D_POOL), F32),
                   jax.ShapeDtypeStruct((1, D_SSM), F32),
                   jax.ShapeDtypeStruct(gbuf.shape, F32)],
        scratch_shapes=[pltpu.VMEM((N_SHARD, ob, D_MODEL), F32), pltpu.VMEM((D_SSM, D_SSM), F32),
                        pltpu.VMEM((N_SHARD, P_GLU_PAD, D_MODEL), F32)],
        input_output_aliases={6: 3},
        compiler_params=_cparams(1),
    )(dhm, yraw, ypool, wp, b_glu, wp, gbuf)


def _ssm_bwd(dyraw, u, sre, sim, layer, cpad_t, bpad_t, ar, ai, dskip):
    L = u.shape[0]
    ts = min(TS, L)
    nt = L // ts
    nq = 4
    cq = N_STATE // nq

    def body(dy_ref, u_ref, sre_ref, sim_ref, ct_ref, bt_ref, ar_ref, ai_ref, dsk_ref,
             du_ref, dcp_ref, dbp_ref, dar_ref, dai_ref, ddsk_ref, gre, gim, cr, ci, tab, accr, acci, up, dyp):
        t = pl.program_id(1)

        @pl.when(t == 0)
        def _():
            for ref in (cr, ci, accr, acci, dcp_ref, dbp_ref, ddsk_ref):
                ref[...] = jnp.zeros_like(ref)
            _scan_tables(ar_ref[...], -ai_ref[...], tab, reverse=True)

        _permute_rows(dy_ref, dyp, ts)
        _permute_rows(u_ref, up, ts)
        dy = dyp[...]
        dyb = dy.astype(BF16)
        uf = up[...]
        ub = uf.astype(BF16)
        for jj in range(4):
            cols = slice(jj * 128, (jj + 1) * 128)
            ds = _dot(dyb, ct_ref[jj])
            gre[:, cols] = ds[:, :128]
            gim[:, cols] = ds[:, 128:]
            scat = jnp.concatenate([sre_ref[:, cols], sim_ref[:, cols]], axis=1).astype(BF16)
            dcp_ref[jj] += _dot_tn(scat, dyb)

        n_blk = ts // SCAN_BLOCK
        shp = (SUBLANES, SCAN_LANES)
        last_row = lax.broadcasted_iota(jnp.int32, shp, 0) == SUBLANES - 1
        for cc in range(cq // SCAN_LANES):
            cols = slice(cc * SCAN_LANES, (cc + 1) * SCAN_LANES)

            def block(i, carry, cols=cols):
                c_r, c_i, a_r, a_i = carry
                base = pl.multiple_of((n_blk - 1 - i) * SCAN_BLOCK, SCAN_BLOCK)
                rows = lambda tau: pl.ds(base + SUBLANES * tau, SUBLANES)
                m_r, m_i = tab[0, :, cols], tab[1, :, cols]
                ys = [None] * SUBLANES
                ys[SUBLANES - 1] = (gre[rows(SUBLANES - 1), cols], gim[rows(SUBLANES - 1), cols])
                for tau in reversed(range(SUBLANES - 1)):
                    ys[tau] = _cmac(gre[rows(tau), cols], gim[rows(tau), cols], m_r, m_i, *ys[tau + 1])
                tr, ti = _chain_segments(*ys[0], c_r, c_i, tab, cols, reverse=True)
                in_r = jnp.where(last_row, c_r, pltpu.roll(tr, SUBLANES - 1, 0))
                in_i = jnp.where(last_row, c_i, pltpu.roll(ti, SUBLANES - 1, 0))
                gs = [_cmac(*ys[tau], tab[10 + 2 * tau, :, cols], tab[11 + 2 * tau, :, cols], in_r, in_i)
                      for tau in range(SUBLANES)]
                for tau in range(SUBLANES):
                    gre[rows(tau), cols] = gs[tau][0]
                    gim[rows(tau), cols] = gs[tau][1]
                    if tau < SUBLANES - 1:
                        nr, ni = gs[tau + 1]
                    else:
                        nr = jnp.where(last_row, c_r, pltpu.roll(gs[0][0], SUBLANES - 1, 0))
                        ni = jnp.where(last_row, c_i, pltpu.roll(gs[0][1], SUBLANES - 1, 0))
                    sr, si = sre_ref[rows(tau), cols], sim_ref[rows(tau), cols]
                    a_r = a_r + sr * nr + si * ni
                    a_i = a_i + sr * ni - si * nr
                return (jnp.broadcast_to(tr[:1, :], shp), jnp.broadcast_to(ti[:1, :], shp), a_r, a_i)

            c_r, c_i, a_r, a_i = lax.fori_loop(
                0, n_blk, block, (cr[:, cols], ci[:, cols], accr[:, cols], acci[:, cols]), unroll=2)
            cr[:, cols] = c_r
            ci[:, cols] = c_i
            accr[:, cols] = a_r
            acci[:, cols] = a_i

        acc = dsk_ref[...] * dy
        for jj in range(4):
            cols = slice(jj * 128, (jj + 1) * 128)
            gcat = jnp.concatenate([gre[:, cols], gim[:, cols]], axis=1).astype(BF16)
            acc = acc + _dot(gcat, bt_ref[jj])
            dbp_ref[jj] += _dot_tn(ub, gcat)
        ddsk_ref[...] += jnp.sum(dy * uf, axis=0, keepdims=True)
        dyp[...] = acc
        _permute_rows(dyp, du_ref, ts)

        @pl.when(t == nt - 1)
        def _():
            dar_ref[...] = jnp.sum(accr[...], axis=0, keepdims=True)
            dai_ref[...] = jnp.sum(acci[...], axis=0, keepdims=True)

    f32_scr = lambda *s: pltpu.VMEM(s, F32)
    return pl.pallas_call(
        body, name="ssm_bwd", grid=(nq, nt),
        in_specs=[pl.BlockSpec((ts, 128), lambda q, t: (nt - 1 - t, q)),
                  pl.BlockSpec((ts, 128), lambda q, t: (nt - 1 - t, 4 + q)),
                  pl.BlockSpec((ts, cq), lambda q, t: (nt - 1 - t, q)),
                  pl.BlockSpec((ts, cq), lambda q, t: (nt - 1 - t, q)),
                  pl.BlockSpec((None, 4, 128, 256), lambda q, t: (layer, q, 0, 0)),
                  pl.BlockSpec((None, 4, 256, 128), lambda q, t: (layer, q, 0, 0)),
                  pl.BlockSpec((None, 1, cq), lambda q, t: (layer, 0, q)),
                  pl.BlockSpec((None, 1, cq), lambda q, t: (layer, 0, q)),
                  pl.BlockSpec((None, 1, 128), lambda q, t: (layer, 0, q))],
        out_specs=[pl.BlockSpec((ts, 128), lambda q, t: (nt - 1 - t, q)),
                   pl.BlockSpec((4, 256, 128), lambda q, t: (q, 0, 0)),
                   pl.BlockSpec((4, 128, 256), lambda q, t: (q, 0, 0)),
                   pl.BlockSpec((1, cq), lambda q, t: (0, q)),
                   pl.BlockSpec((1, cq), lambda q, t: (0, q)),
                   pl.BlockSpec((1, 128), lambda q, t: (0, q))],
        out_shape=[jax.ShapeDtypeStruct((L, D_SSM), F32),
                   jax.ShapeDtypeStruct((N_PAIRS, 256, 128), F32), jax.ShapeDtypeStruct((N_PAIRS, 128, 256), F32),
                   jax.ShapeDtypeStruct((1, N_STATE), F32), jax.ShapeDtypeStruct((1, N_STATE), F32),
                   jax.ShapeDtypeStruct((1, D_SSM), F32)],
        scratch_shapes=[f32_scr(ts, cq), f32_scr(ts, cq), f32_scr(SUBLANES, cq), f32_scr(SUBLANES, cq),
                        f32_scr(N_SCAN_TABLES, SUBLANES, cq), f32_scr(SUBLANES, cq), f32_scr(SUBLANES, cq),
                        f32_scr(ts, 128), f32_scr(ts, 128)],
        compiler_params=_cparams(2),
    )(dyraw, u, sre, sim, cpad_t, bpad_t, ar, ai, dskip)


def _pool_bwd(dyp, u, layer, w_pool, scale):
    L = u.shape[0]
    tm = min(TM, L)
    nt = L // tm
    halo_per_tile = tm // POOL_HALO

    def body(dyp_ref, u_ref, halo_ref, wp_ref, sc_ref, du_ref, dwp_ref, dsc_ref, carry):
        i = pl.program_id(0)
        tile = nt - 1 - i

        @pl.when(i == 0)
        def _():
            carry[...] = jnp.zeros_like(carry)
            dwp_ref[...] = jnp.zeros_like(dwp_ref)
            dsc_ref[...] = jnp.zeros_like(dsc_ref)

        up = u_ref[...]
        halo = jnp.where(tile > 0, halo_ref[...], jnp.zeros_like(halo_ref))
        diffs = _pool_diff(jnp.concatenate([halo, up], axis=0), tile * tm, tm)
        rows = tile * tm + lax.broadcasted_iota(jnp.int32, (tm, 1), 0)
        n_ext = tm + POOL_HALO
        for gi, w in enumerate(POOL_WINDOWS):
            cols = slice(gi * POOL_GROUP, (gi + 1) * POOL_GROUP)
            db = diffs[gi].astype(BF16)
            dyp = dyp_ref[:, cols]
            dsc_ref[:, cols] += jnp.sum(dyp * _dot(db, wp_ref[gi]), axis=0, keepdims=True)
            dp = (dyp * sc_ref[:, cols]).astype(BF16)
            ddiff = _dot_nt(dp, wp_ref[gi])
            dwp_ref[gi] += _dot_tn(db, dp)
            e = ddiff * (1.0 / jnp.minimum(rows + 1, w).astype(F32))
            s = jnp.concatenate([e, carry[:, cols]], axis=0)
            k = 1
            while k < w:
                s = s + pltpu.roll(s, n_ext - k, 0)
                k *= 2
            du_ref[:, cols] = s[:tm, :] - ddiff
            carry[:, cols] = e[:POOL_HALO, :]

    return pl.pallas_call(
        body, name="pool_bwd", grid=(nt,),
        in_specs=[pl.BlockSpec((tm, D_POOL), lambda i: (nt - 1 - i, 0)),
                  pl.BlockSpec((tm, D_POOL), lambda i: (nt - 1 - i, 0)),
                  pl.BlockSpec((POOL_HALO, D_POOL), lambda i: (jnp.maximum((nt - 1 - i) * halo_per_tile - 1, 0), 0)),
                  pl.BlockSpec((None, 4, POOL_GROUP, POOL_GROUP), lambda i: (layer, 0, 0, 0)),
                  pl.BlockSpec((None, 1, D_POOL), lambda i: (layer, 0, 0))],
        out_specs=[pl.BlockSpec((tm, D_POOL), lambda i: (nt - 1 - i, 0)),
                   pl.BlockSpec((4, POOL_GROUP, POOL_GROUP), lambda i: (0, 0, 0)),
                   pl.BlockSpec((1, D_POOL), lambda i: (0, 0))],
        out_shape=[jax.ShapeDtypeStruct((L, D_POOL), F32),
                   jax.ShapeDtypeStruct((4, POOL_GROUP, POOL_GROUP), F32),
                   jax.ShapeDtypeStruct((1, D_POOL), F32)],
        scratch_shapes=[pltpu.VMEM((POOL_HALO, D_POOL), F32)],
        compiler_params=_cparams(1),
    )(dyp, u, u, w_pool, scale)


def _mix_in_bwd(dup, dus, h, dhm, g1, wp, layer, gbuf):
    L = h.shape[0]
    tm = min(TM, L)
    n_steps = L // tm
    blk, idx = P_IN_BLK

    def body(dup_ref, dus_ref, h_ref, dhm_ref, g_ref, w_ref, g1_in, dh_ref, dg_ref, g1_ref, dwin):
        i = pl.program_id(0)

        @pl.when(i == 0)
        def _():
            dg_ref[...] = jnp.zeros_like(dg_ref)
            dwin[...] = jnp.zeros_like(dwin)

        du = jnp.concatenate([dup_ref[...], dus_ref[...]], axis=1).astype(BF16)
        dn1 = _dot_nt(du, w_ref[...].reshape(D_MODEL, D_MODEL))
        xhat, r = _rms_hat(h_ref[...])
        g = g_ref[...]
        n1 = (xhat * g).astype(BF16)
        dwin[...] += _dot_tn(n1, du).reshape(N_SHARD, blk, D_MODEL)
        dg_ref[...] += jnp.sum(dn1 * xhat, axis=0, keepdims=True)
        dh_ref[...] = dhm_ref[...] + _rms_bwd(dn1 * g, xhat, r)

        @pl.when(i == n_steps - 1)
        def _():
            pltpu.sync_copy(dwin, g1_ref.at[0, :, pl.ds(blk * idx, blk), :])

    row_spec = pl.BlockSpec((tm, D_MODEL), lambda i: (i, 0))
    half_spec = pl.BlockSpec((tm, D_POOL), lambda i: (i, 0))
    return pl.pallas_call(
        body, name="mix_in_bwd", grid=(n_steps,),
        in_specs=[half_spec, half_spec, row_spec, row_spec,
                  pl.BlockSpec((None, 1, D_MODEL), lambda i: (layer, 0, 0)),
                  pl.BlockSpec((N_SHARD, None, blk, D_MODEL), lambda i: (0, 0, idx, 0)),
                  pl.BlockSpec(memory_space=pl.ANY)],
        out_specs=[row_spec, pl.BlockSpec((1, D_MODEL), lambda i: (0, 0)), pl.BlockSpec(memory_space=pl.ANY)],
        out_shape=[jax.ShapeDtypeStruct((L, D_MODEL), F32), jax.ShapeDtypeStruct((1, D_MODEL), F32),
                   jax.ShapeDtypeStruct(gbuf.shape, F32)],
        scratch_shapes=[pltpu.VMEM((N_SHARD, blk, D_MODEL), F32)],
        input_output_aliases={6: 2},
        compiler_params=_cparams(1),
    )(dup, dus, h, dhm, g1, wp, gbuf)


def _disc_math(lr, li, ldt, br_t, bi_t):
    dt = jnp.exp(ldt)
    mag = jnp.exp(lr * dt)
    ang = li * dt
    ar = mag * jnp.cos(ang)
    ai = mag * jnp.sin(ang)
    den = lr * lr + li * li
    nr, ni = ar - 1.0, ai
    cr = (nr * lr + ni * li) / den
    ci = (ni * lr - nr * li) / den
    return ar, ai, cr * br_t - ci * bi_t, cr * bi_t + ci * br_t


def _disc_fwd(lr, li, ldt, br_t, bi_t):
    def body(lr_ref, li_ref, ldt_ref, br_ref, bi_ref, ar_ref, ai_ref, bbr_ref, bbi_ref):
        ar, ai, bbr, bbi = _disc_math(lr_ref[...], li_ref[...], ldt_ref[...], br_ref[...], bi_ref[...])
        ar_ref[...] = ar
        ai_ref[...] = ai
        bbr_ref[...] = bbr
        bbi_ref[...] = bbi

    shapes = [jax.ShapeDtypeStruct(a.shape, F32) for a in (lr, li, br_t, bi_t)]
    return pl.pallas_call(body, name="ssm_disc_fwd", out_shape=shapes,
                          compiler_params=pltpu.CompilerParams(vmem_limit_bytes=VMEM_LIMIT))(lr, li, ldt, br_t, bi_t)


def _disc_bwd(lr, li, ldt, br_t, bi_t, dar, dai, dbbr, dbbi):
    def body(lr_ref, li_ref, ldt_ref, br_ref, bi_ref, dar_ref, dai_ref, dbbr_ref, dbbi_ref,
             dlr_ref, dli_ref, dldt_ref, dbr_ref, dbi_ref):
        prim = (lr_ref[...], li_ref[...], ldt_ref[...], br_ref[...], bi_ref[...])
        _, pullback = jax.vjp(_disc_math, *prim)
        dlr, dli, dldt, dbr, dbi = pullback((dar_ref[...], dai_ref[...], dbbr_ref[...], dbbi_ref[...]))
        dlr_ref[...] = dlr
        dli_ref[...] = dli
        dldt_ref[...] = dldt
        dbr_ref[...] = dbr
        dbi_ref[...] = dbi

    shapes = [jax.ShapeDtypeStruct(a.shape, F32) for a in (lr, li, ldt, br_t, bi_t)]
    return pl.pallas_call(body, name="ssm_disc_bwd", out_shape=shapes,
                          compiler_params=pltpu.CompilerParams(vmem_limit_bytes=VMEM_LIMIT))(
        lr, li, ldt, br_t, bi_t, dar, dai, dbbr, dbbi)


def _pad_pairs(m_re, m_im):
    def blocks(m):
        v = m.transpose(0, 2, 1).reshape(N_PAIRS, 2, SSM_GROUP, SSM_STATE)
        return jnp.einsum("ab,jahp->jahbp", jnp.eye(2, dtype=m.dtype), v).reshape(N_PAIRS, 32, 128)
    both = jnp.concatenate([blocks(m_re), blocks(m_im)], axis=-1)
    place = jax.nn.one_hot(jnp.arange(N_PAIRS) % 4, 4, dtype=both.dtype)
    return jnp.einsum("jk,jrc->jkrc", place, both).reshape(N_PAIRS, 128, 256)


def _unpad_pairs(x):
    place = jax.nn.one_hot(jnp.arange(N_PAIRS) % 4, 4, dtype=x.dtype)
    both = jnp.einsum("jk,jkrc->jrc", place, x.reshape(N_PAIRS, 4, 32, 256))

    def unblock(v):
        v = v.reshape(N_PAIRS, 2, SSM_GROUP, 2, SSM_STATE)
        d = jnp.einsum("ab,jahbp->jahp", jnp.eye(2, dtype=x.dtype), v)
        return d.reshape(N_SSM_GROUPS, SSM_GROUP, SSM_STATE).transpose(0, 2, 1)
    return unblock(both[..., :128]), unblock(both[..., 128:])


def _adamw_math(w, g, m, v):
    m = ADAM_B1 * m + (1.0 - ADAM_B1) * g
    v = ADAM_B2 * v + (1.0 - ADAM_B2) * (g * g)
    m_hat = m / (1.0 - ADAM_B1 ** ADAM_STEP)
    v_hat = v / (1.0 - ADAM_B2 ** ADAM_STEP)
    delta = -ADAM_LR * (m_hat / (jnp.sqrt(v_hat) + ADAM_EPS) + ADAM_WD * w)
    return delta, m, v


def _adamw(name, layer, w, m, v, gbuf, g_block, g_row0, row_tile, outs=None, after=(), glu=False):
    nl, r, c = w.shape
    n_tiles = r // row_tile
    g_rows, g_cols = g_block
    g_tile = g_rows // n_tiles
    g_off = g_row0 // g_tile
    if outs is None:
        outs = [lax.empty(w.shape, F32) for _ in range(4)]

    def body(w_ref, m_ref, v_ref, g_ref, *rest):
        go_ref, d_ref, mo_ref, vo_ref = rest[-4:]
        g = g_ref[...]
        if glu:
            g = jnp.concatenate([g[:, :D_SSM], g[:, D_SSM:]], axis=0)
        delta, mn, vn = _adamw_math(w_ref[...], g, m_ref[...], v_ref[...])
        go_ref[...] = g
        d_ref[...] = delta
        mo_ref[...] = mn
        vo_ref[...] = vn

    w_spec = pl.BlockSpec((None, row_tile, c), lambda j: (layer, j, 0))
    shape = jax.ShapeDtypeStruct(w.shape, F32)
    return pl.pallas_call(
        body, name=name, grid=(n_tiles,),
        in_specs=[w_spec, w_spec, w_spec, pl.BlockSpec((None, g_tile, g_cols), lambda j: (0, g_off + j, 0))]
        + [_ANY] * (4 + len(after)),
        out_specs=[w_spec] * 4,
        out_shape=[shape] * 4,
        input_output_aliases={4: 0, 5: 1, 6: 2, 7: 3},
        compiler_params=_cparams(1),
    )(w, m, v, gbuf, *outs, *after)


def _pack_weights(ids, layer, w_in, w_glu, w_out, w_down, w_gate_t, w_up_t, after=()):
    gb, gi = P_GLU_BLK
    ib, ii = P_IN_BLK
    ob, oi = P_OUT_BLK

    def body(ids_ref, in_ref, glu_ref, out_ref, dn_ref, gate_ref, up_ref, *rest):
        p_ref = rest[-1]
        p_ref[0:FF_SHARD, :] = dn_ref[...].astype(BF16)
        p_ref[FF_SHARD:2 * FF_SHARD, :] = gate_ref[...].astype(BF16)
        p_ref[2 * FF_SHARD:P_FF_ROWS, :] = up_ref[...].astype(BF16)
        g = glu_ref[...]
        p_ref[gb * gi:gb * (gi + 1), :] = jnp.concatenate([g[:gb, :], g[gb:, :]], axis=1).astype(BF16)
        p_ref[gb * (gi + 1):ib * ii, :] = jnp.zeros((P_GLU_PAD - gb, D_MODEL), BF16)
        p_ref[ib * ii:ib * (ii + 1), :] = in_ref[...].astype(BF16)
        p_ref[ob * oi:ob * (oi + 1), :] = out_ref[...].astype(BF16)

    def spec(a):
        return pl.BlockSpec((None,) + a.shape[1:], lambda i, ids_ref: (layer, 0, 0))

    ins = (w_in, w_glu, w_out, w_down, w_gate_t, w_up_t)
    grid_spec = pltpu.PrefetchScalarGridSpec(
        num_scalar_prefetch=1, grid=(1,),
        in_specs=[spec(a) for a in ins] + [_ANY] * len(after),
        out_specs=pl.BlockSpec((None, None, P_ROWS, D_MODEL), lambda i, ids_ref: (ids_ref[1], 0, 0, 0)))
    return pl.pallas_call(
        body, name="pack_weights", grid_spec=grid_spec,
        out_shape=jax.ShapeDtypeStruct((N_SHARD, 1, P_ROWS, D_MODEL), BF16),
        compiler_params=_cparams(1),
    )(ids, *ins, *after)


MESH = pl.DeviceIdType.MESH
_ANY = pl.BlockSpec(memory_space=pl.ANY)
P_HALF = P_ROWS // 2
RS_ROW_TILE = 352


def _mesh_pos():
    return lax.axis_index("x"), lax.axis_index("y"), lax.axis_index("c")


def _other_chips(x, y):
    return [(1 - x, y), (x, 1 - y), (1 - x, 1 - y)]


def _remote(src, dst, send_sems, recv_sems, n, to):
    return pltpu.make_async_remote_copy(src_ref=src, dst_ref=dst, send_sem=send_sems.at[n],
                                        recv_sem=recv_sems.at[n], device_id=to, device_id_type=MESH)


_HBM = pl.BlockSpec(memory_space=pltpu.HBM)
_SEM = pl.BlockSpec(memory_space=pltpu.SEMAPHORE)
_EFFECT = pltpu.CompilerParams(has_side_effects=pltpu.SideEffectType.DATAFLOW_SIDE_EFFECTING)
_TOKEN = jax.ShapeDtypeStruct((8, 128), F32)


def _in_hbm(a):
    return pltpu.with_memory_space_constraint(a, pltpu.HBM)


def _ag_start(name, wp, after):
    def body(w_ref, after_ref, send_sems, recv_sems, w_thru, token):
        x, y, c = _mesh_pos()
        mine = w_ref.at[2 * x + y, :, pl.ds(c * P_HALF, P_HALF), :]
        for j, (px, py) in enumerate(_other_chips(x, y)):
            _remote(mine, mine, send_sems, recv_sems, j, (px, py, c)).start()
        token[...] = jnp.zeros_like(token)

    return pl.pallas_call(
        body, name=name,
        out_shape=(pltpu.SemaphoreType.DMA((3,)), pltpu.SemaphoreType.DMA((3,)), pltpu.HBM(wp.shape, wp.dtype), _TOKEN),
        in_specs=(_HBM, _ANY), out_specs=(_SEM, _SEM, _HBM, pl.BlockSpec(memory_space=pltpu.VMEM)),
        input_output_aliases={0: 2}, compiler_params=_EFFECT,
    )(_in_hbm(wp), after)


def _ag_wait(name, send_sems, recv_sems, wp, after):
    def body(w_ref, send_sems, recv_sems, *rest):
        x, y, c = _mesh_pos()
        mine = w_ref.at[2 * x + y, :, pl.ds(c * P_HALF, P_HALF), :]
        for j, (px, py) in enumerate(_other_chips(x, y)):
            landed = w_ref.at[2 * px + py, :, pl.ds(c * P_HALF, P_HALF), :]
            cp = _remote(mine, landed, send_sems, recv_sems, j, (px, py, c))
            cp.wait_send()
            cp.wait_recv()

    return pl.pallas_call(
        body, name=name, out_shape=pltpu.HBM(wp.shape, wp.dtype),
        in_specs=(_HBM, _SEM, _SEM) + (_ANY,) * len(after), out_specs=_HBM,
        input_output_aliases={0: 0}, compiler_params=_EFFECT,
    )(wp, send_sems, recv_sems, *after)


def _ag_forward(wp):
    def body(w_in, o, send_sems, recv_sems):
        x, y, c = _mesh_pos()
        sib = (x, y, 1 - c)
        chips = _other_chips(x, y)
        sends = []
        for j, (px, py) in enumerate(chips):
            landed = o.at[2 * px + py, :, pl.ds(c * P_HALF, P_HALF), :]
            cp = _remote(landed, landed, send_sems, recv_sems, j, sib)
            cp.start()
            sends.append(cp)
        for j, (px, py) in enumerate(chips):
            passed = o.at[2 * px + py, :, pl.ds((1 - c) * P_HALF, P_HALF), :]
            _remote(passed, passed, send_sems, recv_sems, j, sib).wait_recv()
        for cp in sends:
            cp.wait_send()

    return pl.pallas_call(
        body, name="ag_forward",
        in_specs=[_ANY], out_specs=_ANY,
        out_shape=jax.ShapeDtypeStruct(wp.shape, wp.dtype),
        scratch_shapes=[pltpu.SemaphoreType.DMA((3,)), pltpu.SemaphoreType.DMA((3,))],
        input_output_aliases={0: 0},
    )(wp)


def _rs_chips_start(name, t):
    nl = t.shape[0]

    def body(t_ref, land_ref, send_sems, recv_sems, t_thru, land_thru, token):
        x, y, c = _mesh_pos()
        for j, (px, py) in enumerate(_other_chips(x, y)):
            _remote(t_ref.at[:, 2 * px + py], land_ref.at[j], send_sems, recv_sems, j, (px, py, c)).start()
        token[...] = jnp.zeros_like(token)

    land = lax.empty((3, nl, P_HALF, D_MODEL), BF16)
    return pl.pallas_call(
        body, name=name,
        out_shape=(pltpu.SemaphoreType.DMA((3,)), pltpu.SemaphoreType.DMA((3,)), pltpu.HBM(t.shape, t.dtype),
                   pltpu.HBM(land.shape, land.dtype), _TOKEN),
        in_specs=(_HBM, _HBM), out_specs=(_SEM, _SEM, _HBM, _HBM, pl.BlockSpec(memory_space=pltpu.VMEM)),
        input_output_aliases={0: 2, 1: 3}, compiler_params=_EFFECT,
    )(_in_hbm(t), _in_hbm(land))


def _rs_chips_wait(name, send_sems, recv_sems, t, land, after):
    def body(t_ref, land_ref, send_sems, recv_sems, *rest):
        x, y, c = _mesh_pos()
        for j, (px, py) in enumerate(_other_chips(x, y)):
            cp = _remote(t_ref.at[:, 2 * px + py], land_ref.at[j], send_sems, recv_sems, j, (px, py, c))
            cp.wait_send()
            cp.wait_recv()

    return pl.pallas_call(
        body, name=name, out_shape=(pltpu.HBM(t.shape, t.dtype), pltpu.HBM(land.shape, land.dtype)),
        in_specs=(_HBM, _HBM, _SEM, _SEM) + (_ANY,) * len(after), out_specs=(_HBM, _HBM),
        input_output_aliases={0: 0, 1: 1}, compiler_params=_EFFECT,
    )(t, land, send_sems, recv_sems, *after)[1]


def _rs_sibling_start(name, g):
    nl = g.shape[0]

    def body(g_ref, land_ref, send_sems, recv_sems, g_thru, land_thru, token):
        x, y, c = _mesh_pos()
        _remote(g_ref.at[:, :, pl.ds((1 - c) * P_HALF, P_HALF), :], land_ref, send_sems, recv_sems, 0,
                (x, y, 1 - c)).start()
        token[...] = jnp.zeros_like(token)

    land = lax.empty((nl, N_SHARD, P_HALF, D_MODEL), F32)
    return pl.pallas_call(
        body, name=name,
        out_shape=(pltpu.SemaphoreType.DMA((1,)), pltpu.SemaphoreType.DMA((1,)), pltpu.HBM(g.shape, g.dtype),
                   pltpu.HBM(land.shape, land.dtype), _TOKEN),
        in_specs=(_HBM, _HBM), out_specs=(_SEM, _SEM, _HBM, _HBM, pl.BlockSpec(memory_space=pltpu.VMEM)),
        input_output_aliases={0: 2, 1: 3}, compiler_params=_EFFECT,
    )(_in_hbm(g), _in_hbm(land))


def _rs_sibling_wait(name, send_sems, recv_sems, g, land, after):
    def body(g_ref, land_ref, send_sems, recv_sems, *rest):
        x, y, c = _mesh_pos()
        cp = _remote(g_ref.at[:, :, pl.ds((1 - c) * P_HALF, P_HALF), :], land_ref, send_sems, recv_sems, 0,
                     (x, y, 1 - c))
        cp.wait_send()
        cp.wait_recv()

    return pl.pallas_call(
        body, name=name, out_shape=(pltpu.HBM(g.shape, g.dtype), pltpu.HBM(land.shape, land.dtype)),
        in_specs=(_HBM, _HBM, _SEM, _SEM) + (_ANY,) * len(after), out_specs=(_HBM, _HBM),
        input_output_aliases={0: 0, 1: 1}, compiler_params=_EFFECT,
    )(g, land, send_sems, recv_sems, *after)


def _rs_add(name, ids, g, buf, row_tile):
    nl, _, hr, cols = buf.shape
    n_rt = hr // row_tile

    def body(ids_ref, g_ref, b_ref, own_ref, tb_ref):
        t = g_ref[...] + b_ref[...]
        tb_ref[...] = t.astype(BF16)

        @pl.when(pl.program_id(2) == ids_ref[1])
        def _():
            own_ref[...] = t

    blk = (None, None, row_tile, cols)
    grid_spec = pltpu.PrefetchScalarGridSpec(
        num_scalar_prefetch=1, grid=(nl, n_rt, N_SHARD),
        in_specs=[pl.BlockSpec(blk, lambda l, j, s, ids_ref: (l, s, ids_ref[0] * n_rt + j, 0)),
                  pl.BlockSpec(blk, lambda l, j, s, ids_ref: (l, s, j, 0))],
        out_specs=[pl.BlockSpec((None, row_tile, cols), lambda l, j, s, ids_ref: (l, j, 0)),
                   pl.BlockSpec(blk, lambda l, j, s, ids_ref: (l, s, j, 0))])
    return pl.pallas_call(
        body, name=name, grid_spec=grid_spec,
        out_shape=[jax.ShapeDtypeStruct((nl, hr, cols), F32), jax.ShapeDtypeStruct(buf.shape, BF16)],
        compiler_params=_cparams(3),
    )(ids, g, buf)


def _rs_sum(ids, layer, own, bufb, reduced, row_tile):
    _, hr, cols = own.shape
    n_rt = hr // row_tile

    def body(ids_ref, own_ref, b_ref, reduced_in, f_ref):
        f_ref[...] = ((own_ref[...] + b_ref[0].astype(F32)) + b_ref[1].astype(F32)) + b_ref[2].astype(F32)

    grid_spec = pltpu.PrefetchScalarGridSpec(
        num_scalar_prefetch=1, grid=(n_rt,),
        in_specs=[pl.BlockSpec((None, row_tile, cols), lambda j, ids_ref: (0, j, 0)),
                  pl.BlockSpec((3, None, row_tile, cols), lambda j, ids_ref: (0, 0, j, 0)),
                  pl.BlockSpec(memory_space=pl.ANY)],
        out_specs=pl.BlockSpec((None, row_tile, cols), lambda j, ids_ref: (layer, ids_ref[0] * n_rt + j, 0)))
    return pl.pallas_call(
        body, name="rs_sum", grid_spec=grid_spec,
        out_shape=jax.ShapeDtypeStruct(reduced.shape, F32),
        input_output_aliases={3: 0},
        compiler_params=_cparams(1),
    )(ids, own, bufb, reduced)


def _rs_exchange_start(name, f):
    def body(f_ref, send_sems, recv_sems, f_thru):
        x, y, c = _mesh_pos()
        mine = f_ref.at[:, pl.ds(c * P_HALF, P_HALF), :]
        _remote(mine, mine, send_sems, recv_sems, 0, (x, y, 1 - c)).start()

    return pl.pallas_call(
        body, name=name,
        out_shape=(pltpu.SemaphoreType.DMA((1,)), pltpu.SemaphoreType.DMA((1,)), pltpu.HBM(f.shape, f.dtype)),
        in_specs=(_HBM,), out_specs=(_SEM, _SEM, _HBM),
        input_output_aliases={0: 2}, compiler_params=_EFFECT,
    )(_in_hbm(f))


def _rs_exchange_wait(name, send_sems, recv_sems, f, after):
    def body(f_ref, send_sems, recv_sems, *rest):
        x, y, c = _mesh_pos()
        mine = f_ref.at[:, pl.ds(c * P_HALF, P_HALF), :]
        theirs = f_ref.at[:, pl.ds((1 - c) * P_HALF, P_HALF), :]
        cp = _remote(mine, theirs, send_sems, recv_sems, 0, (x, y, 1 - c))
        cp.wait_send()
        cp.wait_recv()

    return pl.pallas_call(
        body, name=name, out_shape=pltpu.HBM(f.shape, f.dtype),
        in_specs=(_HBM, _SEM, _SEM) + (_ANY,) * len(after), out_specs=_HBM,
        input_output_aliases={0: 0}, compiler_params=_EFFECT,
    )(f, send_sems, recv_sems, *after)


def _small_all_reduce(s, after=()):
    n_rows = s.shape[0]
    hr = n_rows // 2
    qr = hr // N_SHARD

    def body(s_ref, *rest):
        o_ref, sibbuf, tbuf, qbuf, fbuf, send_sems, recv_sems = rest[len(after):]
        x, y, c = _mesh_pos()
        k = 2 * x + y
        sib = (x, y, 1 - c)
        chips = _other_chips(x, y)
        mine = pl.ds(pl.multiple_of(c * hr, SUBLANES), hr)
        theirs = pl.ds(pl.multiple_of((1 - c) * hr, SUBLANES), hr)

        def quarter(shard):
            return pl.ds(pl.multiple_of(shard * qr, SUBLANES), qr)

        first = _remote(s_ref.at[theirs], sibbuf, send_sems, recv_sems, 0, sib)
        first.start()
        first.wait()
        tbuf[...] = s_ref[mine, :] + sibbuf[...]
        cps = []
        for j, (px, py) in enumerate(chips):
            cp = _remote(tbuf.at[quarter(2 * px + py)], qbuf.at[j], send_sems, recv_sems, 1 + j, (px, py, c))
            cp.start()
            cps.append(cp)
        for cp in cps:
            cp.wait()
        fbuf[quarter(k), :] = (tbuf[quarter(k), :] + qbuf[1]) + (qbuf[0] + qbuf[2])
        cps = []
        for j, (px, py) in enumerate(chips):
            cp = _remote(fbuf.at[quarter(k)], fbuf.at[quarter(k)], send_sems, recv_sems, 4 + j, (px, py, c))
            cp.start()
            cps.append(cp)
        for j, (px, py) in enumerate(chips):
            got = fbuf.at[quarter(2 * px + py)]
            _remote(got, got, send_sems, recv_sems, 4 + j, (px, py, c)).wait_recv()
        for cp in cps:
            cp.wait_send()
        o_ref[mine, :] = fbuf[...]
        last = _remote(fbuf, o_ref.at[mine], send_sems, recv_sems, 7, sib)
        last.start()
        last.wait()

    vmem = pl.BlockSpec(memory_space=pltpu.VMEM)
    return pl.pallas_call(
        body, name="small_all_reduce",
        in_specs=[vmem] + [_ANY] * len(after), out_specs=vmem,
        out_shape=jax.ShapeDtypeStruct(s.shape, F32),
        scratch_shapes=[pltpu.VMEM((hr, D_MODEL), F32), pltpu.VMEM((hr, D_MODEL), F32),
                        pltpu.VMEM((3, qr, D_MODEL), F32), pltpu.VMEM((hr, D_MODEL), F32),
                        pltpu.SemaphoreType.DMA((8,)), pltpu.SemaphoreType.DMA((8,))],
        compiler_params=pltpu.CompilerParams(vmem_limit_bytes=VMEM_LIMIT),
    )(s, *after)


_SMALL = ("norm_mix", "w_pool", "pool_scale", "lam_re", "lam_im", "log_dt", "b_re", "b_im", "c_re", "c_im",
          "d_skip", "b_glu", "norm_ffn", "norm_final")
_WEIGHTS = ("norm_mix", "w_in", "w_pool", "pool_scale", "lam_re", "lam_im", "log_dt", "b_re", "b_im", "c_re",
            "c_im", "d_skip", "w_glu", "b_glu", "w_out", "norm_ffn", "w_gate", "w_up", "w_down", "norm_final")


def _local_step(x, target, p, get_weights, ffn_bwd_done, put_grads):
    nl = p["norm_mix"].shape[0]

    def tied(a, token):
        return a if token is None else a + token
    n_rows = nl * N_SSM_GROUPS
    lr = p["lam_re"].reshape(n_rows, 1, SSM_STATE)
    li = p["lam_im"].reshape(n_rows, 1, SSM_STATE)
    ldt = p["log_dt"].reshape(n_rows, 1, 1)
    br_t = p["b_re"].reshape(n_rows, SSM_STATE, SSM_GROUP).transpose(0, 2, 1)
    bi_t = p["b_im"].reshape(n_rows, SSM_STATE, SSM_GROUP).transpose(0, 2, 1)
    ar, ai, bbr_t, bbi_t = _disc_fwd(lr, li, ldt, br_t, bi_t)
    ar = ar.reshape(nl, 1, N_STATE)
    ai = ai.reshape(nl, 1, N_STATE)
    bbr = bbr_t.transpose(0, 2, 1).reshape(nl, N_SSM_GROUPS, SSM_STATE, SSM_GROUP)
    bbi = bbi_t.transpose(0, 2, 1).reshape(nl, N_SSM_GROUPS, SSM_STATE, SSM_GROUP)
    w_pool = p["w_pool"].astype(BF16)
    p = dict(p)
    for n in ("norm_mix", "pool_scale", "b_glu", "norm_ffn"):
        p[n] = p[n].reshape(nl, 1, -1)
    swap = lambda a: jnp.swapaxes(a, -1, -2)
    bpad = jax.vmap(_pad_pairs)(bbr, bbi).astype(BF16)
    cpad_t = jax.vmap(_pad_pairs)(swap(p["c_re"]), -swap(p["c_im"])).astype(BF16)
    bpad_t, cpad = swap(bpad), swap(cpad_t)
    dskip = p["d_skip"].reshape(nl, 1, D_SSM)

    layers = []
    h = x
    for l in range(nl):
        wp = get_weights(l, [h] if l else [h, bpad, cpad, bpad_t, cpad_t, ar, ai])
        u, ypool = _mix_in_fwd(h, p["norm_mix"], wp, l, w_pool, p["pool_scale"])
        sre, sim, yraw = _ssm_fwd(u, l, bpad, cpad, ar, ai, dskip)
        hm = _mix_out_fwd(yraw, ypool, h, wp, l, p["b_glu"])
        h_next, n2, act_s, fgate_s, fup_s = _ffn_fwd(hm, p["norm_ffn"], wp, l)
        layers.append(dict(h=h, u=u, ypool=ypool, sre=sre, sim=sim, yraw=yraw, hm=hm, n2=n2, act_s=act_s, wp=wp,
                           fgate_s=fgate_s, fup_s=fup_s))
        h = h_next

    dh, loss, d_norm_final = _final_fwd_bwd(h, p["norm_final"].reshape(1, D_MODEL), target)

    raw = {n: [None] * nl for n in ("dg1", "dwp", "dsc", "dcp", "dbp", "ddsk", "db_glu", "dg2", "dar", "dai")}
    token = None
    for l in reversed(range(nl)):
        s = layers[l]
        wp = s["wp"]
        g1 = lax.empty((1, N_SHARD, P_ROWS, D_MODEL), F32)
        dhm, dg2, dgate_s, dup_s, dhb = _ffn_bwd_act(dh, s["hm"], tied(p["norm_ffn"], token), s["fgate_s"],
                                                      s["fup_s"], wp, l)
        g1 = _ffn_bwd_w(s["n2"], dgate_s, dup_s, s["act_s"], dhb, g1)
        token = ffn_bwd_done(l, [g1])
        dyraw, dyp, db_glu, g1 = _mix_out_bwd(dhm, s["yraw"], s["ypool"], wp, l, tied(p["b_glu"], token), g1)
        dus, dcp, dbp, dar, dai, ddsk = _ssm_bwd(dyraw, s["u"], s["sre"], s["sim"], l, cpad_t, bpad_t, ar, ai, dskip)
        dup, dwp, dsc = _pool_bwd(dyp, s["u"], l, w_pool, p["pool_scale"])
        dh, dg1, g1 = _mix_in_bwd(dup, dus, s["h"], dhm, p["norm_mix"], wp, l, g1)
        token = put_grads(l, g1)
        for n, a in (("dg1", dg1), ("dwp", dwp), ("dsc", dsc), ("dcp", dcp), ("dbp", dbp), ("ddsk", ddsk),
                     ("db_glu", db_glu), ("dg2", dg2), ("dar", dar), ("dai", dai)):
            raw[n][l] = a

    st = {n: jnp.stack(v) for n, v in raw.items()}
    dc_re, dc_im = jax.vmap(_unpad_pairs)(swap(st["dcp"]))
    dbbr, dbbi = jax.vmap(_unpad_pairs)(st["dbp"])
    rows = lambda a: a.reshape((n_rows,) + a.shape[2:])
    dlr, dli, dldt, dbr_t, dbi_t = _disc_bwd(lr, li, ldt, br_t, bi_t, st["dar"].reshape(n_rows, 1, SSM_STATE),
                                              st["dai"].reshape(n_rows, 1, SSM_STATE), rows(swap(dbbr)),
                                              rows(swap(dbbi)))
    small = {"norm_mix": st["dg1"][:, 0], "w_pool": st["dwp"], "pool_scale": st["dsc"][:, 0], "c_re": swap(dc_re),
             "c_im": -swap(dc_im), "d_skip": st["ddsk"].reshape(nl, N_SSM_GROUPS, SSM_GROUP),
             "b_glu": st["db_glu"][:, 0], "norm_ffn": st["dg2"][:, 0]}
    small["lam_re"] = dlr.reshape(nl, N_SSM_GROUPS, SSM_STATE)
    small["lam_im"] = dli.reshape(nl, N_SSM_GROUPS, SSM_STATE)
    small["log_dt"] = dldt.reshape(nl, N_SSM_GROUPS)
    small["b_re"] = dbr_t.reshape(nl, N_SSM_GROUPS, SSM_GROUP, SSM_STATE)
    small["b_im"] = dbi_t.reshape(nl, N_SSM_GROUPS, SSM_GROUP, SSM_STATE)
    small["d_skip"] = small["d_skip"].transpose(_SMALL_VIEW["d_skip"])
    small["norm_final"] = d_norm_final
    return loss, dh, small


_SMALL_VIEW = {"b_re": (0, 1, 3, 2), "b_im": (0, 1, 3, 2), "d_skip": (0, 2, 1)}
_SMALL_GROUPS = (("b_re", "b_im"), ("c_re", "c_im"), ("lam_re", "lam_im"), ("norm_mix", "norm_ffn"),
                 ("pool_scale", "b_glu"), ("w_pool",), ("log_dt",), ("d_skip",), ("norm_final",))


def _view(n, a):
    a = a.transpose(_SMALL_VIEW[n]) if n in _SMALL_VIEW else a
    return a[None] if a.ndim == 1 else a


def _unview(n, a, shape):
    a = a.reshape(shape) if len(shape) == 1 else a
    return a.transpose(_SMALL_VIEW[n]) if n in _SMALL_VIEW else a


def _flatten_small(views):
    flat = jnp.concatenate([views[n].reshape(-1) for n in _SMALL])
    n_rows = -(-flat.shape[0] // (64 * D_MODEL)) * 64
    return jnp.pad(flat, (0, n_rows * D_MODEL - flat.shape[0])).reshape(n_rows, D_MODEL)


def _split_small(flat, like):
    flat = flat.reshape(-1)
    out, at = {}, 0
    for n in _SMALL:
        size = like[n].size
        out[n] = flat[at:at + size].reshape(like[n].shape)
        at += size
    return out


def _adamw_small(name, ws, ms, vs, gs):
    k = len(ws)

    def body(*refs):
        ins, outs = refs[:4 * k], refs[4 * k:]
        for i in range(k):
            w, m, v, g = (ins[j * k + i][...] for j in range(4))
            delta, mn, vn = _adamw_math(w, g, m, v)
            outs[i][...] = delta
            outs[k + i][...] = mn
            outs[2 * k + i][...] = vn

    shapes = [jax.ShapeDtypeStruct(w.shape, F32) for w in ws] * 3
    outs = pl.pallas_call(body, name=name, out_shape=shapes,
                          compiler_params=pltpu.CompilerParams(vmem_limit_bytes=VMEM_LIMIT))(*ws, *ms, *vs, *gs)
    return outs[:k], outs[k:2 * k], outs[2 * k:]


def kernel(x, norm_mix, w_in, w_pool, pool_scale, lam_re, lam_im, log_dt, b_re, b_im, c_re, c_im, d_skip, w_glu, b_glu, w_out, norm_ffn, w_gate, w_up, w_down, norm_final, loss_target, m_norm_mix, m_w_in, m_w_pool, m_pool_scale, m_lam_re, m_lam_im, m_log_dt, m_b_re, m_b_im, m_c_re, m_c_im, m_d_skip, m_w_glu, m_b_glu, m_w_out, m_norm_ffn, m_w_gate, m_w_up, m_w_down, m_norm_final, v_norm_mix, v_w_in, v_w_pool, v_pool_scale, v_lam_re, v_lam_im, v_log_dt, v_b_re, v_b_im, v_c_re, v_c_im, v_d_skip, v_w_glu, v_b_glu, v_w_out, v_norm_ffn, v_w_gate, v_w_up, v_w_down, v_norm_final):
    given = dict(locals())
    w = {n: given[n] for n in _WEIGHTS}
    m = {n: given["m_" + n] for n in _WEIGHTS}
    v = {n: given["v_" + n] for n in _WEIGHTS}
    ids = jnp.stack([lax.axis_index("c"), 2 * lax.axis_index("x") + lax.axis_index("y")]).astype(jnp.int32)

    t_names = ("w_gate", "w_up")
    tr = lambda a: a.transpose(0, 2, 1)
    for d in (w, m, v):
        d.update({n: tr(d[n]) for n in t_names})

    nl = norm_mix.shape[0]
    started, last = {}, None
    for l in range(nl):
        packed = _pack_weights(ids, l, w["w_in"], w["w_glu"], w["w_out"], w["w_down"], w["w_gate"], w["w_up"],
                               [] if last is None else [last])
        started[l] = _ag_start(f"ag_start_{l}", packed, ids if last is None else last)
        last = started[l][3]
    views = [{n: _view(n, d[n]) for n in _SMALL} for d in (w, m, v)]

    def get_weights(l, after):
        send_sems, recv_sems, buf, _ = started[l]
        after = after + ([last] if l == 0 else [])
        return _ag_forward(_ag_wait(f"ag_wait_{l}", send_sems, recv_sems, buf, after))

    to_sibling, to_chips, reduced = {}, {}, {}

    def put_grads(l, g):
        to_sibling[l] = _rs_sibling_start(f"rs_sibling_start_{l}", g)
        token = to_sibling[l][4]
        if l + 1 in to_chips:
            finish(l + 1, [token])
        return token[:1, :1]

    def ffn_bwd_done(l, after):
        return send_to_chips(l + 1, after)[:1, :1] if l + 1 in to_sibling else None

    def send_to_chips(l, after):
        send_sems, recv_sems, g, land, _ = to_sibling.pop(l)
        g, land = _rs_sibling_wait(f"rs_sibling_wait_{l}", send_sems, recv_sems, g, land, after)
        own, t = _rs_add("rs_add", ids, g, land, RS_ROW_TILE)
        send_sems, recv_sems, t, land, token = _rs_chips_start(f"rs_chips_start_{l}", t)
        to_chips[l] = (send_sems, recv_sems, t, land, own)
        return token

    def finish(l, after):
        send_sems, recv_sems, t, land, own = to_chips.pop(l)
        land = _rs_chips_wait(f"rs_chips_wait_{l}", send_sems, recv_sems, t, land, after)
        shard = lax.empty((1, P_ROWS, D_MODEL), F32)
        reduced[l] = _rs_exchange_start(f"rs_exchange_start_{l}", _rs_sum(ids, 0, own, land, shard, RS_ROW_TILE))

    loss, grad_x, small = _local_step(x[0], loss_target[0], {n: w[n] for n in _SMALL}, get_weights, ffn_bwd_done,
                                      put_grads)
    loss = lax.psum(loss[0, 0], ("x", "y", "c"))
    small_flat = _flatten_small(small)
    token = send_to_chips(0, [small_flat])

    big = (("w_in", P_IN_BLK, 256, False), ("w_out", P_OUT_BLK, 256, False), ("w_down", P_WD_BLK, 352, False),
           ("w_gate", P_WG_BLK, 352, False), ("w_up", P_WU_BLK, 352, False), ("w_glu", P_GLU_BLK, 128, True))
    res = {n: None for n, *_ in big}

    def adamw_layer(l, after):
        send_sems, recv_sems, shard = reduced[l]
        shard = _rs_exchange_wait(f"rs_exchange_wait_{l}", send_sems, recv_sems, shard, after)
        for n, (blk, idx), row_tile, glu in big:
            res[n] = _adamw("adamw_" + n, l, w[n], m[n], v[n], shard, (blk, D_MODEL), blk * idx, row_tile, res[n], (), glu)

    for l in reversed(range(1, nl)):
        adamw_layer(l, [token])
    updated = [r[0] for r in res.values() if r is not None]
    small_sum = _small_all_reduce(small_flat, [token] + updated)
    finish(0, [small_sum] + updated)
    adamw_layer(0, [])
    for n in t_names:
        res[n] = tuple(tr(a) for a in res[n])
    g_views = _split_small(small_sum, views[0])
    for group in _SMALL_GROUPS:
        deltas, new_ms, new_vs = _adamw_small("adamw_" + group[0], *[[d[n] for n in group] for d in views],
                                              [g_views[n] for n in group])
        for i, n in enumerate(group):
            res[n] = tuple(_unview(n, a, w[n].shape) for a in (g_views[n], deltas[i], new_ms[i], new_vs[i]))

    return (loss, grad_x[None], *[res[n][0] for n in _WEIGHTS], *[res[n][1] for n in _WEIGHTS],
            *[res[n][2] for n in _WEIGHTS], *[res[n][3] for n in _WEIGHTS])
```

```python
import functools
import math

import jax
import jax.numpy as jnp
from jax import lax
from jax.experimental import pallas as pl
from jax.experimental.pallas import tpu as pltpu

F32 = jnp.float32
BF16 = jnp.bfloat16

D_MODEL = 1024
D_POOL = 512
D_SSM = 512
POOL_WINDOWS = (2, 4, 8, 16)
POOL_GROUP = 128
POOL_HALO = 16
N_SSM_GROUPS = 32
SSM_GROUP = 16
SSM_STATE = 64
N_STATE = N_SSM_GROUPS * SSM_STATE
N_PAIRS = N_SSM_GROUPS // 2
D_FF = 2816
N_SHARD = 4
FF_SHARD = D_FF // N_SHARD
RMS_EPS = 1e-6

ADAM_LR = 0.001
ADAM_B1 = 0.9
ADAM_B2 = 0.999
ADAM_EPS = 1e-08
ADAM_WD = 0.01
ADAM_STEP = 10

P_ROWS = 2816
P_WD_BLK = (704, 0)
P_WG_BLK = (704, 1)
P_WU_BLK = (704, 2)
P_FF_ROWS = 2112
P_GLU_BLK = (64, 33)
P_GLU_PAD = 192
P_IN_BLK = (256, 9)
P_OUT_BLK = (256, 10)

SUBLANES = 8
VMEM_LIMIT = 56 * 1024 * 1024

TM = 1024
TM_FFN = 512
TM_FFN_LONG = 1024
FFN_SPLIT = 2
TS = 1024
SCAN_LANES = 512


def _cparams(n_axes):
    return pltpu.CompilerParams(dimension_semantics=("arbitrary",) * n_axes, vmem_limit_bytes=VMEM_LIMIT)


def _dot(a, b):
    return jnp.dot(a, b, preferred_element_type=F32)


def _dot_nt(a, b):
    return lax.dot_general(a, b, (((1,), (1,)), ((), ())), preferred_element_type=F32)


def _dot_tn(a, b):
    return lax.dot_general(a, b, (((0,), (0,)), ((), ())), preferred_element_type=F32)


def _rms_hat(x):
    r = lax.rsqrt(jnp.mean(x * x, axis=-1, keepdims=True) + RMS_EPS)
    return x * r, r


def _rms_bwd(d_hat, xhat, r):
    return r * (d_hat - xhat * jnp.mean(d_hat * xhat, axis=-1, keepdims=True))


def _sigmoid(x):
    return 1.0 / (1.0 + jnp.exp(-x))


_GELU_C = math.sqrt(2.0 / math.pi)
_GELU_K = 0.044715


def _gelu(x):
    return 0.5 * x * (1.0 + jnp.tanh(_GELU_C * (x + _GELU_K * x * x * x)))


def _gelu_grad(x):
    th = jnp.tanh(_GELU_C * (x + _GELU_K * x * x * x))
    return 0.5 * (1.0 + th) + 0.5 * x * (1.0 - th * th) * _GELU_C * (1.0 + 3.0 * _GELU_K * x * x)


def _glu_weight(ref):
    v = ref[...]
    return jnp.concatenate([v[:, :, :D_SSM], v[:, :, D_SSM:]], axis=1).reshape(D_SSM, D_SSM)


def _glu_pack(w):
    v = w.reshape(N_SHARD, 128, D_SSM)
    return jnp.concatenate([v[:, :64, :], v[:, 64:, :]], axis=2)


def _pool_diff(ext, row0, tm):
    rows = row0 + lax.broadcasted_iota(jnp.int32, (tm, 1), 0)
    outs = []
    for gi, w in enumerate(POOL_WINDOWS):
        e = ext[:, gi * POOL_GROUP:(gi + 1) * POOL_GROUP]
        s = e
        k = 1
        while k < w:
            s = s + pltpu.roll(s, k, 0)
            k *= 2
        inv = 1.0 / jnp.minimum(rows + 1, w).astype(F32)
        outs.append(s[POOL_HALO:, :] * inv - e[POOL_HALO:, :])
    return outs


def _mix_in_fwd(h, g1, wp, layer, w_pool, scale):
    L = h.shape[0]
    tm = min(TM, L)

    def body(h_ref, g_ref, w_ref, wp_ref, sc_ref, u_ref, yp_ref, carry):
        i = pl.program_id(0)

        @pl.when(i == 0)
        def _():
            carry[...] = jnp.zeros_like(carry)

        xhat, _ = _rms_hat(h_ref[...])
        n1 = (xhat * g_ref[...]).astype(BF16)
        u = _dot(n1, w_ref[...].reshape(D_MODEL, D_MODEL))
        u_ref[...] = u
        up = u[:, :D_POOL]
        ext = jnp.concatenate([carry[...], up], axis=0)
        carry[...] = up[tm - POOL_HALO:, :]
        diffs = _pool_diff(ext, i * tm, tm)
        for gi in range(4):
            cols = slice(gi * POOL_GROUP, (gi + 1) * POOL_GROUP)
            yp_ref[:, cols] = _dot(diffs[gi].astype(BF16), wp_ref[gi]) * sc_ref[:, cols]

    blk, idx = P_IN_BLK
    return pl.pallas_call(
        body, name="mix_in_fwd", grid=(L // tm,),
        in_specs=[pl.BlockSpec((tm, D_MODEL), lambda i: (i, 0)),
                  pl.BlockSpec((None, 1, D_MODEL), lambda i: (layer, 0, 0)),
                  pl.BlockSpec((N_SHARD, None, blk, D_MODEL), lambda i: (0, 0, idx, 0)),
                  pl.BlockSpec((None, 4, POOL_GROUP, POOL_GROUP), lambda i: (layer, 0, 0, 0)),
                  pl.BlockSpec((None, 1, D_POOL), lambda i: (layer, 0, 0))],
        out_specs=[pl.BlockSpec((tm, D_MODEL), lambda i: (i, 0)),
                   pl.BlockSpec((tm, D_POOL), lambda i: (i, 0))],
        out_shape=[jax.ShapeDtypeStruct((L, D_MODEL), F32), jax.ShapeDtypeStruct((L, D_POOL), F32)],
        scratch_shapes=[pltpu.VMEM((POOL_HALO, D_POOL), F32)],
        compiler_params=_cparams(1),
    )(h, g1, wp, w_pool, scale)


def _cmul(xr, xi, yr, yi):
    return xr * yr - xi * yi, xr * yi + xi * yr


SCAN_BLOCK = 64
N_SCAN_TABLES = 26


def _permute_rows(src, dst, n_rows):
    for b in range(n_rows // SCAN_BLOCK):
        for tau in range(SUBLANES):
            dst[pl.ds(SCAN_BLOCK * b + SUBLANES * tau, SUBLANES), :] = (
                src[pl.ds(SCAN_BLOCK * b + tau, SUBLANES, stride=SUBLANES), :])


def _scan_tables(ar, ai, tab, reverse):
    c = ar.shape[1]
    row = lax.broadcasted_iota(jnp.int32, (SUBLANES, c), 0)
    zero = jnp.zeros((SUBLANES, c), F32)
    full = lambda v: jnp.broadcast_to(v, (SUBLANES, c))
    pw = [(ar, ai)]
    for _ in range(SUBLANES - 1):
        pw.append(_cmul(*pw[-1], ar, ai))
    a8 = pw[-1]
    a16 = _cmul(*a8, *a8)
    a32 = _cmul(*a16, *a16)
    tab[0] = full(ar)
    tab[1] = full(ai)
    for n, (s, (pr, pi)) in enumerate(((1, a8), (2, a16), (4, a32))):
        keep = (row < SUBLANES - s) if reverse else (row >= s)
        tab[2 + 2 * n] = jnp.where(keep, pr, zero)
        tab[3 + 2 * n] = jnp.where(keep, pi, zero)
    cur = a8
    qr, qi = zero, zero
    for n in range(SUBLANES):
        at = (SUBLANES - 1 - n) if reverse else n
        qr = jnp.where(row == at, cur[0], qr)
        qi = jnp.where(row == at, cur[1], qi)
        cur = _cmul(*cur, *a8)
    tab[8] = qr
    tab[9] = qi
    for tau in range(SUBLANES):
        pr, pi = pw[SUBLANES - 1 - tau] if reverse else pw[tau]
        tab[10 + 2 * tau] = full(pr)
        tab[11 + 2 * tau] = full(pi)


def _cmac(xr, xi, ar, ai, yr, yi):
    return xr + ar * yr - ai * yi, xi + ar * yi + ai * yr


def _chain_segments(er, ei, c_r, c_i, tab, cols, reverse):
    tr, ti = er, ei
    for n, s in enumerate((1, 2, 4)):
        shift = SUBLANES - s if reverse else s
        tr, ti = _cmac(tr, ti, tab[2 + 2 * n, :, cols], tab[3 + 2 * n, :, cols],
                       pltpu.roll(tr, shift, 0), pltpu.roll(ti, shift, 0))
    return _cmac(tr, ti, tab[8, :, cols], tab[9, :, cols], c_r, c_i)


def _ssm_fwd(u, layer, bpad, cpad, ar, ai, dskip):
    L = u.shape[0]
    ts = min(TS, L)
    nq = 4
    cq = N_STATE // nq

    def body(u_ref, bp_ref, cp_ref, ar_ref, ai_ref, dsk_ref, sre_ref, sim_ref, y_ref, cr, ci, tab, up, yp):
        t = pl.program_id(1)

        @pl.when(t == 0)
        def _():
            cr[...] = jnp.zeros_like(cr)
            ci[...] = jnp.zeros_like(ci)
            _scan_tables(ar_ref[...], ai_ref[...], tab, reverse=False)

        _permute_rows(u_ref, up, ts)
        uf = up[...]
        ub = uf.astype(BF16)
        for jj in range(4):
            bu = _dot(ub, bp_ref[jj])
            sre_ref[:, jj * 128:(jj + 1) * 128] = bu[:, :128]
            sim_ref[:, jj * 128:(jj + 1) * 128] = bu[:, 128:]

        shp = (SUBLANES, SCAN_LANES)
        first_row = lax.broadcasted_iota(jnp.int32, shp, 0) == 0
        for cc in range(cq // SCAN_LANES):
            cols = slice(cc * SCAN_LANES, (cc + 1) * SCAN_LANES)

            def block(b, carry, cols=cols):
                c_r, c_i = carry
                base = pl.multiple_of(b * SCAN_BLOCK, SCAN_BLOCK)
                rows = lambda tau: pl.ds(base + SUBLANES * tau, SUBLANES)
                a_r, a_i = tab[0, :, cols], tab[1, :, cols]
                ys = [(sre_ref[rows(0), cols], sim_ref[rows(0), cols])]
                for tau in range(1, SUBLANES):
                    ys.append(_cmac(sre_ref[rows(tau), cols], sim_ref[rows(tau), cols], a_r, a_i, *ys[-1]))
                tr, ti = _chain_segments(*ys[-1], c_r, c_i, tab, cols, reverse=False)
                in_r = jnp.where(first_row, c_r, pltpu.roll(tr, 1, 0))
                in_i = jnp.where(first_row, c_i, pltpu.roll(ti, 1, 0))
                for tau in range(SUBLANES):
                    sr, si = _cmac(*ys[tau], tab[10 + 2 * tau, :, cols], tab[11 + 2 * tau, :, cols], in_r, in_i)
                    sre_ref[rows(tau), cols] = sr
                    sim_ref[rows(tau), cols] = si
                return (jnp.broadcast_to(tr[SUBLANES - 1:, :], shp), jnp.broadcast_to(ti[SUBLANES - 1:, :], shp))

            c_r, c_i = lax.fori_loop(0, ts // SCAN_BLOCK, block, (cr[:, cols], ci[:, cols]), unroll=2)
            cr[:, cols] = c_r
            ci[:, cols] = c_i

        acc = dsk_ref[...] * uf
        for jj in range(4):
            cols = slice(jj * 128, (jj + 1) * 128)
            scat = jnp.concatenate([sre_ref[:, cols], sim_ref[:, cols]], axis=1).astype(BF16)
            acc = acc + _dot(scat, cp_ref[jj])
        yp[...] = acc
        _permute_rows(yp, y_ref, ts)

    return pl.pallas_call(
        body, name="ssm_fwd", grid=(nq, L // ts),
        in_specs=[pl.BlockSpec((ts, 128), lambda q, t: (t, 4 + q)),
                  pl.BlockSpec((None, 4, 128, 256), lambda q, t: (layer, q, 0, 0)),
                  pl.BlockSpec((None, 4, 256, 128), lambda q, t: (layer, q, 0, 0)),
                  pl.BlockSpec((None, 1, cq), lambda q, t: (layer, 0, q)),
                  pl.BlockSpec((None, 1, cq), lambda q, t: (layer, 0, q)),
                  pl.BlockSpec((None, 1, 128), lambda q, t: (layer, 0, q))],
        out_specs=[pl.BlockSpec((ts, cq), lambda q, t: (t, q)),
                   pl.BlockSpec((ts, cq), lambda q, t: (t, q)),
                   pl.BlockSpec((ts, 128), lambda q, t: (t, q))],
        out_shape=[jax.ShapeDtypeStruct((L, N_STATE), F32), jax.ShapeDtypeStruct((L, N_STATE), F32),
                   jax.ShapeDtypeStruct((L, D_SSM), F32)],
        scratch_shapes=[pltpu.VMEM((SUBLANES, cq), F32), pltpu.VMEM((SUBLANES, cq), F32),
                        pltpu.VMEM((N_SCAN_TABLES, SUBLANES, cq), F32),
                        pltpu.VMEM((ts, 128), F32), pltpu.VMEM((ts, 128), F32)],
        compiler_params=_cparams(2),
    )(u, bpad, cpad, ar, ai, dskip)


def _mix_out_fwd(yraw, ypool, h, wp, layer, b_glu):
    L = h.shape[0]
    tm = min(TM, L)

    def body(yr_ref, yp_ref, h_ref, wglu_ref, b_ref, wout_ref, o_ref):
        y = _gelu(yr_ref[...])
        z = _dot(y.astype(BF16), _glu_weight(wglu_ref)) + b_ref[...]
        o = y * _sigmoid(z)
        mix = jnp.concatenate([yp_ref[...], o], axis=1).astype(BF16)
        o_ref[...] = h_ref[...] + _dot(mix, wout_ref[...].reshape(D_MODEL, D_MODEL))

    gb, gi = P_GLU_BLK
    ob, oi = P_OUT_BLK
    return pl.pallas_call(
        body, name="mix_out_fwd", grid=(L // tm,),
        in_specs=[pl.BlockSpec((tm, D_SSM), lambda i: (i, 0)),
                  pl.BlockSpec((tm, D_POOL), lambda i: (i, 0)),
                  pl.BlockSpec((tm, D_MODEL), lambda i: (i, 0)),
                  pl.BlockSpec((N_SHARD, None, gb, D_MODEL), lambda i: (0, 0, gi, 0)),
                  pl.BlockSpec((None, 1, D_SSM), lambda i: (layer, 0, 0)),
                  pl.BlockSpec((N_SHARD, None, ob, D_MODEL), lambda i: (0, 0, oi, 0))],
        out_specs=pl.BlockSpec((tm, D_MODEL), lambda i: (i, 0)),
        out_shape=jax.ShapeDtypeStruct((L, D_MODEL), F32),
        compiler_params=_cparams(1),
    )(yraw, ypool, h, wp, b_glu, wp)


def _ffn_weights(ref, k):
    return ref[k, 0:FF_SHARD, :], ref[k, FF_SHARD:2 * FF_SHARD, :], ref[k, 2 * FF_SHARD:P_FF_ROWS, :]


def _ffn_weight_spec():
    return pl.BlockSpec((N_SHARD, None, P_FF_ROWS, D_MODEL), lambda m, k: (0, 0, 0, 0),
                        pipeline_mode=pl.Buffered(1))


def _ffn_fwd(h, g2, wp, layer):
    L = h.shape[0]
    tm = min(TM_FFN_LONG, L)

    def body(h_ref, g_ref, w_ref, o_ref, n2_ref, act_ref, dgate_ref, dup_ref):
        k = pl.program_id(1)

        @pl.when(k == 0)
        def _():
            x = h_ref[...]
            xhat, _ = _rms_hat(x)
            n2_ref[...] = (xhat * g_ref[...]).astype(BF16)
            o_ref[...] = x

        wd, wg_t, wu_t = _ffn_weights(w_ref, k)
        n2 = n2_ref[...]
        gate = _dot_nt(n2, wg_t)
        up = _dot_nt(n2, wu_t)
        sg = _sigmoid(gate)
        silu = gate * sg
        act = (silu * up).astype(BF16)
        act_ref[...] = act
        dgate_ref[...] = (up * (sg * (1.0 + gate * (1.0 - sg)))).astype(BF16)
        dup_ref[...] = silu.astype(BF16)
        o_ref[...] += _dot(act, wd)

    act_shape = jax.ShapeDtypeStruct((N_SHARD, L, FF_SHARD), BF16)
    return pl.pallas_call(
        body, name="ffn_fwd", grid=(L // tm, N_SHARD),
        in_specs=[pl.BlockSpec((tm, D_MODEL), lambda m, k: (m, 0)),
                  pl.BlockSpec((None, 1, D_MODEL), lambda m, k: (layer, 0, 0)),
                  _ffn_weight_spec()],
        out_specs=[pl.BlockSpec((tm, D_MODEL), lambda m, k: (m, 0)),
                   pl.BlockSpec((tm, D_MODEL), lambda m, k: (m, 0)),
                   pl.BlockSpec((None, tm, FF_SHARD), lambda m, k: (k, m, 0)),
                   pl.BlockSpec((None, tm, FF_SHARD), lambda m, k: (k, m, 0)),
                   pl.BlockSpec((None, tm, FF_SHARD), lambda m, k: (k, m, 0))],
        out_shape=[jax.ShapeDtypeStruct((L, D_MODEL), F32), jax.ShapeDtypeStruct((L, D_MODEL), BF16),
                   act_shape, act_shape, act_shape],
        compiler_params=_cparams(2),
    )(h, g2, wp)


def _final_fwd_bwd(h, gf, target):
    L = h.shape[0]
    tm = min(TM, L)

    def body(h_ref, g_ref, t_ref, dh_ref, loss_ref, dg_ref):
        i = pl.program_id(0)

        @pl.when(i == 0)
        def _():
            loss_ref[...] = jnp.zeros_like(loss_ref)
            dg_ref[...] = jnp.zeros_like(dg_ref)

        xhat, r = _rms_hat(h_ref[...])
        g = g_ref[...]
        e = xhat * g - t_ref[...]
        loss_ref[...] += 0.5 * jnp.sum(jnp.mean(e * e, axis=-1, keepdims=True), axis=0, keepdims=True)
        dy = e * (1.0 / D_MODEL)
        dg_ref[...] += jnp.sum(dy * xhat, axis=0, keepdims=True)
        dh_ref[...] = _rms_bwd(dy * g, xhat, r)

    return pl.pallas_call(
        body, name="final_fwd_bwd", grid=(L // tm,),
        in_specs=[pl.BlockSpec((tm, D_MODEL), lambda i: (i, 0)),
                  pl.BlockSpec((1, D_MODEL), lambda i: (0, 0)),
                  pl.BlockSpec((tm, D_MODEL), lambda i: (i, 0))],
        out_specs=[pl.BlockSpec((tm, D_MODEL), lambda i: (i, 0)),
                   pl.BlockSpec((1, 1), lambda i: (0, 0)),
                   pl.BlockSpec((1, D_MODEL), lambda i: (0, 0))],
        out_shape=[jax.ShapeDtypeStruct((L, D_MODEL), F32), jax.ShapeDtypeStruct((1, 1), F32),
                   jax.ShapeDtypeStruct((1, D_MODEL), F32)],
        compiler_params=_cparams(1),
    )(h, gf, target)


def _ffn_bwd_act(dh, h, g2, fgate_s, fup_s, wp, layer):
    L = h.shape[0]
    tm = min(TM_FFN, L)
    sub = tm // FFN_SPLIT

    def body(dh_ref, h_ref, g_ref, fgate_ref, fup_ref, w_ref,
             dhm_ref, dg_ref, dgate_ref, dup_ref, dhb_ref):
        m, k = pl.program_id(0), pl.program_id(1)
        dn2 = dhm_ref

        @pl.when(jnp.logical_and(m == 0, k == 0))
        def _():
            dg_ref[...] = jnp.zeros_like(dg_ref)

        @pl.when(k == 0)
        def _():
            dhb_ref[...] = dh_ref[...].astype(BF16)
            dn2[...] = jnp.zeros_like(dn2)

        wd, wg_t, wu_t = _ffn_weights(w_ref, k)
        for rows in (slice(r * sub, (r + 1) * sub) for r in range(tm // sub)):
            dact = _dot_nt(dhb_ref[rows, :], wd)
            dgate = (dact * fgate_ref[rows, :].astype(F32)).astype(BF16)
            dup = (dact * fup_ref[rows, :].astype(F32)).astype(BF16)
            dgate_ref[rows, :] = dgate
            dup_ref[rows, :] = dup
            dn2[rows, :] += _dot(dgate, wg_t) + _dot(dup, wu_t)

        @pl.when(k == N_SHARD - 1)
        def _():
            xhat, r = _rms_hat(h_ref[...])
            d = dn2[...]
            dg_ref[...] += jnp.sum(d * xhat, axis=0, keepdims=True)
            dhm_ref[...] = dh_ref[...] + _rms_bwd(d * g_ref[...], xhat, r)

    act_spec = pl.BlockSpec((None, tm, FF_SHARD), lambda m, k: (k, m, 0))
    act_shape = jax.ShapeDtypeStruct((N_SHARD, L, FF_SHARD), BF16)
    row_spec = pl.BlockSpec((tm, D_MODEL), lambda m, k: (m, 0))
    return pl.pallas_call(
        body, name="ffn_bwd_act", grid=(L // tm, N_SHARD),
        in_specs=[row_spec, row_spec,
                  pl.BlockSpec((None, 1, D_MODEL), lambda m, k: (layer, 0, 0)),
                  act_spec, act_spec,
                  _ffn_weight_spec()],
        out_specs=[row_spec,
                   pl.BlockSpec((1, D_MODEL), lambda m, k: (0, 0)),
                   act_spec, act_spec, row_spec],
        out_shape=[jax.ShapeDtypeStruct((L, D_MODEL), F32), jax.ShapeDtypeStruct((1, D_MODEL), F32),
                   act_shape, act_shape, jax.ShapeDtypeStruct((L, D_MODEL), BF16)],
        compiler_params=_cparams(2),
    )(dh, h, g2, fgate_s, fup_s, wp)


def _ffn_bwd_w(n2, dgate_s, dup_s, act_s, dhb, gbuf):
    L = n2.shape[0]
    tm = min(TM_FFN_LONG, L)

    def body(n2_ref, dgate_ref, dup_ref, act_ref, dhb_ref, g_in, g_ref):
        m = pl.program_id(1)

        @pl.when(m == 0)
        def _():
            g_ref[...] = jnp.zeros_like(g_ref)

        n2v = n2_ref[...]
        g_ref[0:FF_SHARD, :] += _dot_tn(act_ref[...], dhb_ref[...])
        g_ref[FF_SHARD:2 * FF_SHARD, :] += _dot_tn(dgate_ref[...], n2v)
        g_ref[2 * FF_SHARD:P_FF_ROWS, :] += _dot_tn(dup_ref[...], n2v)

    act_spec = pl.BlockSpec((None, tm, FF_SHARD), lambda k, m: (k, m, 0))
    row_spec = pl.BlockSpec((tm, D_MODEL), lambda k, m: (m, 0))
    return pl.pallas_call(
        body, name="ffn_bwd_w", grid=(N_SHARD, L // tm),
        in_specs=[row_spec, act_spec, act_spec, act_spec, row_spec, pl.BlockSpec(memory_space=pl.ANY)],
        out_specs=pl.BlockSpec((None, None, P_FF_ROWS, D_MODEL), lambda k, m: (0, k, 0, 0)),
        out_shape=jax.ShapeDtypeStruct(gbuf.shape, F32),
        input_output_aliases={5: 0},
        compiler_params=_cparams(2),
    )(n2, dgate_s, dup_s, act_s, dhb, gbuf)


def _mix_out_bwd(dhm, yraw, ypool, wp, layer, b_glu, gbuf):
    L = dhm.shape[0]
    tm = min(TM, L)

    def body(dhm_ref, yr_ref, yp_ref, wglu_ref, b_ref, wout_ref, g1_in,
             dyr_ref, dyp_ref, db_ref, g1_ref, dwout, dwglu, gpack):
        i = pl.program_id(0)

        @pl.when(i == 0)
        def _():
            db_ref[...] = jnp.zeros_like(db_ref)
            dwout[...] = jnp.zeros_like(dwout)
            dwglu[...] = jnp.zeros_like(dwglu)

        dhb = dhm_ref[...].astype(BF16)
        wglu = _glu_weight(wglu_ref)
        dmix = _dot_nt(dhb, wout_ref[...].reshape(D_MODEL, D_MODEL))
        dyp_ref[...] = dmix[:, :D_POOL]
        d_o = dmix[:, D_POOL:]
        yraw_v = yr_ref[...]
        y = _gelu(yraw_v)
        yb = y.astype(BF16)
        sig = _sigmoid(_dot(yb, wglu) + b_ref[...])
        mix = jnp.concatenate([yp_ref[...], y * sig], axis=1).astype(BF16)
        dwout[...] += _dot_tn(mix, dhb).reshape(N_SHARD, 256, D_MODEL)
        dz = d_o * y * sig * (1.0 - sig)
        dzb = dz.astype(BF16)
        db_ref[...] += jnp.sum(dz, axis=0, keepdims=True)
        dwglu[...] += _dot_tn(yb, dzb)
        dy = d_o * sig + _dot_nt(dzb, wglu)
        dyr_ref[...] = dy * _gelu_grad(yraw_v)

        @pl.when(i == n_steps - 1)
        def _():
            gpack[:, :gb, :] = _glu_pack(dwglu[...])
            gpack[:, gb:, :] = jnp.zeros((N_SHARD, P_GLU_PAD - gb, D_MODEL), F32)
            pltpu.sync_copy(gpack, g1_ref.at[0, :, pl.ds(gb * gi, P_GLU_PAD), :])
            pltpu.sync_copy(dwout, g1_ref.at[0, :, pl.ds(ob * oi, ob), :])

    gb, gi = P_GLU_BLK
    ob, oi = P_OUT_BLK
    n_steps = L // tm
    return pl.pallas_call(
        body, name="mix_out_bwd", grid=(n_steps,),
        in_specs=[pl.BlockSpec((tm, D_MODEL), lambda i: (i, 0)),
                  pl.BlockSpec((tm, D_SSM), lambda i: (i, 0)),
                  pl.BlockSpec((tm, D_POOL), lambda i: (i, 0)),
                  pl.BlockSpec((N_SHARD, None, gb, D_MODEL), lambda i: (0, 0, gi, 0)),
                  pl.BlockSpec((None, 1, D_SSM), lambda i: (layer, 0, 0)),
                  pl.BlockSpec((N_SHARD, None, ob, D_MODEL), lambda i: (0, 0, oi, 0)),
                  pl.BlockSpec(memory_space=pl.ANY)],
        out_specs=[pl.BlockSpec((tm, D_SSM), lambda i: (i, 0)),
                   pl.BlockSpec((tm, D_POOL), lambda i: (i, 0)),
                   pl.BlockSpec((1, D_SSM), lambda i: (0, 0)),
                   pl.BlockSpec(memory_space=pl.ANY)],
        out_shape=[jax.ShapeDtypeStruct((L, D_SSM), F32), jax.ShapeDtypeStruct((L, D_POOL), F32),
                   jax.ShapeDtypeStruct((1, D_SSM), F32),
                   jax.ShapeDtypeStruct(gbuf.shape, F32)],
        scratch_shapes=[pltpu.VMEM((N_SHARD, ob, D_MODEL), F32), pltpu.VMEM((D_SSM, D_SSM), F32),
                        pltpu.VMEM((N_SHARD, P_GLU_PAD, D_MODEL), F32)],
        input_output_aliases={6: 3},
        compiler_params=_cparams(1),
    )(dhm, yraw, ypool, wp, b_glu, wp, gbuf)


def _ssm_bwd(dyraw, u, sre, sim, layer, cpad_t, bpad_t, ar, ai, dskip):
    L = u.shape[0]
    ts = min(TS, L)
    nt = L // ts
    nq = 4
    cq = N_STATE // nq

    def body(dy_ref, u_ref, sre_ref, sim_ref, ct_ref, bt_ref, ar_ref, ai_ref, dsk_ref,
             du_ref, dcp_ref, dbp_ref, dar_ref, dai_ref, ddsk_ref, gre, gim, cr, ci, tab, accr, acci, up, dyp):
        t = pl.program_id(1)

        @pl.when(t == 0)
        def _():
            for ref in (cr, ci, accr, acci, dcp_ref, dbp_ref, ddsk_ref):
                ref[...] = jnp.zeros_like(ref)
            _scan_tables(ar_ref[...], -ai_ref[...], tab, reverse=True)

        _permute_rows(dy_ref, dyp, ts)
        _permute_rows(u_ref, up, ts)
        dy = dyp[...]
        dyb = dy.astype(BF16)
        uf = up[...]
        ub = uf.astype(BF16)
        for jj in range(4):
            cols = slice(jj * 128, (jj + 1) * 128)
            ds = _dot(dyb, ct_ref[jj])
            gre[:, cols] = ds[:, :128]
            gim[:, cols] = ds[:, 128:]
            scat = jnp.concatenate([sre_ref[:, cols], sim_ref[:, cols]], axis=1).astype(BF16)
            dcp_ref[jj] += _dot_tn(scat, dyb)

        n_blk = ts // SCAN_BLOCK
        shp = (SUBLANES, SCAN_LANES)
        last_row = lax.broadcasted_iota(jnp.int32, shp, 0) == SUBLANES - 1
        for cc in range(cq // SCAN_LANES):
            cols = slice(cc * SCAN_LANES, (cc + 1) * SCAN_LANES)

            def block(i, carry, cols=cols):
                c_r, c_i, a_r, a_i = carry
                base = pl.multiple_of((n_blk - 1 - i) * SCAN_BLOCK, SCAN_BLOCK)
                rows = lambda tau: pl.ds(base + SUBLANES * tau, SUBLANES)
                m_r, m_i = tab[0, :, cols], tab[1, :, cols]
                ys = [None] * SUBLANES
                ys[SUBLANES - 1] = (gre[rows(SUBLANES - 1), cols], gim[rows(SUBLANES - 1), cols])
                for tau in reversed(range(SUBLANES - 1)):
                    ys[tau] = _cmac(gre[rows(tau), cols], gim[rows(tau), cols], m_r, m_i, *ys[tau + 1])
                tr, ti = _chain_segments(*ys[0], c_r, c_i, tab, cols, reverse=True)
                in_r = jnp.where(last_row, c_r, pltpu.roll(tr, SUBLANES - 1, 0))
                in_i = jnp.where(last_row, c_i, pltpu.roll(ti, SUBLANES - 1, 0))
                gs = [_cmac(*ys[tau], tab[10 + 2 * tau, :, cols], tab[11 + 2 * tau, :, cols], in_r, in_i)
                      for tau in range(SUBLANES)]
                for tau in range(SUBLANES):
                    gre[rows(tau), cols] = gs[tau][0]
                    gim[rows(tau), cols] = gs[tau][1]
                    if tau < SUBLANES - 1:
                        nr, ni = gs[tau + 1]
                    else:
                        nr = jnp.where(last_row, c_r, pltpu.roll(gs[0][0], SUBLANES - 1, 0))
                        ni = jnp.where(last_row, c_i, pltpu.roll(gs[0][1], SUBLANES - 1, 0))
                    sr, si = sre_ref[rows(tau), cols], sim_ref[rows(tau), cols]
                    a_r = a_r + sr * nr + si * ni
                    a_i = a_i + sr * ni - si * nr
                return (jnp.broadcast_to(tr[:1, :], shp), jnp.broadcast_to(ti[:1, :], shp), a_r, a_i)

            c_r, c_i, a_r, a_i = lax.fori_loop(
                0, n_blk, block, (cr[:, cols], ci[:, cols], accr[:, cols], acci[:, cols]), unroll=2)
            cr[:, cols] = c_r
            ci[:, cols] = c_i
            accr[:, cols] = a_r
            acci[:, cols] = a_i

        acc = dsk_ref[...] * dy
        for jj in range(4):
            cols = slice(jj * 128, (jj + 1) * 128)
            gcat = jnp.concatenate([gre[:, cols], gim[:, cols]], axis=1).astype(BF16)
            acc = acc + _dot(gcat, bt_ref[jj])
            dbp_ref[jj] += _dot_tn(ub, gcat)
        ddsk_ref[...] += jnp.sum(dy * uf, axis=0, keepdims=True)
        dyp[...] = acc
        _permute_rows(dyp, du_ref, ts)

        @pl.when(t == nt - 1)
        def _():
            dar_ref[...] = jnp.sum(accr[...], axis=0, keepdims=True)
            dai_ref[...] = jnp.sum(acci[...], axis=0, keepdims=True)

    f32_scr = lambda *s: pltpu.VMEM(s, F32)
    return pl.pallas_call(
        body, name="ssm_bwd", grid=(nq, nt),
        in_specs=[pl.BlockSpec((ts, 128), lambda q, t: (nt - 1 - t, q)),
                  pl.BlockSpec((ts, 128), lambda q, t: (nt - 1 - t, 4 + q)),
                  pl.BlockSpec((ts, cq), lambda q, t: (nt - 1 - t, q)),
                  pl.BlockSpec((ts, cq), lambda q, t: (nt - 1 - t, q)),
                  pl.BlockSpec((None, 4, 128, 256), lambda q, t: (layer, q, 0, 0)),
                  pl.BlockSpec((None, 4, 256, 128), lambda q, t: (layer, q, 0, 0)),
                  pl.BlockSpec((None, 1, cq), lambda q, t: (layer, 0, q)),
                  pl.BlockSpec((None, 1, cq), lambda q, t: (layer, 0, q)),
                  pl.BlockSpec((None, 1, 128), lambda q, t: (layer, 0, q))],
        out_specs=[pl.BlockSpec((ts, 128), lambda q, t: (nt - 1 - t, q)),
                   pl.BlockSpec((4, 256, 128), lambda q, t: (q, 0, 0)),
                   pl.BlockSpec((4, 128, 256), lambda q, t: (q, 0, 0)),
                   pl.BlockSpec((1, cq), lambda q, t: (0, q)),
                   pl.BlockSpec((1, cq), lambda q, t: (0, q)),
                   pl.BlockSpec((1, 128), lambda q, t: (0, q))],
        out_shape=[jax.ShapeDtypeStruct((L, D_SSM), F32),
                   jax.ShapeDtypeStruct((N_PAIRS, 256, 128), F32), jax.ShapeDtypeStruct((N_PAIRS, 128, 256), F32),
                   jax.ShapeDtypeStruct((1, N_STATE), F32), jax.ShapeDtypeStruct((1, N_STATE), F32),
                   jax.ShapeDtypeStruct((1, D_SSM), F32)],
        scratch_shapes=[f32_scr(ts, cq), f32_scr(ts, cq), f32_scr(SUBLANES, cq), f32_scr(SUBLANES, cq),
                        f32_scr(N_SCAN_TABLES, SUBLANES, cq), f32_scr(SUBLANES, cq), f32_scr(SUBLANES, cq),
                        f32_scr(ts, 128), f32_scr(ts, 128)],
        compiler_params=_cparams(2),
    )(dyraw, u, sre, sim, cpad_t, bpad_t, ar, ai, dskip)


def _pool_bwd(dyp, u, layer, w_pool, scale):
    L = u.shape[0]
    tm = min(TM, L)
    nt = L // tm
    halo_per_tile = tm // POOL_HALO

    def body(dyp_ref, u_ref, halo_ref, wp_ref, sc_ref, du_ref, dwp_ref, dsc_ref, carry):
        i = pl.program_id(0)
        tile = nt - 1 - i

        @pl.when(i == 0)
        def _():
            carry[...] = jnp.zeros_like(carry)
            dwp_ref[...] = jnp.zeros_like(dwp_ref)
            dsc_ref[...] = jnp.zeros_like(dsc_ref)

        up = u_ref[...]
        halo = jnp.where(tile > 0, halo_ref[...], jnp.zeros_like(halo_ref))
        diffs = _pool_diff(jnp.concatenate([halo, up], axis=0), tile * tm, tm)
        rows = tile * tm + lax.broadcasted_iota(jnp.int32, (tm, 1), 0)
        n_ext = tm + POOL_HALO
        for gi, w in enumerate(POOL_WINDOWS):
            cols = slice(gi * POOL_GROUP, (gi + 1) * POOL_GROUP)
            db = diffs[gi].astype(BF16)
            dyp = dyp_ref[:, cols]
            dsc_ref[:, cols] += jnp.sum(dyp * _dot(db, wp_ref[gi]), axis=0, keepdims=True)
            dp = (dyp * sc_ref[:, cols]).astype(BF16)
            ddiff = _dot_nt(dp, wp_ref[gi])
            dwp_ref[gi] += _dot_tn(db, dp)
            e = ddiff * (1.0 / jnp.minimum(rows + 1, w).astype(F32))
            s = jnp.concatenate([e, carry[:, cols]], axis=0)
            k = 1
            while k < w:
                s = s + pltpu.roll(s, n_ext - k, 0)
                k *= 2
            du_ref[:, cols] = s[:tm, :] - ddiff
            carry[:, cols] = e[:POOL_HALO, :]

    return pl.pallas_call(
        body, name="pool_bwd", grid=(nt,),
        in_specs=[pl.BlockSpec((tm, D_POOL), lambda i: (nt - 1 - i, 0)),
                  pl.BlockSpec((tm, D_POOL), lambda i: (nt - 1 - i, 0)),
                  pl.BlockSpec((POOL_HALO, D_POOL), lambda i: (jnp.maximum((nt - 1 - i) * halo_per_tile - 1, 0), 0)),
                  pl.BlockSpec((None, 4, POOL_GROUP, POOL_GROUP), lambda i: (layer, 0, 0, 0)),
                  pl.BlockSpec((None, 1, D_POOL), lambda i: (layer, 0, 0))],
        out_specs=[pl.BlockSpec((tm, D_POOL), lambda i: (nt - 1 - i, 0)),
                   pl.BlockSpec((4, POOL_GROUP, POOL_GROUP), lambda i: (0, 0, 0)),
                   pl.BlockSpec((1, D_POOL), lambda i: (0, 0))],
        out_shape=[jax.ShapeDtypeStruct((L, D_POOL), F32),
                   jax.ShapeDtypeStruct((4, POOL_GROUP, POOL_GROUP), F32),
                   jax.ShapeDtypeStruct((1, D_POOL), F32)],
        scratch_shapes=[pltpu.VMEM((POOL_HALO, D_POOL), F32)],
        compiler_params=_cparams(1),
    )(dyp, u, u, w_pool, scale)


def _mix_in_bwd(dup, dus, h, dhm, g1, wp, layer, gbuf):
    L = h.shape[0]
    tm = min(TM, L)
    n_steps = L // tm
    blk, idx = P_IN_BLK

    def body(dup_ref, dus_ref, h_ref, dhm_ref, g_ref, w_ref, g1_in, dh_ref, dg_ref, g1_ref, dwin):
        i = pl.program_id(0)

        @pl.when(i == 0)
        def _():
            dg_ref[...] = jnp.zeros_like(dg_ref)
            dwin[...] = jnp.zeros_like(dwin)

        du = jnp.concatenate([dup_ref[...], dus_ref[...]], axis=1).astype(BF16)
        dn1 = _dot_nt(du, w_ref[...].reshape(D_MODEL, D_MODEL))
        xhat, r = _rms_hat(h_ref[...])
        g = g_ref[...]
        n1 = (xhat * g).astype(BF16)
        dwin[...] += _dot_tn(n1, du).reshape(N_SHARD, blk, D_MODEL)
        dg_ref[...] += jnp.sum(dn1 * xhat, axis=0, keepdims=True)
        dh_ref[...] = dhm_ref[...] + _rms_bwd(dn1 * g, xhat, r)

        @pl.when(i == n_steps - 1)
        def _():
            pltpu.sync_copy(dwin, g1_ref.at[0, :, pl.ds(blk * idx, blk), :])

    row_spec = pl.BlockSpec((tm, D_MODEL), lambda i: (i, 0))
    half_spec = pl.BlockSpec((tm, D_POOL), lambda i: (i, 0))
    return pl.pallas_call(
        body, name="mix_in_bwd", grid=(n_steps,),
        in_specs=[half_spec, half_spec, row_spec, row_spec,
                  pl.BlockSpec((None, 1, D_MODEL), lambda i: (layer, 0, 0)),
                  pl.BlockSpec((N_SHARD, None, blk, D_MODEL), lambda i: (0, 0, idx, 0)),
                  pl.BlockSpec(memory_space=pl.ANY)],
        out_specs=[row_spec, pl.BlockSpec((1, D_MODEL), lambda i: (0, 0)), pl.BlockSpec(memory_space=pl.ANY)],
        out_shape=[jax.ShapeDtypeStruct((L, D_MODEL), F32), jax.ShapeDtypeStruct((1, D_MODEL), F32),
                   jax.ShapeDtypeStruct(gbuf.shape, F32)],
        scratch_shapes=[pltpu.VMEM((N_SHARD, blk, D_MODEL), F32)],
        input_output_aliases={6: 2},
        compiler_params=_cparams(1),
    )(dup, dus, h, dhm, g1, wp, gbuf)


def _disc_math(lr, li, ldt, br_t, bi_t):
    dt = jnp.exp(ldt)
    mag = jnp.exp(lr * dt)
    ang = li * dt
    ar = mag * jnp.cos(ang)
    ai = mag * jnp.sin(ang)
    den = lr * lr + li * li
    nr, ni = ar - 1.0, ai
    cr = (nr * lr + ni * li) / den
    ci = (ni * lr - nr * li) / den
    return ar, ai, cr * br_t - ci * bi_t, cr * bi_t + ci * br_t


def _disc_fwd(lr, li, ldt, br_t, bi_t):
    def body(lr_ref, li_ref, ldt_ref, br_ref, bi_ref, ar_ref, ai_ref, bbr_ref, bbi_ref):
        ar, ai, bbr, bbi = _disc_math(lr_ref[...], li_ref[...], ldt_ref[...], br_ref[...], bi_ref[...])
        ar_ref[...] = ar
        ai_ref[...] = ai
        bbr_ref[...] = bbr
        bbi_ref[...] = bbi

    shapes = [jax.ShapeDtypeStruct(a.shape, F32) for a in (lr, li, br_t, bi_t)]
    return pl.pallas_call(body, name="ssm_disc_fwd", out_shape=shapes,
                          compiler_params=pltpu.CompilerParams(vmem_limit_bytes=VMEM_LIMIT))(lr, li, ldt, br_t, bi_t)


def _disc_bwd(lr, li, ldt, br_t, bi_t, dar, dai, dbbr, dbbi):
    def body(lr_ref, li_ref, ldt_ref, br_ref, bi_ref, dar_ref, dai_ref, dbbr_ref, dbbi_ref,
             dlr_ref, dli_ref, dldt_ref, dbr_ref, dbi_ref):
        prim = (lr_ref[...], li_ref[...], ldt_ref[...], br_ref[...], bi_ref[...])
        _, pullback = jax.vjp(_disc_math, *prim)
        dlr, dli, dldt, dbr, dbi = pullback((dar_ref[...], dai_ref[...], dbbr_ref[...], dbbi_ref[...]))
        dlr_ref[...] = dlr
        dli_ref[...] = dli
        dldt_ref[...] = dldt
        dbr_ref[...] = dbr
        dbi_ref[...] = dbi

    shapes = [jax.ShapeDtypeStruct(a.shape, F32) for a in (lr, li, ldt, br_t, bi_t)]
    return pl.pallas_call(body, name="ssm_disc_bwd", out_shape=shapes,
                          compiler_params=pltpu.CompilerParams(vmem_limit_bytes=VMEM_LIMIT))(
        lr, li, ldt, br_t, bi_t, dar, dai, dbbr, dbbi)


def _pad_pairs(m_re, m_im):
    def blocks(m):
        v = m.transpose(0, 2, 1).reshape(N_PAIRS, 2, SSM_GROUP, SSM_STATE)
        return jnp.einsum("ab,jahp->jahbp", jnp.eye(2, dtype=m.dtype), v).reshape(N_PAIRS, 32, 128)
    both = jnp.concatenate([blocks(m_re), blocks(m_im)], axis=-1)
    place = jax.nn.one_hot(jnp.arange(N_PAIRS) % 4, 4, dtype=both.dtype)
    return jnp.einsum("jk,jrc->jkrc", place, both).reshape(N_PAIRS, 128, 256)


def _unpad_pairs(x):
    place = jax.nn.one_hot(jnp.arange(N_PAIRS) % 4, 4, dtype=x.dtype)
    both = jnp.einsum("jk,jkrc->jrc", place, x.reshape(N_PAIRS, 4, 32, 256))

    def unblock(v):
        v = v.reshape(N_PAIRS, 2, SSM_GROUP, 2, SSM_STATE)
        d = jnp.einsum("ab,jahbp->jahp", jnp.eye(2, dtype=x.dtype), v)
        return d.reshape(N_SSM_GROUPS, SSM_GROUP, SSM_STATE).transpose(0, 2, 1)
    return unblock(both[..., :128]), unblock(both[..., 128:])


def _adamw_math(w, g, m, v):
    m = ADAM_B1 * m + (1.0 - ADAM_B1) * g
    v = ADAM_B2 * v + (1.0 - ADAM_B2) * (g * g)
    m_hat = m / (1.0 - ADAM_B1 ** ADAM_STEP)
    v_hat = v / (1.0 - ADAM_B2 ** ADAM_STEP)
    delta = -ADAM_LR * (m_hat / (jnp.sqrt(v_hat) + ADAM_EPS) + ADAM_WD * w)
    return delta, m, v


def _adamw(name, layer, w, m, v, gbuf, g_block, g_row0, row_tile, outs=None, after=(), glu=False):
    nl, r, c = w.shape
    n_tiles = r // row_tile
    g_rows, g_cols = g_block
    g_tile = g_rows // n_tiles
    g_off = g_row0 // g_tile
    if outs is None:
        outs = [lax.empty(w.shape, F32) for _ in range(4)]

    def body(w_ref, m_ref, v_ref, g_ref, *rest):
        go_ref, d_ref, mo_ref, vo_ref = rest[-4:]
        g = g_ref[...]
        if glu:
            g = jnp.concatenate([g[:, :D_SSM], g[:, D_SSM:]], axis=0)
        delta, mn, vn = _adamw_math(w_ref[...], g, m_ref[...], v_ref[...])
        go_ref[...] = g
        d_ref[...] = delta
        mo_ref[...] = mn
        vo_ref[...] = vn

    w_spec = pl.BlockSpec((None, row_tile, c), lambda j: (layer, j, 0))
    shape = jax.ShapeDtypeStruct(w.shape, F32)
    return pl.pallas_call(
        body, name=name, grid=(n_tiles,),
        in_specs=[w_spec, w_spec, w_spec, pl.BlockSpec((None, g_tile, g_cols), lambda j: (0, g_off + j, 0))]
        + [_ANY] * (4 + len(after)),
        out_specs=[w_spec] * 4,
        out_shape=[shape] * 4,
        input_output_aliases={4: 0, 5: 1, 6: 2, 7: 3},
        compiler_params=_cparams(1),
    )(w, m, v, gbuf, *outs, *after)


def _pack_weights(ids, layer, w_in, w_glu, w_out, w_down, w_gate_t, w_up_t, after=()):
    gb, gi = P_GLU_BLK
    ib, ii = P_IN_BLK
    ob, oi = P_OUT_BLK

    def body(ids_ref, in_ref, glu_ref, out_ref, dn_ref, gate_ref, up_ref, *rest):
        p_ref = rest[-1]
        p_ref[0:FF_SHARD, :] = dn_ref[...].astype(BF16)
        p_ref[FF_SHARD:2 * FF_SHARD, :] = gate_ref[...].astype(BF16)
        p_ref[2 * FF_SHARD:P_FF_ROWS, :] = up_ref[...].astype(BF16)
        g = glu_ref[...]
        p_ref[gb * gi:gb * (gi + 1), :] = jnp.concatenate([g[:gb, :], g[gb:, :]], axis=1).astype(BF16)
        p_ref[gb * (gi + 1):ib * ii, :] = jnp.zeros((P_GLU_PAD - gb, D_MODEL), BF16)
        p_ref[ib * ii:ib * (ii + 1), :] = in_ref[...].astype(BF16)
        p_ref[ob * oi:ob * (oi + 1), :] = out_ref[...].astype(BF16)

    def spec(a):
        return pl.BlockSpec((None,) + a.shape[1:], lambda i, ids_ref: (layer, 0, 0))

    ins = (w_in, w_glu, w_out, w_down, w_gate_t, w_up_t)
    grid_spec = pltpu.PrefetchScalarGridSpec(
        num_scalar_prefetch=1, grid=(1,),
        in_specs=[spec(a) for a in ins] + [_ANY] * len(after),
        out_specs=pl.BlockSpec((None, None, P_ROWS, D_MODEL), lambda i, ids_ref: (ids_ref[1], 0, 0, 0)))
    return pl.pallas_call(
        body, name="pack_weights", grid_spec=grid_spec,
        out_shape=jax.ShapeDtypeStruct((N_SHARD, 1, P_ROWS, D_MODEL), BF16),
        compiler_params=_cparams(1),
    )(ids, *ins, *after)


MESH = pl.DeviceIdType.MESH
_ANY = pl.BlockSpec(memory_space=pl.ANY)
P_HALF = P_ROWS // 2
RS_ROW_TILE = 352


def _mesh_pos():
    return lax.axis_index("x"), lax.axis_index("y"), lax.axis_index("c")


def _other_chips(x, y):
    return [(1 - x, y), (x, 1 - y), (1 - x, 1 - y)]


def _remote(src, dst, send_sems, recv_sems, n, to):
    return pltpu.make_async_remote_copy(src_ref=src, dst_ref=dst, send_sem=send_sems.at[n],
                                        recv_sem=recv_sems.at[n], device_id=to, device_id_type=MESH)


_HBM = pl.BlockSpec(memory_space=pltpu.HBM)
_SEM = pl.BlockSpec(memory_space=pltpu.SEMAPHORE)
_EFFECT = pltpu.CompilerParams(has_side_effects=pltpu.SideEffectType.DATAFLOW_SIDE_EFFECTING)
_TOKEN = jax.ShapeDtypeStruct((8, 128), F32)


def _in_hbm(a):
    return pltpu.with_memory_space_constraint(a, pltpu.HBM)


def _ag_piece(ref, shard, half, rows):
    row0, n_rows = rows
    return ref.at[shard, :, pl.ds(row0 + half * (n_rows // 2), n_rows // 2), :]


def _ag_start(name, wp, after, rows=(0, P_ROWS)):
    def body(w_ref, after_ref, send_sems, recv_sems, w_thru, token):
        x, y, c = _mesh_pos()
        mine = _ag_piece(w_ref, 2 * x + y, c, rows)
        for j, (px, py) in enumerate(_other_chips(x, y)):
            _remote(mine, mine, send_sems, recv_sems, j, (px, py, c)).start()
        token[...] = jnp.zeros_like(token)

    return pl.pallas_call(
        body, name=name,
        out_shape=(pltpu.SemaphoreType.DMA((3,)), pltpu.SemaphoreType.DMA((3,)), pltpu.HBM(wp.shape, wp.dtype), _TOKEN),
        in_specs=(_HBM, _ANY), out_specs=(_SEM, _SEM, _HBM, pl.BlockSpec(memory_space=pltpu.VMEM)),
        input_output_aliases={0: 2}, compiler_params=_EFFECT,
    )(_in_hbm(wp), after)


def _ag_wait(name, send_sems, recv_sems, wp, after, rows=(0, P_ROWS)):
    def body(w_ref, send_sems, recv_sems, *rest):
        x, y, c = _mesh_pos()
        mine = _ag_piece(w_ref, 2 * x + y, c, rows)
        for j, (px, py) in enumerate(_other_chips(x, y)):
            landed = _ag_piece(w_ref, 2 * px + py, c, rows)
            cp = _remote(mine, landed, send_sems, recv_sems, j, (px, py, c))
            cp.wait_send()
            cp.wait_recv()

    return pl.pallas_call(
        body, name=name, out_shape=pltpu.HBM(wp.shape, wp.dtype),
        in_specs=(_HBM, _SEM, _SEM) + (_ANY,) * len(after), out_specs=_HBM,
        input_output_aliases={0: 0}, compiler_params=_EFFECT,
    )(wp, send_sems, recv_sems, *after)


def _ag_forward(wp, rows=(0, P_ROWS)):
    def body(w_in, o, send_sems, recv_sems):
        x, y, c = _mesh_pos()
        sib = (x, y, 1 - c)
        chips = _other_chips(x, y)
        sends = []
        for j, (px, py) in enumerate(chips):
            landed = _ag_piece(o, 2 * px + py, c, rows)
            cp = _remote(landed, landed, send_sems, recv_sems, j, sib)
            cp.start()
            sends.append(cp)
        for j, (px, py) in enumerate(chips):
            passed = _ag_piece(o, 2 * px + py, 1 - c, rows)
            _remote(passed, passed, send_sems, recv_sems, j, sib).wait_recv()
        for cp in sends:
            cp.wait_send()

    return pl.pallas_call(
        body, name="ag_forward",
        in_specs=[_ANY], out_specs=_ANY,
        out_shape=jax.ShapeDtypeStruct(wp.shape, wp.dtype),
        scratch_shapes=[pltpu.SemaphoreType.DMA((3,)), pltpu.SemaphoreType.DMA((3,))],
        input_output_aliases={0: 0},
    )(wp)


def _rs_chips_start(name, t):
    nl = t.shape[0]

    def body(t_ref, land_ref, send_sems, recv_sems, t_thru, land_thru, token):
        x, y, c = _mesh_pos()
        for j, (px, py) in enumerate(_other_chips(x, y)):
            _remote(t_ref.at[:, 2 * px + py], land_ref.at[j], send_sems, recv_sems, j, (px, py, c)).start()
        token[...] = jnp.zeros_like(token)

    land = lax.empty((3, nl, P_HALF, D_MODEL), BF16)
    return pl.pallas_call(
        body, name=name,
        out_shape=(pltpu.SemaphoreType.DMA((3,)), pltpu.SemaphoreType.DMA((3,)), pltpu.HBM(t.shape, t.dtype),
                   pltpu.HBM(land.shape, land.dtype), _TOKEN),
        in_specs=(_HBM, _HBM), out_specs=(_SEM, _SEM, _HBM, _HBM, pl.BlockSpec(memory_space=pltpu.VMEM)),
        input_output_aliases={0: 2, 1: 3}, compiler_params=_EFFECT,
    )(_in_hbm(t), _in_hbm(land))


def _rs_chips_wait(name, send_sems, recv_sems, t, land, after):
    def body(t_ref, land_ref, send_sems, recv_sems, *rest):
        x, y, c = _mesh_pos()
        for j, (px, py) in enumerate(_other_chips(x, y)):
            cp = _remote(t_ref.at[:, 2 * px + py], land_ref.at[j], send_sems, recv_sems, j, (px, py, c))
            cp.wait_send()
            cp.wait_recv()

    return pl.pallas_call(
        body, name=name, out_shape=(pltpu.HBM(t.shape, t.dtype), pltpu.HBM(land.shape, land.dtype)),
        in_specs=(_HBM, _HBM, _SEM, _SEM) + (_ANY,) * len(after), out_specs=(_HBM, _HBM),
        input_output_aliases={0: 0, 1: 1}, compiler_params=_EFFECT,
    )(t, land, send_sems, recv_sems, *after)[1]


def _rs_sibling_start(name, g):
    nl = g.shape[0]

    def body(g_ref, land_ref, send_sems, recv_sems, g_thru, land_thru, token):
        x, y, c = _mesh_pos()
        _remote(g_ref.at[:, :, pl.ds((1 - c) * P_HALF, P_HALF), :], land_ref, send_sems, recv_sems, 0,
                (x, y, 1 - c)).start()
        token[...] = jnp.zeros_like(token)

    land = lax.empty((nl, N_SHARD, P_HALF, D_MODEL), F32)
    return pl.pallas_call(
        body, name=name,
        out_shape=(pltpu.SemaphoreType.DMA((1,)), pltpu.SemaphoreType.DMA((1,)), pltpu.HBM(g.shape, g.dtype),
                   pltpu.HBM(land.shape, land.dtype), _TOKEN),
        in_specs=(_HBM, _HBM), out_specs=(_SEM, _SEM, _HBM, _HBM, pl.BlockSpec(memory_space=pltpu.VMEM)),
        input_output_aliases={0: 2, 1: 3}, compiler_params=_EFFECT,
    )(_in_hbm(g), _in_hbm(land))


def _rs_sibling_wait(name, send_sems, recv_sems, g, land, after):
    def body(g_ref, land_ref, send_sems, recv_sems, *rest):
        x, y, c = _mesh_pos()
        cp = _remote(g_ref.at[:, :, pl.ds((1 - c) * P_HALF, P_HALF), :], land_ref, send_sems, recv_sems, 0,
                     (x, y, 1 - c))
        cp.wait_send()
        cp.wait_recv()

    return pl.pallas_call(
        body, name=name, out_shape=(pltpu.HBM(g.shape, g.dtype), pltpu.HBM(land.shape, land.dtype)),
        in_specs=(_HBM, _HBM, _SEM, _SEM) + (_ANY,) * len(after), out_specs=(_HBM, _HBM),
        input_output_aliases={0: 0, 1: 1}, compiler_params=_EFFECT,
    )(g, land, send_sems, recv_sems, *after)


def _rs_add(name, ids, g, buf, row_tile):
    nl, _, hr, cols = buf.shape
    n_rt = hr // row_tile

    def body(ids_ref, g_ref, b_ref, own_ref, tb_ref):
        t = g_ref[...] + b_ref[...]
        tb_ref[...] = t.astype(BF16)

        @pl.when(pl.program_id(2) == ids_ref[1])
        def _():
            own_ref[...] = t

    blk = (None, None, row_tile, cols)
    grid_spec = pltpu.PrefetchScalarGridSpec(
        num_scalar_prefetch=1, grid=(nl, n_rt, N_SHARD),
        in_specs=[pl.BlockSpec(blk, lambda l, j, s, ids_ref: (l, s, ids_ref[0] * n_rt + j, 0)),
                  pl.BlockSpec(blk, lambda l, j, s, ids_ref: (l, s, j, 0))],
        out_specs=[pl.BlockSpec((None, row_tile, cols), lambda l, j, s, ids_ref: (l, j, 0)),
                   pl.BlockSpec(blk, lambda l, j, s, ids_ref: (l, s, j, 0))])
    return pl.pallas_call(
        body, name=name, grid_spec=grid_spec,
        out_shape=[jax.ShapeDtypeStruct((nl, hr, cols), F32), jax.ShapeDtypeStruct(buf.shape, BF16)],
        compiler_params=_cparams(3),
    )(ids, g, buf)


def _rs_sum(ids, layer, own, bufb, reduced, row_tile):
    _, hr, cols = own.shape
    n_rt = hr // row_tile

    def body(ids_ref, own_ref, b_ref, reduced_in, f_ref):
        f_ref[...] = ((own_ref[...] + b_ref[0].astype(F32)) + b_ref[1].astype(F32)) + b_ref[2].astype(F32)

    grid_spec = pltpu.PrefetchScalarGridSpec(
        num_scalar_prefetch=1, grid=(n_rt,),
        in_specs=[pl.BlockSpec((None, row_tile, cols), lambda j, ids_ref: (0, j, 0)),
                  pl.BlockSpec((3, None, row_tile, cols), lambda j, ids_ref: (0, 0, j, 0)),
                  pl.BlockSpec(memory_space=pl.ANY)],
        out_specs=pl.BlockSpec((None, row_tile, cols), lambda j, ids_ref: (layer, ids_ref[0] * n_rt + j, 0)))
    return pl.pallas_call(
        body, name="rs_sum", grid_spec=grid_spec,
        out_shape=jax.ShapeDtypeStruct(reduced.shape, F32),
        input_output_aliases={3: 0},
        compiler_params=_cparams(1),
    )(ids, own, bufb, reduced)


def _rs_exchange_start(name, f):
    def body(f_ref, send_sems, recv_sems, f_thru):
        x, y, c = _mesh_pos()
        mine = f_ref.at[:, pl.ds(c * P_HALF, P_HALF), :]
        _remote(mine, mine, send_sems, recv_sems, 0, (x, y, 1 - c)).start()

    return pl.pallas_call(
        body, name=name,
        out_shape=(pltpu.SemaphoreType.DMA((1,)), pltpu.SemaphoreType.DMA((1,)), pltpu.HBM(f.shape, f.dtype)),
        in_specs=(_HBM,), out_specs=(_SEM, _SEM, _HBM),
        input_output_aliases={0: 2}, compiler_params=_EFFECT,
    )(_in_hbm(f))


def _rs_exchange_wait(name, send_sems, recv_sems, f, after):
    def body(f_ref, send_sems, recv_sems, *rest):
        x, y, c = _mesh_pos()
        mine = f_ref.at[:, pl.ds(c * P_HALF, P_HALF), :]
        theirs = f_ref.at[:, pl.ds((1 - c) * P_HALF, P_HALF), :]
        cp = _remote(mine, theirs, send_sems, recv_sems, 0, (x, y, 1 - c))
        cp.wait_send()
        cp.wait_recv()

    return pl.pallas_call(
        body, name=name, out_shape=pltpu.HBM(f.shape, f.dtype),
        in_specs=(_HBM, _SEM, _SEM) + (_ANY,) * len(after), out_specs=_HBM,
        input_output_aliases={0: 0}, compiler_params=_EFFECT,
    )(f, send_sems, recv_sems, *after)


def _small_all_reduce(s, after=()):
    n_rows = s.shape[0]
    hr = n_rows // 2
    qr = hr // N_SHARD

    def body(s_ref, *rest):
        o_ref, sibbuf, tbuf, qbuf, fbuf, send_sems, recv_sems = rest[len(after):]
        x, y, c = _mesh_pos()
        k = 2 * x + y
        sib = (x, y, 1 - c)
        chips = _other_chips(x, y)
        mine = pl.ds(pl.multiple_of(c * hr, SUBLANES), hr)
        theirs = pl.ds(pl.multiple_of((1 - c) * hr, SUBLANES), hr)

        def quarter(shard):
            return pl.ds(pl.multiple_of(shard * qr, SUBLANES), qr)

        first = _remote(s_ref.at[theirs], sibbuf, send_sems, recv_sems, 0, sib)
        first.start()
        first.wait()
        tbuf[...] = s_ref[mine, :] + sibbuf[...]
        cps = []
        for j, (px, py) in enumerate(chips):
            cp = _remote(tbuf.at[quarter(2 * px + py)], qbuf.at[j], send_sems, recv_sems, 1 + j, (px, py, c))
            cp.start()
            cps.append(cp)
        for cp in cps:
            cp.wait()
        fbuf[quarter(k), :] = (tbuf[quarter(k), :] + qbuf[1]) + (qbuf[0] + qbuf[2])
        cps = []
        for j, (px, py) in enumerate(chips):
            cp = _remote(fbuf.at[quarter(k)], fbuf.at[quarter(k)], send_sems, recv_sems, 4 + j, (px, py, c))
            cp.start()
            cps.append(cp)
        for j, (px, py) in enumerate(chips):
            got = fbuf.at[quarter(2 * px + py)]
            _remote(got, got, send_sems, recv_sems, 4 + j, (px, py, c)).wait_recv()
        for cp in cps:
            cp.wait_send()
        o_ref[mine, :] = fbuf[...]
        last = _remote(fbuf, o_ref.at[mine], send_sems, recv_sems, 7, sib)
        last.start()
        last.wait()

    vmem = pl.BlockSpec(memory_space=pltpu.VMEM)
    return pl.pallas_call(
        body, name="small_all_reduce",
        in_specs=[vmem] + [_ANY] * len(after), out_specs=vmem,
        out_shape=jax.ShapeDtypeStruct(s.shape, F32),
        scratch_shapes=[pltpu.VMEM((hr, D_MODEL), F32), pltpu.VMEM((hr, D_MODEL), F32),
                        pltpu.VMEM((3, qr, D_MODEL), F32), pltpu.VMEM((hr, D_MODEL), F32),
                        pltpu.SemaphoreType.DMA((8,)), pltpu.SemaphoreType.DMA((8,))],
        compiler_params=pltpu.CompilerParams(vmem_limit_bytes=VMEM_LIMIT),
    )(s, *after)


_SMALL = ("norm_mix", "w_pool", "pool_scale", "lam_re", "lam_im", "log_dt", "b_re", "b_im", "c_re", "c_im",
          "d_skip", "b_glu", "norm_ffn", "norm_final")
_WEIGHTS = ("norm_mix", "w_in", "w_pool", "pool_scale", "lam_re", "lam_im", "log_dt", "b_re", "b_im", "c_re",
            "c_im", "d_skip", "w_glu", "b_glu", "w_out", "norm_ffn", "w_gate", "w_up", "w_down", "norm_final")


def _local_step(x, target, p, get_weights, get_ffn_weights, ffn_bwd_done, put_grads):
    nl = p["norm_mix"].shape[0]

    def tied(a, token):
        return a if token is None else a + token
    n_rows = nl * N_SSM_GROUPS
    lr = p["lam_re"].reshape(n_rows, 1, SSM_STATE)
    li = p["lam_im"].reshape(n_rows, 1, SSM_STATE)
    ldt = p["log_dt"].reshape(n_rows, 1, 1)
    br_t = p["b_re"].reshape(n_rows, SSM_STATE, SSM_GROUP).transpose(0, 2, 1)
    bi_t = p["b_im"].reshape(n_rows, SSM_STATE, SSM_GROUP).transpose(0, 2, 1)
    ar, ai, bbr_t, bbi_t = _disc_fwd(lr, li, ldt, br_t, bi_t)
    ar = ar.reshape(nl, 1, N_STATE)
    ai = ai.reshape(nl, 1, N_STATE)
    bbr = bbr_t.transpose(0, 2, 1).reshape(nl, N_SSM_GROUPS, SSM_STATE, SSM_GROUP)
    bbi = bbi_t.transpose(0, 2, 1).reshape(nl, N_SSM_GROUPS, SSM_STATE, SSM_GROUP)
    w_pool = p["w_pool"].astype(BF16)
    p = dict(p)
    for n in ("norm_mix", "pool_scale", "b_glu", "norm_ffn"):
        p[n] = p[n].reshape(nl, 1, -1)
    swap = lambda a: jnp.swapaxes(a, -1, -2)
    bpad = jax.vmap(_pad_pairs)(bbr, bbi).astype(BF16)
    cpad_t = jax.vmap(_pad_pairs)(swap(p["c_re"]), -swap(p["c_im"])).astype(BF16)
    bpad_t, cpad = swap(bpad), swap(cpad_t)
    dskip = p["d_skip"].reshape(nl, 1, D_SSM)

    layers = []
    h = x
    for l in range(nl):
        wp = get_weights(l, [h] if l else [h, bpad, cpad, bpad_t, cpad_t, ar, ai])
        u, ypool = _mix_in_fwd(h, p["norm_mix"], wp, l, w_pool, p["pool_scale"])
        sre, sim, yraw = _ssm_fwd(u, l, bpad, cpad, ar, ai, dskip)
        hm = _mix_out_fwd(yraw, ypool, h, wp, l, p["b_glu"])
        wp = get_ffn_weights(l, wp, [hm])
        h_next, n2, act_s, fgate_s, fup_s = _ffn_fwd(hm, p["norm_ffn"], wp, l)
        layers.append(dict(h=h, u=u, ypool=ypool, sre=sre, sim=sim, yraw=yraw, hm=hm, n2=n2, act_s=act_s, wp=wp,
                           fgate_s=fgate_s, fup_s=fup_s))
        h = h_next

    dh, loss, d_norm_final = _final_fwd_bwd(h, p["norm_final"].reshape(1, D_MODEL), target)

    raw = {n: [None] * nl for n in ("dg1", "dwp", "dsc", "dcp", "dbp", "ddsk", "db_glu", "dg2", "dar", "dai")}
    token = None
    for l in reversed(range(nl)):
        s = layers[l]
        wp = s["wp"]
        g1 = lax.empty((1, N_SHARD, P_ROWS, D_MODEL), F32)
        dhm, dg2, dgate_s, dup_s, dhb = _ffn_bwd_act(dh, s["hm"], tied(p["norm_ffn"], token), s["fgate_s"],
                                                      s["fup_s"], wp, l)
        g1 = _ffn_bwd_w(s["n2"], dgate_s, dup_s, s["act_s"], dhb, g1)
        token = ffn_bwd_done(l, [g1])
        dyraw, dyp, db_glu, g1 = _mix_out_bwd(dhm, s["yraw"], s["ypool"], wp, l, tied(p["b_glu"], token), g1)
        dus, dcp, dbp, dar, dai, ddsk = _ssm_bwd(dyraw, s["u"], s["sre"], s["sim"], l, cpad_t, bpad_t, ar, ai, dskip)
        dup, dwp, dsc = _pool_bwd(dyp, s["u"], l, w_pool, p["pool_scale"])
        dh, dg1, g1 = _mix_in_bwd(dup, dus, s["h"], dhm, p["norm_mix"], wp, l, g1)
        token = put_grads(l, g1)
        for n, a in (("dg1", dg1), ("dwp", dwp), ("dsc", dsc), ("dcp", dcp), ("dbp", dbp), ("ddsk", ddsk),
                     ("db_glu", db_glu), ("dg2", dg2), ("dar", dar), ("dai", dai)):
            raw[n][l] = a

    st = {n: jnp.stack(v) for n, v in raw.items()}
    dc_re, dc_im = jax.vmap(_unpad_pairs)(swap(st["dcp"]))
    dbbr, dbbi = jax.vmap(_unpad_pairs)(st["dbp"])
    rows = lambda a: a.reshape((n_rows,) + a.shape[2:])
    dlr, dli, dldt, dbr_t, dbi_t = _disc_bwd(lr, li, ldt, br_t, bi_t, st["dar"].reshape(n_rows, 1, SSM_STATE),
                                              st["dai"].reshape(n_rows, 1, SSM_STATE), rows(swap(dbbr)),
                                              rows(swap(dbbi)))
    small = {"norm_mix": st["dg1"][:, 0], "w_pool": st["dwp"], "pool_scale": st["dsc"][:, 0], "c_re": swap(dc_re),
             "c_im": -swap(dc_im), "d_skip": st["ddsk"].reshape(nl, N_SSM_GROUPS, SSM_GROUP),
             "b_glu": st["db_glu"][:, 0], "norm_ffn": st["dg2"][:, 0]}
    small["lam_re"] = dlr.reshape(nl, N_SSM_GROUPS, SSM_STATE)
    small["lam_im"] = dli.reshape(nl, N_SSM_GROUPS, SSM_STATE)
    small["log_dt"] = dldt.reshape(nl, N_SSM_GROUPS)
    small["b_re"] = dbr_t.reshape(nl, N_SSM_GROUPS, SSM_GROUP, SSM_STATE)
    small["b_im"] = dbi_t.reshape(nl, N_SSM_GROUPS, SSM_GROUP, SSM_STATE)
    small["d_skip"] = small["d_skip"].transpose(_SMALL_VIEW["d_skip"])
    small["norm_final"] = d_norm_final
    return loss, dh, small


_SMALL_VIEW = {"b_re": (0, 1, 3, 2), "b_im": (0, 1, 3, 2), "d_skip": (0, 2, 1)}
_SMALL_GROUPS = (("b_re", "b_im"), ("c_re", "c_im"), ("lam_re", "lam_im"), ("norm_mix", "norm_ffn"),
                 ("pool_scale", "b_glu"), ("w_pool",), ("log_dt",), ("d_skip",), ("norm_final",))


def _view(n, a):
    a = a.transpose(_SMALL_VIEW[n]) if n in _SMALL_VIEW else a
    return a[None] if a.ndim == 1 else a


def _unview(n, a, shape):
    a = a.reshape(shape) if len(shape) == 1 else a
    return a.transpose(_SMALL_VIEW[n]) if n in _SMALL_VIEW else a


def _flatten_small(views):
    flat = jnp.concatenate([views[n].reshape(-1) for n in _SMALL])
    n_rows = -(-flat.shape[0] // (64 * D_MODEL)) * 64
    return jnp.pad(flat, (0, n_rows * D_MODEL - flat.shape[0])).reshape(n_rows, D_MODEL)


def _split_small(flat, like):
    flat = flat.reshape(-1)
    out, at = {}, 0
    for n in _SMALL:
        size = like[n].size
        out[n] = flat[at:at + size].reshape(like[n].shape)
        at += size
    return out


def _adamw_small(name, ws, ms, vs, gs):
    k = len(ws)

    def body(*refs):
        ins, outs = refs[:4 * k], refs[4 * k:]
        for i in range(k):
            w, m, v, g = (ins[j * k + i][...] for j in range(4))
            delta, mn, vn = _adamw_math(w, g, m, v)
            outs[i][...] = delta
            outs[k + i][...] = mn
            outs[2 * k + i][...] = vn

    shapes = [jax.ShapeDtypeStruct(w.shape, F32) for w in ws] * 3
    outs = pl.pallas_call(body, name=name, out_shape=shapes,
                          compiler_params=pltpu.CompilerParams(vmem_limit_bytes=VMEM_LIMIT))(*ws, *ms, *vs, *gs)
    return outs[:k], outs[k:2 * k], outs[2 * k:]


def kernel(x, norm_mix, w_in, w_pool, pool_scale, lam_re, lam_im, log_dt, b_re, b_im, c_re, c_im, d_skip, w_glu, b_glu, w_out, norm_ffn, w_gate, w_up, w_down, norm_final, loss_target, m_norm_mix, m_w_in, m_w_pool, m_pool_scale, m_lam_re, m_lam_im, m_log_dt, m_b_re, m_b_im, m_c_re, m_c_im, m_d_skip, m_w_glu, m_b_glu, m_w_out, m_norm_ffn, m_w_gate, m_w_up, m_w_down, m_norm_final, v_norm_mix, v_w_in, v_w_pool, v_pool_scale, v_lam_re, v_lam_im, v_log_dt, v_b_re, v_b_im, v_c_re, v_c_im, v_d_skip, v_w_glu, v_b_glu, v_w_out, v_norm_ffn, v_w_gate, v_w_up, v_w_down, v_norm_final):
    given = dict(locals())
    w = {n: given[n] for n in _WEIGHTS}
    m = {n: given["m_" + n] for n in _WEIGHTS}
    v = {n: given["v_" + n] for n in _WEIGHTS}
    ids = jnp.stack([lax.axis_index("c"), 2 * lax.axis_index("x") + lax.axis_index("y")]).astype(jnp.int32)

    t_names = ("w_gate", "w_up")
    tr = lambda a: a.transpose(0, 2, 1)
    for d in (w, m, v):
        d.update({n: tr(d[n]) for n in t_names})

    nl = norm_mix.shape[0]
    mixer_rows, ffn_rows = (P_FF_ROWS, P_ROWS - P_FF_ROWS), (0, P_FF_ROWS)
    started, last = {}, None
    for l in range(nl):
        packed = _pack_weights(ids, l, w["w_in"], w["w_glu"], w["w_out"], w["w_down"], w["w_gate"], w["w_up"],
                               [] if last is None else [last])
        if l == 0:
            first = _ag_start("ag_start_0_mixer", packed, ids, mixer_rows)
            started[0] = _ag_start("ag_start_0_ffn", first[2], first[3], ffn_rows)
        else:
            started[l] = _ag_start(f"ag_start_{l}", packed, last)
        last = started[l][3]
    views = [{n: _view(n, d[n]) for n in _SMALL} for d in (w, m, v)]

    def get_weights(l, after):
        send_sems, recv_sems, buf, _ = started[l]
        if l == 0:
            buf = _ag_wait("ag_wait_0_mixer", first[0], first[1], buf, after + [last], mixer_rows)
            return _ag_forward(buf, mixer_rows)
        return _ag_forward(_ag_wait(f"ag_wait_{l}", send_sems, recv_sems, buf, after))

    def get_ffn_weights(l, buf, after):
        if l > 0:
            return buf
        send_sems, recv_sems, _, _ = started[0]
        return _ag_forward(_ag_wait("ag_wait_0_ffn", send_sems, recv_sems, buf, after, ffn_rows), ffn_rows)

    to_sibling, to_chips, reduced = {}, {}, {}

    def put_grads(l, g):
        to_sibling[l] = _rs_sibling_start(f"rs_sibling_start_{l}", g)
        token = to_sibling[l][4]
        if l + 1 in to_chips:
            finish(l + 1, [token])
        return token[:1, :1]

    def ffn_bwd_done(l, after):
        return send_to_chips(l + 1, after)[:1, :1] if l + 1 in to_sibling else None

    def send_to_chips(l, after):
        send_sems, recv_sems, g, land, _ = to_sibling.pop(l)
        g, land = _rs_sibling_wait(f"rs_sibling_wait_{l}", send_sems, recv_sems, g, land, after)
        own, t = _rs_add("rs_add", ids, g, land, RS_ROW_TILE)
        send_sems, recv_sems, t, land, token = _rs_chips_start(f"rs_chips_start_{l}", t)
        to_chips[l] = (send_sems, recv_sems, t, land, own)
        return token

    def finish(l, after):
        send_sems, recv_sems, t, land, own = to_chips.pop(l)
        land = _rs_chips_wait(f"rs_chips_wait_{l}", send_sems, recv_sems, t, land, after)
        shard = lax.empty((1, P_ROWS, D_MODEL), F32)
        reduced[l] = _rs_exchange_start(f"rs_exchange_start_{l}", _rs_sum(ids, 0, own, land, shard, RS_ROW_TILE))

    loss, grad_x, small = _local_step(x[0], loss_target[0], {n: w[n] for n in _SMALL}, get_weights, get_ffn_weights,
                                      ffn_bwd_done, put_grads)
    loss = lax.psum(loss[0, 0], ("x", "y", "c"))
    small_flat = _flatten_small(small)
    token = send_to_chips(0, [small_flat])

    big = (("w_in", P_IN_BLK, 256, False), ("w_out", P_OUT_BLK, 256, False), ("w_down", P_WD_BLK, 352, False),
           ("w_gate", P_WG_BLK, 352, False), ("w_up", P_WU_BLK, 352, False), ("w_glu", P_GLU_BLK, 128, True))
    res = {n: None for n, *_ in big}

    def adamw_layer(l, after):
        send_sems, recv_sems, shard = reduced[l]
        shard = _rs_exchange_wait(f"rs_exchange_wait_{l}", send_sems, recv_sems, shard, after)
        for n, (blk, idx), row_tile, glu in big:
            res[n] = _adamw("adamw_" + n, l, w[n], m[n], v[n], shard, (blk, D_MODEL), blk * idx, row_tile, res[n], (), glu)

    for l in reversed(range(1, nl)):
        adamw_layer(l, [token])
    updated = [r[0] for r in res.values() if r is not None]
    small_sum = _small_all_reduce(small_flat, [token] + updated)
    finish(0, [small_sum] + updated)
    adamw_layer(0, [])
    for n in t_names:
        res[n] = tuple(tr(a) for a in res[n])
    g_views = _split_small(small_sum, views[0])
    for group in _SMALL_GROUPS:
        deltas, new_ms, new_vs = _adamw_small("adamw_" + group[0], *[[d[n] for n in group] for d in views],
                                              [g_views[n] for n in group])
        for i, n in enumerate(group):
            res[n] = tuple(_unview(n, a, w[n].shape) for a in (g_views[n], deltas[i], new_ms[i], new_vs[i]))

    return (loss, grad_x[None], *[res[n][0] for n in _WEIGHTS], *[res[n][1] for n in _WEIGHTS],
            *[res[n][2] for n in _WEIGHTS], *[res[n][3] for n in _WEIGHTS])
```

```python
import functools
import math

import jax
import jax.numpy as jnp
from jax import lax
from jax.experimental import pallas as pl
from jax.experimental.pallas import tpu as pltpu

F32 = jnp.float32
BF16 = jnp.bfloat16

D_MODEL = 1024
D_POOL = 512
D_SSM = 512
POOL_WINDOWS = (2, 4, 8, 16)
POOL_GROUP = 128
POOL_HALO = 16
N_SSM_GROUPS = 32
SSM_GROUP = 16
SSM_STATE = 64
N_STATE = N_SSM_GROUPS * SSM_STATE
N_PAIRS = N_SSM_GROUPS // 2
D_FF = 2816
N_SHARD = 4
FF_SHARD = D_FF // N_SHARD
RMS_EPS = 1e-6

ADAM_LR = 0.001
ADAM_B1 = 0.9
ADAM_B2 = 0.999
ADAM_EPS = 1e-08
ADAM_WD = 0.01
ADAM_STEP = 10

P_ROWS = 2816
P_WD_BLK = (704, 0)
P_WG_BLK = (704, 1)
P_WU_BLK = (704, 2)
P_FF_ROWS = 2112
P_GLU_BLK = (64, 33)
P_GLU_PAD = 192
P_IN_BLK = (256, 9)
P_OUT_BLK = (256, 10)

SUBLANES = 8
VMEM_LIMIT = 56 * 1024 * 1024

TM = 1024
TM_FFN = 512
TM_FFN_LONG = 1024
FFN_SPLIT = 2
TS = 1024
SCAN_LANES = 512


def _cparams(n_axes):
    return pltpu.CompilerParams(dimension_semantics=("arbitrary",) * n_axes, vmem_limit_bytes=VMEM_LIMIT)


def _dot(a, b):
    return jnp.dot(a, b, preferred_element_type=F32)


def _dot_nt(a, b):
    return lax.dot_general(a, b, (((1,), (1,)), ((), ())), preferred_element_type=F32)


def _dot_tn(a, b):
    return lax.dot_general(a, b, (((0,), (0,)), ((), ())), preferred_element_type=F32)


def _rms_hat(x):
    r = lax.rsqrt(jnp.mean(x * x, axis=-1, keepdims=True) + RMS_EPS)
    return x * r, r


def _rms_bwd(d_hat, xhat, r):
    return r * (d_hat - xhat * jnp.mean(d_hat * xhat, axis=-1, keepdims=True))


def _sigmoid(x):
    return 1.0 / (1.0 + jnp.exp(-x))


_GELU_C = math.sqrt(2.0 / math.pi)
_GELU_K = 0.044715


def _gelu(x):
    return 0.5 * x * (1.0 + jnp.tanh(_GELU_C * (x + _GELU_K * x * x * x)))


def _gelu_grad(x):
    th = jnp.tanh(_GELU_C * (x + _GELU_K * x * x * x))
    return 0.5 * (1.0 + th) + 0.5 * x * (1.0 - th * th) * _GELU_C * (1.0 + 3.0 * _GELU_K * x * x)


def _glu_weight(ref):
    v = ref[...]
    return jnp.concatenate([v[:, :, :D_SSM], v[:, :, D_SSM:]], axis=1).reshape(D_SSM, D_SSM)


def _glu_pack(w):
    v = w.reshape(N_SHARD, 128, D_SSM)
    return jnp.concatenate([v[:, :64, :], v[:, 64:, :]], axis=2)


def _pool_diff(ext, row0, tm):
    rows = row0 + lax.broadcasted_iota(jnp.int32, (tm, 1), 0)
    outs = []
    for gi, w in enumerate(POOL_WINDOWS):
        e = ext[:, gi * POOL_GROUP:(gi + 1) * POOL_GROUP]
        s = e
        k = 1
        while k < w:
            s = s + pltpu.roll(s, k, 0)
            k *= 2
        inv = 1.0 / jnp.minimum(rows + 1, w).astype(F32)
        outs.append(s[POOL_HALO:, :] * inv - e[POOL_HALO:, :])
    return outs


def _mix_in_fwd(h, g1, wp, layer, w_pool, scale):
    L = h.shape[0]
    tm = min(TM, L)

    def body(h_ref, g_ref, w_ref, wp_ref, sc_ref, u_ref, yp_ref, carry):
        i = pl.program_id(0)

        @pl.when(i == 0)
        def _():
            carry[...] = jnp.zeros_like(carry)

        xhat, _ = _rms_hat(h_ref[...])
        n1 = (xhat * g_ref[...]).astype(BF16)
        u = _dot(n1, w_ref[...].reshape(D_MODEL, D_MODEL))
        u_ref[...] = u
        up = u[:, :D_POOL]
        ext = jnp.concatenate([carry[...], up], axis=0)
        carry[...] = up[tm - POOL_HALO:, :]
        diffs = _pool_diff(ext, i * tm, tm)
        for gi in range(4):
            cols = slice(gi * POOL_GROUP, (gi + 1) * POOL_GROUP)
            yp_ref[:, cols] = _dot(diffs[gi].astype(BF16), wp_ref[gi]) * sc_ref[:, cols]

    blk, idx = P_IN_BLK
    return pl.pallas_call(
        body, name="mix_in_fwd", grid=(L // tm,),
        in_specs=[pl.BlockSpec((tm, D_MODEL), lambda i: (i, 0)),
                  pl.BlockSpec((None, 1, D_MODEL), lambda i: (layer, 0, 0)),
                  pl.BlockSpec((N_SHARD, None, blk, D_MODEL), lambda i: (0, 0, idx, 0)),
                  pl.BlockSpec((None, 4, POOL_GROUP, POOL_GROUP), lambda i: (layer, 0, 0, 0)),
                  pl.BlockSpec((None, 1, D_POOL), lambda i: (layer, 0, 0))],
        out_specs=[pl.BlockSpec((tm, D_MODEL), lambda i: (i, 0)),
                   pl.BlockSpec((tm, D_POOL), lambda i: (i, 0))],
        out_shape=[jax.ShapeDtypeStruct((L, D_MODEL), F32), jax.ShapeDtypeStruct((L, D_POOL), F32)],
        scratch_shapes=[pltpu.VMEM((POOL_HALO, D_POOL), F32)],
        compiler_params=_cparams(1),
    )(h, g1, wp, w_pool, scale)


def _cmul(xr, xi, yr, yi):
    return xr * yr - xi * yi, xr * yi + xi * yr


SCAN_BLOCK = 64
N_SCAN_TABLES = 26


def _permute_rows(src, dst, n_rows):
    for b in range(n_rows // SCAN_BLOCK):
        for tau in range(SUBLANES):
            dst[pl.ds(SCAN_BLOCK * b + SUBLANES * tau, SUBLANES), :] = (
                src[pl.ds(SCAN_BLOCK * b + tau, SUBLANES, stride=SUBLANES), :])


def _scan_tables(ar, ai, tab, reverse):
    c = ar.shape[1]
    row = lax.broadcasted_iota(jnp.int32, (SUBLANES, c), 0)
    zero = jnp.zeros((SUBLANES, c), F32)
    full = lambda v: jnp.broadcast_to(v, (SUBLANES, c))
    pw = [(ar, ai)]
    for _ in range(SUBLANES - 1):
        pw.append(_cmul(*pw[-1], ar, ai))
    a8 = pw[-1]
    a16 = _cmul(*a8, *a8)
    a32 = _cmul(*a16, *a16)
    tab[0] = full(ar)
    tab[1] = full(ai)
    for n, (s, (pr, pi)) in enumerate(((1, a8), (2, a16), (4, a32))):
        keep = (row < SUBLANES - s) if reverse else (row >= s)
        tab[2 + 2 * n] = jnp.where(keep, pr, zero)
        tab[3 + 2 * n] = jnp.where(keep, pi, zero)
    cur = a8
    qr, qi = zero, zero
    for n in range(SUBLANES):
        at = (SUBLANES - 1 - n) if reverse else n
        qr = jnp.where(row == at, cur[0], qr)
        qi = jnp.where(row == at, cur[1], qi)
        cur = _cmul(*cur, *a8)
    tab[8] = qr
    tab[9] = qi
    for tau in range(SUBLANES):
        pr, pi = pw[SUBLANES - 1 - tau] if reverse else pw[tau]
        tab[10 + 2 * tau] = full(pr)
        tab[11 + 2 * tau] = full(pi)


def _cmac(xr, xi, ar, ai, yr, yi):
    return xr + ar * yr - ai * yi, xi + ar * yi + ai * yr


def _chain_segments(er, ei, c_r, c_i, tab, cols, reverse):
    tr, ti = er, ei
    for n, s in enumerate((1, 2, 4)):
        shift = SUBLANES - s if reverse else s
        tr, ti = _cmac(tr, ti, tab[2 + 2 * n, :, cols], tab[3 + 2 * n, :, cols],
                       pltpu.roll(tr, shift, 0), pltpu.roll(ti, shift, 0))
    return _cmac(tr, ti, tab[8, :, cols], tab[9, :, cols], c_r, c_i)


def _ssm_fwd(u, layer, bpad, cpad, ar, ai, dskip):
    L = u.shape[0]
    ts = min(TS, L)
    nq = 4
    cq = N_STATE // nq

    def body(u_ref, bp_ref, cp_ref, ar_ref, ai_ref, dsk_ref, sre_ref, sim_ref, y_ref, cr, ci, tab, up, yp):
        t = pl.program_id(1)

        @pl.when(t == 0)
        def _():
            cr[...] = jnp.zeros_like(cr)
            ci[...] = jnp.zeros_like(ci)
            _scan_tables(ar_ref[...], ai_ref[...], tab, reverse=False)

        _permute_rows(u_ref, up, ts)
        uf = up[...]
        ub = uf.astype(BF16)
        for jj in range(4):
            bu = _dot(ub, bp_ref[jj])
            sre_ref[:, jj * 128:(jj + 1) * 128] = bu[:, :128]
            sim_ref[:, jj * 128:(jj + 1) * 128] = bu[:, 128:]

        shp = (SUBLANES, SCAN_LANES)
        first_row = lax.broadcasted_iota(jnp.int32, shp, 0) == 0
        for cc in range(cq // SCAN_LANES):
            cols = slice(cc * SCAN_LANES, (cc + 1) * SCAN_LANES)

            def block(b, carry, cols=cols):
                c_r, c_i = carry
                base = pl.multiple_of(b * SCAN_BLOCK, SCAN_BLOCK)
                rows = lambda tau: pl.ds(base + SUBLANES * tau, SUBLANES)
                a_r, a_i = tab[0, :, cols], tab[1, :, cols]
                ys = [(sre_ref[rows(0), cols], sim_ref[rows(0), cols])]
                for tau in range(1, SUBLANES):
                    ys.append(_cmac(sre_ref[rows(tau), cols], sim_ref[rows(tau), cols], a_r, a_i, *ys[-1]))
                tr, ti = _chain_segments(*ys[-1], c_r, c_i, tab, cols, reverse=False)
                in_r = jnp.where(first_row, c_r, pltpu.roll(tr, 1, 0))
                in_i = jnp.where(first_row, c_i, pltpu.roll(ti, 1, 0))
                for tau in range(SUBLANES):
                    sr, si = _cmac(*ys[tau], tab[10 + 2 * tau, :, cols], tab[11 + 2 * tau, :, cols], in_r, in_i)
                    sre_ref[rows(tau), cols] = sr
                    sim_ref[rows(tau), cols] = si
                return (jnp.broadcast_to(tr[SUBLANES - 1:, :], shp), jnp.broadcast_to(ti[SUBLANES - 1:, :], shp))

            c_r, c_i = lax.fori_loop(0, ts // SCAN_BLOCK, block, (cr[:, cols], ci[:, cols]), unroll=2)
            cr[:, cols] = c_r
            ci[:, cols] = c_i

        acc = dsk_ref[...] * uf
        for jj in range(4):
            cols = slice(jj * 128, (jj + 1) * 128)
            scat = jnp.concatenate([sre_ref[:, cols], sim_ref[:, cols]], axis=1).astype(BF16)
            acc = acc + _dot(scat, cp_ref[jj])
        yp[...] = acc
        _permute_rows(yp, y_ref, ts)

    return pl.pallas_call(
        body, name="ssm_fwd", grid=(nq, L // ts),
        in_specs=[pl.BlockSpec((ts, 128), lambda q, t: (t, 4 + q)),
                  pl.BlockSpec((None, 4, 128, 256), lambda q, t: (layer, q, 0, 0)),
                  pl.BlockSpec((None, 4, 256, 128), lambda q, t: (layer, q, 0, 0)),
                  pl.BlockSpec((None, 1, cq), lambda q, t: (layer, 0, q)),
                  pl.BlockSpec((None, 1, cq), lambda q, t: (layer, 0, q)),
                  pl.BlockSpec((None, 1, 128), lambda q, t: (layer, 0, q))],
        out_specs=[pl.BlockSpec((ts, cq), lambda q, t: (t, q)),
                   pl.BlockSpec((ts, cq), lambda q, t: (t, q)),
                   pl.BlockSpec((ts, 128), lambda q, t: (t, q))],
        out_shape=[jax.ShapeDtypeStruct((L, N_STATE), F32), jax.ShapeDtypeStruct((L, N_STATE), F32),
                   jax.ShapeDtypeStruct((L, D_SSM), F32)],
        scratch_shapes=[pltpu.VMEM((SUBLANES, cq), F32), pltpu.VMEM((SUBLANES, cq), F32),
                        pltpu.VMEM((N_SCAN_TABLES, SUBLANES, cq), F32),
                        pltpu.VMEM((ts, 128), F32), pltpu.VMEM((ts, 128), F32)],
        compiler_params=_cparams(2),
    )(u, bpad, cpad, ar, ai, dskip)


def _mix_out_fwd(yraw, ypool, h, wp, layer, b_glu):
    L = h.shape[0]
    tm = min(TM, L)

    def body(yr_ref, yp_ref, h_ref, wglu_ref, b_ref, wout_ref, o_ref):
        y = _gelu(yr_ref[...])
        z = _dot(y.astype(BF16), _glu_weight(wglu_ref)) + b_ref[...]
        o = y * _sigmoid(z)
        mix = jnp.concatenate([yp_ref[...], o], axis=1).astype(BF16)
        o_ref[...] = h_ref[...] + _dot(mix, wout_ref[...].reshape(D_MODEL, D_MODEL))

    gb, gi = P_GLU_BLK
    ob, oi = P_OUT_BLK
    return pl.pallas_call(
        body, name="mix_out_fwd", grid=(L // tm,),
        in_specs=[pl.BlockSpec((tm, D_SSM), lambda i: (i, 0)),
                  pl.BlockSpec((tm, D_POOL), lambda i: (i, 0)),
                  pl.BlockSpec((tm, D_MODEL), lambda i: (i, 0)),
                  pl.BlockSpec((N_SHARD, None, gb, D_MODEL), lambda i: (0, 0, gi, 0)),
                  pl.BlockSpec((None, 1, D_SSM), lambda i: (layer, 0, 0)),
                  pl.BlockSpec((N_SHARD, None, ob, D_MODEL), lambda i: (0, 0, oi, 0))],
        out_specs=pl.BlockSpec((tm, D_MODEL), lambda i: (i, 0)),
        out_shape=jax.ShapeDtypeStruct((L, D_MODEL), F32),
        compiler_params=_cparams(1),
    )(yraw, ypool, h, wp, b_glu, wp)


def _ffn_weights(ref, k):
    return ref[k, 0:FF_SHARD, :], ref[k, FF_SHARD:2 * FF_SHARD, :], ref[k, 2 * FF_SHARD:P_FF_ROWS, :]


def _ffn_weight_spec():
    return pl.BlockSpec((N_SHARD, None, P_FF_ROWS, D_MODEL), lambda m, k: (0, 0, 0, 0),
                        pipeline_mode=pl.Buffered(1))


def _ffn_fwd(h, g2, wp, layer):
    L = h.shape[0]
    tm = min(TM_FFN_LONG, L)

    def body(h_ref, g_ref, w_ref, o_ref, n2_ref, act_ref, dgate_ref, dup_ref):
        k = pl.program_id(1)

        @pl.when(k == 0)
        def _():
            x = h_ref[...]
            xhat, _ = _rms_hat(x)
            n2_ref[...] = (xhat * g_ref[...]).astype(BF16)
            o_ref[...] = x

        wd, wg_t, wu_t = _ffn_weights(w_ref, k)
        n2 = n2_ref[...]
        gate = _dot_nt(n2, wg_t)
        up = _dot_nt(n2, wu_t)
        sg = _sigmoid(gate)
        silu = gate * sg
        act = (silu * up).astype(BF16)
        act_ref[...] = act
        dgate_ref[...] = (up * (sg * (1.0 + gate * (1.0 - sg)))).astype(BF16)
        dup_ref[...] = silu.astype(BF16)
        o_ref[...] += _dot(act, wd)

    act_shape = jax.ShapeDtypeStruct((N_SHARD, L, FF_SHARD), BF16)
    return pl.pallas_call(
        body, name="ffn_fwd", grid=(L // tm, N_SHARD),
        in_specs=[pl.BlockSpec((tm, D_MODEL), lambda m, k: (m, 0)),
                  pl.BlockSpec((None, 1, D_MODEL), lambda m, k: (layer, 0, 0)),
                  _ffn_weight_spec()],
        out_specs=[pl.BlockSpec((tm, D_MODEL), lambda m, k: (m, 0)),
                   pl.BlockSpec((tm, D_MODEL), lambda m, k: (m, 0)),
                   pl.BlockSpec((None, tm, FF_SHARD), lambda m, k: (k, m, 0)),
                   pl.BlockSpec((None, tm, FF_SHARD), lambda m, k: (k, m, 0)),
                   pl.BlockSpec((None, tm, FF_SHARD), lambda m, k: (k, m, 0))],
        out_shape=[jax.ShapeDtypeStruct((L, D_MODEL), F32), jax.ShapeDtypeStruct((L, D_MODEL), BF16),
                   act_shape, act_shape, act_shape],
        compiler_params=_cparams(2),
    )(h, g2, wp)


def _final_fwd_bwd(h, gf, target):
    L = h.shape[0]
    tm = min(TM, L)

    def body(h_ref, g_ref, t_ref, dh_ref, loss_ref, dg_ref):
        i = pl.program_id(0)

        @pl.when(i == 0)
        def _():
            loss_ref[...] = jnp.zeros_like(loss_ref)
            dg_ref[...] = jnp.zeros_like(dg_ref)

        xhat, r = _rms_hat(h_ref[...])
        g = g_ref[...]
        e = xhat * g - t_ref[...]
        loss_ref[...] += 0.5 * jnp.sum(jnp.mean(e * e, axis=-1, keepdims=True), axis=0, keepdims=True)
        dy = e * (1.0 / D_MODEL)
        dg_ref[...] += jnp.sum(dy * xhat, axis=0, keepdims=True)
        dh_ref[...] = _rms_bwd(dy * g, xhat, r)

    return pl.pallas_call(
        body, name="final_fwd_bwd", grid=(L // tm,),
        in_specs=[pl.BlockSpec((tm, D_MODEL), lambda i: (i, 0)),
                  pl.BlockSpec((1, D_MODEL), lambda i: (0, 0)),
                  pl.BlockSpec((tm, D_MODEL), lambda i: (i, 0))],
        out_specs=[pl.BlockSpec((tm, D_MODEL), lambda i: (i, 0)),
                   pl.BlockSpec((1, 1), lambda i: (0, 0)),
                   pl.BlockSpec((1, D_MODEL), lambda i: (0, 0))],
        out_shape=[jax.ShapeDtypeStruct((L, D_MODEL), F32), jax.ShapeDtypeStruct((1, 1), F32),
                   jax.ShapeDtypeStruct((1, D_MODEL), F32)],
        compiler_params=_cparams(1),
    )(h, gf, target)


def _ffn_bwd_act(dh, h, g2, fgate_s, fup_s, wp, layer):
    L = h.shape[0]
    tm = min(TM_FFN, L)
    sub = tm // FFN_SPLIT

    def body(dh_ref, h_ref, g_ref, fgate_ref, fup_ref, w_ref,
             dhm_ref, dg_ref, dgate_ref, dup_ref, dhb_ref):
        m, k = pl.program_id(0), pl.program_id(1)
        dn2 = dhm_ref

        @pl.when(jnp.logical_and(m == 0, k == 0))
        def _():
            dg_ref[...] = jnp.zeros_like(dg_ref)

        @pl.when(k == 0)
        def _():
            dhb_ref[...] = dh_ref[...].astype(BF16)
            dn2[...] = jnp.zeros_like(dn2)

        wd, wg_t, wu_t = _ffn_weights(w_ref, k)
        for rows in (slice(r * sub, (r + 1) * sub) for r in range(tm // sub)):
            dact = _dot_nt(dhb_ref[rows, :], wd)
            dgate = (dact * fgate_ref[rows, :].astype(F32)).astype(BF16)
            dup = (dact * fup_ref[rows, :].astype(F32)).astype(BF16)
            dgate_ref[rows, :] = dgate
            dup_ref[rows, :] = dup
            dn2[rows, :] += _dot(dgate, wg_t) + _dot(dup, wu_t)

        @pl.when(k == N_SHARD - 1)
        def _():
            xhat, r = _rms_hat(h_ref[...])
            d = dn2[...]
            dg_ref[...] += jnp.sum(d * xhat, axis=0, keepdims=True)
            dhm_ref[...] = dh_ref[...] + _rms_bwd(d * g_ref[...], xhat, r)

    act_spec = pl.BlockSpec((None, tm, FF_SHARD), lambda m, k: (k, m, 0))
    act_shape = jax.ShapeDtypeStruct((N_SHARD, L, FF_SHARD), BF16)
    row_spec = pl.BlockSpec((tm, D_MODEL), lambda m, k: (m, 0))
    return pl.pallas_call(
        body, name="ffn_bwd_act", grid=(L // tm, N_SHARD),
        in_specs=[row_spec, row_spec,
                  pl.BlockSpec((None, 1, D_MODEL), lambda m, k: (layer, 0, 0)),
                  act_spec, act_spec,
                  _ffn_weight_spec()],
        out_specs=[row_spec,
                   pl.BlockSpec((1, D_MODEL), lambda m, k: (0, 0)),
                   act_spec, act_spec, row_spec],
        out_shape=[jax.ShapeDtypeStruct((L, D_MODEL), F32), jax.ShapeDtypeStruct((1, D_MODEL), F32),
                   act_shape, act_shape, jax.ShapeDtypeStruct((L, D_MODEL), BF16)],
        compiler_params=_cparams(2),
    )(dh, h, g2, fgate_s, fup_s, wp)


def _ffn_bwd_w(n2, dgate_s, dup_s, act_s, dhb, gbuf):
    L = n2.shape[0]
    tm = min(TM_FFN_LONG, L)

    def body(n2_ref, dgate_ref, dup_ref, act_ref, dhb_ref, g_in, g_ref):
        m = pl.program_id(1)

        @pl.when(m == 0)
        def _():
            g_ref[...] = jnp.zeros_like(g_ref)

        n2v = n2_ref[...]
        g_ref[0:FF_SHARD, :] += _dot_tn(act_ref[...], dhb_ref[...])
        g_ref[FF_SHARD:2 * FF_SHARD, :] += _dot_tn(dgate_ref[...], n2v)
        g_ref[2 * FF_SHARD:P_FF_ROWS, :] += _dot_tn(dup_ref[...], n2v)

    act_spec = pl.BlockSpec((None, tm, FF_SHARD), lambda k, m: (k, m, 0))
    row_spec = pl.BlockSpec((tm, D_MODEL), lambda k, m: (m, 0))
    return pl.pallas_call(
        body, name="ffn_bwd_w", grid=(N_SHARD, L // tm),
        in_specs=[row_spec, act_spec, act_spec, act_spec, row_spec, pl.BlockSpec(memory_space=pl.ANY)],
        out_specs=pl.BlockSpec((None, None, P_FF_ROWS, D_MODEL), lambda k, m: (0, k, 0, 0)),
        out_shape=jax.ShapeDtypeStruct(gbuf.shape, F32),
        input_output_aliases={5: 0},
        compiler_params=_cparams(2),
    )(n2, dgate_s, dup_s, act_s, dhb, gbuf)


def _mix_out_bwd(dhm, yraw, ypool, wp, layer, b_glu, gbuf):
    L = dhm.shape[0]
    tm = min(TM, L)

    def body(dhm_ref, yr_ref, yp_ref, wglu_ref, b_ref, wout_ref, g1_in,
             dyr_ref, dyp_ref, db_ref, g1_ref, dwout, dwglu, gpack):
        i = pl.program_id(0)

        @pl.when(i == 0)
        def _():
            db_ref[...] = jnp.zeros_like(db_ref)
            dwout[...] = jnp.zeros_like(dwout)
            dwglu[...] = jnp.zeros_like(dwglu)

        dhb = dhm_ref[...].astype(BF16)
        wglu = _glu_weight(wglu_ref)
        dmix = _dot_nt(dhb, wout_ref[...].reshape(D_MODEL, D_MODEL))
        dyp_ref[...] = dmix[:, :D_POOL]
        d_o = dmix[:, D_POOL:]
        yraw_v = yr_ref[...]
        y = _gelu(yraw_v)
        yb = y.astype(BF16)
        sig = _sigmoid(_dot(yb, wglu) + b_ref[...])
        mix = jnp.concatenate([yp_ref[...], y * sig], axis=1).astype(BF16)
        dwout[...] += _dot_tn(mix, dhb).reshape(N_SHARD, 256, D_MODEL)
        dz = d_o * y * sig * (1.0 - sig)
        dzb = dz.astype(BF16)
        db_ref[...] += jnp.sum(dz, axis=0, keepdims=True)
        dwglu[...] += _dot_tn(yb, dzb)
        dy = d_o * sig + _dot_nt(dzb, wglu)
        dyr_ref[...] = dy * _gelu_grad(yraw_v)

        @pl.when(i == n_steps - 1)
        def _():
            gpack[:, :gb, :] = _glu_pack(dwglu[...])
            gpack[:, gb:, :] = jnp.zeros((N_SHARD, P_GLU_PAD - gb, D_MODEL), F32)
            pltpu.sync_copy(gpack, g1_ref.at[0, :, pl.ds(gb * gi, P_GLU_PAD), :])
            pltpu.sync_copy(dwout, g1_ref.at[0, :, pl.ds(ob * oi, ob), :])

    gb, gi = P_GLU_BLK
    ob, oi = P_OUT_BLK
    n_steps = L // tm
    return pl.pallas_call(
        body, name="mix_out_bwd", grid=(n_steps,),
        in_specs=[pl.BlockSpec((tm, D_MODEL), lambda i: (i, 0)),
                  pl.BlockSpec((tm, D_SSM), lambda i: (i, 0)),
                  pl.BlockSpec((tm, D_POOL), lambda i: (i, 0)),
                  pl.BlockSpec((N_SHARD, None, gb, D_MODEL), lambda i: (0, 0, gi, 0)),
                  pl.BlockSpec((None, 1, D_SSM), lambda i: (layer, 0, 0)),
                  pl.BlockSpec((N_SHARD, None, ob, D_MODEL), lambda i: (0, 0, oi, 0)),
                  pl.BlockSpec(memory_space=pl.ANY)],
        out_specs=[pl.BlockSpec((tm, D_SSM), lambda i: (i, 0)),
                   pl.BlockSpec((tm, D_POOL), lambda i: (i, 0)),
                   pl.BlockSpec((1, D_SSM), lambda i: (0, 0)),
                   pl.BlockSpec(memory_space=pl.ANY)],
        out_shape=[jax.ShapeDtypeStruct((L, D_SSM), F32), jax.ShapeDtypeStruct((L, D_POOL), F32),
                   jax.ShapeDtypeStruct((1, D_SSM), F32),
                   jax.ShapeDtypeStruct(gbuf.shape, F32)],
        scratch_shapes=[pltpu.VMEM((N_SHARD, ob, D_MODEL), F32), pltpu.VMEM((D_SSM, D_SSM), F32),
                        pltpu.VMEM((N_SHARD, P_GLU_PAD, D_MODEL), F32)],
        input_output_aliases={6: 3},
        compiler_params=_cparams(1),
    )(dhm, yraw, ypool, wp, b_glu, wp, gbuf)


def _ssm_bwd(dyraw, u, sre, sim, layer, cpad_t, bpad_t, ar, ai, dskip):
    L = u.shape[0]
    ts = min(TS, L)
    nt = L // ts
    nq = 4
    cq = N_STATE // nq

    def body(dy_ref, u_ref, sre_ref, sim_ref, ct_ref, bt_ref, ar_ref, ai_ref, dsk_ref,
             du_ref, dcp_ref, dbp_ref, dar_ref, dai_ref, ddsk_ref, gre, gim, cr, ci, tab, accr, acci, up, dyp):
        t = pl.program_id(1)

        @pl.when(t == 0)
        def _():
            for ref in (cr, ci, accr, acci, dcp_ref, dbp_ref, ddsk_ref):
                ref[...] = jnp.zeros_like(ref)
            _scan_tables(ar_ref[...], -ai_ref[...], tab, reverse=True)

        _permute_rows(dy_ref, dyp, ts)
        _permute_rows(u_ref, up, ts)
        dy = dyp[...]
        dyb = dy.astype(BF16)
        uf = up[...]
        ub = uf.astype(BF16)
        for jj in range(4):
            cols = slice(jj * 128, (jj + 1) * 128)
            ds = _dot(dyb, ct_ref[jj])
            gre[:, cols] = ds[:, :128]
            gim[:, cols] = ds[:, 128:]
            scat = jnp.concatenate([sre_ref[:, cols], sim_ref[:, cols]], axis=1).astype(BF16)
            dcp_ref[jj] += _dot_tn(scat, dyb)

        n_blk = ts // SCAN_BLOCK
        shp = (SUBLANES, SCAN_LANES)
        last_row = lax.broadcasted_iota(jnp.int32, shp, 0) == SUBLANES - 1
        for cc in range(cq // SCAN_LANES):
            cols = slice(cc * SCAN_LANES, (cc + 1) * SCAN_LANES)

            def block(i, carry, cols=cols):
                c_r, c_i, a_r, a_i = carry
                base = pl.multiple_of((n_blk - 1 - i) * SCAN_BLOCK, SCAN_BLOCK)
                rows = lambda tau: pl.ds(base + SUBLANES * tau, SUBLANES)
                m_r, m_i = tab[0, :, cols], tab[1, :, cols]
                ys = [None] * SUBLANES
                ys[SUBLANES - 1] = (gre[rows(SUBLANES - 1), cols], gim[rows(SUBLANES - 1), cols])
                for tau in reversed(range(SUBLANES - 1)):
                    ys[tau] = _cmac(gre[rows(tau), cols], gim[rows(tau), cols], m_r, m_i, *ys[tau + 1])
                tr, ti = _chain_segments(*ys[0], c_r, c_i, tab, cols, reverse=True)
                in_r = jnp.where(last_row, c_r, pltpu.roll(tr, SUBLANES - 1, 0))
                in_i = jnp.where(last_row, c_i, pltpu.roll(ti, SUBLANES - 1, 0))
                gs = [_cmac(*ys[tau], tab[10 + 2 * tau, :, cols], tab[11 + 2 * tau, :, cols], in_r, in_i)
                      for tau in range(SUBLANES)]
                for tau in range(SUBLANES):
                    gre[rows(tau), cols] = gs[tau][0]
                    gim[rows(tau), cols] = gs[tau][1]
                    if tau < SUBLANES - 1:
                        nr, ni = gs[tau + 1]
                    else:
                        nr = jnp.where(last_row, c_r, pltpu.roll(gs[0][0], SUBLANES - 1, 0))
                        ni = jnp.where(last_row, c_i, pltpu.roll(gs[0][1], SUBLANES - 1, 0))
                    sr, si = sre_ref[rows(tau), cols], sim_ref[rows(tau), cols]
                    a_r = a_r + sr * nr + si * ni
                    a_i = a_i + sr * ni - si * nr
                return (jnp.broadcast_to(tr[:1, :], shp), jnp.broadcast_to(ti[:1, :], shp), a_r, a_i)

            c_r, c_i, a_r, a_i = lax.fori_loop(
                0, n_blk, block, (cr[:, cols], ci[:, cols], accr[:, cols], acci[:, cols]), unroll=2)
            cr[:, cols] = c_r
            ci[:, cols] = c_i
            accr[:, cols] = a_r
            acci[:, cols] = a_i

        acc = dsk_ref[...] * dy
        for jj in range(4):
            cols = slice(jj * 128, (jj + 1) * 128)
            gcat = jnp.concatenate([gre[:, cols], gim[:, cols]], axis=1).astype(BF16)
            acc = acc + _dot(gcat, bt_ref[jj])
            dbp_ref[jj] += _dot_tn(ub, gcat)
        ddsk_ref[...] += jnp.sum(dy * uf, axis=0, keepdims=True)
        dyp[...] = acc
        _permute_rows(dyp, du_ref, ts)

        @pl.when(t == nt - 1)
        def _():
            dar_ref[...] = jnp.sum(accr[...], axis=0, keepdims=True)
            dai_ref[...] = jnp.sum(acci[...], axis=0, keepdims=True)

    f32_scr = lambda *s: pltpu.VMEM(s, F32)
    return pl.pallas_call(
        body, name="ssm_bwd", grid=(nq, nt),
        in_specs=[pl.BlockSpec((ts, 128), lambda q, t: (nt - 1 - t, q)),
                  pl.BlockSpec((ts, 128), lambda q, t: (nt - 1 - t, 4 + q)),
                  pl.BlockSpec((ts, cq), lambda q, t: (nt - 1 - t, q)),
                  pl.BlockSpec((ts, cq), lambda q, t: (nt - 1 - t, q)),
                  pl.BlockSpec((None, 4, 128, 256), lambda q, t: (layer, q, 0, 0)),
                  pl.BlockSpec((None, 4, 256, 128), lambda q, t: (layer, q, 0, 0)),
                  pl.BlockSpec((None, 1, cq), lambda q, t: (layer, 0, q)),
                  pl.BlockSpec((None, 1, cq), lambda q, t: (layer, 0, q)),
                  pl.BlockSpec((None, 1, 128), lambda q, t: (layer, 0, q))],
        out_specs=[pl.BlockSpec((ts, 128), lambda q, t: (nt - 1 - t, q)),
                   pl.BlockSpec((4, 256, 128), lambda q, t: (q, 0, 0)),
                   pl.BlockSpec((4, 128, 256), lambda q, t: (q, 0, 0)),
                   pl.BlockSpec((1, cq), lambda q, t: (0, q)),
                   pl.BlockSpec((1, cq), lambda q, t: (0, q)),
                   pl.BlockSpec((1, 128), lambda q, t: (0, q))],
        out_shape=[jax.ShapeDtypeStruct((L, D_SSM), F32),
                   jax.ShapeDtypeStruct((N_PAIRS, 256, 128), F32), jax.ShapeDtypeStruct((N_PAIRS, 128, 256), F32),
                   jax.ShapeDtypeStruct((1, N_STATE), F32), jax.ShapeDtypeStruct((1, N_STATE), F32),
                   jax.ShapeDtypeStruct((1, D_SSM), F32)],
        scratch_shapes=[f32_scr(ts, cq), f32_scr(ts, cq), f32_scr(SUBLANES, cq), f32_scr(SUBLANES, cq),
                        f32_scr(N_SCAN_TABLES, SUBLANES, cq), f32_scr(SUBLANES, cq), f32_scr(SUBLANES, cq),
                        f32_scr(ts, 128), f32_scr(ts, 128)],
        compiler_params=_cparams(2),
    )(dyraw, u, sre, sim, cpad_t, bpad_t, ar, ai, dskip)


def _pool_bwd(dyp, u, layer, w_pool, scale):
    L = u.shape[0]
    tm = min(TM, L)
    nt = L // tm
    halo_per_tile = tm // POOL_HALO

    def body(dyp_ref, u_ref, halo_ref, wp_ref, sc_ref, du_ref, dwp_ref, dsc_ref, carry):
        i = pl.program_id(0)
        tile = nt - 1 - i

        @pl.when(i == 0)
        def _():
            carry[...] = jnp.zeros_like(carry)
            dwp_ref[...] = jnp.zeros_like(dwp_ref)
            dsc_ref[...] = jnp.zeros_like(dsc_ref)

        up = u_ref[...]
        halo = jnp.where(tile > 0, halo_ref[...], jnp.zeros_like(halo_ref))
        diffs = _pool_diff(jnp.concatenate([halo, up], axis=0), tile * tm, tm)
        rows = tile * tm + lax.broadcasted_iota(jnp.int32, (tm, 1), 0)
        n_ext = tm + POOL_HALO
        for gi, w in enumerate(POOL_WINDOWS):
            cols = slice(gi * POOL_GROUP, (gi + 1) * POOL_GROUP)
            db = diffs[gi].astype(BF16)
            dyp = dyp_ref[:, cols]
            dsc_ref[:, cols] += jnp.sum(dyp * _dot(db, wp_ref[gi]), axis=0, keepdims=True)
            dp = (dyp * sc_ref[:, cols]).astype(BF16)
            ddiff = _dot_nt(dp, wp_ref[gi])
            dwp_ref[gi] += _dot_tn(db, dp)
            e = ddiff * (1.0 / jnp.minimum(rows + 1, w).astype(F32))
            s = jnp.concatenate([e, carry[:, cols]], axis=0)
            k = 1
            while k < w:
                s = s + pltpu.roll(s, n_ext - k, 0)
                k *= 2
            du_ref[:, cols] = s[:tm, :] - ddiff
            carry[:, cols] = e[:POOL_HALO, :]

    return pl.pallas_call(
        body, name="pool_bwd", grid=(nt,),
        in_specs=[pl.BlockSpec((tm, D_POOL), lambda i: (nt - 1 - i, 0)),
                  pl.BlockSpec((tm, D_POOL), lambda i: (nt - 1 - i, 0)),
                  pl.BlockSpec((POOL_HALO, D_POOL), lambda i: (jnp.maximum((nt - 1 - i) * halo_per_tile - 1, 0), 0)),
                  pl.BlockSpec((None, 4, POOL_GROUP, POOL_GROUP), lambda i: (layer, 0, 0, 0)),
                  pl.BlockSpec((None, 1, D_POOL), lambda i: (layer, 0, 0))],
        out_specs=[pl.BlockSpec((tm, D_POOL), lambda i: (nt - 1 - i, 0)),
                   pl.BlockSpec((4, POOL_GROUP, POOL_GROUP), lambda i: (0, 0, 0)),
                   pl.BlockSpec((1, D_POOL), lambda i: (0, 0))],
        out_shape=[jax.ShapeDtypeStruct((L, D_POOL), F32),
                   jax.ShapeDtypeStruct((4, POOL_GROUP, POOL_GROUP), F32),
                   jax.ShapeDtypeStruct((1, D_POOL), F32)],
        scratch_shapes=[pltpu.VMEM((POOL_HALO, D_POOL), F32)],
        compiler_params=_cparams(1),
    )(dyp, u, u, w_pool, scale)


def _mix_in_bwd(dup, dus, h, dhm, g1, wp, layer, gbuf):
    L = h.shape[0]
    tm = min(TM, L)
    n_steps = L // tm
    blk, idx = P_IN_BLK

    def body(dup_ref, dus_ref, h_ref, dhm_ref, g_ref, w_ref, g1_in, dh_ref, dg_ref, g1_ref, dwin):
        i = pl.program_id(0)

        @pl.when(i == 0)
        def _():
            dg_ref[...] = jnp.zeros_like(dg_ref)
            dwin[...] = jnp.zeros_like(dwin)

        du = jnp.concatenate([dup_ref[...], dus_ref[...]], axis=1).astype(BF16)
        dn1 = _dot_nt(du, w_ref[...].reshape(D_MODEL, D_MODEL))
        xhat, r = _rms_hat(h_ref[...])
        g = g_ref[...]
        n1 = (xhat * g).astype(BF16)
        dwin[...] += _dot_tn(n1, du).reshape(N_SHARD, blk, D_MODEL)
        dg_ref[...] += jnp.sum(dn1 * xhat, axis=0, keepdims=True)
        dh_ref[...] = dhm_ref[...] + _rms_bwd(dn1 * g, xhat, r)

        @pl.when(i == n_steps - 1)
        def _():
            pltpu.sync_copy(dwin, g1_ref.at[0, :, pl.ds(blk * idx, blk), :])

    row_spec = pl.BlockSpec((tm, D_MODEL), lambda i: (i, 0))
    half_spec = pl.BlockSpec((tm, D_POOL), lambda i: (i, 0))
    return pl.pallas_call(
        body, name="mix_in_bwd", grid=(n_steps,),
        in_specs=[half_spec, half_spec, row_spec, row_spec,
                  pl.BlockSpec((None, 1, D_MODEL), lambda i: (layer, 0, 0)),
                  pl.BlockSpec((N_SHARD, None, blk, D_MODEL), lambda i: (0, 0, idx, 0)),
                  pl.BlockSpec(memory_space=pl.ANY)],
        out_specs=[row_spec, pl.BlockSpec((1, D_MODEL), lambda i: (0, 0)), pl.BlockSpec(memory_space=pl.ANY)],
        out_shape=[jax.ShapeDtypeStruct((L, D_MODEL), F32), jax.ShapeDtypeStruct((1, D_MODEL), F32),
                   jax.ShapeDtypeStruct(gbuf.shape, F32)],
        scratch_shapes=[pltpu.VMEM((N_SHARD, blk, D_MODEL), F32)],
        input_output_aliases={6: 2},
        compiler_params=_cparams(1),
    )(dup, dus, h, dhm, g1, wp, gbuf)


def _disc_math(lr, li, ldt, br_t, bi_t):
    dt = jnp.exp(ldt)
    mag = jnp.exp(lr * dt)
    ang = li * dt
    ar = mag * jnp.cos(ang)
    ai = mag * jnp.sin(ang)
    den = lr * lr + li * li
    nr, ni = ar - 1.0, ai
    cr = (nr * lr + ni * li) / den
    ci = (ni * lr - nr * li) / den
    return ar, ai, cr * br_t - ci * bi_t, cr * bi_t + ci * br_t


def _disc_fwd(lr, li, ldt, br_t, bi_t):
    def body(lr_ref, li_ref, ldt_ref, br_ref, bi_ref, ar_ref, ai_ref, bbr_ref, bbi_ref):
        ar, ai, bbr, bbi = _disc_math(lr_ref[...], li_ref[...], ldt_ref[...], br_ref[...], bi_ref[...])
        ar_ref[...] = ar
        ai_ref[...] = ai
        bbr_ref[...] = bbr
        bbi_ref[...] = bbi

    shapes = [jax.ShapeDtypeStruct(a.shape, F32) for a in (lr, li, br_t, bi_t)]
    return pl.pallas_call(body, name="ssm_disc_fwd", out_shape=shapes,
                          compiler_params=pltpu.CompilerParams(vmem_limit_bytes=VMEM_LIMIT))(lr, li, ldt, br_t, bi_t)


def _disc_bwd(lr, li, ldt, br_t, bi_t, dar, dai, dbbr, dbbi):
    def body(lr_ref, li_ref, ldt_ref, br_ref, bi_ref, dar_ref, dai_ref, dbbr_ref, dbbi_ref,
             dlr_ref, dli_ref, dldt_ref, dbr_ref, dbi_ref):
        prim = (lr_ref[...], li_ref[...], ldt_ref[...], br_ref[...], bi_ref[...])
        _, pullback = jax.vjp(_disc_math, *prim)
        dlr, dli, dldt, dbr, dbi = pullback((dar_ref[...], dai_ref[...], dbbr_ref[...], dbbi_ref[...]))
        dlr_ref[...] = dlr
        dli_ref[...] = dli
        dldt_ref[...] = dldt
        dbr_ref[...] = dbr
        dbi_ref[...] = dbi

    shapes = [jax.ShapeDtypeStruct(a.shape, F32) for a in (lr, li, ldt, br_t, bi_t)]
    return pl.pallas_call(body, name="ssm_disc_bwd", out_shape=shapes,
                          compiler_params=pltpu.CompilerParams(vmem_limit_bytes=VMEM_LIMIT))(
        lr, li, ldt, br_t, bi_t, dar, dai, dbbr, dbbi)


def _pad_pairs(m_re, m_im):
    def blocks(m):
        v = m.transpose(0, 2, 1).reshape(N_PAIRS, 2, SSM_GROUP, SSM_STATE)
        return jnp.einsum("ab,jahp->jahbp", jnp.eye(2, dtype=m.dtype), v).reshape(N_PAIRS, 32, 128)
    both = jnp.concatenate([blocks(m_re), blocks(m_im)], axis=-1)
    place = jax.nn.one_hot(jnp.arange(N_PAIRS) % 4, 4, dtype=both.dtype)
    return jnp.einsum("jk,jrc->jkrc", place, both).reshape(N_PAIRS, 128, 256)


def _unpad_pairs(x):
    place = jax.nn.one_hot(jnp.arange(N_PAIRS) % 4, 4, dtype=x.dtype)
    both = jnp.einsum("jk,jkrc->jrc", place, x.reshape(N_PAIRS, 4, 32, 256))

    def unblock(v):
        v = v.reshape(N_PAIRS, 2, SSM_GROUP, 2, SSM_STATE)
        d = jnp.einsum("ab,jahbp->jahp", jnp.eye(2, dtype=x.dtype), v)
        return d.reshape(N_SSM_GROUPS, SSM_GROUP, SSM_STATE).transpose(0, 2, 1)
    return unblock(both[..., :128]), unblock(both[..., 128:])


def _adamw_math(w, g, m, v):
    m = ADAM_B1 * m + (1.0 - ADAM_B1) * g
    v = ADAM_B2 * v + (1.0 - ADAM_B2) * (g * g)
    m_hat = m / (1.0 - ADAM_B1 ** ADAM_STEP)
    v_hat = v / (1.0 - ADAM_B2 ** ADAM_STEP)
    delta = -ADAM_LR * (m_hat / (jnp.sqrt(v_hat) + ADAM_EPS) + ADAM_WD * w)
    return delta, m, v


def _adamw(name, layer, w, m, v, gbuf, g_block, g_row0, row_tile, outs=None, after=(), glu=False):
    nl, r, c = w.shape
    n_tiles = r // row_tile
    g_rows, g_cols = g_block
    g_tile = g_rows // n_tiles
    g_off = g_row0 // g_tile
    if outs is None:
        outs = [lax.empty(w.shape, F32) for _ in range(4)]

    def body(w_ref, m_ref, v_ref, g_ref, *rest):
        go_ref, d_ref, mo_ref, vo_ref = rest[-4:]
        g = g_ref[...]
        if glu:
            g = jnp.concatenate([g[:, :D_SSM], g[:, D_SSM:]], axis=0)
        delta, mn, vn = _adamw_math(w_ref[...], g, m_ref[...], v_ref[...])
        go_ref[...] = g
        d_ref[...] = delta
        mo_ref[...] = mn
        vo_ref[...] = vn

    w_spec = pl.BlockSpec((None, row_tile, c), lambda j: (layer, j, 0))
    shape = jax.ShapeDtypeStruct(w.shape, F32)
    return pl.pallas_call(
        body, name=name, grid=(n_tiles,),
        in_specs=[w_spec, w_spec, w_spec, pl.BlockSpec((None, g_tile, g_cols), lambda j: (0, g_off + j, 0))]
        + [_ANY] * (4 + len(after)),
        out_specs=[w_spec] * 4,
        out_shape=[shape] * 4,
        input_output_aliases={4: 0, 5: 1, 6: 2, 7: 3},
        compiler_params=_cparams(1),
    )(w, m, v, gbuf, *outs, *after)


def _pack_weights(ids, layer, w_in, w_glu, w_out, w_down, w_gate_t, w_up_t, after=()):
    gb, gi = P_GLU_BLK
    ib, ii = P_IN_BLK
    ob, oi = P_OUT_BLK

    def body(ids_ref, in_ref, glu_ref, out_ref, dn_ref, gate_ref, up_ref, *rest):
        p_ref = rest[-1]
        p_ref[0:FF_SHARD, :] = dn_ref[...].astype(BF16)
        p_ref[FF_SHARD:2 * FF_SHARD, :] = gate_ref[...].astype(BF16)
        p_ref[2 * FF_SHARD:P_FF_ROWS, :] = up_ref[...].astype(BF16)
        g = glu_ref[...]
        p_ref[gb * gi:gb * (gi + 1), :] = jnp.concatenate([g[:gb, :], g[gb:, :]], axis=1).astype(BF16)
        p_ref[gb * (gi + 1):ib * ii, :] = jnp.zeros((P_GLU_PAD - gb, D_MODEL), BF16)
        p_ref[ib * ii:ib * (ii + 1), :] = in_ref[...].astype(BF16)
        p_ref[ob * oi:ob * (oi + 1), :] = out_ref[...].astype(BF16)

    def spec(a):
        return pl.BlockSpec((None,) + a.shape[1:], lambda i, ids_ref: (layer, 0, 0))

    ins = (w_in, w_glu, w_out, w_down, w_gate_t, w_up_t)
    grid_spec = pltpu.PrefetchScalarGridSpec(
        num_scalar_prefetch=1, grid=(1,),
        in_specs=[spec(a) for a in ins] + [_ANY] * len(after),
        out_specs=pl.BlockSpec((None, None, P_ROWS, D_MODEL), lambda i, ids_ref: (ids_ref[1], 0, 0, 0)))
    return pl.pallas_call(
        body, name="pack_weights", grid_spec=grid_spec,
        out_shape=jax.ShapeDtypeStruct((N_SHARD, 1, P_ROWS, D_MODEL), BF16),
        compiler_params=_cparams(1),
    )(ids, *ins, *after)


MESH = pl.DeviceIdType.MESH
_ANY = pl.BlockSpec(memory_space=pl.ANY)
P_HALF = P_ROWS // 2
RS_ROW_TILE = 352


def _mesh_pos():
    return lax.axis_index("x"), lax.axis_index("y"), lax.axis_index("c")


def _other_chips(x, y):
    return [(1 - x, y), (x, 1 - y), (1 - x, 1 - y)]


def _remote(src, dst, send_sems, recv_sems, n, to):
    return pltpu.make_async_remote_copy(src_ref=src, dst_ref=dst, send_sem=send_sems.at[n],
                                        recv_sem=recv_sems.at[n], device_id=to, device_id_type=MESH)


_HBM = pl.BlockSpec(memory_space=pltpu.HBM)
_SEM = pl.BlockSpec(memory_space=pltpu.SEMAPHORE)
_EFFECT = pltpu.CompilerParams(has_side_effects=pltpu.SideEffectType.DATAFLOW_SIDE_EFFECTING)
_TOKEN = jax.ShapeDtypeStruct((8, 128), F32)


def _in_hbm(a):
    return pltpu.with_memory_space_constraint(a, pltpu.HBM)


def _ag_piece(ref, shard, half, rows):
    row0, n_rows = rows
    return ref.at[shard, :, pl.ds(row0 + half * (n_rows // 2), n_rows // 2), :]


def _ag_start(name, wp, after, row_ranges):
    n_sems = 3 * len(row_ranges)

    def body(w_ref, after_ref, send_sems, recv_sems, w_thru, token):
        x, y, c = _mesh_pos()
        for i, rows in enumerate(row_ranges):
            mine = _ag_piece(w_ref, 2 * x + y, c, rows)
            for j, (px, py) in enumerate(_other_chips(x, y)):
                _remote(mine, mine, send_sems, recv_sems, 3 * i + j, (px, py, c)).start()
        token[...] = jnp.zeros_like(token)

    return pl.pallas_call(
        body, name=name,
        out_shape=(pltpu.SemaphoreType.DMA((n_sems,)), pltpu.SemaphoreType.DMA((n_sems,)),
                   pltpu.HBM(wp.shape, wp.dtype), _TOKEN),
        in_specs=(_HBM, _ANY), out_specs=(_SEM, _SEM, _HBM, pl.BlockSpec(memory_space=pltpu.VMEM)),
        input_output_aliases={0: 2}, compiler_params=_EFFECT,
    )(_in_hbm(wp), after)


def _ag_wait(name, send_sems, recv_sems, wp, after, row_ranges):
    def body(w_ref, send_sems, recv_sems, *rest):
        x, y, c = _mesh_pos()
        for i, rows in enumerate(row_ranges):
            mine = _ag_piece(w_ref, 2 * x + y, c, rows)
            for j, (px, py) in enumerate(_other_chips(x, y)):
                landed = _ag_piece(w_ref, 2 * px + py, c, rows)
                cp = _remote(mine, landed, send_sems, recv_sems, 3 * i + j, (px, py, c))
                cp.wait_send()
                cp.wait_recv()

    return pl.pallas_call(
        body, name=name, out_shape=pltpu.HBM(wp.shape, wp.dtype),
        in_specs=(_HBM, _SEM, _SEM) + (_ANY,) * len(after), out_specs=_HBM,
        input_output_aliases={0: 0}, compiler_params=_EFFECT,
    )(wp, send_sems, recv_sems, *after)


def _ag_forward(wp, rows):
    def body(w_in, o, send_sems, recv_sems):
        x, y, c = _mesh_pos()
        sib = (x, y, 1 - c)
        chips = _other_chips(x, y)
        sends = []
        for j, (px, py) in enumerate(chips):
            landed = _ag_piece(o, 2 * px + py, c, rows)
            cp = _remote(landed, landed, send_sems, recv_sems, j, sib)
            cp.start()
            sends.append(cp)
        for j, (px, py) in enumerate(chips):
            passed = _ag_piece(o, 2 * px + py, 1 - c, rows)
            _remote(passed, passed, send_sems, recv_sems, j, sib).wait_recv()
        for cp in sends:
            cp.wait_send()

    return pl.pallas_call(
        body, name="ag_forward",
        in_specs=[_ANY], out_specs=_ANY,
        out_shape=jax.ShapeDtypeStruct(wp.shape, wp.dtype),
        scratch_shapes=[pltpu.SemaphoreType.DMA((3,)), pltpu.SemaphoreType.DMA((3,))],
        input_output_aliases={0: 0},
    )(wp)


def _ag_forward_start(name, wp, rows):
    def body(w_ref, send_sems, recv_sems, w_thru):
        x, y, c = _mesh_pos()
        for j, (px, py) in enumerate(_other_chips(x, y)):
            landed = _ag_piece(w_ref, 2 * px + py, c, rows)
            _remote(landed, landed, send_sems, recv_sems, j, (x, y, 1 - c)).start()

    return pl.pallas_call(
        body, name=name,
        out_shape=(pltpu.SemaphoreType.DMA((3,)), pltpu.SemaphoreType.DMA((3,)), pltpu.HBM(wp.shape, wp.dtype)),
        in_specs=(_HBM,), out_specs=(_SEM, _SEM, _HBM),
        input_output_aliases={0: 2}, compiler_params=_EFFECT,
    )(_in_hbm(wp))


def _ag_forward_wait(name, send_sems, recv_sems, wp, after, rows):
    def body(w_ref, send_sems, recv_sems, *rest):
        x, y, c = _mesh_pos()
        for j, (px, py) in enumerate(_other_chips(x, y)):
            cp = _remote(_ag_piece(w_ref, 2 * px + py, c, rows), _ag_piece(w_ref, 2 * px + py, 1 - c, rows),
                         send_sems, recv_sems, j, (x, y, 1 - c))
            cp.wait_send()
            cp.wait_recv()

    return pl.pallas_call(
        body, name=name, out_shape=pltpu.HBM(wp.shape, wp.dtype),
        in_specs=(_HBM, _SEM, _SEM) + (_ANY,) * len(after), out_specs=_HBM,
        input_output_aliases={0: 0}, compiler_params=_EFFECT,
    )(wp, send_sems, recv_sems, *after)


def _rs_chips_start(name, t):
    nl = t.shape[0]

    def body(t_ref, land_ref, send_sems, recv_sems, t_thru, land_thru, token):
        x, y, c = _mesh_pos()
        for j, (px, py) in enumerate(_other_chips(x, y)):
            _remote(t_ref.at[:, 2 * px + py], land_ref.at[j], send_sems, recv_sems, j, (px, py, c)).start()
        token[...] = jnp.zeros_like(token)

    land = lax.empty((3, nl, P_HALF, D_MODEL), BF16)
    return pl.pallas_call(
        body, name=name,
        out_shape=(pltpu.SemaphoreType.DMA((3,)), pltpu.SemaphoreType.DMA((3,)), pltpu.HBM(t.shape, t.dtype),
                   pltpu.HBM(land.shape, land.dtype), _TOKEN),
        in_specs=(_HBM, _HBM), out_specs=(_SEM, _SEM, _HBM, _HBM, pl.BlockSpec(memory_space=pltpu.VMEM)),
        input_output_aliases={0: 2, 1: 3}, compiler_params=_EFFECT,
    )(_in_hbm(t), _in_hbm(land))


def _rs_chips_wait(name, send_sems, recv_sems, t, land, after):
    def body(t_ref, land_ref, send_sems, recv_sems, *rest):
        x, y, c = _mesh_pos()
        for j, (px, py) in enumerate(_other_chips(x, y)):
            cp = _remote(t_ref.at[:, 2 * px + py], land_ref.at[j], send_sems, recv_sems, j, (px, py, c))
            cp.wait_send()
            cp.wait_recv()

    return pl.pallas_call(
        body, name=name, out_shape=(pltpu.HBM(t.shape, t.dtype), pltpu.HBM(land.shape, land.dtype)),
        in_specs=(_HBM, _HBM, _SEM, _SEM) + (_ANY,) * len(after), out_specs=(_HBM, _HBM),
        input_output_aliases={0: 0, 1: 1}, compiler_params=_EFFECT,
    )(t, land, send_sems, recv_sems, *after)[1]


def _rs_sibling_start(name, g):
    nl = g.shape[0]

    def body(g_ref, land_ref, send_sems, recv_sems, g_thru, land_thru, token):
        x, y, c = _mesh_pos()
        _remote(g_ref.at[:, :, pl.ds((1 - c) * P_HALF, P_HALF), :], land_ref, send_sems, recv_sems, 0,
                (x, y, 1 - c)).start()
        token[...] = jnp.zeros_like(token)

    land = lax.empty((nl, N_SHARD, P_HALF, D_MODEL), F32)
    return pl.pallas_call(
        body, name=name,
        out_shape=(pltpu.SemaphoreType.DMA((1,)), pltpu.SemaphoreType.DMA((1,)), pltpu.HBM(g.shape, g.dtype),
                   pltpu.HBM(land.shape, land.dtype), _TOKEN),
        in_specs=(_HBM, _HBM), out_specs=(_SEM, _SEM, _HBM, _HBM, pl.BlockSpec(memory_space=pltpu.VMEM)),
        input_output_aliases={0: 2, 1: 3}, compiler_params=_EFFECT,
    )(_in_hbm(g), _in_hbm(land))


def _rs_sibling_wait(name, send_sems, recv_sems, g, land, after):
    def body(g_ref, land_ref, send_sems, recv_sems, *rest):
        x, y, c = _mesh_pos()
        cp = _remote(g_ref.at[:, :, pl.ds((1 - c) * P_HALF, P_HALF), :], land_ref, send_sems, recv_sems, 0,
                     (x, y, 1 - c))
        cp.wait_send()
        cp.wait_recv()

    return pl.pallas_call(
        body, name=name, out_shape=(pltpu.HBM(g.shape, g.dtype), pltpu.HBM(land.shape, land.dtype)),
        in_specs=(_HBM, _HBM, _SEM, _SEM) + (_ANY,) * len(after), out_specs=(_HBM, _HBM),
        input_output_aliases={0: 0, 1: 1}, compiler_params=_EFFECT,
    )(g, land, send_sems, recv_sems, *after)


def _rs_add(name, ids, g, buf, row_tile):
    nl, _, hr, cols = buf.shape
    n_rt = hr // row_tile

    def body(ids_ref, g_ref, b_ref, own_ref, tb_ref):
        t = g_ref[...] + b_ref[...]
        tb_ref[...] = t.astype(BF16)

        @pl.when(pl.program_id(2) == ids_ref[1])
        def _():
            own_ref[...] = t

    blk = (None, None, row_tile, cols)
    grid_spec = pltpu.PrefetchScalarGridSpec(
        num_scalar_prefetch=1, grid=(nl, n_rt, N_SHARD),
        in_specs=[pl.BlockSpec(blk, lambda l, j, s, ids_ref: (l, s, ids_ref[0] * n_rt + j, 0)),
                  pl.BlockSpec(blk, lambda l, j, s, ids_ref: (l, s, j, 0))],
        out_specs=[pl.BlockSpec((None, row_tile, cols), lambda l, j, s, ids_ref: (l, j, 0)),
                   pl.BlockSpec(blk, lambda l, j, s, ids_ref: (l, s, j, 0))])
    return pl.pallas_call(
        body, name=name, grid_spec=grid_spec,
        out_shape=[jax.ShapeDtypeStruct((nl, hr, cols), F32), jax.ShapeDtypeStruct(buf.shape, BF16)],
        compiler_params=_cparams(3),
    )(ids, g, buf)


def _rs_sum(ids, layer, own, bufb, reduced, row_tile):
    _, hr, cols = own.shape
    n_rt = hr // row_tile

    def body(ids_ref, own_ref, b_ref, reduced_in, f_ref):
        f_ref[...] = ((own_ref[...] + b_ref[0].astype(F32)) + b_ref[1].astype(F32)) + b_ref[2].astype(F32)

    grid_spec = pltpu.PrefetchScalarGridSpec(
        num_scalar_prefetch=1, grid=(n_rt,),
        in_specs=[pl.BlockSpec((None, row_tile, cols), lambda j, ids_ref: (0, j, 0)),
                  pl.BlockSpec((3, None, row_tile, cols), lambda j, ids_ref: (0, 0, j, 0)),
                  pl.BlockSpec(memory_space=pl.ANY)],
        out_specs=pl.BlockSpec((None, row_tile, cols), lambda j, ids_ref: (layer, ids_ref[0] * n_rt + j, 0)))
    return pl.pallas_call(
        body, name="rs_sum", grid_spec=grid_spec,
        out_shape=jax.ShapeDtypeStruct(reduced.shape, F32),
        input_output_aliases={3: 0},
        compiler_params=_cparams(1),
    )(ids, own, bufb, reduced)


def _rs_exchange_start(name, f):
    def body(f_ref, send_sems, recv_sems, f_thru):
        x, y, c = _mesh_pos()
        mine = f_ref.at[:, pl.ds(c * P_HALF, P_HALF), :]
        _remote(mine, mine, send_sems, recv_sems, 0, (x, y, 1 - c)).start()

    return pl.pallas_call(
        body, name=name,
        out_shape=(pltpu.SemaphoreType.DMA((1,)), pltpu.SemaphoreType.DMA((1,)), pltpu.HBM(f.shape, f.dtype)),
        in_specs=(_HBM,), out_specs=(_SEM, _SEM, _HBM),
        input_output_aliases={0: 2}, compiler_params=_EFFECT,
    )(_in_hbm(f))


def _rs_exchange_wait(name, send_sems, recv_sems, f, after):
    def body(f_ref, send_sems, recv_sems, *rest):
        x, y, c = _mesh_pos()
        mine = f_ref.at[:, pl.ds(c * P_HALF, P_HALF), :]
        theirs = f_ref.at[:, pl.ds((1 - c) * P_HALF, P_HALF), :]
        cp = _remote(mine, theirs, send_sems, recv_sems, 0, (x, y, 1 - c))
        cp.wait_send()
        cp.wait_recv()

    return pl.pallas_call(
        body, name=name, out_shape=pltpu.HBM(f.shape, f.dtype),
        in_specs=(_HBM, _SEM, _SEM) + (_ANY,) * len(after), out_specs=_HBM,
        input_output_aliases={0: 0}, compiler_params=_EFFECT,
    )(f, send_sems, recv_sems, *after)


def _small_all_reduce(s, after=()):
    n_rows = s.shape[0]
    hr = n_rows // 2
    qr = hr // N_SHARD

    def body(s_ref, *rest):
        o_ref, sibbuf, tbuf, qbuf, fbuf, send_sems, recv_sems = rest[len(after):]
        x, y, c = _mesh_pos()
        k = 2 * x + y
        sib = (x, y, 1 - c)
        chips = _other_chips(x, y)
        mine = pl.ds(pl.multiple_of(c * hr, SUBLANES), hr)
        theirs = pl.ds(pl.multiple_of((1 - c) * hr, SUBLANES), hr)

        def quarter(shard):
            return pl.ds(pl.multiple_of(shard * qr, SUBLANES), qr)

        first = _remote(s_ref.at[theirs], sibbuf, send_sems, recv_sems, 0, sib)
        first.start()
        first.wait()
        tbuf[...] = s_ref[mine, :] + sibbuf[...]
        cps = []
        for j, (px, py) in enumerate(chips):
            cp = _remote(tbuf.at[quarter(2 * px + py)], qbuf.at[j], send_sems, recv_sems, 1 + j, (px, py, c))
            cp.start()
            cps.append(cp)
        for cp in cps:
            cp.wait()
        fbuf[quarter(k), :] = (tbuf[quarter(k), :] + qbuf[1]) + (qbuf[0] + qbuf[2])
        cps = []
        for j, (px, py) in enumerate(chips):
            cp = _remote(fbuf.at[quarter(k)], fbuf.at[quarter(k)], send_sems, recv_sems, 4 + j, (px, py, c))
            cp.start()
            cps.append(cp)
        for j, (px, py) in enumerate(chips):
            got = fbuf.at[quarter(2 * px + py)]
            _remote(got, got, send_sems, recv_sems, 4 + j, (px, py, c)).wait_recv()
        for cp in cps:
            cp.wait_send()
        o_ref[mine, :] = fbuf[...]
        last = _remote(fbuf, o_ref.at[mine], send_sems, recv_sems, 7, sib)
        last.start()
        last.wait()

    vmem = pl.BlockSpec(memory_space=pltpu.VMEM)
    return pl.pallas_call(
        body, name="small_all_reduce",
        in_specs=[vmem] + [_ANY] * len(after), out_specs=vmem,
        out_shape=jax.ShapeDtypeStruct(s.shape, F32),
        scratch_shapes=[pltpu.VMEM((hr, D_MODEL), F32), pltpu.VMEM((hr, D_MODEL), F32),
                        pltpu.VMEM((3, qr, D_MODEL), F32), pltpu.VMEM((hr, D_MODEL), F32),
                        pltpu.SemaphoreType.DMA((8,)), pltpu.SemaphoreType.DMA((8,))],
        compiler_params=pltpu.CompilerParams(vmem_limit_bytes=VMEM_LIMIT),
    )(s, *after)


_SMALL = ("norm_mix", "w_pool", "pool_scale", "lam_re", "lam_im", "log_dt", "b_re", "b_im", "c_re", "c_im",
          "d_skip", "b_glu", "norm_ffn", "norm_final")
_WEIGHTS = ("norm_mix", "w_in", "w_pool", "pool_scale", "lam_re", "lam_im", "log_dt", "b_re", "b_im", "c_re",
            "c_im", "d_skip", "w_glu", "b_glu", "w_out", "norm_ffn", "w_gate", "w_up", "w_down", "norm_final")


def _local_step(x, target, p, get_weights, get_ffn_weights, ffn_bwd_done, put_grads):
    nl = p["norm_mix"].shape[0]

    def tied(a, token):
        return a if token is None else a + token
    n_rows = nl * N_SSM_GROUPS
    lr = p["lam_re"].reshape(n_rows, 1, SSM_STATE)
    li = p["lam_im"].reshape(n_rows, 1, SSM_STATE)
    ldt = p["log_dt"].reshape(n_rows, 1, 1)
    br_t = p["b_re"].reshape(n_rows, SSM_STATE, SSM_GROUP).transpose(0, 2, 1)
    bi_t = p["b_im"].reshape(n_rows, SSM_STATE, SSM_GROUP).transpose(0, 2, 1)
    ar, ai, bbr_t, bbi_t = _disc_fwd(lr, li, ldt, br_t, bi_t)
    ar = ar.reshape(nl, 1, N_STATE)
    ai = ai.reshape(nl, 1, N_STATE)
    bbr = bbr_t.transpose(0, 2, 1).reshape(nl, N_SSM_GROUPS, SSM_STATE, SSM_GROUP)
    bbi = bbi_t.transpose(0, 2, 1).reshape(nl, N_SSM_GROUPS, SSM_STATE, SSM_GROUP)
    w_pool = p["w_pool"].astype(BF16)
    p = dict(p)
    for n in ("norm_mix", "pool_scale", "b_glu", "norm_ffn"):
        p[n] = p[n].reshape(nl, 1, -1)
    swap = lambda a: jnp.swapaxes(a, -1, -2)
    bpad = jax.vmap(_pad_pairs)(bbr, bbi).astype(BF16)
    cpad_t = jax.vmap(_pad_pairs)(swap(p["c_re"]), -swap(p["c_im"])).astype(BF16)
    bpad_t, cpad = swap(bpad), swap(cpad_t)
    dskip = p["d_skip"].reshape(nl, 1, D_SSM)

    layers = []
    h = x
    for l in range(nl):
        wp = get_weights(l, [h] if l else [h, bpad, cpad, bpad_t, cpad_t, ar, ai])
        u, ypool = _mix_in_fwd(h, p["norm_mix"], wp, l, w_pool, p["pool_scale"])
        sre, sim, yraw = _ssm_fwd(u, l, bpad, cpad, ar, ai, dskip)
        hm = _mix_out_fwd(yraw, ypool, h, wp, l, p["b_glu"])
        wp = get_ffn_weights(l, wp, [hm])
        h_next, n2, act_s, fgate_s, fup_s = _ffn_fwd(hm, p["norm_ffn"], wp, l)
        layers.append(dict(h=h, u=u, ypool=ypool, sre=sre, sim=sim, yraw=yraw, hm=hm, n2=n2, act_s=act_s, wp=wp,
                           fgate_s=fgate_s, fup_s=fup_s))
        h = h_next

    dh, loss, d_norm_final = _final_fwd_bwd(h, p["norm_final"].reshape(1, D_MODEL), target)

    raw = {n: [None] * nl for n in ("dg1", "dwp", "dsc", "dcp", "dbp", "ddsk", "db_glu", "dg2", "dar", "dai")}
    token = None
    for l in reversed(range(nl)):
        s = layers[l]
        wp = s["wp"]
        g1 = lax.empty((1, N_SHARD, P_ROWS, D_MODEL), F32)
        dhm, dg2, dgate_s, dup_s, dhb = _ffn_bwd_act(dh, s["hm"], tied(p["norm_ffn"], token), s["fgate_s"],
                                                      s["fup_s"], wp, l)
        g1 = _ffn_bwd_w(s["n2"], dgate_s, dup_s, s["act_s"], dhb, g1)
        token = ffn_bwd_done(l, [g1])
        dyraw, dyp, db_glu, g1 = _mix_out_bwd(dhm, s["yraw"], s["ypool"], wp, l, tied(p["b_glu"], token), g1)
        dus, dcp, dbp, dar, dai, ddsk = _ssm_bwd(dyraw, s["u"], s["sre"], s["sim"], l, cpad_t, bpad_t, ar, ai, dskip)
        dup, dwp, dsc = _pool_bwd(dyp, s["u"], l, w_pool, p["pool_scale"])
        dh, dg1, g1 = _mix_in_bwd(dup, dus, s["h"], dhm, p["norm_mix"], wp, l, g1)
        token = put_grads(l, g1)
        for n, a in (("dg1", dg1), ("dwp", dwp), ("dsc", dsc), ("dcp", dcp), ("dbp", dbp), ("ddsk", ddsk),
                     ("db_glu", db_glu), ("dg2", dg2), ("dar", dar), ("dai", dai)):
            raw[n][l] = a

    st = {n: jnp.stack(v) for n, v in raw.items()}
    dc_re, dc_im = jax.vmap(_unpad_pairs)(swap(st["dcp"]))
    dbbr, dbbi = jax.vmap(_unpad_pairs)(st["dbp"])
    rows = lambda a: a.reshape((n_rows,) + a.shape[2:])
    dlr, dli, dldt, dbr_t, dbi_t = _disc_bwd(lr, li, ldt, br_t, bi_t, st["dar"].reshape(n_rows, 1, SSM_STATE),
                                              st["dai"].reshape(n_rows, 1, SSM_STATE), rows(swap(dbbr)),
                                              rows(swap(dbbi)))
    small = {"norm_mix": st["dg1"][:, 0], "w_pool": st["dwp"], "pool_scale": st["dsc"][:, 0], "c_re": swap(dc_re),
             "c_im": -swap(dc_im), "d_skip": st["ddsk"].reshape(nl, N_SSM_GROUPS, SSM_GROUP),
             "b_glu": st["db_glu"][:, 0], "norm_ffn": st["dg2"][:, 0]}
    small["lam_re"] = dlr.reshape(nl, N_SSM_GROUPS, SSM_STATE)
    small["lam_im"] = dli.reshape(nl, N_SSM_GROUPS, SSM_STATE)
    small["log_dt"] = dldt.reshape(nl, N_SSM_GROUPS)
    small["b_re"] = dbr_t.reshape(nl, N_SSM_GROUPS, SSM_GROUP, SSM_STATE)
    small["b_im"] = dbi_t.reshape(nl, N_SSM_GROUPS, SSM_GROUP, SSM_STATE)
    small["d_skip"] = small["d_skip"].transpose(_SMALL_VIEW["d_skip"])
    small["norm_final"] = d_norm_final
    return loss, dh, small


_SMALL_VIEW = {"b_re": (0, 1, 3, 2), "b_im": (0, 1, 3, 2), "d_skip": (0, 2, 1)}
_SMALL_GROUPS = (("b_re", "b_im"), ("c_re", "c_im"), ("lam_re", "lam_im"), ("norm_mix", "norm_ffn"),
                 ("pool_scale", "b_glu"), ("w_pool",), ("log_dt",), ("d_skip",), ("norm_final",))


def _view(n, a):
    a = a.transpose(_SMALL_VIEW[n]) if n in _SMALL_VIEW else a
    return a[None] if a.ndim == 1 else a


def _unview(n, a, shape):
    a = a.reshape(shape) if len(shape) == 1 else a
    return a.transpose(_SMALL_VIEW[n]) if n in _SMALL_VIEW else a


def _flatten_small(views):
    flat = jnp.concatenate([views[n].reshape(-1) for n in _SMALL])
    n_rows = -(-flat.shape[0] // (64 * D_MODEL)) * 64
    return jnp.pad(flat, (0, n_rows * D_MODEL - flat.shape[0])).reshape(n_rows, D_MODEL)


def _split_small(flat, like):
    flat = flat.reshape(-1)
    out, at = {}, 0
    for n in _SMALL:
        size = like[n].size
        out[n] = flat[at:at + size].reshape(like[n].shape)
        at += size
    return out


def _adamw_small(name, ws, ms, vs, gs):
    k = len(ws)

    def body(*refs):
        ins, outs = refs[:4 * k], refs[4 * k:]
        for i in range(k):
            w, m, v, g = (ins[j * k + i][...] for j in range(4))
            delta, mn, vn = _adamw_math(w, g, m, v)
            outs[i][...] = delta
            outs[k + i][...] = mn
            outs[2 * k + i][...] = vn

    shapes = [jax.ShapeDtypeStruct(w.shape, F32) for w in ws] * 3
    outs = pl.pallas_call(body, name=name, out_shape=shapes,
                          compiler_params=pltpu.CompilerParams(vmem_limit_bytes=VMEM_LIMIT))(*ws, *ms, *vs, *gs)
    return outs[:k], outs[k:2 * k], outs[2 * k:]


def kernel(x, norm_mix, w_in, w_pool, pool_scale, lam_re, lam_im, log_dt, b_re, b_im, c_re, c_im, d_skip, w_glu, b_glu, w_out, norm_ffn, w_gate, w_up, w_down, norm_final, loss_target, m_norm_mix, m_w_in, m_w_pool, m_pool_scale, m_lam_re, m_lam_im, m_log_dt, m_b_re, m_b_im, m_c_re, m_c_im, m_d_skip, m_w_glu, m_b_glu, m_w_out, m_norm_ffn, m_w_gate, m_w_up, m_w_down, m_norm_final, v_norm_mix, v_w_in, v_w_pool, v_pool_scale, v_lam_re, v_lam_im, v_log_dt, v_b_re, v_b_im, v_c_re, v_c_im, v_d_skip, v_w_glu, v_b_glu, v_w_out, v_norm_ffn, v_w_gate, v_w_up, v_w_down, v_norm_final):
    given = dict(locals())
    w = {n: given[n] for n in _WEIGHTS}
    m = {n: given["m_" + n] for n in _WEIGHTS}
    v = {n: given["v_" + n] for n in _WEIGHTS}
    ids = jnp.stack([lax.axis_index("c"), 2 * lax.axis_index("x") + lax.axis_index("y")]).astype(jnp.int32)

    t_names = ("w_gate", "w_up")
    tr = lambda a: a.transpose(0, 2, 1)
    for d in (w, m, v):
        d.update({n: tr(d[n]) for n in t_names})

    nl = norm_mix.shape[0]
    mixer_rows, ffn_rows = (P_FF_ROWS, P_ROWS - P_FF_ROWS), (0, P_FF_ROWS)
    started, last = {}, None
    for l in range(nl):
        packed = _pack_weights(ids, l, w["w_in"], w["w_glu"], w["w_out"], w["w_down"], w["w_gate"], w["w_up"],
                               [] if last is None else [last])
        if l == 0:
            first = _ag_start("ag_start_0_mixer", packed, ids, [mixer_rows])
            started[0] = _ag_start("ag_start_0_ffn", first[2], first[3], [ffn_rows])
        else:
            started[l] = _ag_start(f"ag_start_{l}", packed, last, [mixer_rows, ffn_rows])
        last = started[l][3]
    views = [{n: _view(n, d[n]) for n in _SMALL} for d in (w, m, v)]

    passing = {}

    def get_weights(l, after):
        send_sems, recv_sems, buf, _ = started[l]
        if l == 0:
            buf = _ag_wait("ag_wait_0_mixer", first[0], first[1], buf, after + [last], [mixer_rows])
            return _ag_forward(buf, mixer_rows)
        buf = _ag_wait(f"ag_wait_{l}", send_sems, recv_sems, buf, after, [mixer_rows, ffn_rows])
        buf = _ag_forward(buf, mixer_rows)
        passing[l] = _ag_forward_start(f"ag_forward_start_{l}", buf, ffn_rows)
        return passing[l][2]

    def get_ffn_weights(l, buf, after):
        if l > 0:
            send_sems, recv_sems, _ = passing[l]
            return _ag_forward_wait(f"ag_forward_wait_{l}", send_sems, recv_sems, buf, after, ffn_rows)
        send_sems, recv_sems, _, _ = started[0]
        return _ag_forward(_ag_wait("ag_wait_0_ffn", send_sems, recv_sems, buf, after, [ffn_rows]), ffn_rows)

    to_sibling, to_chips, reduced = {}, {}, {}

    def put_grads(l, g):
        to_sibling[l] = _rs_sibling_start(f"rs_sibling_start_{l}", g)
        token = to_sibling[l][4]
        if l + 1 in to_chips:
            finish(l + 1, [token])
        return token[:1, :1]

    def ffn_bwd_done(l, after):
        return send_to_chips(l + 1, after)[:1, :1] if l + 1 in to_sibling else None

    def send_to_chips(l, after):
        send_sems, recv_sems, g, land, _ = to_sibling.pop(l)
        g, land = _rs_sibling_wait(f"rs_sibling_wait_{l}", send_sems, recv_sems, g, land, after)
        own, t = _rs_add("rs_add", ids, g, land, RS_ROW_TILE)
        send_sems, recv_sems, t, land, token = _rs_chips_start(f"rs_chips_start_{l}", t)
        to_chips[l] = (send_sems, recv_sems, t, land, own)
        return token

    def finish(l, after):
        send_sems, recv_sems, t, land, own = to_chips.pop(l)
        land = _rs_chips_wait(f"rs_chips_wait_{l}", send_sems, recv_sems, t, land, after)
        shard = lax.empty((1, P_ROWS, D_MODEL), F32)
        reduced[l] = _rs_exchange_start(f"rs_exchange_start_{l}", _rs_sum(ids, 0, own, land, shard, RS_ROW_TILE))

    loss, grad_x, small = _local_step(x[0], loss_target[0], {n: w[n] for n in _SMALL}, get_weights, get_ffn_weights,
                                      ffn_bwd_done, put_grads)
    loss = lax.psum(loss[0, 0], ("x", "y", "c"))
    small_flat = _flatten_small(small)
    token = send_to_chips(0, [small_flat])

    big = (("w_in", P_IN_BLK, 256, False), ("w_out", P_OUT_BLK, 256, False), ("w_down", P_WD_BLK, 352, False),
           ("w_gate", P_WG_BLK, 352, False), ("w_up", P_WU_BLK, 352, False), ("w_glu", P_GLU_BLK, 128, True))
    res = {n: None for n, *_ in big}

    def adamw_layer(l, after):
        send_sems, recv_sems, shard = reduced[l]
        shard = _rs_exchange_wait(f"rs_exchange_wait_{l}", send_sems, recv_sems, shard, after)
        for n, (blk, idx), row_tile, glu in big:
            res[n] = _adamw("adamw_" + n, l, w[n], m[n], v[n], shard, (blk, D_MODEL), blk * idx, row_tile, res[n], (), glu)

    for l in reversed(range(1, nl)):
        adamw_layer(l, [token])
    updated = [r[0] for r in res.values() if r is not None]
    small_sum = _small_all_reduce(small_flat, [token] + updated)
    finish(0, [small_sum] + updated)
    adamw_layer(0, [])
    for n in t_names:
        res[n] = tuple(tr(a) for a in res[n])
    g_views = _split_small(small_sum, views[0])
    for group in _SMALL_GROUPS:
        deltas, new_ms, new_vs = _adamw_small("adamw_" + group[0], *[[d[n] for n in group] for d in views],
                                              [g_views[n] for n in group])
        for i, n in enumerate(group):
            res[n] = tuple(_unview(n, a, w[n].shape) for a in (g_views[n], deltas[i], new_ms[i], new_vs[i]))

    return (loss, grad_x[None], *[res[n][0] for n in _WEIGHTS], *[res[n][1] for n in _WEIGHTS],
            *[res[n][2] for n in _WEIGHTS], *[res[n][3] for n in _WEIGHTS])
```

```python
import functools
import math

import jax
import jax.numpy as jnp
from jax import lax
from jax.experimental import pallas as pl
from jax.experimental.pallas import tpu as pltpu

F32 = jnp.float32
BF16 = jnp.bfloat16

D_MODEL = 1024
D_POOL = 512
D_SSM = 512
POOL_WINDOWS = (2, 4, 8, 16)
POOL_GROUP = 128
POOL_HALO = 16
N_SSM_GROUPS = 32
SSM_GROUP = 16
SSM_STATE = 64
N_STATE = N_SSM_GROUPS * SSM_STATE
N_PAIRS = N_SSM_GROUPS // 2
D_FF = 2816
N_SHARD = 4
FF_SHARD = D_FF // N_SHARD
RMS_EPS = 1e-6

ADAM_LR = 0.001
ADAM_B1 = 0.9
ADAM_B2 = 0.999
ADAM_EPS = 1e-08
ADAM_WD = 0.01
ADAM_STEP = 10

P_ROWS = 2816
P_WD_BLK = (704, 0)
P_WG_BLK = (704, 1)
P_WU_BLK = (704, 2)
P_FF_ROWS = 2112
P_GLU_BLK = (64, 33)
P_GLU_PAD = 192
P_IN_BLK = (256, 9)
P_OUT_BLK = (256, 10)

SUBLANES = 8
VMEM_LIMIT = 56 * 1024 * 1024

TM = 1024
TM_FFN = 512
TM_FFN_LONG = 1024
FFN_SPLIT = 2
TS = 2048
SCAN_LANES = 512


def _cparams(n_axes):
    return pltpu.CompilerParams(dimension_semantics=("arbitrary",) * n_axes, vmem_limit_bytes=VMEM_LIMIT)


def _dot(a, b):
    return jnp.dot(a, b, preferred_element_type=F32)


def _dot_nt(a, b):
    return lax.dot_general(a, b, (((1,), (1,)), ((), ())), preferred_element_type=F32)


def _dot_tn(a, b):
    return lax.dot_general(a, b, (((0,), (0,)), ((), ())), preferred_element_type=F32)


def _rms_hat(x):
    r = lax.rsqrt(jnp.mean(x * x, axis=-1, keepdims=True) + RMS_EPS)
    return x * r, r


def _rms_bwd(d_hat, xhat, r):
    return r * (d_hat - xhat * jnp.mean(d_hat * xhat, axis=-1, keepdims=True))


def _sigmoid(x):
    return 1.0 / (1.0 + jnp.exp(-x))


_GELU_C = math.sqrt(2.0 / math.pi)
_GELU_K = 0.044715


def _gelu(x):
    return 0.5 * x * (1.0 + jnp.tanh(_GELU_C * (x + _GELU_K * x * x * x)))


def _gelu_grad(x):
    th = jnp.tanh(_GELU_C * (x + _GELU_K * x * x * x))
    return 0.5 * (1.0 + th) + 0.5 * x * (1.0 - th * th) * _GELU_C * (1.0 + 3.0 * _GELU_K * x * x)


def _glu_weight(ref):
    v = ref[...]
    return jnp.concatenate([v[:, :, :D_SSM], v[:, :, D_SSM:]], axis=1).reshape(D_SSM, D_SSM)


def _glu_pack(w):
    v = w.reshape(N_SHARD, 128, D_SSM)
    return jnp.concatenate([v[:, :64, :], v[:, 64:, :]], axis=2)


def _pool_diff(ext, row0, tm):
    rows = row0 + lax.broadcasted_iota(jnp.int32, (tm, 1), 0)
    outs = []
    for gi, w in enumerate(POOL_WINDOWS):
        e = ext[:, gi * POOL_GROUP:(gi + 1) * POOL_GROUP]
        s = e
        k = 1
        while k < w:
            s = s + pltpu.roll(s, k, 0)
            k *= 2
        inv = 1.0 / jnp.minimum(rows + 1, w).astype(F32)
        outs.append(s[POOL_HALO:, :] * inv - e[POOL_HALO:, :])
    return outs


def _mix_in_fwd(h, g1, wp, layer, w_pool, scale):
    L = h.shape[0]
    tm = min(TM, L)

    def body(h_ref, g_ref, w_ref, wp_ref, sc_ref, u_ref, yp_ref, carry):
        i = pl.program_id(0)

        @pl.when(i == 0)
        def _():
            carry[...] = jnp.zeros_like(carry)

        xhat, _ = _rms_hat(h_ref[...])
        n1 = (xhat * g_ref[...]).astype(BF16)
        u = _dot(n1, w_ref[...].reshape(D_MODEL, D_MODEL))
        u_ref[...] = u
        up = u[:, :D_POOL]
        ext = jnp.concatenate([carry[...], up], axis=0)
        carry[...] = up[tm - POOL_HALO:, :]
        diffs = _pool_diff(ext, i * tm, tm)
        for gi in range(4):
            cols = slice(gi * POOL_GROUP, (gi + 1) * POOL_GROUP)
            yp_ref[:, cols] = _dot(diffs[gi].astype(BF16), wp_ref[gi]) * sc_ref[:, cols]

    blk, idx = P_IN_BLK
    return pl.pallas_call(
        body, name="mix_in_fwd", grid=(L // tm,),
        in_specs=[pl.BlockSpec((tm, D_MODEL), lambda i: (i, 0)),
                  pl.BlockSpec((None, 1, D_MODEL), lambda i: (layer, 0, 0)),
                  pl.BlockSpec((N_SHARD, None, blk, D_MODEL), lambda i: (0, 0, idx, 0)),
                  pl.BlockSpec((None, 4, POOL_GROUP, POOL_GROUP), lambda i: (layer, 0, 0, 0)),
                  pl.BlockSpec((None, 1, D_POOL), lambda i: (layer, 0, 0))],
        out_specs=[pl.BlockSpec((tm, D_MODEL), lambda i: (i, 0)),
                   pl.BlockSpec((tm, D_POOL), lambda i: (i, 0))],
        out_shape=[jax.ShapeDtypeStruct((L, D_MODEL), F32), jax.ShapeDtypeStruct((L, D_POOL), F32)],
        scratch_shapes=[pltpu.VMEM((POOL_HALO, D_POOL), F32)],
        compiler_params=_cparams(1),
    )(h, g1, wp, w_pool, scale)


def _cmul(xr, xi, yr, yi):
    return xr * yr - xi * yi, xr * yi + xi * yr


SCAN_BLOCK = 64
N_SCAN_TABLES = 26


def _permute_rows(src, dst, n_rows):
    for b in range(n_rows // SCAN_BLOCK):
        for tau in range(SUBLANES):
            dst[pl.ds(SCAN_BLOCK * b + SUBLANES * tau, SUBLANES), :] = (
                src[pl.ds(SCAN_BLOCK * b + tau, SUBLANES, stride=SUBLANES), :])


def _scan_tables(ar, ai, tab, reverse):
    c = ar.shape[1]
    row = lax.broadcasted_iota(jnp.int32, (SUBLANES, c), 0)
    zero = jnp.zeros((SUBLANES, c), F32)
    full = lambda v: jnp.broadcast_to(v, (SUBLANES, c))
    pw = [(ar, ai)]
    for _ in range(SUBLANES - 1):
        pw.append(_cmul(*pw[-1], ar, ai))
    a8 = pw[-1]
    a16 = _cmul(*a8, *a8)
    a32 = _cmul(*a16, *a16)
    tab[0] = full(ar)
    tab[1] = full(ai)
    for n, (s, (pr, pi)) in enumerate(((1, a8), (2, a16), (4, a32))):
        keep = (row < SUBLANES - s) if reverse else (row >= s)
        tab[2 + 2 * n] = jnp.where(keep, pr, zero)
        tab[3 + 2 * n] = jnp.where(keep, pi, zero)
    cur = a8
    qr, qi = zero, zero
    for n in range(SUBLANES):
        at = (SUBLANES - 1 - n) if reverse else n
        qr = jnp.where(row == at, cur[0], qr)
        qi = jnp.where(row == at, cur[1], qi)
        cur = _cmul(*cur, *a8)
    tab[8] = qr
    tab[9] = qi
    for tau in range(SUBLANES):
        pr, pi = pw[SUBLANES - 1 - tau] if reverse else pw[tau]
        tab[10 + 2 * tau] = full(pr)
        tab[11 + 2 * tau] = full(pi)


def _cmac(xr, xi, ar, ai, yr, yi):
    return xr + ar * yr - ai * yi, xi + ar * yi + ai * yr


def _chain_segments(er, ei, c_r, c_i, tab, cols, reverse):
    tr, ti = er, ei
    for n, s in enumerate((1, 2, 4)):
        shift = SUBLANES - s if reverse else s
        tr, ti = _cmac(tr, ti, tab[2 + 2 * n, :, cols], tab[3 + 2 * n, :, cols],
                       pltpu.roll(tr, shift, 0), pltpu.roll(ti, shift, 0))
    return _cmac(tr, ti, tab[8, :, cols], tab[9, :, cols], c_r, c_i)


def _ssm_fwd(u, layer, bpad, cpad, ar, ai, dskip):
    L = u.shape[0]
    ts = min(TS, L)
    nq = 4
    cq = N_STATE // nq

    def body(u_ref, bp_ref, cp_ref, ar_ref, ai_ref, dsk_ref, sre_ref, sim_ref, y_ref, cr, ci, tab, up, yp):
        t = pl.program_id(1)

        @pl.when(t == 0)
        def _():
            cr[...] = jnp.zeros_like(cr)
            ci[...] = jnp.zeros_like(ci)
            _scan_tables(ar_ref[...], ai_ref[...], tab, reverse=False)

        _permute_rows(u_ref, up, ts)
        uf = up[...]
        ub = uf.astype(BF16)
        for jj in range(4):
            bu = _dot(ub, bp_ref[jj])
            sre_ref[:, jj * 128:(jj + 1) * 128] = bu[:, :128]
            sim_ref[:, jj * 128:(jj + 1) * 128] = bu[:, 128:]

        shp = (SUBLANES, SCAN_LANES)
        first_row = lax.broadcasted_iota(jnp.int32, shp, 0) == 0
        for cc in range(cq // SCAN_LANES):
            cols = slice(cc * SCAN_LANES, (cc + 1) * SCAN_LANES)

            def block(b, carry, cols=cols):
                c_r, c_i = carry
                base = pl.multiple_of(b * SCAN_BLOCK, SCAN_BLOCK)
                rows = lambda tau: pl.ds(base + SUBLANES * tau, SUBLANES)
                a_r, a_i = tab[0, :, cols], tab[1, :, cols]
                ys = [(sre_ref[rows(0), cols], sim_ref[rows(0), cols])]
                for tau in range(1, SUBLANES):
                    ys.append(_cmac(sre_ref[rows(tau), cols], sim_ref[rows(tau), cols], a_r, a_i, *ys[-1]))
                tr, ti = _chain_segments(*ys[-1], c_r, c_i, tab, cols, reverse=False)
                in_r = jnp.where(first_row, c_r, pltpu.roll(tr, 1, 0))
                in_i = jnp.where(first_row, c_i, pltpu.roll(ti, 1, 0))
                for tau in range(SUBLANES):
                    sr, si = _cmac(*ys[tau], tab[10 + 2 * tau, :, cols], tab[11 + 2 * tau, :, cols], in_r, in_i)
                    sre_ref[rows(tau), cols] = sr
                    sim_ref[rows(tau), cols] = si
                return (jnp.broadcast_to(tr[SUBLANES - 1:, :], shp), jnp.broadcast_to(ti[SUBLANES - 1:, :], shp))

            c_r, c_i = lax.fori_loop(0, ts // SCAN_BLOCK, block, (cr[:, cols], ci[:, cols]), unroll=2)
            cr[:, cols] = c_r
            ci[:, cols] = c_i

        acc = dsk_ref[...] * uf
        for jj in range(4):
            cols = slice(jj * 128, (jj + 1) * 128)
            scat = jnp.concatenate([sre_ref[:, cols], sim_ref[:, cols]], axis=1).astype(BF16)
            acc = acc + _dot(scat, cp_ref[jj])
        yp[...] = acc
        _permute_rows(yp, y_ref, ts)

    return pl.pallas_call(
        body, name="ssm_fwd", grid=(nq, L // ts),
        in_specs=[pl.BlockSpec((ts, 128), lambda q, t: (t, 4 + q)),
                  pl.BlockSpec((None, 4, 128, 256), lambda q, t: (layer, q, 0, 0)),
                  pl.BlockSpec((None, 4, 256, 128), lambda q, t: (layer, q, 0, 0)),
                  pl.BlockSpec((None, 1, cq), lambda q, t: (layer, 0, q)),
                  pl.BlockSpec((None, 1, cq), lambda q, t: (layer, 0, q)),
                  pl.BlockSpec((None, 1, 128), lambda q, t: (layer, 0, q))],
        out_specs=[pl.BlockSpec((ts, cq), lambda q, t: (t, q)),
                   pl.BlockSpec((ts, cq), lambda q, t: (t, q)),
                   pl.BlockSpec((ts, 128), lambda q, t: (t, q))],
        out_shape=[jax.ShapeDtypeStruct((L, N_STATE), F32), jax.ShapeDtypeStruct((L, N_STATE), F32),
                   jax.ShapeDtypeStruct((L, D_SSM), F32)],
        scratch_shapes=[pltpu.VMEM((SUBLANES, cq), F32), pltpu.VMEM((SUBLANES, cq), F32),
                        pltpu.VMEM((N_SCAN_TABLES, SUBLANES, cq), F32),
                        pltpu.VMEM((ts, 128), F32), pltpu.VMEM((ts, 128), F32)],
        compiler_params=_cparams(2),
    )(u, bpad, cpad, ar, ai, dskip)


def _mix_out_fwd(yraw, ypool, h, wp, layer, b_glu):
    L = h.shape[0]
    tm = min(TM, L)

    def body(yr_ref, yp_ref, h_ref, wglu_ref, b_ref, wout_ref, o_ref):
        y = _gelu(yr_ref[...])
        z = _dot(y.astype(BF16), _glu_weight(wglu_ref)) + b_ref[...]
        o = y * _sigmoid(z)
        mix = jnp.concatenate([yp_ref[...], o], axis=1).astype(BF16)
        o_ref[...] = h_ref[...] + _dot(mix, wout_ref[...].reshape(D_MODEL, D_MODEL))

    gb, gi = P_GLU_BLK
    ob, oi = P_OUT_BLK
    return pl.pallas_call(
        body, name="mix_out_fwd", grid=(L // tm,),
        in_specs=[pl.BlockSpec((tm, D_SSM), lambda i: (i, 0)),
                  pl.BlockSpec((tm, D_POOL), lambda i: (i, 0)),
                  pl.BlockSpec((tm, D_MODEL), lambda i: (i, 0)),
                  pl.BlockSpec((N_SHARD, None, gb, D_MODEL), lambda i: (0, 0, gi, 0)),
                  pl.BlockSpec((None, 1, D_SSM), lambda i: (layer, 0, 0)),
                  pl.BlockSpec((N_SHARD, None, ob, D_MODEL), lambda i: (0, 0, oi, 0))],
        out_specs=pl.BlockSpec((tm, D_MODEL), lambda i: (i, 0)),
        out_shape=jax.ShapeDtypeStruct((L, D_MODEL), F32),
        compiler_params=_cparams(1),
    )(yraw, ypool, h, wp, b_glu, wp)


def _ffn_weights(ref, k):
    return ref[k, 0:FF_SHARD, :], ref[k, FF_SHARD:2 * FF_SHARD, :], ref[k, 2 * FF_SHARD:P_FF_ROWS, :]


def _ffn_weight_spec():
    return pl.BlockSpec((N_SHARD, None, P_FF_ROWS, D_MODEL), lambda m, k: (0, 0, 0, 0),
                        pipeline_mode=pl.Buffered(1))


def _ffn_fwd(h, g2, wp, layer):
    L = h.shape[0]
    tm = min(TM_FFN_LONG, L)

    def body(h_ref, g_ref, w_ref, o_ref, n2_ref, act_ref, dgate_ref, dup_ref):
        k = pl.program_id(1)

        @pl.when(k == 0)
        def _():
            x = h_ref[...]
            xhat, _ = _rms_hat(x)
            n2_ref[...] = (xhat * g_ref[...]).astype(BF16)
            o_ref[...] = x

        wd, wg_t, wu_t = _ffn_weights(w_ref, k)
        n2 = n2_ref[...]
        gate = _dot_nt(n2, wg_t)
        up = _dot_nt(n2, wu_t)
        sg = _sigmoid(gate)
        silu = gate * sg
        act = (silu * up).astype(BF16)
        act_ref[...] = act
        dgate_ref[...] = (up * (sg * (1.0 + gate * (1.0 - sg)))).astype(BF16)
        dup_ref[...] = silu.astype(BF16)
        o_ref[...] += _dot(act, wd)

    act_shape = jax.ShapeDtypeStruct((N_SHARD, L, FF_SHARD), BF16)
    return pl.pallas_call(
        body, name="ffn_fwd", grid=(L // tm, N_SHARD),
        in_specs=[pl.BlockSpec((tm, D_MODEL), lambda m, k: (m, 0)),
                  pl.BlockSpec((None, 1, D_MODEL), lambda m, k: (layer, 0, 0)),
                  _ffn_weight_spec()],
        out_specs=[pl.BlockSpec((tm, D_MODEL), lambda m, k: (m, 0)),
                   pl.BlockSpec((tm, D_MODEL), lambda m, k: (m, 0)),
                   pl.BlockSpec((None, tm, FF_SHARD), lambda m, k: (k, m, 0)),
                   pl.BlockSpec((None, tm, FF_SHARD), lambda m, k: (k, m, 0)),
                   pl.BlockSpec((None, tm, FF_SHARD), lambda m, k: (k, m, 0))],
        out_shape=[jax.ShapeDtypeStruct((L, D_MODEL), F32), jax.ShapeDtypeStruct((L, D_MODEL), BF16),
                   act_shape, act_shape, act_shape],
        compiler_params=_cparams(2),
    )(h, g2, wp)


def _final_fwd_bwd(h, gf, target):
    L = h.shape[0]
    tm = min(TM, L)

    def body(h_ref, g_ref, t_ref, dh_ref, loss_ref, dg_ref):
        i = pl.program_id(0)

        @pl.when(i == 0)
        def _():
            loss_ref[...] = jnp.zeros_like(loss_ref)
            dg_ref[...] = jnp.zeros_like(dg_ref)

        xhat, r = _rms_hat(h_ref[...])
        g = g_ref[...]
        e = xhat * g - t_ref[...]
        loss_ref[...] += 0.5 * jnp.sum(jnp.mean(e * e, axis=-1, keepdims=True), axis=0, keepdims=True)
        dy = e * (1.0 / D_MODEL)
        dg_ref[...] += jnp.sum(dy * xhat, axis=0, keepdims=True)
        dh_ref[...] = _rms_bwd(dy * g, xhat, r)

    return pl.pallas_call(
        body, name="final_fwd_bwd", grid=(L // tm,),
        in_specs=[pl.BlockSpec((tm, D_MODEL), lambda i: (i, 0)),
                  pl.BlockSpec((1, D_MODEL), lambda i: (0, 0)),
                  pl.BlockSpec((tm, D_MODEL), lambda i: (i, 0))],
        out_specs=[pl.BlockSpec((tm, D_MODEL), lambda i: (i, 0)),
                   pl.BlockSpec((1, 1), lambda i: (0, 0)),
                   pl.BlockSpec((1, D_MODEL), lambda i: (0, 0))],
        out_shape=[jax.ShapeDtypeStruct((L, D_MODEL), F32), jax.ShapeDtypeStruct((1, 1), F32),
                   jax.ShapeDtypeStruct((1, D_MODEL), F32)],
        compiler_params=_cparams(1),
    )(h, gf, target)


def _ffn_bwd_act(dh, h, g2, fgate_s, fup_s, wp, layer):
    L = h.shape[0]
    tm = min(TM_FFN, L)
    sub = tm // FFN_SPLIT

    def body(dh_ref, h_ref, g_ref, fgate_ref, fup_ref, w_ref,
             dhm_ref, dg_ref, dgate_ref, dup_ref, dhb_ref):
        m, k = pl.program_id(0), pl.program_id(1)
        dn2 = dhm_ref

        @pl.when(jnp.logical_and(m == 0, k == 0))
        def _():
            dg_ref[...] = jnp.zeros_like(dg_ref)

        @pl.when(k == 0)
        def _():
            dhb_ref[...] = dh_ref[...].astype(BF16)
            dn2[...] = jnp.zeros_like(dn2)

        wd, wg_t, wu_t = _ffn_weights(w_ref, k)
        for rows in (slice(r * sub, (r + 1) * sub) for r in range(tm // sub)):
            dact = _dot_nt(dhb_ref[rows, :], wd)
            dgate = (dact * fgate_ref[rows, :].astype(F32)).astype(BF16)
            dup = (dact * fup_ref[rows, :].astype(F32)).astype(BF16)
            dgate_ref[rows, :] = dgate
            dup_ref[rows, :] = dup
            dn2[rows, :] += _dot(dgate, wg_t) + _dot(dup, wu_t)

        @pl.when(k == N_SHARD - 1)
        def _():
            xhat, r = _rms_hat(h_ref[...])
            d = dn2[...]
            dg_ref[...] += jnp.sum(d * xhat, axis=0, keepdims=True)
            dhm_ref[...] = dh_ref[...] + _rms_bwd(d * g_ref[...], xhat, r)

    act_spec = pl.BlockSpec((None, tm, FF_SHARD), lambda m, k: (k, m, 0))
    act_shape = jax.ShapeDtypeStruct((N_SHARD, L, FF_SHARD), BF16)
    row_spec = pl.BlockSpec((tm, D_MODEL), lambda m, k: (m, 0))
    return pl.pallas_call(
        body, name="ffn_bwd_act", grid=(L // tm, N_SHARD),
        in_specs=[row_spec, row_spec,
                  pl.BlockSpec((None, 1, D_MODEL), lambda m, k: (layer, 0, 0)),
                  act_spec, act_spec,
                  _ffn_weight_spec()],
        out_specs=[row_spec,
                   pl.BlockSpec((1, D_MODEL), lambda m, k: (0, 0)),
                   act_spec, act_spec, row_spec],
        out_shape=[jax.ShapeDtypeStruct((L, D_MODEL), F32), jax.ShapeDtypeStruct((1, D_MODEL), F32),
                   act_shape, act_shape, jax.ShapeDtypeStruct((L, D_MODEL), BF16)],
        compiler_params=_cparams(2),
    )(dh, h, g2, fgate_s, fup_s, wp)


def _ffn_bwd_w(n2, dgate_s, dup_s, act_s, dhb, gbuf):
    L = n2.shape[0]
    tm = min(TM_FFN_LONG, L)

    def body(n2_ref, dgate_ref, dup_ref, act_ref, dhb_ref, g_in, g_ref):
        m = pl.program_id(1)

        @pl.when(m == 0)
        def _():
            g_ref[...] = jnp.zeros_like(g_ref)

        n2v = n2_ref[...]
        g_ref[0:FF_SHARD, :] += _dot_tn(act_ref[...], dhb_ref[...])
        g_ref[FF_SHARD:2 * FF_SHARD, :] += _dot_tn(dgate_ref[...], n2v)
        g_ref[2 * FF_SHARD:P_FF_ROWS, :] += _dot_tn(dup_ref[...], n2v)

    act_spec = pl.BlockSpec((None, tm, FF_SHARD), lambda k, m: (k, m, 0))
    row_spec = pl.BlockSpec((tm, D_MODEL), lambda k, m: (m, 0))
    return pl.pallas_call(
        body, name="ffn_bwd_w", grid=(N_SHARD, L // tm),
        in_specs=[row_spec, act_spec, act_spec, act_spec, row_spec, pl.BlockSpec(memory_space=pl.ANY)],
        out_specs=pl.BlockSpec((None, None, P_FF_ROWS, D_MODEL), lambda k, m: (0, k, 0, 0)),
        out_shape=jax.ShapeDtypeStruct(gbuf.shape, F32),
        input_output_aliases={5: 0},
        compiler_params=_cparams(2),
    )(n2, dgate_s, dup_s, act_s, dhb, gbuf)


def _mix_out_bwd(dhm, yraw, ypool, wp, layer, b_glu, gbuf):
    L = dhm.shape[0]
    tm = min(TM, L)

    def body(dhm_ref, yr_ref, yp_ref, wglu_ref, b_ref, wout_ref, g1_in,
             dyr_ref, dyp_ref, db_ref, g1_ref, dwout, dwglu, gpack):
        i = pl.program_id(0)

        @pl.when(i == 0)
        def _():
            db_ref[...] = jnp.zeros_like(db_ref)
            dwout[...] = jnp.zeros_like(dwout)
            dwglu[...] = jnp.zeros_like(dwglu)

        dhb = dhm_ref[...].astype(BF16)
        wglu = _glu_weight(wglu_ref)
        dmix = _dot_nt(dhb, wout_ref[...].reshape(D_MODEL, D_MODEL))
        dyp_ref[...] = dmix[:, :D_POOL]
        d_o = dmix[:, D_POOL:]
        yraw_v = yr_ref[...]
        y = _gelu(yraw_v)
        yb = y.astype(BF16)
        sig = _sigmoid(_dot(yb, wglu) + b_ref[...])
        mix = jnp.concatenate([yp_ref[...], y * sig], axis=1).astype(BF16)
        dwout[...] += _dot_tn(mix, dhb).reshape(N_SHARD, 256, D_MODEL)
        dz = d_o * y * sig * (1.0 - sig)
        dzb = dz.astype(BF16)
        db_ref[...] += jnp.sum(dz, axis=0, keepdims=True)
        dwglu[...] += _dot_tn(yb, dzb)
        dy = d_o * sig + _dot_nt(dzb, wglu)
        dyr_ref[...] = dy * _gelu_grad(yraw_v)

        @pl.when(i == n_steps - 1)
        def _():
            gpack[:, :gb, :] = _glu_pack(dwglu[...])
            gpack[:, gb:, :] = jnp.zeros((N_SHARD, P_GLU_PAD - gb, D_MODEL), F32)
            pltpu.sync_copy(gpack, g1_ref.at[0, :, pl.ds(gb * gi, P_GLU_PAD), :])
            pltpu.sync_copy(dwout, g1_ref.at[0, :, pl.ds(ob * oi, ob), :])

    gb, gi = P_GLU_BLK
    ob, oi = P_OUT_BLK
    n_steps = L // tm
    return pl.pallas_call(
        body, name="mix_out_bwd", grid=(n_steps,),
        in_specs=[pl.BlockSpec((tm, D_MODEL), lambda i: (i, 0)),
                  pl.BlockSpec((tm, D_SSM), lambda i: (i, 0)),
                  pl.BlockSpec((tm, D_POOL), lambda i: (i, 0)),
                  pl.BlockSpec((N_SHARD, None, gb, D_MODEL), lambda i: (0, 0, gi, 0)),
                  pl.BlockSpec((None, 1, D_SSM), lambda i: (layer, 0, 0)),
                  pl.BlockSpec((N_SHARD, None, ob, D_MODEL), lambda i: (0, 0, oi, 0)),
                  pl.BlockSpec(memory_space=pl.ANY)],
        out_specs=[pl.BlockSpec((tm, D_SSM), lambda i: (i, 0)),
                   pl.BlockSpec((tm, D_POOL), lambda i: (i, 0)),
                   pl.BlockSpec((1, D_SSM), lambda i: (0, 0)),
                   pl.BlockSpec(memory_space=pl.ANY)],
        out_shape=[jax.ShapeDtypeStruct((L, D_SSM), F32), jax.ShapeDtypeStruct((L, D_POOL), F32),
                   jax.ShapeDtypeStruct((1, D_SSM), F32),
                   jax.ShapeDtypeStruct(gbuf.shape, F32)],
        scratch_shapes=[pltpu.VMEM((N_SHARD, ob, D_MODEL), F32), pltpu.VMEM((D_SSM, D_SSM), F32),
                        pltpu.VMEM((N_SHARD, P_GLU_PAD, D_MODEL), F32)],
        input_output_aliases={6: 3},
        compiler_params=_cparams(1),
    )(dhm, yraw, ypool, wp, b_glu, wp, gbuf)


def _ssm_bwd(dyraw, u, sre, sim, layer, cpad_t, bpad_t, ar, ai, dskip):
    L = u.shape[0]
    ts = min(TS, L)
    nt = L // ts
    nq = 4
    cq = N_STATE // nq

    def body(dy_ref, u_ref, sre_ref, sim_ref, ct_ref, bt_ref, ar_ref, ai_ref, dsk_ref,
             du_ref, dcp_ref, dbp_ref, dar_ref, dai_ref, ddsk_ref, gre, gim, cr, ci, tab, accr, acci, up, dyp):
        t = pl.program_id(1)

        @pl.when(t == 0)
        def _():
            for ref in (cr, ci, accr, acci, dcp_ref, dbp_ref, ddsk_ref):
                ref[...] = jnp.zeros_like(ref)
            _scan_tables(ar_ref[...], -ai_ref[...], tab, reverse=True)

        _permute_rows(dy_ref, dyp, ts)
        _permute_rows(u_ref, up, ts)
        dy = dyp[...]
        dyb = dy.astype(BF16)
        uf = up[...]
        ub = uf.astype(BF16)
        for jj in range(4):
            cols = slice(jj * 128, (jj + 1) * 128)
            ds = _dot(dyb, ct_ref[jj])
            gre[:, cols] = ds[:, :128]
            gim[:, cols] = ds[:, 128:]
            scat = jnp.concatenate([sre_ref[:, cols], sim_ref[:, cols]], axis=1).astype(BF16)
            dcp_ref[jj] += _dot_tn(scat, dyb)

        n_blk = ts // SCAN_BLOCK
        shp = (SUBLANES, SCAN_LANES)
        last_row = lax.broadcasted_iota(jnp.int32, shp, 0) == SUBLANES - 1
        for cc in range(cq // SCAN_LANES):
            cols = slice(cc * SCAN_LANES, (cc + 1) * SCAN_LANES)

            def block(i, carry, cols=cols):
                c_r, c_i, a_r, a_i = carry
                base = pl.multiple_of((n_blk - 1 - i) * SCAN_BLOCK, SCAN_BLOCK)
                rows = lambda tau: pl.ds(base + SUBLANES * tau, SUBLANES)
                m_r, m_i = tab[0, :, cols], tab[1, :, cols]
                ys = [None] * SUBLANES
                ys[SUBLANES - 1] = (gre[rows(SUBLANES - 1), cols], gim[rows(SUBLANES - 1), cols])
                for tau in reversed(range(SUBLANES - 1)):
                    ys[tau] = _cmac(gre[rows(tau), cols], gim[rows(tau), cols], m_r, m_i, *ys[tau + 1])
                tr, ti = _chain_segments(*ys[0], c_r, c_i, tab, cols, reverse=True)
                in_r = jnp.where(last_row, c_r, pltpu.roll(tr, SUBLANES - 1, 0))
                in_i = jnp.where(last_row, c_i, pltpu.roll(ti, SUBLANES - 1, 0))
                gs = [_cmac(*ys[tau], tab[10 + 2 * tau, :, cols], tab[11 + 2 * tau, :, cols], in_r, in_i)
                      for tau in range(SUBLANES)]
                for tau in range(SUBLANES):
                    gre[rows(tau), cols] = gs[tau][0]
                    gim[rows(tau), cols] = gs[tau][1]
                    if tau < SUBLANES - 1:
                        nr, ni = gs[tau + 1]
                    else:
                        nr = jnp.where(last_row, c_r, pltpu.roll(gs[0][0], SUBLANES - 1, 0))
                        ni = jnp.where(last_row, c_i, pltpu.roll(gs[0][1], SUBLANES - 1, 0))
                    sr, si = sre_ref[rows(tau), cols], sim_ref[rows(tau), cols]
                    a_r = a_r + sr * nr + si * ni
                    a_i = a_i + sr * ni - si * nr
                return (jnp.broadcast_to(tr[:1, :], shp), jnp.broadcast_to(ti[:1, :], shp), a_r, a_i)

            c_r, c_i, a_r, a_i = lax.fori_loop(
                0, n_blk, block, (cr[:, cols], ci[:, cols], accr[:, cols], acci[:, cols]), unroll=2)
            cr[:, cols] = c_r
            ci[:, cols] = c_i
            accr[:, cols] = a_r
            acci[:, cols] = a_i

        acc = dsk_ref[...] * dy
        for jj in range(4):
            cols = slice(jj * 128, (jj + 1) * 128)
            gcat = jnp.concatenate([gre[:, cols], gim[:, cols]], axis=1).astype(BF16)
            acc = acc + _dot(gcat, bt_ref[jj])
            dbp_ref[jj] += _dot_tn(ub, gcat)
        ddsk_ref[...] += jnp.sum(dy * uf, axis=0, keepdims=True)
        dyp[...] = acc
        _permute_rows(dyp, du_ref, ts)

        @pl.when(t == nt - 1)
        def _():
            dar_ref[...] = jnp.sum(accr[...], axis=0, keepdims=True)
            dai_ref[...] = jnp.sum(acci[...], axis=0, keepdims=True)

    f32_scr = lambda *s: pltpu.VMEM(s, F32)
    return pl.pallas_call(
        body, name="ssm_bwd", grid=(nq, nt),
        in_specs=[pl.BlockSpec((ts, 128), lambda q, t: (nt - 1 - t, q)),
                  pl.BlockSpec((ts, 128), lambda q, t: (nt - 1 - t, 4 + q)),
                  pl.BlockSpec((ts, cq), lambda q, t: (nt - 1 - t, q)),
                  pl.BlockSpec((ts, cq), lambda q, t: (nt - 1 - t, q)),
                  pl.BlockSpec((None, 4, 128, 256), lambda q, t: (layer, q, 0, 0)),
                  pl.BlockSpec((None, 4, 256, 128), lambda q, t: (layer, q, 0, 0)),
                  pl.BlockSpec((None, 1, cq), lambda q, t: (layer, 0, q)),
                  pl.BlockSpec((None, 1, cq), lambda q, t: (layer, 0, q)),
                  pl.BlockSpec((None, 1, 128), lambda q, t: (layer, 0, q))],
        out_specs=[pl.BlockSpec((ts, 128), lambda q, t: (nt - 1 - t, q)),
                   pl.BlockSpec((4, 256, 128), lambda q, t: (q, 0, 0)),
                   pl.BlockSpec((4, 128, 256), lambda q, t: (q, 0, 0)),
                   pl.BlockSpec((1, cq), lambda q, t: (0, q)),
                   pl.BlockSpec((1, cq), lambda q, t: (0, q)),
                   pl.BlockSpec((1, 128), lambda q, t: (0, q))],
        out_shape=[jax.ShapeDtypeStruct((L, D_SSM), F32),
                   jax.ShapeDtypeStruct((N_PAIRS, 256, 128), F32), jax.ShapeDtypeStruct((N_PAIRS, 128, 256), F32),
                   jax.ShapeDtypeStruct((1, N_STATE), F32), jax.ShapeDtypeStruct((1, N_STATE), F32),
                   jax.ShapeDtypeStruct((1, D_SSM), F32)],
        scratch_shapes=[f32_scr(ts, cq), f32_scr(ts, cq), f32_scr(SUBLANES, cq), f32_scr(SUBLANES, cq),
                        f32_scr(N_SCAN_TABLES, SUBLANES, cq), f32_scr(SUBLANES, cq), f32_scr(SUBLANES, cq),
                        f32_scr(ts, 128), f32_scr(ts, 128)],
        compiler_params=_cparams(2),
    )(dyraw, u, sre, sim, cpad_t, bpad_t, ar, ai, dskip)


def _pool_bwd(dyp, u, layer, w_pool, scale):
    L = u.shape[0]
    tm = min(TM, L)
    nt = L // tm
    halo_per_tile = tm // POOL_HALO

    def body(dyp_ref, u_ref, halo_ref, wp_ref, sc_ref, du_ref, dwp_ref, dsc_ref, carry):
        i = pl.program_id(0)
        tile = nt - 1 - i

        @pl.when(i == 0)
        def _():
            carry[...] = jnp.zeros_like(carry)
            dwp_ref[...] = jnp.zeros_like(dwp_ref)
            dsc_ref[...] = jnp.zeros_like(dsc_ref)

        up = u_ref[...]
        halo = jnp.where(tile > 0, halo_ref[...], jnp.zeros_like(halo_ref))
        diffs = _pool_diff(jnp.concatenate([halo, up], axis=0), tile * tm, tm)
        rows = tile * tm + lax.broadcasted_iota(jnp.int32, (tm, 1), 0)
        n_ext = tm + POOL_HALO
        for gi, w in enumerate(POOL_WINDOWS):
            cols = slice(gi * POOL_GROUP, (gi + 1) * POOL_GROUP)
            db = diffs[gi].astype(BF16)
            dyp = dyp_ref[:, cols]
            dsc_ref[:, cols] += jnp.sum(dyp * _dot(db, wp_ref[gi]), axis=0, keepdims=True)
            dp = (dyp * sc_ref[:, cols]).astype(BF16)
            ddiff = _dot_nt(dp, wp_ref[gi])
            dwp_ref[gi] += _dot_tn(db, dp)
            e = ddiff * (1.0 / jnp.minimum(rows + 1, w).astype(F32))
            s = jnp.concatenate([e, carry[:, cols]], axis=0)
            k = 1
            while k < w:
                s = s + pltpu.roll(s, n_ext - k, 0)
                k *= 2
            du_ref[:, cols] = s[:tm, :] - ddiff
            carry[:, cols] = e[:POOL_HALO, :]

    return pl.pallas_call(
        body, name="pool_bwd", grid=(nt,),
        in_specs=[pl.BlockSpec((tm, D_POOL), lambda i: (nt - 1 - i, 0)),
                  pl.BlockSpec((tm, D_POOL), lambda i: (nt - 1 - i, 0)),
                  pl.BlockSpec((POOL_HALO, D_POOL), lambda i: (jnp.maximum((nt - 1 - i) * halo_per_tile - 1, 0), 0)),
                  pl.BlockSpec((None, 4, POOL_GROUP, POOL_GROUP), lambda i: (layer, 0, 0, 0)),
                  pl.BlockSpec((None, 1, D_POOL), lambda i: (layer, 0, 0))],
        out_specs=[pl.BlockSpec((tm, D_POOL), lambda i: (nt - 1 - i, 0)),
                   pl.BlockSpec((4, POOL_GROUP, POOL_GROUP), lambda i: (0, 0, 0)),
                   pl.BlockSpec((1, D_POOL), lambda i: (0, 0))],
        out_shape=[jax.ShapeDtypeStruct((L, D_POOL), F32),
                   jax.ShapeDtypeStruct((4, POOL_GROUP, POOL_GROUP), F32),
                   jax.ShapeDtypeStruct((1, D_POOL), F32)],
        scratch_shapes=[pltpu.VMEM((POOL_HALO, D_POOL), F32)],
        compiler_params=_cparams(1),
    )(dyp, u, u, w_pool, scale)


def _mix_in_bwd(dup, dus, h, dhm, g1, wp, layer, gbuf):
    L = h.shape[0]
    tm = min(TM, L)
    n_steps = L // tm
    blk, idx = P_IN_BLK

    def body(dup_ref, dus_ref, h_ref, dhm_ref, g_ref, w_ref, g1_in, dh_ref, dg_ref, g1_ref, dwin):
        i = pl.program_id(0)

        @pl.when(i == 0)
        def _():
            dg_ref[...] = jnp.zeros_like(dg_ref)
            dwin[...] = jnp.zeros_like(dwin)

        du = jnp.concatenate([dup_ref[...], dus_ref[...]], axis=1).astype(BF16)
        dn1 = _dot_nt(du, w_ref[...].reshape(D_MODEL, D_MODEL))
        xhat, r = _rms_hat(h_ref[...])
        g = g_ref[...]
        n1 = (xhat * g).astype(BF16)
        dwin[...] += _dot_tn(n1, du).reshape(N_SHARD, blk, D_MODEL)
        dg_ref[...] += jnp.sum(dn1 * xhat, axis=0, keepdims=True)
        dh_ref[...] = dhm_ref[...] + _rms_bwd(dn1 * g, xhat, r)

        @pl.when(i == n_steps - 1)
        def _():
            pltpu.sync_copy(dwin, g1_ref.at[0, :, pl.ds(blk * idx, blk), :])

    row_spec = pl.BlockSpec((tm, D_MODEL), lambda i: (i, 0))
    half_spec = pl.BlockSpec((tm, D_POOL), lambda i: (i, 0))
    return pl.pallas_call(
        body, name="mix_in_bwd", grid=(n_steps,),
        in_specs=[half_spec, half_spec, row_spec, row_spec,
                  pl.BlockSpec((None, 1, D_MODEL), lambda i: (layer, 0, 0)),
                  pl.BlockSpec((N_SHARD, None, blk, D_MODEL), lambda i: (0, 0, idx, 0)),
                  pl.BlockSpec(memory_space=pl.ANY)],
        out_specs=[row_spec, pl.BlockSpec((1, D_MODEL), lambda i: (0, 0)), pl.BlockSpec(memory_space=pl.ANY)],
        out_shape=[jax.ShapeDtypeStruct((L, D_MODEL), F32), jax.ShapeDtypeStruct((1, D_MODEL), F32),
                   jax.ShapeDtypeStruct(gbuf.shape, F32)],
        scratch_shapes=[pltpu.VMEM((N_SHARD, blk, D_MODEL), F32)],
        input_output_aliases={6: 2},
        compiler_params=_cparams(1),
    )(dup, dus, h, dhm, g1, wp, gbuf)


def _disc_math(lr, li, ldt, br_t, bi_t):
    dt = jnp.exp(ldt)
    mag = jnp.exp(lr * dt)
    ang = li * dt
    ar = mag * jnp.cos(ang)
    ai = mag * jnp.sin(ang)
    den = lr * lr + li * li
    nr, ni = ar - 1.0, ai
    cr = (nr * lr + ni * li) / den
    ci = (ni * lr - nr * li) / den
    return ar, ai, cr * br_t - ci * bi_t, cr * bi_t + ci * br_t


def _disc_fwd(lr, li, ldt, br_t, bi_t):
    def body(lr_ref, li_ref, ldt_ref, br_ref, bi_ref, ar_ref, ai_ref, bbr_ref, bbi_ref):
        ar, ai, bbr, bbi = _disc_math(lr_ref[...], li_ref[...], ldt_ref[...], br_ref[...], bi_ref[...])
        ar_ref[...] = ar
        ai_ref[...] = ai
        bbr_ref[...] = bbr
        bbi_ref[...] = bbi

    shapes = [jax.ShapeDtypeStruct(a.shape, F32) for a in (lr, li, br_t, bi_t)]
    return pl.pallas_call(body, name="ssm_disc_fwd", out_shape=shapes,
                          compiler_params=pltpu.CompilerParams(vmem_limit_bytes=VMEM_LIMIT))(lr, li, ldt, br_t, bi_t)


def _disc_bwd(lr, li, ldt, br_t, bi_t, dar, dai, dbbr, dbbi):
    def body(lr_ref, li_ref, ldt_ref, br_ref, bi_ref, dar_ref, dai_ref, dbbr_ref, dbbi_ref,
             dlr_ref, dli_ref, dldt_ref, dbr_ref, dbi_ref):
        prim = (lr_ref[...], li_ref[...], ldt_ref[...], br_ref[...], bi_ref[...])
        _, pullback = jax.vjp(_disc_math, *prim)
        dlr, dli, dldt, dbr, dbi = pullback((dar_ref[...], dai_ref[...], dbbr_ref[...], dbbi_ref[...]))
        dlr_ref[...] = dlr
        dli_ref[...] = dli
        dldt_ref[...] = dldt
        dbr_ref[...] = dbr
        dbi_ref[...] = dbi

    shapes = [jax.ShapeDtypeStruct(a.shape, F32) for a in (lr, li, ldt, br_t, bi_t)]
    return pl.pallas_call(body, name="ssm_disc_bwd", out_shape=shapes,
                          compiler_params=pltpu.CompilerParams(vmem_limit_bytes=VMEM_LIMIT))(
        lr, li, ldt, br_t, bi_t, dar, dai, dbbr, dbbi)


def _pad_pairs(m_re, m_im):
    def blocks(m):
        v = m.transpose(0, 2, 1).reshape(N_PAIRS, 2, SSM_GROUP, SSM_STATE)
        return jnp.einsum("ab,jahp->jahbp", jnp.eye(2, dtype=m.dtype), v).reshape(N_PAIRS, 32, 128)
    both = jnp.concatenate([blocks(m_re), blocks(m_im)], axis=-1)
    place = jax.nn.one_hot(jnp.arange(N_PAIRS) % 4, 4, dtype=both.dtype)
    return jnp.einsum("jk,jrc->jkrc", place, both).reshape(N_PAIRS, 128, 256)


def _unpad_pairs(x):
    place = jax.nn.one_hot(jnp.arange(N_PAIRS) % 4, 4, dtype=x.dtype)
    both = jnp.einsum("jk,jkrc->jrc", place, x.reshape(N_PAIRS, 4, 32, 256))

    def unblock(v):
        v = v.reshape(N_PAIRS, 2, SSM_GROUP, 2, SSM_STATE)
        d = jnp.einsum("ab,jahbp->jahp", jnp.eye(2, dtype=x.dtype), v)
        return d.reshape(N_SSM_GROUPS, SSM_GROUP, SSM_STATE).transpose(0, 2, 1)
    return unblock(both[..., :128]), unblock(both[..., 128:])


def _adamw_math(w, g, m, v):
    m = ADAM_B1 * m + (1.0 - ADAM_B1) * g
    v = ADAM_B2 * v + (1.0 - ADAM_B2) * (g * g)
    m_hat = m / (1.0 - ADAM_B1 ** ADAM_STEP)
    v_hat = v / (1.0 - ADAM_B2 ** ADAM_STEP)
    delta = -ADAM_LR * (m_hat / (jnp.sqrt(v_hat) + ADAM_EPS) + ADAM_WD * w)
    return delta, m, v


def _adamw(name, layer, w, m, v, gbuf, g_block, g_row0, row_tile, outs=None, after=(), glu=False):
    nl, r, c = w.shape
    n_tiles = r // row_tile
    g_rows, g_cols = g_block
    g_tile = g_rows // n_tiles
    g_off = g_row0 // g_tile
    if outs is None:
        outs = [lax.empty(w.shape, F32) for _ in range(4)]

    def body(w_ref, m_ref, v_ref, g_ref, *rest):
        go_ref, d_ref, mo_ref, vo_ref = rest[-4:]
        g = g_ref[...]
        if glu:
            g = jnp.concatenate([g[:, :D_SSM], g[:, D_SSM:]], axis=0)
        delta, mn, vn = _adamw_math(w_ref[...], g, m_ref[...], v_ref[...])
        go_ref[...] = g
        d_ref[...] = delta
        mo_ref[...] = mn
        vo_ref[...] = vn

    w_spec = pl.BlockSpec((None, row_tile, c), lambda j: (layer, j, 0))
    shape = jax.ShapeDtypeStruct(w.shape, F32)
    return pl.pallas_call(
        body, name=name, grid=(n_tiles,),
        in_specs=[w_spec, w_spec, w_spec, pl.BlockSpec((None, g_tile, g_cols), lambda j: (0, g_off + j, 0))]
        + [_ANY] * (4 + len(after)),
        out_specs=[w_spec] * 4,
        out_shape=[shape] * 4,
        input_output_aliases={4: 0, 5: 1, 6: 2, 7: 3},
        compiler_params=_cparams(1),
    )(w, m, v, gbuf, *outs, *after)


def _pack_weights(ids, layer, w_in, w_glu, w_out, w_down, w_gate_t, w_up_t, after=()):
    gb, gi = P_GLU_BLK
    ib, ii = P_IN_BLK
    ob, oi = P_OUT_BLK

    def body(ids_ref, in_ref, glu_ref, out_ref, dn_ref, gate_ref, up_ref, *rest):
        p_ref = rest[-1]
        p_ref[0:FF_SHARD, :] = dn_ref[...].astype(BF16)
        p_ref[FF_SHARD:2 * FF_SHARD, :] = gate_ref[...].astype(BF16)
        p_ref[2 * FF_SHARD:P_FF_ROWS, :] = up_ref[...].astype(BF16)
        g = glu_ref[...]
        p_ref[gb * gi:gb * (gi + 1), :] = jnp.concatenate([g[:gb, :], g[gb:, :]], axis=1).astype(BF16)
        p_ref[gb * (gi + 1):ib * ii, :] = jnp.zeros((P_GLU_PAD - gb, D_MODEL), BF16)
        p_ref[ib * ii:ib * (ii + 1), :] = in_ref[...].astype(BF16)
        p_ref[ob * oi:ob * (oi + 1), :] = out_ref[...].astype(BF16)

    def spec(a):
        return pl.BlockSpec((None,) + a.shape[1:], lambda i, ids_ref: (layer, 0, 0))

    ins = (w_in, w_glu, w_out, w_down, w_gate_t, w_up_t)
    grid_spec = pltpu.PrefetchScalarGridSpec(
        num_scalar_prefetch=1, grid=(1,),
        in_specs=[spec(a) for a in ins] + [_ANY] * len(after),
        out_specs=pl.BlockSpec((None, None, P_ROWS, D_MODEL), lambda i, ids_ref: (ids_ref[1], 0, 0, 0)))
    return pl.pallas_call(
        body, name="pack_weights", grid_spec=grid_spec,
        out_shape=jax.ShapeDtypeStruct((N_SHARD, 1, P_ROWS, D_MODEL), BF16),
        compiler_params=_cparams(1),
    )(ids, *ins, *after)


MESH = pl.DeviceIdType.MESH
_ANY = pl.BlockSpec(memory_space=pl.ANY)
P_HALF = P_ROWS // 2
RS_ROW_TILE = 352


def _mesh_pos():
    return lax.axis_index("x"), lax.axis_index("y"), lax.axis_index("c")


def _other_chips(x, y):
    return [(1 - x, y), (x, 1 - y), (1 - x, 1 - y)]


def _remote(src, dst, send_sems, recv_sems, n, to):
    return pltpu.make_async_remote_copy(src_ref=src, dst_ref=dst, send_sem=send_sems.at[n],
                                        recv_sem=recv_sems.at[n], device_id=to, device_id_type=MESH)


_HBM = pl.BlockSpec(memory_space=pltpu.HBM)
_SEM = pl.BlockSpec(memory_space=pltpu.SEMAPHORE)
_EFFECT = pltpu.CompilerParams(has_side_effects=pltpu.SideEffectType.DATAFLOW_SIDE_EFFECTING)
_TOKEN = jax.ShapeDtypeStruct((8, 128), F32)


def _in_hbm(a):
    return pltpu.with_memory_space_constraint(a, pltpu.HBM)


def _ag_piece(ref, shard, half, rows):
    row0, n_rows = rows
    return ref.at[shard, :, pl.ds(row0 + half * (n_rows // 2), n_rows // 2), :]


def _ag_start(name, wp, after, row_ranges):
    n_sems = 3 * len(row_ranges)

    def body(w_ref, after_ref, send_sems, recv_sems, w_thru, token):
        x, y, c = _mesh_pos()
        for i, rows in enumerate(row_ranges):
            mine = _ag_piece(w_ref, 2 * x + y, c, rows)
            for j, (px, py) in enumerate(_other_chips(x, y)):
                _remote(mine, mine, send_sems, recv_sems, 3 * i + j, (px, py, c)).start()
        token[...] = jnp.zeros_like(token)

    return pl.pallas_call(
        body, name=name,
        out_shape=(pltpu.SemaphoreType.DMA((n_sems,)), pltpu.SemaphoreType.DMA((n_sems,)),
                   pltpu.HBM(wp.shape, wp.dtype), _TOKEN),
        in_specs=(_HBM, _ANY), out_specs=(_SEM, _SEM, _HBM, pl.BlockSpec(memory_space=pltpu.VMEM)),
        input_output_aliases={0: 2}, compiler_params=_EFFECT,
    )(_in_hbm(wp), after)


def _ag_wait(name, send_sems, recv_sems, wp, after, row_ranges):
    def body(w_ref, send_sems, recv_sems, *rest):
        x, y, c = _mesh_pos()
        for i, rows in enumerate(row_ranges):
            mine = _ag_piece(w_ref, 2 * x + y, c, rows)
            for j, (px, py) in enumerate(_other_chips(x, y)):
                landed = _ag_piece(w_ref, 2 * px + py, c, rows)
                cp = _remote(mine, landed, send_sems, recv_sems, 3 * i + j, (px, py, c))
                cp.wait_send()
                cp.wait_recv()

    return pl.pallas_call(
        body, name=name, out_shape=pltpu.HBM(wp.shape, wp.dtype),
        in_specs=(_HBM, _SEM, _SEM) + (_ANY,) * len(after), out_specs=_HBM,
        input_output_aliases={0: 0}, compiler_params=_EFFECT,
    )(wp, send_sems, recv_sems, *after)


def _ag_forward(wp, rows):
    def body(w_in, o, send_sems, recv_sems):
        x, y, c = _mesh_pos()
        sib = (x, y, 1 - c)
        chips = _other_chips(x, y)
        sends = []
        for j, (px, py) in enumerate(chips):
            landed = _ag_piece(o, 2 * px + py, c, rows)
            cp = _remote(landed, landed, send_sems, recv_sems, j, sib)
            cp.start()
            sends.append(cp)
        for j, (px, py) in enumerate(chips):
            passed = _ag_piece(o, 2 * px + py, 1 - c, rows)
            _remote(passed, passed, send_sems, recv_sems, j, sib).wait_recv()
        for cp in sends:
            cp.wait_send()

    return pl.pallas_call(
        body, name="ag_forward",
        in_specs=[_ANY], out_specs=_ANY,
        out_shape=jax.ShapeDtypeStruct(wp.shape, wp.dtype),
        scratch_shapes=[pltpu.SemaphoreType.DMA((3,)), pltpu.SemaphoreType.DMA((3,))],
        input_output_aliases={0: 0},
    )(wp)


def _ag_forward_start(name, wp, rows):
    def body(w_ref, send_sems, recv_sems, w_thru):
        x, y, c = _mesh_pos()
        for j, (px, py) in enumerate(_other_chips(x, y)):
            landed = _ag_piece(w_ref, 2 * px + py, c, rows)
            _remote(landed, landed, send_sems, recv_sems, j, (x, y, 1 - c)).start()

    return pl.pallas_call(
        body, name=name,
        out_shape=(pltpu.SemaphoreType.DMA((3,)), pltpu.SemaphoreType.DMA((3,)), pltpu.HBM(wp.shape, wp.dtype)),
        in_specs=(_HBM,), out_specs=(_SEM, _SEM, _HBM),
        input_output_aliases={0: 2}, compiler_params=_EFFECT,
    )(_in_hbm(wp))


def _ag_forward_wait(name, send_sems, recv_sems, wp, after, rows):
    def body(w_ref, send_sems, recv_sems, *rest):
        x, y, c = _mesh_pos()
        for j, (px, py) in enumerate(_other_chips(x, y)):
            cp = _remote(_ag_piece(w_ref, 2 * px + py, c, rows), _ag_piece(w_ref, 2 * px + py, 1 - c, rows),
                         send_sems, recv_sems, j, (x, y, 1 - c))
            cp.wait_send()
            cp.wait_recv()

    return pl.pallas_call(
        body, name=name, out_shape=pltpu.HBM(wp.shape, wp.dtype),
        in_specs=(_HBM, _SEM, _SEM) + (_ANY,) * len(after), out_specs=_HBM,
        input_output_aliases={0: 0}, compiler_params=_EFFECT,
    )(wp, send_sems, recv_sems, *after)


def _rs_chips_start(name, t):
    nl = t.shape[0]

    def body(t_ref, land_ref, send_sems, recv_sems, t_thru, land_thru, token):
        x, y, c = _mesh_pos()
        for j, (px, py) in enumerate(_other_chips(x, y)):
            _remote(t_ref.at[:, 2 * px + py], land_ref.at[j], send_sems, recv_sems, j, (px, py, c)).start()
        token[...] = jnp.zeros_like(token)

    land = lax.empty((3, nl, P_HALF, D_MODEL), BF16)
    return pl.pallas_call(
        body, name=name,
        out_shape=(pltpu.SemaphoreType.DMA((3,)), pltpu.SemaphoreType.DMA((3,)), pltpu.HBM(t.shape, t.dtype),
                   pltpu.HBM(land.shape, land.dtype), _TOKEN),
        in_specs=(_HBM, _HBM), out_specs=(_SEM, _SEM, _HBM, _HBM, pl.BlockSpec(memory_space=pltpu.VMEM)),
        input_output_aliases={0: 2, 1: 3}, compiler_params=_EFFECT,
    )(_in_hbm(t), _in_hbm(land))


def _rs_chips_wait(name, send_sems, recv_sems, t, land, after):
    def body(t_ref, land_ref, send_sems, recv_sems, *rest):
        x, y, c = _mesh_pos()
        for j, (px, py) in enumerate(_other_chips(x, y)):
            cp = _remote(t_ref.at[:, 2 * px + py], land_ref.at[j], send_sems, recv_sems, j, (px, py, c))
            cp.wait_send()
            cp.wait_recv()

    return pl.pallas_call(
        body, name=name, out_shape=(pltpu.HBM(t.shape, t.dtype), pltpu.HBM(land.shape, land.dtype)),
        in_specs=(_HBM, _HBM, _SEM, _SEM) + (_ANY,) * len(after), out_specs=(_HBM, _HBM),
        input_output_aliases={0: 0, 1: 1}, compiler_params=_EFFECT,
    )(t, land, send_sems, recv_sems, *after)[1]


def _rs_sibling_start(name, g):
    nl = g.shape[0]

    def body(g_ref, land_ref, send_sems, recv_sems, g_thru, land_thru, token):
        x, y, c = _mesh_pos()
        _remote(g_ref.at[:, :, pl.ds((1 - c) * P_HALF, P_HALF), :], land_ref, send_sems, recv_sems, 0,
                (x, y, 1 - c)).start()
        token[...] = jnp.zeros_like(token)

    land = lax.empty((nl, N_SHARD, P_HALF, D_MODEL), F32)
    return pl.pallas_call(
        body, name=name,
        out_shape=(pltpu.SemaphoreType.DMA((1,)), pltpu.SemaphoreType.DMA((1,)), pltpu.HBM(g.shape, g.dtype),
                   pltpu.HBM(land.shape, land.dtype), _TOKEN),
        in_specs=(_HBM, _HBM), out_specs=(_SEM, _SEM, _HBM, _HBM, pl.BlockSpec(memory_space=pltpu.VMEM)),
        input_output_aliases={0: 2, 1: 3}, compiler_params=_EFFECT,
    )(_in_hbm(g), _in_hbm(land))


def _rs_sibling_wait(name, send_sems, recv_sems, g, land, after):
    def body(g_ref, land_ref, send_sems, recv_sems, *rest):
        x, y, c = _mesh_pos()
        cp = _remote(g_ref.at[:, :, pl.ds((1 - c) * P_HALF, P_HALF), :], land_ref, send_sems, recv_sems, 0,
                     (x, y, 1 - c))
        cp.wait_send()
        cp.wait_recv()

    return pl.pallas_call(
        body, name=name, out_shape=(pltpu.HBM(g.shape, g.dtype), pltpu.HBM(land.shape, land.dtype)),
        in_specs=(_HBM, _HBM, _SEM, _SEM) + (_ANY,) * len(after), out_specs=(_HBM, _HBM),
        input_output_aliases={0: 0, 1: 1}, compiler_params=_EFFECT,
    )(g, land, send_sems, recv_sems, *after)


def _rs_add(name, ids, g, buf, row_tile):
    nl, _, hr, cols = buf.shape
    n_rt = hr // row_tile

    def body(ids_ref, g_ref, b_ref, own_ref, tb_ref):
        t = g_ref[...] + b_ref[...]
        tb_ref[...] = t.astype(BF16)

        @pl.when(pl.program_id(2) == ids_ref[1])
        def _():
            own_ref[...] = t

    blk = (None, None, row_tile, cols)
    grid_spec = pltpu.PrefetchScalarGridSpec(
        num_scalar_prefetch=1, grid=(nl, n_rt, N_SHARD),
        in_specs=[pl.BlockSpec(blk, lambda l, j, s, ids_ref: (l, s, ids_ref[0] * n_rt + j, 0)),
                  pl.BlockSpec(blk, lambda l, j, s, ids_ref: (l, s, j, 0))],
        out_specs=[pl.BlockSpec((None, row_tile, cols), lambda l, j, s, ids_ref: (l, j, 0)),
                   pl.BlockSpec(blk, lambda l, j, s, ids_ref: (l, s, j, 0))])
    return pl.pallas_call(
        body, name=name, grid_spec=grid_spec,
        out_shape=[jax.ShapeDtypeStruct((nl, hr, cols), F32), jax.ShapeDtypeStruct(buf.shape, BF16)],
        compiler_params=_cparams(3),
    )(ids, g, buf)


def _rs_sum(ids, layer, own, bufb, reduced, row_tile):
    _, hr, cols = own.shape
    n_rt = hr // row_tile

    def body(ids_ref, own_ref, b_ref, reduced_in, f_ref):
        f_ref[...] = ((own_ref[...] + b_ref[0].astype(F32)) + b_ref[1].astype(F32)) + b_ref[2].astype(F32)

    grid_spec = pltpu.PrefetchScalarGridSpec(
        num_scalar_prefetch=1, grid=(n_rt,),
        in_specs=[pl.BlockSpec((None, row_tile, cols), lambda j, ids_ref: (0, j, 0)),
                  pl.BlockSpec((3, None, row_tile, cols), lambda j, ids_ref: (0, 0, j, 0)),
                  pl.BlockSpec(memory_space=pl.ANY)],
        out_specs=pl.BlockSpec((None, row_tile, cols), lambda j, ids_ref: (layer, ids_ref[0] * n_rt + j, 0)))
    return pl.pallas_call(
        body, name="rs_sum", grid_spec=grid_spec,
        out_shape=jax.ShapeDtypeStruct(reduced.shape, F32),
        input_output_aliases={3: 0},
        compiler_params=_cparams(1),
    )(ids, own, bufb, reduced)


def _rs_exchange_start(name, f):
    def body(f_ref, send_sems, recv_sems, f_thru):
        x, y, c = _mesh_pos()
        mine = f_ref.at[:, pl.ds(c * P_HALF, P_HALF), :]
        _remote(mine, mine, send_sems, recv_sems, 0, (x, y, 1 - c)).start()

    return pl.pallas_call(
        body, name=name,
        out_shape=(pltpu.SemaphoreType.DMA((1,)), pltpu.SemaphoreType.DMA((1,)), pltpu.HBM(f.shape, f.dtype)),
        in_specs=(_HBM,), out_specs=(_SEM, _SEM, _HBM),
        input_output_aliases={0: 2}, compiler_params=_EFFECT,
    )(_in_hbm(f))


def _rs_exchange_wait(name, send_sems, recv_sems, f, after):
    def body(f_ref, send_sems, recv_sems, *rest):
        x, y, c = _mesh_pos()
        mine = f_ref.at[:, pl.ds(c * P_HALF, P_HALF), :]
        theirs = f_ref.at[:, pl.ds((1 - c) * P_HALF, P_HALF), :]
        cp = _remote(mine, theirs, send_sems, recv_sems, 0, (x, y, 1 - c))
        cp.wait_send()
        cp.wait_recv()

    return pl.pallas_call(
        body, name=name, out_shape=pltpu.HBM(f.shape, f.dtype),
        in_specs=(_HBM, _SEM, _SEM) + (_ANY,) * len(after), out_specs=_HBM,
        input_output_aliases={0: 0}, compiler_params=_EFFECT,
    )(f, send_sems, recv_sems, *after)


def _small_all_reduce(s, after=()):
    n_rows = s.shape[0]
    hr = n_rows // 2
    qr = hr // N_SHARD

    def body(s_ref, *rest):
        o_ref, sibbuf, tbuf, qbuf, fbuf, send_sems, recv_sems = rest[len(after):]
        x, y, c = _mesh_pos()
        k = 2 * x + y
        sib = (x, y, 1 - c)
        chips = _other_chips(x, y)
        mine = pl.ds(pl.multiple_of(c * hr, SUBLANES), hr)
        theirs = pl.ds(pl.multiple_of((1 - c) * hr, SUBLANES), hr)

        def quarter(shard):
            return pl.ds(pl.multiple_of(shard * qr, SUBLANES), qr)

        first = _remote(s_ref.at[theirs], sibbuf, send_sems, recv_sems, 0, sib)
        first.start()
        first.wait()
        tbuf[...] = s_ref[mine, :] + sibbuf[...]
        cps = []
        for j, (px, py) in enumerate(chips):
            cp = _remote(tbuf.at[quarter(2 * px + py)], qbuf.at[j], send_sems, recv_sems, 1 + j, (px, py, c))
            cp.start()
            cps.append(cp)
        for cp in cps:
            cp.wait()
        fbuf[quarter(k), :] = (tbuf[quarter(k), :] + qbuf[1]) + (qbuf[0] + qbuf[2])
        cps = []
        for j, (px, py) in enumerate(chips):
            cp = _remote(fbuf.at[quarter(k)], fbuf.at[quarter(k)], send_sems, recv_sems, 4 + j, (px, py, c))
            cp.start()
            cps.append(cp)
        for j, (px, py) in enumerate(chips):
            got = fbuf.at[quarter(2 * px + py)]
            _remote(got, got, send_sems, recv_sems, 4 + j, (px, py, c)).wait_recv()
        for cp in cps:
            cp.wait_send()
        o_ref[mine, :] = fbuf[...]
        last = _remote(fbuf, o_ref.at[mine], send_sems, recv_sems, 7, sib)
        last.start()
        last.wait()

    vmem = pl.BlockSpec(memory_space=pltpu.VMEM)
    return pl.pallas_call(
        body, name="small_all_reduce",
        in_specs=[vmem] + [_ANY] * len(after), out_specs=vmem,
        out_shape=jax.ShapeDtypeStruct(s.shape, F32),
        scratch_shapes=[pltpu.VMEM((hr, D_MODEL), F32), pltpu.VMEM((hr, D_MODEL), F32),
                        pltpu.VMEM((3, qr, D_MODEL), F32), pltpu.VMEM((hr, D_MODEL), F32),
                        pltpu.SemaphoreType.DMA((8,)), pltpu.SemaphoreType.DMA((8,))],
        compiler_params=pltpu.CompilerParams(vmem_limit_bytes=VMEM_LIMIT),
    )(s, *after)


_SMALL = ("norm_mix", "w_pool", "pool_scale", "lam_re", "lam_im", "log_dt", "b_re", "b_im", "c_re", "c_im",
          "d_skip", "b_glu", "norm_ffn", "norm_final")
_WEIGHTS = ("norm_mix", "w_in", "w_pool", "pool_scale", "lam_re", "lam_im", "log_dt", "b_re", "b_im", "c_re",
            "c_im", "d_skip", "w_glu", "b_glu", "w_out", "norm_ffn", "w_gate", "w_up", "w_down", "norm_final")


def _local_step(x, target, p, get_weights, get_ffn_weights, ffn_bwd_done, put_grads):
    nl = p["norm_mix"].shape[0]

    def tied(a, token):
        return a if token is None else a + token
    n_rows = nl * N_SSM_GROUPS
    lr = p["lam_re"].reshape(n_rows, 1, SSM_STATE)
    li = p["lam_im"].reshape(n_rows, 1, SSM_STATE)
    ldt = p["log_dt"].reshape(n_rows, 1, 1)
    br_t = p["b_re"].reshape(n_rows, SSM_STATE, SSM_GROUP).transpose(0, 2, 1)
    bi_t = p["b_im"].reshape(n_rows, SSM_STATE, SSM_GROUP).transpose(0, 2, 1)
    ar, ai, bbr_t, bbi_t = _disc_fwd(lr, li, ldt, br_t, bi_t)
    ar = ar.reshape(nl, 1, N_STATE)
    ai = ai.reshape(nl, 1, N_STATE)
    bbr = bbr_t.transpose(0, 2, 1).reshape(nl, N_SSM_GROUPS, SSM_STATE, SSM_GROUP)
    bbi = bbi_t.transpose(0, 2, 1).reshape(nl, N_SSM_GROUPS, SSM_STATE, SSM_GROUP)
    w_pool = p["w_pool"].astype(BF16)
    p = dict(p)
    for n in ("norm_mix", "pool_scale", "b_glu", "norm_ffn"):
        p[n] = p[n].reshape(nl, 1, -1)
    swap = lambda a: jnp.swapaxes(a, -1, -2)
    bpad = jax.vmap(_pad_pairs)(bbr, bbi).astype(BF16)
    cpad_t = jax.vmap(_pad_pairs)(swap(p["c_re"]), -swap(p["c_im"])).astype(BF16)
    bpad_t, cpad = swap(bpad), swap(cpad_t)
    dskip = p["d_skip"].reshape(nl, 1, D_SSM)

    layers = []
    h = x
    for l in range(nl):
        wp = get_weights(l, [h] if l else [h, bpad, cpad, bpad_t, cpad_t, ar, ai])
        u, ypool = _mix_in_fwd(h, p["norm_mix"], wp, l, w_pool, p["pool_scale"])
        sre, sim, yraw = _ssm_fwd(u, l, bpad, cpad, ar, ai, dskip)
        hm = _mix_out_fwd(yraw, ypool, h, wp, l, p["b_glu"])
        wp = get_ffn_weights(l, wp, [hm])
        h_next, n2, act_s, fgate_s, fup_s = _ffn_fwd(hm, p["norm_ffn"], wp, l)
        layers.append(dict(h=h, u=u, ypool=ypool, sre=sre, sim=sim, yraw=yraw, hm=hm, n2=n2, act_s=act_s, wp=wp,
                           fgate_s=fgate_s, fup_s=fup_s))
        h = h_next

    dh, loss, d_norm_final = _final_fwd_bwd(h, p["norm_final"].reshape(1, D_MODEL), target)

    raw = {n: [None] * nl for n in ("dg1", "dwp", "dsc", "dcp", "dbp", "ddsk", "db_glu", "dg2", "dar", "dai")}
    token = None
    for l in reversed(range(nl)):
        s = layers[l]
        wp = s["wp"]
        g1 = lax.empty((1, N_SHARD, P_ROWS, D_MODEL), F32)
        dhm, dg2, dgate_s, dup_s, dhb = _ffn_bwd_act(dh, s["hm"], tied(p["norm_ffn"], token), s["fgate_s"],
                                                      s["fup_s"], wp, l)
        g1 = _ffn_bwd_w(s["n2"], dgate_s, dup_s, s["act_s"], dhb, g1)
        token = ffn_bwd_done(l, [g1])
        dyraw, dyp, db_glu, g1 = _mix_out_bwd(dhm, s["yraw"], s["ypool"], wp, l, tied(p["b_glu"], token), g1)
        dus, dcp, dbp, dar, dai, ddsk = _ssm_bwd(dyraw, s["u"], s["sre"], s["sim"], l, cpad_t, bpad_t, ar, ai, dskip)
        dup, dwp, dsc = _pool_bwd(dyp, s["u"], l, w_pool, p["pool_scale"])
        dh, dg1, g1 = _mix_in_bwd(dup, dus, s["h"], dhm, p["norm_mix"], wp, l, g1)
        token = put_grads(l, g1)
        for n, a in (("dg1", dg1), ("dwp", dwp), ("dsc", dsc), ("dcp", dcp), ("dbp", dbp), ("ddsk", ddsk),
                     ("db_glu", db_glu), ("dg2", dg2), ("dar", dar), ("dai", dai)):
            raw[n][l] = a

    st = {n: jnp.stack(v) for n, v in raw.items()}
    dc_re, dc_im = jax.vmap(_unpad_pairs)(swap(st["dcp"]))
    dbbr, dbbi = jax.vmap(_unpad_pairs)(st["dbp"])
    rows = lambda a: a.reshape((n_rows,) + a.shape[2:])
    dlr, dli, dldt, dbr_t, dbi_t = _disc_bwd(lr, li, ldt, br_t, bi_t, st["dar"].reshape(n_rows, 1, SSM_STATE),
                                              st["dai"].reshape(n_rows, 1, SSM_STATE), rows(swap(dbbr)),
                                              rows(swap(dbbi)))
    small = {"norm_mix": st["dg1"][:, 0], "w_pool": st["dwp"], "pool_scale": st["dsc"][:, 0], "c_re": swap(dc_re),
             "c_im": -swap(dc_im), "d_skip": st["ddsk"].reshape(nl, N_SSM_GROUPS, SSM_GROUP),
             "b_glu": st["db_glu"][:, 0], "norm_ffn": st["dg2"][:, 0]}
    small["lam_re"] = dlr.reshape(nl, N_SSM_GROUPS, SSM_STATE)
    small["lam_im"] = dli.reshape(nl, N_SSM_GROUPS, SSM_STATE)
    small["log_dt"] = dldt.reshape(nl, N_SSM_GROUPS)
    small["b_re"] = dbr_t.reshape(nl, N_SSM_GROUPS, SSM_GROUP, SSM_STATE)
    small["b_im"] = dbi_t.reshape(nl, N_SSM_GROUPS, SSM_GROUP, SSM_STATE)
    small["d_skip"] = small["d_skip"].transpose(_SMALL_VIEW["d_skip"])
    small["norm_final"] = d_norm_final
    return loss, dh, small


_SMALL_VIEW = {"b_re": (0, 1, 3, 2), "b_im": (0, 1, 3, 2), "d_skip": (0, 2, 1)}
_SMALL_GROUPS = (("b_re", "b_im"), ("c_re", "c_im"), ("lam_re", "lam_im"), ("norm_mix", "norm_ffn"),
                 ("pool_scale", "b_glu"), ("w_pool",), ("log_dt",), ("d_skip",), ("norm_final",))


def _view(n, a):
    a = a.transpose(_SMALL_VIEW[n]) if n in _SMALL_VIEW else a
    return a[None] if a.ndim == 1 else a


def _unview(n, a, shape):
    a = a.reshape(shape) if len(shape) == 1 else a
    return a.transpose(_SMALL_VIEW[n]) if n in _SMALL_VIEW else a


def _flatten_small(views):
    flat = jnp.concatenate([views[n].reshape(-1) for n in _SMALL])
    n_rows = -(-flat.shape[0] // (64 * D_MODEL)) * 64
    return jnp.pad(flat, (0, n_rows * D_MODEL - flat.shape[0])).reshape(n_rows, D_MODEL)


def _split_small(flat, like):
    flat = flat.reshape(-1)
    out, at = {}, 0
    for n in _SMALL:
        size = like[n].size
        out[n] = flat[at:at + size].reshape(like[n].shape)
        at += size
    return out


def _adamw_small(name, ws, ms, vs, gs):
    k = len(ws)

    def body(*refs):
        ins, outs = refs[:4 * k], refs[4 * k:]
        for i in range(k):
            w, m, v, g = (ins[j * k + i][...] for j in range(4))
            delta, mn, vn = _adamw_math(w, g, m, v)
            outs[i][...] = delta
            outs[k + i][...] = mn
            outs[2 * k + i][...] = vn

    shapes = [jax.ShapeDtypeStruct(w.shape, F32) for w in ws] * 3
    outs = pl.pallas_call(body, name=name, out_shape=shapes,
                          compiler_params=pltpu.CompilerParams(vmem_limit_bytes=VMEM_LIMIT))(*ws, *ms, *vs, *gs)
    return outs[:k], outs[k:2 * k], outs[2 * k:]


def kernel(x, norm_mix, w_in, w_pool, pool_scale, lam_re, lam_im, log_dt, b_re, b_im, c_re, c_im, d_skip, w_glu, b_glu, w_out, norm_ffn, w_gate, w_up, w_down, norm_final, loss_target, m_norm_mix, m_w_in, m_w_pool, m_pool_scale, m_lam_re, m_lam_im, m_log_dt, m_b_re, m_b_im, m_c_re, m_c_im, m_d_skip, m_w_glu, m_b_glu, m_w_out, m_norm_ffn, m_w_gate, m_w_up, m_w_down, m_norm_final, v_norm_mix, v_w_in, v_w_pool, v_pool_scale, v_lam_re, v_lam_im, v_log_dt, v_b_re, v_b_im, v_c_re, v_c_im, v_d_skip, v_w_glu, v_b_glu, v_w_out, v_norm_ffn, v_w_gate, v_w_up, v_w_down, v_norm_final):
    given = dict(locals())
    w = {n: given[n] for n in _WEIGHTS}
    m = {n: given["m_" + n] for n in _WEIGHTS}
    v = {n: given["v_" + n] for n in _WEIGHTS}
    ids = jnp.stack([lax.axis_index("c"), 2 * lax.axis_index("x") + lax.axis_index("y")]).astype(jnp.int32)

    t_names = ("w_gate", "w_up")
    tr = lambda a: a.transpose(0, 2, 1)
    for d in (w, m, v):
        d.update({n: tr(d[n]) for n in t_names})

    nl = norm_mix.shape[0]
    mixer_rows, ffn_rows = (P_FF_ROWS, P_ROWS - P_FF_ROWS), (0, P_FF_ROWS)
    started, last = {}, None
    for l in range(nl):
        packed = _pack_weights(ids, l, w["w_in"], w["w_glu"], w["w_out"], w["w_down"], w["w_gate"], w["w_up"],
                               [] if last is None else [last])
        if l == 0:
            first = _ag_start("ag_start_0_mixer", packed, ids, [mixer_rows])
            started[0] = _ag_start("ag_start_0_ffn", first[2], first[3], [ffn_rows])
        else:
            started[l] = _ag_start(f"ag_start_{l}", packed, last, [mixer_rows, ffn_rows])
        last = started[l][3]
    views = [{n: _view(n, d[n]) for n in _SMALL} for d in (w, m, v)]

    passing = {}

    def get_weights(l, after):
        send_sems, recv_sems, buf, _ = started[l]
        if l == 0:
            buf = _ag_wait("ag_wait_0_mixer", first[0], first[1], buf, after + [last], [mixer_rows])
            return _ag_forward(buf, mixer_rows)
        buf = _ag_wait(f"ag_wait_{l}", send_sems, recv_sems, buf, after, [mixer_rows, ffn_rows])
        buf = _ag_forward(buf, mixer_rows)
        passing[l] = _ag_forward_start(f"ag_forward_start_{l}", buf, ffn_rows)
        return passing[l][2]

    def get_ffn_weights(l, buf, after):
        if l > 0:
            send_sems, recv_sems, _ = passing[l]
            return _ag_forward_wait(f"ag_forward_wait_{l}", send_sems, recv_sems, buf, after, ffn_rows)
        send_sems, recv_sems, _, _ = started[0]
        return _ag_forward(_ag_wait("ag_wait_0_ffn", send_sems, recv_sems, buf, after, [ffn_rows]), ffn_rows)

    to_sibling, to_chips, reduced = {}, {}, {}

    def put_grads(l, g):
        to_sibling[l] = _rs_sibling_start(f"rs_sibling_start_{l}", g)
        token = to_sibling[l][4]
        if l + 1 in to_chips:
            finish(l + 1, [token])
        return token[:1, :1]

    def ffn_bwd_done(l, after):
        return send_to_chips(l + 1, after)[:1, :1] if l + 1 in to_sibling else None

    def send_to_chips(l, after):
        send_sems, recv_sems, g, land, _ = to_sibling.pop(l)
        g, land = _rs_sibling_wait(f"rs_sibling_wait_{l}", send_sems, recv_sems, g, land, after)
        own, t = _rs_add("rs_add", ids, g, land, RS_ROW_TILE)
        send_sems, recv_sems, t, land, token = _rs_chips_start(f"rs_chips_start_{l}", t)
        to_chips[l] = (send_sems, recv_sems, t, land, own)
        return token

    def finish(l, after):
        send_sems, recv_sems, t, land, own = to_chips.pop(l)
        land = _rs_chips_wait(f"rs_chips_wait_{l}", send_sems, recv_sems, t, land, after)
        shard = lax.empty((1, P_ROWS, D_MODEL), F32)
        reduced[l] = _rs_exchange_start(f"rs_exchange_start_{l}", _rs_sum(ids, 0, own, land, shard, RS_ROW_TILE))

    loss, grad_x, small = _local_step(x[0], loss_target[0], {n: w[n] for n in _SMALL}, get_weights, get_ffn_weights,
                                      ffn_bwd_done, put_grads)
    loss = lax.psum(loss[0, 0], ("x", "y", "c"))
    small_flat = _flatten_small(small)
    token = send_to_chips(0, [small_flat])

    big = (("w_in", P_IN_BLK, 256, False), ("w_out", P_OUT_BLK, 256, False), ("w_down", P_WD_BLK, 352, False),
           ("w_gate", P_WG_BLK, 352, False), ("w_up", P_WU_BLK, 352, False), ("w_glu", P_GLU_BLK, 128, True))
    res = {n: None for n, *_ in big}

    def adamw_layer(l, after):
        send_sems, recv_sems, shard = reduced[l]
        shard = _rs_exchange_wait(f"rs_exchange_wait_{l}", send_sems, recv_sems, shard, after)
        for n, (blk, idx), row_tile, glu in big:
            res[n] = _adamw("adamw_" + n, l, w[n], m[n], v[n], shard, (blk, D_MODEL), blk * idx, row_tile, res[n], (), glu)

    for l in reversed(range(1, nl)):
        adamw_layer(l, [token])
    updated = [r[0] for r in res.values() if r is not None]
    small_sum = _small_all_reduce(small_flat, [token] + updated)
    finish(0, [small_sum] + updated)
    adamw_layer(0, [])
    for n in t_names:
        res[n] = tuple(tr(a) for a in res[n])
    g_views = _split_small(small_sum, views[0])
    for group in _SMALL_GROUPS:
        deltas, new_ms, new_vs = _adamw_small("adamw_" + group[0], *[[d[n] for n in group] for d in views],
                                              [g_views[n] for n in group])
        for i, n in enumerate(group):
            res[n] = tuple(_unview(n, a, w[n].shape) for a in (g_views[n], deltas[i], new_ms[i], new_vs[i]))

    return (loss, grad_x[None], *[res[n][0] for n in _WEIGHTS], *[res[n][1] for n in _WEIGHTS],
            *[res[n][2] for n in _WEIGHTS], *[res[n][3] for n in _WEIGHTS])
```

```python
import functools
import math

import jax
import jax.numpy as jnp
from jax import lax
from jax.experimental import pallas as pl
from jax.experimental.pallas import tpu as pltpu

F32 = jnp.float32
BF16 = jnp.bfloat16

D_MODEL = 1024
D_POOL = 512
D_SSM = 512
POOL_WINDOWS = (2, 4, 8, 16)
POOL_GROUP = 128
POOL_HALO = 16
N_SSM_GROUPS = 32
SSM_GROUP = 16
SSM_STATE = 64
N_STATE = N_SSM_GROUPS * SSM_STATE
N_PAIRS = N_SSM_GROUPS // 2
D_FF = 2816
N_SHARD = 4
FF_SHARD = D_FF // N_SHARD
RMS_EPS = 1e-6

ADAM_LR = 0.001
ADAM_B1 = 0.9
ADAM_B2 = 0.999
ADAM_EPS = 1e-08
ADAM_WD = 0.01
ADAM_STEP = 10

P_ROWS = 2816
P_WD_BLK = (704, 0)
P_WG_BLK = (704, 1)
P_WU_BLK = (704, 2)
P_FF_ROWS = 2112
P_GLU_BLK = (64, 33)
P_GLU_PAD = 192
P_IN_BLK = (256, 9)
P_OUT_BLK = (256, 10)

SUBLANES = 8
VMEM_LIMIT = 56 * 1024 * 1024

TM = 1024
TM_FFN = 512
TM_FFN_LONG = 1024
FFN_SPLIT = 2
TS = 2048
SCAN_LANES = 512


def _cparams(n_axes):
    return pltpu.CompilerParams(dimension_semantics=("arbitrary",) * n_axes, vmem_limit_bytes=VMEM_LIMIT)


def _dot(a, b):
    return jnp.dot(a, b, preferred_element_type=F32)


def _dot_nt(a, b):
    return lax.dot_general(a, b, (((1,), (1,)), ((), ())), preferred_element_type=F32)


def _dot_tn(a, b):
    return lax.dot_general(a, b, (((0,), (0,)), ((), ())), preferred_element_type=F32)


def _rms_hat(x):
    r = lax.rsqrt(jnp.mean(x * x, axis=-1, keepdims=True) + RMS_EPS)
    return x * r, r


def _rms_bwd(d_hat, xhat, r):
    return r * (d_hat - xhat * jnp.mean(d_hat * xhat, axis=-1, keepdims=True))


def _sigmoid(x):
    return 1.0 / (1.0 + jnp.exp(-x))


_GELU_C = math.sqrt(2.0 / math.pi)
_GELU_K = 0.044715


def _gelu(x):
    return 0.5 * x * (1.0 + jnp.tanh(_GELU_C * (x + _GELU_K * x * x * x)))


def _gelu_grad(x):
    th = jnp.tanh(_GELU_C * (x + _GELU_K * x * x * x))
    return 0.5 * (1.0 + th) + 0.5 * x * (1.0 - th * th) * _GELU_C * (1.0 + 3.0 * _GELU_K * x * x)


def _glu_weight(ref):
    v = ref[...]
    return jnp.concatenate([v[:, :, :D_SSM], v[:, :, D_SSM:]], axis=1).reshape(D_SSM, D_SSM)


def _glu_pack(w):
    v = w.reshape(N_SHARD, 128, D_SSM)
    return jnp.concatenate([v[:, :64, :], v[:, 64:, :]], axis=2)


def _pool_diff(ext, row0, tm):
    rows = row0 + lax.broadcasted_iota(jnp.int32, (tm, 1), 0)
    outs = []
    for gi, w in enumerate(POOL_WINDOWS):
        e = ext[:, gi * POOL_GROUP:(gi + 1) * POOL_GROUP]
        s = e
        k = 1
        while k < w:
            s = s + pltpu.roll(s, k, 0)
            k *= 2
        inv = 1.0 / jnp.minimum(rows + 1, w).astype(F32)
        outs.append(s[POOL_HALO:, :] * inv - e[POOL_HALO:, :])
    return outs


def _mix_in_fwd(h, g1, wp, layer, w_pool, scale):
    L = h.shape[0]
    tm = min(TM, L)

    def body(h_ref, g_ref, w_ref, wp_ref, sc_ref, u_ref, yp_ref, carry):
        i = pl.program_id(0)

        @pl.when(i == 0)
        def _():
            carry[...] = jnp.zeros_like(carry)

        xhat, _ = _rms_hat(h_ref[...])
        n1 = (xhat * g_ref[...]).astype(BF16)
        u = _dot(n1, w_ref[...].reshape(D_MODEL, D_MODEL))
        u_ref[...] = u
        up = u[:, :D_POOL]
        ext = jnp.concatenate([carry[...], up], axis=0)
        carry[...] = up[tm - POOL_HALO:, :]
        diffs = _pool_diff(ext, i * tm, tm)
        for gi in range(4):
            cols = slice(gi * POOL_GROUP, (gi + 1) * POOL_GROUP)
            yp_ref[:, cols] = _dot(diffs[gi].astype(BF16), wp_ref[gi]) * sc_ref[:, cols]

    blk, idx = P_IN_BLK
    return pl.pallas_call(
        body, name="mix_in_fwd", grid=(L // tm,),
        in_specs=[pl.BlockSpec((tm, D_MODEL), lambda i: (i, 0)),
                  pl.BlockSpec((None, 1, D_MODEL), lambda i: (layer, 0, 0)),
                  pl.BlockSpec((N_SHARD, None, blk, D_MODEL), lambda i: (0, 0, idx, 0)),
                  pl.BlockSpec((None, 4, POOL_GROUP, POOL_GROUP), lambda i: (layer, 0, 0, 0)),
                  pl.BlockSpec((None, 1, D_POOL), lambda i: (layer, 0, 0))],
        out_specs=[pl.BlockSpec((tm, D_MODEL), lambda i: (i, 0)),
                   pl.BlockSpec((tm, D_POOL), lambda i: (i, 0))],
        out_shape=[jax.ShapeDtypeStruct((L, D_MODEL), F32), jax.ShapeDtypeStruct((L, D_POOL), F32)],
        scratch_shapes=[pltpu.VMEM((POOL_HALO, D_POOL), F32)],
        compiler_params=_cparams(1),
    )(h, g1, wp, w_pool, scale)


def _cmul(xr, xi, yr, yi):
    return xr * yr - xi * yi, xr * yi + xi * yr


SCAN_BLOCK = 64
N_SCAN_TABLES = 26


def _permute_rows(src, dst, n_rows):
    for b in range(n_rows // SCAN_BLOCK):
        for tau in range(SUBLANES):
            dst[pl.ds(SCAN_BLOCK * b + SUBLANES * tau, SUBLANES), :] = (
                src[pl.ds(SCAN_BLOCK * b + tau, SUBLANES, stride=SUBLANES), :])


def _scan_tables(ar, ai, tab, reverse):
    c = ar.shape[1]
    row = lax.broadcasted_iota(jnp.int32, (SUBLANES, c), 0)
    zero = jnp.zeros((SUBLANES, c), F32)
    full = lambda v: jnp.broadcast_to(v, (SUBLANES, c))
    pw = [(ar, ai)]
    for _ in range(SUBLANES - 1):
        pw.append(_cmul(*pw[-1], ar, ai))
    a8 = pw[-1]
    a16 = _cmul(*a8, *a8)
    a32 = _cmul(*a16, *a16)
    tab[0] = full(ar)
    tab[1] = full(ai)
    for n, (s, (pr, pi)) in enumerate(((1, a8), (2, a16), (4, a32))):
        keep = (row < SUBLANES - s) if reverse else (row >= s)
        tab[2 + 2 * n] = jnp.where(keep, pr, zero)
        tab[3 + 2 * n] = jnp.where(keep, pi, zero)
    cur = a8
    qr, qi = zero, zero
    for n in range(SUBLANES):
        at = (SUBLANES - 1 - n) if reverse else n
        qr = jnp.where(row == at, cur[0], qr)
        qi = jnp.where(row == at, cur[1], qi)
        cur = _cmul(*cur, *a8)
    tab[8] = qr
    tab[9] = qi
    for tau in range(SUBLANES):
        pr, pi = pw[SUBLANES - 1 - tau] if reverse else pw[tau]
        tab[10 + 2 * tau] = full(pr)
        tab[11 + 2 * tau] = full(pi)


def _cmac(xr, xi, ar, ai, yr, yi):
    return xr + ar * yr - ai * yi, xi + ar * yi + ai * yr


def _chain_segments(er, ei, c_r, c_i, tab, cols, reverse):
    tr, ti = er, ei
    for n, s in enumerate((1, 2, 4)):
        shift = SUBLANES - s if reverse else s
        tr, ti = _cmac(tr, ti, tab[2 + 2 * n, :, cols], tab[3 + 2 * n, :, cols],
                       pltpu.roll(tr, shift, 0), pltpu.roll(ti, shift, 0))
    return _cmac(tr, ti, tab[8, :, cols], tab[9, :, cols], c_r, c_i)


def _ssm_fwd(u, layer, bpad, cpad, ar, ai, dskip):
    L = u.shape[0]
    ts = min(TS, L)
    nq = 4
    cq = N_STATE // nq

    def body(u_ref, bp_ref, cp_ref, ar_ref, ai_ref, dsk_ref, sre_ref, sim_ref, y_ref, cr, ci, tab, up, yp):
        t = pl.program_id(1)

        @pl.when(t == 0)
        def _():
            cr[...] = jnp.zeros_like(cr)
            ci[...] = jnp.zeros_like(ci)
            _scan_tables(ar_ref[...], ai_ref[...], tab, reverse=False)

        _permute_rows(u_ref, up, ts)
        uf = up[...]
        ub = uf.astype(BF16)
        for jj in range(4):
            bu = _dot(ub, bp_ref[jj])
            sre_ref[:, jj * 128:(jj + 1) * 128] = bu[:, :128]
            sim_ref[:, jj * 128:(jj + 1) * 128] = bu[:, 128:]

        shp = (SUBLANES, SCAN_LANES)
        first_row = lax.broadcasted_iota(jnp.int32, shp, 0) == 0
        for cc in range(cq // SCAN_LANES):
            cols = slice(cc * SCAN_LANES, (cc + 1) * SCAN_LANES)

            def block(b, carry, cols=cols):
                c_r, c_i = carry
                base = pl.multiple_of(b * SCAN_BLOCK, SCAN_BLOCK)
                rows = lambda tau: pl.ds(base + SUBLANES * tau, SUBLANES)
                a_r, a_i = tab[0, :, cols], tab[1, :, cols]
                ys = [(sre_ref[rows(0), cols], sim_ref[rows(0), cols])]
                for tau in range(1, SUBLANES):
                    ys.append(_cmac(sre_ref[rows(tau), cols], sim_ref[rows(tau), cols], a_r, a_i, *ys[-1]))
                tr, ti = _chain_segments(*ys[-1], c_r, c_i, tab, cols, reverse=False)
                in_r = jnp.where(first_row, c_r, pltpu.roll(tr, 1, 0))
                in_i = jnp.where(first_row, c_i, pltpu.roll(ti, 1, 0))
                for tau in range(SUBLANES):
                    sr, si = _cmac(*ys[tau], tab[10 + 2 * tau, :, cols], tab[11 + 2 * tau, :, cols], in_r, in_i)
                    sre_ref[rows(tau), cols] = sr
                    sim_ref[rows(tau), cols] = si
                return (jnp.broadcast_to(tr[SUBLANES - 1:, :], shp), jnp.broadcast_to(ti[SUBLANES - 1:, :], shp))

            c_r, c_i = lax.fori_loop(0, ts // SCAN_BLOCK, block, (cr[:, cols], ci[:, cols]), unroll=2)
            cr[:, cols] = c_r
            ci[:, cols] = c_i

        acc = dsk_ref[...] * uf
        for jj in range(4):
            cols = slice(jj * 128, (jj + 1) * 128)
            scat = jnp.concatenate([sre_ref[:, cols], sim_ref[:, cols]], axis=1).astype(BF16)
            acc = acc + _dot(scat, cp_ref[jj])
        yp[...] = acc
        _permute_rows(yp, y_ref, ts)

    return pl.pallas_call(
        body, name="ssm_fwd", grid=(nq, L // ts),
        in_specs=[pl.BlockSpec((ts, 128), lambda q, t: (t, 4 + q)),
                  pl.BlockSpec((None, 4, 128, 256), lambda q, t: (layer, q, 0, 0)),
                  pl.BlockSpec((None, 4, 256, 128), lambda q, t: (layer, q, 0, 0)),
                  pl.BlockSpec((None, 1, cq), lambda q, t: (layer, 0, q)),
                  pl.BlockSpec((None, 1, cq), lambda q, t: (layer, 0, q)),
                  pl.BlockSpec((None, 1, 128), lambda q, t: (layer, 0, q))],
        out_specs=[pl.BlockSpec((ts, cq), lambda q, t: (t, q)),
                   pl.BlockSpec((ts, cq), lambda q, t: (t, q)),
                   pl.BlockSpec((ts, 128), lambda q, t: (t, q))],
        out_shape=[jax.ShapeDtypeStruct((L, N_STATE), F32), jax.ShapeDtypeStruct((L, N_STATE), F32),
                   jax.ShapeDtypeStruct((L, D_SSM), F32)],
        scratch_shapes=[pltpu.VMEM((SUBLANES, cq), F32), pltpu.VMEM((SUBLANES, cq), F32),
                        pltpu.VMEM((N_SCAN_TABLES, SUBLANES, cq), F32),
                        pltpu.VMEM((ts, 128), F32), pltpu.VMEM((ts, 128), F32)],
        compiler_params=_cparams(2),
    )(u, bpad, cpad, ar, ai, dskip)


def _mix_out_fwd(yraw, ypool, h, wp, layer, b_glu):
    L = h.shape[0]
    tm = min(TM, L)

    def body(yr_ref, yp_ref, h_ref, wglu_ref, b_ref, wout_ref, o_ref):
        y = _gelu(yr_ref[...])
        z = _dot(y.astype(BF16), _glu_weight(wglu_ref)) + b_ref[...]
        o = y * _sigmoid(z)
        mix = jnp.concatenate([yp_ref[...], o], axis=1).astype(BF16)
        o_ref[...] = h_ref[...] + _dot(mix, wout_ref[...].reshape(D_MODEL, D_MODEL))

    gb, gi = P_GLU_BLK
    ob, oi = P_OUT_BLK
    return pl.pallas_call(
        body, name="mix_out_fwd", grid=(L // tm,),
        in_specs=[pl.BlockSpec((tm, D_SSM), lambda i: (i, 0)),
                  pl.BlockSpec((tm, D_POOL), lambda i: (i, 0)),
                  pl.BlockSpec((tm, D_MODEL), lambda i: (i, 0)),
                  pl.BlockSpec((N_SHARD, None, gb, D_MODEL), lambda i: (0, 0, gi, 0)),
                  pl.BlockSpec((None, 1, D_SSM), lambda i: (layer, 0, 0)),
                  pl.BlockSpec((N_SHARD, None, ob, D_MODEL), lambda i: (0, 0, oi, 0))],
        out_specs=pl.BlockSpec((tm, D_MODEL), lambda i: (i, 0)),
        out_shape=jax.ShapeDtypeStruct((L, D_MODEL), F32),
        compiler_params=_cparams(1),
    )(yraw, ypool, h, wp, b_glu, wp)


def _ffn_weights(ref, k):
    return ref[k, 0:FF_SHARD, :], ref[k, FF_SHARD:2 * FF_SHARD, :], ref[k, 2 * FF_SHARD:P_FF_ROWS, :]


def _ffn_weight_spec():
    return pl.BlockSpec((N_SHARD, None, P_FF_ROWS, D_MODEL), lambda m, k: (0, 0, 0, 0),
                        pipeline_mode=pl.Buffered(1))


def _ffn_fwd(h, g2, wp, layer):
    L = h.shape[0]
    tm = min(TM_FFN_LONG, L)

    def body(h_ref, g_ref, w_ref, o_ref, n2_ref, act_ref, dgate_ref, dup_ref):
        k = pl.program_id(1)

        @pl.when(k == 0)
        def _():
            x = h_ref[...]
            xhat, _ = _rms_hat(x)
            n2_ref[...] = (xhat * g_ref[...]).astype(BF16)
            o_ref[...] = x

        wd, wg_t, wu_t = _ffn_weights(w_ref, k)
        n2 = n2_ref[...]
        gate = _dot_nt(n2, wg_t)
        up = _dot_nt(n2, wu_t)
        sg = _sigmoid(gate)
        silu = gate * sg
        act = (silu * up).astype(BF16)
        act_ref[...] = act
        dgate_ref[...] = (up * (sg * (1.0 + gate * (1.0 - sg)))).astype(BF16)
        dup_ref[...] = silu.astype(BF16)
        o_ref[...] += _dot(act, wd)

    act_shape = jax.ShapeDtypeStruct((N_SHARD, L, FF_SHARD), BF16)
    return pl.pallas_call(
        body, name="ffn_fwd", grid=(L // tm, N_SHARD),
        in_specs=[pl.BlockSpec((tm, D_MODEL), lambda m, k: (m, 0)),
                  pl.BlockSpec((None, 1, D_MODEL), lambda m, k: (layer, 0, 0)),
                  _ffn_weight_spec()],
        out_specs=[pl.BlockSpec((tm, D_MODEL), lambda m, k: (m, 0)),
                   pl.BlockSpec((tm, D_MODEL), lambda m, k: (m, 0)),
                   pl.BlockSpec((None, tm, FF_SHARD), lambda m, k: (k, m, 0)),
                   pl.BlockSpec((None, tm, FF_SHARD), lambda m, k: (k, m, 0)),
                   pl.BlockSpec((None, tm, FF_SHARD), lambda m, k: (k, m, 0))],
        out_shape=[jax.ShapeDtypeStruct((L, D_MODEL), F32), jax.ShapeDtypeStruct((L, D_MODEL), BF16),
                   act_shape, act_shape, act_shape],
        compiler_params=_cparams(2),
    )(h, g2, wp)


def _final_fwd_bwd(h, gf, target):
    L = h.shape[0]
    tm = min(TM, L)

    def body(h_ref, g_ref, t_ref, dh_ref, loss_ref, dg_ref):
        i = pl.program_id(0)

        @pl.when(i == 0)
        def _():
            loss_ref[...] = jnp.zeros_like(loss_ref)
            dg_ref[...] = jnp.zeros_like(dg_ref)

        xhat, r = _rms_hat(h_ref[...])
        g = g_ref[...]
        e = xhat * g - t_ref[...]
        loss_ref[...] += 0.5 * jnp.sum(jnp.mean(e * e, axis=-1, keepdims=True), axis=0, keepdims=True)
        dy = e * (1.0 / D_MODEL)
        dg_ref[...] += jnp.sum(dy * xhat, axis=0, keepdims=True)
        dh_ref[...] = _rms_bwd(dy * g, xhat, r)

    return pl.pallas_call(
        body, name="final_fwd_bwd", grid=(L // tm,),
        in_specs=[pl.BlockSpec((tm, D_MODEL), lambda i: (i, 0)),
                  pl.BlockSpec((1, D_MODEL), lambda i: (0, 0)),
                  pl.BlockSpec((tm, D_MODEL), lambda i: (i, 0))],
        out_specs=[pl.BlockSpec((tm, D_MODEL), lambda i: (i, 0)),
                   pl.BlockSpec((1, 1), lambda i: (0, 0)),
                   pl.BlockSpec((1, D_MODEL), lambda i: (0, 0))],
        out_shape=[jax.ShapeDtypeStruct((L, D_MODEL), F32), jax.ShapeDtypeStruct((1, 1), F32),
                   jax.ShapeDtypeStruct((1, D_MODEL), F32)],
        compiler_params=_cparams(1),
    )(h, gf, target)


def _ffn_bwd_act(dh, h, g2, fgate_s, fup_s, wp, layer):
    L = h.shape[0]
    tm = min(TM_FFN, L)
    sub = tm // FFN_SPLIT

    def body(dh_ref, h_ref, g_ref, fgate_ref, fup_ref, w_ref,
             dhm_ref, dg_ref, dgate_ref, dup_ref, dhb_ref):
        m, k = pl.program_id(0), pl.program_id(1)
        dn2 = dhm_ref

        @pl.when(jnp.logical_and(m == 0, k == 0))
        def _():
            dg_ref[...] = jnp.zeros_like(dg_ref)

        @pl.when(k == 0)
        def _():
            dhb_ref[...] = dh_ref[...].astype(BF16)
            dn2[...] = jnp.zeros_like(dn2)

        wd, wg_t, wu_t = _ffn_weights(w_ref, k)
        for rows in (slice(r * sub, (r + 1) * sub) for r in range(tm // sub)):
            dact = _dot_nt(dhb_ref[rows, :], wd)
            dgate = (dact * fgate_ref[rows, :].astype(F32)).astype(BF16)
            dup = (dact * fup_ref[rows, :].astype(F32)).astype(BF16)
            dgate_ref[rows, :] = dgate
            dup_ref[rows, :] = dup
            dn2[rows, :] += _dot(dgate, wg_t) + _dot(dup, wu_t)

        @pl.when(k == N_SHARD - 1)
        def _():
            xhat, r = _rms_hat(h_ref[...])
            d = dn2[...]
            dg_ref[...] += jnp.sum(d * xhat, axis=0, keepdims=True)
            dhm_ref[...] = dh_ref[...] + _rms_bwd(d * g_ref[...], xhat, r)

    act_spec = pl.BlockSpec((None, tm, FF_SHARD), lambda m, k: (k, m, 0))
    act_shape = jax.ShapeDtypeStruct((N_SHARD, L, FF_SHARD), BF16)
    row_spec = pl.BlockSpec((tm, D_MODEL), lambda m, k: (m, 0))
    return pl.pallas_call(
        body, name="ffn_bwd_act", grid=(L // tm, N_SHARD),
        in_specs=[row_spec, row_spec,
                  pl.BlockSpec((None, 1, D_MODEL), lambda m, k: (layer, 0, 0)),
                  act_spec, act_spec,
                  _ffn_weight_spec()],
        out_specs=[row_spec,
                   pl.BlockSpec((1, D_MODEL), lambda m, k: (0, 0)),
                   act_spec, act_spec, row_spec],
        out_shape=[jax.ShapeDtypeStruct((L, D_MODEL), F32), jax.ShapeDtypeStruct((1, D_MODEL), F32),
                   act_shape, act_shape, jax.ShapeDtypeStruct((L, D_MODEL), BF16)],
        compiler_params=_cparams(2),
    )(dh, h, g2, fgate_s, fup_s, wp)


def _ffn_bwd_w(n2, dgate_s, dup_s, act_s, dhb, gbuf):
    L = n2.shape[0]
    tm = min(TM_FFN_LONG, L)

    def body(n2_ref, dgate_ref, dup_ref, act_ref, dhb_ref, g_in, g_ref):
        m = pl.program_id(1)

        @pl.when(m == 0)
        def _():
            g_ref[...] = jnp.zeros_like(g_ref)

        n2v = n2_ref[...]
        g_ref[0:FF_SHARD, :] += _dot_tn(act_ref[...], dhb_ref[...])
        g_ref[FF_SHARD:2 * FF_SHARD, :] += _dot_tn(dgate_ref[...], n2v)
        g_ref[2 * FF_SHARD:P_FF_ROWS, :] += _dot_tn(dup_ref[...], n2v)

    act_spec = pl.BlockSpec((None, tm, FF_SHARD), lambda k, m: (k, m, 0))
    row_spec = pl.BlockSpec((tm, D_MODEL), lambda k, m: (m, 0))
    return pl.pallas_call(
        body, name="ffn_bwd_w", grid=(N_SHARD, L // tm),
        in_specs=[row_spec, act_spec, act_spec, act_spec, row_spec, pl.BlockSpec(memory_space=pl.ANY)],
        out_specs=pl.BlockSpec((None, None, P_FF_ROWS, D_MODEL), lambda k, m: (0, k, 0, 0)),
        out_shape=jax.ShapeDtypeStruct(gbuf.shape, F32),
        input_output_aliases={5: 0},
        compiler_params=_cparams(2),
    )(n2, dgate_s, dup_s, act_s, dhb, gbuf)


def _mix_out_bwd(dhm, yraw, ypool, wp, layer, b_glu, gbuf):
    L = dhm.shape[0]
    tm = min(TM, L)

    def body(dhm_ref, yr_ref, yp_ref, wglu_ref, b_ref, wout_ref, g1_in,
             dyr_ref, dyp_ref, db_ref, g1_ref, dwout, dwglu, gpack):
        i = pl.program_id(0)

        @pl.when(i == 0)
        def _():
            db_ref[...] = jnp.zeros_like(db_ref)
            dwout[...] = jnp.zeros_like(dwout)
            dwglu[...] = jnp.zeros_like(dwglu)

        dhb = dhm_ref[...].astype(BF16)
        wglu = _glu_weight(wglu_ref)
        dmix = _dot_nt(dhb, wout_ref[...].reshape(D_MODEL, D_MODEL))
        dyp_ref[...] = dmix[:, :D_POOL]
        d_o = dmix[:, D_POOL:]
        yraw_v = yr_ref[...]
        y = _gelu(yraw_v)
        yb = y.astype(BF16)
        sig = _sigmoid(_dot(yb, wglu) + b_ref[...])
        mix = jnp.concatenate([yp_ref[...], y * sig], axis=1).astype(BF16)
        dwout[...] += _dot_tn(mix, dhb).reshape(N_SHARD, 256, D_MODEL)
        dz = d_o * y * sig * (1.0 - sig)
        dzb = dz.astype(BF16)
        db_ref[...] += jnp.sum(dz, axis=0, keepdims=True)
        dwglu[...] += _dot_tn(yb, dzb)
        dy = d_o * sig + _dot_nt(dzb, wglu)
        dyr_ref[...] = dy * _gelu_grad(yraw_v)

        @pl.when(i == n_steps - 1)
        def _():
            gpack[:, :gb, :] = _glu_pack(dwglu[...])
            gpack[:, gb:, :] = jnp.zeros((N_SHARD, P_GLU_PAD - gb, D_MODEL), F32)
            pltpu.sync_copy(gpack, g1_ref.at[0, :, pl.ds(gb * gi, P_GLU_PAD), :])
            pltpu.sync_copy(dwout, g1_ref.at[0, :, pl.ds(ob * oi, ob), :])

    gb, gi = P_GLU_BLK
    ob, oi = P_OUT_BLK
    n_steps = L // tm
    return pl.pallas_call(
        body, name="mix_out_bwd", grid=(n_steps,),
        in_specs=[pl.BlockSpec((tm, D_MODEL), lambda i: (i, 0)),
                  pl.BlockSpec((tm, D_SSM), lambda i: (i, 0)),
                  pl.BlockSpec((tm, D_POOL), lambda i: (i, 0)),
                  pl.BlockSpec((N_SHARD, None, gb, D_MODEL), lambda i: (0, 0, gi, 0)),
                  pl.BlockSpec((None, 1, D_SSM), lambda i: (layer, 0, 0)),
                  pl.BlockSpec((N_SHARD, None, ob, D_MODEL), lambda i: (0, 0, oi, 0)),
                  pl.BlockSpec(memory_space=pl.ANY)],
        out_specs=[pl.BlockSpec((tm, D_SSM), lambda i: (i, 0)),
                   pl.BlockSpec((tm, D_POOL), lambda i: (i, 0)),
                   pl.BlockSpec((1, D_SSM), lambda i: (0, 0)),
                   pl.BlockSpec(memory_space=pl.ANY)],
        out_shape=[jax.ShapeDtypeStruct((L, D_SSM), F32), jax.ShapeDtypeStruct((L, D_POOL), F32),
                   jax.ShapeDtypeStruct((1, D_SSM), F32),
                   jax.ShapeDtypeStruct(gbuf.shape, F32)],
        scratch_shapes=[pltpu.VMEM((N_SHARD, ob, D_MODEL), F32), pltpu.VMEM((D_SSM, D_SSM), F32),
                        pltpu.VMEM((N_SHARD, P_GLU_PAD, D_MODEL), F32)],
        input_output_aliases={6: 3},
        compiler_params=_cparams(1),
    )(dhm, yraw, ypool, wp, b_glu, wp, gbuf)


def _ssm_bwd(dyraw, u, sre, sim, layer, cpad_t, bpad_t, ar, ai, dskip):
    L = u.shape[0]
    ts = min(TS, L)
    nt = L // ts
    nq = 4
    cq = N_STATE // nq

    def body(dy_ref, u_ref, sre_ref, sim_ref, ct_ref, bt_ref, ar_ref, ai_ref, dsk_ref,
             du_ref, dcp_ref, dbp_ref, dar_ref, dai_ref, ddsk_ref, gre, gim, cr, ci, tab, accr, acci, up, dyp):
        t = pl.program_id(1)

        @pl.when(t == 0)
        def _():
            for ref in (cr, ci, accr, acci, dcp_ref, dbp_ref, ddsk_ref):
                ref[...] = jnp.zeros_like(ref)
            _scan_tables(ar_ref[...], -ai_ref[...], tab, reverse=True)

        _permute_rows(dy_ref, dyp, ts)
        _permute_rows(u_ref, up, ts)
        dy = dyp[...]
        dyb = dy.astype(BF16)
        uf = up[...]
        ub = uf.astype(BF16)
        for jj in range(4):
            cols = slice(jj * 128, (jj + 1) * 128)
            ds = _dot(dyb, ct_ref[jj])
            gre[:, cols] = ds[:, :128]
            gim[:, cols] = ds[:, 128:]
            scat = jnp.concatenate([sre_ref[:, cols], sim_ref[:, cols]], axis=1).astype(BF16)
            dcp_ref[jj] += _dot_tn(scat, dyb)

        n_blk = ts // SCAN_BLOCK
        shp = (SUBLANES, SCAN_LANES)
        last_row = lax.broadcasted_iota(jnp.int32, shp, 0) == SUBLANES - 1
        for cc in range(cq // SCAN_LANES):
            cols = slice(cc * SCAN_LANES, (cc + 1) * SCAN_LANES)

            def block(i, carry, cols=cols):
                c_r, c_i, a_r, a_i = carry
                base = pl.multiple_of((n_blk - 1 - i) * SCAN_BLOCK, SCAN_BLOCK)
                rows = lambda tau: pl.ds(base + SUBLANES * tau, SUBLANES)
                m_r, m_i = tab[0, :, cols], tab[1, :, cols]
                ys = [None] * SUBLANES
                ys[SUBLANES - 1] = (gre[rows(SUBLANES - 1), cols], gim[rows(SUBLANES - 1), cols])
                for tau in reversed(range(SUBLANES - 1)):
                    ys[tau] = _cmac(gre[rows(tau), cols], gim[rows(tau), cols], m_r, m_i, *ys[tau + 1])
                tr, ti = _chain_segments(*ys[0], c_r, c_i, tab, cols, reverse=True)
                in_r = jnp.where(last_row, c_r, pltpu.roll(tr, SUBLANES - 1, 0))
                in_i = jnp.where(last_row, c_i, pltpu.roll(ti, SUBLANES - 1, 0))
                gs = [_cmac(*ys[tau], tab[10 + 2 * tau, :, cols], tab[11 + 2 * tau, :, cols], in_r, in_i)
                      for tau in range(SUBLANES)]
                for tau in range(SUBLANES):
                    gre[rows(tau), cols] = gs[tau][0]
                    gim[rows(tau), cols] = gs[tau][1]
                    if tau < SUBLANES - 1:
                        nr, ni = gs[tau + 1]
                    else:
                        nr = jnp.where(last_row, c_r, pltpu.roll(gs[0][0], SUBLANES - 1, 0))
                        ni = jnp.where(last_row, c_i, pltpu.roll(gs[0][1], SUBLANES - 1, 0))
                    sr, si = sre_ref[rows(tau), cols], sim_ref[rows(tau), cols]
                    a_r = a_r + sr * nr + si * ni
                    a_i = a_i + sr * ni - si * nr
                return (jnp.broadcast_to(tr[:1, :], shp), jnp.broadcast_to(ti[:1, :], shp), a_r, a_i)

            c_r, c_i, a_r, a_i = lax.fori_loop(
                0, n_blk, block, (cr[:, cols], ci[:, cols], accr[:, cols], acci[:, cols]), unroll=2)
            cr[:, cols] = c_r
            ci[:, cols] = c_i
            accr[:, cols] = a_r
            acci[:, cols] = a_i

        acc = dsk_ref[...] * dy
        for jj in range(4):
            cols = slice(jj * 128, (jj + 1) * 128)
            gcat = jnp.concatenate([gre[:, cols], gim[:, cols]], axis=1).astype(BF16)
            acc = acc + _dot(gcat, bt_ref[jj])
            dbp_ref[jj] += _dot_tn(ub, gcat)
        ddsk_ref[...] += jnp.sum(dy * uf, axis=0, keepdims=True)
        dyp[...] = acc
        _permute_rows(dyp, du_ref, ts)

        @pl.when(t == nt - 1)
        def _():
            dar_ref[...] = jnp.sum(accr[...], axis=0, keepdims=True)
            dai_ref[...] = jnp.sum(acci[...], axis=0, keepdims=True)

    f32_scr = lambda *s: pltpu.VMEM(s, F32)
    return pl.pallas_call(
        body, name="ssm_bwd", grid=(nq, nt),
        in_specs=[pl.BlockSpec((ts, 128), lambda q, t: (nt - 1 - t, q)),
                  pl.BlockSpec((ts, 128), lambda q, t: (nt - 1 - t, 4 + q)),
                  pl.BlockSpec((ts, cq), lambda q, t: (nt - 1 - t, q)),
                  pl.BlockSpec((ts, cq), lambda q, t: (nt - 1 - t, q)),
                  pl.BlockSpec((None, 4, 128, 256), lambda q, t: (layer, q, 0, 0)),
                  pl.BlockSpec((None, 4, 256, 128), lambda q, t: (layer, q, 0, 0)),
                  pl.BlockSpec((None, 1, cq), lambda q, t: (layer, 0, q)),
                  pl.BlockSpec((None, 1, cq), lambda q, t: (layer, 0, q)),
                  pl.BlockSpec((None, 1, 128), lambda q, t: (layer, 0, q))],
        out_specs=[pl.BlockSpec((ts, 128), lambda q, t: (nt - 1 - t, q)),
                   pl.BlockSpec((4, 256, 128), lambda q, t: (q, 0, 0)),
                   pl.BlockSpec((4, 128, 256), lambda q, t: (q, 0, 0)),
                   pl.BlockSpec((1, cq), lambda q, t: (0, q)),
                   pl.BlockSpec((1, cq), lambda q, t: (0, q)),
                   pl.BlockSpec((1, 128), lambda q, t: (0, q))],
        out_shape=[jax.ShapeDtypeStruct((L, D_SSM), F32),
                   jax.ShapeDtypeStruct((N_PAIRS, 256, 128), F32), jax.ShapeDtypeStruct((N_PAIRS, 128, 256), F32),
                   jax.ShapeDtypeStruct((1, N_STATE), F32), jax.ShapeDtypeStruct((1, N_STATE), F32),
                   jax.ShapeDtypeStruct((1, D_SSM), F32)],
        scratch_shapes=[f32_scr(ts, cq), f32_scr(ts, cq), f32_scr(SUBLANES, cq), f32_scr(SUBLANES, cq),
                        f32_scr(N_SCAN_TABLES, SUBLANES, cq), f32_scr(SUBLANES, cq), f32_scr(SUBLANES, cq),
                        f32_scr(ts, 128), f32_scr(ts, 128)],
        compiler_params=_cparams(2),
    )(dyraw, u, sre, sim, cpad_t, bpad_t, ar, ai, dskip)


def _pool_bwd(dyp, u, layer, w_pool, scale):
    L = u.shape[0]
    tm = min(TM, L)
    nt = L // tm
    halo_per_tile = tm // POOL_HALO

    def body(dyp_ref, u_ref, halo_ref, wp_ref, sc_ref, du_ref, dwp_ref, dsc_ref, carry):
        i = pl.program_id(0)
        tile = nt - 1 - i

        @pl.when(i == 0)
        def _():
            carry[...] = jnp.zeros_like(carry)
            dwp_ref[...] = jnp.zeros_like(dwp_ref)
            dsc_ref[...] = jnp.zeros_like(dsc_ref)

        up = u_ref[...]
        halo = jnp.where(tile > 0, halo_ref[...], jnp.zeros_like(halo_ref))
        diffs = _pool_diff(jnp.concatenate([halo, up], axis=0), tile * tm, tm)
        rows = tile * tm + lax.broadcasted_iota(jnp.int32, (tm, 1), 0)
        n_ext = tm + POOL_HALO
        for gi, w in enumerate(POOL_WINDOWS):
            cols = slice(gi * POOL_GROUP, (gi + 1) * POOL_GROUP)
            db = diffs[gi].astype(BF16)
            dyp = dyp_ref[:, cols]
            dsc_ref[:, cols] += jnp.sum(dyp * _dot(db, wp_ref[gi]), axis=0, keepdims=True)
            dp = (dyp * sc_ref[:, cols]).astype(BF16)
            ddiff = _dot_nt(dp, wp_ref[gi])
            dwp_ref[gi] += _dot_tn(db, dp)
            e = ddiff * (1.0 / jnp.minimum(rows + 1, w).astype(F32))
            s = jnp.concatenate([e, carry[:, cols]], axis=0)
            k = 1
            while k < w:
                s = s + pltpu.roll(s, n_ext - k, 0)
                k *= 2
            du_ref[:, cols] = s[:tm, :] - ddiff
            carry[:, cols] = e[:POOL_HALO, :]

    return pl.pallas_call(
        body, name="pool_bwd", grid=(nt,),
        in_specs=[pl.BlockSpec((tm, D_POOL), lambda i: (nt - 1 - i, 0)),
                  pl.BlockSpec((tm, D_POOL), lambda i: (nt - 1 - i, 0)),
                  pl.BlockSpec((POOL_HALO, D_POOL), lambda i: (jnp.maximum((nt - 1 - i) * halo_per_tile - 1, 0), 0)),
                  pl.BlockSpec((None, 4, POOL_GROUP, POOL_GROUP), lambda i: (layer, 0, 0, 0)),
                  pl.BlockSpec((None, 1, D_POOL), lambda i: (layer, 0, 0))],
        out_specs=[pl.BlockSpec((tm, D_POOL), lambda i: (nt - 1 - i, 0)),
                   pl.BlockSpec((4, POOL_GROUP, POOL_GROUP), lambda i: (0, 0, 0)),
                   pl.BlockSpec((1, D_POOL), lambda i: (0, 0))],
        out_shape=[jax.ShapeDtypeStruct((L, D_POOL), F32),
                   jax.ShapeDtypeStruct((4, POOL_GROUP, POOL_GROUP), F32),
                   jax.ShapeDtypeStruct((1, D_POOL), F32)],
        scratch_shapes=[pltpu.VMEM((POOL_HALO, D_POOL), F32)],
        compiler_params=_cparams(1),
    )(dyp, u, u, w_pool, scale)


def _mix_in_bwd(dup, dus, h, dhm, g1, wp, layer, gbuf):
    L = h.shape[0]
    tm = min(TM, L)
    n_steps = L // tm
    blk, idx = P_IN_BLK

    def body(dup_ref, dus_ref, h_ref, dhm_ref, g_ref, w_ref, g1_in, dh_ref, dg_ref, g1_ref, dwin):
        i = pl.program_id(0)

        @pl.when(i == 0)
        def _():
            dg_ref[...] = jnp.zeros_like(dg_ref)
            dwin[...] = jnp.zeros_like(dwin)

        du = jnp.concatenate([dup_ref[...], dus_ref[...]], axis=1).astype(BF16)
        dn1 = _dot_nt(du, w_ref[...].reshape(D_MODEL, D_MODEL))
        xhat, r = _rms_hat(h_ref[...])
        g = g_ref[...]
        n1 = (xhat * g).astype(BF16)
        dwin[...] += _dot_tn(n1, du).reshape(N_SHARD, blk, D_MODEL)
        dg_ref[...] += jnp.sum(dn1 * xhat, axis=0, keepdims=True)
        dh_ref[...] = dhm_ref[...] + _rms_bwd(dn1 * g, xhat, r)

        @pl.when(i == n_steps - 1)
        def _():
            pltpu.sync_copy(dwin, g1_ref.at[0, :, pl.ds(blk * idx, blk), :])

    row_spec = pl.BlockSpec((tm, D_MODEL), lambda i: (i, 0))
    half_spec = pl.BlockSpec((tm, D_POOL), lambda i: (i, 0))
    return pl.pallas_call(
        body, name="mix_in_bwd", grid=(n_steps,),
        in_specs=[half_spec, half_spec, row_spec, row_spec,
                  pl.BlockSpec((None, 1, D_MODEL), lambda i: (layer, 0, 0)),
                  pl.BlockSpec((N_SHARD, None, blk, D_MODEL), lambda i: (0, 0, idx, 0)),
                  pl.BlockSpec(memory_space=pl.ANY)],
        out_specs=[row_spec, pl.BlockSpec((1, D_MODEL), lambda i: (0, 0)), pl.BlockSpec(memory_space=pl.ANY)],
        out_shape=[jax.ShapeDtypeStruct((L, D_MODEL), F32), jax.ShapeDtypeStruct((1, D_MODEL), F32),
                   jax.ShapeDtypeStruct(gbuf.shape, F32)],
        scratch_shapes=[pltpu.VMEM((N_SHARD, blk, D_MODEL), F32)],
        input_output_aliases={6: 2},
        compiler_params=_cparams(1),
    )(dup, dus, h, dhm, g1, wp, gbuf)


def _disc_math(lr, li, ldt, br_t, bi_t):
    dt = jnp.exp(ldt)
    mag = jnp.exp(lr * dt)
    ang = li * dt
    ar = mag * jnp.cos(ang)
    ai = mag * jnp.sin(ang)
    den = lr * lr + li * li
    nr, ni = ar - 1.0, ai
    cr = (nr * lr + ni * li) / den
    ci = (ni * lr - nr * li) / den
    return ar, ai, cr * br_t - ci * bi_t, cr * bi_t + ci * br_t


def _disc_fwd(lr, li, ldt, br_t, bi_t):
    def body(lr_ref, li_ref, ldt_ref, br_ref, bi_ref, ar_ref, ai_ref, bbr_ref, bbi_ref):
        ar, ai, bbr, bbi = _disc_math(lr_ref[...], li_ref[...], ldt_ref[...], br_ref[...], bi_ref[...])
        ar_ref[...] = ar
        ai_ref[...] = ai
        bbr_ref[...] = bbr
        bbi_ref[...] = bbi

    shapes = [jax.ShapeDtypeStruct(a.shape, F32) for a in (lr, li, br_t, bi_t)]
    return pl.pallas_call(body, name="ssm_disc_fwd", out_shape=shapes,
                          compiler_params=pltpu.CompilerParams(vmem_limit_bytes=VMEM_LIMIT))(lr, li, ldt, br_t, bi_t)


def _disc_bwd(lr, li, ldt, br_t, bi_t, dar, dai, dbbr, dbbi):
    def body(lr_ref, li_ref, ldt_ref, br_ref, bi_ref, dar_ref, dai_ref, dbbr_ref, dbbi_ref,
             dlr_ref, dli_ref, dldt_ref, dbr_ref, dbi_ref):
        prim = (lr_ref[...], li_ref[...], ldt_ref[...], br_ref[...], bi_ref[...])
        _, pullback = jax.vjp(_disc_math, *prim)
        dlr, dli, dldt, dbr, dbi = pullback((dar_ref[...], dai_ref[...], dbbr_ref[...], dbbi_ref[...]))
        dlr_ref[...] = dlr
        dli_ref[...] = dli
        dldt_ref[...] = dldt
        dbr_ref[...] = dbr
        dbi_ref[...] = dbi

    shapes = [jax.ShapeDtypeStruct(a.shape, F32) for a in (lr, li, ldt, br_t, bi_t)]
    return pl.pallas_call(body, name="ssm_disc_bwd", out_shape=shapes,
                          compiler_params=pltpu.CompilerParams(vmem_limit_bytes=VMEM_LIMIT))(
        lr, li, ldt, br_t, bi_t, dar, dai, dbbr, dbbi)


def _pad_pairs(m_re, m_im):
    def blocks(m):
        v = m.transpose(0, 2, 1).reshape(N_PAIRS, 2, SSM_GROUP, SSM_STATE)
        return jnp.einsum("ab,jahp->jahbp", jnp.eye(2, dtype=m.dtype), v).reshape(N_PAIRS, 32, 128)
    both = jnp.concatenate([blocks(m_re), blocks(m_im)], axis=-1)
    place = jax.nn.one_hot(jnp.arange(N_PAIRS) % 4, 4, dtype=both.dtype)
    return jnp.einsum("jk,jrc->jkrc", place, both).reshape(N_PAIRS, 128, 256)


def _unpad_pairs(x):
    place = jax.nn.one_hot(jnp.arange(N_PAIRS) % 4, 4, dtype=x.dtype)
    both = jnp.einsum("jk,jkrc->jrc", place, x.reshape(N_PAIRS, 4, 32, 256))

    def unblock(v):
        v = v.reshape(N_PAIRS, 2, SSM_GROUP, 2, SSM_STATE)
        d = jnp.einsum("ab,jahbp->jahp", jnp.eye(2, dtype=x.dtype), v)
        return d.reshape(N_SSM_GROUPS, SSM_GROUP, SSM_STATE).transpose(0, 2, 1)
    return unblock(both[..., :128]), unblock(both[..., 128:])


def _adamw_math(w, g, m, v):
    m = ADAM_B1 * m + (1.0 - ADAM_B1) * g
    v = ADAM_B2 * v + (1.0 - ADAM_B2) * (g * g)
    m_hat = m / (1.0 - ADAM_B1 ** ADAM_STEP)
    v_hat = v / (1.0 - ADAM_B2 ** ADAM_STEP)
    delta = -ADAM_LR * (m_hat / (jnp.sqrt(v_hat) + ADAM_EPS) + ADAM_WD * w)
    return delta, m, v


def _adamw(name, layer, w, m, v, gbuf, g_block, g_row0, row_tile, outs=None, after=(), glu=False):
    nl, r, c = w.shape
    n_tiles = r // row_tile
    g_rows, g_cols = g_block
    g_tile = g_rows // n_tiles
    g_off = g_row0 // g_tile
    if outs is None:
        outs = [lax.empty(w.shape, F32) for _ in range(4)]

    def body(w_ref, m_ref, v_ref, g_ref, *rest):
        go_ref, d_ref, mo_ref, vo_ref = rest[-4:]
        g = g_ref[...]
        if glu:
            g = jnp.concatenate([g[:, :D_SSM], g[:, D_SSM:]], axis=0)
        delta, mn, vn = _adamw_math(w_ref[...], g, m_ref[...], v_ref[...])
        go_ref[...] = g
        d_ref[...] = delta
        mo_ref[...] = mn
        vo_ref[...] = vn

    w_spec = pl.BlockSpec((None, row_tile, c), lambda j: (layer, j, 0))
    shape = jax.ShapeDtypeStruct(w.shape, F32)
    return pl.pallas_call(
        body, name=name, grid=(n_tiles,),
        in_specs=[w_spec, w_spec, w_spec, pl.BlockSpec((None, g_tile, g_cols), lambda j: (0, g_off + j, 0))]
        + [_ANY] * (4 + len(after)),
        out_specs=[w_spec] * 4,
        out_shape=[shape] * 4,
        input_output_aliases={4: 0, 5: 1, 6: 2, 7: 3},
        compiler_params=_cparams(1),
    )(w, m, v, gbuf, *outs, *after)


def _adamw_group(name, layer, ws, ms, vs, gbuf, g_row0s, row_tile, outs=None):
    k = len(ws)
    nl, r, c = ws[0].shape
    n_tiles = r // row_tile
    if outs is None:
        outs = [[lax.empty(ws[0].shape, F32) for _ in range(4)] for _ in range(k)]

    def body(*refs):
        ins, results = refs[:4 * k], refs[-4 * k:]
        for i in range(k):
            w_ref, m_ref, v_ref, g_ref = (ins[j * k + i] for j in range(4))
            g = g_ref[...]
            delta, mn, vn = _adamw_math(w_ref[...], g, m_ref[...], v_ref[...])
            for ref, val in zip(results[4 * i:4 * i + 4], (g, delta, mn, vn)):
                ref[...] = val

    w_spec = pl.BlockSpec((None, row_tile, c), lambda j: (layer, j, 0))
    g_specs = [pl.BlockSpec((None, row_tile, c), functools.partial(lambda j, off: (0, off + j, 0), off=r0 // row_tile))
               for r0 in g_row0s]
    shape = jax.ShapeDtypeStruct(ws[0].shape, F32)
    flat = pl.pallas_call(
        body, name=name, grid=(n_tiles,),
        in_specs=[w_spec] * (3 * k) + g_specs + [_ANY] * (4 * k),
        out_specs=[w_spec] * (4 * k),
        out_shape=[shape] * (4 * k),
        input_output_aliases={4 * k + i: i for i in range(4 * k)},
        compiler_params=_cparams(1),
    )(*ws, *ms, *vs, *([gbuf] * k), *[a for group in outs for a in group])
    return [flat[4 * i:4 * i + 4] for i in range(k)]


def _pack_weights(ids, layer, w_in, w_glu, w_out, w_down, w_gate_t, w_up_t, after=()):
    gb, gi = P_GLU_BLK
    ib, ii = P_IN_BLK
    ob, oi = P_OUT_BLK

    def body(ids_ref, in_ref, glu_ref, out_ref, dn_ref, gate_ref, up_ref, *rest):
        p_ref = rest[-1]
        p_ref[0:FF_SHARD, :] = dn_ref[...].astype(BF16)
        p_ref[FF_SHARD:2 * FF_SHARD, :] = gate_ref[...].astype(BF16)
        p_ref[2 * FF_SHARD:P_FF_ROWS, :] = up_ref[...].astype(BF16)
        g = glu_ref[...]
        p_ref[gb * gi:gb * (gi + 1), :] = jnp.concatenate([g[:gb, :], g[gb:, :]], axis=1).astype(BF16)
        p_ref[gb * (gi + 1):ib * ii, :] = jnp.zeros((P_GLU_PAD - gb, D_MODEL), BF16)
        p_ref[ib * ii:ib * (ii + 1), :] = in_ref[...].astype(BF16)
        p_ref[ob * oi:ob * (oi + 1), :] = out_ref[...].astype(BF16)

    def spec(a):
        return pl.BlockSpec((None,) + a.shape[1:], lambda i, ids_ref: (layer, 0, 0))

    ins = (w_in, w_glu, w_out, w_down, w_gate_t, w_up_t)
    grid_spec = pltpu.PrefetchScalarGridSpec(
        num_scalar_prefetch=1, grid=(1,),
        in_specs=[spec(a) for a in ins] + [_ANY] * len(after),
        out_specs=pl.BlockSpec((None, None, P_ROWS, D_MODEL), lambda i, ids_ref: (ids_ref[1], 0, 0, 0)))
    return pl.pallas_call(
        body, name="pack_weights", grid_spec=grid_spec,
        out_shape=jax.ShapeDtypeStruct((N_SHARD, 1, P_ROWS, D_MODEL), BF16),
        compiler_params=_cparams(1),
    )(ids, *ins, *after)


MESH = pl.DeviceIdType.MESH
_ANY = pl.BlockSpec(memory_space=pl.ANY)
P_HALF = P_ROWS // 2
RS_ROW_TILE = 352


def _mesh_pos():
    return lax.axis_index("x"), lax.axis_index("y"), lax.axis_index("c")


def _other_chips(x, y):
    return [(1 - x, y), (x, 1 - y), (1 - x, 1 - y)]


def _remote(src, dst, send_sems, recv_sems, n, to):
    return pltpu.make_async_remote_copy(src_ref=src, dst_ref=dst, send_sem=send_sems.at[n],
                                        recv_sem=recv_sems.at[n], device_id=to, device_id_type=MESH)


_HBM = pl.BlockSpec(memory_space=pltpu.HBM)
_SEM = pl.BlockSpec(memory_space=pltpu.SEMAPHORE)
_EFFECT = pltpu.CompilerParams(has_side_effects=pltpu.SideEffectType.DATAFLOW_SIDE_EFFECTING)
_TOKEN = jax.ShapeDtypeStruct((8, 128), F32)


def _in_hbm(a):
    return pltpu.with_memory_space_constraint(a, pltpu.HBM)


def _ag_piece(ref, shard, half, rows):
    row0, n_rows = rows
    return ref.at[shard, :, pl.ds(row0 + half * (n_rows // 2), n_rows // 2), :]


def _ag_start(name, wp, after, row_ranges):
    n_sems = 3 * len(row_ranges)

    def body(w_ref, after_ref, send_sems, recv_sems, w_thru, token):
        x, y, c = _mesh_pos()
        for i, rows in enumerate(row_ranges):
            mine = _ag_piece(w_ref, 2 * x + y, c, rows)
            for j, (px, py) in enumerate(_other_chips(x, y)):
                _remote(mine, mine, send_sems, recv_sems, 3 * i + j, (px, py, c)).start()
        token[...] = jnp.zeros_like(token)

    return pl.pallas_call(
        body, name=name,
        out_shape=(pltpu.SemaphoreType.DMA((n_sems,)), pltpu.SemaphoreType.DMA((n_sems,)),
                   pltpu.HBM(wp.shape, wp.dtype), _TOKEN),
        in_specs=(_HBM, _ANY), out_specs=(_SEM, _SEM, _HBM, pl.BlockSpec(memory_space=pltpu.VMEM)),
        input_output_aliases={0: 2}, compiler_params=_EFFECT,
    )(_in_hbm(wp), after)


def _ag_wait(name, send_sems, recv_sems, wp, after, row_ranges):
    def body(w_ref, send_sems, recv_sems, *rest):
        x, y, c = _mesh_pos()
        for i, rows in enumerate(row_ranges):
            mine = _ag_piece(w_ref, 2 * x + y, c, rows)
            for j, (px, py) in enumerate(_other_chips(x, y)):
                landed = _ag_piece(w_ref, 2 * px + py, c, rows)
                cp = _remote(mine, landed, send_sems, recv_sems, 3 * i + j, (px, py, c))
                cp.wait_send()
                cp.wait_recv()

    return pl.pallas_call(
        body, name=name, out_shape=pltpu.HBM(wp.shape, wp.dtype),
        in_specs=(_HBM, _SEM, _SEM) + (_ANY,) * len(after), out_specs=_HBM,
        input_output_aliases={0: 0}, compiler_params=_EFFECT,
    )(wp, send_sems, recv_sems, *after)


def _ag_forward(wp, rows):
    def body(w_in, o, send_sems, recv_sems):
        x, y, c = _mesh_pos()
        sib = (x, y, 1 - c)
        chips = _other_chips(x, y)
        sends = []
        for j, (px, py) in enumerate(chips):
            landed = _ag_piece(o, 2 * px + py, c, rows)
            cp = _remote(landed, landed, send_sems, recv_sems, j, sib)
            cp.start()
            sends.append(cp)
        for j, (px, py) in enumerate(chips):
            passed = _ag_piece(o, 2 * px + py, 1 - c, rows)
            _remote(passed, passed, send_sems, recv_sems, j, sib).wait_recv()
        for cp in sends:
            cp.wait_send()

    return pl.pallas_call(
        body, name="ag_forward",
        in_specs=[_ANY], out_specs=_ANY,
        out_shape=jax.ShapeDtypeStruct(wp.shape, wp.dtype),
        scratch_shapes=[pltpu.SemaphoreType.DMA((3,)), pltpu.SemaphoreType.DMA((3,))],
        input_output_aliases={0: 0},
    )(wp)


def _ag_forward_start(name, wp, rows):
    def body(w_ref, send_sems, recv_sems, w_thru):
        x, y, c = _mesh_pos()
        for j, (px, py) in enumerate(_other_chips(x, y)):
            landed = _ag_piece(w_ref, 2 * px + py, c, rows)
            _remote(landed, landed, send_sems, recv_sems, j, (x, y, 1 - c)).start()

    return pl.pallas_call(
        body, name=name,
        out_shape=(pltpu.SemaphoreType.DMA((3,)), pltpu.SemaphoreType.DMA((3,)), pltpu.HBM(wp.shape, wp.dtype)),
        in_specs=(_HBM,), out_specs=(_SEM, _SEM, _HBM),
        input_output_aliases={0: 2}, compiler_params=_EFFECT,
    )(_in_hbm(wp))


def _ag_forward_wait(name, send_sems, recv_sems, wp, after, rows):
    def body(w_ref, send_sems, recv_sems, *rest):
        x, y, c = _mesh_pos()
        for j, (px, py) in enumerate(_other_chips(x, y)):
            cp = _remote(_ag_piece(w_ref, 2 * px + py, c, rows), _ag_piece(w_ref, 2 * px + py, 1 - c, rows),
                         send_sems, recv_sems, j, (x, y, 1 - c))
            cp.wait_send()
            cp.wait_recv()

    return pl.pallas_call(
        body, name=name, out_shape=pltpu.HBM(wp.shape, wp.dtype),
        in_specs=(_HBM, _SEM, _SEM) + (_ANY,) * len(after), out_specs=_HBM,
        input_output_aliases={0: 0}, compiler_params=_EFFECT,
    )(wp, send_sems, recv_sems, *after)


def _rs_chips_start(name, t):
    nl = t.shape[0]

    def body(t_ref, land_ref, send_sems, recv_sems, t_thru, land_thru, token):
        x, y, c = _mesh_pos()
        for j, (px, py) in enumerate(_other_chips(x, y)):
            _remote(t_ref.at[:, 2 * px + py], land_ref.at[j], send_sems, recv_sems, j, (px, py, c)).start()
        token[...] = jnp.zeros_like(token)

    land = lax.empty((3, nl, P_HALF, D_MODEL), BF16)
    return pl.pallas_call(
        body, name=name,
        out_shape=(pltpu.SemaphoreType.DMA((3,)), pltpu.SemaphoreType.DMA((3,)), pltpu.HBM(t.shape, t.dtype),
                   pltpu.HBM(land.shape, land.dtype), _TOKEN),
        in_specs=(_HBM, _HBM), out_specs=(_SEM, _SEM, _HBM, _HBM, pl.BlockSpec(memory_space=pltpu.VMEM)),
        input_output_aliases={0: 2, 1: 3}, compiler_params=_EFFECT,
    )(_in_hbm(t), _in_hbm(land))


def _rs_chips_wait(name, send_sems, recv_sems, t, land, after):
    def body(t_ref, land_ref, send_sems, recv_sems, *rest):
        x, y, c = _mesh_pos()
        for j, (px, py) in enumerate(_other_chips(x, y)):
            cp = _remote(t_ref.at[:, 2 * px + py], land_ref.at[j], send_sems, recv_sems, j, (px, py, c))
            cp.wait_send()
            cp.wait_recv()

    return pl.pallas_call(
        body, name=name, out_shape=(pltpu.HBM(t.shape, t.dtype), pltpu.HBM(land.shape, land.dtype)),
        in_specs=(_HBM, _HBM, _SEM, _SEM) + (_ANY,) * len(after), out_specs=(_HBM, _HBM),
        input_output_aliases={0: 0, 1: 1}, compiler_params=_EFFECT,
    )(t, land, send_sems, recv_sems, *after)[1]


def _rs_sibling_start(name, g):
    nl = g.shape[0]

    def body(g_ref, land_ref, send_sems, recv_sems, g_thru, land_thru, token):
        x, y, c = _mesh_pos()
        _remote(g_ref.at[:, :, pl.ds((1 - c) * P_HALF, P_HALF), :], land_ref, send_sems, recv_sems, 0,
                (x, y, 1 - c)).start()
        token[...] = jnp.zeros_like(token)

    land = lax.empty((nl, N_SHARD, P_HALF, D_MODEL), F32)
    return pl.pallas_call(
        body, name=name,
        out_shape=(pltpu.SemaphoreType.DMA((1,)), pltpu.SemaphoreType.DMA((1,)), pltpu.HBM(g.shape, g.dtype),
                   pltpu.HBM(land.shape, land.dtype), _TOKEN),
        in_specs=(_HBM, _HBM), out_specs=(_SEM, _SEM, _HBM, _HBM, pl.BlockSpec(memory_space=pltpu.VMEM)),
        input_output_aliases={0: 2, 1: 3}, compiler_params=_EFFECT,
    )(_in_hbm(g), _in_hbm(land))


def _rs_sibling_wait(name, send_sems, recv_sems, g, land, after):
    def body(g_ref, land_ref, send_sems, recv_sems, *rest):
        x, y, c = _mesh_pos()
        cp = _remote(g_ref.at[:, :, pl.ds((1 - c) * P_HALF, P_HALF), :], land_ref, send_sems, recv_sems, 0,
                     (x, y, 1 - c))
        cp.wait_send()
        cp.wait_recv()

    return pl.pallas_call(
        body, name=name, out_shape=(pltpu.HBM(g.shape, g.dtype), pltpu.HBM(land.shape, land.dtype)),
        in_specs=(_HBM, _HBM, _SEM, _SEM) + (_ANY,) * len(after), out_specs=(_HBM, _HBM),
        input_output_aliases={0: 0, 1: 1}, compiler_params=_EFFECT,
    )(g, land, send_sems, recv_sems, *after)


def _rs_add(name, ids, g, buf, row_tile):
    nl, _, hr, cols = buf.shape
    n_rt = hr // row_tile

    def body(ids_ref, g_ref, b_ref, own_ref, tb_ref):
        t = g_ref[...] + b_ref[...]
        tb_ref[...] = t.astype(BF16)

        @pl.when(pl.program_id(2) == ids_ref[1])
        def _():
            own_ref[...] = t

    blk = (None, None, row_tile, cols)
    grid_spec = pltpu.PrefetchScalarGridSpec(
        num_scalar_prefetch=1, grid=(nl, n_rt, N_SHARD),
        in_specs=[pl.BlockSpec(blk, lambda l, j, s, ids_ref: (l, s, ids_ref[0] * n_rt + j, 0)),
                  pl.BlockSpec(blk, lambda l, j, s, ids_ref: (l, s, j, 0))],
        out_specs=[pl.BlockSpec((None, row_tile, cols), lambda l, j, s, ids_ref: (l, j, 0)),
                   pl.BlockSpec(blk, lambda l, j, s, ids_ref: (l, s, j, 0))])
    return pl.pallas_call(
        body, name=name, grid_spec=grid_spec,
        out_shape=[jax.ShapeDtypeStruct((nl, hr, cols), F32), jax.ShapeDtypeStruct(buf.shape, BF16)],
        compiler_params=_cparams(3),
    )(ids, g, buf)


def _rs_sum(ids, layer, own, bufb, reduced, row_tile):
    _, hr, cols = own.shape
    n_rt = hr // row_tile

    def body(ids_ref, own_ref, b_ref, reduced_in, f_ref):
        f_ref[...] = ((own_ref[...] + b_ref[0].astype(F32)) + b_ref[1].astype(F32)) + b_ref[2].astype(F32)

    grid_spec = pltpu.PrefetchScalarGridSpec(
        num_scalar_prefetch=1, grid=(n_rt,),
        in_specs=[pl.BlockSpec((None, row_tile, cols), lambda j, ids_ref: (0, j, 0)),
                  pl.BlockSpec((3, None, row_tile, cols), lambda j, ids_ref: (0, 0, j, 0)),
                  pl.BlockSpec(memory_space=pl.ANY)],
        out_specs=pl.BlockSpec((None, row_tile, cols), lambda j, ids_ref: (layer, ids_ref[0] * n_rt + j, 0)))
    return pl.pallas_call(
        body, name="rs_sum", grid_spec=grid_spec,
        out_shape=jax.ShapeDtypeStruct(reduced.shape, F32),
        input_output_aliases={3: 0},
        compiler_params=_cparams(1),
    )(ids, own, bufb, reduced)


def _rs_exchange_start(name, f):
    def body(f_ref, send_sems, recv_sems, f_thru):
        x, y, c = _mesh_pos()
        mine = f_ref.at[:, pl.ds(c * P_HALF, P_HALF), :]
        _remote(mine, mine, send_sems, recv_sems, 0, (x, y, 1 - c)).start()

    return pl.pallas_call(
        body, name=name,
        out_shape=(pltpu.SemaphoreType.DMA((1,)), pltpu.SemaphoreType.DMA((1,)), pltpu.HBM(f.shape, f.dtype)),
        in_specs=(_HBM,), out_specs=(_SEM, _SEM, _HBM),
        input_output_aliases={0: 2}, compiler_params=_EFFECT,
    )(_in_hbm(f))


def _rs_exchange_wait(name, send_sems, recv_sems, f, after):
    def body(f_ref, send_sems, recv_sems, *rest):
        x, y, c = _mesh_pos()
        mine = f_ref.at[:, pl.ds(c * P_HALF, P_HALF), :]
        theirs = f_ref.at[:, pl.ds((1 - c) * P_HALF, P_HALF), :]
        cp = _remote(mine, theirs, send_sems, recv_sems, 0, (x, y, 1 - c))
        cp.wait_send()
        cp.wait_recv()

    return pl.pallas_call(
        body, name=name, out_shape=pltpu.HBM(f.shape, f.dtype),
        in_specs=(_HBM, _SEM, _SEM) + (_ANY,) * len(after), out_specs=_HBM,
        input_output_aliases={0: 0}, compiler_params=_EFFECT,
    )(f, send_sems, recv_sems, *after)


def _small_all_reduce(s, after=()):
    n_rows = s.shape[0]
    hr = n_rows // 2
    qr = hr // N_SHARD

    def body(s_ref, *rest):
        o_ref, sibbuf, tbuf, qbuf, fbuf, send_sems, recv_sems = rest[len(after):]
        x, y, c = _mesh_pos()
        k = 2 * x + y
        sib = (x, y, 1 - c)
        chips = _other_chips(x, y)
        mine = pl.ds(pl.multiple_of(c * hr, SUBLANES), hr)
        theirs = pl.ds(pl.multiple_of((1 - c) * hr, SUBLANES), hr)

        def quarter(shard):
            return pl.ds(pl.multiple_of(shard * qr, SUBLANES), qr)

        first = _remote(s_ref.at[theirs], sibbuf, send_sems, recv_sems, 0, sib)
        first.start()
        first.wait()
        tbuf[...] = s_ref[mine, :] + sibbuf[...]
        cps = []
        for j, (px, py) in enumerate(chips):
            cp = _remote(tbuf.at[quarter(2 * px + py)], qbuf.at[j], send_sems, recv_sems, 1 + j, (px, py, c))
            cp.start()
            cps.append(cp)
        for cp in cps:
            cp.wait()
        fbuf[quarter(k), :] = (tbuf[quarter(k), :] + qbuf[1]) + (qbuf[0] + qbuf[2])
        cps = []
        for j, (px, py) in enumerate(chips):
            cp = _remote(fbuf.at[quarter(k)], fbuf.at[quarter(k)], send_sems, recv_sems, 4 + j, (px, py, c))
            cp.start()
            cps.append(cp)
        for j, (px, py) in enumerate(chips):
            got = fbuf.at[quarter(2 * px + py)]
            _remote(got, got, send_sems, recv_sems, 4 + j, (px, py, c)).wait_recv()
        for cp in cps:
            cp.wait_send()
        o_ref[mine, :] = fbuf[...]
        last = _remote(fbuf, o_ref.at[mine], send_sems, recv_sems, 7, sib)
        last.start()
        last.wait()

    vmem = pl.BlockSpec(memory_space=pltpu.VMEM)
    return pl.pallas_call(
        body, name="small_all_reduce",
        in_specs=[vmem] + [_ANY] * len(after), out_specs=vmem,
        out_shape=jax.ShapeDtypeStruct(s.shape, F32),
        scratch_shapes=[pltpu.VMEM((hr, D_MODEL), F32), pltpu.VMEM((hr, D_MODEL), F32),
                        pltpu.VMEM((3, qr, D_MODEL), F32), pltpu.VMEM((hr, D_MODEL), F32),
                        pltpu.SemaphoreType.DMA((8,)), pltpu.SemaphoreType.DMA((8,))],
        compiler_params=pltpu.CompilerParams(vmem_limit_bytes=VMEM_LIMIT),
    )(s, *after)


_SMALL = ("norm_mix", "w_pool", "pool_scale", "lam_re", "lam_im", "log_dt", "b_re", "b_im", "c_re", "c_im",
          "d_skip", "b_glu", "norm_ffn", "norm_final")
_WEIGHTS = ("norm_mix", "w_in", "w_pool", "pool_scale", "lam_re", "lam_im", "log_dt", "b_re", "b_im", "c_re",
            "c_im", "d_skip", "w_glu", "b_glu", "w_out", "norm_ffn", "w_gate", "w_up", "w_down", "norm_final")


def _local_step(x, target, p, get_weights, get_ffn_weights, ffn_bwd_done, put_grads):
    nl = p["norm_mix"].shape[0]

    def tied(a, token):
        return a if token is None else a + token
    n_rows = nl * N_SSM_GROUPS
    lr = p["lam_re"].reshape(n_rows, 1, SSM_STATE)
    li = p["lam_im"].reshape(n_rows, 1, SSM_STATE)
    ldt = p["log_dt"].reshape(n_rows, 1, 1)
    br_t = p["b_re"].reshape(n_rows, SSM_STATE, SSM_GROUP).transpose(0, 2, 1)
    bi_t = p["b_im"].reshape(n_rows, SSM_STATE, SSM_GROUP).transpose(0, 2, 1)
    ar, ai, bbr_t, bbi_t = _disc_fwd(lr, li, ldt, br_t, bi_t)
    ar = ar.reshape(nl, 1, N_STATE)
    ai = ai.reshape(nl, 1, N_STATE)
    bbr = bbr_t.transpose(0, 2, 1).reshape(nl, N_SSM_GROUPS, SSM_STATE, SSM_GROUP)
    bbi = bbi_t.transpose(0, 2, 1).reshape(nl, N_SSM_GROUPS, SSM_STATE, SSM_GROUP)
    w_pool = p["w_pool"].astype(BF16)
    p = dict(p)
    for n in ("norm_mix", "pool_scale", "b_glu", "norm_ffn"):
        p[n] = p[n].reshape(nl, 1, -1)
    swap = lambda a: jnp.swapaxes(a, -1, -2)
    bpad = jax.vmap(_pad_pairs)(bbr, bbi).astype(BF16)
    cpad_t = jax.vmap(_pad_pairs)(swap(p["c_re"]), -swap(p["c_im"])).astype(BF16)
    bpad_t, cpad = swap(bpad), swap(cpad_t)
    dskip = p["d_skip"].reshape(nl, 1, D_SSM)

    layers = []
    h = x
    for l in range(nl):
        wp = get_weights(l, [h] if l else [h, bpad, cpad, bpad_t, cpad_t, ar, ai])
        u, ypool = _mix_in_fwd(h, p["norm_mix"], wp, l, w_pool, p["pool_scale"])
        sre, sim, yraw = _ssm_fwd(u, l, bpad, cpad, ar, ai, dskip)
        hm = _mix_out_fwd(yraw, ypool, h, wp, l, p["b_glu"])
        wp = get_ffn_weights(l, wp, [hm])
        h_next, n2, act_s, fgate_s, fup_s = _ffn_fwd(hm, p["norm_ffn"], wp, l)
        layers.append(dict(h=h, u=u, ypool=ypool, sre=sre, sim=sim, yraw=yraw, hm=hm, n2=n2, act_s=act_s, wp=wp,
                           fgate_s=fgate_s, fup_s=fup_s))
        h = h_next

    dh, loss, d_norm_final = _final_fwd_bwd(h, p["norm_final"].reshape(1, D_MODEL), target)

    raw = {n: [None] * nl for n in ("dg1", "dwp", "dsc", "dcp", "dbp", "ddsk", "db_glu", "dg2", "dar", "dai")}
    token = None
    for l in reversed(range(nl)):
        s = layers[l]
        wp = s["wp"]
        g1 = lax.empty((1, N_SHARD, P_ROWS, D_MODEL), F32)
        dhm, dg2, dgate_s, dup_s, dhb = _ffn_bwd_act(dh, s["hm"], tied(p["norm_ffn"], token), s["fgate_s"],
                                                      s["fup_s"], wp, l)
        g1 = _ffn_bwd_w(s["n2"], dgate_s, dup_s, s["act_s"], dhb, g1)
        token = ffn_bwd_done(l, [g1])
        dyraw, dyp, db_glu, g1 = _mix_out_bwd(dhm, s["yraw"], s["ypool"], wp, l, tied(p["b_glu"], token), g1)
        dus, dcp, dbp, dar, dai, ddsk = _ssm_bwd(dyraw, s["u"], s["sre"], s["sim"], l, cpad_t, bpad_t, ar, ai, dskip)
        dup, dwp, dsc = _pool_bwd(dyp, s["u"], l, w_pool, p["pool_scale"])
        dh, dg1, g1 = _mix_in_bwd(dup, dus, s["h"], dhm, p["norm_mix"], wp, l, g1)
        token = put_grads(l, g1)
        for n, a in (("dg1", dg1), ("dwp", dwp), ("dsc", dsc), ("dcp", dcp), ("dbp", dbp), ("ddsk", ddsk),
                     ("db_glu", db_glu), ("dg2", dg2), ("dar", dar), ("dai", dai)):
            raw[n][l] = a

    st = {n: jnp.stack(v) for n, v in raw.items()}
    dc_re, dc_im = jax.vmap(_unpad_pairs)(swap(st["dcp"]))
    dbbr, dbbi = jax.vmap(_unpad_pairs)(st["dbp"])
    rows = lambda a: a.reshape((n_rows,) + a.shape[2:])
    dlr, dli, dldt, dbr_t, dbi_t = _disc_bwd(lr, li, ldt, br_t, bi_t, st["dar"].reshape(n_rows, 1, SSM_STATE),
                                              st["dai"].reshape(n_rows, 1, SSM_STATE), rows(swap(dbbr)),
                                              rows(swap(dbbi)))
    small = {"norm_mix": st["dg1"][:, 0], "w_pool": st["dwp"], "pool_scale": st["dsc"][:, 0], "c_re": swap(dc_re),
             "c_im": -swap(dc_im), "d_skip": st["ddsk"].reshape(nl, N_SSM_GROUPS, SSM_GROUP),
             "b_glu": st["db_glu"][:, 0], "norm_ffn": st["dg2"][:, 0]}
    small["lam_re"] = dlr.reshape(nl, N_SSM_GROUPS, SSM_STATE)
    small["lam_im"] = dli.reshape(nl, N_SSM_GROUPS, SSM_STATE)
    small["log_dt"] = dldt.reshape(nl, N_SSM_GROUPS)
    small["b_re"] = dbr_t.reshape(nl, N_SSM_GROUPS, SSM_GROUP, SSM_STATE)
    small["b_im"] = dbi_t.reshape(nl, N_SSM_GROUPS, SSM_GROUP, SSM_STATE)
    small["d_skip"] = small["d_skip"].transpose(_SMALL_VIEW["d_skip"])
    small["norm_final"] = d_norm_final
    return loss, dh, small


_SMALL_VIEW = {"b_re": (0, 1, 3, 2), "b_im": (0, 1, 3, 2), "d_skip": (0, 2, 1)}
_SMALL_GROUPS = (("b_re", "b_im"), ("c_re", "c_im"), ("lam_re", "lam_im"), ("norm_mix", "norm_ffn"),
                 ("pool_scale", "b_glu"), ("w_pool",), ("log_dt",), ("d_skip",), ("norm_final",))


def _view(n, a):
    a = a.transpose(_SMALL_VIEW[n]) if n in _SMALL_VIEW else a
    return a[None] if a.ndim == 1 else a


def _unview(n, a, shape):
    a = a.reshape(shape) if len(shape) == 1 else a
    return a.transpose(_SMALL_VIEW[n]) if n in _SMALL_VIEW else a


def _flatten_small(views):
    flat = jnp.concatenate([views[n].reshape(-1) for n in _SMALL])
    n_rows = -(-flat.shape[0] // (64 * D_MODEL)) * 64
    return jnp.pad(flat, (0, n_rows * D_MODEL - flat.shape[0])).reshape(n_rows, D_MODEL)


def _split_small(flat, like):
    flat = flat.reshape(-1)
    out, at = {}, 0
    for n in _SMALL:
        size = like[n].size
        out[n] = flat[at:at + size].reshape(like[n].shape)
        at += size
    return out


def _adamw_small(name, ws, ms, vs, gs):
    k = len(ws)

    def body(*refs):
        ins, outs = refs[:4 * k], refs[4 * k:]
        for i in range(k):
            w, m, v, g = (ins[j * k + i][...] for j in range(4))
            delta, mn, vn = _adamw_math(w, g, m, v)
            outs[i][...] = delta
            outs[k + i][...] = mn
            outs[2 * k + i][...] = vn

    shapes = [jax.ShapeDtypeStruct(w.shape, F32) for w in ws] * 3
    outs = pl.pallas_call(body, name=name, out_shape=shapes,
                          compiler_params=pltpu.CompilerParams(vmem_limit_bytes=VMEM_LIMIT))(*ws, *ms, *vs, *gs)
    return outs[:k], outs[k:2 * k], outs[2 * k:]


def kernel(x, norm_mix, w_in, w_pool, pool_scale, lam_re, lam_im, log_dt, b_re, b_im, c_re, c_im, d_skip, w_glu, b_glu, w_out, norm_ffn, w_gate, w_up, w_down, norm_final, loss_target, m_norm_mix, m_w_in, m_w_pool, m_pool_scale, m_lam_re, m_lam_im, m_log_dt, m_b_re, m_b_im, m_c_re, m_c_im, m_d_skip, m_w_glu, m_b_glu, m_w_out, m_norm_ffn, m_w_gate, m_w_up, m_w_down, m_norm_final, v_norm_mix, v_w_in, v_w_pool, v_pool_scale, v_lam_re, v_lam_im, v_log_dt, v_b_re, v_b_im, v_c_re, v_c_im, v_d_skip, v_w_glu, v_b_glu, v_w_out, v_norm_ffn, v_w_gate, v_w_up, v_w_down, v_norm_final):
    given = dict(locals())
    w = {n: given[n] for n in _WEIGHTS}
    m = {n: given["m_" + n] for n in _WEIGHTS}
    v = {n: given["v_" + n] for n in _WEIGHTS}
    ids = jnp.stack([lax.axis_index("c"), 2 * lax.axis_index("x") + lax.axis_index("y")]).astype(jnp.int32)

    t_names = ("w_gate", "w_up")
    tr = lambda a: a.transpose(0, 2, 1)
    for d in (w, m, v):
        d.update({n: tr(d[n]) for n in t_names})

    nl = norm_mix.shape[0]
    mixer_rows, ffn_rows = (P_FF_ROWS, P_ROWS - P_FF_ROWS), (0, P_FF_ROWS)
    started, last = {}, None
    for l in range(nl):
        packed = _pack_weights(ids, l, w["w_in"], w["w_glu"], w["w_out"], w["w_down"], w["w_gate"], w["w_up"],
                               [] if last is None else [last])
        if l == 0:
            first = _ag_start("ag_start_0_mixer", packed, ids, [mixer_rows])
            started[0] = _ag_start("ag_start_0_ffn", first[2], first[3], [ffn_rows])
        else:
            started[l] = _ag_start(f"ag_start_{l}", packed, last, [mixer_rows, ffn_rows])
        last = started[l][3]
    views = [{n: _view(n, d[n]) for n in _SMALL} for d in (w, m, v)]

    passing = {}

    def get_weights(l, after):
        send_sems, recv_sems, buf, _ = started[l]
        if l == 0:
            buf = _ag_wait("ag_wait_0_mixer", first[0], first[1], buf, after + [last], [mixer_rows])
            return _ag_forward(buf, mixer_rows)
        buf = _ag_wait(f"ag_wait_{l}", send_sems, recv_sems, buf, after, [mixer_rows, ffn_rows])
        buf = _ag_forward(buf, mixer_rows)
        passing[l] = _ag_forward_start(f"ag_forward_start_{l}", buf, ffn_rows)
        return passing[l][2]

    def get_ffn_weights(l, buf, after):
        if l > 0:
            send_sems, recv_sems, _ = passing[l]
            return _ag_forward_wait(f"ag_forward_wait_{l}", send_sems, recv_sems, buf, after, ffn_rows)
        send_sems, recv_sems, _, _ = started[0]
        return _ag_forward(_ag_wait("ag_wait_0_ffn", send_sems, recv_sems, buf, after, [ffn_rows]), ffn_rows)

    to_sibling, to_chips, reduced = {}, {}, {}

    def put_grads(l, g):
        to_sibling[l] = _rs_sibling_start(f"rs_sibling_start_{l}", g)
        token = to_sibling[l][4]
        if l + 1 in to_chips:
            finish(l + 1, [token])
        return token[:1, :1]

    def ffn_bwd_done(l, after):
        return send_to_chips(l + 1, after)[:1, :1] if l + 1 in to_sibling else None

    def send_to_chips(l, after):
        send_sems, recv_sems, g, land, _ = to_sibling.pop(l)
        g, land = _rs_sibling_wait(f"rs_sibling_wait_{l}", send_sems, recv_sems, g, land, after)
        own, t = _rs_add("rs_add", ids, g, land, RS_ROW_TILE)
        send_sems, recv_sems, t, land, token = _rs_chips_start(f"rs_chips_start_{l}", t)
        to_chips[l] = (send_sems, recv_sems, t, land, own)
        return token

    def finish(l, after):
        send_sems, recv_sems, t, land, own = to_chips.pop(l)
        land = _rs_chips_wait(f"rs_chips_wait_{l}", send_sems, recv_sems, t, land, after)
        shard = lax.empty((1, P_ROWS, D_MODEL), F32)
        reduced[l] = _rs_exchange_start(f"rs_exchange_start_{l}", _rs_sum(ids, 0, own, land, shard, RS_ROW_TILE))

    loss, grad_x, small = _local_step(x[0], loss_target[0], {n: w[n] for n in _SMALL}, get_weights, get_ffn_weights,
                                      ffn_bwd_done, put_grads)
    loss = lax.psum(loss[0, 0], ("x", "y", "c"))
    small_flat = _flatten_small(small)
    token = send_to_chips(0, [small_flat])

    groups = ((("w_in", P_IN_BLK), ("w_out", P_OUT_BLK)), (("w_down", P_WD_BLK), ("w_gate", P_WG_BLK), ("w_up", P_WU_BLK)))
    res = {n: None for n in ("w_in", "w_out", "w_down", "w_gate", "w_up", "w_glu")}

    def adamw_layer(l, after):
        send_sems, recv_sems, shard = reduced[l]
        shard = _rs_exchange_wait(f"rs_exchange_wait_{l}", send_sems, recv_sems, shard, after)
        for group, row_tile in zip(groups, (128, 176)):
            names = [n for n, _ in group]
            outs = None if res[names[0]] is None else [res[n] for n in names]
            outs = _adamw_group("adamw_" + names[0], l, *[[d[n] for n in names] for d in (w, m, v)], shard,
                                [blk * idx for _, (blk, idx) in group], row_tile, outs)
            res.update(zip(names, outs))
        blk, idx = P_GLU_BLK
        res["w_glu"] = _adamw("adamw_w_glu", l, w["w_glu"], m["w_glu"], v["w_glu"], shard, (blk, D_MODEL), blk * idx,
                              128, res["w_glu"], (), True)

    for l in reversed(range(1, nl)):
        adamw_layer(l, [token])
    updated = [r[0] for r in res.values() if r is not None]
    small_sum = _small_all_reduce(small_flat, [token] + updated)
    finish(0, [small_sum] + updated)
    adamw_layer(0, [])
    for n in t_names:
        res[n] = tuple(tr(a) for a in res[n])
    g_views = _split_small(small_sum, views[0])
    for group in _SMALL_GROUPS:
        deltas, new_ms, new_vs = _adamw_small("adamw_" + group[0], *[[d[n] for n in group] for d in views],
                                              [g_views[n] for n in group])
        for i, n in enumerate(group):
            res[n] = tuple(_unview(n, a, w[n].shape) for a in (g_views[n], deltas[i], new_ms[i], new_vs[i]))

    return (loss, grad_x[None], *[res[n][0] for n in _WEIGHTS], *[res[n][1] for n in _WEIGHTS],
            *[res[n][2] for n in _WEIGHTS], *[res[n][3] for n in _WEIGHTS])
```

```python
import functools
import math

import jax
import jax.numpy as jnp
from jax import lax
from jax.experimental import pallas as pl
from jax.experimental.pallas import tpu as pltpu

F32 = jnp.float32
BF16 = jnp.bfloat16

D_MODEL = 1024
D_POOL = 512
D_SSM = 512
POOL_WINDOWS = (2, 4, 8, 16)
POOL_GROUP = 128
POOL_HALO = 16
N_SSM_GROUPS = 32
SSM_GROUP = 16
SSM_STATE = 64
N_STATE = N_SSM_GROUPS * SSM_STATE
N_PAIRS = N_SSM_GROUPS // 2
D_FF = 2816
N_SHARD = 4
FF_SHARD = D_FF // N_SHARD
RMS_EPS = 1e-6

ADAM_LR = 0.001
ADAM_B1 = 0.9
ADAM_B2 = 0.999
ADAM_EPS = 1e-08
ADAM_WD = 0.01
ADAM_STEP = 10

P_ROWS = 2816
P_WD_BLK = (704, 0)
P_WG_BLK = (704, 1)
P_WU_BLK = (704, 2)
P_FF_ROWS = 2112
P_GLU_BLK = (64, 33)
P_GLU_PAD = 192
P_IN_BLK = (256, 9)
P_OUT_BLK = (256, 10)

SUBLANES = 8
VMEM_LIMIT = 56 * 1024 * 1024

TM = 1024
TM_FFN = 512
TM_FFN_LONG = 1024
FFN_SPLIT = 2
TS = 2048
SCAN_LANES = 512


def _cparams(n_axes):
    return pltpu.CompilerParams(dimension_semantics=("arbitrary",) * n_axes, vmem_limit_bytes=VMEM_LIMIT)


def _dot(a, b):
    return jnp.dot(a, b, preferred_element_type=F32)


def _dot_nt(a, b):
    return lax.dot_general(a, b, (((1,), (1,)), ((), ())), preferred_element_type=F32)


def _dot_tn(a, b):
    return lax.dot_general(a, b, (((0,), (0,)), ((), ())), preferred_element_type=F32)


def _rms_hat(x):
    r = lax.rsqrt(jnp.mean(x * x, axis=-1, keepdims=True) + RMS_EPS)
    return x * r, r


def _rms_bwd(d_hat, xhat, r):
    return r * (d_hat - xhat * jnp.mean(d_hat * xhat, axis=-1, keepdims=True))


def _sigmoid(x):
    return 1.0 / (1.0 + jnp.exp(-x))


_GELU_C = math.sqrt(2.0 / math.pi)
_GELU_K = 0.044715


def _gelu(x):
    return 0.5 * x * (1.0 + jnp.tanh(_GELU_C * (x + _GELU_K * x * x * x)))


def _gelu_grad(x):
    th = jnp.tanh(_GELU_C * (x + _GELU_K * x * x * x))
    return 0.5 * (1.0 + th) + 0.5 * x * (1.0 - th * th) * _GELU_C * (1.0 + 3.0 * _GELU_K * x * x)


def _glu_weight(ref):
    v = ref[...]
    return jnp.concatenate([v[:, :, :D_SSM], v[:, :, D_SSM:]], axis=1).reshape(D_SSM, D_SSM)


def _glu_pack(w):
    v = w.reshape(N_SHARD, 128, D_SSM)
    return jnp.concatenate([v[:, :64, :], v[:, 64:, :]], axis=2)


def _pool_diff(ext, row0, tm):
    rows = row0 + lax.broadcasted_iota(jnp.int32, (tm, 1), 0)
    outs = []
    for gi, w in enumerate(POOL_WINDOWS):
        e = ext[:, gi * POOL_GROUP:(gi + 1) * POOL_GROUP]
        s = e
        k = 1
        while k < w:
            s = s + pltpu.roll(s, k, 0)
            k *= 2
        inv = 1.0 / jnp.minimum(rows + 1, w).astype(F32)
        outs.append(s[POOL_HALO:, :] * inv - e[POOL_HALO:, :])
    return outs


def _mix_in_fwd(h, g1, wp, layer, w_pool, scale):
    L = h.shape[0]
    tm = min(TM, L)

    def body(h_ref, g_ref, w_ref, wp_ref, sc_ref, u_ref, yp_ref, carry):
        i = pl.program_id(0)

        @pl.when(i == 0)
        def _():
            carry[...] = jnp.zeros_like(carry)

        xhat, _ = _rms_hat(h_ref[...])
        n1 = (xhat * g_ref[...]).astype(BF16)
        u = _dot(n1, w_ref[...].reshape(D_MODEL, D_MODEL))
        u_ref[...] = u
        up = u[:, :D_POOL]
        ext = jnp.concatenate([carry[...], up], axis=0)
        carry[...] = up[tm - POOL_HALO:, :]
        diffs = _pool_diff(ext, i * tm, tm)
        for gi in range(4):
            cols = slice(gi * POOL_GROUP, (gi + 1) * POOL_GROUP)
            yp_ref[:, cols] = _dot(diffs[gi].astype(BF16), wp_ref[gi]) * sc_ref[:, cols]

    blk, idx = P_IN_BLK
    return pl.pallas_call(
        body, name="mix_in_fwd", grid=(L // tm,),
        in_specs=[pl.BlockSpec((tm, D_MODEL), lambda i: (i, 0)),
                  pl.BlockSpec((None, 1, D_MODEL), lambda i: (layer, 0, 0)),
                  pl.BlockSpec((N_SHARD, None, blk, D_MODEL), lambda i: (0, 0, idx, 0)),
                  pl.BlockSpec((None, 4, POOL_GROUP, POOL_GROUP), lambda i: (layer, 0, 0, 0)),
                  pl.BlockSpec((None, 1, D_POOL), lambda i: (layer, 0, 0))],
        out_specs=[pl.BlockSpec((tm, D_MODEL), lambda i: (i, 0)),
                   pl.BlockSpec((tm, D_POOL), lambda i: (i, 0))],
        out_shape=[jax.ShapeDtypeStruct((L, D_MODEL), F32), jax.ShapeDtypeStruct((L, D_POOL), F32)],
        scratch_shapes=[pltpu.VMEM((POOL_HALO, D_POOL), F32)],
        compiler_params=_cparams(1),
    )(h, g1, wp, w_pool, scale)


def _cmul(xr, xi, yr, yi):
    return xr * yr - xi * yi, xr * yi + xi * yr


SCAN_BLOCK = 64
N_SCAN_TABLES = 26


def _permute_rows(src, dst, n_rows):
    for b in range(n_rows // SCAN_BLOCK):
        for tau in range(SUBLANES):
            dst[pl.ds(SCAN_BLOCK * b + SUBLANES * tau, SUBLANES), :] = (
                src[pl.ds(SCAN_BLOCK * b + tau, SUBLANES, stride=SUBLANES), :])


def _scan_tables(ar, ai, tab, reverse):
    c = ar.shape[1]
    row = lax.broadcasted_iota(jnp.int32, (SUBLANES, c), 0)
    zero = jnp.zeros((SUBLANES, c), F32)
    full = lambda v: jnp.broadcast_to(v, (SUBLANES, c))
    pw = [(ar, ai)]
    for _ in range(SUBLANES - 1):
        pw.append(_cmul(*pw[-1], ar, ai))
    a8 = pw[-1]
    a16 = _cmul(*a8, *a8)
    a32 = _cmul(*a16, *a16)
    tab[0] = full(ar)
    tab[1] = full(ai)
    for n, (s, (pr, pi)) in enumerate(((1, a8), (2, a16), (4, a32))):
        keep = (row < SUBLANES - s) if reverse else (row >= s)
        tab[2 + 2 * n] = jnp.where(keep, pr, zero)
        tab[3 + 2 * n] = jnp.where(keep, pi, zero)
    cur = a8
    qr, qi = zero, zero
    for n in range(SUBLANES):
        at = (SUBLANES - 1 - n) if reverse else n
        qr = jnp.where(row == at, cur[0], qr)
        qi = jnp.where(row == at, cur[1], qi)
        cur = _cmul(*cur, *a8)
    tab[8] = qr
    tab[9] = qi
    for tau in range(SUBLANES):
        pr, pi = pw[SUBLANES - 1 - tau] if reverse else pw[tau]
        tab[10 + 2 * tau] = full(pr)
        tab[11 + 2 * tau] = full(pi)


def _cmac(xr, xi, ar, ai, yr, yi):
    return xr + ar * yr - ai * yi, xi + ar * yi + ai * yr


def _chain_segments(er, ei, c_r, c_i, tab, cols, reverse):
    tr, ti = er, ei
    for n, s in enumerate((1, 2, 4)):
        shift = SUBLANES - s if reverse else s
        tr, ti = _cmac(tr, ti, tab[2 + 2 * n, :, cols], tab[3 + 2 * n, :, cols],
                       pltpu.roll(tr, shift, 0), pltpu.roll(ti, shift, 0))
    return _cmac(tr, ti, tab[8, :, cols], tab[9, :, cols], c_r, c_i)


def _ssm_fwd(u, layer, bpad, cpad, ar, ai, dskip):
    L = u.shape[0]
    ts = min(TS, L)
    nq = 4
    cq = N_STATE // nq

    def body(u_ref, bp_ref, cp_ref, ar_ref, ai_ref, dsk_ref, sre_ref, sim_ref, y_ref, cr, ci, tab, up, yp):
        t = pl.program_id(1)

        @pl.when(t == 0)
        def _():
            cr[...] = jnp.zeros_like(cr)
            ci[...] = jnp.zeros_like(ci)
            _scan_tables(ar_ref[...], ai_ref[...], tab, reverse=False)

        _permute_rows(u_ref, up, ts)
        uf = up[...]
        ub = uf.astype(BF16)
        for jj in range(4):
            bu = _dot(ub, bp_ref[jj])
            sre_ref[:, jj * 128:(jj + 1) * 128] = bu[:, :128]
            sim_ref[:, jj * 128:(jj + 1) * 128] = bu[:, 128:]

        shp = (SUBLANES, SCAN_LANES)
        first_row = lax.broadcasted_iota(jnp.int32, shp, 0) == 0
        for cc in range(cq // SCAN_LANES):
            cols = slice(cc * SCAN_LANES, (cc + 1) * SCAN_LANES)

            def block(b, carry, cols=cols):
                c_r, c_i = carry
                base = pl.multiple_of(b * SCAN_BLOCK, SCAN_BLOCK)
                rows = lambda tau: pl.ds(base + SUBLANES * tau, SUBLANES)
                a_r, a_i = tab[0, :, cols], tab[1, :, cols]
                ys = [(sre_ref[rows(0), cols], sim_ref[rows(0), cols])]
                for tau in range(1, SUBLANES):
                    ys.append(_cmac(sre_ref[rows(tau), cols], sim_ref[rows(tau), cols], a_r, a_i, *ys[-1]))
                tr, ti = _chain_segments(*ys[-1], c_r, c_i, tab, cols, reverse=False)
                in_r = jnp.where(first_row, c_r, pltpu.roll(tr, 1, 0))
                in_i = jnp.where(first_row, c_i, pltpu.roll(ti, 1, 0))
                for tau in range(SUBLANES):
                    sr, si = _cmac(*ys[tau], tab[10 + 2 * tau, :, cols], tab[11 + 2 * tau, :, cols], in_r, in_i)
                    sre_ref[rows(tau), cols] = sr
                    sim_ref[rows(tau), cols] = si
                return (jnp.broadcast_to(tr[SUBLANES - 1:, :], shp), jnp.broadcast_to(ti[SUBLANES - 1:, :], shp))

            c_r, c_i = lax.fori_loop(0, ts // SCAN_BLOCK, block, (cr[:, cols], ci[:, cols]), unroll=2)
            cr[:, cols] = c_r
            ci[:, cols] = c_i

        acc = dsk_ref[...] * uf
        for jj in range(4):
            cols = slice(jj * 128, (jj + 1) * 128)
            scat = jnp.concatenate([sre_ref[:, cols], sim_ref[:, cols]], axis=1).astype(BF16)
            acc = acc + _dot(scat, cp_ref[jj])
        yp[...] = acc
        _permute_rows(yp, y_ref, ts)

    return pl.pallas_call(
        body, name="ssm_fwd", grid=(nq, L // ts),
        in_specs=[pl.BlockSpec((ts, 128), lambda q, t: (t, 4 + q)),
                  pl.BlockSpec((None, 4, 128, 256), lambda q, t: (layer, q, 0, 0)),
                  pl.BlockSpec((None, 4, 256, 128), lambda q, t: (layer, q, 0, 0)),
                  pl.BlockSpec((None, 1, cq), lambda q, t: (layer, 0, q)),
                  pl.BlockSpec((None, 1, cq), lambda q, t: (layer, 0, q)),
                  pl.BlockSpec((None, 1, 128), lambda q, t: (layer, 0, q))],
        out_specs=[pl.BlockSpec((ts, cq), lambda q, t: (t, q)),
                   pl.BlockSpec((ts, cq), lambda q, t: (t, q)),
                   pl.BlockSpec((ts, 128), lambda q, t: (t, q))],
        out_shape=[jax.ShapeDtypeStruct((L, N_STATE), F32), jax.ShapeDtypeStruct((L, N_STATE), F32),
                   jax.ShapeDtypeStruct((L, D_SSM), F32)],
        scratch_shapes=[pltpu.VMEM((SUBLANES, cq), F32), pltpu.VMEM((SUBLANES, cq), F32),
                        pltpu.VMEM((N_SCAN_TABLES, SUBLANES, cq), F32),
                        pltpu.VMEM((ts, 128), F32), pltpu.VMEM((ts, 128), F32)],
        compiler_params=_cparams(2),
    )(u, bpad, cpad, ar, ai, dskip)


def _mix_out_fwd(yraw, ypool, h, wp, layer, b_glu):
    L = h.shape[0]
    tm = min(TM, L)

    def body(yr_ref, yp_ref, h_ref, wglu_ref, b_ref, wout_ref, o_ref):
        y = _gelu(yr_ref[...])
        z = _dot(y.astype(BF16), _glu_weight(wglu_ref)) + b_ref[...]
        o = y * _sigmoid(z)
        mix = jnp.concatenate([yp_ref[...], o], axis=1).astype(BF16)
        o_ref[...] = h_ref[...] + _dot(mix, wout_ref[...].reshape(D_MODEL, D_MODEL))

    gb, gi = P_GLU_BLK
    ob, oi = P_OUT_BLK
    return pl.pallas_call(
        body, name="mix_out_fwd", grid=(L // tm,),
        in_specs=[pl.BlockSpec((tm, D_SSM), lambda i: (i, 0)),
                  pl.BlockSpec((tm, D_POOL), lambda i: (i, 0)),
                  pl.BlockSpec((tm, D_MODEL), lambda i: (i, 0)),
                  pl.BlockSpec((N_SHARD, None, gb, D_MODEL), lambda i: (0, 0, gi, 0)),
                  pl.BlockSpec((None, 1, D_SSM), lambda i: (layer, 0, 0)),
                  pl.BlockSpec((N_SHARD, None, ob, D_MODEL), lambda i: (0, 0, oi, 0))],
        out_specs=pl.BlockSpec((tm, D_MODEL), lambda i: (i, 0)),
        out_shape=jax.ShapeDtypeStruct((L, D_MODEL), F32),
        compiler_params=_cparams(1),
    )(yraw, ypool, h, wp, b_glu, wp)


def _ffn_weights(ref, k):
    return ref[k, 0:FF_SHARD, :], ref[k, FF_SHARD:2 * FF_SHARD, :], ref[k, 2 * FF_SHARD:P_FF_ROWS, :]


def _ffn_weight_spec():
    return pl.BlockSpec((N_SHARD, None, P_FF_ROWS, D_MODEL), lambda m, k: (0, 0, 0, 0),
                        pipeline_mode=pl.Buffered(1))


def _ffn_fwd(h, g2, wp, layer):
    L = h.shape[0]
    tm = min(TM_FFN_LONG, L)

    def body(h_ref, g_ref, w_ref, o_ref, n2_ref, act_ref, dgate_ref, dup_ref):
        k = pl.program_id(1)

        @pl.when(k == 0)
        def _():
            x = h_ref[...]
            xhat, _ = _rms_hat(x)
            n2_ref[...] = (xhat * g_ref[...]).astype(BF16)
            o_ref[...] = x

        wd, wg_t, wu_t = _ffn_weights(w_ref, k)
        n2 = n2_ref[...]
        gate = _dot_nt(n2, wg_t)
        up = _dot_nt(n2, wu_t)
        sg = _sigmoid(gate)
        silu = gate * sg
        act = (silu * up).astype(BF16)
        act_ref[...] = act
        dgate_ref[...] = (up * (sg * (1.0 + gate * (1.0 - sg)))).astype(BF16)
        dup_ref[...] = silu.astype(BF16)
        o_ref[...] += _dot(act, wd)

    act_shape = jax.ShapeDtypeStruct((N_SHARD, L, FF_SHARD), BF16)
    return pl.pallas_call(
        body, name="ffn_fwd", grid=(L // tm, N_SHARD),
        in_specs=[pl.BlockSpec((tm, D_MODEL), lambda m, k: (m, 0)),
                  pl.BlockSpec((None, 1, D_MODEL), lambda m, k: (layer, 0, 0)),
                  _ffn_weight_spec()],
        out_specs=[pl.BlockSpec((tm, D_MODEL), lambda m, k: (m, 0)),
                   pl.BlockSpec((tm, D_MODEL), lambda m, k: (m, 0)),
                   pl.BlockSpec((None, tm, FF_SHARD), lambda m, k: (k, m, 0)),
                   pl.BlockSpec((None, tm, FF_SHARD), lambda m, k: (k, m, 0)),
                   pl.BlockSpec((None, tm, FF_SHARD), lambda m, k: (k, m, 0))],
        out_shape=[jax.ShapeDtypeStruct((L, D_MODEL), F32), jax.ShapeDtypeStruct((L, D_MODEL), BF16),
                   act_shape, act_shape, act_shape],
        compiler_params=_cparams(2),
    )(h, g2, wp)


def _final_fwd_bwd(h, gf, target):
    L = h.shape[0]
    tm = min(TM, L)

    def body(h_ref, g_ref, t_ref, dh_ref, loss_ref, dg_ref):
        i = pl.program_id(0)

        @pl.when(i == 0)
        def _():
            loss_ref[...] = jnp.zeros_like(loss_ref)
            dg_ref[...] = jnp.zeros_like(dg_ref)

        xhat, r = _rms_hat(h_ref[...])
        g = g_ref[...]
        e = xhat * g - t_ref[...]
        loss_ref[...] += 0.5 * jnp.sum(jnp.mean(e * e, axis=-1, keepdims=True), axis=0, keepdims=True)
        dy = e * (1.0 / D_MODEL)
        dg_ref[...] += jnp.sum(dy * xhat, axis=0, keepdims=True)
        dh_ref[...] = _rms_bwd(dy * g, xhat, r)

    return pl.pallas_call(
        body, name="final_fwd_bwd", grid=(L // tm,),
        in_specs=[pl.BlockSpec((tm, D_MODEL), lambda i: (i, 0)),
                  pl.BlockSpec((1, D_MODEL), lambda i: (0, 0)),
                  pl.BlockSpec((tm, D_MODEL), lambda i: (i, 0))],
        out_specs=[pl.BlockSpec((tm, D_MODEL), lambda i: (i, 0)),
                   pl.BlockSpec((1, 1), lambda i: (0, 0)),
                   pl.BlockSpec((1, D_MODEL), lambda i: (0, 0))],
        out_shape=[jax.ShapeDtypeStruct((L, D_MODEL), F32), jax.ShapeDtypeStruct((1, 1), F32),
                   jax.ShapeDtypeStruct((1, D_MODEL), F32)],
        compiler_params=_cparams(1),
    )(h, gf, target)


def _ffn_bwd_act(dh, h, g2, fgate_s, fup_s, wp, layer):
    L = h.shape[0]
    tm = min(TM_FFN, L)
    sub = tm // FFN_SPLIT

    def body(dh_ref, h_ref, g_ref, fgate_ref, fup_ref, w_ref,
             dhm_ref, dg_ref, dgate_ref, dup_ref, dhb_ref):
        m, k = pl.program_id(0), pl.program_id(1)
        dn2 = dhm_ref

        @pl.when(jnp.logical_and(m == 0, k == 0))
        def _():
            dg_ref[...] = jnp.zeros_like(dg_ref)

        @pl.when(k == 0)
        def _():
            dhb_ref[...] = dh_ref[...].astype(BF16)
            dn2[...] = jnp.zeros_like(dn2)

        wd, wg_t, wu_t = _ffn_weights(w_ref, k)
        for rows in (slice(r * sub, (r + 1) * sub) for r in range(tm // sub)):
            dact = _dot_nt(dhb_ref[rows, :], wd)
            dgate = (dact * fgate_ref[rows, :].astype(F32)).astype(BF16)
            dup = (dact * fup_ref[rows, :].astype(F32)).astype(BF16)
            dgate_ref[rows, :] = dgate
            dup_ref[rows, :] = dup
            dn2[rows, :] += _dot(dgate, wg_t) + _dot(dup, wu_t)

        @pl.when(k == N_SHARD - 1)
        def _():
            xhat, r = _rms_hat(h_ref[...])
            d = dn2[...]
            dg_ref[...] += jnp.sum(d * xhat, axis=0, keepdims=True)
            dhm_ref[...] = dh_ref[...] + _rms_bwd(d * g_ref[...], xhat, r)

    act_spec = pl.BlockSpec((None, tm, FF_SHARD), lambda m, k: (k, m, 0))
    act_shape = jax.ShapeDtypeStruct((N_SHARD, L, FF_SHARD), BF16)
    row_spec = pl.BlockSpec((tm, D_MODEL), lambda m, k: (m, 0))
    return pl.pallas_call(
        body, name="ffn_bwd_act", grid=(L // tm, N_SHARD),
        in_specs=[row_spec, row_spec,
                  pl.BlockSpec((None, 1, D_MODEL), lambda m, k: (layer, 0, 0)),
                  act_spec, act_spec,
                  _ffn_weight_spec()],
        out_specs=[row_spec,
                   pl.BlockSpec((1, D_MODEL), lambda m, k: (0, 0)),
                   act_spec, act_spec, row_spec],
        out_shape=[jax.ShapeDtypeStruct((L, D_MODEL), F32), jax.ShapeDtypeStruct((1, D_MODEL), F32),
                   act_shape, act_shape, jax.ShapeDtypeStruct((L, D_MODEL), BF16)],
        compiler_params=_cparams(2),
    )(dh, h, g2, fgate_s, fup_s, wp)


def _ffn_bwd_w(n2, dgate_s, dup_s, act_s, dhb, gbuf):
    L = n2.shape[0]
    tm = min(TM_FFN_LONG, L)

    def body(n2_ref, dgate_ref, dup_ref, act_ref, dhb_ref, g_in, g_ref):
        m = pl.program_id(1)

        @pl.when(m == 0)
        def _():
            g_ref[...] = jnp.zeros_like(g_ref)

        n2v = n2_ref[...]
        g_ref[0:FF_SHARD, :] += _dot_tn(act_ref[...], dhb_ref[...])
        g_ref[FF_SHARD:2 * FF_SHARD, :] += _dot_tn(dgate_ref[...], n2v)
        g_ref[2 * FF_SHARD:P_FF_ROWS, :] += _dot_tn(dup_ref[...], n2v)

    act_spec = pl.BlockSpec((None, tm, FF_SHARD), lambda k, m: (k, m, 0))
    row_spec = pl.BlockSpec((tm, D_MODEL), lambda k, m: (m, 0))
    return pl.pallas_call(
        body, name="ffn_bwd_w", grid=(N_SHARD, L // tm),
        in_specs=[row_spec, act_spec, act_spec, act_spec, row_spec, pl.BlockSpec(memory_space=pl.ANY)],
        out_specs=pl.BlockSpec((None, None, P_FF_ROWS, D_MODEL), lambda k, m: (0, k, 0, 0)),
        out_shape=jax.ShapeDtypeStruct(gbuf.shape, F32),
        input_output_aliases={5: 0},
        compiler_params=_cparams(2),
    )(n2, dgate_s, dup_s, act_s, dhb, gbuf)


def _mix_out_bwd(dhm, yraw, ypool, wp, layer, b_glu, gbuf):
    L = dhm.shape[0]
    tm = min(TM, L)

    def body(dhm_ref, yr_ref, yp_ref, wglu_ref, b_ref, wout_ref, g1_in,
             dyr_ref, dyp_ref, db_ref, g1_ref, dwout, dwglu, gpack):
        i = pl.program_id(0)

        @pl.when(i == 0)
        def _():
            db_ref[...] = jnp.zeros_like(db_ref)
            dwout[...] = jnp.zeros_like(dwout)
            dwglu[...] = jnp.zeros_like(dwglu)

        dhb = dhm_ref[...].astype(BF16)
        wglu = _glu_weight(wglu_ref)
        dmix = _dot_nt(dhb, wout_ref[...].reshape(D_MODEL, D_MODEL))
        dyp_ref[...] = dmix[:, :D_POOL]
        d_o = dmix[:, D_POOL:]
        yraw_v = yr_ref[...]
        y = _gelu(yraw_v)
        yb = y.astype(BF16)
        sig = _sigmoid(_dot(yb, wglu) + b_ref[...])
        mix = jnp.concatenate([yp_ref[...], y * sig], axis=1).astype(BF16)
        dwout[...] += _dot_tn(mix, dhb).reshape(N_SHARD, 256, D_MODEL)
        dz = d_o * y * sig * (1.0 - sig)
        dzb = dz.astype(BF16)
        db_ref[...] += jnp.sum(dz, axis=0, keepdims=True)
        dwglu[...] += _dot_tn(yb, dzb)
        dy = d_o * sig + _dot_nt(dzb, wglu)
        dyr_ref[...] = dy * _gelu_grad(yraw_v)

        @pl.when(i == n_steps - 1)
        def _():
            gpack[:, :gb, :] = _glu_pack(dwglu[...])
            gpack[:, gb:, :] = jnp.zeros((N_SHARD, P_GLU_PAD - gb, D_MODEL), F32)
            pltpu.sync_copy(gpack, g1_ref.at[0, :, pl.ds(gb * gi, P_GLU_PAD), :])
            pltpu.sync_copy(dwout, g1_ref.at[0, :, pl.ds(ob * oi, ob), :])

    gb, gi = P_GLU_BLK
    ob, oi = P_OUT_BLK
    n_steps = L // tm
    return pl.pallas_call(
        body, name="mix_out_bwd", grid=(n_steps,),
        in_specs=[pl.BlockSpec((tm, D_MODEL), lambda i: (i, 0)),
                  pl.BlockSpec((tm, D_SSM), lambda i: (i, 0)),
                  pl.BlockSpec((tm, D_POOL), lambda i: (i, 0)),
                  pl.BlockSpec((N_SHARD, None, gb, D_MODEL), lambda i: (0, 0, gi, 0)),
                  pl.BlockSpec((None, 1, D_SSM), lambda i: (layer, 0, 0)),
                  pl.BlockSpec((N_SHARD, None, ob, D_MODEL), lambda i: (0, 0, oi, 0)),
                  pl.BlockSpec(memory_space=pl.ANY)],
        out_specs=[pl.BlockSpec((tm, D_SSM), lambda i: (i, 0)),
                   pl.BlockSpec((tm, D_POOL), lambda i: (i, 0)),
                   pl.BlockSpec((1, D_SSM), lambda i: (0, 0)),
                   pl.BlockSpec(memory_space=pl.ANY)],
        out_shape=[jax.ShapeDtypeStruct((L, D_SSM), F32), jax.ShapeDtypeStruct((L, D_POOL), F32),
                   jax.ShapeDtypeStruct((1, D_SSM), F32),
                   jax.ShapeDtypeStruct(gbuf.shape, F32)],
        scratch_shapes=[pltpu.VMEM((N_SHARD, ob, D_MODEL), F32), pltpu.VMEM((D_SSM, D_SSM), F32),
                        pltpu.VMEM((N_SHARD, P_GLU_PAD, D_MODEL), F32)],
        input_output_aliases={6: 3},
        compiler_params=_cparams(1),
    )(dhm, yraw, ypool, wp, b_glu, wp, gbuf)


def _ssm_bwd(dyraw, u, sre, sim, layer, cpad_t, bpad_t, ar, ai, dskip):
    L = u.shape[0]
    ts = min(TS, L)
    nt = L // ts
    nq = 4
    cq = N_STATE // nq

    def body(dy_ref, u_ref, sre_ref, sim_ref, ct_ref, bt_ref, ar_ref, ai_ref, dsk_ref,
             du_ref, dcp_ref, dbp_ref, dar_ref, dai_ref, ddsk_ref, gre, gim, cr, ci, tab, accr, acci, up, dyp):
        t = pl.program_id(1)

        @pl.when(t == 0)
        def _():
            for ref in (cr, ci, accr, acci, dcp_ref, dbp_ref, ddsk_ref):
                ref[...] = jnp.zeros_like(ref)
            _scan_tables(ar_ref[...], -ai_ref[...], tab, reverse=True)

        _permute_rows(dy_ref, dyp, ts)
        _permute_rows(u_ref, up, ts)
        dy = dyp[...]
        dyb = dy.astype(BF16)
        uf = up[...]
        ub = uf.astype(BF16)
        for jj in range(4):
            cols = slice(jj * 128, (jj + 1) * 128)
            ds = _dot(dyb, ct_ref[jj])
            gre[:, cols] = ds[:, :128]
            gim[:, cols] = ds[:, 128:]
            scat = jnp.concatenate([sre_ref[:, cols], sim_ref[:, cols]], axis=1).astype(BF16)
            dcp_ref[jj] += _dot_tn(scat, dyb)

        n_blk = ts // SCAN_BLOCK
        shp = (SUBLANES, SCAN_LANES)
        last_row = lax.broadcasted_iota(jnp.int32, shp, 0) == SUBLANES - 1
        for cc in range(cq // SCAN_LANES):
            cols = slice(cc * SCAN_LANES, (cc + 1) * SCAN_LANES)

            def block(i, carry, cols=cols):
                c_r, c_i, a_r, a_i = carry
                base = pl.multiple_of((n_blk - 1 - i) * SCAN_BLOCK, SCAN_BLOCK)
                rows = lambda tau: pl.ds(base + SUBLANES * tau, SUBLANES)
                m_r, m_i = tab[0, :, cols], tab[1, :, cols]
                ys = [None] * SUBLANES
                ys[SUBLANES - 1] = (gre[rows(SUBLANES - 1), cols], gim[rows(SUBLANES - 1), cols])
                for tau in reversed(range(SUBLANES - 1)):
                    ys[tau] = _cmac(gre[rows(tau), cols], gim[rows(tau), cols], m_r, m_i, *ys[tau + 1])
                tr, ti = _chain_segments(*ys[0], c_r, c_i, tab, cols, reverse=True)
                in_r = jnp.where(last_row, c_r, pltpu.roll(tr, SUBLANES - 1, 0))
                in_i = jnp.where(last_row, c_i, pltpu.roll(ti, SUBLANES - 1, 0))
                gs = [_cmac(*ys[tau], tab[10 + 2 * tau, :, cols], tab[11 + 2 * tau, :, cols], in_r, in_i)
                      for tau in range(SUBLANES)]
                for tau in range(SUBLANES):
                    gre[rows(tau), cols] = gs[tau][0]
                    gim[rows(tau), cols] = gs[tau][1]
                    if tau < SUBLANES - 1:
                        nr, ni = gs[tau + 1]
                    else:
                        nr = jnp.where(last_row, c_r, pltpu.roll(gs[0][0], SUBLANES - 1, 0))
                        ni = jnp.where(last_row, c_i, pltpu.roll(gs[0][1], SUBLANES - 1, 0))
                    sr, si = sre_ref[rows(tau), cols], sim_ref[rows(tau), cols]
                    a_r = a_r + sr * nr + si * ni
                    a_i = a_i + sr * ni - si * nr
                return (jnp.broadcast_to(tr[:1, :], shp), jnp.broadcast_to(ti[:1, :], shp), a_r, a_i)

            c_r, c_i, a_r, a_i = lax.fori_loop(
                0, n_blk, block, (cr[:, cols], ci[:, cols], accr[:, cols], acci[:, cols]), unroll=2)
            cr[:, cols] = c_r
            ci[:, cols] = c_i
            accr[:, cols] = a_r
            acci[:, cols] = a_i

        acc = dsk_ref[...] * dy
        for jj in range(4):
            cols = slice(jj * 128, (jj + 1) * 128)
            gcat = jnp.concatenate([gre[:, cols], gim[:, cols]], axis=1).astype(BF16)
            acc = acc + _dot(gcat, bt_ref[jj])
            dbp_ref[jj] += _dot_tn(ub, gcat)
        ddsk_ref[...] += jnp.sum(dy * uf, axis=0, keepdims=True)
        dyp[...] = acc
        _permute_rows(dyp, du_ref, ts)

        @pl.when(t == nt - 1)
        def _():
            dar_ref[...] = jnp.sum(accr[...], axis=0, keepdims=True)
            dai_ref[...] = jnp.sum(acci[...], axis=0, keepdims=True)

    f32_scr = lambda *s: pltpu.VMEM(s, F32)
    return pl.pallas_call(
        body, name="ssm_bwd", grid=(nq, nt),
        in_specs=[pl.BlockSpec((ts, 128), lambda q, t: (nt - 1 - t, q)),
                  pl.BlockSpec((ts, 128), lambda q, t: (nt - 1 - t, 4 + q)),
                  pl.BlockSpec((ts, cq), lambda q, t: (nt - 1 - t, q)),
                  pl.BlockSpec((ts, cq), lambda q, t: (nt - 1 - t, q)),
                  pl.BlockSpec((None, 4, 128, 256), lambda q, t: (layer, q, 0, 0)),
                  pl.BlockSpec((None, 4, 256, 128), lambda q, t: (layer, q, 0, 0)),
                  pl.BlockSpec((None, 1, cq), lambda q, t: (layer, 0, q)),
                  pl.BlockSpec((None, 1, cq), lambda q, t: (layer, 0, q)),
                  pl.BlockSpec((None, 1, 128), lambda q, t: (layer, 0, q))],
        out_specs=[pl.BlockSpec((ts, 128), lambda q, t: (nt - 1 - t, q)),
                   pl.BlockSpec((4, 256, 128), lambda q, t: (q, 0, 0)),
                   pl.BlockSpec((4, 128, 256), lambda q, t: (q, 0, 0)),
                   pl.BlockSpec((1, cq), lambda q, t: (0, q)),
                   pl.BlockSpec((1, cq), lambda q, t: (0, q)),
                   pl.BlockSpec((1, 128), lambda q, t: (0, q))],
        out_shape=[jax.ShapeDtypeStruct((L, D_SSM), F32),
                   jax.ShapeDtypeStruct((N_PAIRS, 256, 128), F32), jax.ShapeDtypeStruct((N_PAIRS, 128, 256), F32),
                   jax.ShapeDtypeStruct((1, N_STATE), F32), jax.ShapeDtypeStruct((1, N_STATE), F32),
                   jax.ShapeDtypeStruct((1, D_SSM), F32)],
        scratch_shapes=[f32_scr(ts, cq), f32_scr(ts, cq), f32_scr(SUBLANES, cq), f32_scr(SUBLANES, cq),
                        f32_scr(N_SCAN_TABLES, SUBLANES, cq), f32_scr(SUBLANES, cq), f32_scr(SUBLANES, cq),
                        f32_scr(ts, 128), f32_scr(ts, 128)],
        compiler_params=_cparams(2),
    )(dyraw, u, sre, sim, cpad_t, bpad_t, ar, ai, dskip)


def _pool_bwd(dyp, u, layer, w_pool, scale):
    L = u.shape[0]
    tm = min(TM, L)
    nt = L // tm
    halo_per_tile = tm // POOL_HALO

    def body(dyp_ref, u_ref, halo_ref, wp_ref, sc_ref, du_ref, dwp_ref, dsc_ref, carry):
        i = pl.program_id(0)
        tile = nt - 1 - i

        @pl.when(i == 0)
        def _():
            carry[...] = jnp.zeros_like(carry)
            dwp_ref[...] = jnp.zeros_like(dwp_ref)
            dsc_ref[...] = jnp.zeros_like(dsc_ref)

        up = u_ref[...]
        halo = jnp.where(tile > 0, halo_ref[...], jnp.zeros_like(halo_ref))
        diffs = _pool_diff(jnp.concatenate([halo, up], axis=0), tile * tm, tm)
        rows = tile * tm + lax.broadcasted_iota(jnp.int32, (tm, 1), 0)
        n_ext = tm + POOL_HALO
        for gi, w in enumerate(POOL_WINDOWS):
            cols = slice(gi * POOL_GROUP, (gi + 1) * POOL_GROUP)
            db = diffs[gi].astype(BF16)
            dyp = dyp_ref[:, cols]
            dsc_ref[:, cols] += jnp.sum(dyp * _dot(db, wp_ref[gi]), axis=0, keepdims=True)
            dp = (dyp * sc_ref[:, cols]).astype(BF16)
            ddiff = _dot_nt(dp, wp_ref[gi])
            dwp_ref[gi] += _dot_tn(db, dp)
            e = ddiff * (1.0 / jnp.minimum(rows + 1, w).astype(F32))
            s = jnp.concatenate([e, carry[:, cols]], axis=0)
            k = 1
            while k < w:
                s = s + pltpu.roll(s, n_ext - k, 0)
                k *= 2
            du_ref[:, cols] = s[:tm, :] - ddiff
            carry[:, cols] = e[:POOL_HALO, :]

    return pl.pallas_call(
        body, name="pool_bwd", grid=(nt,),
        in_specs=[pl.BlockSpec((tm, D_POOL), lambda i: (nt - 1 - i, 0)),
                  pl.BlockSpec((tm, D_POOL), lambda i: (nt - 1 - i, 0)),
                  pl.BlockSpec((POOL_HALO, D_POOL), lambda i: (jnp.maximum((nt - 1 - i) * halo_per_tile - 1, 0), 0)),
                  pl.BlockSpec((None, 4, POOL_GROUP, POOL_GROUP), lambda i: (layer, 0, 0, 0)),
                  pl.BlockSpec((None, 1, D_POOL), lambda i: (layer, 0, 0))],
        out_specs=[pl.BlockSpec((tm, D_POOL), lambda i: (nt - 1 - i, 0)),
                   pl.BlockSpec((4, POOL_GROUP, POOL_GROUP), lambda i: (0, 0, 0)),
                   pl.BlockSpec((1, D_POOL), lambda i: (0, 0))],
        out_shape=[jax.ShapeDtypeStruct((L, D_POOL), F32),
                   jax.ShapeDtypeStruct((4, POOL_GROUP, POOL_GROUP), F32),
                   jax.ShapeDtypeStruct((1, D_POOL), F32)],
        scratch_shapes=[pltpu.VMEM((POOL_HALO, D_POOL), F32)],
        compiler_params=_cparams(1),
    )(dyp, u, u, w_pool, scale)


def _mix_in_bwd(dup, dus, h, dhm, g1, wp, layer, gbuf):
    L = h.shape[0]
    tm = min(TM, L)
    n_steps = L // tm
    blk, idx = P_IN_BLK

    def body(dup_ref, dus_ref, h_ref, dhm_ref, g_ref, w_ref, g1_in, dh_ref, dg_ref, g1_ref, dwin):
        i = pl.program_id(0)

        @pl.when(i == 0)
        def _():
            dg_ref[...] = jnp.zeros_like(dg_ref)
            dwin[...] = jnp.zeros_like(dwin)

        du = jnp.concatenate([dup_ref[...], dus_ref[...]], axis=1).astype(BF16)
        dn1 = _dot_nt(du, w_ref[...].reshape(D_MODEL, D_MODEL))
        xhat, r = _rms_hat(h_ref[...])
        g = g_ref[...]
        n1 = (xhat * g).astype(BF16)
        dwin[...] += _dot_tn(n1, du).reshape(N_SHARD, blk, D_MODEL)
        dg_ref[...] += jnp.sum(dn1 * xhat, axis=0, keepdims=True)
        dh_ref[...] = dhm_ref[...] + _rms_bwd(dn1 * g, xhat, r)

        @pl.when(i == n_steps - 1)
        def _():
            pltpu.sync_copy(dwin, g1_ref.at[0, :, pl.ds(blk * idx, blk), :])

    row_spec = pl.BlockSpec((tm, D_MODEL), lambda i: (i, 0))
    half_spec = pl.BlockSpec((tm, D_POOL), lambda i: (i, 0))
    return pl.pallas_call(
        body, name="mix_in_bwd", grid=(n_steps,),
        in_specs=[half_spec, half_spec, row_spec, row_spec,
                  pl.BlockSpec((None, 1, D_MODEL), lambda i: (layer, 0, 0)),
                  pl.BlockSpec((N_SHARD, None, blk, D_MODEL), lambda i: (0, 0, idx, 0)),
                  pl.BlockSpec(memory_space=pl.ANY)],
        out_specs=[row_spec, pl.BlockSpec((1, D_MODEL), lambda i: (0, 0)), pl.BlockSpec(memory_space=pl.ANY)],
        out_shape=[jax.ShapeDtypeStruct((L, D_MODEL), F32), jax.ShapeDtypeStruct((1, D_MODEL), F32),
                   jax.ShapeDtypeStruct(gbuf.shape, F32)],
        scratch_shapes=[pltpu.VMEM((N_SHARD, blk, D_MODEL), F32)],
        input_output_aliases={6: 2},
        compiler_params=_cparams(1),
    )(dup, dus, h, dhm, g1, wp, gbuf)


def _disc_math(lr, li, ldt, br_t, bi_t):
    dt = jnp.exp(ldt)
    mag = jnp.exp(lr * dt)
    ang = li * dt
    ar = mag * jnp.cos(ang)
    ai = mag * jnp.sin(ang)
    den = lr * lr + li * li
    nr, ni = ar - 1.0, ai
    cr = (nr * lr + ni * li) / den
    ci = (ni * lr - nr * li) / den
    return ar, ai, cr * br_t - ci * bi_t, cr * bi_t + ci * br_t


def _disc_fwd(lr, li, ldt, br_t, bi_t):
    def body(lr_ref, li_ref, ldt_ref, br_ref, bi_ref, ar_ref, ai_ref, bbr_ref, bbi_ref):
        ar, ai, bbr, bbi = _disc_math(lr_ref[...], li_ref[...], ldt_ref[...], br_ref[...], bi_ref[...])
        ar_ref[...] = ar
        ai_ref[...] = ai
        bbr_ref[...] = bbr
        bbi_ref[...] = bbi

    shapes = [jax.ShapeDtypeStruct(a.shape, F32) for a in (lr, li, br_t, bi_t)]
    return pl.pallas_call(body, name="ssm_disc_fwd", out_shape=shapes,
                          compiler_params=pltpu.CompilerParams(vmem_limit_bytes=VMEM_LIMIT))(lr, li, ldt, br_t, bi_t)


def _disc_bwd(lr, li, ldt, br_t, bi_t, dar, dai, dbbr, dbbi):
    def body(lr_ref, li_ref, ldt_ref, br_ref, bi_ref, dar_ref, dai_ref, dbbr_ref, dbbi_ref,
             dlr_ref, dli_ref, dldt_ref, dbr_ref, dbi_ref):
        prim = (lr_ref[...], li_ref[...], ldt_ref[...], br_ref[...], bi_ref[...])
        _, pullback = jax.vjp(_disc_math, *prim)
        dlr, dli, dldt, dbr, dbi = pullback((dar_ref[...], dai_ref[...], dbbr_ref[...], dbbi_ref[...]))
        dlr_ref[...] = dlr
        dli_ref[...] = dli
        dldt_ref[...] = dldt
        dbr_ref[...] = dbr
        dbi_ref[...] = dbi

    shapes = [jax.ShapeDtypeStruct(a.shape, F32) for a in (lr, li, ldt, br_t, bi_t)]
    return pl.pallas_call(body, name="ssm_disc_bwd", out_shape=shapes,
                          compiler_params=pltpu.CompilerParams(vmem_limit_bytes=VMEM_LIMIT))(
        lr, li, ldt, br_t, bi_t, dar, dai, dbbr, dbbi)


def _pad_pairs(m_re, m_im):
    def blocks(m):
        v = m.transpose(0, 2, 1).reshape(N_PAIRS, 2, SSM_GROUP, SSM_STATE)
        return jnp.einsum("ab,jahp->jahbp", jnp.eye(2, dtype=m.dtype), v).reshape(N_PAIRS, 32, 128)
    both = jnp.concatenate([blocks(m_re), blocks(m_im)], axis=-1)
    place = jax.nn.one_hot(jnp.arange(N_PAIRS) % 4, 4, dtype=both.dtype)
    return jnp.einsum("jk,jrc->jkrc", place, both).reshape(N_PAIRS, 128, 256)


def _unpad_pairs(x):
    place = jax.nn.one_hot(jnp.arange(N_PAIRS) % 4, 4, dtype=x.dtype)
    both = jnp.einsum("jk,jkrc->jrc", place, x.reshape(N_PAIRS, 4, 32, 256))

    def unblock(v):
        v = v.reshape(N_PAIRS, 2, SSM_GROUP, 2, SSM_STATE)
        d = jnp.einsum("ab,jahbp->jahp", jnp.eye(2, dtype=x.dtype), v)
        return d.reshape(N_SSM_GROUPS, SSM_GROUP, SSM_STATE).transpose(0, 2, 1)
    return unblock(both[..., :128]), unblock(both[..., 128:])


def _adamw_math(w, g, m, v):
    m = ADAM_B1 * m + (1.0 - ADAM_B1) * g
    v = ADAM_B2 * v + (1.0 - ADAM_B2) * (g * g)
    m_hat = m / (1.0 - ADAM_B1 ** ADAM_STEP)
    v_hat = v / (1.0 - ADAM_B2 ** ADAM_STEP)
    delta = -ADAM_LR * (m_hat / (jnp.sqrt(v_hat) + ADAM_EPS) + ADAM_WD * w)
    return delta, m, v


def _adamw(name, layer, w, m, v, gbuf, g_block, g_row0, row_tile, outs=None, after=(), glu=False):
    nl, r, c = w.shape
    n_tiles = r // row_tile
    g_rows, g_cols = g_block
    g_tile = g_rows // n_tiles
    g_off = g_row0 // g_tile
    if outs is None:
        outs = [lax.empty(w.shape, F32) for _ in range(4)]

    def body(w_ref, m_ref, v_ref, g_ref, *rest):
        go_ref, d_ref, mo_ref, vo_ref = rest[-4:]
        g = g_ref[...]
        if glu:
            g = jnp.concatenate([g[:, :D_SSM], g[:, D_SSM:]], axis=0)
        delta, mn, vn = _adamw_math(w_ref[...], g, m_ref[...], v_ref[...])
        go_ref[...] = g
        d_ref[...] = delta
        mo_ref[...] = mn
        vo_ref[...] = vn

    w_spec = pl.BlockSpec((None, row_tile, c), lambda j: (layer, j, 0))
    shape = jax.ShapeDtypeStruct(w.shape, F32)
    return pl.pallas_call(
        body, name=name, grid=(n_tiles,),
        in_specs=[w_spec, w_spec, w_spec, pl.BlockSpec((None, g_tile, g_cols), lambda j: (0, g_off + j, 0))]
        + [_ANY] * (4 + len(after)),
        out_specs=[w_spec] * 4,
        out_shape=[shape] * 4,
        input_output_aliases={4: 0, 5: 1, 6: 2, 7: 3},
        compiler_params=_cparams(1),
    )(w, m, v, gbuf, *outs, *after)


def _adamw_group(name, layer, ws, ms, vs, gbuf, g_row0s, row_tile, outs=None):
    k = len(ws)
    nl, r, c = ws[0].shape
    n_tiles = r // row_tile
    if outs is None:
        outs = [[lax.empty(ws[0].shape, F32) for _ in range(4)] for _ in range(k)]

    def body(*refs):
        ins, results = refs[:4 * k], refs[-4 * k:]
        for i in range(k):
            w_ref, m_ref, v_ref, g_ref = (ins[j * k + i] for j in range(4))
            g = g_ref[...]
            delta, mn, vn = _adamw_math(w_ref[...], g, m_ref[...], v_ref[...])
            for ref, val in zip(results[4 * i:4 * i + 4], (g, delta, mn, vn)):
                ref[...] = val

    w_spec = pl.BlockSpec((None, row_tile, c), lambda j: (layer, j, 0))
    g_specs = [pl.BlockSpec((None, row_tile, c), functools.partial(lambda j, off: (0, off + j, 0), off=r0 // row_tile))
               for r0 in g_row0s]
    shape = jax.ShapeDtypeStruct(ws[0].shape, F32)
    flat = pl.pallas_call(
        body, name=name, grid=(n_tiles,),
        in_specs=[w_spec] * (3 * k) + g_specs + [_ANY] * (4 * k),
        out_specs=[w_spec] * (4 * k),
        out_shape=[shape] * (4 * k),
        input_output_aliases={4 * k + i: i for i in range(4 * k)},
        compiler_params=_cparams(1),
    )(*ws, *ms, *vs, *([gbuf] * k), *[a for group in outs for a in group])
    return [flat[4 * i:4 * i + 4] for i in range(k)]


def _pack_weights(ids, layer, w_in, w_glu, w_out, w_down, w_gate_t, w_up_t, after=()):
    gb, gi = P_GLU_BLK
    ib, ii = P_IN_BLK
    ob, oi = P_OUT_BLK

    def body(ids_ref, in_ref, glu_ref, out_ref, dn_ref, gate_ref, up_ref, *rest):
        p_ref = rest[-1]
        p_ref[0:FF_SHARD, :] = dn_ref[...].astype(BF16)
        p_ref[FF_SHARD:2 * FF_SHARD, :] = gate_ref[...].astype(BF16)
        p_ref[2 * FF_SHARD:P_FF_ROWS, :] = up_ref[...].astype(BF16)
        g = glu_ref[...]
        p_ref[gb * gi:gb * (gi + 1), :] = jnp.concatenate([g[:gb, :], g[gb:, :]], axis=1).astype(BF16)
        p_ref[gb * (gi + 1):ib * ii, :] = jnp.zeros((P_GLU_PAD - gb, D_MODEL), BF16)
        p_ref[ib * ii:ib * (ii + 1), :] = in_ref[...].astype(BF16)
        p_ref[ob * oi:ob * (oi + 1), :] = out_ref[...].astype(BF16)

    def spec(a):
        return pl.BlockSpec((None,) + a.shape[1:], lambda i, ids_ref: (layer, 0, 0))

    ins = (w_in, w_glu, w_out, w_down, w_gate_t, w_up_t)
    grid_spec = pltpu.PrefetchScalarGridSpec(
        num_scalar_prefetch=1, grid=(1,),
        in_specs=[spec(a) for a in ins] + [_ANY] * len(after),
        out_specs=pl.BlockSpec((None, None, P_ROWS, D_MODEL), lambda i, ids_ref: (ids_ref[1], 0, 0, 0)))
    return pl.pallas_call(
        body, name="pack_weights", grid_spec=grid_spec,
        out_shape=jax.ShapeDtypeStruct((N_SHARD, 1, P_ROWS, D_MODEL), BF16),
        compiler_params=_cparams(1),
    )(ids, *ins, *after)


MESH = pl.DeviceIdType.MESH
_ANY = pl.BlockSpec(memory_space=pl.ANY)
P_HALF = P_ROWS // 2
RS_ROW_TILE = 352


def _mesh_pos():
    return lax.axis_index("x"), lax.axis_index("y"), lax.axis_index("c")


def _other_chips(x, y):
    return [(1 - x, y), (x, 1 - y), (1 - x, 1 - y)]


def _remote(src, dst, send_sems, recv_sems, n, to):
    return pltpu.make_async_remote_copy(src_ref=src, dst_ref=dst, send_sem=send_sems.at[n],
                                        recv_sem=recv_sems.at[n], device_id=to, device_id_type=MESH)


_HBM = pl.BlockSpec(memory_space=pltpu.HBM)
_SEM = pl.BlockSpec(memory_space=pltpu.SEMAPHORE)
_EFFECT = pltpu.CompilerParams(has_side_effects=pltpu.SideEffectType.DATAFLOW_SIDE_EFFECTING)
_TOKEN = jax.ShapeDtypeStruct((8, 128), F32)


def _in_hbm(a):
    return pltpu.with_memory_space_constraint(a, pltpu.HBM)


def _ag_piece(ref, shard, half, rows):
    row0, n_rows = rows
    return ref.at[shard, :, pl.ds(row0 + half * (n_rows // 2), n_rows // 2), :]


def _ag_start(name, wp, after, row_ranges):
    n_sems = 3 * len(row_ranges)

    def body(w_ref, after_ref, send_sems, recv_sems, w_thru, token):
        x, y, c = _mesh_pos()
        for i, rows in enumerate(row_ranges):
            mine = _ag_piece(w_ref, 2 * x + y, c, rows)
            for j, (px, py) in enumerate(_other_chips(x, y)):
                _remote(mine, mine, send_sems, recv_sems, 3 * i + j, (px, py, c)).start()
        token[...] = jnp.zeros_like(token)

    return pl.pallas_call(
        body, name=name,
        out_shape=(pltpu.SemaphoreType.DMA((n_sems,)), pltpu.SemaphoreType.DMA((n_sems,)),
                   pltpu.HBM(wp.shape, wp.dtype), _TOKEN),
        in_specs=(_HBM, _ANY), out_specs=(_SEM, _SEM, _HBM, pl.BlockSpec(memory_space=pltpu.VMEM)),
        input_output_aliases={0: 2}, compiler_params=_EFFECT,
    )(_in_hbm(wp), after)


def _ag_wait(name, send_sems, recv_sems, wp, after, row_ranges):
    def body(w_ref, send_sems, recv_sems, *rest):
        x, y, c = _mesh_pos()
        for i, rows in enumerate(row_ranges):
            mine = _ag_piece(w_ref, 2 * x + y, c, rows)
            for j, (px, py) in enumerate(_other_chips(x, y)):
                landed = _ag_piece(w_ref, 2 * px + py, c, rows)
                cp = _remote(mine, landed, send_sems, recv_sems, 3 * i + j, (px, py, c))
                cp.wait_send()
                cp.wait_recv()

    return pl.pallas_call(
        body, name=name, out_shape=pltpu.HBM(wp.shape, wp.dtype),
        in_specs=(_HBM, _SEM, _SEM) + (_ANY,) * len(after), out_specs=_HBM,
        input_output_aliases={0: 0}, compiler_params=_EFFECT,
    )(wp, send_sems, recv_sems, *after)


def _ag_forward(wp, rows):
    def body(w_in, o, send_sems, recv_sems):
        x, y, c = _mesh_pos()
        sib = (x, y, 1 - c)
        chips = _other_chips(x, y)
        sends = []
        for j, (px, py) in enumerate(chips):
            landed = _ag_piece(o, 2 * px + py, c, rows)
            cp = _remote(landed, landed, send_sems, recv_sems, j, sib)
            cp.start()
            sends.append(cp)
        for j, (px, py) in enumerate(chips):
            passed = _ag_piece(o, 2 * px + py, 1 - c, rows)
            _remote(passed, passed, send_sems, recv_sems, j, sib).wait_recv()
        for cp in sends:
            cp.wait_send()

    return pl.pallas_call(
        body, name="ag_forward",
        in_specs=[_ANY], out_specs=_ANY,
        out_shape=jax.ShapeDtypeStruct(wp.shape, wp.dtype),
        scratch_shapes=[pltpu.SemaphoreType.DMA((3,)), pltpu.SemaphoreType.DMA((3,))],
        input_output_aliases={0: 0},
    )(wp)


def _ag_forward_start(name, wp, rows):
    def body(w_ref, send_sems, recv_sems, w_thru):
        x, y, c = _mesh_pos()
        for j, (px, py) in enumerate(_other_chips(x, y)):
            landed = _ag_piece(w_ref, 2 * px + py, c, rows)
            _remote(landed, landed, send_sems, recv_sems, j, (x, y, 1 - c)).start()

    return pl.pallas_call(
        body, name=name,
        out_shape=(pltpu.SemaphoreType.DMA((3,)), pltpu.SemaphoreType.DMA((3,)), pltpu.HBM(wp.shape, wp.dtype)),
        in_specs=(_HBM,), out_specs=(_SEM, _SEM, _HBM),
        input_output_aliases={0: 2}, compiler_params=_EFFECT,
    )(_in_hbm(wp))


def _ag_forward_wait(name, send_sems, recv_sems, wp, after, rows):
    def body(w_ref, send_sems, recv_sems, *rest):
        x, y, c = _mesh_pos()
        for j, (px, py) in enumerate(_other_chips(x, y)):
            cp = _remote(_ag_piece(w_ref, 2 * px + py, c, rows), _ag_piece(w_ref, 2 * px + py, 1 - c, rows),
                         send_sems, recv_sems, j, (x, y, 1 - c))
            cp.wait_send()
            cp.wait_recv()

    return pl.pallas_call(
        body, name=name, out_shape=pltpu.HBM(wp.shape, wp.dtype),
        in_specs=(_HBM, _SEM, _SEM) + (_ANY,) * len(after), out_specs=_HBM,
        input_output_aliases={0: 0}, compiler_params=_EFFECT,
    )(wp, send_sems, recv_sems, *after)


def _rs_chips_start(name, t):
    nl = t.shape[0]

    def body(t_ref, land_ref, send_sems, recv_sems, t_thru, land_thru, token):
        x, y, c = _mesh_pos()
        for j, (px, py) in enumerate(_other_chips(x, y)):
            _remote(t_ref.at[:, 2 * px + py], land_ref.at[j], send_sems, recv_sems, j, (px, py, c)).start()
        token[...] = jnp.zeros_like(token)

    land = lax.empty((3, nl, P_HALF, D_MODEL), BF16)
    return pl.pallas_call(
        body, name=name,
        out_shape=(pltpu.SemaphoreType.DMA((3,)), pltpu.SemaphoreType.DMA((3,)), pltpu.HBM(t.shape, t.dtype),
                   pltpu.HBM(land.shape, land.dtype), _TOKEN),
        in_specs=(_HBM, _HBM), out_specs=(_SEM, _SEM, _HBM, _HBM, pl.BlockSpec(memory_space=pltpu.VMEM)),
        input_output_aliases={0: 2, 1: 3}, compiler_params=_EFFECT,
    )(_in_hbm(t), _in_hbm(land))


def _rs_chips_wait(name, send_sems, recv_sems, t, land, after):
    def body(t_ref, land_ref, send_sems, recv_sems, *rest):
        x, y, c = _mesh_pos()
        for j, (px, py) in enumerate(_other_chips(x, y)):
            cp = _remote(t_ref.at[:, 2 * px + py], land_ref.at[j], send_sems, recv_sems, j, (px, py, c))
            cp.wait_send()
            cp.wait_recv()

    return pl.pallas_call(
        body, name=name, out_shape=(pltpu.HBM(t.shape, t.dtype), pltpu.HBM(land.shape, land.dtype)),
        in_specs=(_HBM, _HBM, _SEM, _SEM) + (_ANY,) * len(after), out_specs=(_HBM, _HBM),
        input_output_aliases={0: 0, 1: 1}, compiler_params=_EFFECT,
    )(t, land, send_sems, recv_sems, *after)[1]


def _rs_sibling_start(name, g):
    nl = g.shape[0]

    def body(g_ref, land_ref, send_sems, recv_sems, g_thru, land_thru, token):
        x, y, c = _mesh_pos()
        _remote(g_ref.at[:, :, pl.ds((1 - c) * P_HALF, P_HALF), :], land_ref, send_sems, recv_sems, 0,
                (x, y, 1 - c)).start()
        token[...] = jnp.zeros_like(token)

    land = lax.empty((nl, N_SHARD, P_HALF, D_MODEL), F32)
    return pl.pallas_call(
        body, name=name,
        out_shape=(pltpu.SemaphoreType.DMA((1,)), pltpu.SemaphoreType.DMA((1,)), pltpu.HBM(g.shape, g.dtype),
                   pltpu.HBM(land.shape, land.dtype), _TOKEN),
        in_specs=(_HBM, _HBM), out_specs=(_SEM, _SEM, _HBM, _HBM, pl.BlockSpec(memory_space=pltpu.VMEM)),
        input_output_aliases={0: 2, 1: 3}, compiler_params=_EFFECT,
    )(_in_hbm(g), _in_hbm(land))


def _rs_sibling_wait(name, send_sems, recv_sems, g, land, after):
    def body(g_ref, land_ref, send_sems, recv_sems, *rest):
        x, y, c = _mesh_pos()
        cp = _remote(g_ref.at[:, :, pl.ds((1 - c) * P_HALF, P_HALF), :], land_ref, send_sems, recv_sems, 0,
                     (x, y, 1 - c))
        cp.wait_send()
        cp.wait_recv()

    return pl.pallas_call(
        body, name=name, out_shape=(pltpu.HBM(g.shape, g.dtype), pltpu.HBM(land.shape, land.dtype)),
        in_specs=(_HBM, _HBM, _SEM, _SEM) + (_ANY,) * len(after), out_specs=(_HBM, _HBM),
        input_output_aliases={0: 0, 1: 1}, compiler_params=_EFFECT,
    )(g, land, send_sems, recv_sems, *after)


def _rs_add(name, ids, g, buf, row_tile):
    nl, _, hr, cols = buf.shape
    n_rt = hr // row_tile

    def body(ids_ref, g_ref, b_ref, own_ref, tb_ref):
        t = g_ref[...] + b_ref[...]
        tb_ref[...] = t.astype(BF16)

        @pl.when(pl.program_id(2) == ids_ref[1])
        def _():
            own_ref[...] = t

    blk = (None, None, row_tile, cols)
    grid_spec = pltpu.PrefetchScalarGridSpec(
        num_scalar_prefetch=1, grid=(nl, n_rt, N_SHARD),
        in_specs=[pl.BlockSpec(blk, lambda l, j, s, ids_ref: (l, s, ids_ref[0] * n_rt + j, 0)),
                  pl.BlockSpec(blk, lambda l, j, s, ids_ref: (l, s, j, 0))],
        out_specs=[pl.BlockSpec((None, row_tile, cols), lambda l, j, s, ids_ref: (l, j, 0)),
                   pl.BlockSpec(blk, lambda l, j, s, ids_ref: (l, s, j, 0))])
    return pl.pallas_call(
        body, name=name, grid_spec=grid_spec,
        out_shape=[jax.ShapeDtypeStruct((nl, hr, cols), F32), jax.ShapeDtypeStruct(buf.shape, BF16)],
        compiler_params=_cparams(3),
    )(ids, g, buf)


def _rs_sum(ids, layer, own, bufb, reduced, row_tile):
    _, hr, cols = own.shape
    n_rt = hr // row_tile

    def body(ids_ref, own_ref, b_ref, reduced_in, f_ref):
        f_ref[...] = ((own_ref[...] + b_ref[0].astype(F32)) + b_ref[1].astype(F32)) + b_ref[2].astype(F32)

    grid_spec = pltpu.PrefetchScalarGridSpec(
        num_scalar_prefetch=1, grid=(n_rt,),
        in_specs=[pl.BlockSpec((None, row_tile, cols), lambda j, ids_ref: (0, j, 0)),
                  pl.BlockSpec((3, None, row_tile, cols), lambda j, ids_ref: (0, 0, j, 0)),
                  pl.BlockSpec(memory_space=pl.ANY)],
        out_specs=pl.BlockSpec((None, row_tile, cols), lambda j, ids_ref: (layer, ids_ref[0] * n_rt + j, 0)))
    return pl.pallas_call(
        body, name="rs_sum", grid_spec=grid_spec,
        out_shape=jax.ShapeDtypeStruct(reduced.shape, F32),
        input_output_aliases={3: 0},
        compiler_params=_cparams(1),
    )(ids, own, bufb, reduced)


def _rs_exchange_start(name, f):
    def body(f_ref, send_sems, recv_sems, f_thru):
        x, y, c = _mesh_pos()
        mine = f_ref.at[:, pl.ds(c * P_HALF, P_HALF), :]
        _remote(mine, mine, send_sems, recv_sems, 0, (x, y, 1 - c)).start()

    return pl.pallas_call(
        body, name=name,
        out_shape=(pltpu.SemaphoreType.DMA((1,)), pltpu.SemaphoreType.DMA((1,)), pltpu.HBM(f.shape, f.dtype)),
        in_specs=(_HBM,), out_specs=(_SEM, _SEM, _HBM),
        input_output_aliases={0: 2}, compiler_params=_EFFECT,
    )(_in_hbm(f))


def _rs_exchange_wait(name, send_sems, recv_sems, f, after):
    def body(f_ref, send_sems, recv_sems, *rest):
        x, y, c = _mesh_pos()
        mine = f_ref.at[:, pl.ds(c * P_HALF, P_HALF), :]
        theirs = f_ref.at[:, pl.ds((1 - c) * P_HALF, P_HALF), :]
        cp = _remote(mine, theirs, send_sems, recv_sems, 0, (x, y, 1 - c))
        cp.wait_send()
        cp.wait_recv()

    return pl.pallas_call(
        body, name=name, out_shape=pltpu.HBM(f.shape, f.dtype),
        in_specs=(_HBM, _SEM, _SEM) + (_ANY,) * len(after), out_specs=_HBM,
        input_output_aliases={0: 0}, compiler_params=_EFFECT,
    )(f, send_sems, recv_sems, *after)


def _small_all_reduce(s, after=()):
    n_rows = s.shape[0]
    hr = n_rows // 2
    qr = hr // N_SHARD

    def body(s_ref, *rest):
        o_ref, sibbuf, tbuf, qbuf, fbuf, send_sems, recv_sems = rest[len(after):]
        x, y, c = _mesh_pos()
        k = 2 * x + y
        sib = (x, y, 1 - c)
        chips = _other_chips(x, y)
        mine = pl.ds(pl.multiple_of(c * hr, SUBLANES), hr)
        theirs = pl.ds(pl.multiple_of((1 - c) * hr, SUBLANES), hr)

        def quarter(shard):
            return pl.ds(pl.multiple_of(shard * qr, SUBLANES), qr)

        first = _remote(s_ref.at[theirs], sibbuf, send_sems, recv_sems, 0, sib)
        first.start()
        first.wait()
        tbuf[...] = s_ref[mine, :] + sibbuf[...]
        cps = []
        for j, (px, py) in enumerate(chips):
            cp = _remote(tbuf.at[quarter(2 * px + py)], qbuf.at[j], send_sems, recv_sems, 1 + j, (px, py, c))
            cp.start()
            cps.append(cp)
        for cp in cps:
            cp.wait()
        fbuf[quarter(k), :] = (tbuf[quarter(k), :] + qbuf[1]) + (qbuf[0] + qbuf[2])
        cps = []
        for j, (px, py) in enumerate(chips):
            cp = _remote(fbuf.at[quarter(k)], fbuf.at[quarter(k)], send_sems, recv_sems, 4 + j, (px, py, c))
            cp.start()
            cps.append(cp)
        for j, (px, py) in enumerate(chips):
            got = fbuf.at[quarter(2 * px + py)]
            _remote(got, got, send_sems, recv_sems, 4 + j, (px, py, c)).wait_recv()
        for cp in cps:
            cp.wait_send()
        o_ref[mine, :] = fbuf[...]
        last = _remote(fbuf, o_ref.at[mine], send_sems, recv_sems, 7, sib)
        last.start()
        last.wait()

    vmem = pl.BlockSpec(memory_space=pltpu.VMEM)
    return pl.pallas_call(
        body, name="small_all_reduce",
        in_specs=[vmem] + [_ANY] * len(after), out_specs=vmem,
        out_shape=jax.ShapeDtypeStruct(s.shape, F32),
        scratch_shapes=[pltpu.VMEM((hr, D_MODEL), F32), pltpu.VMEM((hr, D_MODEL), F32),
                        pltpu.VMEM((3, qr, D_MODEL), F32), pltpu.VMEM((hr, D_MODEL), F32),
                        pltpu.SemaphoreType.DMA((8,)), pltpu.SemaphoreType.DMA((8,))],
        compiler_params=pltpu.CompilerParams(vmem_limit_bytes=VMEM_LIMIT),
    )(s, *after)


_SMALL = ("norm_mix", "w_pool", "pool_scale", "lam_re", "lam_im", "log_dt", "b_re", "b_im", "c_re", "c_im",
          "d_skip", "b_glu", "norm_ffn", "norm_final")
_WEIGHTS = ("norm_mix", "w_in", "w_pool", "pool_scale", "lam_re", "lam_im", "log_dt", "b_re", "b_im", "c_re",
            "c_im", "d_skip", "w_glu", "b_glu", "w_out", "norm_ffn", "w_gate", "w_up", "w_down", "norm_final")


def _local_step(x, target, p, get_weights, get_ffn_weights, ffn_bwd_done, put_grads):
    nl = p["norm_mix"].shape[0]

    def tied(a, token):
        return a if token is None else a + token
    n_rows = nl * N_SSM_GROUPS
    lr = p["lam_re"].reshape(n_rows, 1, SSM_STATE)
    li = p["lam_im"].reshape(n_rows, 1, SSM_STATE)
    ldt = p["log_dt"].reshape(n_rows, 1, 1)
    br_t = p["b_re"].reshape(n_rows, SSM_STATE, SSM_GROUP).transpose(0, 2, 1)
    bi_t = p["b_im"].reshape(n_rows, SSM_STATE, SSM_GROUP).transpose(0, 2, 1)
    ar, ai, bbr_t, bbi_t = _disc_fwd(lr, li, ldt, br_t, bi_t)
    ar = ar.reshape(nl, 1, N_STATE)
    ai = ai.reshape(nl, 1, N_STATE)
    bbr = bbr_t.transpose(0, 2, 1).reshape(nl, N_SSM_GROUPS, SSM_STATE, SSM_GROUP)
    bbi = bbi_t.transpose(0, 2, 1).reshape(nl, N_SSM_GROUPS, SSM_STATE, SSM_GROUP)
    w_pool = p["w_pool"].astype(BF16)
    p = dict(p)
    for n in ("norm_mix", "pool_scale", "b_glu", "norm_ffn"):
        p[n] = p[n].reshape(nl, 1, -1)
    swap = lambda a: jnp.swapaxes(a, -1, -2)
    bpad = jax.vmap(_pad_pairs)(bbr, bbi).astype(BF16)
    cpad_t = jax.vmap(_pad_pairs)(swap(p["c_re"]), -swap(p["c_im"])).astype(BF16)
    bpad_t, cpad = swap(bpad), swap(cpad_t)
    dskip = p["d_skip"].reshape(nl, 1, D_SSM)

    layers = []
    h = x
    for l in range(nl):
        wp = get_weights(l, [h] if l else [h, bpad, cpad, bpad_t, cpad_t, ar, ai])
        u, ypool = _mix_in_fwd(h, p["norm_mix"], wp, l, w_pool, p["pool_scale"])
        sre, sim, yraw = _ssm_fwd(u, l, bpad, cpad, ar, ai, dskip)
        hm = _mix_out_fwd(yraw, ypool, h, wp, l, p["b_glu"])
        wp = get_ffn_weights(l, wp, [hm])
        h_next, n2, act_s, fgate_s, fup_s = _ffn_fwd(hm, p["norm_ffn"], wp, l)
        layers.append(dict(h=h, u=u, ypool=ypool, sre=sre, sim=sim, yraw=yraw, hm=hm, n2=n2, act_s=act_s, wp=wp,
                           fgate_s=fgate_s, fup_s=fup_s))
        h = h_next

    dh, loss, d_norm_final = _final_fwd_bwd(h, p["norm_final"].reshape(1, D_MODEL), target)

    raw = {n: [None] * nl for n in ("dg1", "dwp", "dsc", "dcp", "dbp", "ddsk", "db_glu", "dg2", "dar", "dai")}
    token = None
    for l in reversed(range(nl)):
        s = layers[l]
        wp = s["wp"]
        g1 = lax.empty((1, N_SHARD, P_ROWS, D_MODEL), F32)
        dhm, dg2, dgate_s, dup_s, dhb = _ffn_bwd_act(dh, s["hm"], tied(p["norm_ffn"], token), s["fgate_s"],
                                                      s["fup_s"], wp, l)
        g1 = _ffn_bwd_w(s["n2"], dgate_s, dup_s, s["act_s"], dhb, g1)
        token = ffn_bwd_done(l, [g1])
        dyraw, dyp, db_glu, g1 = _mix_out_bwd(dhm, s["yraw"], s["ypool"], wp, l, tied(p["b_glu"], token), g1)
        dus, dcp, dbp, dar, dai, ddsk = _ssm_bwd(dyraw, s["u"], s["sre"], s["sim"], l, cpad_t, bpad_t, ar, ai, dskip)
        dup, dwp, dsc = _pool_bwd(dyp, s["u"], l, w_pool, p["pool_scale"])
        dh, dg1, g1 = _mix_in_bwd(dup, dus, s["h"], dhm, p["norm_mix"], wp, l, g1)
        token = put_grads(l, g1)
        for n, a in (("dg1", dg1), ("dwp", dwp), ("dsc", dsc), ("dcp", dcp), ("dbp", dbp), ("ddsk", ddsk),
                     ("db_glu", db_glu), ("dg2", dg2), ("dar", dar), ("dai", dai)):
            raw[n][l] = a

    st = {n: jnp.stack(v) for n, v in raw.items()}
    dc_re, dc_im = jax.vmap(_unpad_pairs)(swap(st["dcp"]))
    dbbr, dbbi = jax.vmap(_unpad_pairs)(st["dbp"])
    rows = lambda a: a.reshape((n_rows,) + a.shape[2:])
    dlr, dli, dldt, dbr_t, dbi_t = _disc_bwd(lr, li, ldt, br_t, bi_t, st["dar"].reshape(n_rows, 1, SSM_STATE),
                                              st["dai"].reshape(n_rows, 1, SSM_STATE), rows(swap(dbbr)),
                                              rows(swap(dbbi)))
    small = {"norm_mix": st["dg1"][:, 0], "w_pool": st["dwp"], "pool_scale": st["dsc"][:, 0], "c_re": swap(dc_re),
             "c_im": -swap(dc_im), "d_skip": st["ddsk"].reshape(nl, N_SSM_GROUPS, SSM_GROUP),
             "b_glu": st["db_glu"][:, 0], "norm_ffn": st["dg2"][:, 0]}
    small["lam_re"] = dlr.reshape(nl, N_SSM_GROUPS, SSM_STATE)
    small["lam_im"] = dli.reshape(nl, N_SSM_GROUPS, SSM_STATE)
    small["log_dt"] = dldt.reshape(nl, N_SSM_GROUPS)
    small["b_re"] = dbr_t.reshape(nl, N_SSM_GROUPS, SSM_GROUP, SSM_STATE)
    small["b_im"] = dbi_t.reshape(nl, N_SSM_GROUPS, SSM_GROUP, SSM_STATE)
    small["d_skip"] = small["d_skip"].transpose(_SMALL_VIEW["d_skip"])
    small["norm_final"] = d_norm_final
    return loss, dh, small


_SMALL_VIEW = {"b_re": (0, 1, 3, 2), "b_im": (0, 1, 3, 2), "d_skip": (0, 2, 1)}
_SMALL_GROUPS = (("b_re", "b_im"), ("c_re", "c_im"), ("lam_re", "lam_im"), ("norm_mix", "norm_ffn"),
                 ("pool_scale", "b_glu"), ("w_pool",), ("log_dt",), ("d_skip",), ("norm_final",))


def _view(n, a):
    a = a.transpose(_SMALL_VIEW[n]) if n in _SMALL_VIEW else a
    return a[None] if a.ndim == 1 else a


def _unview(n, a, shape):
    a = a.reshape(shape) if len(shape) == 1 else a
    return a.transpose(_SMALL_VIEW[n]) if n in _SMALL_VIEW else a


def _flatten_small(views):
    flat = jnp.concatenate([views[n].reshape(-1) for n in _SMALL])
    n_rows = -(-flat.shape[0] // (64 * D_MODEL)) * 64
    return jnp.pad(flat, (0, n_rows * D_MODEL - flat.shape[0])).reshape(n_rows, D_MODEL)


def _split_small(flat, like):
    flat = flat.reshape(-1)
    out, at = {}, 0
    for n in _SMALL:
        size = like[n].size
        out[n] = flat[at:at + size].reshape(like[n].shape)
        at += size
    return out


def _adamw_small(name, ws, ms, vs, gs):
    k = len(ws)

    def body(*refs):
        ins, outs = refs[:4 * k], refs[4 * k:]
        for i in range(k):
            w, m, v, g = (ins[j * k + i][...] for j in range(4))
            delta, mn, vn = _adamw_math(w, g, m, v)
            outs[i][...] = delta
            outs[k + i][...] = mn
            outs[2 * k + i][...] = vn

    shapes = [jax.ShapeDtypeStruct(w.shape, F32) for w in ws] * 3
    outs = pl.pallas_call(body, name=name, out_shape=shapes,
                          compiler_params=pltpu.CompilerParams(vmem_limit_bytes=VMEM_LIMIT))(*ws, *ms, *vs, *gs)
    return outs[:k], outs[k:2 * k], outs[2 * k:]


def kernel(x, norm_mix, w_in, w_pool, pool_scale, lam_re, lam_im, log_dt, b_re, b_im, c_re, c_im, d_skip, w_glu, b_glu, w_out, norm_ffn, w_gate, w_up, w_down, norm_final, loss_target, m_norm_mix, m_w_in, m_w_pool, m_pool_scale, m_lam_re, m_lam_im, m_log_dt, m_b_re, m_b_im, m_c_re, m_c_im, m_d_skip, m_w_glu, m_b_glu, m_w_out, m_norm_ffn, m_w_gate, m_w_up, m_w_down, m_norm_final, v_norm_mix, v_w_in, v_w_pool, v_pool_scale, v_lam_re, v_lam_im, v_log_dt, v_b_re, v_b_im, v_c_re, v_c_im, v_d_skip, v_w_glu, v_b_glu, v_w_out, v_norm_ffn, v_w_gate, v_w_up, v_w_down, v_norm_final):
    given = dict(locals())
    w = {n: given[n] for n in _WEIGHTS}
    m = {n: given["m_" + n] for n in _WEIGHTS}
    v = {n: given["v_" + n] for n in _WEIGHTS}
    ids = jnp.stack([lax.axis_index("c"), 2 * lax.axis_index("x") + lax.axis_index("y")]).astype(jnp.int32)

    t_names = ("w_gate", "w_up")
    tr = lambda a: a.transpose(0, 2, 1)
    for d in (w, m, v):
        d.update({n: tr(d[n]) for n in t_names})

    nl = norm_mix.shape[0]
    mixer_rows, ffn_rows = (P_FF_ROWS, P_ROWS - P_FF_ROWS), (0, P_FF_ROWS)
    started, last = {}, None
    for l in range(nl):
        packed = _pack_weights(ids, l, w["w_in"], w["w_glu"], w["w_out"], w["w_down"], w["w_gate"], w["w_up"],
                               [] if last is None else [last])
        if l == 0:
            first = _ag_start("ag_start_0_mixer", packed, ids, [mixer_rows])
            started[0] = _ag_start("ag_start_0_ffn", first[2], first[3], [ffn_rows])
        else:
            started[l] = _ag_start(f"ag_start_{l}", packed, last, [mixer_rows, ffn_rows])
        last = started[l][3]
    views = [{n: _view(n, d[n]) for n in _SMALL} for d in (w, m, v)]

    passing = {}

    def get_weights(l, after):
        send_sems, recv_sems, buf, _ = started[l]
        if l == 0:
            buf = _ag_wait("ag_wait_0_mixer", first[0], first[1], buf, after + [last], [mixer_rows])
            return _ag_forward(buf, mixer_rows)
        buf = _ag_wait(f"ag_wait_{l}", send_sems, recv_sems, buf, after, [mixer_rows, ffn_rows])
        buf = _ag_forward(buf, mixer_rows)
        passing[l] = _ag_forward_start(f"ag_forward_start_{l}", buf, ffn_rows)
        return passing[l][2]

    def get_ffn_weights(l, buf, after):
        if l > 0:
            send_sems, recv_sems, _ = passing[l]
            return _ag_forward_wait(f"ag_forward_wait_{l}", send_sems, recv_sems, buf, after, ffn_rows)
        send_sems, recv_sems, _, _ = started[0]
        return _ag_forward(_ag_wait("ag_wait_0_ffn", send_sems, recv_sems, buf, after, [ffn_rows]), ffn_rows)

    to_sibling, to_chips, reduced = {}, {}, {}

    def put_grads(l, g):
        to_sibling[l] = _rs_sibling_start(f"rs_sibling_start_{l}", g)
        token = to_sibling[l][4]
        if l + 1 in to_chips:
            finish(l + 1, [token])
        return token[:1, :1]

    def ffn_bwd_done(l, after):
        return send_to_chips(l + 1, after)[:1, :1] if l + 1 in to_sibling else None

    def send_to_chips(l, after):
        send_sems, recv_sems, g, land, _ = to_sibling.pop(l)
        g, land = _rs_sibling_wait(f"rs_sibling_wait_{l}", send_sems, recv_sems, g, land, after)
        own, t = _rs_add("rs_add", ids, g, land, RS_ROW_TILE)
        send_sems, recv_sems, t, land, token = _rs_chips_start(f"rs_chips_start_{l}", t)
        to_chips[l] = (send_sems, recv_sems, t, land, own)
        return token

    def finish(l, after):
        send_sems, recv_sems, t, land, own = to_chips.pop(l)
        land = _rs_chips_wait(f"rs_chips_wait_{l}", send_sems, recv_sems, t, land, after)
        shard = lax.empty((1, P_ROWS, D_MODEL), F32)
        reduced[l] = _rs_exchange_start(f"rs_exchange_start_{l}", _rs_sum(ids, 0, own, land, shard, RS_ROW_TILE))

    loss, grad_x, small = _local_step(x[0], loss_target[0], {n: w[n] for n in _SMALL}, get_weights, get_ffn_weights,
                                      ffn_bwd_done, put_grads)
    loss = lax.psum(loss[0, 0], ("x", "y", "c"))
    small_flat = _flatten_small(small)

    groups = ((("w_in", P_IN_BLK), ("w_out", P_OUT_BLK)), (("w_down", P_WD_BLK), ("w_gate", P_WG_BLK), ("w_up", P_WU_BLK)))
    res = {n: None for n in ("w_in", "w_out", "w_down", "w_gate", "w_up", "w_glu")}

    def adamw_layer(l, after):
        send_sems, recv_sems, shard = reduced[l]
        shard = _rs_exchange_wait(f"rs_exchange_wait_{l}", send_sems, recv_sems, shard, after)
        for group, row_tile in zip(groups, (128, 176)):
            names = [n for n, _ in group]
            outs = None if res[names[0]] is None else [res[n] for n in names]
            outs = _adamw_group("adamw_" + names[0], l, *[[d[n] for n in names] for d in (w, m, v)], shard,
                                [blk * idx for _, (blk, idx) in group], row_tile, outs)
            res.update(zip(names, outs))
        blk, idx = P_GLU_BLK
        res["w_glu"] = _adamw("adamw_w_glu", l, w["w_glu"], m["w_glu"], v["w_glu"], shard, (blk, D_MODEL), blk * idx,
                              128, res["w_glu"], (), True)

    if nl > 1:
        adamw_layer(nl - 1, [to_sibling[0][4]])
    token = send_to_chips(0, [small_flat] + [r[0] for r in res.values() if r is not None])
    for l in reversed(range(1, nl - 1)):
        adamw_layer(l, [token])
    updated = [r[0] for r in res.values() if r is not None]
    small_sum = _small_all_reduce(small_flat, [token] + updated)
    finish(0, [small_sum] + updated)
    adamw_layer(0, [])
    for n in t_names:
        res[n] = tuple(tr(a) for a in res[n])
    g_views = _split_small(small_sum, views[0])
    for group in _SMALL_GROUPS:
        deltas, new_ms, new_vs = _adamw_small("adamw_" + group[0], *[[d[n] for n in group] for d in views],
                                              [g_views[n] for n in group])
        for i, n in enumerate(group):
            res[n] = tuple(_unview(n, a, w[n].shape) for a in (g_views[n], deltas[i], new_ms[i], new_vs[i]))

    return (loss, grad_x[None], *[res[n][0] for n in _WEIGHTS], *[res[n][1] for n in _WEIGHTS],
            *[res[n][2] for n in _WEIGHTS], *[res[n][3] for n in _WEIGHTS])
```

```python
import functools
import math

import jax
import jax.numpy as jnp
from jax import lax
from jax.experimental import pallas as pl
from jax.experimental.pallas import tpu as pltpu

F32 = jnp.float32
BF16 = jnp.bfloat16

D_MODEL = 1024
D_POOL = 512
D_SSM = 512
POOL_WINDOWS = (2, 4, 8, 16)
POOL_GROUP = 128
POOL_HALO = 16
N_SSM_GROUPS = 32
SSM_GROUP = 16
SSM_STATE = 64
N_STATE = N_SSM_GROUPS * SSM_STATE
N_PAIRS = N_SSM_GROUPS // 2
D_FF = 2816
N_SHARD = 4
FF_SHARD = D_FF // N_SHARD
RMS_EPS = 1e-6

ADAM_LR = 0.001
ADAM_B1 = 0.9
ADAM_B2 = 0.999
ADAM_EPS = 1e-08
ADAM_WD = 0.01
ADAM_STEP = 10

P_ROWS = 2816
P_WD_BLK = (704, 0)
P_WG_BLK = (704, 1)
P_WU_BLK = (704, 2)
P_FF_ROWS = 2112
P_GLU_BLK = (64, 33)
P_GLU_PAD = 192
P_IN_BLK = (256, 9)
P_OUT_BLK = (256, 10)

SUBLANES = 8
VMEM_LIMIT = 56 * 1024 * 1024

TM = 1024
TM_FFN = 512
TM_FFN_LONG = 1024
FFN_SPLIT = 2
TS = 2048
SCAN_LANES = 512


def _cparams(n_axes):
    return pltpu.CompilerParams(dimension_semantics=("arbitrary",) * n_axes, vmem_limit_bytes=VMEM_LIMIT)


def _dot(a, b):
    return jnp.dot(a, b, preferred_element_type=F32)


def _dot_nt(a, b):
    return lax.dot_general(a, b, (((1,), (1,)), ((), ())), preferred_element_type=F32)


def _dot_tn(a, b):
    return lax.dot_general(a, b, (((0,), (0,)), ((), ())), preferred_element_type=F32)


def _rms_hat(x):
    r = lax.rsqrt(jnp.mean(x * x, axis=-1, keepdims=True) + RMS_EPS)
    return x * r, r


def _rms_bwd(d_hat, xhat, r):
    return r * (d_hat - xhat * jnp.mean(d_hat * xhat, axis=-1, keepdims=True))


def _sigmoid(x):
    return 1.0 / (1.0 + jnp.exp(-x))


_GELU_C = math.sqrt(2.0 / math.pi)
_GELU_K = 0.044715


def _gelu(x):
    return 0.5 * x * (1.0 + jnp.tanh(_GELU_C * (x + _GELU_K * x * x * x)))


def _gelu_grad(x):
    th = jnp.tanh(_GELU_C * (x + _GELU_K * x * x * x))
    return 0.5 * (1.0 + th) + 0.5 * x * (1.0 - th * th) * _GELU_C * (1.0 + 3.0 * _GELU_K * x * x)


def _glu_weight(ref):
    v = ref[...]
    return jnp.concatenate([v[:, :, :D_SSM], v[:, :, D_SSM:]], axis=1).reshape(D_SSM, D_SSM)


def _glu_pack(w):
    v = w.reshape(N_SHARD, 128, D_SSM)
    return jnp.concatenate([v[:, :64, :], v[:, 64:, :]], axis=2)


def _pool_diff(ext, row0, tm):
    rows = row0 + lax.broadcasted_iota(jnp.int32, (tm, 1), 0)
    outs = []
    for gi, w in enumerate(POOL_WINDOWS):
        e = ext[:, gi * POOL_GROUP:(gi + 1) * POOL_GROUP]
        s = e
        k = 1
        while k < w:
            s = s + pltpu.roll(s, k, 0)
            k *= 2
        inv = 1.0 / jnp.minimum(rows + 1, w).astype(F32)
        outs.append(s[POOL_HALO:, :] * inv - e[POOL_HALO:, :])
    return outs


def _mix_in_fwd(h, g1, wp, layer, w_pool, scale):
    L = h.shape[0]
    tm = min(TM, L)

    def body(h_ref, g_ref, w_ref, wp_ref, sc_ref, u_ref, yp_ref, carry):
        i = pl.program_id(0)

        @pl.when(i == 0)
        def _():
            carry[...] = jnp.zeros_like(carry)

        xhat, _ = _rms_hat(h_ref[...])
        n1 = (xhat * g_ref[...]).astype(BF16)
        u = _dot(n1, w_ref[...].reshape(D_MODEL, D_MODEL))
        u_ref[...] = u
        up = u[:, :D_POOL]
        ext = jnp.concatenate([carry[...], up], axis=0)
        carry[...] = up[tm - POOL_HALO:, :]
        diffs = _pool_diff(ext, i * tm, tm)
        for gi in range(4):
            cols = slice(gi * POOL_GROUP, (gi + 1) * POOL_GROUP)
            yp_ref[:, cols] = _dot(diffs[gi].astype(BF16), wp_ref[gi]) * sc_ref[:, cols]

    blk, idx = P_IN_BLK
    return pl.pallas_call(
        body, name="mix_in_fwd", grid=(L // tm,),
        in_specs=[pl.BlockSpec((tm, D_MODEL), lambda i: (i, 0)),
                  pl.BlockSpec((None, 1, D_MODEL), lambda i: (layer, 0, 0)),
                  pl.BlockSpec((N_SHARD, None, blk, D_MODEL), lambda i: (0, 0, idx, 0)),
                  pl.BlockSpec((None, 4, POOL_GROUP, POOL_GROUP), lambda i: (layer, 0, 0, 0)),
                  pl.BlockSpec((None, 1, D_POOL), lambda i: (layer, 0, 0))],
        out_specs=[pl.BlockSpec((tm, D_MODEL), lambda i: (i, 0)),
                   pl.BlockSpec((tm, D_POOL), lambda i: (i, 0))],
        out_shape=[jax.ShapeDtypeStruct((L, D_MODEL), F32), jax.ShapeDtypeStruct((L, D_POOL), F32)],
        scratch_shapes=[pltpu.VMEM((POOL_HALO, D_POOL), F32)],
        compiler_params=_cparams(1),
    )(h, g1, wp, w_pool, scale)


def _cmul(xr, xi, yr, yi):
    return xr * yr - xi * yi, xr * yi + xi * yr


SCAN_BLOCK = 64
N_SCAN_TABLES = 26


def _permute_rows(src, dst, n_rows):
    for b in range(n_rows // SCAN_BLOCK):
        for tau in range(SUBLANES):
            dst[pl.ds(SCAN_BLOCK * b + SUBLANES * tau, SUBLANES), :] = (
                src[pl.ds(SCAN_BLOCK * b + tau, SUBLANES, stride=SUBLANES), :])


def _scan_tables(ar, ai, tab, reverse):
    c = ar.shape[1]
    row = lax.broadcasted_iota(jnp.int32, (SUBLANES, c), 0)
    zero = jnp.zeros((SUBLANES, c), F32)
    full = lambda v: jnp.broadcast_to(v, (SUBLANES, c))
    pw = [(ar, ai)]
    for _ in range(SUBLANES - 1):
        pw.append(_cmul(*pw[-1], ar, ai))
    a8 = pw[-1]
    a16 = _cmul(*a8, *a8)
    a32 = _cmul(*a16, *a16)
    tab[0] = full(ar)
    tab[1] = full(ai)
    for n, (s, (pr, pi)) in enumerate(((1, a8), (2, a16), (4, a32))):
        keep = (row < SUBLANES - s) if reverse else (row >= s)
        tab[2 + 2 * n] = jnp.where(keep, pr, zero)
        tab[3 + 2 * n] = jnp.where(keep, pi, zero)
    cur = a8
    qr, qi = zero, zero
    for n in range(SUBLANES):
        at = (SUBLANES - 1 - n) if reverse else n
        qr = jnp.where(row == at, cur[0], qr)
        qi = jnp.where(row == at, cur[1], qi)
        cur = _cmul(*cur, *a8)
    tab[8] = qr
    tab[9] = qi
    for tau in range(SUBLANES):
        pr, pi = pw[SUBLANES - 1 - tau] if reverse else pw[tau]
        tab[10 + 2 * tau] = full(pr)
        tab[11 + 2 * tau] = full(pi)


def _cmac(xr, xi, ar, ai, yr, yi):
    return xr + ar * yr - ai * yi, xi + ar * yi + ai * yr


def _chain_segments(er, ei, c_r, c_i, tab, cols, reverse):
    tr, ti = er, ei
    for n, s in enumerate((1, 2, 4)):
        shift = SUBLANES - s if reverse else s
        tr, ti = _cmac(tr, ti, tab[2 + 2 * n, :, cols], tab[3 + 2 * n, :, cols],
                       pltpu.roll(tr, shift, 0), pltpu.roll(ti, shift, 0))
    return _cmac(tr, ti, tab[8, :, cols], tab[9, :, cols], c_r, c_i)


def _ssm_fwd(u, layer, bpad, cpad, ar, ai, dskip):
    L = u.shape[0]
    ts = min(TS, L)
    nq = 4
    cq = N_STATE // nq

    def body(u_ref, bp_ref, cp_ref, ar_ref, ai_ref, dsk_ref, sre_ref, sim_ref, y_ref, cr, ci, tab, up, yp):
        t = pl.program_id(1)

        @pl.when(t == 0)
        def _():
            cr[...] = jnp.zeros_like(cr)
            ci[...] = jnp.zeros_like(ci)
            _scan_tables(ar_ref[...], ai_ref[...], tab, reverse=False)

        _permute_rows(u_ref, up, ts)
        uf = up[...]
        ub = uf.astype(BF16)
        for jj in range(4):
            bu = _dot(ub, bp_ref[jj])
            sre_ref[:, jj * 128:(jj + 1) * 128] = bu[:, :128]
            sim_ref[:, jj * 128:(jj + 1) * 128] = bu[:, 128:]

        shp = (SUBLANES, SCAN_LANES)
        first_row = lax.broadcasted_iota(jnp.int32, shp, 0) == 0
        for cc in range(cq // SCAN_LANES):
            cols = slice(cc * SCAN_LANES, (cc + 1) * SCAN_LANES)

            def block(b, carry, cols=cols):
                c_r, c_i = carry
                base = pl.multiple_of(b * SCAN_BLOCK, SCAN_BLOCK)
                rows = lambda tau: pl.ds(base + SUBLANES * tau, SUBLANES)
                a_r, a_i = tab[0, :, cols], tab[1, :, cols]
                ys = [(sre_ref[rows(0), cols], sim_ref[rows(0), cols])]
                for tau in range(1, SUBLANES):
                    ys.append(_cmac(sre_ref[rows(tau), cols], sim_ref[rows(tau), cols], a_r, a_i, *ys[-1]))
                tr, ti = _chain_segments(*ys[-1], c_r, c_i, tab, cols, reverse=False)
                in_r = jnp.where(first_row, c_r, pltpu.roll(tr, 1, 0))
                in_i = jnp.where(first_row, c_i, pltpu.roll(ti, 1, 0))
                for tau in range(SUBLANES):
                    sr, si = _cmac(*ys[tau], tab[10 + 2 * tau, :, cols], tab[11 + 2 * tau, :, cols], in_r, in_i)
                    sre_ref[rows(tau), cols] = sr
                    sim_ref[rows(tau), cols] = si
                return (jnp.broadcast_to(tr[SUBLANES - 1:, :], shp), jnp.broadcast_to(ti[SUBLANES - 1:, :], shp))

            c_r, c_i = lax.fori_loop(0, ts // SCAN_BLOCK, block, (cr[:, cols], ci[:, cols]), unroll=2)
            cr[:, cols] = c_r
            ci[:, cols] = c_i

        acc = dsk_ref[...] * uf
        for jj in range(4):
            cols = slice(jj * 128, (jj + 1) * 128)
            scat = jnp.concatenate([sre_ref[:, cols], sim_ref[:, cols]], axis=1).astype(BF16)
            acc = acc + _dot(scat, cp_ref[jj])
        yp[...] = acc
        _permute_rows(yp, y_ref, ts)

    return pl.pallas_call(
        body, name="ssm_fwd", grid=(nq, L // ts),
        in_specs=[pl.BlockSpec((ts, 128), lambda q, t: (t, 4 + q)),
                  pl.BlockSpec((None, 4, 128, 256), lambda q, t: (layer, q, 0, 0)),
                  pl.BlockSpec((None, 4, 256, 128), lambda q, t: (layer, q, 0, 0)),
                  pl.BlockSpec((None, 1, cq), lambda q, t: (layer, 0, q)),
                  pl.BlockSpec((None, 1, cq), lambda q, t: (layer, 0, q)),
                  pl.BlockSpec((None, 1, 128), lambda q, t: (layer, 0, q))],
        out_specs=[pl.BlockSpec((ts, cq), lambda q, t: (t, q)),
                   pl.BlockSpec((ts, cq), lambda q, t: (t, q)),
                   pl.BlockSpec((ts, 128), lambda q, t: (t, q))],
        out_shape=[jax.ShapeDtypeStruct((L, N_STATE), F32), jax.ShapeDtypeStruct((L, N_STATE), F32),
                   jax.ShapeDtypeStruct((L, D_SSM), F32)],
        scratch_shapes=[pltpu.VMEM((SUBLANES, cq), F32), pltpu.VMEM((SUBLANES, cq), F32),
                        pltpu.VMEM((N_SCAN_TABLES, SUBLANES, cq), F32),
                        pltpu.VMEM((ts, 128), F32), pltpu.VMEM((ts, 128), F32)],
        compiler_params=_cparams(2),
    )(u, bpad, cpad, ar, ai, dskip)


def _mix_out_fwd(yraw, ypool, h, wp, layer, b_glu):
    L = h.shape[0]
    tm = min(TM, L)

    def body(yr_ref, yp_ref, h_ref, wglu_ref, b_ref, wout_ref, o_ref):
        y = _gelu(yr_ref[...])
        z = _dot(y.astype(BF16), _glu_weight(wglu_ref)) + b_ref[...]
        o = y * _sigmoid(z)
        mix = jnp.concatenate([yp_ref[...], o], axis=1).astype(BF16)
        o_ref[...] = h_ref[...] + _dot(mix, wout_ref[...].reshape(D_MODEL, D_MODEL))

    gb, gi = P_GLU_BLK
    ob, oi = P_OUT_BLK
    return pl.pallas_call(
        body, name="mix_out_fwd", grid=(L // tm,),
        in_specs=[pl.BlockSpec((tm, D_SSM), lambda i: (i, 0)),
                  pl.BlockSpec((tm, D_POOL), lambda i: (i, 0)),
                  pl.BlockSpec((tm, D_MODEL), lambda i: (i, 0)),
                  pl.BlockSpec((N_SHARD, None, gb, D_MODEL), lambda i: (0, 0, gi, 0)),
                  pl.BlockSpec((None, 1, D_SSM), lambda i: (layer, 0, 0)),
                  pl.BlockSpec((N_SHARD, None, ob, D_MODEL), lambda i: (0, 0, oi, 0))],
        out_specs=pl.BlockSpec((tm, D_MODEL), lambda i: (i, 0)),
        out_shape=jax.ShapeDtypeStruct((L, D_MODEL), F32),
        compiler_params=_cparams(1),
    )(yraw, ypool, h, wp, b_glu, wp)


def _ffn_weights(ref, k):
    return ref[k, 0:FF_SHARD, :], ref[k, FF_SHARD:2 * FF_SHARD, :], ref[k, 2 * FF_SHARD:P_FF_ROWS, :]


def _ffn_weight_spec():
    return pl.BlockSpec((N_SHARD, None, P_FF_ROWS, D_MODEL), lambda m, k: (0, 0, 0, 0),
                        pipeline_mode=pl.Buffered(1))


def _ffn_fwd(h, g2, wp, layer):
    L = h.shape[0]
    tm = min(TM_FFN_LONG, L)

    def body(h_ref, g_ref, w_ref, o_ref, n2_ref, act_ref, dgate_ref, dup_ref):
        k = pl.program_id(1)

        @pl.when(k == 0)
        def _():
            x = h_ref[...]
            xhat, _ = _rms_hat(x)
            n2_ref[...] = (xhat * g_ref[...]).astype(BF16)
            o_ref[...] = x

        wd, wg_t, wu_t = _ffn_weights(w_ref, k)
        n2 = n2_ref[...]
        gate = _dot_nt(n2, wg_t)
        up = _dot_nt(n2, wu_t)
        sg = _sigmoid(gate)
        silu = gate * sg
        act = (silu * up).astype(BF16)
        act_ref[...] = act
        dgate_ref[...] = (up * (sg * (1.0 + gate * (1.0 - sg)))).astype(BF16)
        dup_ref[...] = silu.astype(BF16)
        o_ref[...] += _dot(act, wd)

    act_shape = jax.ShapeDtypeStruct((N_SHARD, L, FF_SHARD), BF16)
    return pl.pallas_call(
        body, name="ffn_fwd", grid=(L // tm, N_SHARD),
        in_specs=[pl.BlockSpec((tm, D_MODEL), lambda m, k: (m, 0)),
                  pl.BlockSpec((None, 1, D_MODEL), lambda m, k: (layer, 0, 0)),
                  _ffn_weight_spec()],
        out_specs=[pl.BlockSpec((tm, D_MODEL), lambda m, k: (m, 0)),
                   pl.BlockSpec((tm, D_MODEL), lambda m, k: (m, 0)),
                   pl.BlockSpec((None, tm, FF_SHARD), lambda m, k: (k, m, 0)),
                   pl.BlockSpec((None, tm, FF_SHARD), lambda m, k: (k, m, 0)),
                   pl.BlockSpec((None, tm, FF_SHARD), lambda m, k: (k, m, 0))],
        out_shape=[jax.ShapeDtypeStruct((L, D_MODEL), F32), jax.ShapeDtypeStruct((L, D_MODEL), BF16),
                   act_shape, act_shape, act_shape],
        compiler_params=_cparams(2),
    )(h, g2, wp)


def _final_fwd_bwd(h, gf, target):
    L = h.shape[0]
    tm = min(TM, L)

    def body(h_ref, g_ref, t_ref, dh_ref, loss_ref, dg_ref):
        i = pl.program_id(0)

        @pl.when(i == 0)
        def _():
            loss_ref[...] = jnp.zeros_like(loss_ref)
            dg_ref[...] = jnp.zeros_like(dg_ref)

        xhat, r = _rms_hat(h_ref[...])
        g = g_ref[...]
        e = xhat * g - t_ref[...]
        loss_ref[...] += 0.5 * jnp.sum(jnp.mean(e * e, axis=-1, keepdims=True), axis=0, keepdims=True)
        dy = e * (1.0 / D_MODEL)
        dg_ref[...] += jnp.sum(dy * xhat, axis=0, keepdims=True)
        dh_ref[...] = _rms_bwd(dy * g, xhat, r)

    return pl.pallas_call(
        body, name="final_fwd_bwd", grid=(L // tm,),
        in_specs=[pl.BlockSpec((tm, D_MODEL), lambda i: (i, 0)),
                  pl.BlockSpec((1, D_MODEL), lambda i: (0, 0)),
                  pl.BlockSpec((tm, D_MODEL), lambda i: (i, 0))],
        out_specs=[pl.BlockSpec((tm, D_MODEL), lambda i: (i, 0)),
                   pl.BlockSpec((1, 1), lambda i: (0, 0)),
                   pl.BlockSpec((1, D_MODEL), lambda i: (0, 0))],
        out_shape=[jax.ShapeDtypeStruct((L, D_MODEL), F32), jax.ShapeDtypeStruct((1, 1), F32),
                   jax.ShapeDtypeStruct((1, D_MODEL), F32)],
        compiler_params=_cparams(1),
    )(h, gf, target)


def _ffn_bwd_act(dh, h, g2, fgate_s, fup_s, wp, layer):
    L = h.shape[0]
    tm = min(TM_FFN, L)
    sub = tm // FFN_SPLIT

    def body(dh_ref, h_ref, g_ref, fgate_ref, fup_ref, w_ref,
             dhm_ref, dg_ref, dgate_ref, dup_ref, dhb_ref):
        m, k = pl.program_id(0), pl.program_id(1)
        dn2 = dhm_ref

        @pl.when(jnp.logical_and(m == 0, k == 0))
        def _():
            dg_ref[...] = jnp.zeros_like(dg_ref)

        @pl.when(k == 0)
        def _():
            dhb_ref[...] = dh_ref[...].astype(BF16)
            dn2[...] = jnp.zeros_like(dn2)

        wd, wg_t, wu_t = _ffn_weights(w_ref, k)
        for rows in (slice(r * sub, (r + 1) * sub) for r in range(tm // sub)):
            dact = _dot_nt(dhb_ref[rows, :], wd)
            dgate = (dact * fgate_ref[rows, :].astype(F32)).astype(BF16)
            dup = (dact * fup_ref[rows, :].astype(F32)).astype(BF16)
            dgate_ref[rows, :] = dgate
            dup_ref[rows, :] = dup
            dn2[rows, :] += _dot(dgate, wg_t) + _dot(dup, wu_t)

        @pl.when(k == N_SHARD - 1)
        def _():
            xhat, r = _rms_hat(h_ref[...])
            d = dn2[...]
            dg_ref[...] += jnp.sum(d * xhat, axis=0, keepdims=True)
            dhm_ref[...] = dh_ref[...] + _rms_bwd(d * g_ref[...], xhat, r)

    act_spec = pl.BlockSpec((None, tm, FF_SHARD), lambda m, k: (k, m, 0))
    act_shape = jax.ShapeDtypeStruct((N_SHARD, L, FF_SHARD), BF16)
    row_spec = pl.BlockSpec((tm, D_MODEL), lambda m, k: (m, 0))
    return pl.pallas_call(
        body, name="ffn_bwd_act", grid=(L // tm, N_SHARD),
        in_specs=[row_spec, row_spec,
                  pl.BlockSpec((None, 1, D_MODEL), lambda m, k: (layer, 0, 0)),
                  act_spec, act_spec,
                  _ffn_weight_spec()],
        out_specs=[row_spec,
                   pl.BlockSpec((1, D_MODEL), lambda m, k: (0, 0)),
                   act_spec, act_spec, row_spec],
        out_shape=[jax.ShapeDtypeStruct((L, D_MODEL), F32), jax.ShapeDtypeStruct((1, D_MODEL), F32),
                   act_shape, act_shape, jax.ShapeDtypeStruct((L, D_MODEL), BF16)],
        compiler_params=_cparams(2),
    )(dh, h, g2, fgate_s, fup_s, wp)


def _ffn_bwd_w(n2, dgate_s, dup_s, act_s, dhb, gbuf):
    L = n2.shape[0]
    tm = min(TM_FFN_LONG, L)

    def body(n2_ref, dgate_ref, dup_ref, act_ref, dhb_ref, g_in, g_ref):
        m = pl.program_id(1)

        @pl.when(m == 0)
        def _():
            g_ref[...] = jnp.zeros_like(g_ref)

        n2v = n2_ref[...]
        g_ref[0:FF_SHARD, :] += _dot_tn(act_ref[...], dhb_ref[...])
        g_ref[FF_SHARD:2 * FF_SHARD, :] += _dot_tn(dgate_ref[...], n2v)
        g_ref[2 * FF_SHARD:P_FF_ROWS, :] += _dot_tn(dup_ref[...], n2v)

    act_spec = pl.BlockSpec((None, tm, FF_SHARD), lambda k, m: (k, m, 0))
    row_spec = pl.BlockSpec((tm, D_MODEL), lambda k, m: (m, 0))
    return pl.pallas_call(
        body, name="ffn_bwd_w", grid=(N_SHARD, L // tm),
        in_specs=[row_spec, act_spec, act_spec, act_spec, row_spec, pl.BlockSpec(memory_space=pl.ANY)],
        out_specs=pl.BlockSpec((None, None, P_FF_ROWS, D_MODEL), lambda k, m: (0, k, 0, 0)),
        out_shape=jax.ShapeDtypeStruct(gbuf.shape, F32),
        input_output_aliases={5: 0},
        compiler_params=_cparams(2),
    )(n2, dgate_s, dup_s, act_s, dhb, gbuf)


def _mix_out_bwd(dhm, yraw, ypool, wp, layer, b_glu, gbuf):
    L = dhm.shape[0]
    tm = min(TM, L)

    def body(dhm_ref, yr_ref, yp_ref, wglu_ref, b_ref, wout_ref, g1_in,
             dyr_ref, dyp_ref, db_ref, g1_ref, dwout, dwglu, gpack):
        i = pl.program_id(0)

        @pl.when(i == 0)
        def _():
            db_ref[...] = jnp.zeros_like(db_ref)
            dwout[...] = jnp.zeros_like(dwout)
            dwglu[...] = jnp.zeros_like(dwglu)

        dhb = dhm_ref[...].astype(BF16)
        wglu = _glu_weight(wglu_ref)
        dmix = _dot_nt(dhb, wout_ref[...].reshape(D_MODEL, D_MODEL))
        dyp_ref[...] = dmix[:, :D_POOL]
        d_o = dmix[:, D_POOL:]
        yraw_v = yr_ref[...]
        y = _gelu(yraw_v)
        yb = y.astype(BF16)
        sig = _sigmoid(_dot(yb, wglu) + b_ref[...])
        mix = jnp.concatenate([yp_ref[...], y * sig], axis=1).astype(BF16)
        dwout[...] += _dot_tn(mix, dhb).reshape(N_SHARD, 256, D_MODEL)
        dz = d_o * y * sig * (1.0 - sig)
        dzb = dz.astype(BF16)
        db_ref[...] += jnp.sum(dz, axis=0, keepdims=True)
        dwglu[...] += _dot_tn(yb, dzb)
        dy = d_o * sig + _dot_nt(dzb, wglu)
        dyr_ref[...] = dy * _gelu_grad(yraw_v)

        @pl.when(i == n_steps - 1)
        def _():
            gpack[:, :gb, :] = _glu_pack(dwglu[...])
            gpack[:, gb:, :] = jnp.zeros((N_SHARD, P_GLU_PAD - gb, D_MODEL), F32)
            pltpu.sync_copy(gpack, g1_ref.at[0, :, pl.ds(gb * gi, P_GLU_PAD), :])
            pltpu.sync_copy(dwout, g1_ref.at[0, :, pl.ds(ob * oi, ob), :])

    gb, gi = P_GLU_BLK
    ob, oi = P_OUT_BLK
    n_steps = L // tm
    return pl.pallas_call(
        body, name="mix_out_bwd", grid=(n_steps,),
        in_specs=[pl.BlockSpec((tm, D_MODEL), lambda i: (i, 0)),
                  pl.BlockSpec((tm, D_SSM), lambda i: (i, 0)),
                  pl.BlockSpec((tm, D_POOL), lambda i: (i, 0)),
                  pl.BlockSpec((N_SHARD, None, gb, D_MODEL), lambda i: (0, 0, gi, 0)),
                  pl.BlockSpec((None, 1, D_SSM), lambda i: (layer, 0, 0)),
                  pl.BlockSpec((N_SHARD, None, ob, D_MODEL), lambda i: (0, 0, oi, 0)),
                  pl.BlockSpec(memory_space=pl.ANY)],
        out_specs=[pl.BlockSpec((tm, D_SSM), lambda i: (i, 0)),
                   pl.BlockSpec((tm, D_POOL), lambda i: (i, 0)),
                   pl.BlockSpec((1, D_SSM), lambda i: (0, 0)),
                   pl.BlockSpec(memory_space=pl.ANY)],
        out_shape=[jax.ShapeDtypeStruct((L, D_SSM), F32), jax.ShapeDtypeStruct((L, D_POOL), F32),
                   jax.ShapeDtypeStruct((1, D_SSM), F32),
                   jax.ShapeDtypeStruct(gbuf.shape, F32)],
        scratch_shapes=[pltpu.VMEM((N_SHARD, ob, D_MODEL), F32), pltpu.VMEM((D_SSM, D_SSM), F32),
                        pltpu.VMEM((N_SHARD, P_GLU_PAD, D_MODEL), F32)],
        input_output_aliases={6: 3},
        compiler_params=_cparams(1),
    )(dhm, yraw, ypool, wp, b_glu, wp, gbuf)


def _ssm_bwd(dyraw, u, sre, sim, layer, cpad_t, bpad_t, ar, ai, dskip):
    L = u.shape[0]
    ts = min(TS, L)
    nt = L // ts
    nq = 4
    cq = N_STATE // nq

    def body(dy_ref, u_ref, sre_ref, sim_ref, ct_ref, bt_ref, ar_ref, ai_ref, dsk_ref,
             du_ref, dcp_ref, dbp_ref, dar_ref, dai_ref, ddsk_ref, gre, gim, cr, ci, tab, accr, acci, up, dyp):
        t = pl.program_id(1)

        @pl.when(t == 0)
        def _():
            for ref in (cr, ci, accr, acci, dcp_ref, dbp_ref, ddsk_ref):
                ref[...] = jnp.zeros_like(ref)
            _scan_tables(ar_ref[...], -ai_ref[...], tab, reverse=True)

        _permute_rows(dy_ref, dyp, ts)
        _permute_rows(u_ref, up, ts)
        dy = dyp[...]
        dyb = dy.astype(BF16)
        uf = up[...]
        ub = uf.astype(BF16)
        for jj in range(4):
            cols = slice(jj * 128, (jj + 1) * 128)
            ds = _dot(dyb, ct_ref[jj])
            gre[:, cols] = ds[:, :128]
            gim[:, cols] = ds[:, 128:]
            scat = jnp.concatenate([sre_ref[:, cols], sim_ref[:, cols]], axis=1).astype(BF16)
            dcp_ref[jj] += _dot_tn(scat, dyb)

        n_blk = ts // SCAN_BLOCK
        shp = (SUBLANES, SCAN_LANES)
        last_row = lax.broadcasted_iota(jnp.int32, shp, 0) == SUBLANES - 1
        for cc in range(cq // SCAN_LANES):
            cols = slice(cc * SCAN_LANES, (cc + 1) * SCAN_LANES)

            def block(i, carry, cols=cols):
                c_r, c_i, a_r, a_i = carry
                base = pl.multiple_of((n_blk - 1 - i) * SCAN_BLOCK, SCAN_BLOCK)
                rows = lambda tau: pl.ds(base + SUBLANES * tau, SUBLANES)
                m_r, m_i = tab[0, :, cols], tab[1, :, cols]
                ys = [None] * SUBLANES
                ys[SUBLANES - 1] = (gre[rows(SUBLANES - 1), cols], gim[rows(SUBLANES - 1), cols])
                for tau in reversed(range(SUBLANES - 1)):
                    ys[tau] = _cmac(gre[rows(tau), cols], gim[rows(tau), cols], m_r, m_i, *ys[tau + 1])
                tr, ti = _chain_segments(*ys[0], c_r, c_i, tab, cols, reverse=True)
                in_r = jnp.where(last_row, c_r, pltpu.roll(tr, SUBLANES - 1, 0))
                in_i = jnp.where(last_row, c_i, pltpu.roll(ti, SUBLANES - 1, 0))
                gs = [_cmac(*ys[tau], tab[10 + 2 * tau, :, cols], tab[11 + 2 * tau, :, cols], in_r, in_i)
                      for tau in range(SUBLANES)]
                for tau in range(SUBLANES):
                    gre[rows(tau), cols] = gs[tau][0]
                    gim[rows(tau), cols] = gs[tau][1]
                    if tau < SUBLANES - 1:
                        nr, ni = gs[tau + 1]
                    else:
                        nr = jnp.where(last_row, c_r, pltpu.roll(gs[0][0], SUBLANES - 1, 0))
                        ni = jnp.where(last_row, c_i, pltpu.roll(gs[0][1], SUBLANES - 1, 0))
                    sr, si = sre_ref[rows(tau), cols], sim_ref[rows(tau), cols]
                    a_r = a_r + sr * nr + si * ni
                    a_i = a_i + sr * ni - si * nr
                return (jnp.broadcast_to(tr[:1, :], shp), jnp.broadcast_to(ti[:1, :], shp), a_r, a_i)

            c_r, c_i, a_r, a_i = lax.fori_loop(
                0, n_blk, block, (cr[:, cols], ci[:, cols], accr[:, cols], acci[:, cols]), unroll=2)
            cr[:, cols] = c_r
            ci[:, cols] = c_i
            accr[:, cols] = a_r
            acci[:, cols] = a_i

        acc = dsk_ref[...] * dy
        for jj in range(4):
            cols = slice(jj * 128, (jj + 1) * 128)
            gcat = jnp.concatenate([gre[:, cols], gim[:, cols]], axis=1).astype(BF16)
            acc = acc + _dot(gcat, bt_ref[jj])
            dbp_ref[jj] += _dot_tn(ub, gcat)
        ddsk_ref[...] += jnp.sum(dy * uf, axis=0, keepdims=True)
        dyp[...] = acc
        _permute_rows(dyp, du_ref, ts)

        @pl.when(t == nt - 1)
        def _():
            dar_ref[...] = jnp.sum(accr[...], axis=0, keepdims=True)
            dai_ref[...] = jnp.sum(acci[...], axis=0, keepdims=True)

    f32_scr = lambda *s: pltpu.VMEM(s, F32)
    return pl.pallas_call(
        body, name="ssm_bwd", grid=(nq, nt),
        in_specs=[pl.BlockSpec((ts, 128), lambda q, t: (nt - 1 - t, q)),
                  pl.BlockSpec((ts, 128), lambda q, t: (nt - 1 - t, 4 + q)),
                  pl.BlockSpec((ts, cq), lambda q, t: (nt - 1 - t, q)),
                  pl.BlockSpec((ts, cq), lambda q, t: (nt - 1 - t, q)),
                  pl.BlockSpec((None, 4, 128, 256), lambda q, t: (layer, q, 0, 0)),
                  pl.BlockSpec((None, 4, 256, 128), lambda q, t: (layer, q, 0, 0)),
                  pl.BlockSpec((None, 1, cq), lambda q, t: (layer, 0, q)),
                  pl.BlockSpec((None, 1, cq), lambda q, t: (layer, 0, q)),
                  pl.BlockSpec((None, 1, 128), lambda q, t: (layer, 0, q))],
        out_specs=[pl.BlockSpec((ts, 128), lambda q, t: (nt - 1 - t, q)),
                   pl.BlockSpec((4, 256, 128), lambda q, t: (q, 0, 0)),
                   pl.BlockSpec((4, 128, 256), lambda q, t: (q, 0, 0)),
                   pl.BlockSpec((1, cq), lambda q, t: (0, q)),
                   pl.BlockSpec((1, cq), lambda q, t: (0, q)),
                   pl.BlockSpec((1, 128), lambda q, t: (0, q))],
        out_shape=[jax.ShapeDtypeStruct((L, D_SSM), F32),
                   jax.ShapeDtypeStruct((N_PAIRS, 256, 128), F32), jax.ShapeDtypeStruct((N_PAIRS, 128, 256), F32),
                   jax.ShapeDtypeStruct((1, N_STATE), F32), jax.ShapeDtypeStruct((1, N_STATE), F32),
                   jax.ShapeDtypeStruct((1, D_SSM), F32)],
        scratch_shapes=[f32_scr(ts, cq), f32_scr(ts, cq), f32_scr(SUBLANES, cq), f32_scr(SUBLANES, cq),
                        f32_scr(N_SCAN_TABLES, SUBLANES, cq), f32_scr(SUBLANES, cq), f32_scr(SUBLANES, cq),
                        f32_scr(ts, 128), f32_scr(ts, 128)],
        compiler_params=_cparams(2),
    )(dyraw, u, sre, sim, cpad_t, bpad_t, ar, ai, dskip)


def _pool_bwd(dyp, u, layer, w_pool, scale):
    L = u.shape[0]
    tm = min(TM, L)
    nt = L // tm
    halo_per_tile = tm // POOL_HALO

    def body(dyp_ref, u_ref, halo_ref, wp_ref, sc_ref, du_ref, dwp_ref, dsc_ref, carry):
        i = pl.program_id(0)
        tile = nt - 1 - i

        @pl.when(i == 0)
        def _():
            carry[...] = jnp.zeros_like(carry)
            dwp_ref[...] = jnp.zeros_like(dwp_ref)
            dsc_ref[...] = jnp.zeros_like(dsc_ref)

        up = u_ref[...]
        halo = jnp.where(tile > 0, halo_ref[...], jnp.zeros_like(halo_ref))
        diffs = _pool_diff(jnp.concatenate([halo, up], axis=0), tile * tm, tm)
        rows = tile * tm + lax.broadcasted_iota(jnp.int32, (tm, 1), 0)
        n_ext = tm + POOL_HALO
        for gi, w in enumerate(POOL_WINDOWS):
            cols = slice(gi * POOL_GROUP, (gi + 1) * POOL_GROUP)
            db = diffs[gi].astype(BF16)
            dyp = dyp_ref[:, cols]
            dsc_ref[:, cols] += jnp.sum(dyp * _dot(db, wp_ref[gi]), axis=0, keepdims=True)
            dp = (dyp * sc_ref[:, cols]).astype(BF16)
            ddiff = _dot_nt(dp, wp_ref[gi])
            dwp_ref[gi] += _dot_tn(db, dp)
            e = ddiff * (1.0 / jnp.minimum(rows + 1, w).astype(F32))
            s = jnp.concatenate([e, carry[:, cols]], axis=0)
            k = 1
            while k < w:
                s = s + pltpu.roll(s, n_ext - k, 0)
                k *= 2
            du_ref[:, cols] = s[:tm, :] - ddiff
            carry[:, cols] = e[:POOL_HALO, :]

    return pl.pallas_call(
        body, name="pool_bwd", grid=(nt,),
        in_specs=[pl.BlockSpec((tm, D_POOL), lambda i: (nt - 1 - i, 0)),
                  pl.BlockSpec((tm, D_POOL), lambda i: (nt - 1 - i, 0)),
                  pl.BlockSpec((POOL_HALO, D_POOL), lambda i: (jnp.maximum((nt - 1 - i) * halo_per_tile - 1, 0), 0)),
                  pl.BlockSpec((None, 4, POOL_GROUP, POOL_GROUP), lambda i: (layer, 0, 0, 0)),
                  pl.BlockSpec((None, 1, D_POOL), lambda i: (layer, 0, 0))],
        out_specs=[pl.BlockSpec((tm, D_POOL), lambda i: (nt - 1 - i, 0)),
                   pl.BlockSpec((4, POOL_GROUP, POOL_GROUP), lambda i: (0, 0, 0)),
                   pl.BlockSpec((1, D_POOL), lambda i: (0, 0))],
        out_shape=[jax.ShapeDtypeStruct((L, D_POOL), F32),
                   jax.ShapeDtypeStruct((4, POOL_GROUP, POOL_GROUP), F32),
                   jax.ShapeDtypeStruct((1, D_POOL), F32)],
        scratch_shapes=[pltpu.VMEM((POOL_HALO, D_POOL), F32)],
        compiler_params=_cparams(1),
    )(dyp, u, u, w_pool, scale)


def _mix_in_bwd(dup, dus, h, dhm, g1, wp, layer, gbuf):
    L = h.shape[0]
    tm = min(TM, L)
    n_steps = L // tm
    blk, idx = P_IN_BLK

    def body(dup_ref, dus_ref, h_ref, dhm_ref, g_ref, w_ref, g1_in, dh_ref, dg_ref, g1_ref, dwin):
        i = pl.program_id(0)

        @pl.when(i == 0)
        def _():
            dg_ref[...] = jnp.zeros_like(dg_ref)
            dwin[...] = jnp.zeros_like(dwin)

        du = jnp.concatenate([dup_ref[...], dus_ref[...]], axis=1).astype(BF16)
        dn1 = _dot_nt(du, w_ref[...].reshape(D_MODEL, D_MODEL))
        xhat, r = _rms_hat(h_ref[...])
        g = g_ref[...]
        n1 = (xhat * g).astype(BF16)
        dwin[...] += _dot_tn(n1, du).reshape(N_SHARD, blk, D_MODEL)
        dg_ref[...] += jnp.sum(dn1 * xhat, axis=0, keepdims=True)
        dh_ref[...] = dhm_ref[...] + _rms_bwd(dn1 * g, xhat, r)

        @pl.when(i == n_steps - 1)
        def _():
            pltpu.sync_copy(dwin, g1_ref.at[0, :, pl.ds(blk * idx, blk), :])

    row_spec = pl.BlockSpec((tm, D_MODEL), lambda i: (i, 0))
    half_spec = pl.BlockSpec((tm, D_POOL), lambda i: (i, 0))
    return pl.pallas_call(
        body, name="mix_in_bwd", grid=(n_steps,),
        in_specs=[half_spec, half_spec, row_spec, row_spec,
                  pl.BlockSpec((None, 1, D_MODEL), lambda i: (layer, 0, 0)),
                  pl.BlockSpec((N_SHARD, None, blk, D_MODEL), lambda i: (0, 0, idx, 0)),
                  pl.BlockSpec(memory_space=pl.ANY)],
        out_specs=[row_spec, pl.BlockSpec((1, D_MODEL), lambda i: (0, 0)), pl.BlockSpec(memory_space=pl.ANY)],
        out_shape=[jax.ShapeDtypeStruct((L, D_MODEL), F32), jax.ShapeDtypeStruct((1, D_MODEL), F32),
                   jax.ShapeDtypeStruct(gbuf.shape, F32)],
        scratch_shapes=[pltpu.VMEM((N_SHARD, blk, D_MODEL), F32)],
        input_output_aliases={6: 2},
        compiler_params=_cparams(1),
    )(dup, dus, h, dhm, g1, wp, gbuf)


def _disc_math(lr, li, ldt, br_t, bi_t):
    dt = jnp.exp(ldt)
    mag = jnp.exp(lr * dt)
    ang = li * dt
    ar = mag * jnp.cos(ang)
    ai = mag * jnp.sin(ang)
    den = lr * lr + li * li
    nr, ni = ar - 1.0, ai
    cr = (nr * lr + ni * li) / den
    ci = (ni * lr - nr * li) / den
    return ar, ai, cr * br_t - ci * bi_t, cr * bi_t + ci * br_t


def _disc_fwd(lr, li, ldt, br_t, bi_t):
    def body(lr_ref, li_ref, ldt_ref, br_ref, bi_ref, ar_ref, ai_ref, bbr_ref, bbi_ref):
        ar, ai, bbr, bbi = _disc_math(lr_ref[...], li_ref[...], ldt_ref[...], br_ref[...], bi_ref[...])
        ar_ref[...] = ar
        ai_ref[...] = ai
        bbr_ref[...] = bbr
        bbi_ref[...] = bbi

    shapes = [jax.ShapeDtypeStruct(a.shape, F32) for a in (lr, li, br_t, bi_t)]
    return pl.pallas_call(body, name="ssm_disc_fwd", out_shape=shapes,
                          compiler_params=pltpu.CompilerParams(vmem_limit_bytes=VMEM_LIMIT))(lr, li, ldt, br_t, bi_t)


def _disc_bwd(lr, li, ldt, br_t, bi_t, dar, dai, dbbr, dbbi):
    def body(lr_ref, li_ref, ldt_ref, br_ref, bi_ref, dar_ref, dai_ref, dbbr_ref, dbbi_ref,
             dlr_ref, dli_ref, dldt_ref, dbr_ref, dbi_ref):
        prim = (lr_ref[...], li_ref[...], ldt_ref[...], br_ref[...], bi_ref[...])
        _, pullback = jax.vjp(_disc_math, *prim)
        dlr, dli, dldt, dbr, dbi = pullback((dar_ref[...], dai_ref[...], dbbr_ref[...], dbbi_ref[...]))
        dlr_ref[...] = dlr
        dli_ref[...] = dli
        dldt_ref[...] = dldt
        dbr_ref[...] = dbr
        dbi_ref[...] = dbi

    shapes = [jax.ShapeDtypeStruct(a.shape, F32) for a in (lr, li, ldt, br_t, bi_t)]
    return pl.pallas_call(body, name="ssm_disc_bwd", out_shape=shapes,
                          compiler_params=pltpu.CompilerParams(vmem_limit_bytes=VMEM_LIMIT))(
        lr, li, ldt, br_t, bi_t, dar, dai, dbbr, dbbi)


def _pad_pairs(m_re, m_im):
    def blocks(m):
        v = m.transpose(0, 2, 1).reshape(N_PAIRS, 2, SSM_GROUP, SSM_STATE)
        return jnp.einsum("ab,jahp->jahbp", jnp.eye(2, dtype=m.dtype), v).reshape(N_PAIRS, 32, 128)
    both = jnp.concatenate([blocks(m_re), blocks(m_im)], axis=-1)
    place = jax.nn.one_hot(jnp.arange(N_PAIRS) % 4, 4, dtype=both.dtype)
    return jnp.einsum("jk,jrc->jkrc", place, both).reshape(N_PAIRS, 128, 256)


def _unpad_pairs(x):
    place = jax.nn.one_hot(jnp.arange(N_PAIRS) % 4, 4, dtype=x.dtype)
    both = jnp.einsum("jk,jkrc->jrc", place, x.reshape(N_PAIRS, 4, 32, 256))

    def unblock(v):
        v = v.reshape(N_PAIRS, 2, SSM_GROUP, 2, SSM_STATE)
        d = jnp.einsum("ab,jahbp->jahp", jnp.eye(2, dtype=x.dtype), v)
        return d.reshape(N_SSM_GROUPS, SSM_GROUP, SSM_STATE).transpose(0, 2, 1)
    return unblock(both[..., :128]), unblock(both[..., 128:])


def _adamw_math(w, g, m, v):
    m = ADAM_B1 * m + (1.0 - ADAM_B1) * g
    v = ADAM_B2 * v + (1.0 - ADAM_B2) * (g * g)
    m_hat = m / (1.0 - ADAM_B1 ** ADAM_STEP)
    v_hat = v / (1.0 - ADAM_B2 ** ADAM_STEP)
    delta = -ADAM_LR * (m_hat / (jnp.sqrt(v_hat) + ADAM_EPS) + ADAM_WD * w)
    return delta, m, v


def _adamw(name, layer, w, m, v, gbuf, g_block, g_row0, row_tile, outs=None, after=(), glu=False):
    nl, r, c = w.shape
    n_tiles = r // row_tile
    g_rows, g_cols = g_block
    g_tile = g_rows // n_tiles
    g_off = g_row0 // g_tile
    if outs is None:
        outs = [lax.empty(w.shape, F32) for _ in range(4)]

    def body(w_ref, m_ref, v_ref, g_ref, *rest):
        go_ref, d_ref, mo_ref, vo_ref = rest[-4:]
        g = g_ref[...]
        if glu:
            g = jnp.concatenate([g[:, :D_SSM], g[:, D_SSM:]], axis=0)
        delta, mn, vn = _adamw_math(w_ref[...], g, m_ref[...], v_ref[...])
        go_ref[...] = g
        d_ref[...] = delta
        mo_ref[...] = mn
        vo_ref[...] = vn

    w_spec = pl.BlockSpec((None, row_tile, c), lambda j: (layer, j, 0))
    shape = jax.ShapeDtypeStruct(w.shape, F32)
    return pl.pallas_call(
        body, name=name, grid=(n_tiles,),
        in_specs=[w_spec, w_spec, w_spec, pl.BlockSpec((None, g_tile, g_cols), lambda j: (0, g_off + j, 0))]
        + [_ANY] * (4 + len(after)),
        out_specs=[w_spec] * 4,
        out_shape=[shape] * 4,
        input_output_aliases={4: 0, 5: 1, 6: 2, 7: 3},
        compiler_params=_cparams(1),
    )(w, m, v, gbuf, *outs, *after)


def _adamw_group(name, layer, ws, ms, vs, gbuf, g_row0s, row_tile, outs=None):
    k = len(ws)
    nl, r, c = ws[0].shape
    n_tiles = r // row_tile
    if outs is None:
        outs = [[lax.empty(ws[0].shape, F32) for _ in range(4)] for _ in range(k)]

    def body(*refs):
        ins, results = refs[:4 * k], refs[-4 * k:]
        for i in range(k):
            w_ref, m_ref, v_ref, g_ref = (ins[j * k + i] for j in range(4))
            g = g_ref[...]
            delta, mn, vn = _adamw_math(w_ref[...], g, m_ref[...], v_ref[...])
            for ref, val in zip(results[4 * i:4 * i + 4], (g, delta, mn, vn)):
                ref[...] = val

    w_spec = pl.BlockSpec((None, row_tile, c), lambda j: (layer, j, 0))
    g_specs = [pl.BlockSpec((None, row_tile, c), functools.partial(lambda j, off: (0, off + j, 0), off=r0 // row_tile))
               for r0 in g_row0s]
    shape = jax.ShapeDtypeStruct(ws[0].shape, F32)
    flat = pl.pallas_call(
        body, name=name, grid=(n_tiles,),
        in_specs=[w_spec] * (3 * k) + g_specs + [_ANY] * (4 * k),
        out_specs=[w_spec] * (4 * k),
        out_shape=[shape] * (4 * k),
        input_output_aliases={4 * k + i: i for i in range(4 * k)},
        compiler_params=_cparams(1),
    )(*ws, *ms, *vs, *([gbuf] * k), *[a for group in outs for a in group])
    return [flat[4 * i:4 * i + 4] for i in range(k)]


def _pack_weights(ids, layer, w_in, w_glu, w_out, w_down, w_gate_t, w_up_t, after=()):
    gb, gi = P_GLU_BLK
    ib, ii = P_IN_BLK
    ob, oi = P_OUT_BLK

    def body(ids_ref, in_ref, glu_ref, out_ref, dn_ref, gate_ref, up_ref, *rest):
        p_ref = rest[-1]
        p_ref[0:FF_SHARD, :] = dn_ref[...].astype(BF16)
        p_ref[FF_SHARD:2 * FF_SHARD, :] = gate_ref[...].astype(BF16)
        p_ref[2 * FF_SHARD:P_FF_ROWS, :] = up_ref[...].astype(BF16)
        g = glu_ref[...]
        p_ref[gb * gi:gb * (gi + 1), :] = jnp.concatenate([g[:gb, :], g[gb:, :]], axis=1).astype(BF16)
        p_ref[gb * (gi + 1):ib * ii, :] = jnp.zeros((P_GLU_PAD - gb, D_MODEL), BF16)
        p_ref[ib * ii:ib * (ii + 1), :] = in_ref[...].astype(BF16)
        p_ref[ob * oi:ob * (oi + 1), :] = out_ref[...].astype(BF16)

    def spec(a):
        return pl.BlockSpec((None,) + a.shape[1:], lambda i, ids_ref: (layer, 0, 0))

    ins = (w_in, w_glu, w_out, w_down, w_gate_t, w_up_t)
    grid_spec = pltpu.PrefetchScalarGridSpec(
        num_scalar_prefetch=1, grid=(1,),
        in_specs=[spec(a) for a in ins] + [_ANY] * len(after),
        out_specs=pl.BlockSpec((None, None, P_ROWS, D_MODEL), lambda i, ids_ref: (ids_ref[1], 0, 0, 0)))
    return pl.pallas_call(
        body, name="pack_weights", grid_spec=grid_spec,
        out_shape=jax.ShapeDtypeStruct((N_SHARD, 1, P_ROWS, D_MODEL), BF16),
        compiler_params=_cparams(1),
    )(ids, *ins, *after)


MESH = pl.DeviceIdType.MESH
_ANY = pl.BlockSpec(memory_space=pl.ANY)
P_HALF = P_ROWS // 2
RS_ROW_TILE = 352


def _mesh_pos():
    return lax.axis_index("x"), lax.axis_index("y"), lax.axis_index("c")


def _other_chips(x, y):
    return [(1 - x, y), (x, 1 - y), (1 - x, 1 - y)]


def _remote(src, dst, send_sems, recv_sems, n, to):
    return pltpu.make_async_remote_copy(src_ref=src, dst_ref=dst, send_sem=send_sems.at[n],
                                        recv_sem=recv_sems.at[n], device_id=to, device_id_type=MESH)


_HBM = pl.BlockSpec(memory_space=pltpu.HBM)
_SEM = pl.BlockSpec(memory_space=pltpu.SEMAPHORE)
_EFFECT = pltpu.CompilerParams(has_side_effects=pltpu.SideEffectType.DATAFLOW_SIDE_EFFECTING)
_TOKEN = jax.ShapeDtypeStruct((8, 128), F32)


def _in_hbm(a):
    return pltpu.with_memory_space_constraint(a, pltpu.HBM)


def _ag_piece(ref, shard, half, rows):
    row0, n_rows = rows
    return ref.at[shard, :, pl.ds(row0 + half * (n_rows // 2), n_rows // 2), :]


def _ag_start(name, wp, after, row_ranges):
    n_sems = 3 * len(row_ranges)

    def body(w_ref, after_ref, send_sems, recv_sems, w_thru, token):
        x, y, c = _mesh_pos()
        for i, rows in enumerate(row_ranges):
            mine = _ag_piece(w_ref, 2 * x + y, c, rows)
            for j, (px, py) in enumerate(_other_chips(x, y)):
                _remote(mine, mine, send_sems, recv_sems, 3 * i + j, (px, py, c)).start()
        token[...] = jnp.zeros_like(token)

    return pl.pallas_call(
        body, name=name,
        out_shape=(pltpu.SemaphoreType.DMA((n_sems,)), pltpu.SemaphoreType.DMA((n_sems,)),
                   pltpu.HBM(wp.shape, wp.dtype), _TOKEN),
        in_specs=(_HBM, _ANY), out_specs=(_SEM, _SEM, _HBM, pl.BlockSpec(memory_space=pltpu.VMEM)),
        input_output_aliases={0: 2}, compiler_params=_EFFECT,
    )(_in_hbm(wp), after)


def _ag_wait(name, send_sems, recv_sems, wp, after, row_ranges):
    def body(w_ref, send_sems, recv_sems, *rest):
        x, y, c = _mesh_pos()
        for i, rows in enumerate(row_ranges):
            mine = _ag_piece(w_ref, 2 * x + y, c, rows)
            for j, (px, py) in enumerate(_other_chips(x, y)):
                landed = _ag_piece(w_ref, 2 * px + py, c, rows)
                cp = _remote(mine, landed, send_sems, recv_sems, 3 * i + j, (px, py, c))
                cp.wait_send()
                cp.wait_recv()

    return pl.pallas_call(
        body, name=name, out_shape=pltpu.HBM(wp.shape, wp.dtype),
        in_specs=(_HBM, _SEM, _SEM) + (_ANY,) * len(after), out_specs=_HBM,
        input_output_aliases={0: 0}, compiler_params=_EFFECT,
    )(wp, send_sems, recv_sems, *after)


def _ag_forward(wp, rows):
    def body(w_in, o, send_sems, recv_sems):
        x, y, c = _mesh_pos()
        sib = (x, y, 1 - c)
        chips = _other_chips(x, y)
        sends = []
        for j, (px, py) in enumerate(chips):
            landed = _ag_piece(o, 2 * px + py, c, rows)
            cp = _remote(landed, landed, send_sems, recv_sems, j, sib)
            cp.start()
            sends.append(cp)
        for j, (px, py) in enumerate(chips):
            passed = _ag_piece(o, 2 * px + py, 1 - c, rows)
            _remote(passed, passed, send_sems, recv_sems, j, sib).wait_recv()
        for cp in sends:
            cp.wait_send()

    return pl.pallas_call(
        body, name="ag_forward",
        in_specs=[_ANY], out_specs=_ANY,
        out_shape=jax.ShapeDtypeStruct(wp.shape, wp.dtype),
        scratch_shapes=[pltpu.SemaphoreType.DMA((3,)), pltpu.SemaphoreType.DMA((3,))],
        input_output_aliases={0: 0},
    )(wp)


def _ag_forward_start(name, wp, rows):
    def body(w_ref, send_sems, recv_sems, w_thru):
        x, y, c = _mesh_pos()
        for j, (px, py) in enumerate(_other_chips(x, y)):
            landed = _ag_piece(w_ref, 2 * px + py, c, rows)
            _remote(landed, landed, send_sems, recv_sems, j, (x, y, 1 - c)).start()

    return pl.pallas_call(
        body, name=name,
        out_shape=(pltpu.SemaphoreType.DMA((3,)), pltpu.SemaphoreType.DMA((3,)), pltpu.HBM(wp.shape, wp.dtype)),
        in_specs=(_HBM,), out_specs=(_SEM, _SEM, _HBM),
        input_output_aliases={0: 2}, compiler_params=_EFFECT,
    )(_in_hbm(wp))


def _ag_forward_wait(name, send_sems, recv_sems, wp, after, rows):
    def body(w_ref, send_sems, recv_sems, *rest):
        x, y, c = _mesh_pos()
        for j, (px, py) in enumerate(_other_chips(x, y)):
            cp = _remote(_ag_piece(w_ref, 2 * px + py, c, rows), _ag_piece(w_ref, 2 * px + py, 1 - c, rows),
                         send_sems, recv_sems, j, (x, y, 1 - c))
            cp.wait_send()
            cp.wait_recv()

    return pl.pallas_call(
        body, name=name, out_shape=pltpu.HBM(wp.shape, wp.dtype),
        in_specs=(_HBM, _SEM, _SEM) + (_ANY,) * len(after), out_specs=_HBM,
        input_output_aliases={0: 0}, compiler_params=_EFFECT,
    )(wp, send_sems, recv_sems, *after)


def _rs_chips_start(name, t):
    nl = t.shape[0]

    def body(t_ref, land_ref, send_sems, recv_sems, t_thru, land_thru, token):
        x, y, c = _mesh_pos()
        for j, (px, py) in enumerate(_other_chips(x, y)):
            _remote(t_ref.at[:, 2 * px + py], land_ref.at[j], send_sems, recv_sems, j, (px, py, c)).start()
        token[...] = jnp.zeros_like(token)

    land = lax.empty((3, nl, P_HALF, D_MODEL), BF16)
    return pl.pallas_call(
        body, name=name,
        out_shape=(pltpu.SemaphoreType.DMA((3,)), pltpu.SemaphoreType.DMA((3,)), pltpu.HBM(t.shape, t.dtype),
                   pltpu.HBM(land.shape, land.dtype), _TOKEN),
        in_specs=(_HBM, _HBM), out_specs=(_SEM, _SEM, _HBM, _HBM, pl.BlockSpec(memory_space=pltpu.VMEM)),
        input_output_aliases={0: 2, 1: 3}, compiler_params=_EFFECT,
    )(_in_hbm(t), _in_hbm(land))


def _rs_chips_wait(name, send_sems, recv_sems, t, land, after):
    def body(t_ref, land_ref, send_sems, recv_sems, *rest):
        x, y, c = _mesh_pos()
        for j, (px, py) in enumerate(_other_chips(x, y)):
            cp = _remote(t_ref.at[:, 2 * px + py], land_ref.at[j], send_sems, recv_sems, j, (px, py, c))
            cp.wait_send()
            cp.wait_recv()

    return pl.pallas_call(
        body, name=name, out_shape=(pltpu.HBM(t.shape, t.dtype), pltpu.HBM(land.shape, land.dtype)),
        in_specs=(_HBM, _HBM, _SEM, _SEM) + (_ANY,) * len(after), out_specs=(_HBM, _HBM),
        input_output_aliases={0: 0, 1: 1}, compiler_params=_EFFECT,
    )(t, land, send_sems, recv_sems, *after)[1]


def _rs_sibling_start(name, g):
    nl = g.shape[0]

    def body(g_ref, land_ref, send_sems, recv_sems, g_thru, land_thru, token):
        x, y, c = _mesh_pos()
        _remote(g_ref.at[:, :, pl.ds((1 - c) * P_HALF, P_HALF), :], land_ref, send_sems, recv_sems, 0,
                (x, y, 1 - c)).start()
        token[...] = jnp.zeros_like(token)

    land = lax.empty((nl, N_SHARD, P_HALF, D_MODEL), F32)
    return pl.pallas_call(
        body, name=name,
        out_shape=(pltpu.SemaphoreType.DMA((1,)), pltpu.SemaphoreType.DMA((1,)), pltpu.HBM(g.shape, g.dtype),
                   pltpu.HBM(land.shape, land.dtype), _TOKEN),
        in_specs=(_HBM, _HBM), out_specs=(_SEM, _SEM, _HBM, _HBM, pl.BlockSpec(memory_space=pltpu.VMEM)),
        input_output_aliases={0: 2, 1: 3}, compiler_params=_EFFECT,
    )(_in_hbm(g), _in_hbm(land))


def _rs_sibling_wait(name, send_sems, recv_sems, g, land, after):
    def body(g_ref, land_ref, send_sems, recv_sems, *rest):
        x, y, c = _mesh_pos()
        cp = _remote(g_ref.at[:, :, pl.ds((1 - c) * P_HALF, P_HALF), :], land_ref, send_sems, recv_sems, 0,
                     (x, y, 1 - c))
        cp.wait_send()
        cp.wait_recv()

    return pl.pallas_call(
        body, name=name, out_shape=(pltpu.HBM(g.shape, g.dtype), pltpu.HBM(land.shape, land.dtype)),
        in_specs=(_HBM, _HBM, _SEM, _SEM) + (_ANY,) * len(after), out_specs=(_HBM, _HBM),
        input_output_aliases={0: 0, 1: 1}, compiler_params=_EFFECT,
    )(g, land, send_sems, recv_sems, *after)


def _rs_add(name, ids, g, buf, row_tile):
    nl, _, hr, cols = buf.shape
    n_rt = hr // row_tile

    def body(ids_ref, g_ref, b_ref, own_ref, tb_ref):
        t = g_ref[...] + b_ref[...]
        tb_ref[...] = t.astype(BF16)

        @pl.when(pl.program_id(2) == ids_ref[1])
        def _():
            own_ref[...] = t

    blk = (None, None, row_tile, cols)
    grid_spec = pltpu.PrefetchScalarGridSpec(
        num_scalar_prefetch=1, grid=(nl, n_rt, N_SHARD),
        in_specs=[pl.BlockSpec(blk, lambda l, j, s, ids_ref: (l, s, ids_ref[0] * n_rt + j, 0)),
                  pl.BlockSpec(blk, lambda l, j, s, ids_ref: (l, s, j, 0))],
        out_specs=[pl.BlockSpec((None, row_tile, cols), lambda l, j, s, ids_ref: (l, j, 0)),
                   pl.BlockSpec(blk, lambda l, j, s, ids_ref: (l, s, j, 0))])
    return pl.pallas_call(
        body, name=name, grid_spec=grid_spec,
        out_shape=[jax.ShapeDtypeStruct((nl, hr, cols), F32), jax.ShapeDtypeStruct(buf.shape, BF16)],
        compiler_params=_cparams(3),
    )(ids, g, buf)


def _rs_sum(ids, layer, own, bufb, reduced, row_tile):
    _, hr, cols = own.shape
    n_rt = hr // row_tile

    def body(ids_ref, own_ref, b_ref, reduced_in, f_ref):
        f_ref[...] = ((own_ref[...] + b_ref[0].astype(F32)) + b_ref[1].astype(F32)) + b_ref[2].astype(F32)

    grid_spec = pltpu.PrefetchScalarGridSpec(
        num_scalar_prefetch=1, grid=(n_rt,),
        in_specs=[pl.BlockSpec((None, row_tile, cols), lambda j, ids_ref: (0, j, 0)),
                  pl.BlockSpec((3, None, row_tile, cols), lambda j, ids_ref: (0, 0, j, 0)),
                  pl.BlockSpec(memory_space=pl.ANY)],
        out_specs=pl.BlockSpec((None, row_tile, cols), lambda j, ids_ref: (layer, ids_ref[0] * n_rt + j, 0)))
    return pl.pallas_call(
        body, name="rs_sum", grid_spec=grid_spec,
        out_shape=jax.ShapeDtypeStruct(reduced.shape, F32),
        input_output_aliases={3: 0},
        compiler_params=_cparams(1),
    )(ids, own, bufb, reduced)


def _rs_exchange_start(name, f):
    def body(f_ref, send_sems, recv_sems, f_thru):
        x, y, c = _mesh_pos()
        mine = f_ref.at[:, pl.ds(c * P_HALF, P_HALF), :]
        _remote(mine, mine, send_sems, recv_sems, 0, (x, y, 1 - c)).start()

    return pl.pallas_call(
        body, name=name,
        out_shape=(pltpu.SemaphoreType.DMA((1,)), pltpu.SemaphoreType.DMA((1,)), pltpu.HBM(f.shape, f.dtype)),
        in_specs=(_HBM,), out_specs=(_SEM, _SEM, _HBM),
        input_output_aliases={0: 2}, compiler_params=_EFFECT,
    )(_in_hbm(f))


def _rs_exchange_wait(name, send_sems, recv_sems, f, after):
    def body(f_ref, send_sems, recv_sems, *rest):
        x, y, c = _mesh_pos()
        mine = f_ref.at[:, pl.ds(c * P_HALF, P_HALF), :]
        theirs = f_ref.at[:, pl.ds((1 - c) * P_HALF, P_HALF), :]
        cp = _remote(mine, theirs, send_sems, recv_sems, 0, (x, y, 1 - c))
        cp.wait_send()
        cp.wait_recv()

    return pl.pallas_call(
        body, name=name, out_shape=pltpu.HBM(f.shape, f.dtype),
        in_specs=(_HBM, _SEM, _SEM) + (_ANY,) * len(after), out_specs=_HBM,
        input_output_aliases={0: 0}, compiler_params=_EFFECT,
    )(f, send_sems, recv_sems, *after)


def _small_all_reduce(s, after=()):
    n_rows = s.shape[0]
    hr = n_rows // 2
    qr = hr // N_SHARD

    def body(s_ref, *rest):
        o_ref, sibbuf, tbuf, qbuf, fbuf, send_sems, recv_sems = rest[len(after):]
        x, y, c = _mesh_pos()
        k = 2 * x + y
        sib = (x, y, 1 - c)
        chips = _other_chips(x, y)
        mine = pl.ds(pl.multiple_of(c * hr, SUBLANES), hr)
        theirs = pl.ds(pl.multiple_of((1 - c) * hr, SUBLANES), hr)

        def quarter(shard):
            return pl.ds(pl.multiple_of(shard * qr, SUBLANES), qr)

        first = _remote(s_ref.at[theirs], sibbuf, send_sems, recv_sems, 0, sib)
        first.start()
        first.wait()
        tbuf[...] = s_ref[mine, :] + sibbuf[...]
        cps = []
        for j, (px, py) in enumerate(chips):
            cp = _remote(tbuf.at[quarter(2 * px + py)], qbuf.at[j], send_sems, recv_sems, 1 + j, (px, py, c))
            cp.start()
            cps.append(cp)
        for cp in cps:
            cp.wait()
        fbuf[quarter(k), :] = (tbuf[quarter(k), :] + qbuf[1]) + (qbuf[0] + qbuf[2])
        cps = []
        for j, (px, py) in enumerate(chips):
            cp = _remote(fbuf.at[quarter(k)], fbuf.at[quarter(k)], send_sems, recv_sems, 4 + j, (px, py, c))
            cp.start()
            cps.append(cp)
        for j, (px, py) in enumerate(chips):
            got = fbuf.at[quarter(2 * px + py)]
            _remote(got, got, send_sems, recv_sems, 4 + j, (px, py, c)).wait_recv()
        for cp in cps:
            cp.wait_send()
        o_ref[mine, :] = fbuf[...]
        last = _remote(fbuf, o_ref.at[mine], send_sems, recv_sems, 7, sib)
        last.start()
        last.wait()

    vmem = pl.BlockSpec(memory_space=pltpu.VMEM)
    return pl.pallas_call(
        body, name="small_all_reduce",
        in_specs=[vmem] + [_ANY] * len(after), out_specs=vmem,
        out_shape=jax.ShapeDtypeStruct(s.shape, F32),
        scratch_shapes=[pltpu.VMEM((hr, D_MODEL), F32), pltpu.VMEM((hr, D_MODEL), F32),
                        pltpu.VMEM((3, qr, D_MODEL), F32), pltpu.VMEM((hr, D_MODEL), F32),
                        pltpu.SemaphoreType.DMA((8,)), pltpu.SemaphoreType.DMA((8,))],
        compiler_params=pltpu.CompilerParams(vmem_limit_bytes=VMEM_LIMIT),
    )(s, *after)


_SMALL = ("norm_mix", "w_pool", "pool_scale", "lam_re", "lam_im", "log_dt", "b_re", "b_im", "c_re", "c_im",
          "d_skip", "b_glu", "norm_ffn", "norm_final")
_WEIGHTS = ("norm_mix", "w_in", "w_pool", "pool_scale", "lam_re", "lam_im", "log_dt", "b_re", "b_im", "c_re",
            "c_im", "d_skip", "w_glu", "b_glu", "w_out", "norm_ffn", "w_gate", "w_up", "w_down", "norm_final")


def _local_step(x, target, p, get_weights, scan_done, get_ffn_weights, ffn_bwd_done, put_grads):
    nl = p["norm_mix"].shape[0]

    def tied(a, token):
        return a if token is None else a + token
    n_rows = nl * N_SSM_GROUPS
    lr = p["lam_re"].reshape(n_rows, 1, SSM_STATE)
    li = p["lam_im"].reshape(n_rows, 1, SSM_STATE)
    ldt = p["log_dt"].reshape(n_rows, 1, 1)
    br_t = p["b_re"].reshape(n_rows, SSM_STATE, SSM_GROUP).transpose(0, 2, 1)
    bi_t = p["b_im"].reshape(n_rows, SSM_STATE, SSM_GROUP).transpose(0, 2, 1)
    ar, ai, bbr_t, bbi_t = _disc_fwd(lr, li, ldt, br_t, bi_t)
    ar = ar.reshape(nl, 1, N_STATE)
    ai = ai.reshape(nl, 1, N_STATE)
    bbr = bbr_t.transpose(0, 2, 1).reshape(nl, N_SSM_GROUPS, SSM_STATE, SSM_GROUP)
    bbi = bbi_t.transpose(0, 2, 1).reshape(nl, N_SSM_GROUPS, SSM_STATE, SSM_GROUP)
    w_pool = p["w_pool"].astype(BF16)
    p = dict(p)
    for n in ("norm_mix", "pool_scale", "b_glu", "norm_ffn"):
        p[n] = p[n].reshape(nl, 1, -1)
    swap = lambda a: jnp.swapaxes(a, -1, -2)
    bpad = jax.vmap(_pad_pairs)(bbr, bbi).astype(BF16)
    cpad_t = jax.vmap(_pad_pairs)(swap(p["c_re"]), -swap(p["c_im"])).astype(BF16)
    bpad_t, cpad = swap(bpad), swap(cpad_t)
    dskip = p["d_skip"].reshape(nl, 1, D_SSM)

    layers = []
    h = x
    for l in range(nl):
        wp = get_weights(l, [h] if l else [h, bpad, cpad, bpad_t, cpad_t, ar, ai])
        u, ypool = _mix_in_fwd(h, p["norm_mix"], wp, l, w_pool, p["pool_scale"])
        sre, sim, yraw = _ssm_fwd(u, l, bpad, cpad, ar, ai, dskip)
        wp = scan_done(l, wp, [yraw])
        hm = _mix_out_fwd(yraw, ypool, h, wp, l, p["b_glu"])
        wp = get_ffn_weights(l, wp, [hm])
        h_next, n2, act_s, fgate_s, fup_s = _ffn_fwd(hm, p["norm_ffn"], wp, l)
        layers.append(dict(h=h, u=u, ypool=ypool, sre=sre, sim=sim, yraw=yraw, hm=hm, n2=n2, act_s=act_s, wp=wp,
                           fgate_s=fgate_s, fup_s=fup_s))
        h = h_next

    dh, loss, d_norm_final = _final_fwd_bwd(h, p["norm_final"].reshape(1, D_MODEL), target)

    raw = {n: [None] * nl for n in ("dg1", "dwp", "dsc", "dcp", "dbp", "ddsk", "db_glu", "dg2", "dar", "dai")}
    token = None
    for l in reversed(range(nl)):
        s = layers[l]
        wp = s["wp"]
        g1 = lax.empty((1, N_SHARD, P_ROWS, D_MODEL), F32)
        dhm, dg2, dgate_s, dup_s, dhb = _ffn_bwd_act(dh, s["hm"], tied(p["norm_ffn"], token), s["fgate_s"],
                                                      s["fup_s"], wp, l)
        g1 = _ffn_bwd_w(s["n2"], dgate_s, dup_s, s["act_s"], dhb, g1)
        token = ffn_bwd_done(l, [g1])
        dyraw, dyp, db_glu, g1 = _mix_out_bwd(dhm, s["yraw"], s["ypool"], wp, l, tied(p["b_glu"], token), g1)
        dus, dcp, dbp, dar, dai, ddsk = _ssm_bwd(dyraw, s["u"], s["sre"], s["sim"], l, cpad_t, bpad_t, ar, ai, dskip)
        dup, dwp, dsc = _pool_bwd(dyp, s["u"], l, w_pool, p["pool_scale"])
        dh, dg1, g1 = _mix_in_bwd(dup, dus, s["h"], dhm, p["norm_mix"], wp, l, g1)
        token = put_grads(l, g1)
        for n, a in (("dg1", dg1), ("dwp", dwp), ("dsc", dsc), ("dcp", dcp), ("dbp", dbp), ("ddsk", ddsk),
                     ("db_glu", db_glu), ("dg2", dg2), ("dar", dar), ("dai", dai)):
            raw[n][l] = a

    st = {n: jnp.stack(v) for n, v in raw.items()}
    dc_re, dc_im = jax.vmap(_unpad_pairs)(swap(st["dcp"]))
    dbbr, dbbi = jax.vmap(_unpad_pairs)(st["dbp"])
    rows = lambda a: a.reshape((n_rows,) + a.shape[2:])
    dlr, dli, dldt, dbr_t, dbi_t = _disc_bwd(lr, li, ldt, br_t, bi_t, st["dar"].reshape(n_rows, 1, SSM_STATE),
                                              st["dai"].reshape(n_rows, 1, SSM_STATE), rows(swap(dbbr)),
                                              rows(swap(dbbi)))
    small = {"norm_mix": st["dg1"][:, 0], "w_pool": st["dwp"], "pool_scale": st["dsc"][:, 0], "c_re": swap(dc_re),
             "c_im": -swap(dc_im), "d_skip": st["ddsk"].reshape(nl, N_SSM_GROUPS, SSM_GROUP),
             "b_glu": st["db_glu"][:, 0], "norm_ffn": st["dg2"][:, 0]}
    small["lam_re"] = dlr.reshape(nl, N_SSM_GROUPS, SSM_STATE)
    small["lam_im"] = dli.reshape(nl, N_SSM_GROUPS, SSM_STATE)
    small["log_dt"] = dldt.reshape(nl, N_SSM_GROUPS)
    small["b_re"] = dbr_t.reshape(nl, N_SSM_GROUPS, SSM_GROUP, SSM_STATE)
    small["b_im"] = dbi_t.reshape(nl, N_SSM_GROUPS, SSM_GROUP, SSM_STATE)
    small["d_skip"] = small["d_skip"].transpose(_SMALL_VIEW["d_skip"])
    small["norm_final"] = d_norm_final
    return loss, dh, small


_SMALL_VIEW = {"b_re": (0, 1, 3, 2), "b_im": (0, 1, 3, 2), "d_skip": (0, 2, 1)}
_SMALL_GROUPS = (("b_re", "b_im"), ("c_re", "c_im"), ("lam_re", "lam_im"), ("norm_mix", "norm_ffn"),
                 ("pool_scale", "b_glu"), ("w_pool",), ("log_dt",), ("d_skip",), ("norm_final",))


def _view(n, a):
    a = a.transpose(_SMALL_VIEW[n]) if n in _SMALL_VIEW else a
    return a[None] if a.ndim == 1 else a


def _unview(n, a, shape):
    a = a.reshape(shape) if len(shape) == 1 else a
    return a.transpose(_SMALL_VIEW[n]) if n in _SMALL_VIEW else a


def _flatten_small(views):
    flat = jnp.concatenate([views[n].reshape(-1) for n in _SMALL])
    n_rows = -(-flat.shape[0] // (64 * D_MODEL)) * 64
    return jnp.pad(flat, (0, n_rows * D_MODEL - flat.shape[0])).reshape(n_rows, D_MODEL)


def _split_small(flat, like):
    flat = flat.reshape(-1)
    out, at = {}, 0
    for n in _SMALL:
        size = like[n].size
        out[n] = flat[at:at + size].reshape(like[n].shape)
        at += size
    return out


def _adamw_small(name, ws, ms, vs, gs):
    k = len(ws)

    def body(*refs):
        ins, outs = refs[:4 * k], refs[4 * k:]
        for i in range(k):
            w, m, v, g = (ins[j * k + i][...] for j in range(4))
            delta, mn, vn = _adamw_math(w, g, m, v)
            outs[i][...] = delta
            outs[k + i][...] = mn
            outs[2 * k + i][...] = vn

    shapes = [jax.ShapeDtypeStruct(w.shape, F32) for w in ws] * 3
    outs = pl.pallas_call(body, name=name, out_shape=shapes,
                          compiler_params=pltpu.CompilerParams(vmem_limit_bytes=VMEM_LIMIT))(*ws, *ms, *vs, *gs)
    return outs[:k], outs[k:2 * k], outs[2 * k:]


def kernel(x, norm_mix, w_in, w_pool, pool_scale, lam_re, lam_im, log_dt, b_re, b_im, c_re, c_im, d_skip, w_glu, b_glu, w_out, norm_ffn, w_gate, w_up, w_down, norm_final, loss_target, m_norm_mix, m_w_in, m_w_pool, m_pool_scale, m_lam_re, m_lam_im, m_log_dt, m_b_re, m_b_im, m_c_re, m_c_im, m_d_skip, m_w_glu, m_b_glu, m_w_out, m_norm_ffn, m_w_gate, m_w_up, m_w_down, m_norm_final, v_norm_mix, v_w_in, v_w_pool, v_pool_scale, v_lam_re, v_lam_im, v_log_dt, v_b_re, v_b_im, v_c_re, v_c_im, v_d_skip, v_w_glu, v_b_glu, v_w_out, v_norm_ffn, v_w_gate, v_w_up, v_w_down, v_norm_final):
    given = dict(locals())
    w = {n: given[n] for n in _WEIGHTS}
    m = {n: given["m_" + n] for n in _WEIGHTS}
    v = {n: given["v_" + n] for n in _WEIGHTS}
    ids = jnp.stack([lax.axis_index("c"), 2 * lax.axis_index("x") + lax.axis_index("y")]).astype(jnp.int32)

    t_names = ("w_gate", "w_up")
    tr = lambda a: a.transpose(0, 2, 1)
    for d in (w, m, v):
        d.update({n: tr(d[n]) for n in t_names})

    nl = norm_mix.shape[0]
    mixer_rows, ffn_rows = (P_FF_ROWS, P_ROWS - P_FF_ROWS), (0, P_FF_ROWS)
    started, last = {}, None
    for l in range(nl):
        packed = _pack_weights(ids, l, w["w_in"], w["w_glu"], w["w_out"], w["w_down"], w["w_gate"], w["w_up"],
                               [] if last is None else [last])
        if l == 0:
            first = _ag_start("ag_start_0_mixer", packed, ids, [mixer_rows])
            started[0] = _ag_start("ag_start_0_ffn", first[2], first[3], [ffn_rows])
        else:
            started[l] = _ag_start(f"ag_start_{l}", packed, last, [mixer_rows, ffn_rows])
        last = started[l][3]
    views = [{n: _view(n, d[n]) for n in _SMALL} for d in (w, m, v)]

    passing = {}

    def get_weights(l, after):
        send_sems, recv_sems, buf, _ = started[l]
        if l == 0:
            buf = _ag_wait("ag_wait_0_mixer", first[0], first[1], buf, after + [last], [mixer_rows])
            return _ag_forward(buf, mixer_rows)
        buf = _ag_wait(f"ag_wait_{l}", send_sems, recv_sems, buf, after, [mixer_rows, ffn_rows])
        buf = _ag_forward(buf, mixer_rows)
        passing[l] = _ag_forward_start(f"ag_forward_start_{l}", buf, ffn_rows)
        return passing[l][2]

    def scan_done(l, buf, after):
        if l > 0:
            return buf
        send_sems, recv_sems, _, _ = started[0]
        buf = _ag_wait("ag_wait_0_ffn", send_sems, recv_sems, buf, after, [ffn_rows])
        passing[0] = _ag_forward_start("ag_forward_start_0", buf, ffn_rows)
        return passing[0][2]

    def get_ffn_weights(l, buf, after):
        send_sems, recv_sems, _ = passing[l]
        return _ag_forward_wait(f"ag_forward_wait_{l}", send_sems, recv_sems, buf, after, ffn_rows)

    to_sibling, to_chips, reduced = {}, {}, {}

    def put_grads(l, g):
        to_sibling[l] = _rs_sibling_start(f"rs_sibling_start_{l}", g)
        token = to_sibling[l][4]
        if l + 1 in to_chips:
            finish(l + 1, [token])
        return token[:1, :1]

    def ffn_bwd_done(l, after):
        return send_to_chips(l + 1, after)[:1, :1] if l + 1 in to_sibling else None

    def send_to_chips(l, after):
        send_sems, recv_sems, g, land, _ = to_sibling.pop(l)
        g, land = _rs_sibling_wait(f"rs_sibling_wait_{l}", send_sems, recv_sems, g, land, after)
        own, t = _rs_add("rs_add", ids, g, land, RS_ROW_TILE)
        send_sems, recv_sems, t, land, token = _rs_chips_start(f"rs_chips_start_{l}", t)
        to_chips[l] = (send_sems, recv_sems, t, land, own)
        return token

    def finish(l, after):
        send_sems, recv_sems, t, land, own = to_chips.pop(l)
        land = _rs_chips_wait(f"rs_chips_wait_{l}", send_sems, recv_sems, t, land, after)
        shard = lax.empty((1, P_ROWS, D_MODEL), F32)
        reduced[l] = _rs_exchange_start(f"rs_exchange_start_{l}", _rs_sum(ids, 0, own, land, shard, RS_ROW_TILE))

    loss, grad_x, small = _local_step(x[0], loss_target[0], {n: w[n] for n in _SMALL}, get_weights, scan_done,
                                      get_ffn_weights, ffn_bwd_done, put_grads)
    loss = lax.psum(loss[0, 0], ("x", "y", "c"))
    small_flat = _flatten_small(small)

    groups = ((("w_in", P_IN_BLK), ("w_out", P_OUT_BLK)), (("w_down", P_WD_BLK), ("w_gate", P_WG_BLK), ("w_up", P_WU_BLK)))
    res = {n: None for n in ("w_in", "w_out", "w_down", "w_gate", "w_up", "w_glu")}

    def adamw_layer(l, after):
        send_sems, recv_sems, shard = reduced[l]
        shard = _rs_exchange_wait(f"rs_exchange_wait_{l}", send_sems, recv_sems, shard, after)
        for group, row_tile in zip(groups, (128, 176)):
            names = [n for n, _ in group]
            outs = None if res[names[0]] is None else [res[n] for n in names]
            outs = _adamw_group("adamw_" + names[0], l, *[[d[n] for n in names] for d in (w, m, v)], shard,
                                [blk * idx for _, (blk, idx) in group], row_tile, outs)
            res.update(zip(names, outs))
        blk, idx = P_GLU_BLK
        res["w_glu"] = _adamw("adamw_w_glu", l, w["w_glu"], m["w_glu"], v["w_glu"], shard, (blk, D_MODEL), blk * idx,
                              128, res["w_glu"], (), True)

    if nl > 1:
        adamw_layer(nl - 1, [to_sibling[0][4]])
    token = send_to_chips(0, [small_flat] + [r[0] for r in res.values() if r is not None])
    for l in reversed(range(1, nl - 1)):
        adamw_layer(l, [token])
    updated = [r[0] for r in res.values() if r is not None]
    small_sum = _small_all_reduce(small_flat, [token] + updated)
    finish(0, [small_sum] + updated)
    adamw_layer(0, [])
    for n in t_names:
        res[n] = tuple(tr(a) for a in res[n])
    g_views = _split_small(small_sum, views[0])
    for group in _SMALL_GROUPS:
        deltas, new_ms, new_vs = _adamw_small("adamw_" + group[0], *[[d[n] for n in group] for d in views],
                                              [g_views[n] for n in group])
        for i, n in enumerate(group):
            res[n] = tuple(_unview(n, a, w[n].shape) for a in (g_views[n], deltas[i], new_ms[i], new_vs[i]))

    return (loss, grad_x[None], *[res[n][0] for n in _WEIGHTS], *[res[n][1] for n in _WEIGHTS],
            *[res[n][2] for n in _WEIGHTS], *[res[n][3] for n in _WEIGHTS])
```

```python
import functools
import math

import jax
import jax.numpy as jnp
from jax import lax
from jax.experimental import pallas as pl
from jax.experimental.pallas import tpu as pltpu

F32 = jnp.float32
BF16 = jnp.bfloat16

D_MODEL = 1024
D_POOL = 512
D_SSM = 512
POOL_WINDOWS = (2, 4, 8, 16)
POOL_GROUP = 128
POOL_HALO = 16
N_SSM_GROUPS = 32
SSM_GROUP = 16
SSM_STATE = 64
N_STATE = N_SSM_GROUPS * SSM_STATE
N_PAIRS = N_SSM_GROUPS // 2
D_FF = 2816
N_SHARD = 4
FF_SHARD = D_FF // N_SHARD
RMS_EPS = 1e-6

ADAM_LR = 0.001
ADAM_B1 = 0.9
ADAM_B2 = 0.999
ADAM_EPS = 1e-08
ADAM_WD = 0.01
ADAM_STEP = 10

P_ROWS = 2816
P_WD_BLK = (704, 0)
P_WG_BLK = (704, 1)
P_WU_BLK = (704, 2)
P_FF_ROWS = 2112
P_GLU_BLK = (64, 33)
P_GLU_PAD = 192
P_IN_BLK = (256, 9)
P_OUT_BLK = (256, 10)

SUBLANES = 8
VMEM_LIMIT = 56 * 1024 * 1024

TM = 1024
TM_FFN = 512
TM_FFN_LONG = 1024
FFN_SPLIT = 2
TS = 2048
SCAN_LANES = 512


def _cparams(n_axes):
    return pltpu.CompilerParams(dimension_semantics=("arbitrary",) * n_axes, vmem_limit_bytes=VMEM_LIMIT)


def _dot(a, b):
    return jnp.dot(a, b, preferred_element_type=F32)


def _dot_nt(a, b):
    return lax.dot_general(a, b, (((1,), (1,)), ((), ())), preferred_element_type=F32)


def _dot_tn(a, b):
    return lax.dot_general(a, b, (((0,), (0,)), ((), ())), preferred_element_type=F32)


def _rms_hat(x):
    r = lax.rsqrt(jnp.mean(x * x, axis=-1, keepdims=True) + RMS_EPS)
    return x * r, r


def _rms_bwd(d_hat, xhat, r):
    return r * (d_hat - xhat * jnp.mean(d_hat * xhat, axis=-1, keepdims=True))


def _sigmoid(x):
    return 1.0 / (1.0 + jnp.exp(-x))


_GELU_C = math.sqrt(2.0 / math.pi)
_GELU_K = 0.044715


def _gelu(x):
    return 0.5 * x * (1.0 + jnp.tanh(_GELU_C * (x + _GELU_K * x * x * x)))


def _gelu_grad(x):
    th = jnp.tanh(_GELU_C * (x + _GELU_K * x * x * x))
    return 0.5 * (1.0 + th) + 0.5 * x * (1.0 - th * th) * _GELU_C * (1.0 + 3.0 * _GELU_K * x * x)


def _glu_weight(ref):
    v = ref[...]
    return jnp.concatenate([v[:, :, :D_SSM], v[:, :, D_SSM:]], axis=1).reshape(D_SSM, D_SSM)


def _glu_pack(w):
    v = w.reshape(N_SHARD, 128, D_SSM)
    return jnp.concatenate([v[:, :64, :], v[:, 64:, :]], axis=2)


def _pool_diff(ext, row0, tm):
    rows = row0 + lax.broadcasted_iota(jnp.int32, (tm, 1), 0)
    outs = []
    for gi, w in enumerate(POOL_WINDOWS):
        e = ext[:, gi * POOL_GROUP:(gi + 1) * POOL_GROUP]
        s = e
        k = 1
        while k < w:
            s = s + pltpu.roll(s, k, 0)
            k *= 2
        inv = 1.0 / jnp.minimum(rows + 1, w).astype(F32)
        outs.append(s[POOL_HALO:, :] * inv - e[POOL_HALO:, :])
    return outs


def _mix_in_fwd(h, g1, wp, layer, w_pool, scale):
    L = h.shape[0]
    tm = min(TM, L)

    def body(h_ref, g_ref, w_ref, wp_ref, sc_ref, u_ref, yp_ref, carry):
        i = pl.program_id(0)

        @pl.when(i == 0)
        def _():
            carry[...] = jnp.zeros_like(carry)

        xhat, _ = _rms_hat(h_ref[...])
        n1 = (xhat * g_ref[...]).astype(BF16)
        u = _dot(n1, w_ref[...].reshape(D_MODEL, D_MODEL))
        u_ref[...] = u
        up = u[:, :D_POOL]
        ext = jnp.concatenate([carry[...], up], axis=0)
        carry[...] = up[tm - POOL_HALO:, :]
        diffs = _pool_diff(ext, i * tm, tm)
        for gi in range(4):
            cols = slice(gi * POOL_GROUP, (gi + 1) * POOL_GROUP)
            yp_ref[:, cols] = _dot(diffs[gi].astype(BF16), wp_ref[gi]) * sc_ref[:, cols]

    blk, idx = P_IN_BLK
    return pl.pallas_call(
        body, name="mix_in_fwd", grid=(L // tm,),
        in_specs=[pl.BlockSpec((tm, D_MODEL), lambda i: (i, 0)),
                  pl.BlockSpec((None, 1, D_MODEL), lambda i: (layer, 0, 0)),
                  pl.BlockSpec((N_SHARD, None, blk, D_MODEL), lambda i: (0, 0, idx, 0)),
                  pl.BlockSpec((None, 4, POOL_GROUP, POOL_GROUP), lambda i: (layer, 0, 0, 0)),
                  pl.BlockSpec((None, 1, D_POOL), lambda i: (layer, 0, 0))],
        out_specs=[pl.BlockSpec((tm, D_MODEL), lambda i: (i, 0)),
                   pl.BlockSpec((tm, D_POOL), lambda i: (i, 0))],
        out_shape=[jax.ShapeDtypeStruct((L, D_MODEL), F32), jax.ShapeDtypeStruct((L, D_POOL), F32)],
        scratch_shapes=[pltpu.VMEM((POOL_HALO, D_POOL), F32)],
        compiler_params=_cparams(1),
    )(h, g1, wp, w_pool, scale)


def _cmul(xr, xi, yr, yi):
    return xr * yr - xi * yi, xr * yi + xi * yr


SCAN_BLOCK = 64
N_SCAN_TABLES = 26


def _permute_rows(src, dst, n_rows):
    for b in range(n_rows // SCAN_BLOCK):
        for tau in range(SUBLANES):
            dst[pl.ds(SCAN_BLOCK * b + SUBLANES * tau, SUBLANES), :] = (
                src[pl.ds(SCAN_BLOCK * b + tau, SUBLANES, stride=SUBLANES), :])


def _scan_tables(ar, ai, tab, reverse):
    c = ar.shape[1]
    row = lax.broadcasted_iota(jnp.int32, (SUBLANES, c), 0)
    zero = jnp.zeros((SUBLANES, c), F32)
    full = lambda v: jnp.broadcast_to(v, (SUBLANES, c))
    pw = [(ar, ai)]
    for _ in range(SUBLANES - 1):
        pw.append(_cmul(*pw[-1], ar, ai))
    a8 = pw[-1]
    a16 = _cmul(*a8, *a8)
    a32 = _cmul(*a16, *a16)
    tab[0] = full(ar)
    tab[1] = full(ai)
    for n, (s, (pr, pi)) in enumerate(((1, a8), (2, a16), (4, a32))):
        keep = (row < SUBLANES - s) if reverse else (row >= s)
        tab[2 + 2 * n] = jnp.where(keep, pr, zero)
        tab[3 + 2 * n] = jnp.where(keep, pi, zero)
    cur = a8
    qr, qi = zero, zero
    for n in range(SUBLANES):
        at = (SUBLANES - 1 - n) if reverse else n
        qr = jnp.where(row == at, cur[0], qr)
        qi = jnp.where(row == at, cur[1], qi)
        cur = _cmul(*cur, *a8)
    tab[8] = qr
    tab[9] = qi
    for tau in range(SUBLANES):
        pr, pi = pw[SUBLANES - 1 - tau] if reverse else pw[tau]
        tab[10 + 2 * tau] = full(pr)
        tab[11 + 2 * tau] = full(pi)


def _cmac(xr, xi, ar, ai, yr, yi):
    return xr + ar * yr - ai * yi, xi + ar * yi + ai * yr


def _chain_segments(er, ei, c_r, c_i, tab, cols, reverse):
    tr, ti = er, ei
    for n, s in enumerate((1, 2, 4)):
        shift = SUBLANES - s if reverse else s
        tr, ti = _cmac(tr, ti, tab[2 + 2 * n, :, cols], tab[3 + 2 * n, :, cols],
                       pltpu.roll(tr, shift, 0), pltpu.roll(ti, shift, 0))
    return _cmac(tr, ti, tab[8, :, cols], tab[9, :, cols], c_r, c_i)


def _ssm_fwd(u, layer, bpad, cpad, ar, ai, dskip):
    L = u.shape[0]
    ts = min(TS, L)
    nq = 4
    cq = N_STATE // nq

    def body(u_ref, bp_ref, cp_ref, ar_ref, ai_ref, dsk_ref, sre_ref, sim_ref, y_ref, cr, ci, tab, up, yp):
        t = pl.program_id(1)

        @pl.when(t == 0)
        def _():
            cr[...] = jnp.zeros_like(cr)
            ci[...] = jnp.zeros_like(ci)
            _scan_tables(ar_ref[...], ai_ref[...], tab, reverse=False)

        _permute_rows(u_ref, up, ts)
        uf = up[...]
        ub = uf.astype(BF16)
        for jj in range(4):
            bu = _dot(ub, bp_ref[jj])
            sre_ref[:, jj * 128:(jj + 1) * 128] = bu[:, :128]
            sim_ref[:, jj * 128:(jj + 1) * 128] = bu[:, 128:]

        shp = (SUBLANES, SCAN_LANES)
        first_row = lax.broadcasted_iota(jnp.int32, shp, 0) == 0
        for cc in range(cq // SCAN_LANES):
            cols = slice(cc * SCAN_LANES, (cc + 1) * SCAN_LANES)

            def block(b, carry, cols=cols):
                c_r, c_i = carry
                base = pl.multiple_of(b * SCAN_BLOCK, SCAN_BLOCK)
                rows = lambda tau: pl.ds(base + SUBLANES * tau, SUBLANES)
                a_r, a_i = tab[0, :, cols], tab[1, :, cols]
                ys = [(sre_ref[rows(0), cols], sim_ref[rows(0), cols])]
                for tau in range(1, SUBLANES):
                    ys.append(_cmac(sre_ref[rows(tau), cols], sim_ref[rows(tau), cols], a_r, a_i, *ys[-1]))
                tr, ti = _chain_segments(*ys[-1], c_r, c_i, tab, cols, reverse=False)
                in_r = jnp.where(first_row, c_r, pltpu.roll(tr, 1, 0))
                in_i = jnp.where(first_row, c_i, pltpu.roll(ti, 1, 0))
                for tau in range(SUBLANES):
                    sr, si = _cmac(*ys[tau], tab[10 + 2 * tau, :, cols], tab[11 + 2 * tau, :, cols], in_r, in_i)
                    sre_ref[rows(tau), cols] = sr
                    sim_ref[rows(tau), cols] = si
                return (jnp.broadcast_to(tr[SUBLANES - 1:, :], shp), jnp.broadcast_to(ti[SUBLANES - 1:, :], shp))

            c_r, c_i = lax.fori_loop(0, ts // SCAN_BLOCK, block, (cr[:, cols], ci[:, cols]), unroll=2)
            cr[:, cols] = c_r
            ci[:, cols] = c_i

        acc = dsk_ref[...] * uf
        for jj in range(4):
            cols = slice(jj * 128, (jj + 1) * 128)
            scat = jnp.concatenate([sre_ref[:, cols], sim_ref[:, cols]], axis=1).astype(BF16)
            acc = acc + _dot(scat, cp_ref[jj])
        yp[...] = acc
        _permute_rows(yp, y_ref, ts)

    return pl.pallas_call(
        body, name="ssm_fwd", grid=(nq, L // ts),
        in_specs=[pl.BlockSpec((ts, 128), lambda q, t: (t, 4 + q)),
                  pl.BlockSpec((None, 4, 128, 256), lambda q, t: (layer, q, 0, 0)),
                  pl.BlockSpec((None, 4, 256, 128), lambda q, t: (layer, q, 0, 0)),
                  pl.BlockSpec((None, 1, cq), lambda q, t: (layer, 0, q)),
                  pl.BlockSpec((None, 1, cq), lambda q, t: (layer, 0, q)),
                  pl.BlockSpec((None, 1, 128), lambda q, t: (layer, 0, q))],
        out_specs=[pl.BlockSpec((ts, cq), lambda q, t: (t, q)),
                   pl.BlockSpec((ts, cq), lambda q, t: (t, q)),
                   pl.BlockSpec((ts, 128), lambda q, t: (t, q))],
        out_shape=[jax.ShapeDtypeStruct((L, N_STATE), F32), jax.ShapeDtypeStruct((L, N_STATE), F32),
                   jax.ShapeDtypeStruct((L, D_SSM), F32)],
        scratch_shapes=[pltpu.VMEM((SUBLANES, cq), F32), pltpu.VMEM((SUBLANES, cq), F32),
                        pltpu.VMEM((N_SCAN_TABLES, SUBLANES, cq), F32),
                        pltpu.VMEM((ts, 128), F32), pltpu.VMEM((ts, 128), F32)],
        compiler_params=_cparams(2),
    )(u, bpad, cpad, ar, ai, dskip)


def _mix_out_fwd(yraw, ypool, h, wp, layer, b_glu):
    L = h.shape[0]
    tm = min(TM, L)

    def body(yr_ref, yp_ref, h_ref, wglu_ref, b_ref, wout_ref, o_ref):
        y = _gelu(yr_ref[...])
        z = _dot(y.astype(BF16), _glu_weight(wglu_ref)) + b_ref[...]
        o = y * _sigmoid(z)
        mix = jnp.concatenate([yp_ref[...], o], axis=1).astype(BF16)
        o_ref[...] = h_ref[...] + _dot(mix, wout_ref[...].reshape(D_MODEL, D_MODEL))

    gb, gi = P_GLU_BLK
    ob, oi = P_OUT_BLK
    return pl.pallas_call(
        body, name="mix_out_fwd", grid=(L // tm,),
        in_specs=[pl.BlockSpec((tm, D_SSM), lambda i: (i, 0)),
                  pl.BlockSpec((tm, D_POOL), lambda i: (i, 0)),
                  pl.BlockSpec((tm, D_MODEL), lambda i: (i, 0)),
                  pl.BlockSpec((N_SHARD, None, gb, D_MODEL), lambda i: (0, 0, gi, 0)),
                  pl.BlockSpec((None, 1, D_SSM), lambda i: (layer, 0, 0)),
                  pl.BlockSpec((N_SHARD, None, ob, D_MODEL), lambda i: (0, 0, oi, 0))],
        out_specs=pl.BlockSpec((tm, D_MODEL), lambda i: (i, 0)),
        out_shape=jax.ShapeDtypeStruct((L, D_MODEL), F32),
        compiler_params=_cparams(1),
    )(yraw, ypool, h, wp, b_glu, wp)


def _ffn_weights(ref, k):
    return ref[k, 0:FF_SHARD, :], ref[k, FF_SHARD:2 * FF_SHARD, :], ref[k, 2 * FF_SHARD:P_FF_ROWS, :]


def _ffn_weight_spec():
    return pl.BlockSpec((N_SHARD, None, P_FF_ROWS, D_MODEL), lambda m, k: (0, 0, 0, 0),
                        pipeline_mode=pl.Buffered(1))


def _ffn_fwd(h, g2, wp, layer):
    L = h.shape[0]
    tm = min(TM_FFN_LONG, L)

    def body(h_ref, g_ref, w_ref, o_ref, n2_ref, act_ref, dgate_ref, dup_ref):
        k = pl.program_id(1)

        @pl.when(k == 0)
        def _():
            x = h_ref[...]
            xhat, _ = _rms_hat(x)
            n2_ref[...] = (xhat * g_ref[...]).astype(BF16)
            o_ref[...] = x

        wd, wg_t, wu_t = _ffn_weights(w_ref, k)
        n2 = n2_ref[...]
        gate = _dot_nt(n2, wg_t)
        up = _dot_nt(n2, wu_t)
        sg = _sigmoid(gate)
        silu = gate * sg
        act = (silu * up).astype(BF16)
        act_ref[...] = act
        dgate_ref[...] = (up * (sg * (1.0 + gate * (1.0 - sg)))).astype(BF16)
        dup_ref[...] = silu.astype(BF16)
        o_ref[...] += _dot(act, wd)

    act_shape = jax.ShapeDtypeStruct((N_SHARD, L, FF_SHARD), BF16)
    return pl.pallas_call(
        body, name="ffn_fwd", grid=(L // tm, N_SHARD),
        in_specs=[pl.BlockSpec((tm, D_MODEL), lambda m, k: (m, 0)),
                  pl.BlockSpec((None, 1, D_MODEL), lambda m, k: (layer, 0, 0)),
                  _ffn_weight_spec()],
        out_specs=[pl.BlockSpec((tm, D_MODEL), lambda m, k: (m, 0)),
                   pl.BlockSpec((tm, D_MODEL), lambda m, k: (m, 0)),
                   pl.BlockSpec((None, tm, FF_SHARD), lambda m, k: (k, m, 0)),
                   pl.BlockSpec((None, tm, FF_SHARD), lambda m, k: (k, m, 0)),
                   pl.BlockSpec((None, tm, FF_SHARD), lambda m, k: (k, m, 0))],
        out_shape=[jax.ShapeDtypeStruct((L, D_MODEL), F32), jax.ShapeDtypeStruct((L, D_MODEL), BF16),
                   act_shape, act_shape, act_shape],
        compiler_params=_cparams(2),
    )(h, g2, wp)


def _final_fwd_bwd(h, gf, target):
    L = h.shape[0]
    tm = min(TM, L)

    def body(h_ref, g_ref, t_ref, dh_ref, loss_ref, dg_ref):
        i = pl.program_id(0)

        @pl.when(i == 0)
        def _():
            loss_ref[...] = jnp.zeros_like(loss_ref)
            dg_ref[...] = jnp.zeros_like(dg_ref)

        xhat, r = _rms_hat(h_ref[...])
        g = g_ref[...]
        e = xhat * g - t_ref[...]
        loss_ref[...] += 0.5 * jnp.sum(jnp.mean(e * e, axis=-1, keepdims=True), axis=0, keepdims=True)
        dy = e * (1.0 / D_MODEL)
        dg_ref[...] += jnp.sum(dy * xhat, axis=0, keepdims=True)
        dh_ref[...] = _rms_bwd(dy * g, xhat, r)

    return pl.pallas_call(
        body, name="final_fwd_bwd", grid=(L // tm,),
        in_specs=[pl.BlockSpec((tm, D_MODEL), lambda i: (i, 0)),
                  pl.BlockSpec((1, D_MODEL), lambda i: (0, 0)),
                  pl.BlockSpec((tm, D_MODEL), lambda i: (i, 0))],
        out_specs=[pl.BlockSpec((tm, D_MODEL), lambda i: (i, 0)),
                   pl.BlockSpec((1, 1), lambda i: (0, 0)),
                   pl.BlockSpec((1, D_MODEL), lambda i: (0, 0))],
        out_shape=[jax.ShapeDtypeStruct((L, D_MODEL), F32), jax.ShapeDtypeStruct((1, 1), F32),
                   jax.ShapeDtypeStruct((1, D_MODEL), F32)],
        compiler_params=_cparams(1),
    )(h, gf, target)


def _ffn_bwd_act(dh, h, g2, fgate_s, fup_s, wp, layer):
    L = h.shape[0]
    tm = min(TM_FFN, L)
    sub = tm // FFN_SPLIT

    def body(dh_ref, h_ref, g_ref, fgate_ref, fup_ref, w_ref,
             dhm_ref, dg_ref, dgate_ref, dup_ref, dhb_ref):
        m, k = pl.program_id(0), pl.program_id(1)
        dn2 = dhm_ref

        @pl.when(jnp.logical_and(m == 0, k == 0))
        def _():
            dg_ref[...] = jnp.zeros_like(dg_ref)

        @pl.when(k == 0)
        def _():
            dhb_ref[...] = dh_ref[...].astype(BF16)
            dn2[...] = jnp.zeros_like(dn2)

        wd, wg_t, wu_t = _ffn_weights(w_ref, k)
        for rows in (slice(r * sub, (r + 1) * sub) for r in range(tm // sub)):
            dact = _dot_nt(dhb_ref[rows, :], wd)
            dgate = (dact * fgate_ref[rows, :].astype(F32)).astype(BF16)
            dup = (dact * fup_ref[rows, :].astype(F32)).astype(BF16)
            dgate_ref[rows, :] = dgate
            dup_ref[rows, :] = dup
            dn2[rows, :] += _dot(dgate, wg_t) + _dot(dup, wu_t)

        @pl.when(k == N_SHARD - 1)
        def _():
            xhat, r = _rms_hat(h_ref[...])
            d = dn2[...]
            dg_ref[...] += jnp.sum(d * xhat, axis=0, keepdims=True)
            dhm_ref[...] = dh_ref[...] + _rms_bwd(d * g_ref[...], xhat, r)

    act_spec = pl.BlockSpec((None, tm, FF_SHARD), lambda m, k: (k, m, 0))
    act_shape = jax.ShapeDtypeStruct((N_SHARD, L, FF_SHARD), BF16)
    row_spec = pl.BlockSpec((tm, D_MODEL), lambda m, k: (m, 0))
    return pl.pallas_call(
        body, name="ffn_bwd_act", grid=(L // tm, N_SHARD),
        in_specs=[row_spec, row_spec,
                  pl.BlockSpec((None, 1, D_MODEL), lambda m, k: (layer, 0, 0)),
                  act_spec, act_spec,
                  _ffn_weight_spec()],
        out_specs=[row_spec,
                   pl.BlockSpec((1, D_MODEL), lambda m, k: (0, 0)),
                   act_spec, act_spec, row_spec],
        out_shape=[jax.ShapeDtypeStruct((L, D_MODEL), F32), jax.ShapeDtypeStruct((1, D_MODEL), F32),
                   act_shape, act_shape, jax.ShapeDtypeStruct((L, D_MODEL), BF16)],
        compiler_params=_cparams(2),
    )(dh, h, g2, fgate_s, fup_s, wp)


def _ffn_bwd_w(n2, dgate_s, dup_s, act_s, dhb, gbuf):
    L = n2.shape[0]
    tm = min(TM_FFN_LONG, L)

    def body(n2_ref, dgate_ref, dup_ref, act_ref, dhb_ref, g_in, g_ref):
        m = pl.program_id(1)

        @pl.when(m == 0)
        def _():
            g_ref[...] = jnp.zeros_like(g_ref)

        n2v = n2_ref[...]
        g_ref[0:FF_SHARD, :] += _dot_tn(act_ref[...], dhb_ref[...])
        g_ref[FF_SHARD:2 * FF_SHARD, :] += _dot_tn(dgate_ref[...], n2v)
        g_ref[2 * FF_SHARD:P_FF_ROWS, :] += _dot_tn(dup_ref[...], n2v)

    act_spec = pl.BlockSpec((None, tm, FF_SHARD), lambda k, m: (k, m, 0))
    row_spec = pl.BlockSpec((tm, D_MODEL), lambda k, m: (m, 0))
    return pl.pallas_call(
        body, name="ffn_bwd_w", grid=(N_SHARD, L // tm),
        in_specs=[row_spec, act_spec, act_spec, act_spec, row_spec, pl.BlockSpec(memory_space=pl.ANY)],
        out_specs=pl.BlockSpec((None, None, P_FF_ROWS, D_MODEL), lambda k, m: (0, k, 0, 0)),
        out_shape=jax.ShapeDtypeStruct(gbuf.shape, F32),
        input_output_aliases={5: 0},
        compiler_params=_cparams(2),
    )(n2, dgate_s, dup_s, act_s, dhb, gbuf)


def _mix_out_bwd(dhm, yraw, ypool, wp, layer, b_glu, gbuf):
    L = dhm.shape[0]
    tm = min(TM, L)

    def body(dhm_ref, yr_ref, yp_ref, wglu_ref, b_ref, wout_ref, g1_in,
             dyr_ref, dyp_ref, db_ref, g1_ref, dwout, dwglu, gpack):
        i = pl.program_id(0)

        @pl.when(i == 0)
        def _():
            db_ref[...] = jnp.zeros_like(db_ref)
            dwout[...] = jnp.zeros_like(dwout)
            dwglu[...] = jnp.zeros_like(dwglu)

        dhb = dhm_ref[...].astype(BF16)
        wglu = _glu_weight(wglu_ref)
        dmix = _dot_nt(dhb, wout_ref[...].reshape(D_MODEL, D_MODEL))
        dyp_ref[...] = dmix[:, :D_POOL]
        d_o = dmix[:, D_POOL:]
        yraw_v = yr_ref[...]
        y = _gelu(yraw_v)
        yb = y.astype(BF16)
        sig = _sigmoid(_dot(yb, wglu) + b_ref[...])
        mix = jnp.concatenate([yp_ref[...], y * sig], axis=1).astype(BF16)
        dwout[...] += _dot_tn(mix, dhb).reshape(N_SHARD, 256, D_MODEL)
        dz = d_o * y * sig * (1.0 - sig)
        dzb = dz.astype(BF16)
        db_ref[...] += jnp.sum(dz, axis=0, keepdims=True)
        dwglu[...] += _dot_tn(yb, dzb)
        dy = d_o * sig + _dot_nt(dzb, wglu)
        dyr_ref[...] = dy * _gelu_grad(yraw_v)

        @pl.when(i == n_steps - 1)
        def _():
            gpack[:, :gb, :] = _glu_pack(dwglu[...])
            gpack[:, gb:, :] = jnp.zeros((N_SHARD, P_GLU_PAD - gb, D_MODEL), F32)
            pltpu.sync_copy(gpack, g1_ref.at[0, :, pl.ds(gb * gi, P_GLU_PAD), :])
            pltpu.sync_copy(dwout, g1_ref.at[0, :, pl.ds(ob * oi, ob), :])

    gb, gi = P_GLU_BLK
    ob, oi = P_OUT_BLK
    n_steps = L // tm
    return pl.pallas_call(
        body, name="mix_out_bwd", grid=(n_steps,),
        in_specs=[pl.BlockSpec((tm, D_MODEL), lambda i: (i, 0)),
                  pl.BlockSpec((tm, D_SSM), lambda i: (i, 0)),
                  pl.BlockSpec((tm, D_POOL), lambda i: (i, 0)),
                  pl.BlockSpec((N_SHARD, None, gb, D_MODEL), lambda i: (0, 0, gi, 0)),
                  pl.BlockSpec((None, 1, D_SSM), lambda i: (layer, 0, 0)),
                  pl.BlockSpec((N_SHARD, None, ob, D_MODEL), lambda i: (0, 0, oi, 0)),
                  pl.BlockSpec(memory_space=pl.ANY)],
        out_specs=[pl.BlockSpec((tm, D_SSM), lambda i: (i, 0)),
                   pl.BlockSpec((tm, D_POOL), lambda i: (i, 0)),
                   pl.BlockSpec((1, D_SSM), lambda i: (0, 0)),
                   pl.BlockSpec(memory_space=pl.ANY)],
        out_shape=[jax.ShapeDtypeStruct((L, D_SSM), F32), jax.ShapeDtypeStruct((L, D_POOL), F32),
                   jax.ShapeDtypeStruct((1, D_SSM), F32),
                   jax.ShapeDtypeStruct(gbuf.shape, F32)],
        scratch_shapes=[pltpu.VMEM((N_SHARD, ob, D_MODEL), F32), pltpu.VMEM((D_SSM, D_SSM), F32),
                        pltpu.VMEM((N_SHARD, P_GLU_PAD, D_MODEL), F32)],
        input_output_aliases={6: 3},
        compiler_params=_cparams(1),
    )(dhm, yraw, ypool, wp, b_glu, wp, gbuf)


def _ssm_bwd(dyraw, u, sre, sim, layer, cpad_t, bpad_t, ar, ai, dskip):
    L = u.shape[0]
    ts = min(TS, L)
    nt = L // ts
    nq = 4
    cq = N_STATE // nq

    def body(dy_ref, u_ref, sre_ref, sim_ref, ct_ref, bt_ref, ar_ref, ai_ref, dsk_ref,
             du_ref, dcp_ref, dbp_ref, dar_ref, dai_ref, ddsk_ref, gre, gim, cr, ci, tab, accr, acci, up, dyp):
        t = pl.program_id(1)

        @pl.when(t == 0)
        def _():
            for ref in (cr, ci, accr, acci, dcp_ref, dbp_ref, ddsk_ref):
                ref[...] = jnp.zeros_like(ref)
            _scan_tables(ar_ref[...], -ai_ref[...], tab, reverse=True)

        _permute_rows(dy_ref, dyp, ts)
        _permute_rows(u_ref, up, ts)
        dy = dyp[...]
        dyb = dy.astype(BF16)
        uf = up[...]
        ub = uf.astype(BF16)
        for jj in range(4):
            cols = slice(jj * 128, (jj + 1) * 128)
            ds = _dot(dyb, ct_ref[jj])
            gre[:, cols] = ds[:, :128]
            gim[:, cols] = ds[:, 128:]
            scat = jnp.concatenate([sre_ref[:, cols], sim_ref[:, cols]], axis=1).astype(BF16)
            dcp_ref[jj] += _dot_tn(scat, dyb)

        n_blk = ts // SCAN_BLOCK
        shp = (SUBLANES, SCAN_LANES)
        last_row = lax.broadcasted_iota(jnp.int32, shp, 0) == SUBLANES - 1
        for cc in range(cq // SCAN_LANES):
            cols = slice(cc * SCAN_LANES, (cc + 1) * SCAN_LANES)

            def block(i, carry, cols=cols):
                c_r, c_i, a_r, a_i = carry
                base = pl.multiple_of((n_blk - 1 - i) * SCAN_BLOCK, SCAN_BLOCK)
                rows = lambda tau: pl.ds(base + SUBLANES * tau, SUBLANES)
                m_r, m_i = tab[0, :, cols], tab[1, :, cols]
                ys = [None] * SUBLANES
                ys[SUBLANES - 1] = (gre[rows(SUBLANES - 1), cols], gim[rows(SUBLANES - 1), cols])
                for tau in reversed(range(SUBLANES - 1)):
                    ys[tau] = _cmac(gre[rows(tau), cols], gim[rows(tau), cols], m_r, m_i, *ys[tau + 1])
                tr, ti = _chain_segments(*ys[0], c_r, c_i, tab, cols, reverse=True)
                in_r = jnp.where(last_row, c_r, pltpu.roll(tr, SUBLANES - 1, 0))
                in_i = jnp.where(last_row, c_i, pltpu.roll(ti, SUBLANES - 1, 0))
                gs = [_cmac(*ys[tau], tab[10 + 2 * tau, :, cols], tab[11 + 2 * tau, :, cols], in_r, in_i)
                      for tau in range(SUBLANES)]
                for tau in range(SUBLANES):
                    gre[rows(tau), cols] = gs[tau][0]
                    gim[rows(tau), cols] = gs[tau][1]
                    if tau < SUBLANES - 1:
                        nr, ni = gs[tau + 1]
                    else:
                        nr = jnp.where(last_row, c_r, pltpu.roll(gs[0][0], SUBLANES - 1, 0))
                        ni = jnp.where(last_row, c_i, pltpu.roll(gs[0][1], SUBLANES - 1, 0))
                    sr, si = sre_ref[rows(tau), cols], sim_ref[rows(tau), cols]
                    a_r = a_r + sr * nr + si * ni
                    a_i = a_i + sr * ni - si * nr
                return (jnp.broadcast_to(tr[:1, :], shp), jnp.broadcast_to(ti[:1, :], shp), a_r, a_i)

            c_r, c_i, a_r, a_i = lax.fori_loop(
                0, n_blk, block, (cr[:, cols], ci[:, cols], accr[:, cols], acci[:, cols]), unroll=2)
            cr[:, cols] = c_r
            ci[:, cols] = c_i
            accr[:, cols] = a_r
            acci[:, cols] = a_i

        acc = dsk_ref[...] * dy
        for jj in range(4):
            cols = slice(jj * 128, (jj + 1) * 128)
            gcat = jnp.concatenate([gre[:, cols], gim[:, cols]], axis=1).astype(BF16)
            acc = acc + _dot(gcat, bt_ref[jj])
            dbp_ref[jj] += _dot_tn(ub, gcat)
        ddsk_ref[...] += jnp.sum(dy * uf, axis=0, keepdims=True)
        dyp[...] = acc
        _permute_rows(dyp, du_ref, ts)

        @pl.when(t == nt - 1)
        def _():
            dar_ref[...] = jnp.sum(accr[...], axis=0, keepdims=True)
            dai_ref[...] = jnp.sum(acci[...], axis=0, keepdims=True)

    f32_scr = lambda *s: pltpu.VMEM(s, F32)
    return pl.pallas_call(
        body, name="ssm_bwd", grid=(nq, nt),
        in_specs=[pl.BlockSpec((ts, 128), lambda q, t: (nt - 1 - t, q)),
                  pl.BlockSpec((ts, 128), lambda q, t: (nt - 1 - t, 4 + q)),
                  pl.BlockSpec((ts, cq), lambda q, t: (nt - 1 - t, q)),
                  pl.BlockSpec((ts, cq), lambda q, t: (nt - 1 - t, q)),
                  pl.BlockSpec((None, 4, 128, 256), lambda q, t: (layer, q, 0, 0)),
                  pl.BlockSpec((None, 4, 256, 128), lambda q, t: (layer, q, 0, 0)),
                  pl.BlockSpec((None, 1, cq), lambda q, t: (layer, 0, q)),
                  pl.BlockSpec((None, 1, cq), lambda q, t: (layer, 0, q)),
                  pl.BlockSpec((None, 1, 128), lambda q, t: (layer, 0, q))],
        out_specs=[pl.BlockSpec((ts, 128), lambda q, t: (nt - 1 - t, q)),
                   pl.BlockSpec((4, 256, 128), lambda q, t: (q, 0, 0)),
                   pl.BlockSpec((4, 128, 256), lambda q, t: (q, 0, 0)),
                   pl.BlockSpec((1, cq), lambda q, t: (0, q)),
                   pl.BlockSpec((1, cq), lambda q, t: (0, q)),
                   pl.BlockSpec((1, 128), lambda q, t: (0, q))],
        out_shape=[jax.ShapeDtypeStruct((L, D_SSM), F32),
                   jax.ShapeDtypeStruct((N_PAIRS, 256, 128), F32), jax.ShapeDtypeStruct((N_PAIRS, 128, 256), F32),
                   jax.ShapeDtypeStruct((1, N_STATE), F32), jax.ShapeDtypeStruct((1, N_STATE), F32),
                   jax.ShapeDtypeStruct((1, D_SSM), F32)],
        scratch_shapes=[f32_scr(ts, cq), f32_scr(ts, cq), f32_scr(SUBLANES, cq), f32_scr(SUBLANES, cq),
                        f32_scr(N_SCAN_TABLES, SUBLANES, cq), f32_scr(SUBLANES, cq), f32_scr(SUBLANES, cq),
                        f32_scr(ts, 128), f32_scr(ts, 128)],
        compiler_params=_cparams(2),
    )(dyraw, u, sre, sim, cpad_t, bpad_t, ar, ai, dskip)


def _pool_bwd(dyp, u, layer, w_pool, scale):
    L = u.shape[0]
    tm = min(TM, L)
    nt = L // tm
    halo_per_tile = tm // POOL_HALO

    def body(dyp_ref, u_ref, halo_ref, wp_ref, sc_ref, du_ref, dwp_ref, dsc_ref, carry):
        i = pl.program_id(0)
        tile = nt - 1 - i

        @pl.when(i == 0)
        def _():
            carry[...] = jnp.zeros_like(carry)
            dwp_ref[...] = jnp.zeros_like(dwp_ref)
            dsc_ref[...] = jnp.zeros_like(dsc_ref)

        up = u_ref[...]
        halo = jnp.where(tile > 0, halo_ref[...], jnp.zeros_like(halo_ref))
        diffs = _pool_diff(jnp.concatenate([halo, up], axis=0), tile * tm, tm)
        rows = tile * tm + lax.broadcasted_iota(jnp.int32, (tm, 1), 0)
        n_ext = tm + POOL_HALO
        for gi, w in enumerate(POOL_WINDOWS):
            cols = slice(gi * POOL_GROUP, (gi + 1) * POOL_GROUP)
            db = diffs[gi].astype(BF16)
            dyp = dyp_ref[:, cols]
            dsc_ref[:, cols] += jnp.sum(dyp * _dot(db, wp_ref[gi]), axis=0, keepdims=True)
            dp = (dyp * sc_ref[:, cols]).astype(BF16)
            ddiff = _dot_nt(dp, wp_ref[gi])
            dwp_ref[gi] += _dot_tn(db, dp)
            e = ddiff * (1.0 / jnp.minimum(rows + 1, w).astype(F32))
            s = jnp.concatenate([e, carry[:, cols]], axis=0)
            k = 1
            while k < w:
                s = s + pltpu.roll(s, n_ext - k, 0)
                k *= 2
            du_ref[:, cols] = s[:tm, :] - ddiff
            carry[:, cols] = e[:POOL_HALO, :]

    return pl.pallas_call(
        body, name="pool_bwd", grid=(nt,),
        in_specs=[pl.BlockSpec((tm, D_POOL), lambda i: (nt - 1 - i, 0)),
                  pl.BlockSpec((tm, D_POOL), lambda i: (nt - 1 - i, 0)),
                  pl.BlockSpec((POOL_HALO, D_POOL), lambda i: (jnp.maximum((nt - 1 - i) * halo_per_tile - 1, 0), 0)),
                  pl.BlockSpec((None, 4, POOL_GROUP, POOL_GROUP), lambda i: (layer, 0, 0, 0)),
                  pl.BlockSpec((None, 1, D_POOL), lambda i: (layer, 0, 0))],
        out_specs=[pl.BlockSpec((tm, D_POOL), lambda i: (nt - 1 - i, 0)),
                   pl.BlockSpec((4, POOL_GROUP, POOL_GROUP), lambda i: (0, 0, 0)),
                   pl.BlockSpec((1, D_POOL), lambda i: (0, 0))],
        out_shape=[jax.ShapeDtypeStruct((L, D_POOL), F32),
                   jax.ShapeDtypeStruct((4, POOL_GROUP, POOL_GROUP), F32),
                   jax.ShapeDtypeStruct((1, D_POOL), F32)],
        scratch_shapes=[pltpu.VMEM((POOL_HALO, D_POOL), F32)],
        compiler_params=_cparams(1),
    )(dyp, u, u, w_pool, scale)


def _mix_in_bwd(dup, dus, h, dhm, g1, wp, layer, gbuf):
    L = h.shape[0]
    tm = min(TM, L)
    n_steps = L // tm
    blk, idx = P_IN_BLK

    def body(dup_ref, dus_ref, h_ref, dhm_ref, g_ref, w_ref, g1_in, dh_ref, dg_ref, g1_ref, dwin):
        i = pl.program_id(0)

        @pl.when(i == 0)
        def _():
            dg_ref[...] = jnp.zeros_like(dg_ref)
            dwin[...] = jnp.zeros_like(dwin)

        du = jnp.concatenate([dup_ref[...], dus_ref[...]], axis=1).astype(BF16)
        dn1 = _dot_nt(du, w_ref[...].reshape(D_MODEL, D_MODEL))
        xhat, r = _rms_hat(h_ref[...])
        g = g_ref[...]
        n1 = (xhat * g).astype(BF16)
        dwin[...] += _dot_tn(n1, du).reshape(N_SHARD, blk, D_MODEL)
        dg_ref[...] += jnp.sum(dn1 * xhat, axis=0, keepdims=True)
        dh_ref[...] = dhm_ref[...] + _rms_bwd(dn1 * g, xhat, r)

        @pl.when(i == n_steps - 1)
        def _():
            pltpu.sync_copy(dwin, g1_ref.at[0, :, pl.ds(blk * idx, blk), :])

    row_spec = pl.BlockSpec((tm, D_MODEL), lambda i: (i, 0))
    half_spec = pl.BlockSpec((tm, D_POOL), lambda i: (i, 0))
    return pl.pallas_call(
        body, name="mix_in_bwd", grid=(n_steps,),
        in_specs=[half_spec, half_spec, row_spec, row_spec,
                  pl.BlockSpec((None, 1, D_MODEL), lambda i: (layer, 0, 0)),
                  pl.BlockSpec((N_SHARD, None, blk, D_MODEL), lambda i: (0, 0, idx, 0)),
                  pl.BlockSpec(memory_space=pl.ANY)],
        out_specs=[row_spec, pl.BlockSpec((1, D_MODEL), lambda i: (0, 0)), pl.BlockSpec(memory_space=pl.ANY)],
        out_shape=[jax.ShapeDtypeStruct((L, D_MODEL), F32), jax.ShapeDtypeStruct((1, D_MODEL), F32),
                   jax.ShapeDtypeStruct(gbuf.shape, F32)],
        scratch_shapes=[pltpu.VMEM((N_SHARD, blk, D_MODEL), F32)],
        input_output_aliases={6: 2},
        compiler_params=_cparams(1),
    )(dup, dus, h, dhm, g1, wp, gbuf)


def _disc_math(lr, li, ldt, br_t, bi_t):
    dt = jnp.exp(ldt)
    mag = jnp.exp(lr * dt)
    ang = li * dt
    ar = mag * jnp.cos(ang)
    ai = mag * jnp.sin(ang)
    den = lr * lr + li * li
    nr, ni = ar - 1.0, ai
    cr = (nr * lr + ni * li) / den
    ci = (ni * lr - nr * li) / den
    return ar, ai, cr * br_t - ci * bi_t, cr * bi_t + ci * br_t


def _disc_fwd(lr, li, ldt, br_t, bi_t):
    def body(lr_ref, li_ref, ldt_ref, br_ref, bi_ref, ar_ref, ai_ref, bbr_ref, bbi_ref):
        ar, ai, bbr, bbi = _disc_math(lr_ref[...], li_ref[...], ldt_ref[...], br_ref[...], bi_ref[...])
        ar_ref[...] = ar
        ai_ref[...] = ai
        bbr_ref[...] = bbr
        bbi_ref[...] = bbi

    shapes = [jax.ShapeDtypeStruct(a.shape, F32) for a in (lr, li, br_t, bi_t)]
    return pl.pallas_call(body, name="ssm_disc_fwd", out_shape=shapes,
                          compiler_params=pltpu.CompilerParams(vmem_limit_bytes=VMEM_LIMIT))(lr, li, ldt, br_t, bi_t)


def _disc_bwd(lr, li, ldt, br_t, bi_t, dar, dai, dbbr, dbbi):
    def body(lr_ref, li_ref, ldt_ref, br_ref, bi_ref, dar_ref, dai_ref, dbbr_ref, dbbi_ref,
             dlr_ref, dli_ref, dldt_ref, dbr_ref, dbi_ref):
        prim = (lr_ref[...], li_ref[...], ldt_ref[...], br_ref[...], bi_ref[...])
        _, pullback = jax.vjp(_disc_math, *prim)
        dlr, dli, dldt, dbr, dbi = pullback((dar_ref[...], dai_ref[...], dbbr_ref[...], dbbi_ref[...]))
        dlr_ref[...] = dlr
        dli_ref[...] = dli
        dldt_ref[...] = dldt
        dbr_ref[...] = dbr
        dbi_ref[...] = dbi

    shapes = [jax.ShapeDtypeStruct(a.shape, F32) for a in (lr, li, ldt, br_t, bi_t)]
    return pl.pallas_call(body, name="ssm_disc_bwd", out_shape=shapes,
                          compiler_params=pltpu.CompilerParams(vmem_limit_bytes=VMEM_LIMIT))(
        lr, li, ldt, br_t, bi_t, dar, dai, dbbr, dbbi)


def _pad_pairs(m_re, m_im):
    def blocks(m):
        v = m.transpose(0, 2, 1).reshape(N_PAIRS, 2, SSM_GROUP, SSM_STATE)
        return jnp.einsum("ab,jahp->jahbp", jnp.eye(2, dtype=m.dtype), v).reshape(N_PAIRS, 32, 128)
    both = jnp.concatenate([blocks(m_re), blocks(m_im)], axis=-1)
    place = jax.nn.one_hot(jnp.arange(N_PAIRS) % 4, 4, dtype=both.dtype)
    return jnp.einsum("jk,jrc->jkrc", place, both).reshape(N_PAIRS, 128, 256)


def _unpad_pairs(x):
    place = jax.nn.one_hot(jnp.arange(N_PAIRS) % 4, 4, dtype=x.dtype)
    both = jnp.einsum("jk,jkrc->jrc", place, x.reshape(N_PAIRS, 4, 32, 256))

    def unblock(v):
        v = v.reshape(N_PAIRS, 2, SSM_GROUP, 2, SSM_STATE)
        d = jnp.einsum("ab,jahbp->jahp", jnp.eye(2, dtype=x.dtype), v)
        return d.reshape(N_SSM_GROUPS, SSM_GROUP, SSM_STATE).transpose(0, 2, 1)
    return unblock(both[..., :128]), unblock(both[..., 128:])


def _adamw_math(w, g, m, v):
    m = ADAM_B1 * m + (1.0 - ADAM_B1) * g
    v = ADAM_B2 * v + (1.0 - ADAM_B2) * (g * g)
    m_hat = m / (1.0 - ADAM_B1 ** ADAM_STEP)
    v_hat = v / (1.0 - ADAM_B2 ** ADAM_STEP)
    delta = -ADAM_LR * (m_hat / (jnp.sqrt(v_hat) + ADAM_EPS) + ADAM_WD * w)
    return delta, m, v


def _adamw(name, layer, w, m, v, gbuf, g_block, g_row0, row_tile, outs=None, after=(), glu=False):
    nl, r, c = w.shape
    n_tiles = r // row_tile
    g_rows, g_cols = g_block
    g_tile = g_rows // n_tiles
    g_off = g_row0 // g_tile
    if outs is None:
        outs = [lax.empty(w.shape, F32) for _ in range(4)]

    def body(w_ref, m_ref, v_ref, g_ref, *rest):
        go_ref, d_ref, mo_ref, vo_ref = rest[-4:]
        g = g_ref[...]
        if glu:
            g = jnp.concatenate([g[:, :D_SSM], g[:, D_SSM:]], axis=0)
        delta, mn, vn = _adamw_math(w_ref[...], g, m_ref[...], v_ref[...])
        go_ref[...] = g
        d_ref[...] = delta
        mo_ref[...] = mn
        vo_ref[...] = vn

    w_spec = pl.BlockSpec((None, row_tile, c), lambda j: (layer, j, 0))
    shape = jax.ShapeDtypeStruct(w.shape, F32)
    return pl.pallas_call(
        body, name=name, grid=(n_tiles,),
        in_specs=[w_spec, w_spec, w_spec, pl.BlockSpec((None, g_tile, g_cols), lambda j: (0, g_off + j, 0))]
        + [_ANY] * (4 + len(after)),
        out_specs=[w_spec] * 4,
        out_shape=[shape] * 4,
        input_output_aliases={4: 0, 5: 1, 6: 2, 7: 3},
        compiler_params=_cparams(1),
    )(w, m, v, gbuf, *outs, *after)


def _adamw_group(name, layer, ws, ms, vs, gbuf, g_row0s, row_tile, outs=None):
    k = len(ws)
    nl, r, c = ws[0].shape
    n_tiles = r // row_tile
    if outs is None:
        outs = [[lax.empty(ws[0].shape, F32) for _ in range(4)] for _ in range(k)]

    def body(*refs):
        ins, results = refs[:4 * k], refs[-4 * k:]
        for i in range(k):
            w_ref, m_ref, v_ref, g_ref = (ins[j * k + i] for j in range(4))
            g = g_ref[...]
            delta, mn, vn = _adamw_math(w_ref[...], g, m_ref[...], v_ref[...])
            for ref, val in zip(results[4 * i:4 * i + 4], (g, delta, mn, vn)):
                ref[...] = val

    w_spec = pl.BlockSpec((None, row_tile, c), lambda j: (layer, j, 0))
    g_specs = [pl.BlockSpec((None, row_tile, c), functools.partial(lambda j, off: (0, off + j, 0), off=r0 // row_tile))
               for r0 in g_row0s]
    shape = jax.ShapeDtypeStruct(ws[0].shape, F32)
    flat = pl.pallas_call(
        body, name=name, grid=(n_tiles,),
        in_specs=[w_spec] * (3 * k) + g_specs + [_ANY] * (4 * k),
        out_specs=[w_spec] * (4 * k),
        out_shape=[shape] * (4 * k),
        input_output_aliases={4 * k + i: i for i in range(4 * k)},
        compiler_params=_cparams(1),
    )(*ws, *ms, *vs, *([gbuf] * k), *[a for group in outs for a in group])
    return [flat[4 * i:4 * i + 4] for i in range(k)]


def _pack_weights(ids, layer, w_in, w_glu, w_out, w_down, w_gate_t, w_up_t, after=()):
    gb, gi = P_GLU_BLK
    ib, ii = P_IN_BLK
    ob, oi = P_OUT_BLK

    def body(ids_ref, in_ref, glu_ref, out_ref, dn_ref, gate_ref, up_ref, *rest):
        p_ref = rest[-1]
        p_ref[0:FF_SHARD, :] = dn_ref[...].astype(BF16)
        p_ref[FF_SHARD:2 * FF_SHARD, :] = gate_ref[...].astype(BF16)
        p_ref[2 * FF_SHARD:P_FF_ROWS, :] = up_ref[...].astype(BF16)
        g = glu_ref[...]
        p_ref[gb * gi:gb * (gi + 1), :] = jnp.concatenate([g[:gb, :], g[gb:, :]], axis=1).astype(BF16)
        p_ref[gb * (gi + 1):ib * ii, :] = jnp.zeros((P_GLU_PAD - gb, D_MODEL), BF16)
        p_ref[ib * ii:ib * (ii + 1), :] = in_ref[...].astype(BF16)
        p_ref[ob * oi:ob * (oi + 1), :] = out_ref[...].astype(BF16)

    def spec(a):
        return pl.BlockSpec((None,) + a.shape[1:], lambda i, ids_ref: (layer, 0, 0))

    ins = (w_in, w_glu, w_out, w_down, w_gate_t, w_up_t)
    grid_spec = pltpu.PrefetchScalarGridSpec(
        num_scalar_prefetch=1, grid=(1,),
        in_specs=[spec(a) for a in ins] + [_ANY] * len(after),
        out_specs=pl.BlockSpec((None, None, P_ROWS, D_MODEL), lambda i, ids_ref: (ids_ref[1], 0, 0, 0)))
    return pl.pallas_call(
        body, name="pack_weights", grid_spec=grid_spec,
        out_shape=jax.ShapeDtypeStruct((N_SHARD, 1, P_ROWS, D_MODEL), BF16),
        compiler_params=_cparams(1),
    )(ids, *ins, *after)


MESH = pl.DeviceIdType.MESH
_ANY = pl.BlockSpec(memory_space=pl.ANY)
P_HALF = P_ROWS // 2
RS_ROW_TILE = 704


def _mesh_pos():
    return lax.axis_index("x"), lax.axis_index("y"), lax.axis_index("c")


def _other_chips(x, y):
    return [(1 - x, y), (x, 1 - y), (1 - x, 1 - y)]


def _remote(src, dst, send_sems, recv_sems, n, to):
    return pltpu.make_async_remote_copy(src_ref=src, dst_ref=dst, send_sem=send_sems.at[n],
                                        recv_sem=recv_sems.at[n], device_id=to, device_id_type=MESH)


_HBM = pl.BlockSpec(memory_space=pltpu.HBM)
_SEM = pl.BlockSpec(memory_space=pltpu.SEMAPHORE)
_EFFECT = pltpu.CompilerParams(has_side_effects=pltpu.SideEffectType.DATAFLOW_SIDE_EFFECTING)
_TOKEN = jax.ShapeDtypeStruct((8, 128), F32)


def _in_hbm(a):
    return pltpu.with_memory_space_constraint(a, pltpu.HBM)


def _ag_piece(ref, shard, half, rows):
    row0, n_rows = rows
    return ref.at[shard, :, pl.ds(row0 + half * (n_rows // 2), n_rows // 2), :]


def _ag_start(name, wp, after, row_ranges):
    n_sems = 3 * len(row_ranges)

    def body(w_ref, after_ref, send_sems, recv_sems, w_thru, token):
        x, y, c = _mesh_pos()
        for i, rows in enumerate(row_ranges):
            mine = _ag_piece(w_ref, 2 * x + y, c, rows)
            for j, (px, py) in enumerate(_other_chips(x, y)):
                _remote(mine, mine, send_sems, recv_sems, 3 * i + j, (px, py, c)).start()
        token[...] = jnp.zeros_like(token)

    return pl.pallas_call(
        body, name=name,
        out_shape=(pltpu.SemaphoreType.DMA((n_sems,)), pltpu.SemaphoreType.DMA((n_sems,)),
                   pltpu.HBM(wp.shape, wp.dtype), _TOKEN),
        in_specs=(_HBM, _ANY), out_specs=(_SEM, _SEM, _HBM, pl.BlockSpec(memory_space=pltpu.VMEM)),
        input_output_aliases={0: 2}, compiler_params=_EFFECT,
    )(_in_hbm(wp), after)


def _ag_wait(name, send_sems, recv_sems, wp, after, row_ranges):
    def body(w_ref, send_sems, recv_sems, *rest):
        x, y, c = _mesh_pos()
        for i, rows in enumerate(row_ranges):
            mine = _ag_piece(w_ref, 2 * x + y, c, rows)
            for j, (px, py) in enumerate(_other_chips(x, y)):
                landed = _ag_piece(w_ref, 2 * px + py, c, rows)
                cp = _remote(mine, landed, send_sems, recv_sems, 3 * i + j, (px, py, c))
                cp.wait_send()
                cp.wait_recv()

    return pl.pallas_call(
        body, name=name, out_shape=pltpu.HBM(wp.shape, wp.dtype),
        in_specs=(_HBM, _SEM, _SEM) + (_ANY,) * len(after), out_specs=_HBM,
        input_output_aliases={0: 0}, compiler_params=_EFFECT,
    )(wp, send_sems, recv_sems, *after)


def _ag_forward(wp, rows):
    def body(w_in, o, send_sems, recv_sems):
        x, y, c = _mesh_pos()
        sib = (x, y, 1 - c)
        chips = _other_chips(x, y)
        sends = []
        for j, (px, py) in enumerate(chips):
            landed = _ag_piece(o, 2 * px + py, c, rows)
            cp = _remote(landed, landed, send_sems, recv_sems, j, sib)
            cp.start()
            sends.append(cp)
        for j, (px, py) in enumerate(chips):
            passed = _ag_piece(o, 2 * px + py, 1 - c, rows)
            _remote(passed, passed, send_sems, recv_sems, j, sib).wait_recv()
        for cp in sends:
            cp.wait_send()

    return pl.pallas_call(
        body, name="ag_forward",
        in_specs=[_ANY], out_specs=_ANY,
        out_shape=jax.ShapeDtypeStruct(wp.shape, wp.dtype),
        scratch_shapes=[pltpu.SemaphoreType.DMA((3,)), pltpu.SemaphoreType.DMA((3,))],
        input_output_aliases={0: 0},
    )(wp)


def _ag_forward_start(name, wp, rows):
    def body(w_ref, send_sems, recv_sems, w_thru):
        x, y, c = _mesh_pos()
        for j, (px, py) in enumerate(_other_chips(x, y)):
            landed = _ag_piece(w_ref, 2 * px + py, c, rows)
            _remote(landed, landed, send_sems, recv_sems, j, (x, y, 1 - c)).start()

    return pl.pallas_call(
        body, name=name,
        out_shape=(pltpu.SemaphoreType.DMA((3,)), pltpu.SemaphoreType.DMA((3,)), pltpu.HBM(wp.shape, wp.dtype)),
        in_specs=(_HBM,), out_specs=(_SEM, _SEM, _HBM),
        input_output_aliases={0: 2}, compiler_params=_EFFECT,
    )(_in_hbm(wp))


def _ag_forward_wait(name, send_sems, recv_sems, wp, after, rows):
    def body(w_ref, send_sems, recv_sems, *rest):
        x, y, c = _mesh_pos()
        for j, (px, py) in enumerate(_other_chips(x, y)):
            cp = _remote(_ag_piece(w_ref, 2 * px + py, c, rows), _ag_piece(w_ref, 2 * px + py, 1 - c, rows),
                         send_sems, recv_sems, j, (x, y, 1 - c))
            cp.wait_send()
            cp.wait_recv()

    return pl.pallas_call(
        body, name=name, out_shape=pltpu.HBM(wp.shape, wp.dtype),
        in_specs=(_HBM, _SEM, _SEM) + (_ANY,) * len(after), out_specs=_HBM,
        input_output_aliases={0: 0}, compiler_params=_EFFECT,
    )(wp, send_sems, recv_sems, *after)


def _rs_chips_start(name, t):
    nl = t.shape[0]

    def body(t_ref, land_ref, send_sems, recv_sems, t_thru, land_thru, token):
        x, y, c = _mesh_pos()
        for j, (px, py) in enumerate(_other_chips(x, y)):
            _remote(t_ref.at[:, 2 * px + py], land_ref.at[j], send_sems, recv_sems, j, (px, py, c)).start()
        token[...] = jnp.zeros_like(token)

    land = lax.empty((3, nl, P_HALF, D_MODEL), BF16)
    return pl.pallas_call(
        body, name=name,
        out_shape=(pltpu.SemaphoreType.DMA((3,)), pltpu.SemaphoreType.DMA((3,)), pltpu.HBM(t.shape, t.dtype),
                   pltpu.HBM(land.shape, land.dtype), _TOKEN),
        in_specs=(_HBM, _HBM), out_specs=(_SEM, _SEM, _HBM, _HBM, pl.BlockSpec(memory_space=pltpu.VMEM)),
        input_output_aliases={0: 2, 1: 3}, compiler_params=_EFFECT,
    )(_in_hbm(t), _in_hbm(land))


def _rs_chips_wait(name, send_sems, recv_sems, t, land, after):
    def body(t_ref, land_ref, send_sems, recv_sems, *rest):
        x, y, c = _mesh_pos()
        for j, (px, py) in enumerate(_other_chips(x, y)):
            cp = _remote(t_ref.at[:, 2 * px + py], land_ref.at[j], send_sems, recv_sems, j, (px, py, c))
            cp.wait_send()
            cp.wait_recv()

    return pl.pallas_call(
        body, name=name, out_shape=(pltpu.HBM(t.shape, t.dtype), pltpu.HBM(land.shape, land.dtype)),
        in_specs=(_HBM, _HBM, _SEM, _SEM) + (_ANY,) * len(after), out_specs=(_HBM, _HBM),
        input_output_aliases={0: 0, 1: 1}, compiler_params=_EFFECT,
    )(t, land, send_sems, recv_sems, *after)[1]


def _rs_sibling_start(name, g):
    nl = g.shape[0]

    def body(g_ref, land_ref, send_sems, recv_sems, g_thru, land_thru, token):
        x, y, c = _mesh_pos()
        _remote(g_ref.at[:, :, pl.ds((1 - c) * P_HALF, P_HALF), :], land_ref, send_sems, recv_sems, 0,
                (x, y, 1 - c)).start()
        token[...] = jnp.zeros_like(token)

    land = lax.empty((nl, N_SHARD, P_HALF, D_MODEL), F32)
    return pl.pallas_call(
        body, name=name,
        out_shape=(pltpu.SemaphoreType.DMA((1,)), pltpu.SemaphoreType.DMA((1,)), pltpu.HBM(g.shape, g.dtype),
                   pltpu.HBM(land.shape, land.dtype), _TOKEN),
        in_specs=(_HBM, _HBM), out_specs=(_SEM, _SEM, _HBM, _HBM, pl.BlockSpec(memory_space=pltpu.VMEM)),
        input_output_aliases={0: 2, 1: 3}, compiler_params=_EFFECT,
    )(_in_hbm(g), _in_hbm(land))


def _rs_sibling_wait(name, send_sems, recv_sems, g, land, after):
    def body(g_ref, land_ref, send_sems, recv_sems, *rest):
        x, y, c = _mesh_pos()
        cp = _remote(g_ref.at[:, :, pl.ds((1 - c) * P_HALF, P_HALF), :], land_ref, send_sems, recv_sems, 0,
                     (x, y, 1 - c))
        cp.wait_send()
        cp.wait_recv()

    return pl.pallas_call(
        body, name=name, out_shape=(pltpu.HBM(g.shape, g.dtype), pltpu.HBM(land.shape, land.dtype)),
        in_specs=(_HBM, _HBM, _SEM, _SEM) + (_ANY,) * len(after), out_specs=(_HBM, _HBM),
        input_output_aliases={0: 0, 1: 1}, compiler_params=_EFFECT,
    )(g, land, send_sems, recv_sems, *after)


def _rs_add(name, ids, g, buf, row_tile):
    nl, _, hr, cols = buf.shape
    n_rt = hr // row_tile

    def body(ids_ref, g_ref, b_ref, own_ref, tb_ref):
        t = g_ref[...] + b_ref[...]
        tb_ref[...] = t.astype(BF16)

        @pl.when(pl.program_id(2) == ids_ref[1])
        def _():
            own_ref[...] = t

    blk = (None, None, row_tile, cols)
    grid_spec = pltpu.PrefetchScalarGridSpec(
        num_scalar_prefetch=1, grid=(nl, n_rt, N_SHARD),
        in_specs=[pl.BlockSpec(blk, lambda l, j, s, ids_ref: (l, s, ids_ref[0] * n_rt + j, 0)),
                  pl.BlockSpec(blk, lambda l, j, s, ids_ref: (l, s, j, 0))],
        out_specs=[pl.BlockSpec((None, row_tile, cols), lambda l, j, s, ids_ref: (l, j, 0)),
                   pl.BlockSpec(blk, lambda l, j, s, ids_ref: (l, s, j, 0))])
    return pl.pallas_call(
        body, name=name, grid_spec=grid_spec,
        out_shape=[jax.ShapeDtypeStruct((nl, hr, cols), F32), jax.ShapeDtypeStruct(buf.shape, BF16)],
        compiler_params=_cparams(3),
    )(ids, g, buf)


def _rs_sum(ids, layer, own, bufb, reduced, row_tile):
    _, hr, cols = own.shape
    n_rt = hr // row_tile

    def body(ids_ref, own_ref, b_ref, reduced_in, f_ref):
        f_ref[...] = ((own_ref[...] + b_ref[0].astype(F32)) + b_ref[1].astype(F32)) + b_ref[2].astype(F32)

    grid_spec = pltpu.PrefetchScalarGridSpec(
        num_scalar_prefetch=1, grid=(n_rt,),
        in_specs=[pl.BlockSpec((None, row_tile, cols), lambda j, ids_ref: (0, j, 0)),
                  pl.BlockSpec((3, None, row_tile, cols), lambda j, ids_ref: (0, 0, j, 0)),
                  pl.BlockSpec(memory_space=pl.ANY)],
        out_specs=pl.BlockSpec((None, row_tile, cols), lambda j, ids_ref: (layer, ids_ref[0] * n_rt + j, 0)))
    return pl.pallas_call(
        body, name="rs_sum", grid_spec=grid_spec,
        out_shape=jax.ShapeDtypeStruct(reduced.shape, F32),
        input_output_aliases={3: 0},
        compiler_params=_cparams(1),
    )(ids, own, bufb, reduced)


def _rs_exchange_start(name, f):
    def body(f_ref, send_sems, recv_sems, f_thru):
        x, y, c = _mesh_pos()
        mine = f_ref.at[:, pl.ds(c * P_HALF, P_HALF), :]
        _remote(mine, mine, send_sems, recv_sems, 0, (x, y, 1 - c)).start()

    return pl.pallas_call(
        body, name=name,
        out_shape=(pltpu.SemaphoreType.DMA((1,)), pltpu.SemaphoreType.DMA((1,)), pltpu.HBM(f.shape, f.dtype)),
        in_specs=(_HBM,), out_specs=(_SEM, _SEM, _HBM),
        input_output_aliases={0: 2}, compiler_params=_EFFECT,
    )(_in_hbm(f))


def _rs_exchange_wait(name, send_sems, recv_sems, f, after):
    def body(f_ref, send_sems, recv_sems, *rest):
        x, y, c = _mesh_pos()
        mine = f_ref.at[:, pl.ds(c * P_HALF, P_HALF), :]
        theirs = f_ref.at[:, pl.ds((1 - c) * P_HALF, P_HALF), :]
        cp = _remote(mine, theirs, send_sems, recv_sems, 0, (x, y, 1 - c))
        cp.wait_send()
        cp.wait_recv()

    return pl.pallas_call(
        body, name=name, out_shape=pltpu.HBM(f.shape, f.dtype),
        in_specs=(_HBM, _SEM, _SEM) + (_ANY,) * len(after), out_specs=_HBM,
        input_output_aliases={0: 0}, compiler_params=_EFFECT,
    )(f, send_sems, recv_sems, *after)


def _small_all_reduce(s, after=()):
    n_rows = s.shape[0]
    hr = n_rows // 2
    qr = hr // N_SHARD

    def body(s_ref, *rest):
        o_ref, sibbuf, tbuf, qbuf, fbuf, send_sems, recv_sems = rest[len(after):]
        x, y, c = _mesh_pos()
        k = 2 * x + y
        sib = (x, y, 1 - c)
        chips = _other_chips(x, y)
        mine = pl.ds(pl.multiple_of(c * hr, SUBLANES), hr)
        theirs = pl.ds(pl.multiple_of((1 - c) * hr, SUBLANES), hr)

        def quarter(shard):
            return pl.ds(pl.multiple_of(shard * qr, SUBLANES), qr)

        first = _remote(s_ref.at[theirs], sibbuf, send_sems, recv_sems, 0, sib)
        first.start()
        first.wait()
        tbuf[...] = s_ref[mine, :] + sibbuf[...]
        cps = []
        for j, (px, py) in enumerate(chips):
            cp = _remote(tbuf.at[quarter(2 * px + py)], qbuf.at[j], send_sems, recv_sems, 1 + j, (px, py, c))
            cp.start()
            cps.append(cp)
        for cp in cps:
            cp.wait()
        fbuf[quarter(k), :] = (tbuf[quarter(k), :] + qbuf[1]) + (qbuf[0] + qbuf[2])
        cps = []
        for j, (px, py) in enumerate(chips):
            cp = _remote(fbuf.at[quarter(k)], fbuf.at[quarter(k)], send_sems, recv_sems, 4 + j, (px, py, c))
            cp.start()
            cps.append(cp)
        for j, (px, py) in enumerate(chips):
            got = fbuf.at[quarter(2 * px + py)]
            _remote(got, got, send_sems, recv_sems, 4 + j, (px, py, c)).wait_recv()
        for cp in cps:
            cp.wait_send()
        o_ref[mine, :] = fbuf[...]
        last = _remote(fbuf, o_ref.at[mine], send_sems, recv_sems, 7, sib)
        last.start()
        last.wait()

    vmem = pl.BlockSpec(memory_space=pltpu.VMEM)
    return pl.pallas_call(
        body, name="small_all_reduce",
        in_specs=[vmem] + [_ANY] * len(after), out_specs=vmem,
        out_shape=jax.ShapeDtypeStruct(s.shape, F32),
        scratch_shapes=[pltpu.VMEM((hr, D_MODEL), F32), pltpu.VMEM((hr, D_MODEL), F32),
                        pltpu.VMEM((3, qr, D_MODEL), F32), pltpu.VMEM((hr, D_MODEL), F32),
                        pltpu.SemaphoreType.DMA((8,)), pltpu.SemaphoreType.DMA((8,))],
        compiler_params=pltpu.CompilerParams(vmem_limit_bytes=VMEM_LIMIT),
    )(s, *after)


_SMALL = ("norm_mix", "w_pool", "pool_scale", "lam_re", "lam_im", "log_dt", "b_re", "b_im", "c_re", "c_im",
          "d_skip", "b_glu", "norm_ffn", "norm_final")
_WEIGHTS = ("norm_mix", "w_in", "w_pool", "pool_scale", "lam_re", "lam_im", "log_dt", "b_re", "b_im", "c_re",
            "c_im", "d_skip", "w_glu", "b_glu", "w_out", "norm_ffn", "w_gate", "w_up", "w_down", "norm_final")


def _local_step(x, target, p, get_weights, scan_done, get_ffn_weights, ffn_bwd_done, put_grads):
    nl = p["norm_mix"].shape[0]

    def tied(a, token):
        return a if token is None else a + token
    n_rows = nl * N_SSM_GROUPS
    lr = p["lam_re"].reshape(n_rows, 1, SSM_STATE)
    li = p["lam_im"].reshape(n_rows, 1, SSM_STATE)
    ldt = p["log_dt"].reshape(n_rows, 1, 1)
    br_t = p["b_re"].reshape(n_rows, SSM_STATE, SSM_GROUP).transpose(0, 2, 1)
    bi_t = p["b_im"].reshape(n_rows, SSM_STATE, SSM_GROUP).transpose(0, 2, 1)
    ar, ai, bbr_t, bbi_t = _disc_fwd(lr, li, ldt, br_t, bi_t)
    ar = ar.reshape(nl, 1, N_STATE)
    ai = ai.reshape(nl, 1, N_STATE)
    bbr = bbr_t.transpose(0, 2, 1).reshape(nl, N_SSM_GROUPS, SSM_STATE, SSM_GROUP)
    bbi = bbi_t.transpose(0, 2, 1).reshape(nl, N_SSM_GROUPS, SSM_STATE, SSM_GROUP)
    w_pool = p["w_pool"].astype(BF16)
    p = dict(p)
    for n in ("norm_mix", "pool_scale", "b_glu", "norm_ffn"):
        p[n] = p[n].reshape(nl, 1, -1)
    swap = lambda a: jnp.swapaxes(a, -1, -2)
    bpad = jax.vmap(_pad_pairs)(bbr, bbi).astype(BF16)
    cpad_t = jax.vmap(_pad_pairs)(swap(p["c_re"]), -swap(p["c_im"])).astype(BF16)
    bpad_t, cpad = swap(bpad), swap(cpad_t)
    dskip = p["d_skip"].reshape(nl, 1, D_SSM)

    layers = []
    h = x
    for l in range(nl):
        wp = get_weights(l, [h] if l else [h, bpad, cpad, bpad_t, cpad_t, ar, ai])
        u, ypool = _mix_in_fwd(h, p["norm_mix"], wp, l, w_pool, p["pool_scale"])
        sre, sim, yraw = _ssm_fwd(u, l, bpad, cpad, ar, ai, dskip)
        wp = scan_done(l, wp, [yraw])
        hm = _mix_out_fwd(yraw, ypool, h, wp, l, p["b_glu"])
        wp = get_ffn_weights(l, wp, [hm])
        h_next, n2, act_s, fgate_s, fup_s = _ffn_fwd(hm, p["norm_ffn"], wp, l)
        layers.append(dict(h=h, u=u, ypool=ypool, sre=sre, sim=sim, yraw=yraw, hm=hm, n2=n2, act_s=act_s, wp=wp,
                           fgate_s=fgate_s, fup_s=fup_s))
        h = h_next

    dh, loss, d_norm_final = _final_fwd_bwd(h, p["norm_final"].reshape(1, D_MODEL), target)

    raw = {n: [None] * nl for n in ("dg1", "dwp", "dsc", "dcp", "dbp", "ddsk", "db_glu", "dg2", "dar", "dai")}
    token = None
    for l in reversed(range(nl)):
        s = layers[l]
        wp = s["wp"]
        g1 = lax.empty((1, N_SHARD, P_ROWS, D_MODEL), F32)
        dhm, dg2, dgate_s, dup_s, dhb = _ffn_bwd_act(dh, s["hm"], tied(p["norm_ffn"], token), s["fgate_s"],
                                                      s["fup_s"], wp, l)
        g1 = _ffn_bwd_w(s["n2"], dgate_s, dup_s, s["act_s"], dhb, g1)
        token = ffn_bwd_done(l, [g1])
        dyraw, dyp, db_glu, g1 = _mix_out_bwd(dhm, s["yraw"], s["ypool"], wp, l, tied(p["b_glu"], token), g1)
        dus, dcp, dbp, dar, dai, ddsk = _ssm_bwd(dyraw, s["u"], s["sre"], s["sim"], l, cpad_t, bpad_t, ar, ai, dskip)
        dup, dwp, dsc = _pool_bwd(dyp, s["u"], l, w_pool, p["pool_scale"])
        dh, dg1, g1 = _mix_in_bwd(dup, dus, s["h"], dhm, p["norm_mix"], wp, l, g1)
        token = put_grads(l, g1)
        for n, a in (("dg1", dg1), ("dwp", dwp), ("dsc", dsc), ("dcp", dcp), ("dbp", dbp), ("ddsk", ddsk),
                     ("db_glu", db_glu), ("dg2", dg2), ("dar", dar), ("dai", dai)):
            raw[n][l] = a

    st = {n: jnp.stack(v) for n, v in raw.items()}
    dc_re, dc_im = jax.vmap(_unpad_pairs)(swap(st["dcp"]))
    dbbr, dbbi = jax.vmap(_unpad_pairs)(st["dbp"])
    rows = lambda a: a.reshape((n_rows,) + a.shape[2:])
    dlr, dli, dldt, dbr_t, dbi_t = _disc_bwd(lr, li, ldt, br_t, bi_t, st["dar"].reshape(n_rows, 1, SSM_STATE),
                                              st["dai"].reshape(n_rows, 1, SSM_STATE), rows(swap(dbbr)),
                                              rows(swap(dbbi)))
    small = {"norm_mix": st["dg1"][:, 0], "w_pool": st["dwp"], "pool_scale": st["dsc"][:, 0], "c_re": swap(dc_re),
             "c_im": -swap(dc_im), "d_skip": st["ddsk"].reshape(nl, N_SSM_GROUPS, SSM_GROUP),
             "b_glu": st["db_glu"][:, 0], "norm_ffn": st["dg2"][:, 0]}
    small["lam_re"] = dlr.reshape(nl, N_SSM_GROUPS, SSM_STATE)
    small["lam_im"] = dli.reshape(nl, N_SSM_GROUPS, SSM_STATE)
    small["log_dt"] = dldt.reshape(nl, N_SSM_GROUPS)
    small["b_re"] = dbr_t.reshape(nl, N_SSM_GROUPS, SSM_GROUP, SSM_STATE)
    small["b_im"] = dbi_t.reshape(nl, N_SSM_GROUPS, SSM_GROUP, SSM_STATE)
    small["d_skip"] = small["d_skip"].transpose(_SMALL_VIEW["d_skip"])
    small["norm_final"] = d_norm_final
    return loss, dh, small


_SMALL_VIEW = {"b_re": (0, 1, 3, 2), "b_im": (0, 1, 3, 2), "d_skip": (0, 2, 1)}
_SMALL_GROUPS = (("b_re", "b_im"), ("c_re", "c_im"), ("lam_re", "lam_im"), ("norm_mix", "norm_ffn"),
                 ("pool_scale", "b_glu"), ("w_pool",), ("log_dt",), ("d_skip",), ("norm_final",))


def _view(n, a):
    a = a.transpose(_SMALL_VIEW[n]) if n in _SMALL_VIEW else a
    return a[None] if a.ndim == 1 else a


def _unview(n, a, shape):
    a = a.reshape(shape) if len(shape) == 1 else a
    return a.transpose(_SMALL_VIEW[n]) if n in _SMALL_VIEW else a


def _flatten_small(views):
    flat = jnp.concatenate([views[n].reshape(-1) for n in _SMALL])
    n_rows = -(-flat.shape[0] // (64 * D_MODEL)) * 64
    return jnp.pad(flat, (0, n_rows * D_MODEL - flat.shape[0])).reshape(n_rows, D_MODEL)


def _split_small(flat, like):
    flat = flat.reshape(-1)
    out, at = {}, 0
    for n in _SMALL:
        size = like[n].size
        out[n] = flat[at:at + size].reshape(like[n].shape)
        at += size
    return out


def _adamw_small(name, ws, ms, vs, gs):
    k = len(ws)

    def body(*refs):
        ins, outs = refs[:4 * k], refs[4 * k:]
        for i in range(k):
            w, m, v, g = (ins[j * k + i][...] for j in range(4))
            delta, mn, vn = _adamw_math(w, g, m, v)
            outs[i][...] = delta
            outs[k + i][...] = mn
            outs[2 * k + i][...] = vn

    shapes = [jax.ShapeDtypeStruct(w.shape, F32) for w in ws] * 3
    outs = pl.pallas_call(body, name=name, out_shape=shapes,
                          compiler_params=pltpu.CompilerParams(vmem_limit_bytes=VMEM_LIMIT))(*ws, *ms, *vs, *gs)
    return outs[:k], outs[k:2 * k], outs[2 * k:]


def kernel(x, norm_mix, w_in, w_pool, pool_scale, lam_re, lam_im, log_dt, b_re, b_im, c_re, c_im, d_skip, w_glu, b_glu, w_out, norm_ffn, w_gate, w_up, w_down, norm_final, loss_target, m_norm_mix, m_w_in, m_w_pool, m_pool_scale, m_lam_re, m_lam_im, m_log_dt, m_b_re, m_b_im, m_c_re, m_c_im, m_d_skip, m_w_glu, m_b_glu, m_w_out, m_norm_ffn, m_w_gate, m_w_up, m_w_down, m_norm_final, v_norm_mix, v_w_in, v_w_pool, v_pool_scale, v_lam_re, v_lam_im, v_log_dt, v_b_re, v_b_im, v_c_re, v_c_im, v_d_skip, v_w_glu, v_b_glu, v_w_out, v_norm_ffn, v_w_gate, v_w_up, v_w_down, v_norm_final):
    given = dict(locals())
    w = {n: given[n] for n in _WEIGHTS}
    m = {n: given["m_" + n] for n in _WEIGHTS}
    v = {n: given["v_" + n] for n in _WEIGHTS}
    ids = jnp.stack([lax.axis_index("c"), 2 * lax.axis_index("x") + lax.axis_index("y")]).astype(jnp.int32)

    t_names = ("w_gate", "w_up")
    tr = lambda a: a.transpose(0, 2, 1)
    for d in (w, m, v):
        d.update({n: tr(d[n]) for n in t_names})

    nl = norm_mix.shape[0]
    mixer_rows, ffn_rows = (P_FF_ROWS, P_ROWS - P_FF_ROWS), (0, P_FF_ROWS)
    started, last = {}, None
    for l in range(nl):
        packed = _pack_weights(ids, l, w["w_in"], w["w_glu"], w["w_out"], w["w_down"], w["w_gate"], w["w_up"],
                               [] if last is None else [last])
        if l == 0:
            first = _ag_start("ag_start_0_mixer", packed, ids, [mixer_rows])
            started[0] = _ag_start("ag_start_0_ffn", first[2], first[3], [ffn_rows])
        else:
            started[l] = _ag_start(f"ag_start_{l}", packed, last, [mixer_rows, ffn_rows])
        last = started[l][3]
    views = [{n: _view(n, d[n]) for n in _SMALL} for d in (w, m, v)]

    passing = {}

    def get_weights(l, after):
        send_sems, recv_sems, buf, _ = started[l]
        if l == 0:
            buf = _ag_wait("ag_wait_0_mixer", first[0], first[1], buf, after + [last], [mixer_rows])
            return _ag_forward(buf, mixer_rows)
        buf = _ag_wait(f"ag_wait_{l}", send_sems, recv_sems, buf, after, [mixer_rows, ffn_rows])
        buf = _ag_forward(buf, mixer_rows)
        passing[l] = _ag_forward_start(f"ag_forward_start_{l}", buf, ffn_rows)
        return passing[l][2]

    def scan_done(l, buf, after):
        if l > 0:
            return buf
        send_sems, recv_sems, _, _ = started[0]
        buf = _ag_wait("ag_wait_0_ffn", send_sems, recv_sems, buf, after, [ffn_rows])
        passing[0] = _ag_forward_start("ag_forward_start_0", buf, ffn_rows)
        return passing[0][2]

    def get_ffn_weights(l, buf, after):
        send_sems, recv_sems, _ = passing[l]
        return _ag_forward_wait(f"ag_forward_wait_{l}", send_sems, recv_sems, buf, after, ffn_rows)

    to_sibling, to_chips, reduced = {}, {}, {}

    def put_grads(l, g):
        to_sibling[l] = _rs_sibling_start(f"rs_sibling_start_{l}", g)
        token = to_sibling[l][4]
        if l + 1 in to_chips:
            finish(l + 1, [token])
        return token[:1, :1]

    def ffn_bwd_done(l, after):
        return send_to_chips(l + 1, after)[:1, :1] if l + 1 in to_sibling else None

    def send_to_chips(l, after):
        send_sems, recv_sems, g, land, _ = to_sibling.pop(l)
        g, land = _rs_sibling_wait(f"rs_sibling_wait_{l}", send_sems, recv_sems, g, land, after)
        own, t = _rs_add("rs_add", ids, g, land, RS_ROW_TILE)
        send_sems, recv_sems, t, land, token = _rs_chips_start(f"rs_chips_start_{l}", t)
        to_chips[l] = (send_sems, recv_sems, t, land, own)
        return token

    def finish(l, after):
        send_sems, recv_sems, t, land, own = to_chips.pop(l)
        land = _rs_chips_wait(f"rs_chips_wait_{l}", send_sems, recv_sems, t, land, after)
        shard = lax.empty((1, P_ROWS, D_MODEL), F32)
        reduced[l] = _rs_exchange_start(f"rs_exchange_start_{l}", _rs_sum(ids, 0, own, land, shard, RS_ROW_TILE))

    loss, grad_x, small = _local_step(x[0], loss_target[0], {n: w[n] for n in _SMALL}, get_weights, scan_done,
                                      get_ffn_weights, ffn_bwd_done, put_grads)
    loss = lax.psum(loss[0, 0], ("x", "y", "c"))
    small_flat = _flatten_small(small)

    groups = ((("w_in", P_IN_BLK), ("w_out", P_OUT_BLK)), (("w_down", P_WD_BLK), ("w_gate", P_WG_BLK), ("w_up", P_WU_BLK)))
    res = {n: None for n in ("w_in", "w_out", "w_down", "w_gate", "w_up", "w_glu")}

    def adamw_layer(l, after):
        send_sems, recv_sems, shard = reduced[l]
        shard = _rs_exchange_wait(f"rs_exchange_wait_{l}", send_sems, recv_sems, shard, after)
        for group, row_tile in zip(groups, (128, 176)):
            names = [n for n, _ in group]
            outs = None if res[names[0]] is None else [res[n] for n in names]
            outs = _adamw_group("adamw_" + names[0], l, *[[d[n] for n in names] for d in (w, m, v)], shard,
                                [blk * idx for _, (blk, idx) in group], row_tile, outs)
            res.update(zip(names, outs))
        blk, idx = P_GLU_BLK
        res["w_glu"] = _adamw("adamw_w_glu", l, w["w_glu"], m["w_glu"], v["w_glu"], shard, (blk, D_MODEL), blk * idx,
                              128, res["w_glu"], (), True)

    if nl > 1:
        adamw_layer(nl - 1, [to_sibling[0][4]])
    token = send_to_chips(0, [small_flat] + [r[0] for r in res.values() if r is not None])
    for l in reversed(range(1, nl - 1)):
        adamw_layer(l, [token])
    updated = [r[0] for r in res.values() if r is not None]
    small_sum = _small_all_reduce(small_flat, [token] + updated)
    finish(0, [small_sum] + updated)
    adamw_layer(0, [])
    for n in t_names:
        res[n] = tuple(tr(a) for a in res[n])
    g_views = _split_small(small_sum, views[0])
    for group in _SMALL_GROUPS:
        deltas, new_ms, new_vs = _adamw_small("adamw_" + group[0], *[[d[n] for n in group] for d in views],
                                              [g_views[n] for n in group])
        for i, n in enumerate(group):
            res[n] = tuple(_unview(n, a, w[n].shape) for a in (g_views[n], deltas[i], new_ms[i], new_vs[i]))

    return (loss, grad_x[None], *[res[n][0] for n in _WEIGHTS], *[res[n][1] for n in _WEIGHTS],
            *[res[n][2] for n in _WEIGHTS], *[res[n][3] for n in _WEIGHTS])
```

```python
import functools
import math

import jax
import jax.numpy as jnp
from jax import lax
from jax.experimental import pallas as pl
from jax.experimental.pallas import tpu as pltpu

F32 = jnp.float32
BF16 = jnp.bfloat16

D_MODEL = 1024
D_POOL = 512
D_SSM = 512
POOL_WINDOWS = (2, 4, 8, 16)
POOL_GROUP = 128
POOL_HALO = 16
N_SSM_GROUPS = 32
SSM_GROUP = 16
SSM_STATE = 64
N_STATE = N_SSM_GROUPS * SSM_STATE
N_PAIRS = N_SSM_GROUPS // 2
D_FF = 2816
N_SHARD = 4
FF_SHARD = D_FF // N_SHARD
RMS_EPS = 1e-6

ADAM_LR = 0.001
ADAM_B1 = 0.9
ADAM_B2 = 0.999
ADAM_EPS = 1e-08
ADAM_WD = 0.01
ADAM_STEP = 10

P_ROWS = 2816
P_WD_BLK = (704, 0)
P_WG_BLK = (704, 1)
P_WU_BLK = (704, 2)
P_FF_ROWS = 2112
P_GLU_BLK = (64, 33)
P_GLU_PAD = 192
P_IN_BLK = (256, 9)
P_OUT_BLK = (256, 10)

SUBLANES = 8
VMEM_LIMIT = 56 * 1024 * 1024

TM = 1024
TM_FFN = 512
TM_FFN_LONG = 1024
FFN_SPLIT = 2
TS = 2048
SCAN_LANES = 512


def _cparams(n_axes):
    return pltpu.CompilerParams(dimension_semantics=("arbitrary",) * n_axes, vmem_limit_bytes=VMEM_LIMIT)


def _dot(a, b):
    return jnp.dot(a, b, preferred_element_type=F32)


def _dot_nt(a, b):
    return lax.dot_general(a, b, (((1,), (1,)), ((), ())), preferred_element_type=F32)


def _dot_tn(a, b):
    return lax.dot_general(a, b, (((0,), (0,)), ((), ())), preferred_element_type=F32)


def _rms_hat(x):
    r = lax.rsqrt(jnp.mean(x * x, axis=-1, keepdims=True) + RMS_EPS)
    return x * r, r


def _rms_bwd(d_hat, xhat, r):
    return r * (d_hat - xhat * jnp.mean(d_hat * xhat, axis=-1, keepdims=True))


def _sigmoid(x):
    return 1.0 / (1.0 + jnp.exp(-x))


_GELU_C = math.sqrt(2.0 / math.pi)
_GELU_K = 0.044715


def _gelu(x):
    return 0.5 * x * (1.0 + jnp.tanh(_GELU_C * (x + _GELU_K * x * x * x)))


def _gelu_grad(x):
    th = jnp.tanh(_GELU_C * (x + _GELU_K * x * x * x))
    return 0.5 * (1.0 + th) + 0.5 * x * (1.0 - th * th) * _GELU_C * (1.0 + 3.0 * _GELU_K * x * x)


def _glu_weight(ref):
    v = ref[...]
    return jnp.concatenate([v[:, :, :D_SSM], v[:, :, D_SSM:]], axis=1).reshape(D_SSM, D_SSM)


def _glu_pack(w):
    v = w.reshape(N_SHARD, 128, D_SSM)
    return jnp.concatenate([v[:, :64, :], v[:, 64:, :]], axis=2)


def _pool_diff(ext, row0, tm):
    rows = row0 + lax.broadcasted_iota(jnp.int32, (tm, 1), 0)
    outs = []
    for gi, w in enumerate(POOL_WINDOWS):
        e = ext[:, gi * POOL_GROUP:(gi + 1) * POOL_GROUP]
        s = e
        k = 1
        while k < w:
            s = s + pltpu.roll(s, k, 0)
            k *= 2
        inv = 1.0 / jnp.minimum(rows + 1, w).astype(F32)
        outs.append(s[POOL_HALO:, :] * inv - e[POOL_HALO:, :])
    return outs


def _mix_in_fwd(h, g1, wp, layer, w_pool, scale):
    L = h.shape[0]
    tm = min(TM, L)

    def body(h_ref, g_ref, w_ref, wp_ref, sc_ref, u_ref, yp_ref, carry):
        i = pl.program_id(0)

        @pl.when(i == 0)
        def _():
            carry[...] = jnp.zeros_like(carry)

        xhat, _ = _rms_hat(h_ref[...])
        n1 = (xhat * g_ref[...]).astype(BF16)
        u = _dot(n1, w_ref[...].reshape(D_MODEL, D_MODEL))
        u_ref[...] = u
        up = u[:, :D_POOL]
        ext = jnp.concatenate([carry[...], up], axis=0)
        carry[...] = up[tm - POOL_HALO:, :]
        diffs = _pool_diff(ext, i * tm, tm)
        for gi in range(4):
            cols = slice(gi * POOL_GROUP, (gi + 1) * POOL_GROUP)
            yp_ref[:, cols] = _dot(diffs[gi].astype(BF16), wp_ref[gi]) * sc_ref[:, cols]

    blk, idx = P_IN_BLK
    return pl.pallas_call(
        body, name="mix_in_fwd", grid=(L // tm,),
        in_specs=[pl.BlockSpec((tm, D_MODEL), lambda i: (i, 0)),
                  pl.BlockSpec((None, 1, D_MODEL), lambda i: (layer, 0, 0)),
                  pl.BlockSpec((N_SHARD, None, blk, D_MODEL), lambda i: (0, 0, idx, 0)),
                  pl.BlockSpec((None, 4, POOL_GROUP, POOL_GROUP), lambda i: (layer, 0, 0, 0)),
                  pl.BlockSpec((None, 1, D_POOL), lambda i: (layer, 0, 0))],
        out_specs=[pl.BlockSpec((tm, D_MODEL), lambda i: (i, 0)),
                   pl.BlockSpec((tm, D_POOL), lambda i: (i, 0))],
        out_shape=[jax.ShapeDtypeStruct((L, D_MODEL), F32), jax.ShapeDtypeStruct((L, D_POOL), F32)],
        scratch_shapes=[pltpu.VMEM((POOL_HALO, D_POOL), F32)],
        compiler_params=_cparams(1),
    )(h, g1, wp, w_pool, scale)


def _cmul(xr, xi, yr, yi):
    return xr * yr - xi * yi, xr * yi + xi * yr


SCAN_BLOCK = 64
N_SCAN_TABLES = 26


def _permute_rows(src, dst, n_rows):
    for b in range(n_rows // SCAN_BLOCK):
        for tau in range(SUBLANES):
            dst[pl.ds(SCAN_BLOCK * b + SUBLANES * tau, SUBLANES), :] = (
                src[pl.ds(SCAN_BLOCK * b + tau, SUBLANES, stride=SUBLANES), :])


def _scan_tables(ar, ai, tab, reverse):
    c = ar.shape[1]
    row = lax.broadcasted_iota(jnp.int32, (SUBLANES, c), 0)
    zero = jnp.zeros((SUBLANES, c), F32)
    full = lambda v: jnp.broadcast_to(v, (SUBLANES, c))
    pw = [(ar, ai)]
    for _ in range(SUBLANES - 1):
        pw.append(_cmul(*pw[-1], ar, ai))
    a8 = pw[-1]
    a16 = _cmul(*a8, *a8)
    a32 = _cmul(*a16, *a16)
    tab[0] = full(ar)
    tab[1] = full(ai)
    for n, (s, (pr, pi)) in enumerate(((1, a8), (2, a16), (4, a32))):
        keep = (row < SUBLANES - s) if reverse else (row >= s)
        tab[2 + 2 * n] = jnp.where(keep, pr, zero)
        tab[3 + 2 * n] = jnp.where(keep, pi, zero)
    cur = a8
    qr, qi = zero, zero
    for n in range(SUBLANES):
        at = (SUBLANES - 1 - n) if reverse else n
        qr = jnp.where(row == at, cur[0], qr)
        qi = jnp.where(row == at, cur[1], qi)
        cur = _cmul(*cur, *a8)
    tab[8] = qr
    tab[9] = qi
    for tau in range(SUBLANES):
        pr, pi = pw[SUBLANES - 1 - tau] if reverse else pw[tau]
        tab[10 + 2 * tau] = full(pr)
        tab[11 + 2 * tau] = full(pi)


def _cmac(xr, xi, ar, ai, yr, yi):
    return xr + ar * yr - ai * yi, xi + ar * yi + ai * yr


def _chain_segments(er, ei, c_r, c_i, tab, cols, reverse):
    tr, ti = er, ei
    for n, s in enumerate((1, 2, 4)):
        shift = SUBLANES - s if reverse else s
        tr, ti = _cmac(tr, ti, tab[2 + 2 * n, :, cols], tab[3 + 2 * n, :, cols],
                       pltpu.roll(tr, shift, 0), pltpu.roll(ti, shift, 0))
    return _cmac(tr, ti, tab[8, :, cols], tab[9, :, cols], c_r, c_i)


def _ssm_fwd(u, layer, bpad, cpad, ar, ai, dskip):
    L = u.shape[0]
    ts = min(TS, L)
    nq = 4
    cq = N_STATE // nq

    def body(u_ref, bp_ref, cp_ref, ar_ref, ai_ref, dsk_ref, sre_ref, sim_ref, y_ref, cr, ci, tab, up, yp):
        t = pl.program_id(1)

        @pl.when(t == 0)
        def _():
            cr[...] = jnp.zeros_like(cr)
            ci[...] = jnp.zeros_like(ci)
            _scan_tables(ar_ref[...], ai_ref[...], tab, reverse=False)

        _permute_rows(u_ref, up, ts)
        uf = up[...]
        ub = uf.astype(BF16)
        for jj in range(4):
            bu = _dot(ub, bp_ref[jj])
            sre_ref[:, jj * 128:(jj + 1) * 128] = bu[:, :128]
            sim_ref[:, jj * 128:(jj + 1) * 128] = bu[:, 128:]

        shp = (SUBLANES, SCAN_LANES)
        first_row = lax.broadcasted_iota(jnp.int32, shp, 0) == 0
        for cc in range(cq // SCAN_LANES):
            cols = slice(cc * SCAN_LANES, (cc + 1) * SCAN_LANES)

            def block(b, carry, cols=cols):
                c_r, c_i = carry
                base = pl.multiple_of(b * SCAN_BLOCK, SCAN_BLOCK)
                rows = lambda tau: pl.ds(base + SUBLANES * tau, SUBLANES)
                a_r, a_i = tab[0, :, cols], tab[1, :, cols]
                ys = [(sre_ref[rows(0), cols], sim_ref[rows(0), cols])]
                for tau in range(1, SUBLANES):
                    ys.append(_cmac(sre_ref[rows(tau), cols], sim_ref[rows(tau), cols], a_r, a_i, *ys[-1]))
                tr, ti = _chain_segments(*ys[-1], c_r, c_i, tab, cols, reverse=False)
                in_r = jnp.where(first_row, c_r, pltpu.roll(tr, 1, 0))
                in_i = jnp.where(first_row, c_i, pltpu.roll(ti, 1, 0))
                for tau in range(SUBLANES):
                    sr, si = _cmac(*ys[tau], tab[10 + 2 * tau, :, cols], tab[11 + 2 * tau, :, cols], in_r, in_i)
                    sre_ref[rows(tau), cols] = sr
                    sim_ref[rows(tau), cols] = si
                return (jnp.broadcast_to(tr[SUBLANES - 1:, :], shp), jnp.broadcast_to(ti[SUBLANES - 1:, :], shp))

            c_r, c_i = lax.fori_loop(0, ts // SCAN_BLOCK, block, (cr[:, cols], ci[:, cols]), unroll=2)
            cr[:, cols] = c_r
            ci[:, cols] = c_i

        acc = dsk_ref[...] * uf
        for jj in range(4):
            cols = slice(jj * 128, (jj + 1) * 128)
            scat = jnp.concatenate([sre_ref[:, cols], sim_ref[:, cols]], axis=1).astype(BF16)
            acc = acc + _dot(scat, cp_ref[jj])
        yp[...] = acc
        _permute_rows(yp, y_ref, ts)

    return pl.pallas_call(
        body, name="ssm_fwd", grid=(nq, L // ts),
        in_specs=[pl.BlockSpec((ts, 128), lambda q, t: (t, 4 + q)),
                  pl.BlockSpec((None, 4, 128, 256), lambda q, t: (layer, q, 0, 0)),
                  pl.BlockSpec((None, 4, 256, 128), lambda q, t: (layer, q, 0, 0)),
                  pl.BlockSpec((None, 1, cq), lambda q, t: (layer, 0, q)),
                  pl.BlockSpec((None, 1, cq), lambda q, t: (layer, 0, q)),
                  pl.BlockSpec((None, 1, 128), lambda q, t: (layer, 0, q))],
        out_specs=[pl.BlockSpec((ts, cq), lambda q, t: (t, q)),
                   pl.BlockSpec((ts, cq), lambda q, t: (t, q)),
                   pl.BlockSpec((ts, 128), lambda q, t: (t, q))],
        out_shape=[jax.ShapeDtypeStruct((L, N_STATE), F32), jax.ShapeDtypeStruct((L, N_STATE), F32),
                   jax.ShapeDtypeStruct((L, D_SSM), F32)],
        scratch_shapes=[pltpu.VMEM((SUBLANES, cq), F32), pltpu.VMEM((SUBLANES, cq), F32),
                        pltpu.VMEM((N_SCAN_TABLES, SUBLANES, cq), F32),
                        pltpu.VMEM((ts, 128), F32), pltpu.VMEM((ts, 128), F32)],
        compiler_params=_cparams(2),
    )(u, bpad, cpad, ar, ai, dskip)


def _mix_out_fwd(yraw, ypool, h, wp, layer, b_glu):
    L = h.shape[0]
    tm = min(TM, L)

    def body(yr_ref, yp_ref, h_ref, wglu_ref, b_ref, wout_ref, o_ref):
        y = _gelu(yr_ref[...])
        z = _dot(y.astype(BF16), _glu_weight(wglu_ref)) + b_ref[...]
        o = y * _sigmoid(z)
        mix = jnp.concatenate([yp_ref[...], o], axis=1).astype(BF16)
        o_ref[...] = h_ref[...] + _dot(mix, wout_ref[...].reshape(D_MODEL, D_MODEL))

    gb, gi = P_GLU_BLK
    ob, oi = P_OUT_BLK
    return pl.pallas_call(
        body, name="mix_out_fwd", grid=(L // tm,),
        in_specs=[pl.BlockSpec((tm, D_SSM), lambda i: (i, 0)),
                  pl.BlockSpec((tm, D_POOL), lambda i: (i, 0)),
                  pl.BlockSpec((tm, D_MODEL), lambda i: (i, 0)),
                  pl.BlockSpec((N_SHARD, None, gb, D_MODEL), lambda i: (0, 0, gi, 0)),
                  pl.BlockSpec((None, 1, D_SSM), lambda i: (layer, 0, 0)),
                  pl.BlockSpec((N_SHARD, None, ob, D_MODEL), lambda i: (0, 0, oi, 0))],
        out_specs=pl.BlockSpec((tm, D_MODEL), lambda i: (i, 0)),
        out_shape=jax.ShapeDtypeStruct((L, D_MODEL), F32),
        compiler_params=_cparams(1),
    )(yraw, ypool, h, wp, b_glu, wp)


def _ffn_weights(ref, k):
    return ref[k, 0:FF_SHARD, :], ref[k, FF_SHARD:2 * FF_SHARD, :], ref[k, 2 * FF_SHARD:P_FF_ROWS, :]


def _ffn_weight_spec():
    return pl.BlockSpec((N_SHARD, None, P_FF_ROWS, D_MODEL), lambda m, k: (0, 0, 0, 0),
                        pipeline_mode=pl.Buffered(1))


def _ffn_fwd(h, g2, wp, layer):
    L = h.shape[0]
    tm = min(TM_FFN_LONG, L)

    def body(h_ref, g_ref, w_ref, o_ref, n2_ref, act_ref, dgate_ref, dup_ref):
        k = pl.program_id(1)

        @pl.when(k == 0)
        def _():
            x = h_ref[...]
            xhat, _ = _rms_hat(x)
            n2_ref[...] = (xhat * g_ref[...]).astype(BF16)
            o_ref[...] = x

        wd, wg_t, wu_t = _ffn_weights(w_ref, k)
        n2 = n2_ref[...]
        gate = _dot_nt(n2, wg_t)
        up = _dot_nt(n2, wu_t)
        sg = _sigmoid(gate)
        silu = gate * sg
        act = (silu * up).astype(BF16)
        act_ref[...] = act
        dgate_ref[...] = (up * (sg * (1.0 + gate * (1.0 - sg)))).astype(BF16)
        dup_ref[...] = silu.astype(BF16)
        o_ref[...] += _dot(act, wd)

    act_shape = jax.ShapeDtypeStruct((N_SHARD, L, FF_SHARD), BF16)
    return pl.pallas_call(
        body, name="ffn_fwd", grid=(L // tm, N_SHARD),
        in_specs=[pl.BlockSpec((tm, D_MODEL), lambda m, k: (m, 0)),
                  pl.BlockSpec((None, 1, D_MODEL), lambda m, k: (layer, 0, 0)),
                  _ffn_weight_spec()],
        out_specs=[pl.BlockSpec((tm, D_MODEL), lambda m, k: (m, 0)),
                   pl.BlockSpec((tm, D_MODEL), lambda m, k: (m, 0)),
                   pl.BlockSpec((None, tm, FF_SHARD), lambda m, k: (k, m, 0)),
                   pl.BlockSpec((None, tm, FF_SHARD), lambda m, k: (k, m, 0)),
                   pl.BlockSpec((None, tm, FF_SHARD), lambda m, k: (k, m, 0))],
        out_shape=[jax.ShapeDtypeStruct((L, D_MODEL), F32), jax.ShapeDtypeStruct((L, D_MODEL), BF16),
                   act_shape, act_shape, act_shape],
        compiler_params=_cparams(2),
    )(h, g2, wp)


def _final_fwd_bwd(h, gf, target):
    L = h.shape[0]
    tm = min(TM, L)

    def body(h_ref, g_ref, t_ref, dh_ref, loss_ref, dg_ref):
        i = pl.program_id(0)

        @pl.when(i == 0)
        def _():
            loss_ref[...] = jnp.zeros_like(loss_ref)
            dg_ref[...] = jnp.zeros_like(dg_ref)

        xhat, r = _rms_hat(h_ref[...])
        g = g_ref[...]
        e = xhat * g - t_ref[...]
        loss_ref[...] += 0.5 * jnp.sum(jnp.mean(e * e, axis=-1, keepdims=True), axis=0, keepdims=True)
        dy = e * (1.0 / D_MODEL)
        dg_ref[...] += jnp.sum(dy * xhat, axis=0, keepdims=True)
        dh_ref[...] = _rms_bwd(dy * g, xhat, r)

    return pl.pallas_call(
        body, name="final_fwd_bwd", grid=(L // tm,),
        in_specs=[pl.BlockSpec((tm, D_MODEL), lambda i: (i, 0)),
                  pl.BlockSpec((1, D_MODEL), lambda i: (0, 0)),
                  pl.BlockSpec((tm, D_MODEL), lambda i: (i, 0))],
        out_specs=[pl.BlockSpec((tm, D_MODEL), lambda i: (i, 0)),
                   pl.BlockSpec((1, 1), lambda i: (0, 0)),
                   pl.BlockSpec((1, D_MODEL), lambda i: (0, 0))],
        out_shape=[jax.ShapeDtypeStruct((L, D_MODEL), F32), jax.ShapeDtypeStruct((1, 1), F32),
                   jax.ShapeDtypeStruct((1, D_MODEL), F32)],
        compiler_params=_cparams(1),
    )(h, gf, target)


def _ffn_bwd_act(dh, h, g2, fgate_s, fup_s, wp, layer):
    L = h.shape[0]
    tm = min(TM_FFN, L)
    sub = tm // FFN_SPLIT

    def body(dh_ref, h_ref, g_ref, fgate_ref, fup_ref, w_ref,
             dhm_ref, dg_ref, dgate_ref, dup_ref, dhb_ref):
        m, k = pl.program_id(0), pl.program_id(1)
        dn2 = dhm_ref

        @pl.when(jnp.logical_and(m == 0, k == 0))
        def _():
            dg_ref[...] = jnp.zeros_like(dg_ref)

        @pl.when(k == 0)
        def _():
            dhb_ref[...] = dh_ref[...].astype(BF16)
            dn2[...] = jnp.zeros_like(dn2)

        wd, wg_t, wu_t = _ffn_weights(w_ref, k)
        for rows in (slice(r * sub, (r + 1) * sub) for r in range(tm // sub)):
            dact = _dot_nt(dhb_ref[rows, :], wd)
            dgate = (dact * fgate_ref[rows, :].astype(F32)).astype(BF16)
            dup = (dact * fup_ref[rows, :].astype(F32)).astype(BF16)
            dgate_ref[rows, :] = dgate
            dup_ref[rows, :] = dup
            dn2[rows, :] += _dot(dgate, wg_t) + _dot(dup, wu_t)

        @pl.when(k == N_SHARD - 1)
        def _():
            xhat, r = _rms_hat(h_ref[...])
            d = dn2[...]
            dg_ref[...] += jnp.sum(d * xhat, axis=0, keepdims=True)
            dhm_ref[...] = dh_ref[...] + _rms_bwd(d * g_ref[...], xhat, r)

    act_spec = pl.BlockSpec((None, tm, FF_SHARD), lambda m, k: (k, m, 0))
    act_shape = jax.ShapeDtypeStruct((N_SHARD, L, FF_SHARD), BF16)
    row_spec = pl.BlockSpec((tm, D_MODEL), lambda m, k: (m, 0))
    return pl.pallas_call(
        body, name="ffn_bwd_act", grid=(L // tm, N_SHARD),
        in_specs=[row_spec, row_spec,
                  pl.BlockSpec((None, 1, D_MODEL), lambda m, k: (layer, 0, 0)),
                  act_spec, act_spec,
                  _ffn_weight_spec()],
        out_specs=[row_spec,
                   pl.BlockSpec((1, D_MODEL), lambda m, k: (0, 0)),
                   act_spec, act_spec, row_spec],
        out_shape=[jax.ShapeDtypeStruct((L, D_MODEL), F32), jax.ShapeDtypeStruct((1, D_MODEL), F32),
                   act_shape, act_shape, jax.ShapeDtypeStruct((L, D_MODEL), BF16)],
        compiler_params=_cparams(2),
    )(dh, h, g2, fgate_s, fup_s, wp)


def _ffn_bwd_w(n2, dgate_s, dup_s, act_s, dhb, gbuf):
    L = n2.shape[0]
    tm = min(TM_FFN_LONG, L)

    def body(n2_ref, dgate_ref, dup_ref, act_ref, dhb_ref, g_in, g_ref):
        m = pl.program_id(1)

        @pl.when(m == 0)
        def _():
            g_ref[...] = jnp.zeros_like(g_ref)

        n2v = n2_ref[...]
        g_ref[0:FF_SHARD, :] += _dot_tn(act_ref[...], dhb_ref[...])
        g_ref[FF_SHARD:2 * FF_SHARD, :] += _dot_tn(dgate_ref[...], n2v)
        g_ref[2 * FF_SHARD:P_FF_ROWS, :] += _dot_tn(dup_ref[...], n2v)

    act_spec = pl.BlockSpec((None, tm, FF_SHARD), lambda k, m: (k, m, 0))
    row_spec = pl.BlockSpec((tm, D_MODEL), lambda k, m: (m, 0))
    return pl.pallas_call(
        body, name="ffn_bwd_w", grid=(N_SHARD, L // tm),
        in_specs=[row_spec, act_spec, act_spec, act_spec, row_spec, pl.BlockSpec(memory_space=pl.ANY)],
        out_specs=pl.BlockSpec((None, None, P_FF_ROWS, D_MODEL), lambda k, m: (0, k, 0, 0)),
        out_shape=jax.ShapeDtypeStruct(gbuf.shape, F32),
        input_output_aliases={5: 0},
        compiler_params=_cparams(2),
    )(n2, dgate_s, dup_s, act_s, dhb, gbuf)


def _mix_out_bwd(dhm, yraw, ypool, wp, layer, b_glu, gbuf):
    L = dhm.shape[0]
    tm = min(TM, L)

    def body(dhm_ref, yr_ref, yp_ref, wglu_ref, b_ref, wout_ref, g1_in,
             dyr_ref, dyp_ref, db_ref, g1_ref, dwout, dwglu, gpack):
        i = pl.program_id(0)

        @pl.when(i == 0)
        def _():
            db_ref[...] = jnp.zeros_like(db_ref)
            dwout[...] = jnp.zeros_like(dwout)
            dwglu[...] = jnp.zeros_like(dwglu)

        dhb = dhm_ref[...].astype(BF16)
        wglu = _glu_weight(wglu_ref)
        dmix = _dot_nt(dhb, wout_ref[...].reshape(D_MODEL, D_MODEL))
        dyp_ref[...] = dmix[:, :D_POOL]
        d_o = dmix[:, D_POOL:]
        yraw_v = yr_ref[...]
        y = _gelu(yraw_v)
        yb = y.astype(BF16)
        sig = _sigmoid(_dot(yb, wglu) + b_ref[...])
        mix = jnp.concatenate([yp_ref[...], y * sig], axis=1).astype(BF16)
        dwout[...] += _dot_tn(mix, dhb).reshape(N_SHARD, 256, D_MODEL)
        dz = d_o * y * sig * (1.0 - sig)
        dzb = dz.astype(BF16)
        db_ref[...] += jnp.sum(dz, axis=0, keepdims=True)
        dwglu[...] += _dot_tn(yb, dzb)
        dy = d_o * sig + _dot_nt(dzb, wglu)
        dyr_ref[...] = dy * _gelu_grad(yraw_v)

        @pl.when(i == n_steps - 1)
        def _():
            gpack[:, :gb, :] = _glu_pack(dwglu[...])
            gpack[:, gb:, :] = jnp.zeros((N_SHARD, P_GLU_PAD - gb, D_MODEL), F32)
            pltpu.sync_copy(gpack, g1_ref.at[0, :, pl.ds(gb * gi, P_GLU_PAD), :])
            pltpu.sync_copy(dwout, g1_ref.at[0, :, pl.ds(ob * oi, ob), :])

    gb, gi = P_GLU_BLK
    ob, oi = P_OUT_BLK
    n_steps = L // tm
    return pl.pallas_call(
        body, name="mix_out_bwd", grid=(n_steps,),
        in_specs=[pl.BlockSpec((tm, D_MODEL), lambda i: (i, 0)),
                  pl.BlockSpec((tm, D_SSM), lambda i: (i, 0)),
                  pl.BlockSpec((tm, D_POOL), lambda i: (i, 0)),
                  pl.BlockSpec((N_SHARD, None, gb, D_MODEL), lambda i: (0, 0, gi, 0)),
                  pl.BlockSpec((None, 1, D_SSM), lambda i: (layer, 0, 0)),
                  pl.BlockSpec((N_SHARD, None, ob, D_MODEL), lambda i: (0, 0, oi, 0)),
                  pl.BlockSpec(memory_space=pl.ANY)],
        out_specs=[pl.BlockSpec((tm, D_SSM), lambda i: (i, 0)),
                   pl.BlockSpec((tm, D_POOL), lambda i: (i, 0)),
                   pl.BlockSpec((1, D_SSM), lambda i: (0, 0)),
                   pl.BlockSpec(memory_space=pl.ANY)],
        out_shape=[jax.ShapeDtypeStruct((L, D_SSM), F32), jax.ShapeDtypeStruct((L, D_POOL), F32),
                   jax.ShapeDtypeStruct((1, D_SSM), F32),
                   jax.ShapeDtypeStruct(gbuf.shape, F32)],
        scratch_shapes=[pltpu.VMEM((N_SHARD, ob, D_MODEL), F32), pltpu.VMEM((D_SSM, D_SSM), F32),
                        pltpu.VMEM((N_SHARD, P_GLU_PAD, D_MODEL), F32)],
        input_output_aliases={6: 3},
        compiler_params=_cparams(1),
    )(dhm, yraw, ypool, wp, b_glu, wp, gbuf)


def _ssm_bwd(dyraw, u, sre, sim, layer, cpad_t, bpad_t, ar, ai, dskip):
    L = u.shape[0]
    ts = min(TS, L)
    nt = L // ts
    nq = 4
    cq = N_STATE // nq

    def body(dy_ref, u_ref, sre_ref, sim_ref, ct_ref, bt_ref, ar_ref, ai_ref, dsk_ref,
             du_ref, dcp_ref, dbp_ref, dar_ref, dai_ref, ddsk_ref, gre, gim, cr, ci, tab, accr, acci, up, dyp):
        t = pl.program_id(1)

        @pl.when(t == 0)
        def _():
            for ref in (cr, ci, accr, acci, dcp_ref, dbp_ref, ddsk_ref):
                ref[...] = jnp.zeros_like(ref)
            _scan_tables(ar_ref[...], -ai_ref[...], tab, reverse=True)

        _permute_rows(dy_ref, dyp, ts)
        _permute_rows(u_ref, up, ts)
        dy = dyp[...]
        dyb = dy.astype(BF16)
        uf = up[...]
        ub = uf.astype(BF16)
        for jj in range(4):
            cols = slice(jj * 128, (jj + 1) * 128)
            ds = _dot(dyb, ct_ref[jj])
            gre[:, cols] = ds[:, :128]
            gim[:, cols] = ds[:, 128:]
            scat = jnp.concatenate([sre_ref[:, cols], sim_ref[:, cols]], axis=1).astype(BF16)
            dcp_ref[jj] += _dot_tn(scat, dyb)

        n_blk = ts // SCAN_BLOCK
        shp = (SUBLANES, SCAN_LANES)
        last_row = lax.broadcasted_iota(jnp.int32, shp, 0) == SUBLANES - 1
        for cc in range(cq // SCAN_LANES):
            cols = slice(cc * SCAN_LANES, (cc + 1) * SCAN_LANES)

            def block(i, carry, cols=cols):
                c_r, c_i, a_r, a_i = carry
                base = pl.multiple_of((n_blk - 1 - i) * SCAN_BLOCK, SCAN_BLOCK)
                rows = lambda tau: pl.ds(base + SUBLANES * tau, SUBLANES)
                m_r, m_i = tab[0, :, cols], tab[1, :, cols]
                ys = [None] * SUBLANES
                ys[SUBLANES - 1] = (gre[rows(SUBLANES - 1), cols], gim[rows(SUBLANES - 1), cols])
                for tau in reversed(range(SUBLANES - 1)):
                    ys[tau] = _cmac(gre[rows(tau), cols], gim[rows(tau), cols], m_r, m_i, *ys[tau + 1])
                tr, ti = _chain_segments(*ys[0], c_r, c_i, tab, cols, reverse=True)
                in_r = jnp.where(last_row, c_r, pltpu.roll(tr, SUBLANES - 1, 0))
                in_i = jnp.where(last_row, c_i, pltpu.roll(ti, SUBLANES - 1, 0))
                gs = [_cmac(*ys[tau], tab[10 + 2 * tau, :, cols], tab[11 + 2 * tau, :, cols], in_r, in_i)
                      for tau in range(SUBLANES)]
                for tau in range(SUBLANES):
                    gre[rows(tau), cols] = gs[tau][0]
                    gim[rows(tau), cols] = gs[tau][1]
                    if tau < SUBLANES - 1:
                        nr, ni = gs[tau + 1]
                    else:
                        nr = jnp.where(last_row, c_r, pltpu.roll(gs[0][0], SUBLANES - 1, 0))
                        ni = jnp.where(last_row, c_i, pltpu.roll(gs[0][1], SUBLANES - 1, 0))
                    sr, si = sre_ref[rows(tau), cols], sim_ref[rows(tau), cols]
                    a_r = a_r + sr * nr + si * ni
                    a_i = a_i + sr * ni - si * nr
                return (jnp.broadcast_to(tr[:1, :], shp), jnp.broadcast_to(ti[:1, :], shp), a_r, a_i)

            c_r, c_i, a_r, a_i = lax.fori_loop(
                0, n_blk, block, (cr[:, cols], ci[:, cols], accr[:, cols], acci[:, cols]), unroll=2)
            cr[:, cols] = c_r
            ci[:, cols] = c_i
            accr[:, cols] = a_r
            acci[:, cols] = a_i

        acc = dsk_ref[...] * dy
        for jj in range(4):
            cols = slice(jj * 128, (jj + 1) * 128)
            gcat = jnp.concatenate([gre[:, cols], gim[:, cols]], axis=1).astype(BF16)
            acc = acc + _dot(gcat, bt_ref[jj])
            dbp_ref[jj] += _dot_tn(ub, gcat)
        ddsk_ref[...] += jnp.sum(dy * uf, axis=0, keepdims=True)
        dyp[...] = acc
        _permute_rows(dyp, du_ref, ts)

        @pl.when(t == nt - 1)
        def _():
            dar_ref[...] = jnp.sum(accr[...], axis=0, keepdims=True)
            dai_ref[...] = jnp.sum(acci[...], axis=0, keepdims=True)

    f32_scr = lambda *s: pltpu.VMEM(s, F32)
    return pl.pallas_call(
        body, name="ssm_bwd", grid=(nq, nt),
        in_specs=[pl.BlockSpec((ts, 128), lambda q, t: (nt - 1 - t, q)),
                  pl.BlockSpec((ts, 128), lambda q, t: (nt - 1 - t, 4 + q)),
                  pl.BlockSpec((ts, cq), lambda q, t: (nt - 1 - t, q)),
                  pl.BlockSpec((ts, cq), lambda q, t: (nt - 1 - t, q)),
                  pl.BlockSpec((None, 4, 128, 256), lambda q, t: (layer, q, 0, 0)),
                  pl.BlockSpec((None, 4, 256, 128), lambda q, t: (layer, q, 0, 0)),
                  pl.BlockSpec((None, 1, cq), lambda q, t: (layer, 0, q)),
                  pl.BlockSpec((None, 1, cq), lambda q, t: (layer, 0, q)),
                  pl.BlockSpec((None, 1, 128), lambda q, t: (layer, 0, q))],
        out_specs=[pl.BlockSpec((ts, 128), lambda q, t: (nt - 1 - t, q)),
                   pl.BlockSpec((4, 256, 128), lambda q, t: (q, 0, 0)),
                   pl.BlockSpec((4, 128, 256), lambda q, t: (q, 0, 0)),
                   pl.BlockSpec((1, cq), lambda q, t: (0, q)),
                   pl.BlockSpec((1, cq), lambda q, t: (0, q)),
                   pl.BlockSpec((1, 128), lambda q, t: (0, q))],
        out_shape=[jax.ShapeDtypeStruct((L, D_SSM), F32),
                   jax.ShapeDtypeStruct((N_PAIRS, 256, 128), F32), jax.ShapeDtypeStruct((N_PAIRS, 128, 256), F32),
                   jax.ShapeDtypeStruct((1, N_STATE), F32), jax.ShapeDtypeStruct((1, N_STATE), F32),
                   jax.ShapeDtypeStruct((1, D_SSM), F32)],
        scratch_shapes=[f32_scr(ts, cq), f32_scr(ts, cq), f32_scr(SUBLANES, cq), f32_scr(SUBLANES, cq),
                        f32_scr(N_SCAN_TABLES, SUBLANES, cq), f32_scr(SUBLANES, cq), f32_scr(SUBLANES, cq),
                        f32_scr(ts, 128), f32_scr(ts, 128)],
        compiler_params=_cparams(2),
    )(dyraw, u, sre, sim, cpad_t, bpad_t, ar, ai, dskip)


def _pool_bwd(dyp, u, layer, w_pool, scale):
    L = u.shape[0]
    tm = min(TM, L)
    nt = L // tm
    halo_per_tile = tm // POOL_HALO

    def body(dyp_ref, u_ref, halo_ref, wp_ref, sc_ref, du_ref, dwp_ref, dsc_ref, carry):
        i = pl.program_id(0)
        tile = nt - 1 - i

        @pl.when(i == 0)
        def _():
            carry[...] = jnp.zeros_like(carry)
            dwp_ref[...] = jnp.zeros_like(dwp_ref)
            dsc_ref[...] = jnp.zeros_like(dsc_ref)

        up = u_ref[...]
        halo = jnp.where(tile > 0, halo_ref[...], jnp.zeros_like(halo_ref))
        diffs = _pool_diff(jnp.concatenate([halo, up], axis=0), tile * tm, tm)
        rows = tile * tm + lax.broadcasted_iota(jnp.int32, (tm, 1), 0)
        n_ext = tm + POOL_HALO
        for gi, w in enumerate(POOL_WINDOWS):
            cols = slice(gi * POOL_GROUP, (gi + 1) * POOL_GROUP)
            db = diffs[gi].astype(BF16)
            dyp = dyp_ref[:, cols]
            dsc_ref[:, cols] += jnp.sum(dyp * _dot(db, wp_ref[gi]), axis=0, keepdims=True)
            dp = (dyp * sc_ref[:, cols]).astype(BF16)
            ddiff = _dot_nt(dp, wp_ref[gi])
            dwp_ref[gi] += _dot_tn(db, dp)
            e = ddiff * (1.0 / jnp.minimum(rows + 1, w).astype(F32))
            s = jnp.concatenate([e, carry[:, cols]], axis=0)
            k = 1
            while k < w:
                s = s + pltpu.roll(s, n_ext - k, 0)
                k *= 2
            du_ref[:, cols] = s[:tm, :] - ddiff
            carry[:, cols] = e[:POOL_HALO, :]

    return pl.pallas_call(
        body, name="pool_bwd", grid=(nt,),
        in_specs=[pl.BlockSpec((tm, D_POOL), lambda i: (nt - 1 - i, 0)),
                  pl.BlockSpec((tm, D_POOL), lambda i: (nt - 1 - i, 0)),
                  pl.BlockSpec((POOL_HALO, D_POOL), lambda i: (jnp.maximum((nt - 1 - i) * halo_per_tile - 1, 0), 0)),
                  pl.BlockSpec((None, 4, POOL_GROUP, POOL_GROUP), lambda i: (layer, 0, 0, 0)),
                  pl.BlockSpec((None, 1, D_POOL), lambda i: (layer, 0, 0))],
        out_specs=[pl.BlockSpec((tm, D_POOL), lambda i: (nt - 1 - i, 0)),
                   pl.BlockSpec((4, POOL_GROUP, POOL_GROUP), lambda i: (0, 0, 0)),
                   pl.BlockSpec((1, D_POOL), lambda i: (0, 0))],
        out_shape=[jax.ShapeDtypeStruct((L, D_POOL), F32),
                   jax.ShapeDtypeStruct((4, POOL_GROUP, POOL_GROUP), F32),
                   jax.ShapeDtypeStruct((1, D_POOL), F32)],
        scratch_shapes=[pltpu.VMEM((POOL_HALO, D_POOL), F32)],
        compiler_params=_cparams(1),
    )(dyp, u, u, w_pool, scale)


def _mix_in_bwd(dup, dus, h, dhm, g1, wp, layer, gbuf):
    L = h.shape[0]
    tm = min(TM, L)
    n_steps = L // tm
    blk, idx = P_IN_BLK

    def body(dup_ref, dus_ref, h_ref, dhm_ref, g_ref, w_ref, g1_in, dh_ref, dg_ref, g1_ref, dwin):
        i = pl.program_id(0)

        @pl.when(i == 0)
        def _():
            dg_ref[...] = jnp.zeros_like(dg_ref)
            dwin[...] = jnp.zeros_like(dwin)

        du = jnp.concatenate([dup_ref[...], dus_ref[...]], axis=1).astype(BF16)
        dn1 = _dot_nt(du, w_ref[...].reshape(D_MODEL, D_MODEL))
        xhat, r = _rms_hat(h_ref[...])
        g = g_ref[...]
        n1 = (xhat * g).astype(BF16)
        dwin[...] += _dot_tn(n1, du).reshape(N_SHARD, blk, D_MODEL)
        dg_ref[...] += jnp.sum(dn1 * xhat, axis=0, keepdims=True)
        dh_ref[...] = dhm_ref[...] + _rms_bwd(dn1 * g, xhat, r)

        @pl.when(i == n_steps - 1)
        def _():
            pltpu.sync_copy(dwin, g1_ref.at[0, :, pl.ds(blk * idx, blk), :])

    row_spec = pl.BlockSpec((tm, D_MODEL), lambda i: (i, 0))
    half_spec = pl.BlockSpec((tm, D_POOL), lambda i: (i, 0))
    return pl.pallas_call(
        body, name="mix_in_bwd", grid=(n_steps,),
        in_specs=[half_spec, half_spec, row_spec, row_spec,
                  pl.BlockSpec((None, 1, D_MODEL), lambda i: (layer, 0, 0)),
                  pl.BlockSpec((N_SHARD, None, blk, D_MODEL), lambda i: (0, 0, idx, 0)),
                  pl.BlockSpec(memory_space=pl.ANY)],
        out_specs=[row_spec, pl.BlockSpec((1, D_MODEL), lambda i: (0, 0)), pl.BlockSpec(memory_space=pl.ANY)],
        out_shape=[jax.ShapeDtypeStruct((L, D_MODEL), F32), jax.ShapeDtypeStruct((1, D_MODEL), F32),
                   jax.ShapeDtypeStruct(gbuf.shape, F32)],
        scratch_shapes=[pltpu.VMEM((N_SHARD, blk, D_MODEL), F32)],
        input_output_aliases={6: 2},
        compiler_params=_cparams(1),
    )(dup, dus, h, dhm, g1, wp, gbuf)


def _disc_math(lr, li, ldt, br_t, bi_t):
    dt = jnp.exp(ldt)
    mag = jnp.exp(lr * dt)
    ang = li * dt
    ar = mag * jnp.cos(ang)
    ai = mag * jnp.sin(ang)
    den = lr * lr + li * li
    nr, ni = ar - 1.0, ai
    cr = (nr * lr + ni * li) / den
    ci = (ni * lr - nr * li) / den
    return ar, ai, cr * br_t - ci * bi_t, cr * bi_t + ci * br_t


def _disc_fwd(lr, li, ldt, br_t, bi_t):
    def body(lr_ref, li_ref, ldt_ref, br_ref, bi_ref, ar_ref, ai_ref, bbr_ref, bbi_ref):
        ar, ai, bbr, bbi = _disc_math(lr_ref[...], li_ref[...], ldt_ref[...], br_ref[...], bi_ref[...])
        ar_ref[...] = ar
        ai_ref[...] = ai
        bbr_ref[...] = bbr
        bbi_ref[...] = bbi

    shapes = [jax.ShapeDtypeStruct(a.shape, F32) for a in (lr, li, br_t, bi_t)]
    return pl.pallas_call(body, name="ssm_disc_fwd", out_shape=shapes,
                          compiler_params=pltpu.CompilerParams(vmem_limit_bytes=VMEM_LIMIT))(lr, li, ldt, br_t, bi_t)


def _disc_bwd(lr, li, ldt, br_t, bi_t, dar, dai, dbbr, dbbi):
    def body(lr_ref, li_ref, ldt_ref, br_ref, bi_ref, dar_ref, dai_ref, dbbr_ref, dbbi_ref,
             dlr_ref, dli_ref, dldt_ref, dbr_ref, dbi_ref):
        prim = (lr_ref[...], li_ref[...], ldt_ref[...], br_ref[...], bi_ref[...])
        _, pullback = jax.vjp(_disc_math, *prim)
        dlr, dli, dldt, dbr, dbi = pullback((dar_ref[...], dai_ref[...], dbbr_ref[...], dbbi_ref[...]))
        dlr_ref[...] = dlr
        dli_ref[...] = dli
        dldt_ref[...] = dldt
        dbr_ref[...] = dbr
        dbi_ref[...] = dbi

    shapes = [jax.ShapeDtypeStruct(a.shape, F32) for a in (lr, li, ldt, br_t, bi_t)]
    return pl.pallas_call(body, name="ssm_disc_bwd", out_shape=shapes,
                          compiler_params=pltpu.CompilerParams(vmem_limit_bytes=VMEM_LIMIT))(
        lr, li, ldt, br_t, bi_t, dar, dai, dbbr, dbbi)


def _pad_pairs(m_re, m_im):
    def blocks(m):
        v = m.transpose(0, 2, 1).reshape(N_PAIRS, 2, SSM_GROUP, SSM_STATE)
        return jnp.einsum("ab,jahp->jahbp", jnp.eye(2, dtype=m.dtype), v).reshape(N_PAIRS, 32, 128)
    both = jnp.concatenate([blocks(m_re), blocks(m_im)], axis=-1)
    place = jax.nn.one_hot(jnp.arange(N_PAIRS) % 4, 4, dtype=both.dtype)
    return jnp.einsum("jk,jrc->jkrc", place, both).reshape(N_PAIRS, 128, 256)


def _unpad_pairs(x):
    place = jax.nn.one_hot(jnp.arange(N_PAIRS) % 4, 4, dtype=x.dtype)
    both = jnp.einsum("jk,jkrc->jrc", place, x.reshape(N_PAIRS, 4, 32, 256))

    def unblock(v):
        v = v.reshape(N_PAIRS, 2, SSM_GROUP, 2, SSM_STATE)
        d = jnp.einsum("ab,jahbp->jahp", jnp.eye(2, dtype=x.dtype), v)
        return d.reshape(N_SSM_GROUPS, SSM_GROUP, SSM_STATE).transpose(0, 2, 1)
    return unblock(both[..., :128]), unblock(both[..., 128:])


def _adamw_math(w, g, m, v):
    m = ADAM_B1 * m + (1.0 - ADAM_B1) * g
    v = ADAM_B2 * v + (1.0 - ADAM_B2) * (g * g)
    m_hat = m / (1.0 - ADAM_B1 ** ADAM_STEP)
    v_hat = v / (1.0 - ADAM_B2 ** ADAM_STEP)
    delta = -ADAM_LR * (m_hat / (jnp.sqrt(v_hat) + ADAM_EPS) + ADAM_WD * w)
    return delta, m, v


def _adamw(name, layer, w, m, v, gbuf, g_block, g_row0, row_tile, outs=None, after=(), glu=False):
    nl, r, c = w.shape
    n_tiles = r // row_tile
    g_rows, g_cols = g_block
    g_tile = g_rows // n_tiles
    g_off = g_row0 // g_tile
    if outs is None:
        outs = [lax.empty(w.shape, F32) for _ in range(4)]

    def body(w_ref, m_ref, v_ref, g_ref, *rest):
        go_ref, d_ref, mo_ref, vo_ref = rest[-4:]
        g = g_ref[...]
        if glu:
            g = jnp.concatenate([g[:, :D_SSM], g[:, D_SSM:]], axis=0)
        delta, mn, vn = _adamw_math(w_ref[...], g, m_ref[...], v_ref[...])
        go_ref[...] = g
        d_ref[...] = delta
        mo_ref[...] = mn
        vo_ref[...] = vn

    w_spec = pl.BlockSpec((None, row_tile, c), lambda j: (layer, j, 0))
    shape = jax.ShapeDtypeStruct(w.shape, F32)
    return pl.pallas_call(
        body, name=name, grid=(n_tiles,),
        in_specs=[w_spec, w_spec, w_spec, pl.BlockSpec((None, g_tile, g_cols), lambda j: (0, g_off + j, 0))]
        + [_ANY] * (4 + len(after)),
        out_specs=[w_spec] * 4,
        out_shape=[shape] * 4,
        input_output_aliases={4: 0, 5: 1, 6: 2, 7: 3},
        compiler_params=_cparams(1),
    )(w, m, v, gbuf, *outs, *after)


def _adamw_group(name, layer, ws, ms, vs, gbuf, g_row0s, row_tile, outs=None):
    k = len(ws)
    nl, r, c = ws[0].shape
    n_tiles = r // row_tile
    if outs is None:
        outs = [[lax.empty(ws[0].shape, F32) for _ in range(4)] for _ in range(k)]

    def body(*refs):
        ins, results = refs[:4 * k], refs[-4 * k:]
        for i in range(k):
            w_ref, m_ref, v_ref, g_ref = (ins[j * k + i] for j in range(4))
            g = g_ref[...]
            delta, mn, vn = _adamw_math(w_ref[...], g, m_ref[...], v_ref[...])
            for ref, val in zip(results[4 * i:4 * i + 4], (g, delta, mn, vn)):
                ref[...] = val

    w_spec = pl.BlockSpec((None, row_tile, c), lambda j: (layer, j, 0))
    g_specs = [pl.BlockSpec((None, row_tile, c), functools.partial(lambda j, off: (0, off + j, 0), off=r0 // row_tile))
               for r0 in g_row0s]
    shape = jax.ShapeDtypeStruct(ws[0].shape, F32)
    flat = pl.pallas_call(
        body, name=name, grid=(n_tiles,),
        in_specs=[w_spec] * (3 * k) + g_specs + [_ANY] * (4 * k),
        out_specs=[w_spec] * (4 * k),
        out_shape=[shape] * (4 * k),
        input_output_aliases={4 * k + i: i for i in range(4 * k)},
        compiler_params=_cparams(1),
    )(*ws, *ms, *vs, *([gbuf] * k), *[a for group in outs for a in group])
    return [flat[4 * i:4 * i + 4] for i in range(k)]


def _pack_weights(ids, layer, w_in, w_glu, w_out, w_down, w_gate_t, w_up_t, after=()):
    gb, gi = P_GLU_BLK
    ib, ii = P_IN_BLK
    ob, oi = P_OUT_BLK

    def body(ids_ref, in_ref, glu_ref, out_ref, dn_ref, gate_ref, up_ref, *rest):
        p_ref = rest[-1]
        p_ref[0:FF_SHARD, :] = dn_ref[...].astype(BF16)
        p_ref[FF_SHARD:2 * FF_SHARD, :] = gate_ref[...].astype(BF16)
        p_ref[2 * FF_SHARD:P_FF_ROWS, :] = up_ref[...].astype(BF16)
        g = glu_ref[...]
        p_ref[gb * gi:gb * (gi + 1), :] = jnp.concatenate([g[:gb, :], g[gb:, :]], axis=1).astype(BF16)
        p_ref[gb * (gi + 1):ib * ii, :] = jnp.zeros((P_GLU_PAD - gb, D_MODEL), BF16)
        p_ref[ib * ii:ib * (ii + 1), :] = in_ref[...].astype(BF16)
        p_ref[ob * oi:ob * (oi + 1), :] = out_ref[...].astype(BF16)

    def spec(a):
        return pl.BlockSpec((None,) + a.shape[1:], lambda i, ids_ref: (layer, 0, 0))

    ins = (w_in, w_glu, w_out, w_down, w_gate_t, w_up_t)
    grid_spec = pltpu.PrefetchScalarGridSpec(
        num_scalar_prefetch=1, grid=(1,),
        in_specs=[spec(a) for a in ins] + [_ANY] * len(after),
        out_specs=pl.BlockSpec((None, None, P_ROWS, D_MODEL), lambda i, ids_ref: (ids_ref[1], 0, 0, 0)))
    return pl.pallas_call(
        body, name="pack_weights", grid_spec=grid_spec,
        out_shape=jax.ShapeDtypeStruct((N_SHARD, 1, P_ROWS, D_MODEL), BF16),
        compiler_params=_cparams(1),
    )(ids, *ins, *after)


MESH = pl.DeviceIdType.MESH
_ANY = pl.BlockSpec(memory_space=pl.ANY)
P_HALF = P_ROWS // 2
RS_ADD_TILE = 1408
RS_SUM_TILE = 352


def _mesh_pos():
    return lax.axis_index("x"), lax.axis_index("y"), lax.axis_index("c")


def _other_chips(x, y):
    return [(1 - x, y), (x, 1 - y), (1 - x, 1 - y)]


def _remote(src, dst, send_sems, recv_sems, n, to):
    return pltpu.make_async_remote_copy(src_ref=src, dst_ref=dst, send_sem=send_sems.at[n],
                                        recv_sem=recv_sems.at[n], device_id=to, device_id_type=MESH)


_HBM = pl.BlockSpec(memory_space=pltpu.HBM)
_SEM = pl.BlockSpec(memory_space=pltpu.SEMAPHORE)
_EFFECT = pltpu.CompilerParams(has_side_effects=pltpu.SideEffectType.DATAFLOW_SIDE_EFFECTING)
_TOKEN = jax.ShapeDtypeStruct((8, 128), F32)


def _in_hbm(a):
    return pltpu.with_memory_space_constraint(a, pltpu.HBM)


def _ag_piece(ref, shard, half, rows):
    row0, n_rows = rows
    return ref.at[shard, :, pl.ds(row0 + half * (n_rows // 2), n_rows // 2), :]


def _ag_start(name, wp, after, row_ranges):
    n_sems = 3 * len(row_ranges)

    def body(w_ref, after_ref, send_sems, recv_sems, w_thru, token):
        x, y, c = _mesh_pos()
        for i, rows in enumerate(row_ranges):
            mine = _ag_piece(w_ref, 2 * x + y, c, rows)
            for j, (px, py) in enumerate(_other_chips(x, y)):
                _remote(mine, mine, send_sems, recv_sems, 3 * i + j, (px, py, c)).start()
        token[...] = jnp.zeros_like(token)

    return pl.pallas_call(
        body, name=name,
        out_shape=(pltpu.SemaphoreType.DMA((n_sems,)), pltpu.SemaphoreType.DMA((n_sems,)),
                   pltpu.HBM(wp.shape, wp.dtype), _TOKEN),
        in_specs=(_HBM, _ANY), out_specs=(_SEM, _SEM, _HBM, pl.BlockSpec(memory_space=pltpu.VMEM)),
        input_output_aliases={0: 2}, compiler_params=_EFFECT,
    )(_in_hbm(wp), after)


def _ag_wait(name, send_sems, recv_sems, wp, after, row_ranges):
    def body(w_ref, send_sems, recv_sems, *rest):
        x, y, c = _mesh_pos()
        for i, rows in enumerate(row_ranges):
            mine = _ag_piece(w_ref, 2 * x + y, c, rows)
            for j, (px, py) in enumerate(_other_chips(x, y)):
                landed = _ag_piece(w_ref, 2 * px + py, c, rows)
                cp = _remote(mine, landed, send_sems, recv_sems, 3 * i + j, (px, py, c))
                cp.wait_send()
                cp.wait_recv()

    return pl.pallas_call(
        body, name=name, out_shape=pltpu.HBM(wp.shape, wp.dtype),
        in_specs=(_HBM, _SEM, _SEM) + (_ANY,) * len(after), out_specs=_HBM,
        input_output_aliases={0: 0}, compiler_params=_EFFECT,
    )(wp, send_sems, recv_sems, *after)


def _ag_forward(wp, rows):
    def body(w_in, o, send_sems, recv_sems):
        x, y, c = _mesh_pos()
        sib = (x, y, 1 - c)
        chips = _other_chips(x, y)
        sends = []
        for j, (px, py) in enumerate(chips):
            landed = _ag_piece(o, 2 * px + py, c, rows)
            cp = _remote(landed, landed, send_sems, recv_sems, j, sib)
            cp.start()
            sends.append(cp)
        for j, (px, py) in enumerate(chips):
            passed = _ag_piece(o, 2 * px + py, 1 - c, rows)
            _remote(passed, passed, send_sems, recv_sems, j, sib).wait_recv()
        for cp in sends:
            cp.wait_send()

    return pl.pallas_call(
        body, name="ag_forward",
        in_specs=[_ANY], out_specs=_ANY,
        out_shape=jax.ShapeDtypeStruct(wp.shape, wp.dtype),
        scratch_shapes=[pltpu.SemaphoreType.DMA((3,)), pltpu.SemaphoreType.DMA((3,))],
        input_output_aliases={0: 0},
    )(wp)


def _ag_forward_start(name, wp, rows):
    def body(w_ref, send_sems, recv_sems, w_thru):
        x, y, c = _mesh_pos()
        for j, (px, py) in enumerate(_other_chips(x, y)):
            landed = _ag_piece(w_ref, 2 * px + py, c, rows)
            _remote(landed, landed, send_sems, recv_sems, j, (x, y, 1 - c)).start()

    return pl.pallas_call(
        body, name=name,
        out_shape=(pltpu.SemaphoreType.DMA((3,)), pltpu.SemaphoreType.DMA((3,)), pltpu.HBM(wp.shape, wp.dtype)),
        in_specs=(_HBM,), out_specs=(_SEM, _SEM, _HBM),
        input_output_aliases={0: 2}, compiler_params=_EFFECT,
    )(_in_hbm(wp))


def _ag_forward_wait(name, send_sems, recv_sems, wp, after, rows):
    def body(w_ref, send_sems, recv_sems, *rest):
        x, y, c = _mesh_pos()
        for j, (px, py) in enumerate(_other_chips(x, y)):
            cp = _remote(_ag_piece(w_ref, 2 * px + py, c, rows), _ag_piece(w_ref, 2 * px + py, 1 - c, rows),
                         send_sems, recv_sems, j, (x, y, 1 - c))
            cp.wait_send()
            cp.wait_recv()

    return pl.pallas_call(
        body, name=name, out_shape=pltpu.HBM(wp.shape, wp.dtype),
        in_specs=(_HBM, _SEM, _SEM) + (_ANY,) * len(after), out_specs=_HBM,
        input_output_aliases={0: 0}, compiler_params=_EFFECT,
    )(wp, send_sems, recv_sems, *after)


def _rs_chips_start(name, t):
    nl = t.shape[0]

    def body(t_ref, land_ref, send_sems, recv_sems, t_thru, land_thru, token):
        x, y, c = _mesh_pos()
        for j, (px, py) in enumerate(_other_chips(x, y)):
            _remote(t_ref.at[:, 2 * px + py], land_ref.at[j], send_sems, recv_sems, j, (px, py, c)).start()
        token[...] = jnp.zeros_like(token)

    land = lax.empty((3, nl, P_HALF, D_MODEL), BF16)
    return pl.pallas_call(
        body, name=name,
        out_shape=(pltpu.SemaphoreType.DMA((3,)), pltpu.SemaphoreType.DMA((3,)), pltpu.HBM(t.shape, t.dtype),
                   pltpu.HBM(land.shape, land.dtype), _TOKEN),
        in_specs=(_HBM, _HBM), out_specs=(_SEM, _SEM, _HBM, _HBM, pl.BlockSpec(memory_space=pltpu.VMEM)),
        input_output_aliases={0: 2, 1: 3}, compiler_params=_EFFECT,
    )(_in_hbm(t), _in_hbm(land))


def _rs_chips_wait(name, send_sems, recv_sems, t, land, after):
    def body(t_ref, land_ref, send_sems, recv_sems, *rest):
        x, y, c = _mesh_pos()
        for j, (px, py) in enumerate(_other_chips(x, y)):
            cp = _remote(t_ref.at[:, 2 * px + py], land_ref.at[j], send_sems, recv_sems, j, (px, py, c))
            cp.wait_send()
            cp.wait_recv()

    return pl.pallas_call(
        body, name=name, out_shape=(pltpu.HBM(t.shape, t.dtype), pltpu.HBM(land.shape, land.dtype)),
        in_specs=(_HBM, _HBM, _SEM, _SEM) + (_ANY,) * len(after), out_specs=(_HBM, _HBM),
        input_output_aliases={0: 0, 1: 1}, compiler_params=_EFFECT,
    )(t, land, send_sems, recv_sems, *after)[1]


def _rs_sibling_start(name, g):
    nl = g.shape[0]

    def body(g_ref, land_ref, send_sems, recv_sems, g_thru, land_thru, token):
        x, y, c = _mesh_pos()
        _remote(g_ref.at[:, :, pl.ds((1 - c) * P_HALF, P_HALF), :], land_ref, send_sems, recv_sems, 0,
                (x, y, 1 - c)).start()
        token[...] = jnp.zeros_like(token)

    land = lax.empty((nl, N_SHARD, P_HALF, D_MODEL), F32)
    return pl.pallas_call(
        body, name=name,
        out_shape=(pltpu.SemaphoreType.DMA((1,)), pltpu.SemaphoreType.DMA((1,)), pltpu.HBM(g.shape, g.dtype),
                   pltpu.HBM(land.shape, land.dtype), _TOKEN),
        in_specs=(_HBM, _HBM), out_specs=(_SEM, _SEM, _HBM, _HBM, pl.BlockSpec(memory_space=pltpu.VMEM)),
        input_output_aliases={0: 2, 1: 3}, compiler_params=_EFFECT,
    )(_in_hbm(g), _in_hbm(land))


def _rs_sibling_wait(name, send_sems, recv_sems, g, land, after):
    def body(g_ref, land_ref, send_sems, recv_sems, *rest):
        x, y, c = _mesh_pos()
        cp = _remote(g_ref.at[:, :, pl.ds((1 - c) * P_HALF, P_HALF), :], land_ref, send_sems, recv_sems, 0,
                     (x, y, 1 - c))
        cp.wait_send()
        cp.wait_recv()

    return pl.pallas_call(
        body, name=name, out_shape=(pltpu.HBM(g.shape, g.dtype), pltpu.HBM(land.shape, land.dtype)),
        in_specs=(_HBM, _HBM, _SEM, _SEM) + (_ANY,) * len(after), out_specs=(_HBM, _HBM),
        input_output_aliases={0: 0, 1: 1}, compiler_params=_EFFECT,
    )(g, land, send_sems, recv_sems, *after)


def _rs_add(name, ids, g, buf, row_tile):
    nl, _, hr, cols = buf.shape
    n_rt = hr // row_tile

    def body(ids_ref, g_ref, b_ref, own_ref, tb_ref):
        t = g_ref[...] + b_ref[...]
        tb_ref[...] = t.astype(BF16)

        @pl.when(pl.program_id(2) == ids_ref[1])
        def _():
            own_ref[...] = t

    blk = (None, None, row_tile, cols)
    grid_spec = pltpu.PrefetchScalarGridSpec(
        num_scalar_prefetch=1, grid=(nl, n_rt, N_SHARD),
        in_specs=[pl.BlockSpec(blk, lambda l, j, s, ids_ref: (l, s, ids_ref[0] * n_rt + j, 0)),
                  pl.BlockSpec(blk, lambda l, j, s, ids_ref: (l, s, j, 0))],
        out_specs=[pl.BlockSpec((None, row_tile, cols), lambda l, j, s, ids_ref: (l, j, 0)),
                   pl.BlockSpec(blk, lambda l, j, s, ids_ref: (l, s, j, 0))])
    return pl.pallas_call(
        body, name=name, grid_spec=grid_spec,
        out_shape=[jax.ShapeDtypeStruct((nl, hr, cols), F32), jax.ShapeDtypeStruct(buf.shape, BF16)],
        compiler_params=_cparams(3),
    )(ids, g, buf)


def _rs_sum(ids, layer, own, bufb, reduced, row_tile):
    _, hr, cols = own.shape
    n_rt = hr // row_tile

    def body(ids_ref, own_ref, b_ref, reduced_in, f_ref):
        f_ref[...] = ((own_ref[...] + b_ref[0].astype(F32)) + b_ref[1].astype(F32)) + b_ref[2].astype(F32)

    grid_spec = pltpu.PrefetchScalarGridSpec(
        num_scalar_prefetch=1, grid=(n_rt,),
        in_specs=[pl.BlockSpec((None, row_tile, cols), lambda j, ids_ref: (0, j, 0)),
                  pl.BlockSpec((3, None, row_tile, cols), lambda j, ids_ref: (0, 0, j, 0)),
                  pl.BlockSpec(memory_space=pl.ANY)],
        out_specs=pl.BlockSpec((None, row_tile, cols), lambda j, ids_ref: (layer, ids_ref[0] * n_rt + j, 0)))
    return pl.pallas_call(
        body, name="rs_sum", grid_spec=grid_spec,
        out_shape=jax.ShapeDtypeStruct(reduced.shape, F32),
        input_output_aliases={3: 0},
        compiler_params=_cparams(1),
    )(ids, own, bufb, reduced)


def _rs_exchange_start(name, f):
    def body(f_ref, send_sems, recv_sems, f_thru):
        x, y, c = _mesh_pos()
        mine = f_ref.at[:, pl.ds(c * P_HALF, P_HALF), :]
        _remote(mine, mine, send_sems, recv_sems, 0, (x, y, 1 - c)).start()

    return pl.pallas_call(
        body, name=name,
        out_shape=(pltpu.SemaphoreType.DMA((1,)), pltpu.SemaphoreType.DMA((1,)), pltpu.HBM(f.shape, f.dtype)),
        in_specs=(_HBM,), out_specs=(_SEM, _SEM, _HBM),
        input_output_aliases={0: 2}, compiler_params=_EFFECT,
    )(_in_hbm(f))


def _rs_exchange_wait(name, send_sems, recv_sems, f, after):
    def body(f_ref, send_sems, recv_sems, *rest):
        x, y, c = _mesh_pos()
        mine = f_ref.at[:, pl.ds(c * P_HALF, P_HALF), :]
        theirs = f_ref.at[:, pl.ds((1 - c) * P_HALF, P_HALF), :]
        cp = _remote(mine, theirs, send_sems, recv_sems, 0, (x, y, 1 - c))
        cp.wait_send()
        cp.wait_recv()

    return pl.pallas_call(
        body, name=name, out_shape=pltpu.HBM(f.shape, f.dtype),
        in_specs=(_HBM, _SEM, _SEM) + (_ANY,) * len(after), out_specs=_HBM,
        input_output_aliases={0: 0}, compiler_params=_EFFECT,
    )(f, send_sems, recv_sems, *after)


def _small_all_reduce(s, after=()):
    n_rows = s.shape[0]
    hr = n_rows // 2
    qr = hr // N_SHARD

    def body(s_ref, *rest):
        o_ref, sibbuf, tbuf, qbuf, fbuf, send_sems, recv_sems = rest[len(after):]
        x, y, c = _mesh_pos()
        k = 2 * x + y
        sib = (x, y, 1 - c)
        chips = _other_chips(x, y)
        mine = pl.ds(pl.multiple_of(c * hr, SUBLANES), hr)
        theirs = pl.ds(pl.multiple_of((1 - c) * hr, SUBLANES), hr)

        def quarter(shard):
            return pl.ds(pl.multiple_of(shard * qr, SUBLANES), qr)

        first = _remote(s_ref.at[theirs], sibbuf, send_sems, recv_sems, 0, sib)
        first.start()
        first.wait()
        tbuf[...] = s_ref[mine, :] + sibbuf[...]
        cps = []
        for j, (px, py) in enumerate(chips):
            cp = _remote(tbuf.at[quarter(2 * px + py)], qbuf.at[j], send_sems, recv_sems, 1 + j, (px, py, c))
            cp.start()
            cps.append(cp)
        for cp in cps:
            cp.wait()
        fbuf[quarter(k), :] = (tbuf[quarter(k), :] + qbuf[1]) + (qbuf[0] + qbuf[2])
        cps = []
        for j, (px, py) in enumerate(chips):
            cp = _remote(fbuf.at[quarter(k)], fbuf.at[quarter(k)], send_sems, recv_sems, 4 + j, (px, py, c))
            cp.start()
            cps.append(cp)
        for j, (px, py) in enumerate(chips):
            got = fbuf.at[quarter(2 * px + py)]
            _remote(got, got, send_sems, recv_sems, 4 + j, (px, py, c)).wait_recv()
        for cp in cps:
            cp.wait_send()
        o_ref[mine, :] = fbuf[...]
        last = _remote(fbuf, o_ref.at[mine], send_sems, recv_sems, 7, sib)
        last.start()
        last.wait()

    vmem = pl.BlockSpec(memory_space=pltpu.VMEM)
    return pl.pallas_call(
        body, name="small_all_reduce",
        in_specs=[vmem] + [_ANY] * len(after), out_specs=vmem,
        out_shape=jax.ShapeDtypeStruct(s.shape, F32),
        scratch_shapes=[pltpu.VMEM((hr, D_MODEL), F32), pltpu.VMEM((hr, D_MODEL), F32),
                        pltpu.VMEM((3, qr, D_MODEL), F32), pltpu.VMEM((hr, D_MODEL), F32),
                        pltpu.SemaphoreType.DMA((8,)), pltpu.SemaphoreType.DMA((8,))],
        compiler_params=pltpu.CompilerParams(vmem_limit_bytes=VMEM_LIMIT),
    )(s, *after)


_SMALL = ("norm_mix", "w_pool", "pool_scale", "lam_re", "lam_im", "log_dt", "b_re", "b_im", "c_re", "c_im",
          "d_skip", "b_glu", "norm_ffn", "norm_final")
_WEIGHTS = ("norm_mix", "w_in", "w_pool", "pool_scale", "lam_re", "lam_im", "log_dt", "b_re", "b_im", "c_re",
            "c_im", "d_skip", "w_glu", "b_glu", "w_out", "norm_ffn", "w_gate", "w_up", "w_down", "norm_final")


def _local_step(x, target, p, get_weights, scan_done, get_ffn_weights, ffn_bwd_done, put_grads):
    nl = p["norm_mix"].shape[0]

    def tied(a, token):
        return a if token is None else a + token
    n_rows = nl * N_SSM_GROUPS
    lr = p["lam_re"].reshape(n_rows, 1, SSM_STATE)
    li = p["lam_im"].reshape(n_rows, 1, SSM_STATE)
    ldt = p["log_dt"].reshape(n_rows, 1, 1)
    br_t = p["b_re"].reshape(n_rows, SSM_STATE, SSM_GROUP).transpose(0, 2, 1)
    bi_t = p["b_im"].reshape(n_rows, SSM_STATE, SSM_GROUP).transpose(0, 2, 1)
    ar, ai, bbr_t, bbi_t = _disc_fwd(lr, li, ldt, br_t, bi_t)
    ar = ar.reshape(nl, 1, N_STATE)
    ai = ai.reshape(nl, 1, N_STATE)
    bbr = bbr_t.transpose(0, 2, 1).reshape(nl, N_SSM_GROUPS, SSM_STATE, SSM_GROUP)
    bbi = bbi_t.transpose(0, 2, 1).reshape(nl, N_SSM_GROUPS, SSM_STATE, SSM_GROUP)
    w_pool = p["w_pool"].astype(BF16)
    p = dict(p)
    for n in ("norm_mix", "pool_scale", "b_glu", "norm_ffn"):
        p[n] = p[n].reshape(nl, 1, -1)
    swap = lambda a: jnp.swapaxes(a, -1, -2)
    bpad = jax.vmap(_pad_pairs)(bbr, bbi).astype(BF16)
    cpad_t = jax.vmap(_pad_pairs)(swap(p["c_re"]), -swap(p["c_im"])).astype(BF16)
    bpad_t, cpad = swap(bpad), swap(cpad_t)
    dskip = p["d_skip"].reshape(nl, 1, D_SSM)

    layers = []
    h = x
    for l in range(nl):
        wp = get_weights(l, [h] if l else [h, bpad, cpad, bpad_t, cpad_t, ar, ai])
        u, ypool = _mix_in_fwd(h, p["norm_mix"], wp, l, w_pool, p["pool_scale"])
        sre, sim, yraw = _ssm_fwd(u, l, bpad, cpad, ar, ai, dskip)
        wp = scan_done(l, wp, [yraw])
        hm = _mix_out_fwd(yraw, ypool, h, wp, l, p["b_glu"])
        wp = get_ffn_weights(l, wp, [hm])
        h_next, n2, act_s, fgate_s, fup_s = _ffn_fwd(hm, p["norm_ffn"], wp, l)
        layers.append(dict(h=h, u=u, ypool=ypool, sre=sre, sim=sim, yraw=yraw, hm=hm, n2=n2, act_s=act_s, wp=wp,
                           fgate_s=fgate_s, fup_s=fup_s))
        h = h_next

    dh, loss, d_norm_final = _final_fwd_bwd(h, p["norm_final"].reshape(1, D_MODEL), target)

    raw = {n: [None] * nl for n in ("dg1", "dwp", "dsc", "dcp", "dbp", "ddsk", "db_glu", "dg2", "dar", "dai")}
    token = None
    for l in reversed(range(nl)):
        s = layers[l]
        wp = s["wp"]
        g1 = lax.empty((1, N_SHARD, P_ROWS, D_MODEL), F32)
        dhm, dg2, dgate_s, dup_s, dhb = _ffn_bwd_act(dh, s["hm"], tied(p["norm_ffn"], token), s["fgate_s"],
                                                      s["fup_s"], wp, l)
        g1 = _ffn_bwd_w(s["n2"], dgate_s, dup_s, s["act_s"], dhb, g1)
        token = ffn_bwd_done(l, [g1])
        dyraw, dyp, db_glu, g1 = _mix_out_bwd(dhm, s["yraw"], s["ypool"], wp, l, tied(p["b_glu"], token), g1)
        dus, dcp, dbp, dar, dai, ddsk = _ssm_bwd(dyraw, s["u"], s["sre"], s["sim"], l, cpad_t, bpad_t, ar, ai, dskip)
        dup, dwp, dsc = _pool_bwd(dyp, s["u"], l, w_pool, p["pool_scale"])
        dh, dg1, g1 = _mix_in_bwd(dup, dus, s["h"], dhm, p["norm_mix"], wp, l, g1)
        token = put_grads(l, g1)
        for n, a in (("dg1", dg1), ("dwp", dwp), ("dsc", dsc), ("dcp", dcp), ("dbp", dbp), ("ddsk", ddsk),
                     ("db_glu", db_glu), ("dg2", dg2), ("dar", dar), ("dai", dai)):
            raw[n][l] = a

    st = {n: jnp.stack(v) for n, v in raw.items()}
    dc_re, dc_im = jax.vmap(_unpad_pairs)(swap(st["dcp"]))
    dbbr, dbbi = jax.vmap(_unpad_pairs)(st["dbp"])
    rows = lambda a: a.reshape((n_rows,) + a.shape[2:])
    dlr, dli, dldt, dbr_t, dbi_t = _disc_bwd(lr, li, ldt, br_t, bi_t, st["dar"].reshape(n_rows, 1, SSM_STATE),
                                              st["dai"].reshape(n_rows, 1, SSM_STATE), rows(swap(dbbr)),
                                              rows(swap(dbbi)))
    small = {"norm_mix": st["dg1"][:, 0], "w_pool": st["dwp"], "pool_scale": st["dsc"][:, 0], "c_re": swap(dc_re),
             "c_im": -swap(dc_im), "d_skip": st["ddsk"].reshape(nl, N_SSM_GROUPS, SSM_GROUP),
             "b_glu": st["db_glu"][:, 0], "norm_ffn": st["dg2"][:, 0]}
    small["lam_re"] = dlr.reshape(nl, N_SSM_GROUPS, SSM_STATE)
    small["lam_im"] = dli.reshape(nl, N_SSM_GROUPS, SSM_STATE)
    small["log_dt"] = dldt.reshape(nl, N_SSM_GROUPS)
    small["b_re"] = dbr_t.reshape(nl, N_SSM_GROUPS, SSM_GROUP, SSM_STATE)
    small["b_im"] = dbi_t.reshape(nl, N_SSM_GROUPS, SSM_GROUP, SSM_STATE)
    small["d_skip"] = small["d_skip"].transpose(_SMALL_VIEW["d_skip"])
    small["norm_final"] = d_norm_final
    return loss, dh, small


_SMALL_VIEW = {"b_re": (0, 1, 3, 2), "b_im": (0, 1, 3, 2), "d_skip": (0, 2, 1)}
_SMALL_GROUPS = (("b_re", "b_im"), ("c_re", "c_im"), ("lam_re", "lam_im"), ("norm_mix", "norm_ffn"),
                 ("pool_scale", "b_glu"), ("w_pool",), ("log_dt",), ("d_skip",), ("norm_final",))


def _view(n, a):
    a = a.transpose(_SMALL_VIEW[n]) if n in _SMALL_VIEW else a
    return a[None] if a.ndim == 1 else a


def _unview(n, a, shape):
    a = a.reshape(shape) if len(shape) == 1 else a
    return a.transpose(_SMALL_VIEW[n]) if n in _SMALL_VIEW else a


def _flatten_small(views):
    flat = jnp.concatenate([views[n].reshape(-1) for n in _SMALL])
    n_rows = -(-flat.shape[0] // (64 * D_MODEL)) * 64
    return jnp.pad(flat, (0, n_rows * D_MODEL - flat.shape[0])).reshape(n_rows, D_MODEL)


def _split_small(flat, like):
    flat = flat.reshape(-1)
    out, at = {}, 0
    for n in _SMALL:
        size = like[n].size
        out[n] = flat[at:at + size].reshape(like[n].shape)
        at += size
    return out


def _adamw_small(name, ws, ms, vs, gs):
    k = len(ws)

    def body(*refs):
        ins, outs = refs[:4 * k], refs[4 * k:]
        for i in range(k):
            w, m, v, g = (ins[j * k + i][...] for j in range(4))
            delta, mn, vn = _adamw_math(w, g, m, v)
            outs[i][...] = delta
            outs[k + i][...] = mn
            outs[2 * k + i][...] = vn

    shapes = [jax.ShapeDtypeStruct(w.shape, F32) for w in ws] * 3
    outs = pl.pallas_call(body, name=name, out_shape=shapes,
                          compiler_params=pltpu.CompilerParams(vmem_limit_bytes=VMEM_LIMIT))(*ws, *ms, *vs, *gs)
    return outs[:k], outs[k:2 * k], outs[2 * k:]


def kernel(x, norm_mix, w_in, w_pool, pool_scale, lam_re, lam_im, log_dt, b_re, b_im, c_re, c_im, d_skip, w_glu, b_glu, w_out, norm_ffn, w_gate, w_up, w_down, norm_final, loss_target, m_norm_mix, m_w_in, m_w_pool, m_pool_scale, m_lam_re, m_lam_im, m_log_dt, m_b_re, m_b_im, m_c_re, m_c_im, m_d_skip, m_w_glu, m_b_glu, m_w_out, m_norm_ffn, m_w_gate, m_w_up, m_w_down, m_norm_final, v_norm_mix, v_w_in, v_w_pool, v_pool_scale, v_lam_re, v_lam_im, v_log_dt, v_b_re, v_b_im, v_c_re, v_c_im, v_d_skip, v_w_glu, v_b_glu, v_w_out, v_norm_ffn, v_w_gate, v_w_up, v_w_down, v_norm_final):
    given = dict(locals())
    w = {n: given[n] for n in _WEIGHTS}
    m = {n: given["m_" + n] for n in _WEIGHTS}
    v = {n: given["v_" + n] for n in _WEIGHTS}
    ids = jnp.stack([lax.axis_index("c"), 2 * lax.axis_index("x") + lax.axis_index("y")]).astype(jnp.int32)

    t_names = ("w_gate", "w_up")
    tr = lambda a: a.transpose(0, 2, 1)
    for d in (w, m, v):
        d.update({n: tr(d[n]) for n in t_names})

    nl = norm_mix.shape[0]
    mixer_rows, ffn_rows = (P_FF_ROWS, P_ROWS - P_FF_ROWS), (0, P_FF_ROWS)
    started, last = {}, None
    for l in range(nl):
        packed = _pack_weights(ids, l, w["w_in"], w["w_glu"], w["w_out"], w["w_down"], w["w_gate"], w["w_up"],
                               [] if last is None else [last])
        if l == 0:
            first = _ag_start("ag_start_0_mixer", packed, ids, [mixer_rows])
            started[0] = _ag_start("ag_start_0_ffn", first[2], first[3], [ffn_rows])
        else:
            started[l] = _ag_start(f"ag_start_{l}", packed, last, [mixer_rows, ffn_rows])
        last = started[l][3]
    views = [{n: _view(n, d[n]) for n in _SMALL} for d in (w, m, v)]

    passing = {}

    def get_weights(l, after):
        send_sems, recv_sems, buf, _ = started[l]
        if l == 0:
            buf = _ag_wait("ag_wait_0_mixer", first[0], first[1], buf, after + [last], [mixer_rows])
            return _ag_forward(buf, mixer_rows)
        buf = _ag_wait(f"ag_wait_{l}", send_sems, recv_sems, buf, after, [mixer_rows, ffn_rows])
        buf = _ag_forward(buf, mixer_rows)
        passing[l] = _ag_forward_start(f"ag_forward_start_{l}", buf, ffn_rows)
        return passing[l][2]

    def scan_done(l, buf, after):
        if l > 0:
            return buf
        send_sems, recv_sems, _, _ = started[0]
        buf = _ag_wait("ag_wait_0_ffn", send_sems, recv_sems, buf, after, [ffn_rows])
        passing[0] = _ag_forward_start("ag_forward_start_0", buf, ffn_rows)
        return passing[0][2]

    def get_ffn_weights(l, buf, after):
        send_sems, recv_sems, _ = passing[l]
        return _ag_forward_wait(f"ag_forward_wait_{l}", send_sems, recv_sems, buf, after, ffn_rows)

    to_sibling, to_chips, reduced = {}, {}, {}

    def put_grads(l, g):
        to_sibling[l] = _rs_sibling_start(f"rs_sibling_start_{l}", g)
        token = to_sibling[l][4]
        if l + 1 in to_chips:
            finish(l + 1, [token])
        return token[:1, :1]

    def ffn_bwd_done(l, after):
        return send_to_chips(l + 1, after)[:1, :1] if l + 1 in to_sibling else None

    def send_to_chips(l, after):
        send_sems, recv_sems, g, land, _ = to_sibling.pop(l)
        g, land = _rs_sibling_wait(f"rs_sibling_wait_{l}", send_sems, recv_sems, g, land, after)
        own, t = _rs_add("rs_add", ids, g, land, RS_ADD_TILE)
        send_sems, recv_sems, t, land, token = _rs_chips_start(f"rs_chips_start_{l}", t)
        to_chips[l] = (send_sems, recv_sems, t, land, own)
        return token

    def finish(l, after):
        send_sems, recv_sems, t, land, own = to_chips.pop(l)
        land = _rs_chips_wait(f"rs_chips_wait_{l}", send_sems, recv_sems, t, land, after)
        shard = lax.empty((1, P_ROWS, D_MODEL), F32)
        reduced[l] = _rs_exchange_start(f"rs_exchange_start_{l}", _rs_sum(ids, 0, own, land, shard, RS_SUM_TILE))

    loss, grad_x, small = _local_step(x[0], loss_target[0], {n: w[n] for n in _SMALL}, get_weights, scan_done,
                                      get_ffn_weights, ffn_bwd_done, put_grads)
    loss = lax.psum(loss[0, 0], ("x", "y", "c"))
    small_flat = _flatten_small(small)

    groups = ((("w_in", P_IN_BLK), ("w_out", P_OUT_BLK)), (("w_down", P_WD_BLK), ("w_gate", P_WG_BLK), ("w_up", P_WU_BLK)))
    res = {n: None for n in ("w_in", "w_out", "w_down", "w_gate", "w_up", "w_glu")}

    def adamw_layer(l, after):
        send_sems, recv_sems, shard = reduced[l]
        shard = _rs_exchange_wait(f"rs_exchange_wait_{l}", send_sems, recv_sems, shard, after)
        for group, row_tile in zip(groups, (128, 176)):
            names = [n for n, _ in group]
            outs = None if res[names[0]] is None else [res[n] for n in names]
            outs = _adamw_group("adamw_" + names[0], l, *[[d[n] for n in names] for d in (w, m, v)], shard,
                                [blk * idx for _, (blk, idx) in group], row_tile, outs)
            res.update(zip(names, outs))
        blk, idx = P_GLU_BLK
        res["w_glu"] = _adamw("adamw_w_glu", l, w["w_glu"], m["w_glu"], v["w_glu"], shard, (blk, D_MODEL), blk * idx,
                              128, res["w_glu"], (), True)

    if nl > 1:
        adamw_layer(nl - 1, [to_sibling[0][4]])
    token = send_to_chips(0, [small_flat] + [r[0] for r in res.values() if r is not None])
    for l in reversed(range(1, nl - 1)):
        adamw_layer(l, [token])
    updated = [r[0] for r in res.values() if r is not None]
    small_sum = _small_all_reduce(small_flat, [token] + updated)
    finish(0, [small_sum] + updated)
    adamw_layer(0, [])
    for n in t_names:
        res[n] = tuple(tr(a) for a in res[n])
    g_views = _split_small(small_sum, views[0])
    for group in _SMALL_GROUPS:
        deltas, new_ms, new_vs = _adamw_small("adamw_" + group[0], *[[d[n] for n in group] for d in views],
                                              [g_views[n] for n in group])
        for i, n in enumerate(group):
            res[n] = tuple(_unview(n, a, w[n].shape) for a in (g_views[n], deltas[i], new_ms[i], new_vs[i]))

    return (loss, grad_x[None], *[res[n][0] for n in _WEIGHTS], *[res[n][1] for n in _WEIGHTS],
            *[res[n][2] for n in _WEIGHTS], *[res[n][3] for n in _WEIGHTS])
```

```python
import functools
import math

import jax
import jax.numpy as jnp
from jax import lax
from jax.experimental import pallas as pl
from jax.experimental.pallas import tpu as pltpu

F32 = jnp.float32
BF16 = jnp.bfloat16

D_MODEL = 1024
D_POOL = 512
D_SSM = 512
POOL_WINDOWS = (2, 4, 8, 16)
POOL_GROUP = 128
POOL_HALO = 16
N_SSM_GROUPS = 32
SSM_GROUP = 16
SSM_STATE = 64
N_STATE = N_SSM_GROUPS * SSM_STATE
N_PAIRS = N_SSM_GROUPS // 2
D_FF = 2816
N_SHARD = 4
FF_SHARD = D_FF // N_SHARD
RMS_EPS = 1e-6

ADAM_LR = 0.001
ADAM_B1 = 0.9
ADAM_B2 = 0.999
ADAM_EPS = 1e-08
ADAM_WD = 0.01
ADAM_STEP = 10

P_ROWS = 2816
P_WD_BLK = (704, 0)
P_WG_BLK = (704, 1)
P_WU_BLK = (704, 2)
P_FF_ROWS = 2112
P_GLU_BLK = (64, 33)
P_GLU_PAD = 192
P_IN_BLK = (256, 9)
P_OUT_BLK = (256, 10)

SUBLANES = 8
VMEM_LIMIT = 56 * 1024 * 1024

TM = 1024
TM_FFN = 512
TM_FFN_LONG = 1024
TS = 2048
SCAN_LANES = 512


def _cparams(n_axes):
    return pltpu.CompilerParams(dimension_semantics=("arbitrary",) * n_axes, vmem_limit_bytes=VMEM_LIMIT)


def _dot(a, b):
    return jnp.dot(a, b, preferred_element_type=F32)


def _dot_nt(a, b):
    return lax.dot_general(a, b, (((1,), (1,)), ((), ())), preferred_element_type=F32)


def _dot_tn(a, b):
    return lax.dot_general(a, b, (((0,), (0,)), ((), ())), preferred_element_type=F32)


def _rms_hat(x):
    r = lax.rsqrt(jnp.mean(x * x, axis=-1, keepdims=True) + RMS_EPS)
    return x * r, r


def _rms_bwd(d_hat, xhat, r):
    return r * (d_hat - xhat * jnp.mean(d_hat * xhat, axis=-1, keepdims=True))


def _sigmoid(x):
    return 1.0 / (1.0 + jnp.exp(-x))


_GELU_C = math.sqrt(2.0 / math.pi)
_GELU_K = 0.044715


def _gelu(x):
    return 0.5 * x * (1.0 + jnp.tanh(_GELU_C * (x + _GELU_K * x * x * x)))


def _gelu_grad(x):
    th = jnp.tanh(_GELU_C * (x + _GELU_K * x * x * x))
    return 0.5 * (1.0 + th) + 0.5 * x * (1.0 - th * th) * _GELU_C * (1.0 + 3.0 * _GELU_K * x * x)


def _glu_weight(ref):
    v = ref[...]
    return jnp.concatenate([v[:, :, :D_SSM], v[:, :, D_SSM:]], axis=1).reshape(D_SSM, D_SSM)


def _glu_pack(w):
    v = w.reshape(N_SHARD, 128, D_SSM)
    return jnp.concatenate([v[:, :64, :], v[:, 64:, :]], axis=2)


def _pool_diff(ext, row0, tm):
    rows = row0 + lax.broadcasted_iota(jnp.int32, (tm, 1), 0)
    outs = []
    for gi, w in enumerate(POOL_WINDOWS):
        e = ext[:, gi * POOL_GROUP:(gi + 1) * POOL_GROUP]
        s = e
        k = 1
        while k < w:
            s = s + pltpu.roll(s, k, 0)
            k *= 2
        inv = 1.0 / jnp.minimum(rows + 1, w).astype(F32)
        outs.append(s[POOL_HALO:, :] * inv - e[POOL_HALO:, :])
    return outs


def _mix_in_fwd(h, g1, wp, layer, w_pool, scale):
    L = h.shape[0]
    tm = min(TM, L)

    def body(h_ref, g_ref, w_ref, wp_ref, sc_ref, u_ref, yp_ref, carry):
        i = pl.program_id(0)

        @pl.when(i == 0)
        def _():
            carry[...] = jnp.zeros_like(carry)

        xhat, _ = _rms_hat(h_ref[...])
        n1 = (xhat * g_ref[...]).astype(BF16)
        u = _dot(n1, w_ref[...].reshape(D_MODEL, D_MODEL))
        u_ref[...] = u
        up = u[:, :D_POOL]
        ext = jnp.concatenate([carry[...], up], axis=0)
        carry[...] = up[tm - POOL_HALO:, :]
        diffs = _pool_diff(ext, i * tm, tm)
        for gi in range(4):
            cols = slice(gi * POOL_GROUP, (gi + 1) * POOL_GROUP)
            yp_ref[:, cols] = _dot(diffs[gi].astype(BF16), wp_ref[gi]) * sc_ref[:, cols]

    blk, idx = P_IN_BLK
    return pl.pallas_call(
        body, name="mix_in_fwd", grid=(L // tm,),
        in_specs=[pl.BlockSpec((tm, D_MODEL), lambda i: (i, 0)),
                  pl.BlockSpec((None, 1, D_MODEL), lambda i: (layer, 0, 0)),
                  pl.BlockSpec((N_SHARD, None, blk, D_MODEL), lambda i: (0, 0, idx, 0)),
                  pl.BlockSpec((None, 4, POOL_GROUP, POOL_GROUP), lambda i: (layer, 0, 0, 0)),
                  pl.BlockSpec((None, 1, D_POOL), lambda i: (layer, 0, 0))],
        out_specs=[pl.BlockSpec((tm, D_MODEL), lambda i: (i, 0)),
                   pl.BlockSpec((tm, D_POOL), lambda i: (i, 0))],
        out_shape=[jax.ShapeDtypeStruct((L, D_MODEL), F32), jax.ShapeDtypeStruct((L, D_POOL), F32)],
        scratch_shapes=[pltpu.VMEM((POOL_HALO, D_POOL), F32)],
        compiler_params=_cparams(1),
    )(h, g1, wp, w_pool, scale)


def _cmul(xr, xi, yr, yi):
    return xr * yr - xi * yi, xr * yi + xi * yr


SCAN_BLOCK = 64
N_SCAN_TABLES = 26


def _permute_rows(src, dst, n_rows):
    for b in range(n_rows // SCAN_BLOCK):
        for tau in range(SUBLANES):
            dst[pl.ds(SCAN_BLOCK * b + SUBLANES * tau, SUBLANES), :] = (
                src[pl.ds(SCAN_BLOCK * b + tau, SUBLANES, stride=SUBLANES), :])


def _scan_tables(ar, ai, tab, reverse):
    c = ar.shape[1]
    row = lax.broadcasted_iota(jnp.int32, (SUBLANES, c), 0)
    zero = jnp.zeros((SUBLANES, c), F32)
    full = lambda v: jnp.broadcast_to(v, (SUBLANES, c))
    pw = [(ar, ai)]
    for _ in range(SUBLANES - 1):
        pw.append(_cmul(*pw[-1], ar, ai))
    a8 = pw[-1]
    a16 = _cmul(*a8, *a8)
    a32 = _cmul(*a16, *a16)
    tab[0] = full(ar)
    tab[1] = full(ai)
    for n, (s, (pr, pi)) in enumerate(((1, a8), (2, a16), (4, a32))):
        keep = (row < SUBLANES - s) if reverse else (row >= s)
        tab[2 + 2 * n] = jnp.where(keep, pr, zero)
        tab[3 + 2 * n] = jnp.where(keep, pi, zero)
    cur = a8
    qr, qi = zero, zero
    for n in range(SUBLANES):
        at = (SUBLANES - 1 - n) if reverse else n
        qr = jnp.where(row == at, cur[0], qr)
        qi = jnp.where(row == at, cur[1], qi)
        cur = _cmul(*cur, *a8)
    tab[8] = qr
    tab[9] = qi
    for tau in range(SUBLANES):
        pr, pi = pw[SUBLANES - 1 - tau] if reverse else pw[tau]
        tab[10 + 2 * tau] = full(pr)
        tab[11 + 2 * tau] = full(pi)


def _cmac(xr, xi, ar, ai, yr, yi):
    return xr + ar * yr - ai * yi, xi + ar * yi + ai * yr


def _chain_segments(er, ei, c_r, c_i, tab, cols, reverse):
    tr, ti = er, ei
    for n, s in enumerate((1, 2, 4)):
        shift = SUBLANES - s if reverse else s
        tr, ti = _cmac(tr, ti, tab[2 + 2 * n, :, cols], tab[3 + 2 * n, :, cols],
                       pltpu.roll(tr, shift, 0), pltpu.roll(ti, shift, 0))
    return _cmac(tr, ti, tab[8, :, cols], tab[9, :, cols], c_r, c_i)


def _ssm_fwd(u, layer, bpad, cpad, ar, ai, dskip):
    L = u.shape[0]
    ts = min(TS, L)
    nq = 4
    cq = N_STATE // nq

    def body(u_ref, bp_ref, cp_ref, ar_ref, ai_ref, dsk_ref, sre_ref, sim_ref, y_ref, cr, ci, tab, up, yp):
        t = pl.program_id(1)

        @pl.when(t == 0)
        def _():
            cr[...] = jnp.zeros_like(cr)
            ci[...] = jnp.zeros_like(ci)
            _scan_tables(ar_ref[...], ai_ref[...], tab, reverse=False)

        _permute_rows(u_ref, up, ts)
        uf = up[...]
        ub = uf.astype(BF16)
        for jj in range(4):
            bu = _dot(ub, bp_ref[jj])
            sre_ref[:, jj * 128:(jj + 1) * 128] = bu[:, :128]
            sim_ref[:, jj * 128:(jj + 1) * 128] = bu[:, 128:]

        shp = (SUBLANES, SCAN_LANES)
        first_row = lax.broadcasted_iota(jnp.int32, shp, 0) == 0
        for cc in range(cq // SCAN_LANES):
            cols = slice(cc * SCAN_LANES, (cc + 1) * SCAN_LANES)

            def block(b, carry, cols=cols):
                c_r, c_i = carry
                base = pl.multiple_of(b * SCAN_BLOCK, SCAN_BLOCK)
                rows = lambda tau: pl.ds(base + SUBLANES * tau, SUBLANES)
                a_r, a_i = tab[0, :, cols], tab[1, :, cols]
                ys = [(sre_ref[rows(0), cols], sim_ref[rows(0), cols])]
                for tau in range(1, SUBLANES):
                    ys.append(_cmac(sre_ref[rows(tau), cols], sim_ref[rows(tau), cols], a_r, a_i, *ys[-1]))
                tr, ti = _chain_segments(*ys[-1], c_r, c_i, tab, cols, reverse=False)
                in_r = jnp.where(first_row, c_r, pltpu.roll(tr, 1, 0))
                in_i = jnp.where(first_row, c_i, pltpu.roll(ti, 1, 0))
                for tau in range(SUBLANES):
                    sr, si = _cmac(*ys[tau], tab[10 + 2 * tau, :, cols], tab[11 + 2 * tau, :, cols], in_r, in_i)
                    sre_ref[rows(tau), cols] = sr
                    sim_ref[rows(tau), cols] = si
                return (jnp.broadcast_to(tr[SUBLANES - 1:, :], shp), jnp.broadcast_to(ti[SUBLANES - 1:, :], shp))

            c_r, c_i = lax.fori_loop(0, ts // SCAN_BLOCK, block, (cr[:, cols], ci[:, cols]), unroll=2)
            cr[:, cols] = c_r
            ci[:, cols] = c_i

        acc = dsk_ref[...] * uf
        for jj in range(4):
            cols = slice(jj * 128, (jj + 1) * 128)
            scat = jnp.concatenate([sre_ref[:, cols], sim_ref[:, cols]], axis=1).astype(BF16)
            acc = acc + _dot(scat, cp_ref[jj])
        yp[...] = acc
        _permute_rows(yp, y_ref, ts)

    return pl.pallas_call(
        body, name="ssm_fwd", grid=(nq, L // ts),
        in_specs=[pl.BlockSpec((ts, 128), lambda q, t: (t, 4 + q)),
                  pl.BlockSpec((None, 4, 128, 256), lambda q, t: (layer, q, 0, 0)),
                  pl.BlockSpec((None, 4, 256, 128), lambda q, t: (layer, q, 0, 0)),
                  pl.BlockSpec((None, 1, cq), lambda q, t: (layer, 0, q)),
                  pl.BlockSpec((None, 1, cq), lambda q, t: (layer, 0, q)),
                  pl.BlockSpec((None, 1, 128), lambda q, t: (layer, 0, q))],
        out_specs=[pl.BlockSpec((ts, cq), lambda q, t: (t, q)),
                   pl.BlockSpec((ts, cq), lambda q, t: (t, q)),
                   pl.BlockSpec((ts, 128), lambda q, t: (t, q))],
        out_shape=[jax.ShapeDtypeStruct((L, N_STATE), F32), jax.ShapeDtypeStruct((L, N_STATE), F32),
                   jax.ShapeDtypeStruct((L, D_SSM), F32)],
        scratch_shapes=[pltpu.VMEM((SUBLANES, cq), F32), pltpu.VMEM((SUBLANES, cq), F32),
                        pltpu.VMEM((N_SCAN_TABLES, SUBLANES, cq), F32),
                        pltpu.VMEM((ts, 128), F32), pltpu.VMEM((ts, 128), F32)],
        compiler_params=_cparams(2),
    )(u, bpad, cpad, ar, ai, dskip)


def _mix_out_fwd(yraw, ypool, h, wp, layer, b_glu):
    L = h.shape[0]
    tm = min(TM, L)

    def body(yr_ref, yp_ref, h_ref, wglu_ref, b_ref, wout_ref, o_ref):
        y = _gelu(yr_ref[...])
        z = _dot(y.astype(BF16), _glu_weight(wglu_ref)) + b_ref[...]
        o = y * _sigmoid(z)
        mix = jnp.concatenate([yp_ref[...], o], axis=1).astype(BF16)
        o_ref[...] = h_ref[...] + _dot(mix, wout_ref[...].reshape(D_MODEL, D_MODEL))

    gb, gi = P_GLU_BLK
    ob, oi = P_OUT_BLK
    return pl.pallas_call(
        body, name="mix_out_fwd", grid=(L // tm,),
        in_specs=[pl.BlockSpec((tm, D_SSM), lambda i: (i, 0)),
                  pl.BlockSpec((tm, D_POOL), lambda i: (i, 0)),
                  pl.BlockSpec((tm, D_MODEL), lambda i: (i, 0)),
                  pl.BlockSpec((N_SHARD, None, gb, D_MODEL), lambda i: (0, 0, gi, 0)),
                  pl.BlockSpec((None, 1, D_SSM), lambda i: (layer, 0, 0)),
                  pl.BlockSpec((N_SHARD, None, ob, D_MODEL), lambda i: (0, 0, oi, 0))],
        out_specs=pl.BlockSpec((tm, D_MODEL), lambda i: (i, 0)),
        out_shape=jax.ShapeDtypeStruct((L, D_MODEL), F32),
        compiler_params=_cparams(1),
    )(yraw, ypool, h, wp, b_glu, wp)


def _ffn_weights(ref, k):
    return ref[k, 0:FF_SHARD, :], ref[k, FF_SHARD:2 * FF_SHARD, :], ref[k, 2 * FF_SHARD:P_FF_ROWS, :]


def _ffn_weight_spec():
    return pl.BlockSpec((N_SHARD, None, P_FF_ROWS, D_MODEL), lambda m, k: (0, 0, 0, 0),
                        pipeline_mode=pl.Buffered(1))


def _ffn_fwd(h, g2, wp, layer):
    L = h.shape[0]
    tm = min(TM_FFN_LONG, L)

    def body(h_ref, g_ref, w_ref, o_ref, n2_ref, act_ref, dgate_ref, dup_ref):
        k = pl.program_id(1)

        @pl.when(k == 0)
        def _():
            x = h_ref[...]
            xhat, _ = _rms_hat(x)
            n2_ref[...] = (xhat * g_ref[...]).astype(BF16)
            o_ref[...] = x

        wd, wg_t, wu_t = _ffn_weights(w_ref, k)
        n2 = n2_ref[...]
        gate = _dot_nt(n2, wg_t)
        up = _dot_nt(n2, wu_t)
        sg = _sigmoid(gate)
        silu = gate * sg
        act = (silu * up).astype(BF16)
        act_ref[...] = act
        dgate_ref[...] = (up * (sg * (1.0 + gate * (1.0 - sg)))).astype(BF16)
        dup_ref[...] = silu.astype(BF16)
        o_ref[...] += _dot(act, wd)

    act_shape = jax.ShapeDtypeStruct((N_SHARD, L, FF_SHARD), BF16)
    return pl.pallas_call(
        body, name="ffn_fwd", grid=(L // tm, N_SHARD),
        in_specs=[pl.BlockSpec((tm, D_MODEL), lambda m, k: (m, 0)),
                  pl.BlockSpec((None, 1, D_MODEL), lambda m, k: (layer, 0, 0)),
                  _ffn_weight_spec()],
        out_specs=[pl.BlockSpec((tm, D_MODEL), lambda m, k: (m, 0)),
                   pl.BlockSpec((tm, D_MODEL), lambda m, k: (m, 0)),
                   pl.BlockSpec((None, tm, FF_SHARD), lambda m, k: (k, m, 0)),
                   pl.BlockSpec((None, tm, FF_SHARD), lambda m, k: (k, m, 0)),
                   pl.BlockSpec((None, tm, FF_SHARD), lambda m, k: (k, m, 0))],
        out_shape=[jax.ShapeDtypeStruct((L, D_MODEL), F32), jax.ShapeDtypeStruct((L, D_MODEL), BF16),
                   act_shape, act_shape, act_shape],
        compiler_params=_cparams(2),
    )(h, g2, wp)


def _final_fwd_bwd(h, gf, target):
    L = h.shape[0]
    tm = min(TM, L)

    def body(h_ref, g_ref, t_ref, dh_ref, loss_ref, dg_ref):
        i = pl.program_id(0)

        @pl.when(i == 0)
        def _():
            loss_ref[...] = jnp.zeros_like(loss_ref)
            dg_ref[...] = jnp.zeros_like(dg_ref)

        xhat, r = _rms_hat(h_ref[...])
        g = g_ref[...]
        e = xhat * g - t_ref[...]
        loss_ref[...] += 0.5 * jnp.sum(jnp.mean(e * e, axis=-1, keepdims=True), axis=0, keepdims=True)
        dy = e * (1.0 / D_MODEL)
        dg_ref[...] += jnp.sum(dy * xhat, axis=0, keepdims=True)
        dh_ref[...] = _rms_bwd(dy * g, xhat, r)

    return pl.pallas_call(
        body, name="final_fwd_bwd", grid=(L // tm,),
        in_specs=[pl.BlockSpec((tm, D_MODEL), lambda i: (i, 0)),
                  pl.BlockSpec((1, D_MODEL), lambda i: (0, 0)),
                  pl.BlockSpec((tm, D_MODEL), lambda i: (i, 0))],
        out_specs=[pl.BlockSpec((tm, D_MODEL), lambda i: (i, 0)),
                   pl.BlockSpec((1, 1), lambda i: (0, 0)),
                   pl.BlockSpec((1, D_MODEL), lambda i: (0, 0))],
        out_shape=[jax.ShapeDtypeStruct((L, D_MODEL), F32), jax.ShapeDtypeStruct((1, 1), F32),
                   jax.ShapeDtypeStruct((1, D_MODEL), F32)],
        compiler_params=_cparams(1),
    )(h, gf, target)


def _ffn_bwd_act(dh, h, g2, fgate_s, fup_s, wp, layer):
    dgate_s, dup_s, dhb = _ffn_bwd_gates(dh, fgate_s, fup_s, wp)
    dhm, dg2 = _ffn_bwd_in(dh, h, g2, dgate_s, dup_s, wp, layer)
    return dhm, dg2, dgate_s, dup_s, dhb


def _ffn_bwd_gates(dh, fgate_s, fup_s, wp):
    L = dh.shape[0]
    tm = min(TM_FFN_LONG, L)

    def body(dh_ref, fgate_ref, fup_ref, wd_ref, dgate_ref, dup_ref, dhb_ref):
        k = pl.program_id(1)

        @pl.when(k == 0)
        def _():
            dhb_ref[...] = dh_ref[...].astype(BF16)

        dact = _dot_nt(dhb_ref[...], wd_ref[k])
        dgate_ref[...] = (dact * fgate_ref[...].astype(F32)).astype(BF16)
        dup_ref[...] = (dact * fup_ref[...].astype(F32)).astype(BF16)

    act_spec = pl.BlockSpec((None, tm, FF_SHARD), lambda m, k: (k, m, 0))
    act_shape = jax.ShapeDtypeStruct((N_SHARD, L, FF_SHARD), BF16)
    row_spec = pl.BlockSpec((tm, D_MODEL), lambda m, k: (m, 0))
    wb, wi = P_WD_BLK
    return pl.pallas_call(
        body, name="ffn_bwd_gates", grid=(L // tm, N_SHARD),
        in_specs=[row_spec, act_spec, act_spec,
                  pl.BlockSpec((N_SHARD, None, wb, D_MODEL), lambda m, k: (0, 0, wi, 0), pipeline_mode=pl.Buffered(1))],
        out_specs=[act_spec, act_spec, row_spec],
        out_shape=[act_shape, act_shape, jax.ShapeDtypeStruct((L, D_MODEL), BF16)],
        compiler_params=_cparams(2),
    )(dh, fgate_s, fup_s, wp)


def _ffn_bwd_in(dh, h, g2, dgate_s, dup_s, wp, layer):
    L = h.shape[0]
    tm = min(TM_FFN, L)

    def body(dh_ref, h_ref, g_ref, dgate_ref, dup_ref, w_ref, dhm_ref, dg_ref):
        m, k = pl.program_id(0), pl.program_id(1)

        @pl.when(jnp.logical_and(m == 0, k == 0))
        def _():
            dg_ref[...] = jnp.zeros_like(dg_ref)

        _, wg_t, wu_t = _ffn_weights(w_ref, k)
        d = _dot(dgate_ref[...], wg_t) + _dot(dup_ref[...], wu_t)

        @pl.when(k == 0)
        def _():
            dhm_ref[...] = d

        @pl.when(k > 0)
        def _():
            dhm_ref[...] += d

        @pl.when(k == N_SHARD - 1)
        def _():
            xhat, r = _rms_hat(h_ref[...])
            dn2 = dhm_ref[...]
            dg_ref[...] += jnp.sum(dn2 * xhat, axis=0, keepdims=True)
            dhm_ref[...] = dh_ref[...] + _rms_bwd(dn2 * g_ref[...], xhat, r)

    act_spec = pl.BlockSpec((None, tm, FF_SHARD), lambda m, k: (k, m, 0))
    row_spec = pl.BlockSpec((tm, D_MODEL), lambda m, k: (m, 0))
    return pl.pallas_call(
        body, name="ffn_bwd_in", grid=(L // tm, N_SHARD),
        in_specs=[row_spec, row_spec,
                  pl.BlockSpec((None, 1, D_MODEL), lambda m, k: (layer, 0, 0)),
                  act_spec, act_spec,
                  _ffn_weight_spec()],
        out_specs=[row_spec, pl.BlockSpec((1, D_MODEL), lambda m, k: (0, 0))],
        out_shape=[jax.ShapeDtypeStruct((L, D_MODEL), F32), jax.ShapeDtypeStruct((1, D_MODEL), F32)],
        compiler_params=_cparams(2),
    )(dh, h, g2, dgate_s, dup_s, wp)


def _ffn_bwd_w(n2, dgate_s, dup_s, act_s, dhb, gbuf):
    L = n2.shape[0]
    tm = min(TM_FFN_LONG, L)

    def body(n2_ref, dgate_ref, dup_ref, act_ref, dhb_ref, g_in, g_ref):
        m = pl.program_id(1)

        @pl.when(m == 0)
        def _():
            g_ref[...] = jnp.zeros_like(g_ref)

        n2v = n2_ref[...]
        g_ref[0:FF_SHARD, :] += _dot_tn(act_ref[...], dhb_ref[...])
        g_ref[FF_SHARD:2 * FF_SHARD, :] += _dot_tn(dgate_ref[...], n2v)
        g_ref[2 * FF_SHARD:P_FF_ROWS, :] += _dot_tn(dup_ref[...], n2v)

    act_spec = pl.BlockSpec((None, tm, FF_SHARD), lambda k, m: (k, m, 0))
    row_spec = pl.BlockSpec((tm, D_MODEL), lambda k, m: (m, 0))
    return pl.pallas_call(
        body, name="ffn_bwd_w", grid=(N_SHARD, L // tm),
        in_specs=[row_spec, act_spec, act_spec, act_spec, row_spec, pl.BlockSpec(memory_space=pl.ANY)],
        out_specs=pl.BlockSpec((None, None, P_FF_ROWS, D_MODEL), lambda k, m: (0, k, 0, 0)),
        out_shape=jax.ShapeDtypeStruct(gbuf.shape, F32),
        input_output_aliases={5: 0},
        compiler_params=_cparams(2),
    )(n2, dgate_s, dup_s, act_s, dhb, gbuf)


def _mix_out_bwd(dhm, yraw, ypool, wp, layer, b_glu, gbuf):
    L = dhm.shape[0]
    tm = min(TM, L)

    def body(dhm_ref, yr_ref, yp_ref, wglu_ref, b_ref, wout_ref, g1_in,
             dyr_ref, dyp_ref, db_ref, g1_ref, dwout, dwglu, gpack):
        i = pl.program_id(0)

        @pl.when(i == 0)
        def _():
            db_ref[...] = jnp.zeros_like(db_ref)
            dwout[...] = jnp.zeros_like(dwout)
            dwglu[...] = jnp.zeros_like(dwglu)

        dhb = dhm_ref[...].astype(BF16)
        wglu = _glu_weight(wglu_ref)
        dmix = _dot_nt(dhb, wout_ref[...].reshape(D_MODEL, D_MODEL))
        dyp_ref[...] = dmix[:, :D_POOL]
        d_o = dmix[:, D_POOL:]
        yraw_v = yr_ref[...]
        y = _gelu(yraw_v)
        yb = y.astype(BF16)
        sig = _sigmoid(_dot(yb, wglu) + b_ref[...])
        mix = jnp.concatenate([yp_ref[...], y * sig], axis=1).astype(BF16)
        dwout[...] += _dot_tn(mix, dhb).reshape(N_SHARD, 256, D_MODEL)
        dz = d_o * y * sig * (1.0 - sig)
        dzb = dz.astype(BF16)
        db_ref[...] += jnp.sum(dz, axis=0, keepdims=True)
        dwglu[...] += _dot_tn(yb, dzb)
        dy = d_o * sig + _dot_nt(dzb, wglu)
        dyr_ref[...] = dy * _gelu_grad(yraw_v)

        @pl.when(i == n_steps - 1)
        def _():
            gpack[:, :gb, :] = _glu_pack(dwglu[...])
            gpack[:, gb:, :] = jnp.zeros((N_SHARD, P_GLU_PAD - gb, D_MODEL), F32)
            pltpu.sync_copy(gpack, g1_ref.at[0, :, pl.ds(gb * gi, P_GLU_PAD), :])
            pltpu.sync_copy(dwout, g1_ref.at[0, :, pl.ds(ob * oi, ob), :])

    gb, gi = P_GLU_BLK
    ob, oi = P_OUT_BLK
    n_steps = L // tm
    return pl.pallas_call(
        body, name="mix_out_bwd", grid=(n_steps,),
        in_specs=[pl.BlockSpec((tm, D_MODEL), lambda i: (i, 0)),
                  pl.BlockSpec((tm, D_SSM), lambda i: (i, 0)),
                  pl.BlockSpec((tm, D_POOL), lambda i: (i, 0)),
                  pl.BlockSpec((N_SHARD, None, gb, D_MODEL), lambda i: (0, 0, gi, 0)),
                  pl.BlockSpec((None, 1, D_SSM), lambda i: (layer, 0, 0)),
                  pl.BlockSpec((N_SHARD, None, ob, D_MODEL), lambda i: (0, 0, oi, 0)),
                  pl.BlockSpec(memory_space=pl.ANY)],
        out_specs=[pl.BlockSpec((tm, D_SSM), lambda i: (i, 0)),
                   pl.BlockSpec((tm, D_POOL), lambda i: (i, 0)),
                   pl.BlockSpec((1, D_SSM), lambda i: (0, 0)),
                   pl.BlockSpec(memory_space=pl.ANY)],
        out_shape=[jax.ShapeDtypeStruct((L, D_SSM), F32), jax.ShapeDtypeStruct((L, D_POOL), F32),
                   jax.ShapeDtypeStruct((1, D_SSM), F32),
                   jax.ShapeDtypeStruct(gbuf.shape, F32)],
        scratch_shapes=[pltpu.VMEM((N_SHARD, ob, D_MODEL), F32), pltpu.VMEM((D_SSM, D_SSM), F32),
                        pltpu.VMEM((N_SHARD, P_GLU_PAD, D_MODEL), F32)],
        input_output_aliases={6: 3},
        compiler_params=_cparams(1),
    )(dhm, yraw, ypool, wp, b_glu, wp, gbuf)


def _ssm_bwd(dyraw, u, sre, sim, layer, cpad_t, bpad_t, ar, ai, dskip):
    L = u.shape[0]
    ts = min(TS, L)
    nt = L // ts
    nq = 4
    cq = N_STATE // nq

    def body(dy_ref, u_ref, sre_ref, sim_ref, ct_ref, bt_ref, ar_ref, ai_ref, dsk_ref,
             du_ref, dcp_ref, dbp_ref, dar_ref, dai_ref, ddsk_ref, gre, gim, cr, ci, tab, accr, acci, up, dyp):
        t = pl.program_id(1)

        @pl.when(t == 0)
        def _():
            for ref in (cr, ci, accr, acci, dcp_ref, dbp_ref, ddsk_ref):
                ref[...] = jnp.zeros_like(ref)
            _scan_tables(ar_ref[...], -ai_ref[...], tab, reverse=True)

        _permute_rows(dy_ref, dyp, ts)
        _permute_rows(u_ref, up, ts)
        dy = dyp[...]
        dyb = dy.astype(BF16)
        uf = up[...]
        ub = uf.astype(BF16)
        for jj in range(4):
            cols = slice(jj * 128, (jj + 1) * 128)
            ds = _dot(dyb, ct_ref[jj])
            gre[:, cols] = ds[:, :128]
            gim[:, cols] = ds[:, 128:]
            scat = jnp.concatenate([sre_ref[:, cols], sim_ref[:, cols]], axis=1).astype(BF16)
            dcp_ref[jj] += _dot_tn(scat, dyb)

        n_blk = ts // SCAN_BLOCK
        shp = (SUBLANES, SCAN_LANES)
        last_row = lax.broadcasted_iota(jnp.int32, shp, 0) == SUBLANES - 1
        for cc in range(cq // SCAN_LANES):
            cols = slice(cc * SCAN_LANES, (cc + 1) * SCAN_LANES)

            def block(i, carry, cols=cols):
                c_r, c_i, a_r, a_i = carry
                base = pl.multiple_of((n_blk - 1 - i) * SCAN_BLOCK, SCAN_BLOCK)
                rows = lambda tau: pl.ds(base + SUBLANES * tau, SUBLANES)
                m_r, m_i = tab[0, :, cols], tab[1, :, cols]
                ys = [None] * SUBLANES
                ys[SUBLANES - 1] = (gre[rows(SUBLANES - 1), cols], gim[rows(SUBLANES - 1), cols])
                for tau in reversed(range(SUBLANES - 1)):
                    ys[tau] = _cmac(gre[rows(tau), cols], gim[rows(tau), cols], m_r, m_i, *ys[tau + 1])
                tr, ti = _chain_segments(*ys[0], c_r, c_i, tab, cols, reverse=True)
                in_r = jnp.where(last_row, c_r, pltpu.roll(tr, SUBLANES - 1, 0))
                in_i = jnp.where(last_row, c_i, pltpu.roll(ti, SUBLANES - 1, 0))
                gs = [_cmac(*ys[tau], tab[10 + 2 * tau, :, cols], tab[11 + 2 * tau, :, cols], in_r, in_i)
                      for tau in range(SUBLANES)]
                for tau in range(SUBLANES):
                    gre[rows(tau), cols] = gs[tau][0]
                    gim[rows(tau), cols] = gs[tau][1]
                    if tau < SUBLANES - 1:
                        nr, ni = gs[tau + 1]
                    else:
                        nr = jnp.where(last_row, c_r, pltpu.roll(gs[0][0], SUBLANES - 1, 0))
                        ni = jnp.where(last_row, c_i, pltpu.roll(gs[0][1], SUBLANES - 1, 0))
                    sr, si = sre_ref[rows(tau), cols], sim_ref[rows(tau), cols]
                    a_r = a_r + sr * nr + si * ni
                    a_i = a_i + sr * ni - si * nr
                return (jnp.broadcast_to(tr[:1, :], shp), jnp.broadcast_to(ti[:1, :], shp), a_r, a_i)

            c_r, c_i, a_r, a_i = lax.fori_loop(
                0, n_blk, block, (cr[:, cols], ci[:, cols], accr[:, cols], acci[:, cols]), unroll=2)
            cr[:, cols] = c_r
            ci[:, cols] = c_i
            accr[:, cols] = a_r
            acci[:, cols] = a_i

        acc = dsk_ref[...] * dy
        for jj in range(4):
            cols = slice(jj * 128, (jj + 1) * 128)
            gcat = jnp.concatenate([gre[:, cols], gim[:, cols]], axis=1).astype(BF16)
            acc = acc + _dot(gcat, bt_ref[jj])
            dbp_ref[jj] += _dot_tn(ub, gcat)
        ddsk_ref[...] += jnp.sum(dy * uf, axis=0, keepdims=True)
        dyp[...] = acc
        _permute_rows(dyp, du_ref, ts)

        @pl.when(t == nt - 1)
        def _():
            dar_ref[...] = jnp.sum(accr[...], axis=0, keepdims=True)
            dai_ref[...] = jnp.sum(acci[...], axis=0, keepdims=True)

    f32_scr = lambda *s: pltpu.VMEM(s, F32)
    return pl.pallas_call(
        body, name="ssm_bwd", grid=(nq, nt),
        in_specs=[pl.BlockSpec((ts, 128), lambda q, t: (nt - 1 - t, q)),
                  pl.BlockSpec((ts, 128), lambda q, t: (nt - 1 - t, 4 + q)),
                  pl.BlockSpec((ts, cq), lambda q, t: (nt - 1 - t, q)),
                  pl.BlockSpec((ts, cq), lambda q, t: (nt - 1 - t, q)),
                  pl.BlockSpec((None, 4, 128, 256), lambda q, t: (layer, q, 0, 0)),
                  pl.BlockSpec((None, 4, 256, 128), lambda q, t: (layer, q, 0, 0)),
                  pl.BlockSpec((None, 1, cq), lambda q, t: (layer, 0, q)),
                  pl.BlockSpec((None, 1, cq), lambda q, t: (layer, 0, q)),
                  pl.BlockSpec((None, 1, 128), lambda q, t: (layer, 0, q))],
        out_specs=[pl.BlockSpec((ts, 128), lambda q, t: (nt - 1 - t, q)),
                   pl.BlockSpec((4, 256, 128), lambda q, t: (q, 0, 0)),
                   pl.BlockSpec((4, 128, 256), lambda q, t: (q, 0, 0)),
                   pl.BlockSpec((1, cq), lambda q, t: (0, q)),
                   pl.BlockSpec((1, cq), lambda q, t: (0, q)),
                   pl.BlockSpec((1, 128), lambda q, t: (0, q))],
        out_shape=[jax.ShapeDtypeStruct((L, D_SSM), F32),
                   jax.ShapeDtypeStruct((N_PAIRS, 256, 128), F32), jax.ShapeDtypeStruct((N_PAIRS, 128, 256), F32),
                   jax.ShapeDtypeStruct((1, N_STATE), F32), jax.ShapeDtypeStruct((1, N_STATE), F32),
                   jax.ShapeDtypeStruct((1, D_SSM), F32)],
        scratch_shapes=[f32_scr(ts, cq), f32_scr(ts, cq), f32_scr(SUBLANES, cq), f32_scr(SUBLANES, cq),
                        f32_scr(N_SCAN_TABLES, SUBLANES, cq), f32_scr(SUBLANES, cq), f32_scr(SUBLANES, cq),
                        f32_scr(ts, 128), f32_scr(ts, 128)],
        compiler_params=_cparams(2),
    )(dyraw, u, sre, sim, cpad_t, bpad_t, ar, ai, dskip)


def _pool_bwd(dyp, u, layer, w_pool, scale):
    L = u.shape[0]
    tm = min(TM, L)
    nt = L // tm
    halo_per_tile = tm // POOL_HALO

    def body(dyp_ref, u_ref, halo_ref, wp_ref, sc_ref, du_ref, dwp_ref, dsc_ref, carry):
        i = pl.program_id(0)
        tile = nt - 1 - i

        @pl.when(i == 0)
        def _():
            carry[...] = jnp.zeros_like(carry)
            dwp_ref[...] = jnp.zeros_like(dwp_ref)
            dsc_ref[...] = jnp.zeros_like(dsc_ref)

        up = u_ref[...]
        halo = jnp.where(tile > 0, halo_ref[...], jnp.zeros_like(halo_ref))
        diffs = _pool_diff(jnp.concatenate([halo, up], axis=0), tile * tm, tm)
        rows = tile * tm + lax.broadcasted_iota(jnp.int32, (tm, 1), 0)
        n_ext = tm + POOL_HALO
        for gi, w in enumerate(POOL_WINDOWS):
            cols = slice(gi * POOL_GROUP, (gi + 1) * POOL_GROUP)
            db = diffs[gi].astype(BF16)
            dyp = dyp_ref[:, cols]
            dsc_ref[:, cols] += jnp.sum(dyp * _dot(db, wp_ref[gi]), axis=0, keepdims=True)
            dp = (dyp * sc_ref[:, cols]).astype(BF16)
            ddiff = _dot_nt(dp, wp_ref[gi])
            dwp_ref[gi] += _dot_tn(db, dp)
            e = ddiff * (1.0 / jnp.minimum(rows + 1, w).astype(F32))
            s = jnp.concatenate([e, carry[:, cols]], axis=0)
            k = 1
            while k < w:
                s = s + pltpu.roll(s, n_ext - k, 0)
                k *= 2
            du_ref[:, cols] = s[:tm, :] - ddiff
            carry[:, cols] = e[:POOL_HALO, :]

    return pl.pallas_call(
        body, name="pool_bwd", grid=(nt,),
        in_specs=[pl.BlockSpec((tm, D_POOL), lambda i: (nt - 1 - i, 0)),
                  pl.BlockSpec((tm, D_POOL), lambda i: (nt - 1 - i, 0)),
                  pl.BlockSpec((POOL_HALO, D_POOL), lambda i: (jnp.maximum((nt - 1 - i) * halo_per_tile - 1, 0), 0)),
                  pl.BlockSpec((None, 4, POOL_GROUP, POOL_GROUP), lambda i: (layer, 0, 0, 0)),
                  pl.BlockSpec((None, 1, D_POOL), lambda i: (layer, 0, 0))],
        out_specs=[pl.BlockSpec((tm, D_POOL), lambda i: (nt - 1 - i, 0)),
                   pl.BlockSpec((4, POOL_GROUP, POOL_GROUP), lambda i: (0, 0, 0)),
                   pl.BlockSpec((1, D_POOL), lambda i: (0, 0))],
        out_shape=[jax.ShapeDtypeStruct((L, D_POOL), F32),
                   jax.ShapeDtypeStruct((4, POOL_GROUP, POOL_GROUP), F32),
                   jax.ShapeDtypeStruct((1, D_POOL), F32)],
        scratch_shapes=[pltpu.VMEM((POOL_HALO, D_POOL), F32)],
        compiler_params=_cparams(1),
    )(dyp, u, u, w_pool, scale)


def _mix_in_bwd(dup, dus, h, dhm, g1, wp, layer, gbuf):
    L = h.shape[0]
    tm = min(TM, L)
    n_steps = L // tm
    blk, idx = P_IN_BLK

    def body(dup_ref, dus_ref, h_ref, dhm_ref, g_ref, w_ref, g1_in, dh_ref, dg_ref, g1_ref, dwin):
        i = pl.program_id(0)

        @pl.when(i == 0)
        def _():
            dg_ref[...] = jnp.zeros_like(dg_ref)
            dwin[...] = jnp.zeros_like(dwin)

        du = jnp.concatenate([dup_ref[...], dus_ref[...]], axis=1).astype(BF16)
        dn1 = _dot_nt(du, w_ref[...].reshape(D_MODEL, D_MODEL))
        xhat, r = _rms_hat(h_ref[...])
        g = g_ref[...]
        n1 = (xhat * g).astype(BF16)
        dwin[...] += _dot_tn(n1, du).reshape(N_SHARD, blk, D_MODEL)
        dg_ref[...] += jnp.sum(dn1 * xhat, axis=0, keepdims=True)
        dh_ref[...] = dhm_ref[...] + _rms_bwd(dn1 * g, xhat, r)

        @pl.when(i == n_steps - 1)
        def _():
            pltpu.sync_copy(dwin, g1_ref.at[0, :, pl.ds(blk * idx, blk), :])

    row_spec = pl.BlockSpec((tm, D_MODEL), lambda i: (i, 0))
    half_spec = pl.BlockSpec((tm, D_POOL), lambda i: (i, 0))
    return pl.pallas_call(
        body, name="mix_in_bwd", grid=(n_steps,),
        in_specs=[half_spec, half_spec, row_spec, row_spec,
                  pl.BlockSpec((None, 1, D_MODEL), lambda i: (layer, 0, 0)),
                  pl.BlockSpec((N_SHARD, None, blk, D_MODEL), lambda i: (0, 0, idx, 0)),
                  pl.BlockSpec(memory_space=pl.ANY)],
        out_specs=[row_spec, pl.BlockSpec((1, D_MODEL), lambda i: (0, 0)), pl.BlockSpec(memory_space=pl.ANY)],
        out_shape=[jax.ShapeDtypeStruct((L, D_MODEL), F32), jax.ShapeDtypeStruct((1, D_MODEL), F32),
                   jax.ShapeDtypeStruct(gbuf.shape, F32)],
        scratch_shapes=[pltpu.VMEM((N_SHARD, blk, D_MODEL), F32)],
        input_output_aliases={6: 2},
        compiler_params=_cparams(1),
    )(dup, dus, h, dhm, g1, wp, gbuf)


def _disc_math(lr, li, ldt, br_t, bi_t):
    dt = jnp.exp(ldt)
    mag = jnp.exp(lr * dt)
    ang = li * dt
    ar = mag * jnp.cos(ang)
    ai = mag * jnp.sin(ang)
    den = lr * lr + li * li
    nr, ni = ar - 1.0, ai
    cr = (nr * lr + ni * li) / den
    ci = (ni * lr - nr * li) / den
    return ar, ai, cr * br_t - ci * bi_t, cr * bi_t + ci * br_t


def _disc_fwd(lr, li, ldt, br_t, bi_t):
    def body(lr_ref, li_ref, ldt_ref, br_ref, bi_ref, ar_ref, ai_ref, bbr_ref, bbi_ref):
        ar, ai, bbr, bbi = _disc_math(lr_ref[...], li_ref[...], ldt_ref[...], br_ref[...], bi_ref[...])
        ar_ref[...] = ar
        ai_ref[...] = ai
        bbr_ref[...] = bbr
        bbi_ref[...] = bbi

    shapes = [jax.ShapeDtypeStruct(a.shape, F32) for a in (lr, li, br_t, bi_t)]
    return pl.pallas_call(body, name="ssm_disc_fwd", out_shape=shapes,
                          compiler_params=pltpu.CompilerParams(vmem_limit_bytes=VMEM_LIMIT))(lr, li, ldt, br_t, bi_t)


def _disc_bwd(lr, li, ldt, br_t, bi_t, dar, dai, dbbr, dbbi):
    def body(lr_ref, li_ref, ldt_ref, br_ref, bi_ref, dar_ref, dai_ref, dbbr_ref, dbbi_ref,
             dlr_ref, dli_ref, dldt_ref, dbr_ref, dbi_ref):
        prim = (lr_ref[...], li_ref[...], ldt_ref[...], br_ref[...], bi_ref[...])
        _, pullback = jax.vjp(_disc_math, *prim)
        dlr, dli, dldt, dbr, dbi = pullback((dar_ref[...], dai_ref[...], dbbr_ref[...], dbbi_ref[...]))
        dlr_ref[...] = dlr
        dli_ref[...] = dli
        dldt_ref[...] = dldt
        dbr_ref[...] = dbr
        dbi_ref[...] = dbi

    shapes = [jax.ShapeDtypeStruct(a.shape, F32) for a in (lr, li, ldt, br_t, bi_t)]
    return pl.pallas_call(body, name="ssm_disc_bwd", out_shape=shapes,
                          compiler_params=pltpu.CompilerParams(vmem_limit_bytes=VMEM_LIMIT))(
        lr, li, ldt, br_t, bi_t, dar, dai, dbbr, dbbi)


def _pad_pairs(m_re, m_im):
    def blocks(m):
        v = m.transpose(0, 2, 1).reshape(N_PAIRS, 2, SSM_GROUP, SSM_STATE)
        return jnp.einsum("ab,jahp->jahbp", jnp.eye(2, dtype=m.dtype), v).reshape(N_PAIRS, 32, 128)
    both = jnp.concatenate([blocks(m_re), blocks(m_im)], axis=-1)
    place = jax.nn.one_hot(jnp.arange(N_PAIRS) % 4, 4, dtype=both.dtype)
    return jnp.einsum("jk,jrc->jkrc", place, both).reshape(N_PAIRS, 128, 256)


def _unpad_pairs(x):
    place = jax.nn.one_hot(jnp.arange(N_PAIRS) % 4, 4, dtype=x.dtype)
    both = jnp.einsum("jk,jkrc->jrc", place, x.reshape(N_PAIRS, 4, 32, 256))

    def unblock(v):
        v = v.reshape(N_PAIRS, 2, SSM_GROUP, 2, SSM_STATE)
        d = jnp.einsum("ab,jahbp->jahp", jnp.eye(2, dtype=x.dtype), v)
        return d.reshape(N_SSM_GROUPS, SSM_GROUP, SSM_STATE).transpose(0, 2, 1)
    return unblock(both[..., :128]), unblock(both[..., 128:])


def _adamw_math(w, g, m, v):
    m = ADAM_B1 * m + (1.0 - ADAM_B1) * g
    v = ADAM_B2 * v + (1.0 - ADAM_B2) * (g * g)
    m_hat = m / (1.0 - ADAM_B1 ** ADAM_STEP)
    v_hat = v / (1.0 - ADAM_B2 ** ADAM_STEP)
    delta = -ADAM_LR * (m_hat / (jnp.sqrt(v_hat) + ADAM_EPS) + ADAM_WD * w)
    return delta, m, v


def _adamw(name, layer, w, m, v, gbuf, g_block, g_row0, row_tile, outs=None, after=(), glu=False):
    nl, r, c = w.shape
    n_tiles = r // row_tile
    g_rows, g_cols = g_block
    g_tile = g_rows // n_tiles
    g_off = g_row0 // g_tile
    if outs is None:
        outs = [lax.empty(w.shape, F32) for _ in range(4)]

    def body(w_ref, m_ref, v_ref, g_ref, *rest):
        go_ref, d_ref, mo_ref, vo_ref = rest[-4:]
        g = g_ref[...]
        if glu:
            g = jnp.concatenate([g[:, :D_SSM], g[:, D_SSM:]], axis=0)
        delta, mn, vn = _adamw_math(w_ref[...], g, m_ref[...], v_ref[...])
        go_ref[...] = g
        d_ref[...] = delta
        mo_ref[...] = mn
        vo_ref[...] = vn

    w_spec = pl.BlockSpec((None, row_tile, c), lambda j: (layer, j, 0))
    shape = jax.ShapeDtypeStruct(w.shape, F32)
    return pl.pallas_call(
        body, name=name, grid=(n_tiles,),
        in_specs=[w_spec, w_spec, w_spec, pl.BlockSpec((None, g_tile, g_cols), lambda j: (0, g_off + j, 0))]
        + [_ANY] * (4 + len(after)),
        out_specs=[w_spec] * 4,
        out_shape=[shape] * 4,
        input_output_aliases={4: 0, 5: 1, 6: 2, 7: 3},
        compiler_params=_cparams(1),
    )(w, m, v, gbuf, *outs, *after)


def _adamw_group(name, layer, ws, ms, vs, gbuf, g_row0s, row_tile, outs=None):
    k = len(ws)
    nl, r, c = ws[0].shape
    n_tiles = r // row_tile
    if outs is None:
        outs = [[lax.empty(ws[0].shape, F32) for _ in range(4)] for _ in range(k)]

    def body(*refs):
        ins, results = refs[:4 * k], refs[-4 * k:]
        for i in range(k):
            w_ref, m_ref, v_ref, g_ref = (ins[j * k + i] for j in range(4))
            g = g_ref[...]
            delta, mn, vn = _adamw_math(w_ref[...], g, m_ref[...], v_ref[...])
            for ref, val in zip(results[4 * i:4 * i + 4], (g, delta, mn, vn)):
                ref[...] = val

    w_spec = pl.BlockSpec((None, row_tile, c), lambda j: (layer, j, 0))
    g_specs = [pl.BlockSpec((None, row_tile, c), functools.partial(lambda j, off: (0, off + j, 0), off=r0 // row_tile))
               for r0 in g_row0s]
    shape = jax.ShapeDtypeStruct(ws[0].shape, F32)
    flat = pl.pallas_call(
        body, name=name, grid=(n_tiles,),
        in_specs=[w_spec] * (3 * k) + g_specs + [_ANY] * (4 * k),
        out_specs=[w_spec] * (4 * k),
        out_shape=[shape] * (4 * k),
        input_output_aliases={4 * k + i: i for i in range(4 * k)},
        compiler_params=_cparams(1),
    )(*ws, *ms, *vs, *([gbuf] * k), *[a for group in outs for a in group])
    return [flat[4 * i:4 * i + 4] for i in range(k)]


def _pack_weights(ids, layer, w_in, w_glu, w_out, w_down, w_gate_t, w_up_t, after=()):
    gb, gi = P_GLU_BLK
    ib, ii = P_IN_BLK
    ob, oi = P_OUT_BLK

    def body(ids_ref, in_ref, glu_ref, out_ref, dn_ref, gate_ref, up_ref, *rest):
        p_ref = rest[-1]
        p_ref[0:FF_SHARD, :] = dn_ref[...].astype(BF16)
        p_ref[FF_SHARD:2 * FF_SHARD, :] = gate_ref[...].astype(BF16)
        p_ref[2 * FF_SHARD:P_FF_ROWS, :] = up_ref[...].astype(BF16)
        g = glu_ref[...]
        p_ref[gb * gi:gb * (gi + 1), :] = jnp.concatenate([g[:gb, :], g[gb:, :]], axis=1).astype(BF16)
        p_ref[gb * (gi + 1):ib * ii, :] = jnp.zeros((P_GLU_PAD - gb, D_MODEL), BF16)
        p_ref[ib * ii:ib * (ii + 1), :] = in_ref[...].astype(BF16)
        p_ref[ob * oi:ob * (oi + 1), :] = out_ref[...].astype(BF16)

    def spec(a):
        return pl.BlockSpec((None,) + a.shape[1:], lambda i, ids_ref: (layer, 0, 0))

    ins = (w_in, w_glu, w_out, w_down, w_gate_t, w_up_t)
    grid_spec = pltpu.PrefetchScalarGridSpec(
        num_scalar_prefetch=1, grid=(1,),
        in_specs=[spec(a) for a in ins] + [_ANY] * len(after),
        out_specs=pl.BlockSpec((None, None, P_ROWS, D_MODEL), lambda i, ids_ref: (ids_ref[1], 0, 0, 0)))
    return pl.pallas_call(
        body, name="pack_weights", grid_spec=grid_spec,
        out_shape=jax.ShapeDtypeStruct((N_SHARD, 1, P_ROWS, D_MODEL), BF16),
        compiler_params=_cparams(1),
    )(ids, *ins, *after)


MESH = pl.DeviceIdType.MESH
_ANY = pl.BlockSpec(memory_space=pl.ANY)
P_HALF = P_ROWS // 2
RS_ADD_TILE = 1408
RS_SUM_TILE = 352


def _mesh_pos():
    return lax.axis_index("x"), lax.axis_index("y"), lax.axis_index("c")


def _other_chips(x, y):
    return [(1 - x, y), (x, 1 - y), (1 - x, 1 - y)]


def _remote(src, dst, send_sems, recv_sems, n, to):
    return pltpu.make_async_remote_copy(src_ref=src, dst_ref=dst, send_sem=send_sems.at[n],
                                        recv_sem=recv_sems.at[n], device_id=to, device_id_type=MESH)


_HBM = pl.BlockSpec(memory_space=pltpu.HBM)
_SEM = pl.BlockSpec(memory_space=pltpu.SEMAPHORE)
_EFFECT = pltpu.CompilerParams(has_side_effects=pltpu.SideEffectType.DATAFLOW_SIDE_EFFECTING)
_TOKEN = jax.ShapeDtypeStruct((8, 128), F32)


def _in_hbm(a):
    return pltpu.with_memory_space_constraint(a, pltpu.HBM)


def _ag_piece(ref, shard, half, rows):
    row0, n_rows = rows
    return ref.at[shard, :, pl.ds(row0 + half * (n_rows // 2), n_rows // 2), :]


def _ag_start(name, wp, after, row_ranges):
    n_sems = 3 * len(row_ranges)

    def body(w_ref, after_ref, send_sems, recv_sems, w_thru, token):
        x, y, c = _mesh_pos()
        for i, rows in enumerate(row_ranges):
            mine = _ag_piece(w_ref, 2 * x + y, c, rows)
            for j, (px, py) in enumerate(_other_chips(x, y)):
                _remote(mine, mine, send_sems, recv_sems, 3 * i + j, (px, py, c)).start()
        token[...] = jnp.zeros_like(token)

    return pl.pallas_call(
        body, name=name,
        out_shape=(pltpu.SemaphoreType.DMA((n_sems,)), pltpu.SemaphoreType.DMA((n_sems,)),
                   pltpu.HBM(wp.shape, wp.dtype), _TOKEN),
        in_specs=(_HBM, _ANY), out_specs=(_SEM, _SEM, _HBM, pl.BlockSpec(memory_space=pltpu.VMEM)),
        input_output_aliases={0: 2}, compiler_params=_EFFECT,
    )(_in_hbm(wp), after)


def _ag_wait(name, send_sems, recv_sems, wp, after, row_ranges):
    def body(w_ref, send_sems, recv_sems, *rest):
        x, y, c = _mesh_pos()
        for i, rows in enumerate(row_ranges):
            mine = _ag_piece(w_ref, 2 * x + y, c, rows)
            for j, (px, py) in enumerate(_other_chips(x, y)):
                landed = _ag_piece(w_ref, 2 * px + py, c, rows)
                cp = _remote(mine, landed, send_sems, recv_sems, 3 * i + j, (px, py, c))
                cp.wait_send()
                cp.wait_recv()

    return pl.pallas_call(
        body, name=name, out_shape=pltpu.HBM(wp.shape, wp.dtype),
        in_specs=(_HBM, _SEM, _SEM) + (_ANY,) * len(after), out_specs=_HBM,
        input_output_aliases={0: 0}, compiler_params=_EFFECT,
    )(wp, send_sems, recv_sems, *after)


def _ag_forward(wp, rows):
    def body(w_in, o, send_sems, recv_sems):
        x, y, c = _mesh_pos()
        sib = (x, y, 1 - c)
        chips = _other_chips(x, y)
        sends = []
        for j, (px, py) in enumerate(chips):
            landed = _ag_piece(o, 2 * px + py, c, rows)
            cp = _remote(landed, landed, send_sems, recv_sems, j, sib)
            cp.start()
            sends.append(cp)
        for j, (px, py) in enumerate(chips):
            passed = _ag_piece(o, 2 * px + py, 1 - c, rows)
            _remote(passed, passed, send_sems, recv_sems, j, sib).wait_recv()
        for cp in sends:
            cp.wait_send()

    return pl.pallas_call(
        body, name="ag_forward",
        in_specs=[_ANY], out_specs=_ANY,
        out_shape=jax.ShapeDtypeStruct(wp.shape, wp.dtype),
        scratch_shapes=[pltpu.SemaphoreType.DMA((3,)), pltpu.SemaphoreType.DMA((3,))],
        input_output_aliases={0: 0},
    )(wp)


def _ag_forward_start(name, wp, rows):
    def body(w_ref, send_sems, recv_sems, w_thru):
        x, y, c = _mesh_pos()
        for j, (px, py) in enumerate(_other_chips(x, y)):
            landed = _ag_piece(w_ref, 2 * px + py, c, rows)
            _remote(landed, landed, send_sems, recv_sems, j, (x, y, 1 - c)).start()

    return pl.pallas_call(
        body, name=name,
        out_shape=(pltpu.SemaphoreType.DMA((3,)), pltpu.SemaphoreType.DMA((3,)), pltpu.HBM(wp.shape, wp.dtype)),
        in_specs=(_HBM,), out_specs=(_SEM, _SEM, _HBM),
        input_output_aliases={0: 2}, compiler_params=_EFFECT,
    )(_in_hbm(wp))


def _ag_forward_wait(name, send_sems, recv_sems, wp, after, rows):
    def body(w_ref, send_sems, recv_sems, *rest):
        x, y, c = _mesh_pos()
        for j, (px, py) in enumerate(_other_chips(x, y)):
            cp = _remote(_ag_piece(w_ref, 2 * px + py, c, rows), _ag_piece(w_ref, 2 * px + py, 1 - c, rows),
                         send_sems, recv_sems, j, (x, y, 1 - c))
            cp.wait_send()
            cp.wait_recv()

    return pl.pallas_call(
        body, name=name, out_shape=pltpu.HBM(wp.shape, wp.dtype),
        in_specs=(_HBM, _SEM, _SEM) + (_ANY,) * len(after), out_specs=_HBM,
        input_output_aliases={0: 0}, compiler_params=_EFFECT,
    )(wp, send_sems, recv_sems, *after)


def _rs_chips_start(name, t):
    nl = t.shape[0]

    def body(t_ref, land_ref, send_sems, recv_sems, t_thru, land_thru, token):
        x, y, c = _mesh_pos()
        for j, (px, py) in enumerate(_other_chips(x, y)):
            _remote(t_ref.at[:, 2 * px + py], land_ref.at[j], send_sems, recv_sems, j, (px, py, c)).start()
        token[...] = jnp.zeros_like(token)

    land = lax.empty((3, nl, P_HALF, D_MODEL), BF16)
    return pl.pallas_call(
        body, name=name,
        out_shape=(pltpu.SemaphoreType.DMA((3,)), pltpu.SemaphoreType.DMA((3,)), pltpu.HBM(t.shape, t.dtype),
                   pltpu.HBM(land.shape, land.dtype), _TOKEN),
        in_specs=(_HBM, _HBM), out_specs=(_SEM, _SEM, _HBM, _HBM, pl.BlockSpec(memory_space=pltpu.VMEM)),
        input_output_aliases={0: 2, 1: 3}, compiler_params=_EFFECT,
    )(_in_hbm(t), _in_hbm(land))


def _rs_chips_wait(name, send_sems, recv_sems, t, land, after):
    def body(t_ref, land_ref, send_sems, recv_sems, *rest):
        x, y, c = _mesh_pos()
        for j, (px, py) in enumerate(_other_chips(x, y)):
            cp = _remote(t_ref.at[:, 2 * px + py], land_ref.at[j], send_sems, recv_sems, j, (px, py, c))
            cp.wait_send()
            cp.wait_recv()

    return pl.pallas_call(
        body, name=name, out_shape=(pltpu.HBM(t.shape, t.dtype), pltpu.HBM(land.shape, land.dtype)),
        in_specs=(_HBM, _HBM, _SEM, _SEM) + (_ANY,) * len(after), out_specs=(_HBM, _HBM),
        input_output_aliases={0: 0, 1: 1}, compiler_params=_EFFECT,
    )(t, land, send_sems, recv_sems, *after)[1]


def _rs_sibling_start(name, g):
    nl = g.shape[0]

    def body(g_ref, land_ref, send_sems, recv_sems, g_thru, land_thru, token):
        x, y, c = _mesh_pos()
        _remote(g_ref.at[:, :, pl.ds((1 - c) * P_HALF, P_HALF), :], land_ref, send_sems, recv_sems, 0,
                (x, y, 1 - c)).start()
        token[...] = jnp.zeros_like(token)

    land = lax.empty((nl, N_SHARD, P_HALF, D_MODEL), F32)
    return pl.pallas_call(
        body, name=name,
        out_shape=(pltpu.SemaphoreType.DMA((1,)), pltpu.SemaphoreType.DMA((1,)), pltpu.HBM(g.shape, g.dtype),
                   pltpu.HBM(land.shape, land.dtype), _TOKEN),
        in_specs=(_HBM, _HBM), out_specs=(_SEM, _SEM, _HBM, _HBM, pl.BlockSpec(memory_space=pltpu.VMEM)),
        input_output_aliases={0: 2, 1: 3}, compiler_params=_EFFECT,
    )(_in_hbm(g), _in_hbm(land))


def _rs_sibling_wait(name, send_sems, recv_sems, g, land, after):
    def body(g_ref, land_ref, send_sems, recv_sems, *rest):
        x, y, c = _mesh_pos()
        cp = _remote(g_ref.at[:, :, pl.ds((1 - c) * P_HALF, P_HALF), :], land_ref, send_sems, recv_sems, 0,
                     (x, y, 1 - c))
        cp.wait_send()
        cp.wait_recv()

    return pl.pallas_call(
        body, name=name, out_shape=(pltpu.HBM(g.shape, g.dtype), pltpu.HBM(land.shape, land.dtype)),
        in_specs=(_HBM, _HBM, _SEM, _SEM) + (_ANY,) * len(after), out_specs=(_HBM, _HBM),
        input_output_aliases={0: 0, 1: 1}, compiler_params=_EFFECT,
    )(g, land, send_sems, recv_sems, *after)


def _rs_add(name, ids, g, buf, row_tile):
    nl, _, hr, cols = buf.shape
    n_rt = hr // row_tile

    def body(ids_ref, g_ref, b_ref, own_ref, tb_ref):
        t = g_ref[...] + b_ref[...]
        tb_ref[...] = t.astype(BF16)

        @pl.when(pl.program_id(2) == ids_ref[1])
        def _():
            own_ref[...] = t

    blk = (None, None, row_tile, cols)
    grid_spec = pltpu.PrefetchScalarGridSpec(
        num_scalar_prefetch=1, grid=(nl, n_rt, N_SHARD),
        in_specs=[pl.BlockSpec(blk, lambda l, j, s, ids_ref: (l, s, ids_ref[0] * n_rt + j, 0)),
                  pl.BlockSpec(blk, lambda l, j, s, ids_ref: (l, s, j, 0))],
        out_specs=[pl.BlockSpec((None, row_tile, cols), lambda l, j, s, ids_ref: (l, j, 0)),
                   pl.BlockSpec(blk, lambda l, j, s, ids_ref: (l, s, j, 0))])
    return pl.pallas_call(
        body, name=name, grid_spec=grid_spec,
        out_shape=[jax.ShapeDtypeStruct((nl, hr, cols), F32), jax.ShapeDtypeStruct(buf.shape, BF16)],
        compiler_params=_cparams(3),
    )(ids, g, buf)


def _rs_sum(ids, layer, own, bufb, reduced, row_tile):
    _, hr, cols = own.shape
    n_rt = hr // row_tile

    def body(ids_ref, own_ref, b_ref, reduced_in, f_ref):
        f_ref[...] = ((own_ref[...] + b_ref[0].astype(F32)) + b_ref[1].astype(F32)) + b_ref[2].astype(F32)

    grid_spec = pltpu.PrefetchScalarGridSpec(
        num_scalar_prefetch=1, grid=(n_rt,),
        in_specs=[pl.BlockSpec((None, row_tile, cols), lambda j, ids_ref: (0, j, 0)),
                  pl.BlockSpec((3, None, row_tile, cols), lambda j, ids_ref: (0, 0, j, 0)),
                  pl.BlockSpec(memory_space=pl.ANY)],
        out_specs=pl.BlockSpec((None, row_tile, cols), lambda j, ids_ref: (layer, ids_ref[0] * n_rt + j, 0)))
    return pl.pallas_call(
        body, name="rs_sum", grid_spec=grid_spec,
        out_shape=jax.ShapeDtypeStruct(reduced.shape, F32),
        input_output_aliases={3: 0},
        compiler_params=_cparams(1),
    )(ids, own, bufb, reduced)


def _rs_exchange_start(name, f):
    def body(f_ref, send_sems, recv_sems, f_thru):
        x, y, c = _mesh_pos()
        mine = f_ref.at[:, pl.ds(c * P_HALF, P_HALF), :]
        _remote(mine, mine, send_sems, recv_sems, 0, (x, y, 1 - c)).start()

    return pl.pallas_call(
        body, name=name,
        out_shape=(pltpu.SemaphoreType.DMA((1,)), pltpu.SemaphoreType.DMA((1,)), pltpu.HBM(f.shape, f.dtype)),
        in_specs=(_HBM,), out_specs=(_SEM, _SEM, _HBM),
        input_output_aliases={0: 2}, compiler_params=_EFFECT,
    )(_in_hbm(f))


def _rs_exchange_wait(name, send_sems, recv_sems, f, after):
    def body(f_ref, send_sems, recv_sems, *rest):
        x, y, c = _mesh_pos()
        mine = f_ref.at[:, pl.ds(c * P_HALF, P_HALF), :]
        theirs = f_ref.at[:, pl.ds((1 - c) * P_HALF, P_HALF), :]
        cp = _remote(mine, theirs, send_sems, recv_sems, 0, (x, y, 1 - c))
        cp.wait_send()
        cp.wait_recv()

    return pl.pallas_call(
        body, name=name, out_shape=pltpu.HBM(f.shape, f.dtype),
        in_specs=(_HBM, _SEM, _SEM) + (_ANY,) * len(after), out_specs=_HBM,
        input_output_aliases={0: 0}, compiler_params=_EFFECT,
    )(f, send_sems, recv_sems, *after)


def _small_all_reduce(s, after=()):
    n_rows = s.shape[0]
    hr = n_rows // 2
    qr = hr // N_SHARD

    def body(s_ref, *rest):
        o_ref, sibbuf, tbuf, qbuf, fbuf, send_sems, recv_sems = rest[len(after):]
        x, y, c = _mesh_pos()
        k = 2 * x + y
        sib = (x, y, 1 - c)
        chips = _other_chips(x, y)
        mine = pl.ds(pl.multiple_of(c * hr, SUBLANES), hr)
        theirs = pl.ds(pl.multiple_of((1 - c) * hr, SUBLANES), hr)

        def quarter(shard):
            return pl.ds(pl.multiple_of(shard * qr, SUBLANES), qr)

        first = _remote(s_ref.at[theirs], sibbuf, send_sems, recv_sems, 0, sib)
        first.start()
        first.wait()
        tbuf[...] = s_ref[mine, :] + sibbuf[...]
        cps = []
        for j, (px, py) in enumerate(chips):
            cp = _remote(tbuf.at[quarter(2 * px + py)], qbuf.at[j], send_sems, recv_sems, 1 + j, (px, py, c))
            cp.start()
            cps.append(cp)
        for cp in cps:
            cp.wait()
        fbuf[quarter(k), :] = (tbuf[quarter(k), :] + qbuf[1]) + (qbuf[0] + qbuf[2])
        cps = []
        for j, (px, py) in enumerate(chips):
            cp = _remote(fbuf.at[quarter(k)], fbuf.at[quarter(k)], send_sems, recv_sems, 4 + j, (px, py, c))
            cp.start()
            cps.append(cp)
        for j, (px, py) in enumerate(chips):
            got = fbuf.at[quarter(2 * px + py)]
            _remote(got, got, send_sems, recv_sems, 4 + j, (px, py, c)).wait_recv()
        for cp in cps:
            cp.wait_send()
        o_ref[mine, :] = fbuf[...]
        last = _remote(fbuf, o_ref.at[mine], send_sems, recv_sems, 7, sib)
        last.start()
        last.wait()

    vmem = pl.BlockSpec(memory_space=pltpu.VMEM)
    return pl.pallas_call(
        body, name="small_all_reduce",
        in_specs=[vmem] + [_ANY] * len(after), out_specs=vmem,
        out_shape=jax.ShapeDtypeStruct(s.shape, F32),
        scratch_shapes=[pltpu.VMEM((hr, D_MODEL), F32), pltpu.VMEM((hr, D_MODEL), F32),
                        pltpu.VMEM((3, qr, D_MODEL), F32), pltpu.VMEM((hr, D_MODEL), F32),
                        pltpu.SemaphoreType.DMA((8,)), pltpu.SemaphoreType.DMA((8,))],
        compiler_params=pltpu.CompilerParams(vmem_limit_bytes=VMEM_LIMIT),
    )(s, *after)


_SMALL = ("norm_mix", "w_pool", "pool_scale", "lam_re", "lam_im", "log_dt", "b_re", "b_im", "c_re", "c_im",
          "d_skip", "b_glu", "norm_ffn", "norm_final")
_WEIGHTS = ("norm_mix", "w_in", "w_pool", "pool_scale", "lam_re", "lam_im", "log_dt", "b_re", "b_im", "c_re",
            "c_im", "d_skip", "w_glu", "b_glu", "w_out", "norm_ffn", "w_gate", "w_up", "w_down", "norm_final")


def _local_step(x, target, p, get_weights, scan_done, get_ffn_weights, ffn_bwd_done, put_grads):
    nl = p["norm_mix"].shape[0]

    def tied(a, token):
        return a if token is None else a + token
    n_rows = nl * N_SSM_GROUPS
    lr = p["lam_re"].reshape(n_rows, 1, SSM_STATE)
    li = p["lam_im"].reshape(n_rows, 1, SSM_STATE)
    ldt = p["log_dt"].reshape(n_rows, 1, 1)
    br_t = p["b_re"].reshape(n_rows, SSM_STATE, SSM_GROUP).transpose(0, 2, 1)
    bi_t = p["b_im"].reshape(n_rows, SSM_STATE, SSM_GROUP).transpose(0, 2, 1)
    ar, ai, bbr_t, bbi_t = _disc_fwd(lr, li, ldt, br_t, bi_t)
    ar = ar.reshape(nl, 1, N_STATE)
    ai = ai.reshape(nl, 1, N_STATE)
    bbr = bbr_t.transpose(0, 2, 1).reshape(nl, N_SSM_GROUPS, SSM_STATE, SSM_GROUP)
    bbi = bbi_t.transpose(0, 2, 1).reshape(nl, N_SSM_GROUPS, SSM_STATE, SSM_GROUP)
    w_pool = p["w_pool"].astype(BF16)
    p = dict(p)
    for n in ("norm_mix", "pool_scale", "b_glu", "norm_ffn"):
        p[n] = p[n].reshape(nl, 1, -1)
    swap = lambda a: jnp.swapaxes(a, -1, -2)
    bpad = jax.vmap(_pad_pairs)(bbr, bbi).astype(BF16)
    cpad_t = jax.vmap(_pad_pairs)(swap(p["c_re"]), -swap(p["c_im"])).astype(BF16)
    bpad_t, cpad = swap(bpad), swap(cpad_t)
    dskip = p["d_skip"].reshape(nl, 1, D_SSM)

    layers = []
    h = x
    for l in range(nl):
        wp = get_weights(l, [h] if l else [h, bpad, cpad, bpad_t, cpad_t, ar, ai])
        u, ypool = _mix_in_fwd(h, p["norm_mix"], wp, l, w_pool, p["pool_scale"])
        sre, sim, yraw = _ssm_fwd(u, l, bpad, cpad, ar, ai, dskip)
        wp = scan_done(l, wp, [yraw])
        hm = _mix_out_fwd(yraw, ypool, h, wp, l, p["b_glu"])
        wp = get_ffn_weights(l, wp, [hm])
        h_next, n2, act_s, fgate_s, fup_s = _ffn_fwd(hm, p["norm_ffn"], wp, l)
        layers.append(dict(h=h, u=u, ypool=ypool, sre=sre, sim=sim, yraw=yraw, hm=hm, n2=n2, act_s=act_s, wp=wp,
                           fgate_s=fgate_s, fup_s=fup_s))
        h = h_next

    dh, loss, d_norm_final = _final_fwd_bwd(h, p["norm_final"].reshape(1, D_MODEL), target)

    raw = {n: [None] * nl for n in ("dg1", "dwp", "dsc", "dcp", "dbp", "ddsk", "db_glu", "dg2", "dar", "dai")}
    token = None
    for l in reversed(range(nl)):
        s = layers[l]
        wp = s["wp"]
        g1 = lax.empty((1, N_SHARD, P_ROWS, D_MODEL), F32)
        dhm, dg2, dgate_s, dup_s, dhb = _ffn_bwd_act(dh, s["hm"], tied(p["norm_ffn"], token), s["fgate_s"],
                                                      s["fup_s"], wp, l)
        g1 = _ffn_bwd_w(s["n2"], dgate_s, dup_s, s["act_s"], dhb, g1)
        token = ffn_bwd_done(l, [g1])
        dyraw, dyp, db_glu, g1 = _mix_out_bwd(dhm, s["yraw"], s["ypool"], wp, l, tied(p["b_glu"], token), g1)
        dus, dcp, dbp, dar, dai, ddsk = _ssm_bwd(dyraw, s["u"], s["sre"], s["sim"], l, cpad_t, bpad_t, ar, ai, dskip)
        dup, dwp, dsc = _pool_bwd(dyp, s["u"], l, w_pool, p["pool_scale"])
        dh, dg1, g1 = _mix_in_bwd(dup, dus, s["h"], dhm, p["norm_mix"], wp, l, g1)
        token = put_grads(l, g1)
        for n, a in (("dg1", dg1), ("dwp", dwp), ("dsc", dsc), ("dcp", dcp), ("dbp", dbp), ("ddsk", ddsk),
                     ("db_glu", db_glu), ("dg2", dg2), ("dar", dar), ("dai", dai)):
            raw[n][l] = a

    st = {n: jnp.stack(v) for n, v in raw.items()}
    dc_re, dc_im = jax.vmap(_unpad_pairs)(swap(st["dcp"]))
    dbbr, dbbi = jax.vmap(_unpad_pairs)(st["dbp"])
    rows = lambda a: a.reshape((n_rows,) + a.shape[2:])
    dlr, dli, dldt, dbr_t, dbi_t = _disc_bwd(lr, li, ldt, br_t, bi_t, st["dar"].reshape(n_rows, 1, SSM_STATE),
                                              st["dai"].reshape(n_rows, 1, SSM_STATE), rows(swap(dbbr)),
                                              rows(swap(dbbi)))
    small = {"norm_mix": st["dg1"][:, 0], "w_pool": st["dwp"], "pool_scale": st["dsc"][:, 0], "c_re": swap(dc_re),
             "c_im": -swap(dc_im), "d_skip": st["ddsk"].reshape(nl, N_SSM_GROUPS, SSM_GROUP),
             "b_glu": st["db_glu"][:, 0], "norm_ffn": st["dg2"][:, 0]}
    small["lam_re"] = dlr.reshape(nl, N_SSM_GROUPS, SSM_STATE)
    small["lam_im"] = dli.reshape(nl, N_SSM_GROUPS, SSM_STATE)
    small["log_dt"] = dldt.reshape(nl, N_SSM_GROUPS)
    small["b_re"] = dbr_t.reshape(nl, N_SSM_GROUPS, SSM_GROUP, SSM_STATE)
    small["b_im"] = dbi_t.reshape(nl, N_SSM_GROUPS, SSM_GROUP, SSM_STATE)
    small["d_skip"] = small["d_skip"].transpose(_SMALL_VIEW["d_skip"])
    small["norm_final"] = d_norm_final
    return loss, dh, small


_SMALL_VIEW = {"b_re": (0, 1, 3, 2), "b_im": (0, 1, 3, 2), "d_skip": (0, 2, 1)}
_SMALL_GROUPS = (("b_re", "b_im"), ("c_re", "c_im"), ("lam_re", "lam_im"), ("norm_mix", "norm_ffn"),
                 ("pool_scale", "b_glu"), ("w_pool",), ("log_dt",), ("d_skip",), ("norm_final",))


def _view(n, a):
    a = a.transpose(_SMALL_VIEW[n]) if n in _SMALL_VIEW else a
    return a[None] if a.ndim == 1 else a


def _unview(n, a, shape):
    a = a.reshape(shape) if len(shape) == 1 else a
    return a.transpose(_SMALL_VIEW[n]) if n in _SMALL_VIEW else a


def _flatten_small(views):
    flat = jnp.concatenate([views[n].reshape(-1) for n in _SMALL])
    n_rows = -(-flat.shape[0] // (64 * D_MODEL)) * 64
    return jnp.pad(flat, (0, n_rows * D_MODEL - flat.shape[0])).reshape(n_rows, D_MODEL)


def _split_small(flat, like):
    flat = flat.reshape(-1)
    out, at = {}, 0
    for n in _SMALL:
        size = like[n].size
        out[n] = flat[at:at + size].reshape(like[n].shape)
        at += size
    return out


def _adamw_small(name, ws, ms, vs, gs):
    k = len(ws)

    def body(*refs):
        ins, outs = refs[:4 * k], refs[4 * k:]
        for i in range(k):
            w, m, v, g = (ins[j * k + i][...] for j in range(4))
            delta, mn, vn = _adamw_math(w, g, m, v)
            outs[i][...] = delta
            outs[k + i][...] = mn
            outs[2 * k + i][...] = vn

    shapes = [jax.ShapeDtypeStruct(w.shape, F32) for w in ws] * 3
    outs = pl.pallas_call(body, name=name, out_shape=shapes,
                          compiler_params=pltpu.CompilerParams(vmem_limit_bytes=VMEM_LIMIT))(*ws, *ms, *vs, *gs)
    return outs[:k], outs[k:2 * k], outs[2 * k:]


def kernel(x, norm_mix, w_in, w_pool, pool_scale, lam_re, lam_im, log_dt, b_re, b_im, c_re, c_im, d_skip, w_glu, b_glu, w_out, norm_ffn, w_gate, w_up, w_down, norm_final, loss_target, m_norm_mix, m_w_in, m_w_pool, m_pool_scale, m_lam_re, m_lam_im, m_log_dt, m_b_re, m_b_im, m_c_re, m_c_im, m_d_skip, m_w_glu, m_b_glu, m_w_out, m_norm_ffn, m_w_gate, m_w_up, m_w_down, m_norm_final, v_norm_mix, v_w_in, v_w_pool, v_pool_scale, v_lam_re, v_lam_im, v_log_dt, v_b_re, v_b_im, v_c_re, v_c_im, v_d_skip, v_w_glu, v_b_glu, v_w_out, v_norm_ffn, v_w_gate, v_w_up, v_w_down, v_norm_final):
    given = dict(locals())
    w = {n: given[n] for n in _WEIGHTS}
    m = {n: given["m_" + n] for n in _WEIGHTS}
    v = {n: given["v_" + n] for n in _WEIGHTS}
    ids = jnp.stack([lax.axis_index("c"), 2 * lax.axis_index("x") + lax.axis_index("y")]).astype(jnp.int32)

    t_names = ("w_gate", "w_up")
    tr = lambda a: a.transpose(0, 2, 1)
    for d in (w, m, v):
        d.update({n: tr(d[n]) for n in t_names})

    nl = norm_mix.shape[0]
    mixer_rows, ffn_rows = (P_FF_ROWS, P_ROWS - P_FF_ROWS), (0, P_FF_ROWS)
    started, last = {}, None
    for l in range(nl):
        packed = _pack_weights(ids, l, w["w_in"], w["w_glu"], w["w_out"], w["w_down"], w["w_gate"], w["w_up"],
                               [] if last is None else [last])
        if l == 0:
            first = _ag_start("ag_start_0_mixer", packed, ids, [mixer_rows])
            started[0] = _ag_start("ag_start_0_ffn", first[2], first[3], [ffn_rows])
        else:
            started[l] = _ag_start(f"ag_start_{l}", packed, last, [mixer_rows, ffn_rows])
        last = started[l][3]
    views = [{n: _view(n, d[n]) for n in _SMALL} for d in (w, m, v)]

    passing = {}

    def get_weights(l, after):
        send_sems, recv_sems, buf, _ = started[l]
        if l == 0:
            buf = _ag_wait("ag_wait_0_mixer", first[0], first[1], buf, after + [last], [mixer_rows])
            return _ag_forward(buf, mixer_rows)
        buf = _ag_wait(f"ag_wait_{l}", send_sems, recv_sems, buf, after, [mixer_rows, ffn_rows])
        buf = _ag_forward(buf, mixer_rows)
        passing[l] = _ag_forward_start(f"ag_forward_start_{l}", buf, ffn_rows)
        return passing[l][2]

    def scan_done(l, buf, after):
        if l > 0:
            return buf
        send_sems, recv_sems, _, _ = started[0]
        buf = _ag_wait("ag_wait_0_ffn", send_sems, recv_sems, buf, after, [ffn_rows])
        passing[0] = _ag_forward_start("ag_forward_start_0", buf, ffn_rows)
        return passing[0][2]

    def get_ffn_weights(l, buf, after):
        send_sems, recv_sems, _ = passing[l]
        return _ag_forward_wait(f"ag_forward_wait_{l}", send_sems, recv_sems, buf, after, ffn_rows)

    to_sibling, to_chips, reduced = {}, {}, {}

    def put_grads(l, g):
        to_sibling[l] = _rs_sibling_start(f"rs_sibling_start_{l}", g)
        token = to_sibling[l][4]
        if l + 1 in to_chips:
            finish(l + 1, [token])
        return token[:1, :1]

    def ffn_bwd_done(l, after):
        return send_to_chips(l + 1, after)[:1, :1] if l + 1 in to_sibling else None

    def send_to_chips(l, after):
        send_sems, recv_sems, g, land, _ = to_sibling.pop(l)
        g, land = _rs_sibling_wait(f"rs_sibling_wait_{l}", send_sems, recv_sems, g, land, after)
        own, t = _rs_add("rs_add", ids, g, land, RS_ADD_TILE)
        send_sems, recv_sems, t, land, token = _rs_chips_start(f"rs_chips_start_{l}", t)
        to_chips[l] = (send_sems, recv_sems, t, land, own)
        return token

    def finish(l, after):
        send_sems, recv_sems, t, land, own = to_chips.pop(l)
        land = _rs_chips_wait(f"rs_chips_wait_{l}", send_sems, recv_sems, t, land, after)
        shard = lax.empty((1, P_ROWS, D_MODEL), F32)
        reduced[l] = _rs_exchange_start(f"rs_exchange_start_{l}", _rs_sum(ids, 0, own, land, shard, RS_SUM_TILE))

    loss, grad_x, small = _local_step(x[0], loss_target[0], {n: w[n] for n in _SMALL}, get_weights, scan_done,
                                      get_ffn_weights, ffn_bwd_done, put_grads)
    loss = lax.psum(loss[0, 0], ("x", "y", "c"))
    small_flat = _flatten_small(small)

    groups = ((("w_in", P_IN_BLK), ("w_out", P_OUT_BLK)), (("w_down", P_WD_BLK), ("w_gate", P_WG_BLK), ("w_up", P_WU_BLK)))
    res = {n: None for n in ("w_in", "w_out", "w_down", "w_gate", "w_up", "w_glu")}

    def adamw_layer(l, after):
        send_sems, recv_sems, shard = reduced[l]
        shard = _rs_exchange_wait(f"rs_exchange_wait_{l}", send_sems, recv_sems, shard, after)
        for group, row_tile in zip(groups, (128, 176)):
            names = [n for n, _ in group]
            outs = None if res[names[0]] is None else [res[n] for n in names]
            outs = _adamw_group("adamw_" + names[0], l, *[[d[n] for n in names] for d in (w, m, v)], shard,
                                [blk * idx for _, (blk, idx) in group], row_tile, outs)
            res.update(zip(names, outs))
        blk, idx = P_GLU_BLK
        res["w_glu"] = _adamw("adamw_w_glu", l, w["w_glu"], m["w_glu"], v["w_glu"], shard, (blk, D_MODEL), blk * idx,
                              128, res["w_glu"], (), True)

    if nl > 1:
        adamw_layer(nl - 1, [to_sibling[0][4]])
    token = send_to_chips(0, [small_flat] + [r[0] for r in res.values() if r is not None])
    for l in reversed(range(1, nl - 1)):
        adamw_layer(l, [token])
    updated = [r[0] for r in res.values() if r is not None]
    small_sum = _small_all_reduce(small_flat, [token] + updated)
    finish(0, [small_sum] + updated)
    adamw_layer(0, [])
    for n in t_names:
        res[n] = tuple(tr(a) for a in res[n])
    g_views = _split_small(small_sum, views[0])
    for group in _SMALL_GROUPS:
        deltas, new_ms, new_vs = _adamw_small("adamw_" + group[0], *[[d[n] for n in group] for d in views],
                                              [g_views[n] for n in group])
        for i, n in enumerate(group):
            res[n] = tuple(_unview(n, a, w[n].shape) for a in (g_views[n], deltas[i], new_ms[i], new_vs[i]))

    return (loss, grad_x[None], *[res[n][0] for n in _WEIGHTS], *[res[n][1] for n in _WEIGHTS],
            *[res[n][2] for n in _WEIGHTS], *[res[n][3] for n in _WEIGHTS])
```

```python
import functools
import math

import jax
import jax.numpy as jnp
from jax import lax
from jax.experimental import pallas as pl
from jax.experimental.pallas import tpu as pltpu

F32 = jnp.float32
BF16 = jnp.bfloat16

D_MODEL = 1024
D_POOL = 512
D_SSM = 512
POOL_WINDOWS = (2, 4, 8, 16)
POOL_GROUP = 128
POOL_HALO = 16
N_SSM_GROUPS = 32
SSM_GROUP = 16
SSM_STATE = 64
N_STATE = N_SSM_GROUPS * SSM_STATE
N_PAIRS = N_SSM_GROUPS // 2
D_FF = 2816
N_SHARD = 4
FF_SHARD = D_FF // N_SHARD
RMS_EPS = 1e-6

ADAM_LR = 0.001
ADAM_B1 = 0.9
ADAM_B2 = 0.999
ADAM_EPS = 1e-08
ADAM_WD = 0.01
ADAM_STEP = 10

P_ROWS = 2816
P_WD_BLK = (704, 0)
P_WG_BLK = (704, 1)
P_WU_BLK = (704, 2)
P_FF_ROWS = 2112
P_GLU_BLK = (64, 33)
P_GLU_PAD = 192
P_IN_BLK = (256, 9)
P_OUT_BLK = (256, 10)

SUBLANES = 8
VMEM_LIMIT = 56 * 1024 * 1024

TM = 1024
TM_FFN = 512
TM_FFN_LONG = 1024
FFN_SPLIT = 2
TS = 2048
SCAN_LANES = 512


def _cparams(n_axes):
    return pltpu.CompilerParams(dimension_semantics=("arbitrary",) * n_axes, vmem_limit_bytes=VMEM_LIMIT)


def _dot(a, b):
    return jnp.dot(a, b, preferred_element_type=F32)


def _dot_nt(a, b):
    return lax.dot_general(a, b, (((1,), (1,)), ((), ())), preferred_element_type=F32)


def _dot_tn(a, b):
    return lax.dot_general(a, b, (((0,), (0,)), ((), ())), preferred_element_type=F32)


def _rms_hat(x):
    r = lax.rsqrt(jnp.mean(x * x, axis=-1, keepdims=True) + RMS_EPS)
    return x * r, r


def _rms_bwd(d_hat, xhat, r):
    return r * (d_hat - xhat * jnp.mean(d_hat * xhat, axis=-1, keepdims=True))


def _sigmoid(x):
    return 1.0 / (1.0 + jnp.exp(-x))


_GELU_C = math.sqrt(2.0 / math.pi)
_GELU_K = 0.044715


def _gelu(x):
    return 0.5 * x * (1.0 + jnp.tanh(_GELU_C * (x + _GELU_K * x * x * x)))


def _gelu_grad(x):
    th = jnp.tanh(_GELU_C * (x + _GELU_K * x * x * x))
    return 0.5 * (1.0 + th) + 0.5 * x * (1.0 - th * th) * _GELU_C * (1.0 + 3.0 * _GELU_K * x * x)


def _glu_weight(ref):
    v = ref[...]
    return jnp.concatenate([v[:, :, :D_SSM], v[:, :, D_SSM:]], axis=1).reshape(D_SSM, D_SSM)


def _glu_pack(w):
    v = w.reshape(N_SHARD, 128, D_SSM)
    return jnp.concatenate([v[:, :64, :], v[:, 64:, :]], axis=2)


def _pool_diff(ext, row0, tm):
    rows = row0 + lax.broadcasted_iota(jnp.int32, (tm, 1), 0)
    outs = []
    for gi, w in enumerate(POOL_WINDOWS):
        e = ext[:, gi * POOL_GROUP:(gi + 1) * POOL_GROUP]
        s = e
        k = 1
        while k < w:
            s = s + pltpu.roll(s, k, 0)
            k *= 2
        inv = 1.0 / jnp.minimum(rows + 1, w).astype(F32)
        outs.append(s[POOL_HALO:, :] * inv - e[POOL_HALO:, :])
    return outs


def _mix_in_fwd(h, g1, wp, layer, w_pool, scale):
    L = h.shape[0]
    tm = min(TM, L)

    def body(h_ref, g_ref, w_ref, wp_ref, sc_ref, u_ref, yp_ref, carry):
        i = pl.program_id(0)

        @pl.when(i == 0)
        def _():
            carry[...] = jnp.zeros_like(carry)

        xhat, _ = _rms_hat(h_ref[...])
        n1 = (xhat * g_ref[...]).astype(BF16)
        u = _dot(n1, w_ref[...].reshape(D_MODEL, D_MODEL))
        u_ref[...] = u
        up = u[:, :D_POOL]
        ext = jnp.concatenate([carry[...], up], axis=0)
        carry[...] = up[tm - POOL_HALO:, :]
        diffs = _pool_diff(ext, i * tm, tm)
        for gi in range(4):
            cols = slice(gi * POOL_GROUP, (gi + 1) * POOL_GROUP)
            yp_ref[:, cols] = _dot(diffs[gi].astype(BF16), wp_ref[gi]) * sc_ref[:, cols]

    blk, idx = P_IN_BLK
    return pl.pallas_call(
        body, name="mix_in_fwd", grid=(L // tm,),
        in_specs=[pl.BlockSpec((tm, D_MODEL), lambda i: (i, 0)),
                  pl.BlockSpec((None, 1, D_MODEL), lambda i: (layer, 0, 0)),
                  pl.BlockSpec((N_SHARD, None, blk, D_MODEL), lambda i: (0, 0, idx, 0)),
                  pl.BlockSpec((None, 4, POOL_GROUP, POOL_GROUP), lambda i: (layer, 0, 0, 0)),
                  pl.BlockSpec((None, 1, D_POOL), lambda i: (layer, 0, 0))],
        out_specs=[pl.BlockSpec((tm, D_MODEL), lambda i: (i, 0)),
                   pl.BlockSpec((tm, D_POOL), lambda i: (i, 0))],
        out_shape=[jax.ShapeDtypeStruct((L, D_MODEL), F32), jax.ShapeDtypeStruct((L, D_POOL), F32)],
        scratch_shapes=[pltpu.VMEM((POOL_HALO, D_POOL), F32)],
        compiler_params=_cparams(1),
    )(h, g1, wp, w_pool, scale)


def _cmul(xr, xi, yr, yi):
    return xr * yr - xi * yi, xr * yi + xi * yr


SCAN_BLOCK = 64
N_SCAN_TABLES = 26


def _permute_rows(src, dst, n_rows):
    for b in range(n_rows // SCAN_BLOCK):
        for tau in range(SUBLANES):
            dst[pl.ds(SCAN_BLOCK * b + SUBLANES * tau, SUBLANES), :] = (
                src[pl.ds(SCAN_BLOCK * b + tau, SUBLANES, stride=SUBLANES), :])


def _scan_tables(ar, ai, tab, reverse):
    c = ar.shape[1]
    row = lax.broadcasted_iota(jnp.int32, (SUBLANES, c), 0)
    zero = jnp.zeros((SUBLANES, c), F32)
    full = lambda v: jnp.broadcast_to(v, (SUBLANES, c))
    pw = [(ar, ai)]
    for _ in range(SUBLANES - 1):
        pw.append(_cmul(*pw[-1], ar, ai))
    a8 = pw[-1]
    a16 = _cmul(*a8, *a8)
    a32 = _cmul(*a16, *a16)
    tab[0] = full(ar)
    tab[1] = full(ai)
    for n, (s, (pr, pi)) in enumerate(((1, a8), (2, a16), (4, a32))):
        keep = (row < SUBLANES - s) if reverse else (row >= s)
        tab[2 + 2 * n] = jnp.where(keep, pr, zero)
        tab[3 + 2 * n] = jnp.where(keep, pi, zero)
    cur = a8
    qr, qi = zero, zero
    for n in range(SUBLANES):
        at = (SUBLANES - 1 - n) if reverse else n
        qr = jnp.where(row == at, cur[0], qr)
        qi = jnp.where(row == at, cur[1], qi)
        cur = _cmul(*cur, *a8)
    tab[8] = qr
    tab[9] = qi
    for tau in range(SUBLANES):
        pr, pi = pw[SUBLANES - 1 - tau] if reverse else pw[tau]
        tab[10 + 2 * tau] = full(pr)
        tab[11 + 2 * tau] = full(pi)


def _cmac(xr, xi, ar, ai, yr, yi):
    return xr + ar * yr - ai * yi, xi + ar * yi + ai * yr


def _chain_segments(er, ei, c_r, c_i, tab, cols, reverse):
    tr, ti = er, ei
    for n, s in enumerate((1, 2, 4)):
        shift = SUBLANES - s if reverse else s
        tr, ti = _cmac(tr, ti, tab[2 + 2 * n, :, cols], tab[3 + 2 * n, :, cols],
                       pltpu.roll(tr, shift, 0), pltpu.roll(ti, shift, 0))
    return _cmac(tr, ti, tab[8, :, cols], tab[9, :, cols], c_r, c_i)


def _ssm_fwd(u, layer, bpad, cpad, ar, ai, dskip):
    L = u.shape[0]
    ts = min(TS, L)
    nq = 4
    cq = N_STATE // nq

    def body(u_ref, bp_ref, cp_ref, ar_ref, ai_ref, dsk_ref, sre_ref, sim_ref, y_ref, cr, ci, tab, up, yp):
        t = pl.program_id(1)

        @pl.when(t == 0)
        def _():
            cr[...] = jnp.zeros_like(cr)
            ci[...] = jnp.zeros_like(ci)
            _scan_tables(ar_ref[...], ai_ref[...], tab, reverse=False)

        _permute_rows(u_ref, up, ts)
        uf = up[...]
        ub = uf.astype(BF16)
        for jj in range(4):
            bu = _dot(ub, bp_ref[jj])
            sre_ref[:, jj * 128:(jj + 1) * 128] = bu[:, :128]
            sim_ref[:, jj * 128:(jj + 1) * 128] = bu[:, 128:]

        shp = (SUBLANES, SCAN_LANES)
        first_row = lax.broadcasted_iota(jnp.int32, shp, 0) == 0
        for cc in range(cq // SCAN_LANES):
            cols = slice(cc * SCAN_LANES, (cc + 1) * SCAN_LANES)

            def block(b, carry, cols=cols):
                c_r, c_i = carry
                base = pl.multiple_of(b * SCAN_BLOCK, SCAN_BLOCK)
                rows = lambda tau: pl.ds(base + SUBLANES * tau, SUBLANES)
                a_r, a_i = tab[0, :, cols], tab[1, :, cols]
                ys = [(sre_ref[rows(0), cols], sim_ref[rows(0), cols])]
                for tau in range(1, SUBLANES):
                    ys.append(_cmac(sre_ref[rows(tau), cols], sim_ref[rows(tau), cols], a_r, a_i, *ys[-1]))
                tr, ti = _chain_segments(*ys[-1], c_r, c_i, tab, cols, reverse=False)
                in_r = jnp.where(first_row, c_r, pltpu.roll(tr, 1, 0))
                in_i = jnp.where(first_row, c_i, pltpu.roll(ti, 1, 0))
                for tau in range(SUBLANES):
                    sr, si = _cmac(*ys[tau], tab[10 + 2 * tau, :, cols], tab[11 + 2 * tau, :, cols], in_r, in_i)
                    sre_ref[rows(tau), cols] = sr
                    sim_ref[rows(tau), cols] = si
                return (jnp.broadcast_to(tr[SUBLANES - 1:, :], shp), jnp.broadcast_to(ti[SUBLANES - 1:, :], shp))

            c_r, c_i = lax.fori_loop(0, ts // SCAN_BLOCK, block, (cr[:, cols], ci[:, cols]), unroll=2)
            cr[:, cols] = c_r
            ci[:, cols] = c_i

        acc = dsk_ref[...] * uf
        for jj in range(4):
            cols = slice(jj * 128, (jj + 1) * 128)
            scat = jnp.concatenate([sre_ref[:, cols], sim_ref[:, cols]], axis=1).astype(BF16)
            acc = acc + _dot(scat, cp_ref[jj])
        yp[...] = acc
        _permute_rows(yp, y_ref, ts)

    return pl.pallas_call(
        body, name="ssm_fwd", grid=(nq, L // ts),
        in_specs=[pl.BlockSpec((ts, 128), lambda q, t: (t, 4 + q)),
                  pl.BlockSpec((None, 4, 128, 256), lambda q, t: (layer, q, 0, 0)),
                  pl.BlockSpec((None, 4, 256, 128), lambda q, t: (layer, q, 0, 0)),
                  pl.BlockSpec((None, 1, cq), lambda q, t: (layer, 0, q)),
                  pl.BlockSpec((None, 1, cq), lambda q, t: (layer, 0, q)),
                  pl.BlockSpec((None, 1, 128), lambda q, t: (layer, 0, q))],
        out_specs=[pl.BlockSpec((ts, cq), lambda q, t: (t, q)),
                   pl.BlockSpec((ts, cq), lambda q, t: (t, q)),
                   pl.BlockSpec((ts, 128), lambda q, t: (t, q))],
        out_shape=[jax.ShapeDtypeStruct((L, N_STATE), F32), jax.ShapeDtypeStruct((L, N_STATE), F32),
                   jax.ShapeDtypeStruct((L, D_SSM), F32)],
        scratch_shapes=[pltpu.VMEM((SUBLANES, cq), F32), pltpu.VMEM((SUBLANES, cq), F32),
                        pltpu.VMEM((N_SCAN_TABLES, SUBLANES, cq), F32),
                        pltpu.VMEM((ts, 128), F32), pltpu.VMEM((ts, 128), F32)],
        compiler_params=_cparams(2),
    )(u, bpad, cpad, ar, ai, dskip)


def _mix_out_fwd(yraw, ypool, h, wp, layer, b_glu):
    L = h.shape[0]
    tm = min(TM, L)

    def body(yr_ref, yp_ref, h_ref, wglu_ref, b_ref, wout_ref, o_ref):
        y = _gelu(yr_ref[...])
        z = _dot(y.astype(BF16), _glu_weight(wglu_ref)) + b_ref[...]
        o = y * _sigmoid(z)
        mix = jnp.concatenate([yp_ref[...], o], axis=1).astype(BF16)
        o_ref[...] = h_ref[...] + _dot(mix, wout_ref[...].reshape(D_MODEL, D_MODEL))

    gb, gi = P_GLU_BLK
    ob, oi = P_OUT_BLK
    return pl.pallas_call(
        body, name="mix_out_fwd", grid=(L // tm,),
        in_specs=[pl.BlockSpec((tm, D_SSM), lambda i: (i, 0)),
                  pl.BlockSpec((tm, D_POOL), lambda i: (i, 0)),
                  pl.BlockSpec((tm, D_MODEL), lambda i: (i, 0)),
                  pl.BlockSpec((N_SHARD, None, gb, D_MODEL), lambda i: (0, 0, gi, 0)),
                  pl.BlockSpec((None, 1, D_SSM), lambda i: (layer, 0, 0)),
                  pl.BlockSpec((N_SHARD, None, ob, D_MODEL), lambda i: (0, 0, oi, 0))],
        out_specs=pl.BlockSpec((tm, D_MODEL), lambda i: (i, 0)),
        out_shape=jax.ShapeDtypeStruct((L, D_MODEL), F32),
        compiler_params=_cparams(1),
    )(yraw, ypool, h, wp, b_glu, wp)


def _ffn_weights(ref, k):
    return ref[k, 0:FF_SHARD, :], ref[k, FF_SHARD:2 * FF_SHARD, :], ref[k, 2 * FF_SHARD:P_FF_ROWS, :]


def _ffn_weight_spec():
    return pl.BlockSpec((N_SHARD, None, P_FF_ROWS, D_MODEL), lambda m, k: (0, 0, 0, 0),
                        pipeline_mode=pl.Buffered(1))


def _ffn_fwd(h, g2, wp, layer):
    L = h.shape[0]
    tm = min(TM_FFN_LONG, L)

    def body(h_ref, g_ref, w_ref, o_ref, n2_ref, act_ref, dgate_ref, dup_ref):
        k = pl.program_id(1)

        @pl.when(k == 0)
        def _():
            x = h_ref[...]
            xhat, _ = _rms_hat(x)
            n2_ref[...] = (xhat * g_ref[...]).astype(BF16)
            o_ref[...] = x

        wd, wg_t, wu_t = _ffn_weights(w_ref, k)
        n2 = n2_ref[...]
        gate = _dot_nt(n2, wg_t)
        up = _dot_nt(n2, wu_t)
        sg = _sigmoid(gate)
        silu = gate * sg
        act = (silu * up).astype(BF16)
        act_ref[...] = act
        dgate_ref[...] = (up * (sg * (1.0 + gate * (1.0 - sg)))).astype(BF16)
        dup_ref[...] = silu.astype(BF16)
        o_ref[...] += _dot(act, wd)

    act_shape = jax.ShapeDtypeStruct((N_SHARD, L, FF_SHARD), BF16)
    return pl.pallas_call(
        body, name="ffn_fwd", grid=(L // tm, N_SHARD),
        in_specs=[pl.BlockSpec((tm, D_MODEL), lambda m, k: (m, 0)),
                  pl.BlockSpec((None, 1, D_MODEL), lambda m, k: (layer, 0, 0)),
                  _ffn_weight_spec()],
        out_specs=[pl.BlockSpec((tm, D_MODEL), lambda m, k: (m, 0)),
                   pl.BlockSpec((tm, D_MODEL), lambda m, k: (m, 0)),
                   pl.BlockSpec((None, tm, FF_SHARD), lambda m, k: (k, m, 0)),
                   pl.BlockSpec((None, tm, FF_SHARD), lambda m, k: (k, m, 0)),
                   pl.BlockSpec((None, tm, FF_SHARD), lambda m, k: (k, m, 0))],
        out_shape=[jax.ShapeDtypeStruct((L, D_MODEL), F32), jax.ShapeDtypeStruct((L, D_MODEL), BF16),
                   act_shape, act_shape, act_shape],
        compiler_params=_cparams(2),
    )(h, g2, wp)


def _final_fwd_bwd(h, gf, target):
    L = h.shape[0]
    tm = min(TM, L)

    def body(h_ref, g_ref, t_ref, dh_ref, loss_ref, dg_ref):
        i = pl.program_id(0)

        @pl.when(i == 0)
        def _():
            loss_ref[...] = jnp.zeros_like(loss_ref)
            dg_ref[...] = jnp.zeros_like(dg_ref)

        xhat, r = _rms_hat(h_ref[...])
        g = g_ref[...]
        e = xhat * g - t_ref[...]
        loss_ref[...] += 0.5 * jnp.sum(jnp.mean(e * e, axis=-1, keepdims=True), axis=0, keepdims=True)
        dy = e * (1.0 / D_MODEL)
        dg_ref[...] += jnp.sum(dy * xhat, axis=0, keepdims=True)
        dh_ref[...] = _rms_bwd(dy * g, xhat, r)

    return pl.pallas_call(
        body, name="final_fwd_bwd", grid=(L // tm,),
        in_specs=[pl.BlockSpec((tm, D_MODEL), lambda i: (i, 0)),
                  pl.BlockSpec((1, D_MODEL), lambda i: (0, 0)),
                  pl.BlockSpec((tm, D_MODEL), lambda i: (i, 0))],
        out_specs=[pl.BlockSpec((tm, D_MODEL), lambda i: (i, 0)),
                   pl.BlockSpec((1, 1), lambda i: (0, 0)),
                   pl.BlockSpec((1, D_MODEL), lambda i: (0, 0))],
        out_shape=[jax.ShapeDtypeStruct((L, D_MODEL), F32), jax.ShapeDtypeStruct((1, 1), F32),
                   jax.ShapeDtypeStruct((1, D_MODEL), F32)],
        compiler_params=_cparams(1),
    )(h, gf, target)


def _ffn_bwd_act(dh, h, g2, fgate_s, fup_s, wp, layer):
    L = h.shape[0]
    tm = min(TM_FFN, L)
    sub = tm // FFN_SPLIT

    def body(dh_ref, h_ref, g_ref, fgate_ref, fup_ref, w_ref,
             dhm_ref, dg_ref, dgate_ref, dup_ref, dhb_ref):
        m, k = pl.program_id(0), pl.program_id(1)
        dn2 = dhm_ref

        @pl.when(jnp.logical_and(m == 0, k == 0))
        def _():
            dg_ref[...] = jnp.zeros_like(dg_ref)

        @pl.when(k == 0)
        def _():
            dhb_ref[...] = dh_ref[...].astype(BF16)
            dn2[...] = jnp.zeros_like(dn2)

        wd, wg_t, wu_t = _ffn_weights(w_ref, k)
        for rows in (slice(r * sub, (r + 1) * sub) for r in range(tm // sub)):
            dact = _dot_nt(dhb_ref[rows, :], wd)
            dgate = (dact * fgate_ref[rows, :].astype(F32)).astype(BF16)
            dup = (dact * fup_ref[rows, :].astype(F32)).astype(BF16)
            dgate_ref[rows, :] = dgate
            dup_ref[rows, :] = dup
            dn2[rows, :] += _dot(dgate, wg_t) + _dot(dup, wu_t)

        @pl.when(k == N_SHARD - 1)
        def _():
            xhat, r = _rms_hat(h_ref[...])
            d = dn2[...]
            dg_ref[...] += jnp.sum(d * xhat, axis=0, keepdims=True)
            dhm_ref[...] = dh_ref[...] + _rms_bwd(d * g_ref[...], xhat, r)

    act_spec = pl.BlockSpec((None, tm, FF_SHARD), lambda m, k: (k, m, 0))
    act_shape = jax.ShapeDtypeStruct((N_SHARD, L, FF_SHARD), BF16)
    row_spec = pl.BlockSpec((tm, D_MODEL), lambda m, k: (m, 0))
    return pl.pallas_call(
        body, name="ffn_bwd_act", grid=(L // tm, N_SHARD),
        in_specs=[row_spec, row_spec,
                  pl.BlockSpec((None, 1, D_MODEL), lambda m, k: (layer, 0, 0)),
                  act_spec, act_spec,
                  _ffn_weight_spec()],
        out_specs=[row_spec,
                   pl.BlockSpec((1, D_MODEL), lambda m, k: (0, 0)),
                   act_spec, act_spec, row_spec],
        out_shape=[jax.ShapeDtypeStruct((L, D_MODEL), F32), jax.ShapeDtypeStruct((1, D_MODEL), F32),
                   act_shape, act_shape, jax.ShapeDtypeStruct((L, D_MODEL), BF16)],
        compiler_params=_cparams(2),
    )(dh, h, g2, fgate_s, fup_s, wp)


def _ffn_bwd_w(n2, dgate_s, dup_s, act_s, dhb, gbuf):
    L = n2.shape[0]
    tm = min(TM_FFN_LONG, L)

    def body(n2_ref, dgate_ref, dup_ref, act_ref, dhb_ref, g_in, g_ref):
        m = pl.program_id(1)

        @pl.when(m == 0)
        def _():
            g_ref[...] = jnp.zeros_like(g_ref)

        n2v = n2_ref[...]
        g_ref[0:FF_SHARD, :] += _dot_tn(act_ref[...], dhb_ref[...])
        g_ref[FF_SHARD:2 * FF_SHARD, :] += _dot_tn(dgate_ref[...], n2v)
        g_ref[2 * FF_SHARD:P_FF_ROWS, :] += _dot_tn(dup_ref[...], n2v)

    act_spec = pl.BlockSpec((None, tm, FF_SHARD), lambda k, m: (k, m, 0))
    row_spec = pl.BlockSpec((tm, D_MODEL), lambda k, m: (m, 0))
    return pl.pallas_call(
        body, name="ffn_bwd_w", grid=(N_SHARD, L // tm),
        in_specs=[row_spec, act_spec, act_spec, act_spec, row_spec, pl.BlockSpec(memory_space=pl.ANY)],
        out_specs=pl.BlockSpec((None, None, P_FF_ROWS, D_MODEL), lambda k, m: (0, k, 0, 0)),
        out_shape=jax.ShapeDtypeStruct(gbuf.shape, F32),
        input_output_aliases={5: 0},
        compiler_params=_cparams(2),
    )(n2, dgate_s, dup_s, act_s, dhb, gbuf)


def _mix_out_bwd(dhm, yraw, ypool, wp, layer, b_glu, gbuf):
    L = dhm.shape[0]
    tm = min(TM, L)

    def body(dhm_ref, yr_ref, yp_ref, wglu_ref, b_ref, wout_ref, g1_in,
             dyr_ref, dyp_ref, db_ref, g1_ref, dwout, dwglu, gpack):
        i = pl.program_id(0)

        @pl.when(i == 0)
        def _():
            db_ref[...] = jnp.zeros_like(db_ref)
            dwout[...] = jnp.zeros_like(dwout)
            dwglu[...] = jnp.zeros_like(dwglu)

        dhb = dhm_ref[...].astype(BF16)
        wglu = _glu_weight(wglu_ref)
        dmix = _dot_nt(dhb, wout_ref[...].reshape(D_MODEL, D_MODEL))
        dyp_ref[...] = dmix[:, :D_POOL]
        d_o = dmix[:, D_POOL:]
        yraw_v = yr_ref[...]
        y = _gelu(yraw_v)
        yb = y.astype(BF16)
        sig = _sigmoid(_dot(yb, wglu) + b_ref[...])
        mix = jnp.concatenate([yp_ref[...], y * sig], axis=1).astype(BF16)
        dwout[...] += _dot_tn(mix, dhb).reshape(N_SHARD, 256, D_MODEL)
        dz = d_o * y * sig * (1.0 - sig)
        dzb = dz.astype(BF16)
        db_ref[...] += jnp.sum(dz, axis=0, keepdims=True)
        dwglu[...] += _dot_tn(yb, dzb)
        dy = d_o * sig + _dot_nt(dzb, wglu)
        dyr_ref[...] = dy * _gelu_grad(yraw_v)

        @pl.when(i == n_steps - 1)
        def _():
            gpack[:, :gb, :] = _glu_pack(dwglu[...])
            gpack[:, gb:, :] = jnp.zeros((N_SHARD, P_GLU_PAD - gb, D_MODEL), F32)
            pltpu.sync_copy(gpack, g1_ref.at[0, :, pl.ds(gb * gi, P_GLU_PAD), :])
            pltpu.sync_copy(dwout, g1_ref.at[0, :, pl.ds(ob * oi, ob), :])

    gb, gi = P_GLU_BLK
    ob, oi = P_OUT_BLK
    n_steps = L // tm
    return pl.pallas_call(
        body, name="mix_out_bwd", grid=(n_steps,),
        in_specs=[pl.BlockSpec((tm, D_MODEL), lambda i: (i, 0)),
                  pl.BlockSpec((tm, D_SSM), lambda i: (i, 0)),
                  pl.BlockSpec((tm, D_POOL), lambda i: (i, 0)),
                  pl.BlockSpec((N_SHARD, None, gb, D_MODEL), lambda i: (0, 0, gi, 0)),
                  pl.BlockSpec((None, 1, D_SSM), lambda i: (layer, 0, 0)),
                  pl.BlockSpec((N_SHARD, None, ob, D_MODEL), lambda i: (0, 0, oi, 0)),
                  pl.BlockSpec(memory_space=pl.ANY)],
        out_specs=[pl.BlockSpec((tm, D_SSM), lambda i: (i, 0)),
                   pl.BlockSpec((tm, D_POOL), lambda i: (i, 0)),
                   pl.BlockSpec((1, D_SSM), lambda i: (0, 0)),
                   pl.BlockSpec(memory_space=pl.ANY)],
        out_shape=[jax.ShapeDtypeStruct((L, D_SSM), F32), jax.ShapeDtypeStruct((L, D_POOL), F32),
                   jax.ShapeDtypeStruct((1, D_SSM), F32),
                   jax.ShapeDtypeStruct(gbuf.shape, F32)],
        scratch_shapes=[pltpu.VMEM((N_SHARD, ob, D_MODEL), F32), pltpu.VMEM((D_SSM, D_SSM), F32),
                        pltpu.VMEM((N_SHARD, P_GLU_PAD, D_MODEL), F32)],
        input_output_aliases={6: 3},
        compiler_params=_cparams(1),
    )(dhm, yraw, ypool, wp, b_glu, wp, gbuf)


def _ssm_bwd(dyraw, u, sre, sim, layer, cpad_t, bpad_t, ar, ai, dskip):
    L = u.shape[0]
    ts = min(TS, L)
    nt = L // ts
    nq = 4
    cq = N_STATE // nq

    def body(dy_ref, u_ref, sre_ref, sim_ref, ct_ref, bt_ref, ar_ref, ai_ref, dsk_ref,
             du_ref, dcp_ref, dbp_ref, dar_ref, dai_ref, ddsk_ref, gre, gim, cr, ci, tab, accr, acci, up, dyp):
        t = pl.program_id(1)

        @pl.when(t == 0)
        def _():
            for ref in (cr, ci, accr, acci, dcp_ref, dbp_ref, ddsk_ref):
                ref[...] = jnp.zeros_like(ref)
            _scan_tables(ar_ref[...], -ai_ref[...], tab, reverse=True)

        _permute_rows(dy_ref, dyp, ts)
        _permute_rows(u_ref, up, ts)
        dy = dyp[...]
        dyb = dy.astype(BF16)
        uf = up[...]
        ub = uf.astype(BF16)
        for jj in range(4):
            cols = slice(jj * 128, (jj + 1) * 128)
            ds = _dot(dyb, ct_ref[jj])
            gre[:, cols] = ds[:, :128]
            gim[:, cols] = ds[:, 128:]
            scat = jnp.concatenate([sre_ref[:, cols], sim_ref[:, cols]], axis=1).astype(BF16)
            dcp_ref[jj] += _dot_tn(scat, dyb)

        n_blk = ts // SCAN_BLOCK
        shp = (SUBLANES, SCAN_LANES)
        last_row = lax.broadcasted_iota(jnp.int32, shp, 0) == SUBLANES - 1
        for cc in range(cq // SCAN_LANES):
            cols = slice(cc * SCAN_LANES, (cc + 1) * SCAN_LANES)

            def block(i, carry, cols=cols):
                c_r, c_i, a_r, a_i = carry
                base = pl.multiple_of((n_blk - 1 - i) * SCAN_BLOCK, SCAN_BLOCK)
                rows = lambda tau: pl.ds(base + SUBLANES * tau, SUBLANES)
                m_r, m_i = tab[0, :, cols], tab[1, :, cols]
                ys = [None] * SUBLANES
                ys[SUBLANES - 1] = (gre[rows(SUBLANES - 1), cols], gim[rows(SUBLANES - 1), cols])
                for tau in reversed(range(SUBLANES - 1)):
                    ys[tau] = _cmac(gre[rows(tau), cols], gim[rows(tau), cols], m_r, m_i, *ys[tau + 1])
                tr, ti = _chain_segments(*ys[0], c_r, c_i, tab, cols, reverse=True)
                in_r = jnp.where(last_row, c_r, pltpu.roll(tr, SUBLANES - 1, 0))
                in_i = jnp.where(last_row, c_i, pltpu.roll(ti, SUBLANES - 1, 0))
                gs = [_cmac(*ys[tau], tab[10 + 2 * tau, :, cols], tab[11 + 2 * tau, :, cols], in_r, in_i)
                      for tau in range(SUBLANES)]
                for tau in range(SUBLANES):
                    gre[rows(tau), cols] = gs[tau][0]
                    gim[rows(tau), cols] = gs[tau][1]
                    if tau < SUBLANES - 1:
                        nr, ni = gs[tau + 1]
                    else:
                        nr = jnp.where(last_row, c_r, pltpu.roll(gs[0][0], SUBLANES - 1, 0))
                        ni = jnp.where(last_row, c_i, pltpu.roll(gs[0][1], SUBLANES - 1, 0))
                    sr, si = sre_ref[rows(tau), cols], sim_ref[rows(tau), cols]
                    a_r = a_r + sr * nr + si * ni
                    a_i = a_i + sr * ni - si * nr
                return (jnp.broadcast_to(tr[:1, :], shp), jnp.broadcast_to(ti[:1, :], shp), a_r, a_i)

            c_r, c_i, a_r, a_i = lax.fori_loop(
                0, n_blk, block, (cr[:, cols], ci[:, cols], accr[:, cols], acci[:, cols]), unroll=2)
            cr[:, cols] = c_r
            ci[:, cols] = c_i
            accr[:, cols] = a_r
            acci[:, cols] = a_i

        acc = dsk_ref[...] * dy
        for jj in range(4):
            cols = slice(jj * 128, (jj + 1) * 128)
            gcat = jnp.concatenate([gre[:, cols], gim[:, cols]], axis=1).astype(BF16)
            acc = acc + _dot(gcat, bt_ref[jj])
            dbp_ref[jj] += _dot_tn(ub, gcat)
        ddsk_ref[...] += jnp.sum(dy * uf, axis=0, keepdims=True)
        dyp[...] = acc
        _permute_rows(dyp, du_ref, ts)

        @pl.when(t == nt - 1)
        def _():
            dar_ref[...] = jnp.sum(accr[...], axis=0, keepdims=True)
            dai_ref[...] = jnp.sum(acci[...], axis=0, keepdims=True)

    f32_scr = lambda *s: pltpu.VMEM(s, F32)
    return pl.pallas_call(
        body, name="ssm_bwd", grid=(nq, nt),
        in_specs=[pl.BlockSpec((ts, 128), lambda q, t: (nt - 1 - t, q)),
                  pl.BlockSpec((ts, 128), lambda q, t: (nt - 1 - t, 4 + q)),
                  pl.BlockSpec((ts, cq), lambda q, t: (nt - 1 - t, q)),
                  pl.BlockSpec((ts, cq), lambda q, t: (nt - 1 - t, q)),
                  pl.BlockSpec((None, 4, 128, 256), lambda q, t: (layer, q, 0, 0)),
                  pl.BlockSpec((None, 4, 256, 128), lambda q, t: (layer, q, 0, 0)),
                  pl.BlockSpec((None, 1, cq), lambda q, t: (layer, 0, q)),
                  pl.BlockSpec((None, 1, cq), lambda q, t: (layer, 0, q)),
                  pl.BlockSpec((None, 1, 128), lambda q, t: (layer, 0, q))],
        out_specs=[pl.BlockSpec((ts, 128), lambda q, t: (nt - 1 - t, q)),
                   pl.BlockSpec((4, 256, 128), lambda q, t: (q, 0, 0)),
                   pl.BlockSpec((4, 128, 256), lambda q, t: (q, 0, 0)),
                   pl.BlockSpec((1, cq), lambda q, t: (0, q)),
                   pl.BlockSpec((1, cq), lambda q, t: (0, q)),
                   pl.BlockSpec((1, 128), lambda q, t: (0, q))],
        out_shape=[jax.ShapeDtypeStruct((L, D_SSM), F32),
                   jax.ShapeDtypeStruct((N_PAIRS, 256, 128), F32), jax.ShapeDtypeStruct((N_PAIRS, 128, 256), F32),
                   jax.ShapeDtypeStruct((1, N_STATE), F32), jax.ShapeDtypeStruct((1, N_STATE), F32),
                   jax.ShapeDtypeStruct((1, D_SSM), F32)],
        scratch_shapes=[f32_scr(ts, cq), f32_scr(ts, cq), f32_scr(SUBLANES, cq), f32_scr(SUBLANES, cq),
                        f32_scr(N_SCAN_TABLES, SUBLANES, cq), f32_scr(SUBLANES, cq), f32_scr(SUBLANES, cq),
                        f32_scr(ts, 128), f32_scr(ts, 128)],
        compiler_params=_cparams(2),
    )(dyraw, u, sre, sim, cpad_t, bpad_t, ar, ai, dskip)


def _pool_bwd(dyp, u, layer, w_pool, scale):
    L = u.shape[0]
    tm = min(TM, L)
    nt = L // tm
    halo_per_tile = tm // POOL_HALO

    def body(dyp_ref, u_ref, halo_ref, wp_ref, sc_ref, du_ref, dwp_ref, dsc_ref, carry):
        i = pl.program_id(0)
        tile = nt - 1 - i

        @pl.when(i == 0)
        def _():
            carry[...] = jnp.zeros_like(carry)
            dwp_ref[...] = jnp.zeros_like(dwp_ref)
            dsc_ref[...] = jnp.zeros_like(dsc_ref)

        up = u_ref[...]
        halo = jnp.where(tile > 0, halo_ref[...], jnp.zeros_like(halo_ref))
        diffs = _pool_diff(jnp.concatenate([halo, up], axis=0), tile * tm, tm)
        rows = tile * tm + lax.broadcasted_iota(jnp.int32, (tm, 1), 0)
        n_ext = tm + POOL_HALO
        for gi, w in enumerate(POOL_WINDOWS):
            cols = slice(gi * POOL_GROUP, (gi + 1) * POOL_GROUP)
            db = diffs[gi].astype(BF16)
            dyp = dyp_ref[:, cols]
            dsc_ref[:, cols] += jnp.sum(dyp * _dot(db, wp_ref[gi]), axis=0, keepdims=True)
            dp = (dyp * sc_ref[:, cols]).astype(BF16)
            ddiff = _dot_nt(dp, wp_ref[gi])
            dwp_ref[gi] += _dot_tn(db, dp)
            e = ddiff * (1.0 / jnp.minimum(rows + 1, w).astype(F32))
            s = jnp.concatenate([e, carry[:, cols]], axis=0)
            k = 1
            while k < w:
                s = s + pltpu.roll(s, n_ext - k, 0)
                k *= 2
            du_ref[:, cols] = s[:tm, :] - ddiff
            carry[:, cols] = e[:POOL_HALO, :]

    return pl.pallas_call(
        body, name="pool_bwd", grid=(nt,),
        in_specs=[pl.BlockSpec((tm, D_POOL), lambda i: (nt - 1 - i, 0)),
                  pl.BlockSpec((tm, D_POOL), lambda i: (nt - 1 - i, 0)),
                  pl.BlockSpec((POOL_HALO, D_POOL), lambda i: (jnp.maximum((nt - 1 - i) * halo_per_tile - 1, 0), 0)),
                  pl.BlockSpec((None, 4, POOL_GROUP, POOL_GROUP), lambda i: (layer, 0, 0, 0)),
                  pl.BlockSpec((None, 1, D_POOL), lambda i: (layer, 0, 0))],
        out_specs=[pl.BlockSpec((tm, D_POOL), lambda i: (nt - 1 - i, 0)),
                   pl.BlockSpec((4, POOL_GROUP, POOL_GROUP), lambda i: (0, 0, 0)),
                   pl.BlockSpec((1, D_POOL), lambda i: (0, 0))],
        out_shape=[jax.ShapeDtypeStruct((L, D_POOL), F32),
                   jax.ShapeDtypeStruct((4, POOL_GROUP, POOL_GROUP), F32),
                   jax.ShapeDtypeStruct((1, D_POOL), F32)],
        scratch_shapes=[pltpu.VMEM((POOL_HALO, D_POOL), F32)],
        compiler_params=_cparams(1),
    )(dyp, u, u, w_pool, scale)


def _mix_in_bwd(dup, dus, h, dhm, g1, wp, layer, gbuf):
    L = h.shape[0]
    tm = min(TM, L)
    n_steps = L // tm
    blk, idx = P_IN_BLK

    def body(dup_ref, dus_ref, h_ref, dhm_ref, g_ref, w_ref, g1_in, dh_ref, dg_ref, g1_ref, dwin):
        i = pl.program_id(0)

        @pl.when(i == 0)
        def _():
            dg_ref[...] = jnp.zeros_like(dg_ref)
            dwin[...] = jnp.zeros_like(dwin)

        du = jnp.concatenate([dup_ref[...], dus_ref[...]], axis=1).astype(BF16)
        dn1 = _dot_nt(du, w_ref[...].reshape(D_MODEL, D_MODEL))
        xhat, r = _rms_hat(h_ref[...])
        g = g_ref[...]
        n1 = (xhat * g).astype(BF16)
        dwin[...] += _dot_tn(n1, du).reshape(N_SHARD, blk, D_MODEL)
        dg_ref[...] += jnp.sum(dn1 * xhat, axis=0, keepdims=True)
        dh_ref[...] = dhm_ref[...] + _rms_bwd(dn1 * g, xhat, r)

        @pl.when(i == n_steps - 1)
        def _():
            pltpu.sync_copy(dwin, g1_ref.at[0, :, pl.ds(blk * idx, blk), :])

    row_spec = pl.BlockSpec((tm, D_MODEL), lambda i: (i, 0))
    half_spec = pl.BlockSpec((tm, D_POOL), lambda i: (i, 0))
    return pl.pallas_call(
        body, name="mix_in_bwd", grid=(n_steps,),
        in_specs=[half_spec, half_spec, row_spec, row_spec,
                  pl.BlockSpec((None, 1, D_MODEL), lambda i: (layer, 0, 0)),
                  pl.BlockSpec((N_SHARD, None, blk, D_MODEL), lambda i: (0, 0, idx, 0)),
                  pl.BlockSpec(memory_space=pl.ANY)],
        out_specs=[row_spec, pl.BlockSpec((1, D_MODEL), lambda i: (0, 0)), pl.BlockSpec(memory_space=pl.ANY)],
        out_shape=[jax.ShapeDtypeStruct((L, D_MODEL), F32), jax.ShapeDtypeStruct((1, D_MODEL), F32),
                   jax.ShapeDtypeStruct(gbuf.shape, F32)],
        scratch_shapes=[pltpu.VMEM((N_SHARD, blk, D_MODEL), F32)],
        input_output_aliases={6: 2},
        compiler_params=_cparams(1),
    )(dup, dus, h, dhm, g1, wp, gbuf)


def _disc_math(lr, li, ldt, br_t, bi_t):
    dt = jnp.exp(ldt)
    mag = jnp.exp(lr * dt)
    ang = li * dt
    ar = mag * jnp.cos(ang)
    ai = mag * jnp.sin(ang)
    den = lr * lr + li * li
    nr, ni = ar - 1.0, ai
    cr = (nr * lr + ni * li) / den
    ci = (ni * lr - nr * li) / den
    return ar, ai, cr * br_t - ci * bi_t, cr * bi_t + ci * br_t


def _disc_fwd(lr, li, ldt, br_t, bi_t):
    def body(lr_ref, li_ref, ldt_ref, br_ref, bi_ref, ar_ref, ai_ref, bbr_ref, bbi_ref):
        ar, ai, bbr, bbi = _disc_math(lr_ref[...], li_ref[...], ldt_ref[...], br_ref[...], bi_ref[...])
        ar_ref[...] = ar
        ai_ref[...] = ai
        bbr_ref[...] = bbr
        bbi_ref[...] = bbi

    shapes = [jax.ShapeDtypeStruct(a.shape, F32) for a in (lr, li, br_t, bi_t)]
    return pl.pallas_call(body, name="ssm_disc_fwd", out_shape=shapes,
                          compiler_params=pltpu.CompilerParams(vmem_limit_bytes=VMEM_LIMIT))(lr, li, ldt, br_t, bi_t)


def _disc_bwd(lr, li, ldt, br_t, bi_t, dar, dai, dbbr, dbbi):
    def body(lr_ref, li_ref, ldt_ref, br_ref, bi_ref, dar_ref, dai_ref, dbbr_ref, dbbi_ref,
             dlr_ref, dli_ref, dldt_ref, dbr_ref, dbi_ref):
        prim = (lr_ref[...], li_ref[...], ldt_ref[...], br_ref[...], bi_ref[...])
        _, pullback = jax.vjp(_disc_math, *prim)
        dlr, dli, dldt, dbr, dbi = pullback((dar_ref[...], dai_ref[...], dbbr_ref[...], dbbi_ref[...]))
        dlr_ref[...] = dlr
        dli_ref[...] = dli
        dldt_ref[...] = dldt
        dbr_ref[...] = dbr
        dbi_ref[...] = dbi

    shapes = [jax.ShapeDtypeStruct(a.shape, F32) for a in (lr, li, ldt, br_t, bi_t)]
    return pl.pallas_call(body, name="ssm_disc_bwd", out_shape=shapes,
                          compiler_params=pltpu.CompilerParams(vmem_limit_bytes=VMEM_LIMIT))(
        lr, li, ldt, br_t, bi_t, dar, dai, dbbr, dbbi)


def _pad_pairs(m_re, m_im):
    def blocks(m):
        v = m.transpose(0, 2, 1).reshape(N_PAIRS, 2, SSM_GROUP, SSM_STATE)
        return jnp.einsum("ab,jahp->jahbp", jnp.eye(2, dtype=m.dtype), v).reshape(N_PAIRS, 32, 128)
    both = jnp.concatenate([blocks(m_re), blocks(m_im)], axis=-1)
    place = jax.nn.one_hot(jnp.arange(N_PAIRS) % 4, 4, dtype=both.dtype)
    return jnp.einsum("jk,jrc->jkrc", place, both).reshape(N_PAIRS, 128, 256)


def _unpad_pairs(x):
    place = jax.nn.one_hot(jnp.arange(N_PAIRS) % 4, 4, dtype=x.dtype)
    both = jnp.einsum("jk,jkrc->jrc", place, x.reshape(N_PAIRS, 4, 32, 256))

    def unblock(v):
        v = v.reshape(N_PAIRS, 2, SSM_GROUP, 2, SSM_STATE)
        d = jnp.einsum("ab,jahbp->jahp", jnp.eye(2, dtype=x.dtype), v)
        return d.reshape(N_SSM_GROUPS, SSM_GROUP, SSM_STATE).transpose(0, 2, 1)
    return unblock(both[..., :128]), unblock(both[..., 128:])


def _adamw_math(w, g, m, v):
    m = ADAM_B1 * m + (1.0 - ADAM_B1) * g
    v = ADAM_B2 * v + (1.0 - ADAM_B2) * (g * g)
    m_hat = m / (1.0 - ADAM_B1 ** ADAM_STEP)
    v_hat = v / (1.0 - ADAM_B2 ** ADAM_STEP)
    delta = -ADAM_LR * (m_hat / (jnp.sqrt(v_hat) + ADAM_EPS) + ADAM_WD * w)
    return delta, m, v


def _adamw(name, layer, w, m, v, gbuf, g_block, g_row0, row_tile, outs=None, after=(), glu=False):
    nl, r, c = w.shape
    n_tiles = r // row_tile
    g_rows, g_cols = g_block
    g_tile = g_rows // n_tiles
    g_off = g_row0 // g_tile
    if outs is None:
        outs = [lax.empty(w.shape, F32) for _ in range(4)]

    def body(w_ref, m_ref, v_ref, g_ref, *rest):
        go_ref, d_ref, mo_ref, vo_ref = rest[-4:]
        g = g_ref[...]
        if glu:
            g = jnp.concatenate([g[:, :D_SSM], g[:, D_SSM:]], axis=0)
        delta, mn, vn = _adamw_math(w_ref[...], g, m_ref[...], v_ref[...])
        go_ref[...] = g
        d_ref[...] = delta
        mo_ref[...] = mn
        vo_ref[...] = vn

    w_spec = pl.BlockSpec((None, row_tile, c), lambda j: (layer, j, 0))
    shape = jax.ShapeDtypeStruct(w.shape, F32)
    return pl.pallas_call(
        body, name=name, grid=(n_tiles,),
        in_specs=[w_spec, w_spec, w_spec, pl.BlockSpec((None, g_tile, g_cols), lambda j: (0, g_off + j, 0))]
        + [_ANY] * (4 + len(after)),
        out_specs=[w_spec] * 4,
        out_shape=[shape] * 4,
        input_output_aliases={4: 0, 5: 1, 6: 2, 7: 3},
        compiler_params=_cparams(1),
    )(w, m, v, gbuf, *outs, *after)


def _adamw_group(name, layer, ws, ms, vs, gbuf, g_row0s, row_tile, outs=None):
    k = len(ws)
    nl, r, c = ws[0].shape
    n_tiles = r // row_tile
    if outs is None:
        outs = [[lax.empty(ws[0].shape, F32) for _ in range(4)] for _ in range(k)]

    def body(*refs):
        ins, results = refs[:4 * k], refs[-4 * k:]
        for i in range(k):
            w_ref, m_ref, v_ref, g_ref = (ins[j * k + i] for j in range(4))
            g = g_ref[...]
            delta, mn, vn = _adamw_math(w_ref[...], g, m_ref[...], v_ref[...])
            for ref, val in zip(results[4 * i:4 * i + 4], (g, delta, mn, vn)):
                ref[...] = val

    w_spec = pl.BlockSpec((None, row_tile, c), lambda j: (layer, j, 0))
    g_specs = [pl.BlockSpec((None, row_tile, c), functools.partial(lambda j, off: (0, off + j, 0), off=r0 // row_tile))
               for r0 in g_row0s]
    shape = jax.ShapeDtypeStruct(ws[0].shape, F32)
    flat = pl.pallas_call(
        body, name=name, grid=(n_tiles,),
        in_specs=[w_spec] * (3 * k) + g_specs + [_ANY] * (4 * k),
        out_specs=[w_spec] * (4 * k),
        out_shape=[shape] * (4 * k),
        input_output_aliases={4 * k + i: i for i in range(4 * k)},
        compiler_params=_cparams(1),
    )(*ws, *ms, *vs, *([gbuf] * k), *[a for group in outs for a in group])
    return [flat[4 * i:4 * i + 4] for i in range(k)]


def _pack_weights(ids, layer, w_in, w_glu, w_out, w_down, w_gate_t, w_up_t, after=()):
    gb, gi = P_GLU_BLK
    ib, ii = P_IN_BLK
    ob, oi = P_OUT_BLK

    def body(ids_ref, in_ref, glu_ref, out_ref, dn_ref, gate_ref, up_ref, *rest):
        p_ref = rest[-1]
        p_ref[0:FF_SHARD, :] = dn_ref[...].astype(BF16)
        p_ref[FF_SHARD:2 * FF_SHARD, :] = gate_ref[...].astype(BF16)
        p_ref[2 * FF_SHARD:P_FF_ROWS, :] = up_ref[...].astype(BF16)
        g = glu_ref[...]
        p_ref[gb * gi:gb * (gi + 1), :] = jnp.concatenate([g[:gb, :], g[gb:, :]], axis=1).astype(BF16)
        p_ref[gb * (gi + 1):ib * ii, :] = jnp.zeros((P_GLU_PAD - gb, D_MODEL), BF16)
        p_ref[ib * ii:ib * (ii + 1), :] = in_ref[...].astype(BF16)
        p_ref[ob * oi:ob * (oi + 1), :] = out_ref[...].astype(BF16)

    def spec(a):
        return pl.BlockSpec((None,) + a.shape[1:], lambda i, ids_ref: (layer, 0, 0))

    ins = (w_in, w_glu, w_out, w_down, w_gate_t, w_up_t)
    grid_spec = pltpu.PrefetchScalarGridSpec(
        num_scalar_prefetch=1, grid=(1,),
        in_specs=[spec(a) for a in ins] + [_ANY] * len(after),
        out_specs=pl.BlockSpec((None, None, P_ROWS, D_MODEL), lambda i, ids_ref: (ids_ref[1], 0, 0, 0)))
    return pl.pallas_call(
        body, name="pack_weights", grid_spec=grid_spec,
        out_shape=jax.ShapeDtypeStruct((N_SHARD, 1, P_ROWS, D_MODEL), BF16),
        compiler_params=_cparams(1),
    )(ids, *ins, *after)


MESH = pl.DeviceIdType.MESH
_ANY = pl.BlockSpec(memory_space=pl.ANY)
P_HALF = P_ROWS // 2
RS_ADD_TILE = 1408
RS_SUM_TILE = 352


def _mesh_pos():
    return lax.axis_index("x"), lax.axis_index("y"), lax.axis_index("c")


def _other_chips(x, y):
    return [(1 - x, y), (x, 1 - y), (1 - x, 1 - y)]


def _remote(src, dst, send_sems, recv_sems, n, to):
    return pltpu.make_async_remote_copy(src_ref=src, dst_ref=dst, send_sem=send_sems.at[n],
                                        recv_sem=recv_sems.at[n], device_id=to, device_id_type=MESH)


_HBM = pl.BlockSpec(memory_space=pltpu.HBM)
_SEM = pl.BlockSpec(memory_space=pltpu.SEMAPHORE)
_EFFECT = pltpu.CompilerParams(has_side_effects=pltpu.SideEffectType.DATAFLOW_SIDE_EFFECTING)
_TOKEN = jax.ShapeDtypeStruct((8, 128), F32)


def _in_hbm(a):
    return pltpu.with_memory_space_constraint(a, pltpu.HBM)


def _ag_piece(ref, shard, half, rows):
    row0, n_rows = rows
    return ref.at[shard, :, pl.ds(row0 + half * (n_rows // 2), n_rows // 2), :]


def _ag_start(name, wp, after, row_ranges):
    n_sems = 3 * len(row_ranges)

    def body(w_ref, after_ref, send_sems, recv_sems, w_thru, token):
        x, y, c = _mesh_pos()
        for i, rows in enumerate(row_ranges):
            mine = _ag_piece(w_ref, 2 * x + y, c, rows)
            for j, (px, py) in enumerate(_other_chips(x, y)):
                _remote(mine, mine, send_sems, recv_sems, 3 * i + j, (px, py, c)).start()
        token[...] = jnp.zeros_like(token)

    return pl.pallas_call(
        body, name=name,
        out_shape=(pltpu.SemaphoreType.DMA((n_sems,)), pltpu.SemaphoreType.DMA((n_sems,)),
                   pltpu.HBM(wp.shape, wp.dtype), _TOKEN),
        in_specs=(_HBM, _ANY), out_specs=(_SEM, _SEM, _HBM, pl.BlockSpec(memory_space=pltpu.VMEM)),
        input_output_aliases={0: 2}, compiler_params=_EFFECT,
    )(_in_hbm(wp), after)


def _ag_wait(name, send_sems, recv_sems, wp, after, row_ranges):
    def body(w_ref, send_sems, recv_sems, *rest):
        x, y, c = _mesh_pos()
        for i, rows in enumerate(row_ranges):
            mine = _ag_piece(w_ref, 2 * x + y, c, rows)
            for j, (px, py) in enumerate(_other_chips(x, y)):
                landed = _ag_piece(w_ref, 2 * px + py, c, rows)
                cp = _remote(mine, landed, send_sems, recv_sems, 3 * i + j, (px, py, c))
                cp.wait_send()
                cp.wait_recv()

    return pl.pallas_call(
        body, name=name, out_shape=pltpu.HBM(wp.shape, wp.dtype),
        in_specs=(_HBM, _SEM, _SEM) + (_ANY,) * len(after), out_specs=_HBM,
        input_output_aliases={0: 0}, compiler_params=_EFFECT,
    )(wp, send_sems, recv_sems, *after)


def _ag_forward(wp, rows):
    def body(w_in, o, send_sems, recv_sems):
        x, y, c = _mesh_pos()
        sib = (x, y, 1 - c)
        chips = _other_chips(x, y)
        sends = []
        for j, (px, py) in enumerate(chips):
            landed = _ag_piece(o, 2 * px + py, c, rows)
            cp = _remote(landed, landed, send_sems, recv_sems, j, sib)
            cp.start()
            sends.append(cp)
        for j, (px, py) in enumerate(chips):
            passed = _ag_piece(o, 2 * px + py, 1 - c, rows)
            _remote(passed, passed, send_sems, recv_sems, j, sib).wait_recv()
        for cp in sends:
            cp.wait_send()

    return pl.pallas_call(
        body, name="ag_forward",
        in_specs=[_ANY], out_specs=_ANY,
        out_shape=jax.ShapeDtypeStruct(wp.shape, wp.dtype),
        scratch_shapes=[pltpu.SemaphoreType.DMA((3,)), pltpu.SemaphoreType.DMA((3,))],
        input_output_aliases={0: 0},
    )(wp)


def _ag_forward_start(name, wp, rows):
    def body(w_ref, send_sems, recv_sems, w_thru):
        x, y, c = _mesh_pos()
        for j, (px, py) in enumerate(_other_chips(x, y)):
            landed = _ag_piece(w_ref, 2 * px + py, c, rows)
            _remote(landed, landed, send_sems, recv_sems, j, (x, y, 1 - c)).start()

    return pl.pallas_call(
        body, name=name,
        out_shape=(pltpu.SemaphoreType.DMA((3,)), pltpu.SemaphoreType.DMA((3,)), pltpu.HBM(wp.shape, wp.dtype)),
        in_specs=(_HBM,), out_specs=(_SEM, _SEM, _HBM),
        input_output_aliases={0: 2}, compiler_params=_EFFECT,
    )(_in_hbm(wp))


def _ag_forward_wait(name, send_sems, recv_sems, wp, after, rows):
    def body(w_ref, send_sems, recv_sems, *rest):
        x, y, c = _mesh_pos()
        for j, (px, py) in enumerate(_other_chips(x, y)):
            cp = _remote(_ag_piece(w_ref, 2 * px + py, c, rows), _ag_piece(w_ref, 2 * px + py, 1 - c, rows),
                         send_sems, recv_sems, j, (x, y, 1 - c))
            cp.wait_send()
            cp.wait_recv()

    return pl.pallas_call(
        body, name=name, out_shape=pltpu.HBM(wp.shape, wp.dtype),
        in_specs=(_HBM, _SEM, _SEM) + (_ANY,) * len(after), out_specs=_HBM,
        input_output_aliases={0: 0}, compiler_params=_EFFECT,
    )(wp, send_sems, recv_sems, *after)


def _rs_chips_start(name, t):
    nl = t.shape[0]

    def body(t_ref, land_ref, send_sems, recv_sems, t_thru, land_thru, token):
        x, y, c = _mesh_pos()
        for j, (px, py) in enumerate(_other_chips(x, y)):
            _remote(t_ref.at[:, 2 * px + py], land_ref.at[j], send_sems, recv_sems, j, (px, py, c)).start()
        token[...] = jnp.zeros_like(token)

    land = lax.empty((3, nl, P_HALF, D_MODEL), BF16)
    return pl.pallas_call(
        body, name=name,
        out_shape=(pltpu.SemaphoreType.DMA((3,)), pltpu.SemaphoreType.DMA((3,)), pltpu.HBM(t.shape, t.dtype),
                   pltpu.HBM(land.shape, land.dtype), _TOKEN),
        in_specs=(_HBM, _HBM), out_specs=(_SEM, _SEM, _HBM, _HBM, pl.BlockSpec(memory_space=pltpu.VMEM)),
        input_output_aliases={0: 2, 1: 3}, compiler_params=_EFFECT,
    )(_in_hbm(t), _in_hbm(land))


def _rs_chips_wait(name, send_sems, recv_sems, t, land, after):
    def body(t_ref, land_ref, send_sems, recv_sems, *rest):
        x, y, c = _mesh_pos()
        for j, (px, py) in enumerate(_other_chips(x, y)):
            cp = _remote(t_ref.at[:, 2 * px + py], land_ref.at[j], send_sems, recv_sems, j, (px, py, c))
            cp.wait_send()
            cp.wait_recv()

    return pl.pallas_call(
        body, name=name, out_shape=(pltpu.HBM(t.shape, t.dtype), pltpu.HBM(land.shape, land.dtype)),
        in_specs=(_HBM, _HBM, _SEM, _SEM) + (_ANY,) * len(after), out_specs=(_HBM, _HBM),
        input_output_aliases={0: 0, 1: 1}, compiler_params=_EFFECT,
    )(t, land, send_sems, recv_sems, *after)[1]


def _rs_sibling_start(name, g):
    nl = g.shape[0]

    def body(g_ref, land_ref, send_sems, recv_sems, g_thru, land_thru, token):
        x, y, c = _mesh_pos()
        _remote(g_ref.at[:, :, pl.ds((1 - c) * P_HALF, P_HALF), :], land_ref, send_sems, recv_sems, 0,
                (x, y, 1 - c)).start()
        token[...] = jnp.zeros_like(token)

    land = lax.empty((nl, N_SHARD, P_HALF, D_MODEL), F32)
    return pl.pallas_call(
        body, name=name,
        out_shape=(pltpu.SemaphoreType.DMA((1,)), pltpu.SemaphoreType.DMA((1,)), pltpu.HBM(g.shape, g.dtype),
                   pltpu.HBM(land.shape, land.dtype), _TOKEN),
        in_specs=(_HBM, _HBM), out_specs=(_SEM, _SEM, _HBM, _HBM, pl.BlockSpec(memory_space=pltpu.VMEM)),
        input_output_aliases={0: 2, 1: 3}, compiler_params=_EFFECT,
    )(_in_hbm(g), _in_hbm(land))


def _rs_sibling_wait(name, send_sems, recv_sems, g, land, after):
    def body(g_ref, land_ref, send_sems, recv_sems, *rest):
        x, y, c = _mesh_pos()
        cp = _remote(g_ref.at[:, :, pl.ds((1 - c) * P_HALF, P_HALF), :], land_ref, send_sems, recv_sems, 0,
                     (x, y, 1 - c))
        cp.wait_send()
        cp.wait_recv()

    return pl.pallas_call(
        body, name=name, out_shape=(pltpu.HBM(g.shape, g.dtype), pltpu.HBM(land.shape, land.dtype)),
        in_specs=(_HBM, _HBM, _SEM, _SEM) + (_ANY,) * len(after), out_specs=(_HBM, _HBM),
        input_output_aliases={0: 0, 1: 1}, compiler_params=_EFFECT,
    )(g, land, send_sems, recv_sems, *after)


def _rs_add(name, ids, g, buf, row_tile):
    nl, _, hr, cols = buf.shape
    n_rt = hr // row_tile

    def body(ids_ref, g_ref, b_ref, own_ref, tb_ref):
        t = g_ref[...] + b_ref[...]
        tb_ref[...] = t.astype(BF16)

        @pl.when(pl.program_id(2) == ids_ref[1])
        def _():
            own_ref[...] = t

    blk = (None, None, row_tile, cols)
    grid_spec = pltpu.PrefetchScalarGridSpec(
        num_scalar_prefetch=1, grid=(nl, n_rt, N_SHARD),
        in_specs=[pl.BlockSpec(blk, lambda l, j, s, ids_ref: (l, s, ids_ref[0] * n_rt + j, 0)),
                  pl.BlockSpec(blk, lambda l, j, s, ids_ref: (l, s, j, 0))],
        out_specs=[pl.BlockSpec((None, row_tile, cols), lambda l, j, s, ids_ref: (l, j, 0)),
                   pl.BlockSpec(blk, lambda l, j, s, ids_ref: (l, s, j, 0))])
    return pl.pallas_call(
        body, name=name, grid_spec=grid_spec,
        out_shape=[jax.ShapeDtypeStruct((nl, hr, cols), F32), jax.ShapeDtypeStruct(buf.shape, BF16)],
        compiler_params=_cparams(3),
    )(ids, g, buf)


def _rs_sum(ids, layer, own, bufb, reduced, row_tile):
    _, hr, cols = own.shape
    n_rt = hr // row_tile

    def body(ids_ref, own_ref, b_ref, reduced_in, f_ref):
        f_ref[...] = ((own_ref[...] + b_ref[0].astype(F32)) + b_ref[1].astype(F32)) + b_ref[2].astype(F32)

    grid_spec = pltpu.PrefetchScalarGridSpec(
        num_scalar_prefetch=1, grid=(n_rt,),
        in_specs=[pl.BlockSpec((None, row_tile, cols), lambda j, ids_ref: (0, j, 0)),
                  pl.BlockSpec((3, None, row_tile, cols), lambda j, ids_ref: (0, 0, j, 0)),
                  pl.BlockSpec(memory_space=pl.ANY)],
        out_specs=pl.BlockSpec((None, row_tile, cols), lambda j, ids_ref: (layer, ids_ref[0] * n_rt + j, 0)))
    return pl.pallas_call(
        body, name="rs_sum", grid_spec=grid_spec,
        out_shape=jax.ShapeDtypeStruct(reduced.shape, F32),
        input_output_aliases={3: 0},
        compiler_params=_cparams(1),
    )(ids, own, bufb, reduced)


def _rs_exchange_start(name, f):
    def body(f_ref, send_sems, recv_sems, f_thru):
        x, y, c = _mesh_pos()
        mine = f_ref.at[:, pl.ds(c * P_HALF, P_HALF), :]
        _remote(mine, mine, send_sems, recv_sems, 0, (x, y, 1 - c)).start()

    return pl.pallas_call(
        body, name=name,
        out_shape=(pltpu.SemaphoreType.DMA((1,)), pltpu.SemaphoreType.DMA((1,)), pltpu.HBM(f.shape, f.dtype)),
        in_specs=(_HBM,), out_specs=(_SEM, _SEM, _HBM),
        input_output_aliases={0: 2}, compiler_params=_EFFECT,
    )(_in_hbm(f))


def _rs_exchange_wait(name, send_sems, recv_sems, f, after):
    def body(f_ref, send_sems, recv_sems, *rest):
        x, y, c = _mesh_pos()
        mine = f_ref.at[:, pl.ds(c * P_HALF, P_HALF), :]
        theirs = f_ref.at[:, pl.ds((1 - c) * P_HALF, P_HALF), :]
        cp = _remote(mine, theirs, send_sems, recv_sems, 0, (x, y, 1 - c))
        cp.wait_send()
        cp.wait_recv()

    return pl.pallas_call(
        body, name=name, out_shape=pltpu.HBM(f.shape, f.dtype),
        in_specs=(_HBM, _SEM, _SEM) + (_ANY,) * len(after), out_specs=_HBM,
        input_output_aliases={0: 0}, compiler_params=_EFFECT,
    )(f, send_sems, recv_sems, *after)


def _small_all_reduce(s, after=()):
    n_rows = s.shape[0]
    hr = n_rows // 2
    qr = hr // N_SHARD

    def body(s_ref, *rest):
        o_ref, sibbuf, tbuf, qbuf, fbuf, send_sems, recv_sems = rest[len(after):]
        x, y, c = _mesh_pos()
        k = 2 * x + y
        sib = (x, y, 1 - c)
        chips = _other_chips(x, y)
        mine = pl.ds(pl.multiple_of(c * hr, SUBLANES), hr)
        theirs = pl.ds(pl.multiple_of((1 - c) * hr, SUBLANES), hr)

        def quarter(shard):
            return pl.ds(pl.multiple_of(shard * qr, SUBLANES), qr)

        first = _remote(s_ref.at[theirs], sibbuf, send_sems, recv_sems, 0, sib)
        first.start()
        first.wait()
        tbuf[...] = s_ref[mine, :] + sibbuf[...]
        cps = []
        for j, (px, py) in enumerate(chips):
            cp = _remote(tbuf.at[quarter(2 * px + py)], qbuf.at[j], send_sems, recv_sems, 1 + j, (px, py, c))
            cp.start()
            cps.append(cp)
        for cp in cps:
            cp.wait()
        fbuf[quarter(k), :] = (tbuf[quarter(k), :] + qbuf[1]) + (qbuf[0] + qbuf[2])
        cps = []
        for j, (px, py) in enumerate(chips):
            cp = _remote(fbuf.at[quarter(k)], fbuf.at[quarter(k)], send_sems, recv_sems, 4 + j, (px, py, c))
            cp.start()
            cps.append(cp)
        for j, (px, py) in enumerate(chips):
            got = fbuf.at[quarter(2 * px + py)]
            _remote(got, got, send_sems, recv_sems, 4 + j, (px, py, c)).wait_recv()
        for cp in cps:
            cp.wait_send()
        o_ref[mine, :] = fbuf[...]
        last = _remote(fbuf, o_ref.at[mine], send_sems, recv_sems, 7, sib)
        last.start()
        last.wait()

    vmem = pl.BlockSpec(memory_space=pltpu.VMEM)
    return pl.pallas_call(
        body, name="small_all_reduce",
        in_specs=[vmem] + [_ANY] * len(after), out_specs=vmem,
        out_shape=jax.ShapeDtypeStruct(s.shape, F32),
        scratch_shapes=[pltpu.VMEM((hr, D_MODEL), F32), pltpu.VMEM((hr, D_MODEL), F32),
                        pltpu.VMEM((3, qr, D_MODEL), F32), pltpu.VMEM((hr, D_MODEL), F32),
                        pltpu.SemaphoreType.DMA((8,)), pltpu.SemaphoreType.DMA((8,))],
        compiler_params=pltpu.CompilerParams(vmem_limit_bytes=VMEM_LIMIT),
    )(s, *after)


_SMALL = ("norm_mix", "w_pool", "pool_scale", "lam_re", "lam_im", "log_dt", "b_re", "b_im", "c_re", "c_im",
          "d_skip", "b_glu", "norm_ffn", "norm_final")
_WEIGHTS = ("norm_mix", "w_in", "w_pool", "pool_scale", "lam_re", "lam_im", "log_dt", "b_re", "b_im", "c_re",
            "c_im", "d_skip", "w_glu", "b_glu", "w_out", "norm_ffn", "w_gate", "w_up", "w_down", "norm_final")


def _local_step(x, target, p, get_weights, scan_done, get_ffn_weights, ffn_bwd_done, put_grads):
    nl = p["norm_mix"].shape[0]

    def tied(a, token):
        return a if token is None else a + token
    n_rows = nl * N_SSM_GROUPS
    lr = p["lam_re"].reshape(n_rows, 1, SSM_STATE)
    li = p["lam_im"].reshape(n_rows, 1, SSM_STATE)
    ldt = p["log_dt"].reshape(n_rows, 1, 1)
    br_t = p["b_re"].reshape(n_rows, SSM_STATE, SSM_GROUP).transpose(0, 2, 1)
    bi_t = p["b_im"].reshape(n_rows, SSM_STATE, SSM_GROUP).transpose(0, 2, 1)
    ar, ai, bbr_t, bbi_t = _disc_fwd(lr, li, ldt, br_t, bi_t)
    ar = ar.reshape(nl, 1, N_STATE)
    ai = ai.reshape(nl, 1, N_STATE)
    bbr = bbr_t.transpose(0, 2, 1).reshape(nl, N_SSM_GROUPS, SSM_STATE, SSM_GROUP)
    bbi = bbi_t.transpose(0, 2, 1).reshape(nl, N_SSM_GROUPS, SSM_STATE, SSM_GROUP)
    w_pool = p["w_pool"].astype(BF16)
    p = dict(p)
    for n in ("norm_mix", "pool_scale", "b_glu", "norm_ffn"):
        p[n] = p[n].reshape(nl, 1, -1)
    swap = lambda a: jnp.swapaxes(a, -1, -2)
    bpad = jax.vmap(_pad_pairs)(bbr, bbi).astype(BF16)
    cpad_t = jax.vmap(_pad_pairs)(swap(p["c_re"]), -swap(p["c_im"])).astype(BF16)
    bpad_t, cpad = swap(bpad), swap(cpad_t)
    dskip = p["d_skip"].reshape(nl, 1, D_SSM)

    layers = []
    h = x
    for l in range(nl):
        wp = get_weights(l, [h] if l else [h, bpad, cpad, bpad_t, cpad_t, ar, ai])
        u, ypool = _mix_in_fwd(h, p["norm_mix"], wp, l, w_pool, p["pool_scale"])
        sre, sim, yraw = _ssm_fwd(u, l, bpad, cpad, ar, ai, dskip)
        wp = scan_done(l, wp, [yraw])
        hm = _mix_out_fwd(yraw, ypool, h, wp, l, p["b_glu"])
        wp = get_ffn_weights(l, wp, [hm])
        h_next, n2, act_s, fgate_s, fup_s = _ffn_fwd(hm, p["norm_ffn"], wp, l)
        layers.append(dict(h=h, u=u, ypool=ypool, sre=sre, sim=sim, yraw=yraw, hm=hm, n2=n2, act_s=act_s, wp=wp,
                           fgate_s=fgate_s, fup_s=fup_s))
        h = h_next

    dh, loss, d_norm_final = _final_fwd_bwd(h, p["norm_final"].reshape(1, D_MODEL), target)

    raw = {n: [None] * nl for n in ("dg1", "dwp", "dsc", "dcp", "dbp", "ddsk", "db_glu", "dg2", "dar", "dai")}
    token = None
    for l in reversed(range(nl)):
        s = layers[l]
        wp = s["wp"]
        g1 = lax.empty((1, N_SHARD, P_ROWS, D_MODEL), F32)
        dhm, dg2, dgate_s, dup_s, dhb = _ffn_bwd_act(dh, s["hm"], tied(p["norm_ffn"], token), s["fgate_s"],
                                                      s["fup_s"], wp, l)
        g1 = _ffn_bwd_w(s["n2"], dgate_s, dup_s, s["act_s"], dhb, g1)
        token = ffn_bwd_done(l, [g1])
        dyraw, dyp, db_glu, g1 = _mix_out_bwd(dhm, s["yraw"], s["ypool"], wp, l, tied(p["b_glu"], token), g1)
        dus, dcp, dbp, dar, dai, ddsk = _ssm_bwd(dyraw, s["u"], s["sre"], s["sim"], l, cpad_t, bpad_t, ar, ai, dskip)
        dup, dwp, dsc = _pool_bwd(dyp, s["u"], l, w_pool, p["pool_scale"])
        dh, dg1, g1 = _mix_in_bwd(dup, dus, s["h"], dhm, p["norm_mix"], wp, l, g1)
        token = put_grads(l, g1)
        for n, a in (("dg1", dg1), ("dwp", dwp), ("dsc", dsc), ("dcp", dcp), ("dbp", dbp), ("ddsk", ddsk),
                     ("db_glu", db_glu), ("dg2", dg2), ("dar", dar), ("dai", dai)):
            raw[n][l] = a

    st = {n: jnp.stack(v) for n, v in raw.items()}
    dc_re, dc_im = jax.vmap(_unpad_pairs)(swap(st["dcp"]))
    dbbr, dbbi = jax.vmap(_unpad_pairs)(st["dbp"])
    rows = lambda a: a.reshape((n_rows,) + a.shape[2:])
    dlr, dli, dldt, dbr_t, dbi_t = _disc_bwd(lr, li, ldt, br_t, bi_t, st["dar"].reshape(n_rows, 1, SSM_STATE),
                                              st["dai"].reshape(n_rows, 1, SSM_STATE), rows(swap(dbbr)),
                                              rows(swap(dbbi)))
    small = {"norm_mix": st["dg1"][:, 0], "w_pool": st["dwp"], "pool_scale": st["dsc"][:, 0], "c_re": swap(dc_re),
             "c_im": -swap(dc_im), "d_skip": st["ddsk"].reshape(nl, N_SSM_GROUPS, SSM_GROUP),
             "b_glu": st["db_glu"][:, 0], "norm_ffn": st["dg2"][:, 0]}
    small["lam_re"] = dlr.reshape(nl, N_SSM_GROUPS, SSM_STATE)
    small["lam_im"] = dli.reshape(nl, N_SSM_GROUPS, SSM_STATE)
    small["log_dt"] = dldt.reshape(nl, N_SSM_GROUPS)
    small["b_re"] = dbr_t.reshape(nl, N_SSM_GROUPS, SSM_GROUP, SSM_STATE)
    small["b_im"] = dbi_t.reshape(nl, N_SSM_GROUPS, SSM_GROUP, SSM_STATE)
    small["d_skip"] = small["d_skip"].transpose(_SMALL_VIEW["d_skip"])
    small["norm_final"] = d_norm_final
    return loss, dh, small


_SMALL_VIEW = {"b_re": (0, 1, 3, 2), "b_im": (0, 1, 3, 2), "d_skip": (0, 2, 1)}
_SMALL_GROUPS = (("b_re", "b_im"), ("c_re", "c_im"), ("w_pool", "lam_re", "lam_im", "norm_mix", "norm_ffn",
                                                      "pool_scale", "b_glu", "log_dt", "d_skip", "norm_final"))


def _view(n, a):
    a = a.transpose(_SMALL_VIEW[n]) if n in _SMALL_VIEW else a
    return a[None] if a.ndim == 1 else a


def _unview(n, a, shape):
    a = a.reshape(shape) if len(shape) == 1 else a
    return a.transpose(_SMALL_VIEW[n]) if n in _SMALL_VIEW else a


def _flatten_small(views):
    flat = jnp.concatenate([views[n].reshape(-1) for n in _SMALL])
    n_rows = -(-flat.shape[0] // (64 * D_MODEL)) * 64
    return jnp.pad(flat, (0, n_rows * D_MODEL - flat.shape[0])).reshape(n_rows, D_MODEL)


def _split_small(flat, like):
    flat = flat.reshape(-1)
    out, at = {}, 0
    for n in _SMALL:
        size = like[n].size
        out[n] = flat[at:at + size].reshape(like[n].shape)
        at += size
    return out


def _adamw_small(name, ws, ms, vs, gs):
    k = len(ws)

    def body(*refs):
        ins, outs = refs[:4 * k], refs[4 * k:]
        for i in range(k):
            w, m, v, g = (ins[j * k + i][...] for j in range(4))
            delta, mn, vn = _adamw_math(w, g, m, v)
            outs[i][...] = delta
            outs[k + i][...] = mn
            outs[2 * k + i][...] = vn

    shapes = [jax.ShapeDtypeStruct(w.shape, F32) for w in ws] * 3
    outs = pl.pallas_call(body, name=name, out_shape=shapes,
                          compiler_params=pltpu.CompilerParams(vmem_limit_bytes=VMEM_LIMIT))(*ws, *ms, *vs, *gs)
    return outs[:k], outs[k:2 * k], outs[2 * k:]


def kernel(x, norm_mix, w_in, w_pool, pool_scale, lam_re, lam_im, log_dt, b_re, b_im, c_re, c_im, d_skip, w_glu, b_glu, w_out, norm_ffn, w_gate, w_up, w_down, norm_final, loss_target, m_norm_mix, m_w_in, m_w_pool, m_pool_scale, m_lam_re, m_lam_im, m_log_dt, m_b_re, m_b_im, m_c_re, m_c_im, m_d_skip, m_w_glu, m_b_glu, m_w_out, m_norm_ffn, m_w_gate, m_w_up, m_w_down, m_norm_final, v_norm_mix, v_w_in, v_w_pool, v_pool_scale, v_lam_re, v_lam_im, v_log_dt, v_b_re, v_b_im, v_c_re, v_c_im, v_d_skip, v_w_glu, v_b_glu, v_w_out, v_norm_ffn, v_w_gate, v_w_up, v_w_down, v_norm_final):
    given = dict(locals())
    w = {n: given[n] for n in _WEIGHTS}
    m = {n: given["m_" + n] for n in _WEIGHTS}
    v = {n: given["v_" + n] for n in _WEIGHTS}
    ids = jnp.stack([lax.axis_index("c"), 2 * lax.axis_index("x") + lax.axis_index("y")]).astype(jnp.int32)

    t_names = ("w_gate", "w_up")
    tr = lambda a: a.transpose(0, 2, 1)
    for d in (w, m, v):
        d.update({n: tr(d[n]) for n in t_names})

    nl = norm_mix.shape[0]
    mixer_rows, ffn_rows = (P_FF_ROWS, P_ROWS - P_FF_ROWS), (0, P_FF_ROWS)
    started, last = {}, None
    for l in range(nl):
        packed = _pack_weights(ids, l, w["w_in"], w["w_glu"], w["w_out"], w["w_down"], w["w_gate"], w["w_up"],
                               [] if last is None else [last])
        if l == 0:
            first = _ag_start("ag_start_0_mixer", packed, ids, [mixer_rows])
            started[0] = _ag_start("ag_start_0_ffn", first[2], first[3], [ffn_rows])
        else:
            started[l] = _ag_start(f"ag_start_{l}", packed, last, [mixer_rows, ffn_rows])
        last = started[l][3]
    views = [{n: _view(n, d[n]) for n in _SMALL} for d in (w, m, v)]

    passing = {}

    def get_weights(l, after):
        send_sems, recv_sems, buf, _ = started[l]
        if l == 0:
            buf = _ag_wait("ag_wait_0_mixer", first[0], first[1], buf, after + [last], [mixer_rows])
            return _ag_forward(buf, mixer_rows)
        buf = _ag_wait(f"ag_wait_{l}", send_sems, recv_sems, buf, after, [mixer_rows, ffn_rows])
        buf = _ag_forward(buf, mixer_rows)
        passing[l] = _ag_forward_start(f"ag_forward_start_{l}", buf, ffn_rows)
        return passing[l][2]

    def scan_done(l, buf, after):
        if l > 0:
            return buf
        send_sems, recv_sems, _, _ = started[0]
        buf = _ag_wait("ag_wait_0_ffn", send_sems, recv_sems, buf, after, [ffn_rows])
        passing[0] = _ag_forward_start("ag_forward_start_0", buf, ffn_rows)
        return passing[0][2]

    def get_ffn_weights(l, buf, after):
        send_sems, recv_sems, _ = passing[l]
        return _ag_forward_wait(f"ag_forward_wait_{l}", send_sems, recv_sems, buf, after, ffn_rows)

    to_sibling, to_chips, reduced = {}, {}, {}

    def put_grads(l, g):
        to_sibling[l] = _rs_sibling_start(f"rs_sibling_start_{l}", g)
        token = to_sibling[l][4]
        if l + 1 in to_chips:
            finish(l + 1, [token])
        return token[:1, :1]

    def ffn_bwd_done(l, after):
        return send_to_chips(l + 1, after)[:1, :1] if l + 1 in to_sibling else None

    def send_to_chips(l, after):
        send_sems, recv_sems, g, land, _ = to_sibling.pop(l)
        g, land = _rs_sibling_wait(f"rs_sibling_wait_{l}", send_sems, recv_sems, g, land, after)
        own, t = _rs_add("rs_add", ids, g, land, RS_ADD_TILE)
        send_sems, recv_sems, t, land, token = _rs_chips_start(f"rs_chips_start_{l}", t)
        to_chips[l] = (send_sems, recv_sems, t, land, own)
        return token

    def finish(l, after):
        send_sems, recv_sems, t, land, own = to_chips.pop(l)
        land = _rs_chips_wait(f"rs_chips_wait_{l}", send_sems, recv_sems, t, land, after)
        shard = lax.empty((1, P_ROWS, D_MODEL), F32)
        reduced[l] = _rs_exchange_start(f"rs_exchange_start_{l}", _rs_sum(ids, 0, own, land, shard, RS_SUM_TILE))

    loss, grad_x, small = _local_step(x[0], loss_target[0], {n: w[n] for n in _SMALL}, get_weights, scan_done,
                                      get_ffn_weights, ffn_bwd_done, put_grads)
    loss = lax.psum(loss[0, 0], ("x", "y", "c"))
    small_flat = _flatten_small(small)

    groups = ((("w_in", P_IN_BLK), ("w_out", P_OUT_BLK)), (("w_down", P_WD_BLK), ("w_gate", P_WG_BLK), ("w_up", P_WU_BLK)))
    res = {n: None for n in ("w_in", "w_out", "w_down", "w_gate", "w_up", "w_glu")}

    def adamw_layer(l, after):
        send_sems, recv_sems, shard = reduced[l]
        shard = _rs_exchange_wait(f"rs_exchange_wait_{l}", send_sems, recv_sems, shard, after)
        for group, row_tile in zip(groups, (128, 176)):
            names = [n for n, _ in group]
            outs = None if res[names[0]] is None else [res[n] for n in names]
            outs = _adamw_group("adamw_" + names[0], l, *[[d[n] for n in names] for d in (w, m, v)], shard,
                                [blk * idx for _, (blk, idx) in group], row_tile, outs)
            res.update(zip(names, outs))
        blk, idx = P_GLU_BLK
        res["w_glu"] = _adamw("adamw_w_glu", l, w["w_glu"], m["w_glu"], v["w_glu"], shard, (blk, D_MODEL), blk * idx,
                              128, res["w_glu"], (), True)

    if nl > 1:
        adamw_layer(nl - 1, [to_sibling[0][4]])
    token = send_to_chips(0, [small_flat] + [r[0] for r in res.values() if r is not None])
    for l in reversed(range(1, nl - 1)):
        adamw_layer(l, [token])
    updated = [r[0] for r in res.values() if r is not None]
    small_sum = _small_all_reduce(small_flat, [token] + updated)
    finish(0, [small_sum] + updated)
    adamw_layer(0, [])
    for n in t_names:
        res[n] = tuple(tr(a) for a in res[n])
    g_views = _split_small(small_sum, views[0])
    for group in _SMALL_GROUPS:
        deltas, new_ms, new_vs = _adamw_small("adamw_" + group[0], *[[d[n] for n in group] for d in views],
                                              [g_views[n] for n in group])
        for i, n in enumerate(group):
            res[n] = tuple(_unview(n, a, w[n].shape) for a in (g_views[n], deltas[i], new_ms[i], new_vs[i]))

    return (loss, grad_x[None], *[res[n][0] for n in _WEIGHTS], *[res[n][1] for n in _WEIGHTS],
            *[res[n][2] for n in _WEIGHTS], *[res[n][3] for n in _WEIGHTS])
```
